```python
import jax, jax.numpy as jnp
from jax import lax
import numpy as np

D_MODEL = 1024
BATCH = 16
SEQ = 2048
DEPTH = 1

EPS = 1e-6
A_GROUPS = 8
A_GROUP_DIM = D_MODEL // A_GROUPS
A_WIDTH = A_GROUPS * A_GROUP_DIM
CHUNK = 128
MLA_HEADS = 8
QK_NOPE_DIM = 128
QK_ROPE_DIM = 64
QK_HEAD_DIM = QK_NOPE_DIM + QK_ROPE_DIM
V_HEAD_DIM = D_MODEL // MLA_HEADS
Q_LORA_RANK = 256
KV_LORA_RANK = 128
ROPE_THETA = 10000.0
Q_BLOCK = 128
D_FF = 2816
CONV_WIDTH = 3

IN_DIM = 2 * A_WIDTH + Q_LORA_RANK + KV_LORA_RANK + QK_ROPE_DIM + 2 * D_MODEL
SPLIT_U = A_WIDTH
SPLIT_V = SPLIT_U + A_WIDTH
SPLIT_CQ = SPLIT_V + Q_LORA_RANK
SPLIT_CKV = SPLIT_CQ + KV_LORA_RANK
SPLIT_KR = SPLIT_CKV + QK_ROPE_DIM
SPLIT_GA = SPLIT_KR + D_MODEL

kernel_name = "hybrid_gmlp_mla_convffn"


def rms_norm(x, g):
    xf = x.astype(jnp.float32)
    xf = xf * lax.rsqrt(jnp.mean(jnp.square(xf), axis=-1, keepdims=True) + EPS)
    return xf.astype(x.dtype) * g


def layer_norm(x, g, b):
    xf = x.astype(jnp.float32)
    mu = jnp.mean(xf, axis=-1, keepdims=True)
    var = jnp.mean(jnp.square(xf - mu), axis=-1, keepdims=True)
    return ((xf - mu) * lax.rsqrt(var + EPS)).astype(x.dtype) * g + b


def rope_cos_sin(positions):
    inv_freq = 1.0 / (ROPE_THETA ** (jnp.arange(0, QK_ROPE_DIM, 2, dtype=jnp.float32) / QK_ROPE_DIM))
    ang = positions.astype(jnp.float32)[..., None] * inv_freq
    return jnp.cos(ang), jnp.sin(ang)


def apply_rope(x, cos, sin):
    x1, x2 = jnp.split(x.astype(jnp.float32), 2, axis=-1)
    return jnp.concatenate([x1 * cos - x2 * sin, x1 * sin + x2 * cos], axis=-1).astype(x.dtype)


def chunked_spatial_gating(u, v, v_g, v_b, w_s, b_s):
    B, S, _ = v.shape
    n_chunks = S // CHUNK
    v = layer_norm(v, v_g, v_b)
    vc = v.reshape(B, n_chunks, CHUNK, A_GROUPS, A_GROUP_DIM)
    causal = jnp.tril(jnp.ones((CHUNK, CHUNK), dtype=bool))
    w = jnp.where(causal[None], w_s, 0.0).astype(vc.dtype)
    mixed = jnp.einsum('gts,bnsgc->bntgc', w, vc) + b_s.T[None, None, :, :, None]
    return u * mixed.reshape(B, S, A_WIDTH)


def latent_attention(c_q, c_kv, k_rope, cos, sin, q_norm_g, w_uq, kv_norm_g, w_ukv):
    B, S, _ = c_q.shape
    q = (rms_norm(c_q, q_norm_g) @ w_uq).reshape(B, S, MLA_HEADS, QK_HEAD_DIM)
    q_nope, q_rope = jnp.split(q, [QK_NOPE_DIM], axis=-1)
    q_rope = apply_rope(q_rope, cos[:, :, None, :], sin[:, :, None, :])
    kv = (rms_norm(c_kv, kv_norm_g) @ w_ukv).reshape(B, S, MLA_HEADS, QK_NOPE_DIM + V_HEAD_DIM)
    k_nope, v = jnp.split(kv, [QK_NOPE_DIM], axis=-1)
    k_rope = apply_rope(k_rope, cos, sin)
    scale = QK_HEAD_DIM ** -0.5
    n_blocks = S // Q_BLOCK
    qn_blocks = q_nope.reshape(B, n_blocks, Q_BLOCK, MLA_HEADS, QK_NOPE_DIM).transpose(1, 0, 2, 3, 4)
    qr_blocks = q_rope.reshape(B, n_blocks, Q_BLOCK, MLA_HEADS, QK_ROPE_DIM).transpose(1, 0, 2, 3, 4)
    key_pos = jnp.arange(S)

    def one_block(args):
        qn, qr, i = args
        s = jnp.einsum('bqhd,bkhd->bhqk', qn, k_nope) + jnp.einsum('bqhr,bkr->bhqk', qr, k_rope)
        s = s.astype(jnp.float32) * scale
        q_pos = i * Q_BLOCK + jnp.arange(Q_BLOCK)
        s = jnp.where(key_pos[None, :] <= q_pos[:, None], s, -jnp.inf)
        p = jax.nn.softmax(s, axis=-1).astype(v.dtype)
        return jnp.einsum('bhqk,bkhd->bqhd', p, v)

    out = lax.map(one_block, (qn_blocks, qr_blocks, jnp.arange(n_blocks)))
    return out.transpose(1, 0, 2, 3, 4).reshape(B, S, MLA_HEADS * V_HEAD_DIM)


def causal_depthwise_conv(x, w, b):
    S = x.shape[1]
    xp = jnp.pad(x, ((0, 0), (CONV_WIDTH - 1, 0), (0, 0)))
    return b + sum(w[k] * xp[:, k:k + S] for k in range(CONV_WIDTH))


def conv_gated_ffn(h, w_up, conv_w, conv_b, w_down):
    up = causal_depthwise_conv(h @ w_up, conv_w, conv_b)
    gate, val = jnp.split(up, 2, axis=-1)
    return (jax.nn.silu(gate) * val) @ w_down


def _fwd_setup_inputs(seed: int = 0) -> dict:
    key = jax.random.key(seed)
    ks = jax.random.split(key, 20)
    f32 = jnp.float32

    def nrm(k, shape, fan_in):
        return jax.random.normal(k, shape, f32) * (fan_in ** -0.5)

    def gain(k, shape):
        return 1.0 + 0.02 * jax.random.normal(k, shape, f32)

    x = jax.random.normal(ks[0], (BATCH, SEQ, D_MODEL), f32)
    offset = jax.random.randint(ks[1], (BATCH, 1), 0, 1024, dtype=jnp.int32)
    positions = jnp.arange(SEQ, dtype=jnp.int32)[None, :] + offset
    return {
        "x": x,
        "positions": positions,
        "mix_norm": gain(ks[2], (DEPTH, D_MODEL)),
        "w_in": nrm(ks[3], (DEPTH, D_MODEL, IN_DIM), D_MODEL),
        "a_v_norm_g": gain(ks[4], (DEPTH, A_WIDTH)),
        "a_v_norm_b": 0.02 * jax.random.normal(ks[5], (DEPTH, A_WIDTH), f32),
        "a_spatial_w": nrm(ks[6], (DEPTH, A_GROUPS, CHUNK, CHUNK), CHUNK),
        "a_spatial_b": gain(ks[7], (DEPTH, A_GROUPS, CHUNK)),
        "q_a_norm": gain(ks[8], (DEPTH, Q_LORA_RANK)),
        "w_uq": nrm(ks[9], (DEPTH, Q_LORA_RANK, MLA_HEADS * QK_HEAD_DIM), Q_LORA_RANK),
        "kv_a_norm": gain(ks[10], (DEPTH, KV_LORA_RANK)),
        "w_ukv": nrm(ks[11], (DEPTH, KV_LORA_RANK, MLA_HEADS * (QK_NOPE_DIM + V_HEAD_DIM)), KV_LORA_RANK),
        "w_out": nrm(ks[12], (DEPTH, D_MODEL, D_MODEL), D_MODEL),
        "ffn_norm": gain(ks[13], (DEPTH, D_MODEL)),
        "w_up": nrm(ks[14], (DEPTH, D_MODEL, 2 * D_FF), D_MODEL),
        "conv_w": nrm(ks[15], (DEPTH, CONV_WIDTH, 2 * D_FF), CONV_WIDTH),
        "conv_b": 0.02 * jax.random.normal(ks[16], (DEPTH, 2 * D_FF), f32),
        "w_down": nrm(ks[17], (DEPTH, D_FF, D_MODEL), D_FF),
        "final_norm": gain(ks[18], (D_MODEL,)),
    }


def _fwd_reference(x, positions, mix_norm, w_in, a_v_norm_g, a_v_norm_b, a_spatial_w, a_spatial_b,
              q_a_norm, w_uq, kv_a_norm, w_ukv, w_out, ffn_norm, w_up, conv_w, conv_b, w_down,
              final_norm):
    cos, sin = rope_cos_sin(positions)
    for l in range(DEPTH):
        h = rms_norm(x, mix_norm[l])
        z = h @ w_in[l]
        u_a, v_a, c_q, c_kv, k_rope, g_a, g_b = jnp.split(
            z, [SPLIT_U, SPLIT_V, SPLIT_CQ, SPLIT_CKV, SPLIT_KR, SPLIT_GA], axis=-1)
        y_a = chunked_spatial_gating(jax.nn.gelu(u_a), jax.nn.gelu(v_a), a_v_norm_g[l], a_v_norm_b[l],
                                     a_spatial_w[l], a_spatial_b[l])
        y_b = latent_attention(c_q, c_kv, k_rope, cos, sin, q_a_norm[l], w_uq[l],
                               kv_a_norm[l], w_ukv[l])
        merged = jax.nn.sigmoid(g_a) * y_a + jax.nn.sigmoid(g_b) * y_b
        x = x + merged @ w_out[l]
        x = x + conv_gated_ffn(rms_norm(x, ffn_norm[l]), w_up[l], conv_w[l], conv_b[l], w_down[l])
    return rms_norm(x, final_norm)


import jax as _jax
import jax.numpy as _jnp

TWIN_FORMAT = 'train_step'
FWD_PARAMS = ['x', 'positions', 'mix_norm', 'w_in', 'a_v_norm_g', 'a_v_norm_b', 'a_spatial_w', 'a_spatial_b', 'q_a_norm', 'w_uq', 'kv_a_norm', 'w_ukv', 'w_out', 'ffn_norm', 'w_up', 'conv_w', 'conv_b', 'w_down', 'final_norm']
TWIN_WEIGHTS = ['mix_norm', 'w_in', 'a_v_norm_g', 'a_v_norm_b', 'a_spatial_w', 'a_spatial_b', 'q_a_norm', 'w_uq', 'kv_a_norm', 'w_ukv', 'w_out', 'ffn_norm', 'w_up', 'conv_w', 'conv_b', 'w_down', 'final_norm']
TWIN_DIFF_INPUT = 'x'
TWIN_INPUTS = ['x', 'positions', 'mix_norm', 'w_in', 'a_v_norm_g', 'a_v_norm_b', 'a_spatial_w', 'a_spatial_b', 'q_a_norm', 'w_uq', 'kv_a_norm', 'w_ukv', 'w_out', 'ffn_norm', 'w_up', 'conv_w', 'conv_b', 'w_down', 'final_norm', 'loss_target', 'm_mix_norm', 'm_w_in', 'm_a_v_norm_g', 'm_a_v_norm_b', 'm_a_spatial_w', 'm_a_spatial_b', 'm_q_a_norm', 'm_w_uq', 'm_kv_a_norm', 'm_w_ukv', 'm_w_out', 'm_ffn_norm', 'm_w_up', 'm_conv_w', 'm_conv_b', 'm_w_down', 'm_final_norm', 'v_mix_norm', 'v_w_in', 'v_a_v_norm_g', 'v_a_v_norm_b', 'v_a_spatial_w', 'v_a_spatial_b', 'v_q_a_norm', 'v_w_uq', 'v_kv_a_norm', 'v_w_ukv', 'v_w_out', 'v_ffn_norm', 'v_w_up', 'v_conv_w', 'v_conv_b', 'v_w_down', 'v_final_norm']
TWIN_OUTPUTS = ['loss', 'grad_x', 'grad_mix_norm', 'grad_w_in', 'grad_a_v_norm_g', 'grad_a_v_norm_b', 'grad_a_spatial_w', 'grad_a_spatial_b', 'grad_q_a_norm', 'grad_w_uq', 'grad_kv_a_norm', 'grad_w_ukv', 'grad_w_out', 'grad_ffn_norm', 'grad_w_up', 'grad_conv_w', 'grad_conv_b', 'grad_w_down', 'grad_final_norm', 'delta_mix_norm', 'delta_w_in', 'delta_a_v_norm_g', 'delta_a_v_norm_b', 'delta_a_spatial_w', 'delta_a_spatial_b', 'delta_q_a_norm', 'delta_w_uq', 'delta_kv_a_norm', 'delta_w_ukv', 'delta_w_out', 'delta_ffn_norm', 'delta_w_up', 'delta_conv_w', 'delta_conv_b', 'delta_w_down', 'delta_final_norm', 'new_m_mix_norm', 'new_m_w_in', 'new_m_a_v_norm_g', 'new_m_a_v_norm_b', 'new_m_a_spatial_w', 'new_m_a_spatial_b', 'new_m_q_a_norm', 'new_m_w_uq', 'new_m_kv_a_norm', 'new_m_w_ukv', 'new_m_w_out', 'new_m_ffn_norm', 'new_m_w_up', 'new_m_conv_w', 'new_m_conv_b', 'new_m_w_down', 'new_m_final_norm', 'new_v_mix_norm', 'new_v_w_in', 'new_v_a_v_norm_g', 'new_v_a_v_norm_b', 'new_v_a_spatial_w', 'new_v_a_spatial_b', 'new_v_q_a_norm', 'new_v_w_uq', 'new_v_kv_a_norm', 'new_v_w_ukv', 'new_v_w_out', 'new_v_ffn_norm', 'new_v_w_up', 'new_v_conv_w', 'new_v_conv_b', 'new_v_w_down', 'new_v_final_norm']
TWIN_LEAF_KINDS = {'loss': 'loss', 'grad_x': 'grad_x', 'grad_mix_norm': 'grad_w', 'grad_w_in': 'grad_w', 'grad_a_v_norm_g': 'grad_w', 'grad_a_v_norm_b': 'grad_w', 'grad_a_spatial_w': 'grad_w', 'grad_a_spatial_b': 'grad_w', 'grad_q_a_norm': 'grad_w', 'grad_w_uq': 'grad_w', 'grad_kv_a_norm': 'grad_w', 'grad_w_ukv': 'grad_w', 'grad_w_out': 'grad_w', 'grad_ffn_norm': 'grad_w', 'grad_w_up': 'grad_w', 'grad_conv_w': 'grad_w', 'grad_conv_b': 'grad_w', 'grad_w_down': 'grad_w', 'grad_final_norm': 'grad_w', 'delta_mix_norm': 'delta_w', 'delta_w_in': 'delta_w', 'delta_a_v_norm_g': 'delta_w', 'delta_a_v_norm_b': 'delta_w', 'delta_a_spatial_w': 'delta_w', 'delta_a_spatial_b': 'delta_w', 'delta_q_a_norm': 'delta_w', 'delta_w_uq': 'delta_w', 'delta_kv_a_norm': 'delta_w', 'delta_w_ukv': 'delta_w', 'delta_w_out': 'delta_w', 'delta_ffn_norm': 'delta_w', 'delta_w_up': 'delta_w', 'delta_conv_w': 'delta_w', 'delta_conv_b': 'delta_w', 'delta_w_down': 'delta_w', 'delta_final_norm': 'delta_w', 'new_m_mix_norm': 'new_m', 'new_m_w_in': 'new_m', 'new_m_a_v_norm_g': 'new_m', 'new_m_a_v_norm_b': 'new_m', 'new_m_a_spatial_w': 'new_m', 'new_m_a_spatial_b': 'new_m', 'new_m_q_a_norm': 'new_m', 'new_m_w_uq': 'new_m', 'new_m_kv_a_norm': 'new_m', 'new_m_w_ukv': 'new_m', 'new_m_w_out': 'new_m', 'new_m_ffn_norm': 'new_m', 'new_m_w_up': 'new_m', 'new_m_conv_w': 'new_m', 'new_m_conv_b': 'new_m', 'new_m_w_down': 'new_m', 'new_m_final_norm': 'new_m', 'new_v_mix_norm': 'new_v', 'new_v_w_in': 'new_v', 'new_v_a_v_norm_g': 'new_v', 'new_v_a_v_norm_b': 'new_v', 'new_v_a_spatial_w': 'new_v', 'new_v_a_spatial_b': 'new_v', 'new_v_q_a_norm': 'new_v', 'new_v_w_uq': 'new_v', 'new_v_kv_a_norm': 'new_v', 'new_v_w_ukv': 'new_v', 'new_v_w_out': 'new_v', 'new_v_ffn_norm': 'new_v', 'new_v_w_up': 'new_v', 'new_v_conv_w': 'new_v', 'new_v_conv_b': 'new_v', 'new_v_w_down': 'new_v', 'new_v_final_norm': 'new_v'}


def _forward(args):
    return _fwd_reference(*[args[k] for k in FWD_PARAMS])


def _output_shape():
    out = _jax.eval_shape(lambda: _forward(_fwd_setup_inputs(0)))
    return out.shape, out.dtype

N_MICROBATCH = 1
ADAM_LR = 0.001
ADAM_B1 = 0.9
ADAM_B2 = 0.999
ADAM_EPS = 1e-08
ADAM_WD = 0.01
ADAM_STEP = 10
PER_EXAMPLE_BATCH_AXIS = {'x': 0, 'positions': 0, 'loss_target': 0}
SHARED_INPUTS = []
_WEIGHT_DTYPES = {'mix_norm': _jnp.float32, 'w_in': _jnp.float32, 'a_v_norm_g': _jnp.float32, 'a_v_norm_b': _jnp.float32, 'a_spatial_w': _jnp.float32, 'a_spatial_b': _jnp.float32, 'q_a_norm': _jnp.float32, 'w_uq': _jnp.float32, 'kv_a_norm': _jnp.float32, 'w_ukv': _jnp.float32, 'w_out': _jnp.float32, 'ffn_norm': _jnp.float32, 'w_up': _jnp.float32, 'conv_w': _jnp.float32, 'conv_b': _jnp.float32, 'w_down': _jnp.float32, 'final_norm': _jnp.float32}
MOMENT_SCALE = {'mix_norm': 1.049558e-01, 'w_in': 5.068803e-02, 'a_v_norm_g': 4.452198e-02, 'a_v_norm_b': 4.806323e-02, 'a_spatial_w': 4.598508e-02, 'a_spatial_b': 7.044837e-02, 'q_a_norm': 3.525842e-02, 'w_uq': 1.523238e-02, 'kv_a_norm': 7.513694e-02, 'w_ukv': 1.914653e-02, 'w_out': 8.113464e-02, 'ffn_norm': 1.304285e-01, 'w_up': 5.367246e-02, 'conv_w': 5.304594e-02, 'conv_b': 5.284949e-02, 'w_down': 8.790921e-02, 'final_norm': 3.196775e+01}


def _to_microbatches(a, axis):
    t = _jnp.moveaxis(a, axis, 0)
    t = t.reshape((N_MICROBATCH, t.shape[0] // N_MICROBATCH) + t.shape[1:])
    return _jnp.moveaxis(t, 1, axis + 1)


def setup_inputs(seed: int = 0) -> dict:
    inp = _fwd_setup_inputs(seed)
    key = _jax.random.fold_in(_jax.random.key(seed), 7919)
    shape, _ = _output_shape()
    out = dict(inp)
    out["loss_target"] = _jax.random.normal(_jax.random.fold_in(key, 0), shape, _jnp.float32)
    for i, name in enumerate(TWIN_WEIGHTS):
        w = inp[name].astype(_jnp.float32)
        if MOMENT_SCALE is None:
            s = _jnp.sqrt(_jnp.mean(_jnp.square(w)) + 1e-30)
        else:
            s = MOMENT_SCALE[name]
        km, kv = _jax.random.split(_jax.random.fold_in(key, i + 1))
        out[name] = w
        out["m_" + name] = s * _jax.random.normal(km, w.shape, _jnp.float32)
        out["v_" + name] = (s * s) * _jax.random.uniform(kv, w.shape, _jnp.float32, 0.5, 1.5)
    if N_MICROBATCH > 1:
        for name, axis in PER_EXAMPLE_BATCH_AXIS.items():
            out[name] = _to_microbatches(out[name], axis)
    return {'x': out['x'], 'positions': out['positions'], 'mix_norm': out['mix_norm'], 'w_in': out['w_in'], 'a_v_norm_g': out['a_v_norm_g'], 'a_v_norm_b': out['a_v_norm_b'], 'a_spatial_w': out['a_spatial_w'], 'a_spatial_b': out['a_spatial_b'], 'q_a_norm': out['q_a_norm'], 'w_uq': out['w_uq'], 'kv_a_norm': out['kv_a_norm'], 'w_ukv': out['w_ukv'], 'w_out': out['w_out'], 'ffn_norm': out['ffn_norm'], 'w_up': out['w_up'], 'conv_w': out['conv_w'], 'conv_b': out['conv_b'], 'w_down': out['w_down'], 'final_norm': out['final_norm'], 'loss_target': out['loss_target'], 'm_mix_norm': out['m_mix_norm'], 'm_w_in': out['m_w_in'], 'm_a_v_norm_g': out['m_a_v_norm_g'], 'm_a_v_norm_b': out['m_a_v_norm_b'], 'm_a_spatial_w': out['m_a_spatial_w'], 'm_a_spatial_b': out['m_a_spatial_b'], 'm_q_a_norm': out['m_q_a_norm'], 'm_w_uq': out['m_w_uq'], 'm_kv_a_norm': out['m_kv_a_norm'], 'm_w_ukv': out['m_w_ukv'], 'm_w_out': out['m_w_out'], 'm_ffn_norm': out['m_ffn_norm'], 'm_w_up': out['m_w_up'], 'm_conv_w': out['m_conv_w'], 'm_conv_b': out['m_conv_b'], 'm_w_down': out['m_w_down'], 'm_final_norm': out['m_final_norm'], 'v_mix_norm': out['v_mix_norm'], 'v_w_in': out['v_w_in'], 'v_a_v_norm_g': out['v_a_v_norm_g'], 'v_a_v_norm_b': out['v_a_v_norm_b'], 'v_a_spatial_w': out['v_a_spatial_w'], 'v_a_spatial_b': out['v_a_spatial_b'], 'v_q_a_norm': out['v_q_a_norm'], 'v_w_uq': out['v_w_uq'], 'v_kv_a_norm': out['v_kv_a_norm'], 'v_w_ukv': out['v_w_ukv'], 'v_w_out': out['v_w_out'], 'v_ffn_norm': out['v_ffn_norm'], 'v_w_up': out['v_w_up'], 'v_conv_w': out['v_conv_w'], 'v_conv_b': out['v_conv_b'], 'v_w_down': out['v_w_down'], 'v_final_norm': out['v_final_norm']}


def _loss(weights, diff, rest, loss_target):
    with _jax.named_scope("forward"):
        args = {**rest, TWIN_DIFF_INPUT: diff, **{k: w.astype(_WEIGHT_DTYPES[k]) for k, w in weights.items()}}
        y = _forward(args)
    with _jax.named_scope("loss_head"):
        err = _jnp.square(y.astype(_jnp.float32) - loss_target)
        return 0.5 * _jnp.sum(_jnp.mean(err, axis=-1)) if err.ndim else 0.5 * err


def _adamw(w, g, m, v):
    m = ADAM_B1 * m + (1.0 - ADAM_B1) * g
    v = ADAM_B2 * v + (1.0 - ADAM_B2) * _jnp.square(g)
    m_hat = m / (1.0 - ADAM_B1 ** ADAM_STEP)
    v_hat = v / (1.0 - ADAM_B2 ** ADAM_STEP)
    delta = -ADAM_LR * (m_hat / (_jnp.sqrt(v_hat) + ADAM_EPS) + ADAM_WD * w)
    return delta, m, v


def reference(x, positions, mix_norm, w_in, a_v_norm_g, a_v_norm_b, a_spatial_w, a_spatial_b, q_a_norm, w_uq, kv_a_norm, w_ukv, w_out, ffn_norm, w_up, conv_w, conv_b, w_down, final_norm, loss_target, m_mix_norm, m_w_in, m_a_v_norm_g, m_a_v_norm_b, m_a_spatial_w, m_a_spatial_b, m_q_a_norm, m_w_uq, m_kv_a_norm, m_w_ukv, m_w_out, m_ffn_norm, m_w_up, m_conv_w, m_conv_b, m_w_down, m_final_norm, v_mix_norm, v_w_in, v_a_v_norm_g, v_a_v_norm_b, v_a_spatial_w, v_a_spatial_b, v_q_a_norm, v_w_uq, v_kv_a_norm, v_w_ukv, v_w_out, v_ffn_norm, v_w_up, v_conv_w, v_conv_b, v_w_down, v_final_norm):
    given = dict(x=x, positions=positions, mix_norm=mix_norm, w_in=w_in, a_v_norm_g=a_v_norm_g, a_v_norm_b=a_v_norm_b, a_spatial_w=a_spatial_w, a_spatial_b=a_spatial_b, q_a_norm=q_a_norm, w_uq=w_uq, kv_a_norm=kv_a_norm, w_ukv=w_ukv, w_out=w_out, ffn_norm=ffn_norm, w_up=w_up, conv_w=conv_w, conv_b=conv_b, w_down=w_down, final_norm=final_norm, loss_target=loss_target, m_mix_norm=m_mix_norm, m_w_in=m_w_in, m_a_v_norm_g=m_a_v_norm_g, m_a_v_norm_b=m_a_v_norm_b, m_a_spatial_w=m_a_spatial_w, m_a_spatial_b=m_a_spatial_b, m_q_a_norm=m_q_a_norm, m_w_uq=m_w_uq, m_kv_a_norm=m_kv_a_norm, m_w_ukv=m_w_ukv, m_w_out=m_w_out, m_ffn_norm=m_ffn_norm, m_w_up=m_w_up, m_conv_w=m_conv_w, m_conv_b=m_conv_b, m_w_down=m_w_down, m_final_norm=m_final_norm, v_mix_norm=v_mix_norm, v_w_in=v_w_in, v_a_v_norm_g=v_a_v_norm_g, v_a_v_norm_b=v_a_v_norm_b, v_a_spatial_w=v_a_spatial_w, v_a_spatial_b=v_a_spatial_b, v_q_a_norm=v_q_a_norm, v_w_uq=v_w_uq, v_kv_a_norm=v_kv_a_norm, v_w_ukv=v_w_ukv, v_w_out=v_w_out, v_ffn_norm=v_ffn_norm, v_w_up=v_w_up, v_conv_w=v_conv_w, v_conv_b=v_conv_b, v_w_down=v_w_down, v_final_norm=v_final_norm)
    weights = {n: given[n] for n in TWIN_WEIGHTS}
    shared = {n: given[n] for n in SHARED_INPUTS}
    per_example = {n: given[n] for n in ['x', 'positions']}
    grad_fn = _jax.value_and_grad(_loss, argnums=(0, 1))

    def one_microbatch(ex, loss_target):
        ex = dict(ex)
        diff = ex.pop(TWIN_DIFF_INPUT)
        return grad_fn(weights, diff, {**shared, **ex}, loss_target)

    if N_MICROBATCH == 1:
        loss, (grad_w, grad_x) = one_microbatch(per_example, given["loss_target"])
    else:
        def body(carry, xs):
            loss_sum, grad_sum = carry
            l_k, (gw_k, gx_k) = one_microbatch(xs[0], xs[1])
            with _jax.named_scope("update"):
                return (loss_sum + l_k, _jax.tree.map(_jnp.add, grad_sum, gw_k)), gx_k

        init = (_jnp.zeros((), _jnp.float32), _jax.tree.map(_jnp.zeros_like, weights))
        (loss, grad_w), grad_x = _jax.lax.scan(body, init, (per_example, given["loss_target"]))
    with _jax.named_scope("update"):
        delta_w, new_m, new_v = {}, {}, {}
        for n in TWIN_WEIGHTS:
            delta_w[n], new_m[n], new_v[n] = _adamw(weights[n], grad_w[n], given["m_" + n], given["v_" + n])
    return (loss, grad_x, *[grad_w[n] for n in TWIN_WEIGHTS], *[delta_w[n] for n in TWIN_WEIGHTS],
            *[new_m[n] for n in TWIN_WEIGHTS], *[new_v[n] for n in TWIN_WEIGHTS])
```

```python
import functools
import math

import jax
import jax.numpy as jnp
from jax import lax
from jax.experimental import pallas as pl
from jax.experimental.pallas import tpu as pltpu

F32 = jnp.float32
BF16 = jnp.bfloat16

N_DEV = 8
D_MODEL = 1024
EPS = 1e-6
A_GROUPS = 8
CHUNK = 128
MLA_HEADS = 8
QK_NOPE = 128
QK_ROPE = 64
QK_HEAD = QK_NOPE + QK_ROPE
HEAD_PAD = 256
V_HEAD = 128
Q_LORA = 256
KV_LORA = 128
ROPE_THETA = 10000.0
D_FF = 2816
ZS_W = 512
ATTN_SCALE = QK_HEAD ** -0.5
NEG_BIG = -1e30

ADAM_LR = 0.001
ADAM_B1 = 0.9
ADAM_B2 = 0.999
ADAM_EPS = 1e-08
ADAM_WD = 0.01
ADAM_STEP = 10

VMEM_LIMIT = 56 * 1024 * 1024
LANES = 128

GELU_K = math.sqrt(2.0 / math.pi)
GELU_C = 0.044715


def _tile(n, pref):
    for t in (pref, 512, 256, 128, 64, 32, 16, 8):
        if t <= pref and n % t == 0:
            return t
    return n


def _params(*sem):
    return pltpu.CompilerParams(dimension_semantics=sem, vmem_limit_bytes=VMEM_LIMIT)


def _dot(a, b):
    return jnp.dot(a, b, preferred_element_type=F32)


def _dot_nt(a, b):
    return lax.dot_general(a, b, (((1,), (1,)), ((), ())), preferred_element_type=F32)


def _dot_tn(a, b):
    return lax.dot_general(a, b, (((0,), (0,)), ((), ())), preferred_element_type=F32)


def _sigmoid(x):
    return 1.0 / (1.0 + jnp.exp(-x))


def _gelu(x):
    t = jnp.tanh(GELU_K * (x + GELU_C * x * x * x))
    return 0.5 * x * (1.0 + t)


def _gelu_grad(x):
    t = jnp.tanh(GELU_K * (x + GELU_C * x * x * x))
    return 0.5 * (1.0 + t) + 0.5 * x * (1.0 - t * t) * GELU_K * (1.0 + 3.0 * GELU_C * x * x)


def _norm_mm(x, g, w, name):
    T, Dm = x.shape
    N = w.shape[1]
    tm, tn = _tile(T, 512), _tile(N, 512)

    def body(x_ref, g_ref, w_ref, h_ref, z_ref):
        @pl.when(pl.program_id(1) == 0)
        def _():
            xf = x_ref[...]
            r = lax.rsqrt(jnp.mean(xf * xf, axis=-1, keepdims=True) + EPS)
            h_ref[...] = (xf * r * g_ref[...]).astype(BF16)

        z_ref[...] = _dot(h_ref[...], w_ref[...])

    return pl.pallas_call(
        body, grid=(T // tm, N // tn),
        in_specs=[pl.BlockSpec((tm, Dm), lambda i, j: (i, 0)),
                  pl.BlockSpec((1, Dm), lambda i, j: (0, 0)),
                  pl.BlockSpec((Dm, tn), lambda i, j: (0, j))],
        out_specs=[pl.BlockSpec((tm, Dm), lambda i, j: (i, 0)),
                   pl.BlockSpec((tm, tn), lambda i, j: (i, j))],
        out_shape=[jax.ShapeDtypeStruct((T, Dm), BF16), jax.ShapeDtypeStruct((T, N), F32)],
        name=name, compiler_params=_params("parallel", "arbitrary"))(x, g, w)


def _mm_nn(a, w, out_dtype, name):
    T, K = a.shape
    N = w.shape[1]
    tm, tn = _tile(T, 512), _tile(N, 512)

    def body(a_ref, w_ref, o_ref):
        o_ref[...] = _dot(a_ref[...].astype(BF16), w_ref[...]).astype(o_ref.dtype)

    return pl.pallas_call(
        body, grid=(T // tm, N // tn),
        in_specs=[pl.BlockSpec((tm, K), lambda i, j: (i, 0)),
                  pl.BlockSpec((K, tn), lambda i, j: (0, j))],
        out_specs=pl.BlockSpec((tm, tn), lambda i, j: (i, j)),
        out_shape=jax.ShapeDtypeStruct((T, N), out_dtype),
        name=name, compiler_params=_params("parallel", "parallel"))(a, w)


def _mm_nt(pairs, out_dtype, name):
    T = pairs[0][0].shape[0]
    K = pairs[0][1].shape[0]
    tm, tk = _tile(T, 512), _tile(K, 512)
    n_pairs = len(pairs)

    def body(*refs):
        o_ref = refs[-1]
        acc = None
        for p in range(n_pairs):
            t = _dot_nt(refs[2 * p][...].astype(BF16), refs[2 * p + 1][...])
            acc = t if acc is None else acc + t
        o_ref[...] = acc.astype(o_ref.dtype)

    in_specs, args = [], []
    for a, w, c in pairs:
        in_specs += [pl.BlockSpec((tm, a.shape[1]), lambda i, j: (i, 0)),
                     pl.BlockSpec((tk, a.shape[1]), functools.partial(lambda i, j, c: (j, c), c=c))]
        args += [a, w]
    return pl.pallas_call(
        body, grid=(T // tm, K // tk), in_specs=in_specs,
        out_specs=pl.BlockSpec((tm, tk), lambda i, j: (i, j)),
        out_shape=jax.ShapeDtypeStruct((T, K), out_dtype),
        name=name, compiler_params=_params("parallel", "parallel"))(*args)


def _mm_tn(a, b, name):
    T, M = a.shape
    N = b.shape[1]
    tm, tn, tt = _tile(M, 512), _tile(N, 512), _tile(T, 512)

    def body(a_ref, b_ref, o_ref):
        @pl.when(pl.program_id(2) == 0)
        def _():
            o_ref[...] = jnp.zeros_like(o_ref)

        o_ref[...] += _dot_tn(a_ref[...].astype(BF16), b_ref[...].astype(BF16))

    return pl.pallas_call(
        body, grid=(M // tm, N // tn, T // tt),
        in_specs=[pl.BlockSpec((tt, tm), lambda i, j, t: (t, i)),
                  pl.BlockSpec((tt, tn), lambda i, j, t: (t, j))],
        out_specs=pl.BlockSpec((tm, tn), lambda i, j, t: (i, j)),
        out_shape=jax.ShapeDtypeStruct((M, N), F32),
        name=name, compiler_params=_params("parallel", "parallel", "arbitrary"))(a, b)


def _rms_bwd(x, g, dy, dres, name):
    T, Dm = x.shape
    tm = _tile(T, 512)

    def body(x_ref, g_ref, dy_ref, dres_ref, dx_ref, dg_ref):
        @pl.when(pl.program_id(0) == 0)
        def _():
            dg_ref[...] = jnp.zeros_like(dg_ref)

        xf = x_ref[...]
        r = lax.rsqrt(jnp.mean(xf * xf, axis=-1, keepdims=True) + EPS)
        xh = xf * r
        dyv = dy_ref[...]
        dg_ref[...] += jnp.sum(dyv * xh, axis=0, keepdims=True)
        dxh = dyv * g_ref[...]
        dx_ref[...] = dres_ref[...] + r * (dxh - xh * jnp.mean(dxh * xh, axis=-1, keepdims=True))

    row = pl.BlockSpec((tm, Dm), lambda i: (i, 0))
    vec = pl.BlockSpec((1, Dm), lambda i: (0, 0))
    return pl.pallas_call(
        body, grid=(T // tm,), in_specs=[row, vec, row, row], out_specs=[row, vec],
        out_shape=[jax.ShapeDtypeStruct((T, Dm), F32), jax.ShapeDtypeStruct((1, Dm), F32)],
        name=name, compiler_params=_params("arbitrary"))(x, g, dy, dres)


def _layer_norm_fwd(gv, g, b):
    mu = jnp.mean(gv, axis=-1, keepdims=True)
    xc = gv - mu
    rs = lax.rsqrt(jnp.mean(xc * xc, axis=-1, keepdims=True) + EPS)
    xh = xc * rs
    return xh, rs, xh * g + b


def _tri_mask(transposed=False):
    r = lax.broadcasted_iota(jnp.int32, (CHUNK, CHUNK), 0)
    c = lax.broadcasted_iota(jnp.int32, (CHUNK, CHUNK), 1)
    return r <= c if transposed else c <= r


def _mixer_a_fwd(zm, av_g, av_b, w_s, b_col):
    T = zm.shape[0]
    tm = _tile(T, 256)
    n_chunk = tm // CHUNK

    def body(u_ref, v_ref, ga_ref, g_ref, b_ref, w_ref, bc_ref, y_ref, vn_s, mx_s):
        gu = _gelu(u_ref[...])
        _, _, vn = _layer_norm_fwd(_gelu(v_ref[...]), g_ref[...], b_ref[...])
        vn_s[...] = vn.astype(BF16)
        tri = _tri_mask()
        for gi in range(A_GROUPS):
            wm = jnp.where(tri, w_ref[gi], 0.0).astype(BF16)
            cols = slice(gi * CHUNK, (gi + 1) * CHUNK)
            for n in range(n_chunk):
                rows = slice(n * CHUNK, (n + 1) * CHUNK)
                mx_s[rows, cols] = _dot(wm, vn_s[rows, cols]) + bc_ref[gi]
        y_ref[...] = _sigmoid(ga_ref[...]) * gu * mx_s[...]

    col = lambda c: pl.BlockSpec((tm, D_MODEL), lambda i: (i, c))
    vec = pl.BlockSpec((1, D_MODEL), lambda i: (0, 0))
    return pl.pallas_call(
        body, grid=(T // tm,),
        in_specs=[col(0), col(1), col(2), vec, vec,
                  pl.BlockSpec((A_GROUPS, CHUNK, CHUNK), lambda i: (0, 0, 0)),
                  pl.BlockSpec((A_GROUPS, CHUNK, 1), lambda i: (0, 0, 0))],
        out_specs=pl.BlockSpec((tm, D_MODEL), lambda i: (i, 0)),
        out_shape=jax.ShapeDtypeStruct((T, D_MODEL), F32),
        scratch_shapes=[pltpu.VMEM((tm, D_MODEL), BF16), pltpu.VMEM((tm, D_MODEL), F32)],
        name="mixer_a_fwd", compiler_params=_params("parallel"))(zm, zm, zm, av_g, av_b, w_s, b_col)


def _mixer_bwd(zm, o, dm, av_g, av_b, w_s, w_st, b_col):
    T = zm.shape[0]
    tm = _tile(T, 256)
    n_chunk = tm // CHUNK

    def body(u_ref, v_ref, ga_ref, gb_ref, o_ref, dm_ref, g_ref, b_ref, w_ref, wt_ref, bc_ref,
             dz_ref, do_ref, dg_ref, db_ref, dw_ref, dbs_ref, vn_s, mx_s, dmx_s, dvn_s):
        @pl.when(pl.program_id(0) == 0)
        def _():
            dg_ref[...] = jnp.zeros_like(dg_ref)
            db_ref[...] = jnp.zeros_like(db_ref)
            dw_ref[...] = jnp.zeros_like(dw_ref)
            dbs_ref[...] = jnp.zeros_like(dbs_ref)

        dm_v = dm_ref[...]
        gb = gb_ref[...]
        sb = _sigmoid(gb)
        o_v = o_ref[...]
        do_ref[...] = (dm_v * sb).astype(BF16)
        dz_ref[:, 3 * D_MODEL:4 * D_MODEL] = (dm_v * o_v * sb * (1.0 - sb)).astype(BF16)
        u = u_ref[...]
        v = v_ref[...]
        gu = _gelu(u)
        xh, rs, vn = _layer_norm_fwd(_gelu(v), g_ref[...], b_ref[...])
        vn_s[...] = vn.astype(BF16)
        tri = _tri_mask()
        for gi in range(A_GROUPS):
            wm = jnp.where(tri, w_ref[gi], 0.0).astype(BF16)
            cols = slice(gi * CHUNK, (gi + 1) * CHUNK)
            for n in range(n_chunk):
                rows = slice(n * CHUNK, (n + 1) * CHUNK)
                mx_s[rows, cols] = _dot(wm, vn_s[rows, cols]) + bc_ref[gi]
        mixed = mx_s[...]
        sa = _sigmoid(ga_ref[...])
        dya = dm_v * sa
        dz_ref[:, 2 * D_MODEL:3 * D_MODEL] = (dm_v * gu * mixed * sa * (1.0 - sa)).astype(BF16)
        dz_ref[:, 0:D_MODEL] = (dya * mixed * _gelu_grad(u)).astype(BF16)
        dmx = dya * gu
        dmx_s[...] = dmx.astype(BF16)
        tri_t = _tri_mask(transposed=True)
        for gi in range(A_GROUPS):
            wmt = jnp.where(tri_t, wt_ref[gi], 0.0).astype(BF16)
            cols = slice(gi * CHUNK, (gi + 1) * CHUNK)
            dw_acc = jnp.zeros((CHUNK, CHUNK), F32)
            dmx_sum = jnp.zeros((CHUNK, CHUNK), F32)
            for n in range(n_chunk):
                rows = slice(n * CHUNK, (n + 1) * CHUNK)
                blk = dmx_s[rows, cols]
                dvn_s[rows, cols] = _dot(wmt, blk)
                dw_acc = dw_acc + _dot_nt(blk, vn_s[rows, cols])
                dmx_sum = dmx_sum + dmx[rows, cols]
            dw_ref[gi] += jnp.where(tri, dw_acc, 0.0)
            dbs_ref[gi] += jnp.sum(dmx_sum, axis=-1, keepdims=True)
        dvn = dvn_s[...]
        dg_ref[...] += jnp.sum(dvn * xh, axis=0, keepdims=True)
        db_ref[...] += jnp.sum(dvn, axis=0, keepdims=True)
        dxh = dvn * g_ref[...]
        dgv = rs * (dxh - jnp.mean(dxh, axis=-1, keepdims=True)
                    - xh * jnp.mean(dxh * xh, axis=-1, keepdims=True))
        dz_ref[:, D_MODEL:2 * D_MODEL] = (dgv * _gelu_grad(v)).astype(BF16)

    col = lambda c: pl.BlockSpec((tm, D_MODEL), lambda i: (i, c))
    row = pl.BlockSpec((tm, D_MODEL), lambda i: (i, 0))
    vec = pl.BlockSpec((1, D_MODEL), lambda i: (0, 0))
    wsp = pl.BlockSpec((A_GROUPS, CHUNK, CHUNK), lambda i: (0, 0, 0))
    bsp = pl.BlockSpec((A_GROUPS, CHUNK, 1), lambda i: (0, 0, 0))
    return pl.pallas_call(
        body, grid=(T // tm,),
        in_specs=[col(0), col(1), col(2), col(3), row, row, vec, vec, wsp, wsp, bsp],
        out_specs=[pl.BlockSpec((tm, 4 * D_MODEL), lambda i: (i, 0)), row, vec, vec, wsp, bsp],
        out_shape=[jax.ShapeDtypeStruct((T, 4 * D_MODEL), BF16), jax.ShapeDtypeStruct((T, D_MODEL), BF16),
                   jax.ShapeDtypeStruct((1, D_MODEL), F32), jax.ShapeDtypeStruct((1, D_MODEL), F32),
                   jax.ShapeDtypeStruct((A_GROUPS, CHUNK, CHUNK), F32),
                   jax.ShapeDtypeStruct((A_GROUPS, CHUNK, 1), F32)],
        scratch_shapes=[pltpu.VMEM((tm, D_MODEL), BF16), pltpu.VMEM((tm, D_MODEL), F32),
                        pltpu.VMEM((tm, D_MODEL), BF16), pltpu.VMEM((tm, D_MODEL), F32)],
        name="mixer_bwd", compiler_params=_params("arbitrary"))(
            zm, zm, zm, zm, o, dm, av_g, av_b, w_s, w_st, b_col)


def _rope_tables(pos_ref, invf_ref):
    ang = pos_ref[...].astype(F32) * invf_ref[...]
    lane = lax.broadcasted_iota(jnp.int32, ang.shape, 1)
    cos, sin = jnp.cos(ang), jnp.sin(ang)
    c = jnp.where(lane < QK_ROPE, cos, 0.0)
    sa = jnp.where(lane < QK_ROPE // 2, -sin, 0.0)
    sb = jnp.where((lane >= QK_ROPE // 2) & (lane < QK_ROPE), sin, 0.0)
    return c, sa, sb


def _rope(blk, tabs):
    c, sa, sb = tabs
    return blk * c + pltpu.roll(blk, LANES - QK_ROPE // 2, 1) * sa + pltpu.roll(blk, QK_ROPE // 2, 1) * sb


def _rope_t(dout, tabs):
    c, sa, sb = tabs
    return dout * c + pltpu.roll(dout * sa, QK_ROPE // 2, 1) + pltpu.roll(dout * sb, LANES - QK_ROPE // 2, 1)


def _rms_small(x, g):
    r = lax.rsqrt(jnp.mean(x * x, axis=-1, keepdims=True) + EPS)
    xh = x * r
    return xh, r, xh * g


def _mla_prep_fwd(zs, pos, invf, qg, kvg, wuq_p, wukv):
    T = zs.shape[0]
    tm = _tile(T, 512)
    HW = MLA_HEADS * HEAD_PAD

    def body(zs_ref, pos_ref, invf_ref, qg_ref, kvg_ref, wq_ref, wkv_ref, q_ref, k_ref, v_ref):
        tabs = _rope_tables(pos_ref, invf_ref)
        _, _, cqn = _rms_small(zs_ref[:, 0:Q_LORA], qg_ref[...])
        _, _, ckvn = _rms_small(zs_ref[:, Q_LORA:Q_LORA + KV_LORA], kvg_ref[...])
        q = _dot(cqn.astype(BF16), wq_ref[...])
        kv = _dot(ckvn.astype(BF16), wkv_ref[...])
        kr = _rope(zs_ref[:, Q_LORA + KV_LORA:ZS_W], tabs).astype(BF16)
        for h in range(MLA_HEADS):
            b0 = h * HEAD_PAD
            q_ref[:, b0:b0 + QK_NOPE] = q[:, b0:b0 + QK_NOPE].astype(BF16)
            q_ref[:, b0 + QK_NOPE:b0 + HEAD_PAD] = _rope(q[:, b0 + QK_NOPE:b0 + HEAD_PAD], tabs).astype(BF16)
            k_ref[:, b0:b0 + QK_NOPE] = kv[:, b0:b0 + QK_NOPE].astype(BF16)
            k_ref[:, b0 + QK_NOPE:b0 + HEAD_PAD] = kr
            v_ref[:, h * V_HEAD:(h + 1) * V_HEAD] = kv[:, b0 + QK_NOPE:b0 + HEAD_PAD].astype(BF16)

    full = lambda a: pl.BlockSpec(a.shape, lambda i: (0,) * a.ndim)
    return pl.pallas_call(
        body, grid=(T // tm,),
        in_specs=[pl.BlockSpec((tm, ZS_W), lambda i: (i, 0)), pl.BlockSpec((tm, 1), lambda i: (i, 0)),
                  full(invf), full(qg), full(kvg), full(wuq_p), full(wukv)],
        out_specs=[pl.BlockSpec((tm, HW), lambda i: (i, 0)), pl.BlockSpec((tm, HW), lambda i: (i, 0)),
                   pl.BlockSpec((tm, D_MODEL), lambda i: (i, 0))],
        out_shape=[jax.ShapeDtypeStruct((T, HW), BF16), jax.ShapeDtypeStruct((T, HW), BF16),
                   jax.ShapeDtypeStruct((T, D_MODEL), BF16)],
        name="mla_prep_fwd", compiler_params=_params("parallel"))(zs, pos, invf, qg, kvg, wuq_p, wukv)


def _mla_prep_bwd(zs, pos, invf, qg, kvg, wuq_p, wukv, dq, dk, dv):
    T = zs.shape[0]
    tm = _tile(T, 256)
    HW = MLA_HEADS * HEAD_PAD

    def body(zs_ref, pos_ref, invf_ref, qg_ref, kvg_ref, wq_ref, wkv_ref, dq_ref, dk_ref, dv_ref,
             dzs_ref, cqn_ref, dqp_ref, ckvn_ref, dkv_ref, dqg_ref, dkvg_ref):
        @pl.when(pl.program_id(0) == 0)
        def _():
            dqg_ref[...] = jnp.zeros_like(dqg_ref)
            dkvg_ref[...] = jnp.zeros_like(dkvg_ref)

        tabs = _rope_tables(pos_ref, invf_ref)
        cqh, rq, cqn = _rms_small(zs_ref[:, 0:Q_LORA], qg_ref[...])
        ckvh, rkv, ckvn = _rms_small(zs_ref[:, Q_LORA:Q_LORA + KV_LORA], kvg_ref[...])
        cqn_ref[...] = cqn.astype(BF16)
        ckvn_ref[...] = ckvn.astype(BF16)
        dkr = jnp.zeros((tm, LANES), F32)
        for h in range(MLA_HEADS):
            b0 = h * HEAD_PAD
            dqp_ref[:, b0:b0 + QK_NOPE] = dq_ref[:, b0:b0 + QK_NOPE].astype(BF16)
            dqp_ref[:, b0 + QK_NOPE:b0 + HEAD_PAD] = _rope_t(dq_ref[:, b0 + QK_NOPE:b0 + HEAD_PAD], tabs).astype(BF16)
            dkv_ref[:, b0:b0 + QK_NOPE] = dk_ref[:, b0:b0 + QK_NOPE].astype(BF16)
            dkv_ref[:, b0 + QK_NOPE:b0 + HEAD_PAD] = dv_ref[:, h * V_HEAD:(h + 1) * V_HEAD].astype(BF16)
            dkr = dkr + dk_ref[:, b0 + QK_NOPE:b0 + HEAD_PAD]
        dcqn = _dot_nt(dqp_ref[...], wq_ref[...])
        dckvn = _dot_nt(dkv_ref[...], wkv_ref[...])
        dqg_ref[...] += jnp.sum(dcqn * cqh, axis=0, keepdims=True)
        dkvg_ref[...] += jnp.sum(dckvn * ckvh, axis=0, keepdims=True)
        dxh = dcqn * qg_ref[...]
        dzs_ref[:, 0:Q_LORA] = (rq * (dxh - cqh * jnp.mean(dxh * cqh, axis=-1, keepdims=True))).astype(BF16)
        dxh = dckvn * kvg_ref[...]
        dzs_ref[:, Q_LORA:Q_LORA + KV_LORA] = (
            rkv * (dxh - ckvh * jnp.mean(dxh * ckvh, axis=-1, keepdims=True))).astype(BF16)
        dzs_ref[:, Q_LORA + KV_LORA:ZS_W] = _rope_t(dkr, tabs).astype(BF16)

    full = lambda a: pl.BlockSpec(a.shape, lambda i: (0,) * a.ndim)
    rowb = lambda w: pl.BlockSpec((tm, w), lambda i: (i, 0))
    return pl.pallas_call(
        body, grid=(T // tm,),
        in_specs=[rowb(ZS_W), rowb(1), full(invf), full(qg), full(kvg), full(wuq_p), full(wukv),
                  rowb(HW), rowb(HW), rowb(D_MODEL)],
        out_specs=[rowb(ZS_W), rowb(Q_LORA), rowb(HW), rowb(KV_LORA), rowb(HW), full(qg), full(kvg)],
        out_shape=[jax.ShapeDtypeStruct((T, ZS_W), BF16), jax.ShapeDtypeStruct((T, Q_LORA), BF16),
                   jax.ShapeDtypeStruct((T, HW), BF16), jax.ShapeDtypeStruct((T, KV_LORA), BF16),
                   jax.ShapeDtypeStruct((T, HW), BF16), jax.ShapeDtypeStruct(qg.shape, F32),
                   jax.ShapeDtypeStruct(kvg.shape, F32)],
        name="mla_prep_bwd", compiler_params=_params("arbitrary"))(
            zs, pos, invf, qg, kvg, wuq_p, wukv, dq, dk, dv)


def _causal(tq, kmax, q0):
    r = lax.broadcasted_iota(jnp.int32, (tq, kmax), 0) + q0
    c = lax.broadcasted_iota(jnp.int32, (tq, kmax), 1)
    return c <= r


def _attn_fwd(q, k, v, batch, seq):
    tq = _tile(seq, 512)
    nq = seq // tq

    def body(q_ref, k_ref, v_ref, o_ref, lse_ref):
        for qi in range(nq):
            rows = slice(qi * tq, (qi + 1) * tq)
            kmax = (qi + 1) * tq
            s = _dot_nt(q_ref[rows, :], k_ref[0:kmax, :]) * ATTN_SCALE
            s = jnp.where(_causal(tq, kmax, qi * tq), s, NEG_BIG)
            m = jnp.max(s, axis=-1, keepdims=True)
            p = jnp.exp(s - m)
            l = jnp.sum(p, axis=-1, keepdims=True)
            o_ref[rows, :] = _dot(p.astype(BF16), v_ref[0:kmax, :]) / l
            lse_ref[rows, :] = jnp.broadcast_to(m + jnp.log(l), (tq, V_HEAD))

    return pl.pallas_call(
        body, grid=(batch, MLA_HEADS),
        in_specs=[pl.BlockSpec((seq, HEAD_PAD), lambda b, h: (b, h)),
                  pl.BlockSpec((seq, HEAD_PAD), lambda b, h: (b, h)),
                  pl.BlockSpec((seq, V_HEAD), lambda b, h: (b, h))],
        out_specs=[pl.BlockSpec((seq, V_HEAD), lambda b, h: (b, h)),
                   pl.BlockSpec((seq, V_HEAD), lambda b, h: (b, h))],
        out_shape=[jax.ShapeDtypeStruct((batch * seq, D_MODEL), F32),
                   jax.ShapeDtypeStruct((batch * seq, D_MODEL), F32)],
        name="attn_fwd", compiler_params=_params("parallel", "parallel"))(q, k, v)


def _attn_bwd(q, k, v, o, do, lse, batch, seq):
    tq = _tile(seq, 512)
    nq = seq // tq

    def body(q_ref, k_ref, v_ref, o_ref, do_ref, lse_ref, dq_ref, dk_ref, dv_ref):
        dk_ref[...] = jnp.zeros_like(dk_ref)
        dv_ref[...] = jnp.zeros_like(dv_ref)
        for qi in range(nq):
            rows = slice(qi * tq, (qi + 1) * tq)
            kmax = (qi + 1) * tq
            qr = q_ref[rows, :]
            dor = do_ref[rows, :]
            kk = k_ref[0:kmax, :]
            s = _dot_nt(qr, kk) * ATTN_SCALE
            p = jnp.where(_causal(tq, kmax, qi * tq), jnp.exp(s - lse_ref[rows, 0:1]), 0.0)
            dp = _dot_nt(dor, v_ref[0:kmax, :])
            delta = jnp.sum(dor.astype(F32) * o_ref[rows, :], axis=-1, keepdims=True)
            ds = (p * (dp - delta) * ATTN_SCALE).astype(BF16)
            dq_ref[rows, :] = _dot(ds, kk)
            dk_ref[0:kmax, :] += _dot_tn(ds, qr)
            dv_ref[0:kmax, :] += _dot_tn(p.astype(BF16), dor)

    qspec = pl.BlockSpec((seq, HEAD_PAD), lambda b, h: (b, h))
    vspec = pl.BlockSpec((seq, V_HEAD), lambda b, h: (b, h))
    T = batch * seq
    return pl.pallas_call(
        body, grid=(batch, MLA_HEADS),
        in_specs=[qspec, qspec, vspec, vspec, vspec, vspec],
        out_specs=[qspec, qspec, vspec],
        out_shape=[jax.ShapeDtypeStruct((T, MLA_HEADS * HEAD_PAD), F32),
                   jax.ShapeDtypeStruct((T, MLA_HEADS * HEAD_PAD), F32),
                   jax.ShapeDtypeStruct((T, D_MODEL), F32)],
        name="attn_bwd", compiler_params=_params("parallel", "parallel"))(q, k, v, o, do, lse)


def _merge_out(x, yag, zm, o, w_out):
    T = x.shape[0]
    tm = _tile(T, 512)

    def body(x_ref, ya_ref, gb_ref, o_ref, w_ref, mg_ref, x1_ref):
        mg = (ya_ref[...] + _sigmoid(gb_ref[...]) * o_ref[...]).astype(BF16)
        mg_ref[...] = mg
        x1_ref[...] = x_ref[...] + _dot(mg, w_ref[...])

    row = pl.BlockSpec((tm, D_MODEL), lambda i: (i, 0))
    return pl.pallas_call(
        body, grid=(T // tm,),
        in_specs=[row, row, pl.BlockSpec((tm, D_MODEL), lambda i: (i, 3)), row,
                  pl.BlockSpec((D_MODEL, D_MODEL), lambda i: (0, 0))],
        out_specs=[row, row],
        out_shape=[jax.ShapeDtypeStruct((T, D_MODEL), BF16), jax.ShapeDtypeStruct((T, D_MODEL), F32)],
        name="merge_out", compiler_params=_params("parallel"))(x, yag, zm, o, w_out)


FF_TILE = 256
FF_BLOCKS = D_FF // FF_TILE


def _shift_down(x, k):
    row = lax.broadcasted_iota(jnp.int32, x.shape, 0)
    return jnp.where(row >= k, pltpu.roll(x, k, 0), 0.0)


def _shift_up(x, k):
    n = x.shape[0]
    row = lax.broadcasted_iota(jnp.int32, x.shape, 0)
    return jnp.where(row < n - k, pltpu.roll(x, n - k, 0), 0.0)


def _conv(x, w_ref, b_ref):
    return b_ref[...] + w_ref[2:3, :] * x + w_ref[1:2, :] * _shift_down(x, 1) + w_ref[0:1, :] * _shift_down(x, 2)


def _ffn_act(up, cw, cb, batch, seq):
    def body(ug_ref, uv_ref, wg_ref, wv_ref, bg_ref, bv_ref, a_ref):
        gate = _conv(ug_ref[...], wg_ref, bg_ref)
        val = _conv(uv_ref[...], wv_ref, bv_ref)
        a_ref[...] = (gate * _sigmoid(gate) * val).astype(BF16)

    blk = lambda off: pl.BlockSpec((seq, FF_TILE), lambda b, j: (b, j + off))
    wsp = lambda off: pl.BlockSpec((3, FF_TILE), lambda b, j: (0, j + off))
    bsp = lambda off: pl.BlockSpec((1, FF_TILE), lambda b, j: (0, j + off))
    return pl.pallas_call(
        body, grid=(batch, FF_BLOCKS),
        in_specs=[blk(0), blk(FF_BLOCKS), wsp(0), wsp(FF_BLOCKS), bsp(0), bsp(FF_BLOCKS)],
        out_specs=blk(0),
        out_shape=jax.ShapeDtypeStruct((batch * seq, D_FF), BF16),
        name="ffn_act", compiler_params=_params("parallel", "parallel"))(up, up, cw, cw, cb, cb)


def _ffn_act_bwd(up, cw, cb, da, batch, seq):
    def half(du, x, w_ref, dx_ref, dw_ref, db_ref):
        dx_ref[...] = (w_ref[2:3, :] * du + w_ref[1:2, :] * _shift_up(du, 1)
                       + w_ref[0:1, :] * _shift_up(du, 2)).astype(BF16)
        dw_ref[2:3, :] += jnp.sum(du * x, axis=0, keepdims=True)
        dw_ref[1:2, :] += jnp.sum(du * _shift_down(x, 1), axis=0, keepdims=True)
        dw_ref[0:1, :] += jnp.sum(du * _shift_down(x, 2), axis=0, keepdims=True)
        db_ref[...] += jnp.sum(du, axis=0, keepdims=True)

    def body(ug_ref, uv_ref, wg_ref, wv_ref, bg_ref, bv_ref, da_ref,
             dg_ref, dv_ref, dwg_ref, dwv_ref, dbg_ref, dbv_ref):
        @pl.when(pl.program_id(1) == 0)
        def _():
            for r in (dwg_ref, dwv_ref, dbg_ref, dbv_ref):
                r[...] = jnp.zeros_like(r)

        ug, uv = ug_ref[...], uv_ref[...]
        gate = _conv(ug, wg_ref, bg_ref)
        val = _conv(uv, wv_ref, bv_ref)
        sg = _sigmoid(gate)
        dav = da_ref[...]
        half(dav * val * sg * (1.0 + gate * (1.0 - sg)), ug, wg_ref, dg_ref, dwg_ref, dbg_ref)
        half(dav * gate * sg, uv, wv_ref, dv_ref, dwv_ref, dbv_ref)

    blk = lambda off: pl.BlockSpec((seq, FF_TILE), lambda j, b: (b, j + off))
    wsp = lambda off: pl.BlockSpec((3, FF_TILE), lambda j, b: (0, j + off))
    bsp = lambda off: pl.BlockSpec((1, FF_TILE), lambda j, b: (0, j + off))
    T = batch * seq
    return pl.pallas_call(
        body, grid=(FF_BLOCKS, batch),
        in_specs=[blk(0), blk(FF_BLOCKS), wsp(0), wsp(FF_BLOCKS), bsp(0), bsp(FF_BLOCKS), blk(0)],
        out_specs=[blk(0), blk(0), wsp(0), wsp(0), bsp(0), bsp(0)],
        out_shape=[jax.ShapeDtypeStruct((T, D_FF), BF16), jax.ShapeDtypeStruct((T, D_FF), BF16),
                   jax.ShapeDtypeStruct((3, D_FF), F32), jax.ShapeDtypeStruct((3, D_FF), F32),
                   jax.ShapeDtypeStruct((1, D_FF), F32), jax.ShapeDtypeStruct((1, D_FF), F32)],
        name="ffn_act_bwd", compiler_params=_params("parallel", "arbitrary"))(up, up, cw, cw, cb, cb, da)


def _down_loss(a, w_down, x1, target, gfin):
    T = x1.shape[0]
    tm = _tile(T, 512)

    def body(a_ref, w_ref, x1_ref, t_ref, g_ref, dx_ref, loss_ref, dg_ref):
        @pl.when(pl.program_id(0) == 0)
        def _():
            loss_ref[...] = jnp.zeros_like(loss_ref)
            dg_ref[...] = jnp.zeros_like(dg_ref)

        x2 = x1_ref[...] + _dot(a_ref[...], w_ref[...])
        r = lax.rsqrt(jnp.mean(x2 * x2, axis=-1, keepdims=True) + EPS)
        xh = x2 * r
        g = g_ref[...]
        diff = xh * g - t_ref[...]
        loss_ref[...] += 0.5 * jnp.sum(jnp.mean(diff * diff, axis=-1, keepdims=True))
        dy = diff * (1.0 / D_MODEL)
        dg_ref[...] += jnp.sum(dy * xh, axis=0, keepdims=True)
        dxh = dy * g
        dx_ref[...] = r * (dxh - xh * jnp.mean(dxh * xh, axis=-1, keepdims=True))

    row = pl.BlockSpec((tm, D_MODEL), lambda i: (i, 0))
    vec = pl.BlockSpec((1, D_MODEL), lambda i: (0, 0))
    return pl.pallas_call(
        body, grid=(T // tm,),
        in_specs=[pl.BlockSpec((tm, D_FF), lambda i: (i, 0)),
                  pl.BlockSpec((D_FF, D_MODEL), lambda i: (0, 0)), row, row, vec],
        out_specs=[row, pl.BlockSpec((8, LANES), lambda i: (0, 0)), vec],
        out_shape=[jax.ShapeDtypeStruct((T, D_MODEL), F32), jax.ShapeDtypeStruct((8, LANES), F32),
                   jax.ShapeDtypeStruct((1, D_MODEL), F32)],
        name="down_loss", compiler_params=_params("arbitrary"))(a, w_down, x1, target, gfin)


def _local_step(x, positions, target, mix_norm, w_main, w_zs, av_g, av_b, w_s, b_s, q_norm, wuq_p, kv_norm,
                wukv, w_out, ffn_norm, w_up, conv_w, conv_b, w_down, final_norm):
    batch, seq, _ = x.shape
    T = batch * seq
    x = x.reshape(T, D_MODEL)
    target = target.reshape(T, D_MODEL)
    pos = positions.reshape(T, 1)
    half = jnp.arange(0, QK_ROPE, 2, dtype=F32) / QK_ROPE
    inv_freq = 1.0 / (ROPE_THETA ** half)
    invf = jnp.concatenate([inv_freq, inv_freq, jnp.zeros((LANES - QK_ROPE,), F32)]).reshape(1, LANES)
    w_st = jnp.swapaxes(w_s, 1, 2)
    b_col = b_s.reshape(A_GROUPS, CHUNK, 1)

    h, zm = _norm_mm(x, mix_norm, w_main, "in_proj")
    zs = _mm_nn(h, w_zs, F32, "in_proj_small")
    yag = _mixer_a_fwd(zm, av_g, av_b, w_s, b_col)
    q, k, v = _mla_prep_fwd(zs, pos, invf, q_norm, kv_norm, wuq_p, wukv)
    o, lse = _attn_fwd(q, k, v, batch, seq)
    merged, x1 = _merge_out(x, yag, zm, o, w_out)
    h2, up = _norm_mm(x1, ffn_norm, w_up, "up_proj")
    act = _ffn_act(up, conv_w, conv_b, batch, seq)
    dx2, loss_acc, d_final = _down_loss(act, w_down, x1, target, final_norm)

    d_wdown = _mm_tn(act, dx2, "dw_down")
    da = _mm_nt([(dx2, w_down, 0)], F32, "d_act")
    dupg, dupv, dcwg, dcwv, dcbg, dcbv = _ffn_act_bwd(up, conv_w, conv_b, da, batch, seq)
    d_wup = jnp.concatenate([_mm_tn(h2, dupg, "dw_up_gate"), _mm_tn(h2, dupv, "dw_up_val")], axis=1)
    dh2 = _mm_nt([(dupg, w_up, 0), (dupv, w_up, 1)], F32, "d_h2")
    dx1, d_ffn_norm = _rms_bwd(x1, ffn_norm, dh2, dx2, "ffn_norm_bwd")
    d_wout = _mm_tn(merged, dx1, "dw_out")
    dmerged = _mm_nt([(dx1, w_out, 0)], F32, "d_merged")
    dzm, do, d_avg, d_avb, d_ws, d_bs = _mixer_bwd(zm, o, dmerged, av_g, av_b, w_s, w_st, b_col)
    dq, dk, dv = _attn_bwd(q, k, v, o, do, lse, batch, seq)
    dzs, cqn, dqp, ckvn, dkv, d_qn, d_kvn = _mla_prep_bwd(zs, pos, invf, q_norm, kv_norm, wuq_p, wukv, dq, dk, dv)
    d_wuq_p = _mm_tn(cqn, dqp, "dw_uq")
    d_wukv = _mm_tn(ckvn, dkv, "dw_ukv")
    d_wmain = _mm_tn(h, dzm, "dw_in_main")
    d_wzs = _mm_tn(h, dzs, "dw_in_small")
    dh = _mm_nt([(dzm, w_main, 0), (dzs, w_zs, 0)], F32, "d_h")
    dx, d_mix_norm = _rms_bwd(x, mix_norm, dh, dx1, "mix_norm_bwd")

    grads = dict(
        mix_norm=d_mix_norm, w_main=d_wmain, w_zs=d_wzs, a_v_norm_g=d_avg, a_v_norm_b=d_avb,
        a_spatial_w=d_ws, a_spatial_b=d_bs.reshape(A_GROUPS, CHUNK), q_a_norm=d_qn, wuq_p=d_wuq_p,
        kv_a_norm=d_kvn, w_ukv=d_wukv, w_out=d_wout, ffn_norm=d_ffn_norm, w_up=d_wup,
        conv_w=jnp.concatenate([dcwg, dcwv], axis=1), conv_b=jnp.concatenate([dcbg, dcbv], axis=1),
        w_down=d_wdown, final_norm=d_final)
    return loss_acc[0, 0], dx.reshape(batch, seq, D_MODEL), grads


MESH_ID = pl.DeviceIdType.MESH
ANY = pl.BlockSpec(memory_space=pl.ANY)


def _mesh_pos():
    return lax.axis_index("x"), lax.axis_index("y"), lax.axis_index("c")


def _peer(pos, d):
    x, y, c = pos
    px = 1 - x if d & 4 else x
    py = 1 - y if d & 2 else y
    pc = 1 - c if d & 1 else c
    return (px, py, pc), 4 * px + 2 * py + pc


def _all_gather(arrays, name):
    n = len(arrays)

    def body(*refs):
        ins, outs = refs[:n], refs[n:2 * n]
        send_sems, recv_sems, loc_sems = refs[2 * n:]
        pos = _mesh_pos()
        me = 4 * pos[0] + 2 * pos[1] + pos[2]
        local = []
        for a in range(n):
            cp = pltpu.make_async_copy(ins[a], outs[a].at[me], loc_sems.at[a])
            cp.start()
            local.append(cp)
            for d in range(1, N_DEV):
                peer, _ = _peer(pos, d)
                pltpu.make_async_remote_copy(
                    src_ref=ins[a], dst_ref=outs[a].at[me], send_sem=send_sems.at[a, d - 1],
                    recv_sem=recv_sems.at[a, d - 1], device_id=peer, device_id_type=MESH_ID).start()
        for a in range(n):
            for d in range(1, N_DEV):
                peer, pid = _peer(pos, d)
                pltpu.make_async_remote_copy(
                    src_ref=ins[a], dst_ref=outs[a].at[pid], send_sem=send_sems.at[a, d - 1],
                    recv_sem=recv_sems.at[a, d - 1], device_id=peer, device_id_type=MESH_ID).wait()
            local[a].wait()

    return pl.pallas_call(
        body, in_specs=[ANY] * n, out_specs=[ANY] * n,
        out_shape=[jax.ShapeDtypeStruct((N_DEV,) + a.shape, a.dtype) for a in arrays],
        scratch_shapes=[pltpu.SemaphoreType.DMA((n, N_DEV - 1)), pltpu.SemaphoreType.DMA((n, N_DEV - 1)),
                        pltpu.SemaphoreType.DMA((n,))],
        name=name, compiler_params=pltpu.CompilerParams(has_side_effects=True))(*arrays)


def _all_to_all(arrays, name):
    n = len(arrays)

    def body(*refs):
        ins, outs = refs[:n], refs[n:2 * n]
        send_sems, recv_sems, loc_sems = refs[2 * n:]
        pos = _mesh_pos()
        me = 4 * pos[0] + 2 * pos[1] + pos[2]
        local = []
        for a in range(n):
            cp = pltpu.make_async_copy(ins[a].at[me], outs[a].at[0], loc_sems.at[a])
            cp.start()
            local.append(cp)
            for d in range(1, N_DEV):
                peer, pid = _peer(pos, d)
                pltpu.make_async_remote_copy(
                    src_ref=ins[a].at[pid], dst_ref=outs[a].at[d], send_sem=send_sems.at[a, d - 1],
                    recv_sem=recv_sems.at[a, d - 1], device_id=peer, device_id_type=MESH_ID).start()
        for a in range(n):
            for d in range(1, N_DEV):
                peer, pid = _peer(pos, d)
                pltpu.make_async_remote_copy(
                    src_ref=ins[a].at[pid], dst_ref=outs[a].at[d], send_sem=send_sems.at[a, d - 1],
                    recv_sem=recv_sems.at[a, d - 1], device_id=peer, device_id_type=MESH_ID).wait()
            local[a].wait()

    return pl.pallas_call(
        body, in_specs=[ANY] * n, out_specs=[ANY] * n,
        out_shape=[jax.ShapeDtypeStruct(a.shape, a.dtype) for a in arrays],
        scratch_shapes=[pltpu.SemaphoreType.DMA((n, N_DEV - 1)), pltpu.SemaphoreType.DMA((n, N_DEV - 1)),
                        pltpu.SemaphoreType.DMA((n,))],
        name=name, compiler_params=pltpu.CompilerParams(has_side_effects=True))(*arrays)


def _adamw(parts, w, m, v, name):
    R, C = w.shape
    tr = next((t for t in range(min(R, 256) // 8 * 8, 7, -8) if R % t == 0), R)
    c1 = 1.0 - ADAM_B1 ** ADAM_STEP
    c2 = 1.0 - ADAM_B2 ** ADAM_STEP

    def body(p_ref, w_ref, m_ref, v_ref, g_ref, d_ref, nm_ref, nv_ref):
        g = p_ref[0]
        for k in range(1, N_DEV):
            g = g + p_ref[k]
        nm = ADAM_B1 * m_ref[...] + (1.0 - ADAM_B1) * g
        nv = ADAM_B2 * v_ref[...] + (1.0 - ADAM_B2) * (g * g)
        g_ref[...] = g
        nm_ref[...] = nm
        nv_ref[...] = nv
        d_ref[...] = -ADAM_LR * ((nm / c1) / (jnp.sqrt(nv / c2) + ADAM_EPS) + ADAM_WD * w_ref[...])

    blk = pl.BlockSpec((tr, C), lambda i: (i, 0))
    shp = jax.ShapeDtypeStruct((R, C), F32)
    return pl.pallas_call(
        body, grid=(R // tr,),
        in_specs=[pl.BlockSpec((N_DEV, tr, C), lambda i: (0, i, 0)), blk, blk, blk],
        out_specs=[blk, blk, blk, blk], out_shape=[shp, shp, shp, shp],
        name=name, compiler_params=_params("parallel"))(parts, w, m, v)


SPLIT_V = 2 * D_MODEL
SPLIT_KR = SPLIT_V + Q_LORA + KV_LORA + QK_ROPE
IN_DIM = SPLIT_KR + 2 * D_MODEL

SMALL = ("mix_norm", "a_v_norm_g", "a_v_norm_b", "a_spatial_w", "a_spatial_b", "q_a_norm", "kv_a_norm",
         "ffn_norm", "conv_b", "final_norm")


def _pack_rows(a):
    flat = a.reshape(-1)
    rows = -(-flat.shape[0] // LANES)
    rows8 = -(-rows // 8) * 8
    return jnp.pad(flat, (0, rows8 * LANES - flat.shape[0])).reshape(rows8, LANES)


def _pack_small(tree, extra=None):
    parts = [_pack_rows(tree[n]) for n in SMALL]
    if extra is not None:
        parts.append(_pack_rows(extra))
    return jnp.concatenate(parts, axis=0)


def _unpack_small(buf, shapes):
    out, r = {}, 0
    for n in SMALL:
        size = math.prod(shapes[n])
        rows8 = -(-(-(-size // LANES)) // 8) * 8
        out[n] = buf[r:r + rows8].reshape(-1)[:size].reshape(shapes[n])
        r += rows8
    return out, r


def _cols_from_shards(g):
    return jnp.transpose(g, (1, 0, 2)).reshape(g.shape[1], N_DEV * g.shape[2])


def _shards_from_cols(a):
    R, W = a.shape
    return jnp.transpose(a.reshape(R, N_DEV, W // N_DEV), (1, 0, 2))


def kernel(x, positions, mix_norm, w_in, a_v_norm_g, a_v_norm_b, a_spatial_w, a_spatial_b, q_a_norm, w_uq, kv_a_norm, w_ukv, w_out, ffn_norm, w_up, conv_w, conv_b, w_down, final_norm, loss_target, m_mix_norm, m_w_in, m_a_v_norm_g, m_a_v_norm_b, m_a_spatial_w, m_a_spatial_b, m_q_a_norm, m_w_uq, m_kv_a_norm, m_w_ukv, m_w_out, m_ffn_norm, m_w_up, m_conv_w, m_conv_b, m_w_down, m_final_norm, v_mix_norm, v_w_in, v_a_v_norm_g, v_a_v_norm_b, v_a_spatial_w, v_a_spatial_b, v_q_a_norm, v_w_uq, v_kv_a_norm, v_w_ukv, v_w_out, v_ffn_norm, v_w_up, v_conv_w, v_conv_b, v_w_down, v_final_norm):
    names = ("mix_norm", "w_in", "a_v_norm_g", "a_v_norm_b", "a_spatial_w", "a_spatial_b", "q_a_norm", "w_uq",
             "kv_a_norm", "w_ukv", "w_out", "ffn_norm", "w_up", "conv_w", "conv_b", "w_down", "final_norm")
    w = dict(zip(names, (mix_norm, w_in, a_v_norm_g, a_v_norm_b, a_spatial_w, a_spatial_b, q_a_norm, w_uq,
                         kv_a_norm, w_ukv, w_out, ffn_norm, w_up, conv_w, conv_b, w_down, final_norm)))
    m = dict(zip(names, (m_mix_norm, m_w_in, m_a_v_norm_g, m_a_v_norm_b, m_a_spatial_w, m_a_spatial_b,
                         m_q_a_norm, m_w_uq, m_kv_a_norm, m_w_ukv, m_w_out, m_ffn_norm, m_w_up, m_conv_w,
                         m_conv_b, m_w_down, m_final_norm)))
    v = dict(zip(names, (v_mix_norm, v_w_in, v_a_v_norm_g, v_a_v_norm_b, v_a_spatial_w, v_a_spatial_b,
                         v_q_a_norm, v_w_uq, v_kv_a_norm, v_w_ukv, v_w_out, v_ffn_norm, v_w_up, v_conv_w,
                         v_conv_b, v_w_down, v_final_norm)))
    shapes = {n: w[n].shape for n in names}
    sharded = ("w_in", "w_uq", "w_ukv", "w_out", "w_up", "conv_w", "w_down")

    local = [w[n][0].astype(F32 if n == "conv_w" else BF16) for n in sharded]
    g_in, g_uq, g_ukv, g_out, g_up, g_cw, g_down = _all_gather(local, "gather_weights")
    w_full = _cols_from_shards(g_in)
    w_main = jnp.concatenate([w_full[:, :SPLIT_V], w_full[:, SPLIT_KR:]], axis=1)
    w_zs = jnp.pad(w_full[:, SPLIT_V:SPLIT_KR], ((0, 0), (0, ZS_W - (SPLIT_KR - SPLIT_V))))
    wuq_p = _cols_from_shards(jnp.pad(g_uq, ((0, 0), (0, 0), (0, HEAD_PAD - QK_HEAD))))
    wukv = _cols_from_shards(g_ukv)
    w_out_f = g_out.reshape(D_MODEL, D_MODEL)
    w_up_f = _cols_from_shards(g_up)
    conv_w_f = _cols_from_shards(g_cw)
    w_down_f = g_down.reshape(D_FF, D_MODEL)

    loss_part, grad_x, g = _local_step(
        x, positions, loss_target, w["mix_norm"], w_main, w_zs, w["a_v_norm_g"], w["a_v_norm_b"],
        w["a_spatial_w"][0], w["a_spatial_b"][0], w["q_a_norm"], wuq_p, w["kv_a_norm"], wukv, w_out_f,
        w["ffn_norm"], w_up_f, conv_w_f, w["conv_b"], w_down_f, w["final_norm"].reshape(1, D_MODEL))

    d_in = jnp.concatenate([g["w_main"][:, :SPLIT_V], g["w_zs"][:, :SPLIT_KR - SPLIT_V],
                            g["w_main"][:, SPLIT_V:]], axis=1)
    send = [
        _shards_from_cols(d_in),
        _shards_from_cols(g["wuq_p"])[:, :, :QK_HEAD],
        _shards_from_cols(g["w_ukv"]),
        g["w_out"].reshape(N_DEV, D_MODEL // N_DEV, D_MODEL),
        _shards_from_cols(g["w_up"]),
        _shards_from_cols(g["conv_w"]),
        g["w_down"].reshape(N_DEV, D_FF // N_DEV, D_MODEL),
    ]
    recv = _all_to_all(send, "exchange_grads")

    small_g = {n: g[n].reshape(shapes[n]) for n in SMALL}
    small_parts = _all_gather([_pack_small(small_g, extra=loss_part.reshape(1))], "gather_small_grads")[0]

    out_g, out_d, out_m, out_v = {}, {}, {}, {}
    for n, parts in zip(sharded, recv):
        shp = shapes[n]
        r2 = (shp[-2], shp[-1])
        res = _adamw(parts.reshape((N_DEV,) + r2), w[n].reshape(r2), m[n].reshape(r2), v[n].reshape(r2),
                     "adamw_" + n)
        out_g[n], out_d[n], out_m[n], out_v[n] = (t.reshape(shp) for t in res)

    zero = jnp.zeros((1,), F32)
    res = _adamw(small_parts, _pack_small(w, extra=zero), _pack_small(m, extra=zero),
                 _pack_small(v, extra=zero + 1.0), "adamw_small")
    unpacked = [_unpack_small(t, shapes) for t in res]
    for n in SMALL:
        out_g[n], out_d[n], out_m[n], out_v[n] = (u[0][n] for u in unpacked)
    loss = res[0][unpacked[0][1], 0]

    return (loss, grad_x, *[out_g[n] for n in names], *[out_d[n] for n in names],
            *[out_m[n] for n in names], *[out_v[n] for n in names])
```

```python
import functools
import math

import jax
import jax.numpy as jnp
from jax import lax
from jax.experimental import pallas as pl
from jax.experimental.pallas import tpu as pltpu

F32 = jnp.float32
BF16 = jnp.bfloat16

N_DEV = 8
D_MODEL = 1024
EPS = 1e-6
A_GROUPS = 8
CHUNK = 128
MLA_HEADS = 8
QK_NOPE = 128
QK_ROPE = 64
QK_HEAD = QK_NOPE + QK_ROPE
HEAD_PAD = 256
V_HEAD = 128
Q_LORA = 256
KV_LORA = 128
ROPE_THETA = 10000.0
D_FF = 2816
ZS_W = 512
ATTN_SCALE = QK_HEAD ** -0.5
NEG_BIG = -1e30

ADAM_LR = 0.001
ADAM_B1 = 0.9
ADAM_B2 = 0.999
ADAM_EPS = 1e-08
ADAM_WD = 0.01
ADAM_STEP = 10

VMEM_LIMIT = 56 * 1024 * 1024
LANES = 128

GELU_K = math.sqrt(2.0 / math.pi)
GELU_C = 0.044715

ANY = pl.BlockSpec(memory_space=pl.ANY)
HBM = pl.BlockSpec(memory_space=pltpu.HBM)
SEM = pl.BlockSpec(memory_space=pltpu.SEMAPHORE)


def _tile(n, pref):
    for t in (pref, 512, 256, 128, 64, 32, 16, 8):
        if t <= pref and n % t == 0:
            return t
    return n


def _wide_tile(n, cap=1408):
    return next((t for t in range(min(n, cap) // LANES * LANES, 0, -LANES) if n % t == 0), n)


def _params(*sem):
    return pltpu.CompilerParams(dimension_semantics=sem, vmem_limit_bytes=VMEM_LIMIT)


def _dot(a, b):
    return jnp.dot(a, b, preferred_element_type=F32)


def _dot_nt(a, b):
    return lax.dot_general(a, b, (((1,), (1,)), ((), ())), preferred_element_type=F32)


def _dot_tn(a, b):
    return lax.dot_general(a, b, (((0,), (0,)), ((), ())), preferred_element_type=F32)


def _sigmoid(x):
    return 1.0 / (1.0 + jnp.exp(-x))


def _gelu(x):
    t = jnp.tanh(GELU_K * (x + GELU_C * x * x * x))
    return 0.5 * x * (1.0 + t)


def _gelu_grad(x):
    t = jnp.tanh(GELU_K * (x + GELU_C * x * x * x))
    return 0.5 * (1.0 + t) + 0.5 * x * (1.0 - t * t) * GELU_K * (1.0 + 3.0 * GELU_C * x * x)


def _norm_mm(x, g, w, name):
    T, Dm = x.shape
    N = w.shape[1]
    tm, tn = _tile(T, 512), _tile(N, 512)

    def body(x_ref, g_ref, w_ref, h_ref, z_ref):
        @pl.when(pl.program_id(1) == 0)
        def _():
            xf = x_ref[...]
            r = lax.rsqrt(jnp.mean(xf * xf, axis=-1, keepdims=True) + EPS)
            h_ref[...] = (xf * r * g_ref[...]).astype(BF16)

        z_ref[...] = _dot(h_ref[...], w_ref[...])

    return pl.pallas_call(
        body, grid=(T // tm, N // tn),
        in_specs=[pl.BlockSpec((tm, Dm), lambda i, j: (i, 0)),
                  pl.BlockSpec((1, Dm), lambda i, j: (0, 0)),
                  pl.BlockSpec((Dm, tn), lambda i, j: (0, j))],
        out_specs=[pl.BlockSpec((tm, Dm), lambda i, j: (i, 0)),
                   pl.BlockSpec((tm, tn), lambda i, j: (i, j))],
        out_shape=[jax.ShapeDtypeStruct((T, Dm), BF16), jax.ShapeDtypeStruct((T, N), F32)],
        name=name, compiler_params=_params("parallel", "arbitrary"))(x, g, w)


def _mm_nn(a, w, out_dtype, name):
    T, K = a.shape
    N = w.shape[1]
    tm, tn = _tile(T, 512), _tile(N, 512)

    def body(a_ref, w_ref, o_ref):
        o_ref[...] = _dot(a_ref[...].astype(BF16), w_ref[...]).astype(o_ref.dtype)

    return pl.pallas_call(
        body, grid=(T // tm, N // tn),
        in_specs=[pl.BlockSpec((tm, K), lambda i, j: (i, 0)),
                  pl.BlockSpec((K, tn), lambda i, j: (0, j))],
        out_specs=pl.BlockSpec((tm, tn), lambda i, j: (i, j)),
        out_shape=jax.ShapeDtypeStruct((T, N), out_dtype),
        name=name, compiler_params=_params("parallel", "parallel"))(a, w)


def _mm_nt(pairs, out_dtype, name, dep=None):
    T = pairs[0][0].shape[0]
    K = pairs[0][1].shape[0]
    tm, tk = _tile(T, 512), _tile(K, 512)
    n_pairs = len(pairs)

    def body(*refs):
        o_ref = refs[-1]
        acc = None
        for p in range(n_pairs):
            t = _dot_nt(refs[2 * p][...].astype(BF16), refs[2 * p + 1][...])
            acc = t if acc is None else acc + t
        o_ref[...] = acc.astype(o_ref.dtype)

    in_specs, args = [], []
    for a, w, c in pairs:
        in_specs += [pl.BlockSpec((tm, a.shape[1]), lambda i, j: (i, 0)),
                     pl.BlockSpec((tk, a.shape[1]), functools.partial(lambda i, j, c: (j, c), c=c))]
        args += [a, w]
    if dep is not None:
        in_specs.append(ANY)
        args.append(dep)
    return pl.pallas_call(
        body, grid=(T // tm, K // tk), in_specs=in_specs,
        out_specs=pl.BlockSpec((tm, tk), lambda i, j: (i, j)),
        out_shape=jax.ShapeDtypeStruct((T, K), out_dtype),
        name=name, compiler_params=_params("parallel", "parallel"))(*args)


def _mm_tn(a, b, name):
    T, M = a.shape
    N = b.shape[1]
    tm, tn, tt = _wide_tile(M), _wide_tile(N), _tile(T, 512)

    def body(a_ref, b_ref, o_ref):
        @pl.when(pl.program_id(2) == 0)
        def _():
            o_ref[...] = jnp.zeros_like(o_ref)

        o_ref[...] += _dot_tn(a_ref[...].astype(BF16), b_ref[...].astype(BF16))

    return pl.pallas_call(
        body, grid=(M // tm, N // tn, T // tt),
        in_specs=[pl.BlockSpec((tt, tm), lambda i, j, t: (t, i)),
                  pl.BlockSpec((tt, tn), lambda i, j, t: (t, j))],
        out_specs=pl.BlockSpec((tm, tn), lambda i, j, t: (i, j)),
        out_shape=jax.ShapeDtypeStruct((M, N), F32),
        name=name, compiler_params=_params("parallel", "parallel", "arbitrary"))(a, b)


def _rms_bwd(x, g, dy, dres, name):
    T, Dm = x.shape
    tm = _tile(T, 512)

    def body(x_ref, g_ref, dy_ref, dres_ref, dx_ref, dg_ref):
        @pl.when(pl.program_id(0) == 0)
        def _():
            dg_ref[...] = jnp.zeros_like(dg_ref)

        xf = x_ref[...]
        r = lax.rsqrt(jnp.mean(xf * xf, axis=-1, keepdims=True) + EPS)
        xh = xf * r
        dyv = dy_ref[...]
        dg_ref[...] += jnp.sum(dyv * xh, axis=0, keepdims=True)
        dxh = dyv * g_ref[...]
        dx_ref[...] = dres_ref[...] + r * (dxh - xh * jnp.mean(dxh * xh, axis=-1, keepdims=True))

    row = pl.BlockSpec((tm, Dm), lambda i: (i, 0))
    vec = pl.BlockSpec((1, Dm), lambda i: (0, 0))
    return pl.pallas_call(
        body, grid=(T // tm,), in_specs=[row, vec, row, row], out_specs=[row, vec],
        out_shape=[jax.ShapeDtypeStruct((T, Dm), F32), jax.ShapeDtypeStruct((1, Dm), F32)],
        name=name, compiler_params=_params("arbitrary"))(x, g, dy, dres)


def _layer_norm_fwd(gv, g, b):
    mu = jnp.mean(gv, axis=-1, keepdims=True)
    xc = gv - mu
    rs = lax.rsqrt(jnp.mean(xc * xc, axis=-1, keepdims=True) + EPS)
    xh = xc * rs
    return xh, rs, xh * g + b


def _tri_mask(transposed=False):
    r = lax.broadcasted_iota(jnp.int32, (CHUNK, CHUNK), 0)
    c = lax.broadcasted_iota(jnp.int32, (CHUNK, CHUNK), 1)
    return r <= c if transposed else c <= r


def _mixer_a_fwd(zm, av_g, av_b, w_s, b_col):
    T = zm.shape[0]
    tm = _tile(T, 256)
    n_chunk = tm // CHUNK

    def body(u_ref, v_ref, ga_ref, g_ref, b_ref, w_ref, bc_ref, y_ref, vn_s, mx_s):
        gu = _gelu(u_ref[...])
        _, _, vn = _layer_norm_fwd(_gelu(v_ref[...]), g_ref[...], b_ref[...])
        vn_s[...] = vn.astype(BF16)
        tri = _tri_mask()
        for gi in range(A_GROUPS):
            wm = jnp.where(tri, w_ref[gi], 0.0).astype(BF16)
            cols = slice(gi * CHUNK, (gi + 1) * CHUNK)
            for n in range(n_chunk):
                rows = slice(n * CHUNK, (n + 1) * CHUNK)
                mx_s[rows, cols] = _dot(wm, vn_s[rows, cols]) + bc_ref[gi]
        y_ref[...] = _sigmoid(ga_ref[...]) * gu * mx_s[...]

    col = lambda c: pl.BlockSpec((tm, D_MODEL), lambda i: (i, c))
    vec = pl.BlockSpec((1, D_MODEL), lambda i: (0, 0))
    return pl.pallas_call(
        body, grid=(T // tm,),
        in_specs=[col(0), col(1), col(2), vec, vec,
                  pl.BlockSpec((A_GROUPS, CHUNK, CHUNK), lambda i: (0, 0, 0)),
                  pl.BlockSpec((A_GROUPS, CHUNK, 1), lambda i: (0, 0, 0))],
        out_specs=pl.BlockSpec((tm, D_MODEL), lambda i: (i, 0)),
        out_shape=jax.ShapeDtypeStruct((T, D_MODEL), F32),
        scratch_shapes=[pltpu.VMEM((tm, D_MODEL), BF16), pltpu.VMEM((tm, D_MODEL), F32)],
        name="mixer_a_fwd", compiler_params=_params("parallel"))(zm, zm, zm, av_g, av_b, w_s, b_col)


def _mixer_bwd(zm, o, dm, av_g, av_b, w_s, w_st, b_col):
    T = zm.shape[0]
    tm = _tile(T, 256)
    n_chunk = tm // CHUNK

    def body(u_ref, v_ref, ga_ref, gb_ref, o_ref, dm_ref, g_ref, b_ref, w_ref, wt_ref, bc_ref,
             dz_ref, do_ref, dg_ref, db_ref, dw_ref, dbs_ref, vn_s, mx_s, dmx_s, dvn_s):
        @pl.when(pl.program_id(0) == 0)
        def _():
            dg_ref[...] = jnp.zeros_like(dg_ref)
            db_ref[...] = jnp.zeros_like(db_ref)
            dw_ref[...] = jnp.zeros_like(dw_ref)
            dbs_ref[...] = jnp.zeros_like(dbs_ref)

        dm_v = dm_ref[...]
        gb = gb_ref[...]
        sb = _sigmoid(gb)
        o_v = o_ref[...]
        do_ref[...] = (dm_v * sb).astype(BF16)
        dz_ref[:, 3 * D_MODEL:4 * D_MODEL] = (dm_v * o_v * sb * (1.0 - sb)).astype(BF16)
        u = u_ref[...]
        v = v_ref[...]
        gu = _gelu(u)
        xh, rs, vn = _layer_norm_fwd(_gelu(v), g_ref[...], b_ref[...])
        vn_s[...] = vn.astype(BF16)
        tri = _tri_mask()
        for gi in range(A_GROUPS):
            wm = jnp.where(tri, w_ref[gi], 0.0).astype(BF16)
            cols = slice(gi * CHUNK, (gi + 1) * CHUNK)
            for n in range(n_chunk):
                rows = slice(n * CHUNK, (n + 1) * CHUNK)
                mx_s[rows, cols] = _dot(wm, vn_s[rows, cols]) + bc_ref[gi]
        mixed = mx_s[...]
        sa = _sigmoid(ga_ref[...])
        dya = dm_v * sa
        dz_ref[:, 2 * D_MODEL:3 * D_MODEL] = (dm_v * gu * mixed * sa * (1.0 - sa)).astype(BF16)
        dz_ref[:, 0:D_MODEL] = (dya * mixed * _gelu_grad(u)).astype(BF16)
        dmx = dya * gu
        dmx_s[...] = dmx.astype(BF16)
        tri_t = _tri_mask(transposed=True)
        for gi in range(A_GROUPS):
            wmt = jnp.where(tri_t, wt_ref[gi], 0.0).astype(BF16)
            cols = slice(gi * CHUNK, (gi + 1) * CHUNK)
            dw_acc = jnp.zeros((CHUNK, CHUNK), F32)
            dmx_sum = jnp.zeros((CHUNK, CHUNK), F32)
            for n in range(n_chunk):
                rows = slice(n * CHUNK, (n + 1) * CHUNK)
                blk = dmx_s[rows, cols]
                dvn_s[rows, cols] = _dot(wmt, blk)
                dw_acc = dw_acc + _dot_nt(blk, vn_s[rows, cols])
                dmx_sum = dmx_sum + dmx[rows, cols]
            dw_ref[gi] += jnp.where(tri, dw_acc, 0.0)
            dbs_ref[gi] += jnp.sum(dmx_sum, axis=-1, keepdims=True)
        dvn = dvn_s[...]
        dg_ref[...] += jnp.sum(dvn * xh, axis=0, keepdims=True)
        db_ref[...] += jnp.sum(dvn, axis=0, keepdims=True)
        dxh = dvn * g_ref[...]
        dgv = rs * (dxh - jnp.mean(dxh, axis=-1, keepdims=True)
                    - xh * jnp.mean(dxh * xh, axis=-1, keepdims=True))
        dz_ref[:, D_MODEL:2 * D_MODEL] = (dgv * _gelu_grad(v)).astype(BF16)

    col = lambda c: pl.BlockSpec((tm, D_MODEL), lambda i: (i, c))
    row = pl.BlockSpec((tm, D_MODEL), lambda i: (i, 0))
    vec = pl.BlockSpec((1, D_MODEL), lambda i: (0, 0))
    wsp = pl.BlockSpec((A_GROUPS, CHUNK, CHUNK), lambda i: (0, 0, 0))
    bsp = pl.BlockSpec((A_GROUPS, CHUNK, 1), lambda i: (0, 0, 0))
    return pl.pallas_call(
        body, grid=(T // tm,),
        in_specs=[col(0), col(1), col(2), col(3), row, row, vec, vec, wsp, wsp, bsp],
        out_specs=[pl.BlockSpec((tm, 4 * D_MODEL), lambda i: (i, 0)), row, vec, vec, wsp, bsp],
        out_shape=[jax.ShapeDtypeStruct((T, 4 * D_MODEL), BF16), jax.ShapeDtypeStruct((T, D_MODEL), BF16),
                   jax.ShapeDtypeStruct((1, D_MODEL), F32), jax.ShapeDtypeStruct((1, D_MODEL), F32),
                   jax.ShapeDtypeStruct((A_GROUPS, CHUNK, CHUNK), F32),
                   jax.ShapeDtypeStruct((A_GROUPS, CHUNK, 1), F32)],
        scratch_shapes=[pltpu.VMEM((tm, D_MODEL), BF16), pltpu.VMEM((tm, D_MODEL), F32),
                        pltpu.VMEM((tm, D_MODEL), BF16), pltpu.VMEM((tm, D_MODEL), F32)],
        name="mixer_bwd", compiler_params=_params("arbitrary"))(
            zm, zm, zm, zm, o, dm, av_g, av_b, w_s, w_st, b_col)


def _rope_tables(pos_ref, invf_ref):
    ang = pos_ref[...].astype(F32) * invf_ref[...]
    lane = lax.broadcasted_iota(jnp.int32, ang.shape, 1)
    cos, sin = jnp.cos(ang), jnp.sin(ang)
    c = jnp.where(lane < QK_ROPE, cos, 0.0)
    sa = jnp.where(lane < QK_ROPE // 2, -sin, 0.0)
    sb = jnp.where((lane >= QK_ROPE // 2) & (lane < QK_ROPE), sin, 0.0)
    return c, sa, sb


def _rope(blk, tabs):
    c, sa, sb = tabs
    return blk * c + pltpu.roll(blk, LANES - QK_ROPE // 2, 1) * sa + pltpu.roll(blk, QK_ROPE // 2, 1) * sb


def _rope_t(dout, tabs):
    c, sa, sb = tabs
    return dout * c + pltpu.roll(dout * sa, QK_ROPE // 2, 1) + pltpu.roll(dout * sb, LANES - QK_ROPE // 2, 1)


def _rms_small(x, g):
    r = lax.rsqrt(jnp.mean(x * x, axis=-1, keepdims=True) + EPS)
    xh = x * r
    return xh, r, xh * g


def _mla_prep_fwd(zs, pos, invf, qg, kvg, wuq_p, wukv):
    T = zs.shape[0]
    tm = _tile(T, 512)
    HW = MLA_HEADS * HEAD_PAD

    def body(zs_ref, pos_ref, invf_ref, qg_ref, kvg_ref, wq_ref, wkv_ref, q_ref, k_ref, v_ref):
        tabs = _rope_tables(pos_ref, invf_ref)
        _, _, cqn = _rms_small(zs_ref[:, 0:Q_LORA], qg_ref[...])
        _, _, ckvn = _rms_small(zs_ref[:, Q_LORA:Q_LORA + KV_LORA], kvg_ref[...])
        q = _dot(cqn.astype(BF16), wq_ref[...])
        kv = _dot(ckvn.astype(BF16), wkv_ref[...])
        kr = _rope(zs_ref[:, Q_LORA + KV_LORA:ZS_W], tabs).astype(BF16)
        for h in range(MLA_HEADS):
            b0 = h * HEAD_PAD
            q_ref[:, b0:b0 + QK_NOPE] = q[:, b0:b0 + QK_NOPE].astype(BF16)
            q_ref[:, b0 + QK_NOPE:b0 + HEAD_PAD] = _rope(q[:, b0 + QK_NOPE:b0 + HEAD_PAD], tabs).astype(BF16)
            k_ref[:, b0:b0 + QK_NOPE] = kv[:, b0:b0 + QK_NOPE].astype(BF16)
            k_ref[:, b0 + QK_NOPE:b0 + HEAD_PAD] = kr
            v_ref[:, h * V_HEAD:(h + 1) * V_HEAD] = kv[:, b0 + QK_NOPE:b0 + HEAD_PAD].astype(BF16)

    full = lambda a: pl.BlockSpec(a.shape, lambda i: (0,) * a.ndim)
    return pl.pallas_call(
        body, grid=(T // tm,),
        in_specs=[pl.BlockSpec((tm, ZS_W), lambda i: (i, 0)), pl.BlockSpec((tm, 1), lambda i: (i, 0)),
                  full(invf), full(qg), full(kvg), full(wuq_p), full(wukv)],
        out_specs=[pl.BlockSpec((tm, HW), lambda i: (i, 0)), pl.BlockSpec((tm, HW), lambda i: (i, 0)),
                   pl.BlockSpec((tm, D_MODEL), lambda i: (i, 0))],
        out_shape=[jax.ShapeDtypeStruct((T, HW), BF16), jax.ShapeDtypeStruct((T, HW), BF16),
                   jax.ShapeDtypeStruct((T, D_MODEL), BF16)],
        name="mla_prep_fwd", compiler_params=_params("parallel"))(zs, pos, invf, qg, kvg, wuq_p, wukv)


def _mla_prep_bwd(zs, pos, invf, qg, kvg, wuq_p, wukv, dq, dk, dv):
    T = zs.shape[0]
    tm = _tile(T, 256)
    HW = MLA_HEADS * HEAD_PAD

    def body(zs_ref, pos_ref, invf_ref, qg_ref, kvg_ref, wq_ref, wkv_ref, dq_ref, dk_ref, dv_ref,
             dzs_ref, cqn_ref, dqp_ref, ckvn_ref, dkv_ref, dqg_ref, dkvg_ref):
        @pl.when(pl.program_id(0) == 0)
        def _():
            dqg_ref[...] = jnp.zeros_like(dqg_ref)
            dkvg_ref[...] = jnp.zeros_like(dkvg_ref)

        tabs = _rope_tables(pos_ref, invf_ref)
        cqh, rq, cqn = _rms_small(zs_ref[:, 0:Q_LORA], qg_ref[...])
        ckvh, rkv, ckvn = _rms_small(zs_ref[:, Q_LORA:Q_LORA + KV_LORA], kvg_ref[...])
        cqn_ref[...] = cqn.astype(BF16)
        ckvn_ref[...] = ckvn.astype(BF16)
        dkr = jnp.zeros((tm, LANES), F32)
        for h in range(MLA_HEADS):
            b0 = h * HEAD_PAD
            dqp_ref[:, b0:b0 + QK_NOPE] = dq_ref[:, b0:b0 + QK_NOPE].astype(BF16)
            dqp_ref[:, b0 + QK_NOPE:b0 + HEAD_PAD] = _rope_t(dq_ref[:, b0 + QK_NOPE:b0 + HEAD_PAD], tabs).astype(BF16)
            dkv_ref[:, b0:b0 + QK_NOPE] = dk_ref[:, b0:b0 + QK_NOPE].astype(BF16)
            dkv_ref[:, b0 + QK_NOPE:b0 + HEAD_PAD] = dv_ref[:, h * V_HEAD:(h + 1) * V_HEAD].astype(BF16)
            dkr = dkr + dk_ref[:, b0 + QK_NOPE:b0 + HEAD_PAD]
        dcqn = _dot_nt(dqp_ref[...], wq_ref[...])
        dckvn = _dot_nt(dkv_ref[...], wkv_ref[...])
        dqg_ref[...] += jnp.sum(dcqn * cqh, axis=0, keepdims=True)
        dkvg_ref[...] += jnp.sum(dckvn * ckvh, axis=0, keepdims=True)
        dxh = dcqn * qg_ref[...]
        dzs_ref[:, 0:Q_LORA] = (rq * (dxh - cqh * jnp.mean(dxh * cqh, axis=-1, keepdims=True))).astype(BF16)
        dxh = dckvn * kvg_ref[...]
        dzs_ref[:, Q_LORA:Q_LORA + KV_LORA] = (
            rkv * (dxh - ckvh * jnp.mean(dxh * ckvh, axis=-1, keepdims=True))).astype(BF16)
        dzs_ref[:, Q_LORA + KV_LORA:ZS_W] = _rope_t(dkr, tabs).astype(BF16)

    full = lambda a: pl.BlockSpec(a.shape, lambda i: (0,) * a.ndim)
    rowb = lambda w: pl.BlockSpec((tm, w), lambda i: (i, 0))
    return pl.pallas_call(
        body, grid=(T // tm,),
        in_specs=[rowb(ZS_W), rowb(1), full(invf), full(qg), full(kvg), full(wuq_p), full(wukv),
                  rowb(HW), rowb(HW), rowb(D_MODEL)],
        out_specs=[rowb(ZS_W), rowb(Q_LORA), rowb(HW), rowb(KV_LORA), rowb(HW), full(qg), full(kvg)],
        out_shape=[jax.ShapeDtypeStruct((T, ZS_W), BF16), jax.ShapeDtypeStruct((T, Q_LORA), BF16),
                   jax.ShapeDtypeStruct((T, HW), BF16), jax.ShapeDtypeStruct((T, KV_LORA), BF16),
                   jax.ShapeDtypeStruct((T, HW), BF16), jax.ShapeDtypeStruct(qg.shape, F32),
                   jax.ShapeDtypeStruct(kvg.shape, F32)],
        name="mla_prep_bwd", compiler_params=_params("arbitrary"))(
            zs, pos, invf, qg, kvg, wuq_p, wukv, dq, dk, dv)


def _causal(tq, kmax, q0):
    r = lax.broadcasted_iota(jnp.int32, (tq, kmax), 0) + q0
    c = lax.broadcasted_iota(jnp.int32, (tq, kmax), 1)
    return c <= r


def _attn_fwd(q, k, v, batch, seq):
    tq = _tile(seq, 512)
    nq = seq // tq

    def body(q_ref, k_ref, v_ref, o_ref, lse_ref):
        for qi in range(nq):
            rows = slice(qi * tq, (qi + 1) * tq)
            kmax = (qi + 1) * tq
            s = _dot_nt(q_ref[rows, :], k_ref[0:kmax, :]) * ATTN_SCALE
            s = jnp.where(_causal(tq, kmax, qi * tq), s, NEG_BIG)
            m = jnp.max(s, axis=-1, keepdims=True)
            p = jnp.exp(s - m)
            l = jnp.sum(p, axis=-1, keepdims=True)
            o_ref[rows, :] = _dot(p.astype(BF16), v_ref[0:kmax, :]) / l
            lse_ref[rows, :] = jnp.broadcast_to(m + jnp.log(l), (tq, V_HEAD))

    return pl.pallas_call(
        body, grid=(batch, MLA_HEADS),
        in_specs=[pl.BlockSpec((seq, HEAD_PAD), lambda b, h: (b, h)),
                  pl.BlockSpec((seq, HEAD_PAD), lambda b, h: (b, h)),
                  pl.BlockSpec((seq, V_HEAD), lambda b, h: (b, h))],
        out_specs=[pl.BlockSpec((seq, V_HEAD), lambda b, h: (b, h)),
                   pl.BlockSpec((seq, V_HEAD), lambda b, h: (b, h))],
        out_shape=[jax.ShapeDtypeStruct((batch * seq, D_MODEL), F32),
                   jax.ShapeDtypeStruct((batch * seq, D_MODEL), F32)],
        name="attn_fwd", compiler_params=_params("parallel", "parallel"))(q, k, v)


def _attn_bwd(q, k, v, o, do, lse, batch, seq):
    tq = _tile(seq, 512)
    nq = seq // tq

    def body(q_ref, k_ref, v_ref, o_ref, do_ref, lse_ref, dq_ref, dk_ref, dv_ref):
        dk_ref[...] = jnp.zeros_like(dk_ref)
        dv_ref[...] = jnp.zeros_like(dv_ref)
        for qi in range(nq):
            rows = slice(qi * tq, (qi + 1) * tq)
            kmax = (qi + 1) * tq
            qr = q_ref[rows, :]
            dor = do_ref[rows, :]
            kk = k_ref[0:kmax, :]
            s = _dot_nt(qr, kk) * ATTN_SCALE
            p = jnp.where(_causal(tq, kmax, qi * tq), jnp.exp(s - lse_ref[rows, 0:1]), 0.0)
            dp = _dot_nt(dor, v_ref[0:kmax, :])
            delta = jnp.sum(dor.astype(F32) * o_ref[rows, :], axis=-1, keepdims=True)
            ds = (p * (dp - delta) * ATTN_SCALE).astype(BF16)
            dq_ref[rows, :] = _dot(ds, kk)
            dk_ref[0:kmax, :] += _dot_tn(ds, qr)
            dv_ref[0:kmax, :] += _dot_tn(p.astype(BF16), dor)

    qspec = pl.BlockSpec((seq, HEAD_PAD), lambda b, h: (b, h))
    vspec = pl.BlockSpec((seq, V_HEAD), lambda b, h: (b, h))
    T = batch * seq
    return pl.pallas_call(
        body, grid=(batch, MLA_HEADS),
        in_specs=[qspec, qspec, vspec, vspec, vspec, vspec],
        out_specs=[qspec, qspec, vspec],
        out_shape=[jax.ShapeDtypeStruct((T, MLA_HEADS * HEAD_PAD), F32),
                   jax.ShapeDtypeStruct((T, MLA_HEADS * HEAD_PAD), F32),
                   jax.ShapeDtypeStruct((T, D_MODEL), F32)],
        name="attn_bwd", compiler_params=_params("parallel", "parallel"))(q, k, v, o, do, lse)


def _merge_out(x, yag, zm, o, w_out):
    T = x.shape[0]
    tm = _tile(T, 512)

    def body(x_ref, ya_ref, gb_ref, o_ref, w_ref, mg_ref, x1_ref):
        mg = (ya_ref[...] + _sigmoid(gb_ref[...]) * o_ref[...]).astype(BF16)
        mg_ref[...] = mg
        x1_ref[...] = x_ref[...] + _dot(mg, w_ref[...])

    row = pl.BlockSpec((tm, D_MODEL), lambda i: (i, 0))
    return pl.pallas_call(
        body, grid=(T // tm,),
        in_specs=[row, row, pl.BlockSpec((tm, D_MODEL), lambda i: (i, 3)), row,
                  pl.BlockSpec((D_MODEL, D_MODEL), lambda i: (0, 0))],
        out_specs=[row, row],
        out_shape=[jax.ShapeDtypeStruct((T, D_MODEL), BF16), jax.ShapeDtypeStruct((T, D_MODEL), F32)],
        name="merge_out", compiler_params=_params("parallel"))(x, yag, zm, o, w_out)


FF_TILE = 256
FF_BLOCKS = D_FF // FF_TILE


def _shift_down(x, k):
    row = lax.broadcasted_iota(jnp.int32, x.shape, 0)
    return jnp.where(row >= k, pltpu.roll(x, k, 0), 0.0)


def _shift_up(x, k):
    n = x.shape[0]
    row = lax.broadcasted_iota(jnp.int32, x.shape, 0)
    return jnp.where(row < n - k, pltpu.roll(x, n - k, 0), 0.0)


def _conv(x, w_ref, b_ref):
    return b_ref[...] + w_ref[2:3, :] * x + w_ref[1:2, :] * _shift_down(x, 1) + w_ref[0:1, :] * _shift_down(x, 2)


def _ffn_act(up, cw, cb, batch, seq):
    def body(ug_ref, uv_ref, wg_ref, wv_ref, bg_ref, bv_ref, a_ref):
        gate = _conv(ug_ref[...], wg_ref, bg_ref)
        val = _conv(uv_ref[...], wv_ref, bv_ref)
        a_ref[...] = (gate * _sigmoid(gate) * val).astype(BF16)

    blk = lambda off: pl.BlockSpec((seq, FF_TILE), lambda b, j: (b, j + off))
    wsp = lambda off: pl.BlockSpec((3, FF_TILE), lambda b, j: (0, j + off))
    bsp = lambda off: pl.BlockSpec((1, FF_TILE), lambda b, j: (0, j + off))
    return pl.pallas_call(
        body, grid=(batch, FF_BLOCKS),
        in_specs=[blk(0), blk(FF_BLOCKS), wsp(0), wsp(FF_BLOCKS), bsp(0), bsp(FF_BLOCKS)],
        out_specs=blk(0),
        out_shape=jax.ShapeDtypeStruct((batch * seq, D_FF), BF16),
        name="ffn_act", compiler_params=_params("parallel", "parallel"))(up, up, cw, cw, cb, cb)


def _ffn_act_bwd(up, cw, cb, da, batch, seq):
    def half(du, x, w_ref, dx_ref, dw_ref, db_ref):
        dx_ref[...] = (w_ref[2:3, :] * du + w_ref[1:2, :] * _shift_up(du, 1)
                       + w_ref[0:1, :] * _shift_up(du, 2)).astype(BF16)
        dw_ref[2:3, :] += jnp.sum(du * x, axis=0, keepdims=True)
        dw_ref[1:2, :] += jnp.sum(du * _shift_down(x, 1), axis=0, keepdims=True)
        dw_ref[0:1, :] += jnp.sum(du * _shift_down(x, 2), axis=0, keepdims=True)
        db_ref[...] += jnp.sum(du, axis=0, keepdims=True)

    def body(ug_ref, uv_ref, wg_ref, wv_ref, bg_ref, bv_ref, da_ref,
             dg_ref, dv_ref, dwg_ref, dwv_ref, dbg_ref, dbv_ref):
        @pl.when(pl.program_id(1) == 0)
        def _():
            for r in (dwg_ref, dwv_ref, dbg_ref, dbv_ref):
                r[...] = jnp.zeros_like(r)

        ug, uv = ug_ref[...], uv_ref[...]
        gate = _conv(ug, wg_ref, bg_ref)
        val = _conv(uv, wv_ref, bv_ref)
        sg = _sigmoid(gate)
        dav = da_ref[...]
        half(dav * val * sg * (1.0 + gate * (1.0 - sg)), ug, wg_ref, dg_ref, dwg_ref, dbg_ref)
        half(dav * gate * sg, uv, wv_ref, dv_ref, dwv_ref, dbv_ref)

    blk = lambda off: pl.BlockSpec((seq, FF_TILE), lambda j, b: (b, j + off))
    wsp = lambda off: pl.BlockSpec((3, FF_TILE), lambda j, b: (0, j + off))
    bsp = lambda off: pl.BlockSpec((1, FF_TILE), lambda j, b: (0, j + off))
    T = batch * seq
    return pl.pallas_call(
        body, grid=(FF_BLOCKS, batch),
        in_specs=[blk(0), blk(FF_BLOCKS), wsp(0), wsp(FF_BLOCKS), bsp(0), bsp(FF_BLOCKS), blk(0)],
        out_specs=[blk(0), blk(0), wsp(0), wsp(0), bsp(0), bsp(0)],
        out_shape=[jax.ShapeDtypeStruct((T, D_FF), BF16), jax.ShapeDtypeStruct((T, D_FF), BF16),
                   jax.ShapeDtypeStruct((3, D_FF), F32), jax.ShapeDtypeStruct((3, D_FF), F32),
                   jax.ShapeDtypeStruct((1, D_FF), F32), jax.ShapeDtypeStruct((1, D_FF), F32)],
        name="ffn_act_bwd", compiler_params=_params("parallel", "arbitrary"))(up, up, cw, cw, cb, cb, da)


def _down_loss(a, w_down, x1, target, gfin):
    T = x1.shape[0]
    tm = _tile(T, 512)

    def body(a_ref, w_ref, x1_ref, t_ref, g_ref, dx_ref, loss_ref, dg_ref):
        @pl.when(pl.program_id(0) == 0)
        def _():
            loss_ref[...] = jnp.zeros_like(loss_ref)
            dg_ref[...] = jnp.zeros_like(dg_ref)

        x2 = x1_ref[...] + _dot(a_ref[...], w_ref[...])
        r = lax.rsqrt(jnp.mean(x2 * x2, axis=-1, keepdims=True) + EPS)
        xh = x2 * r
        g = g_ref[...]
        diff = xh * g - t_ref[...]
        loss_ref[...] += 0.5 * jnp.sum(jnp.mean(diff * diff, axis=-1, keepdims=True))
        dy = diff * (1.0 / D_MODEL)
        dg_ref[...] += jnp.sum(dy * xh, axis=0, keepdims=True)
        dxh = dy * g
        dx_ref[...] = r * (dxh - xh * jnp.mean(dxh * xh, axis=-1, keepdims=True))

    row = pl.BlockSpec((tm, D_MODEL), lambda i: (i, 0))
    vec = pl.BlockSpec((1, D_MODEL), lambda i: (0, 0))
    return pl.pallas_call(
        body, grid=(T // tm,),
        in_specs=[pl.BlockSpec((tm, D_FF), lambda i: (i, 0)),
                  pl.BlockSpec((D_FF, D_MODEL), lambda i: (0, 0)), row, row, vec],
        out_specs=[row, pl.BlockSpec((8, LANES), lambda i: (0, 0)), vec],
        out_shape=[jax.ShapeDtypeStruct((T, D_MODEL), F32), jax.ShapeDtypeStruct((8, LANES), F32),
                   jax.ShapeDtypeStruct((1, D_MODEL), F32)],
        name="down_loss", compiler_params=_params("arbitrary"))(a, w_down, x1, target, gfin)


def _local_step(x, positions, target, mix_norm, av_g, av_b, w_s, b_s, q_norm, kv_norm, ffn_norm, conv_b,
                final_norm, comm):
    batch, seq, _ = x.shape
    T = batch * seq
    x = x.reshape(T, D_MODEL)
    target = target.reshape(T, D_MODEL)
    pos = positions.reshape(T, 1)
    half = jnp.arange(0, QK_ROPE, 2, dtype=F32) / QK_ROPE
    inv_freq = 1.0 / (ROPE_THETA ** half)
    invf = jnp.concatenate([inv_freq, inv_freq, jnp.zeros((LANES - QK_ROPE,), F32)]).reshape(1, LANES)
    w_st = jnp.swapaxes(w_s, 1, 2)
    b_col = b_s.reshape(A_GROUPS, CHUNK, 1)

    w_main, w_zs = comm.in_weights()
    h, zm = _norm_mm(x, mix_norm, w_main, "in_proj")
    zs = _mm_nn(h, w_zs, F32, "in_proj_small")
    yag = _mixer_a_fwd(zm, av_g, av_b, w_s, b_col)
    wuq_p, wukv, w_out = comm.mla_weights(after=yag)
    q, k, v = _mla_prep_fwd(zs, pos, invf, q_norm, kv_norm, wuq_p, wukv)
    o, lse = _attn_fwd(q, k, v, batch, seq)
    merged, x1 = _merge_out(x, yag, zm, o, w_out)
    w_up, conv_w, w_down = comm.ffn_weights(after=merged)
    h2, up = _norm_mm(x1, ffn_norm, w_up, "up_proj")
    act = _ffn_act(up, conv_w, conv_b, batch, seq)
    dx2, loss_acc, d_final = _down_loss(act, w_down, x1, target, final_norm)

    d_wdown = _mm_tn(act, dx2, "dw_down")
    da = _mm_nt([(dx2, w_down, 0)], F32, "d_act")
    dupg, dupv, dcwg, dcwv, dcbg, dcbv = _ffn_act_bwd(up, conv_w, conv_b, da, batch, seq)
    d_wup = jnp.concatenate([_mm_tn(h2, dupg, "dw_up_gate"), _mm_tn(h2, dupv, "dw_up_val")], axis=1)
    dh2 = _mm_nt([(dupg, w_up, 0), (dupv, w_up, 1)], F32, "d_h2")
    dx1, d_ffn_norm = _rms_bwd(x1, ffn_norm, dh2, dx2, "ffn_norm_bwd")
    d_wout = _mm_tn(merged, dx1, "dw_out")
    token = comm.send_ffn_grads(d_wdown, d_wup, jnp.concatenate([dcwg, dcwv], axis=1), d_wout)
    dmerged = _mm_nt([(dx1, w_out, 0)], F32, "d_merged", dep=token)
    dzm, do, d_avg, d_avb, d_ws, d_bs = _mixer_bwd(zm, o, dmerged, av_g, av_b, w_s, w_st, b_col)
    dq, dk, dv = _attn_bwd(q, k, v, o, do, lse, batch, seq)
    dzs, cqn, dqp, ckvn, dkv, d_qn, d_kvn = _mla_prep_bwd(zs, pos, invf, q_norm, kv_norm, wuq_p, wukv, dq, dk, dv)
    d_wuq_p = _mm_tn(cqn, dqp, "dw_uq")
    d_wukv = _mm_tn(ckvn, dkv, "dw_ukv")
    d_wmain = _mm_tn(h, dzm, "dw_in_main")
    d_wzs = _mm_tn(h, dzs, "dw_in_small")
    token = comm.send_in_grads(d_wmain, d_wzs, d_wuq_p, d_wukv)
    dh = _mm_nt([(dzm, w_main, 0), (dzs, w_zs, 0)], F32, "d_h", dep=token)
    dx, d_mix_norm = _rms_bwd(x, mix_norm, dh, dx1, "mix_norm_bwd")

    small = dict(
        mix_norm=d_mix_norm, a_v_norm_g=d_avg, a_v_norm_b=d_avb, a_spatial_w=d_ws,
        a_spatial_b=d_bs.reshape(A_GROUPS, CHUNK), q_a_norm=d_qn, kv_a_norm=d_kvn, ffn_norm=d_ffn_norm,
        conv_b=jnp.concatenate([dcbg, dcbv], axis=1), final_norm=d_final)
    return loss_acc[0, 0], dx.reshape(batch, seq, D_MODEL), small


MESH_ID = pl.DeviceIdType.MESH
EFFECT = pltpu.SideEffectType.DATAFLOW_SIDE_EFFECTING


def _mesh_pos():
    return lax.axis_index("x"), lax.axis_index("y"), lax.axis_index("c")


def _peer(pos, d):
    x, y, c = pos
    px = 1 - x if d & 4 else x
    py = 1 - y if d & 2 else y
    pc = 1 - c if d & 1 else c
    return (px, py, pc), 4 * px + 2 * py + pc


def _copy(src_ref, land_ref, send_sems, recv_sems, a, d, pos, exchange, landing_here):
    peer, pid = _peer(pos, d)
    me = 4 * pos[0] + 2 * pos[1] + pos[2]
    if exchange:
        src, dst = src_ref.at[pid], land_ref.at[d]
    else:
        src, dst = src_ref, land_ref.at[pid if landing_here else me]
    return pltpu.make_async_remote_copy(
        src_ref=src, dst_ref=dst, send_sem=send_sems.at[a * (N_DEV - 1) + d - 1],
        recv_sem=recv_sems.at[a * (N_DEV - 1) + d - 1],
        device_id=peer, device_id_type=MESH_ID)


def _start_copies(groups, exchange, name):
    sizes = [len(g) for g in groups]
    srcs = [s for g in groups for s, _ in g]
    lands = [l for g in groups for _, l in g]
    n, ng = len(srcs), len(groups)

    def body(*refs):
        src_refs, land_refs = refs[:n], refs[n:2 * n]
        sems = refs[2 * n:2 * n + 2 * ng]
        token = refs[-1]
        pos = _mesh_pos()
        k = 0
        for gi, size in enumerate(sizes):
            for a in range(size):
                for d in range(1, N_DEV):
                    _copy(src_refs[k], land_refs[k], sems[2 * gi], sems[2 * gi + 1], a, d, pos, exchange,
                          landing_here=False).start()
                k += 1
        token[...] = jnp.zeros_like(token)

    sem_shapes = [pltpu.SemaphoreType.DMA((size * (N_DEV - 1),)) for size in sizes for _ in range(2)]
    out = pl.pallas_call(
        body, name=name,
        out_shape=(*sem_shapes, *[pltpu.HBM(a.shape, a.dtype) for a in srcs + lands],
                   jax.ShapeDtypeStruct((8, LANES), F32)),
        in_specs=[HBM] * (2 * n),
        out_specs=(*[SEM] * (2 * ng), *[HBM] * (2 * n), pl.BlockSpec(memory_space=pltpu.VMEM)),
        input_output_aliases={i: 2 * ng + i for i in range(2 * n)},
        compiler_params=pltpu.CompilerParams(has_side_effects=EFFECT),
    )(*[pltpu.with_memory_space_constraint(a, pltpu.HBM) for a in srcs + lands])
    thru = out[2 * ng:2 * ng + 2 * n]
    handles, k = [], 0
    for gi, size in enumerate(sizes):
        handles.append((out[2 * gi], out[2 * gi + 1], thru[k:k + size], thru[n + k:n + k + size]))
        k += size
    return handles, out[-1]


def _wait_copies(handle, exchange, after, name):
    send_sems, recv_sems, srcs, lands = handle
    n = len(srcs)

    def body(*refs):
        src_refs, land_refs = refs[:n], refs[n:2 * n]
        send, recv = refs[2 * n], refs[2 * n + 1]
        pos = _mesh_pos()
        for a in range(n):
            for d in range(1, N_DEV):
                cp = _copy(src_refs[a], land_refs[a], send, recv, a, d, pos, exchange, landing_here=True)
                cp.wait_send()
                cp.wait_recv()

    out = pl.pallas_call(
        body, name=name,
        out_shape=tuple(pltpu.HBM(a.shape, a.dtype) for a in (*srcs, *lands)),
        in_specs=[HBM] * (2 * n) + [SEM, SEM, ANY], out_specs=[HBM] * (2 * n),
        input_output_aliases={i: i for i in range(2 * n)},
        compiler_params=pltpu.CompilerParams(has_side_effects=EFFECT),
    )(*srcs, *lands, send_sems, recv_sems, after)
    return out[n:]


def _land_gather(a, me):
    zone = jnp.zeros((N_DEV,) + a.shape, a.dtype)
    return lax.dynamic_update_slice(zone, a[None], (me,) + (0,) * a.ndim)


def _land_exchange(s, me):
    own = lax.dynamic_index_in_dim(s, me, 0, keepdims=True)
    return lax.dynamic_update_slice(jnp.zeros_like(s), own, (0,) * s.ndim)


def _adamw(parts, w, m, v, name):
    R, C = w.shape
    tr = next((t for t in range(min(R, 256) // 8 * 8, 7, -8) if R % t == 0), R)
    c1 = 1.0 - ADAM_B1 ** ADAM_STEP
    c2 = 1.0 - ADAM_B2 ** ADAM_STEP

    def body(p_ref, w_ref, m_ref, v_ref, g_ref, d_ref, nm_ref, nv_ref):
        g = p_ref[0].astype(F32)
        for k in range(1, N_DEV):
            g = g + p_ref[k].astype(F32)
        nm = ADAM_B1 * m_ref[...] + (1.0 - ADAM_B1) * g
        nv = ADAM_B2 * v_ref[...] + (1.0 - ADAM_B2) * (g * g)
        g_ref[...] = g
        nm_ref[...] = nm
        nv_ref[...] = nv
        d_ref[...] = -ADAM_LR * ((nm / c1) / (jnp.sqrt(nv / c2) + ADAM_EPS) + ADAM_WD * w_ref[...])

    blk = pl.BlockSpec((tr, C), lambda i: (i, 0))
    shp = jax.ShapeDtypeStruct((R, C), F32)
    return pl.pallas_call(
        body, grid=(R // tr,),
        in_specs=[pl.BlockSpec((N_DEV, tr, C), lambda i: (0, i, 0)), blk, blk, blk],
        out_specs=[blk, blk, blk, blk], out_shape=[shp, shp, shp, shp],
        name=name, compiler_params=_params("parallel"))(parts, w, m, v)


SPLIT_V = 2 * D_MODEL
SPLIT_KR = SPLIT_V + Q_LORA + KV_LORA + QK_ROPE
IN_DIM = SPLIT_KR + 2 * D_MODEL

SMALL = ("mix_norm", "a_v_norm_g", "a_v_norm_b", "a_spatial_w", "a_spatial_b", "q_a_norm", "kv_a_norm",
         "ffn_norm", "conv_b", "final_norm")


def _pack_rows(a):
    flat = a.reshape(-1)
    rows = -(-flat.shape[0] // LANES)
    rows8 = -(-rows // 8) * 8
    return jnp.pad(flat, (0, rows8 * LANES - flat.shape[0])).reshape(rows8, LANES)


def _pack_small(tree, extra=None):
    parts = [_pack_rows(tree[n]) for n in SMALL]
    if extra is not None:
        parts.append(_pack_rows(extra))
    return jnp.concatenate(parts, axis=0)


def _unpack_small(buf, shapes):
    out, r = {}, 0
    for n in SMALL:
        size = math.prod(shapes[n])
        rows8 = -(-(-(-size // LANES)) // 8) * 8
        out[n] = buf[r:r + rows8].reshape(-1)[:size].reshape(shapes[n])
        r += rows8
    return out, r


def _cols_from_shards(g):
    return jnp.transpose(g, (1, 0, 2)).reshape(g.shape[1], N_DEV * g.shape[2])


def _shards_from_cols(a):
    R, W = a.shape
    return jnp.transpose(a.reshape(R, N_DEV, W // N_DEV), (1, 0, 2))


class _Comm:
    GATHER_GROUPS = (("w_in",), ("w_uq", "w_ukv", "w_out"), ("w_up", "conv_w", "w_down"))
    FFN_GRADS = ("w_down", "w_up", "conv_w", "w_out")
    IN_GRADS = ("w_in", "w_uq", "w_ukv")

    def __init__(self, shards, me):
        self.me = me
        local = {n: a.astype(F32 if n == "conv_w" else BF16) for n, a in shards.items()}
        groups = [[(local[n], _land_gather(local[n], me)) for n in g] for g in self.GATHER_GROUPS]
        (self.h_in, self.h_mla, self.h_ffn), self.token = _start_copies(groups, False, "gather_start")

    def in_weights(self):
        (g_in,) = _wait_copies(self.h_in, False, self.token, "gather_wait_in")
        w_full = _cols_from_shards(g_in)
        w_main = jnp.concatenate([w_full[:, :SPLIT_V], w_full[:, SPLIT_KR:]], axis=1)
        w_zs = jnp.pad(w_full[:, SPLIT_V:SPLIT_KR], ((0, 0), (0, ZS_W - (SPLIT_KR - SPLIT_V))))
        return w_main, w_zs

    def mla_weights(self, after):
        g_uq, g_ukv, g_out = _wait_copies(self.h_mla, False, after, "gather_wait_mla")
        wuq_p = _cols_from_shards(jnp.pad(g_uq, ((0, 0), (0, 0), (0, HEAD_PAD - QK_HEAD))))
        return wuq_p, _cols_from_shards(g_ukv), g_out.reshape(D_MODEL, D_MODEL)

    def ffn_weights(self, after):
        g_up, g_cw, g_down = _wait_copies(self.h_ffn, False, after, "gather_wait_ffn")
        return _cols_from_shards(g_up), _cols_from_shards(g_cw), g_down.reshape(D_FF, D_MODEL)

    def _send(self, blocks, name):
        (handle,), token = _start_copies([[(s, _land_exchange(s, self.me)) for s in blocks]], True, name)
        return handle, token

    def send_ffn_grads(self, d_wdown, d_wup, d_convw, d_wout):
        self.h_ffn_grads, token = self._send([
            d_wdown.reshape(N_DEV, D_FF // N_DEV, D_MODEL).astype(BF16),
            _shards_from_cols(d_wup).astype(BF16),
            _shards_from_cols(d_convw),
            d_wout.reshape(N_DEV, D_MODEL // N_DEV, D_MODEL).astype(BF16)], "ffn_grads_start")
        return token

    def send_in_grads(self, d_wmain, d_wzs, d_wuq_p, d_wukv):
        d_in = jnp.concatenate([d_wmain[:, :SPLIT_V], d_wzs[:, :SPLIT_KR - SPLIT_V], d_wmain[:, SPLIT_V:]], axis=1)
        self.h_in_grads, token = self._send([
            _shards_from_cols(d_in).astype(BF16),
            _shards_from_cols(d_wuq_p)[:, :, :QK_HEAD].astype(BF16),
            _shards_from_cols(d_wukv).astype(BF16)], "in_grads_start")
        return token


def kernel(x, positions, mix_norm, w_in, a_v_norm_g, a_v_norm_b, a_spatial_w, a_spatial_b, q_a_norm, w_uq, kv_a_norm, w_ukv, w_out, ffn_norm, w_up, conv_w, conv_b, w_down, final_norm, loss_target, m_mix_norm, m_w_in, m_a_v_norm_g, m_a_v_norm_b, m_a_spatial_w, m_a_spatial_b, m_q_a_norm, m_w_uq, m_kv_a_norm, m_w_ukv, m_w_out, m_ffn_norm, m_w_up, m_conv_w, m_conv_b, m_w_down, m_final_norm, v_mix_norm, v_w_in, v_a_v_norm_g, v_a_v_norm_b, v_a_spatial_w, v_a_spatial_b, v_q_a_norm, v_w_uq, v_kv_a_norm, v_w_ukv, v_w_out, v_ffn_norm, v_w_up, v_conv_w, v_conv_b, v_w_down, v_final_norm):
    names = ("mix_norm", "w_in", "a_v_norm_g", "a_v_norm_b", "a_spatial_w", "a_spatial_b", "q_a_norm", "w_uq",
             "kv_a_norm", "w_ukv", "w_out", "ffn_norm", "w_up", "conv_w", "conv_b", "w_down", "final_norm")
    w = dict(zip(names, (mix_norm, w_in, a_v_norm_g, a_v_norm_b, a_spatial_w, a_spatial_b, q_a_norm, w_uq,
                         kv_a_norm, w_ukv, w_out, ffn_norm, w_up, conv_w, conv_b, w_down, final_norm)))
    m = dict(zip(names, (m_mix_norm, m_w_in, m_a_v_norm_g, m_a_v_norm_b, m_a_spatial_w, m_a_spatial_b,
                         m_q_a_norm, m_w_uq, m_kv_a_norm, m_w_ukv, m_w_out, m_ffn_norm, m_w_up, m_conv_w,
                         m_conv_b, m_w_down, m_final_norm)))
    v = dict(zip(names, (v_mix_norm, v_w_in, v_a_v_norm_g, v_a_v_norm_b, v_a_spatial_w, v_a_spatial_b,
                         v_q_a_norm, v_w_uq, v_kv_a_norm, v_w_ukv, v_w_out, v_ffn_norm, v_w_up, v_conv_w,
                         v_conv_b, v_w_down, v_final_norm)))
    shapes = {n: w[n].shape for n in names}
    me = 4 * lax.axis_index("x") + 2 * lax.axis_index("y") + lax.axis_index("c")
    comm = _Comm({n: w[n][0] for n in _Comm.GATHER_GROUPS[0] + _Comm.GATHER_GROUPS[1] + _Comm.GATHER_GROUPS[2]}, me)

    loss_part, grad_x, small_g = _local_step(
        x, positions, loss_target, w["mix_norm"], w["a_v_norm_g"], w["a_v_norm_b"], w["a_spatial_w"][0],
        w["a_spatial_b"][0], w["q_a_norm"], w["kv_a_norm"], w["ffn_norm"], w["conv_b"],
        w["final_norm"].reshape(1, D_MODEL), comm)

    small_local = _pack_small({n: small_g[n].reshape(shapes[n]) for n in SMALL}, extra=loss_part.reshape(1))
    (h_small,), token = _start_copies([[(small_local, _land_gather(small_local, me))]], False, "small_grads_start")

    out_g, out_d, out_m, out_v = {}, {}, {}, {}

    def update(n, parts):
        shp = shapes[n]
        r2 = (shp[-2], shp[-1])
        res = _adamw(parts, w[n].reshape(r2), m[n].reshape(r2), v[n].reshape(r2), "adamw_" + n)
        out_g[n], out_d[n], out_m[n], out_v[n] = (t.reshape(shp) for t in res)

    for n, parts in zip(_Comm.FFN_GRADS, _wait_copies(comm.h_ffn_grads, True, token, "ffn_grads_wait")):
        update(n, parts)
    for n, parts in zip(_Comm.IN_GRADS, _wait_copies(comm.h_in_grads, True, out_d["w_up"], "in_grads_wait")):
        update(n, parts)
    (small_parts,) = _wait_copies(h_small, False, out_d["w_in"], "small_grads_wait")

    zero = jnp.zeros((1,), F32)
    res = _adamw(small_parts, _pack_small(w, extra=zero), _pack_small(m, extra=zero),
                 _pack_small(v, extra=zero + 1.0), "adamw_small")
    unpacked = [_unpack_small(t, shapes) for t in res]
    for n in SMALL:
        out_g[n], out_d[n], out_m[n], out_v[n] = (u[0][n] for u in unpacked)
    loss = res[0][unpacked[0][1], 0]

    return (loss, grad_x, *[out_g[n] for n in names], *[out_d[n] for n in names],
            *[out_m[n] for n in names], *[out_v[n] for n in names])
```

```python
import functools
import math

import jax
import jax.numpy as jnp
from jax import lax
from jax.experimental import pallas as pl
from jax.experimental.pallas import tpu as pltpu

F32 = jnp.float32
BF16 = jnp.bfloat16

N_DEV = 8
D_MODEL = 1024
EPS = 1e-6
A_GROUPS = 8
CHUNK = 128
MLA_HEADS = 8
QK_NOPE = 128
QK_ROPE = 64
QK_HEAD = QK_NOPE + QK_ROPE
HEAD_PAD = 256
V_HEAD = 128
Q_LORA = 256
KV_LORA = 128
ROPE_THETA = 10000.0
D_FF = 2816
ZS_W = 512
ATTN_SCALE = QK_HEAD ** -0.5
NEG_BIG = -1e30

ADAM_LR = 0.001
ADAM_B1 = 0.9
ADAM_B2 = 0.999
ADAM_EPS = 1e-08
ADAM_WD = 0.01
ADAM_STEP = 10

VMEM_LIMIT = 56 * 1024 * 1024
SMALL_BLOCK_BYTES = 5 * 1024 * 1024
LANES = 128

GELU_K = math.sqrt(2.0 / math.pi)
GELU_C = 0.044715

ANY = pl.BlockSpec(memory_space=pl.ANY)
HBM = pl.BlockSpec(memory_space=pltpu.HBM)
SEM = pl.BlockSpec(memory_space=pltpu.SEMAPHORE)


def _tile(n, pref):
    for t in (pref, 512, 256, 128, 64, 32, 16, 8):
        if t <= pref and n % t == 0:
            return t
    return n


def _wide_tile(n, cap=1408):
    return next((t for t in range(min(n, cap) // LANES * LANES, 0, -LANES) if n % t == 0), n)


def _params(*sem):
    return pltpu.CompilerParams(dimension_semantics=sem, vmem_limit_bytes=VMEM_LIMIT)


def _dot(a, b):
    return jnp.dot(a, b, preferred_element_type=F32)


def _dot_nt(a, b):
    return lax.dot_general(a, b, (((1,), (1,)), ((), ())), preferred_element_type=F32)


def _dot_tn(a, b):
    return lax.dot_general(a, b, (((0,), (0,)), ((), ())), preferred_element_type=F32)


def _sigmoid(x):
    return 1.0 / (1.0 + jnp.exp(-x))


def _gelu(x):
    t = jnp.tanh(GELU_K * (x + GELU_C * x * x * x))
    return 0.5 * x * (1.0 + t)


def _gelu_grad(x):
    t = jnp.tanh(GELU_K * (x + GELU_C * x * x * x))
    return 0.5 * (1.0 + t) + 0.5 * x * (1.0 - t * t) * GELU_K * (1.0 + 3.0 * GELU_C * x * x)


def _norm_mm(x, g, ws, name):
    T, Dm = x.shape
    tm = _tile(T, 256)
    n_w = len(ws)

    def body(x_ref, g_ref, *refs):
        w_refs, h_ref, z_refs = refs[:n_w], refs[n_w], refs[n_w + 1:]
        xf = x_ref[...]
        r = lax.rsqrt(jnp.mean(xf * xf, axis=-1, keepdims=True) + EPS)
        h = (xf * r * g_ref[...]).astype(BF16)
        h_ref[...] = h
        for w_ref, z_ref in zip(w_refs, z_refs):
            z_ref[...] = _dot(h, w_ref[...])

    row = lambda n: pl.BlockSpec((tm, n), lambda i: (i, 0))
    return pl.pallas_call(
        body, grid=(T // tm,),
        in_specs=[row(Dm), pl.BlockSpec((1, Dm), lambda i: (0, 0))]
        + [pl.BlockSpec(w.shape, lambda i: (0, 0)) for w in ws],
        out_specs=[row(Dm)] + [row(w.shape[1]) for w in ws],
        out_shape=[jax.ShapeDtypeStruct((T, Dm), BF16)]
        + [jax.ShapeDtypeStruct((T, w.shape[1]), F32) for w in ws],
        name=name, compiler_params=_params("parallel"))(x, g, *ws)


def _mm_nt(pairs, out_dtype, name, dep=None):
    T = pairs[0][0].shape[0]
    K = pairs[0][1].shape[0]
    tm = _tile(T, 512)
    n_pairs = len(pairs)

    def body(*refs):
        o_ref = refs[-1]
        acc = None
        for p in range(n_pairs):
            t = _dot_nt(refs[2 * p][...].astype(BF16), refs[2 * p + 1][...])
            acc = t if acc is None else acc + t
        o_ref[...] = acc.astype(o_ref.dtype)

    in_specs, args = [], []
    for a, w, c in pairs:
        in_specs += [pl.BlockSpec((tm, a.shape[1]), lambda i: (i, 0)),
                     pl.BlockSpec((K, a.shape[1]), functools.partial(lambda i, c: (0, c), c=c))]
        args += [a, w]
    if dep is not None:
        in_specs.append(ANY)
        args.append(dep)
    return pl.pallas_call(
        body, grid=(T // tm,), in_specs=in_specs,
        out_specs=pl.BlockSpec((tm, K), lambda i: (i, 0)),
        out_shape=jax.ShapeDtypeStruct((T, K), out_dtype),
        name=name, compiler_params=_params("parallel"))(*args)


def _mm_tn(a, b, name):
    T, M = a.shape
    N = b.shape[1]
    tm, tn, tt = _wide_tile(M), _wide_tile(N), _tile(T, 512)
    n_t = T // tt

    def body(a_ref, b_ref, o_ref, acc_ref):
        t = pl.program_id(2)

        @pl.when(t == 0)
        def _():
            acc_ref[...] = jnp.zeros_like(acc_ref)

        acc_ref[...] += _dot_tn(a_ref[...].astype(BF16), b_ref[...].astype(BF16))

        @pl.when(t == n_t - 1)
        def _():
            o_ref[...] = acc_ref[...].astype(BF16)

    return pl.pallas_call(
        body, grid=(M // tm, N // tn, n_t),
        in_specs=[pl.BlockSpec((tt, tm), lambda i, j, t: (t, i)),
                  pl.BlockSpec((tt, tn), lambda i, j, t: (t, j))],
        out_specs=pl.BlockSpec((tm, tn), lambda i, j, t: (i, j)),
        out_shape=jax.ShapeDtypeStruct((M, N), BF16),
        scratch_shapes=[pltpu.VMEM((tm, tn), F32)],
        name=name, compiler_params=_params("parallel", "parallel", "arbitrary"))(a, b)


def _rms_bwd(x, g, dy, dres, name):
    T, Dm = x.shape
    tm = _tile(T, 512)

    def body(x_ref, g_ref, dy_ref, dres_ref, dx_ref, dg_ref):
        @pl.when(pl.program_id(0) == 0)
        def _():
            dg_ref[...] = jnp.zeros_like(dg_ref)

        xf = x_ref[...]
        r = lax.rsqrt(jnp.mean(xf * xf, axis=-1, keepdims=True) + EPS)
        xh = xf * r
        dyv = dy_ref[...]
        dg_ref[...] += jnp.sum(dyv * xh, axis=0, keepdims=True)
        dxh = dyv * g_ref[...]
        dx_ref[...] = dres_ref[...] + r * (dxh - xh * jnp.mean(dxh * xh, axis=-1, keepdims=True))

    row = pl.BlockSpec((tm, Dm), lambda i: (i, 0))
    vec = pl.BlockSpec((1, Dm), lambda i: (0, 0))
    return pl.pallas_call(
        body, grid=(T // tm,), in_specs=[row, vec, row, row], out_specs=[row, vec],
        out_shape=[jax.ShapeDtypeStruct((T, Dm), F32), jax.ShapeDtypeStruct((1, Dm), F32)],
        name=name, compiler_params=_params("arbitrary"))(x, g, dy, dres)


def _layer_norm_fwd(gv, g, b):
    mu = jnp.mean(gv, axis=-1, keepdims=True)
    xc = gv - mu
    rs = lax.rsqrt(jnp.mean(xc * xc, axis=-1, keepdims=True) + EPS)
    xh = xc * rs
    return xh, rs, xh * g + b


def _tri_mask(transposed=False):
    r = lax.broadcasted_iota(jnp.int32, (CHUNK, CHUNK), 0)
    c = lax.broadcasted_iota(jnp.int32, (CHUNK, CHUNK), 1)
    return r <= c if transposed else c <= r


def _mixer_a_fwd(zm, av_g, av_b, w_s, b_col):
    T = zm.shape[0]
    tm = _tile(T, 256)
    n_chunk = tm // CHUNK

    def body(u_ref, v_ref, ga_ref, g_ref, b_ref, w_ref, bc_ref, y_ref, vn_s, mx_s):
        gu = _gelu(u_ref[...])
        _, _, vn = _layer_norm_fwd(_gelu(v_ref[...]), g_ref[...], b_ref[...])
        vn_s[...] = vn.astype(BF16)
        tri = _tri_mask()
        for gi in range(A_GROUPS):
            wm = jnp.where(tri, w_ref[gi], 0.0).astype(BF16)
            cols = slice(gi * CHUNK, (gi + 1) * CHUNK)
            for n in range(n_chunk):
                rows = slice(n * CHUNK, (n + 1) * CHUNK)
                mx_s[rows, cols] = _dot(wm, vn_s[rows, cols]) + bc_ref[gi]
        y_ref[...] = _sigmoid(ga_ref[...]) * gu * mx_s[...]

    col = lambda c: pl.BlockSpec((tm, D_MODEL), lambda i: (i, c))
    vec = pl.BlockSpec((1, D_MODEL), lambda i: (0, 0))
    return pl.pallas_call(
        body, grid=(T // tm,),
        in_specs=[col(0), col(1), col(2), vec, vec,
                  pl.BlockSpec((A_GROUPS, CHUNK, CHUNK), lambda i: (0, 0, 0)),
                  pl.BlockSpec((A_GROUPS, CHUNK, 1), lambda i: (0, 0, 0))],
        out_specs=pl.BlockSpec((tm, D_MODEL), lambda i: (i, 0)),
        out_shape=jax.ShapeDtypeStruct((T, D_MODEL), F32),
        scratch_shapes=[pltpu.VMEM((tm, D_MODEL), BF16), pltpu.VMEM((tm, D_MODEL), F32)],
        name="mixer_a_fwd", compiler_params=_params("parallel"))(zm, zm, zm, av_g, av_b, w_s, b_col)


def _mixer_bwd(zm, o, dm, av_g, av_b, w_s, w_st, b_col):
    T = zm.shape[0]
    tm = _tile(T, 256)
    n_chunk = tm // CHUNK

    def body(u_ref, v_ref, ga_ref, gb_ref, o_ref, dm_ref, g_ref, b_ref, w_ref, wt_ref, bc_ref,
             dz_ref, do_ref, dg_ref, db_ref, dw_ref, dbs_ref, vn_s, mx_s, dmx_s, dvn_s):
        @pl.when(pl.program_id(0) == 0)
        def _():
            dg_ref[...] = jnp.zeros_like(dg_ref)
            db_ref[...] = jnp.zeros_like(db_ref)
            dw_ref[...] = jnp.zeros_like(dw_ref)
            dbs_ref[...] = jnp.zeros_like(dbs_ref)

        dm_v = dm_ref[...]
        gb = gb_ref[...]
        sb = _sigmoid(gb)
        o_v = o_ref[...]
        do_ref[...] = (dm_v * sb).astype(BF16)
        dz_ref[:, 3 * D_MODEL:4 * D_MODEL] = (dm_v * o_v * sb * (1.0 - sb)).astype(BF16)
        u = u_ref[...]
        v = v_ref[...]
        gu = _gelu(u)
        xh, rs, vn = _layer_norm_fwd(_gelu(v), g_ref[...], b_ref[...])
        vn_s[...] = vn.astype(BF16)
        tri = _tri_mask()
        for gi in range(A_GROUPS):
            wm = jnp.where(tri, w_ref[gi], 0.0).astype(BF16)
            cols = slice(gi * CHUNK, (gi + 1) * CHUNK)
            for n in range(n_chunk):
                rows = slice(n * CHUNK, (n + 1) * CHUNK)
                mx_s[rows, cols] = _dot(wm, vn_s[rows, cols]) + bc_ref[gi]
        mixed = mx_s[...]
        sa = _sigmoid(ga_ref[...])
        dya = dm_v * sa
        dz_ref[:, 2 * D_MODEL:3 * D_MODEL] = (dm_v * gu * mixed * sa * (1.0 - sa)).astype(BF16)
        dz_ref[:, 0:D_MODEL] = (dya * mixed * _gelu_grad(u)).astype(BF16)
        dmx = dya * gu
        dmx_s[...] = dmx.astype(BF16)
        tri_t = _tri_mask(transposed=True)
        for gi in range(A_GROUPS):
            wmt = jnp.where(tri_t, wt_ref[gi], 0.0).astype(BF16)
            cols = slice(gi * CHUNK, (gi + 1) * CHUNK)
            dw_acc = jnp.zeros((CHUNK, CHUNK), F32)
            dmx_sum = jnp.zeros((CHUNK, CHUNK), F32)
            for n in range(n_chunk):
                rows = slice(n * CHUNK, (n + 1) * CHUNK)
                blk = dmx_s[rows, cols]
                dvn_s[rows, cols] = _dot(wmt, blk)
                dw_acc = dw_acc + _dot_nt(blk, vn_s[rows, cols])
                dmx_sum = dmx_sum + dmx[rows, cols]
            dw_ref[gi] += jnp.where(tri, dw_acc, 0.0)
            dbs_ref[gi] += jnp.sum(dmx_sum, axis=-1, keepdims=True)
        dvn = dvn_s[...]
        dg_ref[...] += jnp.sum(dvn * xh, axis=0, keepdims=True)
        db_ref[...] += jnp.sum(dvn, axis=0, keepdims=True)
        dxh = dvn * g_ref[...]
        dgv = rs * (dxh - jnp.mean(dxh, axis=-1, keepdims=True)
                    - xh * jnp.mean(dxh * xh, axis=-1, keepdims=True))
        dz_ref[:, D_MODEL:2 * D_MODEL] = (dgv * _gelu_grad(v)).astype(BF16)

    col = lambda c: pl.BlockSpec((tm, D_MODEL), lambda i: (i, c))
    row = pl.BlockSpec((tm, D_MODEL), lambda i: (i, 0))
    vec = pl.BlockSpec((1, D_MODEL), lambda i: (0, 0))
    wsp = pl.BlockSpec((A_GROUPS, CHUNK, CHUNK), lambda i: (0, 0, 0))
    bsp = pl.BlockSpec((A_GROUPS, CHUNK, 1), lambda i: (0, 0, 0))
    return pl.pallas_call(
        body, grid=(T // tm,),
        in_specs=[col(0), col(1), col(2), col(3), row, row, vec, vec, wsp, wsp, bsp],
        out_specs=[pl.BlockSpec((tm, 4 * D_MODEL), lambda i: (i, 0)), row, vec, vec, wsp, bsp],
        out_shape=[jax.ShapeDtypeStruct((T, 4 * D_MODEL), BF16), jax.ShapeDtypeStruct((T, D_MODEL), BF16),
                   jax.ShapeDtypeStruct((1, D_MODEL), F32), jax.ShapeDtypeStruct((1, D_MODEL), F32),
                   jax.ShapeDtypeStruct((A_GROUPS, CHUNK, CHUNK), F32),
                   jax.ShapeDtypeStruct((A_GROUPS, CHUNK, 1), F32)],
        scratch_shapes=[pltpu.VMEM((tm, D_MODEL), BF16), pltpu.VMEM((tm, D_MODEL), F32),
                        pltpu.VMEM((tm, D_MODEL), BF16), pltpu.VMEM((tm, D_MODEL), F32)],
        name="mixer_bwd", compiler_params=_params("arbitrary"))(
            zm, zm, zm, zm, o, dm, av_g, av_b, w_s, w_st, b_col)


def _rope_tables(pos_ref, invf_ref):
    ang = pos_ref[...].astype(F32) * invf_ref[...]
    lane = lax.broadcasted_iota(jnp.int32, ang.shape, 1)
    cos, sin = jnp.cos(ang), jnp.sin(ang)
    c = jnp.where(lane < QK_ROPE, cos, 0.0)
    sa = jnp.where(lane < QK_ROPE // 2, -sin, 0.0)
    sb = jnp.where((lane >= QK_ROPE // 2) & (lane < QK_ROPE), sin, 0.0)
    return c, sa, sb


def _rope(blk, tabs):
    c, sa, sb = tabs
    return blk * c + pltpu.roll(blk, LANES - QK_ROPE // 2, 1) * sa + pltpu.roll(blk, QK_ROPE // 2, 1) * sb


def _rope_t(dout, tabs):
    c, sa, sb = tabs
    return dout * c + pltpu.roll(dout * sa, QK_ROPE // 2, 1) + pltpu.roll(dout * sb, LANES - QK_ROPE // 2, 1)


def _rms_small(x, g):
    r = lax.rsqrt(jnp.mean(x * x, axis=-1, keepdims=True) + EPS)
    xh = x * r
    return xh, r, xh * g


def _mla_prep_fwd(zs, pos, invf, qg, kvg, wuq_p, wukv):
    T = zs.shape[0]
    tm = _tile(T, 512)
    HW = MLA_HEADS * HEAD_PAD

    def body(zs_ref, pos_ref, invf_ref, qg_ref, kvg_ref, wq_ref, wkv_ref, q_ref, k_ref, v_ref):
        tabs = _rope_tables(pos_ref, invf_ref)
        _, _, cqn = _rms_small(zs_ref[:, 0:Q_LORA], qg_ref[...])
        _, _, ckvn = _rms_small(zs_ref[:, Q_LORA:Q_LORA + KV_LORA], kvg_ref[...])
        q = _dot(cqn.astype(BF16), wq_ref[...])
        kv = _dot(ckvn.astype(BF16), wkv_ref[...])
        kr = _rope(zs_ref[:, Q_LORA + KV_LORA:ZS_W], tabs).astype(BF16)
        for h in range(MLA_HEADS):
            b0 = h * HEAD_PAD
            q_ref[:, b0:b0 + QK_NOPE] = q[:, b0:b0 + QK_NOPE].astype(BF16)
            q_ref[:, b0 + QK_NOPE:b0 + HEAD_PAD] = _rope(q[:, b0 + QK_NOPE:b0 + HEAD_PAD], tabs).astype(BF16)
            k_ref[:, b0:b0 + QK_NOPE] = kv[:, b0:b0 + QK_NOPE].astype(BF16)
            k_ref[:, b0 + QK_NOPE:b0 + HEAD_PAD] = kr
            v_ref[:, h * V_HEAD:(h + 1) * V_HEAD] = kv[:, b0 + QK_NOPE:b0 + HEAD_PAD].astype(BF16)

    full = lambda a: pl.BlockSpec(a.shape, lambda i: (0,) * a.ndim)
    return pl.pallas_call(
        body, grid=(T // tm,),
        in_specs=[pl.BlockSpec((tm, ZS_W), lambda i: (i, 0)), pl.BlockSpec((tm, 1), lambda i: (i, 0)),
                  full(invf), full(qg), full(kvg), full(wuq_p), full(wukv)],
        out_specs=[pl.BlockSpec((tm, HW), lambda i: (i, 0)), pl.BlockSpec((tm, HW), lambda i: (i, 0)),
                   pl.BlockSpec((tm, D_MODEL), lambda i: (i, 0))],
        out_shape=[jax.ShapeDtypeStruct((T, HW), BF16), jax.ShapeDtypeStruct((T, HW), BF16),
                   jax.ShapeDtypeStruct((T, D_MODEL), BF16)],
        name="mla_prep_fwd", compiler_params=_params("parallel"))(zs, pos, invf, qg, kvg, wuq_p, wukv)


def _mla_prep_bwd(zs, pos, invf, qg, kvg, wuq_p, wukv, dq, dk, dv):
    T = zs.shape[0]
    tm = _tile(T, 256)
    HW = MLA_HEADS * HEAD_PAD

    def body(zs_ref, pos_ref, invf_ref, qg_ref, kvg_ref, wq_ref, wkv_ref, dq_ref, dk_ref, dv_ref,
             dzs_ref, cqn_ref, dqp_ref, ckvn_ref, dkv_ref, dqg_ref, dkvg_ref):
        @pl.when(pl.program_id(0) == 0)
        def _():
            dqg_ref[...] = jnp.zeros_like(dqg_ref)
            dkvg_ref[...] = jnp.zeros_like(dkvg_ref)

        tabs = _rope_tables(pos_ref, invf_ref)
        cqh, rq, cqn = _rms_small(zs_ref[:, 0:Q_LORA], qg_ref[...])
        ckvh, rkv, ckvn = _rms_small(zs_ref[:, Q_LORA:Q_LORA + KV_LORA], kvg_ref[...])
        cqn_ref[...] = cqn.astype(BF16)
        ckvn_ref[...] = ckvn.astype(BF16)
        dkr = jnp.zeros((tm, LANES), F32)
        for h in range(MLA_HEADS):
            b0 = h * HEAD_PAD
            dqp_ref[:, b0:b0 + QK_NOPE] = dq_ref[:, b0:b0 + QK_NOPE].astype(BF16)
            dqp_ref[:, b0 + QK_NOPE:b0 + HEAD_PAD] = _rope_t(dq_ref[:, b0 + QK_NOPE:b0 + HEAD_PAD], tabs).astype(BF16)
            dkv_ref[:, b0:b0 + QK_NOPE] = dk_ref[:, b0:b0 + QK_NOPE].astype(BF16)
            dkv_ref[:, b0 + QK_NOPE:b0 + HEAD_PAD] = dv_ref[:, h * V_HEAD:(h + 1) * V_HEAD].astype(BF16)
            dkr = dkr + dk_ref[:, b0 + QK_NOPE:b0 + HEAD_PAD]
        dcqn = _dot_nt(dqp_ref[...], wq_ref[...])
        dckvn = _dot_nt(dkv_ref[...], wkv_ref[...])
        dqg_ref[...] += jnp.sum(dcqn * cqh, axis=0, keepdims=True)
        dkvg_ref[...] += jnp.sum(dckvn * ckvh, axis=0, keepdims=True)
        dxh = dcqn * qg_ref[...]
        dzs_ref[:, 0:Q_LORA] = (rq * (dxh - cqh * jnp.mean(dxh * cqh, axis=-1, keepdims=True))).astype(BF16)
        dxh = dckvn * kvg_ref[...]
        dzs_ref[:, Q_LORA:Q_LORA + KV_LORA] = (
            rkv * (dxh - ckvh * jnp.mean(dxh * ckvh, axis=-1, keepdims=True))).astype(BF16)
        dzs_ref[:, Q_LORA + KV_LORA:ZS_W] = _rope_t(dkr, tabs).astype(BF16)

    full = lambda a: pl.BlockSpec(a.shape, lambda i: (0,) * a.ndim)
    rowb = lambda w: pl.BlockSpec((tm, w), lambda i: (i, 0))
    return pl.pallas_call(
        body, grid=(T // tm,),
        in_specs=[rowb(ZS_W), rowb(1), full(invf), full(qg), full(kvg), full(wuq_p), full(wukv),
                  rowb(HW), rowb(HW), rowb(D_MODEL)],
        out_specs=[rowb(ZS_W), rowb(Q_LORA), rowb(HW), rowb(KV_LORA), rowb(HW), full(qg), full(kvg)],
        out_shape=[jax.ShapeDtypeStruct((T, ZS_W), BF16), jax.ShapeDtypeStruct((T, Q_LORA), BF16),
                   jax.ShapeDtypeStruct((T, HW), BF16), jax.ShapeDtypeStruct((T, KV_LORA), BF16),
                   jax.ShapeDtypeStruct((T, HW), BF16), jax.ShapeDtypeStruct(qg.shape, F32),
                   jax.ShapeDtypeStruct(kvg.shape, F32)],
        name="mla_prep_bwd", compiler_params=_params("arbitrary"))(
            zs, pos, invf, qg, kvg, wuq_p, wukv, dq, dk, dv)


def _causal(tq, kmax, q0):
    r = lax.broadcasted_iota(jnp.int32, (tq, kmax), 0) + q0
    c = lax.broadcasted_iota(jnp.int32, (tq, kmax), 1)
    return c <= r


def _attn_fwd(q, k, v, batch, seq):
    tq = _tile(seq, 512)
    nq = seq // tq

    def body(q_ref, k_ref, v_ref, o_ref, lse_ref):
        for qi in range(nq):
            rows = slice(qi * tq, (qi + 1) * tq)
            kmax = (qi + 1) * tq
            s = _dot_nt(q_ref[rows, :], k_ref[0:kmax, :]) * ATTN_SCALE
            s = jnp.where(_causal(tq, kmax, qi * tq), s, NEG_BIG)
            m = jnp.max(s, axis=-1, keepdims=True)
            p = jnp.exp(s - m)
            l = jnp.sum(p, axis=-1, keepdims=True)
            o_ref[rows, :] = _dot(p.astype(BF16), v_ref[0:kmax, :]) / l
            lse_ref[rows, :] = jnp.broadcast_to(m + jnp.log(l), (tq, V_HEAD))

    return pl.pallas_call(
        body, grid=(batch, MLA_HEADS),
        in_specs=[pl.BlockSpec((seq, HEAD_PAD), lambda b, h: (b, h)),
                  pl.BlockSpec((seq, HEAD_PAD), lambda b, h: (b, h)),
                  pl.BlockSpec((seq, V_HEAD), lambda b, h: (b, h))],
        out_specs=[pl.BlockSpec((seq, V_HEAD), lambda b, h: (b, h)),
                   pl.BlockSpec((seq, V_HEAD), lambda b, h: (b, h))],
        out_shape=[jax.ShapeDtypeStruct((batch * seq, D_MODEL), F32),
                   jax.ShapeDtypeStruct((batch * seq, D_MODEL), F32)],
        name="attn_fwd", compiler_params=_params("parallel", "parallel"))(q, k, v)


def _attn_bwd(q, k, v, o, do, lse, batch, seq):
    tq = _tile(seq, 512)
    nq = seq // tq

    def body(q_ref, k_ref, v_ref, o_ref, do_ref, lse_ref, dq_ref, dk_ref, dv_ref):
        dk_ref[...] = jnp.zeros_like(dk_ref)
        dv_ref[...] = jnp.zeros_like(dv_ref)
        for qi in range(nq):
            rows = slice(qi * tq, (qi + 1) * tq)
            kmax = (qi + 1) * tq
            qr = q_ref[rows, :]
            dor = do_ref[rows, :]
            kk = k_ref[0:kmax, :]
            s = _dot_nt(qr, kk) * ATTN_SCALE
            p = jnp.where(_causal(tq, kmax, qi * tq), jnp.exp(s - lse_ref[rows, 0:1]), 0.0)
            dp = _dot_nt(dor, v_ref[0:kmax, :])
            delta = jnp.sum(dor.astype(F32) * o_ref[rows, :], axis=-1, keepdims=True)
            ds = (p * (dp - delta) * ATTN_SCALE).astype(BF16)
            dq_ref[rows, :] = _dot(ds, kk)
            dk_ref[0:kmax, :] += _dot_tn(ds, qr)
            dv_ref[0:kmax, :] += _dot_tn(p.astype(BF16), dor)

    qspec = pl.BlockSpec((seq, HEAD_PAD), lambda b, h: (b, h))
    vspec = pl.BlockSpec((seq, V_HEAD), lambda b, h: (b, h))
    T = batch * seq
    return pl.pallas_call(
        body, grid=(batch, MLA_HEADS),
        in_specs=[qspec, qspec, vspec, vspec, vspec, vspec],
        out_specs=[qspec, qspec, vspec],
        out_shape=[jax.ShapeDtypeStruct((T, MLA_HEADS * HEAD_PAD), F32),
                   jax.ShapeDtypeStruct((T, MLA_HEADS * HEAD_PAD), F32),
                   jax.ShapeDtypeStruct((T, D_MODEL), F32)],
        name="attn_bwd", compiler_params=_params("parallel", "parallel"))(q, k, v, o, do, lse)


def _merge_out(x, yag, zm, o, w_out):
    T = x.shape[0]
    tm = _tile(T, 512)

    def body(x_ref, ya_ref, gb_ref, o_ref, w_ref, mg_ref, x1_ref):
        mg = (ya_ref[...] + _sigmoid(gb_ref[...]) * o_ref[...]).astype(BF16)
        mg_ref[...] = mg
        x1_ref[...] = x_ref[...] + _dot(mg, w_ref[...])

    row = pl.BlockSpec((tm, D_MODEL), lambda i: (i, 0))
    return pl.pallas_call(
        body, grid=(T // tm,),
        in_specs=[row, row, pl.BlockSpec((tm, D_MODEL), lambda i: (i, 3)), row,
                  pl.BlockSpec((D_MODEL, D_MODEL), lambda i: (0, 0))],
        out_specs=[row, row],
        out_shape=[jax.ShapeDtypeStruct((T, D_MODEL), BF16), jax.ShapeDtypeStruct((T, D_MODEL), F32)],
        name="merge_out", compiler_params=_params("parallel"))(x, yag, zm, o, w_out)


FF_TILE = 256
FF_BLOCKS = D_FF // FF_TILE


def _shift_down(x, k):
    row = lax.broadcasted_iota(jnp.int32, x.shape, 0)
    return jnp.where(row >= k, pltpu.roll(x, k, 0), 0.0)


def _shift_up(x, k):
    n = x.shape[0]
    row = lax.broadcasted_iota(jnp.int32, x.shape, 0)
    return jnp.where(row < n - k, pltpu.roll(x, n - k, 0), 0.0)


def _conv(x, w_ref, b_ref):
    return b_ref[...] + w_ref[2:3, :] * x + w_ref[1:2, :] * _shift_down(x, 1) + w_ref[0:1, :] * _shift_down(x, 2)


def _ffn_act(up, cw, cb, batch, seq):
    def body(ug_ref, uv_ref, wg_ref, wv_ref, bg_ref, bv_ref, a_ref):
        gate = _conv(ug_ref[...], wg_ref, bg_ref)
        val = _conv(uv_ref[...], wv_ref, bv_ref)
        a_ref[...] = (gate * _sigmoid(gate) * val).astype(BF16)

    blk = lambda off: pl.BlockSpec((seq, FF_TILE), lambda b, j: (b, j + off))
    wsp = lambda off: pl.BlockSpec((3, FF_TILE), lambda b, j: (0, j + off))
    bsp = lambda off: pl.BlockSpec((1, FF_TILE), lambda b, j: (0, j + off))
    return pl.pallas_call(
        body, grid=(batch, FF_BLOCKS),
        in_specs=[blk(0), blk(FF_BLOCKS), wsp(0), wsp(FF_BLOCKS), bsp(0), bsp(FF_BLOCKS)],
        out_specs=blk(0),
        out_shape=jax.ShapeDtypeStruct((batch * seq, D_FF), BF16),
        name="ffn_act", compiler_params=_params("parallel", "parallel"))(up, up, cw, cw, cb, cb)


def _ffn_act_bwd(up, cw, cb, da, batch, seq):
    def half(du, x, w_ref, dx_ref, dw_ref, db_ref):
        dx_ref[...] = (w_ref[2:3, :] * du + w_ref[1:2, :] * _shift_up(du, 1)
                       + w_ref[0:1, :] * _shift_up(du, 2)).astype(BF16)
        dw_ref[2:3, :] += jnp.sum(du * x, axis=0, keepdims=True)
        dw_ref[1:2, :] += jnp.sum(du * _shift_down(x, 1), axis=0, keepdims=True)
        dw_ref[0:1, :] += jnp.sum(du * _shift_down(x, 2), axis=0, keepdims=True)
        db_ref[...] += jnp.sum(du, axis=0, keepdims=True)

    def body(ug_ref, uv_ref, wg_ref, wv_ref, bg_ref, bv_ref, da_ref,
             dg_ref, dv_ref, dwg_ref, dwv_ref, dbg_ref, dbv_ref):
        @pl.when(pl.program_id(1) == 0)
        def _():
            for r in (dwg_ref, dwv_ref, dbg_ref, dbv_ref):
                r[...] = jnp.zeros_like(r)

        ug, uv = ug_ref[...], uv_ref[...]
        gate = _conv(ug, wg_ref, bg_ref)
        val = _conv(uv, wv_ref, bv_ref)
        sg = _sigmoid(gate)
        dav = da_ref[...]
        half(dav * val * sg * (1.0 + gate * (1.0 - sg)), ug, wg_ref, dg_ref, dwg_ref, dbg_ref)
        half(dav * gate * sg, uv, wv_ref, dv_ref, dwv_ref, dbv_ref)

    blk = lambda off: pl.BlockSpec((seq, FF_TILE), lambda j, b: (b, j + off))
    wsp = lambda off: pl.BlockSpec((3, FF_TILE), lambda j, b: (0, j + off))
    bsp = lambda off: pl.BlockSpec((1, FF_TILE), lambda j, b: (0, j + off))
    T = batch * seq
    return pl.pallas_call(
        body, grid=(FF_BLOCKS, batch),
        in_specs=[blk(0), blk(FF_BLOCKS), wsp(0), wsp(FF_BLOCKS), bsp(0), bsp(FF_BLOCKS), blk(0)],
        out_specs=[blk(0), blk(0), wsp(0), wsp(0), bsp(0), bsp(0)],
        out_shape=[jax.ShapeDtypeStruct((T, D_FF), BF16), jax.ShapeDtypeStruct((T, D_FF), BF16),
                   jax.ShapeDtypeStruct((3, D_FF), F32), jax.ShapeDtypeStruct((3, D_FF), F32),
                   jax.ShapeDtypeStruct((1, D_FF), F32), jax.ShapeDtypeStruct((1, D_FF), F32)],
        name="ffn_act_bwd", compiler_params=_params("parallel", "arbitrary"))(up, up, cw, cw, cb, cb, da)


def _down_loss(a, w_down, x1, target, gfin):
    T = x1.shape[0]
    tm = _tile(T, 512)

    def body(a_ref, w_ref, x1_ref, t_ref, g_ref, dx_ref, loss_ref, dg_ref):
        @pl.when(pl.program_id(0) == 0)
        def _():
            loss_ref[...] = jnp.zeros_like(loss_ref)
            dg_ref[...] = jnp.zeros_like(dg_ref)

        x2 = x1_ref[...] + _dot(a_ref[...], w_ref[...])
        r = lax.rsqrt(jnp.mean(x2 * x2, axis=-1, keepdims=True) + EPS)
        xh = x2 * r
        g = g_ref[...]
        diff = xh * g - t_ref[...]
        loss_ref[...] += 0.5 * jnp.sum(jnp.mean(diff * diff, axis=-1, keepdims=True))
        dy = diff * (1.0 / D_MODEL)
        dg_ref[...] += jnp.sum(dy * xh, axis=0, keepdims=True)
        dxh = dy * g
        dx_ref[...] = r * (dxh - xh * jnp.mean(dxh * xh, axis=-1, keepdims=True))

    row = pl.BlockSpec((tm, D_MODEL), lambda i: (i, 0))
    vec = pl.BlockSpec((1, D_MODEL), lambda i: (0, 0))
    return pl.pallas_call(
        body, grid=(T // tm,),
        in_specs=[pl.BlockSpec((tm, D_FF), lambda i: (i, 0)),
                  pl.BlockSpec((D_FF, D_MODEL), lambda i: (0, 0)), row, row, vec],
        out_specs=[row, pl.BlockSpec((8, LANES), lambda i: (0, 0)), vec],
        out_shape=[jax.ShapeDtypeStruct((T, D_MODEL), F32), jax.ShapeDtypeStruct((8, LANES), F32),
                   jax.ShapeDtypeStruct((1, D_MODEL), F32)],
        name="down_loss", compiler_params=_params("arbitrary"))(a, w_down, x1, target, gfin)


def _local_step(x, positions, target, mix_norm, av_g, av_b, w_s, b_s, q_norm, kv_norm, ffn_norm, conv_b,
                final_norm, comm):
    batch, seq, _ = x.shape
    T = batch * seq
    x = x.reshape(T, D_MODEL)
    target = target.reshape(T, D_MODEL)
    pos = positions.reshape(T, 1)
    half = jnp.arange(0, QK_ROPE, 2, dtype=F32) / QK_ROPE
    inv_freq = 1.0 / (ROPE_THETA ** half)
    invf = jnp.concatenate([inv_freq, inv_freq, jnp.zeros((LANES - QK_ROPE,), F32)]).reshape(1, LANES)
    w_st = jnp.swapaxes(w_s, 1, 2)
    b_col = b_s.reshape(A_GROUPS, CHUNK, 1)

    w_main, w_zs = comm.in_weights()
    h, zm, zs = _norm_mm(x, mix_norm, [w_main, w_zs], "in_proj")
    yag = _mixer_a_fwd(zm, av_g, av_b, w_s, b_col)
    wuq_p, wukv, w_out = comm.mla_weights(after=yag)
    q, k, v = _mla_prep_fwd(zs, pos, invf, q_norm, kv_norm, wuq_p, wukv)
    o, lse = _attn_fwd(q, k, v, batch, seq)
    merged, x1 = _merge_out(x, yag, zm, o, w_out)
    w_up, conv_w, w_down = comm.ffn_weights(after=merged)
    h2, up = _norm_mm(x1, ffn_norm, [w_up], "up_proj")
    act = _ffn_act(up, conv_w, conv_b, batch, seq)
    dx2, loss_acc, d_final = _down_loss(act, w_down, x1, target, final_norm)

    d_wdown = _mm_tn(act, dx2, "dw_down")
    da = _mm_nt([(dx2, w_down, 0)], F32, "d_act")
    dupg, dupv, dcwg, dcwv, dcbg, dcbv = _ffn_act_bwd(up, conv_w, conv_b, da, batch, seq)
    d_wup = jnp.concatenate([_mm_tn(h2, dupg, "dw_up_gate"), _mm_tn(h2, dupv, "dw_up_val")], axis=1)
    dh2 = _mm_nt([(dupg, w_up, 0), (dupv, w_up, 1)], F32, "d_h2")
    dx1, d_ffn_norm = _rms_bwd(x1, ffn_norm, dh2, dx2, "ffn_norm_bwd")
    d_wout = _mm_tn(merged, dx1, "dw_out")
    token = comm.send_ffn_grads(d_wdown, d_wup, jnp.concatenate([dcwg, dcwv], axis=1), d_wout)
    dmerged = _mm_nt([(dx1, w_out, 0)], F32, "d_merged", dep=token)
    dzm, do, d_avg, d_avb, d_ws, d_bs = _mixer_bwd(zm, o, dmerged, av_g, av_b, w_s, w_st, b_col)
    dq, dk, dv = _attn_bwd(q, k, v, o, do, lse, batch, seq)
    dzs, cqn, dqp, ckvn, dkv, d_qn, d_kvn = _mla_prep_bwd(zs, pos, invf, q_norm, kv_norm, wuq_p, wukv, dq, dk, dv)
    d_wuq_p = _mm_tn(cqn, dqp, "dw_uq")
    d_wukv = _mm_tn(ckvn, dkv, "dw_ukv")
    d_wmain = _mm_tn(h, dzm, "dw_in_main")
    d_wzs = _mm_tn(h, dzs, "dw_in_small")
    small_early = _pack_small(dict(
        a_v_norm_g=d_avg, a_v_norm_b=d_avb, a_spatial_w=d_ws, a_spatial_b=d_bs, q_a_norm=d_qn, kv_a_norm=d_kvn,
        ffn_norm=d_ffn_norm, conv_b=jnp.concatenate([dcbg, dcbv], axis=1), final_norm=d_final), SMALL_EARLY)
    token = comm.send_in_grads(d_wmain, d_wzs, d_wuq_p, d_wukv, small_early)
    dh = _mm_nt([(dzm, w_main, 0), (dzs, w_zs, 0)], F32, "d_h", dep=token)
    dx, d_mix_norm = _rms_bwd(x, mix_norm, dh, dx1, "mix_norm_bwd")
    return loss_acc[0, 0], dx.reshape(batch, seq, D_MODEL), d_mix_norm


MESH_ID = pl.DeviceIdType.MESH
EFFECT = pltpu.SideEffectType.DATAFLOW_SIDE_EFFECTING


def _mesh_pos():
    return lax.axis_index("x"), lax.axis_index("y"), lax.axis_index("c")


def _peer(pos, d):
    x, y, c = pos
    px = 1 - x if d & 4 else x
    py = 1 - y if d & 2 else y
    pc = 1 - c if d & 1 else c
    return (px, py, pc), 4 * px + 2 * py + pc


def _copy(src_ref, land_ref, send_sems, recv_sems, a, d, pos, exchange, landing_here):
    peer, pid = _peer(pos, d)
    me = 4 * pos[0] + 2 * pos[1] + pos[2]
    if exchange:
        src, dst = src_ref.at[pid], land_ref.at[d]
    else:
        src, dst = src_ref, land_ref.at[pid if landing_here else me]
    return pltpu.make_async_remote_copy(
        src_ref=src, dst_ref=dst, send_sem=send_sems.at[a * (N_DEV - 1) + d - 1],
        recv_sem=recv_sems.at[a * (N_DEV - 1) + d - 1],
        device_id=peer, device_id_type=MESH_ID)


def _start_copies(groups, modes, name):
    sizes = [len(g) for g in groups]
    srcs = [s for g in groups for s, _ in g]
    lands = [l for g in groups for _, l in g]
    n, ng = len(srcs), len(groups)

    def body(*refs):
        src_refs, land_refs = refs[:n], refs[n:2 * n]
        sems = refs[2 * n:2 * n + 2 * ng]
        token = refs[-1]
        pos = _mesh_pos()
        k = 0
        for gi, size in enumerate(sizes):
            for a in range(size):
                for d in range(1, N_DEV):
                    _copy(src_refs[k], land_refs[k], sems[2 * gi], sems[2 * gi + 1], a, d, pos, modes[gi],
                          landing_here=False).start()
                k += 1
        token[...] = jnp.zeros_like(token)

    sem_shapes = [pltpu.SemaphoreType.DMA((size * (N_DEV - 1),)) for size in sizes for _ in range(2)]
    out = pl.pallas_call(
        body, name=name,
        out_shape=(*sem_shapes, *[pltpu.HBM(a.shape, a.dtype) for a in srcs + lands],
                   jax.ShapeDtypeStruct((8, LANES), F32)),
        in_specs=[HBM] * (2 * n),
        out_specs=(*[SEM] * (2 * ng), *[HBM] * (2 * n), pl.BlockSpec(memory_space=pltpu.VMEM)),
        input_output_aliases={i: 2 * ng + i for i in range(2 * n)},
        compiler_params=pltpu.CompilerParams(has_side_effects=EFFECT),
    )(*[pltpu.with_memory_space_constraint(a, pltpu.HBM) for a in srcs + lands])
    thru = out[2 * ng:2 * ng + 2 * n]
    handles, k = [], 0
    for gi, size in enumerate(sizes):
        handles.append((out[2 * gi], out[2 * gi + 1], thru[k:k + size], thru[n + k:n + k + size]))
        k += size
    return handles, out[-1]


def _wait_copies(handle, exchange, after, name):
    send_sems, recv_sems, srcs, lands = handle
    n = len(srcs)

    def body(*refs):
        src_refs, land_refs = refs[:n], refs[n:2 * n]
        send, recv = refs[2 * n], refs[2 * n + 1]
        pos = _mesh_pos()
        for a in range(n):
            for d in range(1, N_DEV):
                cp = _copy(src_refs[a], land_refs[a], send, recv, a, d, pos, exchange, landing_here=True)
                cp.wait_send()
                cp.wait_recv()

    out = pl.pallas_call(
        body, name=name,
        out_shape=tuple(pltpu.HBM(a.shape, a.dtype) for a in (*srcs, *lands)),
        in_specs=[HBM] * (2 * n) + [SEM, SEM, ANY], out_specs=[HBM] * (2 * n),
        input_output_aliases={i: i for i in range(2 * n)},
        compiler_params=pltpu.CompilerParams(has_side_effects=EFFECT),
    )(*srcs, *lands, send_sems, recv_sems, after)
    return out[n:]


def _land_gather(a, me):
    zone = jnp.zeros((N_DEV,) + a.shape, a.dtype)
    return lax.dynamic_update_slice(zone, a[None], (me,) + (0,) * a.ndim)


def _land_exchange(s, me):
    own = lax.dynamic_index_in_dim(s, me, 0, keepdims=True)
    return lax.dynamic_update_slice(jnp.zeros_like(s), own, (0,) * s.ndim)


def _adamw(parts, w, m, v, name):
    R, C = w.shape
    if N_DEV * R * C * parts.dtype.itemsize <= SMALL_BLOCK_BYTES:
        tr = R
    else:
        tr = next((t for t in range(min(R, 256) // 16 * 16, 15, -16) if R % t == 0), R)
    c1 = 1.0 - ADAM_B1 ** ADAM_STEP
    c2 = 1.0 - ADAM_B2 ** ADAM_STEP

    def body(p_ref, w_ref, m_ref, v_ref, g_ref, d_ref, nm_ref, nv_ref):
        g = p_ref[0].astype(F32)
        for k in range(1, N_DEV):
            g = g + p_ref[k].astype(F32)
        nm = ADAM_B1 * m_ref[...] + (1.0 - ADAM_B1) * g
        nv = ADAM_B2 * v_ref[...] + (1.0 - ADAM_B2) * (g * g)
        g_ref[...] = g
        nm_ref[...] = nm
        nv_ref[...] = nv
        d_ref[...] = -ADAM_LR * ((nm / c1) / (jnp.sqrt(nv / c2) + ADAM_EPS) + ADAM_WD * w_ref[...])

    blk = pl.BlockSpec((tr, C), lambda i: (i, 0))
    shp = jax.ShapeDtypeStruct((R, C), F32)
    return pl.pallas_call(
        body, grid=(R // tr,),
        in_specs=[pl.BlockSpec((N_DEV, tr, C), lambda i: (0, i, 0)), blk, blk, blk],
        out_specs=[blk, blk, blk, blk], out_shape=[shp, shp, shp, shp],
        name=name, compiler_params=_params("parallel"))(parts, w, m, v)


SPLIT_V = 2 * D_MODEL
SPLIT_KR = SPLIT_V + Q_LORA + KV_LORA + QK_ROPE
IN_DIM = SPLIT_KR + 2 * D_MODEL

SMALL_EARLY = ("a_v_norm_g", "a_v_norm_b", "a_spatial_w", "a_spatial_b", "q_a_norm", "kv_a_norm", "ffn_norm",
               "conv_b", "final_norm")
SMALL_LATE = ("mix_norm",)


def _pack_rows(a):
    flat = a.reshape(-1)
    rows = -(-flat.shape[0] // LANES)
    rows8 = -(-rows // 8) * 8
    return jnp.pad(flat, (0, rows8 * LANES - flat.shape[0])).reshape(rows8, LANES)


def _pack_small(tree, names, extra=None):
    parts = [_pack_rows(tree[n]) for n in names]
    if extra is not None:
        parts.append(_pack_rows(extra))
    return jnp.concatenate(parts, axis=0)


def _unpack_small(buf, names, shapes):
    out, r = {}, 0
    for n in names:
        size = math.prod(shapes[n])
        rows8 = -(-(-(-size // LANES)) // 8) * 8
        out[n] = buf[r:r + rows8].reshape(-1)[:size].reshape(shapes[n])
        r += rows8
    return out, r


def _cols_from_shards(g):
    return jnp.transpose(g, (1, 0, 2)).reshape(g.shape[1], N_DEV * g.shape[2])


def _shards_from_cols(a):
    R, W = a.shape
    return jnp.transpose(a.reshape(R, N_DEV, W // N_DEV), (1, 0, 2))


class _Comm:
    GATHER_GROUPS = (("w_in",), ("w_uq", "w_ukv", "w_out"), ("w_up", "conv_w", "w_down"))
    FFN_GRADS = ("w_down", "w_up", "conv_w", "w_out")
    IN_GRADS = ("w_in", "w_uq", "w_ukv")

    def __init__(self, shards, me):
        self.me = me
        local = {n: a.astype(F32 if n == "conv_w" else BF16) for n, a in shards.items()}
        groups = [[(local[n], _land_gather(local[n], me)) for n in g] for g in self.GATHER_GROUPS]
        (self.h_in, self.h_mla, self.h_ffn), self.token = _start_copies(groups, [False] * 3, "gather_start")

    def in_weights(self):
        (g_in,) = _wait_copies(self.h_in, False, self.token, "gather_wait_in")
        w_full = _cols_from_shards(g_in)
        w_main = jnp.concatenate([w_full[:, :SPLIT_V], w_full[:, SPLIT_KR:]], axis=1)
        w_zs = jnp.pad(w_full[:, SPLIT_V:SPLIT_KR], ((0, 0), (0, ZS_W - (SPLIT_KR - SPLIT_V))))
        return w_main, w_zs

    def mla_weights(self, after):
        g_uq, g_ukv, g_out = _wait_copies(self.h_mla, False, after, "gather_wait_mla")
        wuq_p = _cols_from_shards(jnp.pad(g_uq, ((0, 0), (0, 0), (0, HEAD_PAD - QK_HEAD))))
        return wuq_p, _cols_from_shards(g_ukv), g_out.reshape(D_MODEL, D_MODEL)

    def ffn_weights(self, after):
        g_up, g_cw, g_down = _wait_copies(self.h_ffn, False, after, "gather_wait_ffn")
        return _cols_from_shards(g_up), _cols_from_shards(g_cw), g_down.reshape(D_FF, D_MODEL)

    def _exchange_group(self, blocks):
        return [(s, _land_exchange(s, self.me)) for s in blocks]

    def send_ffn_grads(self, d_wdown, d_wup, d_convw, d_wout):
        group = self._exchange_group([
            d_wdown.reshape(N_DEV, D_FF // N_DEV, D_MODEL), _shards_from_cols(d_wup), _shards_from_cols(d_convw),
            d_wout.reshape(N_DEV, D_MODEL // N_DEV, D_MODEL)])
        (self.h_ffn_grads,), token = _start_copies([group], [True], "ffn_grads_start")
        return token

    def send_in_grads(self, d_wmain, d_wzs, d_wuq_p, d_wukv, small_early):
        d_in = jnp.concatenate([d_wmain[:, :SPLIT_V], d_wzs[:, :SPLIT_KR - SPLIT_V], d_wmain[:, SPLIT_V:]], axis=1)
        group = self._exchange_group([
            _shards_from_cols(d_in), _shards_from_cols(d_wuq_p)[:, :, :QK_HEAD], _shards_from_cols(d_wukv)])
        small = [(small_early, _land_gather(small_early, self.me))]
        (self.h_in_grads, self.h_small_early), token = _start_copies([group, small], [True, False], "in_grads_start")
        return token


def kernel(x, positions, mix_norm, w_in, a_v_norm_g, a_v_norm_b, a_spatial_w, a_spatial_b, q_a_norm, w_uq, kv_a_norm, w_ukv, w_out, ffn_norm, w_up, conv_w, conv_b, w_down, final_norm, loss_target, m_mix_norm, m_w_in, m_a_v_norm_g, m_a_v_norm_b, m_a_spatial_w, m_a_spatial_b, m_q_a_norm, m_w_uq, m_kv_a_norm, m_w_ukv, m_w_out, m_ffn_norm, m_w_up, m_conv_w, m_conv_b, m_w_down, m_final_norm, v_mix_norm, v_w_in, v_a_v_norm_g, v_a_v_norm_b, v_a_spatial_w, v_a_spatial_b, v_q_a_norm, v_w_uq, v_kv_a_norm, v_w_ukv, v_w_out, v_ffn_norm, v_w_up, v_conv_w, v_conv_b, v_w_down, v_final_norm):
    names = ("mix_norm", "w_in", "a_v_norm_g", "a_v_norm_b", "a_spatial_w", "a_spatial_b", "q_a_norm", "w_uq",
             "kv_a_norm", "w_ukv", "w_out", "ffn_norm", "w_up", "conv_w", "conv_b", "w_down", "final_norm")
    w = dict(zip(names, (mix_norm, w_in, a_v_norm_g, a_v_norm_b, a_spatial_w, a_spatial_b, q_a_norm, w_uq,
                         kv_a_norm, w_ukv, w_out, ffn_norm, w_up, conv_w, conv_b, w_down, final_norm)))
    m = dict(zip(names, (m_mix_norm, m_w_in, m_a_v_norm_g, m_a_v_norm_b, m_a_spatial_w, m_a_spatial_b,
                         m_q_a_norm, m_w_uq, m_kv_a_norm, m_w_ukv, m_w_out, m_ffn_norm, m_w_up, m_conv_w,
                         m_conv_b, m_w_down, m_final_norm)))
    v = dict(zip(names, (v_mix_norm, v_w_in, v_a_v_norm_g, v_a_v_norm_b, v_a_spatial_w, v_a_spatial_b,
                         v_q_a_norm, v_w_uq, v_kv_a_norm, v_w_ukv, v_w_out, v_ffn_norm, v_w_up, v_conv_w,
                         v_conv_b, v_w_down, v_final_norm)))
    shapes = {n: w[n].shape for n in names}
    me = 4 * lax.axis_index("x") + 2 * lax.axis_index("y") + lax.axis_index("c")
    comm = _Comm({n: w[n][0] for n in _Comm.GATHER_GROUPS[0] + _Comm.GATHER_GROUPS[1] + _Comm.GATHER_GROUPS[2]}, me)

    loss_part, grad_x, d_mix_norm = _local_step(
        x, positions, loss_target, w["mix_norm"], w["a_v_norm_g"], w["a_v_norm_b"], w["a_spatial_w"][0],
        w["a_spatial_b"][0], w["q_a_norm"], w["kv_a_norm"], w["ffn_norm"], w["conv_b"],
        w["final_norm"].reshape(1, D_MODEL), comm)

    late_local = _pack_small({"mix_norm": d_mix_norm}, SMALL_LATE, extra=loss_part.reshape(1))
    (h_late,), token = _start_copies([[(late_local, _land_gather(late_local, me))]], [False], "late_grads_start")

    out_g, out_d, out_m, out_v = {}, {}, {}, {}

    def update(n, parts):
        shp = shapes[n]
        r2 = (shp[-2], shp[-1])
        res = _adamw(parts, w[n].reshape(r2), m[n].reshape(r2), v[n].reshape(r2), "adamw_" + n)
        out_g[n], out_d[n], out_m[n], out_v[n] = (t.reshape(shp) for t in res)

    for n, parts in zip(_Comm.FFN_GRADS, _wait_copies(comm.h_ffn_grads, True, token, "ffn_grads_wait")):
        update(n, parts)
    for n, parts in zip(_Comm.IN_GRADS, _wait_copies(comm.h_in_grads, True, out_d["w_up"], "in_grads_wait")):
        update(n, parts)

    (early_parts,) = _wait_copies(comm.h_small_early, False, out_d["w_in"], "small_grads_wait")
    res = _adamw(early_parts, _pack_small(w, SMALL_EARLY), _pack_small(m, SMALL_EARLY), _pack_small(v, SMALL_EARLY),
                 "adamw_small")
    unpacked = [_unpack_small(t, SMALL_EARLY, shapes)[0] for t in res]
    for n in SMALL_EARLY:
        out_g[n], out_d[n], out_m[n], out_v[n] = (u[n] for u in unpacked)

    (late_parts,) = _wait_copies(h_late, False, res[1], "late_grads_wait")
    zero = jnp.zeros((1,), F32)
    res = _adamw(late_parts, _pack_small(w, SMALL_LATE, extra=zero), _pack_small(m, SMALL_LATE, extra=zero),
                 _pack_small(v, SMALL_LATE, extra=zero), "adamw_late")
    unpacked = [_unpack_small(t, SMALL_LATE, shapes) for t in res]
    for n in SMALL_LATE:
        out_g[n], out_d[n], out_m[n], out_v[n] = (u[0][n] for u in unpacked)
    loss = res[0][unpacked[0][1], 0]

    return (loss, grad_x, *[out_g[n] for n in names], *[out_d[n] for n in names],
            *[out_m[n] for n in names], *[out_v[n] for n in names])
```

```python
import functools
import math

import jax
import jax.numpy as jnp
from jax import lax
from jax.experimental import pallas as pl
from jax.experimental.pallas import tpu as pltpu

F32 = jnp.float32
BF16 = jnp.bfloat16

N_DEV = 8
D_MODEL = 1024
EPS = 1e-6
A_GROUPS = 8
CHUNK = 128
MLA_HEADS = 8
QK_NOPE = 128
QK_ROPE = 64
QK_HEAD = QK_NOPE + QK_ROPE
HEAD_PAD = 256
V_HEAD = 128
Q_LORA = 256
KV_LORA = 128
ROPE_THETA = 10000.0
D_FF = 2816
ZS_W = 512
ATTN_SCALE = QK_HEAD ** -0.5
NEG_BIG = -1e30

ADAM_LR = 0.001
ADAM_B1 = 0.9
ADAM_B2 = 0.999
ADAM_EPS = 1e-08
ADAM_WD = 0.01
ADAM_STEP = 10

VMEM_LIMIT = 56 * 1024 * 1024
SMALL_BLOCK_BYTES = 5 * 1024 * 1024
LANES = 128

GELU_K = math.sqrt(2.0 / math.pi)
GELU_C = 0.044715

ANY = pl.BlockSpec(memory_space=pl.ANY)
HBM = pl.BlockSpec(memory_space=pltpu.HBM)
SEM = pl.BlockSpec(memory_space=pltpu.SEMAPHORE)


def _tile(n, pref):
    for t in (pref, 512, 256, 128, 64, 32, 16, 8):
        if t <= pref and n % t == 0:
            return t
    return n


def _wide_tile(n, cap=1408):
    return next((t for t in range(min(n, cap) // LANES * LANES, 0, -LANES) if n % t == 0), n)


def _params(*sem):
    return pltpu.CompilerParams(dimension_semantics=sem, vmem_limit_bytes=VMEM_LIMIT)


def _dot(a, b):
    return jnp.dot(a, b, preferred_element_type=F32)


def _dot_nt(a, b):
    return lax.dot_general(a, b, (((1,), (1,)), ((), ())), preferred_element_type=F32)


def _dot_tn(a, b):
    return lax.dot_general(a, b, (((0,), (0,)), ((), ())), preferred_element_type=F32)


def _sigmoid(x):
    return 1.0 / (1.0 + jnp.exp(-x))


def _gelu(x):
    t = jnp.tanh(GELU_K * (x + GELU_C * x * x * x))
    return 0.5 * x * (1.0 + t)


def _gelu_grad(x):
    t = jnp.tanh(GELU_K * (x + GELU_C * x * x * x))
    return 0.5 * (1.0 + t) + 0.5 * x * (1.0 - t * t) * GELU_K * (1.0 + 3.0 * GELU_C * x * x)


def _norm_mm(x, g, ws, name):
    T, Dm = x.shape
    tm = _tile(T, 256)
    n_w = len(ws)

    def body(x_ref, g_ref, *refs):
        w_refs, h_ref, z_refs = refs[:n_w], refs[n_w], refs[n_w + 1:]
        xf = x_ref[...]
        r = lax.rsqrt(jnp.mean(xf * xf, axis=-1, keepdims=True) + EPS)
        h = (xf * r * g_ref[...]).astype(BF16)
        h_ref[...] = h
        for w_ref, z_ref in zip(w_refs, z_refs):
            z_ref[...] = _dot(h, w_ref[...])

    row = lambda n: pl.BlockSpec((tm, n), lambda i: (i, 0))
    return pl.pallas_call(
        body, grid=(T // tm,),
        in_specs=[row(Dm), pl.BlockSpec((1, Dm), lambda i: (0, 0))]
        + [pl.BlockSpec(w.shape, lambda i: (0, 0)) for w in ws],
        out_specs=[row(Dm)] + [row(w.shape[1]) for w in ws],
        out_shape=[jax.ShapeDtypeStruct((T, Dm), BF16)]
        + [jax.ShapeDtypeStruct((T, w.shape[1]), F32) for w in ws],
        name=name, compiler_params=_params("parallel"))(x, g, *ws)


def _proj_bwd(pairs, x, g, dres, name, w2=None, dep=None):
    T, Dm = x.shape
    tm = _tile(T, 256)
    n_pairs = len(pairs)

    def body(*refs):
        ins, outs = refs[:2 * n_pairs + 3 + (w2 is not None) + (dep is not None)], refs[-2 - (w2 is not None):]
        x_ref, g_ref, dres_ref = ins[2 * n_pairs:2 * n_pairs + 3]
        dx_ref, dg_ref = outs[0], outs[1]

        @pl.when(pl.program_id(0) == 0)
        def _():
            dg_ref[...] = jnp.zeros_like(dg_ref)

        dy = None
        for p in range(n_pairs):
            t = _dot_nt(ins[2 * p][...], ins[2 * p + 1][...])
            dy = t if dy is None else dy + t
        xf = x_ref[...]
        r = lax.rsqrt(jnp.mean(xf * xf, axis=-1, keepdims=True) + EPS)
        xh = xf * r
        dg_ref[...] += jnp.sum(dy * xh, axis=0, keepdims=True)
        dxh = dy * g_ref[...]
        dx = dres_ref[...] + r * (dxh - xh * jnp.mean(dxh * xh, axis=-1, keepdims=True))
        dx_ref[...] = dx
        if w2 is not None:
            outs[2][...] = _dot_nt(dx.astype(BF16), ins[2 * n_pairs + 3][...])

    row = pl.BlockSpec((tm, Dm), lambda i: (i, 0))
    vec = pl.BlockSpec((1, Dm), lambda i: (0, 0))
    in_specs, args = [], []
    for a, w, c in pairs:
        in_specs += [pl.BlockSpec((tm, a.shape[1]), lambda i: (i, 0)),
                     pl.BlockSpec((Dm, a.shape[1]), functools.partial(lambda i, c: (0, c), c=c))]
        args += [a, w]
    in_specs += [row, vec, row]
    args += [x, g, dres]
    out_specs = [row, vec]
    out_shape = [jax.ShapeDtypeStruct((T, Dm), F32), jax.ShapeDtypeStruct((1, Dm), F32)]
    if w2 is not None:
        in_specs.append(pl.BlockSpec(w2.shape, lambda i: (0, 0)))
        args.append(w2)
        out_specs.append(pl.BlockSpec((tm, w2.shape[0]), lambda i: (i, 0)))
        out_shape.append(jax.ShapeDtypeStruct((T, w2.shape[0]), F32))
    if dep is not None:
        in_specs.append(ANY)
        args.append(dep)
    return pl.pallas_call(
        body, grid=(T // tm,), in_specs=in_specs, out_specs=out_specs, out_shape=out_shape,
        name=name, compiler_params=_params("arbitrary"))(*args)


def _mm_tn(a, b, name):
    T, M = a.shape
    N = b.shape[1]
    tm, tn, tt = _wide_tile(M), _wide_tile(N), _tile(T, 512)
    n_t = T // tt

    def body(a_ref, b_ref, o_ref, acc_ref):
        t = pl.program_id(2)

        @pl.when(t == 0)
        def _():
            acc_ref[...] = jnp.zeros_like(acc_ref)

        acc_ref[...] += _dot_tn(a_ref[...].astype(BF16), b_ref[...].astype(BF16))

        @pl.when(t == n_t - 1)
        def _():
            o_ref[...] = acc_ref[...].astype(BF16)

    return pl.pallas_call(
        body, grid=(M // tm, N // tn, n_t),
        in_specs=[pl.BlockSpec((tt, tm), lambda i, j, t: (t, i)),
                  pl.BlockSpec((tt, tn), lambda i, j, t: (t, j))],
        out_specs=pl.BlockSpec((tm, tn), lambda i, j, t: (i, j)),
        out_shape=jax.ShapeDtypeStruct((M, N), BF16),
        scratch_shapes=[pltpu.VMEM((tm, tn), F32)],
        name=name, compiler_params=_params("parallel", "parallel", "arbitrary"))(a, b)


def _layer_norm_fwd(gv, g, b):
    mu = jnp.mean(gv, axis=-1, keepdims=True)
    xc = gv - mu
    rs = lax.rsqrt(jnp.mean(xc * xc, axis=-1, keepdims=True) + EPS)
    xh = xc * rs
    return xh, rs, xh * g + b


def _tri_mask(transposed=False):
    r = lax.broadcasted_iota(jnp.int32, (CHUNK, CHUNK), 0)
    c = lax.broadcasted_iota(jnp.int32, (CHUNK, CHUNK), 1)
    return r <= c if transposed else c <= r


def _mixer_a_fwd(zm, av_g, av_b, w_s, b_col):
    T = zm.shape[0]
    tm = _tile(T, 256)
    n_chunk = tm // CHUNK

    def body(u_ref, v_ref, ga_ref, g_ref, b_ref, w_ref, bc_ref, y_ref, vn_s, mx_s):
        gu = _gelu(u_ref[...])
        _, _, vn = _layer_norm_fwd(_gelu(v_ref[...]), g_ref[...], b_ref[...])
        vn_s[...] = vn.astype(BF16)
        tri = _tri_mask()
        for gi in range(A_GROUPS):
            wm = jnp.where(tri, w_ref[gi], 0.0).astype(BF16)
            cols = slice(gi * CHUNK, (gi + 1) * CHUNK)
            for n in range(n_chunk):
                rows = slice(n * CHUNK, (n + 1) * CHUNK)
                mx_s[rows, cols] = _dot(wm, vn_s[rows, cols]) + bc_ref[gi]
        y_ref[...] = _sigmoid(ga_ref[...]) * gu * mx_s[...]

    col = lambda c: pl.BlockSpec((tm, D_MODEL), lambda i: (i, c))
    vec = pl.BlockSpec((1, D_MODEL), lambda i: (0, 0))
    return pl.pallas_call(
        body, grid=(T // tm,),
        in_specs=[col(0), col(1), col(2), vec, vec,
                  pl.BlockSpec((A_GROUPS, CHUNK, CHUNK), lambda i: (0, 0, 0)),
                  pl.BlockSpec((A_GROUPS, CHUNK, 1), lambda i: (0, 0, 0))],
        out_specs=pl.BlockSpec((tm, D_MODEL), lambda i: (i, 0)),
        out_shape=jax.ShapeDtypeStruct((T, D_MODEL), F32),
        scratch_shapes=[pltpu.VMEM((tm, D_MODEL), BF16), pltpu.VMEM((tm, D_MODEL), F32)],
        name="mixer_a_fwd", compiler_params=_params("parallel"))(zm, zm, zm, av_g, av_b, w_s, b_col)


def _mixer_bwd(zm, o, dm, av_g, av_b, w_s, w_st, b_col, dep):
    T = zm.shape[0]
    tm = _tile(T, 256)
    n_chunk = tm // CHUNK

    def body(u_ref, v_ref, ga_ref, gb_ref, o_ref, dm_ref, g_ref, b_ref, w_ref, wt_ref, bc_ref, dep_ref,
             dz_ref, do_ref, dg_ref, db_ref, dw_ref, dbs_ref, vn_s, mx_s, dmx_s, dvn_s):
        @pl.when(pl.program_id(0) == 0)
        def _():
            dg_ref[...] = jnp.zeros_like(dg_ref)
            db_ref[...] = jnp.zeros_like(db_ref)
            dw_ref[...] = jnp.zeros_like(dw_ref)
            dbs_ref[...] = jnp.zeros_like(dbs_ref)

        dm_v = dm_ref[...]
        gb = gb_ref[...]
        sb = _sigmoid(gb)
        o_v = o_ref[...]
        do_ref[...] = (dm_v * sb).astype(BF16)
        dz_ref[:, 3 * D_MODEL:4 * D_MODEL] = (dm_v * o_v * sb * (1.0 - sb)).astype(BF16)
        u = u_ref[...]
        v = v_ref[...]
        gu = _gelu(u)
        xh, rs, vn = _layer_norm_fwd(_gelu(v), g_ref[...], b_ref[...])
        vn_s[...] = vn.astype(BF16)
        tri = _tri_mask()
        for gi in range(A_GROUPS):
            wm = jnp.where(tri, w_ref[gi], 0.0).astype(BF16)
            cols = slice(gi * CHUNK, (gi + 1) * CHUNK)
            for n in range(n_chunk):
                rows = slice(n * CHUNK, (n + 1) * CHUNK)
                mx_s[rows, cols] = _dot(wm, vn_s[rows, cols]) + bc_ref[gi]
        mixed = mx_s[...]
        sa = _sigmoid(ga_ref[...])
        dya = dm_v * sa
        dz_ref[:, 2 * D_MODEL:3 * D_MODEL] = (dm_v * gu * mixed * sa * (1.0 - sa)).astype(BF16)
        dz_ref[:, 0:D_MODEL] = (dya * mixed * _gelu_grad(u)).astype(BF16)
        dmx = dya * gu
        dmx_s[...] = dmx.astype(BF16)
        tri_t = _tri_mask(transposed=True)
        for gi in range(A_GROUPS):
            wmt = jnp.where(tri_t, wt_ref[gi], 0.0).astype(BF16)
            cols = slice(gi * CHUNK, (gi + 1) * CHUNK)
            dw_acc = jnp.zeros((CHUNK, CHUNK), F32)
            dmx_sum = jnp.zeros((CHUNK, CHUNK), F32)
            for n in range(n_chunk):
                rows = slice(n * CHUNK, (n + 1) * CHUNK)
                blk = dmx_s[rows, cols]
                dvn_s[rows, cols] = _dot(wmt, blk)
                dw_acc = dw_acc + _dot_nt(blk, vn_s[rows, cols])
                dmx_sum = dmx_sum + dmx[rows, cols]
            dw_ref[gi] += jnp.where(tri, dw_acc, 0.0)
            dbs_ref[gi] += jnp.sum(dmx_sum, axis=-1, keepdims=True)
        dvn = dvn_s[...]
        dg_ref[...] += jnp.sum(dvn * xh, axis=0, keepdims=True)
        db_ref[...] += jnp.sum(dvn, axis=0, keepdims=True)
        dxh = dvn * g_ref[...]
        dgv = rs * (dxh - jnp.mean(dxh, axis=-1, keepdims=True)
                    - xh * jnp.mean(dxh * xh, axis=-1, keepdims=True))
        dz_ref[:, D_MODEL:2 * D_MODEL] = (dgv * _gelu_grad(v)).astype(BF16)

    col = lambda c: pl.BlockSpec((tm, D_MODEL), lambda i: (i, c))
    row = pl.BlockSpec((tm, D_MODEL), lambda i: (i, 0))
    vec = pl.BlockSpec((1, D_MODEL), lambda i: (0, 0))
    wsp = pl.BlockSpec((A_GROUPS, CHUNK, CHUNK), lambda i: (0, 0, 0))
    bsp = pl.BlockSpec((A_GROUPS, CHUNK, 1), lambda i: (0, 0, 0))
    return pl.pallas_call(
        body, grid=(T // tm,),
        in_specs=[col(0), col(1), col(2), col(3), row, row, vec, vec, wsp, wsp, bsp, ANY],
        out_specs=[pl.BlockSpec((tm, 4 * D_MODEL), lambda i: (i, 0)), row, vec, vec, wsp, bsp],
        out_shape=[jax.ShapeDtypeStruct((T, 4 * D_MODEL), BF16), jax.ShapeDtypeStruct((T, D_MODEL), BF16),
                   jax.ShapeDtypeStruct((1, D_MODEL), F32), jax.ShapeDtypeStruct((1, D_MODEL), F32),
                   jax.ShapeDtypeStruct((A_GROUPS, CHUNK, CHUNK), F32),
                   jax.ShapeDtypeStruct((A_GROUPS, CHUNK, 1), F32)],
        scratch_shapes=[pltpu.VMEM((tm, D_MODEL), BF16), pltpu.VMEM((tm, D_MODEL), F32),
                        pltpu.VMEM((tm, D_MODEL), BF16), pltpu.VMEM((tm, D_MODEL), F32)],
        name="mixer_bwd", compiler_params=_params("arbitrary"))(
            zm, zm, zm, zm, o, dm, av_g, av_b, w_s, w_st, b_col, dep)


def _rope_tables(pos_ref, invf_ref):
    ang = pos_ref[...].astype(F32) * invf_ref[...]
    lane = lax.broadcasted_iota(jnp.int32, ang.shape, 1)
    cos, sin = jnp.cos(ang), jnp.sin(ang)
    c = jnp.where(lane < QK_ROPE, cos, 0.0)
    sa = jnp.where(lane < QK_ROPE // 2, -sin, 0.0)
    sb = jnp.where((lane >= QK_ROPE // 2) & (lane < QK_ROPE), sin, 0.0)
    return c, sa, sb


def _rope(blk, tabs):
    c, sa, sb = tabs
    return blk * c + pltpu.roll(blk, LANES - QK_ROPE // 2, 1) * sa + pltpu.roll(blk, QK_ROPE // 2, 1) * sb


def _rope_t(dout, tabs):
    c, sa, sb = tabs
    return dout * c + pltpu.roll(dout * sa, QK_ROPE // 2, 1) + pltpu.roll(dout * sb, LANES - QK_ROPE // 2, 1)


def _rms_small(x, g):
    r = lax.rsqrt(jnp.mean(x * x, axis=-1, keepdims=True) + EPS)
    xh = x * r
    return xh, r, xh * g


def _mla_prep_fwd(zs, pos, invf, qg, kvg, wuq_p, wukv):
    T = zs.shape[0]
    tm = _tile(T, 512)
    HW = MLA_HEADS * HEAD_PAD

    def body(zs_ref, pos_ref, invf_ref, qg_ref, kvg_ref, wq_ref, wkv_ref, q_ref, k_ref, v_ref):
        tabs = _rope_tables(pos_ref, invf_ref)
        _, _, cqn = _rms_small(zs_ref[:, 0:Q_LORA], qg_ref[...])
        _, _, ckvn = _rms_small(zs_ref[:, Q_LORA:Q_LORA + KV_LORA], kvg_ref[...])
        q = _dot(cqn.astype(BF16), wq_ref[...])
        kv = _dot(ckvn.astype(BF16), wkv_ref[...])
        kr = _rope(zs_ref[:, Q_LORA + KV_LORA:ZS_W], tabs).astype(BF16)
        for h in range(MLA_HEADS):
            b0 = h * HEAD_PAD
            q_ref[:, b0:b0 + QK_NOPE] = q[:, b0:b0 + QK_NOPE].astype(BF16)
            q_ref[:, b0 + QK_NOPE:b0 + HEAD_PAD] = _rope(q[:, b0 + QK_NOPE:b0 + HEAD_PAD], tabs).astype(BF16)
            k_ref[:, b0:b0 + QK_NOPE] = kv[:, b0:b0 + QK_NOPE].astype(BF16)
            k_ref[:, b0 + QK_NOPE:b0 + HEAD_PAD] = kr
            v_ref[:, h * V_HEAD:(h + 1) * V_HEAD] = kv[:, b0 + QK_NOPE:b0 + HEAD_PAD].astype(BF16)

    full = lambda a: pl.BlockSpec(a.shape, lambda i: (0,) * a.ndim)
    return pl.pallas_call(
        body, grid=(T // tm,),
        in_specs=[pl.BlockSpec((tm, ZS_W), lambda i: (i, 0)), pl.BlockSpec((tm, 1), lambda i: (i, 0)),
                  full(invf), full(qg), full(kvg), full(wuq_p), full(wukv)],
        out_specs=[pl.BlockSpec((tm, HW), lambda i: (i, 0)), pl.BlockSpec((tm, HW), lambda i: (i, 0)),
                   pl.BlockSpec((tm, D_MODEL), lambda i: (i, 0))],
        out_shape=[jax.ShapeDtypeStruct((T, HW), BF16), jax.ShapeDtypeStruct((T, HW), BF16),
                   jax.ShapeDtypeStruct((T, D_MODEL), BF16)],
        name="mla_prep_fwd", compiler_params=_params("parallel"))(zs, pos, invf, qg, kvg, wuq_p, wukv)


def _mla_prep_bwd(zs, pos, invf, qg, kvg, wuq_p, wukv, dq, dk, dv):
    T = zs.shape[0]
    tm = _tile(T, 256)
    HW = MLA_HEADS * HEAD_PAD

    def body(zs_ref, pos_ref, invf_ref, qg_ref, kvg_ref, wq_ref, wkv_ref, dq_ref, dk_ref, dv_ref,
             dzs_ref, cqn_ref, dqp_ref, ckvn_ref, dkv_ref, dqg_ref, dkvg_ref):
        @pl.when(pl.program_id(0) == 0)
        def _():
            dqg_ref[...] = jnp.zeros_like(dqg_ref)
            dkvg_ref[...] = jnp.zeros_like(dkvg_ref)

        tabs = _rope_tables(pos_ref, invf_ref)
        cqh, rq, cqn = _rms_small(zs_ref[:, 0:Q_LORA], qg_ref[...])
        ckvh, rkv, ckvn = _rms_small(zs_ref[:, Q_LORA:Q_LORA + KV_LORA], kvg_ref[...])
        cqn_ref[...] = cqn.astype(BF16)
        ckvn_ref[...] = ckvn.astype(BF16)
        dkr = jnp.zeros((tm, LANES), F32)
        for h in range(MLA_HEADS):
            b0 = h * HEAD_PAD
            dqp_ref[:, b0:b0 + QK_NOPE] = dq_ref[:, b0:b0 + QK_NOPE].astype(BF16)
            dqp_ref[:, b0 + QK_NOPE:b0 + HEAD_PAD] = _rope_t(dq_ref[:, b0 + QK_NOPE:b0 + HEAD_PAD], tabs).astype(BF16)
            dkv_ref[:, b0:b0 + QK_NOPE] = dk_ref[:, b0:b0 + QK_NOPE].astype(BF16)
            dkv_ref[:, b0 + QK_NOPE:b0 + HEAD_PAD] = dv_ref[:, h * V_HEAD:(h + 1) * V_HEAD].astype(BF16)
            dkr = dkr + dk_ref[:, b0 + QK_NOPE:b0 + HEAD_PAD]
        dcqn = _dot_nt(dqp_ref[...], wq_ref[...])
        dckvn = _dot_nt(dkv_ref[...], wkv_ref[...])
        dqg_ref[...] += jnp.sum(dcqn * cqh, axis=0, keepdims=True)
        dkvg_ref[...] += jnp.sum(dckvn * ckvh, axis=0, keepdims=True)
        dxh = dcqn * qg_ref[...]
        dzs_ref[:, 0:Q_LORA] = (rq * (dxh - cqh * jnp.mean(dxh * cqh, axis=-1, keepdims=True))).astype(BF16)
        dxh = dckvn * kvg_ref[...]
        dzs_ref[:, Q_LORA:Q_LORA + KV_LORA] = (
            rkv * (dxh - ckvh * jnp.mean(dxh * ckvh, axis=-1, keepdims=True))).astype(BF16)
        dzs_ref[:, Q_LORA + KV_LORA:ZS_W] = _rope_t(dkr, tabs).astype(BF16)

    full = lambda a: pl.BlockSpec(a.shape, lambda i: (0,) * a.ndim)
    rowb = lambda w: pl.BlockSpec((tm, w), lambda i: (i, 0))
    return pl.pallas_call(
        body, grid=(T // tm,),
        in_specs=[rowb(ZS_W), rowb(1), full(invf), full(qg), full(kvg), full(wuq_p), full(wukv),
                  rowb(HW), rowb(HW), rowb(D_MODEL)],
        out_specs=[rowb(ZS_W), rowb(Q_LORA), rowb(HW), rowb(KV_LORA), rowb(HW), full(qg), full(kvg)],
        out_shape=[jax.ShapeDtypeStruct((T, ZS_W), BF16), jax.ShapeDtypeStruct((T, Q_LORA), BF16),
                   jax.ShapeDtypeStruct((T, HW), BF16), jax.ShapeDtypeStruct((T, KV_LORA), BF16),
                   jax.ShapeDtypeStruct((T, HW), BF16), jax.ShapeDtypeStruct(qg.shape, F32),
                   jax.ShapeDtypeStruct(kvg.shape, F32)],
        name="mla_prep_bwd", compiler_params=_params("arbitrary"))(
            zs, pos, invf, qg, kvg, wuq_p, wukv, dq, dk, dv)


def _causal(tq, kmax, q0):
    r = lax.broadcasted_iota(jnp.int32, (tq, kmax), 0) + q0
    c = lax.broadcasted_iota(jnp.int32, (tq, kmax), 1)
    return c <= r


def _attn_fwd(q, k, v, batch, seq):
    tq = _tile(seq, 512)
    nq = seq // tq

    def body(q_ref, k_ref, v_ref, o_ref, lse_ref):
        for qi in range(nq):
            rows = slice(qi * tq, (qi + 1) * tq)
            kmax = (qi + 1) * tq
            s = _dot_nt(q_ref[rows, :], k_ref[0:kmax, :]) * ATTN_SCALE
            s = jnp.where(_causal(tq, kmax, qi * tq), s, NEG_BIG)
            m = jnp.max(s, axis=-1, keepdims=True)
            p = jnp.exp(s - m)
            l = jnp.sum(p, axis=-1, keepdims=True)
            o_ref[rows, :] = _dot(p.astype(BF16), v_ref[0:kmax, :]) / l
            lse_ref[rows, :] = jnp.broadcast_to(m + jnp.log(l), (tq, V_HEAD))

    return pl.pallas_call(
        body, grid=(batch, MLA_HEADS),
        in_specs=[pl.BlockSpec((seq, HEAD_PAD), lambda b, h: (b, h)),
                  pl.BlockSpec((seq, HEAD_PAD), lambda b, h: (b, h)),
                  pl.BlockSpec((seq, V_HEAD), lambda b, h: (b, h))],
        out_specs=[pl.BlockSpec((seq, V_HEAD), lambda b, h: (b, h)),
                   pl.BlockSpec((seq, V_HEAD), lambda b, h: (b, h))],
        out_shape=[jax.ShapeDtypeStruct((batch * seq, D_MODEL), F32),
                   jax.ShapeDtypeStruct((batch * seq, D_MODEL), F32)],
        name="attn_fwd", compiler_params=_params("parallel", "parallel"))(q, k, v)


def _attn_bwd(q, k, v, o, do, lse, batch, seq, dep):
    tq = _tile(seq, 512)
    nq = seq // tq

    def body(q_ref, k_ref, v_ref, o_ref, do_ref, lse_ref, dep_ref, dq_ref, dk_ref, dv_ref):
        dk_ref[...] = jnp.zeros_like(dk_ref)
        dv_ref[...] = jnp.zeros_like(dv_ref)
        for qi in range(nq):
            rows = slice(qi * tq, (qi + 1) * tq)
            kmax = (qi + 1) * tq
            qr = q_ref[rows, :]
            dor = do_ref[rows, :]
            kk = k_ref[0:kmax, :]
            s = _dot_nt(qr, kk) * ATTN_SCALE
            p = jnp.where(_causal(tq, kmax, qi * tq), jnp.exp(s - lse_ref[rows, 0:1]), 0.0)
            dp = _dot_nt(dor, v_ref[0:kmax, :])
            delta = jnp.sum(dor.astype(F32) * o_ref[rows, :], axis=-1, keepdims=True)
            ds = (p * (dp - delta) * ATTN_SCALE).astype(BF16)
            dq_ref[rows, :] = _dot(ds, kk)
            dk_ref[0:kmax, :] += _dot_tn(ds, qr)
            dv_ref[0:kmax, :] += _dot_tn(p.astype(BF16), dor)

    qspec = pl.BlockSpec((seq, HEAD_PAD), lambda b, h: (b, h))
    vspec = pl.BlockSpec((seq, V_HEAD), lambda b, h: (b, h))
    T = batch * seq
    return pl.pallas_call(
        body, grid=(batch, MLA_HEADS),
        in_specs=[qspec, qspec, vspec, vspec, vspec, vspec, ANY],
        out_specs=[qspec, qspec, vspec],
        out_shape=[jax.ShapeDtypeStruct((T, MLA_HEADS * HEAD_PAD), F32),
                   jax.ShapeDtypeStruct((T, MLA_HEADS * HEAD_PAD), F32),
                   jax.ShapeDtypeStruct((T, D_MODEL), F32)],
        name="attn_bwd", compiler_params=_params("parallel", "parallel"))(q, k, v, o, do, lse, dep)


def _merge_out(x, yag, zm, o, w_out, ffn_g):
    T = x.shape[0]
    tm = _tile(T, 512)

    def body(x_ref, ya_ref, gb_ref, o_ref, w_ref, g_ref, mg_ref, x1_ref, h2_ref):
        mg = (ya_ref[...] + _sigmoid(gb_ref[...]) * o_ref[...]).astype(BF16)
        mg_ref[...] = mg
        x1 = x_ref[...] + _dot(mg, w_ref[...])
        x1_ref[...] = x1
        r = lax.rsqrt(jnp.mean(x1 * x1, axis=-1, keepdims=True) + EPS)
        h2_ref[...] = (x1 * r * g_ref[...]).astype(BF16)

    row = pl.BlockSpec((tm, D_MODEL), lambda i: (i, 0))
    return pl.pallas_call(
        body, grid=(T // tm,),
        in_specs=[row, row, pl.BlockSpec((tm, D_MODEL), lambda i: (i, 3)), row,
                  pl.BlockSpec((D_MODEL, D_MODEL), lambda i: (0, 0)), pl.BlockSpec((1, D_MODEL), lambda i: (0, 0))],
        out_specs=[row, row, row],
        out_shape=[jax.ShapeDtypeStruct((T, D_MODEL), BF16), jax.ShapeDtypeStruct((T, D_MODEL), F32),
                   jax.ShapeDtypeStruct((T, D_MODEL), BF16)],
        name="merge_out", compiler_params=_params("parallel"))(x, yag, zm, o, w_out, ffn_g)


FF_TILE = 256
FF_BLOCKS = D_FF // FF_TILE


def _shift_down(x, k):
    row = lax.broadcasted_iota(jnp.int32, x.shape, 0)
    return jnp.where(row >= k, pltpu.roll(x, k, 0), 0.0)


def _shift_up(x, k):
    n = x.shape[0]
    row = lax.broadcasted_iota(jnp.int32, x.shape, 0)
    return jnp.where(row < n - k, pltpu.roll(x, n - k, 0), 0.0)


def _conv(x, w_ref, b_ref):
    return b_ref[...] + w_ref[2:3, :] * x + w_ref[1:2, :] * _shift_down(x, 1) + w_ref[0:1, :] * _shift_down(x, 2)


def _up_act(h2, w_up, cw, cb, batch, seq):
    def body(h_ref, wug_ref, wuv_ref, wg_ref, wv_ref, bg_ref, bv_ref, ug_ref, uv_ref, a_ref):
        h = h_ref[...]
        ug = _dot(h, wug_ref[...])
        uv = _dot(h, wuv_ref[...])
        ug_ref[...] = ug
        uv_ref[...] = uv
        gate = _conv(ug, wg_ref, bg_ref)
        val = _conv(uv, wv_ref, bv_ref)
        a_ref[...] = (gate * _sigmoid(gate) * val).astype(BF16)

    blk = pl.BlockSpec((seq, FF_TILE), lambda b, j: (b, j))
    wup = lambda off: pl.BlockSpec((D_MODEL, FF_TILE), lambda b, j: (0, j + off))
    wsp = lambda off: pl.BlockSpec((3, FF_TILE), lambda b, j: (0, j + off))
    bsp = lambda off: pl.BlockSpec((1, FF_TILE), lambda b, j: (0, j + off))
    T = batch * seq
    return pl.pallas_call(
        body, grid=(batch, FF_BLOCKS),
        in_specs=[pl.BlockSpec((seq, D_MODEL), lambda b, j: (b, 0)), wup(0), wup(FF_BLOCKS),
                  wsp(0), wsp(FF_BLOCKS), bsp(0), bsp(FF_BLOCKS)],
        out_specs=[blk, blk, blk],
        out_shape=[jax.ShapeDtypeStruct((T, D_FF), F32), jax.ShapeDtypeStruct((T, D_FF), F32),
                   jax.ShapeDtypeStruct((T, D_FF), BF16)],
        name="up_act", compiler_params=_params("parallel", "arbitrary"))(h2, w_up, w_up, cw, cw, cb, cb)


def _ffn_act_bwd(upg, upv, cw, cb, dx2b, w_down, batch, seq):
    def half(du, x, w_ref, dx_ref, dw_ref, db_ref):
        dx_ref[...] = (w_ref[2:3, :] * du + w_ref[1:2, :] * _shift_up(du, 1)
                       + w_ref[0:1, :] * _shift_up(du, 2)).astype(BF16)
        dw_ref[2:3, :] += jnp.sum(du * x, axis=0, keepdims=True)
        dw_ref[1:2, :] += jnp.sum(du * _shift_down(x, 1), axis=0, keepdims=True)
        dw_ref[0:1, :] += jnp.sum(du * _shift_down(x, 2), axis=0, keepdims=True)
        db_ref[...] += jnp.sum(du, axis=0, keepdims=True)

    def body(ug_ref, uv_ref, wg_ref, wv_ref, bg_ref, bv_ref, dx_ref, wd_ref,
             dg_ref, dv_ref, dwg_ref, dwv_ref, dbg_ref, dbv_ref):
        @pl.when(pl.program_id(1) == 0)
        def _():
            for r in (dwg_ref, dwv_ref, dbg_ref, dbv_ref):
                r[...] = jnp.zeros_like(r)

        ug, uv = ug_ref[...], uv_ref[...]
        gate = _conv(ug, wg_ref, bg_ref)
        val = _conv(uv, wv_ref, bv_ref)
        sg = _sigmoid(gate)
        dav = _dot_nt(dx_ref[...], wd_ref[...])
        half(dav * val * sg * (1.0 + gate * (1.0 - sg)), ug, wg_ref, dg_ref, dwg_ref, dbg_ref)
        half(dav * gate * sg, uv, wv_ref, dv_ref, dwv_ref, dbv_ref)

    blk = pl.BlockSpec((seq, FF_TILE), lambda j, b: (b, j))
    wsp = lambda off: pl.BlockSpec((3, FF_TILE), lambda j, b: (0, j + off))
    bsp = lambda off: pl.BlockSpec((1, FF_TILE), lambda j, b: (0, j + off))
    T = batch * seq
    return pl.pallas_call(
        body, grid=(FF_BLOCKS, batch),
        in_specs=[blk, blk, wsp(0), wsp(FF_BLOCKS), bsp(0), bsp(FF_BLOCKS),
                  pl.BlockSpec((seq, D_MODEL), lambda j, b: (b, 0)),
                  pl.BlockSpec((FF_TILE, D_MODEL), lambda j, b: (j, 0))],
        out_specs=[blk, blk, wsp(0), wsp(0), bsp(0), bsp(0)],
        out_shape=[jax.ShapeDtypeStruct((T, D_FF), BF16), jax.ShapeDtypeStruct((T, D_FF), BF16),
                   jax.ShapeDtypeStruct((3, D_FF), F32), jax.ShapeDtypeStruct((3, D_FF), F32),
                   jax.ShapeDtypeStruct((1, D_FF), F32), jax.ShapeDtypeStruct((1, D_FF), F32)],
        name="ffn_act_bwd", compiler_params=_params("parallel", "arbitrary"))(
            upg, upv, cw, cw, cb, cb, dx2b, w_down)


def _down_loss(a, w_down, x1, target, gfin):
    T = x1.shape[0]
    tm = _tile(T, 512)

    def body(a_ref, w_ref, x1_ref, t_ref, g_ref, dx_ref, dxb_ref, loss_ref, dg_ref):
        @pl.when(pl.program_id(0) == 0)
        def _():
            loss_ref[...] = jnp.zeros_like(loss_ref)
            dg_ref[...] = jnp.zeros_like(dg_ref)

        x2 = x1_ref[...] + _dot(a_ref[...], w_ref[...])
        r = lax.rsqrt(jnp.mean(x2 * x2, axis=-1, keepdims=True) + EPS)
        xh = x2 * r
        g = g_ref[...]
        diff = xh * g - t_ref[...]
        loss_ref[...] += 0.5 * jnp.sum(jnp.mean(diff * diff, axis=-1, keepdims=True))
        dy = diff * (1.0 / D_MODEL)
        dg_ref[...] += jnp.sum(dy * xh, axis=0, keepdims=True)
        dxh = dy * g
        dx = r * (dxh - xh * jnp.mean(dxh * xh, axis=-1, keepdims=True))
        dx_ref[...] = dx
        dxb_ref[...] = dx.astype(BF16)

    row = pl.BlockSpec((tm, D_MODEL), lambda i: (i, 0))
    vec = pl.BlockSpec((1, D_MODEL), lambda i: (0, 0))
    return pl.pallas_call(
        body, grid=(T // tm,),
        in_specs=[pl.BlockSpec((tm, D_FF), lambda i: (i, 0)),
                  pl.BlockSpec((D_FF, D_MODEL), lambda i: (0, 0)), row, row, vec],
        out_specs=[row, row, pl.BlockSpec((8, LANES), lambda i: (0, 0)), vec],
        out_shape=[jax.ShapeDtypeStruct((T, D_MODEL), F32), jax.ShapeDtypeStruct((T, D_MODEL), BF16),
                   jax.ShapeDtypeStruct((8, LANES), F32), jax.ShapeDtypeStruct((1, D_MODEL), F32)],
        name="down_loss", compiler_params=_params("arbitrary"))(a, w_down, x1, target, gfin)


def _local_step(x, positions, target, mix_norm, av_g, av_b, w_s, b_s, q_norm, kv_norm, ffn_norm, conv_b,
                final_norm, comm):
    batch, seq, _ = x.shape
    T = batch * seq
    x = x.reshape(T, D_MODEL)
    target = target.reshape(T, D_MODEL)
    pos = positions.reshape(T, 1)
    half = jnp.arange(0, QK_ROPE, 2, dtype=F32) / QK_ROPE
    inv_freq = 1.0 / (ROPE_THETA ** half)
    invf = jnp.concatenate([inv_freq, inv_freq, jnp.zeros((LANES - QK_ROPE,), F32)]).reshape(1, LANES)
    w_st = jnp.swapaxes(w_s, 1, 2)
    b_col = b_s.reshape(A_GROUPS, CHUNK, 1)

    w_main, w_zs = comm.in_weights()
    h, zm, zs = _norm_mm(x, mix_norm, [w_main, w_zs], "in_proj")
    yag = _mixer_a_fwd(zm, av_g, av_b, w_s, b_col)
    wuq_p, wukv, w_out = comm.mla_weights(after=yag)
    q, k, v = _mla_prep_fwd(zs, pos, invf, q_norm, kv_norm, wuq_p, wukv)
    o, lse = _attn_fwd(q, k, v, batch, seq)
    merged, x1, h2 = _merge_out(x, yag, zm, o, w_out, ffn_norm)
    w_up, conv_w, w_down = comm.ffn_weights(after=merged)
    upg, upv, act = _up_act(h2, w_up, conv_w, conv_b, batch, seq)
    dx2, dx2b, loss_acc, d_final = _down_loss(act, w_down, x1, target, final_norm)

    d_wdown = _mm_tn(act, dx2b, "dw_down")
    dupg, dupv, dcwg, dcwv, dcbg, dcbv = _ffn_act_bwd(upg, upv, conv_w, conv_b, dx2b, w_down, batch, seq)
    d_wup = jnp.concatenate([_mm_tn(h2, dupg, "dw_up_gate"), _mm_tn(h2, dupv, "dw_up_val")], axis=1)
    dx1, d_ffn_norm, dmerged = _proj_bwd([(dupg, w_up, 0), (dupv, w_up, 1)], x1, ffn_norm, dx2, "up_proj_bwd",
                                         w2=w_out)
    d_wout = _mm_tn(merged, dx1, "dw_out")
    token = comm.send_ffn_grads(d_wdown, d_wup, jnp.concatenate([dcwg, dcwv], axis=1), d_wout)
    dzm, do, d_avg, d_avb, d_ws, d_bs = _mixer_bwd(zm, o, dmerged, av_g, av_b, w_s, w_st, b_col, token)
    token = comm.send_small_grads(_pack_small(dict(
        a_v_norm_g=d_avg, a_v_norm_b=d_avb, a_spatial_w=d_ws, a_spatial_b=d_bs, ffn_norm=d_ffn_norm,
        conv_b=jnp.concatenate([dcbg, dcbv], axis=1), final_norm=d_final), SMALL_EARLY))
    dq, dk, dv = _attn_bwd(q, k, v, o, do, lse, batch, seq, token)
    dzs, cqn, dqp, ckvn, dkv, d_qn, d_kvn = _mla_prep_bwd(zs, pos, invf, q_norm, kv_norm, wuq_p, wukv, dq, dk, dv)
    d_wuq_p = _mm_tn(cqn, dqp, "dw_uq")
    d_wukv = _mm_tn(ckvn, dkv, "dw_ukv")
    d_wmain = _mm_tn(h, dzm, "dw_in_main")
    d_wzs = _mm_tn(h, dzs, "dw_in_small")
    token = comm.send_in_grads(d_wmain, d_wzs, d_wuq_p, d_wukv)
    dx, d_mix_norm = _proj_bwd([(dzm, w_main, 0), (dzs, w_zs, 0)], x, mix_norm, dx1, "in_proj_bwd", dep=token)
    return loss_acc[0, 0], dx.reshape(batch, seq, D_MODEL), dict(q_a_norm=d_qn, kv_a_norm=d_kvn, mix_norm=d_mix_norm)


MESH_ID = pl.DeviceIdType.MESH
EFFECT = pltpu.SideEffectType.DATAFLOW_SIDE_EFFECTING


def _mesh_pos():
    return lax.axis_index("x"), lax.axis_index("y"), lax.axis_index("c")


def _peer(pos, d):
    x, y, c = pos
    px = 1 - x if d & 4 else x
    py = 1 - y if d & 2 else y
    pc = 1 - c if d & 1 else c
    return (px, py, pc), 4 * px + 2 * py + pc


def _copy(src_ref, land_ref, send_sems, recv_sems, a, d, pos, exchange, landing_here):
    peer, pid = _peer(pos, d)
    me = 4 * pos[0] + 2 * pos[1] + pos[2]
    if exchange:
        src, dst = src_ref.at[pid], land_ref.at[d]
    else:
        src, dst = src_ref, land_ref.at[pid if landing_here else me]
    return pltpu.make_async_remote_copy(
        src_ref=src, dst_ref=dst, send_sem=send_sems.at[a * (N_DEV - 1) + d - 1],
        recv_sem=recv_sems.at[a * (N_DEV - 1) + d - 1],
        device_id=peer, device_id_type=MESH_ID)


def _start_copies(groups, modes, name):
    sizes = [len(g) for g in groups]
    srcs = [s for g in groups for s, _ in g]
    lands = [l for g in groups for _, l in g]
    n, ng = len(srcs), len(groups)

    def body(*refs):
        src_refs, land_refs = refs[:n], refs[n:2 * n]
        sems = refs[2 * n:2 * n + 2 * ng]
        token = refs[-1]
        pos = _mesh_pos()
        k = 0
        for gi, size in enumerate(sizes):
            for a in range(size):
                for d in range(1, N_DEV):
                    _copy(src_refs[k], land_refs[k], sems[2 * gi], sems[2 * gi + 1], a, d, pos, modes[gi],
                          landing_here=False).start()
                k += 1
        token[...] = jnp.zeros_like(token)

    sem_shapes = [pltpu.SemaphoreType.DMA((size * (N_DEV - 1),)) for size in sizes for _ in range(2)]
    out = pl.pallas_call(
        body, name=name,
        out_shape=(*sem_shapes, *[pltpu.HBM(a.shape, a.dtype) for a in srcs + lands],
                   jax.ShapeDtypeStruct((8, LANES), F32)),
        in_specs=[HBM] * (2 * n),
        out_specs=(*[SEM] * (2 * ng), *[HBM] * (2 * n), pl.BlockSpec(memory_space=pltpu.VMEM)),
        input_output_aliases={i: 2 * ng + i for i in range(2 * n)},
        compiler_params=pltpu.CompilerParams(has_side_effects=EFFECT),
    )(*[pltpu.with_memory_space_constraint(a, pltpu.HBM) for a in srcs + lands])
    thru = out[2 * ng:2 * ng + 2 * n]
    handles, k = [], 0
    for gi, size in enumerate(sizes):
        handles.append((out[2 * gi], out[2 * gi + 1], thru[k:k + size], thru[n + k:n + k + size]))
        k += size
    return handles, out[-1]


def _wait_copies(handle, exchange, after, name):
    send_sems, recv_sems, srcs, lands = handle
    n = len(srcs)

    def body(*refs):
        src_refs, land_refs = refs[:n], refs[n:2 * n]
        send, recv = refs[2 * n], refs[2 * n + 1]
        pos = _mesh_pos()
        for a in range(n):
            for d in range(1, N_DEV):
                cp = _copy(src_refs[a], land_refs[a], send, recv, a, d, pos, exchange, landing_here=True)
                cp.wait_send()
                cp.wait_recv()

    out = pl.pallas_call(
        body, name=name,
        out_shape=tuple(pltpu.HBM(a.shape, a.dtype) for a in (*srcs, *lands)),
        in_specs=[HBM] * (2 * n) + [SEM, SEM, ANY], out_specs=[HBM] * (2 * n),
        input_output_aliases={i: i for i in range(2 * n)},
        compiler_params=pltpu.CompilerParams(has_side_effects=EFFECT),
    )(*srcs, *lands, send_sems, recv_sems, after)
    return out[n:]


def _land_gather(a, me):
    zone = jnp.zeros((N_DEV,) + a.shape, a.dtype)
    return lax.dynamic_update_slice(zone, a[None], (me,) + (0,) * a.ndim)


def _land_exchange(s, me):
    own = lax.dynamic_index_in_dim(s, me, 0, keepdims=True)
    return lax.dynamic_update_slice(jnp.zeros_like(s), own, (0,) * s.ndim)


def _adamw(parts, w, m, v, name):
    R, C = w.shape
    if N_DEV * R * C * parts.dtype.itemsize <= SMALL_BLOCK_BYTES:
        tr = R
    else:
        tr = next((t for t in range(min(R, 256) // 16 * 16, 15, -16) if R % t == 0), R)
    c1 = 1.0 - ADAM_B1 ** ADAM_STEP
    c2 = 1.0 - ADAM_B2 ** ADAM_STEP

    def body(p_ref, w_ref, m_ref, v_ref, g_ref, d_ref, nm_ref, nv_ref):
        g = p_ref[0].astype(F32)
        for k in range(1, N_DEV):
            g = g + p_ref[k].astype(F32)
        nm = ADAM_B1 * m_ref[...] + (1.0 - ADAM_B1) * g
        nv = ADAM_B2 * v_ref[...] + (1.0 - ADAM_B2) * (g * g)
        g_ref[...] = g
        nm_ref[...] = nm
        nv_ref[...] = nv
        d_ref[...] = -ADAM_LR * ((nm / c1) / (jnp.sqrt(nv / c2) + ADAM_EPS) + ADAM_WD * w_ref[...])

    blk = pl.BlockSpec((tr, C), lambda i: (i, 0))
    shp = jax.ShapeDtypeStruct((R, C), F32)
    return pl.pallas_call(
        body, grid=(R // tr,),
        in_specs=[pl.BlockSpec((N_DEV, tr, C), lambda i: (0, i, 0)), blk, blk, blk],
        out_specs=[blk, blk, blk, blk], out_shape=[shp, shp, shp, shp],
        name=name, compiler_params=_params("parallel"))(parts, w, m, v)


SPLIT_V = 2 * D_MODEL
SPLIT_KR = SPLIT_V + Q_LORA + KV_LORA + QK_ROPE
IN_DIM = SPLIT_KR + 2 * D_MODEL

SMALL_EARLY = ("a_v_norm_g", "a_v_norm_b", "a_spatial_w", "a_spatial_b", "ffn_norm", "conv_b", "final_norm")
SMALL_LATE = ("q_a_norm", "kv_a_norm", "mix_norm")


def _pack_rows(a):
    flat = a.reshape(-1)
    rows = -(-flat.shape[0] // LANES)
    rows8 = -(-rows // 8) * 8
    return jnp.pad(flat, (0, rows8 * LANES - flat.shape[0])).reshape(rows8, LANES)


def _pack_small(tree, names, extra=None):
    parts = [_pack_rows(tree[n]) for n in names]
    if extra is not None:
        parts.append(_pack_rows(extra))
    return jnp.concatenate(parts, axis=0)


def _unpack_small(buf, names, shapes):
    out, r = {}, 0
    for n in names:
        size = math.prod(shapes[n])
        rows8 = -(-(-(-size // LANES)) // 8) * 8
        out[n] = buf[r:r + rows8].reshape(-1)[:size].reshape(shapes[n])
        r += rows8
    return out, r


def _cols_from_shards(g):
    return jnp.transpose(g, (1, 0, 2)).reshape(g.shape[1], N_DEV * g.shape[2])


def _shards_from_cols(a):
    R, W = a.shape
    return jnp.transpose(a.reshape(R, N_DEV, W // N_DEV), (1, 0, 2))


class _Comm:
    GATHER_GROUPS = (("w_in",), ("w_uq", "w_ukv", "w_out"), ("w_up", "conv_w", "w_down"))
    FFN_GRADS = ("w_down", "w_up", "conv_w", "w_out")
    IN_GRADS = ("w_in", "w_uq", "w_ukv")

    def __init__(self, shards, me):
        self.me = me
        local = {n: a.astype(F32 if n == "conv_w" else BF16) for n, a in shards.items()}
        groups = [[(local[n], _land_gather(local[n], me)) for n in g] for g in self.GATHER_GROUPS]
        (self.h_in, self.h_mla, self.h_ffn), self.token = _start_copies(groups, [False] * 3, "gather_start")

    def in_weights(self):
        (g_in,) = _wait_copies(self.h_in, False, self.token, "gather_wait_in")
        w_full = _cols_from_shards(g_in)
        w_main = jnp.concatenate([w_full[:, :SPLIT_V], w_full[:, SPLIT_KR:]], axis=1)
        w_zs = jnp.pad(w_full[:, SPLIT_V:SPLIT_KR], ((0, 0), (0, ZS_W - (SPLIT_KR - SPLIT_V))))
        return w_main, w_zs

    def mla_weights(self, after):
        g_uq, g_ukv, g_out = _wait_copies(self.h_mla, False, after, "gather_wait_mla")
        wuq_p = _cols_from_shards(jnp.pad(g_uq, ((0, 0), (0, 0), (0, HEAD_PAD - QK_HEAD))))
        return wuq_p, _cols_from_shards(g_ukv), g_out.reshape(D_MODEL, D_MODEL)

    def ffn_weights(self, after):
        g_up, g_cw, g_down = _wait_copies(self.h_ffn, False, after, "gather_wait_ffn")
        return _cols_from_shards(g_up), _cols_from_shards(g_cw), g_down.reshape(D_FF, D_MODEL)

    def _exchange_group(self, blocks):
        return [(s, _land_exchange(s, self.me)) for s in blocks]

    def send_ffn_grads(self, d_wdown, d_wup, d_convw, d_wout):
        group = self._exchange_group([
            d_wdown.reshape(N_DEV, D_FF // N_DEV, D_MODEL), _shards_from_cols(d_wup), _shards_from_cols(d_convw),
            d_wout.reshape(N_DEV, D_MODEL // N_DEV, D_MODEL)])
        (self.h_ffn_grads,), token = _start_copies([group], [True], "ffn_grads_start")
        return token

    def send_small_grads(self, packed):
        (self.h_small_early,), token = _start_copies(
            [[(packed, _land_gather(packed, self.me))]], [False], "small_grads_start")
        return token

    def send_in_grads(self, d_wmain, d_wzs, d_wuq_p, d_wukv):
        d_in = jnp.concatenate([d_wmain[:, :SPLIT_V], d_wzs[:, :SPLIT_KR - SPLIT_V], d_wmain[:, SPLIT_V:]], axis=1)
        group = self._exchange_group([
            _shards_from_cols(d_in), _shards_from_cols(d_wuq_p)[:, :, :QK_HEAD], _shards_from_cols(d_wukv)])
        (self.h_in_grads,), token = _start_copies([group], [True], "in_grads_start")
        return token


def kernel(x, positions, mix_norm, w_in, a_v_norm_g, a_v_norm_b, a_spatial_w, a_spatial_b, q_a_norm, w_uq, kv_a_norm, w_ukv, w_out, ffn_norm, w_up, conv_w, conv_b, w_down, final_norm, loss_target, m_mix_norm, m_w_in, m_a_v_norm_g, m_a_v_norm_b, m_a_spatial_w, m_a_spatial_b, m_q_a_norm, m_w_uq, m_kv_a_norm, m_w_ukv, m_w_out, m_ffn_norm, m_w_up, m_conv_w, m_conv_b, m_w_down, m_final_norm, v_mix_norm, v_w_in, v_a_v_norm_g, v_a_v_norm_b, v_a_spatial_w, v_a_spatial_b, v_q_a_norm, v_w_uq, v_kv_a_norm, v_w_ukv, v_w_out, v_ffn_norm, v_w_up, v_conv_w, v_conv_b, v_w_down, v_final_norm):
    names = ("mix_norm", "w_in", "a_v_norm_g", "a_v_norm_b", "a_spatial_w", "a_spatial_b", "q_a_norm", "w_uq",
             "kv_a_norm", "w_ukv", "w_out", "ffn_norm", "w_up", "conv_w", "conv_b", "w_down", "final_norm")
    w = dict(zip(names, (mix_norm, w_in, a_v_norm_g, a_v_norm_b, a_spatial_w, a_spatial_b, q_a_norm, w_uq,
                         kv_a_norm, w_ukv, w_out, ffn_norm, w_up, conv_w, conv_b, w_down, final_norm)))
    m = dict(zip(names, (m_mix_norm, m_w_in, m_a_v_norm_g, m_a_v_norm_b, m_a_spatial_w, m_a_spatial_b,
                         m_q_a_norm, m_w_uq, m_kv_a_norm, m_w_ukv, m_w_out, m_ffn_norm, m_w_up, m_conv_w,
                         m_conv_b, m_w_down, m_final_norm)))
    v = dict(zip(names, (v_mix_norm, v_w_in, v_a_v_norm_g, v_a_v_norm_b, v_a_spatial_w, v_a_spatial_b,
                         v_q_a_norm, v_w_uq, v_kv_a_norm, v_w_ukv, v_w_out, v_ffn_norm, v_w_up, v_conv_w,
                         v_conv_b, v_w_down, v_final_norm)))
    shapes = {n: w[n].shape for n in names}
    me = 4 * lax.axis_index("x") + 2 * lax.axis_index("y") + lax.axis_index("c")
    comm = _Comm({n: w[n][0] for n in _Comm.GATHER_GROUPS[0] + _Comm.GATHER_GROUPS[1] + _Comm.GATHER_GROUPS[2]}, me)

    loss_part, grad_x, late_g = _local_step(
        x, positions, loss_target, w["mix_norm"], w["a_v_norm_g"], w["a_v_norm_b"], w["a_spatial_w"][0],
        w["a_spatial_b"][0], w["q_a_norm"], w["kv_a_norm"], w["ffn_norm"], w["conv_b"],
        w["final_norm"].reshape(1, D_MODEL), comm)

    late_local = _pack_small(late_g, SMALL_LATE, extra=loss_part.reshape(1))
    (h_late,), token = _start_copies([[(late_local, _land_gather(late_local, me))]], [False], "late_grads_start")

    out_g, out_d, out_m, out_v = {}, {}, {}, {}

    def update(n, parts):
        shp = shapes[n]
        r2 = (shp[-2], shp[-1])
        res = _adamw(parts, w[n].reshape(r2), m[n].reshape(r2), v[n].reshape(r2), "adamw_" + n)
        out_g[n], out_d[n], out_m[n], out_v[n] = (t.reshape(shp) for t in res)

    for n, parts in zip(_Comm.FFN_GRADS, _wait_copies(comm.h_ffn_grads, True, token, "ffn_grads_wait")):
        update(n, parts)
    for n, parts in zip(_Comm.IN_GRADS, _wait_copies(comm.h_in_grads, True, out_d["w_up"], "in_grads_wait")):
        update(n, parts)

    (early_parts,) = _wait_copies(comm.h_small_early, False, out_d["w_in"], "small_grads_wait")
    res = _adamw(early_parts, _pack_small(w, SMALL_EARLY), _pack_small(m, SMALL_EARLY), _pack_small(v, SMALL_EARLY),
                 "adamw_small")
    unpacked = [_unpack_small(t, SMALL_EARLY, shapes)[0] for t in res]
    for n in SMALL_EARLY:
        out_g[n], out_d[n], out_m[n], out_v[n] = (u[n] for u in unpacked)

    (late_parts,) = _wait_copies(h_late, False, res[1], "late_grads_wait")
    zero = jnp.zeros((1,), F32)
    res = _adamw(late_parts, _pack_small(w, SMALL_LATE, extra=zero), _pack_small(m, SMALL_LATE, extra=zero),
                 _pack_small(v, SMALL_LATE, extra=zero), "adamw_late")
    unpacked = [_unpack_small(t, SMALL_LATE, shapes) for t in res]
    for n in SMALL_LATE:
        out_g[n], out_d[n], out_m[n], out_v[n] = (u[0][n] for u in unpacked)
    loss = res[0][unpacked[0][1], 0]

    return (loss, grad_x, *[out_g[n] for n in names], *[out_d[n] for n in names],
            *[out_m[n] for n in names], *[out_v[n] for n in names])
```

```python
import functools
import math

import jax
import jax.numpy as jnp
from jax import lax
from jax.experimental import pallas as pl
from jax.experimental.pallas import tpu as pltpu

F32 = jnp.float32
BF16 = jnp.bfloat16

N_DEV = 8
D_MODEL = 1024
EPS = 1e-6
A_GROUPS = 8
CHUNK = 128
MLA_HEADS = 8
QK_NOPE = 128
QK_ROPE = 64
QK_HEAD = QK_NOPE + QK_ROPE
HEAD_PAD = 256
V_HEAD = 128
Q_LORA = 256
KV_LORA = 128
ROPE_THETA = 10000.0
D_FF = 2816
ZS_W = 512
ATTN_SCALE = QK_HEAD ** -0.5
NEG_BIG = -1e30

ADAM_LR = 0.001
ADAM_B1 = 0.9
ADAM_B2 = 0.999
ADAM_EPS = 1e-08
ADAM_WD = 0.01
ADAM_STEP = 10

VMEM_LIMIT = 56 * 1024 * 1024
SMALL_BLOCK_BYTES = 5 * 1024 * 1024
LANES = 128

GELU_K = math.sqrt(2.0 / math.pi)
GELU_C = 0.044715

ANY = pl.BlockSpec(memory_space=pl.ANY)
HBM = pl.BlockSpec(memory_space=pltpu.HBM)
SEM = pl.BlockSpec(memory_space=pltpu.SEMAPHORE)


def _tile(n, pref):
    for t in (pref, 512, 256, 128, 64, 32, 16, 8):
        if t <= pref and n % t == 0:
            return t
    return n


def _wide_tile(n, cap=1408):
    return next((t for t in range(min(n, cap) // LANES * LANES, 0, -LANES) if n % t == 0), n)


def _params(*sem):
    return pltpu.CompilerParams(dimension_semantics=sem, vmem_limit_bytes=VMEM_LIMIT)


def _dot(a, b):
    return jnp.dot(a, b, preferred_element_type=F32)


def _dot_nt(a, b):
    return lax.dot_general(a, b, (((1,), (1,)), ((), ())), preferred_element_type=F32)


def _dot_tn(a, b):
    return lax.dot_general(a, b, (((0,), (0,)), ((), ())), preferred_element_type=F32)


def _sigmoid(x):
    return 1.0 / (1.0 + jnp.exp(-x))


def _gelu(x):
    t = jnp.tanh(GELU_K * (x + GELU_C * x * x * x))
    return 0.5 * x * (1.0 + t)


def _gelu_grad(x):
    t = jnp.tanh(GELU_K * (x + GELU_C * x * x * x))
    return 0.5 * (1.0 + t) + 0.5 * x * (1.0 - t * t) * GELU_K * (1.0 + 3.0 * GELU_C * x * x)


def _norm_mm(x, g, ws, name):
    T, Dm = x.shape
    tm = _tile(T, 256)
    n_w = len(ws)

    def body(x_ref, g_ref, *refs):
        w_refs, h_ref, z_refs = refs[:n_w], refs[n_w], refs[n_w + 1:]
        xf = x_ref[...]
        r = lax.rsqrt(jnp.mean(xf * xf, axis=-1, keepdims=True) + EPS)
        h = (xf * r * g_ref[...]).astype(BF16)
        h_ref[...] = h
        for w_ref, z_ref in zip(w_refs, z_refs):
            z_ref[...] = _dot(h, w_ref[...])

    row = lambda n: pl.BlockSpec((tm, n), lambda i: (i, 0))
    return pl.pallas_call(
        body, grid=(T // tm,),
        in_specs=[row(Dm), pl.BlockSpec((1, Dm), lambda i: (0, 0))]
        + [pl.BlockSpec(w.shape, lambda i: (0, 0)) for w in ws],
        out_specs=[row(Dm)] + [row(w.shape[1]) for w in ws],
        out_shape=[jax.ShapeDtypeStruct((T, Dm), BF16)]
        + [jax.ShapeDtypeStruct((T, w.shape[1]), F32) for w in ws],
        name=name, compiler_params=_params("parallel"))(x, g, *ws)


def _proj_bwd(pairs, x, g, dres, name, w2=None, dep=None):
    T, Dm = x.shape
    tm = _tile(T, 256)
    n_pairs = len(pairs)

    def body(*refs):
        ins, outs = refs[:2 * n_pairs + 3 + (w2 is not None) + (dep is not None)], refs[-2 - (w2 is not None):]
        x_ref, g_ref, dres_ref = ins[2 * n_pairs:2 * n_pairs + 3]
        dx_ref, dg_ref = outs[0], outs[1]

        @pl.when(pl.program_id(0) == 0)
        def _():
            dg_ref[...] = jnp.zeros_like(dg_ref)

        dy = None
        for p in range(n_pairs):
            t = _dot_nt(ins[2 * p][...], ins[2 * p + 1][...])
            dy = t if dy is None else dy + t
        xf = x_ref[...]
        r = lax.rsqrt(jnp.mean(xf * xf, axis=-1, keepdims=True) + EPS)
        xh = xf * r
        dg_ref[...] += jnp.sum(dy * xh, axis=0, keepdims=True)
        dxh = dy * g_ref[...]
        dx = dres_ref[...] + r * (dxh - xh * jnp.mean(dxh * xh, axis=-1, keepdims=True))
        dx_ref[...] = dx
        if w2 is not None:
            outs[2][...] = _dot_nt(dx.astype(BF16), ins[2 * n_pairs + 3][...])

    row = pl.BlockSpec((tm, Dm), lambda i: (i, 0))
    vec = pl.BlockSpec((1, Dm), lambda i: (0, 0))
    in_specs, args = [], []
    for a, w, c in pairs:
        in_specs += [pl.BlockSpec((tm, a.shape[1]), lambda i: (i, 0)),
                     pl.BlockSpec((Dm, a.shape[1]), functools.partial(lambda i, c: (0, c), c=c))]
        args += [a, w]
    in_specs += [row, vec, row]
    args += [x, g, dres]
    out_specs = [row, vec]
    out_shape = [jax.ShapeDtypeStruct((T, Dm), F32), jax.ShapeDtypeStruct((1, Dm), F32)]
    if w2 is not None:
        in_specs.append(pl.BlockSpec(w2.shape, lambda i: (0, 0)))
        args.append(w2)
        out_specs.append(pl.BlockSpec((tm, w2.shape[0]), lambda i: (i, 0)))
        out_shape.append(jax.ShapeDtypeStruct((T, w2.shape[0]), F32))
    if dep is not None:
        in_specs.append(ANY)
        args.append(dep)
    return pl.pallas_call(
        body, grid=(T // tm,), in_specs=in_specs, out_specs=out_specs, out_shape=out_shape,
        name=name, compiler_params=_params("arbitrary"))(*args)


def _mm_tn(a, b, name, dep=None):
    T, M = a.shape
    N = b.shape[1]
    tm, tn, tt = _wide_tile(M), _wide_tile(N), _tile(T, 512)
    n_t = T // tt

    def body(a_ref, b_ref, *refs):
        o_ref, acc_ref = refs[-2:]
        t = pl.program_id(2)

        @pl.when(t == 0)
        def _():
            acc_ref[...] = jnp.zeros_like(acc_ref)

        acc_ref[...] += _dot_tn(a_ref[...].astype(BF16), b_ref[...].astype(BF16))

        @pl.when(t == n_t - 1)
        def _():
            o_ref[...] = acc_ref[...].astype(BF16)

    return pl.pallas_call(
        body, grid=(M // tm, N // tn, n_t),
        in_specs=[pl.BlockSpec((tt, tm), lambda i, j, t: (t, i)),
                  pl.BlockSpec((tt, tn), lambda i, j, t: (t, j))] + [ANY] * (dep is not None),
        out_specs=pl.BlockSpec((tm, tn), lambda i, j, t: (i, j)),
        out_shape=jax.ShapeDtypeStruct((M, N), BF16),
        scratch_shapes=[pltpu.VMEM((tm, tn), F32)],
        name=name, compiler_params=_params("parallel", "parallel", "arbitrary"))(
            a, b, *([dep] if dep is not None else []))


def _layer_norm_fwd(gv, g, b):
    mu = jnp.mean(gv, axis=-1, keepdims=True)
    xc = gv - mu
    rs = lax.rsqrt(jnp.mean(xc * xc, axis=-1, keepdims=True) + EPS)
    xh = xc * rs
    return xh, rs, xh * g + b


def _tri_mask(transposed=False):
    r = lax.broadcasted_iota(jnp.int32, (CHUNK, CHUNK), 0)
    c = lax.broadcasted_iota(jnp.int32, (CHUNK, CHUNK), 1)
    return r <= c if transposed else c <= r


def _mixer_a_fwd(zm, av_g, av_b, w_s, b_col):
    T = zm.shape[0]
    tm = _tile(T, 256)
    n_chunk = tm // CHUNK

    def body(u_ref, v_ref, ga_ref, g_ref, b_ref, w_ref, bc_ref, y_ref, vn_s, mx_s):
        gu = _gelu(u_ref[...])
        _, _, vn = _layer_norm_fwd(_gelu(v_ref[...]), g_ref[...], b_ref[...])
        vn_s[...] = vn.astype(BF16)
        tri = _tri_mask()
        for gi in range(A_GROUPS):
            wm = jnp.where(tri, w_ref[gi], 0.0).astype(BF16)
            cols = slice(gi * CHUNK, (gi + 1) * CHUNK)
            for n in range(n_chunk):
                rows = slice(n * CHUNK, (n + 1) * CHUNK)
                mx_s[rows, cols] = _dot(wm, vn_s[rows, cols]) + bc_ref[gi]
        y_ref[...] = _sigmoid(ga_ref[...]) * gu * mx_s[...]

    col = lambda c: pl.BlockSpec((tm, D_MODEL), lambda i: (i, c))
    vec = pl.BlockSpec((1, D_MODEL), lambda i: (0, 0))
    return pl.pallas_call(
        body, grid=(T // tm,),
        in_specs=[col(0), col(1), col(2), vec, vec,
                  pl.BlockSpec((A_GROUPS, CHUNK, CHUNK), lambda i: (0, 0, 0)),
                  pl.BlockSpec((A_GROUPS, CHUNK, 1), lambda i: (0, 0, 0))],
        out_specs=pl.BlockSpec((tm, D_MODEL), lambda i: (i, 0)),
        out_shape=jax.ShapeDtypeStruct((T, D_MODEL), F32),
        scratch_shapes=[pltpu.VMEM((tm, D_MODEL), BF16), pltpu.VMEM((tm, D_MODEL), F32)],
        name="mixer_a_fwd", compiler_params=_params("parallel"))(zm, zm, zm, av_g, av_b, w_s, b_col)


def _mixer_bwd(zm, o, dm, av_g, av_b, w_s, w_st, b_col, dep):
    T = zm.shape[0]
    tm = _tile(T, 256)
    n_chunk = tm // CHUNK

    def body(u_ref, v_ref, ga_ref, gb_ref, o_ref, dm_ref, g_ref, b_ref, w_ref, wt_ref, bc_ref, dep_ref,
             dz_ref, do_ref, dg_ref, db_ref, dw_ref, dbs_ref, vn_s, mx_s, dmx_s, dvn_s):
        @pl.when(pl.program_id(0) == 0)
        def _():
            dg_ref[...] = jnp.zeros_like(dg_ref)
            db_ref[...] = jnp.zeros_like(db_ref)
            dw_ref[...] = jnp.zeros_like(dw_ref)
            dbs_ref[...] = jnp.zeros_like(dbs_ref)

        dm_v = dm_ref[...]
        gb = gb_ref[...]
        sb = _sigmoid(gb)
        o_v = o_ref[...]
        do_ref[...] = (dm_v * sb).astype(BF16)
        dz_ref[:, 3 * D_MODEL:4 * D_MODEL] = (dm_v * o_v * sb * (1.0 - sb)).astype(BF16)
        u = u_ref[...]
        v = v_ref[...]
        gu = _gelu(u)
        xh, rs, vn = _layer_norm_fwd(_gelu(v), g_ref[...], b_ref[...])
        vn_s[...] = vn.astype(BF16)
        tri = _tri_mask()
        for gi in range(A_GROUPS):
            wm = jnp.where(tri, w_ref[gi], 0.0).astype(BF16)
            cols = slice(gi * CHUNK, (gi + 1) * CHUNK)
            for n in range(n_chunk):
                rows = slice(n * CHUNK, (n + 1) * CHUNK)
                mx_s[rows, cols] = _dot(wm, vn_s[rows, cols]) + bc_ref[gi]
        mixed = mx_s[...]
        sa = _sigmoid(ga_ref[...])
        dya = dm_v * sa
        dz_ref[:, 2 * D_MODEL:3 * D_MODEL] = (dm_v * gu * mixed * sa * (1.0 - sa)).astype(BF16)
        dz_ref[:, 0:D_MODEL] = (dya * mixed * _gelu_grad(u)).astype(BF16)
        dmx = dya * gu
        dmx_s[...] = dmx.astype(BF16)
        tri_t = _tri_mask(transposed=True)
        for gi in range(A_GROUPS):
            wmt = jnp.where(tri_t, wt_ref[gi], 0.0).astype(BF16)
            cols = slice(gi * CHUNK, (gi + 1) * CHUNK)
            dw_acc = jnp.zeros((CHUNK, CHUNK), F32)
            dmx_sum = jnp.zeros((CHUNK, CHUNK), F32)
            for n in range(n_chunk):
                rows = slice(n * CHUNK, (n + 1) * CHUNK)
                blk = dmx_s[rows, cols]
                dvn_s[rows, cols] = _dot(wmt, blk)
                dw_acc = dw_acc + _dot_nt(blk, vn_s[rows, cols])
                dmx_sum = dmx_sum + dmx[rows, cols]
            dw_ref[gi] += jnp.where(tri, dw_acc, 0.0)
            dbs_ref[gi] += jnp.sum(dmx_sum, axis=-1, keepdims=True)
        dvn = dvn_s[...]
        dg_ref[...] += jnp.sum(dvn * xh, axis=0, keepdims=True)
        db_ref[...] += jnp.sum(dvn, axis=0, keepdims=True)
        dxh = dvn * g_ref[...]
        dgv = rs * (dxh - jnp.mean(dxh, axis=-1, keepdims=True)
                    - xh * jnp.mean(dxh * xh, axis=-1, keepdims=True))
        dz_ref[:, D_MODEL:2 * D_MODEL] = (dgv * _gelu_grad(v)).astype(BF16)

    col = lambda c: pl.BlockSpec((tm, D_MODEL), lambda i: (i, c))
    row = pl.BlockSpec((tm, D_MODEL), lambda i: (i, 0))
    vec = pl.BlockSpec((1, D_MODEL), lambda i: (0, 0))
    wsp = pl.BlockSpec((A_GROUPS, CHUNK, CHUNK), lambda i: (0, 0, 0))
    bsp = pl.BlockSpec((A_GROUPS, CHUNK, 1), lambda i: (0, 0, 0))
    return pl.pallas_call(
        body, grid=(T // tm,),
        in_specs=[col(0), col(1), col(2), col(3), row, row, vec, vec, wsp, wsp, bsp, ANY],
        out_specs=[pl.BlockSpec((tm, 4 * D_MODEL), lambda i: (i, 0)), row, vec, vec, wsp, bsp],
        out_shape=[jax.ShapeDtypeStruct((T, 4 * D_MODEL), BF16), jax.ShapeDtypeStruct((T, D_MODEL), BF16),
                   jax.ShapeDtypeStruct((1, D_MODEL), F32), jax.ShapeDtypeStruct((1, D_MODEL), F32),
                   jax.ShapeDtypeStruct((A_GROUPS, CHUNK, CHUNK), F32),
                   jax.ShapeDtypeStruct((A_GROUPS, CHUNK, 1), F32)],
        scratch_shapes=[pltpu.VMEM((tm, D_MODEL), BF16), pltpu.VMEM((tm, D_MODEL), F32),
                        pltpu.VMEM((tm, D_MODEL), BF16), pltpu.VMEM((tm, D_MODEL), F32)],
        name="mixer_bwd", compiler_params=_params("arbitrary"))(
            zm, zm, zm, zm, o, dm, av_g, av_b, w_s, w_st, b_col, dep)


def _rope_tables(pos_ref, invf_ref):
    ang = pos_ref[...].astype(F32) * invf_ref[...]
    lane = lax.broadcasted_iota(jnp.int32, ang.shape, 1)
    cos, sin = jnp.cos(ang), jnp.sin(ang)
    c = jnp.where(lane < QK_ROPE, cos, 0.0)
    sa = jnp.where(lane < QK_ROPE // 2, -sin, 0.0)
    sb = jnp.where((lane >= QK_ROPE // 2) & (lane < QK_ROPE), sin, 0.0)
    return c, sa, sb


def _rope(blk, tabs):
    c, sa, sb = tabs
    return blk * c + pltpu.roll(blk, LANES - QK_ROPE // 2, 1) * sa + pltpu.roll(blk, QK_ROPE // 2, 1) * sb


def _rope_t(dout, tabs):
    c, sa, sb = tabs
    return dout * c + pltpu.roll(dout * sa, QK_ROPE // 2, 1) + pltpu.roll(dout * sb, LANES - QK_ROPE // 2, 1)


def _rms_small(x, g):
    r = lax.rsqrt(jnp.mean(x * x, axis=-1, keepdims=True) + EPS)
    xh = x * r
    return xh, r, xh * g


def _mla_prep_fwd(zs, pos, invf, qg, kvg, wuq_p, wukv):
    T = zs.shape[0]
    tm = _tile(T, 512)
    HW = MLA_HEADS * HEAD_PAD

    def body(zs_ref, pos_ref, invf_ref, qg_ref, kvg_ref, wq_ref, wkv_ref, q_ref, k_ref, v_ref):
        tabs = _rope_tables(pos_ref, invf_ref)
        _, _, cqn = _rms_small(zs_ref[:, 0:Q_LORA], qg_ref[...])
        _, _, ckvn = _rms_small(zs_ref[:, Q_LORA:Q_LORA + KV_LORA], kvg_ref[...])
        q = _dot(cqn.astype(BF16), wq_ref[...])
        kv = _dot(ckvn.astype(BF16), wkv_ref[...])
        kr = _rope(zs_ref[:, Q_LORA + KV_LORA:ZS_W], tabs).astype(BF16)
        for h in range(MLA_HEADS):
            b0 = h * HEAD_PAD
            q_ref[:, b0:b0 + QK_NOPE] = q[:, b0:b0 + QK_NOPE].astype(BF16)
            q_ref[:, b0 + QK_NOPE:b0 + HEAD_PAD] = _rope(q[:, b0 + QK_NOPE:b0 + HEAD_PAD], tabs).astype(BF16)
            k_ref[:, b0:b0 + QK_NOPE] = kv[:, b0:b0 + QK_NOPE].astype(BF16)
            k_ref[:, b0 + QK_NOPE:b0 + HEAD_PAD] = kr
            v_ref[:, h * V_HEAD:(h + 1) * V_HEAD] = kv[:, b0 + QK_NOPE:b0 + HEAD_PAD].astype(BF16)

    full = lambda a: pl.BlockSpec(a.shape, lambda i: (0,) * a.ndim)
    return pl.pallas_call(
        body, grid=(T // tm,),
        in_specs=[pl.BlockSpec((tm, ZS_W), lambda i: (i, 0)), pl.BlockSpec((tm, 1), lambda i: (i, 0)),
                  full(invf), full(qg), full(kvg), full(wuq_p), full(wukv)],
        out_specs=[pl.BlockSpec((tm, HW), lambda i: (i, 0)), pl.BlockSpec((tm, HW), lambda i: (i, 0)),
                   pl.BlockSpec((tm, D_MODEL), lambda i: (i, 0))],
        out_shape=[jax.ShapeDtypeStruct((T, HW), BF16), jax.ShapeDtypeStruct((T, HW), BF16),
                   jax.ShapeDtypeStruct((T, D_MODEL), BF16)],
        name="mla_prep_fwd", compiler_params=_params("parallel"))(zs, pos, invf, qg, kvg, wuq_p, wukv)


def _mla_prep_bwd(zs, pos, invf, qg, kvg, wuq_p, wukv, dq, dk, dv):
    T = zs.shape[0]
    tm = _tile(T, 256)
    HW = MLA_HEADS * HEAD_PAD

    def body(zs_ref, pos_ref, invf_ref, qg_ref, kvg_ref, wq_ref, wkv_ref, dq_ref, dk_ref, dv_ref,
             dzs_ref, cqn_ref, dqp_ref, ckvn_ref, dkv_ref, dqg_ref, dkvg_ref):
        @pl.when(pl.program_id(0) == 0)
        def _():
            dqg_ref[...] = jnp.zeros_like(dqg_ref)
            dkvg_ref[...] = jnp.zeros_like(dkvg_ref)

        tabs = _rope_tables(pos_ref, invf_ref)
        cqh, rq, cqn = _rms_small(zs_ref[:, 0:Q_LORA], qg_ref[...])
        ckvh, rkv, ckvn = _rms_small(zs_ref[:, Q_LORA:Q_LORA + KV_LORA], kvg_ref[...])
        cqn_ref[...] = cqn.astype(BF16)
        ckvn_ref[...] = ckvn.astype(BF16)
        dkr = jnp.zeros((tm, LANES), F32)
        for h in range(MLA_HEADS):
            b0 = h * HEAD_PAD
            dqp_ref[:, b0:b0 + QK_NOPE] = dq_ref[:, b0:b0 + QK_NOPE].astype(BF16)
            dqp_ref[:, b0 + QK_NOPE:b0 + HEAD_PAD] = _rope_t(dq_ref[:, b0 + QK_NOPE:b0 + HEAD_PAD], tabs).astype(BF16)
            dkv_ref[:, b0:b0 + QK_NOPE] = dk_ref[:, b0:b0 + QK_NOPE].astype(BF16)
            dkv_ref[:, b0 + QK_NOPE:b0 + HEAD_PAD] = dv_ref[:, h * V_HEAD:(h + 1) * V_HEAD].astype(BF16)
            dkr = dkr + dk_ref[:, b0 + QK_NOPE:b0 + HEAD_PAD]
        dcqn = _dot_nt(dqp_ref[...], wq_ref[...])
        dckvn = _dot_nt(dkv_ref[...], wkv_ref[...])
        dqg_ref[...] += jnp.sum(dcqn * cqh, axis=0, keepdims=True)
        dkvg_ref[...] += jnp.sum(dckvn * ckvh, axis=0, keepdims=True)
        dxh = dcqn * qg_ref[...]
        dzs_ref[:, 0:Q_LORA] = (rq * (dxh - cqh * jnp.mean(dxh * cqh, axis=-1, keepdims=True))).astype(BF16)
        dxh = dckvn * kvg_ref[...]
        dzs_ref[:, Q_LORA:Q_LORA + KV_LORA] = (
            rkv * (dxh - ckvh * jnp.mean(dxh * ckvh, axis=-1, keepdims=True))).astype(BF16)
        dzs_ref[:, Q_LORA + KV_LORA:ZS_W] = _rope_t(dkr, tabs).astype(BF16)

    full = lambda a: pl.BlockSpec(a.shape, lambda i: (0,) * a.ndim)
    rowb = lambda w: pl.BlockSpec((tm, w), lambda i: (i, 0))
    return pl.pallas_call(
        body, grid=(T // tm,),
        in_specs=[rowb(ZS_W), rowb(1), full(invf), full(qg), full(kvg), full(wuq_p), full(wukv),
                  rowb(HW), rowb(HW), rowb(D_MODEL)],
        out_specs=[rowb(ZS_W), rowb(Q_LORA), rowb(HW), rowb(KV_LORA), rowb(HW), full(qg), full(kvg)],
        out_shape=[jax.ShapeDtypeStruct((T, ZS_W), BF16), jax.ShapeDtypeStruct((T, Q_LORA), BF16),
                   jax.ShapeDtypeStruct((T, HW), BF16), jax.ShapeDtypeStruct((T, KV_LORA), BF16),
                   jax.ShapeDtypeStruct((T, HW), BF16), jax.ShapeDtypeStruct(qg.shape, F32),
                   jax.ShapeDtypeStruct(kvg.shape, F32)],
        name="mla_prep_bwd", compiler_params=_params("arbitrary"))(
            zs, pos, invf, qg, kvg, wuq_p, wukv, dq, dk, dv)


def _causal(tq, kmax, q0):
    r = lax.broadcasted_iota(jnp.int32, (tq, kmax), 0) + q0
    c = lax.broadcasted_iota(jnp.int32, (tq, kmax), 1)
    return c <= r


def _attn_fwd(q, k, v, batch, seq):
    tq = _tile(seq, 512)
    nq = seq // tq

    def body(q_ref, k_ref, v_ref, o_ref, lse_ref):
        for qi in range(nq):
            rows = slice(qi * tq, (qi + 1) * tq)
            kmax = (qi + 1) * tq
            s = _dot_nt(q_ref[rows, :], k_ref[0:kmax, :]) * ATTN_SCALE
            s = jnp.where(_causal(tq, kmax, qi * tq), s, NEG_BIG)
            m = jnp.max(s, axis=-1, keepdims=True)
            p = jnp.exp(s - m)
            l = jnp.sum(p, axis=-1, keepdims=True)
            o_ref[rows, :] = _dot(p.astype(BF16), v_ref[0:kmax, :]) / l
            lse_ref[rows, :] = jnp.broadcast_to(m + jnp.log(l), (tq, V_HEAD))

    return pl.pallas_call(
        body, grid=(batch, MLA_HEADS),
        in_specs=[pl.BlockSpec((seq, HEAD_PAD), lambda b, h: (b, h)),
                  pl.BlockSpec((seq, HEAD_PAD), lambda b, h: (b, h)),
                  pl.BlockSpec((seq, V_HEAD), lambda b, h: (b, h))],
        out_specs=[pl.BlockSpec((seq, V_HEAD), lambda b, h: (b, h)),
                   pl.BlockSpec((seq, V_HEAD), lambda b, h: (b, h))],
        out_shape=[jax.ShapeDtypeStruct((batch * seq, D_MODEL), F32),
                   jax.ShapeDtypeStruct((batch * seq, D_MODEL), F32)],
        name="attn_fwd", compiler_params=_params("parallel", "parallel"))(q, k, v)


def _attn_bwd(q, k, v, o, do, lse, batch, seq, dep):
    tq = _tile(seq, 512)
    nq = seq // tq

    def body(q_ref, k_ref, v_ref, o_ref, do_ref, lse_ref, dep_ref, dq_ref, dk_ref, dv_ref):
        dk_ref[...] = jnp.zeros_like(dk_ref)
        dv_ref[...] = jnp.zeros_like(dv_ref)
        for qi in range(nq):
            rows = slice(qi * tq, (qi + 1) * tq)
            kmax = (qi + 1) * tq
            qr = q_ref[rows, :]
            dor = do_ref[rows, :]
            kk = k_ref[0:kmax, :]
            s = _dot_nt(qr, kk) * ATTN_SCALE
            p = jnp.where(_causal(tq, kmax, qi * tq), jnp.exp(s - lse_ref[rows, 0:1]), 0.0)
            dp = _dot_nt(dor, v_ref[0:kmax, :])
            delta = jnp.sum(dor.astype(F32) * o_ref[rows, :], axis=-1, keepdims=True)
            ds = (p * (dp - delta) * ATTN_SCALE).astype(BF16)
            dq_ref[rows, :] = _dot(ds, kk)
            dk_ref[0:kmax, :] += _dot_tn(ds, qr)
            dv_ref[0:kmax, :] += _dot_tn(p.astype(BF16), dor)

    qspec = pl.BlockSpec((seq, HEAD_PAD), lambda b, h: (b, h))
    vspec = pl.BlockSpec((seq, V_HEAD), lambda b, h: (b, h))
    T = batch * seq
    return pl.pallas_call(
        body, grid=(batch, MLA_HEADS),
        in_specs=[qspec, qspec, vspec, vspec, vspec, vspec, ANY],
        out_specs=[qspec, qspec, vspec],
        out_shape=[jax.ShapeDtypeStruct((T, MLA_HEADS * HEAD_PAD), F32),
                   jax.ShapeDtypeStruct((T, MLA_HEADS * HEAD_PAD), F32),
                   jax.ShapeDtypeStruct((T, D_MODEL), F32)],
        name="attn_bwd", compiler_params=_params("parallel", "parallel"))(q, k, v, o, do, lse, dep)


def _merge_out(x, yag, zm, o, w_out, ffn_g):
    T = x.shape[0]
    tm = _tile(T, 512)

    def body(x_ref, ya_ref, gb_ref, o_ref, w_ref, g_ref, mg_ref, x1_ref, h2_ref):
        mg = (ya_ref[...] + _sigmoid(gb_ref[...]) * o_ref[...]).astype(BF16)
        mg_ref[...] = mg
        x1 = x_ref[...] + _dot(mg, w_ref[...])
        x1_ref[...] = x1
        r = lax.rsqrt(jnp.mean(x1 * x1, axis=-1, keepdims=True) + EPS)
        h2_ref[...] = (x1 * r * g_ref[...]).astype(BF16)

    row = pl.BlockSpec((tm, D_MODEL), lambda i: (i, 0))
    return pl.pallas_call(
        body, grid=(T // tm,),
        in_specs=[row, row, pl.BlockSpec((tm, D_MODEL), lambda i: (i, 3)), row,
                  pl.BlockSpec((D_MODEL, D_MODEL), lambda i: (0, 0)), pl.BlockSpec((1, D_MODEL), lambda i: (0, 0))],
        out_specs=[row, row, row],
        out_shape=[jax.ShapeDtypeStruct((T, D_MODEL), BF16), jax.ShapeDtypeStruct((T, D_MODEL), F32),
                   jax.ShapeDtypeStruct((T, D_MODEL), BF16)],
        name="merge_out", compiler_params=_params("parallel"))(x, yag, zm, o, w_out, ffn_g)


FF_TILE = 256
FF_BLOCKS = D_FF // FF_TILE


def _shift_down(x, k):
    row = lax.broadcasted_iota(jnp.int32, x.shape, 0)
    return jnp.where(row >= k, pltpu.roll(x, k, 0), 0.0)


def _shift_up(x, k):
    n = x.shape[0]
    row = lax.broadcasted_iota(jnp.int32, x.shape, 0)
    return jnp.where(row < n - k, pltpu.roll(x, n - k, 0), 0.0)


def _conv(x, w_ref, b_ref):
    return b_ref[...] + w_ref[2:3, :] * x + w_ref[1:2, :] * _shift_down(x, 1) + w_ref[0:1, :] * _shift_down(x, 2)


def _up_act(h2, w_up, cw, cb, batch, seq):
    def body(h_ref, wug_ref, wuv_ref, wg_ref, wv_ref, bg_ref, bv_ref, ug_ref, uv_ref, a_ref):
        h = h_ref[...]
        ug = _dot(h, wug_ref[...])
        uv = _dot(h, wuv_ref[...])
        ug_ref[...] = ug
        uv_ref[...] = uv
        gate = _conv(ug, wg_ref, bg_ref)
        val = _conv(uv, wv_ref, bv_ref)
        a_ref[...] = (gate * _sigmoid(gate) * val).astype(BF16)

    blk = pl.BlockSpec((seq, FF_TILE), lambda b, j: (b, j))
    wup = lambda off: pl.BlockSpec((D_MODEL, FF_TILE), lambda b, j: (0, j + off))
    wsp = lambda off: pl.BlockSpec((3, FF_TILE), lambda b, j: (0, j + off))
    bsp = lambda off: pl.BlockSpec((1, FF_TILE), lambda b, j: (0, j + off))
    T = batch * seq
    return pl.pallas_call(
        body, grid=(batch, FF_BLOCKS),
        in_specs=[pl.BlockSpec((seq, D_MODEL), lambda b, j: (b, 0)), wup(0), wup(FF_BLOCKS),
                  wsp(0), wsp(FF_BLOCKS), bsp(0), bsp(FF_BLOCKS)],
        out_specs=[blk, blk, blk],
        out_shape=[jax.ShapeDtypeStruct((T, D_FF), F32), jax.ShapeDtypeStruct((T, D_FF), F32),
                   jax.ShapeDtypeStruct((T, D_FF), BF16)],
        name="up_act", compiler_params=_params("parallel", "arbitrary"))(h2, w_up, w_up, cw, cw, cb, cb)


def _ffn_act_bwd(upg, upv, cw, cb, dx2b, w_down, batch, seq):
    def half(du, x, w_ref, dx_ref, dw_ref, db_ref):
        dx_ref[...] = (w_ref[2:3, :] * du + w_ref[1:2, :] * _shift_up(du, 1)
                       + w_ref[0:1, :] * _shift_up(du, 2)).astype(BF16)
        dw_ref[2:3, :] += jnp.sum(du * x, axis=0, keepdims=True)
        dw_ref[1:2, :] += jnp.sum(du * _shift_down(x, 1), axis=0, keepdims=True)
        dw_ref[0:1, :] += jnp.sum(du * _shift_down(x, 2), axis=0, keepdims=True)
        db_ref[...] += jnp.sum(du, axis=0, keepdims=True)

    def body(ug_ref, uv_ref, wg_ref, wv_ref, bg_ref, bv_ref, dx_ref, wd_ref,
             dg_ref, dv_ref, dwg_ref, dwv_ref, dbg_ref, dbv_ref):
        @pl.when(pl.program_id(1) == 0)
        def _():
            for r in (dwg_ref, dwv_ref, dbg_ref, dbv_ref):
                r[...] = jnp.zeros_like(r)

        ug, uv = ug_ref[...], uv_ref[...]
        gate = _conv(ug, wg_ref, bg_ref)
        val = _conv(uv, wv_ref, bv_ref)
        sg = _sigmoid(gate)
        dav = _dot_nt(dx_ref[...], wd_ref[...])
        half(dav * val * sg * (1.0 + gate * (1.0 - sg)), ug, wg_ref, dg_ref, dwg_ref, dbg_ref)
        half(dav * gate * sg, uv, wv_ref, dv_ref, dwv_ref, dbv_ref)

    blk = pl.BlockSpec((seq, FF_TILE), lambda j, b: (b, j))
    wsp = lambda off: pl.BlockSpec((3, FF_TILE), lambda j, b: (0, j + off))
    bsp = lambda off: pl.BlockSpec((1, FF_TILE), lambda j, b: (0, j + off))
    T = batch * seq
    return pl.pallas_call(
        body, grid=(FF_BLOCKS, batch),
        in_specs=[blk, blk, wsp(0), wsp(FF_BLOCKS), bsp(0), bsp(FF_BLOCKS),
                  pl.BlockSpec((seq, D_MODEL), lambda j, b: (b, 0)),
                  pl.BlockSpec((FF_TILE, D_MODEL), lambda j, b: (j, 0))],
        out_specs=[blk, blk, wsp(0), wsp(0), bsp(0), bsp(0)],
        out_shape=[jax.ShapeDtypeStruct((T, D_FF), BF16), jax.ShapeDtypeStruct((T, D_FF), BF16),
                   jax.ShapeDtypeStruct((3, D_FF), F32), jax.ShapeDtypeStruct((3, D_FF), F32),
                   jax.ShapeDtypeStruct((1, D_FF), F32), jax.ShapeDtypeStruct((1, D_FF), F32)],
        name="ffn_act_bwd", compiler_params=_params("parallel", "arbitrary"))(
            upg, upv, cw, cw, cb, cb, dx2b, w_down)


def _down_loss(a, w_down, x1, target, gfin):
    T = x1.shape[0]
    tm = _tile(T, 512)

    def body(a_ref, w_ref, x1_ref, t_ref, g_ref, dx_ref, dxb_ref, loss_ref, dg_ref):
        @pl.when(pl.program_id(0) == 0)
        def _():
            loss_ref[...] = jnp.zeros_like(loss_ref)
            dg_ref[...] = jnp.zeros_like(dg_ref)

        x2 = x1_ref[...] + _dot(a_ref[...], w_ref[...])
        r = lax.rsqrt(jnp.mean(x2 * x2, axis=-1, keepdims=True) + EPS)
        xh = x2 * r
        g = g_ref[...]
        diff = xh * g - t_ref[...]
        loss_ref[...] += 0.5 * jnp.sum(jnp.mean(diff * diff, axis=-1, keepdims=True))
        dy = diff * (1.0 / D_MODEL)
        dg_ref[...] += jnp.sum(dy * xh, axis=0, keepdims=True)
        dxh = dy * g
        dx = r * (dxh - xh * jnp.mean(dxh * xh, axis=-1, keepdims=True))
        dx_ref[...] = dx
        dxb_ref[...] = dx.astype(BF16)

    row = pl.BlockSpec((tm, D_MODEL), lambda i: (i, 0))
    vec = pl.BlockSpec((1, D_MODEL), lambda i: (0, 0))
    return pl.pallas_call(
        body, grid=(T // tm,),
        in_specs=[pl.BlockSpec((tm, D_FF), lambda i: (i, 0)),
                  pl.BlockSpec((D_FF, D_MODEL), lambda i: (0, 0)), row, row, vec],
        out_specs=[row, row, pl.BlockSpec((8, LANES), lambda i: (0, 0)), vec],
        out_shape=[jax.ShapeDtypeStruct((T, D_MODEL), F32), jax.ShapeDtypeStruct((T, D_MODEL), BF16),
                   jax.ShapeDtypeStruct((8, LANES), F32), jax.ShapeDtypeStruct((1, D_MODEL), F32)],
        name="down_loss", compiler_params=_params("arbitrary"))(a, w_down, x1, target, gfin)


def _local_step(x, positions, target, mix_norm, av_g, av_b, w_s, b_s, q_norm, kv_norm, ffn_norm, conv_b,
                final_norm, comm):
    batch, seq, _ = x.shape
    T = batch * seq
    x = x.reshape(T, D_MODEL)
    target = target.reshape(T, D_MODEL)
    pos = positions.reshape(T, 1)
    half = jnp.arange(0, QK_ROPE, 2, dtype=F32) / QK_ROPE
    inv_freq = 1.0 / (ROPE_THETA ** half)
    invf = jnp.concatenate([inv_freq, inv_freq, jnp.zeros((LANES - QK_ROPE,), F32)]).reshape(1, LANES)
    w_st = jnp.swapaxes(w_s, 1, 2)
    b_col = b_s.reshape(A_GROUPS, CHUNK, 1)

    w_main, w_zs = comm.in_weights()
    h, zm, zs = _norm_mm(x, mix_norm, [w_main, w_zs], "in_proj")
    yag = _mixer_a_fwd(zm, av_g, av_b, w_s, b_col)
    wuq_p, wukv, w_out = comm.mla_weights(after=yag)
    q, k, v = _mla_prep_fwd(zs, pos, invf, q_norm, kv_norm, wuq_p, wukv)
    o, lse = _attn_fwd(q, k, v, batch, seq)
    merged, x1, h2 = _merge_out(x, yag, zm, o, w_out, ffn_norm)
    w_up, conv_w, w_down = comm.ffn_weights(after=merged)
    upg, upv, act = _up_act(h2, w_up, conv_w, conv_b, batch, seq)
    dx2, dx2b, loss_acc, d_final = _down_loss(act, w_down, x1, target, final_norm)

    d_wdown = _mm_tn(act, dx2b, "dw_down")
    dupg, dupv, dcwg, dcwv, dcbg, dcbv = _ffn_act_bwd(upg, upv, conv_w, conv_b, dx2b, w_down, batch, seq)
    d_wup = jnp.concatenate([_mm_tn(h2, dupg, "dw_up_gate"), _mm_tn(h2, dupv, "dw_up_val")], axis=1)
    dx1, d_ffn_norm, dmerged = _proj_bwd([(dupg, w_up, 0), (dupv, w_up, 1)], x1, ffn_norm, dx2, "up_proj_bwd",
                                         w2=w_out)
    d_wout = _mm_tn(merged, dx1, "dw_out")
    token = comm.send_ffn_grads(d_wdown, d_wup, jnp.concatenate([dcwg, dcwv], axis=1), d_wout)
    dzm, do, d_avg, d_avb, d_ws, d_bs = _mixer_bwd(zm, o, dmerged, av_g, av_b, w_s, w_st, b_col, token)
    token = comm.send_small_grads(_pack_small(dict(
        a_v_norm_g=d_avg, a_v_norm_b=d_avb, a_spatial_w=d_ws, a_spatial_b=d_bs, ffn_norm=d_ffn_norm,
        conv_b=jnp.concatenate([dcbg, dcbv], axis=1), final_norm=d_final), SMALL_EARLY))
    dq, dk, dv = _attn_bwd(q, k, v, o, do, lse, batch, seq, token)
    dzs, cqn, dqp, ckvn, dkv, d_qn, d_kvn = _mla_prep_bwd(zs, pos, invf, q_norm, kv_norm, wuq_p, wukv, dq, dk, dv)
    d_wmain = _mm_tn(h, dzm, "dw_in_main")
    d_wzs = _mm_tn(h, dzs, "dw_in_small")
    token = comm.send_in_grads(d_wmain, d_wzs)
    d_wuq_p = _mm_tn(cqn, dqp, "dw_uq", dep=token)
    d_wukv = _mm_tn(ckvn, dkv, "dw_ukv", dep=token)
    dx, d_mix_norm = _proj_bwd([(dzm, w_main, 0), (dzs, w_zs, 0)], x, mix_norm, dx1, "in_proj_bwd", dep=token)
    late = _pack_small(dict(q_a_norm=d_qn, kv_a_norm=d_kvn, mix_norm=d_mix_norm), SMALL_LATE,
                       extra=loss_acc[0, 0].reshape(1))
    token = comm.send_late_grads(d_wuq_p, d_wukv, late)
    return dx.reshape(batch, seq, D_MODEL), token


MESH_ID = pl.DeviceIdType.MESH
EFFECT = pltpu.SideEffectType.DATAFLOW_SIDE_EFFECTING


def _mesh_pos():
    return lax.axis_index("x"), lax.axis_index("y"), lax.axis_index("c")


def _peer(pos, d):
    x, y, c = pos
    px = 1 - x if d & 4 else x
    py = 1 - y if d & 2 else y
    pc = 1 - c if d & 1 else c
    return (px, py, pc), 4 * px + 2 * py + pc


def _copy(src_ref, land_ref, send_sems, recv_sems, a, d, pos, exchange, landing_here):
    peer, pid = _peer(pos, d)
    me = 4 * pos[0] + 2 * pos[1] + pos[2]
    if exchange:
        src, dst = src_ref.at[pid], land_ref.at[d]
    else:
        src, dst = src_ref, land_ref.at[pid if landing_here else me]
    return pltpu.make_async_remote_copy(
        src_ref=src, dst_ref=dst, send_sem=send_sems.at[a * (N_DEV - 1) + d - 1],
        recv_sem=recv_sems.at[a * (N_DEV - 1) + d - 1],
        device_id=peer, device_id_type=MESH_ID)


def _start_copies(groups, modes, name, dep=None):
    sizes = [len(g) for g in groups]
    srcs = [s for g in groups for s, _ in g]
    lands = [l for g in groups for _, l in g]
    n, ng = len(srcs), len(groups)
    n_in = 2 * n + (dep is not None)

    def body(*refs):
        src_refs, land_refs = refs[:n], refs[n:2 * n]
        sems = refs[n_in:n_in + 2 * ng]
        token = refs[-1]
        pos = _mesh_pos()
        k = 0
        for gi, size in enumerate(sizes):
            for a in range(size):
                for d in range(1, N_DEV):
                    _copy(src_refs[k], land_refs[k], sems[2 * gi], sems[2 * gi + 1], a, d, pos, modes[gi],
                          landing_here=False).start()
                k += 1
        token[...] = jnp.zeros_like(token)

    sem_shapes = [pltpu.SemaphoreType.DMA((size * (N_DEV - 1),)) for size in sizes for _ in range(2)]
    out = pl.pallas_call(
        body, name=name,
        out_shape=(*sem_shapes, *[pltpu.HBM(a.shape, a.dtype) for a in srcs + lands],
                   jax.ShapeDtypeStruct((8, LANES), F32)),
        in_specs=[HBM] * (2 * n) + [ANY] * (dep is not None),
        out_specs=(*[SEM] * (2 * ng), *[HBM] * (2 * n), pl.BlockSpec(memory_space=pltpu.VMEM)),
        input_output_aliases={i: 2 * ng + i for i in range(2 * n)},
        compiler_params=pltpu.CompilerParams(has_side_effects=EFFECT),
    )(*[pltpu.with_memory_space_constraint(a, pltpu.HBM) for a in srcs + lands], *([dep] if dep is not None else []))
    thru = out[2 * ng:2 * ng + 2 * n]
    handles, k = [], 0
    for gi, size in enumerate(sizes):
        handles.append((out[2 * gi], out[2 * gi + 1], thru[k:k + size], thru[n + k:n + k + size]))
        k += size
    return handles, out[-1]


def _wait_copies(handle, exchange, after, name):
    send_sems, recv_sems, srcs, lands = handle
    n = len(srcs)

    def body(*refs):
        src_refs, land_refs = refs[:n], refs[n:2 * n]
        send, recv = refs[2 * n], refs[2 * n + 1]
        pos = _mesh_pos()
        for a in range(n):
            for d in range(1, N_DEV):
                cp = _copy(src_refs[a], land_refs[a], send, recv, a, d, pos, exchange, landing_here=True)
                cp.wait_send()
                cp.wait_recv()

    out = pl.pallas_call(
        body, name=name,
        out_shape=tuple(pltpu.HBM(a.shape, a.dtype) for a in (*srcs, *lands)),
        in_specs=[HBM] * (2 * n) + [SEM, SEM, ANY], out_specs=[HBM] * (2 * n),
        input_output_aliases={i: i for i in range(2 * n)},
        compiler_params=pltpu.CompilerParams(has_side_effects=EFFECT),
    )(*srcs, *lands, send_sems, recv_sems, after)
    return out[n:]


def _land_gather(a, me):
    zone = jnp.zeros((N_DEV,) + a.shape, a.dtype)
    return lax.dynamic_update_slice(zone, a[None], (me,) + (0,) * a.ndim)


def _land_exchange(s, me):
    own = lax.dynamic_index_in_dim(s, me, 0, keepdims=True)
    return lax.dynamic_update_slice(jnp.zeros_like(s), own, (0,) * s.ndim)


def _gather_now(a, name):
    def body(x_ref, out_ref, send_sems, recv_sems, local_sem):
        x, y, c = _mesh_pos()
        me, sibling = (x, y, c), (x, y, 1 - c)
        chips = [(1 - x, y), (x, 1 - y), (1 - x, 1 - y)]

        def slot(p):
            return out_ref.at[4 * p[0] + 2 * p[1] + p[2]]

        def copy(k, block, to, src=None):
            return pltpu.make_async_remote_copy(
                src_ref=slot(block) if src is None else src, dst_ref=slot(block), send_sem=send_sems.at[k],
                recv_sem=recv_sems.at[k], device_id=to, device_id_type=MESH_ID)

        mine = pltpu.make_async_copy(x_ref, slot(me), local_sem)
        mine.start()
        first = [copy(0, me, sibling, src=x_ref)]
        first += [copy(1 + j, me, (*chip, c), src=x_ref) for j, chip in enumerate(chips)]
        for cp in first:
            cp.start()
        passed = [copy(4 + j, (*chip, c), sibling) for j, chip in enumerate(chips)]
        for j, chip in enumerate(chips):
            copy(1 + j, (*chip, c), me).wait_recv()
            passed[j].start()
        copy(0, sibling, me).wait_recv()
        for j, chip in enumerate(chips):
            copy(4 + j, (*chip, 1 - c), me).wait_recv()
        for cp in first + passed:
            cp.wait_send()
        mine.wait()

    return pl.pallas_call(
        body, in_specs=[ANY], out_specs=ANY,
        out_shape=jax.ShapeDtypeStruct((N_DEV,) + a.shape, a.dtype),
        scratch_shapes=[pltpu.SemaphoreType.DMA((N_DEV - 1,)), pltpu.SemaphoreType.DMA((N_DEV - 1,)),
                        pltpu.SemaphoreType.DMA],
        name=name, compiler_params=pltpu.CompilerParams(has_side_effects=True))(a)


def _adamw(parts, w, m, v, name):
    R, C = w.shape
    if N_DEV * R * C * parts.dtype.itemsize <= SMALL_BLOCK_BYTES:
        tr = R
    else:
        tr = next((t for t in range(min(R, 256) // 16 * 16, 15, -16) if R % t == 0), R)
    c1 = 1.0 - ADAM_B1 ** ADAM_STEP
    c2 = 1.0 - ADAM_B2 ** ADAM_STEP

    def body(p_ref, w_ref, m_ref, v_ref, g_ref, d_ref, nm_ref, nv_ref):
        g = p_ref[0].astype(F32)
        for k in range(1, N_DEV):
            g = g + p_ref[k].astype(F32)
        nm = ADAM_B1 * m_ref[...] + (1.0 - ADAM_B1) * g
        nv = ADAM_B2 * v_ref[...] + (1.0 - ADAM_B2) * (g * g)
        g_ref[...] = g
        nm_ref[...] = nm
        nv_ref[...] = nv
        d_ref[...] = -ADAM_LR * ((nm / c1) / (jnp.sqrt(nv / c2) + ADAM_EPS) + ADAM_WD * w_ref[...])

    blk = pl.BlockSpec((tr, C), lambda i: (i, 0))
    shp = jax.ShapeDtypeStruct((R, C), F32)
    return pl.pallas_call(
        body, grid=(R // tr,),
        in_specs=[pl.BlockSpec((N_DEV, tr, C), lambda i: (0, i, 0)), blk, blk, blk],
        out_specs=[blk, blk, blk, blk], out_shape=[shp, shp, shp, shp],
        name=name, compiler_params=_params("parallel"))(parts, w, m, v)


SPLIT_V = 2 * D_MODEL
SPLIT_KR = SPLIT_V + Q_LORA + KV_LORA + QK_ROPE
IN_DIM = SPLIT_KR + 2 * D_MODEL

SMALL_EARLY = ("a_v_norm_g", "a_v_norm_b", "a_spatial_w", "a_spatial_b", "ffn_norm", "conv_b", "final_norm")
SMALL_LATE = ("q_a_norm", "kv_a_norm", "mix_norm")


def _pack_rows(a):
    flat = a.reshape(-1)
    rows = -(-flat.shape[0] // LANES)
    rows8 = -(-rows // 8) * 8
    return jnp.pad(flat, (0, rows8 * LANES - flat.shape[0])).reshape(rows8, LANES)


def _pack_small(tree, names, extra=None):
    parts = [_pack_rows(tree[n]) for n in names]
    if extra is not None:
        parts.append(_pack_rows(extra))
    return jnp.concatenate(parts, axis=0)


def _unpack_small(buf, names, shapes):
    out, r = {}, 0
    for n in names:
        size = math.prod(shapes[n])
        rows8 = -(-(-(-size // LANES)) // 8) * 8
        out[n] = buf[r:r + rows8].reshape(-1)[:size].reshape(shapes[n])
        r += rows8
    return out, r


def _cols_from_shards(g):
    return jnp.transpose(g, (1, 0, 2)).reshape(g.shape[1], N_DEV * g.shape[2])


def _shards_from_cols(a):
    R, W = a.shape
    return jnp.transpose(a.reshape(R, N_DEV, W // N_DEV), (1, 0, 2))


class _Comm:
    GATHER_GROUPS = (("w_uq", "w_ukv", "w_out"), ("w_up", "conv_w", "w_down"))
    FFN_GRADS = ("w_down", "w_up", "conv_w", "w_out")
    LATE_GRADS = ("w_uq", "w_ukv")

    def __init__(self, shards, me):
        self.me = me
        local = {n: a.astype(F32 if n == "conv_w" else BF16) for n, a in shards.items()}
        self.g_in = _gather_now(local["w_in"], "gather_w_in")
        groups = [[(local[n], _land_gather(local[n], me)) for n in g] for g in self.GATHER_GROUPS]
        (self.h_mla, self.h_ffn), _ = _start_copies(groups, [False] * 2, "gather_start", dep=self.g_in)

    def in_weights(self):
        w_full = _cols_from_shards(self.g_in)
        w_main = jnp.concatenate([w_full[:, :SPLIT_V], w_full[:, SPLIT_KR:]], axis=1)
        w_zs = jnp.pad(w_full[:, SPLIT_V:SPLIT_KR], ((0, 0), (0, ZS_W - (SPLIT_KR - SPLIT_V))))
        return w_main, w_zs

    def mla_weights(self, after):
        g_uq, g_ukv, g_out = _wait_copies(self.h_mla, False, after, "gather_wait_mla")
        wuq_p = _cols_from_shards(jnp.pad(g_uq, ((0, 0), (0, 0), (0, HEAD_PAD - QK_HEAD))))
        return wuq_p, _cols_from_shards(g_ukv), g_out.reshape(D_MODEL, D_MODEL)

    def ffn_weights(self, after):
        g_up, g_cw, g_down = _wait_copies(self.h_ffn, False, after, "gather_wait_ffn")
        return _cols_from_shards(g_up), _cols_from_shards(g_cw), g_down.reshape(D_FF, D_MODEL)

    def _exchange_group(self, blocks):
        return [(s, _land_exchange(s, self.me)) for s in blocks]

    def send_ffn_grads(self, d_wdown, d_wup, d_convw, d_wout):
        group = self._exchange_group([
            d_wdown.reshape(N_DEV, D_FF // N_DEV, D_MODEL), _shards_from_cols(d_wup), _shards_from_cols(d_convw),
            d_wout.reshape(N_DEV, D_MODEL // N_DEV, D_MODEL)])
        (self.h_ffn_grads,), token = _start_copies([group], [True], "ffn_grads_start")
        return token

    def send_small_grads(self, packed):
        (self.h_small_early,), token = _start_copies(
            [[(packed, _land_gather(packed, self.me))]], [False], "small_grads_start")
        return token

    def send_in_grads(self, d_wmain, d_wzs):
        d_in = jnp.concatenate([d_wmain[:, :SPLIT_V], d_wzs[:, :SPLIT_KR - SPLIT_V], d_wmain[:, SPLIT_V:]], axis=1)
        (self.h_in_grads,), token = _start_copies(
            [self._exchange_group([_shards_from_cols(d_in)])], [True], "in_grads_start")
        return token

    def send_late_grads(self, d_wuq_p, d_wukv, packed):
        group = self._exchange_group([_shards_from_cols(d_wuq_p)[:, :, :QK_HEAD], _shards_from_cols(d_wukv)])
        small = [(packed, _land_gather(packed, self.me))]
        (self.h_late_grads, self.h_late_small), token = _start_copies(
            [group, small], [True, False], "late_grads_start")
        return token


def kernel(x, positions, mix_norm, w_in, a_v_norm_g, a_v_norm_b, a_spatial_w, a_spatial_b, q_a_norm, w_uq, kv_a_norm, w_ukv, w_out, ffn_norm, w_up, conv_w, conv_b, w_down, final_norm, loss_target, m_mix_norm, m_w_in, m_a_v_norm_g, m_a_v_norm_b, m_a_spatial_w, m_a_spatial_b, m_q_a_norm, m_w_uq, m_kv_a_norm, m_w_ukv, m_w_out, m_ffn_norm, m_w_up, m_conv_w, m_conv_b, m_w_down, m_final_norm, v_mix_norm, v_w_in, v_a_v_norm_g, v_a_v_norm_b, v_a_spatial_w, v_a_spatial_b, v_q_a_norm, v_w_uq, v_kv_a_norm, v_w_ukv, v_w_out, v_ffn_norm, v_w_up, v_conv_w, v_conv_b, v_w_down, v_final_norm):
    names = ("mix_norm", "w_in", "a_v_norm_g", "a_v_norm_b", "a_spatial_w", "a_spatial_b", "q_a_norm", "w_uq",
             "kv_a_norm", "w_ukv", "w_out", "ffn_norm", "w_up", "conv_w", "conv_b", "w_down", "final_norm")
    w = dict(zip(names, (mix_norm, w_in, a_v_norm_g, a_v_norm_b, a_spatial_w, a_spatial_b, q_a_norm, w_uq,
                         kv_a_norm, w_ukv, w_out, ffn_norm, w_up, conv_w, conv_b, w_down, final_norm)))
    m = dict(zip(names, (m_mix_norm, m_w_in, m_a_v_norm_g, m_a_v_norm_b, m_a_spatial_w, m_a_spatial_b,
                         m_q_a_norm, m_w_uq, m_kv_a_norm, m_w_ukv, m_w_out, m_ffn_norm, m_w_up, m_conv_w,
                         m_conv_b, m_w_down, m_final_norm)))
    v = dict(zip(names, (v_mix_norm, v_w_in, v_a_v_norm_g, v_a_v_norm_b, v_a_spatial_w, v_a_spatial_b,
                         v_q_a_norm, v_w_uq, v_kv_a_norm, v_w_ukv, v_w_out, v_ffn_norm, v_w_up, v_conv_w,
                         v_conv_b, v_w_down, v_final_norm)))
    shapes = {n: w[n].shape for n in names}
    me = 4 * lax.axis_index("x") + 2 * lax.axis_index("y") + lax.axis_index("c")
    comm = _Comm({n: w[n][0] for n in ("w_in",) + _Comm.GATHER_GROUPS[0] + _Comm.GATHER_GROUPS[1]}, me)

    grad_x, token = _local_step(
        x, positions, loss_target, w["mix_norm"], w["a_v_norm_g"], w["a_v_norm_b"], w["a_spatial_w"][0],
        w["a_spatial_b"][0], w["q_a_norm"], w["kv_a_norm"], w["ffn_norm"], w["conv_b"],
        w["final_norm"].reshape(1, D_MODEL), comm)

    out_g, out_d, out_m, out_v = {}, {}, {}, {}

    def update(n, parts):
        shp = shapes[n]
        r2 = (shp[-2], shp[-1])
        res = _adamw(parts, w[n].reshape(r2), m[n].reshape(r2), v[n].reshape(r2), "adamw_" + n)
        out_g[n], out_d[n], out_m[n], out_v[n] = (t.reshape(shp) for t in res)

    for n, parts in zip(_Comm.FFN_GRADS, _wait_copies(comm.h_ffn_grads, True, token, "ffn_grads_wait")):
        update(n, parts)
    update("w_in", _wait_copies(comm.h_in_grads, True, out_d["w_up"], "in_grads_wait")[0])
    for n, parts in zip(_Comm.LATE_GRADS, _wait_copies(comm.h_late_grads, True, out_d["w_in"], "late_grads_wait")):
        update(n, parts)

    (early_parts,) = _wait_copies(comm.h_small_early, False, out_d["w_uq"], "small_grads_wait")
    res = _adamw(early_parts, _pack_small(w, SMALL_EARLY), _pack_small(m, SMALL_EARLY), _pack_small(v, SMALL_EARLY),
                 "adamw_small")
    unpacked = [_unpack_small(t, SMALL_EARLY, shapes)[0] for t in res]
    for n in SMALL_EARLY:
        out_g[n], out_d[n], out_m[n], out_v[n] = (u[n] for u in unpacked)

    (late_parts,) = _wait_copies(comm.h_late_small, False, res[1], "late_small_wait")
    zero = jnp.zeros((1,), F32)
    res = _adamw(late_parts, _pack_small(w, SMALL_LATE, extra=zero), _pack_small(m, SMALL_LATE, extra=zero),
                 _pack_small(v, SMALL_LATE, extra=zero), "adamw_late")
    unpacked = [_unpack_small(t, SMALL_LATE, shapes) for t in res]
    for n in SMALL_LATE:
        out_g[n], out_d[n], out_m[n], out_v[n] = (u[0][n] for u in unpacked)
    loss = res[0][unpacked[0][1], 0]

    return (loss, grad_x, *[out_g[n] for n in names], *[out_d[n] for n in names],
            *[out_m[n] for n in names], *[out_v[n] for n in names])
```

```python
import functools
import math

import jax
import jax.numpy as jnp
from jax import lax
from jax.experimental import pallas as pl
from jax.experimental.pallas import tpu as pltpu

F32 = jnp.float32
BF16 = jnp.bfloat16

N_DEV = 8
D_MODEL = 1024
EPS = 1e-6
A_GROUPS = 8
CHUNK = 128
MLA_HEADS = 8
QK_NOPE = 128
QK_ROPE = 64
QK_HEAD = QK_NOPE + QK_ROPE
HEAD_PAD = 256
V_HEAD = 128
Q_LORA = 256
KV_LORA = 128
ROPE_THETA = 10000.0
D_FF = 2816
ZS_W = 512
ATTN_SCALE = QK_HEAD ** -0.5
NEG_BIG = -1e30

ADAM_LR = 0.001
ADAM_B1 = 0.9
ADAM_B2 = 0.999
ADAM_EPS = 1e-08
ADAM_WD = 0.01
ADAM_STEP = 10

VMEM_LIMIT = 56 * 1024 * 1024
SMALL_BLOCK_BYTES = 5 * 1024 * 1024
LANES = 128

GELU_K = math.sqrt(2.0 / math.pi)
GELU_C = 0.044715

ANY = pl.BlockSpec(memory_space=pl.ANY)
HBM = pl.BlockSpec(memory_space=pltpu.HBM)
SEM = pl.BlockSpec(memory_space=pltpu.SEMAPHORE)


def _tile(n, pref):
    for t in (pref, 512, 256, 128, 64, 32, 16, 8):
        if t <= pref and n % t == 0:
            return t
    return n


def _wide_tile(n, cap=1408):
    return next((t for t in range(min(n, cap) // LANES * LANES, 0, -LANES) if n % t == 0), n)


def _params(*sem):
    return pltpu.CompilerParams(dimension_semantics=sem, vmem_limit_bytes=VMEM_LIMIT)


def _dot(a, b):
    return jnp.dot(a, b, preferred_element_type=F32)


def _dot_nt(a, b):
    return lax.dot_general(a, b, (((1,), (1,)), ((), ())), preferred_element_type=F32)


def _dot_tn(a, b):
    return lax.dot_general(a, b, (((0,), (0,)), ((), ())), preferred_element_type=F32)


def _sigmoid(x):
    return 1.0 / (1.0 + jnp.exp(-x))


def _gelu(x):
    t = jnp.tanh(GELU_K * (x + GELU_C * x * x * x))
    return 0.5 * x * (1.0 + t)


def _gelu_grad(x):
    t = jnp.tanh(GELU_K * (x + GELU_C * x * x * x))
    return 0.5 * (1.0 + t) + 0.5 * x * (1.0 - t * t) * GELU_K * (1.0 + 3.0 * GELU_C * x * x)


def _in_proj(x, g, wt):
    T, Dm = x.shape
    tm = _tile(T, 256)

    def body(x_ref, g_ref, wt_ref, h_ref, zm_ref, zs_ref):
        xf = x_ref[...]
        r = lax.rsqrt(jnp.mean(xf * xf, axis=-1, keepdims=True) + EPS)
        h = (xf * r * g_ref[...]).astype(BF16)
        h_ref[...] = h
        for i, (r0, r1) in enumerate(IN_ROWS_MAIN):
            zm_ref[:, i * D_MODEL:(i + 1) * D_MODEL] = _dot_nt(h, wt_ref[r0:r1, :])
        zs_ref[...] = _dot_nt(h, wt_ref[IN_ROWS_ZS[0]:IN_ROWS_ZS[1], :])

    row = lambda n: pl.BlockSpec((tm, n), lambda i: (i, 0))
    return pl.pallas_call(
        body, grid=(T // tm,),
        in_specs=[row(Dm), pl.BlockSpec((1, Dm), lambda i: (0, 0)), pl.BlockSpec(wt.shape, lambda i: (0, 0))],
        out_specs=[row(Dm), row(4 * D_MODEL), row(ZS_W)],
        out_shape=[jax.ShapeDtypeStruct((T, Dm), BF16), jax.ShapeDtypeStruct((T, 4 * D_MODEL), F32),
                   jax.ShapeDtypeStruct((T, ZS_W), F32)],
        name="in_proj", compiler_params=_params("parallel"))(x, g, wt)


def _proj_bwd(acts, wt, terms, x, g, dres, name, w2=None, dep=None):
    T, Dm = x.shape
    tm = _tile(T, 256)
    n_a = len(acts)

    def body(*refs):
        ins, outs = refs[:n_a + 4 + (w2 is not None) + (dep is not None)], refs[-2 - (w2 is not None):]
        wt_ref, x_ref, g_ref, dres_ref = ins[n_a:n_a + 4]
        dx_ref, dg_ref = outs[0], outs[1]

        @pl.when(pl.program_id(0) == 0)
        def _():
            dg_ref[...] = jnp.zeros_like(dg_ref)

        dy = None
        for i, (c0, c1), (r0, r1) in terms:
            t = _dot(ins[i][:, c0:c1], wt_ref[r0:r1, :])
            dy = t if dy is None else dy + t
        xf = x_ref[...]
        r = lax.rsqrt(jnp.mean(xf * xf, axis=-1, keepdims=True) + EPS)
        xh = xf * r
        dg_ref[...] += jnp.sum(dy * xh, axis=0, keepdims=True)
        dxh = dy * g_ref[...]
        dx = dres_ref[...] + r * (dxh - xh * jnp.mean(dxh * xh, axis=-1, keepdims=True))
        dx_ref[...] = dx
        if w2 is not None:
            outs[2][...] = _dot_nt(dx.astype(BF16), ins[n_a + 4][...])

    row = pl.BlockSpec((tm, Dm), lambda i: (i, 0))
    vec = pl.BlockSpec((1, Dm), lambda i: (0, 0))
    in_specs = [pl.BlockSpec((tm, a.shape[1]), lambda i: (i, 0)) for a in acts]
    in_specs += [pl.BlockSpec(wt.shape, lambda i: (0, 0)), row, vec, row]
    args = [*acts, wt, x, g, dres]
    out_specs = [row, vec]
    out_shape = [jax.ShapeDtypeStruct((T, Dm), F32), jax.ShapeDtypeStruct((1, Dm), F32)]
    if w2 is not None:
        in_specs.append(pl.BlockSpec(w2.shape, lambda i: (0, 0)))
        args.append(w2)
        out_specs.append(pl.BlockSpec((tm, w2.shape[0]), lambda i: (i, 0)))
        out_shape.append(jax.ShapeDtypeStruct((T, w2.shape[0]), F32))
    if dep is not None:
        in_specs.append(ANY)
        args.append(dep)
    return pl.pallas_call(
        body, grid=(T // tm,), in_specs=in_specs, out_specs=out_specs, out_shape=out_shape,
        name=name, compiler_params=_params("arbitrary"))(*args)


def _mm_tn(a, b, name, dep=None):
    T, M = a.shape
    N = b.shape[1]
    tm, tn, tt = _wide_tile(M), _wide_tile(N), _tile(T, 512)
    n_t = T // tt

    def body(a_ref, b_ref, *refs):
        o_ref, acc_ref = refs[-2:]
        t = pl.program_id(2)

        @pl.when(t == 0)
        def _():
            acc_ref[...] = jnp.zeros_like(acc_ref)

        acc_ref[...] += _dot_tn(a_ref[...].astype(BF16), b_ref[...].astype(BF16))

        @pl.when(t == n_t - 1)
        def _():
            o_ref[...] = acc_ref[...].astype(BF16)

    return pl.pallas_call(
        body, grid=(M // tm, N // tn, n_t),
        in_specs=[pl.BlockSpec((tt, tm), lambda i, j, t: (t, i)),
                  pl.BlockSpec((tt, tn), lambda i, j, t: (t, j))] + [ANY] * (dep is not None),
        out_specs=pl.BlockSpec((tm, tn), lambda i, j, t: (i, j)),
        out_shape=jax.ShapeDtypeStruct((M, N), BF16),
        scratch_shapes=[pltpu.VMEM((tm, tn), F32)],
        name=name, compiler_params=_params("parallel", "parallel", "arbitrary"))(
            a, b, *([dep] if dep is not None else []))


def _layer_norm_fwd(gv, g, b):
    mu = jnp.mean(gv, axis=-1, keepdims=True)
    xc = gv - mu
    rs = lax.rsqrt(jnp.mean(xc * xc, axis=-1, keepdims=True) + EPS)
    xh = xc * rs
    return xh, rs, xh * g + b


def _tri_mask(transposed=False):
    r = lax.broadcasted_iota(jnp.int32, (CHUNK, CHUNK), 0)
    c = lax.broadcasted_iota(jnp.int32, (CHUNK, CHUNK), 1)
    return r <= c if transposed else c <= r


def _mixer_a_fwd(zm, av_g, av_b, w_s, b_col):
    T = zm.shape[0]
    tm = _tile(T, 256)
    n_chunk = tm // CHUNK

    def body(u_ref, v_ref, ga_ref, g_ref, b_ref, w_ref, bc_ref, y_ref, vn_s, mx_s):
        gu = _gelu(u_ref[...])
        _, _, vn = _layer_norm_fwd(_gelu(v_ref[...]), g_ref[...], b_ref[...])
        vn_s[...] = vn.astype(BF16)
        tri = _tri_mask()
        for gi in range(A_GROUPS):
            wm = jnp.where(tri, w_ref[gi], 0.0).astype(BF16)
            cols = slice(gi * CHUNK, (gi + 1) * CHUNK)
            for n in range(n_chunk):
                rows = slice(n * CHUNK, (n + 1) * CHUNK)
                mx_s[rows, cols] = _dot(wm, vn_s[rows, cols]) + bc_ref[gi]
        y_ref[...] = _sigmoid(ga_ref[...]) * gu * mx_s[...]

    col = lambda c: pl.BlockSpec((tm, D_MODEL), lambda i: (i, c))
    vec = pl.BlockSpec((1, D_MODEL), lambda i: (0, 0))
    return pl.pallas_call(
        body, grid=(T // tm,),
        in_specs=[col(0), col(1), col(2), vec, vec,
                  pl.BlockSpec((A_GROUPS, CHUNK, CHUNK), lambda i: (0, 0, 0)),
                  pl.BlockSpec((A_GROUPS, CHUNK, 1), lambda i: (0, 0, 0))],
        out_specs=pl.BlockSpec((tm, D_MODEL), lambda i: (i, 0)),
        out_shape=jax.ShapeDtypeStruct((T, D_MODEL), F32),
        scratch_shapes=[pltpu.VMEM((tm, D_MODEL), BF16), pltpu.VMEM((tm, D_MODEL), F32)],
        name="mixer_a_fwd", compiler_params=_params("parallel"))(zm, zm, zm, av_g, av_b, w_s, b_col)


def _mixer_bwd(zm, o, dm, av_g, av_b, w_s, w_st, b_col, dep):
    T = zm.shape[0]
    tm = _tile(T, 256)
    n_chunk = tm // CHUNK

    def body(u_ref, v_ref, ga_ref, gb_ref, o_ref, dm_ref, g_ref, b_ref, w_ref, wt_ref, bc_ref, dep_ref,
             dz_ref, do_ref, dg_ref, db_ref, dw_ref, dbs_ref, vn_s, mx_s, dmx_s, dvn_s):
        @pl.when(pl.program_id(0) == 0)
        def _():
            dg_ref[...] = jnp.zeros_like(dg_ref)
            db_ref[...] = jnp.zeros_like(db_ref)
            dw_ref[...] = jnp.zeros_like(dw_ref)
            dbs_ref[...] = jnp.zeros_like(dbs_ref)

        dm_v = dm_ref[...]
        gb = gb_ref[...]
        sb = _sigmoid(gb)
        o_v = o_ref[...]
        do_ref[...] = (dm_v * sb).astype(BF16)
        dz_ref[:, 3 * D_MODEL:4 * D_MODEL] = (dm_v * o_v * sb * (1.0 - sb)).astype(BF16)
        u = u_ref[...]
        v = v_ref[...]
        gu = _gelu(u)
        xh, rs, vn = _layer_norm_fwd(_gelu(v), g_ref[...], b_ref[...])
        vn_s[...] = vn.astype(BF16)
        tri = _tri_mask()
        for gi in range(A_GROUPS):
            wm = jnp.where(tri, w_ref[gi], 0.0).astype(BF16)
            cols = slice(gi * CHUNK, (gi + 1) * CHUNK)
            for n in range(n_chunk):
                rows = slice(n * CHUNK, (n + 1) * CHUNK)
                mx_s[rows, cols] = _dot(wm, vn_s[rows, cols]) + bc_ref[gi]
        mixed = mx_s[...]
        sa = _sigmoid(ga_ref[...])
        dya = dm_v * sa
        dz_ref[:, 2 * D_MODEL:3 * D_MODEL] = (dm_v * gu * mixed * sa * (1.0 - sa)).astype(BF16)
        dz_ref[:, 0:D_MODEL] = (dya * mixed * _gelu_grad(u)).astype(BF16)
        dmx = dya * gu
        dmx_s[...] = dmx.astype(BF16)
        tri_t = _tri_mask(transposed=True)
        for gi in range(A_GROUPS):
            wmt = jnp.where(tri_t, wt_ref[gi], 0.0).astype(BF16)
            cols = slice(gi * CHUNK, (gi + 1) * CHUNK)
            dw_acc = jnp.zeros((CHUNK, CHUNK), F32)
            dmx_sum = jnp.zeros((CHUNK, CHUNK), F32)
            for n in range(n_chunk):
                rows = slice(n * CHUNK, (n + 1) * CHUNK)
                blk = dmx_s[rows, cols]
                dvn_s[rows, cols] = _dot(wmt, blk)
                dw_acc = dw_acc + _dot_nt(blk, vn_s[rows, cols])
                dmx_sum = dmx_sum + dmx[rows, cols]
            dw_ref[gi] += jnp.where(tri, dw_acc, 0.0)
            dbs_ref[gi] += jnp.sum(dmx_sum, axis=-1, keepdims=True)
        dvn = dvn_s[...]
        dg_ref[...] += jnp.sum(dvn * xh, axis=0, keepdims=True)
        db_ref[...] += jnp.sum(dvn, axis=0, keepdims=True)
        dxh = dvn * g_ref[...]
        dgv = rs * (dxh - jnp.mean(dxh, axis=-1, keepdims=True)
                    - xh * jnp.mean(dxh * xh, axis=-1, keepdims=True))
        dz_ref[:, D_MODEL:2 * D_MODEL] = (dgv * _gelu_grad(v)).astype(BF16)

    col = lambda c: pl.BlockSpec((tm, D_MODEL), lambda i: (i, c))
    row = pl.BlockSpec((tm, D_MODEL), lambda i: (i, 0))
    vec = pl.BlockSpec((1, D_MODEL), lambda i: (0, 0))
    wsp = pl.BlockSpec((A_GROUPS, CHUNK, CHUNK), lambda i: (0, 0, 0))
    bsp = pl.BlockSpec((A_GROUPS, CHUNK, 1), lambda i: (0, 0, 0))
    return pl.pallas_call(
        body, grid=(T // tm,),
        in_specs=[col(0), col(1), col(2), col(3), row, row, vec, vec, wsp, wsp, bsp, ANY],
        out_specs=[pl.BlockSpec((tm, 4 * D_MODEL), lambda i: (i, 0)), row, vec, vec, wsp, bsp],
        out_shape=[jax.ShapeDtypeStruct((T, 4 * D_MODEL), BF16), jax.ShapeDtypeStruct((T, D_MODEL), BF16),
                   jax.ShapeDtypeStruct((1, D_MODEL), F32), jax.ShapeDtypeStruct((1, D_MODEL), F32),
                   jax.ShapeDtypeStruct((A_GROUPS, CHUNK, CHUNK), F32),
                   jax.ShapeDtypeStruct((A_GROUPS, CHUNK, 1), F32)],
        scratch_shapes=[pltpu.VMEM((tm, D_MODEL), BF16), pltpu.VMEM((tm, D_MODEL), F32),
                        pltpu.VMEM((tm, D_MODEL), BF16), pltpu.VMEM((tm, D_MODEL), F32)],
        name="mixer_bwd", compiler_params=_params("arbitrary"))(
            zm, zm, zm, zm, o, dm, av_g, av_b, w_s, w_st, b_col, dep)


def _rope_tables(pos_ref, invf_ref):
    ang = pos_ref[...].astype(F32) * invf_ref[...]
    lane = lax.broadcasted_iota(jnp.int32, ang.shape, 1)
    cos, sin = jnp.cos(ang), jnp.sin(ang)
    c = jnp.where(lane < QK_ROPE, cos, 0.0)
    sa = jnp.where(lane < QK_ROPE // 2, -sin, 0.0)
    sb = jnp.where((lane >= QK_ROPE // 2) & (lane < QK_ROPE), sin, 0.0)
    return c, sa, sb


def _rope(blk, tabs):
    c, sa, sb = tabs
    return blk * c + pltpu.roll(blk, LANES - QK_ROPE // 2, 1) * sa + pltpu.roll(blk, QK_ROPE // 2, 1) * sb


def _rope_t(dout, tabs):
    c, sa, sb = tabs
    return dout * c + pltpu.roll(dout * sa, QK_ROPE // 2, 1) + pltpu.roll(dout * sb, LANES - QK_ROPE // 2, 1)


def _rms_small(x, g):
    r = lax.rsqrt(jnp.mean(x * x, axis=-1, keepdims=True) + EPS)
    xh = x * r
    return xh, r, xh * g


def _mla_prep_fwd(zs, pos, invf, qg, kvg, wuq_p, wukv):
    T = zs.shape[0]
    tm = _tile(T, 512)
    HW = MLA_HEADS * HEAD_PAD

    def body(zs_ref, pos_ref, invf_ref, qg_ref, kvg_ref, wq_ref, wkv_ref, q_ref, k_ref, v_ref):
        tabs = _rope_tables(pos_ref, invf_ref)
        _, _, cqn = _rms_small(zs_ref[:, 0:Q_LORA], qg_ref[...])
        _, _, ckvn = _rms_small(zs_ref[:, Q_LORA:Q_LORA + KV_LORA], kvg_ref[...])
        q = _dot_nt(cqn.astype(BF16), wq_ref[...])
        kv = _dot(ckvn.astype(BF16), wkv_ref[...])
        kr = _rope(zs_ref[:, Q_LORA + KV_LORA:ZS_W], tabs).astype(BF16)
        for h in range(MLA_HEADS):
            b0 = h * HEAD_PAD
            q_ref[:, b0:b0 + QK_NOPE] = q[:, b0:b0 + QK_NOPE].astype(BF16)
            q_ref[:, b0 + QK_NOPE:b0 + HEAD_PAD] = _rope(q[:, b0 + QK_NOPE:b0 + HEAD_PAD], tabs).astype(BF16)
            k_ref[:, b0:b0 + QK_NOPE] = kv[:, b0:b0 + QK_NOPE].astype(BF16)
            k_ref[:, b0 + QK_NOPE:b0 + HEAD_PAD] = kr
            v_ref[:, h * V_HEAD:(h + 1) * V_HEAD] = kv[:, b0 + QK_NOPE:b0 + HEAD_PAD].astype(BF16)

    full = lambda a: pl.BlockSpec(a.shape, lambda i: (0,) * a.ndim)
    return pl.pallas_call(
        body, grid=(T // tm,),
        in_specs=[pl.BlockSpec((tm, ZS_W), lambda i: (i, 0)), pl.BlockSpec((tm, 1), lambda i: (i, 0)),
                  full(invf), full(qg), full(kvg), full(wuq_p), full(wukv)],
        out_specs=[pl.BlockSpec((tm, HW), lambda i: (i, 0)), pl.BlockSpec((tm, HW), lambda i: (i, 0)),
                   pl.BlockSpec((tm, D_MODEL), lambda i: (i, 0))],
        out_shape=[jax.ShapeDtypeStruct((T, HW), BF16), jax.ShapeDtypeStruct((T, HW), BF16),
                   jax.ShapeDtypeStruct((T, D_MODEL), BF16)],
        name="mla_prep_fwd", compiler_params=_params("parallel"))(zs, pos, invf, qg, kvg, wuq_p, wukv)


def _mla_prep_bwd(zs, pos, invf, qg, kvg, wuq_p, wukv, dq, dk, dv):
    T = zs.shape[0]
    tm = _tile(T, 256)
    HW = MLA_HEADS * HEAD_PAD

    def body(zs_ref, pos_ref, invf_ref, qg_ref, kvg_ref, wq_ref, wkv_ref, dq_ref, dk_ref, dv_ref,
             dzs_ref, cqn_ref, dqp_ref, ckvn_ref, dkv_ref, dqg_ref, dkvg_ref):
        @pl.when(pl.program_id(0) == 0)
        def _():
            dqg_ref[...] = jnp.zeros_like(dqg_ref)
            dkvg_ref[...] = jnp.zeros_like(dkvg_ref)

        tabs = _rope_tables(pos_ref, invf_ref)
        cqh, rq, cqn = _rms_small(zs_ref[:, 0:Q_LORA], qg_ref[...])
        ckvh, rkv, ckvn = _rms_small(zs_ref[:, Q_LORA:Q_LORA + KV_LORA], kvg_ref[...])
        cqn_ref[...] = cqn.astype(BF16)
        ckvn_ref[...] = ckvn.astype(BF16)
        dkr = jnp.zeros((tm, LANES), F32)
        for h in range(MLA_HEADS):
            b0 = h * HEAD_PAD
            dqp_ref[:, b0:b0 + QK_NOPE] = dq_ref[:, b0:b0 + QK_NOPE].astype(BF16)
            dqp_ref[:, b0 + QK_NOPE:b0 + HEAD_PAD] = _rope_t(dq_ref[:, b0 + QK_NOPE:b0 + HEAD_PAD], tabs).astype(BF16)
            dkv_ref[:, b0:b0 + QK_NOPE] = dk_ref[:, b0:b0 + QK_NOPE].astype(BF16)
            dkv_ref[:, b0 + QK_NOPE:b0 + HEAD_PAD] = dv_ref[:, h * V_HEAD:(h + 1) * V_HEAD].astype(BF16)
            dkr = dkr + dk_ref[:, b0 + QK_NOPE:b0 + HEAD_PAD]
        dcqn = _dot(dqp_ref[...], wq_ref[...])
        dckvn = _dot_nt(dkv_ref[...], wkv_ref[...])
        dqg_ref[...] += jnp.sum(dcqn * cqh, axis=0, keepdims=True)
        dkvg_ref[...] += jnp.sum(dckvn * ckvh, axis=0, keepdims=True)
        dxh = dcqn * qg_ref[...]
        dzs_ref[:, 0:Q_LORA] = (rq * (dxh - cqh * jnp.mean(dxh * cqh, axis=-1, keepdims=True))).astype(BF16)
        dxh = dckvn * kvg_ref[...]
        dzs_ref[:, Q_LORA:Q_LORA + KV_LORA] = (
            rkv * (dxh - ckvh * jnp.mean(dxh * ckvh, axis=-1, keepdims=True))).astype(BF16)
        dzs_ref[:, Q_LORA + KV_LORA:ZS_W] = _rope_t(dkr, tabs).astype(BF16)

    full = lambda a: pl.BlockSpec(a.shape, lambda i: (0,) * a.ndim)
    rowb = lambda w: pl.BlockSpec((tm, w), lambda i: (i, 0))
    return pl.pallas_call(
        body, grid=(T // tm,),
        in_specs=[rowb(ZS_W), rowb(1), full(invf), full(qg), full(kvg), full(wuq_p), full(wukv),
                  rowb(HW), rowb(HW), rowb(D_MODEL)],
        out_specs=[rowb(ZS_W), rowb(Q_LORA), rowb(HW), rowb(KV_LORA), rowb(HW), full(qg), full(kvg)],
        out_shape=[jax.ShapeDtypeStruct((T, ZS_W), BF16), jax.ShapeDtypeStruct((T, Q_LORA), BF16),
                   jax.ShapeDtypeStruct((T, HW), BF16), jax.ShapeDtypeStruct((T, KV_LORA), BF16),
                   jax.ShapeDtypeStruct((T, HW), BF16), jax.ShapeDtypeStruct(qg.shape, F32),
                   jax.ShapeDtypeStruct(kvg.shape, F32)],
        name="mla_prep_bwd", compiler_params=_params("arbitrary"))(
            zs, pos, invf, qg, kvg, wuq_p, wukv, dq, dk, dv)


def _causal(tq, kmax, q0):
    r = lax.broadcasted_iota(jnp.int32, (tq, kmax), 0) + q0
    c = lax.broadcasted_iota(jnp.int32, (tq, kmax), 1)
    return c <= r


def _attn_fwd(q, k, v, batch, seq):
    tq = _tile(seq, 512)
    nq = seq // tq

    def body(q_ref, k_ref, v_ref, o_ref, lse_ref):
        for qi in range(nq):
            rows = slice(qi * tq, (qi + 1) * tq)
            kmax = (qi + 1) * tq
            s = _dot_nt(q_ref[rows, :], k_ref[0:kmax, :]) * ATTN_SCALE
            s = jnp.where(_causal(tq, kmax, qi * tq), s, NEG_BIG)
            m = jnp.max(s, axis=-1, keepdims=True)
            p = jnp.exp(s - m)
            l = jnp.sum(p, axis=-1, keepdims=True)
            o_ref[rows, :] = _dot(p.astype(BF16), v_ref[0:kmax, :]) / l
            lse_ref[rows, :] = jnp.broadcast_to(m + jnp.log(l), (tq, V_HEAD))

    return pl.pallas_call(
        body, grid=(batch, MLA_HEADS),
        in_specs=[pl.BlockSpec((seq, HEAD_PAD), lambda b, h: (b, h)),
                  pl.BlockSpec((seq, HEAD_PAD), lambda b, h: (b, h)),
                  pl.BlockSpec((seq, V_HEAD), lambda b, h: (b, h))],
        out_specs=[pl.BlockSpec((seq, V_HEAD), lambda b, h: (b, h)),
                   pl.BlockSpec((seq, V_HEAD), lambda b, h: (b, h))],
        out_shape=[jax.ShapeDtypeStruct((batch * seq, D_MODEL), F32),
                   jax.ShapeDtypeStruct((batch * seq, D_MODEL), F32)],
        name="attn_fwd", compiler_params=_params("parallel", "parallel"))(q, k, v)


def _attn_bwd(q, k, v, o, do, lse, batch, seq, dep):
    tq = _tile(seq, 512)
    nq = seq // tq

    def body(q_ref, k_ref, v_ref, o_ref, do_ref, lse_ref, dep_ref, dq_ref, dk_ref, dv_ref):
        dk_ref[...] = jnp.zeros_like(dk_ref)
        dv_ref[...] = jnp.zeros_like(dv_ref)
        for qi in range(nq):
            rows = slice(qi * tq, (qi + 1) * tq)
            kmax = (qi + 1) * tq
            qr = q_ref[rows, :]
            dor = do_ref[rows, :]
            kk = k_ref[0:kmax, :]
            s = _dot_nt(qr, kk) * ATTN_SCALE
            p = jnp.where(_causal(tq, kmax, qi * tq), jnp.exp(s - lse_ref[rows, 0:1]), 0.0)
            dp = _dot_nt(dor, v_ref[0:kmax, :])
            delta = jnp.sum(dor.astype(F32) * o_ref[rows, :], axis=-1, keepdims=True)
            ds = (p * (dp - delta) * ATTN_SCALE).astype(BF16)
            dq_ref[rows, :] = _dot(ds, kk)
            dk_ref[0:kmax, :] += _dot_tn(ds, qr)
            dv_ref[0:kmax, :] += _dot_tn(p.astype(BF16), dor)

    qspec = pl.BlockSpec((seq, HEAD_PAD), lambda b, h: (b, h))
    vspec = pl.BlockSpec((seq, V_HEAD), lambda b, h: (b, h))
    T = batch * seq
    return pl.pallas_call(
        body, grid=(batch, MLA_HEADS),
        in_specs=[qspec, qspec, vspec, vspec, vspec, vspec, ANY],
        out_specs=[qspec, qspec, vspec],
        out_shape=[jax.ShapeDtypeStruct((T, MLA_HEADS * HEAD_PAD), F32),
                   jax.ShapeDtypeStruct((T, MLA_HEADS * HEAD_PAD), F32),
                   jax.ShapeDtypeStruct((T, D_MODEL), F32)],
        name="attn_bwd", compiler_params=_params("parallel", "parallel"))(q, k, v, o, do, lse, dep)


def _merge_out(x, yag, zm, o, w_out, ffn_g):
    T = x.shape[0]
    tm = _tile(T, 512)

    def body(x_ref, ya_ref, gb_ref, o_ref, w_ref, g_ref, mg_ref, x1_ref, h2_ref):
        mg = (ya_ref[...] + _sigmoid(gb_ref[...]) * o_ref[...]).astype(BF16)
        mg_ref[...] = mg
        x1 = x_ref[...] + _dot(mg, w_ref[...])
        x1_ref[...] = x1
        r = lax.rsqrt(jnp.mean(x1 * x1, axis=-1, keepdims=True) + EPS)
        h2_ref[...] = (x1 * r * g_ref[...]).astype(BF16)

    row = pl.BlockSpec((tm, D_MODEL), lambda i: (i, 0))
    return pl.pallas_call(
        body, grid=(T // tm,),
        in_specs=[row, row, pl.BlockSpec((tm, D_MODEL), lambda i: (i, 3)), row,
                  pl.BlockSpec((D_MODEL, D_MODEL), lambda i: (0, 0)), pl.BlockSpec((1, D_MODEL), lambda i: (0, 0))],
        out_specs=[row, row, row],
        out_shape=[jax.ShapeDtypeStruct((T, D_MODEL), BF16), jax.ShapeDtypeStruct((T, D_MODEL), F32),
                   jax.ShapeDtypeStruct((T, D_MODEL), BF16)],
        name="merge_out", compiler_params=_params("parallel"))(x, yag, zm, o, w_out, ffn_g)


FF_TILE = 256
FF_BLOCKS = D_FF // FF_TILE


def _shift_down(x, k):
    row = lax.broadcasted_iota(jnp.int32, x.shape, 0)
    return jnp.where(row >= k, pltpu.roll(x, k, 0), 0.0)


def _shift_up(x, k):
    n = x.shape[0]
    row = lax.broadcasted_iota(jnp.int32, x.shape, 0)
    return jnp.where(row < n - k, pltpu.roll(x, n - k, 0), 0.0)


def _conv(x, w_ref, b_ref):
    return b_ref[...] + w_ref[2:3, :] * x + w_ref[1:2, :] * _shift_down(x, 1) + w_ref[0:1, :] * _shift_down(x, 2)


def _up_act(h2, w_up, cw, cb, batch, seq):
    def body(h_ref, wug_ref, wuv_ref, wg_ref, wv_ref, bg_ref, bv_ref, ug_ref, uv_ref, a_ref):
        h = h_ref[...]
        ug = _dot_nt(h, wug_ref[...])
        uv = _dot_nt(h, wuv_ref[...])
        ug_ref[...] = ug
        uv_ref[...] = uv
        gate = _conv(ug, wg_ref, bg_ref)
        val = _conv(uv, wv_ref, bv_ref)
        a_ref[...] = (gate * _sigmoid(gate) * val).astype(BF16)

    blk = pl.BlockSpec((seq, FF_TILE), lambda b, j: (b, j))
    wup = lambda off: pl.BlockSpec((FF_TILE, D_MODEL), lambda b, j: (j + off, 0))
    wsp = lambda off: pl.BlockSpec((3, FF_TILE), lambda b, j: (0, j + off))
    bsp = lambda off: pl.BlockSpec((1, FF_TILE), lambda b, j: (0, j + off))
    T = batch * seq
    return pl.pallas_call(
        body, grid=(batch, FF_BLOCKS),
        in_specs=[pl.BlockSpec((seq, D_MODEL), lambda b, j: (b, 0)), wup(0), wup(FF_BLOCKS),
                  wsp(0), wsp(FF_BLOCKS), bsp(0), bsp(FF_BLOCKS)],
        out_specs=[blk, blk, blk],
        out_shape=[jax.ShapeDtypeStruct((T, D_FF), F32), jax.ShapeDtypeStruct((T, D_FF), F32),
                   jax.ShapeDtypeStruct((T, D_FF), BF16)],
        name="up_act", compiler_params=_params("parallel", "arbitrary"))(h2, w_up, w_up, cw, cw, cb, cb)


def _ffn_act_bwd(upg, upv, cw, cb, dx2b, w_down, batch, seq):
    def half(du, x, w_ref, dx_ref, dw_ref, db_ref):
        dx_ref[...] = (w_ref[2:3, :] * du + w_ref[1:2, :] * _shift_up(du, 1)
                       + w_ref[0:1, :] * _shift_up(du, 2)).astype(BF16)
        dw_ref[2:3, :] += jnp.sum(du * x, axis=0, keepdims=True)
        dw_ref[1:2, :] += jnp.sum(du * _shift_down(x, 1), axis=0, keepdims=True)
        dw_ref[0:1, :] += jnp.sum(du * _shift_down(x, 2), axis=0, keepdims=True)
        db_ref[...] += jnp.sum(du, axis=0, keepdims=True)

    def body(ug_ref, uv_ref, wg_ref, wv_ref, bg_ref, bv_ref, dx_ref, wd_ref,
             dg_ref, dv_ref, dwg_ref, dwv_ref, dbg_ref, dbv_ref):
        @pl.when(pl.program_id(1) == 0)
        def _():
            for r in (dwg_ref, dwv_ref, dbg_ref, dbv_ref):
                r[...] = jnp.zeros_like(r)

        ug, uv = ug_ref[...], uv_ref[...]
        gate = _conv(ug, wg_ref, bg_ref)
        val = _conv(uv, wv_ref, bv_ref)
        sg = _sigmoid(gate)
        dav = _dot_nt(dx_ref[...], wd_ref[...])
        half(dav * val * sg * (1.0 + gate * (1.0 - sg)), ug, wg_ref, dg_ref, dwg_ref, dbg_ref)
        half(dav * gate * sg, uv, wv_ref, dv_ref, dwv_ref, dbv_ref)

    blk = pl.BlockSpec((seq, FF_TILE), lambda j, b: (b, j))
    wsp = lambda off: pl.BlockSpec((3, FF_TILE), lambda j, b: (0, j + off))
    bsp = lambda off: pl.BlockSpec((1, FF_TILE), lambda j, b: (0, j + off))
    T = batch * seq
    return pl.pallas_call(
        body, grid=(FF_BLOCKS, batch),
        in_specs=[blk, blk, wsp(0), wsp(FF_BLOCKS), bsp(0), bsp(FF_BLOCKS),
                  pl.BlockSpec((seq, D_MODEL), lambda j, b: (b, 0)),
                  pl.BlockSpec((FF_TILE, D_MODEL), lambda j, b: (j, 0))],
        out_specs=[blk, blk, wsp(0), wsp(0), bsp(0), bsp(0)],
        out_shape=[jax.ShapeDtypeStruct((T, D_FF), BF16), jax.ShapeDtypeStruct((T, D_FF), BF16),
                   jax.ShapeDtypeStruct((3, D_FF), F32), jax.ShapeDtypeStruct((3, D_FF), F32),
                   jax.ShapeDtypeStruct((1, D_FF), F32), jax.ShapeDtypeStruct((1, D_FF), F32)],
        name="ffn_act_bwd", compiler_params=_params("parallel", "arbitrary"))(
            upg, upv, cw, cw, cb, cb, dx2b, w_down)


def _down_loss(a, w_down, x1, target, gfin):
    T = x1.shape[0]
    tm = _tile(T, 512)

    def body(a_ref, w_ref, x1_ref, t_ref, g_ref, dx_ref, dxb_ref, loss_ref, dg_ref):
        @pl.when(pl.program_id(0) == 0)
        def _():
            loss_ref[...] = jnp.zeros_like(loss_ref)
            dg_ref[...] = jnp.zeros_like(dg_ref)

        x2 = x1_ref[...] + _dot(a_ref[...], w_ref[...])
        r = lax.rsqrt(jnp.mean(x2 * x2, axis=-1, keepdims=True) + EPS)
        xh = x2 * r
        g = g_ref[...]
        diff = xh * g - t_ref[...]
        loss_ref[...] += 0.5 * jnp.sum(jnp.mean(diff * diff, axis=-1, keepdims=True))
        dy = diff * (1.0 / D_MODEL)
        dg_ref[...] += jnp.sum(dy * xh, axis=0, keepdims=True)
        dxh = dy * g
        dx = r * (dxh - xh * jnp.mean(dxh * xh, axis=-1, keepdims=True))
        dx_ref[...] = dx
        dxb_ref[...] = dx.astype(BF16)

    row = pl.BlockSpec((tm, D_MODEL), lambda i: (i, 0))
    vec = pl.BlockSpec((1, D_MODEL), lambda i: (0, 0))
    return pl.pallas_call(
        body, grid=(T // tm,),
        in_specs=[pl.BlockSpec((tm, D_FF), lambda i: (i, 0)),
                  pl.BlockSpec((D_FF, D_MODEL), lambda i: (0, 0)), row, row, vec],
        out_specs=[row, row, pl.BlockSpec((8, LANES), lambda i: (0, 0)), vec],
        out_shape=[jax.ShapeDtypeStruct((T, D_MODEL), F32), jax.ShapeDtypeStruct((T, D_MODEL), BF16),
                   jax.ShapeDtypeStruct((8, LANES), F32), jax.ShapeDtypeStruct((1, D_MODEL), F32)],
        name="down_loss", compiler_params=_params("arbitrary"))(a, w_down, x1, target, gfin)


def _local_step(x, positions, target, mix_norm, av_g, av_b, w_s, b_s, q_norm, kv_norm, ffn_norm, conv_b,
                final_norm, comm):
    batch, seq, _ = x.shape
    T = batch * seq
    x = x.reshape(T, D_MODEL)
    target = target.reshape(T, D_MODEL)
    pos = positions.reshape(T, 1)
    half = jnp.arange(0, QK_ROPE, 2, dtype=F32) / QK_ROPE
    inv_freq = 1.0 / (ROPE_THETA ** half)
    invf = jnp.concatenate([inv_freq, inv_freq, jnp.zeros((LANES - QK_ROPE,), F32)]).reshape(1, LANES)
    w_st = jnp.swapaxes(w_s, 1, 2)
    b_col = b_s.reshape(A_GROUPS, CHUNK, 1)

    wt_in = comm.in_weights()
    h, zm, zs = _in_proj(x, mix_norm, wt_in)
    yag = _mixer_a_fwd(zm, av_g, av_b, w_s, b_col)
    wuq_p, wukv, w_out = comm.mla_weights(after=yag)
    q, k, v = _mla_prep_fwd(zs, pos, invf, q_norm, kv_norm, wuq_p, wukv)
    o, lse = _attn_fwd(q, k, v, batch, seq)
    merged, x1, h2 = _merge_out(x, yag, zm, o, w_out, ffn_norm)
    wt_up, conv_w, w_down = comm.ffn_weights(after=merged)
    upg, upv, act = _up_act(h2, wt_up, conv_w, conv_b, batch, seq)
    dx2, dx2b, loss_acc, d_final = _down_loss(act, w_down, x1, target, final_norm)

    d_wdown = _mm_tn(act, dx2b, "dw_down")
    dupg, dupv, dcwg, dcwv, dcbg, dcbv = _ffn_act_bwd(upg, upv, conv_w, conv_b, dx2b, w_down, batch, seq)
    d_wt_up = jnp.concatenate([_mm_tn(dupg, h2, "dw_up_gate"), _mm_tn(dupv, h2, "dw_up_val")], axis=0)
    dx1, d_ffn_norm, dmerged = _proj_bwd(
        [dupg, dupv], wt_up, [(0, (0, D_FF), (0, D_FF)), (1, (0, D_FF), (D_FF, 2 * D_FF))],
        x1, ffn_norm, dx2, "up_proj_bwd", w2=w_out)
    d_wout = _mm_tn(merged, dx1, "dw_out")
    token = comm.send_ffn_grads(d_wdown, d_wt_up, jnp.concatenate([dcwg, dcwv], axis=1), d_wout)
    dzm, do, d_avg, d_avb, d_ws, d_bs = _mixer_bwd(zm, o, dmerged, av_g, av_b, w_s, w_st, b_col, token)
    token = comm.send_small_grads(_pack_small(dict(
        a_v_norm_g=d_avg, a_v_norm_b=d_avb, a_spatial_w=d_ws, a_spatial_b=d_bs, ffn_norm=d_ffn_norm,
        conv_b=jnp.concatenate([dcbg, dcbv], axis=1), final_norm=d_final), SMALL_EARLY))
    dq, dk, dv = _attn_bwd(q, k, v, o, do, lse, batch, seq, token)
    dzs, cqn, dqp, ckvn, dkv, d_qn, d_kvn = _mla_prep_bwd(zs, pos, invf, q_norm, kv_norm, wuq_p, wukv, dq, dk, dv)
    d_wt_main = _mm_tn(dzm, h, "dw_in_main")
    d_wt_zs = _mm_tn(dzs, h, "dw_in_small")
    token = comm.send_in_grads(d_wt_main, d_wt_zs)
    d_wuq_p = _mm_tn(dqp, cqn, "dw_uq", dep=token)
    d_wukv = _mm_tn(ckvn, dkv, "dw_ukv", dep=token)
    terms = [(0, (i * D_MODEL, (i + 1) * D_MODEL), rows) for i, rows in enumerate(IN_ROWS_MAIN)]
    terms.append((1, (0, ZS_W), IN_ROWS_ZS))
    dx, d_mix_norm = _proj_bwd([dzm, dzs], wt_in, terms, x, mix_norm, dx1, "in_proj_bwd", dep=token)
    late = _pack_small(dict(q_a_norm=d_qn, kv_a_norm=d_kvn, mix_norm=d_mix_norm), SMALL_LATE,
                       extra=loss_acc[0, 0].reshape(1))
    token = comm.send_late_grads(d_wuq_p, d_wukv, late)
    return dx.reshape(batch, seq, D_MODEL), token


MESH_ID = pl.DeviceIdType.MESH
EFFECT = pltpu.SideEffectType.DATAFLOW_SIDE_EFFECTING


def _mesh_pos():
    return lax.axis_index("x"), lax.axis_index("y"), lax.axis_index("c")


def _peer(pos, d):
    x, y, c = pos
    px = 1 - x if d & 4 else x
    py = 1 - y if d & 2 else y
    pc = 1 - c if d & 1 else c
    return (px, py, pc), 4 * px + 2 * py + pc


def _copy(src_ref, land_ref, send_sems, recv_sems, a, d, pos, exchange, landing_here):
    peer, pid = _peer(pos, d)
    me = 4 * pos[0] + 2 * pos[1] + pos[2]
    if exchange:
        src, dst = src_ref.at[pid], land_ref.at[d]
    else:
        src, dst = src_ref, land_ref.at[pid if landing_here else me]
    return pltpu.make_async_remote_copy(
        src_ref=src, dst_ref=dst, send_sem=send_sems.at[a * (N_DEV - 1) + d - 1],
        recv_sem=recv_sems.at[a * (N_DEV - 1) + d - 1],
        device_id=peer, device_id_type=MESH_ID)


def _start_copies(groups, modes, name, dep=None):
    sizes = [len(g) for g in groups]
    srcs = [s for g in groups for s, _ in g]
    lands = [l for g in groups for _, l in g]
    n, ng = len(srcs), len(groups)
    n_in = 2 * n + (dep is not None)

    def body(*refs):
        src_refs, land_refs = refs[:n], refs[n:2 * n]
        sems = refs[n_in:n_in + 2 * ng]
        token = refs[-1]
        pos = _mesh_pos()
        k = 0
        for gi, size in enumerate(sizes):
            for a in range(size):
                for d in range(1, N_DEV):
                    _copy(src_refs[k], land_refs[k], sems[2 * gi], sems[2 * gi + 1], a, d, pos, modes[gi],
                          landing_here=False).start()
                k += 1
        token[...] = jnp.zeros_like(token)

    sem_shapes = [pltpu.SemaphoreType.DMA((size * (N_DEV - 1),)) for size in sizes for _ in range(2)]
    out = pl.pallas_call(
        body, name=name,
        out_shape=(*sem_shapes, *[pltpu.HBM(a.shape, a.dtype) for a in srcs + lands],
                   jax.ShapeDtypeStruct((8, LANES), F32)),
        in_specs=[HBM] * (2 * n) + [ANY] * (dep is not None),
        out_specs=(*[SEM] * (2 * ng), *[HBM] * (2 * n), pl.BlockSpec(memory_space=pltpu.VMEM)),
        input_output_aliases={i: 2 * ng + i for i in range(2 * n)},
        compiler_params=pltpu.CompilerParams(has_side_effects=EFFECT),
    )(*[pltpu.with_memory_space_constraint(a, pltpu.HBM) for a in srcs + lands], *([dep] if dep is not None else []))
    thru = out[2 * ng:2 * ng + 2 * n]
    handles, k = [], 0
    for gi, size in enumerate(sizes):
        handles.append((out[2 * gi], out[2 * gi + 1], thru[k:k + size], thru[n + k:n + k + size]))
        k += size
    return handles, out[-1]


def _wait_copies(handle, exchange, after, name):
    send_sems, recv_sems, srcs, lands = handle
    n = len(srcs)

    def body(*refs):
        src_refs, land_refs = refs[:n], refs[n:2 * n]
        send, recv = refs[2 * n], refs[2 * n + 1]
        pos = _mesh_pos()
        for a in range(n):
            for d in range(1, N_DEV):
                cp = _copy(src_refs[a], land_refs[a], send, recv, a, d, pos, exchange, landing_here=True)
                cp.wait_send()
                cp.wait_recv()

    out = pl.pallas_call(
        body, name=name,
        out_shape=tuple(pltpu.HBM(a.shape, a.dtype) for a in (*srcs, *lands)),
        in_specs=[HBM] * (2 * n) + [SEM, SEM, ANY], out_specs=[HBM] * (2 * n),
        input_output_aliases={i: i for i in range(2 * n)},
        compiler_params=pltpu.CompilerParams(has_side_effects=EFFECT),
    )(*srcs, *lands, send_sems, recv_sems, after)
    return out[n:]


def _land_gather(a, me):
    zone = jnp.zeros((N_DEV,) + a.shape, a.dtype)
    return lax.dynamic_update_slice(zone, a[None], (me,) + (0,) * a.ndim)


def _land_exchange(s, me):
    own = lax.dynamic_index_in_dim(s, me, 0, keepdims=True)
    return lax.dynamic_update_slice(jnp.zeros_like(s), own, (0,) * s.ndim)


def _gather_now(a, name):
    def body(x_ref, out_ref, send_sems, recv_sems, local_sem):
        x, y, c = _mesh_pos()
        me, sibling = (x, y, c), (x, y, 1 - c)
        chips = [(1 - x, y), (x, 1 - y), (1 - x, 1 - y)]

        def slot(p):
            return out_ref.at[4 * p[0] + 2 * p[1] + p[2]]

        def copy(k, block, to, src=None):
            return pltpu.make_async_remote_copy(
                src_ref=slot(block) if src is None else src, dst_ref=slot(block), send_sem=send_sems.at[k],
                recv_sem=recv_sems.at[k], device_id=to, device_id_type=MESH_ID)

        mine = pltpu.make_async_copy(x_ref, slot(me), local_sem)
        mine.start()
        first = [copy(0, me, sibling, src=x_ref)]
        first += [copy(1 + j, me, (*chip, c), src=x_ref) for j, chip in enumerate(chips)]
        for cp in first:
            cp.start()
        passed = [copy(4 + j, (*chip, c), sibling) for j, chip in enumerate(chips)]
        for j, chip in enumerate(chips):
            copy(1 + j, (*chip, c), me).wait_recv()
            passed[j].start()
        copy(0, sibling, me).wait_recv()
        for j, chip in enumerate(chips):
            copy(4 + j, (*chip, 1 - c), me).wait_recv()
        for cp in first + passed:
            cp.wait_send()
        mine.wait()

    return pl.pallas_call(
        body, in_specs=[ANY], out_specs=ANY,
        out_shape=jax.ShapeDtypeStruct((N_DEV,) + a.shape, a.dtype),
        scratch_shapes=[pltpu.SemaphoreType.DMA((N_DEV - 1,)), pltpu.SemaphoreType.DMA((N_DEV - 1,)),
                        pltpu.SemaphoreType.DMA],
        name=name, compiler_params=pltpu.CompilerParams(has_side_effects=True))(a)


def _adamw(parts, w, m, v, name):
    R, C = w.shape
    tr, tc = R, C
    if N_DEV * R * C * parts.dtype.itemsize > SMALL_BLOCK_BYTES:
        tr = next((t for t in range(min(R, 256) // 16 * 16, 15, -16) if R % t == 0), R)
        if tr == R:
            tc = _tile(C, 256)
    c1 = 1.0 - ADAM_B1 ** ADAM_STEP
    c2 = 1.0 - ADAM_B2 ** ADAM_STEP

    def body(p_ref, w_ref, m_ref, v_ref, g_ref, d_ref, nm_ref, nv_ref):
        g = p_ref[0].astype(F32)
        for k in range(1, N_DEV):
            g = g + p_ref[k].astype(F32)
        nm = ADAM_B1 * m_ref[...] + (1.0 - ADAM_B1) * g
        nv = ADAM_B2 * v_ref[...] + (1.0 - ADAM_B2) * (g * g)
        g_ref[...] = g
        nm_ref[...] = nm
        nv_ref[...] = nv
        d_ref[...] = -ADAM_LR * ((nm / c1) / (jnp.sqrt(nv / c2) + ADAM_EPS) + ADAM_WD * w_ref[...])

    blk = pl.BlockSpec((tr, tc), lambda i, j: (i, j))
    shp = jax.ShapeDtypeStruct((R, C), F32)
    return pl.pallas_call(
        body, grid=(R // tr, C // tc),
        in_specs=[pl.BlockSpec((N_DEV, tr, tc), lambda i, j: (0, i, j)), blk, blk, blk],
        out_specs=[blk, blk, blk, blk], out_shape=[shp, shp, shp, shp],
        name=name, compiler_params=_params("parallel", "parallel"))(parts, w, m, v)


SPLIT_V = 2 * D_MODEL
SPLIT_KR = SPLIT_V + Q_LORA + KV_LORA + QK_ROPE
IN_DIM = SPLIT_KR + 2 * D_MODEL
IN_ROWS_MAIN = ((0, D_MODEL), (D_MODEL, SPLIT_V), (SPLIT_KR, SPLIT_KR + D_MODEL), (SPLIT_KR + D_MODEL, IN_DIM))
IN_ROWS_ZS = (SPLIT_V, SPLIT_V + ZS_W)

SMALL_EARLY = ("a_v_norm_g", "a_v_norm_b", "a_spatial_w", "a_spatial_b", "ffn_norm", "conv_b", "final_norm")
SMALL_LATE = ("q_a_norm", "kv_a_norm", "mix_norm")


def _pack_rows(a):
    flat = a.reshape(-1)
    rows = -(-flat.shape[0] // LANES)
    rows8 = -(-rows // 8) * 8
    return jnp.pad(flat, (0, rows8 * LANES - flat.shape[0])).reshape(rows8, LANES)


def _pack_small(tree, names, extra=None):
    parts = [_pack_rows(tree[n]) for n in names]
    if extra is not None:
        parts.append(_pack_rows(extra))
    return jnp.concatenate(parts, axis=0)


def _unpack_small(buf, names, shapes):
    out, r = {}, 0
    for n in names:
        size = math.prod(shapes[n])
        rows8 = -(-(-(-size // LANES)) // 8) * 8
        out[n] = buf[r:r + rows8].reshape(-1)[:size].reshape(shapes[n])
        r += rows8
    return out, r


def _cols_from_shards(g):
    return jnp.transpose(g, (1, 0, 2)).reshape(g.shape[1], N_DEV * g.shape[2])


def _shards_from_cols(a):
    R, W = a.shape
    return jnp.transpose(a.reshape(R, N_DEV, W // N_DEV), (1, 0, 2))


class _Comm:
    GATHER_GROUPS = (("w_uq", "w_ukv", "w_out"), ("w_up", "conv_w", "w_down"))
    FFN_GRADS = ("w_down", "w_up", "conv_w", "w_out")
    LATE_GRADS = ("w_uq", "w_ukv")
    TRANSPOSED = ("w_in", "w_up", "w_uq")

    def __init__(self, shards, me):
        self.me = me
        local = {n: a.astype(F32 if n == "conv_w" else BF16) for n, a in shards.items()}
        self.g_in = _gather_now(local["w_in"], "gather_w_in")
        groups = [[(local[n], _land_gather(local[n], me)) for n in g] for g in self.GATHER_GROUPS]
        (self.h_mla, self.h_ffn), _ = _start_copies(groups, [False] * 2, "gather_start", dep=self.g_in)

    def in_weights(self):
        return self.g_in.reshape(IN_DIM, D_MODEL)

    def mla_weights(self, after):
        g_uq, g_ukv, g_out = _wait_copies(self.h_mla, False, after, "gather_wait_mla")
        wuq_p = jnp.pad(g_uq, ((0, 0), (0, HEAD_PAD - QK_HEAD), (0, 0))).reshape(MLA_HEADS * HEAD_PAD, Q_LORA)
        return wuq_p, _cols_from_shards(g_ukv), g_out.reshape(D_MODEL, D_MODEL)

    def ffn_weights(self, after):
        g_up, g_cw, g_down = _wait_copies(self.h_ffn, False, after, "gather_wait_ffn")
        return g_up.reshape(2 * D_FF, D_MODEL), _cols_from_shards(g_cw), g_down.reshape(D_FF, D_MODEL)

    def _exchange_group(self, blocks):
        return [(s, _land_exchange(s, self.me)) for s in blocks]

    def send_ffn_grads(self, d_wdown, d_wt_up, d_convw, d_wout):
        group = self._exchange_group([
            d_wdown.reshape(N_DEV, D_FF // N_DEV, D_MODEL), d_wt_up.reshape(N_DEV, 2 * D_FF // N_DEV, D_MODEL),
            _shards_from_cols(d_convw), d_wout.reshape(N_DEV, D_MODEL // N_DEV, D_MODEL)])
        (self.h_ffn_grads,), token = _start_copies([group], [True], "ffn_grads_start")
        return token

    def send_small_grads(self, packed):
        (self.h_small_early,), token = _start_copies(
            [[(packed, _land_gather(packed, self.me))]], [False], "small_grads_start")
        return token

    def send_in_grads(self, d_wt_main, d_wt_zs):
        d_in = jnp.concatenate([d_wt_main[:SPLIT_V], d_wt_zs[:SPLIT_KR - SPLIT_V], d_wt_main[SPLIT_V:]], axis=0)
        blocks = d_in.reshape(N_DEV, IN_DIM // N_DEV, D_MODEL)
        (self.h_in_grads,), token = _start_copies([self._exchange_group([blocks])], [True], "in_grads_start")
        return token

    def send_late_grads(self, d_wuq_p, d_wukv, packed):
        d_uq = d_wuq_p.reshape(MLA_HEADS, HEAD_PAD, Q_LORA)[:, :QK_HEAD, :]
        group = self._exchange_group([d_uq, _shards_from_cols(d_wukv)])
        small = [(packed, _land_gather(packed, self.me))]
        (self.h_late_grads, self.h_late_small), token = _start_copies(
            [group, small], [True, False], "late_grads_start")
        return token


def kernel(x, positions, mix_norm, w_in, a_v_norm_g, a_v_norm_b, a_spatial_w, a_spatial_b, q_a_norm, w_uq, kv_a_norm, w_ukv, w_out, ffn_norm, w_up, conv_w, conv_b, w_down, final_norm, loss_target, m_mix_norm, m_w_in, m_a_v_norm_g, m_a_v_norm_b, m_a_spatial_w, m_a_spatial_b, m_q_a_norm, m_w_uq, m_kv_a_norm, m_w_ukv, m_w_out, m_ffn_norm, m_w_up, m_conv_w, m_conv_b, m_w_down, m_final_norm, v_mix_norm, v_w_in, v_a_v_norm_g, v_a_v_norm_b, v_a_spatial_w, v_a_spatial_b, v_q_a_norm, v_w_uq, v_kv_a_norm, v_w_ukv, v_w_out, v_ffn_norm, v_w_up, v_conv_w, v_conv_b, v_w_down, v_final_norm):
    names = ("mix_norm", "w_in", "a_v_norm_g", "a_v_norm_b", "a_spatial_w", "a_spatial_b", "q_a_norm", "w_uq",
             "kv_a_norm", "w_ukv", "w_out", "ffn_norm", "w_up", "conv_w", "conv_b", "w_down", "final_norm")
    w = dict(zip(names, (mix_norm, w_in, a_v_norm_g, a_v_norm_b, a_spatial_w, a_spatial_b, q_a_norm, w_uq,
                         kv_a_norm, w_ukv, w_out, ffn_norm, w_up, conv_w, conv_b, w_down, final_norm)))
    m = dict(zip(names, (m_mix_norm, m_w_in, m_a_v_norm_g, m_a_v_norm_b, m_a_spatial_w, m_a_spatial_b,
                         m_q_a_norm, m_w_uq, m_kv_a_norm, m_w_ukv, m_w_out, m_ffn_norm, m_w_up, m_conv_w,
                         m_conv_b, m_w_down, m_final_norm)))
    v = dict(zip(names, (v_mix_norm, v_w_in, v_a_v_norm_g, v_a_v_norm_b, v_a_spatial_w, v_a_spatial_b,
                         v_q_a_norm, v_w_uq, v_kv_a_norm, v_w_ukv, v_w_out, v_ffn_norm, v_w_up, v_conv_w,
                         v_conv_b, v_w_down, v_final_norm)))
    shapes = {n: w[n].shape for n in names}
    me = 4 * lax.axis_index("x") + 2 * lax.axis_index("y") + lax.axis_index("c")
    def view(tree, n):
        a = tree[n].reshape(tree[n].shape[-2:])
        return a.T if n in _Comm.TRANSPOSED else a

    comm = _Comm({n: view(w, n) for n in ("w_in",) + _Comm.GATHER_GROUPS[0] + _Comm.GATHER_GROUPS[1]}, me)

    grad_x, token = _local_step(
        x, positions, loss_target, w["mix_norm"], w["a_v_norm_g"], w["a_v_norm_b"], w["a_spatial_w"][0],
        w["a_spatial_b"][0], w["q_a_norm"], w["kv_a_norm"], w["ffn_norm"], w["conv_b"],
        w["final_norm"].reshape(1, D_MODEL), comm)

    out_g, out_d, out_m, out_v = {}, {}, {}, {}

    def update(n, parts):
        res = _adamw(parts, view(w, n), view(m, n), view(v, n), "adamw_" + n)
        out_g[n], out_d[n], out_m[n], out_v[n] = (
            (t.T if n in _Comm.TRANSPOSED else t).reshape(shapes[n]) for t in res)

    for n, parts in zip(_Comm.FFN_GRADS, _wait_copies(comm.h_ffn_grads, True, token, "ffn_grads_wait")):
        update(n, parts)
    update("w_in", _wait_copies(comm.h_in_grads, True, out_d["w_up"], "in_grads_wait")[0])
    for n, parts in zip(_Comm.LATE_GRADS, _wait_copies(comm.h_late_grads, True, out_d["w_in"], "late_grads_wait")):
        update(n, parts)

    (early_parts,) = _wait_copies(comm.h_small_early, False, out_d["w_uq"], "small_grads_wait")
    res = _adamw(early_parts, _pack_small(w, SMALL_EARLY), _pack_small(m, SMALL_EARLY), _pack_small(v, SMALL_EARLY),
                 "adamw_small")
    unpacked = [_unpack_small(t, SMALL_EARLY, shapes)[0] for t in res]
    for n in SMALL_EARLY:
        out_g[n], out_d[n], out_m[n], out_v[n] = (u[n] for u in unpacked)

    (late_parts,) = _wait_copies(comm.h_late_small, False, res[1], "late_small_wait")
    zero = jnp.zeros((1,), F32)
    res = _adamw(late_parts, _pack_small(w, SMALL_LATE, extra=zero), _pack_small(m, SMALL_LATE, extra=zero),
                 _pack_small(v, SMALL_LATE, extra=zero), "adamw_late")
    unpacked = [_unpack_small(t, SMALL_LATE, shapes) for t in res]
    for n in SMALL_LATE:
        out_g[n], out_d[n], out_m[n], out_v[n] = (u[0][n] for u in unpacked)
    loss = res[0][unpacked[0][1], 0]

    return (loss, grad_x, *[out_g[n] for n in names], *[out_d[n] for n in names],
            *[out_m[n] for n in names], *[out_v[n] for n in names])
```

```python
import functools
import math

import jax
import jax.numpy as jnp
from jax import lax
from jax.experimental import pallas as pl
from jax.experimental.pallas import tpu as pltpu

F32 = jnp.float32
BF16 = jnp.bfloat16

N_DEV = 8
D_MODEL = 1024
EPS = 1e-6
A_GROUPS = 8
CHUNK = 128
MLA_HEADS = 8
QK_NOPE = 128
QK_ROPE = 64
QK_HEAD = QK_NOPE + QK_ROPE
HEAD_PAD = 256
V_HEAD = 128
Q_LORA = 256
KV_LORA = 128
ROPE_THETA = 10000.0
D_FF = 2816
ZS_W = 512
ATTN_SCALE = QK_HEAD ** -0.5
ATTN_TILE = 256
NEG_BIG = -1e30

ADAM_LR = 0.001
ADAM_B1 = 0.9
ADAM_B2 = 0.999
ADAM_EPS = 1e-08
ADAM_WD = 0.01
ADAM_STEP = 10

VMEM_LIMIT = 56 * 1024 * 1024
SMALL_BLOCK_BYTES = 5 * 1024 * 1024
LANES = 128

GELU_K = math.sqrt(2.0 / math.pi)
GELU_C = 0.044715

ANY = pl.BlockSpec(memory_space=pl.ANY)
HBM = pl.BlockSpec(memory_space=pltpu.HBM)
SEM = pl.BlockSpec(memory_space=pltpu.SEMAPHORE)


def _tile(n, pref):
    for t in (pref, 512, 256, 128, 64, 32, 16, 8):
        if t <= pref and n % t == 0:
            return t
    return n


def _wide_tile(n, cap=1408):
    return next((t for t in range(min(n, cap) // LANES * LANES, 0, -LANES) if n % t == 0), n)


def _params(*sem):
    return pltpu.CompilerParams(dimension_semantics=sem, vmem_limit_bytes=VMEM_LIMIT)


def _dot(a, b):
    return jnp.dot(a, b, preferred_element_type=F32)


def _dot_nt(a, b):
    return lax.dot_general(a, b, (((1,), (1,)), ((), ())), preferred_element_type=F32)


def _dot_tn(a, b):
    return lax.dot_general(a, b, (((0,), (0,)), ((), ())), preferred_element_type=F32)


def _sigmoid(x):
    return 1.0 / (1.0 + jnp.exp(-x))


def _gelu(x):
    t = jnp.tanh(GELU_K * (x + GELU_C * x * x * x))
    return 0.5 * x * (1.0 + t)


def _gelu_grad(x):
    t = jnp.tanh(GELU_K * (x + GELU_C * x * x * x))
    return 0.5 * (1.0 + t) + 0.5 * x * (1.0 - t * t) * GELU_K * (1.0 + 3.0 * GELU_C * x * x)


def _in_proj(x, g, wt):
    T, Dm = x.shape
    tm = _tile(T, 256)

    def body(x_ref, g_ref, wt_ref, h_ref, zm_ref, zs_ref):
        xf = x_ref[...]
        r = lax.rsqrt(jnp.mean(xf * xf, axis=-1, keepdims=True) + EPS)
        h = (xf * r * g_ref[...]).astype(BF16)
        h_ref[...] = h
        for i, (r0, r1) in enumerate(IN_ROWS_MAIN):
            zm_ref[:, i * D_MODEL:(i + 1) * D_MODEL] = _dot_nt(h, wt_ref[r0:r1, :])
        zs_ref[...] = _dot_nt(h, wt_ref[IN_ROWS_ZS[0]:IN_ROWS_ZS[1], :])

    row = lambda n: pl.BlockSpec((tm, n), lambda i: (i, 0))
    return pl.pallas_call(
        body, grid=(T // tm,),
        in_specs=[row(Dm), pl.BlockSpec((1, Dm), lambda i: (0, 0)), pl.BlockSpec(wt.shape, lambda i: (0, 0))],
        out_specs=[row(Dm), row(4 * D_MODEL), row(ZS_W)],
        out_shape=[jax.ShapeDtypeStruct((T, Dm), BF16), jax.ShapeDtypeStruct((T, 4 * D_MODEL), F32),
                   jax.ShapeDtypeStruct((T, ZS_W), F32)],
        name="in_proj", compiler_params=_params("parallel"))(x, g, wt)


def _proj_bwd(acts, wt, terms, x, g, dres, name, w2=None, dep=None):
    T, Dm = x.shape
    tm = _tile(T, 256)
    n_a = len(acts)

    def body(*refs):
        ins, outs = refs[:n_a + 4 + (w2 is not None) + (dep is not None)], refs[-2 - (w2 is not None):]
        wt_ref, x_ref, g_ref, dres_ref = ins[n_a:n_a + 4]
        dx_ref, dg_ref = outs[0], outs[1]

        @pl.when(pl.program_id(0) == 0)
        def _():
            dg_ref[...] = jnp.zeros_like(dg_ref)

        dy = None
        for i, (c0, c1), (r0, r1) in terms:
            t = _dot(ins[i][:, c0:c1], wt_ref[r0:r1, :])
            dy = t if dy is None else dy + t
        xf = x_ref[...]
        r = lax.rsqrt(jnp.mean(xf * xf, axis=-1, keepdims=True) + EPS)
        xh = xf * r
        dg_ref[...] += jnp.sum(dy * xh, axis=0, keepdims=True)
        dxh = dy * g_ref[...]
        dx = dres_ref[...] + r * (dxh - xh * jnp.mean(dxh * xh, axis=-1, keepdims=True))
        dx_ref[...] = dx
        if w2 is not None:
            outs[2][...] = _dot_nt(dx.astype(BF16), ins[n_a + 4][...])

    row = pl.BlockSpec((tm, Dm), lambda i: (i, 0))
    vec = pl.BlockSpec((1, Dm), lambda i: (0, 0))
    in_specs = [pl.BlockSpec((tm, a.shape[1]), lambda i: (i, 0)) for a in acts]
    in_specs += [pl.BlockSpec(wt.shape, lambda i: (0, 0)), row, vec, row]
    args = [*acts, wt, x, g, dres]
    out_specs = [row, vec]
    out_shape = [jax.ShapeDtypeStruct((T, Dm), F32), jax.ShapeDtypeStruct((1, Dm), F32)]
    if w2 is not None:
        in_specs.append(pl.BlockSpec(w2.shape, lambda i: (0, 0)))
        args.append(w2)
        out_specs.append(pl.BlockSpec((tm, w2.shape[0]), lambda i: (i, 0)))
        out_shape.append(jax.ShapeDtypeStruct((T, w2.shape[0]), F32))
    if dep is not None:
        in_specs.append(ANY)
        args.append(dep)
    return pl.pallas_call(
        body, grid=(T // tm,), in_specs=in_specs, out_specs=out_specs, out_shape=out_shape,
        name=name, compiler_params=_params("arbitrary"))(*args)


def _mm_tn(a, b, name, dep=None):
    T, M = a.shape
    N = b.shape[1]
    tm, tn, tt = _wide_tile(M), _wide_tile(N), _tile(T, 2048)
    n_t = T // tt

    def body(a_ref, b_ref, *refs):
        o_ref, acc_ref = refs[-2:]
        t = pl.program_id(2)

        @pl.when(t == 0)
        def _():
            acc_ref[...] = jnp.zeros_like(acc_ref)

        acc_ref[...] += _dot_tn(a_ref[...].astype(BF16), b_ref[...].astype(BF16))

        @pl.when(t == n_t - 1)
        def _():
            o_ref[...] = acc_ref[...].astype(BF16)

    return pl.pallas_call(
        body, grid=(M // tm, N // tn, n_t),
        in_specs=[pl.BlockSpec((tt, tm), lambda i, j, t: (t, i)),
                  pl.BlockSpec((tt, tn), lambda i, j, t: (t, j))] + [ANY] * (dep is not None),
        out_specs=pl.BlockSpec((tm, tn), lambda i, j, t: (i, j)),
        out_shape=jax.ShapeDtypeStruct((M, N), BF16),
        scratch_shapes=[pltpu.VMEM((tm, tn), F32)],
        name=name, compiler_params=_params("parallel", "parallel", "arbitrary"))(
            a, b, *([dep] if dep is not None else []))


def _layer_norm_fwd(gv, g, b):
    mu = jnp.mean(gv, axis=-1, keepdims=True)
    xc = gv - mu
    rs = lax.rsqrt(jnp.mean(xc * xc, axis=-1, keepdims=True) + EPS)
    xh = xc * rs
    return xh, rs, xh * g + b


def _tri_mask(transposed=False):
    r = lax.broadcasted_iota(jnp.int32, (CHUNK, CHUNK), 0)
    c = lax.broadcasted_iota(jnp.int32, (CHUNK, CHUNK), 1)
    return r <= c if transposed else c <= r


def _mixer_a_fwd(zm, av_g, av_b, w_s, b_col):
    T = zm.shape[0]
    tm = _tile(T, 256)
    n_chunk = tm // CHUNK

    def body(u_ref, v_ref, ga_ref, g_ref, b_ref, w_ref, bc_ref, y_ref, vn_s, mx_s):
        gu = _gelu(u_ref[...])
        _, _, vn = _layer_norm_fwd(_gelu(v_ref[...]), g_ref[...], b_ref[...])
        vn_s[...] = vn.astype(BF16)
        tri = _tri_mask()
        for gi in range(A_GROUPS):
            wm = jnp.where(tri, w_ref[gi], 0.0).astype(BF16)
            cols = slice(gi * CHUNK, (gi + 1) * CHUNK)
            for n in range(n_chunk):
                rows = slice(n * CHUNK, (n + 1) * CHUNK)
                mx_s[rows, cols] = _dot(wm, vn_s[rows, cols]) + bc_ref[gi]
        y_ref[...] = _sigmoid(ga_ref[...]) * gu * mx_s[...]

    col = lambda c: pl.BlockSpec((tm, D_MODEL), lambda i: (i, c))
    vec = pl.BlockSpec((1, D_MODEL), lambda i: (0, 0))
    return pl.pallas_call(
        body, grid=(T // tm,),
        in_specs=[col(0), col(1), col(2), vec, vec,
                  pl.BlockSpec((A_GROUPS, CHUNK, CHUNK), lambda i: (0, 0, 0)),
                  pl.BlockSpec((A_GROUPS, CHUNK, 1), lambda i: (0, 0, 0))],
        out_specs=pl.BlockSpec((tm, D_MODEL), lambda i: (i, 0)),
        out_shape=jax.ShapeDtypeStruct((T, D_MODEL), F32),
        scratch_shapes=[pltpu.VMEM((tm, D_MODEL), BF16), pltpu.VMEM((tm, D_MODEL), F32)],
        name="mixer_a_fwd", compiler_params=_params("parallel"))(zm, zm, zm, av_g, av_b, w_s, b_col)


def _mixer_bwd(zm, o, dm, av_g, av_b, w_s, w_st, b_col, dep):
    T = zm.shape[0]
    tm = _tile(T, 256)
    n_chunk = tm // CHUNK

    def body(u_ref, v_ref, ga_ref, gb_ref, o_ref, dm_ref, g_ref, b_ref, w_ref, wt_ref, bc_ref, dep_ref,
             dz_ref, do_ref, dg_ref, db_ref, dw_ref, dbs_ref, vn_s, mx_s, dmx_s, dvn_s):
        @pl.when(pl.program_id(0) == 0)
        def _():
            dg_ref[...] = jnp.zeros_like(dg_ref)
            db_ref[...] = jnp.zeros_like(db_ref)
            dw_ref[...] = jnp.zeros_like(dw_ref)
            dbs_ref[...] = jnp.zeros_like(dbs_ref)

        dm_v = dm_ref[...]
        gb = gb_ref[...]
        sb = _sigmoid(gb)
        o_v = o_ref[...]
        do_ref[...] = (dm_v * sb).astype(BF16)
        dz_ref[:, 3 * D_MODEL:4 * D_MODEL] = (dm_v * o_v * sb * (1.0 - sb)).astype(BF16)
        u = u_ref[...]
        v = v_ref[...]
        gu = _gelu(u)
        xh, rs, vn = _layer_norm_fwd(_gelu(v), g_ref[...], b_ref[...])
        vn_s[...] = vn.astype(BF16)
        tri = _tri_mask()
        for gi in range(A_GROUPS):
            wm = jnp.where(tri, w_ref[gi], 0.0).astype(BF16)
            cols = slice(gi * CHUNK, (gi + 1) * CHUNK)
            for n in range(n_chunk):
                rows = slice(n * CHUNK, (n + 1) * CHUNK)
                mx_s[rows, cols] = _dot(wm, vn_s[rows, cols]) + bc_ref[gi]
        mixed = mx_s[...]
        sa = _sigmoid(ga_ref[...])
        dya = dm_v * sa
        dz_ref[:, 2 * D_MODEL:3 * D_MODEL] = (dm_v * gu * mixed * sa * (1.0 - sa)).astype(BF16)
        dz_ref[:, 0:D_MODEL] = (dya * mixed * _gelu_grad(u)).astype(BF16)
        dmx = dya * gu
        dmx_s[...] = dmx.astype(BF16)
        tri_t = _tri_mask(transposed=True)
        for gi in range(A_GROUPS):
            wmt = jnp.where(tri_t, wt_ref[gi], 0.0).astype(BF16)
            cols = slice(gi * CHUNK, (gi + 1) * CHUNK)
            dw_acc = jnp.zeros((CHUNK, CHUNK), F32)
            dmx_sum = jnp.zeros((CHUNK, CHUNK), F32)
            for n in range(n_chunk):
                rows = slice(n * CHUNK, (n + 1) * CHUNK)
                blk = dmx_s[rows, cols]
                dvn_s[rows, cols] = _dot(wmt, blk)
                dw_acc = dw_acc + _dot_nt(blk, vn_s[rows, cols])
                dmx_sum = dmx_sum + dmx[rows, cols]
            dw_ref[gi] += jnp.where(tri, dw_acc, 0.0)
            dbs_ref[gi] += jnp.sum(dmx_sum, axis=-1, keepdims=True)
        dvn = dvn_s[...]
        dg_ref[...] += jnp.sum(dvn * xh, axis=0, keepdims=True)
        db_ref[...] += jnp.sum(dvn, axis=0, keepdims=True)
        dxh = dvn * g_ref[...]
        dgv = rs * (dxh - jnp.mean(dxh, axis=-1, keepdims=True)
                    - xh * jnp.mean(dxh * xh, axis=-1, keepdims=True))
        dz_ref[:, D_MODEL:2 * D_MODEL] = (dgv * _gelu_grad(v)).astype(BF16)

    col = lambda c: pl.BlockSpec((tm, D_MODEL), lambda i: (i, c))
    row = pl.BlockSpec((tm, D_MODEL), lambda i: (i, 0))
    vec = pl.BlockSpec((1, D_MODEL), lambda i: (0, 0))
    wsp = pl.BlockSpec((A_GROUPS, CHUNK, CHUNK), lambda i: (0, 0, 0))
    bsp = pl.BlockSpec((A_GROUPS, CHUNK, 1), lambda i: (0, 0, 0))
    return pl.pallas_call(
        body, grid=(T // tm,),
        in_specs=[col(0), col(1), col(2), col(3), row, row, vec, vec, wsp, wsp, bsp, ANY],
        out_specs=[pl.BlockSpec((tm, 4 * D_MODEL), lambda i: (i, 0)), row, vec, vec, wsp, bsp],
        out_shape=[jax.ShapeDtypeStruct((T, 4 * D_MODEL), BF16), jax.ShapeDtypeStruct((T, D_MODEL), BF16),
                   jax.ShapeDtypeStruct((1, D_MODEL), F32), jax.ShapeDtypeStruct((1, D_MODEL), F32),
                   jax.ShapeDtypeStruct((A_GROUPS, CHUNK, CHUNK), F32),
                   jax.ShapeDtypeStruct((A_GROUPS, CHUNK, 1), F32)],
        scratch_shapes=[pltpu.VMEM((tm, D_MODEL), BF16), pltpu.VMEM((tm, D_MODEL), F32),
                        pltpu.VMEM((tm, D_MODEL), BF16), pltpu.VMEM((tm, D_MODEL), F32)],
        name="mixer_bwd", compiler_params=_params("arbitrary"))(
            zm, zm, zm, zm, o, dm, av_g, av_b, w_s, w_st, b_col, dep)


def _rope_tables(pos_ref, invf_ref):
    ang = pos_ref[...].astype(F32) * invf_ref[...]
    lane = lax.broadcasted_iota(jnp.int32, ang.shape, 1)
    cos, sin = jnp.cos(ang), jnp.sin(ang)
    c = jnp.where(lane < QK_ROPE, cos, 0.0)
    sa = jnp.where(lane < QK_ROPE // 2, -sin, 0.0)
    sb = jnp.where((lane >= QK_ROPE // 2) & (lane < QK_ROPE), sin, 0.0)
    return c, sa, sb


def _rope(blk, tabs):
    c, sa, sb = tabs
    return blk * c + pltpu.roll(blk, LANES - QK_ROPE // 2, 1) * sa + pltpu.roll(blk, QK_ROPE // 2, 1) * sb


def _rope_t(dout, tabs):
    c, sa, sb = tabs
    return dout * c + pltpu.roll(dout * sa, QK_ROPE // 2, 1) + pltpu.roll(dout * sb, LANES - QK_ROPE // 2, 1)


def _rms_small(x, g):
    r = lax.rsqrt(jnp.mean(x * x, axis=-1, keepdims=True) + EPS)
    xh = x * r
    return xh, r, xh * g


def _mla_prep_fwd(zs, pos, invf, qg, kvg, wuq_p, wukv):
    T = zs.shape[0]
    tm = _tile(T, 512)
    HW = MLA_HEADS * HEAD_PAD

    def body(zs_ref, pos_ref, invf_ref, qg_ref, kvg_ref, wq_ref, wkv_ref, q_ref, k_ref, v_ref):
        tabs = _rope_tables(pos_ref, invf_ref)
        _, _, cqn = _rms_small(zs_ref[:, 0:Q_LORA], qg_ref[...])
        _, _, ckvn = _rms_small(zs_ref[:, Q_LORA:Q_LORA + KV_LORA], kvg_ref[...])
        q = _dot_nt(cqn.astype(BF16), wq_ref[...])
        kv = _dot(ckvn.astype(BF16), wkv_ref[...])
        kr = _rope(zs_ref[:, Q_LORA + KV_LORA:ZS_W], tabs).astype(BF16)
        for h in range(MLA_HEADS):
            b0 = h * HEAD_PAD
            q_ref[:, b0:b0 + QK_NOPE] = q[:, b0:b0 + QK_NOPE].astype(BF16)
            q_ref[:, b0 + QK_NOPE:b0 + HEAD_PAD] = _rope(q[:, b0 + QK_NOPE:b0 + HEAD_PAD], tabs).astype(BF16)
            k_ref[:, b0:b0 + QK_NOPE] = kv[:, b0:b0 + QK_NOPE].astype(BF16)
            k_ref[:, b0 + QK_NOPE:b0 + HEAD_PAD] = kr
            v_ref[:, h * V_HEAD:(h + 1) * V_HEAD] = kv[:, b0 + QK_NOPE:b0 + HEAD_PAD].astype(BF16)

    full = lambda a: pl.BlockSpec(a.shape, lambda i: (0,) * a.ndim)
    return pl.pallas_call(
        body, grid=(T // tm,),
        in_specs=[pl.BlockSpec((tm, ZS_W), lambda i: (i, 0)), pl.BlockSpec((tm, 1), lambda i: (i, 0)),
                  full(invf), full(qg), full(kvg), full(wuq_p), full(wukv)],
        out_specs=[pl.BlockSpec((tm, HW), lambda i: (i, 0)), pl.BlockSpec((tm, HW), lambda i: (i, 0)),
                   pl.BlockSpec((tm, D_MODEL), lambda i: (i, 0))],
        out_shape=[jax.ShapeDtypeStruct((T, HW), BF16), jax.ShapeDtypeStruct((T, HW), BF16),
                   jax.ShapeDtypeStruct((T, D_MODEL), BF16)],
        name="mla_prep_fwd", compiler_params=_params("parallel"))(zs, pos, invf, qg, kvg, wuq_p, wukv)


def _mla_prep_bwd(zs, pos, invf, qg, kvg, wuq_p, wukv, dq, dk, dv):
    T = zs.shape[0]
    tm = _tile(T, 256)
    HW = MLA_HEADS * HEAD_PAD

    def body(zs_ref, pos_ref, invf_ref, qg_ref, kvg_ref, wq_ref, wkv_ref, dq_ref, dk_ref, dv_ref,
             dzs_ref, cqn_ref, dqp_ref, ckvn_ref, dkv_ref, dqg_ref, dkvg_ref):
        @pl.when(pl.program_id(0) == 0)
        def _():
            dqg_ref[...] = jnp.zeros_like(dqg_ref)
            dkvg_ref[...] = jnp.zeros_like(dkvg_ref)

        tabs = _rope_tables(pos_ref, invf_ref)
        cqh, rq, cqn = _rms_small(zs_ref[:, 0:Q_LORA], qg_ref[...])
        ckvh, rkv, ckvn = _rms_small(zs_ref[:, Q_LORA:Q_LORA + KV_LORA], kvg_ref[...])
        cqn_ref[...] = cqn.astype(BF16)
        ckvn_ref[...] = ckvn.astype(BF16)
        dkr = jnp.zeros((tm, LANES), F32)
        for h in range(MLA_HEADS):
            b0 = h * HEAD_PAD
            dqp_ref[:, b0:b0 + QK_NOPE] = dq_ref[:, b0:b0 + QK_NOPE].astype(BF16)
            dqp_ref[:, b0 + QK_NOPE:b0 + HEAD_PAD] = _rope_t(dq_ref[:, b0 + QK_NOPE:b0 + HEAD_PAD], tabs).astype(BF16)
            dkv_ref[:, b0:b0 + QK_NOPE] = dk_ref[:, b0:b0 + QK_NOPE].astype(BF16)
            dkv_ref[:, b0 + QK_NOPE:b0 + HEAD_PAD] = dv_ref[:, h * V_HEAD:(h + 1) * V_HEAD].astype(BF16)
            dkr = dkr + dk_ref[:, b0 + QK_NOPE:b0 + HEAD_PAD]
        dcqn = _dot(dqp_ref[...], wq_ref[...])
        dckvn = _dot_nt(dkv_ref[...], wkv_ref[...])
        dqg_ref[...] += jnp.sum(dcqn * cqh, axis=0, keepdims=True)
        dkvg_ref[...] += jnp.sum(dckvn * ckvh, axis=0, keepdims=True)
        dxh = dcqn * qg_ref[...]
        dzs_ref[:, 0:Q_LORA] = (rq * (dxh - cqh * jnp.mean(dxh * cqh, axis=-1, keepdims=True))).astype(BF16)
        dxh = dckvn * kvg_ref[...]
        dzs_ref[:, Q_LORA:Q_LORA + KV_LORA] = (
            rkv * (dxh - ckvh * jnp.mean(dxh * ckvh, axis=-1, keepdims=True))).astype(BF16)
        dzs_ref[:, Q_LORA + KV_LORA:ZS_W] = _rope_t(dkr, tabs).astype(BF16)

    full = lambda a: pl.BlockSpec(a.shape, lambda i: (0,) * a.ndim)
    rowb = lambda w: pl.BlockSpec((tm, w), lambda i: (i, 0))
    return pl.pallas_call(
        body, grid=(T // tm,),
        in_specs=[rowb(ZS_W), rowb(1), full(invf), full(qg), full(kvg), full(wuq_p), full(wukv),
                  rowb(HW), rowb(HW), rowb(D_MODEL)],
        out_specs=[rowb(ZS_W), rowb(Q_LORA), rowb(HW), rowb(KV_LORA), rowb(HW), full(qg), full(kvg)],
        out_shape=[jax.ShapeDtypeStruct((T, ZS_W), BF16), jax.ShapeDtypeStruct((T, Q_LORA), BF16),
                   jax.ShapeDtypeStruct((T, HW), BF16), jax.ShapeDtypeStruct((T, KV_LORA), BF16),
                   jax.ShapeDtypeStruct((T, HW), BF16), jax.ShapeDtypeStruct(qg.shape, F32),
                   jax.ShapeDtypeStruct(kvg.shape, F32)],
        name="mla_prep_bwd", compiler_params=_params("arbitrary"))(
            zs, pos, invf, qg, kvg, wuq_p, wukv, dq, dk, dv)


def _causal(tq, kmax, q0):
    r = lax.broadcasted_iota(jnp.int32, (tq, kmax), 0) + q0
    c = lax.broadcasted_iota(jnp.int32, (tq, kmax), 1)
    return c <= r


def _attn_fwd(q, k, v, batch, seq):
    tq = _tile(seq, ATTN_TILE)
    nq = seq // tq

    def body(q_ref, k_ref, v_ref, o_ref, lse_ref):
        diag = _causal(tq, tq, 0)
        for qi in range(nq):
            rows = slice(qi * tq, (qi + 1) * tq)
            qr = q_ref[rows, :]
            s_d = jnp.where(diag, _dot_nt(qr, k_ref[rows, :]) * ATTN_SCALE, NEG_BIG)
            m = jnp.max(s_d, axis=-1, keepdims=True)
            if qi > 0:
                before = slice(0, qi * tq)
                s_b = _dot_nt(qr, k_ref[before, :]) * ATTN_SCALE
                m = jnp.maximum(m, jnp.max(s_b, axis=-1, keepdims=True))
                p_b = jnp.exp(s_b - m)
                l = jnp.sum(p_b, axis=-1, keepdims=True)
                acc = _dot(p_b.astype(BF16), v_ref[before, :])
            p_d = jnp.exp(s_d - m)
            l_d = jnp.sum(p_d, axis=-1, keepdims=True)
            acc_d = _dot(p_d.astype(BF16), v_ref[rows, :])
            l, acc = (l + l_d, acc + acc_d) if qi > 0 else (l_d, acc_d)
            o_ref[rows, :] = acc / l
            lse_ref[rows, :] = jnp.broadcast_to(m + jnp.log(l), (tq, V_HEAD))

    return pl.pallas_call(
        body, grid=(batch, MLA_HEADS),
        in_specs=[pl.BlockSpec((seq, HEAD_PAD), lambda b, h: (b, h)),
                  pl.BlockSpec((seq, HEAD_PAD), lambda b, h: (b, h)),
                  pl.BlockSpec((seq, V_HEAD), lambda b, h: (b, h))],
        out_specs=[pl.BlockSpec((seq, V_HEAD), lambda b, h: (b, h)),
                   pl.BlockSpec((seq, V_HEAD), lambda b, h: (b, h))],
        out_shape=[jax.ShapeDtypeStruct((batch * seq, D_MODEL), F32),
                   jax.ShapeDtypeStruct((batch * seq, D_MODEL), F32)],
        name="attn_fwd", compiler_params=_params("parallel", "parallel"))(q, k, v)


def _attn_bwd(q, k, v, o, do, lse, batch, seq, dep):
    tq = _tile(seq, ATTN_TILE)
    nq = seq // tq

    def body(q_ref, k_ref, v_ref, o_ref, do_ref, lse_ref, dep_ref, dq_ref, dk_ref, dv_ref):
        dk_ref[...] = jnp.zeros_like(dk_ref)
        dv_ref[...] = jnp.zeros_like(dv_ref)
        diag = _causal(tq, tq, 0)
        for qi in range(nq):
            rows = slice(qi * tq, (qi + 1) * tq)
            qr = q_ref[rows, :]
            dor = do_ref[rows, :]
            lse = lse_ref[rows, 0:1]
            delta = jnp.sum(dor.astype(F32) * o_ref[rows, :], axis=-1, keepdims=True)

            def block(keys, mask):
                kk = k_ref[keys, :]
                p = jnp.exp(_dot_nt(qr, kk) * ATTN_SCALE - lse)
                if mask is not None:
                    p = jnp.where(mask, p, 0.0)
                dp = _dot_nt(dor, v_ref[keys, :])
                ds = (p * (dp - delta) * ATTN_SCALE).astype(BF16)
                dk_ref[keys, :] += _dot_tn(ds, qr)
                dv_ref[keys, :] += _dot_tn(p.astype(BF16), dor)
                return _dot(ds, kk)

            dq = block(rows, diag)
            if qi > 0:
                dq = dq + block(slice(0, qi * tq), None)
            dq_ref[rows, :] = dq

    qspec = pl.BlockSpec((seq, HEAD_PAD), lambda b, h: (b, h))
    vspec = pl.BlockSpec((seq, V_HEAD), lambda b, h: (b, h))
    T = batch * seq
    return pl.pallas_call(
        body, grid=(batch, MLA_HEADS),
        in_specs=[qspec, qspec, vspec, vspec, vspec, vspec, ANY],
        out_specs=[qspec, qspec, vspec],
        out_shape=[jax.ShapeDtypeStruct((T, MLA_HEADS * HEAD_PAD), F32),
                   jax.ShapeDtypeStruct((T, MLA_HEADS * HEAD_PAD), F32),
                   jax.ShapeDtypeStruct((T, D_MODEL), F32)],
        name="attn_bwd", compiler_params=_params("parallel", "parallel"))(q, k, v, o, do, lse, dep)


def _merge_out(x, yag, zm, o, w_out, ffn_g):
    T = x.shape[0]
    tm = _tile(T, 512)

    def body(x_ref, ya_ref, gb_ref, o_ref, w_ref, g_ref, mg_ref, x1_ref, h2_ref):
        mg = (ya_ref[...] + _sigmoid(gb_ref[...]) * o_ref[...]).astype(BF16)
        mg_ref[...] = mg
        x1 = x_ref[...] + _dot(mg, w_ref[...])
        x1_ref[...] = x1
        r = lax.rsqrt(jnp.mean(x1 * x1, axis=-1, keepdims=True) + EPS)
        h2_ref[...] = (x1 * r * g_ref[...]).astype(BF16)

    row = pl.BlockSpec((tm, D_MODEL), lambda i: (i, 0))
    return pl.pallas_call(
        body, grid=(T // tm,),
        in_specs=[row, row, pl.BlockSpec((tm, D_MODEL), lambda i: (i, 3)), row,
                  pl.BlockSpec((D_MODEL, D_MODEL), lambda i: (0, 0)), pl.BlockSpec((1, D_MODEL), lambda i: (0, 0))],
        out_specs=[row, row, row],
        out_shape=[jax.ShapeDtypeStruct((T, D_MODEL), BF16), jax.ShapeDtypeStruct((T, D_MODEL), F32),
                   jax.ShapeDtypeStruct((T, D_MODEL), BF16)],
        name="merge_out", compiler_params=_params("parallel"))(x, yag, zm, o, w_out, ffn_g)


FF_TILE = 256
FF_BLOCKS = D_FF // FF_TILE
FFB_TILE = 128


def _shift_down(x, k):
    row = lax.broadcasted_iota(jnp.int32, x.shape, 0)
    return jnp.where(row >= k, pltpu.roll(x, k, 0), 0.0)


def _shift_up(x, k):
    n = x.shape[0]
    row = lax.broadcasted_iota(jnp.int32, x.shape, 0)
    return jnp.where(row < n - k, pltpu.roll(x, n - k, 0), 0.0)


def _conv(x, w_ref, b_ref):
    return b_ref[...] + w_ref[2:3, :] * x + w_ref[1:2, :] * _shift_down(x, 1) + w_ref[0:1, :] * _shift_down(x, 2)


def _up_act(h2, wt_up, cw, cb, batch, seq):
    def body(h_ref, wug_ref, wuv_ref, wg_ref, wv_ref, bg_ref, bv_ref, ug_ref, uv_ref, g_ref, v_ref, a_ref):
        h = h_ref[...]
        ug = _dot_nt(h, wug_ref[...])
        uv = _dot_nt(h, wuv_ref[...])
        ug_ref[...] = ug
        uv_ref[...] = uv
        gate = _conv(ug, wg_ref, bg_ref)
        val = _conv(uv, wv_ref, bv_ref)
        g_ref[...] = gate
        v_ref[...] = val
        a_ref[...] = (gate * _sigmoid(gate) * val).astype(BF16)

    blk = pl.BlockSpec((seq, FF_TILE), lambda b, j: (b, j))
    wup = lambda off: pl.BlockSpec((FF_TILE, D_MODEL), lambda b, j: (j + off, 0))
    wsp = lambda off: pl.BlockSpec((3, FF_TILE), lambda b, j: (0, j + off))
    bsp = lambda off: pl.BlockSpec((1, FF_TILE), lambda b, j: (0, j + off))
    T = batch * seq
    f32 = jax.ShapeDtypeStruct((T, D_FF), F32)
    return pl.pallas_call(
        body, grid=(batch, FF_BLOCKS),
        in_specs=[pl.BlockSpec((seq, D_MODEL), lambda b, j: (b, 0)), wup(0), wup(FF_BLOCKS),
                  wsp(0), wsp(FF_BLOCKS), bsp(0), bsp(FF_BLOCKS)],
        out_specs=[blk] * 5,
        out_shape=[f32, f32, f32, f32, jax.ShapeDtypeStruct((T, D_FF), BF16)],
        name="up_act", compiler_params=_params("parallel", "arbitrary"))(h2, wt_up, wt_up, cw, cw, cb, cb)


def _ffn_act_bwd(upg, upv, gate, val, cw, dx2b, w_down, batch, seq):
    def half(du, x, w_ref, dx_ref, dw_ref, db_ref):
        up1, up2 = _shift_up(du, 1), _shift_up(du, 2)
        dx_ref[...] = (w_ref[2:3, :] * du + w_ref[1:2, :] * up1 + w_ref[0:1, :] * up2).astype(BF16)
        dw_ref[2:3, :] += jnp.sum(du * x, axis=0, keepdims=True)
        dw_ref[1:2, :] += jnp.sum(up1 * x, axis=0, keepdims=True)
        dw_ref[0:1, :] += jnp.sum(up2 * x, axis=0, keepdims=True)
        db_ref[...] += jnp.sum(du, axis=0, keepdims=True)

    def body(ug_ref, uv_ref, g_ref, v_ref, wg_ref, wv_ref, dx_ref, wd_ref,
             dg_ref, dv_ref, dwg_ref, dwv_ref, dbg_ref, dbv_ref):
        @pl.when(pl.program_id(1) == 0)
        def _():
            for r in (dwg_ref, dwv_ref, dbg_ref, dbv_ref):
                r[...] = jnp.zeros_like(r)

        gate, val = g_ref[...], v_ref[...]
        sg = _sigmoid(gate)
        dav = _dot_nt(dx_ref[...], wd_ref[...])
        half(dav * val * sg * (1.0 + gate * (1.0 - sg)), ug_ref[...], wg_ref, dg_ref, dwg_ref, dbg_ref)
        half(dav * gate * sg, uv_ref[...], wv_ref, dv_ref, dwv_ref, dbv_ref)

    nb = D_FF // FFB_TILE
    blk = pl.BlockSpec((seq, FFB_TILE), lambda j, b: (b, j))
    wsp = lambda off: pl.BlockSpec((3, FFB_TILE), lambda j, b: (0, j + off))
    bsp = pl.BlockSpec((1, FFB_TILE), lambda j, b: (0, j))
    T = batch * seq
    return pl.pallas_call(
        body, grid=(nb, batch),
        in_specs=[blk, blk, blk, blk, wsp(0), wsp(nb),
                  pl.BlockSpec((seq, D_MODEL), lambda j, b: (b, 0)),
                  pl.BlockSpec((FFB_TILE, D_MODEL), lambda j, b: (j, 0))],
        out_specs=[blk, blk, wsp(0), wsp(0), bsp, bsp],
        out_shape=[jax.ShapeDtypeStruct((T, D_FF), BF16), jax.ShapeDtypeStruct((T, D_FF), BF16),
                   jax.ShapeDtypeStruct((3, D_FF), F32), jax.ShapeDtypeStruct((3, D_FF), F32),
                   jax.ShapeDtypeStruct((1, D_FF), F32), jax.ShapeDtypeStruct((1, D_FF), F32)],
        name="ffn_act_bwd", compiler_params=_params("parallel", "arbitrary"))(
            upg, upv, gate, val, cw, cw, dx2b, w_down)


def _down_loss(a, w_down, x1, target, gfin):
    T = x1.shape[0]
    tm = _tile(T, 512)

    def body(a_ref, w_ref, x1_ref, t_ref, g_ref, dx_ref, dxb_ref, loss_ref, dg_ref):
        @pl.when(pl.program_id(0) == 0)
        def _():
            loss_ref[...] = jnp.zeros_like(loss_ref)
            dg_ref[...] = jnp.zeros_like(dg_ref)

        x2 = x1_ref[...] + _dot(a_ref[...], w_ref[...])
        r = lax.rsqrt(jnp.mean(x2 * x2, axis=-1, keepdims=True) + EPS)
        xh = x2 * r
        g = g_ref[...]
        diff = xh * g - t_ref[...]
        loss_ref[...] += 0.5 * jnp.sum(jnp.mean(diff * diff, axis=-1, keepdims=True))
        dy = diff * (1.0 / D_MODEL)
        dg_ref[...] += jnp.sum(dy * xh, axis=0, keepdims=True)
        dxh = dy * g
        dx = r * (dxh - xh * jnp.mean(dxh * xh, axis=-1, keepdims=True))
        dx_ref[...] = dx
        dxb_ref[...] = dx.astype(BF16)

    row = pl.BlockSpec((tm, D_MODEL), lambda i: (i, 0))
    vec = pl.BlockSpec((1, D_MODEL), lambda i: (0, 0))
    return pl.pallas_call(
        body, grid=(T // tm,),
        in_specs=[pl.BlockSpec((tm, D_FF), lambda i: (i, 0)),
                  pl.BlockSpec((D_FF, D_MODEL), lambda i: (0, 0)), row, row, vec],
        out_specs=[row, row, pl.BlockSpec((8, LANES), lambda i: (0, 0)), vec],
        out_shape=[jax.ShapeDtypeStruct((T, D_MODEL), F32), jax.ShapeDtypeStruct((T, D_MODEL), BF16),
                   jax.ShapeDtypeStruct((8, LANES), F32), jax.ShapeDtypeStruct((1, D_MODEL), F32)],
        name="down_loss", compiler_params=_params("arbitrary"))(a, w_down, x1, target, gfin)


def _local_step(x, positions, target, mix_norm, av_g, av_b, w_s, b_s, q_norm, kv_norm, ffn_norm, conv_b,
                final_norm, comm):
    batch, seq, _ = x.shape
    T = batch * seq
    x = x.reshape(T, D_MODEL)
    target = target.reshape(T, D_MODEL)
    pos = positions.reshape(T, 1)
    half = jnp.arange(0, QK_ROPE, 2, dtype=F32) / QK_ROPE
    inv_freq = 1.0 / (ROPE_THETA ** half)
    invf = jnp.concatenate([inv_freq, inv_freq, jnp.zeros((LANES - QK_ROPE,), F32)]).reshape(1, LANES)
    w_st = jnp.swapaxes(w_s, 1, 2)
    b_col = b_s.reshape(A_GROUPS, CHUNK, 1)

    wt_in = comm.in_weights()
    h, zm, zs = _in_proj(x, mix_norm, wt_in)
    yag = _mixer_a_fwd(zm, av_g, av_b, w_s, b_col)
    wuq_p, wukv, w_out = comm.mla_weights(after=yag)
    q, k, v = _mla_prep_fwd(zs, pos, invf, q_norm, kv_norm, wuq_p, wukv)
    o, lse = _attn_fwd(q, k, v, batch, seq)
    merged, x1, h2 = _merge_out(x, yag, zm, o, w_out, ffn_norm)
    wt_up, conv_w, w_down = comm.ffn_weights(after=merged)
    upg, upv, gate, val, act = _up_act(h2, wt_up, conv_w, conv_b, batch, seq)
    dx2, dx2b, loss_acc, d_final = _down_loss(act, w_down, x1, target, final_norm)

    d_wdown = _mm_tn(act, dx2b, "dw_down")
    dupg, dupv, dcwg, dcwv, dcbg, dcbv = _ffn_act_bwd(upg, upv, gate, val, conv_w, dx2b, w_down, batch, seq)
    d_wt_up = jnp.concatenate([_mm_tn(dupg, h2, "dw_up_gate"), _mm_tn(dupv, h2, "dw_up_val")], axis=0)
    dx1, d_ffn_norm, dmerged = _proj_bwd(
        [dupg, dupv], wt_up, [(0, (0, D_FF), (0, D_FF)), (1, (0, D_FF), (D_FF, 2 * D_FF))],
        x1, ffn_norm, dx2, "up_proj_bwd", w2=w_out)
    d_wout = _mm_tn(merged, dx1, "dw_out")
    token = comm.send_ffn_grads(d_wdown, d_wt_up, jnp.concatenate([dcwg, dcwv], axis=1), d_wout)
    dzm, do, d_avg, d_avb, d_ws, d_bs = _mixer_bwd(zm, o, dmerged, av_g, av_b, w_s, w_st, b_col, token)
    token = comm.send_small_grads(_pack_small(dict(
        a_v_norm_g=d_avg, a_v_norm_b=d_avb, a_spatial_w=d_ws, a_spatial_b=d_bs, ffn_norm=d_ffn_norm,
        conv_b=jnp.concatenate([dcbg, dcbv], axis=1), final_norm=d_final), SMALL_EARLY))
    dq, dk, dv = _attn_bwd(q, k, v, o, do, lse, batch, seq, token)
    dzs, cqn, dqp, ckvn, dkv, d_qn, d_kvn = _mla_prep_bwd(zs, pos, invf, q_norm, kv_norm, wuq_p, wukv, dq, dk, dv)
    d_wt_main = _mm_tn(dzm, h, "dw_in_main")
    d_wt_zs = _mm_tn(dzs, h, "dw_in_small")
    token = comm.send_in_grads(d_wt_main, d_wt_zs)
    d_wuq_p = _mm_tn(dqp, cqn, "dw_uq", dep=token)
    d_wukv = _mm_tn(ckvn, dkv, "dw_ukv", dep=token)
    terms = [(0, (i * D_MODEL, (i + 1) * D_MODEL), rows) for i, rows in enumerate(IN_ROWS_MAIN)]
    terms.append((1, (0, ZS_W), IN_ROWS_ZS))
    dx, d_mix_norm = _proj_bwd([dzm, dzs], wt_in, terms, x, mix_norm, dx1, "in_proj_bwd", dep=token)
    late = _pack_small(dict(q_a_norm=d_qn, kv_a_norm=d_kvn, mix_norm=d_mix_norm), SMALL_LATE,
                       extra=loss_acc[0, 0].reshape(1))
    token = comm.send_late_grads(d_wuq_p, d_wukv, late)
    return dx.reshape(batch, seq, D_MODEL), token


MESH_ID = pl.DeviceIdType.MESH
EFFECT = pltpu.SideEffectType.DATAFLOW_SIDE_EFFECTING


def _mesh_pos():
    return lax.axis_index("x"), lax.axis_index("y"), lax.axis_index("c")


def _peer(pos, d):
    x, y, c = pos
    px = 1 - x if d & 4 else x
    py = 1 - y if d & 2 else y
    pc = 1 - c if d & 1 else c
    return (px, py, pc), 4 * px + 2 * py + pc


def _copy(src_ref, land_ref, send_sems, recv_sems, a, d, pos, exchange, landing_here):
    peer, pid = _peer(pos, d)
    me = 4 * pos[0] + 2 * pos[1] + pos[2]
    if exchange:
        src, dst = src_ref.at[pid], land_ref.at[d]
    else:
        src, dst = src_ref, land_ref.at[pid if landing_here else me]
    return pltpu.make_async_remote_copy(
        src_ref=src, dst_ref=dst, send_sem=send_sems.at[a * (N_DEV - 1) + d - 1],
        recv_sem=recv_sems.at[a * (N_DEV - 1) + d - 1],
        device_id=peer, device_id_type=MESH_ID)


def _start_copies(groups, modes, name, dep=None):
    sizes = [len(g) for g in groups]
    srcs = [s for g in groups for s, _ in g]
    lands = [l for g in groups for _, l in g]
    n, ng = len(srcs), len(groups)
    n_in = 2 * n + (dep is not None)

    def body(*refs):
        src_refs, land_refs = refs[:n], refs[n:2 * n]
        sems = refs[n_in:n_in + 2 * ng]
        token = refs[-1]
        pos = _mesh_pos()
        k = 0
        for gi, size in enumerate(sizes):
            for a in range(size):
                for d in range(1, N_DEV):
                    _copy(src_refs[k], land_refs[k], sems[2 * gi], sems[2 * gi + 1], a, d, pos, modes[gi],
                          landing_here=False).start()
                k += 1
        token[...] = jnp.zeros_like(token)

    sem_shapes = [pltpu.SemaphoreType.DMA((size * (N_DEV - 1),)) for size in sizes for _ in range(2)]
    out = pl.pallas_call(
        body, name=name,
        out_shape=(*sem_shapes, *[pltpu.HBM(a.shape, a.dtype) for a in srcs + lands],
                   jax.ShapeDtypeStruct((8, LANES), F32)),
        in_specs=[HBM] * (2 * n) + [ANY] * (dep is not None),
        out_specs=(*[SEM] * (2 * ng), *[HBM] * (2 * n), pl.BlockSpec(memory_space=pltpu.VMEM)),
        input_output_aliases={i: 2 * ng + i for i in range(2 * n)},
        compiler_params=pltpu.CompilerParams(has_side_effects=EFFECT),
    )(*[pltpu.with_memory_space_constraint(a, pltpu.HBM) for a in srcs + lands], *([dep] if dep is not None else []))
    thru = out[2 * ng:2 * ng + 2 * n]
    handles, k = [], 0
    for gi, size in enumerate(sizes):
        handles.append((out[2 * gi], out[2 * gi + 1], thru[k:k + size], thru[n + k:n + k + size]))
        k += size
    return handles, out[-1]


def _wait_copies(handle, exchange, after, name):
    send_sems, recv_sems, srcs, lands = handle
    n = len(srcs)

    def body(*refs):
        src_refs, land_refs = refs[:n], refs[n:2 * n]
        send, recv = refs[2 * n], refs[2 * n + 1]
        pos = _mesh_pos()
        for a in range(n):
            for d in range(1, N_DEV):
                cp = _copy(src_refs[a], land_refs[a], send, recv, a, d, pos, exchange, landing_here=True)
                cp.wait_send()
                cp.wait_recv()

    out = pl.pallas_call(
        body, name=name,
        out_shape=tuple(pltpu.HBM(a.shape, a.dtype) for a in (*srcs, *lands)),
        in_specs=[HBM] * (2 * n) + [SEM, SEM, ANY], out_specs=[HBM] * (2 * n),
        input_output_aliases={i: i for i in range(2 * n)},
        compiler_params=pltpu.CompilerParams(has_side_effects=EFFECT),
    )(*srcs, *lands, send_sems, recv_sems, after)
    return out[n:]


def _land_gather(a, me):
    zone = jnp.zeros((N_DEV,) + a.shape, a.dtype)
    return lax.dynamic_update_slice(zone, a[None], (me,) + (0,) * a.ndim)


def _land_exchange(s, me):
    own = lax.dynamic_index_in_dim(s, me, 0, keepdims=True)
    return lax.dynamic_update_slice(jnp.zeros_like(s), own, (0,) * s.ndim)


def _gather_now(a, name):
    def body(x_ref, out_ref, send_sems, recv_sems, local_sem):
        x, y, c = _mesh_pos()
        me, sibling = (x, y, c), (x, y, 1 - c)
        chips = [(1 - x, y), (x, 1 - y), (1 - x, 1 - y)]

        def slot(p):
            return out_ref.at[4 * p[0] + 2 * p[1] + p[2]]

        def copy(k, block, to, src=None):
            return pltpu.make_async_remote_copy(
                src_ref=slot(block) if src is None else src, dst_ref=slot(block), send_sem=send_sems.at[k],
                recv_sem=recv_sems.at[k], device_id=to, device_id_type=MESH_ID)

        mine = pltpu.make_async_copy(x_ref, slot(me), local_sem)
        mine.start()
        first = [copy(0, me, sibling, src=x_ref)]
        first += [copy(1 + j, me, (*chip, c), src=x_ref) for j, chip in enumerate(chips)]
        for cp in first:
            cp.start()
        passed = [copy(4 + j, (*chip, c), sibling) for j, chip in enumerate(chips)]
        for j, chip in enumerate(chips):
            copy(1 + j, (*chip, c), me).wait_recv()
            passed[j].start()
        copy(0, sibling, me).wait_recv()
        for j, chip in enumerate(chips):
            copy(4 + j, (*chip, 1 - c), me).wait_recv()
        for cp in first + passed:
            cp.wait_send()
        mine.wait()

    return pl.pallas_call(
        body, in_specs=[ANY], out_specs=ANY,
        out_shape=jax.ShapeDtypeStruct((N_DEV,) + a.shape, a.dtype),
        scratch_shapes=[pltpu.SemaphoreType.DMA((N_DEV - 1,)), pltpu.SemaphoreType.DMA((N_DEV - 1,)),
                        pltpu.SemaphoreType.DMA],
        name=name, compiler_params=pltpu.CompilerParams(has_side_effects=True))(a)


def _adamw(parts, w, m, v, name):
    R, C = w.shape
    tr, tc = R, C
    if N_DEV * R * C * parts.dtype.itemsize > SMALL_BLOCK_BYTES:
        tr = next((t for t in range(min(R, 256) // 16 * 16, 15, -16) if R % t == 0), R)
        if tr == R:
            tc = _tile(C, 256)
    c1 = 1.0 - ADAM_B1 ** ADAM_STEP
    c2 = 1.0 - ADAM_B2 ** ADAM_STEP

    def body(p_ref, w_ref, m_ref, v_ref, g_ref, d_ref, nm_ref, nv_ref):
        g = p_ref[0].astype(F32)
        for k in range(1, N_DEV):
            g = g + p_ref[k].astype(F32)
        nm = ADAM_B1 * m_ref[...] + (1.0 - ADAM_B1) * g
        nv = ADAM_B2 * v_ref[...] + (1.0 - ADAM_B2) * (g * g)
        g_ref[...] = g
        nm_ref[...] = nm
        nv_ref[...] = nv
        d_ref[...] = -ADAM_LR * ((nm / c1) / (jnp.sqrt(nv / c2) + ADAM_EPS) + ADAM_WD * w_ref[...])

    blk = pl.BlockSpec((tr, tc), lambda i, j: (i, j))
    shp = jax.ShapeDtypeStruct((R, C), F32)
    return pl.pallas_call(
        body, grid=(R // tr, C // tc),
        in_specs=[pl.BlockSpec((N_DEV, tr, tc), lambda i, j: (0, i, j)), blk, blk, blk],
        out_specs=[blk, blk, blk, blk], out_shape=[shp, shp, shp, shp],
        name=name, compiler_params=_params("parallel", "parallel"))(parts, w, m, v)


SPLIT_V = 2 * D_MODEL
SPLIT_KR = SPLIT_V + Q_LORA + KV_LORA + QK_ROPE
IN_DIM = SPLIT_KR + 2 * D_MODEL
IN_ROWS_MAIN = ((0, D_MODEL), (D_MODEL, SPLIT_V), (SPLIT_KR, SPLIT_KR + D_MODEL), (SPLIT_KR + D_MODEL, IN_DIM))
IN_ROWS_ZS = (SPLIT_V, SPLIT_V + ZS_W)

SMALL_EARLY = ("a_v_norm_g", "a_v_norm_b", "a_spatial_w", "a_spatial_b", "ffn_norm", "conv_b", "final_norm")
SMALL_LATE = ("q_a_norm", "kv_a_norm", "mix_norm")


def _pack_rows(a):
    flat = a.reshape(-1)
    rows = -(-flat.shape[0] // LANES)
    rows8 = -(-rows // 8) * 8
    return jnp.pad(flat, (0, rows8 * LANES - flat.shape[0])).reshape(rows8, LANES)


def _pack_small(tree, names, extra=None):
    parts = [_pack_rows(tree[n]) for n in names]
    if extra is not None:
        parts.append(_pack_rows(extra))
    return jnp.concatenate(parts, axis=0)


def _unpack_small(buf, names, shapes):
    out, r = {}, 0
    for n in names:
        size = math.prod(shapes[n])
        rows8 = -(-(-(-size // LANES)) // 8) * 8
        out[n] = buf[r:r + rows8].reshape(-1)[:size].reshape(shapes[n])
        r += rows8
    return out, r


def _cols_from_shards(g):
    return jnp.transpose(g, (1, 0, 2)).reshape(g.shape[1], N_DEV * g.shape[2])


def _shards_from_cols(a):
    R, W = a.shape
    return jnp.transpose(a.reshape(R, N_DEV, W // N_DEV), (1, 0, 2))


class _Comm:
    GATHER_GROUPS = (("w_uq", "w_ukv", "w_out"), ("w_up", "conv_w", "w_down"))
    FFN_GRADS = ("w_down", "w_up", "conv_w", "w_out")
    LATE_GRADS = ("w_uq", "w_ukv")
    TRANSPOSED = ("w_in", "w_up", "w_uq")

    def __init__(self, shards, me):
        self.me = me
        local = {n: a.astype(F32 if n == "conv_w" else BF16) for n, a in shards.items()}
        self.g_in = _gather_now(local["w_in"], "gather_w_in")
        groups = [[(local[n], _land_gather(local[n], me)) for n in g] for g in self.GATHER_GROUPS]
        (self.h_mla, self.h_ffn), _ = _start_copies(groups, [False] * 2, "gather_start", dep=self.g_in)

    def in_weights(self):
        return self.g_in.reshape(IN_DIM, D_MODEL)

    def mla_weights(self, after):
        g_uq, g_ukv, g_out = _wait_copies(self.h_mla, False, after, "gather_wait_mla")
        wuq_p = jnp.pad(g_uq, ((0, 0), (0, HEAD_PAD - QK_HEAD), (0, 0))).reshape(MLA_HEADS * HEAD_PAD, Q_LORA)
        return wuq_p, _cols_from_shards(g_ukv), g_out.reshape(D_MODEL, D_MODEL)

    def ffn_weights(self, after):
        g_up, g_cw, g_down = _wait_copies(self.h_ffn, False, after, "gather_wait_ffn")
        return g_up.reshape(2 * D_FF, D_MODEL), _cols_from_shards(g_cw), g_down.reshape(D_FF, D_MODEL)

    def _exchange_group(self, blocks):
        return [(s, _land_exchange(s, self.me)) for s in blocks]

    def send_ffn_grads(self, d_wdown, d_wt_up, d_convw, d_wout):
        group = self._exchange_group([
            d_wdown.reshape(N_DEV, D_FF // N_DEV, D_MODEL), d_wt_up.reshape(N_DEV, 2 * D_FF // N_DEV, D_MODEL),
            _shards_from_cols(d_convw), d_wout.reshape(N_DEV, D_MODEL // N_DEV, D_MODEL)])
        (self.h_ffn_grads,), token = _start_copies([group], [True], "ffn_grads_start")
        return token

    def send_small_grads(self, packed):
        (self.h_small_early,), token = _start_copies(
            [[(packed, _land_gather(packed, self.me))]], [False], "small_grads_start")
        return token

    def send_in_grads(self, d_wt_main, d_wt_zs):
        d_in = jnp.concatenate([d_wt_main[:SPLIT_V], d_wt_zs[:SPLIT_KR - SPLIT_V], d_wt_main[SPLIT_V:]], axis=0)
        blocks = d_in.reshape(N_DEV, IN_DIM // N_DEV, D_MODEL)
        (self.h_in_grads,), token = _start_copies([self._exchange_group([blocks])], [True], "in_grads_start")
        return token

    def send_late_grads(self, d_wuq_p, d_wukv, packed):
        d_uq = d_wuq_p.reshape(MLA_HEADS, HEAD_PAD, Q_LORA)[:, :QK_HEAD, :]
        group = self._exchange_group([d_uq, _shards_from_cols(d_wukv)])
        small = [(packed, _land_gather(packed, self.me))]
        (self.h_late_grads, self.h_late_small), token = _start_copies(
            [group, small], [True, False], "late_grads_start")
        return token


def kernel(x, positions, mix_norm, w_in, a_v_norm_g, a_v_norm_b, a_spatial_w, a_spatial_b, q_a_norm, w_uq, kv_a_norm, w_ukv, w_out, ffn_norm, w_up, conv_w, conv_b, w_down, final_norm, loss_target, m_mix_norm, m_w_in, m_a_v_norm_g, m_a_v_norm_b, m_a_spatial_w, m_a_spatial_b, m_q_a_norm, m_w_uq, m_kv_a_norm, m_w_ukv, m_w_out, m_ffn_norm, m_w_up, m_conv_w, m_conv_b, m_w_down, m_final_norm, v_mix_norm, v_w_in, v_a_v_norm_g, v_a_v_norm_b, v_a_spatial_w, v_a_spatial_b, v_q_a_norm, v_w_uq, v_kv_a_norm, v_w_ukv, v_w_out, v_ffn_norm, v_w_up, v_conv_w, v_conv_b, v_w_down, v_final_norm):
    names = ("mix_norm", "w_in", "a_v_norm_g", "a_v_norm_b", "a_spatial_w", "a_spatial_b", "q_a_norm", "w_uq",
             "kv_a_norm", "w_ukv", "w_out", "ffn_norm", "w_up", "conv_w", "conv_b", "w_down", "final_norm")
    w = dict(zip(names, (mix_norm, w_in, a_v_norm_g, a_v_norm_b, a_spatial_w, a_spatial_b, q_a_norm, w_uq,
                         kv_a_norm, w_ukv, w_out, ffn_norm, w_up, conv_w, conv_b, w_down, final_norm)))
    m = dict(zip(names, (m_mix_norm, m_w_in, m_a_v_norm_g, m_a_v_norm_b, m_a_spatial_w, m_a_spatial_b,
                         m_q_a_norm, m_w_uq, m_kv_a_norm, m_w_ukv, m_w_out, m_ffn_norm, m_w_up, m_conv_w,
                         m_conv_b, m_w_down, m_final_norm)))
    v = dict(zip(names, (v_mix_norm, v_w_in, v_a_v_norm_g, v_a_v_norm_b, v_a_spatial_w, v_a_spatial_b,
                         v_q_a_norm, v_w_uq, v_kv_a_norm, v_w_ukv, v_w_out, v_ffn_norm, v_w_up, v_conv_w,
                         v_conv_b, v_w_down, v_final_norm)))
    shapes = {n: w[n].shape for n in names}
    me = 4 * lax.axis_index("x") + 2 * lax.axis_index("y") + lax.axis_index("c")
    def view(tree, n):
        a = tree[n].reshape(tree[n].shape[-2:])
        return a.T if n in _Comm.TRANSPOSED else a

    comm = _Comm({n: view(w, n) for n in ("w_in",) + _Comm.GATHER_GROUPS[0] + _Comm.GATHER_GROUPS[1]}, me)

    grad_x, token = _local_step(
        x, positions, loss_target, w["mix_norm"], w["a_v_norm_g"], w["a_v_norm_b"], w["a_spatial_w"][0],
        w["a_spatial_b"][0], w["q_a_norm"], w["kv_a_norm"], w["ffn_norm"], w["conv_b"],
        w["final_norm"].reshape(1, D_MODEL), comm)

    out_g, out_d, out_m, out_v = {}, {}, {}, {}

    def update(n, parts):
        res = _adamw(parts, view(w, n), view(m, n), view(v, n), "adamw_" + n)
        out_g[n], out_d[n], out_m[n], out_v[n] = (
            (t.T if n in _Comm.TRANSPOSED else t).reshape(shapes[n]) for t in res)

    for n, parts in zip(_Comm.FFN_GRADS, _wait_copies(comm.h_ffn_grads, True, token, "ffn_grads_wait")):
        update(n, parts)
    update("w_in", _wait_copies(comm.h_in_grads, True, out_d["w_up"], "in_grads_wait")[0])
    for n, parts in zip(_Comm.LATE_GRADS, _wait_copies(comm.h_late_grads, True, out_d["w_in"], "late_grads_wait")):
        update(n, parts)

    (early_parts,) = _wait_copies(comm.h_small_early, False, out_d["w_uq"], "small_grads_wait")
    res = _adamw(early_parts, _pack_small(w, SMALL_EARLY), _pack_small(m, SMALL_EARLY), _pack_small(v, SMALL_EARLY),
                 "adamw_small")
    unpacked = [_unpack_small(t, SMALL_EARLY, shapes)[0] for t in res]
    for n in SMALL_EARLY:
        out_g[n], out_d[n], out_m[n], out_v[n] = (u[n] for u in unpacked)

    (late_parts,) = _wait_copies(comm.h_late_small, False, res[1], "late_small_wait")
    zero = jnp.zeros((1,), F32)
    res = _adamw(late_parts, _pack_small(w, SMALL_LATE, extra=zero), _pack_small(m, SMALL_LATE, extra=zero),
                 _pack_small(v, SMALL_LATE, extra=zero), "adamw_late")
    unpacked = [_unpack_small(t, SMALL_LATE, shapes) for t in res]
    for n in SMALL_LATE:
        out_g[n], out_d[n], out_m[n], out_v[n] = (u[0][n] for u in unpacked)
    loss = res[0][unpacked[0][1], 0]

    return (loss, grad_x, *[out_g[n] for n in names], *[out_d[n] for n in names],
            *[out_m[n] for n in names], *[out_v[n] for n in names])
```

```python
import functools
import math

import jax
import jax.numpy as jnp
from jax import lax
from jax.experimental import pallas as pl
from jax.experimental.pallas import tpu as pltpu

F32 = jnp.float32
BF16 = jnp.bfloat16

N_DEV = 8
D_MODEL = 1024
EPS = 1e-6
A_GROUPS = 8
CHUNK = 128
MLA_HEADS = 8
QK_NOPE = 128
QK_ROPE = 64
QK_HEAD = QK_NOPE + QK_ROPE
HEAD_PAD = 256
V_HEAD = 128
Q_LORA = 256
KV_LORA = 128
ROPE_THETA = 10000.0
D_FF = 2816
ZS_W = 512
ATTN_SCALE = QK_HEAD ** -0.5
ATTN_TILE = 512
NEG_BIG = -1e30

ADAM_LR = 0.001
ADAM_B1 = 0.9
ADAM_B2 = 0.999
ADAM_EPS = 1e-08
ADAM_WD = 0.01
ADAM_STEP = 10

VMEM_LIMIT = 56 * 1024 * 1024
SMALL_BLOCK_BYTES = 5 * 1024 * 1024
LANES = 128

GELU_K = math.sqrt(2.0 / math.pi)
GELU_C = 0.044715

ANY = pl.BlockSpec(memory_space=pl.ANY)
HBM = pl.BlockSpec(memory_space=pltpu.HBM)
SEM = pl.BlockSpec(memory_space=pltpu.SEMAPHORE)


def _tile(n, pref):
    for t in (pref, 512, 256, 128, 64, 32, 16, 8):
        if t <= pref and n % t == 0:
            return t
    return n


def _wide_tile(n, cap=1408):
    return next((t for t in range(min(n, cap) // LANES * LANES, 0, -LANES) if n % t == 0), n)


def _params(*sem):
    return pltpu.CompilerParams(dimension_semantics=sem, vmem_limit_bytes=VMEM_LIMIT)


def _dot(a, b):
    return jnp.dot(a, b, preferred_element_type=F32)


def _dot_nt(a, b):
    return lax.dot_general(a, b, (((1,), (1,)), ((), ())), preferred_element_type=F32)


def _dot_tn(a, b):
    return lax.dot_general(a, b, (((0,), (0,)), ((), ())), preferred_element_type=F32)


def _sigmoid(x):
    return 1.0 / (1.0 + jnp.exp(-x))


def _gelu(x):
    t = jnp.tanh(GELU_K * (x + GELU_C * x * x * x))
    return 0.5 * x * (1.0 + t)


def _gelu_grad(x):
    t = jnp.tanh(GELU_K * (x + GELU_C * x * x * x))
    return 0.5 * (1.0 + t) + 0.5 * x * (1.0 - t * t) * GELU_K * (1.0 + 3.0 * GELU_C * x * x)


def _in_proj(x, g, wt):
    T, Dm = x.shape
    tm = _tile(T, 256)

    def body(x_ref, g_ref, wt_ref, h_ref, zm_ref, zs_ref):
        xf = x_ref[...]
        r = lax.rsqrt(jnp.mean(xf * xf, axis=-1, keepdims=True) + EPS)
        h = (xf * r * g_ref[...]).astype(BF16)
        h_ref[...] = h
        for i, (r0, r1) in enumerate(IN_ROWS_MAIN):
            zm_ref[:, i * D_MODEL:(i + 1) * D_MODEL] = _dot_nt(h, wt_ref[r0:r1, :])
        zs_ref[...] = _dot_nt(h, wt_ref[IN_ROWS_ZS[0]:IN_ROWS_ZS[1], :])

    row = lambda n: pl.BlockSpec((tm, n), lambda i: (i, 0))
    return pl.pallas_call(
        body, grid=(T // tm,),
        in_specs=[row(Dm), pl.BlockSpec((1, Dm), lambda i: (0, 0)), pl.BlockSpec(wt.shape, lambda i: (0, 0))],
        out_specs=[row(Dm), row(4 * D_MODEL), row(ZS_W)],
        out_shape=[jax.ShapeDtypeStruct((T, Dm), BF16), jax.ShapeDtypeStruct((T, 4 * D_MODEL), F32),
                   jax.ShapeDtypeStruct((T, ZS_W), F32)],
        name="in_proj", compiler_params=_params("parallel"))(x, g, wt)


def _proj_bwd(acts, wt, terms, x, g, dres, name, w2=None, dep=None):
    T, Dm = x.shape
    tm = _tile(T, 256)
    n_a = len(acts)

    def body(*refs):
        ins, outs = refs[:n_a + 4 + (w2 is not None) + (dep is not None)], refs[-2 - (w2 is not None):]
        wt_ref, x_ref, g_ref, dres_ref = ins[n_a:n_a + 4]
        dx_ref, dg_ref = outs[0], outs[1]

        @pl.when(pl.program_id(0) == 0)
        def _():
            dg_ref[...] = jnp.zeros_like(dg_ref)

        dy = None
        for i, (c0, c1), (r0, r1) in terms:
            t = _dot(ins[i][:, c0:c1], wt_ref[r0:r1, :])
            dy = t if dy is None else dy + t
        xf = x_ref[...]
        r = lax.rsqrt(jnp.mean(xf * xf, axis=-1, keepdims=True) + EPS)
        xh = xf * r
        dg_ref[...] += jnp.sum(dy * xh, axis=0, keepdims=True)
        dxh = dy * g_ref[...]
        dx = dres_ref[...] + r * (dxh - xh * jnp.mean(dxh * xh, axis=-1, keepdims=True))
        dx_ref[...] = dx
        if w2 is not None:
            outs[2][...] = _dot_nt(dx.astype(BF16), ins[n_a + 4][...])

    row = pl.BlockSpec((tm, Dm), lambda i: (i, 0))
    vec = pl.BlockSpec((1, Dm), lambda i: (0, 0))
    in_specs = [pl.BlockSpec((tm, a.shape[1]), lambda i: (i, 0)) for a in acts]
    in_specs += [pl.BlockSpec(wt.shape, lambda i: (0, 0)), row, vec, row]
    args = [*acts, wt, x, g, dres]
    out_specs = [row, vec]
    out_shape = [jax.ShapeDtypeStruct((T, Dm), F32), jax.ShapeDtypeStruct((1, Dm), F32)]
    if w2 is not None:
        in_specs.append(pl.BlockSpec(w2.shape, lambda i: (0, 0)))
        args.append(w2)
        out_specs.append(pl.BlockSpec((tm, w2.shape[0]), lambda i: (i, 0)))
        out_shape.append(jax.ShapeDtypeStruct((T, w2.shape[0]), F32))
    if dep is not None:
        in_specs.append(ANY)
        args.append(dep)
    return pl.pallas_call(
        body, grid=(T // tm,), in_specs=in_specs, out_specs=out_specs, out_shape=out_shape,
        name=name, compiler_params=_params("arbitrary"))(*args)


def _mm_tn(a, b, name, dep=None):
    T, M = a.shape
    N = b.shape[1]
    tm, tn, tt = _wide_tile(M), _wide_tile(N), _tile(T, 2048)
    n_t = T // tt

    def body(a_ref, b_ref, *refs):
        o_ref, acc_ref = refs[-2:]
        t = pl.program_id(2)

        @pl.when(t == 0)
        def _():
            acc_ref[...] = jnp.zeros_like(acc_ref)

        acc_ref[...] += _dot_tn(a_ref[...].astype(BF16), b_ref[...].astype(BF16))

        @pl.when(t == n_t - 1)
        def _():
            o_ref[...] = acc_ref[...].astype(BF16)

    return pl.pallas_call(
        body, grid=(M // tm, N // tn, n_t),
        in_specs=[pl.BlockSpec((tt, tm), lambda i, j, t: (t, i)),
                  pl.BlockSpec((tt, tn), lambda i, j, t: (t, j))] + [ANY] * (dep is not None),
        out_specs=pl.BlockSpec((tm, tn), lambda i, j, t: (i, j)),
        out_shape=jax.ShapeDtypeStruct((M, N), BF16),
        scratch_shapes=[pltpu.VMEM((tm, tn), F32)],
        name=name, compiler_params=_params("parallel", "parallel", "arbitrary"))(
            a, b, *([dep] if dep is not None else []))


def _layer_norm_fwd(gv, g, b):
    mu = jnp.mean(gv, axis=-1, keepdims=True)
    xc = gv - mu
    rs = lax.rsqrt(jnp.mean(xc * xc, axis=-1, keepdims=True) + EPS)
    xh = xc * rs
    return xh, rs, xh * g + b


def _tri_mask(transposed=False):
    r = lax.broadcasted_iota(jnp.int32, (CHUNK, CHUNK), 0)
    c = lax.broadcasted_iota(jnp.int32, (CHUNK, CHUNK), 1)
    return r <= c if transposed else c <= r


def _mixer_a_fwd(zm, av_g, av_b, w_s, b_col):
    T = zm.shape[0]
    tm = _tile(T, 256)
    n_chunk = tm // CHUNK

    def body(u_ref, v_ref, ga_ref, g_ref, b_ref, w_ref, bc_ref, y_ref, vn_s, mx_s):
        gu = _gelu(u_ref[...])
        _, _, vn = _layer_norm_fwd(_gelu(v_ref[...]), g_ref[...], b_ref[...])
        vn_s[...] = vn.astype(BF16)
        tri = _tri_mask()
        for gi in range(A_GROUPS):
            wm = jnp.where(tri, w_ref[gi], 0.0).astype(BF16)
            cols = slice(gi * CHUNK, (gi + 1) * CHUNK)
            for n in range(n_chunk):
                rows = slice(n * CHUNK, (n + 1) * CHUNK)
                mx_s[rows, cols] = _dot(wm, vn_s[rows, cols]) + bc_ref[gi]
        y_ref[...] = _sigmoid(ga_ref[...]) * gu * mx_s[...]

    col = lambda c: pl.BlockSpec((tm, D_MODEL), lambda i: (i, c))
    vec = pl.BlockSpec((1, D_MODEL), lambda i: (0, 0))
    return pl.pallas_call(
        body, grid=(T // tm,),
        in_specs=[col(0), col(1), col(2), vec, vec,
                  pl.BlockSpec((A_GROUPS, CHUNK, CHUNK), lambda i: (0, 0, 0)),
                  pl.BlockSpec((A_GROUPS, CHUNK, 1), lambda i: (0, 0, 0))],
        out_specs=pl.BlockSpec((tm, D_MODEL), lambda i: (i, 0)),
        out_shape=jax.ShapeDtypeStruct((T, D_MODEL), F32),
        scratch_shapes=[pltpu.VMEM((tm, D_MODEL), BF16), pltpu.VMEM((tm, D_MODEL), F32)],
        name="mixer_a_fwd", compiler_params=_params("parallel"))(zm, zm, zm, av_g, av_b, w_s, b_col)


def _mixer_bwd(zm, o, dm, av_g, av_b, w_s, w_st, b_col, dep):
    T = zm.shape[0]
    tm = _tile(T, 256)
    n_chunk = tm // CHUNK

    def body(u_ref, v_ref, ga_ref, gb_ref, o_ref, dm_ref, g_ref, b_ref, w_ref, wt_ref, bc_ref, dep_ref,
             dz_ref, do_ref, dg_ref, db_ref, dw_ref, dbs_ref, vn_s, mx_s, dmx_s, dvn_s):
        @pl.when(pl.program_id(0) == 0)
        def _():
            dg_ref[...] = jnp.zeros_like(dg_ref)
            db_ref[...] = jnp.zeros_like(db_ref)
            dw_ref[...] = jnp.zeros_like(dw_ref)
            dbs_ref[...] = jnp.zeros_like(dbs_ref)

        dm_v = dm_ref[...]
        gb = gb_ref[...]
        sb = _sigmoid(gb)
        o_v = o_ref[...]
        do_ref[...] = (dm_v * sb).astype(BF16)
        dz_ref[:, 3 * D_MODEL:4 * D_MODEL] = (dm_v * o_v * sb * (1.0 - sb)).astype(BF16)
        u = u_ref[...]
        v = v_ref[...]
        gu = _gelu(u)
        xh, rs, vn = _layer_norm_fwd(_gelu(v), g_ref[...], b_ref[...])
        vn_s[...] = vn.astype(BF16)
        tri = _tri_mask()
        for gi in range(A_GROUPS):
            wm = jnp.where(tri, w_ref[gi], 0.0).astype(BF16)
            cols = slice(gi * CHUNK, (gi + 1) * CHUNK)
            for n in range(n_chunk):
                rows = slice(n * CHUNK, (n + 1) * CHUNK)
                mx_s[rows, cols] = _dot(wm, vn_s[rows, cols]) + bc_ref[gi]
        mixed = mx_s[...]
        sa = _sigmoid(ga_ref[...])
        dya = dm_v * sa
        dz_ref[:, 2 * D_MODEL:3 * D_MODEL] = (dm_v * gu * mixed * sa * (1.0 - sa)).astype(BF16)
        dz_ref[:, 0:D_MODEL] = (dya * mixed * _gelu_grad(u)).astype(BF16)
        dmx = dya * gu
        dmx_s[...] = dmx.astype(BF16)
        tri_t = _tri_mask(transposed=True)
        for gi in range(A_GROUPS):
            wmt = jnp.where(tri_t, wt_ref[gi], 0.0).astype(BF16)
            cols = slice(gi * CHUNK, (gi + 1) * CHUNK)
            dw_acc = jnp.zeros((CHUNK, CHUNK), F32)
            dmx_sum = jnp.zeros((CHUNK, CHUNK), F32)
            for n in range(n_chunk):
                rows = slice(n * CHUNK, (n + 1) * CHUNK)
                blk = dmx_s[rows, cols]
                dvn_s[rows, cols] = _dot(wmt, blk)
                dw_acc = dw_acc + _dot_nt(blk, vn_s[rows, cols])
                dmx_sum = dmx_sum + dmx[rows, cols]
            dw_ref[gi] += jnp.where(tri, dw_acc, 0.0)
            dbs_ref[gi] += jnp.sum(dmx_sum, axis=-1, keepdims=True)
        dvn = dvn_s[...]
        dg_ref[...] += jnp.sum(dvn * xh, axis=0, keepdims=True)
        db_ref[...] += jnp.sum(dvn, axis=0, keepdims=True)
        dxh = dvn * g_ref[...]
        dgv = rs * (dxh - jnp.mean(dxh, axis=-1, keepdims=True)
                    - xh * jnp.mean(dxh * xh, axis=-1, keepdims=True))
        dz_ref[:, D_MODEL:2 * D_MODEL] = (dgv * _gelu_grad(v)).astype(BF16)

    col = lambda c: pl.BlockSpec((tm, D_MODEL), lambda i: (i, c))
    row = pl.BlockSpec((tm, D_MODEL), lambda i: (i, 0))
    vec = pl.BlockSpec((1, D_MODEL), lambda i: (0, 0))
    wsp = pl.BlockSpec((A_GROUPS, CHUNK, CHUNK), lambda i: (0, 0, 0))
    bsp = pl.BlockSpec((A_GROUPS, CHUNK, 1), lambda i: (0, 0, 0))
    return pl.pallas_call(
        body, grid=(T // tm,),
        in_specs=[col(0), col(1), col(2), col(3), row, row, vec, vec, wsp, wsp, bsp, ANY],
        out_specs=[pl.BlockSpec((tm, 4 * D_MODEL), lambda i: (i, 0)), row, vec, vec, wsp, bsp],
        out_shape=[jax.ShapeDtypeStruct((T, 4 * D_MODEL), BF16), jax.ShapeDtypeStruct((T, D_MODEL), BF16),
                   jax.ShapeDtypeStruct((1, D_MODEL), F32), jax.ShapeDtypeStruct((1, D_MODEL), F32),
                   jax.ShapeDtypeStruct((A_GROUPS, CHUNK, CHUNK), F32),
                   jax.ShapeDtypeStruct((A_GROUPS, CHUNK, 1), F32)],
        scratch_shapes=[pltpu.VMEM((tm, D_MODEL), BF16), pltpu.VMEM((tm, D_MODEL), F32),
                        pltpu.VMEM((tm, D_MODEL), BF16), pltpu.VMEM((tm, D_MODEL), F32)],
        name="mixer_bwd", compiler_params=_params("arbitrary"))(
            zm, zm, zm, zm, o, dm, av_g, av_b, w_s, w_st, b_col, dep)


def _rope_tables(pos_ref, invf_ref):
    ang = pos_ref[...].astype(F32) * invf_ref[...]
    lane = lax.broadcasted_iota(jnp.int32, ang.shape, 1)
    cos, sin = jnp.cos(ang), jnp.sin(ang)
    c = jnp.where(lane < QK_ROPE, cos, 0.0)
    sa = jnp.where(lane < QK_ROPE // 2, -sin, 0.0)
    sb = jnp.where((lane >= QK_ROPE // 2) & (lane < QK_ROPE), sin, 0.0)
    return c, sa, sb


def _rope(blk, tabs):
    c, sa, sb = tabs
    return blk * c + pltpu.roll(blk, LANES - QK_ROPE // 2, 1) * sa + pltpu.roll(blk, QK_ROPE // 2, 1) * sb


def _rope_t(dout, tabs):
    c, sa, sb = tabs
    return dout * c + pltpu.roll(dout * sa, QK_ROPE // 2, 1) + pltpu.roll(dout * sb, LANES - QK_ROPE // 2, 1)


def _rms_small(x, g):
    r = lax.rsqrt(jnp.mean(x * x, axis=-1, keepdims=True) + EPS)
    xh = x * r
    return xh, r, xh * g


def _mla_prep_fwd(zs, pos, invf, qg, kvg, wuq_p, wukv):
    T = zs.shape[0]
    tm = _tile(T, 512)
    HW = MLA_HEADS * HEAD_PAD

    def body(zs_ref, pos_ref, invf_ref, qg_ref, kvg_ref, wq_ref, wkv_ref, q_ref, k_ref, v_ref):
        tabs = _rope_tables(pos_ref, invf_ref)
        _, _, cqn = _rms_small(zs_ref[:, 0:Q_LORA], qg_ref[...])
        _, _, ckvn = _rms_small(zs_ref[:, Q_LORA:Q_LORA + KV_LORA], kvg_ref[...])
        q = _dot_nt(cqn.astype(BF16), wq_ref[...])
        kv = _dot(ckvn.astype(BF16), wkv_ref[...])
        kr = _rope(zs_ref[:, Q_LORA + KV_LORA:ZS_W], tabs).astype(BF16)
        for h in range(MLA_HEADS):
            b0 = h * HEAD_PAD
            q_ref[:, b0:b0 + QK_NOPE] = q[:, b0:b0 + QK_NOPE].astype(BF16)
            q_ref[:, b0 + QK_NOPE:b0 + HEAD_PAD] = _rope(q[:, b0 + QK_NOPE:b0 + HEAD_PAD], tabs).astype(BF16)
            k_ref[:, b0:b0 + QK_NOPE] = kv[:, b0:b0 + QK_NOPE].astype(BF16)
            k_ref[:, b0 + QK_NOPE:b0 + HEAD_PAD] = kr
            v_ref[:, h * V_HEAD:(h + 1) * V_HEAD] = kv[:, b0 + QK_NOPE:b0 + HEAD_PAD].astype(BF16)

    full = lambda a: pl.BlockSpec(a.shape, lambda i: (0,) * a.ndim)
    return pl.pallas_call(
        body, grid=(T // tm,),
        in_specs=[pl.BlockSpec((tm, ZS_W), lambda i: (i, 0)), pl.BlockSpec((tm, 1), lambda i: (i, 0)),
                  full(invf), full(qg), full(kvg), full(wuq_p), full(wukv)],
        out_specs=[pl.BlockSpec((tm, HW), lambda i: (i, 0)), pl.BlockSpec((tm, HW), lambda i: (i, 0)),
                   pl.BlockSpec((tm, D_MODEL), lambda i: (i, 0))],
        out_shape=[jax.ShapeDtypeStruct((T, HW), BF16), jax.ShapeDtypeStruct((T, HW), BF16),
                   jax.ShapeDtypeStruct((T, D_MODEL), BF16)],
        name="mla_prep_fwd", compiler_params=_params("parallel"))(zs, pos, invf, qg, kvg, wuq_p, wukv)


def _mla_prep_bwd(zs, pos, invf, qg, kvg, wuq_p, wukv, dq, dk, dv):
    T = zs.shape[0]
    tm = _tile(T, 256)
    HW = MLA_HEADS * HEAD_PAD

    def body(zs_ref, pos_ref, invf_ref, qg_ref, kvg_ref, wq_ref, wkv_ref, dq_ref, dk_ref, dv_ref,
             dzs_ref, cqn_ref, dqp_ref, ckvn_ref, dkv_ref, dqg_ref, dkvg_ref):
        @pl.when(pl.program_id(0) == 0)
        def _():
            dqg_ref[...] = jnp.zeros_like(dqg_ref)
            dkvg_ref[...] = jnp.zeros_like(dkvg_ref)

        tabs = _rope_tables(pos_ref, invf_ref)
        cqh, rq, cqn = _rms_small(zs_ref[:, 0:Q_LORA], qg_ref[...])
        ckvh, rkv, ckvn = _rms_small(zs_ref[:, Q_LORA:Q_LORA + KV_LORA], kvg_ref[...])
        cqn_ref[...] = cqn.astype(BF16)
        ckvn_ref[...] = ckvn.astype(BF16)
        dkr = jnp.zeros((tm, LANES), F32)
        for h in range(MLA_HEADS):
            b0 = h * HEAD_PAD
            dqp_ref[:, b0:b0 + QK_NOPE] = dq_ref[:, b0:b0 + QK_NOPE].astype(BF16)
            dqp_ref[:, b0 + QK_NOPE:b0 + HEAD_PAD] = _rope_t(dq_ref[:, b0 + QK_NOPE:b0 + HEAD_PAD], tabs).astype(BF16)
            dkv_ref[:, b0:b0 + QK_NOPE] = dk_ref[:, b0:b0 + QK_NOPE].astype(BF16)
            dkv_ref[:, b0 + QK_NOPE:b0 + HEAD_PAD] = dv_ref[:, h * V_HEAD:(h + 1) * V_HEAD].astype(BF16)
            dkr = dkr + dk_ref[:, b0 + QK_NOPE:b0 + HEAD_PAD]
        dcqn = _dot(dqp_ref[...], wq_ref[...])
        dckvn = _dot_nt(dkv_ref[...], wkv_ref[...])
        dqg_ref[...] += jnp.sum(dcqn * cqh, axis=0, keepdims=True)
        dkvg_ref[...] += jnp.sum(dckvn * ckvh, axis=0, keepdims=True)
        dxh = dcqn * qg_ref[...]
        dzs_ref[:, 0:Q_LORA] = (rq * (dxh - cqh * jnp.mean(dxh * cqh, axis=-1, keepdims=True))).astype(BF16)
        dxh = dckvn * kvg_ref[...]
        dzs_ref[:, Q_LORA:Q_LORA + KV_LORA] = (
            rkv * (dxh - ckvh * jnp.mean(dxh * ckvh, axis=-1, keepdims=True))).astype(BF16)
        dzs_ref[:, Q_LORA + KV_LORA:ZS_W] = _rope_t(dkr, tabs).astype(BF16)

    full = lambda a: pl.BlockSpec(a.shape, lambda i: (0,) * a.ndim)
    rowb = lambda w: pl.BlockSpec((tm, w), lambda i: (i, 0))
    return pl.pallas_call(
        body, grid=(T // tm,),
        in_specs=[rowb(ZS_W), rowb(1), full(invf), full(qg), full(kvg), full(wuq_p), full(wukv),
                  rowb(HW), rowb(HW), rowb(D_MODEL)],
        out_specs=[rowb(ZS_W), rowb(Q_LORA), rowb(HW), rowb(KV_LORA), rowb(HW), full(qg), full(kvg)],
        out_shape=[jax.ShapeDtypeStruct((T, ZS_W), BF16), jax.ShapeDtypeStruct((T, Q_LORA), BF16),
                   jax.ShapeDtypeStruct((T, HW), BF16), jax.ShapeDtypeStruct((T, KV_LORA), BF16),
                   jax.ShapeDtypeStruct((T, HW), BF16), jax.ShapeDtypeStruct(qg.shape, F32),
                   jax.ShapeDtypeStruct(kvg.shape, F32)],
        name="mla_prep_bwd", compiler_params=_params("arbitrary"))(
            zs, pos, invf, qg, kvg, wuq_p, wukv, dq, dk, dv)


def _causal(tq, kmax, q0):
    r = lax.broadcasted_iota(jnp.int32, (tq, kmax), 0) + q0
    c = lax.broadcasted_iota(jnp.int32, (tq, kmax), 1)
    return c <= r


def _attn_fwd(q, k, v, batch, seq):
    tq = _tile(seq, ATTN_TILE)
    nq = seq // tq

    def body(q_ref, k_ref, v_ref, o_ref, lse_ref):
        diag = _causal(tq, tq, 0)
        for qi in range(nq):
            rows = slice(qi * tq, (qi + 1) * tq)
            qr = q_ref[rows, :]
            s_d = jnp.where(diag, _dot_nt(qr, k_ref[rows, :]) * ATTN_SCALE, NEG_BIG)
            m = jnp.max(s_d, axis=-1, keepdims=True)
            if qi > 0:
                before = slice(0, qi * tq)
                s_b = _dot_nt(qr, k_ref[before, :]) * ATTN_SCALE
                m = jnp.maximum(m, jnp.max(s_b, axis=-1, keepdims=True))
                p_b = jnp.exp(s_b - m)
                l = jnp.sum(p_b, axis=-1, keepdims=True)
                acc = _dot(p_b.astype(BF16), v_ref[before, :])
            p_d = jnp.exp(s_d - m)
            l_d = jnp.sum(p_d, axis=-1, keepdims=True)
            acc_d = _dot(p_d.astype(BF16), v_ref[rows, :])
            l, acc = (l + l_d, acc + acc_d) if qi > 0 else (l_d, acc_d)
            o_ref[rows, :] = acc / l
            lse_ref[rows, :] = jnp.broadcast_to(m + jnp.log(l), (tq, V_HEAD))

    return pl.pallas_call(
        body, grid=(batch, MLA_HEADS),
        in_specs=[pl.BlockSpec((seq, HEAD_PAD), lambda b, h: (b, h)),
                  pl.BlockSpec((seq, HEAD_PAD), lambda b, h: (b, h)),
                  pl.BlockSpec((seq, V_HEAD), lambda b, h: (b, h))],
        out_specs=[pl.BlockSpec((seq, V_HEAD), lambda b, h: (b, h)),
                   pl.BlockSpec((seq, V_HEAD), lambda b, h: (b, h))],
        out_shape=[jax.ShapeDtypeStruct((batch * seq, D_MODEL), F32),
                   jax.ShapeDtypeStruct((batch * seq, D_MODEL), F32)],
        name="attn_fwd", compiler_params=_params("parallel", "parallel"))(q, k, v)


def _attn_bwd(q, k, v, o, do, lse, batch, seq, dep):
    tq = _tile(seq, ATTN_TILE)
    nq = seq // tq

    def body(q_ref, k_ref, v_ref, o_ref, do_ref, lse_ref, dep_ref, dq_ref, dk_ref, dv_ref):
        dk_ref[...] = jnp.zeros_like(dk_ref)
        dv_ref[...] = jnp.zeros_like(dv_ref)
        for qi in range(nq):
            rows = slice(qi * tq, (qi + 1) * tq)
            kmax = (qi + 1) * tq
            qr = q_ref[rows, :]
            dor = do_ref[rows, :]
            kk = k_ref[0:kmax, :]
            s = _dot_nt(qr, kk) * ATTN_SCALE
            p = jnp.where(_causal(tq, kmax, qi * tq), jnp.exp(s - lse_ref[rows, 0:1]), 0.0)
            dp = _dot_nt(dor, v_ref[0:kmax, :])
            delta = jnp.sum(dor.astype(F32) * o_ref[rows, :], axis=-1, keepdims=True)
            ds = (p * (dp - delta) * ATTN_SCALE).astype(BF16)
            dq_ref[rows, :] = _dot(ds, kk)
            dk_ref[0:kmax, :] += _dot_tn(ds, qr)
            dv_ref[0:kmax, :] += _dot_tn(p.astype(BF16), dor)

    qspec = pl.BlockSpec((seq, HEAD_PAD), lambda b, h: (b, h))
    vspec = pl.BlockSpec((seq, V_HEAD), lambda b, h: (b, h))
    T = batch * seq
    return pl.pallas_call(
        body, grid=(batch, MLA_HEADS),
        in_specs=[qspec, qspec, vspec, vspec, vspec, vspec, ANY],
        out_specs=[qspec, qspec, vspec],
        out_shape=[jax.ShapeDtypeStruct((T, MLA_HEADS * HEAD_PAD), F32),
                   jax.ShapeDtypeStruct((T, MLA_HEADS * HEAD_PAD), F32),
                   jax.ShapeDtypeStruct((T, D_MODEL), F32)],
        name="attn_bwd", compiler_params=_params("parallel", "parallel"))(q, k, v, o, do, lse, dep)


def _merge_out(x, yag, zm, o, w_out, ffn_g):
    T = x.shape[0]
    tm = _tile(T, 512)

    def body(x_ref, ya_ref, gb_ref, o_ref, w_ref, g_ref, mg_ref, x1_ref, h2_ref):
        mg = (ya_ref[...] + _sigmoid(gb_ref[...]) * o_ref[...]).astype(BF16)
        mg_ref[...] = mg
        x1 = x_ref[...] + _dot(mg, w_ref[...])
        x1_ref[...] = x1
        r = lax.rsqrt(jnp.mean(x1 * x1, axis=-1, keepdims=True) + EPS)
        h2_ref[...] = (x1 * r * g_ref[...]).astype(BF16)

    row = pl.BlockSpec((tm, D_MODEL), lambda i: (i, 0))
    return pl.pallas_call(
        body, grid=(T // tm,),
        in_specs=[row, row, pl.BlockSpec((tm, D_MODEL), lambda i: (i, 3)), row,
                  pl.BlockSpec((D_MODEL, D_MODEL), lambda i: (0, 0)), pl.BlockSpec((1, D_MODEL), lambda i: (0, 0))],
        out_specs=[row, row, row],
        out_shape=[jax.ShapeDtypeStruct((T, D_MODEL), BF16), jax.ShapeDtypeStruct((T, D_MODEL), F32),
                   jax.ShapeDtypeStruct((T, D_MODEL), BF16)],
        name="merge_out", compiler_params=_params("parallel"))(x, yag, zm, o, w_out, ffn_g)


FF_TILE = 256
FF_BLOCKS = D_FF // FF_TILE
FFB_TILE = 256


def _shift_down(x, k):
    row = lax.broadcasted_iota(jnp.int32, x.shape, 0)
    return jnp.where(row >= k, pltpu.roll(x, k, 0), 0.0)


def _shift_up(x, k):
    n = x.shape[0]
    row = lax.broadcasted_iota(jnp.int32, x.shape, 0)
    return jnp.where(row < n - k, pltpu.roll(x, n - k, 0), 0.0)


def _conv(x, w_ref, b_ref):
    return b_ref[...] + w_ref[2:3, :] * x + w_ref[1:2, :] * _shift_down(x, 1) + w_ref[0:1, :] * _shift_down(x, 2)


def _up_act(h2, wt_up, cw, cb, batch, seq):
    def body(h_ref, wug_ref, wuv_ref, wg_ref, wv_ref, bg_ref, bv_ref, ug_ref, uv_ref, g_ref, v_ref, a_ref):
        h = h_ref[...]
        ug = _dot_nt(h, wug_ref[...])
        uv = _dot_nt(h, wuv_ref[...])
        ug_ref[...] = ug
        uv_ref[...] = uv
        gate = _conv(ug, wg_ref, bg_ref)
        val = _conv(uv, wv_ref, bv_ref)
        g_ref[...] = gate
        v_ref[...] = val
        a_ref[...] = (gate * _sigmoid(gate) * val).astype(BF16)

    blk = pl.BlockSpec((seq, FF_TILE), lambda b, j: (b, j))
    wup = lambda off: pl.BlockSpec((FF_TILE, D_MODEL), lambda b, j: (j + off, 0))
    wsp = lambda off: pl.BlockSpec((3, FF_TILE), lambda b, j: (0, j + off))
    bsp = lambda off: pl.BlockSpec((1, FF_TILE), lambda b, j: (0, j + off))
    T = batch * seq
    f32 = jax.ShapeDtypeStruct((T, D_FF), F32)
    return pl.pallas_call(
        body, grid=(batch, FF_BLOCKS),
        in_specs=[pl.BlockSpec((seq, D_MODEL), lambda b, j: (b, 0)), wup(0), wup(FF_BLOCKS),
                  wsp(0), wsp(FF_BLOCKS), bsp(0), bsp(FF_BLOCKS)],
        out_specs=[blk] * 5,
        out_shape=[f32, f32, f32, f32, jax.ShapeDtypeStruct((T, D_FF), BF16)],
        name="up_act", compiler_params=_params("parallel", "arbitrary"))(h2, wt_up, wt_up, cw, cw, cb, cb)


def _ffn_act_bwd(upg, upv, gate, val, cw, dx2b, w_down, batch, seq):
    def half(du, x, w_ref, dx_ref, dw_ref):
        j = pl.program_id(1)
        up1, up2 = _shift_up(du, 1), _shift_up(du, 2)
        dx_ref[...] = (w_ref[2:3, :] * du + w_ref[1:2, :] * up1 + w_ref[0:1, :] * up2).astype(BF16)
        dw_ref[j, 2:3, :] += jnp.sum(du * x, axis=0, keepdims=True)
        dw_ref[j, 1:2, :] += jnp.sum(up1 * x, axis=0, keepdims=True)
        dw_ref[j, 0:1, :] += jnp.sum(up2 * x, axis=0, keepdims=True)
        dw_ref[j, 3:4, :] += jnp.sum(du, axis=0, keepdims=True)

    def body(ug_ref, uv_ref, g_ref, v_ref, wg_ref, wv_ref, dx_ref, wd_ref, dg_ref, dv_ref, dwg_ref, dwv_ref):
        @pl.when((pl.program_id(0) == 0) & (pl.program_id(1) == 0))
        def _():
            dwg_ref[...] = jnp.zeros_like(dwg_ref)
            dwv_ref[...] = jnp.zeros_like(dwv_ref)

        gate, val = g_ref[...], v_ref[...]
        sg = _sigmoid(gate)
        dav = _dot_nt(dx_ref[...], wd_ref[...])
        half(dav * val * sg * (1.0 + gate * (1.0 - sg)), ug_ref[...], wg_ref, dg_ref, dwg_ref)
        half(dav * gate * sg, uv_ref[...], wv_ref, dv_ref, dwv_ref)

    nb = D_FF // FFB_TILE
    blk = pl.BlockSpec((seq, FFB_TILE), lambda b, j: (b, j))
    wsp = lambda off: pl.BlockSpec((3, FFB_TILE), lambda b, j: (0, j + off))
    acc = pl.BlockSpec((nb, 4, FFB_TILE), lambda b, j: (0, 0, 0))
    T = batch * seq
    dupg, dupv, dwg, dwv = pl.pallas_call(
        body, grid=(batch, nb),
        in_specs=[blk, blk, blk, blk, wsp(0), wsp(nb),
                  pl.BlockSpec((seq, D_MODEL), lambda b, j: (b, 0)),
                  pl.BlockSpec((FFB_TILE, D_MODEL), lambda b, j: (j, 0))],
        out_specs=[blk, blk, acc, acc],
        out_shape=[jax.ShapeDtypeStruct((T, D_FF), BF16), jax.ShapeDtypeStruct((T, D_FF), BF16),
                   jax.ShapeDtypeStruct((nb, 4, FFB_TILE), F32), jax.ShapeDtypeStruct((nb, 4, FFB_TILE), F32)],
        name="ffn_act_bwd", compiler_params=_params("arbitrary", "arbitrary"))(
            upg, upv, gate, val, cw, cw, dx2b, w_down)
    dwg, dwv = (jnp.transpose(a, (1, 0, 2)).reshape(4, D_FF) for a in (dwg, dwv))
    return dupg, dupv, dwg[:3], dwv[:3], dwg[3:], dwv[3:]


def _down_loss(a, w_down, x1, target, gfin):
    T = x1.shape[0]
    tm = _tile(T, 512)

    def body(a_ref, w_ref, x1_ref, t_ref, g_ref, dx_ref, dxb_ref, loss_ref, dg_ref):
        @pl.when(pl.program_id(0) == 0)
        def _():
            loss_ref[...] = jnp.zeros_like(loss_ref)
            dg_ref[...] = jnp.zeros_like(dg_ref)

        x2 = x1_ref[...] + _dot(a_ref[...], w_ref[...])
        r = lax.rsqrt(jnp.mean(x2 * x2, axis=-1, keepdims=True) + EPS)
        xh = x2 * r
        g = g_ref[...]
        diff = xh * g - t_ref[...]
        loss_ref[...] += 0.5 * jnp.sum(jnp.mean(diff * diff, axis=-1, keepdims=True))
        dy = diff * (1.0 / D_MODEL)
        dg_ref[...] += jnp.sum(dy * xh, axis=0, keepdims=True)
        dxh = dy * g
        dx = r * (dxh - xh * jnp.mean(dxh * xh, axis=-1, keepdims=True))
        dx_ref[...] = dx
        dxb_ref[...] = dx.astype(BF16)

    row = pl.BlockSpec((tm, D_MODEL), lambda i: (i, 0))
    vec = pl.BlockSpec((1, D_MODEL), lambda i: (0, 0))
    return pl.pallas_call(
        body, grid=(T // tm,),
        in_specs=[pl.BlockSpec((tm, D_FF), lambda i: (i, 0)),
                  pl.BlockSpec((D_FF, D_MODEL), lambda i: (0, 0)), row, row, vec],
        out_specs=[row, row, pl.BlockSpec((8, LANES), lambda i: (0, 0)), vec],
        out_shape=[jax.ShapeDtypeStruct((T, D_MODEL), F32), jax.ShapeDtypeStruct((T, D_MODEL), BF16),
                   jax.ShapeDtypeStruct((8, LANES), F32), jax.ShapeDtypeStruct((1, D_MODEL), F32)],
        name="down_loss", compiler_params=_params("arbitrary"))(a, w_down, x1, target, gfin)


def _local_step(x, positions, target, mix_norm, av_g, av_b, w_s, b_s, q_norm, kv_norm, ffn_norm, conv_b,
                final_norm, comm):
    batch, seq, _ = x.shape
    T = batch * seq
    x = x.reshape(T, D_MODEL)
    target = target.reshape(T, D_MODEL)
    pos = positions.reshape(T, 1)
    half = jnp.arange(0, QK_ROPE, 2, dtype=F32) / QK_ROPE
    inv_freq = 1.0 / (ROPE_THETA ** half)
    invf = jnp.concatenate([inv_freq, inv_freq, jnp.zeros((LANES - QK_ROPE,), F32)]).reshape(1, LANES)
    w_st = jnp.swapaxes(w_s, 1, 2)
    b_col = b_s.reshape(A_GROUPS, CHUNK, 1)

    wt_in = comm.in_weights()
    h, zm, zs = _in_proj(x, mix_norm, wt_in)
    yag = _mixer_a_fwd(zm, av_g, av_b, w_s, b_col)
    wuq_p, wukv, w_out = comm.mla_weights(after=yag)
    q, k, v = _mla_prep_fwd(zs, pos, invf, q_norm, kv_norm, wuq_p, wukv)
    o, lse = _attn_fwd(q, k, v, batch, seq)
    merged, x1, h2 = _merge_out(x, yag, zm, o, w_out, ffn_norm)
    wt_up, conv_w, w_down = comm.ffn_weights(after=merged)
    upg, upv, gate, val, act = _up_act(h2, wt_up, conv_w, conv_b, batch, seq)
    dx2, dx2b, loss_acc, d_final = _down_loss(act, w_down, x1, target, final_norm)

    d_wdown = _mm_tn(act, dx2b, "dw_down")
    dupg, dupv, dcwg, dcwv, dcbg, dcbv = _ffn_act_bwd(upg, upv, gate, val, conv_w, dx2b, w_down, batch, seq)
    d_wt_up = jnp.concatenate([_mm_tn(dupg, h2, "dw_up_gate"), _mm_tn(dupv, h2, "dw_up_val")], axis=0)
    dx1, d_ffn_norm, dmerged = _proj_bwd(
        [dupg, dupv], wt_up, [(0, (0, D_FF), (0, D_FF)), (1, (0, D_FF), (D_FF, 2 * D_FF))],
        x1, ffn_norm, dx2, "up_proj_bwd", w2=w_out)
    d_wout = _mm_tn(merged, dx1, "dw_out")
    token = comm.send_ffn_grads(d_wdown, d_wt_up, jnp.concatenate([dcwg, dcwv], axis=1), d_wout)
    dzm, do, d_avg, d_avb, d_ws, d_bs = _mixer_bwd(zm, o, dmerged, av_g, av_b, w_s, w_st, b_col, token)
    token = comm.send_small_grads(_pack_small(dict(
        a_v_norm_g=d_avg, a_v_norm_b=d_avb, a_spatial_w=d_ws, a_spatial_b=d_bs, ffn_norm=d_ffn_norm,
        conv_b=jnp.concatenate([dcbg, dcbv], axis=1), final_norm=d_final), SMALL_EARLY))
    dq, dk, dv = _attn_bwd(q, k, v, o, do, lse, batch, seq, token)
    dzs, cqn, dqp, ckvn, dkv, d_qn, d_kvn = _mla_prep_bwd(zs, pos, invf, q_norm, kv_norm, wuq_p, wukv, dq, dk, dv)
    d_wt_main = _mm_tn(dzm, h, "dw_in_main")
    d_wt_zs = _mm_tn(dzs, h, "dw_in_small")
    token = comm.send_in_grads(d_wt_main, d_wt_zs)
    d_wuq_p = _mm_tn(dqp, cqn, "dw_uq", dep=token)
    d_wukv = _mm_tn(ckvn, dkv, "dw_ukv", dep=token)
    terms = [(0, (i * D_MODEL, (i + 1) * D_MODEL), rows) for i, rows in enumerate(IN_ROWS_MAIN)]
    terms.append((1, (0, ZS_W), IN_ROWS_ZS))
    dx, d_mix_norm = _proj_bwd([dzm, dzs], wt_in, terms, x, mix_norm, dx1, "in_proj_bwd", dep=token)
    late = _pack_small(dict(q_a_norm=d_qn, kv_a_norm=d_kvn, mix_norm=d_mix_norm), SMALL_LATE,
                       extra=loss_acc[0, 0].reshape(1))
    token = comm.send_late_grads(d_wuq_p, d_wukv, late)
    return dx.reshape(batch, seq, D_MODEL), token


MESH_ID = pl.DeviceIdType.MESH
EFFECT = pltpu.SideEffectType.DATAFLOW_SIDE_EFFECTING


def _mesh_pos():
    return lax.axis_index("x"), lax.axis_index("y"), lax.axis_index("c")


def _peer(pos, d):
    x, y, c = pos
    px = 1 - x if d & 4 else x
    py = 1 - y if d & 2 else y
    pc = 1 - c if d & 1 else c
    return (px, py, pc), 4 * px + 2 * py + pc


def _copy(src_ref, land_ref, send_sems, recv_sems, a, d, pos, exchange, landing_here):
    peer, pid = _peer(pos, d)
    me = 4 * pos[0] + 2 * pos[1] + pos[2]
    if exchange:
        src, dst = src_ref.at[pid], land_ref.at[d]
    else:
        src, dst = src_ref, land_ref.at[pid if landing_here else me]
    return pltpu.make_async_remote_copy(
        src_ref=src, dst_ref=dst, send_sem=send_sems.at[a * (N_DEV - 1) + d - 1],
        recv_sem=recv_sems.at[a * (N_DEV - 1) + d - 1],
        device_id=peer, device_id_type=MESH_ID)


def _start_copies(groups, modes, name, dep=None):
    sizes = [len(g) for g in groups]
    srcs = [s for g in groups for s in g]
    lands = [lax.empty(s.shape if modes[gi] else (N_DEV,) + s.shape, s.dtype)
             for gi, g in enumerate(groups) for s in g]
    n, ng = len(srcs), len(groups)
    n_in = 2 * n + (dep is not None)

    def body(*refs):
        src_refs, land_refs = refs[:n], refs[n:2 * n]
        sems = refs[n_in:n_in + 2 * ng]
        token, local_sems = refs[-2], refs[-1]
        pos = _mesh_pos()
        me = 4 * pos[0] + 2 * pos[1] + pos[2]
        k, local = 0, []
        for gi, size in enumerate(sizes):
            for a in range(size):
                for d in range(1, N_DEV):
                    _copy(src_refs[k], land_refs[k], sems[2 * gi], sems[2 * gi + 1], a, d, pos, modes[gi],
                          landing_here=False).start()
                own = (src_refs[k].at[me], land_refs[k].at[0]) if modes[gi] else (src_refs[k], land_refs[k].at[me])
                local.append(pltpu.make_async_copy(*own, local_sems.at[k]))
                local[-1].start()
                k += 1
        for cp in local:
            cp.wait()
        token[...] = jnp.zeros_like(token)

    sem_shapes = [pltpu.SemaphoreType.DMA((size * (N_DEV - 1),)) for size in sizes for _ in range(2)]
    out = pl.pallas_call(
        body, name=name,
        out_shape=(*sem_shapes, *[pltpu.HBM(a.shape, a.dtype) for a in srcs + lands],
                   jax.ShapeDtypeStruct((8, LANES), F32)),
        in_specs=[HBM] * (2 * n) + [ANY] * (dep is not None),
        out_specs=(*[SEM] * (2 * ng), *[HBM] * (2 * n), pl.BlockSpec(memory_space=pltpu.VMEM)),
        input_output_aliases={i: 2 * ng + i for i in range(2 * n)},
        scratch_shapes=[pltpu.SemaphoreType.DMA((n,))],
        compiler_params=pltpu.CompilerParams(has_side_effects=EFFECT),
    )(*[pltpu.with_memory_space_constraint(a, pltpu.HBM) for a in srcs + lands], *([dep] if dep is not None else []))
    thru = out[2 * ng:2 * ng + 2 * n]
    handles, k = [], 0
    for gi, size in enumerate(sizes):
        handles.append((out[2 * gi], out[2 * gi + 1], thru[k:k + size], thru[n + k:n + k + size]))
        k += size
    return handles, out[-1]


def _wait_copies(handle, exchange, after, name):
    send_sems, recv_sems, srcs, lands = handle
    n = len(srcs)

    def body(*refs):
        src_refs, land_refs = refs[:n], refs[n:2 * n]
        send, recv = refs[2 * n], refs[2 * n + 1]
        pos = _mesh_pos()
        for a in range(n):
            for d in range(1, N_DEV):
                cp = _copy(src_refs[a], land_refs[a], send, recv, a, d, pos, exchange, landing_here=True)
                cp.wait_send()
                cp.wait_recv()

    out = pl.pallas_call(
        body, name=name,
        out_shape=tuple(pltpu.HBM(a.shape, a.dtype) for a in (*srcs, *lands)),
        in_specs=[HBM] * (2 * n) + [SEM, SEM, ANY], out_specs=[HBM] * (2 * n),
        input_output_aliases={i: i for i in range(2 * n)},
        compiler_params=pltpu.CompilerParams(has_side_effects=EFFECT),
    )(*srcs, *lands, send_sems, recv_sems, after)
    return out[n:]


def _gather_now(a, name):
    def body(x_ref, out_ref, send_sems, recv_sems, local_sem):
        x, y, c = _mesh_pos()
        me, sibling = (x, y, c), (x, y, 1 - c)
        chips = [(1 - x, y), (x, 1 - y), (1 - x, 1 - y)]

        def slot(p):
            return out_ref.at[4 * p[0] + 2 * p[1] + p[2]]

        def copy(k, block, to, src=None):
            return pltpu.make_async_remote_copy(
                src_ref=slot(block) if src is None else src, dst_ref=slot(block), send_sem=send_sems.at[k],
                recv_sem=recv_sems.at[k], device_id=to, device_id_type=MESH_ID)

        mine = pltpu.make_async_copy(x_ref, slot(me), local_sem)
        mine.start()
        first = [copy(0, me, sibling, src=x_ref)]
        first += [copy(1 + j, me, (*chip, c), src=x_ref) for j, chip in enumerate(chips)]
        for cp in first:
            cp.start()
        passed = [copy(4 + j, (*chip, c), sibling) for j, chip in enumerate(chips)]
        for j, chip in enumerate(chips):
            copy(1 + j, (*chip, c), me).wait_recv()
            passed[j].start()
        copy(0, sibling, me).wait_recv()
        for j, chip in enumerate(chips):
            copy(4 + j, (*chip, 1 - c), me).wait_recv()
        for cp in first + passed:
            cp.wait_send()
        mine.wait()

    return pl.pallas_call(
        body, in_specs=[ANY], out_specs=ANY,
        out_shape=jax.ShapeDtypeStruct((N_DEV,) + a.shape, a.dtype),
        scratch_shapes=[pltpu.SemaphoreType.DMA((N_DEV - 1,)), pltpu.SemaphoreType.DMA((N_DEV - 1,)),
                        pltpu.SemaphoreType.DMA],
        name=name, compiler_params=pltpu.CompilerParams(has_side_effects=True))(a)


def _adamw(parts, w, m, v, name):
    R, C = w.shape
    tr, tc = R, C
    if N_DEV * R * C * parts.dtype.itemsize > SMALL_BLOCK_BYTES:
        tr = next((t for t in range(min(R, 256) // 16 * 16, 15, -16) if R % t == 0), R)
        if tr == R:
            tc = _tile(C, 256)
    c1 = 1.0 - ADAM_B1 ** ADAM_STEP
    c2 = 1.0 - ADAM_B2 ** ADAM_STEP

    def body(p_ref, w_ref, m_ref, v_ref, g_ref, d_ref, nm_ref, nv_ref):
        g = p_ref[0].astype(F32)
        for k in range(1, N_DEV):
            g = g + p_ref[k].astype(F32)
        nm = ADAM_B1 * m_ref[...] + (1.0 - ADAM_B1) * g
        nv = ADAM_B2 * v_ref[...] + (1.0 - ADAM_B2) * (g * g)
        g_ref[...] = g
        nm_ref[...] = nm
        nv_ref[...] = nv
        d_ref[...] = -ADAM_LR * ((nm / c1) / (jnp.sqrt(nv / c2) + ADAM_EPS) + ADAM_WD * w_ref[...])

    blk = pl.BlockSpec((tr, tc), lambda i, j: (i, j))
    shp = jax.ShapeDtypeStruct((R, C), F32)
    return pl.pallas_call(
        body, grid=(R // tr, C // tc),
        in_specs=[pl.BlockSpec((N_DEV, tr, tc), lambda i, j: (0, i, j)), blk, blk, blk],
        out_specs=[blk, blk, blk, blk], out_shape=[shp, shp, shp, shp],
        name=name, compiler_params=_params("parallel", "parallel"))(parts, w, m, v)


SPLIT_V = 2 * D_MODEL
SPLIT_KR = SPLIT_V + Q_LORA + KV_LORA + QK_ROPE
IN_DIM = SPLIT_KR + 2 * D_MODEL
IN_ROWS_MAIN = ((0, D_MODEL), (D_MODEL, SPLIT_V), (SPLIT_KR, SPLIT_KR + D_MODEL), (SPLIT_KR + D_MODEL, IN_DIM))
IN_ROWS_ZS = (SPLIT_V, SPLIT_V + ZS_W)

SMALL_EARLY = ("a_v_norm_g", "a_v_norm_b", "a_spatial_w", "a_spatial_b", "ffn_norm", "conv_b", "final_norm")
SMALL_LATE = ("q_a_norm", "kv_a_norm", "mix_norm")


def _pack_rows(a):
    flat = a.reshape(-1)
    rows = -(-flat.shape[0] // LANES)
    rows8 = -(-rows // 8) * 8
    return jnp.pad(flat, (0, rows8 * LANES - flat.shape[0])).reshape(rows8, LANES)


def _pack_small(tree, names, extra=None):
    parts = [_pack_rows(tree[n]) for n in names]
    if extra is not None:
        parts.append(_pack_rows(extra))
    return jnp.concatenate(parts, axis=0)


def _unpack_small(buf, names, shapes):
    out, r = {}, 0
    for n in names:
        size = math.prod(shapes[n])
        rows8 = -(-(-(-size // LANES)) // 8) * 8
        out[n] = buf[r:r + rows8].reshape(-1)[:size].reshape(shapes[n])
        r += rows8
    return out, r


def _cols_from_shards(g):
    return jnp.transpose(g, (1, 0, 2)).reshape(g.shape[1], N_DEV * g.shape[2])


def _shards_from_cols(a):
    R, W = a.shape
    return jnp.transpose(a.reshape(R, N_DEV, W // N_DEV), (1, 0, 2))


class _Comm:
    GATHER_GROUPS = (("w_uq", "w_ukv", "w_out"), ("w_up", "conv_w", "w_down"))
    FFN_GRADS = ("w_down", "w_up", "conv_w", "w_out")
    LATE_GRADS = ("w_uq", "w_ukv")
    TRANSPOSED = ("w_in", "w_up", "w_uq")

    def __init__(self, shards, me):
        self.me = me
        local = {n: a.astype(F32 if n == "conv_w" else BF16) for n, a in shards.items()}
        self.g_in = _gather_now(local["w_in"], "gather_w_in")
        groups = [[local[n] for n in g] for g in self.GATHER_GROUPS]
        (self.h_mla, self.h_ffn), _ = _start_copies(groups, [False] * 2, "gather_start", dep=self.g_in)

    def in_weights(self):
        return self.g_in.reshape(IN_DIM, D_MODEL)

    def mla_weights(self, after):
        g_uq, g_ukv, g_out = _wait_copies(self.h_mla, False, after, "gather_wait_mla")
        wuq_p = jnp.pad(g_uq, ((0, 0), (0, HEAD_PAD - QK_HEAD), (0, 0))).reshape(MLA_HEADS * HEAD_PAD, Q_LORA)
        return wuq_p, _cols_from_shards(g_ukv), g_out.reshape(D_MODEL, D_MODEL)

    def ffn_weights(self, after):
        g_up, g_cw, g_down = _wait_copies(self.h_ffn, False, after, "gather_wait_ffn")
        return g_up.reshape(2 * D_FF, D_MODEL), _cols_from_shards(g_cw), g_down.reshape(D_FF, D_MODEL)

    def send_ffn_grads(self, d_wdown, d_wt_up, d_convw, d_wout):
        group = [d_wdown.reshape(N_DEV, D_FF // N_DEV, D_MODEL), d_wt_up.reshape(N_DEV, 2 * D_FF // N_DEV, D_MODEL),
                 _shards_from_cols(d_convw), d_wout.reshape(N_DEV, D_MODEL // N_DEV, D_MODEL)]
        (self.h_ffn_grads,), token = _start_copies([group], [True], "ffn_grads_start")
        return token

    def send_small_grads(self, packed):
        (self.h_small_early,), token = _start_copies([[packed]], [False], "small_grads_start")
        return token

    def send_in_grads(self, d_wt_main, d_wt_zs):
        d_in = jnp.concatenate([d_wt_main[:SPLIT_V], d_wt_zs[:SPLIT_KR - SPLIT_V], d_wt_main[SPLIT_V:]], axis=0)
        blocks = d_in.reshape(N_DEV, IN_DIM // N_DEV, D_MODEL)
        (self.h_in_grads,), token = _start_copies([[blocks]], [True], "in_grads_start")
        return token

    def send_late_grads(self, d_wuq_p, d_wukv, packed):
        d_uq = d_wuq_p.reshape(MLA_HEADS, HEAD_PAD, Q_LORA)[:, :QK_HEAD, :]
        (self.h_late_grads, self.h_late_small), token = _start_copies(
            [[d_uq, _shards_from_cols(d_wukv)], [packed]], [True, False], "late_grads_start")
        return token


def kernel(x, positions, mix_norm, w_in, a_v_norm_g, a_v_norm_b, a_spatial_w, a_spatial_b, q_a_norm, w_uq, kv_a_norm, w_ukv, w_out, ffn_norm, w_up, conv_w, conv_b, w_down, final_norm, loss_target, m_mix_norm, m_w_in, m_a_v_norm_g, m_a_v_norm_b, m_a_spatial_w, m_a_spatial_b, m_q_a_norm, m_w_uq, m_kv_a_norm, m_w_ukv, m_w_out, m_ffn_norm, m_w_up, m_conv_w, m_conv_b, m_w_down, m_final_norm, v_mix_norm, v_w_in, v_a_v_norm_g, v_a_v_norm_b, v_a_spatial_w, v_a_spatial_b, v_q_a_norm, v_w_uq, v_kv_a_norm, v_w_ukv, v_w_out, v_ffn_norm, v_w_up, v_conv_w, v_conv_b, v_w_down, v_final_norm):
    names = ("mix_norm", "w_in", "a_v_norm_g", "a_v_norm_b", "a_spatial_w", "a_spatial_b", "q_a_norm", "w_uq",
             "kv_a_norm", "w_ukv", "w_out", "ffn_norm", "w_up", "conv_w", "conv_b", "w_down", "final_norm")
    w = dict(zip(names, (mix_norm, w_in, a_v_norm_g, a_v_norm_b, a_spatial_w, a_spatial_b, q_a_norm, w_uq,
                         kv_a_norm, w_ukv, w_out, ffn_norm, w_up, conv_w, conv_b, w_down, final_norm)))
    m = dict(zip(names, (m_mix_norm, m_w_in, m_a_v_norm_g, m_a_v_norm_b, m_a_spatial_w, m_a_spatial_b,
                         m_q_a_norm, m_w_uq, m_kv_a_norm, m_w_ukv, m_w_out, m_ffn_norm, m_w_up, m_conv_w,
                         m_conv_b, m_w_down, m_final_norm)))
    v = dict(zip(names, (v_mix_norm, v_w_in, v_a_v_norm_g, v_a_v_norm_b, v_a_spatial_w, v_a_spatial_b,
                         v_q_a_norm, v_w_uq, v_kv_a_norm, v_w_ukv, v_w_out, v_ffn_norm, v_w_up, v_conv_w,
                         v_conv_b, v_w_down, v_final_norm)))
    shapes = {n: w[n].shape for n in names}
    me = 4 * lax.axis_index("x") + 2 * lax.axis_index("y") + lax.axis_index("c")
    def view(tree, n):
        a = tree[n].reshape(tree[n].shape[-2:])
        return a.T if n in _Comm.TRANSPOSED else a

    comm = _Comm({n: view(w, n) for n in ("w_in",) + _Comm.GATHER_GROUPS[0] + _Comm.GATHER_GROUPS[1]}, me)

    grad_x, token = _local_step(
        x, positions, loss_target, w["mix_norm"], w["a_v_norm_g"], w["a_v_norm_b"], w["a_spatial_w"][0],
        w["a_spatial_b"][0], w["q_a_norm"], w["kv_a_norm"], w["ffn_norm"], w["conv_b"],
        w["final_norm"].reshape(1, D_MODEL), comm)

    out_g, out_d, out_m, out_v = {}, {}, {}, {}

    def update(n, parts):
        res = _adamw(parts, view(w, n), view(m, n), view(v, n), "adamw_" + n)
        out_g[n], out_d[n], out_m[n], out_v[n] = (
            (t.T if n in _Comm.TRANSPOSED else t).reshape(shapes[n]) for t in res)

    for n, parts in zip(_Comm.FFN_GRADS, _wait_copies(comm.h_ffn_grads, True, token, "ffn_grads_wait")):
        update(n, parts)
    update("w_in", _wait_copies(comm.h_in_grads, True, out_d["w_up"], "in_grads_wait")[0])
    for n, parts in zip(_Comm.LATE_GRADS, _wait_copies(comm.h_late_grads, True, out_d["w_in"], "late_grads_wait")):
        update(n, parts)

    (early_parts,) = _wait_copies(comm.h_small_early, False, out_d["w_uq"], "small_grads_wait")
    res = _adamw(early_parts, _pack_small(w, SMALL_EARLY), _pack_small(m, SMALL_EARLY), _pack_small(v, SMALL_EARLY),
                 "adamw_small")
    unpacked = [_unpack_small(t, SMALL_EARLY, shapes)[0] for t in res]
    for n in SMALL_EARLY:
        out_g[n], out_d[n], out_m[n], out_v[n] = (u[n] for u in unpacked)

    (late_parts,) = _wait_copies(comm.h_late_small, False, res[1], "late_small_wait")
    zero = jnp.zeros((1,), F32)
    res = _adamw(late_parts, _pack_small(w, SMALL_LATE, extra=zero), _pack_small(m, SMALL_LATE, extra=zero),
                 _pack_small(v, SMALL_LATE, extra=zero), "adamw_late")
    unpacked = [_unpack_small(t, SMALL_LATE, shapes) for t in res]
    for n in SMALL_LATE:
        out_g[n], out_d[n], out_m[n], out_v[n] = (u[0][n] for u in unpacked)
    loss = res[0][unpacked[0][1], 0]

    return (loss, grad_x, *[out_g[n] for n in names], *[out_d[n] for n in names],
            *[out_m[n] for n in names], *[out_v[n] for n in names])
```

```python
import functools
import math

import jax
import jax.numpy as jnp
from jax import lax
from jax.experimental import pallas as pl
from jax.experimental.pallas import tpu as pltpu

F32 = jnp.float32
BF16 = jnp.bfloat16

N_DEV = 8
D_MODEL = 1024
EPS = 1e-6
A_GROUPS = 8
CHUNK = 128
MLA_HEADS = 8
QK_NOPE = 128
QK_ROPE = 64
QK_HEAD = QK_NOPE + QK_ROPE
HEAD_PAD = 256
V_HEAD = 128
Q_LORA = 256
KV_LORA = 128
ROPE_THETA = 10000.0
D_FF = 2816
ZS_W = 512
ATTN_SCALE = QK_HEAD ** -0.5
ATTN_TILE = 512
NEG_BIG = -1e30

ADAM_LR = 0.001
ADAM_B1 = 0.9
ADAM_B2 = 0.999
ADAM_EPS = 1e-08
ADAM_WD = 0.01
ADAM_STEP = 10

VMEM_LIMIT = 56 * 1024 * 1024
SMALL_BLOCK_BYTES = 5 * 1024 * 1024
LANES = 128

GELU_K = math.sqrt(2.0 / math.pi)
GELU_C = 0.044715

ANY = pl.BlockSpec(memory_space=pl.ANY)
HBM = pl.BlockSpec(memory_space=pltpu.HBM)
SEM = pl.BlockSpec(memory_space=pltpu.SEMAPHORE)


def _tile(n, pref):
    for t in (pref, 512, 256, 128, 64, 32, 16, 8):
        if t <= pref and n % t == 0:
            return t
    return n


def _wide_tile(n, cap=1408):
    return next((t for t in range(min(n, cap) // LANES * LANES, 0, -LANES) if n % t == 0), n)


def _params(*sem):
    return pltpu.CompilerParams(dimension_semantics=sem, vmem_limit_bytes=VMEM_LIMIT)


def _dot(a, b):
    return jnp.dot(a, b, preferred_element_type=F32)


def _dot_nt(a, b):
    return lax.dot_general(a, b, (((1,), (1,)), ((), ())), preferred_element_type=F32)


def _dot_tn(a, b):
    return lax.dot_general(a, b, (((0,), (0,)), ((), ())), preferred_element_type=F32)


def _sigmoid(x):
    return 1.0 / (1.0 + jnp.exp(-x))


def _gelu(x):
    t = jnp.tanh(GELU_K * (x + GELU_C * x * x * x))
    return 0.5 * x * (1.0 + t)


def _gelu_grad(x):
    t = jnp.tanh(GELU_K * (x + GELU_C * x * x * x))
    return 0.5 * (1.0 + t) + 0.5 * x * (1.0 - t * t) * GELU_K * (1.0 + 3.0 * GELU_C * x * x)


def _in_proj(x, g, wt):
    T, Dm = x.shape
    tm = _tile(T, 256)

    def body(x_ref, g_ref, wt_ref, h_ref, zm_ref, zs_ref):
        xf = x_ref[...]
        r = lax.rsqrt(jnp.mean(xf * xf, axis=-1, keepdims=True) + EPS)
        h = (xf * r * g_ref[...]).astype(BF16)
        h_ref[...] = h
        for i, (r0, r1) in enumerate(IN_ROWS_MAIN):
            zm_ref[:, i * D_MODEL:(i + 1) * D_MODEL] = _dot_nt(h, wt_ref[r0:r1, :])
        zs_ref[...] = _dot_nt(h, wt_ref[IN_ROWS_ZS[0]:IN_ROWS_ZS[1], :])

    row = lambda n: pl.BlockSpec((tm, n), lambda i: (i, 0))
    return pl.pallas_call(
        body, grid=(T // tm,),
        in_specs=[row(Dm), pl.BlockSpec((1, Dm), lambda i: (0, 0)), pl.BlockSpec(wt.shape, lambda i: (0, 0))],
        out_specs=[row(Dm), row(4 * D_MODEL), row(ZS_W)],
        out_shape=[jax.ShapeDtypeStruct((T, Dm), BF16), jax.ShapeDtypeStruct((T, 4 * D_MODEL), F32),
                   jax.ShapeDtypeStruct((T, ZS_W), F32)],
        name="in_proj", compiler_params=_params("parallel"))(x, g, wt)


def _proj_bwd(acts, wt, terms, x, g, dres, name, w2=None, dep=None):
    T, Dm = x.shape
    tm = _tile(T, 256)
    n_a = len(acts)

    def body(*refs):
        ins, outs = refs[:n_a + 4 + (w2 is not None) + (dep is not None)], refs[-2 - (w2 is not None):]
        wt_ref, x_ref, g_ref, dres_ref = ins[n_a:n_a + 4]
        dx_ref, dg_ref = outs[0], outs[1]

        @pl.when(pl.program_id(0) == 0)
        def _():
            dg_ref[...] = jnp.zeros_like(dg_ref)

        dy = None
        for i, (c0, c1), (r0, r1) in terms:
            t = _dot(ins[i][:, c0:c1], wt_ref[r0:r1, :])
            dy = t if dy is None else dy + t
        xf = x_ref[...]
        r = lax.rsqrt(jnp.mean(xf * xf, axis=-1, keepdims=True) + EPS)
        xh = xf * r
        dg_ref[...] += jnp.sum(dy * xh, axis=0, keepdims=True)
        dxh = dy * g_ref[...]
        dx = dres_ref[...] + r * (dxh - xh * jnp.mean(dxh * xh, axis=-1, keepdims=True))
        dx_ref[...] = dx
        if w2 is not None:
            outs[2][...] = _dot_nt(dx.astype(BF16), ins[n_a + 4][...])

    row = pl.BlockSpec((tm, Dm), lambda i: (i, 0))
    vec = pl.BlockSpec((1, Dm), lambda i: (0, 0))
    in_specs = [pl.BlockSpec((tm, a.shape[1]), lambda i: (i, 0)) for a in acts]
    in_specs += [pl.BlockSpec(wt.shape, lambda i: (0, 0)), row, vec, row]
    args = [*acts, wt, x, g, dres]
    out_specs = [row, vec]
    out_shape = [jax.ShapeDtypeStruct((T, Dm), F32), jax.ShapeDtypeStruct((1, Dm), F32)]
    if w2 is not None:
        in_specs.append(pl.BlockSpec(w2.shape, lambda i: (0, 0)))
        args.append(w2)
        out_specs.append(pl.BlockSpec((tm, w2.shape[0]), lambda i: (i, 0)))
        out_shape.append(jax.ShapeDtypeStruct((T, w2.shape[0]), F32))
    if dep is not None:
        in_specs.append(ANY)
        args.append(dep)
    return pl.pallas_call(
        body, grid=(T // tm,), in_specs=in_specs, out_specs=out_specs, out_shape=out_shape,
        name=name, compiler_params=_params("arbitrary"))(*args)


def _mm_tn(a, b, name, dep=None):
    T, M = a.shape
    N = b.shape[1]
    tm, tn, tt = _wide_tile(M), _wide_tile(N), _tile(T, 2048)
    n_t = T // tt

    def body(a_ref, b_ref, *refs):
        o_ref, acc_ref = refs[-2:]
        t = pl.program_id(2)

        @pl.when(t == 0)
        def _():
            acc_ref[...] = jnp.zeros_like(acc_ref)

        acc_ref[...] += _dot_tn(a_ref[...].astype(BF16), b_ref[...].astype(BF16))

        @pl.when(t == n_t - 1)
        def _():
            o_ref[...] = acc_ref[...].astype(BF16)

    return pl.pallas_call(
        body, grid=(M // tm, N // tn, n_t),
        in_specs=[pl.BlockSpec((tt, tm), lambda i, j, t: (t, i)),
                  pl.BlockSpec((tt, tn), lambda i, j, t: (t, j))] + [ANY] * (dep is not None),
        out_specs=pl.BlockSpec((tm, tn), lambda i, j, t: (i, j)),
        out_shape=jax.ShapeDtypeStruct((M, N), BF16),
        scratch_shapes=[pltpu.VMEM((tm, tn), F32)],
        name=name, compiler_params=_params("parallel", "parallel", "arbitrary"))(
            a, b, *([dep] if dep is not None else []))


def _layer_norm_fwd(gv, g, b):
    mu = jnp.mean(gv, axis=-1, keepdims=True)
    xc = gv - mu
    rs = lax.rsqrt(jnp.mean(xc * xc, axis=-1, keepdims=True) + EPS)
    xh = xc * rs
    return xh, rs, xh * g + b


def _tri_mask(transposed=False):
    r = lax.broadcasted_iota(jnp.int32, (CHUNK, CHUNK), 0)
    c = lax.broadcasted_iota(jnp.int32, (CHUNK, CHUNK), 1)
    return r <= c if transposed else c <= r


def _mixer_a_fwd(zm, av_g, av_b, w_s, b_col):
    T = zm.shape[0]
    tm = _tile(T, 256)
    n_chunk = tm // CHUNK

    def body(u_ref, v_ref, ga_ref, g_ref, b_ref, w_ref, bc_ref, y_ref, vn_s, mx_s):
        gu = _gelu(u_ref[...])
        _, _, vn = _layer_norm_fwd(_gelu(v_ref[...]), g_ref[...], b_ref[...])
        vn_s[...] = vn.astype(BF16)
        tri = _tri_mask()
        for gi in range(A_GROUPS):
            wm = jnp.where(tri, w_ref[gi], 0.0).astype(BF16)
            cols = slice(gi * CHUNK, (gi + 1) * CHUNK)
            for n in range(n_chunk):
                rows = slice(n * CHUNK, (n + 1) * CHUNK)
                mx_s[rows, cols] = _dot(wm, vn_s[rows, cols]) + bc_ref[gi]
        y_ref[...] = _sigmoid(ga_ref[...]) * gu * mx_s[...]

    col = lambda c: pl.BlockSpec((tm, D_MODEL), lambda i: (i, c))
    vec = pl.BlockSpec((1, D_MODEL), lambda i: (0, 0))
    return pl.pallas_call(
        body, grid=(T // tm,),
        in_specs=[col(0), col(1), col(2), vec, vec,
                  pl.BlockSpec((A_GROUPS, CHUNK, CHUNK), lambda i: (0, 0, 0)),
                  pl.BlockSpec((A_GROUPS, CHUNK, 1), lambda i: (0, 0, 0))],
        out_specs=pl.BlockSpec((tm, D_MODEL), lambda i: (i, 0)),
        out_shape=jax.ShapeDtypeStruct((T, D_MODEL), F32),
        scratch_shapes=[pltpu.VMEM((tm, D_MODEL), BF16), pltpu.VMEM((tm, D_MODEL), F32)],
        name="mixer_a_fwd", compiler_params=_params("parallel"))(zm, zm, zm, av_g, av_b, w_s, b_col)


def _mixer_bwd(zm, o, dm, av_g, av_b, w_s, w_st, b_col, dep):
    T = zm.shape[0]
    tm = _tile(T, 256)
    n_chunk = tm // CHUNK

    def body(u_ref, v_ref, ga_ref, gb_ref, o_ref, dm_ref, g_ref, b_ref, w_ref, wt_ref, bc_ref, dep_ref,
             dz_ref, do_ref, dg_ref, db_ref, dw_ref, dbs_ref, vn_s, mx_s, dmx_s, dvn_s):
        @pl.when(pl.program_id(0) == 0)
        def _():
            dg_ref[...] = jnp.zeros_like(dg_ref)
            db_ref[...] = jnp.zeros_like(db_ref)
            dw_ref[...] = jnp.zeros_like(dw_ref)
            dbs_ref[...] = jnp.zeros_like(dbs_ref)

        dm_v = dm_ref[...]
        gb = gb_ref[...]
        sb = _sigmoid(gb)
        o_v = o_ref[...]
        do_ref[...] = (dm_v * sb).astype(BF16)
        dz_ref[:, 3 * D_MODEL:4 * D_MODEL] = (dm_v * o_v * sb * (1.0 - sb)).astype(BF16)
        u = u_ref[...]
        v = v_ref[...]
        gu = _gelu(u)
        xh, rs, vn = _layer_norm_fwd(_gelu(v), g_ref[...], b_ref[...])
        vn_s[...] = vn.astype(BF16)
        tri = _tri_mask()
        for gi in range(A_GROUPS):
            wm = jnp.where(tri, w_ref[gi], 0.0).astype(BF16)
            cols = slice(gi * CHUNK, (gi + 1) * CHUNK)
            for n in range(n_chunk):
                rows = slice(n * CHUNK, (n + 1) * CHUNK)
                mx_s[rows, cols] = _dot(wm, vn_s[rows, cols]) + bc_ref[gi]
        mixed = mx_s[...]
        sa = _sigmoid(ga_ref[...])
        dya = dm_v * sa
        dz_ref[:, 2 * D_MODEL:3 * D_MODEL] = (dm_v * gu * mixed * sa * (1.0 - sa)).astype(BF16)
        dz_ref[:, 0:D_MODEL] = (dya * mixed * _gelu_grad(u)).astype(BF16)
        dmx = dya * gu
        dmx_s[...] = dmx.astype(BF16)
        tri_t = _tri_mask(transposed=True)
        for gi in range(A_GROUPS):
            wmt = jnp.where(tri_t, wt_ref[gi], 0.0).astype(BF16)
            cols = slice(gi * CHUNK, (gi + 1) * CHUNK)
            dw_acc = jnp.zeros((CHUNK, CHUNK), F32)
            dmx_sum = jnp.zeros((CHUNK, CHUNK), F32)
            for n in range(n_chunk):
                rows = slice(n * CHUNK, (n + 1) * CHUNK)
                blk = dmx_s[rows, cols]
                dvn_s[rows, cols] = _dot(wmt, blk)
                dw_acc = dw_acc + _dot_nt(blk, vn_s[rows, cols])
                dmx_sum = dmx_sum + dmx[rows, cols]
            dw_ref[gi] += jnp.where(tri, dw_acc, 0.0)
            dbs_ref[gi] += jnp.sum(dmx_sum, axis=-1, keepdims=True)
        dvn = dvn_s[...]
        dg_ref[...] += jnp.sum(dvn * xh, axis=0, keepdims=True)
        db_ref[...] += jnp.sum(dvn, axis=0, keepdims=True)
        dxh = dvn * g_ref[...]
        dgv = rs * (dxh - jnp.mean(dxh, axis=-1, keepdims=True)
                    - xh * jnp.mean(dxh * xh, axis=-1, keepdims=True))
        dz_ref[:, D_MODEL:2 * D_MODEL] = (dgv * _gelu_grad(v)).astype(BF16)

    col = lambda c: pl.BlockSpec((tm, D_MODEL), lambda i: (i, c))
    row = pl.BlockSpec((tm, D_MODEL), lambda i: (i, 0))
    vec = pl.BlockSpec((1, D_MODEL), lambda i: (0, 0))
    wsp = pl.BlockSpec((A_GROUPS, CHUNK, CHUNK), lambda i: (0, 0, 0))
    bsp = pl.BlockSpec((A_GROUPS, CHUNK, 1), lambda i: (0, 0, 0))
    return pl.pallas_call(
        body, grid=(T // tm,),
        in_specs=[col(0), col(1), col(2), col(3), row, row, vec, vec, wsp, wsp, bsp, ANY],
        out_specs=[pl.BlockSpec((tm, 4 * D_MODEL), lambda i: (i, 0)), row, vec, vec, wsp, bsp],
        out_shape=[jax.ShapeDtypeStruct((T, 4 * D_MODEL), BF16), jax.ShapeDtypeStruct((T, D_MODEL), BF16),
                   jax.ShapeDtypeStruct((1, D_MODEL), F32), jax.ShapeDtypeStruct((1, D_MODEL), F32),
                   jax.ShapeDtypeStruct((A_GROUPS, CHUNK, CHUNK), F32),
                   jax.ShapeDtypeStruct((A_GROUPS, CHUNK, 1), F32)],
        scratch_shapes=[pltpu.VMEM((tm, D_MODEL), BF16), pltpu.VMEM((tm, D_MODEL), F32),
                        pltpu.VMEM((tm, D_MODEL), BF16), pltpu.VMEM((tm, D_MODEL), F32)],
        name="mixer_bwd", compiler_params=_params("arbitrary"))(
            zm, zm, zm, zm, o, dm, av_g, av_b, w_s, w_st, b_col, dep)


def _rope_tables(pos_ref, invf_ref):
    ang = pos_ref[...].astype(F32) * invf_ref[...]
    lane = lax.broadcasted_iota(jnp.int32, ang.shape, 1)
    cos, sin = jnp.cos(ang), jnp.sin(ang)
    c = jnp.where(lane < QK_ROPE, cos, 0.0)
    sa = jnp.where(lane < QK_ROPE // 2, -sin, 0.0)
    sb = jnp.where((lane >= QK_ROPE // 2) & (lane < QK_ROPE), sin, 0.0)
    return c, sa, sb


def _rope(blk, tabs):
    c, sa, sb = tabs
    return blk * c + pltpu.roll(blk, LANES - QK_ROPE // 2, 1) * sa + pltpu.roll(blk, QK_ROPE // 2, 1) * sb


def _rope_t(dout, tabs):
    c, sa, sb = tabs
    return dout * c + pltpu.roll(dout * sa, QK_ROPE // 2, 1) + pltpu.roll(dout * sb, LANES - QK_ROPE // 2, 1)


def _rms_small(x, g):
    r = lax.rsqrt(jnp.mean(x * x, axis=-1, keepdims=True) + EPS)
    xh = x * r
    return xh, r, xh * g


def _mla_prep_fwd(zs, pos, invf, qg, kvg, wuq_p, wukv):
    T = zs.shape[0]
    tm = _tile(T, 512)
    HW = MLA_HEADS * HEAD_PAD

    def body(zs_ref, pos_ref, invf_ref, qg_ref, kvg_ref, wq_ref, wkv_ref, q_ref, k_ref, v_ref):
        tabs = _rope_tables(pos_ref, invf_ref)
        _, _, cqn = _rms_small(zs_ref[:, 0:Q_LORA], qg_ref[...])
        _, _, ckvn = _rms_small(zs_ref[:, Q_LORA:Q_LORA + KV_LORA], kvg_ref[...])
        q = _dot_nt(cqn.astype(BF16), wq_ref[...])
        kv = _dot(ckvn.astype(BF16), wkv_ref[...])
        kr = _rope(zs_ref[:, Q_LORA + KV_LORA:ZS_W], tabs).astype(BF16)
        for h in range(MLA_HEADS):
            b0 = h * HEAD_PAD
            q_ref[:, b0:b0 + QK_NOPE] = q[:, b0:b0 + QK_NOPE].astype(BF16)
            q_ref[:, b0 + QK_NOPE:b0 + HEAD_PAD] = _rope(q[:, b0 + QK_NOPE:b0 + HEAD_PAD], tabs).astype(BF16)
            k_ref[:, b0:b0 + QK_NOPE] = kv[:, b0:b0 + QK_NOPE].astype(BF16)
            k_ref[:, b0 + QK_NOPE:b0 + HEAD_PAD] = kr
            v_ref[:, h * V_HEAD:(h + 1) * V_HEAD] = kv[:, b0 + QK_NOPE:b0 + HEAD_PAD].astype(BF16)

    full = lambda a: pl.BlockSpec(a.shape, lambda i: (0,) * a.ndim)
    return pl.pallas_call(
        body, grid=(T // tm,),
        in_specs=[pl.BlockSpec((tm, ZS_W), lambda i: (i, 0)), pl.BlockSpec((tm, 1), lambda i: (i, 0)),
                  full(invf), full(qg), full(kvg), full(wuq_p), full(wukv)],
        out_specs=[pl.BlockSpec((tm, HW), lambda i: (i, 0)), pl.BlockSpec((tm, HW), lambda i: (i, 0)),
                   pl.BlockSpec((tm, D_MODEL), lambda i: (i, 0))],
        out_shape=[jax.ShapeDtypeStruct((T, HW), BF16), jax.ShapeDtypeStruct((T, HW), BF16),
                   jax.ShapeDtypeStruct((T, D_MODEL), BF16)],
        name="mla_prep_fwd", compiler_params=_params("parallel"))(zs, pos, invf, qg, kvg, wuq_p, wukv)


def _mla_prep_bwd(zs, pos, invf, qg, kvg, wuq_p, wukv, dq, dk, dv):
    T = zs.shape[0]
    tm = _tile(T, 256)
    HW = MLA_HEADS * HEAD_PAD

    def body(zs_ref, pos_ref, invf_ref, qg_ref, kvg_ref, wq_ref, wkv_ref, dq_ref, dk_ref, dv_ref,
             dzs_ref, cqn_ref, dqp_ref, ckvn_ref, dkv_ref, dqg_ref, dkvg_ref):
        @pl.when(pl.program_id(0) == 0)
        def _():
            dqg_ref[...] = jnp.zeros_like(dqg_ref)
            dkvg_ref[...] = jnp.zeros_like(dkvg_ref)

        tabs = _rope_tables(pos_ref, invf_ref)
        cqh, rq, cqn = _rms_small(zs_ref[:, 0:Q_LORA], qg_ref[...])
        ckvh, rkv, ckvn = _rms_small(zs_ref[:, Q_LORA:Q_LORA + KV_LORA], kvg_ref[...])
        cqn_ref[...] = cqn.astype(BF16)
        ckvn_ref[...] = ckvn.astype(BF16)
        dkr = jnp.zeros((tm, LANES), F32)
        for h in range(MLA_HEADS):
            b0 = h * HEAD_PAD
            dqp_ref[:, b0:b0 + QK_NOPE] = dq_ref[:, b0:b0 + QK_NOPE].astype(BF16)
            dqp_ref[:, b0 + QK_NOPE:b0 + HEAD_PAD] = _rope_t(dq_ref[:, b0 + QK_NOPE:b0 + HEAD_PAD], tabs).astype(BF16)
            dkv_ref[:, b0:b0 + QK_NOPE] = dk_ref[:, b0:b0 + QK_NOPE].astype(BF16)
            dkv_ref[:, b0 + QK_NOPE:b0 + HEAD_PAD] = dv_ref[:, h * V_HEAD:(h + 1) * V_HEAD].astype(BF16)
            dkr = dkr + dk_ref[:, b0 + QK_NOPE:b0 + HEAD_PAD]
        dcqn = _dot(dqp_ref[...], wq_ref[...])
        dckvn = _dot_nt(dkv_ref[...], wkv_ref[...])
        dqg_ref[...] += jnp.sum(dcqn * cqh, axis=0, keepdims=True)
        dkvg_ref[...] += jnp.sum(dckvn * ckvh, axis=0, keepdims=True)
        dxh = dcqn * qg_ref[...]
        dzs_ref[:, 0:Q_LORA] = (rq * (dxh - cqh * jnp.mean(dxh * cqh, axis=-1, keepdims=True))).astype(BF16)
        dxh = dckvn * kvg_ref[...]
        dzs_ref[:, Q_LORA:Q_LORA + KV_LORA] = (
            rkv * (dxh - ckvh * jnp.mean(dxh * ckvh, axis=-1, keepdims=True))).astype(BF16)
        dzs_ref[:, Q_LORA + KV_LORA:ZS_W] = _rope_t(dkr, tabs).astype(BF16)

    full = lambda a: pl.BlockSpec(a.shape, lambda i: (0,) * a.ndim)
    rowb = lambda w: pl.BlockSpec((tm, w), lambda i: (i, 0))
    return pl.pallas_call(
        body, grid=(T // tm,),
        in_specs=[rowb(ZS_W), rowb(1), full(invf), full(qg), full(kvg), full(wuq_p), full(wukv),
                  rowb(HW), rowb(HW), rowb(D_MODEL)],
        out_specs=[rowb(ZS_W), rowb(Q_LORA), rowb(HW), rowb(KV_LORA), rowb(HW), full(qg), full(kvg)],
        out_shape=[jax.ShapeDtypeStruct((T, ZS_W), BF16), jax.ShapeDtypeStruct((T, Q_LORA), BF16),
                   jax.ShapeDtypeStruct((T, HW), BF16), jax.ShapeDtypeStruct((T, KV_LORA), BF16),
                   jax.ShapeDtypeStruct((T, HW), BF16), jax.ShapeDtypeStruct(qg.shape, F32),
                   jax.ShapeDtypeStruct(kvg.shape, F32)],
        name="mla_prep_bwd", compiler_params=_params("arbitrary"))(
            zs, pos, invf, qg, kvg, wuq_p, wukv, dq, dk, dv)


def _causal(tq, kmax, q0):
    r = lax.broadcasted_iota(jnp.int32, (tq, kmax), 0) + q0
    c = lax.broadcasted_iota(jnp.int32, (tq, kmax), 1)
    return c <= r


def _attn_fwd(q, k, v, batch, seq):
    tq = _tile(seq, ATTN_TILE)
    nq = seq // tq

    def body(q_ref, k_ref, v_ref, o_ref, lse_ref):
        diag = _causal(tq, tq, 0)
        for qi in range(nq):
            rows = slice(qi * tq, (qi + 1) * tq)
            qr = q_ref[rows, :]
            s_d = jnp.where(diag, _dot_nt(qr, k_ref[rows, :]) * ATTN_SCALE, NEG_BIG)
            m = jnp.max(s_d, axis=-1, keepdims=True)
            if qi > 0:
                before = slice(0, qi * tq)
                s_b = _dot_nt(qr, k_ref[before, :]) * ATTN_SCALE
                m = jnp.maximum(m, jnp.max(s_b, axis=-1, keepdims=True))
                p_b = jnp.exp(s_b - m)
                l = jnp.sum(p_b, axis=-1, keepdims=True)
                acc = _dot(p_b.astype(BF16), v_ref[before, :])
            p_d = jnp.exp(s_d - m)
            l_d = jnp.sum(p_d, axis=-1, keepdims=True)
            acc_d = _dot(p_d.astype(BF16), v_ref[rows, :])
            l, acc = (l + l_d, acc + acc_d) if qi > 0 else (l_d, acc_d)
            o_ref[rows, :] = acc / l
            lse_ref[rows, :] = jnp.broadcast_to(m + jnp.log(l), (tq, V_HEAD))

    return pl.pallas_call(
        body, grid=(batch, MLA_HEADS),
        in_specs=[pl.BlockSpec((seq, HEAD_PAD), lambda b, h: (b, h)),
                  pl.BlockSpec((seq, HEAD_PAD), lambda b, h: (b, h)),
                  pl.BlockSpec((seq, V_HEAD), lambda b, h: (b, h))],
        out_specs=[pl.BlockSpec((seq, V_HEAD), lambda b, h: (b, h)),
                   pl.BlockSpec((seq, V_HEAD), lambda b, h: (b, h))],
        out_shape=[jax.ShapeDtypeStruct((batch * seq, D_MODEL), F32),
                   jax.ShapeDtypeStruct((batch * seq, D_MODEL), F32)],
        name="attn_fwd", compiler_params=_params("parallel", "parallel"))(q, k, v)


def _attn_bwd(q, k, v, o, do, lse, batch, seq, dep):
    tq = _tile(seq, ATTN_TILE)
    nq = seq // tq

    def body(q_ref, k_ref, v_ref, o_ref, do_ref, lse_ref, dep_ref, dq_ref, dk_ref, dv_ref):
        dk_ref[...] = jnp.zeros_like(dk_ref)
        dv_ref[...] = jnp.zeros_like(dv_ref)
        for qi in range(nq):
            rows = slice(qi * tq, (qi + 1) * tq)
            kmax = (qi + 1) * tq
            qr = q_ref[rows, :]
            dor = do_ref[rows, :]
            kk = k_ref[0:kmax, :]
            s = _dot_nt(qr, kk) * ATTN_SCALE
            p = jnp.where(_causal(tq, kmax, qi * tq), jnp.exp(s - lse_ref[rows, 0:1]), 0.0)
            dp = _dot_nt(dor, v_ref[0:kmax, :])
            delta = jnp.sum(dor.astype(F32) * o_ref[rows, :], axis=-1, keepdims=True)
            ds = (p * (dp - delta) * ATTN_SCALE).astype(BF16)
            dq_ref[rows, :] = _dot(ds, kk)
            dk_ref[0:kmax, :] += _dot_tn(ds, qr)
            dv_ref[0:kmax, :] += _dot_tn(p.astype(BF16), dor)

    qspec = pl.BlockSpec((seq, HEAD_PAD), lambda b, h: (b, h))
    vspec = pl.BlockSpec((seq, V_HEAD), lambda b, h: (b, h))
    T = batch * seq
    return pl.pallas_call(
        body, grid=(batch, MLA_HEADS),
        in_specs=[qspec, qspec, vspec, vspec, vspec, vspec, ANY],
        out_specs=[qspec, qspec, vspec],
        out_shape=[jax.ShapeDtypeStruct((T, MLA_HEADS * HEAD_PAD), F32),
                   jax.ShapeDtypeStruct((T, MLA_HEADS * HEAD_PAD), F32),
                   jax.ShapeDtypeStruct((T, D_MODEL), F32)],
        name="attn_bwd", compiler_params=_params("parallel", "parallel"))(q, k, v, o, do, lse, dep)


def _merge_out(x, yag, zm, o, w_out, ffn_g):
    T = x.shape[0]
    tm = _tile(T, 512)

    def body(x_ref, ya_ref, gb_ref, o_ref, w_ref, g_ref, mg_ref, x1_ref, h2_ref):
        mg = (ya_ref[...] + _sigmoid(gb_ref[...]) * o_ref[...]).astype(BF16)
        mg_ref[...] = mg
        x1 = x_ref[...] + _dot(mg, w_ref[...])
        x1_ref[...] = x1
        r = lax.rsqrt(jnp.mean(x1 * x1, axis=-1, keepdims=True) + EPS)
        h2_ref[...] = (x1 * r * g_ref[...]).astype(BF16)

    row = pl.BlockSpec((tm, D_MODEL), lambda i: (i, 0))
    return pl.pallas_call(
        body, grid=(T // tm,),
        in_specs=[row, row, pl.BlockSpec((tm, D_MODEL), lambda i: (i, 3)), row,
                  pl.BlockSpec((D_MODEL, D_MODEL), lambda i: (0, 0)), pl.BlockSpec((1, D_MODEL), lambda i: (0, 0))],
        out_specs=[row, row, row],
        out_shape=[jax.ShapeDtypeStruct((T, D_MODEL), BF16), jax.ShapeDtypeStruct((T, D_MODEL), F32),
                   jax.ShapeDtypeStruct((T, D_MODEL), BF16)],
        name="merge_out", compiler_params=_params("parallel"))(x, yag, zm, o, w_out, ffn_g)


FF_TILE = 256
FF_BLOCKS = D_FF // FF_TILE
FFB_TILE = 256


def _shift_down(x, k):
    row = lax.broadcasted_iota(jnp.int32, x.shape, 0)
    return jnp.where(row >= k, pltpu.roll(x, k, 0), 0.0)


def _shift_up(x, k):
    n = x.shape[0]
    row = lax.broadcasted_iota(jnp.int32, x.shape, 0)
    return jnp.where(row < n - k, pltpu.roll(x, n - k, 0), 0.0)


def _conv(x, w_ref, b_ref):
    return b_ref[...] + w_ref[2:3, :] * x + w_ref[1:2, :] * _shift_down(x, 1) + w_ref[0:1, :] * _shift_down(x, 2)


def _up_act(h2, wt_up, cw, cb, batch, seq):
    def body(h_ref, wug_ref, wuv_ref, wg_ref, wv_ref, bg_ref, bv_ref, ug_ref, uv_ref, g_ref, v_ref, a_ref):
        h = h_ref[...]
        ug = _dot_nt(h, wug_ref[...])
        uv = _dot_nt(h, wuv_ref[...])
        ug_ref[...] = ug
        uv_ref[...] = uv
        gate = _conv(ug, wg_ref, bg_ref)
        val = _conv(uv, wv_ref, bv_ref)
        g_ref[...] = gate
        v_ref[...] = val
        a_ref[...] = (gate * _sigmoid(gate) * val).astype(BF16)

    blk = pl.BlockSpec((seq, FF_TILE), lambda b, j: (b, j))
    wup = lambda off: pl.BlockSpec((FF_TILE, D_MODEL), lambda b, j: (j + off, 0))
    wsp = lambda off: pl.BlockSpec((3, FF_TILE), lambda b, j: (0, j + off))
    bsp = lambda off: pl.BlockSpec((1, FF_TILE), lambda b, j: (0, j + off))
    T = batch * seq
    f32 = jax.ShapeDtypeStruct((T, D_FF), F32)
    return pl.pallas_call(
        body, grid=(batch, FF_BLOCKS),
        in_specs=[pl.BlockSpec((seq, D_MODEL), lambda b, j: (b, 0)), wup(0), wup(FF_BLOCKS),
                  wsp(0), wsp(FF_BLOCKS), bsp(0), bsp(FF_BLOCKS)],
        out_specs=[blk] * 5,
        out_shape=[f32, f32, f32, f32, jax.ShapeDtypeStruct((T, D_FF), BF16)],
        name="up_act", compiler_params=_params("parallel", "arbitrary"))(h2, wt_up, wt_up, cw, cw, cb, cb)


def _ffn_act_bwd(upg, upv, gate, val, cw, dx2b, w_down, batch, seq):
    def half(du, x, w_ref, dx_ref, dw_ref):
        j = pl.program_id(1)
        up1, up2 = _shift_up(du, 1), _shift_up(du, 2)
        dx_ref[...] = (w_ref[2:3, :] * du + w_ref[1:2, :] * up1 + w_ref[0:1, :] * up2).astype(BF16)
        dw_ref[j, 2:3, :] += jnp.sum(du * x, axis=0, keepdims=True)
        dw_ref[j, 1:2, :] += jnp.sum(up1 * x, axis=0, keepdims=True)
        dw_ref[j, 0:1, :] += jnp.sum(up2 * x, axis=0, keepdims=True)
        dw_ref[j, 3:4, :] += jnp.sum(du, axis=0, keepdims=True)

    def body(ug_ref, uv_ref, g_ref, v_ref, wg_ref, wv_ref, dx_ref, wd_ref, dg_ref, dv_ref, dwg_ref, dwv_ref):
        @pl.when((pl.program_id(0) == 0) & (pl.program_id(1) == 0))
        def _():
            dwg_ref[...] = jnp.zeros_like(dwg_ref)
            dwv_ref[...] = jnp.zeros_like(dwv_ref)

        gate, val = g_ref[...], v_ref[...]
        sg = _sigmoid(gate)
        dav = _dot_nt(dx_ref[...], wd_ref[...])
        half(dav * val * sg * (1.0 + gate * (1.0 - sg)), ug_ref[...], wg_ref, dg_ref, dwg_ref)
        half(dav * gate * sg, uv_ref[...], wv_ref, dv_ref, dwv_ref)

    nb = D_FF // FFB_TILE
    blk = pl.BlockSpec((seq, FFB_TILE), lambda b, j: (b, j))
    wsp = lambda off: pl.BlockSpec((3, FFB_TILE), lambda b, j: (0, j + off))
    acc = pl.BlockSpec((nb, 4, FFB_TILE), lambda b, j: (0, 0, 0))
    T = batch * seq
    dupg, dupv, dwg, dwv = pl.pallas_call(
        body, grid=(batch, nb),
        in_specs=[blk, blk, blk, blk, wsp(0), wsp(nb),
                  pl.BlockSpec((seq, D_MODEL), lambda b, j: (b, 0)),
                  pl.BlockSpec((FFB_TILE, D_MODEL), lambda b, j: (j, 0))],
        out_specs=[blk, blk, acc, acc],
        out_shape=[jax.ShapeDtypeStruct((T, D_FF), BF16), jax.ShapeDtypeStruct((T, D_FF), BF16),
                   jax.ShapeDtypeStruct((nb, 4, FFB_TILE), F32), jax.ShapeDtypeStruct((nb, 4, FFB_TILE), F32)],
        name="ffn_act_bwd", compiler_params=_params("arbitrary", "arbitrary"))(
            upg, upv, gate, val, cw, cw, dx2b, w_down)
    dwg, dwv = (jnp.transpose(a, (1, 0, 2)).reshape(4, D_FF) for a in (dwg, dwv))
    return dupg, dupv, dwg[:3], dwv[:3], dwg[3:], dwv[3:]


def _down_loss(a, w_down, x1, target, gfin):
    T = x1.shape[0]
    tm = _tile(T, 512)

    def body(a_ref, w_ref, x1_ref, t_ref, g_ref, dx_ref, dxb_ref, loss_ref, dg_ref):
        @pl.when(pl.program_id(0) == 0)
        def _():
            loss_ref[...] = jnp.zeros_like(loss_ref)
            dg_ref[...] = jnp.zeros_like(dg_ref)

        x2 = x1_ref[...] + _dot(a_ref[...], w_ref[...])
        r = lax.rsqrt(jnp.mean(x2 * x2, axis=-1, keepdims=True) + EPS)
        xh = x2 * r
        g = g_ref[...]
        diff = xh * g - t_ref[...]
        loss_ref[...] += 0.5 * jnp.sum(jnp.mean(diff * diff, axis=-1, keepdims=True))
        dy = diff * (1.0 / D_MODEL)
        dg_ref[...] += jnp.sum(dy * xh, axis=0, keepdims=True)
        dxh = dy * g
        dx = r * (dxh - xh * jnp.mean(dxh * xh, axis=-1, keepdims=True))
        dx_ref[...] = dx
        dxb_ref[...] = dx.astype(BF16)

    row = pl.BlockSpec((tm, D_MODEL), lambda i: (i, 0))
    vec = pl.BlockSpec((1, D_MODEL), lambda i: (0, 0))
    return pl.pallas_call(
        body, grid=(T // tm,),
        in_specs=[pl.BlockSpec((tm, D_FF), lambda i: (i, 0)),
                  pl.BlockSpec((D_FF, D_MODEL), lambda i: (0, 0)), row, row, vec],
        out_specs=[row, row, pl.BlockSpec((8, LANES), lambda i: (0, 0)), vec],
        out_shape=[jax.ShapeDtypeStruct((T, D_MODEL), F32), jax.ShapeDtypeStruct((T, D_MODEL), BF16),
                   jax.ShapeDtypeStruct((8, LANES), F32), jax.ShapeDtypeStruct((1, D_MODEL), F32)],
        name="down_loss", compiler_params=_params("arbitrary"))(a, w_down, x1, target, gfin)


def _local_step(x, positions, target, mix_norm, av_g, av_b, w_s, b_s, q_norm, kv_norm, ffn_norm, conv_b,
                final_norm, comm):
    batch, seq, _ = x.shape
    T = batch * seq
    x = x.reshape(T, D_MODEL)
    target = target.reshape(T, D_MODEL)
    pos = positions.reshape(T, 1)
    half = jnp.arange(0, QK_ROPE, 2, dtype=F32) / QK_ROPE
    inv_freq = 1.0 / (ROPE_THETA ** half)
    invf = jnp.concatenate([inv_freq, inv_freq, jnp.zeros((LANES - QK_ROPE,), F32)]).reshape(1, LANES)
    w_st = jnp.swapaxes(w_s, 1, 2)
    b_col = b_s.reshape(A_GROUPS, CHUNK, 1)

    wt_in = comm.in_weights()
    h, zm, zs = _in_proj(x, mix_norm, wt_in)
    yag = _mixer_a_fwd(zm, av_g, av_b, w_s, b_col)
    wuq_p, wukv, w_out = comm.mla_weights(after=yag)
    q, k, v = _mla_prep_fwd(zs, pos, invf, q_norm, kv_norm, wuq_p, wukv)
    o, lse = _attn_fwd(q, k, v, batch, seq)
    merged, x1, h2 = _merge_out(x, yag, zm, o, w_out, ffn_norm)
    wt_up, conv_w, w_down = comm.ffn_weights(after=merged)
    upg, upv, gate, val, act = _up_act(h2, wt_up, conv_w, conv_b, batch, seq)
    dx2, dx2b, loss_acc, d_final = _down_loss(act, w_down, x1, target, final_norm)

    d_wdown = _mm_tn(act, dx2b, "dw_down")
    dupg, dupv, dcwg, dcwv, dcbg, dcbv = _ffn_act_bwd(upg, upv, gate, val, conv_w, dx2b, w_down, batch, seq)
    d_wt_up = jnp.concatenate([_mm_tn(dupg, h2, "dw_up_gate"), _mm_tn(dupv, h2, "dw_up_val")], axis=0)
    dx1, d_ffn_norm, dmerged = _proj_bwd(
        [dupg, dupv], wt_up, [(0, (0, D_FF), (0, D_FF)), (1, (0, D_FF), (D_FF, 2 * D_FF))],
        x1, ffn_norm, dx2, "up_proj_bwd", w2=w_out)
    d_wout = _mm_tn(merged, dx1, "dw_out")
    token = comm.send_ffn_grads(d_wdown, d_wt_up, jnp.concatenate([dcwg, dcwv], axis=1), d_wout)
    dzm, do, d_avg, d_avb, d_ws, d_bs = _mixer_bwd(zm, o, dmerged, av_g, av_b, w_s, w_st, b_col, token)
    token = comm.send_small_grads(_pack_small(dict(
        a_v_norm_g=d_avg, a_v_norm_b=d_avb, a_spatial_w=d_ws, a_spatial_b=d_bs, ffn_norm=d_ffn_norm,
        conv_b=jnp.concatenate([dcbg, dcbv], axis=1), final_norm=d_final), SMALL_EARLY))
    dq, dk, dv = _attn_bwd(q, k, v, o, do, lse, batch, seq, token)
    dzs, cqn, dqp, ckvn, dkv, d_qn, d_kvn = _mla_prep_bwd(zs, pos, invf, q_norm, kv_norm, wuq_p, wukv, dq, dk, dv)
    d_wt_main = _mm_tn(dzm, h, "dw_in_main")
    d_wt_zs = _mm_tn(dzs, h, "dw_in_small")
    token = comm.send_in_grads(d_wt_main, d_wt_zs)
    d_wuq_p = _mm_tn(dqp, cqn, "dw_uq", dep=token)
    d_wukv = _mm_tn(ckvn, dkv, "dw_ukv", dep=token)
    terms = [(0, (i * D_MODEL, (i + 1) * D_MODEL), rows) for i, rows in enumerate(IN_ROWS_MAIN)]
    terms.append((1, (0, ZS_W), IN_ROWS_ZS))
    dx, d_mix_norm = _proj_bwd([dzm, dzs], wt_in, terms, x, mix_norm, dx1, "in_proj_bwd", dep=token)
    late = _pack_small(dict(q_a_norm=d_qn, kv_a_norm=d_kvn, mix_norm=d_mix_norm), SMALL_LATE,
                       extra=loss_acc[0, 0].reshape(1))
    token = comm.send_late_grads(d_wuq_p, d_wukv, late)
    return dx.reshape(batch, seq, D_MODEL), token


MESH_ID = pl.DeviceIdType.MESH
EFFECT = pltpu.SideEffectType.DATAFLOW_SIDE_EFFECTING


def _mesh_pos():
    return lax.axis_index("x"), lax.axis_index("y"), lax.axis_index("c")


def _peer(pos, d):
    x, y, c = pos
    px = 1 - x if d & 4 else x
    py = 1 - y if d & 2 else y
    pc = 1 - c if d & 1 else c
    return (px, py, pc), 4 * px + 2 * py + pc


def _copy(src_ref, land_ref, send_sems, recv_sems, a, d, pos, exchange, landing_here):
    peer, pid = _peer(pos, d)
    me = 4 * pos[0] + 2 * pos[1] + pos[2]
    if exchange:
        src, dst = src_ref.at[pid], land_ref.at[d]
    else:
        src, dst = src_ref, land_ref.at[pid if landing_here else me]
    return pltpu.make_async_remote_copy(
        src_ref=src, dst_ref=dst, send_sem=send_sems.at[a * (N_DEV - 1) + d - 1],
        recv_sem=recv_sems.at[a * (N_DEV - 1) + d - 1],
        device_id=peer, device_id_type=MESH_ID)


def _start_copies(groups, modes, name, dep=None):
    sizes = [len(g) for g in groups]
    srcs = [s for g in groups for s in g]
    lands = [lax.empty(s.shape if modes[gi] else (N_DEV,) + s.shape, s.dtype)
             for gi, g in enumerate(groups) for s in g]
    n, ng = len(srcs), len(groups)
    n_in = 2 * n + (dep is not None)

    def body(*refs):
        src_refs, land_refs = refs[:n], refs[n:2 * n]
        sems = refs[n_in:n_in + 3 * ng]
        token = refs[-1]
        pos = _mesh_pos()
        k = 0
        for gi, size in enumerate(sizes):
            for a in range(size):
                _own_copy(src_refs[k], land_refs[k], sems[3 * gi + 2], a, pos, modes[gi]).start()
                for d in range(1, N_DEV):
                    _copy(src_refs[k], land_refs[k], sems[3 * gi], sems[3 * gi + 1], a, d, pos, modes[gi],
                          landing_here=False).start()
                k += 1
        token[...] = jnp.zeros_like(token)

    sem_shapes = []
    for size in sizes:
        remote = pltpu.SemaphoreType.DMA((size * (N_DEV - 1),))
        sem_shapes += [remote, remote, pltpu.SemaphoreType.DMA((size,))]
    out = pl.pallas_call(
        body, name=name,
        out_shape=(*sem_shapes, *[pltpu.HBM(a.shape, a.dtype) for a in srcs + lands],
                   jax.ShapeDtypeStruct((8, LANES), F32)),
        in_specs=[HBM] * (2 * n) + [ANY] * (dep is not None),
        out_specs=(*[SEM] * (3 * ng), *[HBM] * (2 * n), pl.BlockSpec(memory_space=pltpu.VMEM)),
        input_output_aliases={i: 3 * ng + i for i in range(2 * n)},
        compiler_params=pltpu.CompilerParams(has_side_effects=EFFECT),
    )(*[pltpu.with_memory_space_constraint(a, pltpu.HBM) for a in srcs + lands], *([dep] if dep is not None else []))
    thru = out[3 * ng:3 * ng + 2 * n]
    handles, k = [], 0
    for gi, size in enumerate(sizes):
        handles.append((out[3 * gi:3 * gi + 3], thru[k:k + size], thru[n + k:n + k + size]))
        k += size
    return handles, out[-1]


def _own_copy(src_ref, land_ref, local_sems, a, pos, exchange):
    me = 4 * pos[0] + 2 * pos[1] + pos[2]
    src, dst = (src_ref.at[me], land_ref.at[0]) if exchange else (src_ref, land_ref.at[me])
    return pltpu.make_async_copy(src, dst, local_sems.at[a])


def _wait_copies(handle, exchange, after, name):
    sems, srcs, lands = handle
    n = len(srcs)

    def body(*refs):
        src_refs, land_refs = refs[:n], refs[n:2 * n]
        send, recv, local = refs[2 * n:2 * n + 3]
        pos = _mesh_pos()
        for a in range(n):
            _own_copy(src_refs[a], land_refs[a], local, a, pos, exchange).wait()
            for d in range(1, N_DEV):
                cp = _copy(src_refs[a], land_refs[a], send, recv, a, d, pos, exchange, landing_here=True)
                cp.wait_send()
                cp.wait_recv()

    out = pl.pallas_call(
        body, name=name,
        out_shape=tuple(pltpu.HBM(a.shape, a.dtype) for a in (*srcs, *lands)),
        in_specs=[HBM] * (2 * n) + [SEM, SEM, SEM, ANY], out_specs=[HBM] * (2 * n),
        input_output_aliases={i: i for i in range(2 * n)},
        compiler_params=pltpu.CompilerParams(has_side_effects=EFFECT),
    )(*srcs, *lands, *sems, after)
    return out[n:]


def _gather_now(a, name):
    def body(x_ref, out_ref, send_sems, recv_sems, local_sem):
        x, y, c = _mesh_pos()
        me, sibling = (x, y, c), (x, y, 1 - c)
        chips = [(1 - x, y), (x, 1 - y), (1 - x, 1 - y)]

        def slot(p):
            return out_ref.at[4 * p[0] + 2 * p[1] + p[2]]

        def copy(k, block, to, src=None):
            return pltpu.make_async_remote_copy(
                src_ref=slot(block) if src is None else src, dst_ref=slot(block), send_sem=send_sems.at[k],
                recv_sem=recv_sems.at[k], device_id=to, device_id_type=MESH_ID)

        mine = pltpu.make_async_copy(x_ref, slot(me), local_sem)
        mine.start()
        first = [copy(0, me, sibling, src=x_ref)]
        first += [copy(1 + j, me, (*chip, c), src=x_ref) for j, chip in enumerate(chips)]
        for cp in first:
            cp.start()
        passed = [copy(4 + j, (*chip, c), sibling) for j, chip in enumerate(chips)]
        for j, chip in enumerate(chips):
            copy(1 + j, (*chip, c), me).wait_recv()
            passed[j].start()
        copy(0, sibling, me).wait_recv()
        for j, chip in enumerate(chips):
            copy(4 + j, (*chip, 1 - c), me).wait_recv()
        for cp in first + passed:
            cp.wait_send()
        mine.wait()

    return pl.pallas_call(
        body, in_specs=[ANY], out_specs=ANY,
        out_shape=jax.ShapeDtypeStruct((N_DEV,) + a.shape, a.dtype),
        scratch_shapes=[pltpu.SemaphoreType.DMA((N_DEV - 1,)), pltpu.SemaphoreType.DMA((N_DEV - 1,)),
                        pltpu.SemaphoreType.DMA],
        name=name, compiler_params=pltpu.CompilerParams(has_side_effects=True))(a)


def _adamw(parts, w, m, v, name):
    R, C = w.shape
    tr, tc = R, C
    if N_DEV * R * C * parts.dtype.itemsize > SMALL_BLOCK_BYTES:
        tr = next((t for t in range(min(R, 256) // 16 * 16, 15, -16) if R % t == 0), R)
        if tr == R:
            tc = _tile(C, 256)
    c1 = 1.0 - ADAM_B1 ** ADAM_STEP
    c2 = 1.0 - ADAM_B2 ** ADAM_STEP

    def body(p_ref, w_ref, m_ref, v_ref, g_ref, d_ref, nm_ref, nv_ref):
        g = p_ref[0].astype(F32)
        for k in range(1, N_DEV):
            g = g + p_ref[k].astype(F32)
        nm = ADAM_B1 * m_ref[...] + (1.0 - ADAM_B1) * g
        nv = ADAM_B2 * v_ref[...] + (1.0 - ADAM_B2) * (g * g)
        g_ref[...] = g
        nm_ref[...] = nm
        nv_ref[...] = nv
        d_ref[...] = -ADAM_LR * ((nm / c1) / (jnp.sqrt(nv / c2) + ADAM_EPS) + ADAM_WD * w_ref[...])

    blk = pl.BlockSpec((tr, tc), lambda i, j: (i, j))
    shp = jax.ShapeDtypeStruct((R, C), F32)
    return pl.pallas_call(
        body, grid=(R // tr, C // tc),
        in_specs=[pl.BlockSpec((N_DEV, tr, tc), lambda i, j: (0, i, j)), blk, blk, blk],
        out_specs=[blk, blk, blk, blk], out_shape=[shp, shp, shp, shp],
        name=name, compiler_params=_params("parallel", "parallel"))(parts, w, m, v)


SPLIT_V = 2 * D_MODEL
SPLIT_KR = SPLIT_V + Q_LORA + KV_LORA + QK_ROPE
IN_DIM = SPLIT_KR + 2 * D_MODEL
IN_ROWS_MAIN = ((0, D_MODEL), (D_MODEL, SPLIT_V), (SPLIT_KR, SPLIT_KR + D_MODEL), (SPLIT_KR + D_MODEL, IN_DIM))
IN_ROWS_ZS = (SPLIT_V, SPLIT_V + ZS_W)

SMALL_EARLY = ("a_v_norm_g", "a_v_norm_b", "a_spatial_w", "a_spatial_b", "ffn_norm", "conv_b", "final_norm")
SMALL_LATE = ("q_a_norm", "kv_a_norm", "mix_norm")


def _pack_rows(a):
    flat = a.reshape(-1)
    rows = -(-flat.shape[0] // LANES)
    rows8 = -(-rows // 8) * 8
    return jnp.pad(flat, (0, rows8 * LANES - flat.shape[0])).reshape(rows8, LANES)


def _pack_small(tree, names, extra=None):
    parts = [_pack_rows(tree[n]) for n in names]
    if extra is not None:
        parts.append(_pack_rows(extra))
    return jnp.concatenate(parts, axis=0)


def _unpack_small(buf, names, shapes):
    out, r = {}, 0
    for n in names:
        size = math.prod(shapes[n])
        rows8 = -(-(-(-size // LANES)) // 8) * 8
        out[n] = buf[r:r + rows8].reshape(-1)[:size].reshape(shapes[n])
        r += rows8
    return out, r


def _cols_from_shards(g):
    return jnp.transpose(g, (1, 0, 2)).reshape(g.shape[1], N_DEV * g.shape[2])


def _shards_from_cols(a):
    R, W = a.shape
    return jnp.transpose(a.reshape(R, N_DEV, W // N_DEV), (1, 0, 2))


class _Comm:
    GATHER_GROUPS = (("w_uq", "w_ukv", "w_out"), ("w_up", "conv_w", "w_down"))
    FFN_GRADS = ("w_down", "w_up", "conv_w", "w_out")
    LATE_GRADS = ("w_uq", "w_ukv")
    TRANSPOSED = ("w_in", "w_up", "w_uq")

    def __init__(self, shards, me):
        self.me = me
        local = {n: a.astype(F32 if n == "conv_w" else BF16) for n, a in shards.items()}
        self.g_in = _gather_now(local["w_in"], "gather_w_in")
        groups = [[local[n] for n in g] for g in self.GATHER_GROUPS]
        (self.h_mla, self.h_ffn), _ = _start_copies(groups, [False] * 2, "gather_start", dep=self.g_in)

    def in_weights(self):
        return self.g_in.reshape(IN_DIM, D_MODEL)

    def mla_weights(self, after):
        g_uq, g_ukv, g_out = _wait_copies(self.h_mla, False, after, "gather_wait_mla")
        wuq_p = jnp.pad(g_uq, ((0, 0), (0, HEAD_PAD - QK_HEAD), (0, 0))).reshape(MLA_HEADS * HEAD_PAD, Q_LORA)
        return wuq_p, _cols_from_shards(g_ukv), g_out.reshape(D_MODEL, D_MODEL)

    def ffn_weights(self, after):
        g_up, g_cw, g_down = _wait_copies(self.h_ffn, False, after, "gather_wait_ffn")
        return g_up.reshape(2 * D_FF, D_MODEL), _cols_from_shards(g_cw), g_down.reshape(D_FF, D_MODEL)

    def send_ffn_grads(self, d_wdown, d_wt_up, d_convw, d_wout):
        group = [d_wdown.reshape(N_DEV, D_FF // N_DEV, D_MODEL), d_wt_up.reshape(N_DEV, 2 * D_FF // N_DEV, D_MODEL),
                 _shards_from_cols(d_convw), d_wout.reshape(N_DEV, D_MODEL // N_DEV, D_MODEL)]
        (self.h_ffn_grads,), token = _start_copies([group], [True], "ffn_grads_start")
        return token

    def send_small_grads(self, packed):
        (self.h_small_early,), token = _start_copies([[packed]], [False], "small_grads_start")
        return token

    def send_in_grads(self, d_wt_main, d_wt_zs):
        d_in = jnp.concatenate([d_wt_main[:SPLIT_V], d_wt_zs[:SPLIT_KR - SPLIT_V], d_wt_main[SPLIT_V:]], axis=0)
        blocks = d_in.reshape(N_DEV, IN_DIM // N_DEV, D_MODEL)
        (self.h_in_grads,), token = _start_copies([[blocks]], [True], "in_grads_start")
        return token

    def send_late_grads(self, d_wuq_p, d_wukv, packed):
        d_uq = d_wuq_p.reshape(MLA_HEADS, HEAD_PAD, Q_LORA)[:, :QK_HEAD, :]
        (self.h_late_grads, self.h_late_small), token = _start_copies(
            [[d_uq, _shards_from_cols(d_wukv)], [packed]], [True, False], "late_grads_start")
        return token


def kernel(x, positions, mix_norm, w_in, a_v_norm_g, a_v_norm_b, a_spatial_w, a_spatial_b, q_a_norm, w_uq, kv_a_norm, w_ukv, w_out, ffn_norm, w_up, conv_w, conv_b, w_down, final_norm, loss_target, m_mix_norm, m_w_in, m_a_v_norm_g, m_a_v_norm_b, m_a_spatial_w, m_a_spatial_b, m_q_a_norm, m_w_uq, m_kv_a_norm, m_w_ukv, m_w_out, m_ffn_norm, m_w_up, m_conv_w, m_conv_b, m_w_down, m_final_norm, v_mix_norm, v_w_in, v_a_v_norm_g, v_a_v_norm_b, v_a_spatial_w, v_a_spatial_b, v_q_a_norm, v_w_uq, v_kv_a_norm, v_w_ukv, v_w_out, v_ffn_norm, v_w_up, v_conv_w, v_conv_b, v_w_down, v_final_norm):
    names = ("mix_norm", "w_in", "a_v_norm_g", "a_v_norm_b", "a_spatial_w", "a_spatial_b", "q_a_norm", "w_uq",
             "kv_a_norm", "w_ukv", "w_out", "ffn_norm", "w_up", "conv_w", "conv_b", "w_down", "final_norm")
    w = dict(zip(names, (mix_norm, w_in, a_v_norm_g, a_v_norm_b, a_spatial_w, a_spatial_b, q_a_norm, w_uq,
                         kv_a_norm, w_ukv, w_out, ffn_norm, w_up, conv_w, conv_b, w_down, final_norm)))
    m = dict(zip(names, (m_mix_norm, m_w_in, m_a_v_norm_g, m_a_v_norm_b, m_a_spatial_w, m_a_spatial_b,
                         m_q_a_norm, m_w_uq, m_kv_a_norm, m_w_ukv, m_w_out, m_ffn_norm, m_w_up, m_conv_w,
                         m_conv_b, m_w_down, m_final_norm)))
    v = dict(zip(names, (v_mix_norm, v_w_in, v_a_v_norm_g, v_a_v_norm_b, v_a_spatial_w, v_a_spatial_b,
                         v_q_a_norm, v_w_uq, v_kv_a_norm, v_w_ukv, v_w_out, v_ffn_norm, v_w_up, v_conv_w,
                         v_conv_b, v_w_down, v_final_norm)))
    shapes = {n: w[n].shape for n in names}
    me = 4 * lax.axis_index("x") + 2 * lax.axis_index("y") + lax.axis_index("c")
    def view(tree, n):
        a = tree[n].reshape(tree[n].shape[-2:])
        return a.T if n in _Comm.TRANSPOSED else a

    comm = _Comm({n: view(w, n) for n in ("w_in",) + _Comm.GATHER_GROUPS[0] + _Comm.GATHER_GROUPS[1]}, me)

    grad_x, token = _local_step(
        x, positions, loss_target, w["mix_norm"], w["a_v_norm_g"], w["a_v_norm_b"], w["a_spatial_w"][0],
        w["a_spatial_b"][0], w["q_a_norm"], w["kv_a_norm"], w["ffn_norm"], w["conv_b"],
        w["final_norm"].reshape(1, D_MODEL), comm)

    out_g, out_d, out_m, out_v = {}, {}, {}, {}

    def update(n, parts):
        res = _adamw(parts, view(w, n), view(m, n), view(v, n), "adamw_" + n)
        out_g[n], out_d[n], out_m[n], out_v[n] = (
            (t.T if n in _Comm.TRANSPOSED else t).reshape(shapes[n]) for t in res)

    for n, parts in zip(_Comm.FFN_GRADS, _wait_copies(comm.h_ffn_grads, True, token, "ffn_grads_wait")):
        update(n, parts)
    update("w_in", _wait_copies(comm.h_in_grads, True, out_d["w_up"], "in_grads_wait")[0])
    for n, parts in zip(_Comm.LATE_GRADS, _wait_copies(comm.h_late_grads, True, out_d["w_in"], "late_grads_wait")):
        update(n, parts)

    (early_parts,) = _wait_copies(comm.h_small_early, False, out_d["w_uq"], "small_grads_wait")
    res = _adamw(early_parts, _pack_small(w, SMALL_EARLY), _pack_small(m, SMALL_EARLY), _pack_small(v, SMALL_EARLY),
                 "adamw_small")
    unpacked = [_unpack_small(t, SMALL_EARLY, shapes)[0] for t in res]
    for n in SMALL_EARLY:
        out_g[n], out_d[n], out_m[n], out_v[n] = (u[n] for u in unpacked)

    (late_parts,) = _wait_copies(comm.h_late_small, False, res[1], "late_small_wait")
    zero = jnp.zeros((1,), F32)
    res = _adamw(late_parts, _pack_small(w, SMALL_LATE, extra=zero), _pack_small(m, SMALL_LATE, extra=zero),
                 _pack_small(v, SMALL_LATE, extra=zero), "adamw_late")
    unpacked = [_unpack_small(t, SMALL_LATE, shapes) for t in res]
    for n in SMALL_LATE:
        out_g[n], out_d[n], out_m[n], out_v[n] = (u[0][n] for u in unpacked)
    loss = res[0][unpacked[0][1], 0]

    return (loss, grad_x, *[out_g[n] for n in names], *[out_d[n] for n in names],
            *[out_m[n] for n in names], *[out_v[n] for n in names])
```

```python
import functools
import math

import jax
import jax.numpy as jnp
from jax import lax
from jax.experimental import pallas as pl
from jax.experimental.pallas import tpu as pltpu

F32 = jnp.float32
BF16 = jnp.bfloat16

N_DEV = 8
D_MODEL = 1024
EPS = 1e-6
A_GROUPS = 8
CHUNK = 128
MLA_HEADS = 8
QK_NOPE = 128
QK_ROPE = 64
QK_HEAD = QK_NOPE + QK_ROPE
HEAD_PAD = 256
V_HEAD = 128
Q_LORA = 256
KV_LORA = 128
ROPE_THETA = 10000.0
D_FF = 2816
ZS_W = 512
ATTN_SCALE = QK_HEAD ** -0.5
ATTN_TILE = 512
NEG_BIG = -1e30

ADAM_LR = 0.001
ADAM_B1 = 0.9
ADAM_B2 = 0.999
ADAM_EPS = 1e-08
ADAM_WD = 0.01
ADAM_STEP = 10

VMEM_LIMIT = 56 * 1024 * 1024
SMALL_BLOCK_BYTES = 5 * 1024 * 1024
LANES = 128

GELU_K = math.sqrt(2.0 / math.pi)
GELU_C = 0.044715

ANY = pl.BlockSpec(memory_space=pl.ANY)
HBM = pl.BlockSpec(memory_space=pltpu.HBM)
SEM = pl.BlockSpec(memory_space=pltpu.SEMAPHORE)


def _tile(n, pref):
    for t in (pref, 512, 256, 128, 64, 32, 16, 8):
        if t <= pref and n % t == 0:
            return t
    return n


def _wide_tile(n, cap=1408):
    return next((t for t in range(min(n, cap) // LANES * LANES, 0, -LANES) if n % t == 0), n)


def _params(*sem):
    return pltpu.CompilerParams(dimension_semantics=sem, vmem_limit_bytes=VMEM_LIMIT)


def _dot(a, b):
    return jnp.dot(a, b, preferred_element_type=F32)


def _dot_nt(a, b):
    return lax.dot_general(a, b, (((1,), (1,)), ((), ())), preferred_element_type=F32)


def _dot_tn(a, b):
    return lax.dot_general(a, b, (((0,), (0,)), ((), ())), preferred_element_type=F32)


def _sigmoid(x):
    return 1.0 / (1.0 + jnp.exp(-x))


def _gelu(x):
    t = jnp.tanh(GELU_K * (x + GELU_C * x * x * x))
    return 0.5 * x * (1.0 + t)


def _gelu_grad(x):
    t = jnp.tanh(GELU_K * (x + GELU_C * x * x * x))
    return 0.5 * (1.0 + t) + 0.5 * x * (1.0 - t * t) * GELU_K * (1.0 + 3.0 * GELU_C * x * x)


def _in_proj(x, g, wt):
    T, Dm = x.shape
    tm = _tile(T, 256)

    def body(x_ref, g_ref, wt_ref, h_ref, zm_ref, zs_ref):
        xf = x_ref[...]
        r = lax.rsqrt(jnp.mean(xf * xf, axis=-1, keepdims=True) + EPS)
        h = (xf * r * g_ref[...]).astype(BF16)
        h_ref[...] = h
        for i, (r0, r1) in enumerate(IN_ROWS_MAIN):
            zm_ref[:, i * D_MODEL:(i + 1) * D_MODEL] = _dot_nt(h, wt_ref[r0:r1, :])
        zs_ref[...] = _dot_nt(h, wt_ref[IN_ROWS_ZS[0]:IN_ROWS_ZS[1], :])

    row = lambda n: pl.BlockSpec((tm, n), lambda i: (i, 0))
    return pl.pallas_call(
        body, grid=(T // tm,),
        in_specs=[row(Dm), pl.BlockSpec((1, Dm), lambda i: (0, 0)), pl.BlockSpec(wt.shape, lambda i: (0, 0))],
        out_specs=[row(Dm), row(4 * D_MODEL), row(ZS_W)],
        out_shape=[jax.ShapeDtypeStruct((T, Dm), BF16), jax.ShapeDtypeStruct((T, 4 * D_MODEL), F32),
                   jax.ShapeDtypeStruct((T, ZS_W), F32)],
        name="in_proj", compiler_params=_params("parallel"))(x, g, wt)


def _proj_bwd(acts, wt, terms, x, g, dres, name, w2=None, dep=None):
    T, Dm = x.shape
    tm = _tile(T, 256)
    n_a = len(acts)

    def body(*refs):
        ins, outs = refs[:n_a + 4 + (w2 is not None) + (dep is not None)], refs[-2 - (w2 is not None):]
        wt_ref, x_ref, g_ref, dres_ref = ins[n_a:n_a + 4]
        dx_ref, dg_ref = outs[0], outs[1]

        @pl.when(pl.program_id(0) == 0)
        def _():
            dg_ref[...] = jnp.zeros_like(dg_ref)

        dy = None
        for i, (c0, c1), (r0, r1) in terms:
            t = _dot(ins[i][:, c0:c1], wt_ref[r0:r1, :])
            dy = t if dy is None else dy + t
        xf = x_ref[...]
        r = lax.rsqrt(jnp.mean(xf * xf, axis=-1, keepdims=True) + EPS)
        xh = xf * r
        dg_ref[...] += jnp.sum(dy * xh, axis=0, keepdims=True)
        dxh = dy * g_ref[...]
        dx = dres_ref[...] + r * (dxh - xh * jnp.mean(dxh * xh, axis=-1, keepdims=True))
        dx_ref[...] = dx
        if w2 is not None:
            outs[2][...] = _dot_nt(dx.astype(BF16), ins[n_a + 4][...])

    row = pl.BlockSpec((tm, Dm), lambda i: (i, 0))
    vec = pl.BlockSpec((1, Dm), lambda i: (0, 0))
    in_specs = [pl.BlockSpec((tm, a.shape[1]), lambda i: (i, 0)) for a in acts]
    in_specs += [pl.BlockSpec(wt.shape, lambda i: (0, 0)), row, vec, row]
    args = [*acts, wt, x, g, dres]
    out_specs = [row, vec]
    out_shape = [jax.ShapeDtypeStruct((T, Dm), F32), jax.ShapeDtypeStruct((1, Dm), F32)]
    if w2 is not None:
        in_specs.append(pl.BlockSpec(w2.shape, lambda i: (0, 0)))
        args.append(w2)
        out_specs.append(pl.BlockSpec((tm, w2.shape[0]), lambda i: (i, 0)))
        out_shape.append(jax.ShapeDtypeStruct((T, w2.shape[0]), F32))
    if dep is not None:
        in_specs.append(ANY)
        args.append(dep)
    return pl.pallas_call(
        body, grid=(T // tm,), in_specs=in_specs, out_specs=out_specs, out_shape=out_shape,
        name=name, compiler_params=_params("arbitrary"))(*args)


def _mm_tn(a, b, name, dep=None, rows=None, row0=0, into=None):
    T, M = a.shape
    N = b.shape[1]
    tm, tn, tt = _wide_tile(M), _wide_tile(N), _tile(T, 2048)
    n_t = T // tt
    off = row0 // tm
    extra = ([dep] if dep is not None else []) + ([into] if into is not None else [])

    def body(a_ref, b_ref, *refs):
        o_ref, acc_ref = refs[-2:]
        t = pl.program_id(2)

        @pl.when(t == 0)
        def _():
            acc_ref[...] = jnp.zeros_like(acc_ref)

        acc_ref[...] += _dot_tn(a_ref[...].astype(BF16), b_ref[...].astype(BF16))

        @pl.when(t == n_t - 1)
        def _():
            o_ref[...] = acc_ref[...].astype(BF16)

    return pl.pallas_call(
        body, grid=(M // tm, N // tn, n_t),
        in_specs=[pl.BlockSpec((tt, tm), lambda i, j, t: (t, i)),
                  pl.BlockSpec((tt, tn), lambda i, j, t: (t, j))] + [ANY] * len(extra),
        out_specs=pl.BlockSpec((tm, tn), lambda i, j, t: (i + off, j)),
        out_shape=jax.ShapeDtypeStruct((rows or M, N), BF16),
        scratch_shapes=[pltpu.VMEM((tm, tn), F32)],
        input_output_aliases={} if into is None else {1 + len(extra): 0},
        name=name, compiler_params=_params("parallel", "parallel", "arbitrary"))(a, b, *extra)


def _layer_norm_fwd(gv, g, b):
    mu = jnp.mean(gv, axis=-1, keepdims=True)
    xc = gv - mu
    rs = lax.rsqrt(jnp.mean(xc * xc, axis=-1, keepdims=True) + EPS)
    xh = xc * rs
    return xh, rs, xh * g + b


def _tri_mask(transposed=False):
    r = lax.broadcasted_iota(jnp.int32, (CHUNK, CHUNK), 0)
    c = lax.broadcasted_iota(jnp.int32, (CHUNK, CHUNK), 1)
    return r <= c if transposed else c <= r


def _mixer_a_fwd(zm, av_g, av_b, w_s, b_col):
    T = zm.shape[0]
    tm = _tile(T, 256)
    n_chunk = tm // CHUNK

    def body(u_ref, v_ref, ga_ref, g_ref, b_ref, w_ref, bc_ref, y_ref, vn_s, mx_s):
        gu = _gelu(u_ref[...])
        _, _, vn = _layer_norm_fwd(_gelu(v_ref[...]), g_ref[...], b_ref[...])
        vn_s[...] = vn.astype(BF16)
        tri = _tri_mask()
        for gi in range(A_GROUPS):
            wm = jnp.where(tri, w_ref[gi], 0.0).astype(BF16)
            cols = slice(gi * CHUNK, (gi + 1) * CHUNK)
            for n in range(n_chunk):
                rows = slice(n * CHUNK, (n + 1) * CHUNK)
                mx_s[rows, cols] = _dot(wm, vn_s[rows, cols]) + bc_ref[gi]
        y_ref[...] = _sigmoid(ga_ref[...]) * gu * mx_s[...]

    col = lambda c: pl.BlockSpec((tm, D_MODEL), lambda i: (i, c))
    vec = pl.BlockSpec((1, D_MODEL), lambda i: (0, 0))
    return pl.pallas_call(
        body, grid=(T // tm,),
        in_specs=[col(0), col(1), col(2), vec, vec,
                  pl.BlockSpec((A_GROUPS, CHUNK, CHUNK), lambda i: (0, 0, 0)),
                  pl.BlockSpec((A_GROUPS, CHUNK, 1), lambda i: (0, 0, 0))],
        out_specs=pl.BlockSpec((tm, D_MODEL), lambda i: (i, 0)),
        out_shape=jax.ShapeDtypeStruct((T, D_MODEL), F32),
        scratch_shapes=[pltpu.VMEM((tm, D_MODEL), BF16), pltpu.VMEM((tm, D_MODEL), F32)],
        name="mixer_a_fwd", compiler_params=_params("parallel"))(zm, zm, zm, av_g, av_b, w_s, b_col)


def _mixer_bwd(zm, o, dm, av_g, av_b, w_s, w_st, b_col, dep):
    T = zm.shape[0]
    tm = _tile(T, 256)
    n_chunk = tm // CHUNK

    def body(u_ref, v_ref, ga_ref, gb_ref, o_ref, dm_ref, g_ref, b_ref, w_ref, wt_ref, bc_ref, dep_ref,
             dz_ref, do_ref, dg_ref, db_ref, dw_ref, dbs_ref, vn_s, mx_s, dmx_s, dvn_s):
        @pl.when(pl.program_id(0) == 0)
        def _():
            dg_ref[...] = jnp.zeros_like(dg_ref)
            db_ref[...] = jnp.zeros_like(db_ref)
            dw_ref[...] = jnp.zeros_like(dw_ref)
            dbs_ref[...] = jnp.zeros_like(dbs_ref)

        dm_v = dm_ref[...]
        gb = gb_ref[...]
        sb = _sigmoid(gb)
        o_v = o_ref[...]
        do_ref[...] = (dm_v * sb).astype(BF16)
        dz_ref[:, 3 * D_MODEL:4 * D_MODEL] = (dm_v * o_v * sb * (1.0 - sb)).astype(BF16)
        u = u_ref[...]
        v = v_ref[...]
        gu = _gelu(u)
        xh, rs, vn = _layer_norm_fwd(_gelu(v), g_ref[...], b_ref[...])
        vn_s[...] = vn.astype(BF16)
        tri = _tri_mask()
        for gi in range(A_GROUPS):
            wm = jnp.where(tri, w_ref[gi], 0.0).astype(BF16)
            cols = slice(gi * CHUNK, (gi + 1) * CHUNK)
            for n in range(n_chunk):
                rows = slice(n * CHUNK, (n + 1) * CHUNK)
                mx_s[rows, cols] = _dot(wm, vn_s[rows, cols]) + bc_ref[gi]
        mixed = mx_s[...]
        sa = _sigmoid(ga_ref[...])
        dya = dm_v * sa
        dz_ref[:, 2 * D_MODEL:3 * D_MODEL] = (dm_v * gu * mixed * sa * (1.0 - sa)).astype(BF16)
        dz_ref[:, 0:D_MODEL] = (dya * mixed * _gelu_grad(u)).astype(BF16)
        dmx = dya * gu
        dmx_s[...] = dmx.astype(BF16)
        tri_t = _tri_mask(transposed=True)
        for gi in range(A_GROUPS):
            wmt = jnp.where(tri_t, wt_ref[gi], 0.0).astype(BF16)
            cols = slice(gi * CHUNK, (gi + 1) * CHUNK)
            dw_acc = jnp.zeros((CHUNK, CHUNK), F32)
            dmx_sum = jnp.zeros((CHUNK, CHUNK), F32)
            for n in range(n_chunk):
                rows = slice(n * CHUNK, (n + 1) * CHUNK)
                blk = dmx_s[rows, cols]
                dvn_s[rows, cols] = _dot(wmt, blk)
                dw_acc = dw_acc + _dot_nt(blk, vn_s[rows, cols])
                dmx_sum = dmx_sum + dmx[rows, cols]
            dw_ref[gi] += jnp.where(tri, dw_acc, 0.0)
            dbs_ref[gi] += jnp.sum(dmx_sum, axis=-1, keepdims=True)
        dvn = dvn_s[...]
        dg_ref[...] += jnp.sum(dvn * xh, axis=0, keepdims=True)
        db_ref[...] += jnp.sum(dvn, axis=0, keepdims=True)
        dxh = dvn * g_ref[...]
        dgv = rs * (dxh - jnp.mean(dxh, axis=-1, keepdims=True)
                    - xh * jnp.mean(dxh * xh, axis=-1, keepdims=True))
        dz_ref[:, D_MODEL:2 * D_MODEL] = (dgv * _gelu_grad(v)).astype(BF16)

    col = lambda c: pl.BlockSpec((tm, D_MODEL), lambda i: (i, c))
    row = pl.BlockSpec((tm, D_MODEL), lambda i: (i, 0))
    vec = pl.BlockSpec((1, D_MODEL), lambda i: (0, 0))
    wsp = pl.BlockSpec((A_GROUPS, CHUNK, CHUNK), lambda i: (0, 0, 0))
    bsp = pl.BlockSpec((A_GROUPS, CHUNK, 1), lambda i: (0, 0, 0))
    return pl.pallas_call(
        body, grid=(T // tm,),
        in_specs=[col(0), col(1), col(2), col(3), row, row, vec, vec, wsp, wsp, bsp, ANY],
        out_specs=[pl.BlockSpec((tm, 4 * D_MODEL), lambda i: (i, 0)), row, vec, vec, wsp, bsp],
        out_shape=[jax.ShapeDtypeStruct((T, 4 * D_MODEL), BF16), jax.ShapeDtypeStruct((T, D_MODEL), BF16),
                   jax.ShapeDtypeStruct((1, D_MODEL), F32), jax.ShapeDtypeStruct((1, D_MODEL), F32),
                   jax.ShapeDtypeStruct((A_GROUPS, CHUNK, CHUNK), F32),
                   jax.ShapeDtypeStruct((A_GROUPS, CHUNK, 1), F32)],
        scratch_shapes=[pltpu.VMEM((tm, D_MODEL), BF16), pltpu.VMEM((tm, D_MODEL), F32),
                        pltpu.VMEM((tm, D_MODEL), BF16), pltpu.VMEM((tm, D_MODEL), F32)],
        name="mixer_bwd", compiler_params=_params("arbitrary"))(
            zm, zm, zm, zm, o, dm, av_g, av_b, w_s, w_st, b_col, dep)


def _rope_tables(pos_ref, invf_ref):
    ang = pos_ref[...].astype(F32) * invf_ref[...]
    lane = lax.broadcasted_iota(jnp.int32, ang.shape, 1)
    cos, sin = jnp.cos(ang), jnp.sin(ang)
    c = jnp.where(lane < QK_ROPE, cos, 0.0)
    sa = jnp.where(lane < QK_ROPE // 2, -sin, 0.0)
    sb = jnp.where((lane >= QK_ROPE // 2) & (lane < QK_ROPE), sin, 0.0)
    return c, sa, sb


def _rope(blk, tabs):
    c, sa, sb = tabs
    return blk * c + pltpu.roll(blk, LANES - QK_ROPE // 2, 1) * sa + pltpu.roll(blk, QK_ROPE // 2, 1) * sb


def _rope_t(dout, tabs):
    c, sa, sb = tabs
    return dout * c + pltpu.roll(dout * sa, QK_ROPE // 2, 1) + pltpu.roll(dout * sb, LANES - QK_ROPE // 2, 1)


def _rms_small(x, g):
    r = lax.rsqrt(jnp.mean(x * x, axis=-1, keepdims=True) + EPS)
    xh = x * r
    return xh, r, xh * g


def _mla_prep_fwd(zs, pos, invf, qg, kvg, wuq_p, wukv):
    T = zs.shape[0]
    tm = _tile(T, 512)
    HW = MLA_HEADS * HEAD_PAD

    def body(zs_ref, pos_ref, invf_ref, qg_ref, kvg_ref, wq_ref, wkv_ref, q_ref, k_ref, v_ref):
        tabs = _rope_tables(pos_ref, invf_ref)
        _, _, cqn = _rms_small(zs_ref[:, 0:Q_LORA], qg_ref[...])
        _, _, ckvn = _rms_small(zs_ref[:, Q_LORA:Q_LORA + KV_LORA], kvg_ref[...])
        q = _dot_nt(cqn.astype(BF16), wq_ref[...])
        kv = _dot(ckvn.astype(BF16), wkv_ref[...])
        kr = _rope(zs_ref[:, Q_LORA + KV_LORA:ZS_W], tabs).astype(BF16)
        for h in range(MLA_HEADS):
            b0 = h * HEAD_PAD
            q_ref[:, b0:b0 + QK_NOPE] = q[:, b0:b0 + QK_NOPE].astype(BF16)
            q_ref[:, b0 + QK_NOPE:b0 + HEAD_PAD] = _rope(q[:, b0 + QK_NOPE:b0 + HEAD_PAD], tabs).astype(BF16)
            k_ref[:, b0:b0 + QK_NOPE] = kv[:, b0:b0 + QK_NOPE].astype(BF16)
            k_ref[:, b0 + QK_NOPE:b0 + HEAD_PAD] = kr
            v_ref[:, h * V_HEAD:(h + 1) * V_HEAD] = kv[:, b0 + QK_NOPE:b0 + HEAD_PAD].astype(BF16)

    full = lambda a: pl.BlockSpec(a.shape, lambda i: (0,) * a.ndim)
    return pl.pallas_call(
        body, grid=(T // tm,),
        in_specs=[pl.BlockSpec((tm, ZS_W), lambda i: (i, 0)), pl.BlockSpec((tm, 1), lambda i: (i, 0)),
                  full(invf), full(qg), full(kvg), full(wuq_p), full(wukv)],
        out_specs=[pl.BlockSpec((tm, HW), lambda i: (i, 0)), pl.BlockSpec((tm, HW), lambda i: (i, 0)),
                   pl.BlockSpec((tm, D_MODEL), lambda i: (i, 0))],
        out_shape=[jax.ShapeDtypeStruct((T, HW), BF16), jax.ShapeDtypeStruct((T, HW), BF16),
                   jax.ShapeDtypeStruct((T, D_MODEL), BF16)],
        name="mla_prep_fwd", compiler_params=_params("parallel"))(zs, pos, invf, qg, kvg, wuq_p, wukv)


def _mla_prep_bwd(zs, pos, invf, qg, kvg, wuq_p, wukv, dq, dk, dv):
    T = zs.shape[0]
    tm = _tile(T, 256)
    HW = MLA_HEADS * HEAD_PAD

    def body(zs_ref, pos_ref, invf_ref, qg_ref, kvg_ref, wq_ref, wkv_ref, dq_ref, dk_ref, dv_ref,
             dzs_ref, cqn_ref, dqp_ref, ckvn_ref, dkv_ref, dqg_ref, dkvg_ref):
        @pl.when(pl.program_id(0) == 0)
        def _():
            dqg_ref[...] = jnp.zeros_like(dqg_ref)
            dkvg_ref[...] = jnp.zeros_like(dkvg_ref)

        tabs = _rope_tables(pos_ref, invf_ref)
        cqh, rq, cqn = _rms_small(zs_ref[:, 0:Q_LORA], qg_ref[...])
        ckvh, rkv, ckvn = _rms_small(zs_ref[:, Q_LORA:Q_LORA + KV_LORA], kvg_ref[...])
        cqn_ref[...] = cqn.astype(BF16)
        ckvn_ref[...] = ckvn.astype(BF16)
        dkr = jnp.zeros((tm, LANES), F32)
        for h in range(MLA_HEADS):
            b0 = h * HEAD_PAD
            dqp_ref[:, b0:b0 + QK_NOPE] = dq_ref[:, b0:b0 + QK_NOPE].astype(BF16)
            dqp_ref[:, b0 + QK_NOPE:b0 + HEAD_PAD] = _rope_t(dq_ref[:, b0 + QK_NOPE:b0 + HEAD_PAD], tabs).astype(BF16)
            dkv_ref[:, b0:b0 + QK_NOPE] = dk_ref[:, b0:b0 + QK_NOPE].astype(BF16)
            dkv_ref[:, b0 + QK_NOPE:b0 + HEAD_PAD] = dv_ref[:, h * V_HEAD:(h + 1) * V_HEAD].astype(BF16)
            dkr = dkr + dk_ref[:, b0 + QK_NOPE:b0 + HEAD_PAD]
        dcqn = _dot(dqp_ref[...], wq_ref[...])
        dckvn = _dot_nt(dkv_ref[...], wkv_ref[...])
        dqg_ref[...] += jnp.sum(dcqn * cqh, axis=0, keepdims=True)
        dkvg_ref[...] += jnp.sum(dckvn * ckvh, axis=0, keepdims=True)
        dxh = dcqn * qg_ref[...]
        dzs_ref[:, 0:Q_LORA] = (rq * (dxh - cqh * jnp.mean(dxh * cqh, axis=-1, keepdims=True))).astype(BF16)
        dxh = dckvn * kvg_ref[...]
        dzs_ref[:, Q_LORA:Q_LORA + KV_LORA] = (
            rkv * (dxh - ckvh * jnp.mean(dxh * ckvh, axis=-1, keepdims=True))).astype(BF16)
        dzs_ref[:, Q_LORA + KV_LORA:ZS_W] = _rope_t(dkr, tabs).astype(BF16)

    full = lambda a: pl.BlockSpec(a.shape, lambda i: (0,) * a.ndim)
    rowb = lambda w: pl.BlockSpec((tm, w), lambda i: (i, 0))
    return pl.pallas_call(
        body, grid=(T // tm,),
        in_specs=[rowb(ZS_W), rowb(1), full(invf), full(qg), full(kvg), full(wuq_p), full(wukv),
                  rowb(HW), rowb(HW), rowb(D_MODEL)],
        out_specs=[rowb(ZS_W), rowb(Q_LORA), rowb(HW), rowb(KV_LORA), rowb(HW), full(qg), full(kvg)],
        out_shape=[jax.ShapeDtypeStruct((T, ZS_W), BF16), jax.ShapeDtypeStruct((T, Q_LORA), BF16),
                   jax.ShapeDtypeStruct((T, HW), BF16), jax.ShapeDtypeStruct((T, KV_LORA), BF16),
                   jax.ShapeDtypeStruct((T, HW), BF16), jax.ShapeDtypeStruct(qg.shape, F32),
                   jax.ShapeDtypeStruct(kvg.shape, F32)],
        name="mla_prep_bwd", compiler_params=_params("arbitrary"))(
            zs, pos, invf, qg, kvg, wuq_p, wukv, dq, dk, dv)


def _causal(tq, kmax, q0):
    r = lax.broadcasted_iota(jnp.int32, (tq, kmax), 0) + q0
    c = lax.broadcasted_iota(jnp.int32, (tq, kmax), 1)
    return c <= r


def _attn_fwd(q, k, v, batch, seq):
    tq = _tile(seq, ATTN_TILE)
    nq = seq // tq

    def body(q_ref, k_ref, v_ref, o_ref, lse_ref):
        diag = _causal(tq, tq, 0)
        for qi in range(nq):
            rows = slice(qi * tq, (qi + 1) * tq)
            qr = q_ref[rows, :]
            s_d = jnp.where(diag, _dot_nt(qr, k_ref[rows, :]) * ATTN_SCALE, NEG_BIG)
            m = jnp.max(s_d, axis=-1, keepdims=True)
            if qi > 0:
                before = slice(0, qi * tq)
                s_b = _dot_nt(qr, k_ref[before, :]) * ATTN_SCALE
                m = jnp.maximum(m, jnp.max(s_b, axis=-1, keepdims=True))
                p_b = jnp.exp(s_b - m)
                l = jnp.sum(p_b, axis=-1, keepdims=True)
                acc = _dot(p_b.astype(BF16), v_ref[before, :])
            p_d = jnp.exp(s_d - m)
            l_d = jnp.sum(p_d, axis=-1, keepdims=True)
            acc_d = _dot(p_d.astype(BF16), v_ref[rows, :])
            l, acc = (l + l_d, acc + acc_d) if qi > 0 else (l_d, acc_d)
            o_ref[rows, :] = acc / l
            lse_ref[rows, :] = jnp.broadcast_to(m + jnp.log(l), (tq, V_HEAD))

    return pl.pallas_call(
        body, grid=(batch, MLA_HEADS),
        in_specs=[pl.BlockSpec((seq, HEAD_PAD), lambda b, h: (b, h)),
                  pl.BlockSpec((seq, HEAD_PAD), lambda b, h: (b, h)),
                  pl.BlockSpec((seq, V_HEAD), lambda b, h: (b, h))],
        out_specs=[pl.BlockSpec((seq, V_HEAD), lambda b, h: (b, h)),
                   pl.BlockSpec((seq, V_HEAD), lambda b, h: (b, h))],
        out_shape=[jax.ShapeDtypeStruct((batch * seq, D_MODEL), F32),
                   jax.ShapeDtypeStruct((batch * seq, D_MODEL), F32)],
        name="attn_fwd", compiler_params=_params("parallel", "parallel"))(q, k, v)


def _attn_bwd(q, k, v, o, do, lse, batch, seq, dep):
    tq = _tile(seq, ATTN_TILE)
    nq = seq // tq

    def body(q_ref, k_ref, v_ref, o_ref, do_ref, lse_ref, dep_ref, dq_ref, dk_ref, dv_ref):
        dk_ref[...] = jnp.zeros_like(dk_ref)
        dv_ref[...] = jnp.zeros_like(dv_ref)
        for qi in range(nq):
            rows = slice(qi * tq, (qi + 1) * tq)
            kmax = (qi + 1) * tq
            qr = q_ref[rows, :]
            dor = do_ref[rows, :]
            kk = k_ref[0:kmax, :]
            s = _dot_nt(qr, kk) * ATTN_SCALE
            p = jnp.where(_causal(tq, kmax, qi * tq), jnp.exp(s - lse_ref[rows, 0:1]), 0.0)
            dp = _dot_nt(dor, v_ref[0:kmax, :])
            delta = jnp.sum(dor.astype(F32) * o_ref[rows, :], axis=-1, keepdims=True)
            ds = (p * (dp - delta) * ATTN_SCALE).astype(BF16)
            dq_ref[rows, :] = _dot(ds, kk)
            dk_ref[0:kmax, :] += _dot_tn(ds, qr)
            dv_ref[0:kmax, :] += _dot_tn(p.astype(BF16), dor)

    qspec = pl.BlockSpec((seq, HEAD_PAD), lambda b, h: (b, h))
    vspec = pl.BlockSpec((seq, V_HEAD), lambda b, h: (b, h))
    T = batch * seq
    return pl.pallas_call(
        body, grid=(batch, MLA_HEADS),
        in_specs=[qspec, qspec, vspec, vspec, vspec, vspec, ANY],
        out_specs=[qspec, qspec, vspec],
        out_shape=[jax.ShapeDtypeStruct((T, MLA_HEADS * HEAD_PAD), F32),
                   jax.ShapeDtypeStruct((T, MLA_HEADS * HEAD_PAD), F32),
                   jax.ShapeDtypeStruct((T, D_MODEL), F32)],
        name="attn_bwd", compiler_params=_params("parallel", "parallel"))(q, k, v, o, do, lse, dep)


def _merge_out(x, yag, zm, o, w_out, ffn_g):
    T = x.shape[0]
    tm = _tile(T, 512)

    def body(x_ref, ya_ref, gb_ref, o_ref, w_ref, g_ref, mg_ref, x1_ref, h2_ref):
        mg = (ya_ref[...] + _sigmoid(gb_ref[...]) * o_ref[...]).astype(BF16)
        mg_ref[...] = mg
        x1 = x_ref[...] + _dot(mg, w_ref[...])
        x1_ref[...] = x1
        r = lax.rsqrt(jnp.mean(x1 * x1, axis=-1, keepdims=True) + EPS)
        h2_ref[...] = (x1 * r * g_ref[...]).astype(BF16)

    row = pl.BlockSpec((tm, D_MODEL), lambda i: (i, 0))
    return pl.pallas_call(
        body, grid=(T // tm,),
        in_specs=[row, row, pl.BlockSpec((tm, D_MODEL), lambda i: (i, 3)), row,
                  pl.BlockSpec((D_MODEL, D_MODEL), lambda i: (0, 0)), pl.BlockSpec((1, D_MODEL), lambda i: (0, 0))],
        out_specs=[row, row, row],
        out_shape=[jax.ShapeDtypeStruct((T, D_MODEL), BF16), jax.ShapeDtypeStruct((T, D_MODEL), F32),
                   jax.ShapeDtypeStruct((T, D_MODEL), BF16)],
        name="merge_out", compiler_params=_params("parallel"))(x, yag, zm, o, w_out, ffn_g)


FF_TILE = 256
FF_BLOCKS = D_FF // FF_TILE
FFB_TILE = 256


def _shift_down(x, k):
    row = lax.broadcasted_iota(jnp.int32, x.shape, 0)
    return jnp.where(row >= k, pltpu.roll(x, k, 0), 0.0)


def _shift_up(x, k):
    n = x.shape[0]
    row = lax.broadcasted_iota(jnp.int32, x.shape, 0)
    return jnp.where(row < n - k, pltpu.roll(x, n - k, 0), 0.0)


def _conv(x, w_ref, b_ref):
    return b_ref[...] + w_ref[2:3, :] * x + w_ref[1:2, :] * _shift_down(x, 1) + w_ref[0:1, :] * _shift_down(x, 2)


def _up_act(h2, wt_up, cw, cb, batch, seq):
    def body(h_ref, wug_ref, wuv_ref, wg_ref, wv_ref, bg_ref, bv_ref, ug_ref, uv_ref, g_ref, v_ref, a_ref):
        h = h_ref[...]
        ug = _dot_nt(h, wug_ref[...])
        uv = _dot_nt(h, wuv_ref[...])
        ug_ref[...] = ug
        uv_ref[...] = uv
        gate = _conv(ug, wg_ref, bg_ref)
        val = _conv(uv, wv_ref, bv_ref)
        g_ref[...] = gate
        v_ref[...] = val
        a_ref[...] = (gate * _sigmoid(gate) * val).astype(BF16)

    blk = pl.BlockSpec((seq, FF_TILE), lambda b, j: (b, j))
    wup = lambda off: pl.BlockSpec((FF_TILE, D_MODEL), lambda b, j: (j + off, 0))
    wsp = lambda off: pl.BlockSpec((3, FF_TILE), lambda b, j: (0, j + off))
    bsp = lambda off: pl.BlockSpec((1, FF_TILE), lambda b, j: (0, j + off))
    T = batch * seq
    f32 = jax.ShapeDtypeStruct((T, D_FF), F32)
    return pl.pallas_call(
        body, grid=(batch, FF_BLOCKS),
        in_specs=[pl.BlockSpec((seq, D_MODEL), lambda b, j: (b, 0)), wup(0), wup(FF_BLOCKS),
                  wsp(0), wsp(FF_BLOCKS), bsp(0), bsp(FF_BLOCKS)],
        out_specs=[blk] * 5,
        out_shape=[f32, f32, f32, f32, jax.ShapeDtypeStruct((T, D_FF), BF16)],
        name="up_act", compiler_params=_params("parallel", "arbitrary"))(h2, wt_up, wt_up, cw, cw, cb, cb)


def _ffn_act_bwd(upg, upv, gate, val, cw, dx2b, w_down, batch, seq):
    def half(du, x, w_ref, dx_ref, dw_ref):
        j = pl.program_id(1)
        up1, up2 = _shift_up(du, 1), _shift_up(du, 2)
        dx_ref[...] = (w_ref[2:3, :] * du + w_ref[1:2, :] * up1 + w_ref[0:1, :] * up2).astype(BF16)
        dw_ref[j, 2:3, :] += jnp.sum(du * x, axis=0, keepdims=True)
        dw_ref[j, 1:2, :] += jnp.sum(up1 * x, axis=0, keepdims=True)
        dw_ref[j, 0:1, :] += jnp.sum(up2 * x, axis=0, keepdims=True)
        dw_ref[j, 3:4, :] += jnp.sum(du, axis=0, keepdims=True)

    def body(ug_ref, uv_ref, g_ref, v_ref, wg_ref, wv_ref, dx_ref, wd_ref, dg_ref, dv_ref, dwg_ref, dwv_ref):
        @pl.when((pl.program_id(0) == 0) & (pl.program_id(1) == 0))
        def _():
            dwg_ref[...] = jnp.zeros_like(dwg_ref)
            dwv_ref[...] = jnp.zeros_like(dwv_ref)

        gate, val = g_ref[...], v_ref[...]
        sg = _sigmoid(gate)
        dav = _dot_nt(dx_ref[...], wd_ref[...])
        half(dav * val * sg * (1.0 + gate * (1.0 - sg)), ug_ref[...], wg_ref, dg_ref, dwg_ref)
        half(dav * gate * sg, uv_ref[...], wv_ref, dv_ref, dwv_ref)

    nb = D_FF // FFB_TILE
    blk = pl.BlockSpec((seq, FFB_TILE), lambda b, j: (b, j))
    wsp = lambda off: pl.BlockSpec((3, FFB_TILE), lambda b, j: (0, j + off))
    acc = pl.BlockSpec((nb, 4, FFB_TILE), lambda b, j: (0, 0, 0))
    T = batch * seq
    dupg, dupv, dwg, dwv = pl.pallas_call(
        body, grid=(batch, nb),
        in_specs=[blk, blk, blk, blk, wsp(0), wsp(nb),
                  pl.BlockSpec((seq, D_MODEL), lambda b, j: (b, 0)),
                  pl.BlockSpec((FFB_TILE, D_MODEL), lambda b, j: (j, 0))],
        out_specs=[blk, blk, acc, acc],
        out_shape=[jax.ShapeDtypeStruct((T, D_FF), BF16), jax.ShapeDtypeStruct((T, D_FF), BF16),
                   jax.ShapeDtypeStruct((nb, 4, FFB_TILE), F32), jax.ShapeDtypeStruct((nb, 4, FFB_TILE), F32)],
        name="ffn_act_bwd", compiler_params=_params("arbitrary", "arbitrary"))(
            upg, upv, gate, val, cw, cw, dx2b, w_down)
    dwg, dwv = (jnp.transpose(a, (1, 0, 2)).reshape(4, D_FF) for a in (dwg, dwv))
    return dupg, dupv, dwg[:3], dwv[:3], dwg[3:], dwv[3:]


def _down_loss(a, w_down, x1, target, gfin):
    T = x1.shape[0]
    tm = _tile(T, 512)

    def body(a_ref, w_ref, x1_ref, t_ref, g_ref, dx_ref, dxb_ref, loss_ref, dg_ref):
        @pl.when(pl.program_id(0) == 0)
        def _():
            loss_ref[...] = jnp.zeros_like(loss_ref)
            dg_ref[...] = jnp.zeros_like(dg_ref)

        x2 = x1_ref[...] + _dot(a_ref[...], w_ref[...])
        r = lax.rsqrt(jnp.mean(x2 * x2, axis=-1, keepdims=True) + EPS)
        xh = x2 * r
        g = g_ref[...]
        diff = xh * g - t_ref[...]
        loss_ref[...] += 0.5 * jnp.sum(jnp.mean(diff * diff, axis=-1, keepdims=True))
        dy = diff * (1.0 / D_MODEL)
        dg_ref[...] += jnp.sum(dy * xh, axis=0, keepdims=True)
        dxh = dy * g
        dx = r * (dxh - xh * jnp.mean(dxh * xh, axis=-1, keepdims=True))
        dx_ref[...] = dx
        dxb_ref[...] = dx.astype(BF16)

    row = pl.BlockSpec((tm, D_MODEL), lambda i: (i, 0))
    vec = pl.BlockSpec((1, D_MODEL), lambda i: (0, 0))
    return pl.pallas_call(
        body, grid=(T // tm,),
        in_specs=[pl.BlockSpec((tm, D_FF), lambda i: (i, 0)),
                  pl.BlockSpec((D_FF, D_MODEL), lambda i: (0, 0)), row, row, vec],
        out_specs=[row, row, pl.BlockSpec((8, LANES), lambda i: (0, 0)), vec],
        out_shape=[jax.ShapeDtypeStruct((T, D_MODEL), F32), jax.ShapeDtypeStruct((T, D_MODEL), BF16),
                   jax.ShapeDtypeStruct((8, LANES), F32), jax.ShapeDtypeStruct((1, D_MODEL), F32)],
        name="down_loss", compiler_params=_params("arbitrary"))(a, w_down, x1, target, gfin)


def _local_step(x, positions, target, mix_norm, av_g, av_b, w_s, b_s, q_norm, kv_norm, ffn_norm, conv_b,
                final_norm, comm):
    batch, seq, _ = x.shape
    T = batch * seq
    x = x.reshape(T, D_MODEL)
    target = target.reshape(T, D_MODEL)
    pos = positions.reshape(T, 1)
    half = jnp.arange(0, QK_ROPE, 2, dtype=F32) / QK_ROPE
    inv_freq = 1.0 / (ROPE_THETA ** half)
    invf = jnp.concatenate([inv_freq, inv_freq, jnp.zeros((LANES - QK_ROPE,), F32)]).reshape(1, LANES)
    w_st = jnp.swapaxes(w_s, 1, 2)
    b_col = b_s.reshape(A_GROUPS, CHUNK, 1)

    wt_in = comm.in_weights()
    h, zm, zs = _in_proj(x, mix_norm, wt_in)
    yag = _mixer_a_fwd(zm, av_g, av_b, w_s, b_col)
    wuq_p, wukv, w_out = comm.mla_weights(after=yag)
    q, k, v = _mla_prep_fwd(zs, pos, invf, q_norm, kv_norm, wuq_p, wukv)
    o, lse = _attn_fwd(q, k, v, batch, seq)
    merged, x1, h2 = _merge_out(x, yag, zm, o, w_out, ffn_norm)
    wt_up, conv_w, w_down = comm.ffn_weights(after=merged)
    upg, upv, gate, val, act = _up_act(h2, wt_up, conv_w, conv_b, batch, seq)
    dx2, dx2b, loss_acc, d_final = _down_loss(act, w_down, x1, target, final_norm)

    d_wdown = _mm_tn(act, dx2b, "dw_down")
    dupg, dupv, dcwg, dcwv, dcbg, dcbv = _ffn_act_bwd(upg, upv, gate, val, conv_w, dx2b, w_down, batch, seq)
    d_wt_up = _mm_tn(dupv, h2, "dw_up_val", rows=2 * D_FF, row0=D_FF,
                     into=_mm_tn(dupg, h2, "dw_up_gate", rows=2 * D_FF))
    dx1, d_ffn_norm, dmerged = _proj_bwd(
        [dupg, dupv], wt_up, [(0, (0, D_FF), (0, D_FF)), (1, (0, D_FF), (D_FF, 2 * D_FF))],
        x1, ffn_norm, dx2, "up_proj_bwd", w2=w_out)
    d_wout = _mm_tn(merged, dx1, "dw_out")
    token = comm.send_ffn_grads(d_wdown, d_wt_up, jnp.concatenate([dcwg, dcwv], axis=1), d_wout)
    dzm, do, d_avg, d_avb, d_ws, d_bs = _mixer_bwd(zm, o, dmerged, av_g, av_b, w_s, w_st, b_col, token)
    token = comm.send_small_grads([
        d_avg, d_avb, _small_2d(d_ws), d_bs.reshape(A_GROUPS, CHUNK), d_ffn_norm,
        jnp.concatenate([dcbg, dcbv], axis=1), d_final])
    dq, dk, dv = _attn_bwd(q, k, v, o, do, lse, batch, seq, token)
    dzs, cqn, dqp, ckvn, dkv, d_qn, d_kvn = _mla_prep_bwd(zs, pos, invf, q_norm, kv_norm, wuq_p, wukv, dq, dk, dv)
    d_wt_main = _mm_tn(dzm, h, "dw_in_main")
    d_wt_zs = _mm_tn(dzs, h, "dw_in_small")
    token = comm.send_in_grads(d_wt_main, d_wt_zs)
    d_wuq_p = _mm_tn(dqp, cqn, "dw_uq", dep=token)
    d_wukv = _mm_tn(ckvn, dkv, "dw_ukv", dep=token)
    terms = [(0, (i * D_MODEL, (i + 1) * D_MODEL), rows) for i, rows in enumerate(IN_ROWS_MAIN)]
    terms.append((1, (0, ZS_W), IN_ROWS_ZS))
    dx, d_mix_norm = _proj_bwd([dzm, dzs], wt_in, terms, x, mix_norm, dx1, "in_proj_bwd", dep=token)
    token = comm.send_late_grads(d_wuq_p, d_wukv, [d_qn, d_kvn, d_mix_norm, loss_acc])
    return dx.reshape(batch, seq, D_MODEL), token


MESH_ID = pl.DeviceIdType.MESH
EFFECT = pltpu.SideEffectType.DATAFLOW_SIDE_EFFECTING


def _mesh_pos():
    return lax.axis_index("x"), lax.axis_index("y"), lax.axis_index("c")


def _peer(pos, d):
    x, y, c = pos
    px = 1 - x if d & 4 else x
    py = 1 - y if d & 2 else y
    pc = 1 - c if d & 1 else c
    return (px, py, pc), 4 * px + 2 * py + pc


def _copy(src_ref, land_ref, send_sems, recv_sems, a, d, pos, exchange, landing_here):
    peer, pid = _peer(pos, d)
    me = 4 * pos[0] + 2 * pos[1] + pos[2]
    if exchange:
        src, dst = src_ref.at[pid], land_ref.at[d]
    else:
        src, dst = src_ref, land_ref.at[pid if landing_here else me]
    return pltpu.make_async_remote_copy(
        src_ref=src, dst_ref=dst, send_sem=send_sems.at[a * (N_DEV - 1) + d - 1],
        recv_sem=recv_sems.at[a * (N_DEV - 1) + d - 1],
        device_id=peer, device_id_type=MESH_ID)


def _start_copies(groups, modes, name, dep=None):
    sizes = [len(g) for g in groups]
    srcs = [s for g in groups for s in g]
    lands = [lax.empty(s.shape if modes[gi] else (N_DEV,) + s.shape, s.dtype)
             for gi, g in enumerate(groups) for s in g]
    n, ng = len(srcs), len(groups)
    n_in = 2 * n + (dep is not None)

    def body(*refs):
        src_refs, land_refs = refs[:n], refs[n:2 * n]
        sems = refs[n_in:n_in + 3 * ng]
        token = refs[-1]
        pos = _mesh_pos()
        k = 0
        for gi, size in enumerate(sizes):
            for a in range(size):
                _own_copy(src_refs[k], land_refs[k], sems[3 * gi + 2], a, pos, modes[gi]).start()
                for d in range(1, N_DEV):
                    _copy(src_refs[k], land_refs[k], sems[3 * gi], sems[3 * gi + 1], a, d, pos, modes[gi],
                          landing_here=False).start()
                k += 1
        token[...] = jnp.zeros_like(token)

    sem_shapes = []
    for size in sizes:
        remote = pltpu.SemaphoreType.DMA((size * (N_DEV - 1),))
        sem_shapes += [remote, remote, pltpu.SemaphoreType.DMA((size,))]
    out = pl.pallas_call(
        body, name=name,
        out_shape=(*sem_shapes, *[pltpu.HBM(a.shape, a.dtype) for a in srcs + lands],
                   jax.ShapeDtypeStruct((8, LANES), F32)),
        in_specs=[HBM] * (2 * n) + [ANY] * (dep is not None),
        out_specs=(*[SEM] * (3 * ng), *[HBM] * (2 * n), pl.BlockSpec(memory_space=pltpu.VMEM)),
        input_output_aliases={i: 3 * ng + i for i in range(2 * n)},
        compiler_params=pltpu.CompilerParams(has_side_effects=EFFECT),
    )(*[pltpu.with_memory_space_constraint(a, pltpu.HBM) for a in srcs + lands], *([dep] if dep is not None else []))
    thru = out[3 * ng:3 * ng + 2 * n]
    handles, k = [], 0
    for gi, size in enumerate(sizes):
        handles.append((out[3 * gi:3 * gi + 3], thru[k:k + size], thru[n + k:n + k + size]))
        k += size
    return handles, out[-1]


def _own_copy(src_ref, land_ref, local_sems, a, pos, exchange):
    me = 4 * pos[0] + 2 * pos[1] + pos[2]
    src, dst = (src_ref.at[me], land_ref.at[0]) if exchange else (src_ref, land_ref.at[me])
    return pltpu.make_async_copy(src, dst, local_sems.at[a])


def _wait_copies(handle, exchange, after, name):
    sems, srcs, lands = handle
    n = len(srcs)

    def body(*refs):
        src_refs, land_refs = refs[:n], refs[n:2 * n]
        send, recv, local = refs[2 * n:2 * n + 3]
        pos = _mesh_pos()
        for a in range(n):
            _own_copy(src_refs[a], land_refs[a], local, a, pos, exchange).wait()
            for d in range(1, N_DEV):
                cp = _copy(src_refs[a], land_refs[a], send, recv, a, d, pos, exchange, landing_here=True)
                cp.wait_send()
                cp.wait_recv()

    out = pl.pallas_call(
        body, name=name,
        out_shape=tuple(pltpu.HBM(a.shape, a.dtype) for a in (*srcs, *lands)),
        in_specs=[HBM] * (2 * n) + [SEM, SEM, SEM, ANY], out_specs=[HBM] * (2 * n),
        input_output_aliases={i: i for i in range(2 * n)},
        compiler_params=pltpu.CompilerParams(has_side_effects=EFFECT),
    )(*srcs, *lands, *sems, after)
    return out[n:]


def _gather_now(a, name):
    def body(x_ref, out_ref, send_sems, recv_sems, local_sem):
        x, y, c = _mesh_pos()
        me, sibling = (x, y, c), (x, y, 1 - c)
        chips = [(1 - x, y), (x, 1 - y), (1 - x, 1 - y)]

        def slot(p):
            return out_ref.at[4 * p[0] + 2 * p[1] + p[2]]

        def copy(k, block, to, src=None):
            return pltpu.make_async_remote_copy(
                src_ref=slot(block) if src is None else src, dst_ref=slot(block), send_sem=send_sems.at[k],
                recv_sem=recv_sems.at[k], device_id=to, device_id_type=MESH_ID)

        mine = pltpu.make_async_copy(x_ref, slot(me), local_sem)
        mine.start()
        first = [copy(0, me, sibling, src=x_ref)]
        first += [copy(1 + j, me, (*chip, c), src=x_ref) for j, chip in enumerate(chips)]
        for cp in first:
            cp.start()
        passed = [copy(4 + j, (*chip, c), sibling) for j, chip in enumerate(chips)]
        for j, chip in enumerate(chips):
            copy(1 + j, (*chip, c), me).wait_recv()
            passed[j].start()
        copy(0, sibling, me).wait_recv()
        for j, chip in enumerate(chips):
            copy(4 + j, (*chip, 1 - c), me).wait_recv()
        for cp in first + passed:
            cp.wait_send()
        mine.wait()

    return pl.pallas_call(
        body, in_specs=[ANY], out_specs=ANY,
        out_shape=jax.ShapeDtypeStruct((N_DEV,) + a.shape, a.dtype),
        scratch_shapes=[pltpu.SemaphoreType.DMA((N_DEV - 1,)), pltpu.SemaphoreType.DMA((N_DEV - 1,)),
                        pltpu.SemaphoreType.DMA],
        name=name, compiler_params=pltpu.CompilerParams(has_side_effects=True))(a)


def _sum_parts(p_ref):
    g = p_ref[0].astype(F32)
    for k in range(1, N_DEV):
        g = g + p_ref[k].astype(F32)
    return g


def _adamw_update(p_ref, w_ref, m_ref, v_ref, g_ref, d_ref, nm_ref, nv_ref):
    c1 = 1.0 - ADAM_B1 ** ADAM_STEP
    c2 = 1.0 - ADAM_B2 ** ADAM_STEP
    g = _sum_parts(p_ref)
    nm = ADAM_B1 * m_ref[...] + (1.0 - ADAM_B1) * g
    nv = ADAM_B2 * v_ref[...] + (1.0 - ADAM_B2) * (g * g)
    g_ref[...] = g
    nm_ref[...] = nm
    nv_ref[...] = nv
    d_ref[...] = -ADAM_LR * ((nm / c1) / (jnp.sqrt(nv / c2) + ADAM_EPS) + ADAM_WD * w_ref[...])


def _adamw_many(parts, ws, ms, vs, sums, name):
    n, ns = len(ws), len(sums)

    def body(*refs):
        ins, outs = refs[:4 * n + ns], refs[4 * n + ns:]
        for i in range(n):
            _adamw_update(ins[i], ins[n + i], ins[2 * n + i], ins[3 * n + i],
                          outs[i], outs[n + i], outs[2 * n + i], outs[3 * n + i])
        for i in range(ns):
            outs[4 * n + i][...] = _sum_parts(ins[4 * n + i])

    full = lambda a: pl.BlockSpec(a.shape, lambda: (0,) * a.ndim)
    args = [*parts, *ws, *ms, *vs, *sums]
    outs = [jax.ShapeDtypeStruct(w.shape, F32) for _ in range(4) for w in ws]
    outs += [jax.ShapeDtypeStruct(s.shape[1:], F32) for s in sums]
    res = pl.pallas_call(
        body, in_specs=[full(a) for a in args], out_specs=[full(o) for o in outs], out_shape=outs,
        name=name, compiler_params=pltpu.CompilerParams(vmem_limit_bytes=VMEM_LIMIT))(*args)
    return res[:n], res[n:2 * n], res[2 * n:3 * n], res[3 * n:4 * n], res[4 * n:]


def _adamw(parts, w, m, v, name):
    R, C = w.shape
    tr, tc = R, C
    if N_DEV * R * C * parts.dtype.itemsize > SMALL_BLOCK_BYTES:
        tr = next((t for t in range(min(R, 256) // 16 * 16, 15, -16) if R % t == 0), R)
        if tr == R:
            tc = _tile(C, 256)

    def body(p_ref, w_ref, m_ref, v_ref, g_ref, d_ref, nm_ref, nv_ref):
        _adamw_update(p_ref, w_ref, m_ref, v_ref, g_ref, d_ref, nm_ref, nv_ref)

    blk = pl.BlockSpec((tr, tc), lambda i, j: (i, j))
    shp = jax.ShapeDtypeStruct((R, C), F32)
    return pl.pallas_call(
        body, grid=(R // tr, C // tc),
        in_specs=[pl.BlockSpec((N_DEV, tr, tc), lambda i, j: (0, i, j)), blk, blk, blk],
        out_specs=[blk, blk, blk, blk], out_shape=[shp, shp, shp, shp],
        name=name, compiler_params=_params("parallel", "parallel"))(parts, w, m, v)


SPLIT_V = 2 * D_MODEL
SPLIT_KR = SPLIT_V + Q_LORA + KV_LORA + QK_ROPE
IN_DIM = SPLIT_KR + 2 * D_MODEL
IN_ROWS_MAIN = ((0, D_MODEL), (D_MODEL, SPLIT_V), (SPLIT_KR, SPLIT_KR + D_MODEL), (SPLIT_KR + D_MODEL, IN_DIM))
IN_ROWS_ZS = (SPLIT_V, SPLIT_V + ZS_W)

SMALL_EARLY = ("a_v_norm_g", "a_v_norm_b", "a_spatial_w", "a_spatial_b", "ffn_norm", "conv_b", "final_norm")
SMALL_LATE = ("q_a_norm", "kv_a_norm", "mix_norm")


def _small_2d(a):
    return a.reshape(-1, a.shape[-1])


def _cols_from_shards(g):
    return jnp.transpose(g, (1, 0, 2)).reshape(g.shape[1], N_DEV * g.shape[2])


def _shards_from_cols(a):
    R, W = a.shape
    return jnp.transpose(a.reshape(R, N_DEV, W // N_DEV), (1, 0, 2))


class _Comm:
    GATHER_GROUPS = (("w_uq", "w_ukv", "w_out"), ("w_up", "conv_w", "w_down"))
    FFN_GRADS = ("w_down", "w_up", "conv_w", "w_out")
    LATE_GRADS = ("w_uq", "w_ukv")
    TRANSPOSED = ("w_in", "w_up", "w_uq")

    def __init__(self, shards):
        local = {n: a.astype(F32 if n == "conv_w" else BF16) for n, a in shards.items()}
        self.g_in = _gather_now(local["w_in"], "gather_w_in")
        groups = [[local[n] for n in g] for g in self.GATHER_GROUPS]
        (self.h_mla, self.h_ffn), _ = _start_copies(groups, [False] * 2, "gather_start", dep=self.g_in)

    def in_weights(self):
        return self.g_in.reshape(IN_DIM, D_MODEL)

    def mla_weights(self, after):
        g_uq, g_ukv, g_out = _wait_copies(self.h_mla, False, after, "gather_wait_mla")
        wuq_p = jnp.pad(g_uq, ((0, 0), (0, HEAD_PAD - QK_HEAD), (0, 0))).reshape(MLA_HEADS * HEAD_PAD, Q_LORA)
        return wuq_p, _cols_from_shards(g_ukv), g_out.reshape(D_MODEL, D_MODEL)

    def ffn_weights(self, after):
        g_up, g_cw, g_down = _wait_copies(self.h_ffn, False, after, "gather_wait_ffn")
        return g_up.reshape(2 * D_FF, D_MODEL), _cols_from_shards(g_cw), g_down.reshape(D_FF, D_MODEL)

    def send_ffn_grads(self, d_wdown, d_wt_up, d_convw, d_wout):
        group = [d_wdown.reshape(N_DEV, D_FF // N_DEV, D_MODEL), d_wt_up.reshape(N_DEV, 2 * D_FF // N_DEV, D_MODEL),
                 _shards_from_cols(d_convw), d_wout.reshape(N_DEV, D_MODEL // N_DEV, D_MODEL)]
        (self.h_ffn_grads,), token = _start_copies([group], [True], "ffn_grads_start")
        return token

    def send_small_grads(self, grads):
        (self.h_small_early,), token = _start_copies([grads], [False], "small_grads_start")
        return token

    def send_in_grads(self, d_wt_main, d_wt_zs):
        d_in = jnp.concatenate([d_wt_main[:SPLIT_V], d_wt_zs[:SPLIT_KR - SPLIT_V], d_wt_main[SPLIT_V:]], axis=0)
        blocks = d_in.reshape(N_DEV, IN_DIM // N_DEV, D_MODEL)
        (self.h_in_grads,), token = _start_copies([[blocks]], [True], "in_grads_start")
        return token

    def send_late_grads(self, d_wuq_p, d_wukv, small):
        d_uq = d_wuq_p.reshape(MLA_HEADS, HEAD_PAD, Q_LORA)[:, :QK_HEAD, :]
        (self.h_late_grads, self.h_late_small), token = _start_copies(
            [[d_uq, _shards_from_cols(d_wukv)], small], [True, False], "late_grads_start")
        return token


def kernel(x, positions, mix_norm, w_in, a_v_norm_g, a_v_norm_b, a_spatial_w, a_spatial_b, q_a_norm, w_uq, kv_a_norm, w_ukv, w_out, ffn_norm, w_up, conv_w, conv_b, w_down, final_norm, loss_target, m_mix_norm, m_w_in, m_a_v_norm_g, m_a_v_norm_b, m_a_spatial_w, m_a_spatial_b, m_q_a_norm, m_w_uq, m_kv_a_norm, m_w_ukv, m_w_out, m_ffn_norm, m_w_up, m_conv_w, m_conv_b, m_w_down, m_final_norm, v_mix_norm, v_w_in, v_a_v_norm_g, v_a_v_norm_b, v_a_spatial_w, v_a_spatial_b, v_q_a_norm, v_w_uq, v_kv_a_norm, v_w_ukv, v_w_out, v_ffn_norm, v_w_up, v_conv_w, v_conv_b, v_w_down, v_final_norm):
    names = ("mix_norm", "w_in", "a_v_norm_g", "a_v_norm_b", "a_spatial_w", "a_spatial_b", "q_a_norm", "w_uq",
             "kv_a_norm", "w_ukv", "w_out", "ffn_norm", "w_up", "conv_w", "conv_b", "w_down", "final_norm")
    w = dict(zip(names, (mix_norm, w_in, a_v_norm_g, a_v_norm_b, a_spatial_w, a_spatial_b, q_a_norm, w_uq,
                         kv_a_norm, w_ukv, w_out, ffn_norm, w_up, conv_w, conv_b, w_down, final_norm)))
    m = dict(zip(names, (m_mix_norm, m_w_in, m_a_v_norm_g, m_a_v_norm_b, m_a_spatial_w, m_a_spatial_b,
                         m_q_a_norm, m_w_uq, m_kv_a_norm, m_w_ukv, m_w_out, m_ffn_norm, m_w_up, m_conv_w,
                         m_conv_b, m_w_down, m_final_norm)))
    v = dict(zip(names, (v_mix_norm, v_w_in, v_a_v_norm_g, v_a_v_norm_b, v_a_spatial_w, v_a_spatial_b,
                         v_q_a_norm, v_w_uq, v_kv_a_norm, v_w_ukv, v_w_out, v_ffn_norm, v_w_up, v_conv_w,
                         v_conv_b, v_w_down, v_final_norm)))
    shapes = {n: w[n].shape for n in names}
    def view(tree, n):
        a = tree[n].reshape(tree[n].shape[-2:])
        return a.T if n in _Comm.TRANSPOSED else a

    comm = _Comm({n: view(w, n) for n in ("w_in",) + _Comm.GATHER_GROUPS[0] + _Comm.GATHER_GROUPS[1]})

    grad_x, token = _local_step(
        x, positions, loss_target, w["mix_norm"], w["a_v_norm_g"], w["a_v_norm_b"], w["a_spatial_w"][0],
        w["a_spatial_b"][0], w["q_a_norm"], w["kv_a_norm"], w["ffn_norm"], w["conv_b"],
        w["final_norm"].reshape(1, D_MODEL), comm)

    out_g, out_d, out_m, out_v = {}, {}, {}, {}

    def update(n, parts):
        res = _adamw(parts, view(w, n), view(m, n), view(v, n), "adamw_" + n)
        out_g[n], out_d[n], out_m[n], out_v[n] = (
            (t.T if n in _Comm.TRANSPOSED else t).reshape(shapes[n]) for t in res)
        return res[1]

    for n, parts in zip(_Comm.FFN_GRADS, _wait_copies(comm.h_ffn_grads, True, token, "ffn_grads_wait")):
        last = update(n, parts)
    last = update("w_in", _wait_copies(comm.h_in_grads, True, last, "in_grads_wait")[0])
    for n, parts in zip(_Comm.LATE_GRADS, _wait_copies(comm.h_late_grads, True, last, "late_grads_wait")):
        last = update(n, parts)

    def update_small(names, parts, sums, name):
        res = _adamw_many(parts, *[[_small_2d(t[n]) for n in names] for t in (w, m, v)], sums, name)
        for i, n in enumerate(names):
            out_g[n], out_d[n], out_m[n], out_v[n] = (r[i].reshape(shapes[n]) for r in res[:4])
        return res

    early = _wait_copies(comm.h_small_early, False, last, "small_grads_wait")
    res = update_small(SMALL_EARLY, early, [], "adamw_small")
    late = _wait_copies(comm.h_late_small, False, res[1][0], "late_small_wait")
    res = update_small(SMALL_LATE, late[:-1], late[-1:], "adamw_late")
    loss = res[4][0][0, 0]

    return (loss, grad_x, *[out_g[n] for n in names], *[out_d[n] for n in names],
            *[out_m[n] for n in names], *[out_v[n] for n in names])
```

```python
import functools
import math

import jax
import jax.numpy as jnp
from jax import lax
from jax.experimental import pallas as pl
from jax.experimental.pallas import tpu as pltpu

F32 = jnp.float32
BF16 = jnp.bfloat16
KEPT = jnp.bfloat16

N_DEV = 8
D_MODEL = 1024
EPS = 1e-6
A_GROUPS = 8
CHUNK = 128
MLA_HEADS = 8
QK_NOPE = 128
QK_ROPE = 64
QK_HEAD = QK_NOPE + QK_ROPE
HEAD_PAD = 256
V_HEAD = 128
Q_LORA = 256
KV_LORA = 128
ROPE_THETA = 10000.0
D_FF = 2816
ZS_W = 512
ATTN_SCALE = QK_HEAD ** -0.5
ATTN_TILE = 512
NEG_BIG = -1e30

ADAM_LR = 0.001
ADAM_B1 = 0.9
ADAM_B2 = 0.999
ADAM_EPS = 1e-08
ADAM_WD = 0.01
ADAM_STEP = 10

VMEM_LIMIT = 56 * 1024 * 1024
SMALL_BLOCK_BYTES = 5 * 1024 * 1024
LANES = 128

GELU_K = math.sqrt(2.0 / math.pi)
GELU_C = 0.044715

ANY = pl.BlockSpec(memory_space=pl.ANY)
HBM = pl.BlockSpec(memory_space=pltpu.HBM)
SEM = pl.BlockSpec(memory_space=pltpu.SEMAPHORE)


def _tile(n, pref):
    for t in (pref, 512, 256, 128, 64, 32, 16, 8):
        if t <= pref and n % t == 0:
            return t
    return n


def _wide_tile(n, cap=1408):
    return next((t for t in range(min(n, cap) // LANES * LANES, 0, -LANES) if n % t == 0), n)


def _params(*sem):
    return pltpu.CompilerParams(dimension_semantics=sem, vmem_limit_bytes=VMEM_LIMIT)


def _dot(a, b):
    return jnp.dot(a, b, preferred_element_type=F32)


def _dot_nt(a, b):
    return lax.dot_general(a, b, (((1,), (1,)), ((), ())), preferred_element_type=F32)


def _dot_tn(a, b):
    return lax.dot_general(a, b, (((0,), (0,)), ((), ())), preferred_element_type=F32)


def _sigmoid(x):
    return 1.0 / (1.0 + jnp.exp(-x))


def _gelu(x):
    t = jnp.tanh(GELU_K * (x + GELU_C * x * x * x))
    return 0.5 * x * (1.0 + t)


def _gelu_grad(x):
    t = jnp.tanh(GELU_K * (x + GELU_C * x * x * x))
    return 0.5 * (1.0 + t) + 0.5 * x * (1.0 - t * t) * GELU_K * (1.0 + 3.0 * GELU_C * x * x)


def _in_proj(x, g, wt):
    T, Dm = x.shape
    tm = _tile(T, 256)

    def body(x_ref, g_ref, wt_ref, h_ref, zm_ref, zs_ref):
        xf = x_ref[...]
        r = lax.rsqrt(jnp.mean(xf * xf, axis=-1, keepdims=True) + EPS)
        h = (xf * r * g_ref[...]).astype(BF16)
        h_ref[...] = h
        for i, (r0, r1) in enumerate(IN_ROWS_MAIN):
            zm_ref[:, i * D_MODEL:(i + 1) * D_MODEL] = _dot_nt(h, wt_ref[r0:r1, :]).astype(KEPT)
        zs_ref[...] = _dot_nt(h, wt_ref[IN_ROWS_ZS[0]:IN_ROWS_ZS[1], :])

    row = lambda n: pl.BlockSpec((tm, n), lambda i: (i, 0))
    return pl.pallas_call(
        body, grid=(T // tm,),
        in_specs=[row(Dm), pl.BlockSpec((1, Dm), lambda i: (0, 0)), pl.BlockSpec(wt.shape, lambda i: (0, 0))],
        out_specs=[row(Dm), row(4 * D_MODEL), row(ZS_W)],
        out_shape=[jax.ShapeDtypeStruct((T, Dm), BF16), jax.ShapeDtypeStruct((T, 4 * D_MODEL), KEPT),
                   jax.ShapeDtypeStruct((T, ZS_W), F32)],
        name="in_proj", compiler_params=_params("parallel"))(x, g, wt)


def _proj_bwd(acts, wt, terms, x, g, dres, name, w2=None, dep=None):
    T, Dm = x.shape
    tm = _tile(T, 256)
    n_a = len(acts)

    def body(*refs):
        ins, outs = refs[:n_a + 4 + (w2 is not None) + (dep is not None)], refs[-2 - (w2 is not None):]
        wt_ref, x_ref, g_ref, dres_ref = ins[n_a:n_a + 4]
        dx_ref, dg_ref = outs[0], outs[1]

        @pl.when(pl.program_id(0) == 0)
        def _():
            dg_ref[...] = jnp.zeros_like(dg_ref)

        dy = None
        for i, (c0, c1), (r0, r1) in terms:
            t = _dot(ins[i][:, c0:c1], wt_ref[r0:r1, :])
            dy = t if dy is None else dy + t
        xf = x_ref[...]
        r = lax.rsqrt(jnp.mean(xf * xf, axis=-1, keepdims=True) + EPS)
        xh = xf * r
        dg_ref[...] += jnp.sum(dy * xh, axis=0, keepdims=True)
        dxh = dy * g_ref[...]
        dx = dres_ref[...] + r * (dxh - xh * jnp.mean(dxh * xh, axis=-1, keepdims=True))
        dx_ref[...] = dx
        if w2 is not None:
            outs[2][...] = _dot_nt(dx.astype(BF16), ins[n_a + 4][...])

    row = pl.BlockSpec((tm, Dm), lambda i: (i, 0))
    vec = pl.BlockSpec((1, Dm), lambda i: (0, 0))
    in_specs = [pl.BlockSpec((tm, a.shape[1]), lambda i: (i, 0)) for a in acts]
    in_specs += [pl.BlockSpec(wt.shape, lambda i: (0, 0)), row, vec, row]
    args = [*acts, wt, x, g, dres]
    out_specs = [row, vec]
    out_shape = [jax.ShapeDtypeStruct((T, Dm), F32), jax.ShapeDtypeStruct((1, Dm), F32)]
    if w2 is not None:
        in_specs.append(pl.BlockSpec(w2.shape, lambda i: (0, 0)))
        args.append(w2)
        out_specs.append(pl.BlockSpec((tm, w2.shape[0]), lambda i: (i, 0)))
        out_shape.append(jax.ShapeDtypeStruct((T, w2.shape[0]), F32))
    if dep is not None:
        in_specs.append(ANY)
        args.append(dep)
    return pl.pallas_call(
        body, grid=(T // tm,), in_specs=in_specs, out_specs=out_specs, out_shape=out_shape,
        name=name, compiler_params=_params("arbitrary"))(*args)


def _mm_tn(a, b, name, dep=None, rows=None, row0=0, into=None):
    T, M = a.shape
    N = b.shape[1]
    tm, tn, tt = _wide_tile(M), _wide_tile(N), _tile(T, 2048)
    n_t = T // tt
    off = row0 // tm
    extra = ([dep] if dep is not None else []) + ([into] if into is not None else [])

    def body(a_ref, b_ref, *refs):
        o_ref, acc_ref = refs[-2:]
        t = pl.program_id(2)

        @pl.when(t == 0)
        def _():
            acc_ref[...] = jnp.zeros_like(acc_ref)

        acc_ref[...] += _dot_tn(a_ref[...].astype(BF16), b_ref[...].astype(BF16))

        @pl.when(t == n_t - 1)
        def _():
            o_ref[...] = acc_ref[...].astype(BF16)

    return pl.pallas_call(
        body, grid=(M // tm, N // tn, n_t),
        in_specs=[pl.BlockSpec((tt, tm), lambda i, j, t: (t, i)),
                  pl.BlockSpec((tt, tn), lambda i, j, t: (t, j))] + [ANY] * len(extra),
        out_specs=pl.BlockSpec((tm, tn), lambda i, j, t: (i + off, j)),
        out_shape=jax.ShapeDtypeStruct((rows or M, N), BF16),
        scratch_shapes=[pltpu.VMEM((tm, tn), F32)],
        input_output_aliases={} if into is None else {1 + len(extra): 0},
        name=name, compiler_params=_params("parallel", "parallel", "arbitrary"))(a, b, *extra)


def _layer_norm_fwd(gv, g, b):
    mu = jnp.mean(gv, axis=-1, keepdims=True)
    xc = gv - mu
    rs = lax.rsqrt(jnp.mean(xc * xc, axis=-1, keepdims=True) + EPS)
    xh = xc * rs
    return xh, rs, xh * g + b


def _tri_mask(transposed=False):
    r = lax.broadcasted_iota(jnp.int32, (CHUNK, CHUNK), 0)
    c = lax.broadcasted_iota(jnp.int32, (CHUNK, CHUNK), 1)
    return r <= c if transposed else c <= r


def _mixer_a_fwd(zm, av_g, av_b, w_s, b_col):
    T = zm.shape[0]
    tm = _tile(T, 256)
    n_chunk = tm // CHUNK

    def body(u_ref, v_ref, ga_ref, g_ref, b_ref, w_ref, bc_ref, y_ref, vn_s, mx_s):
        gu = _gelu(u_ref[...].astype(F32))
        _, _, vn = _layer_norm_fwd(_gelu(v_ref[...].astype(F32)), g_ref[...], b_ref[...])
        vn_s[...] = vn.astype(BF16)
        tri = _tri_mask()
        for gi in range(A_GROUPS):
            wm = jnp.where(tri, w_ref[gi], 0.0).astype(BF16)
            cols = slice(gi * CHUNK, (gi + 1) * CHUNK)
            for n in range(n_chunk):
                rows = slice(n * CHUNK, (n + 1) * CHUNK)
                mx_s[rows, cols] = _dot(wm, vn_s[rows, cols]) + bc_ref[gi]
        y_ref[...] = _sigmoid(ga_ref[...].astype(F32)) * gu * mx_s[...]

    col = lambda c: pl.BlockSpec((tm, D_MODEL), lambda i: (i, c))
    vec = pl.BlockSpec((1, D_MODEL), lambda i: (0, 0))
    return pl.pallas_call(
        body, grid=(T // tm,),
        in_specs=[col(0), col(1), col(2), vec, vec,
                  pl.BlockSpec((A_GROUPS, CHUNK, CHUNK), lambda i: (0, 0, 0)),
                  pl.BlockSpec((A_GROUPS, CHUNK, 1), lambda i: (0, 0, 0))],
        out_specs=pl.BlockSpec((tm, D_MODEL), lambda i: (i, 0)),
        out_shape=jax.ShapeDtypeStruct((T, D_MODEL), F32),
        scratch_shapes=[pltpu.VMEM((tm, D_MODEL), BF16), pltpu.VMEM((tm, D_MODEL), F32)],
        name="mixer_a_fwd", compiler_params=_params("parallel"))(zm, zm, zm, av_g, av_b, w_s, b_col)


def _mixer_bwd(zm, o, dm, av_g, av_b, w_s, w_st, b_col, dep):
    T = zm.shape[0]
    tm = _tile(T, 256)
    n_chunk = tm // CHUNK

    def body(u_ref, v_ref, ga_ref, gb_ref, o_ref, dm_ref, g_ref, b_ref, w_ref, wt_ref, bc_ref, dep_ref,
             dz_ref, do_ref, dg_ref, db_ref, dw_ref, dbs_ref, vn_s, mx_s, dmx_s, dvn_s):
        @pl.when(pl.program_id(0) == 0)
        def _():
            dg_ref[...] = jnp.zeros_like(dg_ref)
            db_ref[...] = jnp.zeros_like(db_ref)
            dw_ref[...] = jnp.zeros_like(dw_ref)
            dbs_ref[...] = jnp.zeros_like(dbs_ref)

        dm_v = dm_ref[...]
        gb = gb_ref[...].astype(F32)
        sb = _sigmoid(gb)
        o_v = o_ref[...]
        do_ref[...] = (dm_v * sb).astype(BF16)
        dz_ref[:, 3 * D_MODEL:4 * D_MODEL] = (dm_v * o_v * sb * (1.0 - sb)).astype(BF16)
        u = u_ref[...].astype(F32)
        v = v_ref[...].astype(F32)
        gu = _gelu(u)
        xh, rs, vn = _layer_norm_fwd(_gelu(v), g_ref[...], b_ref[...])
        vn_s[...] = vn.astype(BF16)
        tri = _tri_mask()
        for gi in range(A_GROUPS):
            wm = jnp.where(tri, w_ref[gi], 0.0).astype(BF16)
            cols = slice(gi * CHUNK, (gi + 1) * CHUNK)
            for n in range(n_chunk):
                rows = slice(n * CHUNK, (n + 1) * CHUNK)
                mx_s[rows, cols] = _dot(wm, vn_s[rows, cols]) + bc_ref[gi]
        mixed = mx_s[...]
        sa = _sigmoid(ga_ref[...].astype(F32))
        dya = dm_v * sa
        dz_ref[:, 2 * D_MODEL:3 * D_MODEL] = (dm_v * gu * mixed * sa * (1.0 - sa)).astype(BF16)
        dz_ref[:, 0:D_MODEL] = (dya * mixed * _gelu_grad(u)).astype(BF16)
        dmx = dya * gu
        dmx_s[...] = dmx.astype(BF16)
        tri_t = _tri_mask(transposed=True)
        for gi in range(A_GROUPS):
            wmt = jnp.where(tri_t, wt_ref[gi], 0.0).astype(BF16)
            cols = slice(gi * CHUNK, (gi + 1) * CHUNK)
            dw_acc = jnp.zeros((CHUNK, CHUNK), F32)
            dmx_sum = jnp.zeros((CHUNK, CHUNK), F32)
            for n in range(n_chunk):
                rows = slice(n * CHUNK, (n + 1) * CHUNK)
                blk = dmx_s[rows, cols]
                dvn_s[rows, cols] = _dot(wmt, blk)
                dw_acc = dw_acc + _dot_nt(blk, vn_s[rows, cols])
                dmx_sum = dmx_sum + dmx[rows, cols]
            dw_ref[gi] += jnp.where(tri, dw_acc, 0.0)
            dbs_ref[gi] += jnp.sum(dmx_sum, axis=-1, keepdims=True)
        dvn = dvn_s[...]
        dg_ref[...] += jnp.sum(dvn * xh, axis=0, keepdims=True)
        db_ref[...] += jnp.sum(dvn, axis=0, keepdims=True)
        dxh = dvn * g_ref[...]
        dgv = rs * (dxh - jnp.mean(dxh, axis=-1, keepdims=True)
                    - xh * jnp.mean(dxh * xh, axis=-1, keepdims=True))
        dz_ref[:, D_MODEL:2 * D_MODEL] = (dgv * _gelu_grad(v)).astype(BF16)

    col = lambda c: pl.BlockSpec((tm, D_MODEL), lambda i: (i, c))
    row = pl.BlockSpec((tm, D_MODEL), lambda i: (i, 0))
    vec = pl.BlockSpec((1, D_MODEL), lambda i: (0, 0))
    wsp = pl.BlockSpec((A_GROUPS, CHUNK, CHUNK), lambda i: (0, 0, 0))
    bsp = pl.BlockSpec((A_GROUPS, CHUNK, 1), lambda i: (0, 0, 0))
    return pl.pallas_call(
        body, grid=(T // tm,),
        in_specs=[col(0), col(1), col(2), col(3), row, row, vec, vec, wsp, wsp, bsp, ANY],
        out_specs=[pl.BlockSpec((tm, 4 * D_MODEL), lambda i: (i, 0)), row, vec, vec, wsp, bsp],
        out_shape=[jax.ShapeDtypeStruct((T, 4 * D_MODEL), BF16), jax.ShapeDtypeStruct((T, D_MODEL), BF16),
                   jax.ShapeDtypeStruct((1, D_MODEL), F32), jax.ShapeDtypeStruct((1, D_MODEL), F32),
                   jax.ShapeDtypeStruct((A_GROUPS, CHUNK, CHUNK), F32),
                   jax.ShapeDtypeStruct((A_GROUPS, CHUNK, 1), F32)],
        scratch_shapes=[pltpu.VMEM((tm, D_MODEL), BF16), pltpu.VMEM((tm, D_MODEL), F32),
                        pltpu.VMEM((tm, D_MODEL), BF16), pltpu.VMEM((tm, D_MODEL), F32)],
        name="mixer_bwd", compiler_params=_params("arbitrary"))(
            zm, zm, zm, zm, o, dm, av_g, av_b, w_s, w_st, b_col, dep)


def _rope_tables(pos_ref, invf_ref):
    ang = pos_ref[...].astype(F32) * invf_ref[...]
    lane = lax.broadcasted_iota(jnp.int32, ang.shape, 1)
    cos, sin = jnp.cos(ang), jnp.sin(ang)
    c = jnp.where(lane < QK_ROPE, cos, 0.0)
    sa = jnp.where(lane < QK_ROPE // 2, -sin, 0.0)
    sb = jnp.where((lane >= QK_ROPE // 2) & (lane < QK_ROPE), sin, 0.0)
    return c, sa, sb


def _rope(blk, tabs):
    c, sa, sb = tabs
    return blk * c + pltpu.roll(blk, LANES - QK_ROPE // 2, 1) * sa + pltpu.roll(blk, QK_ROPE // 2, 1) * sb


def _rope_t(dout, tabs):
    c, sa, sb = tabs
    return dout * c + pltpu.roll(dout * sa, QK_ROPE // 2, 1) + pltpu.roll(dout * sb, LANES - QK_ROPE // 2, 1)


def _rms_small(x, g):
    r = lax.rsqrt(jnp.mean(x * x, axis=-1, keepdims=True) + EPS)
    xh = x * r
    return xh, r, xh * g


def _mla_prep_fwd(zs, pos, invf, qg, kvg, wuq_p, wukv):
    T = zs.shape[0]
    tm = _tile(T, 512)
    HW = MLA_HEADS * HEAD_PAD

    def body(zs_ref, pos_ref, invf_ref, qg_ref, kvg_ref, wq_ref, wkv_ref, q_ref, k_ref, v_ref):
        tabs = _rope_tables(pos_ref, invf_ref)
        _, _, cqn = _rms_small(zs_ref[:, 0:Q_LORA], qg_ref[...])
        _, _, ckvn = _rms_small(zs_ref[:, Q_LORA:Q_LORA + KV_LORA], kvg_ref[...])
        q = _dot_nt(cqn.astype(BF16), wq_ref[...])
        kv = _dot(ckvn.astype(BF16), wkv_ref[...])
        kr = _rope(zs_ref[:, Q_LORA + KV_LORA:ZS_W], tabs).astype(BF16)
        for h in range(MLA_HEADS):
            b0 = h * HEAD_PAD
            q_ref[:, b0:b0 + QK_NOPE] = q[:, b0:b0 + QK_NOPE].astype(BF16)
            q_ref[:, b0 + QK_NOPE:b0 + HEAD_PAD] = _rope(q[:, b0 + QK_NOPE:b0 + HEAD_PAD], tabs).astype(BF16)
            k_ref[:, b0:b0 + QK_NOPE] = kv[:, b0:b0 + QK_NOPE].astype(BF16)
            k_ref[:, b0 + QK_NOPE:b0 + HEAD_PAD] = kr
            v_ref[:, h * V_HEAD:(h + 1) * V_HEAD] = kv[:, b0 + QK_NOPE:b0 + HEAD_PAD].astype(BF16)

    full = lambda a: pl.BlockSpec(a.shape, lambda i: (0,) * a.ndim)
    return pl.pallas_call(
        body, grid=(T // tm,),
        in_specs=[pl.BlockSpec((tm, ZS_W), lambda i: (i, 0)), pl.BlockSpec((tm, 1), lambda i: (i, 0)),
                  full(invf), full(qg), full(kvg), full(wuq_p), full(wukv)],
        out_specs=[pl.BlockSpec((tm, HW), lambda i: (i, 0)), pl.BlockSpec((tm, HW), lambda i: (i, 0)),
                   pl.BlockSpec((tm, D_MODEL), lambda i: (i, 0))],
        out_shape=[jax.ShapeDtypeStruct((T, HW), BF16), jax.ShapeDtypeStruct((T, HW), BF16),
                   jax.ShapeDtypeStruct((T, D_MODEL), BF16)],
        name="mla_prep_fwd", compiler_params=_params("parallel"))(zs, pos, invf, qg, kvg, wuq_p, wukv)


def _mla_prep_bwd(zs, pos, invf, qg, kvg, wuq_p, wukv, dq, dk, dv):
    T = zs.shape[0]
    tm = _tile(T, 256)
    HW = MLA_HEADS * HEAD_PAD

    def body(zs_ref, pos_ref, invf_ref, qg_ref, kvg_ref, wq_ref, wkv_ref, dq_ref, dk_ref, dv_ref,
             dzs_ref, cqn_ref, dqp_ref, ckvn_ref, dkv_ref, dqg_ref, dkvg_ref):
        @pl.when(pl.program_id(0) == 0)
        def _():
            dqg_ref[...] = jnp.zeros_like(dqg_ref)
            dkvg_ref[...] = jnp.zeros_like(dkvg_ref)

        tabs = _rope_tables(pos_ref, invf_ref)
        cqh, rq, cqn = _rms_small(zs_ref[:, 0:Q_LORA], qg_ref[...])
        ckvh, rkv, ckvn = _rms_small(zs_ref[:, Q_LORA:Q_LORA + KV_LORA], kvg_ref[...])
        cqn_ref[...] = cqn.astype(BF16)
        ckvn_ref[...] = ckvn.astype(BF16)
        dkr = jnp.zeros((tm, LANES), F32)
        for h in range(MLA_HEADS):
            b0 = h * HEAD_PAD
            dqp_ref[:, b0:b0 + QK_NOPE] = dq_ref[:, b0:b0 + QK_NOPE]
            dqp_ref[:, b0 + QK_NOPE:b0 + HEAD_PAD] = _rope_t(
                dq_ref[:, b0 + QK_NOPE:b0 + HEAD_PAD].astype(F32), tabs).astype(BF16)
            dkv_ref[:, b0:b0 + QK_NOPE] = dk_ref[:, b0:b0 + QK_NOPE]
            dkv_ref[:, b0 + QK_NOPE:b0 + HEAD_PAD] = dv_ref[:, h * V_HEAD:(h + 1) * V_HEAD]
            dkr = dkr + dk_ref[:, b0 + QK_NOPE:b0 + HEAD_PAD].astype(F32)
        dcqn = _dot(dqp_ref[...], wq_ref[...])
        dckvn = _dot_nt(dkv_ref[...], wkv_ref[...])
        dqg_ref[...] += jnp.sum(dcqn * cqh, axis=0, keepdims=True)
        dkvg_ref[...] += jnp.sum(dckvn * ckvh, axis=0, keepdims=True)
        dxh = dcqn * qg_ref[...]
        dzs_ref[:, 0:Q_LORA] = (rq * (dxh - cqh * jnp.mean(dxh * cqh, axis=-1, keepdims=True))).astype(BF16)
        dxh = dckvn * kvg_ref[...]
        dzs_ref[:, Q_LORA:Q_LORA + KV_LORA] = (
            rkv * (dxh - ckvh * jnp.mean(dxh * ckvh, axis=-1, keepdims=True))).astype(BF16)
        dzs_ref[:, Q_LORA + KV_LORA:ZS_W] = _rope_t(dkr, tabs).astype(BF16)

    full = lambda a: pl.BlockSpec(a.shape, lambda i: (0,) * a.ndim)
    rowb = lambda w: pl.BlockSpec((tm, w), lambda i: (i, 0))
    return pl.pallas_call(
        body, grid=(T // tm,),
        in_specs=[rowb(ZS_W), rowb(1), full(invf), full(qg), full(kvg), full(wuq_p), full(wukv),
                  rowb(HW), rowb(HW), rowb(D_MODEL)],
        out_specs=[rowb(ZS_W), rowb(Q_LORA), rowb(HW), rowb(KV_LORA), rowb(HW), full(qg), full(kvg)],
        out_shape=[jax.ShapeDtypeStruct((T, ZS_W), BF16), jax.ShapeDtypeStruct((T, Q_LORA), BF16),
                   jax.ShapeDtypeStruct((T, HW), BF16), jax.ShapeDtypeStruct((T, KV_LORA), BF16),
                   jax.ShapeDtypeStruct((T, HW), BF16), jax.ShapeDtypeStruct(qg.shape, F32),
                   jax.ShapeDtypeStruct(kvg.shape, F32)],
        name="mla_prep_bwd", compiler_params=_params("arbitrary"))(
            zs, pos, invf, qg, kvg, wuq_p, wukv, dq, dk, dv)


def _causal(tq, kmax, q0):
    r = lax.broadcasted_iota(jnp.int32, (tq, kmax), 0) + q0
    c = lax.broadcasted_iota(jnp.int32, (tq, kmax), 1)
    return c <= r


def _attn_fwd(q, k, v, batch, seq):
    tq = _tile(seq, ATTN_TILE)
    nq = seq // tq

    def body(q_ref, k_ref, v_ref, o_ref, lse_ref):
        diag = _causal(tq, tq, 0)
        for qi in range(nq):
            rows = slice(qi * tq, (qi + 1) * tq)
            qr = q_ref[rows, :]
            s_d = jnp.where(diag, _dot_nt(qr, k_ref[rows, :]) * ATTN_SCALE, NEG_BIG)
            m = jnp.max(s_d, axis=-1, keepdims=True)
            if qi > 0:
                before = slice(0, qi * tq)
                s_b = _dot_nt(qr, k_ref[before, :]) * ATTN_SCALE
                m = jnp.maximum(m, jnp.max(s_b, axis=-1, keepdims=True))
                p_b = jnp.exp(s_b - m)
                l = jnp.sum(p_b, axis=-1, keepdims=True)
                acc = _dot(p_b.astype(BF16), v_ref[before, :])
            p_d = jnp.exp(s_d - m)
            l_d = jnp.sum(p_d, axis=-1, keepdims=True)
            acc_d = _dot(p_d.astype(BF16), v_ref[rows, :])
            l, acc = (l + l_d, acc + acc_d) if qi > 0 else (l_d, acc_d)
            o_ref[rows, :] = acc / l
            lse_ref[rows, :] = jnp.broadcast_to(m + jnp.log(l), (tq, V_HEAD))

    return pl.pallas_call(
        body, grid=(batch, MLA_HEADS),
        in_specs=[pl.BlockSpec((seq, HEAD_PAD), lambda b, h: (b, h)),
                  pl.BlockSpec((seq, HEAD_PAD), lambda b, h: (b, h)),
                  pl.BlockSpec((seq, V_HEAD), lambda b, h: (b, h))],
        out_specs=[pl.BlockSpec((seq, V_HEAD), lambda b, h: (b, h)),
                   pl.BlockSpec((seq, V_HEAD), lambda b, h: (b, h))],
        out_shape=[jax.ShapeDtypeStruct((batch * seq, D_MODEL), F32),
                   jax.ShapeDtypeStruct((batch * seq, D_MODEL), F32)],
        name="attn_fwd", compiler_params=_params("parallel", "parallel"))(q, k, v)


def _attn_bwd(q, k, v, o, do, lse, batch, seq, dep):
    tq = _tile(seq, ATTN_TILE)
    nq = seq // tq

    def body(q_ref, k_ref, v_ref, o_ref, do_ref, lse_ref, dep_ref, dq_ref, dk_ref, dv_ref, dk_acc, dv_acc):
        dk_acc[...] = jnp.zeros_like(dk_acc)
        dv_acc[...] = jnp.zeros_like(dv_acc)
        for qi in range(nq):
            rows = slice(qi * tq, (qi + 1) * tq)
            kmax = (qi + 1) * tq
            qr = q_ref[rows, :]
            dor = do_ref[rows, :]
            kk = k_ref[0:kmax, :]
            s = _dot_nt(qr, kk) * ATTN_SCALE
            p = jnp.where(_causal(tq, kmax, qi * tq), jnp.exp(s - lse_ref[rows, 0:1]), 0.0)
            dp = _dot_nt(dor, v_ref[0:kmax, :])
            delta = jnp.sum(dor.astype(F32) * o_ref[rows, :], axis=-1, keepdims=True)
            ds = (p * (dp - delta) * ATTN_SCALE).astype(BF16)
            dq_ref[rows, :] = _dot(ds, kk).astype(BF16)
            dk_acc[0:kmax, :] += _dot_tn(ds, qr)
            dv_acc[0:kmax, :] += _dot_tn(p.astype(BF16), dor)
        dk_ref[...] = dk_acc[...].astype(BF16)
        dv_ref[...] = dv_acc[...].astype(BF16)

    qspec = pl.BlockSpec((seq, HEAD_PAD), lambda b, h: (b, h))
    vspec = pl.BlockSpec((seq, V_HEAD), lambda b, h: (b, h))
    T = batch * seq
    return pl.pallas_call(
        body, grid=(batch, MLA_HEADS),
        in_specs=[qspec, qspec, vspec, vspec, vspec, vspec, ANY],
        out_specs=[qspec, qspec, vspec],
        out_shape=[jax.ShapeDtypeStruct((T, MLA_HEADS * HEAD_PAD), BF16),
                   jax.ShapeDtypeStruct((T, MLA_HEADS * HEAD_PAD), BF16),
                   jax.ShapeDtypeStruct((T, D_MODEL), BF16)],
        scratch_shapes=[pltpu.VMEM((seq, HEAD_PAD), F32), pltpu.VMEM((seq, V_HEAD), F32)],
        name="attn_bwd", compiler_params=_params("parallel", "parallel"))(q, k, v, o, do, lse, dep)


def _merge_out(x, yag, zm, o, w_out, ffn_g):
    T = x.shape[0]
    tm = _tile(T, 512)

    def body(x_ref, ya_ref, gb_ref, o_ref, w_ref, g_ref, mg_ref, x1_ref, h2_ref):
        mg = (ya_ref[...] + _sigmoid(gb_ref[...].astype(F32)) * o_ref[...]).astype(BF16)
        mg_ref[...] = mg
        x1 = x_ref[...] + _dot(mg, w_ref[...])
        x1_ref[...] = x1
        r = lax.rsqrt(jnp.mean(x1 * x1, axis=-1, keepdims=True) + EPS)
        h2_ref[...] = (x1 * r * g_ref[...]).astype(BF16)

    row = pl.BlockSpec((tm, D_MODEL), lambda i: (i, 0))
    return pl.pallas_call(
        body, grid=(T // tm,),
        in_specs=[row, row, pl.BlockSpec((tm, D_MODEL), lambda i: (i, 3)), row,
                  pl.BlockSpec((D_MODEL, D_MODEL), lambda i: (0, 0)), pl.BlockSpec((1, D_MODEL), lambda i: (0, 0))],
        out_specs=[row, row, row],
        out_shape=[jax.ShapeDtypeStruct((T, D_MODEL), BF16), jax.ShapeDtypeStruct((T, D_MODEL), F32),
                   jax.ShapeDtypeStruct((T, D_MODEL), BF16)],
        name="merge_out", compiler_params=_params("parallel"))(x, yag, zm, o, w_out, ffn_g)


FF_TILE = 256
FF_BLOCKS = D_FF // FF_TILE
FFB_TILE = 256


def _shift_down(x, k):
    row = lax.broadcasted_iota(jnp.int32, x.shape, 0)
    return jnp.where(row >= k, pltpu.roll(x, k, 0), 0.0)


def _shift_up(x, k):
    n = x.shape[0]
    row = lax.broadcasted_iota(jnp.int32, x.shape, 0)
    return jnp.where(row < n - k, pltpu.roll(x, n - k, 0), 0.0)


def _conv(x, w_ref, b_ref):
    return b_ref[...] + w_ref[2:3, :] * x + w_ref[1:2, :] * _shift_down(x, 1) + w_ref[0:1, :] * _shift_down(x, 2)


def _up_act(h2, wt_up, cw, cb, batch, seq):
    def body(h_ref, wug_ref, wuv_ref, wg_ref, wv_ref, bg_ref, bv_ref, ug_ref, uv_ref, g_ref, v_ref, a_ref):
        h = h_ref[...]
        ug = _dot_nt(h, wug_ref[...])
        uv = _dot_nt(h, wuv_ref[...])
        ug_ref[...] = ug.astype(KEPT)
        uv_ref[...] = uv.astype(KEPT)
        gate = _conv(ug, wg_ref, bg_ref)
        val = _conv(uv, wv_ref, bv_ref)
        g_ref[...] = gate.astype(KEPT)
        v_ref[...] = val.astype(KEPT)
        a_ref[...] = (gate * _sigmoid(gate) * val).astype(BF16)

    blk = pl.BlockSpec((seq, FF_TILE), lambda b, j: (b, j))
    wup = lambda off: pl.BlockSpec((FF_TILE, D_MODEL), lambda b, j: (j + off, 0))
    wsp = lambda off: pl.BlockSpec((3, FF_TILE), lambda b, j: (0, j + off))
    bsp = lambda off: pl.BlockSpec((1, FF_TILE), lambda b, j: (0, j + off))
    T = batch * seq
    kept = jax.ShapeDtypeStruct((T, D_FF), KEPT)
    return pl.pallas_call(
        body, grid=(batch, FF_BLOCKS),
        in_specs=[pl.BlockSpec((seq, D_MODEL), lambda b, j: (b, 0)), wup(0), wup(FF_BLOCKS),
                  wsp(0), wsp(FF_BLOCKS), bsp(0), bsp(FF_BLOCKS)],
        out_specs=[blk] * 5,
        out_shape=[kept, kept, kept, kept, jax.ShapeDtypeStruct((T, D_FF), BF16)],
        name="up_act", compiler_params=_params("parallel", "arbitrary"))(h2, wt_up, wt_up, cw, cw, cb, cb)


def _ffn_act_bwd(upg, upv, gate, val, cw, dx2b, w_down, batch, seq):
    def half(du, x, w_ref, dx_ref, dw_ref):
        j = pl.program_id(1)
        up1, up2 = _shift_up(du, 1), _shift_up(du, 2)
        dx_ref[...] = (w_ref[2:3, :] * du + w_ref[1:2, :] * up1 + w_ref[0:1, :] * up2).astype(BF16)
        dw_ref[j, 2:3, :] += jnp.sum(du * x, axis=0, keepdims=True)
        dw_ref[j, 1:2, :] += jnp.sum(up1 * x, axis=0, keepdims=True)
        dw_ref[j, 0:1, :] += jnp.sum(up2 * x, axis=0, keepdims=True)
        dw_ref[j, 3:4, :] += jnp.sum(du, axis=0, keepdims=True)

    def body(ug_ref, uv_ref, g_ref, v_ref, wg_ref, wv_ref, dx_ref, wd_ref, dg_ref, dv_ref, dwg_ref, dwv_ref):
        @pl.when((pl.program_id(0) == 0) & (pl.program_id(1) == 0))
        def _():
            dwg_ref[...] = jnp.zeros_like(dwg_ref)
            dwv_ref[...] = jnp.zeros_like(dwv_ref)

        gate, val = g_ref[...].astype(F32), v_ref[...].astype(F32)
        sg = _sigmoid(gate)
        dav = _dot_nt(dx_ref[...], wd_ref[...])
        half(dav * val * sg * (1.0 + gate * (1.0 - sg)), ug_ref[...].astype(F32), wg_ref, dg_ref, dwg_ref)
        half(dav * gate * sg, uv_ref[...].astype(F32), wv_ref, dv_ref, dwv_ref)

    nb = D_FF // FFB_TILE
    blk = pl.BlockSpec((seq, FFB_TILE), lambda b, j: (b, j))
    wsp = lambda off: pl.BlockSpec((3, FFB_TILE), lambda b, j: (0, j + off))
    acc = pl.BlockSpec((nb, 4, FFB_TILE), lambda b, j: (0, 0, 0))
    T = batch * seq
    dupg, dupv, dwg, dwv = pl.pallas_call(
        body, grid=(batch, nb),
        in_specs=[blk, blk, blk, blk, wsp(0), wsp(nb),
                  pl.BlockSpec((seq, D_MODEL), lambda b, j: (b, 0)),
                  pl.BlockSpec((FFB_TILE, D_MODEL), lambda b, j: (j, 0))],
        out_specs=[blk, blk, acc, acc],
        out_shape=[jax.ShapeDtypeStruct((T, D_FF), BF16), jax.ShapeDtypeStruct((T, D_FF), BF16),
                   jax.ShapeDtypeStruct((nb, 4, FFB_TILE), F32), jax.ShapeDtypeStruct((nb, 4, FFB_TILE), F32)],
        name="ffn_act_bwd", compiler_params=_params("arbitrary", "arbitrary"))(
            upg, upv, gate, val, cw, cw, dx2b, w_down)
    dwg, dwv = (jnp.transpose(a, (1, 0, 2)).reshape(4, D_FF) for a in (dwg, dwv))
    return dupg, dupv, dwg[:3], dwv[:3], dwg[3:], dwv[3:]


def _down_loss(a, w_down, x1, target, gfin):
    T = x1.shape[0]
    tm = _tile(T, 512)

    def body(a_ref, w_ref, x1_ref, t_ref, g_ref, dx_ref, dxb_ref, loss_ref, dg_ref):
        @pl.when(pl.program_id(0) == 0)
        def _():
            loss_ref[...] = jnp.zeros_like(loss_ref)
            dg_ref[...] = jnp.zeros_like(dg_ref)

        x2 = x1_ref[...] + _dot(a_ref[...], w_ref[...])
        r = lax.rsqrt(jnp.mean(x2 * x2, axis=-1, keepdims=True) + EPS)
        xh = x2 * r
        g = g_ref[...]
        diff = xh * g - t_ref[...]
        loss_ref[...] += 0.5 * jnp.sum(jnp.mean(diff * diff, axis=-1, keepdims=True))
        dy = diff * (1.0 / D_MODEL)
        dg_ref[...] += jnp.sum(dy * xh, axis=0, keepdims=True)
        dxh = dy * g
        dx = r * (dxh - xh * jnp.mean(dxh * xh, axis=-1, keepdims=True))
        dx_ref[...] = dx
        dxb_ref[...] = dx.astype(BF16)

    row = pl.BlockSpec((tm, D_MODEL), lambda i: (i, 0))
    vec = pl.BlockSpec((1, D_MODEL), lambda i: (0, 0))
    return pl.pallas_call(
        body, grid=(T // tm,),
        in_specs=[pl.BlockSpec((tm, D_FF), lambda i: (i, 0)),
                  pl.BlockSpec((D_FF, D_MODEL), lambda i: (0, 0)), row, row, vec],
        out_specs=[row, row, pl.BlockSpec((8, LANES), lambda i: (0, 0)), vec],
        out_shape=[jax.ShapeDtypeStruct((T, D_MODEL), F32), jax.ShapeDtypeStruct((T, D_MODEL), BF16),
                   jax.ShapeDtypeStruct((8, LANES), F32), jax.ShapeDtypeStruct((1, D_MODEL), F32)],
        name="down_loss", compiler_params=_params("arbitrary"))(a, w_down, x1, target, gfin)


def _local_step(x, positions, target, mix_norm, av_g, av_b, w_s, b_s, q_norm, kv_norm, ffn_norm, conv_b,
                final_norm, comm):
    batch, seq, _ = x.shape
    T = batch * seq
    x = x.reshape(T, D_MODEL)
    target = target.reshape(T, D_MODEL)
    pos = positions.reshape(T, 1)
    half = jnp.arange(0, QK_ROPE, 2, dtype=F32) / QK_ROPE
    inv_freq = 1.0 / (ROPE_THETA ** half)
    invf = jnp.concatenate([inv_freq, inv_freq, jnp.zeros((LANES - QK_ROPE,), F32)]).reshape(1, LANES)
    w_st = jnp.swapaxes(w_s, 1, 2)
    b_col = b_s.reshape(A_GROUPS, CHUNK, 1)

    wt_in = comm.in_weights()
    h, zm, zs = _in_proj(x, mix_norm, wt_in)
    yag = _mixer_a_fwd(zm, av_g, av_b, w_s, b_col)
    wuq_p, wukv, w_out = comm.mla_weights(after=yag)
    q, k, v = _mla_prep_fwd(zs, pos, invf, q_norm, kv_norm, wuq_p, wukv)
    o, lse = _attn_fwd(q, k, v, batch, seq)
    merged, x1, h2 = _merge_out(x, yag, zm, o, w_out, ffn_norm)
    wt_up, conv_w, w_down = comm.ffn_weights(after=merged)
    upg, upv, gate, val, act = _up_act(h2, wt_up, conv_w, conv_b, batch, seq)
    dx2, dx2b, loss_acc, d_final = _down_loss(act, w_down, x1, target, final_norm)

    d_wdown = _mm_tn(act, dx2b, "dw_down")
    dupg, dupv, dcwg, dcwv, dcbg, dcbv = _ffn_act_bwd(upg, upv, gate, val, conv_w, dx2b, w_down, batch, seq)
    d_wt_up = _mm_tn(dupv, h2, "dw_up_val", rows=2 * D_FF, row0=D_FF,
                     into=_mm_tn(dupg, h2, "dw_up_gate", rows=2 * D_FF))
    dx1, d_ffn_norm, dmerged = _proj_bwd(
        [dupg, dupv], wt_up, [(0, (0, D_FF), (0, D_FF)), (1, (0, D_FF), (D_FF, 2 * D_FF))],
        x1, ffn_norm, dx2, "up_proj_bwd", w2=w_out)
    d_wout = _mm_tn(merged, dx1, "dw_out")
    token = comm.send_ffn_grads(d_wdown, d_wt_up, jnp.concatenate([dcwg, dcwv], axis=1), d_wout)
    dzm, do, d_avg, d_avb, d_ws, d_bs = _mixer_bwd(zm, o, dmerged, av_g, av_b, w_s, w_st, b_col, token)
    token = comm.send_small_grads([
        d_avg, d_avb, _small_2d(d_ws), d_bs.reshape(A_GROUPS, CHUNK), d_ffn_norm,
        jnp.concatenate([dcbg, dcbv], axis=1), d_final])
    dq, dk, dv = _attn_bwd(q, k, v, o, do, lse, batch, seq, token)
    dzs, cqn, dqp, ckvn, dkv, d_qn, d_kvn = _mla_prep_bwd(zs, pos, invf, q_norm, kv_norm, wuq_p, wukv, dq, dk, dv)
    d_wt_main = _mm_tn(dzm, h, "dw_in_main")
    d_wt_zs = _mm_tn(dzs, h, "dw_in_small")
    token = comm.send_in_grads(d_wt_main, d_wt_zs)
    d_wuq_p = _mm_tn(dqp, cqn, "dw_uq", dep=token)
    d_wukv = _mm_tn(ckvn, dkv, "dw_ukv", dep=token)
    terms = [(0, (i * D_MODEL, (i + 1) * D_MODEL), rows) for i, rows in enumerate(IN_ROWS_MAIN)]
    terms.append((1, (0, ZS_W), IN_ROWS_ZS))
    dx, d_mix_norm = _proj_bwd([dzm, dzs], wt_in, terms, x, mix_norm, dx1, "in_proj_bwd", dep=token)
    token = comm.send_late_grads(d_wuq_p, d_wukv, [d_qn, d_kvn, d_mix_norm, loss_acc])
    return dx.reshape(batch, seq, D_MODEL), token


MESH_ID = pl.DeviceIdType.MESH
EFFECT = pltpu.SideEffectType.DATAFLOW_SIDE_EFFECTING


def _mesh_pos():
    return lax.axis_index("x"), lax.axis_index("y"), lax.axis_index("c")


def _peer(pos, d):
    x, y, c = pos
    px = 1 - x if d & 4 else x
    py = 1 - y if d & 2 else y
    pc = 1 - c if d & 1 else c
    return (px, py, pc), 4 * px + 2 * py + pc


def _copy(src_ref, land_ref, send_sems, recv_sems, a, d, pos, exchange, landing_here):
    peer, pid = _peer(pos, d)
    me = 4 * pos[0] + 2 * pos[1] + pos[2]
    if exchange:
        src, dst = src_ref.at[pid], land_ref.at[d]
    else:
        src, dst = src_ref, land_ref.at[pid if landing_here else me]
    return pltpu.make_async_remote_copy(
        src_ref=src, dst_ref=dst, send_sem=send_sems.at[a * (N_DEV - 1) + d - 1],
        recv_sem=recv_sems.at[a * (N_DEV - 1) + d - 1],
        device_id=peer, device_id_type=MESH_ID)


def _start_copies(groups, modes, name, dep=None):
    sizes = [len(g) for g in groups]
    srcs = [s for g in groups for s in g]
    lands = [lax.empty(s.shape if modes[gi] else (N_DEV,) + s.shape, s.dtype)
             for gi, g in enumerate(groups) for s in g]
    n, ng = len(srcs), len(groups)
    n_in = 2 * n + (dep is not None)

    def body(*refs):
        src_refs, land_refs = refs[:n], refs[n:2 * n]
        sems = refs[n_in:n_in + 3 * ng]
        token = refs[-1]
        pos = _mesh_pos()
        k = 0
        for gi, size in enumerate(sizes):
            for a in range(size):
                _own_copy(src_refs[k], land_refs[k], sems[3 * gi + 2], a, pos, modes[gi]).start()
                for d in range(1, N_DEV):
                    _copy(src_refs[k], land_refs[k], sems[3 * gi], sems[3 * gi + 1], a, d, pos, modes[gi],
                          landing_here=False).start()
                k += 1
        token[...] = jnp.zeros_like(token)

    sem_shapes = []
    for size in sizes:
        remote = pltpu.SemaphoreType.DMA((size * (N_DEV - 1),))
        sem_shapes += [remote, remote, pltpu.SemaphoreType.DMA((size,))]
    out = pl.pallas_call(
        body, name=name,
        out_shape=(*sem_shapes, *[pltpu.HBM(a.shape, a.dtype) for a in srcs + lands],
                   jax.ShapeDtypeStruct((8, LANES), F32)),
        in_specs=[HBM] * (2 * n) + [ANY] * (dep is not None),
        out_specs=(*[SEM] * (3 * ng), *[HBM] * (2 * n), pl.BlockSpec(memory_space=pltpu.VMEM)),
        input_output_aliases={i: 3 * ng + i for i in range(2 * n)},
        compiler_params=pltpu.CompilerParams(has_side_effects=EFFECT),
    )(*[pltpu.with_memory_space_constraint(a, pltpu.HBM) for a in srcs + lands], *([dep] if dep is not None else []))
    thru = out[3 * ng:3 * ng + 2 * n]
    handles, k = [], 0
    for gi, size in enumerate(sizes):
        handles.append((out[3 * gi:3 * gi + 3], thru[k:k + size], thru[n + k:n + k + size]))
        k += size
    return handles, out[-1]


def _own_copy(src_ref, land_ref, local_sems, a, pos, exchange):
    me = 4 * pos[0] + 2 * pos[1] + pos[2]
    src, dst = (src_ref.at[me], land_ref.at[0]) if exchange else (src_ref, land_ref.at[me])
    return pltpu.make_async_copy(src, dst, local_sems.at[a])


def _wait_copies(handle, exchange, after, name):
    sems, srcs, lands = handle
    n = len(srcs)

    def body(*refs):
        src_refs, land_refs = refs[:n], refs[n:2 * n]
        send, recv, local = refs[2 * n:2 * n + 3]
        pos = _mesh_pos()
        for a in range(n):
            _own_copy(src_refs[a], land_refs[a], local, a, pos, exchange).wait()
            for d in range(1, N_DEV):
                cp = _copy(src_refs[a], land_refs[a], send, recv, a, d, pos, exchange, landing_here=True)
                cp.wait_send()
                cp.wait_recv()

    out = pl.pallas_call(
        body, name=name,
        out_shape=tuple(pltpu.HBM(a.shape, a.dtype) for a in (*srcs, *lands)),
        in_specs=[HBM] * (2 * n) + [SEM, SEM, SEM, ANY], out_specs=[HBM] * (2 * n),
        input_output_aliases={i: i for i in range(2 * n)},
        compiler_params=pltpu.CompilerParams(has_side_effects=EFFECT),
    )(*srcs, *lands, *sems, after)
    return out[n:]


def _gather_now(a, name):
    def body(x_ref, out_ref, send_sems, recv_sems, local_sem):
        x, y, c = _mesh_pos()
        me, sibling = (x, y, c), (x, y, 1 - c)
        chips = [(1 - x, y), (x, 1 - y), (1 - x, 1 - y)]

        def slot(p):
            return out_ref.at[4 * p[0] + 2 * p[1] + p[2]]

        def copy(k, block, to, src=None):
            return pltpu.make_async_remote_copy(
                src_ref=slot(block) if src is None else src, dst_ref=slot(block), send_sem=send_sems.at[k],
                recv_sem=recv_sems.at[k], device_id=to, device_id_type=MESH_ID)

        mine = pltpu.make_async_copy(x_ref, slot(me), local_sem)
        mine.start()
        first = [copy(0, me, sibling, src=x_ref)]
        first += [copy(1 + j, me, (*chip, c), src=x_ref) for j, chip in enumerate(chips)]
        for cp in first:
            cp.start()
        passed = [copy(4 + j, (*chip, c), sibling) for j, chip in enumerate(chips)]
        for j, chip in enumerate(chips):
            copy(1 + j, (*chip, c), me).wait_recv()
            passed[j].start()
        copy(0, sibling, me).wait_recv()
        for j, chip in enumerate(chips):
            copy(4 + j, (*chip, 1 - c), me).wait_recv()
        for cp in first + passed:
            cp.wait_send()
        mine.wait()

    return pl.pallas_call(
        body, in_specs=[ANY], out_specs=ANY,
        out_shape=jax.ShapeDtypeStruct((N_DEV,) + a.shape, a.dtype),
        scratch_shapes=[pltpu.SemaphoreType.DMA((N_DEV - 1,)), pltpu.SemaphoreType.DMA((N_DEV - 1,)),
                        pltpu.SemaphoreType.DMA],
        name=name, compiler_params=pltpu.CompilerParams(has_side_effects=True))(a)


def _sum_parts(p_ref):
    g = p_ref[0].astype(F32)
    for k in range(1, N_DEV):
        g = g + p_ref[k].astype(F32)
    return g


def _adamw_update(p_ref, w_ref, m_ref, v_ref, g_ref, d_ref, nm_ref, nv_ref):
    c1 = 1.0 - ADAM_B1 ** ADAM_STEP
    c2 = 1.0 - ADAM_B2 ** ADAM_STEP
    g = _sum_parts(p_ref)
    nm = ADAM_B1 * m_ref[...] + (1.0 - ADAM_B1) * g
    nv = ADAM_B2 * v_ref[...] + (1.0 - ADAM_B2) * (g * g)
    g_ref[...] = g
    nm_ref[...] = nm
    nv_ref[...] = nv
    d_ref[...] = -ADAM_LR * ((nm / c1) / (jnp.sqrt(nv / c2) + ADAM_EPS) + ADAM_WD * w_ref[...])


def _adamw_many(parts, ws, ms, vs, sums, name):
    n, ns = len(ws), len(sums)

    def body(*refs):
        ins, outs = refs[:4 * n + ns], refs[4 * n + ns:]
        for i in range(n):
            _adamw_update(ins[i], ins[n + i], ins[2 * n + i], ins[3 * n + i],
                          outs[i], outs[n + i], outs[2 * n + i], outs[3 * n + i])
        for i in range(ns):
            outs[4 * n + i][...] = _sum_parts(ins[4 * n + i])

    full = lambda a: pl.BlockSpec(a.shape, lambda: (0,) * a.ndim)
    args = [*parts, *ws, *ms, *vs, *sums]
    outs = [jax.ShapeDtypeStruct(w.shape, F32) for _ in range(4) for w in ws]
    outs += [jax.ShapeDtypeStruct(s.shape[1:], F32) for s in sums]
    res = pl.pallas_call(
        body, in_specs=[full(a) for a in args], out_specs=[full(o) for o in outs], out_shape=outs,
        name=name, compiler_params=pltpu.CompilerParams(vmem_limit_bytes=VMEM_LIMIT))(*args)
    return res[:n], res[n:2 * n], res[2 * n:3 * n], res[3 * n:4 * n], res[4 * n:]


def _adamw(parts, w, m, v, name):
    R, C = w.shape
    tr, tc = R, C
    if N_DEV * R * C * parts.dtype.itemsize > SMALL_BLOCK_BYTES:
        tr = next((t for t in range(min(R, 256) // 16 * 16, 15, -16) if R % t == 0), R)
        if tr == R:
            tc = _tile(C, 256)

    def body(p_ref, w_ref, m_ref, v_ref, g_ref, d_ref, nm_ref, nv_ref):
        _adamw_update(p_ref, w_ref, m_ref, v_ref, g_ref, d_ref, nm_ref, nv_ref)

    blk = pl.BlockSpec((tr, tc), lambda i, j: (i, j))
    shp = jax.ShapeDtypeStruct((R, C), F32)
    return pl.pallas_call(
        body, grid=(R // tr, C // tc),
        in_specs=[pl.BlockSpec((N_DEV, tr, tc), lambda i, j: (0, i, j)), blk, blk, blk],
        out_specs=[blk, blk, blk, blk], out_shape=[shp, shp, shp, shp],
        name=name, compiler_params=_params("parallel", "parallel"))(parts, w, m, v)


SPLIT_V = 2 * D_MODEL
SPLIT_KR = SPLIT_V + Q_LORA + KV_LORA + QK_ROPE
IN_DIM = SPLIT_KR + 2 * D_MODEL
IN_ROWS_MAIN = ((0, D_MODEL), (D_MODEL, SPLIT_V), (SPLIT_KR, SPLIT_KR + D_MODEL), (SPLIT_KR + D_MODEL, IN_DIM))
IN_ROWS_ZS = (SPLIT_V, SPLIT_V + ZS_W)

SMALL_EARLY = ("a_v_norm_g", "a_v_norm_b", "a_spatial_w", "a_spatial_b", "ffn_norm", "conv_b", "final_norm")
SMALL_LATE = ("q_a_norm", "kv_a_norm", "mix_norm")


def _small_2d(a):
    return a.reshape(-1, a.shape[-1])


def _cols_from_shards(g):
    return jnp.transpose(g, (1, 0, 2)).reshape(g.shape[1], N_DEV * g.shape[2])


def _shards_from_cols(a):
    R, W = a.shape
    return jnp.transpose(a.reshape(R, N_DEV, W // N_DEV), (1, 0, 2))


class _Comm:
    GATHER_GROUPS = (("w_uq", "w_ukv", "w_out"), ("w_up", "conv_w", "w_down"))
    FFN_GRADS = ("w_down", "w_up", "conv_w", "w_out")
    LATE_GRADS = ("w_uq", "w_ukv")
    TRANSPOSED = ("w_in", "w_up", "w_uq")

    def __init__(self, shards):
        local = {n: a.astype(F32 if n == "conv_w" else BF16) for n, a in shards.items()}
        self.g_in = _gather_now(local["w_in"], "gather_w_in")
        groups = [[local[n] for n in g] for g in self.GATHER_GROUPS]
        (self.h_mla, self.h_ffn), _ = _start_copies(groups, [False] * 2, "gather_start", dep=self.g_in)

    def in_weights(self):
        return self.g_in.reshape(IN_DIM, D_MODEL)

    def mla_weights(self, after):
        g_uq, g_ukv, g_out = _wait_copies(self.h_mla, False, after, "gather_wait_mla")
        wuq_p = jnp.pad(g_uq, ((0, 0), (0, HEAD_PAD - QK_HEAD), (0, 0))).reshape(MLA_HEADS * HEAD_PAD, Q_LORA)
        return wuq_p, _cols_from_shards(g_ukv), g_out.reshape(D_MODEL, D_MODEL)

    def ffn_weights(self, after):
        g_up, g_cw, g_down = _wait_copies(self.h_ffn, False, after, "gather_wait_ffn")
        return g_up.reshape(2 * D_FF, D_MODEL), _cols_from_shards(g_cw), g_down.reshape(D_FF, D_MODEL)

    def send_ffn_grads(self, d_wdown, d_wt_up, d_convw, d_wout):
        group = [d_wdown.reshape(N_DEV, D_FF // N_DEV, D_MODEL), d_wt_up.reshape(N_DEV, 2 * D_FF // N_DEV, D_MODEL),
                 _shards_from_cols(d_convw), d_wout.reshape(N_DEV, D_MODEL // N_DEV, D_MODEL)]
        (self.h_ffn_grads,), token = _start_copies([group], [True], "ffn_grads_start")
        return token

    def send_small_grads(self, grads):
        (self.h_small_early,), token = _start_copies([grads], [False], "small_grads_start")
        return token

    def send_in_grads(self, d_wt_main, d_wt_zs):
        d_in = jnp.concatenate([d_wt_main[:SPLIT_V], d_wt_zs[:SPLIT_KR - SPLIT_V], d_wt_main[SPLIT_V:]], axis=0)
        blocks = d_in.reshape(N_DEV, IN_DIM // N_DEV, D_MODEL)
        (self.h_in_grads,), token = _start_copies([[blocks]], [True], "in_grads_start")
        return token

    def send_late_grads(self, d_wuq_p, d_wukv, small):
        d_uq = d_wuq_p.reshape(MLA_HEADS, HEAD_PAD, Q_LORA)[:, :QK_HEAD, :]
        (self.h_late_grads, self.h_late_small), token = _start_copies(
            [[d_uq, _shards_from_cols(d_wukv)], small], [True, False], "late_grads_start")
        return token


def kernel(x, positions, mix_norm, w_in, a_v_norm_g, a_v_norm_b, a_spatial_w, a_spatial_b, q_a_norm, w_uq, kv_a_norm, w_ukv, w_out, ffn_norm, w_up, conv_w, conv_b, w_down, final_norm, loss_target, m_mix_norm, m_w_in, m_a_v_norm_g, m_a_v_norm_b, m_a_spatial_w, m_a_spatial_b, m_q_a_norm, m_w_uq, m_kv_a_norm, m_w_ukv, m_w_out, m_ffn_norm, m_w_up, m_conv_w, m_conv_b, m_w_down, m_final_norm, v_mix_norm, v_w_in, v_a_v_norm_g, v_a_v_norm_b, v_a_spatial_w, v_a_spatial_b, v_q_a_norm, v_w_uq, v_kv_a_norm, v_w_ukv, v_w_out, v_ffn_norm, v_w_up, v_conv_w, v_conv_b, v_w_down, v_final_norm):
    names = ("mix_norm", "w_in", "a_v_norm_g", "a_v_norm_b", "a_spatial_w", "a_spatial_b", "q_a_norm", "w_uq",
             "kv_a_norm", "w_ukv", "w_out", "ffn_norm", "w_up", "conv_w", "conv_b", "w_down", "final_norm")
    w = dict(zip(names, (mix_norm, w_in, a_v_norm_g, a_v_norm_b, a_spatial_w, a_spatial_b, q_a_norm, w_uq,
                         kv_a_norm, w_ukv, w_out, ffn_norm, w_up, conv_w, conv_b, w_down, final_norm)))
    m = dict(zip(names, (m_mix_norm, m_w_in, m_a_v_norm_g, m_a_v_norm_b, m_a_spatial_w, m_a_spatial_b,
                         m_q_a_norm, m_w_uq, m_kv_a_norm, m_w_ukv, m_w_out, m_ffn_norm, m_w_up, m_conv_w,
                         m_conv_b, m_w_down, m_final_norm)))
    v = dict(zip(names, (v_mix_norm, v_w_in, v_a_v_norm_g, v_a_v_norm_b, v_a_spatial_w, v_a_spatial_b,
                         v_q_a_norm, v_w_uq, v_kv_a_norm, v_w_ukv, v_w_out, v_ffn_norm, v_w_up, v_conv_w,
                         v_conv_b, v_w_down, v_final_norm)))
    shapes = {n: w[n].shape for n in names}
    def view(tree, n):
        a = tree[n].reshape(tree[n].shape[-2:])
        return a.T if n in _Comm.TRANSPOSED else a

    comm = _Comm({n: view(w, n) for n in ("w_in",) + _Comm.GATHER_GROUPS[0] + _Comm.GATHER_GROUPS[1]})

    grad_x, token = _local_step(
        x, positions, loss_target, w["mix_norm"], w["a_v_norm_g"], w["a_v_norm_b"], w["a_spatial_w"][0],
        w["a_spatial_b"][0], w["q_a_norm"], w["kv_a_norm"], w["ffn_norm"], w["conv_b"],
        w["final_norm"].reshape(1, D_MODEL), comm)

    out_g, out_d, out_m, out_v = {}, {}, {}, {}

    def update(n, parts):
        res = _adamw(parts, view(w, n), view(m, n), view(v, n), "adamw_" + n)
        out_g[n], out_d[n], out_m[n], out_v[n] = (
            (t.T if n in _Comm.TRANSPOSED else t).reshape(shapes[n]) for t in res)
        return res[1]

    for n, parts in zip(_Comm.FFN_GRADS, _wait_copies(comm.h_ffn_grads, True, token, "ffn_grads_wait")):
        last = update(n, parts)
    last = update("w_in", _wait_copies(comm.h_in_grads, True, last, "in_grads_wait")[0])
    for n, parts in zip(_Comm.LATE_GRADS, _wait_copies(comm.h_late_grads, True, last, "late_grads_wait")):
        last = update(n, parts)

    def update_small(names, parts, sums, name):
        res = _adamw_many(parts, *[[_small_2d(t[n]) for n in names] for t in (w, m, v)], sums, name)
        for i, n in enumerate(names):
            out_g[n], out_d[n], out_m[n], out_v[n] = (r[i].reshape(shapes[n]) for r in res[:4])
        return res

    early = _wait_copies(comm.h_small_early, False, last, "small_grads_wait")
    res = update_small(SMALL_EARLY, early, [], "adamw_small")
    late = _wait_copies(comm.h_late_small, False, res[1][0], "late_small_wait")
    res = update_small(SMALL_LATE, late[:-1], late[-1:], "adamw_late")
    loss = res[4][0][0, 0]

    return (loss, grad_x, *[out_g[n] for n in names], *[out_d[n] for n in names],
            *[out_m[n] for n in names], *[out_v[n] for n in names])
```

```python
import functools
import math

import jax
import jax.numpy as jnp
from jax import lax
from jax.experimental import pallas as pl
from jax.experimental.pallas import tpu as pltpu

F32 = jnp.float32
BF16 = jnp.bfloat16
KEPT = jnp.bfloat16

N_DEV = 8
D_MODEL = 1024
EPS = 1e-6
A_GROUPS = 8
CHUNK = 128
MLA_HEADS = 8
QK_NOPE = 128
QK_ROPE = 64
QK_HEAD = QK_NOPE + QK_ROPE
HEAD_PAD = 256
V_HEAD = 128
Q_LORA = 256
KV_LORA = 128
ROPE_THETA = 10000.0
D_FF = 2816
ZS_W = 512
ATTN_SCALE = QK_HEAD ** -0.5
ATTN_TILE = 512
NEG_BIG = -1e30

ADAM_LR = 0.001
ADAM_B1 = 0.9
ADAM_B2 = 0.999
ADAM_EPS = 1e-08
ADAM_WD = 0.01
ADAM_STEP = 10

VMEM_LIMIT = 56 * 1024 * 1024
SMALL_BLOCK_BYTES = 5 * 1024 * 1024
LANES = 128

GELU_K = math.sqrt(2.0 / math.pi)
GELU_C = 0.044715

ANY = pl.BlockSpec(memory_space=pl.ANY)
HBM = pl.BlockSpec(memory_space=pltpu.HBM)
SEM = pl.BlockSpec(memory_space=pltpu.SEMAPHORE)


def _tile(n, pref):
    for t in (pref, 512, 256, 128, 64, 32, 16, 8):
        if t <= pref and n % t == 0:
            return t
    return n


def _wide_tile(n, cap=1408):
    return next((t for t in range(min(n, cap) // LANES * LANES, 0, -LANES) if n % t == 0), n)


def _params(*sem):
    return pltpu.CompilerParams(dimension_semantics=sem, vmem_limit_bytes=VMEM_LIMIT)


def _dot(a, b):
    return jnp.dot(a, b, preferred_element_type=F32)


def _dot_nt(a, b):
    return lax.dot_general(a, b, (((1,), (1,)), ((), ())), preferred_element_type=F32)


def _dot_tn(a, b):
    return lax.dot_general(a, b, (((0,), (0,)), ((), ())), preferred_element_type=F32)


def _sigmoid(x):
    return 1.0 / (1.0 + jnp.exp(-x))


def _gelu(x):
    t = jnp.tanh(GELU_K * (x + GELU_C * x * x * x))
    return 0.5 * x * (1.0 + t)


def _gelu_grad(x):
    t = jnp.tanh(GELU_K * (x + GELU_C * x * x * x))
    return 0.5 * (1.0 + t) + 0.5 * x * (1.0 - t * t) * GELU_K * (1.0 + 3.0 * GELU_C * x * x)


def _in_proj(x, g, wt):
    T, Dm = x.shape
    tm = _tile(T, 512)

    def body(x_ref, g_ref, wt_ref, h_ref, zm_ref, zs_ref):
        xf = x_ref[...]
        r = lax.rsqrt(jnp.mean(xf * xf, axis=-1, keepdims=True) + EPS)
        h = (xf * r * g_ref[...]).astype(BF16)
        h_ref[...] = h
        for i, (r0, r1) in enumerate(IN_ROWS_MAIN):
            zm_ref[:, i * D_MODEL:(i + 1) * D_MODEL] = _dot_nt(h, wt_ref[r0:r1, :]).astype(KEPT)
        zs_ref[...] = _dot_nt(h, wt_ref[IN_ROWS_ZS[0]:IN_ROWS_ZS[1], :])

    row = lambda n: pl.BlockSpec((tm, n), lambda i: (i, 0))
    return pl.pallas_call(
        body, grid=(T // tm,),
        in_specs=[row(Dm), pl.BlockSpec((1, Dm), lambda i: (0, 0)), pl.BlockSpec(wt.shape, lambda i: (0, 0))],
        out_specs=[row(Dm), row(4 * D_MODEL), row(ZS_W)],
        out_shape=[jax.ShapeDtypeStruct((T, Dm), BF16), jax.ShapeDtypeStruct((T, 4 * D_MODEL), KEPT),
                   jax.ShapeDtypeStruct((T, ZS_W), F32)],
        name="in_proj", compiler_params=_params("parallel"))(x, g, wt)


def _proj_bwd(acts, wt, terms, x, g, dres, name, w2=None, dep=None, rows=256):
    T, Dm = x.shape
    tm = _tile(T, rows)
    n_a = len(acts)

    def body(*refs):
        ins, outs = refs[:n_a + 4 + (w2 is not None) + (dep is not None)], refs[-2 - (w2 is not None):]
        wt_ref, x_ref, g_ref, dres_ref = ins[n_a:n_a + 4]
        dx_ref, dg_ref = outs[0], outs[1]

        @pl.when(pl.program_id(0) == 0)
        def _():
            dg_ref[...] = jnp.zeros_like(dg_ref)

        dy = None
        for i, (c0, c1), (r0, r1) in terms:
            t = _dot(ins[i][:, c0:c1], wt_ref[r0:r1, :])
            dy = t if dy is None else dy + t
        xf = x_ref[...]
        r = lax.rsqrt(jnp.mean(xf * xf, axis=-1, keepdims=True) + EPS)
        xh = xf * r
        dg_ref[...] += jnp.sum(dy * xh, axis=0, keepdims=True)
        dxh = dy * g_ref[...]
        dx = dres_ref[...] + r * (dxh - xh * jnp.mean(dxh * xh, axis=-1, keepdims=True))
        dx_ref[...] = dx
        if w2 is not None:
            outs[2][...] = _dot_nt(dx.astype(BF16), ins[n_a + 4][...])

    row = pl.BlockSpec((tm, Dm), lambda i: (i, 0))
    vec = pl.BlockSpec((1, Dm), lambda i: (0, 0))
    in_specs = [pl.BlockSpec((tm, a.shape[1]), lambda i: (i, 0)) for a in acts]
    in_specs += [pl.BlockSpec(wt.shape, lambda i: (0, 0)), row, vec, row]
    args = [*acts, wt, x, g, dres]
    out_specs = [row, vec]
    out_shape = [jax.ShapeDtypeStruct((T, Dm), F32), jax.ShapeDtypeStruct((1, Dm), F32)]
    if w2 is not None:
        in_specs.append(pl.BlockSpec(w2.shape, lambda i: (0, 0)))
        args.append(w2)
        out_specs.append(pl.BlockSpec((tm, w2.shape[0]), lambda i: (i, 0)))
        out_shape.append(jax.ShapeDtypeStruct((T, w2.shape[0]), F32))
    if dep is not None:
        in_specs.append(ANY)
        args.append(dep)
    return pl.pallas_call(
        body, grid=(T // tm,), in_specs=in_specs, out_specs=out_specs, out_shape=out_shape,
        name=name, compiler_params=_params("arbitrary"))(*args)


def _mm_tn(a, b, name, dep=None, rows=None, row0=0, into=None):
    T, M = a.shape
    N = b.shape[1]
    tm, tn, tt = _wide_tile(M), _wide_tile(N), _tile(T, 2048)
    n_t = T // tt
    off = row0 // tm
    extra = ([dep] if dep is not None else []) + ([into] if into is not None else [])

    def body(a_ref, b_ref, *refs):
        o_ref, acc_ref = refs[-2:]
        t = pl.program_id(2)

        @pl.when(t == 0)
        def _():
            acc_ref[...] = jnp.zeros_like(acc_ref)

        acc_ref[...] += _dot_tn(a_ref[...].astype(BF16), b_ref[...].astype(BF16))

        @pl.when(t == n_t - 1)
        def _():
            o_ref[...] = acc_ref[...].astype(BF16)

    return pl.pallas_call(
        body, grid=(M // tm, N // tn, n_t),
        in_specs=[pl.BlockSpec((tt, tm), lambda i, j, t: (t, i)),
                  pl.BlockSpec((tt, tn), lambda i, j, t: (t, j))] + [ANY] * len(extra),
        out_specs=pl.BlockSpec((tm, tn), lambda i, j, t: (i + off, j)),
        out_shape=jax.ShapeDtypeStruct((rows or M, N), BF16),
        scratch_shapes=[pltpu.VMEM((tm, tn), F32)],
        input_output_aliases={} if into is None else {1 + len(extra): 0},
        name=name, compiler_params=_params("parallel", "parallel", "arbitrary"))(a, b, *extra)


def _layer_norm_fwd(gv, g, b):
    mu = jnp.mean(gv, axis=-1, keepdims=True)
    xc = gv - mu
    rs = lax.rsqrt(jnp.mean(xc * xc, axis=-1, keepdims=True) + EPS)
    xh = xc * rs
    return xh, rs, xh * g + b


def _tri_mask(transposed=False):
    r = lax.broadcasted_iota(jnp.int32, (CHUNK, CHUNK), 0)
    c = lax.broadcasted_iota(jnp.int32, (CHUNK, CHUNK), 1)
    return r <= c if transposed else c <= r


def _mixer_a_fwd(zm, av_g, av_b, w_s, b_col):
    T = zm.shape[0]
    tm = _tile(T, 256)
    n_chunk = tm // CHUNK

    def body(u_ref, v_ref, ga_ref, g_ref, b_ref, w_ref, bc_ref, y_ref, vn_s, mx_s):
        gu = _gelu(u_ref[...].astype(F32))
        _, _, vn = _layer_norm_fwd(_gelu(v_ref[...].astype(F32)), g_ref[...], b_ref[...])
        vn_s[...] = vn.astype(BF16)
        tri = _tri_mask()
        for gi in range(A_GROUPS):
            wm = jnp.where(tri, w_ref[gi], 0.0).astype(BF16)
            cols = slice(gi * CHUNK, (gi + 1) * CHUNK)
            for n in range(n_chunk):
                rows = slice(n * CHUNK, (n + 1) * CHUNK)
                mx_s[rows, cols] = _dot(wm, vn_s[rows, cols]) + bc_ref[gi]
        y_ref[...] = _sigmoid(ga_ref[...].astype(F32)) * gu * mx_s[...]

    col = lambda c: pl.BlockSpec((tm, D_MODEL), lambda i: (i, c))
    vec = pl.BlockSpec((1, D_MODEL), lambda i: (0, 0))
    return pl.pallas_call(
        body, grid=(T // tm,),
        in_specs=[col(0), col(1), col(2), vec, vec,
                  pl.BlockSpec((A_GROUPS, CHUNK, CHUNK), lambda i: (0, 0, 0)),
                  pl.BlockSpec((A_GROUPS, CHUNK, 1), lambda i: (0, 0, 0))],
        out_specs=pl.BlockSpec((tm, D_MODEL), lambda i: (i, 0)),
        out_shape=jax.ShapeDtypeStruct((T, D_MODEL), F32),
        scratch_shapes=[pltpu.VMEM((tm, D_MODEL), BF16), pltpu.VMEM((tm, D_MODEL), F32)],
        name="mixer_a_fwd", compiler_params=_params("parallel"))(zm, zm, zm, av_g, av_b, w_s, b_col)


def _mixer_bwd(zm, o, dm, av_g, av_b, w_s, w_st, b_col, dep):
    T = zm.shape[0]
    tm = _tile(T, 256)
    n_chunk = tm // CHUNK

    def body(u_ref, v_ref, ga_ref, gb_ref, o_ref, dm_ref, g_ref, b_ref, w_ref, wt_ref, bc_ref, dep_ref,
             dz_ref, do_ref, dg_ref, db_ref, dw_ref, dbs_ref, vn_s, mx_s, dmx_s, dvn_s):
        @pl.when(pl.program_id(0) == 0)
        def _():
            dg_ref[...] = jnp.zeros_like(dg_ref)
            db_ref[...] = jnp.zeros_like(db_ref)
            dw_ref[...] = jnp.zeros_like(dw_ref)
            dbs_ref[...] = jnp.zeros_like(dbs_ref)

        dm_v = dm_ref[...]
        gb = gb_ref[...].astype(F32)
        sb = _sigmoid(gb)
        o_v = o_ref[...]
        do_ref[...] = (dm_v * sb).astype(BF16)
        dz_ref[:, 3 * D_MODEL:4 * D_MODEL] = (dm_v * o_v * sb * (1.0 - sb)).astype(BF16)
        u = u_ref[...].astype(F32)
        v = v_ref[...].astype(F32)
        gu = _gelu(u)
        xh, rs, vn = _layer_norm_fwd(_gelu(v), g_ref[...], b_ref[...])
        vn_s[...] = vn.astype(BF16)
        tri = _tri_mask()
        for gi in range(A_GROUPS):
            wm = jnp.where(tri, w_ref[gi], 0.0).astype(BF16)
            cols = slice(gi * CHUNK, (gi + 1) * CHUNK)
            for n in range(n_chunk):
                rows = slice(n * CHUNK, (n + 1) * CHUNK)
                mx_s[rows, cols] = _dot(wm, vn_s[rows, cols]) + bc_ref[gi]
        mixed = mx_s[...]
        sa = _sigmoid(ga_ref[...].astype(F32))
        dya = dm_v * sa
        dz_ref[:, 2 * D_MODEL:3 * D_MODEL] = (dm_v * gu * mixed * sa * (1.0 - sa)).astype(BF16)
        dz_ref[:, 0:D_MODEL] = (dya * mixed * _gelu_grad(u)).astype(BF16)
        dmx = dya * gu
        dmx_s[...] = dmx.astype(BF16)
        tri_t = _tri_mask(transposed=True)
        for gi in range(A_GROUPS):
            wmt = jnp.where(tri_t, wt_ref[gi], 0.0).astype(BF16)
            cols = slice(gi * CHUNK, (gi + 1) * CHUNK)
            dw_acc = jnp.zeros((CHUNK, CHUNK), F32)
            dmx_sum = jnp.zeros((CHUNK, CHUNK), F32)
            for n in range(n_chunk):
                rows = slice(n * CHUNK, (n + 1) * CHUNK)
                blk = dmx_s[rows, cols]
                dvn_s[rows, cols] = _dot(wmt, blk)
                dw_acc = dw_acc + _dot_nt(blk, vn_s[rows, cols])
                dmx_sum = dmx_sum + dmx[rows, cols]
            dw_ref[gi] += jnp.where(tri, dw_acc, 0.0)
            dbs_ref[gi] += jnp.sum(dmx_sum, axis=-1, keepdims=True)
        dvn = dvn_s[...]
        dg_ref[...] += jnp.sum(dvn * xh, axis=0, keepdims=True)
        db_ref[...] += jnp.sum(dvn, axis=0, keepdims=True)
        dxh = dvn * g_ref[...]
        dgv = rs * (dxh - jnp.mean(dxh, axis=-1, keepdims=True)
                    - xh * jnp.mean(dxh * xh, axis=-1, keepdims=True))
        dz_ref[:, D_MODEL:2 * D_MODEL] = (dgv * _gelu_grad(v)).astype(BF16)

    col = lambda c: pl.BlockSpec((tm, D_MODEL), lambda i: (i, c))
    row = pl.BlockSpec((tm, D_MODEL), lambda i: (i, 0))
    vec = pl.BlockSpec((1, D_MODEL), lambda i: (0, 0))
    wsp = pl.BlockSpec((A_GROUPS, CHUNK, CHUNK), lambda i: (0, 0, 0))
    bsp = pl.BlockSpec((A_GROUPS, CHUNK, 1), lambda i: (0, 0, 0))
    return pl.pallas_call(
        body, grid=(T // tm,),
        in_specs=[col(0), col(1), col(2), col(3), row, row, vec, vec, wsp, wsp, bsp, ANY],
        out_specs=[pl.BlockSpec((tm, 4 * D_MODEL), lambda i: (i, 0)), row, vec, vec, wsp, bsp],
        out_shape=[jax.ShapeDtypeStruct((T, 4 * D_MODEL), BF16), jax.ShapeDtypeStruct((T, D_MODEL), BF16),
                   jax.ShapeDtypeStruct((1, D_MODEL), F32), jax.ShapeDtypeStruct((1, D_MODEL), F32),
                   jax.ShapeDtypeStruct((A_GROUPS, CHUNK, CHUNK), F32),
                   jax.ShapeDtypeStruct((A_GROUPS, CHUNK, 1), F32)],
        scratch_shapes=[pltpu.VMEM((tm, D_MODEL), BF16), pltpu.VMEM((tm, D_MODEL), F32),
                        pltpu.VMEM((tm, D_MODEL), BF16), pltpu.VMEM((tm, D_MODEL), F32)],
        name="mixer_bwd", compiler_params=_params("arbitrary"))(
            zm, zm, zm, zm, o, dm, av_g, av_b, w_s, w_st, b_col, dep)


def _rope_tables(pos_ref, invf_ref):
    ang = pos_ref[...].astype(F32) * invf_ref[...]
    lane = lax.broadcasted_iota(jnp.int32, ang.shape, 1)
    cos, sin = jnp.cos(ang), jnp.sin(ang)
    c = jnp.where(lane < QK_ROPE, cos, 0.0)
    sa = jnp.where(lane < QK_ROPE // 2, -sin, 0.0)
    sb = jnp.where((lane >= QK_ROPE // 2) & (lane < QK_ROPE), sin, 0.0)
    return c, sa, sb


def _rope(blk, tabs):
    c, sa, sb = tabs
    return blk * c + pltpu.roll(blk, LANES - QK_ROPE // 2, 1) * sa + pltpu.roll(blk, QK_ROPE // 2, 1) * sb


def _rope_t(dout, tabs):
    c, sa, sb = tabs
    return dout * c + pltpu.roll(dout * sa, QK_ROPE // 2, 1) + pltpu.roll(dout * sb, LANES - QK_ROPE // 2, 1)


def _rms_small(x, g):
    r = lax.rsqrt(jnp.mean(x * x, axis=-1, keepdims=True) + EPS)
    xh = x * r
    return xh, r, xh * g


def _mla_prep_fwd(zs, pos, invf, qg, kvg, wuq_p, wukv):
    T = zs.shape[0]
    tm = _tile(T, 512)
    HW = MLA_HEADS * HEAD_PAD

    def body(zs_ref, pos_ref, invf_ref, qg_ref, kvg_ref, wq_ref, wkv_ref, q_ref, k_ref, v_ref):
        tabs = _rope_tables(pos_ref, invf_ref)
        _, _, cqn = _rms_small(zs_ref[:, 0:Q_LORA], qg_ref[...])
        _, _, ckvn = _rms_small(zs_ref[:, Q_LORA:Q_LORA + KV_LORA], kvg_ref[...])
        q = _dot_nt(cqn.astype(BF16), wq_ref[...])
        kv = _dot(ckvn.astype(BF16), wkv_ref[...])
        kr = _rope(zs_ref[:, Q_LORA + KV_LORA:ZS_W], tabs).astype(BF16)
        for h in range(MLA_HEADS):
            b0 = h * HEAD_PAD
            q_ref[:, b0:b0 + QK_NOPE] = q[:, b0:b0 + QK_NOPE].astype(BF16)
            q_ref[:, b0 + QK_NOPE:b0 + HEAD_PAD] = _rope(q[:, b0 + QK_NOPE:b0 + HEAD_PAD], tabs).astype(BF16)
            k_ref[:, b0:b0 + QK_NOPE] = kv[:, b0:b0 + QK_NOPE].astype(BF16)
            k_ref[:, b0 + QK_NOPE:b0 + HEAD_PAD] = kr
            v_ref[:, h * V_HEAD:(h + 1) * V_HEAD] = kv[:, b0 + QK_NOPE:b0 + HEAD_PAD].astype(BF16)

    full = lambda a: pl.BlockSpec(a.shape, lambda i: (0,) * a.ndim)
    return pl.pallas_call(
        body, grid=(T // tm,),
        in_specs=[pl.BlockSpec((tm, ZS_W), lambda i: (i, 0)), pl.BlockSpec((tm, 1), lambda i: (i, 0)),
                  full(invf), full(qg), full(kvg), full(wuq_p), full(wukv)],
        out_specs=[pl.BlockSpec((tm, HW), lambda i: (i, 0)), pl.BlockSpec((tm, HW), lambda i: (i, 0)),
                   pl.BlockSpec((tm, D_MODEL), lambda i: (i, 0))],
        out_shape=[jax.ShapeDtypeStruct((T, HW), BF16), jax.ShapeDtypeStruct((T, HW), BF16),
                   jax.ShapeDtypeStruct((T, D_MODEL), BF16)],
        name="mla_prep_fwd", compiler_params=_params("parallel"))(zs, pos, invf, qg, kvg, wuq_p, wukv)


def _mla_prep_bwd(zs, pos, invf, qg, kvg, wuq_p, wukv, dq, dk, dv):
    T = zs.shape[0]
    tm = _tile(T, 256)
    HW = MLA_HEADS * HEAD_PAD

    def body(zs_ref, pos_ref, invf_ref, qg_ref, kvg_ref, wq_ref, wkv_ref, dq_ref, dk_ref, dv_ref,
             dzs_ref, cqn_ref, dqp_ref, ckvn_ref, dkv_ref, dqg_ref, dkvg_ref):
        @pl.when(pl.program_id(0) == 0)
        def _():
            dqg_ref[...] = jnp.zeros_like(dqg_ref)
            dkvg_ref[...] = jnp.zeros_like(dkvg_ref)

        tabs = _rope_tables(pos_ref, invf_ref)
        cqh, rq, cqn = _rms_small(zs_ref[:, 0:Q_LORA], qg_ref[...])
        ckvh, rkv, ckvn = _rms_small(zs_ref[:, Q_LORA:Q_LORA + KV_LORA], kvg_ref[...])
        cqn_ref[...] = cqn.astype(BF16)
        ckvn_ref[...] = ckvn.astype(BF16)
        dkr = jnp.zeros((tm, LANES), F32)
        for h in range(MLA_HEADS):
            b0 = h * HEAD_PAD
            dqp_ref[:, b0:b0 + QK_NOPE] = dq_ref[:, b0:b0 + QK_NOPE]
            dqp_ref[:, b0 + QK_NOPE:b0 + HEAD_PAD] = _rope_t(
                dq_ref[:, b0 + QK_NOPE:b0 + HEAD_PAD].astype(F32), tabs).astype(BF16)
            dkv_ref[:, b0:b0 + QK_NOPE] = dk_ref[:, b0:b0 + QK_NOPE]
            dkv_ref[:, b0 + QK_NOPE:b0 + HEAD_PAD] = dv_ref[:, h * V_HEAD:(h + 1) * V_HEAD]
            dkr = dkr + dk_ref[:, b0 + QK_NOPE:b0 + HEAD_PAD].astype(F32)
        dcqn = _dot(dqp_ref[...], wq_ref[...])
        dckvn = _dot_nt(dkv_ref[...], wkv_ref[...])
        dqg_ref[...] += jnp.sum(dcqn * cqh, axis=0, keepdims=True)
        dkvg_ref[...] += jnp.sum(dckvn * ckvh, axis=0, keepdims=True)
        dxh = dcqn * qg_ref[...]
        dzs_ref[:, 0:Q_LORA] = (rq * (dxh - cqh * jnp.mean(dxh * cqh, axis=-1, keepdims=True))).astype(BF16)
        dxh = dckvn * kvg_ref[...]
        dzs_ref[:, Q_LORA:Q_LORA + KV_LORA] = (
            rkv * (dxh - ckvh * jnp.mean(dxh * ckvh, axis=-1, keepdims=True))).astype(BF16)
        dzs_ref[:, Q_LORA + KV_LORA:ZS_W] = _rope_t(dkr, tabs).astype(BF16)

    full = lambda a: pl.BlockSpec(a.shape, lambda i: (0,) * a.ndim)
    rowb = lambda w: pl.BlockSpec((tm, w), lambda i: (i, 0))
    return pl.pallas_call(
        body, grid=(T // tm,),
        in_specs=[rowb(ZS_W), rowb(1), full(invf), full(qg), full(kvg), full(wuq_p), full(wukv),
                  rowb(HW), rowb(HW), rowb(D_MODEL)],
        out_specs=[rowb(ZS_W), rowb(Q_LORA), rowb(HW), rowb(KV_LORA), rowb(HW), full(qg), full(kvg)],
        out_shape=[jax.ShapeDtypeStruct((T, ZS_W), BF16), jax.ShapeDtypeStruct((T, Q_LORA), BF16),
                   jax.ShapeDtypeStruct((T, HW), BF16), jax.ShapeDtypeStruct((T, KV_LORA), BF16),
                   jax.ShapeDtypeStruct((T, HW), BF16), jax.ShapeDtypeStruct(qg.shape, F32),
                   jax.ShapeDtypeStruct(kvg.shape, F32)],
        name="mla_prep_bwd", compiler_params=_params("arbitrary"))(
            zs, pos, invf, qg, kvg, wuq_p, wukv, dq, dk, dv)


def _causal(tq, kmax, q0):
    r = lax.broadcasted_iota(jnp.int32, (tq, kmax), 0) + q0
    c = lax.broadcasted_iota(jnp.int32, (tq, kmax), 1)
    return c <= r


def _attn_fwd(q, k, v, batch, seq):
    tq = _tile(seq, ATTN_TILE)
    nq = seq // tq

    def body(q_ref, k_ref, v_ref, o_ref, lse_ref):
        diag = _causal(tq, tq, 0)
        for qi in range(nq):
            rows = slice(qi * tq, (qi + 1) * tq)
            qr = q_ref[rows, :]
            s_d = jnp.where(diag, _dot_nt(qr, k_ref[rows, :]) * ATTN_SCALE, NEG_BIG)
            m = jnp.max(s_d, axis=-1, keepdims=True)
            if qi > 0:
                before = slice(0, qi * tq)
                s_b = _dot_nt(qr, k_ref[before, :]) * ATTN_SCALE
                m = jnp.maximum(m, jnp.max(s_b, axis=-1, keepdims=True))
                p_b = jnp.exp(s_b - m)
                l = jnp.sum(p_b, axis=-1, keepdims=True)
                acc = _dot(p_b.astype(BF16), v_ref[before, :])
            p_d = jnp.exp(s_d - m)
            l_d = jnp.sum(p_d, axis=-1, keepdims=True)
            acc_d = _dot(p_d.astype(BF16), v_ref[rows, :])
            l, acc = (l + l_d, acc + acc_d) if qi > 0 else (l_d, acc_d)
            o_ref[rows, :] = acc / l
            lse_ref[rows, :] = jnp.broadcast_to(m + jnp.log(l), (tq, V_HEAD))

    return pl.pallas_call(
        body, grid=(batch, MLA_HEADS),
        in_specs=[pl.BlockSpec((seq, HEAD_PAD), lambda b, h: (b, h)),
                  pl.BlockSpec((seq, HEAD_PAD), lambda b, h: (b, h)),
                  pl.BlockSpec((seq, V_HEAD), lambda b, h: (b, h))],
        out_specs=[pl.BlockSpec((seq, V_HEAD), lambda b, h: (b, h)),
                   pl.BlockSpec((seq, V_HEAD), lambda b, h: (b, h))],
        out_shape=[jax.ShapeDtypeStruct((batch * seq, D_MODEL), F32),
                   jax.ShapeDtypeStruct((batch * seq, D_MODEL), F32)],
        name="attn_fwd", compiler_params=_params("parallel", "parallel"))(q, k, v)


def _attn_bwd(q, k, v, o, do, lse, batch, seq, dep):
    tq = _tile(seq, ATTN_TILE)
    nq = seq // tq

    def body(q_ref, k_ref, v_ref, o_ref, do_ref, lse_ref, dep_ref, dq_ref, dk_ref, dv_ref, dk_acc, dv_acc):
        dk_acc[...] = jnp.zeros_like(dk_acc)
        dv_acc[...] = jnp.zeros_like(dv_acc)
        for qi in range(nq):
            rows = slice(qi * tq, (qi + 1) * tq)
            kmax = (qi + 1) * tq
            qr = q_ref[rows, :]
            dor = do_ref[rows, :]
            kk = k_ref[0:kmax, :]
            s = _dot_nt(qr, kk) * ATTN_SCALE
            p = jnp.where(_causal(tq, kmax, qi * tq), jnp.exp(s - lse_ref[rows, 0:1]), 0.0)
            dp = _dot_nt(dor, v_ref[0:kmax, :])
            delta = jnp.sum(dor.astype(F32) * o_ref[rows, :], axis=-1, keepdims=True)
            ds = (p * (dp - delta) * ATTN_SCALE).astype(BF16)
            dq_ref[rows, :] = _dot(ds, kk).astype(BF16)
            dk_acc[0:kmax, :] += _dot_tn(ds, qr)
            dv_acc[0:kmax, :] += _dot_tn(p.astype(BF16), dor)
        dk_ref[...] = dk_acc[...].astype(BF16)
        dv_ref[...] = dv_acc[...].astype(BF16)

    qspec = pl.BlockSpec((seq, HEAD_PAD), lambda b, h: (b, h))
    vspec = pl.BlockSpec((seq, V_HEAD), lambda b, h: (b, h))
    T = batch * seq
    return pl.pallas_call(
        body, grid=(batch, MLA_HEADS),
        in_specs=[qspec, qspec, vspec, vspec, vspec, vspec, ANY],
        out_specs=[qspec, qspec, vspec],
        out_shape=[jax.ShapeDtypeStruct((T, MLA_HEADS * HEAD_PAD), BF16),
                   jax.ShapeDtypeStruct((T, MLA_HEADS * HEAD_PAD), BF16),
                   jax.ShapeDtypeStruct((T, D_MODEL), BF16)],
        scratch_shapes=[pltpu.VMEM((seq, HEAD_PAD), F32), pltpu.VMEM((seq, V_HEAD), F32)],
        name="attn_bwd", compiler_params=_params("parallel", "parallel"))(q, k, v, o, do, lse, dep)


def _merge_out(x, yag, zm, o, w_out, ffn_g):
    T = x.shape[0]
    tm = _tile(T, 512)

    def body(x_ref, ya_ref, gb_ref, o_ref, w_ref, g_ref, mg_ref, x1_ref, h2_ref):
        mg = (ya_ref[...] + _sigmoid(gb_ref[...].astype(F32)) * o_ref[...]).astype(BF16)
        mg_ref[...] = mg
        x1 = x_ref[...] + _dot(mg, w_ref[...])
        x1_ref[...] = x1
        r = lax.rsqrt(jnp.mean(x1 * x1, axis=-1, keepdims=True) + EPS)
        h2_ref[...] = (x1 * r * g_ref[...]).astype(BF16)

    row = pl.BlockSpec((tm, D_MODEL), lambda i: (i, 0))
    return pl.pallas_call(
        body, grid=(T // tm,),
        in_specs=[row, row, pl.BlockSpec((tm, D_MODEL), lambda i: (i, 3)), row,
                  pl.BlockSpec((D_MODEL, D_MODEL), lambda i: (0, 0)), pl.BlockSpec((1, D_MODEL), lambda i: (0, 0))],
        out_specs=[row, row, row],
        out_shape=[jax.ShapeDtypeStruct((T, D_MODEL), BF16), jax.ShapeDtypeStruct((T, D_MODEL), F32),
                   jax.ShapeDtypeStruct((T, D_MODEL), BF16)],
        name="merge_out", compiler_params=_params("parallel"))(x, yag, zm, o, w_out, ffn_g)


FF_TILE = 256
FF_BLOCKS = D_FF // FF_TILE
FFB_TILE = 256


def _shift_down(x, k):
    row = lax.broadcasted_iota(jnp.int32, x.shape, 0)
    return jnp.where(row >= k, pltpu.roll(x, k, 0), 0.0)


def _shift_up(x, k):
    n = x.shape[0]
    row = lax.broadcasted_iota(jnp.int32, x.shape, 0)
    return jnp.where(row < n - k, pltpu.roll(x, n - k, 0), 0.0)


def _conv(x, w_ref, b_ref):
    return b_ref[...] + w_ref[2:3, :] * x + w_ref[1:2, :] * _shift_down(x, 1) + w_ref[0:1, :] * _shift_down(x, 2)


def _up_act(h2, wt_up, cw, cb, batch, seq):
    def body(h_ref, wug_ref, wuv_ref, wg_ref, wv_ref, bg_ref, bv_ref, ug_ref, uv_ref, g_ref, v_ref, a_ref):
        h = h_ref[...]
        ug = _dot_nt(h, wug_ref[...])
        uv = _dot_nt(h, wuv_ref[...])
        ug_ref[...] = ug.astype(KEPT)
        uv_ref[...] = uv.astype(KEPT)
        gate = _conv(ug, wg_ref, bg_ref)
        val = _conv(uv, wv_ref, bv_ref)
        g_ref[...] = gate.astype(KEPT)
        v_ref[...] = val.astype(KEPT)
        a_ref[...] = (gate * _sigmoid(gate) * val).astype(BF16)

    blk = pl.BlockSpec((seq, FF_TILE), lambda b, j: (b, j))
    wup = lambda off: pl.BlockSpec((FF_TILE, D_MODEL), lambda b, j: (j + off, 0))
    wsp = lambda off: pl.BlockSpec((3, FF_TILE), lambda b, j: (0, j + off))
    bsp = lambda off: pl.BlockSpec((1, FF_TILE), lambda b, j: (0, j + off))
    T = batch * seq
    kept = jax.ShapeDtypeStruct((T, D_FF), KEPT)
    return pl.pallas_call(
        body, grid=(batch, FF_BLOCKS),
        in_specs=[pl.BlockSpec((seq, D_MODEL), lambda b, j: (b, 0)), wup(0), wup(FF_BLOCKS),
                  wsp(0), wsp(FF_BLOCKS), bsp(0), bsp(FF_BLOCKS)],
        out_specs=[blk] * 5,
        out_shape=[kept, kept, kept, kept, jax.ShapeDtypeStruct((T, D_FF), BF16)],
        name="up_act", compiler_params=_params("parallel", "arbitrary"))(h2, wt_up, wt_up, cw, cw, cb, cb)


def _ffn_act_bwd(upg, upv, gate, val, cw, dx2b, w_down, batch, seq):
    def half(du, x, w_ref, dx_ref, dw_ref):
        j = pl.program_id(1)
        up1, up2 = _shift_up(du, 1), _shift_up(du, 2)
        dx_ref[...] = (w_ref[2:3, :] * du + w_ref[1:2, :] * up1 + w_ref[0:1, :] * up2).astype(BF16)
        dw_ref[j, 2:3, :] += jnp.sum(du * x, axis=0, keepdims=True)
        dw_ref[j, 1:2, :] += jnp.sum(up1 * x, axis=0, keepdims=True)
        dw_ref[j, 0:1, :] += jnp.sum(up2 * x, axis=0, keepdims=True)
        dw_ref[j, 3:4, :] += jnp.sum(du, axis=0, keepdims=True)

    def body(ug_ref, uv_ref, g_ref, v_ref, wg_ref, wv_ref, dx_ref, wd_ref, dg_ref, dv_ref, dwg_ref, dwv_ref):
        @pl.when((pl.program_id(0) == 0) & (pl.program_id(1) == 0))
        def _():
            dwg_ref[...] = jnp.zeros_like(dwg_ref)
            dwv_ref[...] = jnp.zeros_like(dwv_ref)

        gate, val = g_ref[...].astype(F32), v_ref[...].astype(F32)
        sg = _sigmoid(gate)
        dav = _dot_nt(dx_ref[...], wd_ref[...])
        half(dav * val * sg * (1.0 + gate * (1.0 - sg)), ug_ref[...].astype(F32), wg_ref, dg_ref, dwg_ref)
        half(dav * gate * sg, uv_ref[...].astype(F32), wv_ref, dv_ref, dwv_ref)

    nb = D_FF // FFB_TILE
    blk = pl.BlockSpec((seq, FFB_TILE), lambda b, j: (b, j))
    wsp = lambda off: pl.BlockSpec((3, FFB_TILE), lambda b, j: (0, j + off))
    acc = pl.BlockSpec((nb, 4, FFB_TILE), lambda b, j: (0, 0, 0))
    T = batch * seq
    dupg, dupv, dwg, dwv = pl.pallas_call(
        body, grid=(batch, nb),
        in_specs=[blk, blk, blk, blk, wsp(0), wsp(nb),
                  pl.BlockSpec((seq, D_MODEL), lambda b, j: (b, 0)),
                  pl.BlockSpec((FFB_TILE, D_MODEL), lambda b, j: (j, 0))],
        out_specs=[blk, blk, acc, acc],
        out_shape=[jax.ShapeDtypeStruct((T, D_FF), BF16), jax.ShapeDtypeStruct((T, D_FF), BF16),
                   jax.ShapeDtypeStruct((nb, 4, FFB_TILE), F32), jax.ShapeDtypeStruct((nb, 4, FFB_TILE), F32)],
        name="ffn_act_bwd", compiler_params=_params("arbitrary", "arbitrary"))(
            upg, upv, gate, val, cw, cw, dx2b, w_down)
    dwg, dwv = (jnp.transpose(a, (1, 0, 2)).reshape(4, D_FF) for a in (dwg, dwv))
    return dupg, dupv, dwg[:3], dwv[:3], dwg[3:], dwv[3:]


def _down_loss(a, w_down, x1, target, gfin):
    T = x1.shape[0]
    tm = _tile(T, 512)

    def body(a_ref, w_ref, x1_ref, t_ref, g_ref, dx_ref, dxb_ref, loss_ref, dg_ref):
        @pl.when(pl.program_id(0) == 0)
        def _():
            loss_ref[...] = jnp.zeros_like(loss_ref)
            dg_ref[...] = jnp.zeros_like(dg_ref)

        x2 = x1_ref[...] + _dot(a_ref[...], w_ref[...])
        r = lax.rsqrt(jnp.mean(x2 * x2, axis=-1, keepdims=True) + EPS)
        xh = x2 * r
        g = g_ref[...]
        diff = xh * g - t_ref[...]
        loss_ref[...] += 0.5 * jnp.sum(jnp.mean(diff * diff, axis=-1, keepdims=True))
        dy = diff * (1.0 / D_MODEL)
        dg_ref[...] += jnp.sum(dy * xh, axis=0, keepdims=True)
        dxh = dy * g
        dx = r * (dxh - xh * jnp.mean(dxh * xh, axis=-1, keepdims=True))
        dx_ref[...] = dx
        dxb_ref[...] = dx.astype(BF16)

    row = pl.BlockSpec((tm, D_MODEL), lambda i: (i, 0))
    vec = pl.BlockSpec((1, D_MODEL), lambda i: (0, 0))
    return pl.pallas_call(
        body, grid=(T // tm,),
        in_specs=[pl.BlockSpec((tm, D_FF), lambda i: (i, 0)),
                  pl.BlockSpec((D_FF, D_MODEL), lambda i: (0, 0)), row, row, vec],
        out_specs=[row, row, pl.BlockSpec((8, LANES), lambda i: (0, 0)), vec],
        out_shape=[jax.ShapeDtypeStruct((T, D_MODEL), F32), jax.ShapeDtypeStruct((T, D_MODEL), BF16),
                   jax.ShapeDtypeStruct((8, LANES), F32), jax.ShapeDtypeStruct((1, D_MODEL), F32)],
        name="down_loss", compiler_params=_params("arbitrary"))(a, w_down, x1, target, gfin)


def _local_step(x, positions, target, mix_norm, av_g, av_b, w_s, b_s, q_norm, kv_norm, ffn_norm, conv_b,
                final_norm, comm):
    batch, seq, _ = x.shape
    T = batch * seq
    x = x.reshape(T, D_MODEL)
    target = target.reshape(T, D_MODEL)
    pos = positions.reshape(T, 1)
    half = jnp.arange(0, QK_ROPE, 2, dtype=F32) / QK_ROPE
    inv_freq = 1.0 / (ROPE_THETA ** half)
    invf = jnp.concatenate([inv_freq, inv_freq, jnp.zeros((LANES - QK_ROPE,), F32)]).reshape(1, LANES)
    w_st = jnp.swapaxes(w_s, 1, 2)
    b_col = b_s.reshape(A_GROUPS, CHUNK, 1)

    wt_in = comm.in_weights()
    h, zm, zs = _in_proj(x, mix_norm, wt_in)
    yag = _mixer_a_fwd(zm, av_g, av_b, w_s, b_col)
    wuq_p, wukv, w_out = comm.mla_weights(after=yag)
    q, k, v = _mla_prep_fwd(zs, pos, invf, q_norm, kv_norm, wuq_p, wukv)
    o, lse = _attn_fwd(q, k, v, batch, seq)
    merged, x1, h2 = _merge_out(x, yag, zm, o, w_out, ffn_norm)
    wt_up, conv_w, w_down = comm.ffn_weights(after=merged)
    upg, upv, gate, val, act = _up_act(h2, wt_up, conv_w, conv_b, batch, seq)
    dx2, dx2b, loss_acc, d_final = _down_loss(act, w_down, x1, target, final_norm)

    d_wdown = _mm_tn(act, dx2b, "dw_down")
    dupg, dupv, dcwg, dcwv, dcbg, dcbv = _ffn_act_bwd(upg, upv, gate, val, conv_w, dx2b, w_down, batch, seq)
    d_wt_up = _mm_tn(dupv, h2, "dw_up_val", rows=2 * D_FF, row0=D_FF,
                     into=_mm_tn(dupg, h2, "dw_up_gate", rows=2 * D_FF))
    dx1, d_ffn_norm, dmerged = _proj_bwd(
        [dupg, dupv], wt_up, [(0, (0, D_FF), (0, D_FF)), (1, (0, D_FF), (D_FF, 2 * D_FF))],
        x1, ffn_norm, dx2, "up_proj_bwd", w2=w_out)
    d_wout = _mm_tn(merged, dx1, "dw_out")
    token = comm.send_ffn_grads(d_wdown, d_wt_up, jnp.concatenate([dcwg, dcwv], axis=1), d_wout)
    dzm, do, d_avg, d_avb, d_ws, d_bs = _mixer_bwd(zm, o, dmerged, av_g, av_b, w_s, w_st, b_col, token)
    token = comm.send_small_grads([
        d_avg, d_avb, _small_2d(d_ws), d_bs.reshape(A_GROUPS, CHUNK), d_ffn_norm,
        jnp.concatenate([dcbg, dcbv], axis=1), d_final])
    dq, dk, dv = _attn_bwd(q, k, v, o, do, lse, batch, seq, token)
    dzs, cqn, dqp, ckvn, dkv, d_qn, d_kvn = _mla_prep_bwd(zs, pos, invf, q_norm, kv_norm, wuq_p, wukv, dq, dk, dv)
    d_wt_main = _mm_tn(dzm, h, "dw_in_main")
    d_wt_zs = _mm_tn(dzs, h, "dw_in_small")
    token = comm.send_in_grads(d_wt_main, d_wt_zs)
    d_wuq_p = _mm_tn(dqp, cqn, "dw_uq", dep=token)
    d_wukv = _mm_tn(ckvn, dkv, "dw_ukv", dep=token)
    terms = [(0, (i * D_MODEL, (i + 1) * D_MODEL), rows) for i, rows in enumerate(IN_ROWS_MAIN)]
    terms.append((1, (0, ZS_W), IN_ROWS_ZS))
    dx, d_mix_norm = _proj_bwd([dzm, dzs], wt_in, terms, x, mix_norm, dx1, "in_proj_bwd", dep=token, rows=512)
    token = comm.send_late_grads(d_wuq_p, d_wukv, [d_qn, d_kvn, d_mix_norm, loss_acc])
    return dx.reshape(batch, seq, D_MODEL), token


MESH_ID = pl.DeviceIdType.MESH
EFFECT = pltpu.SideEffectType.DATAFLOW_SIDE_EFFECTING


def _mesh_pos():
    return lax.axis_index("x"), lax.axis_index("y"), lax.axis_index("c")


def _peer(pos, d):
    x, y, c = pos
    px = 1 - x if d & 4 else x
    py = 1 - y if d & 2 else y
    pc = 1 - c if d & 1 else c
    return (px, py, pc), 4 * px + 2 * py + pc


def _copy(src_ref, land_ref, send_sems, recv_sems, a, d, pos, exchange, landing_here):
    peer, pid = _peer(pos, d)
    me = 4 * pos[0] + 2 * pos[1] + pos[2]
    if exchange:
        src, dst = src_ref.at[pid], land_ref.at[d]
    else:
        src, dst = src_ref, land_ref.at[pid if landing_here else me]
    return pltpu.make_async_remote_copy(
        src_ref=src, dst_ref=dst, send_sem=send_sems.at[a * (N_DEV - 1) + d - 1],
        recv_sem=recv_sems.at[a * (N_DEV - 1) + d - 1],
        device_id=peer, device_id_type=MESH_ID)


def _start_copies(groups, modes, name, dep=None):
    sizes = [len(g) for g in groups]
    srcs = [s for g in groups for s in g]
    lands = [lax.empty(s.shape if modes[gi] else (N_DEV,) + s.shape, s.dtype)
             for gi, g in enumerate(groups) for s in g]
    n, ng = len(srcs), len(groups)
    n_in = 2 * n + (dep is not None)

    def body(*refs):
        src_refs, land_refs = refs[:n], refs[n:2 * n]
        sems = refs[n_in:n_in + 3 * ng]
        token = refs[-1]
        pos = _mesh_pos()
        k = 0
        for gi, size in enumerate(sizes):
            for a in range(size):
                _own_copy(src_refs[k], land_refs[k], sems[3 * gi + 2], a, pos, modes[gi]).start()
                for d in range(1, N_DEV):
                    _copy(src_refs[k], land_refs[k], sems[3 * gi], sems[3 * gi + 1], a, d, pos, modes[gi],
                          landing_here=False).start()
                k += 1
        token[...] = jnp.zeros_like(token)

    sem_shapes = []
    for size in sizes:
        remote = pltpu.SemaphoreType.DMA((size * (N_DEV - 1),))
        sem_shapes += [remote, remote, pltpu.SemaphoreType.DMA((size,))]
    out = pl.pallas_call(
        body, name=name,
        out_shape=(*sem_shapes, *[pltpu.HBM(a.shape, a.dtype) for a in srcs + lands],
                   jax.ShapeDtypeStruct((8, LANES), F32)),
        in_specs=[HBM] * (2 * n) + [ANY] * (dep is not None),
        out_specs=(*[SEM] * (3 * ng), *[HBM] * (2 * n), pl.BlockSpec(memory_space=pltpu.VMEM)),
        input_output_aliases={i: 3 * ng + i for i in range(2 * n)},
        compiler_params=pltpu.CompilerParams(has_side_effects=EFFECT),
    )(*[pltpu.with_memory_space_constraint(a, pltpu.HBM) for a in srcs + lands], *([dep] if dep is not None else []))
    thru = out[3 * ng:3 * ng + 2 * n]
    handles, k = [], 0
    for gi, size in enumerate(sizes):
        handles.append((out[3 * gi:3 * gi + 3], thru[k:k + size], thru[n + k:n + k + size]))
        k += size
    return handles, out[-1]


def _own_copy(src_ref, land_ref, local_sems, a, pos, exchange):
    me = 4 * pos[0] + 2 * pos[1] + pos[2]
    src, dst = (src_ref.at[me], land_ref.at[0]) if exchange else (src_ref, land_ref.at[me])
    return pltpu.make_async_copy(src, dst, local_sems.at[a])


def _wait_copies(handle, exchange, after, name):
    sems, srcs, lands = handle
    n = len(srcs)

    def body(*refs):
        src_refs, land_refs = refs[:n], refs[n:2 * n]
        send, recv, local = refs[2 * n:2 * n + 3]
        pos = _mesh_pos()
        for a in range(n):
            _own_copy(src_refs[a], land_refs[a], local, a, pos, exchange).wait()
            for d in range(1, N_DEV):
                cp = _copy(src_refs[a], land_refs[a], send, recv, a, d, pos, exchange, landing_here=True)
                cp.wait_send()
                cp.wait_recv()

    out = pl.pallas_call(
        body, name=name,
        out_shape=tuple(pltpu.HBM(a.shape, a.dtype) for a in (*srcs, *lands)),
        in_specs=[HBM] * (2 * n) + [SEM, SEM, SEM, ANY], out_specs=[HBM] * (2 * n),
        input_output_aliases={i: i for i in range(2 * n)},
        compiler_params=pltpu.CompilerParams(has_side_effects=EFFECT),
    )(*srcs, *lands, *sems, after)
    return out[n:]


def _gather_now(a, name):
    def body(x_ref, out_ref, send_sems, recv_sems, local_sem):
        x, y, c = _mesh_pos()
        me, sibling = (x, y, c), (x, y, 1 - c)
        chips = [(1 - x, y), (x, 1 - y), (1 - x, 1 - y)]

        def slot(p):
            return out_ref.at[4 * p[0] + 2 * p[1] + p[2]]

        def copy(k, block, to, src=None):
            return pltpu.make_async_remote_copy(
                src_ref=slot(block) if src is None else src, dst_ref=slot(block), send_sem=send_sems.at[k],
                recv_sem=recv_sems.at[k], device_id=to, device_id_type=MESH_ID)

        mine = pltpu.make_async_copy(x_ref, slot(me), local_sem)
        mine.start()
        first = [copy(0, me, sibling, src=x_ref)]
        first += [copy(1 + j, me, (*chip, c), src=x_ref) for j, chip in enumerate(chips)]
        for cp in first:
            cp.start()
        passed = [copy(4 + j, (*chip, c), sibling) for j, chip in enumerate(chips)]
        for j, chip in enumerate(chips):
            copy(1 + j, (*chip, c), me).wait_recv()
            passed[j].start()
        copy(0, sibling, me).wait_recv()
        for j, chip in enumerate(chips):
            copy(4 + j, (*chip, 1 - c), me).wait_recv()
        for cp in first + passed:
            cp.wait_send()
        mine.wait()

    return pl.pallas_call(
        body, in_specs=[ANY], out_specs=ANY,
        out_shape=jax.ShapeDtypeStruct((N_DEV,) + a.shape, a.dtype),
        scratch_shapes=[pltpu.SemaphoreType.DMA((N_DEV - 1,)), pltpu.SemaphoreType.DMA((N_DEV - 1,)),
                        pltpu.SemaphoreType.DMA],
        name=name, compiler_params=pltpu.CompilerParams(has_side_effects=True))(a)


def _sum_parts(p_ref):
    g = p_ref[0].astype(F32)
    for k in range(1, N_DEV):
        g = g + p_ref[k].astype(F32)
    return g


def _adamw_update(p_ref, w_ref, m_ref, v_ref, g_ref, d_ref, nm_ref, nv_ref):
    c1 = 1.0 - ADAM_B1 ** ADAM_STEP
    c2 = 1.0 - ADAM_B2 ** ADAM_STEP
    g = _sum_parts(p_ref)
    nm = ADAM_B1 * m_ref[...] + (1.0 - ADAM_B1) * g
    nv = ADAM_B2 * v_ref[...] + (1.0 - ADAM_B2) * (g * g)
    g_ref[...] = g
    nm_ref[...] = nm
    nv_ref[...] = nv
    d_ref[...] = -ADAM_LR * ((nm / c1) / (jnp.sqrt(nv / c2) + ADAM_EPS) + ADAM_WD * w_ref[...])


def _adamw_many(parts, ws, ms, vs, sums, name):
    n, ns = len(ws), len(sums)

    def body(*refs):
        ins, outs = refs[:4 * n + ns], refs[4 * n + ns:]
        for i in range(n):
            _adamw_update(ins[i], ins[n + i], ins[2 * n + i], ins[3 * n + i],
                          outs[i], outs[n + i], outs[2 * n + i], outs[3 * n + i])
        for i in range(ns):
            outs[4 * n + i][...] = _sum_parts(ins[4 * n + i])

    full = lambda a: pl.BlockSpec(a.shape, lambda: (0,) * a.ndim)
    args = [*parts, *ws, *ms, *vs, *sums]
    outs = [jax.ShapeDtypeStruct(w.shape, F32) for _ in range(4) for w in ws]
    outs += [jax.ShapeDtypeStruct(s.shape[1:], F32) for s in sums]
    res = pl.pallas_call(
        body, in_specs=[full(a) for a in args], out_specs=[full(o) for o in outs], out_shape=outs,
        name=name, compiler_params=pltpu.CompilerParams(vmem_limit_bytes=VMEM_LIMIT))(*args)
    return res[:n], res[n:2 * n], res[2 * n:3 * n], res[3 * n:4 * n], res[4 * n:]


def _adamw(parts, w, m, v, name):
    R, C = w.shape
    tr, tc = R, C
    if N_DEV * R * C * parts.dtype.itemsize > SMALL_BLOCK_BYTES:
        tr = next((t for t in range(min(R, 256) // 16 * 16, 15, -16) if R % t == 0), R)
        if tr == R:
            tc = _tile(C, 256)

    def body(p_ref, w_ref, m_ref, v_ref, g_ref, d_ref, nm_ref, nv_ref):
        _adamw_update(p_ref, w_ref, m_ref, v_ref, g_ref, d_ref, nm_ref, nv_ref)

    blk = pl.BlockSpec((tr, tc), lambda i, j: (i, j))
    shp = jax.ShapeDtypeStruct((R, C), F32)
    return pl.pallas_call(
        body, grid=(R // tr, C // tc),
        in_specs=[pl.BlockSpec((N_DEV, tr, tc), lambda i, j: (0, i, j)), blk, blk, blk],
        out_specs=[blk, blk, blk, blk], out_shape=[shp, shp, shp, shp],
        name=name, compiler_params=_params("parallel", "parallel"))(parts, w, m, v)


SPLIT_V = 2 * D_MODEL
SPLIT_KR = SPLIT_V + Q_LORA + KV_LORA + QK_ROPE
IN_DIM = SPLIT_KR + 2 * D_MODEL
IN_ROWS_MAIN = ((0, D_MODEL), (D_MODEL, SPLIT_V), (SPLIT_KR, SPLIT_KR + D_MODEL), (SPLIT_KR + D_MODEL, IN_DIM))
IN_ROWS_ZS = (SPLIT_V, SPLIT_V + ZS_W)

SMALL_EARLY = ("a_v_norm_g", "a_v_norm_b", "a_spatial_w", "a_spatial_b", "ffn_norm", "conv_b", "final_norm")
SMALL_LATE = ("q_a_norm", "kv_a_norm", "mix_norm")


def _small_2d(a):
    return a.reshape(-1, a.shape[-1])


def _cols_from_shards(g):
    return jnp.transpose(g, (1, 0, 2)).reshape(g.shape[1], N_DEV * g.shape[2])


def _shards_from_cols(a):
    R, W = a.shape
    return jnp.transpose(a.reshape(R, N_DEV, W // N_DEV), (1, 0, 2))


class _Comm:
    GATHER_GROUPS = (("w_uq", "w_ukv", "w_out"), ("w_up", "conv_w", "w_down"))
    FFN_GRADS = ("w_down", "w_up", "conv_w", "w_out")
    LATE_GRADS = ("w_uq", "w_ukv")
    TRANSPOSED = ("w_in", "w_up", "w_uq")

    def __init__(self, shards):
        local = {n: a.astype(F32 if n == "conv_w" else BF16) for n, a in shards.items()}
        self.g_in = _gather_now(local["w_in"], "gather_w_in")
        groups = [[local[n] for n in g] for g in self.GATHER_GROUPS]
        (self.h_mla, self.h_ffn), _ = _start_copies(groups, [False] * 2, "gather_start", dep=self.g_in)

    def in_weights(self):
        return self.g_in.reshape(IN_DIM, D_MODEL)

    def mla_weights(self, after):
        g_uq, g_ukv, g_out = _wait_copies(self.h_mla, False, after, "gather_wait_mla")
        wuq_p = jnp.pad(g_uq, ((0, 0), (0, HEAD_PAD - QK_HEAD), (0, 0))).reshape(MLA_HEADS * HEAD_PAD, Q_LORA)
        return wuq_p, _cols_from_shards(g_ukv), g_out.reshape(D_MODEL, D_MODEL)

    def ffn_weights(self, after):
        g_up, g_cw, g_down = _wait_copies(self.h_ffn, False, after, "gather_wait_ffn")
        return g_up.reshape(2 * D_FF, D_MODEL), _cols_from_shards(g_cw), g_down.reshape(D_FF, D_MODEL)

    def send_ffn_grads(self, d_wdown, d_wt_up, d_convw, d_wout):
        group = [d_wdown.reshape(N_DEV, D_FF // N_DEV, D_MODEL), d_wt_up.reshape(N_DEV, 2 * D_FF // N_DEV, D_MODEL),
                 _shards_from_cols(d_convw), d_wout.reshape(N_DEV, D_MODEL // N_DEV, D_MODEL)]
        (self.h_ffn_grads,), token = _start_copies([group], [True], "ffn_grads_start")
        return token

    def send_small_grads(self, grads):
        (self.h_small_early,), token = _start_copies([grads], [False], "small_grads_start")
        return token

    def send_in_grads(self, d_wt_main, d_wt_zs):
        d_in = jnp.concatenate([d_wt_main[:SPLIT_V], d_wt_zs[:SPLIT_KR - SPLIT_V], d_wt_main[SPLIT_V:]], axis=0)
        blocks = d_in.reshape(N_DEV, IN_DIM // N_DEV, D_MODEL)
        (self.h_in_grads,), token = _start_copies([[blocks]], [True], "in_grads_start")
        return token

    def send_late_grads(self, d_wuq_p, d_wukv, small):
        d_uq = d_wuq_p.reshape(MLA_HEADS, HEAD_PAD, Q_LORA)[:, :QK_HEAD, :]
        (self.h_late_grads, self.h_late_small), token = _start_copies(
            [[d_uq, _shards_from_cols(d_wukv)], small], [True, False], "late_grads_start")
        return token


def kernel(x, positions, mix_norm, w_in, a_v_norm_g, a_v_norm_b, a_spatial_w, a_spatial_b, q_a_norm, w_uq, kv_a_norm, w_ukv, w_out, ffn_norm, w_up, conv_w, conv_b, w_down, final_norm, loss_target, m_mix_norm, m_w_in, m_a_v_norm_g, m_a_v_norm_b, m_a_spatial_w, m_a_spatial_b, m_q_a_norm, m_w_uq, m_kv_a_norm, m_w_ukv, m_w_out, m_ffn_norm, m_w_up, m_conv_w, m_conv_b, m_w_down, m_final_norm, v_mix_norm, v_w_in, v_a_v_norm_g, v_a_v_norm_b, v_a_spatial_w, v_a_spatial_b, v_q_a_norm, v_w_uq, v_kv_a_norm, v_w_ukv, v_w_out, v_ffn_norm, v_w_up, v_conv_w, v_conv_b, v_w_down, v_final_norm):
    names = ("mix_norm", "w_in", "a_v_norm_g", "a_v_norm_b", "a_spatial_w", "a_spatial_b", "q_a_norm", "w_uq",
             "kv_a_norm", "w_ukv", "w_out", "ffn_norm", "w_up", "conv_w", "conv_b", "w_down", "final_norm")
    w = dict(zip(names, (mix_norm, w_in, a_v_norm_g, a_v_norm_b, a_spatial_w, a_spatial_b, q_a_norm, w_uq,
                         kv_a_norm, w_ukv, w_out, ffn_norm, w_up, conv_w, conv_b, w_down, final_norm)))
    m = dict(zip(names, (m_mix_norm, m_w_in, m_a_v_norm_g, m_a_v_norm_b, m_a_spatial_w, m_a_spatial_b,
                         m_q_a_norm, m_w_uq, m_kv_a_norm, m_w_ukv, m_w_out, m_ffn_norm, m_w_up, m_conv_w,
                         m_conv_b, m_w_down, m_final_norm)))
    v = dict(zip(names, (v_mix_norm, v_w_in, v_a_v_norm_g, v_a_v_norm_b, v_a_spatial_w, v_a_spatial_b,
                         v_q_a_norm, v_w_uq, v_kv_a_norm, v_w_ukv, v_w_out, v_ffn_norm, v_w_up, v_conv_w,
                         v_conv_b, v_w_down, v_final_norm)))
    shapes = {n: w[n].shape for n in names}
    def view(tree, n):
        a = tree[n].reshape(tree[n].shape[-2:])
        return a.T if n in _Comm.TRANSPOSED else a

    comm = _Comm({n: view(w, n) for n in ("w_in",) + _Comm.GATHER_GROUPS[0] + _Comm.GATHER_GROUPS[1]})

    grad_x, token = _local_step(
        x, positions, loss_target, w["mix_norm"], w["a_v_norm_g"], w["a_v_norm_b"], w["a_spatial_w"][0],
        w["a_spatial_b"][0], w["q_a_norm"], w["kv_a_norm"], w["ffn_norm"], w["conv_b"],
        w["final_norm"].reshape(1, D_MODEL), comm)

    out_g, out_d, out_m, out_v = {}, {}, {}, {}

    def update(n, parts):
        res = _adamw(parts, view(w, n), view(m, n), view(v, n), "adamw_" + n)
        out_g[n], out_d[n], out_m[n], out_v[n] = (
            (t.T if n in _Comm.TRANSPOSED else t).reshape(shapes[n]) for t in res)
        return res[1]

    def update_small(names, parts, sums, name):
        res = _adamw_many(parts, *[[_small_2d(t[n]) for n in names] for t in (w, m, v)], sums, name)
        for i, n in enumerate(names):
            out_g[n], out_d[n], out_m[n], out_v[n] = (r[i].reshape(shapes[n]) for r in res[:4])
        return res

    for n, parts in zip(_Comm.FFN_GRADS, _wait_copies(comm.h_ffn_grads, True, token, "ffn_grads_wait")):
        last = update(n, parts)
    early = _wait_copies(comm.h_small_early, False, last, "small_grads_wait")
    last = update_small(SMALL_EARLY, early, [], "adamw_small")[1][0]
    last = update("w_in", _wait_copies(comm.h_in_grads, True, last, "in_grads_wait")[0])
    for n, parts in zip(_Comm.LATE_GRADS, _wait_copies(comm.h_late_grads, True, last, "late_grads_wait")):
        last = update(n, parts)
    late = _wait_copies(comm.h_late_small, False, last, "late_small_wait")
    res = update_small(SMALL_LATE, late[:-1], late[-1:], "adamw_late")
    loss = res[4][0][0, 0]

    return (loss, grad_x, *[out_g[n] for n in names], *[out_d[n] for n in names],
            *[out_m[n] for n in names], *[out_v[n] for n in names])
```

```python
import functools
import math

import jax
import jax.numpy as jnp
from jax import lax
from jax.experimental import pallas as pl
from jax.experimental.pallas import tpu as pltpu

F32 = jnp.float32
BF16 = jnp.bfloat16
KEPT = jnp.bfloat16

N_DEV = 8
D_MODEL = 1024
EPS = 1e-6
A_GROUPS = 8
CHUNK = 128
MLA_HEADS = 8
QK_NOPE = 128
QK_ROPE = 64
QK_HEAD = QK_NOPE + QK_ROPE
HEAD_PAD = 256
V_HEAD = 128
Q_LORA = 256
KV_LORA = 128
ROPE_THETA = 10000.0
D_FF = 2816
ZS_W = 512
ATTN_SCALE = QK_HEAD ** -0.5
ATTN_TILE = 512
NEG_BIG = -1e30

ADAM_LR = 0.001
ADAM_B1 = 0.9
ADAM_B2 = 0.999
ADAM_EPS = 1e-08
ADAM_WD = 0.01
ADAM_STEP = 10

VMEM_LIMIT = 56 * 1024 * 1024
SMALL_BLOCK_BYTES = 5 * 1024 * 1024
LANES = 128

GELU_K = math.sqrt(2.0 / math.pi)
GELU_C = 0.044715

ANY = pl.BlockSpec(memory_space=pl.ANY)
HBM = pl.BlockSpec(memory_space=pltpu.HBM)
SEM = pl.BlockSpec(memory_space=pltpu.SEMAPHORE)


def _tile(n, pref):
    for t in (pref, 512, 256, 128, 64, 32, 16, 8):
        if t <= pref and n % t == 0:
            return t
    return n


def _wide_tile(n, cap=1408):
    return next((t for t in range(min(n, cap) // LANES * LANES, 0, -LANES) if n % t == 0), n)


def _params(*sem):
    return pltpu.CompilerParams(dimension_semantics=sem, vmem_limit_bytes=VMEM_LIMIT)


def _dot(a, b):
    return jnp.dot(a, b, preferred_element_type=F32)


def _dot_nt(a, b):
    return lax.dot_general(a, b, (((1,), (1,)), ((), ())), preferred_element_type=F32)


def _dot_tn(a, b):
    return lax.dot_general(a, b, (((0,), (0,)), ((), ())), preferred_element_type=F32)


def _sigmoid(x):
    return 1.0 / (1.0 + jnp.exp(-x))


def _gelu(x):
    t = jnp.tanh(GELU_K * (x + GELU_C * x * x * x))
    return 0.5 * x * (1.0 + t)


def _gelu_grad(x):
    t = jnp.tanh(GELU_K * (x + GELU_C * x * x * x))
    return 0.5 * (1.0 + t) + 0.5 * x * (1.0 - t * t) * GELU_K * (1.0 + 3.0 * GELU_C * x * x)


def _in_proj(x, g, wt):
    T, Dm = x.shape
    tm = _tile(T, 512)

    def body(x_ref, g_ref, wt_ref, h_ref, zm_ref, zs_ref):
        xf = x_ref[...]
        r = lax.rsqrt(jnp.mean(xf * xf, axis=-1, keepdims=True) + EPS)
        h = (xf * r * g_ref[...]).astype(BF16)
        h_ref[...] = h
        for i, (r0, r1) in enumerate(IN_ROWS_MAIN):
            zm_ref[:, i * D_MODEL:(i + 1) * D_MODEL] = _dot_nt(h, wt_ref[r0:r1, :]).astype(KEPT)
        zs_ref[...] = _dot_nt(h, wt_ref[IN_ROWS_ZS[0]:IN_ROWS_ZS[1], :])

    row = lambda n: pl.BlockSpec((tm, n), lambda i: (i, 0))
    return pl.pallas_call(
        body, grid=(T // tm,),
        in_specs=[row(Dm), pl.BlockSpec((1, Dm), lambda i: (0, 0)), pl.BlockSpec(wt.shape, lambda i: (0, 0))],
        out_specs=[row(Dm), row(4 * D_MODEL), row(ZS_W)],
        out_shape=[jax.ShapeDtypeStruct((T, Dm), BF16), jax.ShapeDtypeStruct((T, 4 * D_MODEL), KEPT),
                   jax.ShapeDtypeStruct((T, ZS_W), F32)],
        name="in_proj", compiler_params=_params("parallel"))(x, g, wt)


def _proj_bwd(acts, wt, terms, x, g, dres, name, w2=None, dep=None, rows=256):
    T, Dm = x.shape
    tm = _tile(T, rows)
    n_a = len(acts)

    def body(*refs):
        ins, outs = refs[:n_a + 4 + (w2 is not None) + (dep is not None)], refs[-2 - (w2 is not None):]
        wt_ref, x_ref, g_ref, dres_ref = ins[n_a:n_a + 4]
        dx_ref, dg_ref = outs[0], outs[1]

        @pl.when(pl.program_id(0) == 0)
        def _():
            dg_ref[...] = jnp.zeros_like(dg_ref)

        dy = None
        for i, (c0, c1), (r0, r1) in terms:
            t = _dot(ins[i][:, c0:c1], wt_ref[r0:r1, :])
            dy = t if dy is None else dy + t
        xf = x_ref[...]
        r = lax.rsqrt(jnp.mean(xf * xf, axis=-1, keepdims=True) + EPS)
        xh = xf * r
        dg_ref[...] += jnp.sum(dy * xh, axis=0, keepdims=True)
        dxh = dy * g_ref[...]
        dx = dres_ref[...] + r * (dxh - xh * jnp.mean(dxh * xh, axis=-1, keepdims=True))
        dx_ref[...] = dx
        if w2 is not None:
            outs[2][...] = _dot_nt(dx.astype(BF16), ins[n_a + 4][...]).astype(KEPT)

    row = pl.BlockSpec((tm, Dm), lambda i: (i, 0))
    vec = pl.BlockSpec((1, Dm), lambda i: (0, 0))
    in_specs = [pl.BlockSpec((tm, a.shape[1]), lambda i: (i, 0)) for a in acts]
    in_specs += [pl.BlockSpec(wt.shape, lambda i: (0, 0)), row, vec, row]
    args = [*acts, wt, x, g, dres]
    out_specs = [row, vec]
    out_shape = [jax.ShapeDtypeStruct((T, Dm), F32), jax.ShapeDtypeStruct((1, Dm), F32)]
    if w2 is not None:
        in_specs.append(pl.BlockSpec(w2.shape, lambda i: (0, 0)))
        args.append(w2)
        out_specs.append(pl.BlockSpec((tm, w2.shape[0]), lambda i: (i, 0)))
        out_shape.append(jax.ShapeDtypeStruct((T, w2.shape[0]), KEPT))
    if dep is not None:
        in_specs.append(ANY)
        args.append(dep)
    return pl.pallas_call(
        body, grid=(T // tm,), in_specs=in_specs, out_specs=out_specs, out_shape=out_shape,
        name=name, compiler_params=_params("arbitrary"))(*args)


def _mm_tn(a, b, name, dep=None, rows=None, row0=0, into=None):
    T, M = a.shape
    N = b.shape[1]
    tm, tn, tt = _wide_tile(M), _wide_tile(N), _tile(T, 2048)
    n_t = T // tt
    off = row0 // tm
    extra = ([dep] if dep is not None else []) + ([into] if into is not None else [])

    def body(a_ref, b_ref, *refs):
        o_ref, acc_ref = refs[-2:]
        t = pl.program_id(2)

        @pl.when(t == 0)
        def _():
            acc_ref[...] = jnp.zeros_like(acc_ref)

        acc_ref[...] += _dot_tn(a_ref[...].astype(BF16), b_ref[...].astype(BF16))

        @pl.when(t == n_t - 1)
        def _():
            o_ref[...] = acc_ref[...].astype(BF16)

    return pl.pallas_call(
        body, grid=(M // tm, N // tn, n_t),
        in_specs=[pl.BlockSpec((tt, tm), lambda i, j, t: (t, i)),
                  pl.BlockSpec((tt, tn), lambda i, j, t: (t, j))] + [ANY] * len(extra),
        out_specs=pl.BlockSpec((tm, tn), lambda i, j, t: (i + off, j)),
        out_shape=jax.ShapeDtypeStruct((rows or M, N), BF16),
        scratch_shapes=[pltpu.VMEM((tm, tn), F32)],
        input_output_aliases={} if into is None else {1 + len(extra): 0},
        name=name, compiler_params=_params("parallel", "parallel", "arbitrary"))(a, b, *extra)


def _layer_norm_fwd(gv, g, b):
    mu = jnp.mean(gv, axis=-1, keepdims=True)
    xc = gv - mu
    rs = lax.rsqrt(jnp.mean(xc * xc, axis=-1, keepdims=True) + EPS)
    xh = xc * rs
    return xh, rs, xh * g + b


def _tri_mask(transposed=False):
    r = lax.broadcasted_iota(jnp.int32, (CHUNK, CHUNK), 0)
    c = lax.broadcasted_iota(jnp.int32, (CHUNK, CHUNK), 1)
    return r <= c if transposed else c <= r


def _mixer_a_fwd(zm, av_g, av_b, w_s, b_col):
    T = zm.shape[0]
    tm = _tile(T, 256)
    n_chunk = tm // CHUNK

    def body(u_ref, v_ref, ga_ref, g_ref, b_ref, w_ref, bc_ref, y_ref, vn_s, mx_s):
        gu = _gelu(u_ref[...].astype(F32))
        _, _, vn = _layer_norm_fwd(_gelu(v_ref[...].astype(F32)), g_ref[...], b_ref[...])
        vn_s[...] = vn.astype(BF16)
        tri = _tri_mask()
        for gi in range(A_GROUPS):
            wm = jnp.where(tri, w_ref[gi], 0.0).astype(BF16)
            cols = slice(gi * CHUNK, (gi + 1) * CHUNK)
            for n in range(n_chunk):
                rows = slice(n * CHUNK, (n + 1) * CHUNK)
                mx_s[rows, cols] = _dot(wm, vn_s[rows, cols]) + bc_ref[gi]
        y_ref[...] = (_sigmoid(ga_ref[...].astype(F32)) * gu * mx_s[...]).astype(KEPT)

    col = lambda c: pl.BlockSpec((tm, D_MODEL), lambda i: (i, c))
    vec = pl.BlockSpec((1, D_MODEL), lambda i: (0, 0))
    return pl.pallas_call(
        body, grid=(T // tm,),
        in_specs=[col(0), col(1), col(2), vec, vec,
                  pl.BlockSpec((A_GROUPS, CHUNK, CHUNK), lambda i: (0, 0, 0)),
                  pl.BlockSpec((A_GROUPS, CHUNK, 1), lambda i: (0, 0, 0))],
        out_specs=pl.BlockSpec((tm, D_MODEL), lambda i: (i, 0)),
        out_shape=jax.ShapeDtypeStruct((T, D_MODEL), KEPT),
        scratch_shapes=[pltpu.VMEM((tm, D_MODEL), BF16), pltpu.VMEM((tm, D_MODEL), F32)],
        name="mixer_a_fwd", compiler_params=_params("parallel"))(zm, zm, zm, av_g, av_b, w_s, b_col)


def _mixer_bwd(zm, o, dm, av_g, av_b, w_s, w_st, b_col, dep):
    T = zm.shape[0]
    tm = _tile(T, 256)
    n_chunk = tm // CHUNK

    def body(u_ref, v_ref, ga_ref, gb_ref, o_ref, dm_ref, g_ref, b_ref, w_ref, wt_ref, bc_ref, dep_ref,
             dz_ref, do_ref, dg_ref, db_ref, dw_ref, dbs_ref, vn_s, mx_s, dmx_s, dvn_s):
        @pl.when(pl.program_id(0) == 0)
        def _():
            dg_ref[...] = jnp.zeros_like(dg_ref)
            db_ref[...] = jnp.zeros_like(db_ref)
            dw_ref[...] = jnp.zeros_like(dw_ref)
            dbs_ref[...] = jnp.zeros_like(dbs_ref)

        dm_v = dm_ref[...].astype(F32)
        gb = gb_ref[...].astype(F32)
        sb = _sigmoid(gb)
        o_v = o_ref[...].astype(F32)
        do_ref[...] = (dm_v * sb).astype(BF16)
        dz_ref[:, 3 * D_MODEL:4 * D_MODEL] = (dm_v * o_v * sb * (1.0 - sb)).astype(BF16)
        u = u_ref[...].astype(F32)
        v = v_ref[...].astype(F32)
        gu = _gelu(u)
        xh, rs, vn = _layer_norm_fwd(_gelu(v), g_ref[...], b_ref[...])
        vn_s[...] = vn.astype(BF16)
        tri = _tri_mask()
        for gi in range(A_GROUPS):
            wm = jnp.where(tri, w_ref[gi], 0.0).astype(BF16)
            cols = slice(gi * CHUNK, (gi + 1) * CHUNK)
            for n in range(n_chunk):
                rows = slice(n * CHUNK, (n + 1) * CHUNK)
                mx_s[rows, cols] = _dot(wm, vn_s[rows, cols]) + bc_ref[gi]
        mixed = mx_s[...]
        sa = _sigmoid(ga_ref[...].astype(F32))
        dya = dm_v * sa
        dz_ref[:, 2 * D_MODEL:3 * D_MODEL] = (dm_v * gu * mixed * sa * (1.0 - sa)).astype(BF16)
        dz_ref[:, 0:D_MODEL] = (dya * mixed * _gelu_grad(u)).astype(BF16)
        dmx = dya * gu
        dmx_s[...] = dmx.astype(BF16)
        tri_t = _tri_mask(transposed=True)
        for gi in range(A_GROUPS):
            wmt = jnp.where(tri_t, wt_ref[gi], 0.0).astype(BF16)
            cols = slice(gi * CHUNK, (gi + 1) * CHUNK)
            dw_acc = jnp.zeros((CHUNK, CHUNK), F32)
            dmx_sum = jnp.zeros((CHUNK, CHUNK), F32)
            for n in range(n_chunk):
                rows = slice(n * CHUNK, (n + 1) * CHUNK)
                blk = dmx_s[rows, cols]
                dvn_s[rows, cols] = _dot(wmt, blk)
                dw_acc = dw_acc + _dot_nt(blk, vn_s[rows, cols])
                dmx_sum = dmx_sum + dmx[rows, cols]
            dw_ref[gi] += jnp.where(tri, dw_acc, 0.0)
            dbs_ref[gi] += jnp.sum(dmx_sum, axis=-1, keepdims=True)
        dvn = dvn_s[...]
        dg_ref[...] += jnp.sum(dvn * xh, axis=0, keepdims=True)
        db_ref[...] += jnp.sum(dvn, axis=0, keepdims=True)
        dxh = dvn * g_ref[...]
        dgv = rs * (dxh - jnp.mean(dxh, axis=-1, keepdims=True)
                    - xh * jnp.mean(dxh * xh, axis=-1, keepdims=True))
        dz_ref[:, D_MODEL:2 * D_MODEL] = (dgv * _gelu_grad(v)).astype(BF16)

    col = lambda c: pl.BlockSpec((tm, D_MODEL), lambda i: (i, c))
    row = pl.BlockSpec((tm, D_MODEL), lambda i: (i, 0))
    vec = pl.BlockSpec((1, D_MODEL), lambda i: (0, 0))
    wsp = pl.BlockSpec((A_GROUPS, CHUNK, CHUNK), lambda i: (0, 0, 0))
    bsp = pl.BlockSpec((A_GROUPS, CHUNK, 1), lambda i: (0, 0, 0))
    return pl.pallas_call(
        body, grid=(T // tm,),
        in_specs=[col(0), col(1), col(2), col(3), row, row, vec, vec, wsp, wsp, bsp, ANY],
        out_specs=[pl.BlockSpec((tm, 4 * D_MODEL), lambda i: (i, 0)), row, vec, vec, wsp, bsp],
        out_shape=[jax.ShapeDtypeStruct((T, 4 * D_MODEL), BF16), jax.ShapeDtypeStruct((T, D_MODEL), BF16),
                   jax.ShapeDtypeStruct((1, D_MODEL), F32), jax.ShapeDtypeStruct((1, D_MODEL), F32),
                   jax.ShapeDtypeStruct((A_GROUPS, CHUNK, CHUNK), F32),
                   jax.ShapeDtypeStruct((A_GROUPS, CHUNK, 1), F32)],
        scratch_shapes=[pltpu.VMEM((tm, D_MODEL), BF16), pltpu.VMEM((tm, D_MODEL), F32),
                        pltpu.VMEM((tm, D_MODEL), BF16), pltpu.VMEM((tm, D_MODEL), F32)],
        name="mixer_bwd", compiler_params=_params("arbitrary"))(
            zm, zm, zm, zm, o, dm, av_g, av_b, w_s, w_st, b_col, dep)


def _rope_tables(pos_ref, invf_ref):
    ang = pos_ref[...].astype(F32) * invf_ref[...]
    lane = lax.broadcasted_iota(jnp.int32, ang.shape, 1)
    cos, sin = jnp.cos(ang), jnp.sin(ang)
    c = jnp.where(lane < QK_ROPE, cos, 0.0)
    sa = jnp.where(lane < QK_ROPE // 2, -sin, 0.0)
    sb = jnp.where((lane >= QK_ROPE // 2) & (lane < QK_ROPE), sin, 0.0)
    return c, sa, sb


def _rope(blk, tabs):
    c, sa, sb = tabs
    return blk * c + pltpu.roll(blk, LANES - QK_ROPE // 2, 1) * sa + pltpu.roll(blk, QK_ROPE // 2, 1) * sb


def _rope_t(dout, tabs):
    c, sa, sb = tabs
    return dout * c + pltpu.roll(dout * sa, QK_ROPE // 2, 1) + pltpu.roll(dout * sb, LANES - QK_ROPE // 2, 1)


def _rms_small(x, g):
    r = lax.rsqrt(jnp.mean(x * x, axis=-1, keepdims=True) + EPS)
    xh = x * r
    return xh, r, xh * g


def _mla_prep_fwd(zs, pos, invf, qg, kvg, wuq_p, wukv):
    T = zs.shape[0]
    tm = _tile(T, 512)
    HW = MLA_HEADS * HEAD_PAD

    def body(zs_ref, pos_ref, invf_ref, qg_ref, kvg_ref, wq_ref, wkv_ref, q_ref, k_ref, v_ref):
        tabs = _rope_tables(pos_ref, invf_ref)
        _, _, cqn = _rms_small(zs_ref[:, 0:Q_LORA], qg_ref[...])
        _, _, ckvn = _rms_small(zs_ref[:, Q_LORA:Q_LORA + KV_LORA], kvg_ref[...])
        q = _dot_nt(cqn.astype(BF16), wq_ref[...]) * ATTN_SCALE
        kv = _dot(ckvn.astype(BF16), wkv_ref[...])
        kr = _rope(zs_ref[:, Q_LORA + KV_LORA:ZS_W], tabs).astype(BF16)
        for h in range(MLA_HEADS):
            b0 = h * HEAD_PAD
            q_ref[:, b0:b0 + QK_NOPE] = q[:, b0:b0 + QK_NOPE].astype(BF16)
            q_ref[:, b0 + QK_NOPE:b0 + HEAD_PAD] = _rope(q[:, b0 + QK_NOPE:b0 + HEAD_PAD], tabs).astype(BF16)
            k_ref[:, b0:b0 + QK_NOPE] = kv[:, b0:b0 + QK_NOPE].astype(BF16)
            k_ref[:, b0 + QK_NOPE:b0 + HEAD_PAD] = kr
            v_ref[:, h * V_HEAD:(h + 1) * V_HEAD] = kv[:, b0 + QK_NOPE:b0 + HEAD_PAD].astype(BF16)

    full = lambda a: pl.BlockSpec(a.shape, lambda i: (0,) * a.ndim)
    return pl.pallas_call(
        body, grid=(T // tm,),
        in_specs=[pl.BlockSpec((tm, ZS_W), lambda i: (i, 0)), pl.BlockSpec((tm, 1), lambda i: (i, 0)),
                  full(invf), full(qg), full(kvg), full(wuq_p), full(wukv)],
        out_specs=[pl.BlockSpec((tm, HW), lambda i: (i, 0)), pl.BlockSpec((tm, HW), lambda i: (i, 0)),
                   pl.BlockSpec((tm, D_MODEL), lambda i: (i, 0))],
        out_shape=[jax.ShapeDtypeStruct((T, HW), BF16), jax.ShapeDtypeStruct((T, HW), BF16),
                   jax.ShapeDtypeStruct((T, D_MODEL), BF16)],
        name="mla_prep_fwd", compiler_params=_params("parallel"))(zs, pos, invf, qg, kvg, wuq_p, wukv)


def _mla_prep_bwd(zs, pos, invf, qg, kvg, wuq_p, wukv, dq, dk, dv):
    T = zs.shape[0]
    tm = _tile(T, 256)
    HW = MLA_HEADS * HEAD_PAD

    def body(zs_ref, pos_ref, invf_ref, qg_ref, kvg_ref, wq_ref, wkv_ref, dq_ref, dk_ref, dv_ref,
             dzs_ref, cqn_ref, dqp_ref, ckvn_ref, dkv_ref, dqg_ref, dkvg_ref):
        @pl.when(pl.program_id(0) == 0)
        def _():
            dqg_ref[...] = jnp.zeros_like(dqg_ref)
            dkvg_ref[...] = jnp.zeros_like(dkvg_ref)

        tabs = _rope_tables(pos_ref, invf_ref)
        cqh, rq, cqn = _rms_small(zs_ref[:, 0:Q_LORA], qg_ref[...])
        ckvh, rkv, ckvn = _rms_small(zs_ref[:, Q_LORA:Q_LORA + KV_LORA], kvg_ref[...])
        cqn_ref[...] = cqn.astype(BF16)
        ckvn_ref[...] = ckvn.astype(BF16)
        dkr = jnp.zeros((tm, LANES), F32)
        for h in range(MLA_HEADS):
            b0 = h * HEAD_PAD
            dqp_ref[:, b0:b0 + QK_NOPE] = dq_ref[:, b0:b0 + QK_NOPE]
            dqp_ref[:, b0 + QK_NOPE:b0 + HEAD_PAD] = _rope_t(
                dq_ref[:, b0 + QK_NOPE:b0 + HEAD_PAD].astype(F32), tabs).astype(BF16)
            dkv_ref[:, b0:b0 + QK_NOPE] = dk_ref[:, b0:b0 + QK_NOPE]
            dkv_ref[:, b0 + QK_NOPE:b0 + HEAD_PAD] = dv_ref[:, h * V_HEAD:(h + 1) * V_HEAD]
            dkr = dkr + dk_ref[:, b0 + QK_NOPE:b0 + HEAD_PAD].astype(F32)
        dcqn = _dot(dqp_ref[...], wq_ref[...])
        dckvn = _dot_nt(dkv_ref[...], wkv_ref[...])
        dqg_ref[...] += jnp.sum(dcqn * cqh, axis=0, keepdims=True)
        dkvg_ref[...] += jnp.sum(dckvn * ckvh, axis=0, keepdims=True)
        dxh = dcqn * qg_ref[...]
        dzs_ref[:, 0:Q_LORA] = (rq * (dxh - cqh * jnp.mean(dxh * cqh, axis=-1, keepdims=True))).astype(BF16)
        dxh = dckvn * kvg_ref[...]
        dzs_ref[:, Q_LORA:Q_LORA + KV_LORA] = (
            rkv * (dxh - ckvh * jnp.mean(dxh * ckvh, axis=-1, keepdims=True))).astype(BF16)
        dzs_ref[:, Q_LORA + KV_LORA:ZS_W] = _rope_t(dkr, tabs).astype(BF16)

    full = lambda a: pl.BlockSpec(a.shape, lambda i: (0,) * a.ndim)
    rowb = lambda w: pl.BlockSpec((tm, w), lambda i: (i, 0))
    return pl.pallas_call(
        body, grid=(T // tm,),
        in_specs=[rowb(ZS_W), rowb(1), full(invf), full(qg), full(kvg), full(wuq_p), full(wukv),
                  rowb(HW), rowb(HW), rowb(D_MODEL)],
        out_specs=[rowb(ZS_W), rowb(Q_LORA), rowb(HW), rowb(KV_LORA), rowb(HW), full(qg), full(kvg)],
        out_shape=[jax.ShapeDtypeStruct((T, ZS_W), BF16), jax.ShapeDtypeStruct((T, Q_LORA), BF16),
                   jax.ShapeDtypeStruct((T, HW), BF16), jax.ShapeDtypeStruct((T, KV_LORA), BF16),
                   jax.ShapeDtypeStruct((T, HW), BF16), jax.ShapeDtypeStruct(qg.shape, F32),
                   jax.ShapeDtypeStruct(kvg.shape, F32)],
        name="mla_prep_bwd", compiler_params=_params("arbitrary"))(
            zs, pos, invf, qg, kvg, wuq_p, wukv, dq, dk, dv)


def _causal(tq, kmax, q0):
    r = lax.broadcasted_iota(jnp.int32, (tq, kmax), 0) + q0
    c = lax.broadcasted_iota(jnp.int32, (tq, kmax), 1)
    return c <= r


def _attn_fwd(q, k, v, batch, seq):
    tq = _tile(seq, ATTN_TILE)
    nq = seq // tq

    def body(q_ref, k_ref, v_ref, o_ref, lse_ref):
        diag = _causal(tq, tq, 0)
        for qi in range(nq):
            rows = slice(qi * tq, (qi + 1) * tq)
            qr = q_ref[rows, :]
            s_d = jnp.where(diag, _dot_nt(qr, k_ref[rows, :]), NEG_BIG)
            m = jnp.max(s_d, axis=-1, keepdims=True)
            if qi > 0:
                before = slice(0, qi * tq)
                s_b = _dot_nt(qr, k_ref[before, :])
                m = jnp.maximum(m, jnp.max(s_b, axis=-1, keepdims=True))
                p_b = jnp.exp(s_b - m)
                l = jnp.sum(p_b, axis=-1, keepdims=True)
                acc = _dot(p_b.astype(BF16), v_ref[before, :])
            p_d = jnp.exp(s_d - m)
            l_d = jnp.sum(p_d, axis=-1, keepdims=True)
            acc_d = _dot(p_d.astype(BF16), v_ref[rows, :])
            l, acc = (l + l_d, acc + acc_d) if qi > 0 else (l_d, acc_d)
            o_ref[rows, :] = (acc / l).astype(KEPT)
            lse_ref[rows, :] = jnp.broadcast_to(m + jnp.log(l), (tq, V_HEAD))

    return pl.pallas_call(
        body, grid=(batch, MLA_HEADS),
        in_specs=[pl.BlockSpec((seq, HEAD_PAD), lambda b, h: (b, h)),
                  pl.BlockSpec((seq, HEAD_PAD), lambda b, h: (b, h)),
                  pl.BlockSpec((seq, V_HEAD), lambda b, h: (b, h))],
        out_specs=[pl.BlockSpec((seq, V_HEAD), lambda b, h: (b, h)),
                   pl.BlockSpec((seq, V_HEAD), lambda b, h: (b, h))],
        out_shape=[jax.ShapeDtypeStruct((batch * seq, D_MODEL), KEPT),
                   jax.ShapeDtypeStruct((batch * seq, D_MODEL), F32)],
        name="attn_fwd", compiler_params=_params("parallel", "parallel"))(q, k, v)


def _attn_bwd(q, k, v, o, do, lse, batch, seq, dep):
    tq = _tile(seq, ATTN_TILE)
    nq = seq // tq

    def body(q_ref, k_ref, v_ref, o_ref, do_ref, lse_ref, dep_ref, dq_ref, dk_ref, dv_ref, dk_acc, dv_acc):
        dk_acc[...] = jnp.zeros_like(dk_acc)
        dv_acc[...] = jnp.zeros_like(dv_acc)
        for qi in range(nq):
            rows = slice(qi * tq, (qi + 1) * tq)
            kmax = (qi + 1) * tq
            qr = q_ref[rows, :]
            dor = do_ref[rows, :]
            kk = k_ref[0:kmax, :]
            s = _dot_nt(qr, kk)
            p = jnp.where(_causal(tq, kmax, qi * tq), jnp.exp(s - lse_ref[rows, 0:1]), 0.0)
            dp = _dot_nt(dor, v_ref[0:kmax, :])
            delta = jnp.sum(dor.astype(F32) * o_ref[rows, :].astype(F32), axis=-1, keepdims=True)
            ds = (p * (dp - delta)).astype(BF16)
            dq_ref[rows, :] = (_dot(ds, kk) * ATTN_SCALE).astype(BF16)
            dk_acc[0:kmax, :] += _dot_tn(ds, qr)
            dv_acc[0:kmax, :] += _dot_tn(p.astype(BF16), dor)
        dk_ref[...] = dk_acc[...].astype(BF16)
        dv_ref[...] = dv_acc[...].astype(BF16)

    qspec = pl.BlockSpec((seq, HEAD_PAD), lambda b, h: (b, h))
    vspec = pl.BlockSpec((seq, V_HEAD), lambda b, h: (b, h))
    T = batch * seq
    return pl.pallas_call(
        body, grid=(batch, MLA_HEADS),
        in_specs=[qspec, qspec, vspec, vspec, vspec, vspec, ANY],
        out_specs=[qspec, qspec, vspec],
        out_shape=[jax.ShapeDtypeStruct((T, MLA_HEADS * HEAD_PAD), BF16),
                   jax.ShapeDtypeStruct((T, MLA_HEADS * HEAD_PAD), BF16),
                   jax.ShapeDtypeStruct((T, D_MODEL), BF16)],
        scratch_shapes=[pltpu.VMEM((seq, HEAD_PAD), F32), pltpu.VMEM((seq, V_HEAD), F32)],
        name="attn_bwd", compiler_params=_params("parallel", "parallel"))(q, k, v, o, do, lse, dep)


def _merge_out(x, yag, zm, o, w_out, ffn_g):
    T = x.shape[0]
    tm = _tile(T, 512)

    def body(x_ref, ya_ref, gb_ref, o_ref, w_ref, g_ref, mg_ref, x1_ref, h2_ref):
        mg = (ya_ref[...].astype(F32) + _sigmoid(gb_ref[...].astype(F32)) * o_ref[...].astype(F32)).astype(BF16)
        mg_ref[...] = mg
        x1 = x_ref[...] + _dot(mg, w_ref[...])
        x1_ref[...] = x1
        r = lax.rsqrt(jnp.mean(x1 * x1, axis=-1, keepdims=True) + EPS)
        h2_ref[...] = (x1 * r * g_ref[...]).astype(BF16)

    row = pl.BlockSpec((tm, D_MODEL), lambda i: (i, 0))
    return pl.pallas_call(
        body, grid=(T // tm,),
        in_specs=[row, row, pl.BlockSpec((tm, D_MODEL), lambda i: (i, 3)), row,
                  pl.BlockSpec((D_MODEL, D_MODEL), lambda i: (0, 0)), pl.BlockSpec((1, D_MODEL), lambda i: (0, 0))],
        out_specs=[row, row, row],
        out_shape=[jax.ShapeDtypeStruct((T, D_MODEL), BF16), jax.ShapeDtypeStruct((T, D_MODEL), F32),
                   jax.ShapeDtypeStruct((T, D_MODEL), BF16)],
        name="merge_out", compiler_params=_params("parallel"))(x, yag, zm, o, w_out, ffn_g)


FF_TILE = 256
FF_BLOCKS = D_FF // FF_TILE
FFB_TILE = 256


def _shift_down(x, k):
    row = lax.broadcasted_iota(jnp.int32, x.shape, 0)
    return jnp.where(row >= k, pltpu.roll(x, k, 0), 0.0)


def _shift_up(x, k):
    n = x.shape[0]
    row = lax.broadcasted_iota(jnp.int32, x.shape, 0)
    return jnp.where(row < n - k, pltpu.roll(x, n - k, 0), 0.0)


def _conv(x, w_ref, b_ref):
    return b_ref[...] + w_ref[2:3, :] * x + w_ref[1:2, :] * _shift_down(x, 1) + w_ref[0:1, :] * _shift_down(x, 2)


def _up_act(h2, wt_up, cw, cb, batch, seq):
    def body(h_ref, wug_ref, wuv_ref, wg_ref, wv_ref, bg_ref, bv_ref, ug_ref, uv_ref, g_ref, v_ref, a_ref):
        h = h_ref[...]
        ug = _dot_nt(h, wug_ref[...])
        uv = _dot_nt(h, wuv_ref[...])
        ug_ref[...] = ug.astype(KEPT)
        uv_ref[...] = uv.astype(KEPT)
        gate = _conv(ug, wg_ref, bg_ref)
        val = _conv(uv, wv_ref, bv_ref)
        g_ref[...] = gate.astype(KEPT)
        v_ref[...] = val.astype(KEPT)
        a_ref[...] = (gate * _sigmoid(gate) * val).astype(BF16)

    blk = pl.BlockSpec((seq, FF_TILE), lambda b, j: (b, j))
    wup = lambda off: pl.BlockSpec((FF_TILE, D_MODEL), lambda b, j: (j + off, 0))
    wsp = lambda off: pl.BlockSpec((3, FF_TILE), lambda b, j: (0, j + off))
    bsp = lambda off: pl.BlockSpec((1, FF_TILE), lambda b, j: (0, j + off))
    T = batch * seq
    kept = jax.ShapeDtypeStruct((T, D_FF), KEPT)
    return pl.pallas_call(
        body, grid=(batch, FF_BLOCKS),
        in_specs=[pl.BlockSpec((seq, D_MODEL), lambda b, j: (b, 0)), wup(0), wup(FF_BLOCKS),
                  wsp(0), wsp(FF_BLOCKS), bsp(0), bsp(FF_BLOCKS)],
        out_specs=[blk] * 5,
        out_shape=[kept, kept, kept, kept, jax.ShapeDtypeStruct((T, D_FF), BF16)],
        name="up_act", compiler_params=_params("parallel", "arbitrary"))(h2, wt_up, wt_up, cw, cw, cb, cb)


def _ffn_act_bwd(upg, upv, gate, val, cw, dx2b, w_down, batch, seq):
    def half(du, x, w_ref, dx_ref, dw_ref):
        j = pl.program_id(1)
        up1, up2 = _shift_up(du, 1), _shift_up(du, 2)
        dx_ref[...] = (w_ref[2:3, :] * du + w_ref[1:2, :] * up1 + w_ref[0:1, :] * up2).astype(BF16)
        dw_ref[j, 2:3, :] += jnp.sum(du * x, axis=0, keepdims=True)
        dw_ref[j, 1:2, :] += jnp.sum(up1 * x, axis=0, keepdims=True)
        dw_ref[j, 0:1, :] += jnp.sum(up2 * x, axis=0, keepdims=True)
        dw_ref[j, 3:4, :] += jnp.sum(du, axis=0, keepdims=True)

    def body(ug_ref, uv_ref, g_ref, v_ref, wg_ref, wv_ref, dx_ref, wd_ref, dg_ref, dv_ref, dwg_ref, dwv_ref):
        @pl.when((pl.program_id(0) == 0) & (pl.program_id(1) == 0))
        def _():
            dwg_ref[...] = jnp.zeros_like(dwg_ref)
            dwv_ref[...] = jnp.zeros_like(dwv_ref)

        gate, val = g_ref[...].astype(F32), v_ref[...].astype(F32)
        sg = _sigmoid(gate)
        dav = _dot_nt(dx_ref[...], wd_ref[...])
        half(dav * val * sg * (1.0 + gate * (1.0 - sg)), ug_ref[...].astype(F32), wg_ref, dg_ref, dwg_ref)
        half(dav * gate * sg, uv_ref[...].astype(F32), wv_ref, dv_ref, dwv_ref)

    nb = D_FF // FFB_TILE
    blk = pl.BlockSpec((seq, FFB_TILE), lambda b, j: (b, j))
    wsp = lambda off: pl.BlockSpec((3, FFB_TILE), lambda b, j: (0, j + off))
    acc = pl.BlockSpec((nb, 4, FFB_TILE), lambda b, j: (0, 0, 0))
    T = batch * seq
    dupg, dupv, dwg, dwv = pl.pallas_call(
        body, grid=(batch, nb),
        in_specs=[blk, blk, blk, blk, wsp(0), wsp(nb),
                  pl.BlockSpec((seq, D_MODEL), lambda b, j: (b, 0)),
                  pl.BlockSpec((FFB_TILE, D_MODEL), lambda b, j: (j, 0))],
        out_specs=[blk, blk, acc, acc],
        out_shape=[jax.ShapeDtypeStruct((T, D_FF), BF16), jax.ShapeDtypeStruct((T, D_FF), BF16),
                   jax.ShapeDtypeStruct((nb, 4, FFB_TILE), F32), jax.ShapeDtypeStruct((nb, 4, FFB_TILE), F32)],
        name="ffn_act_bwd", compiler_params=_params("arbitrary", "arbitrary"))(
            upg, upv, gate, val, cw, cw, dx2b, w_down)
    dwg, dwv = (jnp.transpose(a, (1, 0, 2)).reshape(4, D_FF) for a in (dwg, dwv))
    return dupg, dupv, dwg[:3], dwv[:3], dwg[3:], dwv[3:]


def _down_loss(a, w_down, x1, target, gfin):
    T = x1.shape[0]
    tm = _tile(T, 512)

    def body(a_ref, w_ref, x1_ref, t_ref, g_ref, dx_ref, dxb_ref, loss_ref, dg_ref):
        @pl.when(pl.program_id(0) == 0)
        def _():
            loss_ref[...] = jnp.zeros_like(loss_ref)
            dg_ref[...] = jnp.zeros_like(dg_ref)

        x2 = x1_ref[...] + _dot(a_ref[...], w_ref[...])
        r = lax.rsqrt(jnp.mean(x2 * x2, axis=-1, keepdims=True) + EPS)
        xh = x2 * r
        g = g_ref[...]
        diff = xh * g - t_ref[...]
        loss_ref[...] += 0.5 * jnp.sum(jnp.mean(diff * diff, axis=-1, keepdims=True))
        dy = diff * (1.0 / D_MODEL)
        dg_ref[...] += jnp.sum(dy * xh, axis=0, keepdims=True)
        dxh = dy * g
        dx = r * (dxh - xh * jnp.mean(dxh * xh, axis=-1, keepdims=True))
        dx_ref[...] = dx
        dxb_ref[...] = dx.astype(BF16)

    row = pl.BlockSpec((tm, D_MODEL), lambda i: (i, 0))
    vec = pl.BlockSpec((1, D_MODEL), lambda i: (0, 0))
    return pl.pallas_call(
        body, grid=(T // tm,),
        in_specs=[pl.BlockSpec((tm, D_FF), lambda i: (i, 0)),
                  pl.BlockSpec((D_FF, D_MODEL), lambda i: (0, 0)), row, row, vec],
        out_specs=[row, row, pl.BlockSpec((8, LANES), lambda i: (0, 0)), vec],
        out_shape=[jax.ShapeDtypeStruct((T, D_MODEL), F32), jax.ShapeDtypeStruct((T, D_MODEL), BF16),
                   jax.ShapeDtypeStruct((8, LANES), F32), jax.ShapeDtypeStruct((1, D_MODEL), F32)],
        name="down_loss", compiler_params=_params("arbitrary"))(a, w_down, x1, target, gfin)


def _local_step(x, positions, target, mix_norm, av_g, av_b, w_s, b_s, q_norm, kv_norm, ffn_norm, conv_b,
                final_norm, comm):
    batch, seq, _ = x.shape
    T = batch * seq
    x = x.reshape(T, D_MODEL)
    target = target.reshape(T, D_MODEL)
    pos = positions.reshape(T, 1)
    half = jnp.arange(0, QK_ROPE, 2, dtype=F32) / QK_ROPE
    inv_freq = 1.0 / (ROPE_THETA ** half)
    invf = jnp.concatenate([inv_freq, inv_freq, jnp.zeros((LANES - QK_ROPE,), F32)]).reshape(1, LANES)
    w_st = jnp.swapaxes(w_s, 1, 2)
    b_col = b_s.reshape(A_GROUPS, CHUNK, 1)

    wt_in = comm.in_weights()
    h, zm, zs = _in_proj(x, mix_norm, wt_in)
    yag = _mixer_a_fwd(zm, av_g, av_b, w_s, b_col)
    wuq_p, wukv, w_out = comm.mla_weights(after=yag)
    q, k, v = _mla_prep_fwd(zs, pos, invf, q_norm, kv_norm, wuq_p, wukv)
    o, lse = _attn_fwd(q, k, v, batch, seq)
    merged, x1, h2 = _merge_out(x, yag, zm, o, w_out, ffn_norm)
    wt_up, conv_w, w_down = comm.ffn_weights(after=merged)
    upg, upv, gate, val, act = _up_act(h2, wt_up, conv_w, conv_b, batch, seq)
    dx2, dx2b, loss_acc, d_final = _down_loss(act, w_down, x1, target, final_norm)

    d_wdown = _mm_tn(act, dx2b, "dw_down")
    dupg, dupv, dcwg, dcwv, dcbg, dcbv = _ffn_act_bwd(upg, upv, gate, val, conv_w, dx2b, w_down, batch, seq)
    d_wt_up = _mm_tn(dupv, h2, "dw_up_val", rows=2 * D_FF, row0=D_FF,
                     into=_mm_tn(dupg, h2, "dw_up_gate", rows=2 * D_FF))
    dx1, d_ffn_norm, dmerged = _proj_bwd(
        [dupg, dupv], wt_up, [(0, (0, D_FF), (0, D_FF)), (1, (0, D_FF), (D_FF, 2 * D_FF))],
        x1, ffn_norm, dx2, "up_proj_bwd", w2=w_out)
    d_wout = _mm_tn(merged, dx1, "dw_out")
    token = comm.send_ffn_grads(d_wdown, d_wt_up, jnp.concatenate([dcwg, dcwv], axis=1), d_wout)
    dzm, do, d_avg, d_avb, d_ws, d_bs = _mixer_bwd(zm, o, dmerged, av_g, av_b, w_s, w_st, b_col, token)
    token = comm.send_small_grads([
        d_avg, d_avb, _small_2d(d_ws), d_bs.reshape(A_GROUPS, CHUNK), d_ffn_norm,
        jnp.concatenate([dcbg, dcbv], axis=1), d_final])
    dq, dk, dv = _attn_bwd(q, k, v, o, do, lse, batch, seq, token)
    dzs, cqn, dqp, ckvn, dkv, d_qn, d_kvn = _mla_prep_bwd(zs, pos, invf, q_norm, kv_norm, wuq_p, wukv, dq, dk, dv)
    d_wt_main = _mm_tn(dzm, h, "dw_in_main")
    d_wt_zs = _mm_tn(dzs, h, "dw_in_small")
    token = comm.send_in_grads(d_wt_main, d_wt_zs)
    d_wuq_p = _mm_tn(dqp, cqn, "dw_uq", dep=token)
    d_wukv = _mm_tn(ckvn, dkv, "dw_ukv", dep=token)
    terms = [(0, (i * D_MODEL, (i + 1) * D_MODEL), rows) for i, rows in enumerate(IN_ROWS_MAIN)]
    terms.append((1, (0, ZS_W), IN_ROWS_ZS))
    dx, d_mix_norm = _proj_bwd([dzm, dzs], wt_in, terms, x, mix_norm, dx1, "in_proj_bwd", dep=token, rows=512)
    token = comm.send_late_grads(d_wuq_p, d_wukv, [d_qn, d_kvn, d_mix_norm, loss_acc])
    return dx.reshape(batch, seq, D_MODEL), token


MESH_ID = pl.DeviceIdType.MESH
EFFECT = pltpu.SideEffectType.DATAFLOW_SIDE_EFFECTING


def _mesh_pos():
    return lax.axis_index("x"), lax.axis_index("y"), lax.axis_index("c")


def _peer(pos, d):
    x, y, c = pos
    px = 1 - x if d & 4 else x
    py = 1 - y if d & 2 else y
    pc = 1 - c if d & 1 else c
    return (px, py, pc), 4 * px + 2 * py + pc


def _copy(src_ref, land_ref, send_sems, recv_sems, a, d, pos, exchange, landing_here):
    peer, pid = _peer(pos, d)
    me = 4 * pos[0] + 2 * pos[1] + pos[2]
    if exchange:
        src, dst = src_ref.at[pid], land_ref.at[d]
    else:
        src, dst = src_ref, land_ref.at[pid if landing_here else me]
    return pltpu.make_async_remote_copy(
        src_ref=src, dst_ref=dst, send_sem=send_sems.at[a * (N_DEV - 1) + d - 1],
        recv_sem=recv_sems.at[a * (N_DEV - 1) + d - 1],
        device_id=peer, device_id_type=MESH_ID)


def _start_copies(groups, modes, name, dep=None):
    sizes = [len(g) for g in groups]
    srcs = [s for g in groups for s in g]
    lands = [lax.empty(s.shape if modes[gi] else (N_DEV,) + s.shape, s.dtype)
             for gi, g in enumerate(groups) for s in g]
    n, ng = len(srcs), len(groups)
    n_in = 2 * n + (dep is not None)

    def body(*refs):
        src_refs, land_refs = refs[:n], refs[n:2 * n]
        sems = refs[n_in:n_in + 3 * ng]
        token = refs[-1]
        pos = _mesh_pos()
        k = 0
        for gi, size in enumerate(sizes):
            for a in range(size):
                _own_copy(src_refs[k], land_refs[k], sems[3 * gi + 2], a, pos, modes[gi]).start()
                for d in range(1, N_DEV):
                    _copy(src_refs[k], land_refs[k], sems[3 * gi], sems[3 * gi + 1], a, d, pos, modes[gi],
                          landing_here=False).start()
                k += 1
        token[...] = jnp.zeros_like(token)

    sem_shapes = []
    for size in sizes:
        remote = pltpu.SemaphoreType.DMA((size * (N_DEV - 1),))
        sem_shapes += [remote, remote, pltpu.SemaphoreType.DMA((size,))]
    out = pl.pallas_call(
        body, name=name,
        out_shape=(*sem_shapes, *[pltpu.HBM(a.shape, a.dtype) for a in srcs + lands],
                   jax.ShapeDtypeStruct((8, LANES), F32)),
        in_specs=[HBM] * (2 * n) + [ANY] * (dep is not None),
        out_specs=(*[SEM] * (3 * ng), *[HBM] * (2 * n), pl.BlockSpec(memory_space=pltpu.VMEM)),
        input_output_aliases={i: 3 * ng + i for i in range(2 * n)},
        compiler_params=pltpu.CompilerParams(has_side_effects=EFFECT),
    )(*[pltpu.with_memory_space_constraint(a, pltpu.HBM) for a in srcs + lands], *([dep] if dep is not None else []))
    thru = out[3 * ng:3 * ng + 2 * n]
    handles, k = [], 0
    for gi, size in enumerate(sizes):
        handles.append((out[3 * gi:3 * gi + 3], thru[k:k + size], thru[n + k:n + k + size]))
        k += size
    return handles, out[-1]


def _own_copy(src_ref, land_ref, local_sems, a, pos, exchange):
    me = 4 * pos[0] + 2 * pos[1] + pos[2]
    src, dst = (src_ref.at[me], land_ref.at[0]) if exchange else (src_ref, land_ref.at[me])
    return pltpu.make_async_copy(src, dst, local_sems.at[a])


def _wait_copies(handle, exchange, after, name):
    sems, srcs, lands = handle
    n = len(srcs)

    def body(*refs):
        src_refs, land_refs = refs[:n], refs[n:2 * n]
        send, recv, local = refs[2 * n:2 * n + 3]
        pos = _mesh_pos()
        for a in range(n):
            _own_copy(src_refs[a], land_refs[a], local, a, pos, exchange).wait()
            for d in range(1, N_DEV):
                cp = _copy(src_refs[a], land_refs[a], send, recv, a, d, pos, exchange, landing_here=True)
                cp.wait_send()
                cp.wait_recv()

    out = pl.pallas_call(
        body, name=name,
        out_shape=tuple(pltpu.HBM(a.shape, a.dtype) for a in (*srcs, *lands)),
        in_specs=[HBM] * (2 * n) + [SEM, SEM, SEM, ANY], out_specs=[HBM] * (2 * n),
        input_output_aliases={i: i for i in range(2 * n)},
        compiler_params=pltpu.CompilerParams(has_side_effects=EFFECT),
    )(*srcs, *lands, *sems, after)
    return out[n:]


def _gather_now(a, name):
    def body(x_ref, out_ref, send_sems, recv_sems, local_sem):
        x, y, c = _mesh_pos()
        me, sibling = (x, y, c), (x, y, 1 - c)
        chips = [(1 - x, y), (x, 1 - y), (1 - x, 1 - y)]

        def slot(p):
            return out_ref.at[4 * p[0] + 2 * p[1] + p[2]]

        def copy(k, block, to, src=None):
            return pltpu.make_async_remote_copy(
                src_ref=slot(block) if src is None else src, dst_ref=slot(block), send_sem=send_sems.at[k],
                recv_sem=recv_sems.at[k], device_id=to, device_id_type=MESH_ID)

        mine = pltpu.make_async_copy(x_ref, slot(me), local_sem)
        mine.start()
        first = [copy(0, me, sibling, src=x_ref)]
        first += [copy(1 + j, me, (*chip, c), src=x_ref) for j, chip in enumerate(chips)]
        for cp in first:
            cp.start()
        passed = [copy(4 + j, (*chip, c), sibling) for j, chip in enumerate(chips)]
        for j, chip in enumerate(chips):
            copy(1 + j, (*chip, c), me).wait_recv()
            passed[j].start()
        copy(0, sibling, me).wait_recv()
        for j, chip in enumerate(chips):
            copy(4 + j, (*chip, 1 - c), me).wait_recv()
        for cp in first + passed:
            cp.wait_send()
        mine.wait()

    return pl.pallas_call(
        body, in_specs=[ANY], out_specs=ANY,
        out_shape=jax.ShapeDtypeStruct((N_DEV,) + a.shape, a.dtype),
        scratch_shapes=[pltpu.SemaphoreType.DMA((N_DEV - 1,)), pltpu.SemaphoreType.DMA((N_DEV - 1,)),
                        pltpu.SemaphoreType.DMA],
        name=name, compiler_params=pltpu.CompilerParams(has_side_effects=True))(a)


def _sum_parts(p_ref):
    g = p_ref[0].astype(F32)
    for k in range(1, N_DEV):
        g = g + p_ref[k].astype(F32)
    return g


def _adamw_update(p_ref, w_ref, m_ref, v_ref, g_ref, d_ref, nm_ref, nv_ref):
    c1 = 1.0 - ADAM_B1 ** ADAM_STEP
    c2 = 1.0 - ADAM_B2 ** ADAM_STEP
    g = _sum_parts(p_ref)
    nm = ADAM_B1 * m_ref[...] + (1.0 - ADAM_B1) * g
    nv = ADAM_B2 * v_ref[...] + (1.0 - ADAM_B2) * (g * g)
    g_ref[...] = g
    nm_ref[...] = nm
    nv_ref[...] = nv
    d_ref[...] = -ADAM_LR * ((nm / c1) / (jnp.sqrt(nv / c2) + ADAM_EPS) + ADAM_WD * w_ref[...])


def _adamw_many(parts, ws, ms, vs, sums, name):
    n, ns = len(ws), len(sums)

    def body(*refs):
        ins, outs = refs[:4 * n + ns], refs[4 * n + ns:]
        for i in range(n):
            _adamw_update(ins[i], ins[n + i], ins[2 * n + i], ins[3 * n + i],
                          outs[i], outs[n + i], outs[2 * n + i], outs[3 * n + i])
        for i in range(ns):
            outs[4 * n + i][...] = _sum_parts(ins[4 * n + i])

    full = lambda a: pl.BlockSpec(a.shape, lambda: (0,) * a.ndim)
    args = [*parts, *ws, *ms, *vs, *sums]
    outs = [jax.ShapeDtypeStruct(w.shape, F32) for _ in range(4) for w in ws]
    outs += [jax.ShapeDtypeStruct(s.shape[1:], F32) for s in sums]
    res = pl.pallas_call(
        body, in_specs=[full(a) for a in args], out_specs=[full(o) for o in outs], out_shape=outs,
        name=name, compiler_params=pltpu.CompilerParams(vmem_limit_bytes=VMEM_LIMIT))(*args)
    return res[:n], res[n:2 * n], res[2 * n:3 * n], res[3 * n:4 * n], res[4 * n:]


def _adamw(parts, w, m, v, name):
    R, C = w.shape
    tr, tc = R, C
    if N_DEV * R * C * parts.dtype.itemsize > SMALL_BLOCK_BYTES:
        tr = next((t for t in range(min(R, 256) // 16 * 16, 15, -16) if R % t == 0), R)
        if tr == R:
            tc = _tile(C, 256)

    def body(p_ref, w_ref, m_ref, v_ref, g_ref, d_ref, nm_ref, nv_ref):
        _adamw_update(p_ref, w_ref, m_ref, v_ref, g_ref, d_ref, nm_ref, nv_ref)

    blk = pl.BlockSpec((tr, tc), lambda i, j: (i, j))
    shp = jax.ShapeDtypeStruct((R, C), F32)
    return pl.pallas_call(
        body, grid=(R // tr, C // tc),
        in_specs=[pl.BlockSpec((N_DEV, tr, tc), lambda i, j: (0, i, j)), blk, blk, blk],
        out_specs=[blk, blk, blk, blk], out_shape=[shp, shp, shp, shp],
        name=name, compiler_params=_params("parallel", "parallel"))(parts, w, m, v)


SPLIT_V = 2 * D_MODEL
SPLIT_KR = SPLIT_V + Q_LORA + KV_LORA + QK_ROPE
IN_DIM = SPLIT_KR + 2 * D_MODEL
IN_ROWS_MAIN = ((0, D_MODEL), (D_MODEL, SPLIT_V), (SPLIT_KR, SPLIT_KR + D_MODEL), (SPLIT_KR + D_MODEL, IN_DIM))
IN_ROWS_ZS = (SPLIT_V, SPLIT_V + ZS_W)

SMALL_EARLY = ("a_v_norm_g", "a_v_norm_b", "a_spatial_w", "a_spatial_b", "ffn_norm", "conv_b", "final_norm")
SMALL_LATE = ("q_a_norm", "kv_a_norm", "mix_norm")


def _small_2d(a):
    return a.reshape(-1, a.shape[-1])


def _cols_from_shards(g):
    return jnp.transpose(g, (1, 0, 2)).reshape(g.shape[1], N_DEV * g.shape[2])


def _shards_from_cols(a):
    R, W = a.shape
    return jnp.transpose(a.reshape(R, N_DEV, W // N_DEV), (1, 0, 2))


class _Comm:
    GATHER_GROUPS = (("w_uq", "w_ukv", "w_out"), ("w_up", "conv_w", "w_down"))
    FFN_GRADS = ("w_down", "w_up", "conv_w", "w_out")
    LATE_GRADS = ("w_uq", "w_ukv")
    TRANSPOSED = ("w_in", "w_up", "w_uq")

    def __init__(self, shards):
        local = {n: a.astype(F32 if n == "conv_w" else BF16) for n, a in shards.items()}
        self.g_in = _gather_now(local["w_in"], "gather_w_in")
        groups = [[local[n] for n in g] for g in self.GATHER_GROUPS]
        (self.h_mla, self.h_ffn), _ = _start_copies(groups, [False] * 2, "gather_start", dep=self.g_in)

    def in_weights(self):
        return self.g_in.reshape(IN_DIM, D_MODEL)

    def mla_weights(self, after):
        g_uq, g_ukv, g_out = _wait_copies(self.h_mla, False, after, "gather_wait_mla")
        wuq_p = jnp.pad(g_uq, ((0, 0), (0, HEAD_PAD - QK_HEAD), (0, 0))).reshape(MLA_HEADS * HEAD_PAD, Q_LORA)
        return wuq_p, _cols_from_shards(g_ukv), g_out.reshape(D_MODEL, D_MODEL)

    def ffn_weights(self, after):
        g_up, g_cw, g_down = _wait_copies(self.h_ffn, False, after, "gather_wait_ffn")
        return g_up.reshape(2 * D_FF, D_MODEL), _cols_from_shards(g_cw), g_down.reshape(D_FF, D_MODEL)

    def send_ffn_grads(self, d_wdown, d_wt_up, d_convw, d_wout):
        group = [d_wdown.reshape(N_DEV, D_FF // N_DEV, D_MODEL), d_wt_up.reshape(N_DEV, 2 * D_FF // N_DEV, D_MODEL),
                 _shards_from_cols(d_convw), d_wout.reshape(N_DEV, D_MODEL // N_DEV, D_MODEL)]
        (self.h_ffn_grads,), token = _start_copies([group], [True], "ffn_grads_start")
        return token

    def send_small_grads(self, grads):
        (self.h_small_early,), token = _start_copies([grads], [False], "small_grads_start")
        return token

    def send_in_grads(self, d_wt_main, d_wt_zs):
        d_in = jnp.concatenate([d_wt_main[:SPLIT_V], d_wt_zs[:SPLIT_KR - SPLIT_V], d_wt_main[SPLIT_V:]], axis=0)
        blocks = d_in.reshape(N_DEV, IN_DIM // N_DEV, D_MODEL)
        (self.h_in_grads,), token = _start_copies([[blocks]], [True], "in_grads_start")
        return token

    def send_late_grads(self, d_wuq_p, d_wukv, small):
        d_uq = d_wuq_p.reshape(MLA_HEADS, HEAD_PAD, Q_LORA)[:, :QK_HEAD, :]
        (self.h_late_grads, self.h_late_small), token = _start_copies(
            [[d_uq, _shards_from_cols(d_wukv)], small], [True, False], "late_grads_start")
        return token


def kernel(x, positions, mix_norm, w_in, a_v_norm_g, a_v_norm_b, a_spatial_w, a_spatial_b, q_a_norm, w_uq, kv_a_norm, w_ukv, w_out, ffn_norm, w_up, conv_w, conv_b, w_down, final_norm, loss_target, m_mix_norm, m_w_in, m_a_v_norm_g, m_a_v_norm_b, m_a_spatial_w, m_a_spatial_b, m_q_a_norm, m_w_uq, m_kv_a_norm, m_w_ukv, m_w_out, m_ffn_norm, m_w_up, m_conv_w, m_conv_b, m_w_down, m_final_norm, v_mix_norm, v_w_in, v_a_v_norm_g, v_a_v_norm_b, v_a_spatial_w, v_a_spatial_b, v_q_a_norm, v_w_uq, v_kv_a_norm, v_w_ukv, v_w_out, v_ffn_norm, v_w_up, v_conv_w, v_conv_b, v_w_down, v_final_norm):
    names = ("mix_norm", "w_in", "a_v_norm_g", "a_v_norm_b", "a_spatial_w", "a_spatial_b", "q_a_norm", "w_uq",
             "kv_a_norm", "w_ukv", "w_out", "ffn_norm", "w_up", "conv_w", "conv_b", "w_down", "final_norm")
    w = dict(zip(names, (mix_norm, w_in, a_v_norm_g, a_v_norm_b, a_spatial_w, a_spatial_b, q_a_norm, w_uq,
                         kv_a_norm, w_ukv, w_out, ffn_norm, w_up, conv_w, conv_b, w_down, final_norm)))
    m = dict(zip(names, (m_mix_norm, m_w_in, m_a_v_norm_g, m_a_v_norm_b, m_a_spatial_w, m_a_spatial_b,
                         m_q_a_norm, m_w_uq, m_kv_a_norm, m_w_ukv, m_w_out, m_ffn_norm, m_w_up, m_conv_w,
                         m_conv_b, m_w_down, m_final_norm)))
    v = dict(zip(names, (v_mix_norm, v_w_in, v_a_v_norm_g, v_a_v_norm_b, v_a_spatial_w, v_a_spatial_b,
                         v_q_a_norm, v_w_uq, v_kv_a_norm, v_w_ukv, v_w_out, v_ffn_norm, v_w_up, v_conv_w,
                         v_conv_b, v_w_down, v_final_norm)))
    shapes = {n: w[n].shape for n in names}
    def view(tree, n):
        a = tree[n].reshape(tree[n].shape[-2:])
        return a.T if n in _Comm.TRANSPOSED else a

    comm = _Comm({n: view(w, n) for n in ("w_in",) + _Comm.GATHER_GROUPS[0] + _Comm.GATHER_GROUPS[1]})

    grad_x, token = _local_step(
        x, positions, loss_target, w["mix_norm"], w["a_v_norm_g"], w["a_v_norm_b"], w["a_spatial_w"][0],
        w["a_spatial_b"][0], w["q_a_norm"], w["kv_a_norm"], w["ffn_norm"], w["conv_b"],
        w["final_norm"].reshape(1, D_MODEL), comm)

    out_g, out_d, out_m, out_v = {}, {}, {}, {}

    def update(n, parts):
        res = _adamw(parts, view(w, n), view(m, n), view(v, n), "adamw_" + n)
        out_g[n], out_d[n], out_m[n], out_v[n] = (
            (t.T if n in _Comm.TRANSPOSED else t).reshape(shapes[n]) for t in res)
        return res[1]

    def update_small(names, parts, sums, name):
        res = _adamw_many(parts, *[[_small_2d(t[n]) for n in names] for t in (w, m, v)], sums, name)
        for i, n in enumerate(names):
            out_g[n], out_d[n], out_m[n], out_v[n] = (r[i].reshape(shapes[n]) for r in res[:4])
        return res

    for n, parts in zip(_Comm.FFN_GRADS, _wait_copies(comm.h_ffn_grads, True, token, "ffn_grads_wait")):
        last = update(n, parts)
    early = _wait_copies(comm.h_small_early, False, last, "small_grads_wait")
    last = update_small(SMALL_EARLY, early, [], "adamw_small")[1][0]
    last = update("w_in", _wait_copies(comm.h_in_grads, True, last, "in_grads_wait")[0])
    for n, parts in zip(_Comm.LATE_GRADS, _wait_copies(comm.h_late_grads, True, last, "late_grads_wait")):
        last = update(n, parts)
    late = _wait_copies(comm.h_late_small, False, last, "late_small_wait")
    res = update_small(SMALL_LATE, late[:-1], late[-1:], "adamw_late")
    loss = res[4][0][0, 0]

    return (loss, grad_x, *[out_g[n] for n in names], *[out_d[n] for n in names],
            *[out_m[n] for n in names], *[out_v[n] for n in names])
```

```python
import functools
import math

import jax
import jax.numpy as jnp
from jax import lax
from jax.experimental import pallas as pl
from jax.experimental.pallas import tpu as pltpu

F32 = jnp.float32
BF16 = jnp.bfloat16
KEPT = jnp.bfloat16

N_DEV = 8
D_MODEL = 1024
EPS = 1e-6
A_GROUPS = 8
CHUNK = 128
MLA_HEADS = 8
QK_NOPE = 128
QK_ROPE = 64
QK_HEAD = QK_NOPE + QK_ROPE
HEAD_PAD = 256
V_HEAD = 128
Q_LORA = 256
KV_LORA = 128
ROPE_THETA = 10000.0
D_FF = 2816
ZS_W = 512
ATTN_SCALE = QK_HEAD ** -0.5
ATTN_TILE = 512
NEG_BIG = -1e30

ADAM_LR = 0.001
ADAM_B1 = 0.9
ADAM_B2 = 0.999
ADAM_EPS = 1e-08
ADAM_WD = 0.01
ADAM_STEP = 10

VMEM_LIMIT = 56 * 1024 * 1024
SMALL_BLOCK_BYTES = 5 * 1024 * 1024
LANES = 128

GELU_K = math.sqrt(2.0 / math.pi)
GELU_C = 0.044715

ANY = pl.BlockSpec(memory_space=pl.ANY)
HBM = pl.BlockSpec(memory_space=pltpu.HBM)
SEM = pl.BlockSpec(memory_space=pltpu.SEMAPHORE)


def _tile(n, pref):
    for t in (pref, 512, 256, 128, 64, 32, 16, 8):
        if t <= pref and n % t == 0:
            return t
    return n


def _wide_tile(n, cap=1408):
    return next((t for t in range(min(n, cap) // LANES * LANES, 0, -LANES) if n % t == 0), n)


def _params(*sem):
    return pltpu.CompilerParams(dimension_semantics=sem, vmem_limit_bytes=VMEM_LIMIT)


def _dot(a, b):
    return jnp.dot(a, b, preferred_element_type=F32)


def _dot_nt(a, b):
    return lax.dot_general(a, b, (((1,), (1,)), ((), ())), preferred_element_type=F32)


def _dot_tn(a, b):
    return lax.dot_general(a, b, (((0,), (0,)), ((), ())), preferred_element_type=F32)


def _sigmoid(x):
    return 1.0 / (1.0 + jnp.exp(-x))


def _gelu(x):
    t = jnp.tanh(GELU_K * (x + GELU_C * x * x * x))
    return 0.5 * x * (1.0 + t)


def _gelu_and_grad(x):
    x2 = x * x
    t = jnp.tanh(GELU_K * (x + GELU_C * x * x2))
    half = 0.5 * (1.0 + t)
    return x * half, half + 0.5 * x * (1.0 - t * t) * GELU_K * (1.0 + 3.0 * GELU_C * x2)


def _in_proj(x, g, wt):
    T, Dm = x.shape
    tm = _tile(T, 512)

    def body(x_ref, g_ref, wt_ref, h_ref, zm_ref, zs_ref):
        xf = x_ref[...]
        r = lax.rsqrt(jnp.mean(xf * xf, axis=-1, keepdims=True) + EPS)
        h = (xf * r * g_ref[...]).astype(BF16)
        h_ref[...] = h
        for i, (r0, r1) in enumerate(IN_ROWS_MAIN):
            zm_ref[:, i * D_MODEL:(i + 1) * D_MODEL] = _dot_nt(h, wt_ref[r0:r1, :]).astype(KEPT)
        zs_ref[...] = _dot_nt(h, wt_ref[IN_ROWS_ZS[0]:IN_ROWS_ZS[1], :])

    row = lambda n: pl.BlockSpec((tm, n), lambda i: (i, 0))
    return pl.pallas_call(
        body, grid=(T // tm,),
        in_specs=[row(Dm), pl.BlockSpec((1, Dm), lambda i: (0, 0)), pl.BlockSpec(wt.shape, lambda i: (0, 0))],
        out_specs=[row(Dm), row(4 * D_MODEL), row(ZS_W)],
        out_shape=[jax.ShapeDtypeStruct((T, Dm), BF16), jax.ShapeDtypeStruct((T, 4 * D_MODEL), KEPT),
                   jax.ShapeDtypeStruct((T, ZS_W), F32)],
        name="in_proj", compiler_params=_params("parallel"))(x, g, wt)


def _proj_bwd(acts, wt, terms, x, g, dres, name, w2=None, dep=None, rows=256):
    T, Dm = x.shape
    tm = _tile(T, rows)
    n_a = len(acts)

    def body(*refs):
        ins, outs = refs[:n_a + 4 + (w2 is not None) + (dep is not None)], refs[-2 - (w2 is not None):]
        wt_ref, x_ref, g_ref, dres_ref = ins[n_a:n_a + 4]
        dx_ref, dg_ref = outs[0], outs[1]

        @pl.when(pl.program_id(0) == 0)
        def _():
            dg_ref[...] = jnp.zeros_like(dg_ref)

        dy = None
        for i, (c0, c1), (r0, r1) in terms:
            t = _dot(ins[i][:, c0:c1], wt_ref[r0:r1, :])
            dy = t if dy is None else dy + t
        xf = x_ref[...]
        r = lax.rsqrt(jnp.mean(xf * xf, axis=-1, keepdims=True) + EPS)
        xh = xf * r
        dg_ref[...] += jnp.sum(dy * xh, axis=0, keepdims=True)
        dxh = dy * g_ref[...]
        dx = dres_ref[...] + r * (dxh - xh * jnp.mean(dxh * xh, axis=-1, keepdims=True))
        dx_ref[...] = dx
        if w2 is not None:
            outs[2][...] = _dot_nt(dx.astype(BF16), ins[n_a + 4][...]).astype(KEPT)

    row = pl.BlockSpec((tm, Dm), lambda i: (i, 0))
    vec = pl.BlockSpec((1, Dm), lambda i: (0, 0))
    in_specs = [pl.BlockSpec((tm, a.shape[1]), lambda i: (i, 0)) for a in acts]
    in_specs += [pl.BlockSpec(wt.shape, lambda i: (0, 0)), row, vec, row]
    args = [*acts, wt, x, g, dres]
    out_specs = [row, vec]
    out_shape = [jax.ShapeDtypeStruct((T, Dm), F32), jax.ShapeDtypeStruct((1, Dm), F32)]
    if w2 is not None:
        in_specs.append(pl.BlockSpec(w2.shape, lambda i: (0, 0)))
        args.append(w2)
        out_specs.append(pl.BlockSpec((tm, w2.shape[0]), lambda i: (i, 0)))
        out_shape.append(jax.ShapeDtypeStruct((T, w2.shape[0]), KEPT))
    if dep is not None:
        in_specs.append(ANY)
        args.append(dep)
    return pl.pallas_call(
        body, grid=(T // tm,), in_specs=in_specs, out_specs=out_specs, out_shape=out_shape,
        name=name, compiler_params=_params("arbitrary"))(*args)


def _mm_tn(a, b, name, dep=None, rows=None, row0=0, into=None):
    T, M = a.shape
    N = b.shape[1]
    tm, tn, tt = _wide_tile(M), _wide_tile(N), _tile(T, 2048)
    n_t = T // tt
    off = row0 // tm
    extra = ([dep] if dep is not None else []) + ([into] if into is not None else [])

    def body(a_ref, b_ref, *refs):
        o_ref, acc_ref = refs[-2:]
        t = pl.program_id(2)

        @pl.when(t == 0)
        def _():
            acc_ref[...] = jnp.zeros_like(acc_ref)

        acc_ref[...] += _dot_tn(a_ref[...].astype(BF16), b_ref[...].astype(BF16))

        @pl.when(t == n_t - 1)
        def _():
            o_ref[...] = acc_ref[...].astype(BF16)

    return pl.pallas_call(
        body, grid=(M // tm, N // tn, n_t),
        in_specs=[pl.BlockSpec((tt, tm), lambda i, j, t: (t, i)),
                  pl.BlockSpec((tt, tn), lambda i, j, t: (t, j))] + [ANY] * len(extra),
        out_specs=pl.BlockSpec((tm, tn), lambda i, j, t: (i + off, j)),
        out_shape=jax.ShapeDtypeStruct((rows or M, N), BF16),
        scratch_shapes=[pltpu.VMEM((tm, tn), F32)],
        input_output_aliases={} if into is None else {1 + len(extra): 0},
        name=name, compiler_params=_params("parallel", "parallel", "arbitrary"))(a, b, *extra)


def _layer_norm_fwd(gv, g, b):
    mu = jnp.mean(gv, axis=-1, keepdims=True)
    xc = gv - mu
    rs = lax.rsqrt(jnp.mean(xc * xc, axis=-1, keepdims=True) + EPS)
    xh = xc * rs
    return xh, rs, xh * g + b


def _tri_mask(transposed=False):
    r = lax.broadcasted_iota(jnp.int32, (CHUNK, CHUNK), 0)
    c = lax.broadcasted_iota(jnp.int32, (CHUNK, CHUNK), 1)
    return r <= c if transposed else c <= r


def _mixer_a_fwd(zm, av_g, av_b, w_s, b_col):
    T = zm.shape[0]
    tm = _tile(T, 256)
    n_chunk = tm // CHUNK

    def body(u_ref, v_ref, ga_ref, g_ref, b_ref, w_ref, bc_ref, y_ref, vn_s, mx_s):
        gu = _gelu(u_ref[...].astype(F32))
        _, _, vn = _layer_norm_fwd(_gelu(v_ref[...].astype(F32)), g_ref[...], b_ref[...])
        vn_s[...] = vn.astype(BF16)
        tri = _tri_mask()
        for gi in range(A_GROUPS):
            wm = jnp.where(tri, w_ref[gi], 0.0).astype(BF16)
            cols = slice(gi * CHUNK, (gi + 1) * CHUNK)
            for n in range(n_chunk):
                rows = slice(n * CHUNK, (n + 1) * CHUNK)
                mx_s[rows, cols] = _dot(wm, vn_s[rows, cols]) + bc_ref[gi]
        y_ref[...] = (_sigmoid(ga_ref[...].astype(F32)) * gu * mx_s[...]).astype(KEPT)

    col = lambda c: pl.BlockSpec((tm, D_MODEL), lambda i: (i, c))
    vec = pl.BlockSpec((1, D_MODEL), lambda i: (0, 0))
    return pl.pallas_call(
        body, grid=(T // tm,),
        in_specs=[col(0), col(1), col(2), vec, vec,
                  pl.BlockSpec((A_GROUPS, CHUNK, CHUNK), lambda i: (0, 0, 0)),
                  pl.BlockSpec((A_GROUPS, CHUNK, 1), lambda i: (0, 0, 0))],
        out_specs=pl.BlockSpec((tm, D_MODEL), lambda i: (i, 0)),
        out_shape=jax.ShapeDtypeStruct((T, D_MODEL), KEPT),
        scratch_shapes=[pltpu.VMEM((tm, D_MODEL), BF16), pltpu.VMEM((tm, D_MODEL), F32)],
        name="mixer_a_fwd", compiler_params=_params("parallel"))(zm, zm, zm, av_g, av_b, w_s, b_col)


def _mixer_bwd(zm, o, dm, av_g, av_b, w_s, w_st, b_col, dep):
    T = zm.shape[0]
    tm = _tile(T, 256)
    n_chunk = tm // CHUNK

    def body(u_ref, v_ref, ga_ref, gb_ref, o_ref, dm_ref, g_ref, b_ref, w_ref, wt_ref, bc_ref, dep_ref,
             dz_ref, do_ref, dg_ref, db_ref, dw_ref, dbs_ref, vn_s, mx_s, dmx_s, dvn_s):
        @pl.when(pl.program_id(0) == 0)
        def _():
            dg_ref[...] = jnp.zeros_like(dg_ref)
            db_ref[...] = jnp.zeros_like(db_ref)
            dw_ref[...] = jnp.zeros_like(dw_ref)
            dbs_ref[...] = jnp.zeros_like(dbs_ref)

        dm_v = dm_ref[...].astype(F32)
        gb = gb_ref[...].astype(F32)
        sb = _sigmoid(gb)
        o_v = o_ref[...].astype(F32)
        do_ref[...] = (dm_v * sb).astype(BF16)
        dz_ref[:, 3 * D_MODEL:4 * D_MODEL] = (dm_v * o_v * sb * (1.0 - sb)).astype(BF16)
        u = u_ref[...].astype(F32)
        v = v_ref[...].astype(F32)
        gu, gu_grad = _gelu_and_grad(u)
        gv, gv_grad = _gelu_and_grad(v)
        xh, rs, vn = _layer_norm_fwd(gv, g_ref[...], b_ref[...])
        vn_s[...] = vn.astype(BF16)
        tri = _tri_mask()
        for gi in range(A_GROUPS):
            wm = jnp.where(tri, w_ref[gi], 0.0).astype(BF16)
            cols = slice(gi * CHUNK, (gi + 1) * CHUNK)
            for n in range(n_chunk):
                rows = slice(n * CHUNK, (n + 1) * CHUNK)
                mx_s[rows, cols] = _dot(wm, vn_s[rows, cols]) + bc_ref[gi]
        mixed = mx_s[...]
        sa = _sigmoid(ga_ref[...].astype(F32))
        dya = dm_v * sa
        dz_ref[:, 2 * D_MODEL:3 * D_MODEL] = (dm_v * gu * mixed * sa * (1.0 - sa)).astype(BF16)
        dz_ref[:, 0:D_MODEL] = (dya * mixed * gu_grad).astype(BF16)
        dmx = dya * gu
        dmx_s[...] = dmx.astype(BF16)
        tri_t = _tri_mask(transposed=True)
        for gi in range(A_GROUPS):
            wmt = jnp.where(tri_t, wt_ref[gi], 0.0).astype(BF16)
            cols = slice(gi * CHUNK, (gi + 1) * CHUNK)
            dw_acc = jnp.zeros((CHUNK, CHUNK), F32)
            dmx_sum = jnp.zeros((CHUNK, CHUNK), F32)
            for n in range(n_chunk):
                rows = slice(n * CHUNK, (n + 1) * CHUNK)
                blk = dmx_s[rows, cols]
                dvn_s[rows, cols] = _dot(wmt, blk)
                dw_acc = dw_acc + _dot_nt(blk, vn_s[rows, cols])
                dmx_sum = dmx_sum + dmx[rows, cols]
            dw_ref[gi] += jnp.where(tri, dw_acc, 0.0)
            dbs_ref[gi] += jnp.sum(dmx_sum, axis=-1, keepdims=True)
        dvn = dvn_s[...]
        dg_ref[...] += jnp.sum(dvn * xh, axis=0, keepdims=True)
        db_ref[...] += jnp.sum(dvn, axis=0, keepdims=True)
        dxh = dvn * g_ref[...]
        dgv = rs * (dxh - jnp.mean(dxh, axis=-1, keepdims=True)
                    - xh * jnp.mean(dxh * xh, axis=-1, keepdims=True))
        dz_ref[:, D_MODEL:2 * D_MODEL] = (dgv * gv_grad).astype(BF16)

    col = lambda c: pl.BlockSpec((tm, D_MODEL), lambda i: (i, c))
    row = pl.BlockSpec((tm, D_MODEL), lambda i: (i, 0))
    vec = pl.BlockSpec((1, D_MODEL), lambda i: (0, 0))
    wsp = pl.BlockSpec((A_GROUPS, CHUNK, CHUNK), lambda i: (0, 0, 0))
    bsp = pl.BlockSpec((A_GROUPS, CHUNK, 1), lambda i: (0, 0, 0))
    return pl.pallas_call(
        body, grid=(T // tm,),
        in_specs=[col(0), col(1), col(2), col(3), row, row, vec, vec, wsp, wsp, bsp, ANY],
        out_specs=[pl.BlockSpec((tm, 4 * D_MODEL), lambda i: (i, 0)), row, vec, vec, wsp, bsp],
        out_shape=[jax.ShapeDtypeStruct((T, 4 * D_MODEL), BF16), jax.ShapeDtypeStruct((T, D_MODEL), BF16),
                   jax.ShapeDtypeStruct((1, D_MODEL), F32), jax.ShapeDtypeStruct((1, D_MODEL), F32),
                   jax.ShapeDtypeStruct((A_GROUPS, CHUNK, CHUNK), F32),
                   jax.ShapeDtypeStruct((A_GROUPS, CHUNK, 1), F32)],
        scratch_shapes=[pltpu.VMEM((tm, D_MODEL), BF16), pltpu.VMEM((tm, D_MODEL), F32),
                        pltpu.VMEM((tm, D_MODEL), BF16), pltpu.VMEM((tm, D_MODEL), F32)],
        name="mixer_bwd", compiler_params=_params("arbitrary"))(
            zm, zm, zm, zm, o, dm, av_g, av_b, w_s, w_st, b_col, dep)


def _rope_tables(pos_ref, invf_ref):
    ang = pos_ref[...].astype(F32) * invf_ref[...]
    lane = lax.broadcasted_iota(jnp.int32, ang.shape, 1)
    cos, sin = jnp.cos(ang), jnp.sin(ang)
    c = jnp.where(lane < QK_ROPE, cos, 0.0)
    sa = jnp.where(lane < QK_ROPE // 2, -sin, 0.0)
    sb = jnp.where((lane >= QK_ROPE // 2) & (lane < QK_ROPE), sin, 0.0)
    return c, sa, sb


def _rope(blk, tabs):
    c, sa, sb = tabs
    return blk * c + pltpu.roll(blk, LANES - QK_ROPE // 2, 1) * sa + pltpu.roll(blk, QK_ROPE // 2, 1) * sb


def _rope_t(dout, tabs):
    c, sa, sb = tabs
    return dout * c + pltpu.roll(dout * sa, QK_ROPE // 2, 1) + pltpu.roll(dout * sb, LANES - QK_ROPE // 2, 1)


def _rms_small(x, g):
    r = lax.rsqrt(jnp.mean(x * x, axis=-1, keepdims=True) + EPS)
    xh = x * r
    return xh, r, xh * g


def _mla_prep_fwd(zs, pos, invf, qg, kvg, wuq_p, wukv):
    T = zs.shape[0]
    tm = _tile(T, 512)
    HW = MLA_HEADS * HEAD_PAD

    def body(zs_ref, pos_ref, invf_ref, qg_ref, kvg_ref, wq_ref, wkv_ref, q_ref, k_ref, v_ref):
        tabs = _rope_tables(pos_ref, invf_ref)
        _, _, cqn = _rms_small(zs_ref[:, 0:Q_LORA], qg_ref[...])
        _, _, ckvn = _rms_small(zs_ref[:, Q_LORA:Q_LORA + KV_LORA], kvg_ref[...])
        q = _dot_nt(cqn.astype(BF16), wq_ref[...]) * ATTN_SCALE
        kv = _dot(ckvn.astype(BF16), wkv_ref[...])
        kr = _rope(zs_ref[:, Q_LORA + KV_LORA:ZS_W], tabs).astype(BF16)
        for h in range(MLA_HEADS):
            b0 = h * HEAD_PAD
            q_ref[:, b0:b0 + QK_NOPE] = q[:, b0:b0 + QK_NOPE].astype(BF16)
            q_ref[:, b0 + QK_NOPE:b0 + HEAD_PAD] = _rope(q[:, b0 + QK_NOPE:b0 + HEAD_PAD], tabs).astype(BF16)
            k_ref[:, b0:b0 + QK_NOPE] = kv[:, b0:b0 + QK_NOPE].astype(BF16)
            k_ref[:, b0 + QK_NOPE:b0 + HEAD_PAD] = kr
            v_ref[:, h * V_HEAD:(h + 1) * V_HEAD] = kv[:, b0 + QK_NOPE:b0 + HEAD_PAD].astype(BF16)

    full = lambda a: pl.BlockSpec(a.shape, lambda i: (0,) * a.ndim)
    return pl.pallas_call(
        body, grid=(T // tm,),
        in_specs=[pl.BlockSpec((tm, ZS_W), lambda i: (i, 0)), pl.BlockSpec((tm, 1), lambda i: (i, 0)),
                  full(invf), full(qg), full(kvg), full(wuq_p), full(wukv)],
        out_specs=[pl.BlockSpec((tm, HW), lambda i: (i, 0)), pl.BlockSpec((tm, HW), lambda i: (i, 0)),
                   pl.BlockSpec((tm, D_MODEL), lambda i: (i, 0))],
        out_shape=[jax.ShapeDtypeStruct((T, HW), BF16), jax.ShapeDtypeStruct((T, HW), BF16),
                   jax.ShapeDtypeStruct((T, D_MODEL), BF16)],
        name="mla_prep_fwd", compiler_params=_params("parallel"))(zs, pos, invf, qg, kvg, wuq_p, wukv)


def _mla_prep_bwd(zs, pos, invf, qg, kvg, wuq_p, wukv, dq, dk, dv):
    T = zs.shape[0]
    tm = _tile(T, 256)
    HW = MLA_HEADS * HEAD_PAD

    def body(zs_ref, pos_ref, invf_ref, qg_ref, kvg_ref, wq_ref, wkv_ref, dq_ref, dk_ref, dv_ref,
             dzs_ref, cqn_ref, dqp_ref, ckvn_ref, dkv_ref, dqg_ref, dkvg_ref):
        @pl.when(pl.program_id(0) == 0)
        def _():
            dqg_ref[...] = jnp.zeros_like(dqg_ref)
            dkvg_ref[...] = jnp.zeros_like(dkvg_ref)

        tabs = _rope_tables(pos_ref, invf_ref)
        cqh, rq, cqn = _rms_small(zs_ref[:, 0:Q_LORA], qg_ref[...])
        ckvh, rkv, ckvn = _rms_small(zs_ref[:, Q_LORA:Q_LORA + KV_LORA], kvg_ref[...])
        cqn_ref[...] = cqn.astype(BF16)
        ckvn_ref[...] = ckvn.astype(BF16)
        dkr = jnp.zeros((tm, LANES), F32)
        for h in range(MLA_HEADS):
            b0 = h * HEAD_PAD
            dqp_ref[:, b0:b0 + QK_NOPE] = dq_ref[:, b0:b0 + QK_NOPE]
            dqp_ref[:, b0 + QK_NOPE:b0 + HEAD_PAD] = _rope_t(
                dq_ref[:, b0 + QK_NOPE:b0 + HEAD_PAD].astype(F32), tabs).astype(BF16)
            dkv_ref[:, b0:b0 + QK_NOPE] = dk_ref[:, b0:b0 + QK_NOPE]
            dkv_ref[:, b0 + QK_NOPE:b0 + HEAD_PAD] = dv_ref[:, h * V_HEAD:(h + 1) * V_HEAD]
            dkr = dkr + dk_ref[:, b0 + QK_NOPE:b0 + HEAD_PAD].astype(F32)
        dcqn = _dot(dqp_ref[...], wq_ref[...])
        dckvn = _dot_nt(dkv_ref[...], wkv_ref[...])
        dqg_ref[...] += jnp.sum(dcqn * cqh, axis=0, keepdims=True)
        dkvg_ref[...] += jnp.sum(dckvn * ckvh, axis=0, keepdims=True)
        dxh = dcqn * qg_ref[...]
        dzs_ref[:, 0:Q_LORA] = (rq * (dxh - cqh * jnp.mean(dxh * cqh, axis=-1, keepdims=True))).astype(BF16)
        dxh = dckvn * kvg_ref[...]
        dzs_ref[:, Q_LORA:Q_LORA + KV_LORA] = (
            rkv * (dxh - ckvh * jnp.mean(dxh * ckvh, axis=-1, keepdims=True))).astype(BF16)
        dzs_ref[:, Q_LORA + KV_LORA:ZS_W] = _rope_t(dkr, tabs).astype(BF16)

    full = lambda a: pl.BlockSpec(a.shape, lambda i: (0,) * a.ndim)
    rowb = lambda w: pl.BlockSpec((tm, w), lambda i: (i, 0))
    return pl.pallas_call(
        body, grid=(T // tm,),
        in_specs=[rowb(ZS_W), rowb(1), full(invf), full(qg), full(kvg), full(wuq_p), full(wukv),
                  rowb(HW), rowb(HW), rowb(D_MODEL)],
        out_specs=[rowb(ZS_W), rowb(Q_LORA), rowb(HW), rowb(KV_LORA), rowb(HW), full(qg), full(kvg)],
        out_shape=[jax.ShapeDtypeStruct((T, ZS_W), BF16), jax.ShapeDtypeStruct((T, Q_LORA), BF16),
                   jax.ShapeDtypeStruct((T, HW), BF16), jax.ShapeDtypeStruct((T, KV_LORA), BF16),
                   jax.ShapeDtypeStruct((T, HW), BF16), jax.ShapeDtypeStruct(qg.shape, F32),
                   jax.ShapeDtypeStruct(kvg.shape, F32)],
        name="mla_prep_bwd", compiler_params=_params("arbitrary"))(
            zs, pos, invf, qg, kvg, wuq_p, wukv, dq, dk, dv)


def _causal(tq, kmax, q0):
    r = lax.broadcasted_iota(jnp.int32, (tq, kmax), 0) + q0
    c = lax.broadcasted_iota(jnp.int32, (tq, kmax), 1)
    return c <= r


def _attn_fwd(q, k, v, batch, seq):
    tq = _tile(seq, ATTN_TILE)
    nq = seq // tq

    def body(q_ref, k_ref, v_ref, o_ref, lse_ref):
        diag = _causal(tq, tq, 0)
        for qi in range(nq):
            rows = slice(qi * tq, (qi + 1) * tq)
            qr = q_ref[rows, :]
            s_d = jnp.where(diag, _dot_nt(qr, k_ref[rows, :]), NEG_BIG)
            m = jnp.max(s_d, axis=-1, keepdims=True)
            if qi > 0:
                before = slice(0, qi * tq)
                s_b = _dot_nt(qr, k_ref[before, :])
                m = jnp.maximum(m, jnp.max(s_b, axis=-1, keepdims=True))
                p_b = jnp.exp(s_b - m)
                l = jnp.sum(p_b, axis=-1, keepdims=True)
                acc = _dot(p_b.astype(BF16), v_ref[before, :])
            p_d = jnp.exp(s_d - m)
            l_d = jnp.sum(p_d, axis=-1, keepdims=True)
            acc_d = _dot(p_d.astype(BF16), v_ref[rows, :])
            l, acc = (l + l_d, acc + acc_d) if qi > 0 else (l_d, acc_d)
            o_ref[rows, :] = (acc / l).astype(KEPT)
            lse_ref[rows, :] = jnp.broadcast_to(m + jnp.log(l), (tq, V_HEAD))

    return pl.pallas_call(
        body, grid=(batch, MLA_HEADS),
        in_specs=[pl.BlockSpec((seq, HEAD_PAD), lambda b, h: (b, h)),
                  pl.BlockSpec((seq, HEAD_PAD), lambda b, h: (b, h)),
                  pl.BlockSpec((seq, V_HEAD), lambda b, h: (b, h))],
        out_specs=[pl.BlockSpec((seq, V_HEAD), lambda b, h: (b, h)),
                   pl.BlockSpec((seq, V_HEAD), lambda b, h: (b, h))],
        out_shape=[jax.ShapeDtypeStruct((batch * seq, D_MODEL), KEPT),
                   jax.ShapeDtypeStruct((batch * seq, D_MODEL), F32)],
        name="attn_fwd", compiler_params=_params("parallel", "parallel"))(q, k, v)


def _attn_bwd(q, k, v, o, do, lse, batch, seq, dep):
    tq = _tile(seq, ATTN_TILE)
    nq = seq // tq

    def body(q_ref, k_ref, v_ref, o_ref, do_ref, lse_ref, dep_ref, dq_ref, dk_ref, dv_ref, dk_acc, dv_acc):
        dk_acc[...] = jnp.zeros_like(dk_acc)
        dv_acc[...] = jnp.zeros_like(dv_acc)
        for qi in range(nq):
            rows = slice(qi * tq, (qi + 1) * tq)
            kmax = (qi + 1) * tq
            qr = q_ref[rows, :]
            dor = do_ref[rows, :]
            kk = k_ref[0:kmax, :]
            s = _dot_nt(qr, kk)
            p = jnp.where(_causal(tq, kmax, qi * tq), jnp.exp(s - lse_ref[rows, 0:1]), 0.0)
            dp = _dot_nt(dor, v_ref[0:kmax, :])
            delta = jnp.sum(dor.astype(F32) * o_ref[rows, :].astype(F32), axis=-1, keepdims=True)
            ds = (p * (dp - delta)).astype(BF16)
            dq_ref[rows, :] = (_dot(ds, kk) * ATTN_SCALE).astype(BF16)
            dk_acc[0:kmax, :] += _dot_tn(ds, qr)
            dv_acc[0:kmax, :] += _dot_tn(p.astype(BF16), dor)
        dk_ref[...] = dk_acc[...].astype(BF16)
        dv_ref[...] = dv_acc[...].astype(BF16)

    qspec = pl.BlockSpec((seq, HEAD_PAD), lambda b, h: (b, h))
    vspec = pl.BlockSpec((seq, V_HEAD), lambda b, h: (b, h))
    T = batch * seq
    return pl.pallas_call(
        body, grid=(batch, MLA_HEADS),
        in_specs=[qspec, qspec, vspec, vspec, vspec, vspec, ANY],
        out_specs=[qspec, qspec, vspec],
        out_shape=[jax.ShapeDtypeStruct((T, MLA_HEADS * HEAD_PAD), BF16),
                   jax.ShapeDtypeStruct((T, MLA_HEADS * HEAD_PAD), BF16),
                   jax.ShapeDtypeStruct((T, D_MODEL), BF16)],
        scratch_shapes=[pltpu.VMEM((seq, HEAD_PAD), F32), pltpu.VMEM((seq, V_HEAD), F32)],
        name="attn_bwd", compiler_params=_params("parallel", "parallel"))(q, k, v, o, do, lse, dep)


def _merge_out(x, yag, zm, o, w_out, ffn_g):
    T = x.shape[0]
    tm = _tile(T, 512)

    def body(x_ref, ya_ref, gb_ref, o_ref, w_ref, g_ref, mg_ref, x1_ref, h2_ref):
        mg = (ya_ref[...].astype(F32) + _sigmoid(gb_ref[...].astype(F32)) * o_ref[...].astype(F32)).astype(BF16)
        mg_ref[...] = mg
        x1 = x_ref[...] + _dot(mg, w_ref[...])
        x1_ref[...] = x1
        r = lax.rsqrt(jnp.mean(x1 * x1, axis=-1, keepdims=True) + EPS)
        h2_ref[...] = (x1 * r * g_ref[...]).astype(BF16)

    row = pl.BlockSpec((tm, D_MODEL), lambda i: (i, 0))
    return pl.pallas_call(
        body, grid=(T // tm,),
        in_specs=[row, row, pl.BlockSpec((tm, D_MODEL), lambda i: (i, 3)), row,
                  pl.BlockSpec((D_MODEL, D_MODEL), lambda i: (0, 0)), pl.BlockSpec((1, D_MODEL), lambda i: (0, 0))],
        out_specs=[row, row, row],
        out_shape=[jax.ShapeDtypeStruct((T, D_MODEL), BF16), jax.ShapeDtypeStruct((T, D_MODEL), F32),
                   jax.ShapeDtypeStruct((T, D_MODEL), BF16)],
        name="merge_out", compiler_params=_params("parallel"))(x, yag, zm, o, w_out, ffn_g)


FF_TILE = 256
FF_BLOCKS = D_FF // FF_TILE
FFB_TILE = 256


def _shift_down(x, k):
    row = lax.broadcasted_iota(jnp.int32, x.shape, 0)
    return jnp.where(row >= k, pltpu.roll(x, k, 0), 0.0)


def _shift_up(x, k):
    n = x.shape[0]
    row = lax.broadcasted_iota(jnp.int32, x.shape, 0)
    return jnp.where(row < n - k, pltpu.roll(x, n - k, 0), 0.0)


def _conv(x, w_ref, b_ref):
    return b_ref[...] + w_ref[2:3, :] * x + w_ref[1:2, :] * _shift_down(x, 1) + w_ref[0:1, :] * _shift_down(x, 2)


EDGE = 16


def _conv_wrapped(x, w_ref, b_ref):
    return (b_ref[...] + w_ref[2:3, :] * x + w_ref[1:2, :] * pltpu.roll(x, 1, 0)
            + w_ref[0:1, :] * pltpu.roll(x, 2, 0))


def _up_act(h2, wt_up, cw, cb, batch, seq):
    def body(h_ref, wug_ref, wuv_ref, wg_ref, wv_ref, bg_ref, bv_ref, ug_ref, uv_ref, g_ref, v_ref, a_ref):
        h = h_ref[...]
        ug = _dot_nt(h, wug_ref[...])
        uv = _dot_nt(h, wuv_ref[...])
        ug_ref[...] = ug.astype(KEPT)
        uv_ref[...] = uv.astype(KEPT)

        def write(rows, gate, val):
            g_ref[rows, :] = gate.astype(KEPT)
            v_ref[rows, :] = val.astype(KEPT)
            a_ref[rows, :] = (gate * _sigmoid(gate) * val).astype(BF16)

        write(slice(None), _conv_wrapped(ug, wg_ref, bg_ref), _conv_wrapped(uv, wv_ref, bv_ref))
        write(slice(0, EDGE), _conv(ug[0:EDGE], wg_ref, bg_ref), _conv(uv[0:EDGE], wv_ref, bv_ref))

    blk = pl.BlockSpec((seq, FF_TILE), lambda b, j: (b, j))
    wup = lambda off: pl.BlockSpec((FF_TILE, D_MODEL), lambda b, j: (j + off, 0))
    wsp = lambda off: pl.BlockSpec((3, FF_TILE), lambda b, j: (0, j + off))
    bsp = lambda off: pl.BlockSpec((1, FF_TILE), lambda b, j: (0, j + off))
    T = batch * seq
    kept = jax.ShapeDtypeStruct((T, D_FF), KEPT)
    return pl.pallas_call(
        body, grid=(batch, FF_BLOCKS),
        in_specs=[pl.BlockSpec((seq, D_MODEL), lambda b, j: (b, 0)), wup(0), wup(FF_BLOCKS),
                  wsp(0), wsp(FF_BLOCKS), bsp(0), bsp(FF_BLOCKS)],
        out_specs=[blk] * 5,
        out_shape=[kept, kept, kept, kept, jax.ShapeDtypeStruct((T, D_FF), BF16)],
        name="up_act", compiler_params=_params("parallel", "arbitrary"))(h2, wt_up, wt_up, cw, cw, cb, cb)


def _ffn_act_bwd(upg, upv, gate, val, cw, dx2b, w_down, batch, seq):
    def half(du, x, w_ref, dx_ref, dw_ref):
        j = pl.program_id(1)
        n = du.shape[0]
        up1, up2 = pltpu.roll(du, n - 1, 0), pltpu.roll(du, n - 2, 0)
        dx_ref[...] = (w_ref[2:3, :] * du + w_ref[1:2, :] * up1 + w_ref[0:1, :] * up2).astype(BF16)
        tail = du[n - EDGE:n]
        dx_ref[n - EDGE:n, :] = (w_ref[2:3, :] * tail + w_ref[1:2, :] * _shift_up(tail, 1)
                                 + w_ref[0:1, :] * _shift_up(tail, 2)).astype(BF16)
        row = lax.broadcasted_iota(jnp.int32, (EDGE, du.shape[1]), 0)
        head, x_tail = du[0:EDGE], x[n - EDGE:n]
        wrap1 = jnp.sum(jnp.where(row >= EDGE - 1, pltpu.roll(head, EDGE - 1, 0), 0.0) * x_tail, axis=0, keepdims=True)
        wrap2 = jnp.sum(jnp.where(row >= EDGE - 2, pltpu.roll(head, EDGE - 2, 0), 0.0) * x_tail, axis=0, keepdims=True)
        dw_ref[j, 2:3, :] += jnp.sum(du * x, axis=0, keepdims=True)
        dw_ref[j, 1:2, :] += jnp.sum(up1 * x, axis=0, keepdims=True) - wrap1
        dw_ref[j, 0:1, :] += jnp.sum(up2 * x, axis=0, keepdims=True) - wrap2
        dw_ref[j, 3:4, :] += jnp.sum(du, axis=0, keepdims=True)

    def body(ug_ref, uv_ref, g_ref, v_ref, wg_ref, wv_ref, dx_ref, wd_ref, dg_ref, dv_ref, dwg_ref, dwv_ref):
        @pl.when((pl.program_id(0) == 0) & (pl.program_id(1) == 0))
        def _():
            dwg_ref[...] = jnp.zeros_like(dwg_ref)
            dwv_ref[...] = jnp.zeros_like(dwv_ref)

        gate, val = g_ref[...].astype(F32), v_ref[...].astype(F32)
        sg = _sigmoid(gate)
        dav = _dot_nt(dx_ref[...], wd_ref[...])
        half(dav * val * sg * (1.0 + gate * (1.0 - sg)), ug_ref[...].astype(F32), wg_ref, dg_ref, dwg_ref)
        half(dav * gate * sg, uv_ref[...].astype(F32), wv_ref, dv_ref, dwv_ref)

    nb = D_FF // FFB_TILE
    blk = pl.BlockSpec((seq, FFB_TILE), lambda b, j: (b, j))
    wsp = lambda off: pl.BlockSpec((3, FFB_TILE), lambda b, j: (0, j + off))
    acc = pl.BlockSpec((nb, 4, FFB_TILE), lambda b, j: (0, 0, 0))
    T = batch * seq
    dupg, dupv, dwg, dwv = pl.pallas_call(
        body, grid=(batch, nb),
        in_specs=[blk, blk, blk, blk, wsp(0), wsp(nb),
                  pl.BlockSpec((seq, D_MODEL), lambda b, j: (b, 0)),
                  pl.BlockSpec((FFB_TILE, D_MODEL), lambda b, j: (j, 0))],
        out_specs=[blk, blk, acc, acc],
        out_shape=[jax.ShapeDtypeStruct((T, D_FF), BF16), jax.ShapeDtypeStruct((T, D_FF), BF16),
                   jax.ShapeDtypeStruct((nb, 4, FFB_TILE), F32), jax.ShapeDtypeStruct((nb, 4, FFB_TILE), F32)],
        name="ffn_act_bwd", compiler_params=_params("arbitrary", "arbitrary"))(
            upg, upv, gate, val, cw, cw, dx2b, w_down)
    dwg, dwv = (jnp.transpose(a, (1, 0, 2)).reshape(4, D_FF) for a in (dwg, dwv))
    return dupg, dupv, dwg[:3], dwv[:3], dwg[3:], dwv[3:]


def _down_loss(a, w_down, x1, target, gfin):
    T = x1.shape[0]
    tm = _tile(T, 512)

    def body(a_ref, w_ref, x1_ref, t_ref, g_ref, dx_ref, dxb_ref, loss_ref, dg_ref):
        @pl.when(pl.program_id(0) == 0)
        def _():
            loss_ref[...] = jnp.zeros_like(loss_ref)
            dg_ref[...] = jnp.zeros_like(dg_ref)

        x2 = x1_ref[...] + _dot(a_ref[...], w_ref[...])
        r = lax.rsqrt(jnp.mean(x2 * x2, axis=-1, keepdims=True) + EPS)
        xh = x2 * r
        g = g_ref[...]
        diff = xh * g - t_ref[...]
        loss_ref[...] += 0.5 * jnp.sum(jnp.mean(diff * diff, axis=-1, keepdims=True))
        dy = diff * (1.0 / D_MODEL)
        dg_ref[...] += jnp.sum(dy * xh, axis=0, keepdims=True)
        dxh = dy * g
        dx = r * (dxh - xh * jnp.mean(dxh * xh, axis=-1, keepdims=True))
        dx_ref[...] = dx
        dxb_ref[...] = dx.astype(BF16)

    row = pl.BlockSpec((tm, D_MODEL), lambda i: (i, 0))
    vec = pl.BlockSpec((1, D_MODEL), lambda i: (0, 0))
    return pl.pallas_call(
        body, grid=(T // tm,),
        in_specs=[pl.BlockSpec((tm, D_FF), lambda i: (i, 0)),
                  pl.BlockSpec((D_FF, D_MODEL), lambda i: (0, 0)), row, row, vec],
        out_specs=[row, row, pl.BlockSpec((8, LANES), lambda i: (0, 0)), vec],
        out_shape=[jax.ShapeDtypeStruct((T, D_MODEL), F32), jax.ShapeDtypeStruct((T, D_MODEL), BF16),
                   jax.ShapeDtypeStruct((8, LANES), F32), jax.ShapeDtypeStruct((1, D_MODEL), F32)],
        name="down_loss", compiler_params=_params("arbitrary"))(a, w_down, x1, target, gfin)


def _local_step(x, positions, target, mix_norm, av_g, av_b, w_s, b_s, q_norm, kv_norm, ffn_norm, conv_b,
                final_norm, comm):
    batch, seq, _ = x.shape
    T = batch * seq
    x = x.reshape(T, D_MODEL)
    target = target.reshape(T, D_MODEL)
    pos = positions.reshape(T, 1)
    half = jnp.arange(0, QK_ROPE, 2, dtype=F32) / QK_ROPE
    inv_freq = 1.0 / (ROPE_THETA ** half)
    invf = jnp.concatenate([inv_freq, inv_freq, jnp.zeros((LANES - QK_ROPE,), F32)]).reshape(1, LANES)
    w_st = jnp.swapaxes(w_s, 1, 2)
    b_col = b_s.reshape(A_GROUPS, CHUNK, 1)

    wt_in = comm.in_weights()
    h, zm, zs = _in_proj(x, mix_norm, wt_in)
    yag = _mixer_a_fwd(zm, av_g, av_b, w_s, b_col)
    wuq_p, wukv, w_out = comm.mla_weights(after=yag)
    q, k, v = _mla_prep_fwd(zs, pos, invf, q_norm, kv_norm, wuq_p, wukv)
    o, lse = _attn_fwd(q, k, v, batch, seq)
    merged, x1, h2 = _merge_out(x, yag, zm, o, w_out, ffn_norm)
    wt_up, conv_w, w_down = comm.ffn_weights(after=merged)
    upg, upv, gate, val, act = _up_act(h2, wt_up, conv_w, conv_b, batch, seq)
    dx2, dx2b, loss_acc, d_final = _down_loss(act, w_down, x1, target, final_norm)

    d_wdown = _mm_tn(act, dx2b, "dw_down")
    dupg, dupv, dcwg, dcwv, dcbg, dcbv = _ffn_act_bwd(upg, upv, gate, val, conv_w, dx2b, w_down, batch, seq)
    d_wt_up = _mm_tn(dupv, h2, "dw_up_val", rows=2 * D_FF, row0=D_FF,
                     into=_mm_tn(dupg, h2, "dw_up_gate", rows=2 * D_FF))
    dx1, d_ffn_norm, dmerged = _proj_bwd(
        [dupg, dupv], wt_up, [(0, (0, D_FF), (0, D_FF)), (1, (0, D_FF), (D_FF, 2 * D_FF))],
        x1, ffn_norm, dx2, "up_proj_bwd", w2=w_out)
    d_wout = _mm_tn(merged, dx1, "dw_out")
    token = comm.send_ffn_grads(d_wdown, d_wt_up, jnp.concatenate([dcwg, dcwv], axis=1), d_wout)
    dzm, do, d_avg, d_avb, d_ws, d_bs = _mixer_bwd(zm, o, dmerged, av_g, av_b, w_s, w_st, b_col, token)
    token = comm.send_small_grads([
        d_avg, d_avb, _small_2d(d_ws), d_bs.reshape(A_GROUPS, CHUNK), d_ffn_norm,
        jnp.concatenate([dcbg, dcbv], axis=1), d_final])
    dq, dk, dv = _attn_bwd(q, k, v, o, do, lse, batch, seq, token)
    dzs, cqn, dqp, ckvn, dkv, d_qn, d_kvn = _mla_prep_bwd(zs, pos, invf, q_norm, kv_norm, wuq_p, wukv, dq, dk, dv)
    d_wt_main = _mm_tn(dzm, h, "dw_in_main")
    d_wt_zs = _mm_tn(dzs, h, "dw_in_small")
    token = comm.send_in_grads(d_wt_main, d_wt_zs)
    d_wuq_p = _mm_tn(dqp, cqn, "dw_uq", dep=token)
    d_wukv = _mm_tn(ckvn, dkv, "dw_ukv", dep=token)
    terms = [(0, (i * D_MODEL, (i + 1) * D_MODEL), rows) for i, rows in enumerate(IN_ROWS_MAIN)]
    terms.append((1, (0, ZS_W), IN_ROWS_ZS))
    dx, d_mix_norm = _proj_bwd([dzm, dzs], wt_in, terms, x, mix_norm, dx1, "in_proj_bwd", dep=token, rows=512)
    token = comm.send_late_grads(d_wuq_p, d_wukv, [d_qn, d_kvn, d_mix_norm, loss_acc])
    return dx.reshape(batch, seq, D_MODEL), token


MESH_ID = pl.DeviceIdType.MESH
EFFECT = pltpu.SideEffectType.DATAFLOW_SIDE_EFFECTING


def _mesh_pos():
    return lax.axis_index("x"), lax.axis_index("y"), lax.axis_index("c")


def _peer(pos, d):
    x, y, c = pos
    px = 1 - x if d & 4 else x
    py = 1 - y if d & 2 else y
    pc = 1 - c if d & 1 else c
    return (px, py, pc), 4 * px + 2 * py + pc


def _copy(src_ref, land_ref, send_sems, recv_sems, a, d, pos, exchange, landing_here):
    peer, pid = _peer(pos, d)
    me = 4 * pos[0] + 2 * pos[1] + pos[2]
    if exchange:
        src, dst = src_ref.at[pid], land_ref.at[d]
    else:
        src, dst = src_ref, land_ref.at[pid if landing_here else me]
    return pltpu.make_async_remote_copy(
        src_ref=src, dst_ref=dst, send_sem=send_sems.at[a * (N_DEV - 1) + d - 1],
        recv_sem=recv_sems.at[a * (N_DEV - 1) + d - 1],
        device_id=peer, device_id_type=MESH_ID)


def _start_copies(groups, modes, name, dep=None):
    sizes = [len(g) for g in groups]
    srcs = [s for g in groups for s in g]
    lands = [lax.empty(s.shape if modes[gi] else (N_DEV,) + s.shape, s.dtype)
             for gi, g in enumerate(groups) for s in g]
    n, ng = len(srcs), len(groups)
    n_in = 2 * n + (dep is not None)

    def body(*refs):
        src_refs, land_refs = refs[:n], refs[n:2 * n]
        sems = refs[n_in:n_in + 3 * ng]
        token = refs[-1]
        pos = _mesh_pos()
        k = 0
        for gi, size in enumerate(sizes):
            for a in range(size):
                _own_copy(src_refs[k], land_refs[k], sems[3 * gi + 2], a, pos, modes[gi]).start()
                for d in range(1, N_DEV):
                    _copy(src_refs[k], land_refs[k], sems[3 * gi], sems[3 * gi + 1], a, d, pos, modes[gi],
                          landing_here=False).start()
                k += 1
        token[...] = jnp.zeros_like(token)

    sem_shapes = []
    for size in sizes:
        remote = pltpu.SemaphoreType.DMA((size * (N_DEV - 1),))
        sem_shapes += [remote, remote, pltpu.SemaphoreType.DMA((size,))]
    out = pl.pallas_call(
        body, name=name,
        out_shape=(*sem_shapes, *[pltpu.HBM(a.shape, a.dtype) for a in srcs + lands],
                   jax.ShapeDtypeStruct((8, LANES), F32)),
        in_specs=[HBM] * (2 * n) + [ANY] * (dep is not None),
        out_specs=(*[SEM] * (3 * ng), *[HBM] * (2 * n), pl.BlockSpec(memory_space=pltpu.VMEM)),
        input_output_aliases={i: 3 * ng + i for i in range(2 * n)},
        compiler_params=pltpu.CompilerParams(has_side_effects=EFFECT),
    )(*[pltpu.with_memory_space_constraint(a, pltpu.HBM) for a in srcs + lands], *([dep] if dep is not None else []))
    thru = out[3 * ng:3 * ng + 2 * n]
    handles, k = [], 0
    for gi, size in enumerate(sizes):
        handles.append((out[3 * gi:3 * gi + 3], thru[k:k + size], thru[n + k:n + k + size]))
        k += size
    return handles, out[-1]


def _own_copy(src_ref, land_ref, local_sems, a, pos, exchange):
    me = 4 * pos[0] + 2 * pos[1] + pos[2]
    src, dst = (src_ref.at[me], land_ref.at[0]) if exchange else (src_ref, land_ref.at[me])
    return pltpu.make_async_copy(src, dst, local_sems.at[a])


def _wait_copies(handle, exchange, after, name):
    sems, srcs, lands = handle
    n = len(srcs)

    def body(*refs):
        src_refs, land_refs = refs[:n], refs[n:2 * n]
        send, recv, local = refs[2 * n:2 * n + 3]
        pos = _mesh_pos()
        for a in range(n):
            _own_copy(src_refs[a], land_refs[a], local, a, pos, exchange).wait()
            for d in range(1, N_DEV):
                cp = _copy(src_refs[a], land_refs[a], send, recv, a, d, pos, exchange, landing_here=True)
                cp.wait_send()
                cp.wait_recv()

    out = pl.pallas_call(
        body, name=name,
        out_shape=tuple(pltpu.HBM(a.shape, a.dtype) for a in (*srcs, *lands)),
        in_specs=[HBM] * (2 * n) + [SEM, SEM, SEM, ANY], out_specs=[HBM] * (2 * n),
        input_output_aliases={i: i for i in range(2 * n)},
        compiler_params=pltpu.CompilerParams(has_side_effects=EFFECT),
    )(*srcs, *lands, *sems, after)
    return out[n:]


def _gather_now(a, name):
    def body(x_ref, out_ref, send_sems, recv_sems, local_sem):
        x, y, c = _mesh_pos()
        me, sibling = (x, y, c), (x, y, 1 - c)
        chips = [(1 - x, y), (x, 1 - y), (1 - x, 1 - y)]

        def slot(p):
            return out_ref.at[4 * p[0] + 2 * p[1] + p[2]]

        def copy(k, block, to, src=None):
            return pltpu.make_async_remote_copy(
                src_ref=slot(block) if src is None else src, dst_ref=slot(block), send_sem=send_sems.at[k],
                recv_sem=recv_sems.at[k], device_id=to, device_id_type=MESH_ID)

        mine = pltpu.make_async_copy(x_ref, slot(me), local_sem)
        mine.start()
        first = [copy(0, me, sibling, src=x_ref)]
        first += [copy(1 + j, me, (*chip, c), src=x_ref) for j, chip in enumerate(chips)]
        for cp in first:
            cp.start()
        passed = [copy(4 + j, (*chip, c), sibling) for j, chip in enumerate(chips)]
        for j, chip in enumerate(chips):
            copy(1 + j, (*chip, c), me).wait_recv()
            passed[j].start()
        copy(0, sibling, me).wait_recv()
        for j, chip in enumerate(chips):
            copy(4 + j, (*chip, 1 - c), me).wait_recv()
        for cp in first + passed:
            cp.wait_send()
        mine.wait()

    return pl.pallas_call(
        body, in_specs=[ANY], out_specs=ANY,
        out_shape=jax.ShapeDtypeStruct((N_DEV,) + a.shape, a.dtype),
        scratch_shapes=[pltpu.SemaphoreType.DMA((N_DEV - 1,)), pltpu.SemaphoreType.DMA((N_DEV - 1,)),
                        pltpu.SemaphoreType.DMA],
        name=name, compiler_params=pltpu.CompilerParams(has_side_effects=True))(a)


def _sum_parts(p_ref):
    g = p_ref[0].astype(F32)
    for k in range(1, N_DEV):
        g = g + p_ref[k].astype(F32)
    return g


def _adamw_update(p_ref, w_ref, m_ref, v_ref, g_ref, d_ref, nm_ref, nv_ref):
    c1 = 1.0 - ADAM_B1 ** ADAM_STEP
    c2 = 1.0 - ADAM_B2 ** ADAM_STEP
    g = _sum_parts(p_ref)
    nm = ADAM_B1 * m_ref[...] + (1.0 - ADAM_B1) * g
    nv = ADAM_B2 * v_ref[...] + (1.0 - ADAM_B2) * (g * g)
    g_ref[...] = g
    nm_ref[...] = nm
    nv_ref[...] = nv
    d_ref[...] = -ADAM_LR * ((nm / c1) / (jnp.sqrt(nv / c2) + ADAM_EPS) + ADAM_WD * w_ref[...])


def _adamw_many(parts, ws, ms, vs, sums, name):
    n, ns = len(ws), len(sums)

    def body(*refs):
        ins, outs = refs[:4 * n + ns], refs[4 * n + ns:]
        for i in range(n):
            _adamw_update(ins[i], ins[n + i], ins[2 * n + i], ins[3 * n + i],
                          outs[i], outs[n + i], outs[2 * n + i], outs[3 * n + i])
        for i in range(ns):
            outs[4 * n + i][...] = _sum_parts(ins[4 * n + i])

    full = lambda a: pl.BlockSpec(a.shape, lambda: (0,) * a.ndim)
    args = [*parts, *ws, *ms, *vs, *sums]
    outs = [jax.ShapeDtypeStruct(w.shape, F32) for _ in range(4) for w in ws]
    outs += [jax.ShapeDtypeStruct(s.shape[1:], F32) for s in sums]
    res = pl.pallas_call(
        body, in_specs=[full(a) for a in args], out_specs=[full(o) for o in outs], out_shape=outs,
        name=name, compiler_params=pltpu.CompilerParams(vmem_limit_bytes=VMEM_LIMIT))(*args)
    return res[:n], res[n:2 * n], res[2 * n:3 * n], res[3 * n:4 * n], res[4 * n:]


def _adamw(parts, w, m, v, name):
    R, C = w.shape
    tr, tc = R, C
    if N_DEV * R * C * parts.dtype.itemsize > SMALL_BLOCK_BYTES:
        tr = next((t for t in range(min(R, 256) // 16 * 16, 15, -16) if R % t == 0), R)
        if tr == R:
            tc = _tile(C, 256)

    def body(p_ref, w_ref, m_ref, v_ref, g_ref, d_ref, nm_ref, nv_ref):
        _adamw_update(p_ref, w_ref, m_ref, v_ref, g_ref, d_ref, nm_ref, nv_ref)

    blk = pl.BlockSpec((tr, tc), lambda i, j: (i, j))
    shp = jax.ShapeDtypeStruct((R, C), F32)
    return pl.pallas_call(
        body, grid=(R // tr, C // tc),
        in_specs=[pl.BlockSpec((N_DEV, tr, tc), lambda i, j: (0, i, j)), blk, blk, blk],
        out_specs=[blk, blk, blk, blk], out_shape=[shp, shp, shp, shp],
        name=name, compiler_params=_params("parallel", "parallel"))(parts, w, m, v)


SPLIT_V = 2 * D_MODEL
SPLIT_KR = SPLIT_V + Q_LORA + KV_LORA + QK_ROPE
IN_DIM = SPLIT_KR + 2 * D_MODEL
IN_ROWS_MAIN = ((0, D_MODEL), (D_MODEL, SPLIT_V), (SPLIT_KR, SPLIT_KR + D_MODEL), (SPLIT_KR + D_MODEL, IN_DIM))
IN_ROWS_ZS = (SPLIT_V, SPLIT_V + ZS_W)

SMALL_EARLY = ("a_v_norm_g", "a_v_norm_b", "a_spatial_w", "a_spatial_b", "ffn_norm", "conv_b", "final_norm")
SMALL_LATE = ("q_a_norm", "kv_a_norm", "mix_norm")


def _small_2d(a):
    return a.reshape(-1, a.shape[-1])


def _cols_from_shards(g):
    return jnp.transpose(g, (1, 0, 2)).reshape(g.shape[1], N_DEV * g.shape[2])


def _shards_from_cols(a):
    R, W = a.shape
    return jnp.transpose(a.reshape(R, N_DEV, W // N_DEV), (1, 0, 2))


class _Comm:
    GATHER_GROUPS = (("w_uq", "w_ukv", "w_out"), ("w_up", "conv_w", "w_down"))
    FFN_GRADS = ("w_down", "w_up", "conv_w", "w_out")
    LATE_GRADS = ("w_uq", "w_ukv")
    TRANSPOSED = ("w_in", "w_up", "w_uq")

    def __init__(self, shards):
        local = {n: a.astype(F32 if n == "conv_w" else BF16) for n, a in shards.items()}
        self.g_in = _gather_now(local["w_in"], "gather_w_in")
        groups = [[local[n] for n in g] for g in self.GATHER_GROUPS]
        (self.h_mla, self.h_ffn), _ = _start_copies(groups, [False] * 2, "gather_start", dep=self.g_in)

    def in_weights(self):
        return self.g_in.reshape(IN_DIM, D_MODEL)

    def mla_weights(self, after):
        g_uq, g_ukv, g_out = _wait_copies(self.h_mla, False, after, "gather_wait_mla")
        wuq_p = jnp.pad(g_uq, ((0, 0), (0, HEAD_PAD - QK_HEAD), (0, 0))).reshape(MLA_HEADS * HEAD_PAD, Q_LORA)
        return wuq_p, _cols_from_shards(g_ukv), g_out.reshape(D_MODEL, D_MODEL)

    def ffn_weights(self, after):
        g_up, g_cw, g_down = _wait_copies(self.h_ffn, False, after, "gather_wait_ffn")
        return g_up.reshape(2 * D_FF, D_MODEL), _cols_from_shards(g_cw), g_down.reshape(D_FF, D_MODEL)

    def send_ffn_grads(self, d_wdown, d_wt_up, d_convw, d_wout):
        group = [d_wdown.reshape(N_DEV, D_FF // N_DEV, D_MODEL), d_wt_up.reshape(N_DEV, 2 * D_FF // N_DEV, D_MODEL),
                 _shards_from_cols(d_convw), d_wout.reshape(N_DEV, D_MODEL // N_DEV, D_MODEL)]
        (self.h_ffn_grads,), token = _start_copies([group], [True], "ffn_grads_start")
        return token

    def send_small_grads(self, grads):
        (self.h_small_early,), token = _start_copies([grads], [False], "small_grads_start")
        return token

    def send_in_grads(self, d_wt_main, d_wt_zs):
        d_in = jnp.concatenate([d_wt_main[:SPLIT_V], d_wt_zs[:SPLIT_KR - SPLIT_V], d_wt_main[SPLIT_V:]], axis=0)
        blocks = d_in.reshape(N_DEV, IN_DIM // N_DEV, D_MODEL)
        (self.h_in_grads,), token = _start_copies([[blocks]], [True], "in_grads_start")
        return token

    def send_late_grads(self, d_wuq_p, d_wukv, small):
        d_uq = d_wuq_p.reshape(MLA_HEADS, HEAD_PAD, Q_LORA)[:, :QK_HEAD, :]
        (self.h_late_grads, self.h_late_small), token = _start_copies(
            [[d_uq, _shards_from_cols(d_wukv)], small], [True, False], "late_grads_start")
        return token


def kernel(x, positions, mix_norm, w_in, a_v_norm_g, a_v_norm_b, a_spatial_w, a_spatial_b, q_a_norm, w_uq, kv_a_norm, w_ukv, w_out, ffn_norm, w_up, conv_w, conv_b, w_down, final_norm, loss_target, m_mix_norm, m_w_in, m_a_v_norm_g, m_a_v_norm_b, m_a_spatial_w, m_a_spatial_b, m_q_a_norm, m_w_uq, m_kv_a_norm, m_w_ukv, m_w_out, m_ffn_norm, m_w_up, m_conv_w, m_conv_b, m_w_down, m_final_norm, v_mix_norm, v_w_in, v_a_v_norm_g, v_a_v_norm_b, v_a_spatial_w, v_a_spatial_b, v_q_a_norm, v_w_uq, v_kv_a_norm, v_w_ukv, v_w_out, v_ffn_norm, v_w_up, v_conv_w, v_conv_b, v_w_down, v_final_norm):
    names = ("mix_norm", "w_in", "a_v_norm_g", "a_v_norm_b", "a_spatial_w", "a_spatial_b", "q_a_norm", "w_uq",
             "kv_a_norm", "w_ukv", "w_out", "ffn_norm", "w_up", "conv_w", "conv_b", "w_down", "final_norm")
    w = dict(zip(names, (mix_norm, w_in, a_v_norm_g, a_v_norm_b, a_spatial_w, a_spatial_b, q_a_norm, w_uq,
                         kv_a_norm, w_ukv, w_out, ffn_norm, w_up, conv_w, conv_b, w_down, final_norm)))
    m = dict(zip(names, (m_mix_norm, m_w_in, m_a_v_norm_g, m_a_v_norm_b, m_a_spatial_w, m_a_spatial_b,
                         m_q_a_norm, m_w_uq, m_kv_a_norm, m_w_ukv, m_w_out, m_ffn_norm, m_w_up, m_conv_w,
                         m_conv_b, m_w_down, m_final_norm)))
    v = dict(zip(names, (v_mix_norm, v_w_in, v_a_v_norm_g, v_a_v_norm_b, v_a_spatial_w, v_a_spatial_b,
                         v_q_a_norm, v_w_uq, v_kv_a_norm, v_w_ukv, v_w_out, v_ffn_norm, v_w_up, v_conv_w,
                         v_conv_b, v_w_down, v_final_norm)))
    shapes = {n: w[n].shape for n in names}
    def view(tree, n):
        a = tree[n].reshape(tree[n].shape[-2:])
        return a.T if n in _Comm.TRANSPOSED else a

    comm = _Comm({n: view(w, n) for n in ("w_in",) + _Comm.GATHER_GROUPS[0] + _Comm.GATHER_GROUPS[1]})

    grad_x, token = _local_step(
        x, positions, loss_target, w["mix_norm"], w["a_v_norm_g"], w["a_v_norm_b"], w["a_spatial_w"][0],
        w["a_spatial_b"][0], w["q_a_norm"], w["kv_a_norm"], w["ffn_norm"], w["conv_b"],
        w["final_norm"].reshape(1, D_MODEL), comm)

    out_g, out_d, out_m, out_v = {}, {}, {}, {}

    def update(n, parts):
        res = _adamw(parts, view(w, n), view(m, n), view(v, n), "adamw_" + n)
        out_g[n], out_d[n], out_m[n], out_v[n] = (
            (t.T if n in _Comm.TRANSPOSED else t).reshape(shapes[n]) for t in res)
        return res[1]

    def update_small(names, parts, sums, name):
        res = _adamw_many(parts, *[[_small_2d(t[n]) for n in names] for t in (w, m, v)], sums, name)
        for i, n in enumerate(names):
            out_g[n], out_d[n], out_m[n], out_v[n] = (r[i].reshape(shapes[n]) for r in res[:4])
        return res

    for n, parts in zip(_Comm.FFN_GRADS, _wait_copies(comm.h_ffn_grads, True, token, "ffn_grads_wait")):
        last = update(n, parts)
    early = _wait_copies(comm.h_small_early, False, last, "small_grads_wait")
    last = update_small(SMALL_EARLY, early, [], "adamw_small")[1][0]
    last = update("w_in", _wait_copies(comm.h_in_grads, True, last, "in_grads_wait")[0])
    for n, parts in zip(_Comm.LATE_GRADS, _wait_copies(comm.h_late_grads, True, last, "late_grads_wait")):
        last = update(n, parts)
    late = _wait_copies(comm.h_late_small, False, last, "late_small_wait")
    res = update_small(SMALL_LATE, late[:-1], late[-1:], "adamw_late")
    loss = res[4][0][0, 0]

    return (loss, grad_x, *[out_g[n] for n in names], *[out_d[n] for n in names],
            *[out_m[n] for n in names], *[out_v[n] for n in names])
```

```python
import math

import jax
import jax.numpy as jnp
from jax import lax
from jax.experimental import pallas as pl
from jax.experimental.pallas import tpu as pltpu

F32 = jnp.float32
BF16 = jnp.bfloat16
KEPT = jnp.bfloat16

N_DEV = 8
D_MODEL = 1024
EPS = 1e-6
A_GROUPS = 8
CHUNK = 128
MLA_HEADS = 8
QK_NOPE = 128
QK_ROPE = 64
QK_HEAD = QK_NOPE + QK_ROPE
HEAD_PAD = 256
V_HEAD = 128
Q_LORA = 256
KV_LORA = 128
ROPE_THETA = 10000.0
D_FF = 2816
ZS_W = 512
ATTN_SCALE = QK_HEAD ** -0.5
ATTN_TILE = 512
NEG_BIG = -1e30

ADAM_LR = 0.001
ADAM_B1 = 0.9
ADAM_B2 = 0.999
ADAM_EPS = 1e-08
ADAM_WD = 0.01
ADAM_STEP = 10

VMEM_LIMIT = 56 * 1024 * 1024
SMALL_BLOCK_BYTES = 5 * 1024 * 1024
LANES = 128

GELU_K = math.sqrt(2.0 / math.pi)
GELU_C = 0.044715

ANY = pl.BlockSpec(memory_space=pl.ANY)
HBM = pl.BlockSpec(memory_space=pltpu.HBM)
SEM = pl.BlockSpec(memory_space=pltpu.SEMAPHORE)


def _tile(n, pref):
    for t in (pref, 512, 256, 128, 64, 32, 16, 8):
        if t <= pref and n % t == 0:
            return t
    return n


def _wide_tile(n, cap=1408):
    return next((t for t in range(min(n, cap) // LANES * LANES, 0, -LANES) if n % t == 0), n)


def _params(*sem):
    return pltpu.CompilerParams(dimension_semantics=sem, vmem_limit_bytes=VMEM_LIMIT)


def _dot(a, b):
    return jnp.dot(a, b, preferred_element_type=F32)


def _dot_nt(a, b):
    return lax.dot_general(a, b, (((1,), (1,)), ((), ())), preferred_element_type=F32)


def _dot_tn(a, b):
    return lax.dot_general(a, b, (((0,), (0,)), ((), ())), preferred_element_type=F32)


def _sigmoid(x):
    return 1.0 / (1.0 + jnp.exp(-x))


def _gelu(x):
    t = jnp.tanh(GELU_K * (x + GELU_C * x * x * x))
    return 0.5 * x * (1.0 + t)


def _gelu_and_grad(x):
    x2 = x * x
    t = jnp.tanh(GELU_K * (x + GELU_C * x * x2))
    half = 0.5 * (1.0 + t)
    return x * half, half + 0.5 * x * (1.0 - t * t) * GELU_K * (1.0 + 3.0 * GELU_C * x2)


class _StreamedWeight:
    def __init__(self, hbm_ref, vmem_ref, sems, ranges):
        cuts = sorted({r for rng in ranges for r in rng})
        self.chunks = list(zip(cuts[:-1], cuts[1:]))
        self.hbm, self.vmem, self.sems = hbm_ref, vmem_ref, sems
        self.first = pl.program_id(0) == 0
        self.waited = set()

        @pl.when(self.first)
        def _():
            for c in range(len(self.chunks)):
                self._copy(c).start()

    @staticmethod
    def n_chunks(ranges):
        return len({r for rng in ranges for r in rng}) - 1

    def _copy(self, c):
        r0, r1 = self.chunks[c]
        return pltpu.make_async_copy(self.hbm.at[r0:r1], self.vmem.at[r0:r1], self.sems.at[c])

    def rows(self, r0, r1):
        need = [c for c, (c0, c1) in enumerate(self.chunks) if r0 <= c0 and c1 <= r1 and c not in self.waited]
        if need:
            self.waited.update(need)

            @pl.when(self.first)
            def _():
                for c in need:
                    self._copy(c).wait()

        return self.vmem[r0:r1, :]


def _in_proj(x, g, wt):
    T, Dm = x.shape
    tm = _tile(T, 512)
    ranges = (*IN_ROWS_MAIN, IN_ROWS_ZS)

    def body(x_ref, g_ref, wt_hbm, h_ref, zm_ref, zs_ref, wt_vmem, sems):
        w = _StreamedWeight(wt_hbm, wt_vmem, sems, ranges)
        xf = x_ref[...]
        r = lax.rsqrt(jnp.mean(xf * xf, axis=-1, keepdims=True) + EPS)
        h = (xf * r * g_ref[...]).astype(BF16)
        h_ref[...] = h
        for i, (r0, r1) in enumerate(IN_ROWS_MAIN):
            zm_ref[:, i * D_MODEL:(i + 1) * D_MODEL] = _dot_nt(h, w.rows(r0, r1)).astype(KEPT)
        zs_ref[...] = _dot_nt(h, w.rows(*IN_ROWS_ZS))

    row = lambda n: pl.BlockSpec((tm, n), lambda i: (i, 0))
    return pl.pallas_call(
        body, grid=(T // tm,),
        in_specs=[row(Dm), pl.BlockSpec((1, Dm), lambda i: (0, 0)), ANY],
        out_specs=[row(Dm), row(4 * D_MODEL), row(ZS_W)],
        out_shape=[jax.ShapeDtypeStruct((T, Dm), BF16), jax.ShapeDtypeStruct((T, 4 * D_MODEL), KEPT),
                   jax.ShapeDtypeStruct((T, ZS_W), F32)],
        scratch_shapes=[pltpu.VMEM(wt.shape, wt.dtype),
                        pltpu.SemaphoreType.DMA((_StreamedWeight.n_chunks(ranges),))],
        name="in_proj", compiler_params=_params("arbitrary"))(x, g, wt)


def _proj_bwd(acts, wt, terms, x, g, dres, name, w2=None, dep=None, rows=256):
    T, Dm = x.shape
    tm = _tile(T, rows)
    n_a = len(acts)
    n_in = n_a + 4 + (w2 is not None) + (dep is not None)
    n_out = 2 + (w2 is not None)
    ranges = [rng for _, _, rng in terms]

    def body(*refs):
        ins, outs, (wt_vmem, sems) = refs[:n_in], refs[n_in:n_in + n_out], refs[n_in + n_out:]
        wt_hbm, x_ref, g_ref, dres_ref = ins[n_a:n_a + 4]
        dx_ref, dg_ref = outs[0], outs[1]
        w = _StreamedWeight(wt_hbm, wt_vmem, sems, ranges)

        @pl.when(pl.program_id(0) == 0)
        def _():
            dg_ref[...] = jnp.zeros_like(dg_ref)

        dy = None
        for i, (c0, c1), (r0, r1) in terms:
            t = _dot(ins[i][:, c0:c1], w.rows(r0, r1))
            dy = t if dy is None else dy + t
        xf = x_ref[...]
        r = lax.rsqrt(jnp.mean(xf * xf, axis=-1, keepdims=True) + EPS)
        xh = xf * r
        dg_ref[...] += jnp.sum(dy * xh, axis=0, keepdims=True)
        dxh = dy * g_ref[...]
        dx = dres_ref[...] + r * (dxh - xh * jnp.mean(dxh * xh, axis=-1, keepdims=True))
        dx_ref[...] = dx
        if w2 is not None:
            outs[2][...] = _dot_nt(dx.astype(BF16), ins[n_a + 4][...]).astype(KEPT)

    row = pl.BlockSpec((tm, Dm), lambda i: (i, 0))
    vec = pl.BlockSpec((1, Dm), lambda i: (0, 0))
    in_specs = [pl.BlockSpec((tm, a.shape[1]), lambda i: (i, 0)) for a in acts]
    in_specs += [ANY, row, vec, row]
    args = [*acts, wt, x, g, dres]
    out_specs = [row, vec]
    out_shape = [jax.ShapeDtypeStruct((T, Dm), F32), jax.ShapeDtypeStruct((1, Dm), F32)]
    if w2 is not None:
        in_specs.append(pl.BlockSpec(w2.shape, lambda i: (0, 0)))
        args.append(w2)
        out_specs.append(pl.BlockSpec((tm, w2.shape[0]), lambda i: (i, 0)))
        out_shape.append(jax.ShapeDtypeStruct((T, w2.shape[0]), KEPT))
    if dep is not None:
        in_specs.append(ANY)
        args.append(dep)
    return pl.pallas_call(
        body, grid=(T // tm,), in_specs=in_specs, out_specs=out_specs, out_shape=out_shape,
        scratch_shapes=[pltpu.VMEM(wt.shape, wt.dtype),
                        pltpu.SemaphoreType.DMA((_StreamedWeight.n_chunks(ranges),))],
        name=name, compiler_params=_params("arbitrary"))(*args)


def _mm_tn(a, b, name, dep=None, rows=None, row0=0, into=None):
    T, M = a.shape
    N = b.shape[1]
    tm, tn, tt = _wide_tile(M), _wide_tile(N), _tile(T, 2048)
    n_t = T // tt
    off = row0 // tm
    extra = ([dep] if dep is not None else []) + ([into] if into is not None else [])

    def body(a_ref, b_ref, *refs):
        o_ref, acc_ref = refs[-2:]
        t = pl.program_id(2)

        @pl.when(t == 0)
        def _():
            acc_ref[...] = jnp.zeros_like(acc_ref)

        acc_ref[...] += _dot_tn(a_ref[...].astype(BF16), b_ref[...].astype(BF16))

        @pl.when(t == n_t - 1)
        def _():
            o_ref[...] = acc_ref[...].astype(BF16)

    return pl.pallas_call(
        body, grid=(M // tm, N // tn, n_t),
        in_specs=[pl.BlockSpec((tt, tm), lambda i, j, t: (t, i)),
                  pl.BlockSpec((tt, tn), lambda i, j, t: (t, j))] + [ANY] * len(extra),
        out_specs=pl.BlockSpec((tm, tn), lambda i, j, t: (i + off, j)),
        out_shape=jax.ShapeDtypeStruct((rows or M, N), BF16),
        scratch_shapes=[pltpu.VMEM((tm, tn), F32)],
        input_output_aliases={} if into is None else {1 + len(extra): 0},
        name=name, compiler_params=_params("parallel", "parallel", "arbitrary"))(a, b, *extra)


def _layer_norm_fwd(gv, g, b):
    mu = jnp.mean(gv, axis=-1, keepdims=True)
    xc = gv - mu
    rs = lax.rsqrt(jnp.mean(xc * xc, axis=-1, keepdims=True) + EPS)
    xh = xc * rs
    return xh, rs, xh * g + b


def _tri_mask(transposed=False):
    r = lax.broadcasted_iota(jnp.int32, (CHUNK, CHUNK), 0)
    c = lax.broadcasted_iota(jnp.int32, (CHUNK, CHUNK), 1)
    return r <= c if transposed else c <= r


def _mixer_a_fwd(zm, av_g, av_b, w_s, b_col):
    T = zm.shape[0]
    tm = _tile(T, 256)
    n_chunk = tm // CHUNK

    def body(u_ref, v_ref, ga_ref, g_ref, b_ref, w_ref, bc_ref, y_ref, vn_s, mx_s):
        gu = _gelu(u_ref[...].astype(F32))
        _, _, vn = _layer_norm_fwd(_gelu(v_ref[...].astype(F32)), g_ref[...], b_ref[...])
        vn_s[...] = vn.astype(BF16)
        tri = _tri_mask()
        for gi in range(A_GROUPS):
            wm = jnp.where(tri, w_ref[gi], 0.0).astype(BF16)
            cols = slice(gi * CHUNK, (gi + 1) * CHUNK)
            for n in range(n_chunk):
                rows = slice(n * CHUNK, (n + 1) * CHUNK)
                mx_s[rows, cols] = _dot(wm, vn_s[rows, cols]) + bc_ref[gi]
        y_ref[...] = (_sigmoid(ga_ref[...].astype(F32)) * gu * mx_s[...]).astype(KEPT)

    col = lambda c: pl.BlockSpec((tm, D_MODEL), lambda i: (i, c))
    vec = pl.BlockSpec((1, D_MODEL), lambda i: (0, 0))
    return pl.pallas_call(
        body, grid=(T // tm,),
        in_specs=[col(0), col(1), col(2), vec, vec,
                  pl.BlockSpec((A_GROUPS, CHUNK, CHUNK), lambda i: (0, 0, 0)),
                  pl.BlockSpec((A_GROUPS, CHUNK, 1), lambda i: (0, 0, 0))],
        out_specs=pl.BlockSpec((tm, D_MODEL), lambda i: (i, 0)),
        out_shape=jax.ShapeDtypeStruct((T, D_MODEL), KEPT),
        scratch_shapes=[pltpu.VMEM((tm, D_MODEL), BF16), pltpu.VMEM((tm, D_MODEL), F32)],
        name="mixer_a_fwd", compiler_params=_params("parallel"))(zm, zm, zm, av_g, av_b, w_s, b_col)


def _mixer_bwd(zm, o, dm, av_g, av_b, w_s, w_st, b_col, dep):
    T = zm.shape[0]
    tm = _tile(T, 256)
    n_chunk = tm // CHUNK

    def body(u_ref, v_ref, ga_ref, gb_ref, o_ref, dm_ref, g_ref, b_ref, w_ref, wt_ref, bc_ref, dep_ref,
             dz_ref, do_ref, dg_ref, db_ref, dw_ref, dbs_ref, vn_s, mx_s, dmx_s, dvn_s):
        @pl.when(pl.program_id(0) == 0)
        def _():
            dg_ref[...] = jnp.zeros_like(dg_ref)
            db_ref[...] = jnp.zeros_like(db_ref)
            dw_ref[...] = jnp.zeros_like(dw_ref)
            dbs_ref[...] = jnp.zeros_like(dbs_ref)

        dm_v = dm_ref[...].astype(F32)
        gb = gb_ref[...].astype(F32)
        sb = _sigmoid(gb)
        o_v = o_ref[...].astype(F32)
        do_ref[...] = (dm_v * sb).astype(BF16)
        dz_ref[:, 3 * D_MODEL:4 * D_MODEL] = (dm_v * o_v * sb * (1.0 - sb)).astype(BF16)
        u = u_ref[...].astype(F32)
        v = v_ref[...].astype(F32)
        gu, gu_grad = _gelu_and_grad(u)
        gv, gv_grad = _gelu_and_grad(v)
        xh, rs, vn = _layer_norm_fwd(gv, g_ref[...], b_ref[...])
        vn_s[...] = vn.astype(BF16)
        tri = _tri_mask()
        for gi in range(A_GROUPS):
            wm = jnp.where(tri, w_ref[gi], 0.0).astype(BF16)
            cols = slice(gi * CHUNK, (gi + 1) * CHUNK)
            for n in range(n_chunk):
                rows = slice(n * CHUNK, (n + 1) * CHUNK)
                mx_s[rows, cols] = _dot(wm, vn_s[rows, cols]) + bc_ref[gi]
        mixed = mx_s[...]
        sa = _sigmoid(ga_ref[...].astype(F32))
        dya = dm_v * sa
        dz_ref[:, 2 * D_MODEL:3 * D_MODEL] = (dm_v * gu * mixed * sa * (1.0 - sa)).astype(BF16)
        dz_ref[:, 0:D_MODEL] = (dya * mixed * gu_grad).astype(BF16)
        dmx = dya * gu
        dmx_s[...] = dmx.astype(BF16)
        tri_t = _tri_mask(transposed=True)
        for gi in range(A_GROUPS):
            wmt = jnp.where(tri_t, wt_ref[gi], 0.0).astype(BF16)
            cols = slice(gi * CHUNK, (gi + 1) * CHUNK)
            dw_acc = jnp.zeros((CHUNK, CHUNK), F32)
            dmx_sum = jnp.zeros((CHUNK, CHUNK), F32)
            for n in range(n_chunk):
                rows = slice(n * CHUNK, (n + 1) * CHUNK)
                blk = dmx_s[rows, cols]
                dvn_s[rows, cols] = _dot(wmt, blk)
                dw_acc = dw_acc + _dot_nt(blk, vn_s[rows, cols])
                dmx_sum = dmx_sum + dmx[rows, cols]
            dw_ref[gi] += jnp.where(tri, dw_acc, 0.0)
            dbs_ref[gi] += jnp.sum(dmx_sum, axis=-1, keepdims=True)
        dvn = dvn_s[...]
        dg_ref[...] += jnp.sum(dvn * xh, axis=0, keepdims=True)
        db_ref[...] += jnp.sum(dvn, axis=0, keepdims=True)
        dxh = dvn * g_ref[...]
        dgv = rs * (dxh - jnp.mean(dxh, axis=-1, keepdims=True)
                    - xh * jnp.mean(dxh * xh, axis=-1, keepdims=True))
        dz_ref[:, D_MODEL:2 * D_MODEL] = (dgv * gv_grad).astype(BF16)

    col = lambda c: pl.BlockSpec((tm, D_MODEL), lambda i: (i, c))
    row = pl.BlockSpec((tm, D_MODEL), lambda i: (i, 0))
    vec = pl.BlockSpec((1, D_MODEL), lambda i: (0, 0))
    wsp = pl.BlockSpec((A_GROUPS, CHUNK, CHUNK), lambda i: (0, 0, 0))
    bsp = pl.BlockSpec((A_GROUPS, CHUNK, 1), lambda i: (0, 0, 0))
    return pl.pallas_call(
        body, grid=(T // tm,),
        in_specs=[col(0), col(1), col(2), col(3), row, row, vec, vec, wsp, wsp, bsp, ANY],
        out_specs=[pl.BlockSpec((tm, 4 * D_MODEL), lambda i: (i, 0)), row, vec, vec, wsp, bsp],
        out_shape=[jax.ShapeDtypeStruct((T, 4 * D_MODEL), BF16), jax.ShapeDtypeStruct((T, D_MODEL), BF16),
                   jax.ShapeDtypeStruct((1, D_MODEL), F32), jax.ShapeDtypeStruct((1, D_MODEL), F32),
                   jax.ShapeDtypeStruct((A_GROUPS, CHUNK, CHUNK), F32),
                   jax.ShapeDtypeStruct((A_GROUPS, CHUNK, 1), F32)],
        scratch_shapes=[pltpu.VMEM((tm, D_MODEL), BF16), pltpu.VMEM((tm, D_MODEL), F32),
                        pltpu.VMEM((tm, D_MODEL), BF16), pltpu.VMEM((tm, D_MODEL), F32)],
        name="mixer_bwd", compiler_params=_params("arbitrary"))(
            zm, zm, zm, zm, o, dm, av_g, av_b, w_s, w_st, b_col, dep)


def _rope_tables(pos_ref, invf_ref):
    ang = pos_ref[...].astype(F32) * invf_ref[...]
    lane = lax.broadcasted_iota(jnp.int32, ang.shape, 1)
    cos, sin = jnp.cos(ang), jnp.sin(ang)
    c = jnp.where(lane < QK_ROPE, cos, 0.0)
    sa = jnp.where(lane < QK_ROPE // 2, -sin, 0.0)
    sb = jnp.where((lane >= QK_ROPE // 2) & (lane < QK_ROPE), sin, 0.0)
    return c, sa, sb


def _rope(blk, tabs):
    c, sa, sb = tabs
    return blk * c + pltpu.roll(blk, LANES - QK_ROPE // 2, 1) * sa + pltpu.roll(blk, QK_ROPE // 2, 1) * sb


def _rope_t(dout, tabs):
    c, sa, sb = tabs
    return dout * c + pltpu.roll(dout * sa, QK_ROPE // 2, 1) + pltpu.roll(dout * sb, LANES - QK_ROPE // 2, 1)


def _rms_small(x, g):
    r = lax.rsqrt(jnp.mean(x * x, axis=-1, keepdims=True) + EPS)
    xh = x * r
    return xh, r, xh * g


def _mla_prep_fwd(zs, pos, invf, qg, kvg, wuq_p, wukv):
    T = zs.shape[0]
    tm = _tile(T, 512)
    HW = MLA_HEADS * HEAD_PAD

    def body(zs_ref, pos_ref, invf_ref, qg_ref, kvg_ref, wq_ref, wkv_ref, q_ref, k_ref, v_ref):
        tabs = _rope_tables(pos_ref, invf_ref)
        _, _, cqn = _rms_small(zs_ref[:, 0:Q_LORA], qg_ref[...])
        _, _, ckvn = _rms_small(zs_ref[:, Q_LORA:Q_LORA + KV_LORA], kvg_ref[...])
        q = _dot_nt(cqn.astype(BF16), wq_ref[...]) * ATTN_SCALE
        kv = _dot(ckvn.astype(BF16), wkv_ref[...])
        kr = _rope(zs_ref[:, Q_LORA + KV_LORA:ZS_W], tabs).astype(BF16)
        for h in range(MLA_HEADS):
            b0 = h * HEAD_PAD
            q_ref[:, b0:b0 + QK_NOPE] = q[:, b0:b0 + QK_NOPE].astype(BF16)
            q_ref[:, b0 + QK_NOPE:b0 + HEAD_PAD] = _rope(q[:, b0 + QK_NOPE:b0 + HEAD_PAD], tabs).astype(BF16)
            k_ref[:, b0:b0 + QK_NOPE] = kv[:, b0:b0 + QK_NOPE].astype(BF16)
            k_ref[:, b0 + QK_NOPE:b0 + HEAD_PAD] = kr
            v_ref[:, h * V_HEAD:(h + 1) * V_HEAD] = kv[:, b0 + QK_NOPE:b0 + HEAD_PAD].astype(BF16)

    full = lambda a: pl.BlockSpec(a.shape, lambda i: (0,) * a.ndim)
    return pl.pallas_call(
        body, grid=(T // tm,),
        in_specs=[pl.BlockSpec((tm, ZS_W), lambda i: (i, 0)), pl.BlockSpec((tm, 1), lambda i: (i, 0)),
                  full(invf), full(qg), full(kvg), full(wuq_p), full(wukv)],
        out_specs=[pl.BlockSpec((tm, HW), lambda i: (i, 0)), pl.BlockSpec((tm, HW), lambda i: (i, 0)),
                   pl.BlockSpec((tm, D_MODEL), lambda i: (i, 0))],
        out_shape=[jax.ShapeDtypeStruct((T, HW), BF16), jax.ShapeDtypeStruct((T, HW), BF16),
                   jax.ShapeDtypeStruct((T, D_MODEL), BF16)],
        name="mla_prep_fwd", compiler_params=_params("parallel"))(zs, pos, invf, qg, kvg, wuq_p, wukv)


def _mla_prep_bwd(zs, pos, invf, qg, kvg, wuq_p, wukv, dq, dk, dv):
    T = zs.shape[0]
    tm = _tile(T, 256)
    HW = MLA_HEADS * HEAD_PAD

    def body(zs_ref, pos_ref, invf_ref, qg_ref, kvg_ref, wq_ref, wkv_ref, dq_ref, dk_ref, dv_ref,
             dzs_ref, cqn_ref, dqp_ref, ckvn_ref, dkv_ref, dqg_ref, dkvg_ref):
        @pl.when(pl.program_id(0) == 0)
        def _():
            dqg_ref[...] = jnp.zeros_like(dqg_ref)
            dkvg_ref[...] = jnp.zeros_like(dkvg_ref)

        tabs = _rope_tables(pos_ref, invf_ref)
        cqh, rq, cqn = _rms_small(zs_ref[:, 0:Q_LORA], qg_ref[...])
        ckvh, rkv, ckvn = _rms_small(zs_ref[:, Q_LORA:Q_LORA + KV_LORA], kvg_ref[...])
        cqn_ref[...] = cqn.astype(BF16)
        ckvn_ref[...] = ckvn.astype(BF16)
        dkr = jnp.zeros((tm, LANES), F32)
        for h in range(MLA_HEADS):
            b0 = h * HEAD_PAD
            dqp_ref[:, b0:b0 + QK_NOPE] = dq_ref[:, b0:b0 + QK_NOPE]
            dqp_ref[:, b0 + QK_NOPE:b0 + HEAD_PAD] = _rope_t(
                dq_ref[:, b0 + QK_NOPE:b0 + HEAD_PAD].astype(F32), tabs).astype(BF16)
            dkv_ref[:, b0:b0 + QK_NOPE] = dk_ref[:, b0:b0 + QK_NOPE]
            dkv_ref[:, b0 + QK_NOPE:b0 + HEAD_PAD] = dv_ref[:, h * V_HEAD:(h + 1) * V_HEAD]
            dkr = dkr + dk_ref[:, b0 + QK_NOPE:b0 + HEAD_PAD].astype(F32)
        dcqn = _dot(dqp_ref[...], wq_ref[...])
        dckvn = _dot_nt(dkv_ref[...], wkv_ref[...])
        dqg_ref[...] += jnp.sum(dcqn * cqh, axis=0, keepdims=True)
        dkvg_ref[...] += jnp.sum(dckvn * ckvh, axis=0, keepdims=True)
        dxh = dcqn * qg_ref[...]
        dzs_ref[:, 0:Q_LORA] = (rq * (dxh - cqh * jnp.mean(dxh * cqh, axis=-1, keepdims=True))).astype(BF16)
        dxh = dckvn * kvg_ref[...]
        dzs_ref[:, Q_LORA:Q_LORA + KV_LORA] = (
            rkv * (dxh - ckvh * jnp.mean(dxh * ckvh, axis=-1, keepdims=True))).astype(BF16)
        dzs_ref[:, Q_LORA + KV_LORA:ZS_W] = _rope_t(dkr, tabs).astype(BF16)

    full = lambda a: pl.BlockSpec(a.shape, lambda i: (0,) * a.ndim)
    rowb = lambda w: pl.BlockSpec((tm, w), lambda i: (i, 0))
    return pl.pallas_call(
        body, grid=(T // tm,),
        in_specs=[rowb(ZS_W), rowb(1), full(invf), full(qg), full(kvg), full(wuq_p), full(wukv),
                  rowb(HW), rowb(HW), rowb(D_MODEL)],
        out_specs=[rowb(ZS_W), rowb(Q_LORA), rowb(HW), rowb(KV_LORA), rowb(HW), full(qg), full(kvg)],
        out_shape=[jax.ShapeDtypeStruct((T, ZS_W), BF16), jax.ShapeDtypeStruct((T, Q_LORA), BF16),
                   jax.ShapeDtypeStruct((T, HW), BF16), jax.ShapeDtypeStruct((T, KV_LORA), BF16),
                   jax.ShapeDtypeStruct((T, HW), BF16), jax.ShapeDtypeStruct(qg.shape, F32),
                   jax.ShapeDtypeStruct(kvg.shape, F32)],
        name="mla_prep_bwd", compiler_params=_params("arbitrary"))(
            zs, pos, invf, qg, kvg, wuq_p, wukv, dq, dk, dv)


def _causal(tq, kmax, q0):
    r = lax.broadcasted_iota(jnp.int32, (tq, kmax), 0) + q0
    c = lax.broadcasted_iota(jnp.int32, (tq, kmax), 1)
    return c <= r


def _attn_fwd(q, k, v, batch, seq):
    tq = _tile(seq, ATTN_TILE)
    nq = seq // tq

    def body(q_ref, k_ref, v_ref, o_ref, lse_ref):
        diag = _causal(tq, tq, 0)
        for qi in range(nq):
            rows = slice(qi * tq, (qi + 1) * tq)
            qr = q_ref[rows, :]
            s_d = jnp.where(diag, _dot_nt(qr, k_ref[rows, :]), NEG_BIG)
            m = jnp.max(s_d, axis=-1, keepdims=True)
            if qi > 0:
                before = slice(0, qi * tq)
                s_b = _dot_nt(qr, k_ref[before, :])
                m = jnp.maximum(m, jnp.max(s_b, axis=-1, keepdims=True))
                p_b = jnp.exp(s_b - m)
                l = jnp.sum(p_b, axis=-1, keepdims=True)
                acc = _dot(p_b.astype(BF16), v_ref[before, :])
            p_d = jnp.exp(s_d - m)
            l_d = jnp.sum(p_d, axis=-1, keepdims=True)
            acc_d = _dot(p_d.astype(BF16), v_ref[rows, :])
            l, acc = (l + l_d, acc + acc_d) if qi > 0 else (l_d, acc_d)
            o_ref[rows, :] = (acc / l).astype(KEPT)
            lse_ref[rows, :] = jnp.broadcast_to(m + jnp.log(l), (tq, V_HEAD))

    return pl.pallas_call(
        body, grid=(batch, MLA_HEADS),
        in_specs=[pl.BlockSpec((seq, HEAD_PAD), lambda b, h: (b, h)),
                  pl.BlockSpec((seq, HEAD_PAD), lambda b, h: (b, h)),
                  pl.BlockSpec((seq, V_HEAD), lambda b, h: (b, h))],
        out_specs=[pl.BlockSpec((seq, V_HEAD), lambda b, h: (b, h)),
                   pl.BlockSpec((seq, V_HEAD), lambda b, h: (b, h))],
        out_shape=[jax.ShapeDtypeStruct((batch * seq, D_MODEL), KEPT),
                   jax.ShapeDtypeStruct((batch * seq, D_MODEL), F32)],
        name="attn_fwd", compiler_params=_params("parallel", "parallel"))(q, k, v)


def _attn_bwd(q, k, v, o, do, lse, batch, seq, dep):
    tq = _tile(seq, ATTN_TILE)
    nq = seq // tq

    def body(q_ref, k_ref, v_ref, o_ref, do_ref, lse_ref, dep_ref, dq_ref, dk_ref, dv_ref, dk_acc, dv_acc):
        dk_acc[...] = jnp.zeros_like(dk_acc)
        dv_acc[...] = jnp.zeros_like(dv_acc)
        for qi in range(nq):
            rows = slice(qi * tq, (qi + 1) * tq)
            kmax = (qi + 1) * tq
            qr = q_ref[rows, :]
            dor = do_ref[rows, :]
            kk = k_ref[0:kmax, :]
            s = _dot_nt(qr, kk)
            p = jnp.where(_causal(tq, kmax, qi * tq), jnp.exp(s - lse_ref[rows, 0:1]), 0.0)
            dp = _dot_nt(dor, v_ref[0:kmax, :])
            delta = jnp.sum(dor.astype(F32) * o_ref[rows, :].astype(F32), axis=-1, keepdims=True)
            ds = (p * (dp - delta)).astype(BF16)
            dq_ref[rows, :] = (_dot(ds, kk) * ATTN_SCALE).astype(BF16)
            dk_acc[0:kmax, :] += _dot_tn(ds, qr)
            dv_acc[0:kmax, :] += _dot_tn(p.astype(BF16), dor)
        dk_ref[...] = dk_acc[...].astype(BF16)
        dv_ref[...] = dv_acc[...].astype(BF16)

    qspec = pl.BlockSpec((seq, HEAD_PAD), lambda b, h: (b, h))
    vspec = pl.BlockSpec((seq, V_HEAD), lambda b, h: (b, h))
    T = batch * seq
    return pl.pallas_call(
        body, grid=(batch, MLA_HEADS),
        in_specs=[qspec, qspec, vspec, vspec, vspec, vspec, ANY],
        out_specs=[qspec, qspec, vspec],
        out_shape=[jax.ShapeDtypeStruct((T, MLA_HEADS * HEAD_PAD), BF16),
                   jax.ShapeDtypeStruct((T, MLA_HEADS * HEAD_PAD), BF16),
                   jax.ShapeDtypeStruct((T, D_MODEL), BF16)],
        scratch_shapes=[pltpu.VMEM((seq, HEAD_PAD), F32), pltpu.VMEM((seq, V_HEAD), F32)],
        name="attn_bwd", compiler_params=_params("parallel", "parallel"))(q, k, v, o, do, lse, dep)


def _merge_out(x, yag, zm, o, w_out, ffn_g):
    T = x.shape[0]
    tm = _tile(T, 512)

    def body(x_ref, ya_ref, gb_ref, o_ref, w_ref, g_ref, mg_ref, x1_ref, h2_ref):
        mg = (ya_ref[...].astype(F32) + _sigmoid(gb_ref[...].astype(F32)) * o_ref[...].astype(F32)).astype(BF16)
        mg_ref[...] = mg
        x1 = x_ref[...] + _dot(mg, w_ref[...])
        x1_ref[...] = x1
        r = lax.rsqrt(jnp.mean(x1 * x1, axis=-1, keepdims=True) + EPS)
        h2_ref[...] = (x1 * r * g_ref[...]).astype(BF16)

    row = pl.BlockSpec((tm, D_MODEL), lambda i: (i, 0))
    return pl.pallas_call(
        body, grid=(T // tm,),
        in_specs=[row, row, pl.BlockSpec((tm, D_MODEL), lambda i: (i, 3)), row,
                  pl.BlockSpec((D_MODEL, D_MODEL), lambda i: (0, 0)), pl.BlockSpec((1, D_MODEL), lambda i: (0, 0))],
        out_specs=[row, row, row],
        out_shape=[jax.ShapeDtypeStruct((T, D_MODEL), BF16), jax.ShapeDtypeStruct((T, D_MODEL), F32),
                   jax.ShapeDtypeStruct((T, D_MODEL), BF16)],
        name="merge_out", compiler_params=_params("parallel"))(x, yag, zm, o, w_out, ffn_g)


FF_TILE = 256
FF_BLOCKS = D_FF // FF_TILE
FFB_TILE = 256


def _shift_down(x, k):
    row = lax.broadcasted_iota(jnp.int32, x.shape, 0)
    return jnp.where(row >= k, pltpu.roll(x, k, 0), 0.0)


def _shift_up(x, k):
    n = x.shape[0]
    row = lax.broadcasted_iota(jnp.int32, x.shape, 0)
    return jnp.where(row < n - k, pltpu.roll(x, n - k, 0), 0.0)


def _conv(x, w_ref, b_ref):
    return b_ref[...] + w_ref[2:3, :] * x + w_ref[1:2, :] * _shift_down(x, 1) + w_ref[0:1, :] * _shift_down(x, 2)


EDGE = 16


def _conv_wrapped(x, w_ref, b_ref):
    return (b_ref[...] + w_ref[2:3, :] * x + w_ref[1:2, :] * pltpu.roll(x, 1, 0)
            + w_ref[0:1, :] * pltpu.roll(x, 2, 0))


def _up_act(h2, wt_up, cw, cb, batch, seq):
    def body(h_ref, wug_ref, wuv_ref, wg_ref, wv_ref, bg_ref, bv_ref, ug_ref, uv_ref, g_ref, v_ref, a_ref):
        h = h_ref[...]
        ug = _dot_nt(h, wug_ref[...])
        uv = _dot_nt(h, wuv_ref[...])
        ug_ref[...] = ug.astype(KEPT)
        uv_ref[...] = uv.astype(KEPT)

        def write(rows, gate, val):
            g_ref[rows, :] = gate.astype(KEPT)
            v_ref[rows, :] = val.astype(KEPT)
            a_ref[rows, :] = (gate * _sigmoid(gate) * val).astype(BF16)

        write(slice(None), _conv_wrapped(ug, wg_ref, bg_ref), _conv_wrapped(uv, wv_ref, bv_ref))
        write(slice(0, EDGE), _conv(ug[0:EDGE], wg_ref, bg_ref), _conv(uv[0:EDGE], wv_ref, bv_ref))

    blk = pl.BlockSpec((seq, FF_TILE), lambda b, j: (b, j))
    wup = lambda off: pl.BlockSpec((FF_TILE, D_MODEL), lambda b, j: (j + off, 0))
    wsp = lambda off: pl.BlockSpec((3, FF_TILE), lambda b, j: (0, j + off))
    bsp = lambda off: pl.BlockSpec((1, FF_TILE), lambda b, j: (0, j + off))
    T = batch * seq
    kept = jax.ShapeDtypeStruct((T, D_FF), KEPT)
    return pl.pallas_call(
        body, grid=(batch, FF_BLOCKS),
        in_specs=[pl.BlockSpec((seq, D_MODEL), lambda b, j: (b, 0)), wup(0), wup(FF_BLOCKS),
                  wsp(0), wsp(FF_BLOCKS), bsp(0), bsp(FF_BLOCKS)],
        out_specs=[blk] * 5,
        out_shape=[kept, kept, kept, kept, jax.ShapeDtypeStruct((T, D_FF), BF16)],
        name="up_act", compiler_params=_params("parallel", "arbitrary"))(h2, wt_up, wt_up, cw, cw, cb, cb)


def _ffn_act_bwd(upg, upv, gate, val, cw, dx2b, w_down, batch, seq):
    def half(du, x, w_ref, dx_ref, dw_ref):
        j = pl.program_id(1)
        n = du.shape[0]
        up1, up2 = pltpu.roll(du, n - 1, 0), pltpu.roll(du, n - 2, 0)
        dx_ref[...] = (w_ref[2:3, :] * du + w_ref[1:2, :] * up1 + w_ref[0:1, :] * up2).astype(BF16)
        tail = du[n - EDGE:n]
        dx_ref[n - EDGE:n, :] = (w_ref[2:3, :] * tail + w_ref[1:2, :] * _shift_up(tail, 1)
                                 + w_ref[0:1, :] * _shift_up(tail, 2)).astype(BF16)
        row = lax.broadcasted_iota(jnp.int32, (EDGE, du.shape[1]), 0)
        head, x_tail = du[0:EDGE], x[n - EDGE:n]
        wrap1 = jnp.sum(jnp.where(row >= EDGE - 1, pltpu.roll(head, EDGE - 1, 0), 0.0) * x_tail, axis=0, keepdims=True)
        wrap2 = jnp.sum(jnp.where(row >= EDGE - 2, pltpu.roll(head, EDGE - 2, 0), 0.0) * x_tail, axis=0, keepdims=True)
        dw_ref[j, 2:3, :] += jnp.sum(du * x, axis=0, keepdims=True)
        dw_ref[j, 1:2, :] += jnp.sum(up1 * x, axis=0, keepdims=True) - wrap1
        dw_ref[j, 0:1, :] += jnp.sum(up2 * x, axis=0, keepdims=True) - wrap2
        dw_ref[j, 3:4, :] += jnp.sum(du, axis=0, keepdims=True)

    def body(ug_ref, uv_ref, g_ref, v_ref, wg_ref, wv_ref, dx_ref, wd_ref, dg_ref, dv_ref, dwg_ref, dwv_ref):
        @pl.when((pl.program_id(0) == 0) & (pl.program_id(1) == 0))
        def _():
            dwg_ref[...] = jnp.zeros_like(dwg_ref)
            dwv_ref[...] = jnp.zeros_like(dwv_ref)

        gate, val = g_ref[...].astype(F32), v_ref[...].astype(F32)
        sg = _sigmoid(gate)
        dav = _dot_nt(dx_ref[...], wd_ref[...])
        half(dav * val * sg * (1.0 + gate * (1.0 - sg)), ug_ref[...].astype(F32), wg_ref, dg_ref, dwg_ref)
        half(dav * gate * sg, uv_ref[...].astype(F32), wv_ref, dv_ref, dwv_ref)

    nb = D_FF // FFB_TILE
    blk = pl.BlockSpec((seq, FFB_TILE), lambda b, j: (b, j))
    wsp = lambda off: pl.BlockSpec((3, FFB_TILE), lambda b, j: (0, j + off))
    acc = pl.BlockSpec((nb, 4, FFB_TILE), lambda b, j: (0, 0, 0))
    T = batch * seq
    dupg, dupv, dwg, dwv = pl.pallas_call(
        body, grid=(batch, nb),
        in_specs=[blk, blk, blk, blk, wsp(0), wsp(nb),
                  pl.BlockSpec((seq, D_MODEL), lambda b, j: (b, 0)),
                  pl.BlockSpec((FFB_TILE, D_MODEL), lambda b, j: (j, 0))],
        out_specs=[blk, blk, acc, acc],
        out_shape=[jax.ShapeDtypeStruct((T, D_FF), BF16), jax.ShapeDtypeStruct((T, D_FF), BF16),
                   jax.ShapeDtypeStruct((nb, 4, FFB_TILE), F32), jax.ShapeDtypeStruct((nb, 4, FFB_TILE), F32)],
        name="ffn_act_bwd", compiler_params=_params("arbitrary", "arbitrary"))(
            upg, upv, gate, val, cw, cw, dx2b, w_down)
    dwg, dwv = (jnp.transpose(a, (1, 0, 2)).reshape(4, D_FF) for a in (dwg, dwv))
    return dupg, dupv, dwg[:3], dwv[:3], dwg[3:], dwv[3:]


def _down_loss(a, w_down, x1, target, gfin):
    T = x1.shape[0]
    tm = _tile(T, 512)

    def body(a_ref, w_ref, x1_ref, t_ref, g_ref, dx_ref, dxb_ref, loss_ref, dg_ref):
        @pl.when(pl.program_id(0) == 0)
        def _():
            loss_ref[...] = jnp.zeros_like(loss_ref)
            dg_ref[...] = jnp.zeros_like(dg_ref)

        x2 = x1_ref[...] + _dot(a_ref[...], w_ref[...])
        r = lax.rsqrt(jnp.mean(x2 * x2, axis=-1, keepdims=True) + EPS)
        xh = x2 * r
        g = g_ref[...]
        diff = xh * g - t_ref[...]
        loss_ref[...] += 0.5 * jnp.sum(jnp.mean(diff * diff, axis=-1, keepdims=True))
        dy = diff * (1.0 / D_MODEL)
        dg_ref[...] += jnp.sum(dy * xh, axis=0, keepdims=True)
        dxh = dy * g
        dx = r * (dxh - xh * jnp.mean(dxh * xh, axis=-1, keepdims=True))
        dx_ref[...] = dx
        dxb_ref[...] = dx.astype(BF16)

    row = pl.BlockSpec((tm, D_MODEL), lambda i: (i, 0))
    vec = pl.BlockSpec((1, D_MODEL), lambda i: (0, 0))
    return pl.pallas_call(
        body, grid=(T // tm,),
        in_specs=[pl.BlockSpec((tm, D_FF), lambda i: (i, 0)),
                  pl.BlockSpec((D_FF, D_MODEL), lambda i: (0, 0)), row, row, vec],
        out_specs=[row, row, pl.BlockSpec((8, LANES), lambda i: (0, 0)), vec],
        out_shape=[jax.ShapeDtypeStruct((T, D_MODEL), F32), jax.ShapeDtypeStruct((T, D_MODEL), BF16),
                   jax.ShapeDtypeStruct((8, LANES), F32), jax.ShapeDtypeStruct((1, D_MODEL), F32)],
        name="down_loss", compiler_params=_params("arbitrary"))(a, w_down, x1, target, gfin)


def _local_step(x, positions, target, mix_norm, av_g, av_b, w_s, b_s, q_norm, kv_norm, ffn_norm, conv_b,
                final_norm, comm):
    batch, seq, _ = x.shape
    T = batch * seq
    x = x.reshape(T, D_MODEL)
    target = target.reshape(T, D_MODEL)
    pos = positions.reshape(T, 1)
    half = jnp.arange(0, QK_ROPE, 2, dtype=F32) / QK_ROPE
    inv_freq = 1.0 / (ROPE_THETA ** half)
    invf = jnp.concatenate([inv_freq, inv_freq, jnp.zeros((LANES - QK_ROPE,), F32)]).reshape(1, LANES)
    w_st = jnp.swapaxes(w_s, 1, 2)
    b_col = b_s.reshape(A_GROUPS, CHUNK, 1)

    wt_in = comm.in_weights()
    h, zm, zs = _in_proj(x, mix_norm, wt_in)
    yag = _mixer_a_fwd(zm, av_g, av_b, w_s, b_col)
    wuq_p, wukv, w_out = comm.mla_weights(after=yag)
    q, k, v = _mla_prep_fwd(zs, pos, invf, q_norm, kv_norm, wuq_p, wukv)
    o, lse = _attn_fwd(q, k, v, batch, seq)
    merged, x1, h2 = _merge_out(x, yag, zm, o, w_out, ffn_norm)
    wt_up, conv_w, w_down = comm.ffn_weights(after=merged)
    upg, upv, gate, val, act = _up_act(h2, wt_up, conv_w, conv_b, batch, seq)
    dx2, dx2b, loss_acc, d_final = _down_loss(act, w_down, x1, target, final_norm)

    d_wdown = _mm_tn(act, dx2b, "dw_down")
    dupg, dupv, dcwg, dcwv, dcbg, dcbv = _ffn_act_bwd(upg, upv, gate, val, conv_w, dx2b, w_down, batch, seq)
    d_wt_up = _mm_tn(dupv, h2, "dw_up_val", rows=2 * D_FF, row0=D_FF,
                     into=_mm_tn(dupg, h2, "dw_up_gate", rows=2 * D_FF))
    dx1, d_ffn_norm, dmerged = _proj_bwd(
        [dupg, dupv], wt_up, [(0, (0, D_FF), (0, D_FF)), (1, (0, D_FF), (D_FF, 2 * D_FF))],
        x1, ffn_norm, dx2, "up_proj_bwd", w2=w_out)
    d_wout = _mm_tn(merged, dx1, "dw_out")
    token = comm.send_ffn_grads(d_wdown, d_wt_up, jnp.concatenate([dcwg, dcwv], axis=1), d_wout)
    dzm, do, d_avg, d_avb, d_ws, d_bs = _mixer_bwd(zm, o, dmerged, av_g, av_b, w_s, w_st, b_col, token)
    token = comm.send_small_grads([
        d_avg, d_avb, _small_2d(d_ws), d_bs.reshape(A_GROUPS, CHUNK), d_ffn_norm,
        jnp.concatenate([dcbg, dcbv], axis=1), d_final])
    dq, dk, dv = _attn_bwd(q, k, v, o, do, lse, batch, seq, token)
    dzs, cqn, dqp, ckvn, dkv, d_qn, d_kvn = _mla_prep_bwd(zs, pos, invf, q_norm, kv_norm, wuq_p, wukv, dq, dk, dv)
    d_wt_main = _mm_tn(dzm, h, "dw_in_main")
    d_wt_zs = _mm_tn(dzs, h, "dw_in_small")
    token = comm.send_in_grads(d_wt_main, d_wt_zs)
    d_wuq_p = _mm_tn(dqp, cqn, "dw_uq", dep=token)
    d_wukv = _mm_tn(ckvn, dkv, "dw_ukv", dep=token)
    terms = [(0, (i * D_MODEL, (i + 1) * D_MODEL), rows) for i, rows in enumerate(IN_ROWS_MAIN)]
    terms.append((1, (0, ZS_W), IN_ROWS_ZS))
    dx, d_mix_norm = _proj_bwd([dzm, dzs], wt_in, terms, x, mix_norm, dx1, "in_proj_bwd", dep=token, rows=512)
    token = comm.send_late_grads(d_wuq_p, d_wukv, [d_qn, d_kvn, d_mix_norm, loss_acc])
    return dx.reshape(batch, seq, D_MODEL), token


MESH_ID = pl.DeviceIdType.MESH
EFFECT = pltpu.SideEffectType.DATAFLOW_SIDE_EFFECTING


def _mesh_pos():
    return lax.axis_index("x"), lax.axis_index("y"), lax.axis_index("c")


def _peer(pos, d):
    x, y, c = pos
    px = 1 - x if d & 4 else x
    py = 1 - y if d & 2 else y
    pc = 1 - c if d & 1 else c
    return (px, py, pc), 4 * px + 2 * py + pc


def _copy(src_ref, land_ref, send_sems, recv_sems, a, d, pos, exchange, landing_here):
    peer, pid = _peer(pos, d)
    me = 4 * pos[0] + 2 * pos[1] + pos[2]
    if exchange:
        src, dst = src_ref.at[pid], land_ref.at[d]
    else:
        src, dst = src_ref, land_ref.at[pid if landing_here else me]
    return pltpu.make_async_remote_copy(
        src_ref=src, dst_ref=dst, send_sem=send_sems.at[a * (N_DEV - 1) + d - 1],
        recv_sem=recv_sems.at[a * (N_DEV - 1) + d - 1],
        device_id=peer, device_id_type=MESH_ID)


def _start_copies(groups, modes, name, dep=None):
    sizes = [len(g) for g in groups]
    srcs = [s for g in groups for s in g]
    lands = [lax.empty(s.shape if modes[gi] else (N_DEV,) + s.shape, s.dtype)
             for gi, g in enumerate(groups) for s in g]
    n, ng = len(srcs), len(groups)
    n_in = 2 * n + (dep is not None)

    def body(*refs):
        src_refs, land_refs = refs[:n], refs[n:2 * n]
        sems = refs[n_in:n_in + 3 * ng]
        token = refs[-1]
        pos = _mesh_pos()
        k = 0
        for gi, size in enumerate(sizes):
            for a in range(size):
                _own_copy(src_refs[k], land_refs[k], sems[3 * gi + 2], a, pos, modes[gi]).start()
                for d in range(1, N_DEV):
                    _copy(src_refs[k], land_refs[k], sems[3 * gi], sems[3 * gi + 1], a, d, pos, modes[gi],
                          landing_here=False).start()
                k += 1
        token[...] = jnp.zeros_like(token)

    sem_shapes = []
    for size in sizes:
        remote = pltpu.SemaphoreType.DMA((size * (N_DEV - 1),))
        sem_shapes += [remote, remote, pltpu.SemaphoreType.DMA((size,))]
    out = pl.pallas_call(
        body, name=name,
        out_shape=(*sem_shapes, *[pltpu.HBM(a.shape, a.dtype) for a in srcs + lands],
                   jax.ShapeDtypeStruct((8, LANES), F32)),
        in_specs=[HBM] * (2 * n) + [ANY] * (dep is not None),
        out_specs=(*[SEM] * (3 * ng), *[HBM] * (2 * n), pl.BlockSpec(memory_space=pltpu.VMEM)),
        input_output_aliases={i: 3 * ng + i for i in range(2 * n)},
        compiler_params=pltpu.CompilerParams(has_side_effects=EFFECT),
    )(*[pltpu.with_memory_space_constraint(a, pltpu.HBM) for a in srcs + lands], *([dep] if dep is not None else []))
    thru = out[3 * ng:3 * ng + 2 * n]
    handles, k = [], 0
    for gi, size in enumerate(sizes):
        handles.append((out[3 * gi:3 * gi + 3], thru[k:k + size], thru[n + k:n + k + size]))
        k += size
    return handles, out[-1]


def _own_copy(src_ref, land_ref, local_sems, a, pos, exchange):
    me = 4 * pos[0] + 2 * pos[1] + pos[2]
    src, dst = (src_ref.at[me], land_ref.at[0]) if exchange else (src_ref, land_ref.at[me])
    return pltpu.make_async_copy(src, dst, local_sems.at[a])


def _wait_copies(handle, exchange, after, name):
    sems, srcs, lands = handle
    n = len(srcs)

    def body(*refs):
        src_refs, land_refs = refs[:n], refs[n:2 * n]
        send, recv, local = refs[2 * n:2 * n + 3]
        pos = _mesh_pos()
        for a in range(n):
            _own_copy(src_refs[a], land_refs[a], local, a, pos, exchange).wait()
            for d in range(1, N_DEV):
                cp = _copy(src_refs[a], land_refs[a], send, recv, a, d, pos, exchange, landing_here=True)
                cp.wait_send()
                cp.wait_recv()

    out = pl.pallas_call(
        body, name=name,
        out_shape=tuple(pltpu.HBM(a.shape, a.dtype) for a in (*srcs, *lands)),
        in_specs=[HBM] * (2 * n) + [SEM, SEM, SEM, ANY], out_specs=[HBM] * (2 * n),
        input_output_aliases={i: i for i in range(2 * n)},
        compiler_params=pltpu.CompilerParams(has_side_effects=EFFECT),
    )(*srcs, *lands, *sems, after)
    return out[n:]


def _gather_now(a, name):
    def body(x_ref, out_ref, send_sems, recv_sems, local_sem):
        x, y, c = _mesh_pos()
        me, sibling = (x, y, c), (x, y, 1 - c)
        chips = [(1 - x, y), (x, 1 - y), (1 - x, 1 - y)]

        def slot(p):
            return out_ref.at[4 * p[0] + 2 * p[1] + p[2]]

        def copy(k, block, to, src=None):
            return pltpu.make_async_remote_copy(
                src_ref=slot(block) if src is None else src, dst_ref=slot(block), send_sem=send_sems.at[k],
                recv_sem=recv_sems.at[k], device_id=to, device_id_type=MESH_ID)

        mine = pltpu.make_async_copy(x_ref, slot(me), local_sem)
        mine.start()
        first = [copy(0, me, sibling, src=x_ref)]
        first += [copy(1 + j, me, (*chip, c), src=x_ref) for j, chip in enumerate(chips)]
        for cp in first:
            cp.start()
        passed = [copy(4 + j, (*chip, c), sibling) for j, chip in enumerate(chips)]
        for j, chip in enumerate(chips):
            copy(1 + j, (*chip, c), me).wait_recv()
            passed[j].start()
        copy(0, sibling, me).wait_recv()
        for j, chip in enumerate(chips):
            copy(4 + j, (*chip, 1 - c), me).wait_recv()
        for cp in first + passed:
            cp.wait_send()
        mine.wait()

    return pl.pallas_call(
        body, in_specs=[ANY], out_specs=ANY,
        out_shape=jax.ShapeDtypeStruct((N_DEV,) + a.shape, a.dtype),
        scratch_shapes=[pltpu.SemaphoreType.DMA((N_DEV - 1,)), pltpu.SemaphoreType.DMA((N_DEV - 1,)),
                        pltpu.SemaphoreType.DMA],
        name=name, compiler_params=pltpu.CompilerParams(has_side_effects=True))(a)


def _sum_parts(p_ref):
    g = p_ref[0].astype(F32)
    for k in range(1, N_DEV):
        g = g + p_ref[k].astype(F32)
    return g


def _adamw_update(p_ref, w_ref, m_ref, v_ref, g_ref, d_ref, nm_ref, nv_ref):
    c1 = 1.0 - ADAM_B1 ** ADAM_STEP
    c2 = 1.0 - ADAM_B2 ** ADAM_STEP
    g = _sum_parts(p_ref)
    nm = ADAM_B1 * m_ref[...] + (1.0 - ADAM_B1) * g
    nv = ADAM_B2 * v_ref[...] + (1.0 - ADAM_B2) * (g * g)
    g_ref[...] = g
    nm_ref[...] = nm
    nv_ref[...] = nv
    d_ref[...] = -ADAM_LR * ((nm / c1) / (jnp.sqrt(nv / c2) + ADAM_EPS) + ADAM_WD * w_ref[...])


def _adamw_many(parts, ws, ms, vs, sums, name):
    n, ns = len(ws), len(sums)

    def body(*refs):
        ins, outs = refs[:4 * n + ns], refs[4 * n + ns:]
        for i in range(n):
            _adamw_update(ins[i], ins[n + i], ins[2 * n + i], ins[3 * n + i],
                          outs[i], outs[n + i], outs[2 * n + i], outs[3 * n + i])
        for i in range(ns):
            outs[4 * n + i][...] = _sum_parts(ins[4 * n + i])

    full = lambda a: pl.BlockSpec(a.shape, lambda: (0,) * a.ndim)
    args = [*parts, *ws, *ms, *vs, *sums]
    outs = [jax.ShapeDtypeStruct(w.shape, F32) for _ in range(4) for w in ws]
    outs += [jax.ShapeDtypeStruct(s.shape[1:], F32) for s in sums]
    res = pl.pallas_call(
        body, in_specs=[full(a) for a in args], out_specs=[full(o) for o in outs], out_shape=outs,
        name=name, compiler_params=pltpu.CompilerParams(vmem_limit_bytes=VMEM_LIMIT))(*args)
    return res[:n], res[n:2 * n], res[2 * n:3 * n], res[3 * n:4 * n], res[4 * n:]


def _adamw(parts, w, m, v, name):
    R, C = w.shape
    tr, tc = R, C
    if N_DEV * R * C * parts.dtype.itemsize > SMALL_BLOCK_BYTES:
        tr = next((t for t in range(min(R, 256) // 16 * 16, 15, -16) if R % t == 0), R)
        if tr == R:
            tc = _tile(C, 256)

    def body(p_ref, w_ref, m_ref, v_ref, g_ref, d_ref, nm_ref, nv_ref):
        _adamw_update(p_ref, w_ref, m_ref, v_ref, g_ref, d_ref, nm_ref, nv_ref)

    blk = pl.BlockSpec((tr, tc), lambda i, j: (i, j))
    shp = jax.ShapeDtypeStruct((R, C), F32)
    return pl.pallas_call(
        body, grid=(R // tr, C // tc),
        in_specs=[pl.BlockSpec((N_DEV, tr, tc), lambda i, j: (0, i, j)), blk, blk, blk],
        out_specs=[blk, blk, blk, blk], out_shape=[shp, shp, shp, shp],
        name=name, compiler_params=_params("parallel", "parallel"))(parts, w, m, v)


SPLIT_V = 2 * D_MODEL
SPLIT_KR = SPLIT_V + Q_LORA + KV_LORA + QK_ROPE
IN_DIM = SPLIT_KR + 2 * D_MODEL
IN_ROWS_MAIN = ((0, D_MODEL), (D_MODEL, SPLIT_V), (SPLIT_KR, SPLIT_KR + D_MODEL), (SPLIT_KR + D_MODEL, IN_DIM))
IN_ROWS_ZS = (SPLIT_V, SPLIT_V + ZS_W)

SMALL_EARLY = ("a_v_norm_g", "a_v_norm_b", "a_spatial_w", "a_spatial_b", "ffn_norm", "conv_b", "final_norm")
SMALL_LATE = ("q_a_norm", "kv_a_norm", "mix_norm")


def _small_2d(a):
    return a.reshape(-1, a.shape[-1])


def _cols_from_shards(g):
    return jnp.transpose(g, (1, 0, 2)).reshape(g.shape[1], N_DEV * g.shape[2])


def _shards_from_cols(a):
    R, W = a.shape
    return jnp.transpose(a.reshape(R, N_DEV, W // N_DEV), (1, 0, 2))


class _Comm:
    GATHER_GROUPS = (("w_uq", "w_ukv", "w_out"), ("w_up", "conv_w", "w_down"))
    FFN_GRADS = ("w_down", "w_up", "conv_w", "w_out")
    LATE_GRADS = ("w_uq", "w_ukv")
    TRANSPOSED = ("w_in", "w_up", "w_uq")

    def __init__(self, shards):
        local = {n: a.astype(F32 if n == "conv_w" else BF16) for n, a in shards.items()}
        self.g_in = _gather_now(local["w_in"], "gather_w_in")
        groups = [[local[n] for n in g] for g in self.GATHER_GROUPS]
        (self.h_mla, self.h_ffn), _ = _start_copies(groups, [False] * 2, "gather_start", dep=self.g_in)

    def in_weights(self):
        return self.g_in.reshape(IN_DIM, D_MODEL)

    def mla_weights(self, after):
        g_uq, g_ukv, g_out = _wait_copies(self.h_mla, False, after, "gather_wait_mla")
        wuq_p = jnp.pad(g_uq, ((0, 0), (0, HEAD_PAD - QK_HEAD), (0, 0))).reshape(MLA_HEADS * HEAD_PAD, Q_LORA)
        return wuq_p, _cols_from_shards(g_ukv), g_out.reshape(D_MODEL, D_MODEL)

    def ffn_weights(self, after):
        g_up, g_cw, g_down = _wait_copies(self.h_ffn, False, after, "gather_wait_ffn")
        return g_up.reshape(2 * D_FF, D_MODEL), _cols_from_shards(g_cw), g_down.reshape(D_FF, D_MODEL)

    def send_ffn_grads(self, d_wdown, d_wt_up, d_convw, d_wout):
        group = [d_wdown.reshape(N_DEV, D_FF // N_DEV, D_MODEL), d_wt_up.reshape(N_DEV, 2 * D_FF // N_DEV, D_MODEL),
                 _shards_from_cols(d_convw), d_wout.reshape(N_DEV, D_MODEL // N_DEV, D_MODEL)]
        (self.h_ffn_grads,), token = _start_copies([group], [True], "ffn_grads_start")
        return token

    def send_small_grads(self, grads):
        (self.h_small_early,), token = _start_copies([grads], [False], "small_grads_start")
        return token

    def send_in_grads(self, d_wt_main, d_wt_zs):
        d_in = jnp.concatenate([d_wt_main[:SPLIT_V], d_wt_zs[:SPLIT_KR - SPLIT_V], d_wt_main[SPLIT_V:]], axis=0)
        blocks = d_in.reshape(N_DEV, IN_DIM // N_DEV, D_MODEL)
        (self.h_in_grads,), token = _start_copies([[blocks]], [True], "in_grads_start")
        return token

    def send_late_grads(self, d_wuq_p, d_wukv, small):
        d_uq = d_wuq_p.reshape(MLA_HEADS, HEAD_PAD, Q_LORA)[:, :QK_HEAD, :]
        (self.h_late_grads, self.h_late_small), token = _start_copies(
            [[d_uq, _shards_from_cols(d_wukv)], small], [True, False], "late_grads_start")
        return token


def kernel(x, positions, mix_norm, w_in, a_v_norm_g, a_v_norm_b, a_spatial_w, a_spatial_b, q_a_norm, w_uq, kv_a_norm, w_ukv, w_out, ffn_norm, w_up, conv_w, conv_b, w_down, final_norm, loss_target, m_mix_norm, m_w_in, m_a_v_norm_g, m_a_v_norm_b, m_a_spatial_w, m_a_spatial_b, m_q_a_norm, m_w_uq, m_kv_a_norm, m_w_ukv, m_w_out, m_ffn_norm, m_w_up, m_conv_w, m_conv_b, m_w_down, m_final_norm, v_mix_norm, v_w_in, v_a_v_norm_g, v_a_v_norm_b, v_a_spatial_w, v_a_spatial_b, v_q_a_norm, v_w_uq, v_kv_a_norm, v_w_ukv, v_w_out, v_ffn_norm, v_w_up, v_conv_w, v_conv_b, v_w_down, v_final_norm):
    names = ("mix_norm", "w_in", "a_v_norm_g", "a_v_norm_b", "a_spatial_w", "a_spatial_b", "q_a_norm", "w_uq",
             "kv_a_norm", "w_ukv", "w_out", "ffn_norm", "w_up", "conv_w", "conv_b", "w_down", "final_norm")
    w = dict(zip(names, (mix_norm, w_in, a_v_norm_g, a_v_norm_b, a_spatial_w, a_spatial_b, q_a_norm, w_uq,
                         kv_a_norm, w_ukv, w_out, ffn_norm, w_up, conv_w, conv_b, w_down, final_norm)))
    m = dict(zip(names, (m_mix_norm, m_w_in, m_a_v_norm_g, m_a_v_norm_b, m_a_spatial_w, m_a_spatial_b,
                         m_q_a_norm, m_w_uq, m_kv_a_norm, m_w_ukv, m_w_out, m_ffn_norm, m_w_up, m_conv_w,
                         m_conv_b, m_w_down, m_final_norm)))
    v = dict(zip(names, (v_mix_norm, v_w_in, v_a_v_norm_g, v_a_v_norm_b, v_a_spatial_w, v_a_spatial_b,
                         v_q_a_norm, v_w_uq, v_kv_a_norm, v_w_ukv, v_w_out, v_ffn_norm, v_w_up, v_conv_w,
                         v_conv_b, v_w_down, v_final_norm)))
    shapes = {n: w[n].shape for n in names}
    def view(tree, n):
        a = tree[n].reshape(tree[n].shape[-2:])
        return a.T if n in _Comm.TRANSPOSED else a

    comm = _Comm({n: view(w, n) for n in ("w_in",) + _Comm.GATHER_GROUPS[0] + _Comm.GATHER_GROUPS[1]})

    grad_x, token = _local_step(
        x, positions, loss_target, w["mix_norm"], w["a_v_norm_g"], w["a_v_norm_b"], w["a_spatial_w"][0],
        w["a_spatial_b"][0], w["q_a_norm"], w["kv_a_norm"], w["ffn_norm"], w["conv_b"],
        w["final_norm"].reshape(1, D_MODEL), comm)

    out_g, out_d, out_m, out_v = {}, {}, {}, {}

    def update(n, parts):
        res = _adamw(parts, view(w, n), view(m, n), view(v, n), "adamw_" + n)
        out_g[n], out_d[n], out_m[n], out_v[n] = (
            (t.T if n in _Comm.TRANSPOSED else t).reshape(shapes[n]) for t in res)
        return res[1]

    def update_small(names, parts, sums, name):
        res = _adamw_many(parts, *[[_small_2d(t[n]) for n in names] for t in (w, m, v)], sums, name)
        for i, n in enumerate(names):
            out_g[n], out_d[n], out_m[n], out_v[n] = (r[i].reshape(shapes[n]) for r in res[:4])
        return res

    for n, parts in zip(_Comm.FFN_GRADS, _wait_copies(comm.h_ffn_grads, True, token, "ffn_grads_wait")):
        last = update(n, parts)
    early = _wait_copies(comm.h_small_early, False, last, "small_grads_wait")
    last = update_small(SMALL_EARLY, early, [], "adamw_small")[1][0]
    last = update("w_in", _wait_copies(comm.h_in_grads, True, last, "in_grads_wait")[0])
    for n, parts in zip(_Comm.LATE_GRADS, _wait_copies(comm.h_late_grads, True, last, "late_grads_wait")):
        last = update(n, parts)
    late = _wait_copies(comm.h_late_small, False, last, "late_small_wait")
    res = update_small(SMALL_LATE, late[:-1], late[-1:], "adamw_late")
    loss = res[4][0][0, 0]

    return (loss, grad_x, *[out_g[n] for n in names], *[out_d[n] for n in names],
            *[out_m[n] for n in names], *[out_v[n] for n in names])
```

```python
import math

import jax
import jax.numpy as jnp
from jax import lax
from jax.experimental import pallas as pl
from jax.experimental.pallas import tpu as pltpu

F32 = jnp.float32
BF16 = jnp.bfloat16
KEPT = jnp.bfloat16

N_DEV = 8
D_MODEL = 1024
EPS = 1e-6
A_GROUPS = 8
CHUNK = 128
MLA_HEADS = 8
QK_NOPE = 128
QK_ROPE = 64
QK_HEAD = QK_NOPE + QK_ROPE
HEAD_PAD = 256
V_HEAD = 128
Q_LORA = 256
KV_LORA = 128
ROPE_THETA = 10000.0
D_FF = 2816
ZS_W = 512
ATTN_SCALE = QK_HEAD ** -0.5
ATTN_TILE = 512
NEG_BIG = -1e30

ADAM_LR = 0.001
ADAM_B1 = 0.9
ADAM_B2 = 0.999
ADAM_EPS = 1e-08
ADAM_WD = 0.01
ADAM_STEP = 10

VMEM_LIMIT = 56 * 1024 * 1024
SMALL_BLOCK_BYTES = 5 * 1024 * 1024
LANES = 128

GELU_K = math.sqrt(2.0 / math.pi)
GELU_C = 0.044715

ANY = pl.BlockSpec(memory_space=pl.ANY)
HBM = pl.BlockSpec(memory_space=pltpu.HBM)
SEM = pl.BlockSpec(memory_space=pltpu.SEMAPHORE)


def _tile(n, pref):
    for t in (pref, 512, 256, 128, 64, 32, 16, 8):
        if t <= pref and n % t == 0:
            return t
    return n


def _wide_tile(n, cap=1408):
    return next((t for t in range(min(n, cap) // LANES * LANES, 0, -LANES) if n % t == 0), n)


def _params(*sem):
    return pltpu.CompilerParams(dimension_semantics=sem, vmem_limit_bytes=VMEM_LIMIT)


def _dot(a, b):
    return jnp.dot(a, b, preferred_element_type=F32)


def _dot_nt(a, b):
    return lax.dot_general(a, b, (((1,), (1,)), ((), ())), preferred_element_type=F32)


def _dot_tn(a, b):
    return lax.dot_general(a, b, (((0,), (0,)), ((), ())), preferred_element_type=F32)


def _sigmoid(x):
    return 1.0 / (1.0 + jnp.exp(-x))


def _gelu(x):
    t = jnp.tanh(GELU_K * (x + GELU_C * x * x * x))
    return 0.5 * x * (1.0 + t)


def _gelu_and_grad(x):
    x2 = x * x
    t = jnp.tanh(GELU_K * (x + GELU_C * x * x2))
    half = 0.5 * (1.0 + t)
    return x * half, half + 0.5 * x * (1.0 - t * t) * GELU_K * (1.0 + 3.0 * GELU_C * x2)


def _in_proj(x, g, wt):
    T, Dm = x.shape
    tm = _tile(T, 512)

    def body(x_ref, g_ref, wt_ref, h_ref, zm_ref, zs_ref):
        xf = x_ref[...]
        r = lax.rsqrt(jnp.mean(xf * xf, axis=-1, keepdims=True) + EPS)
        h = (xf * r * g_ref[...]).astype(BF16)
        h_ref[...] = h
        for i, (r0, r1) in enumerate(IN_ROWS_MAIN):
            zm_ref[:, i * D_MODEL:(i + 1) * D_MODEL] = _dot_nt(h, wt_ref[r0:r1, :]).astype(KEPT)
        zs_ref[...] = _dot_nt(h, wt_ref[IN_ROWS_ZS[0]:IN_ROWS_ZS[1], :])

    row = lambda n: pl.BlockSpec((tm, n), lambda i: (i, 0))
    return pl.pallas_call(
        body, grid=(T // tm,),
        in_specs=[row(Dm), pl.BlockSpec((1, Dm), lambda i: (0, 0)), pl.BlockSpec(wt.shape, lambda i: (0, 0))],
        out_specs=[row(Dm), row(4 * D_MODEL), row(ZS_W)],
        out_shape=[jax.ShapeDtypeStruct((T, Dm), BF16), jax.ShapeDtypeStruct((T, 4 * D_MODEL), KEPT),
                   jax.ShapeDtypeStruct((T, ZS_W), F32)],
        name="in_proj", compiler_params=_params("parallel"))(x, g, wt)


def _proj_bwd(acts, wt, terms, x, g, dres, name, w2=None, dep=None, rows=256):
    T, Dm = x.shape
    tm = _tile(T, rows)
    n_a = len(acts)

    def body(*refs):
        ins, outs = refs[:n_a + 4 + (w2 is not None) + (dep is not None)], refs[-2 - (w2 is not None):]
        wt_ref, x_ref, g_ref, dres_ref = ins[n_a:n_a + 4]
        dx_ref, dg_ref = outs[0], outs[1]

        @pl.when(pl.program_id(0) == 0)
        def _():
            dg_ref[...] = jnp.zeros_like(dg_ref)

        dy = None
        for i, (c0, c1), (r0, r1) in terms:
            t = _dot(ins[i][:, c0:c1], wt_ref[r0:r1, :])
            dy = t if dy is None else dy + t
        xf = x_ref[...]
        r = lax.rsqrt(jnp.mean(xf * xf, axis=-1, keepdims=True) + EPS)
        xh = xf * r
        dg_ref[...] += jnp.sum(dy * xh, axis=0, keepdims=True)
        dxh = dy * g_ref[...]
        dx = dres_ref[...] + r * (dxh - xh * jnp.mean(dxh * xh, axis=-1, keepdims=True))
        dx_ref[...] = dx
        if w2 is not None:
            outs[2][...] = _dot_nt(dx.astype(BF16), ins[n_a + 4][...]).astype(KEPT)

    row = pl.BlockSpec((tm, Dm), lambda i: (i, 0))
    vec = pl.BlockSpec((1, Dm), lambda i: (0, 0))
    in_specs = [pl.BlockSpec((tm, a.shape[1]), lambda i: (i, 0)) for a in acts]
    in_specs += [pl.BlockSpec(wt.shape, lambda i: (0, 0)), row, vec, row]
    args = [*acts, wt, x, g, dres]
    out_specs = [row, vec]
    out_shape = [jax.ShapeDtypeStruct((T, Dm), F32), jax.ShapeDtypeStruct((1, Dm), F32)]
    if w2 is not None:
        in_specs.append(pl.BlockSpec(w2.shape, lambda i: (0, 0)))
        args.append(w2)
        out_specs.append(pl.BlockSpec((tm, w2.shape[0]), lambda i: (i, 0)))
        out_shape.append(jax.ShapeDtypeStruct((T, w2.shape[0]), KEPT))
    if dep is not None:
        in_specs.append(ANY)
        args.append(dep)
    return pl.pallas_call(
        body, grid=(T // tm,), in_specs=in_specs, out_specs=out_specs, out_shape=out_shape,
        name=name, compiler_params=_params("arbitrary"))(*args)


def _mm_tn(a, b, name, dep=None, rows=None, row0=0, into=None):
    T, M = a.shape
    N = b.shape[1]
    tm, tn, tt = _wide_tile(M), _wide_tile(N), _tile(T, 2048)
    n_t = T // tt
    off = row0 // tm
    extra = ([dep] if dep is not None else []) + ([into] if into is not None else [])

    def body(a_ref, b_ref, *refs):
        o_ref, acc_ref = refs[-2:]
        t = pl.program_id(2)

        @pl.when(t == 0)
        def _():
            acc_ref[...] = jnp.zeros_like(acc_ref)

        acc_ref[...] += _dot_tn(a_ref[...].astype(BF16), b_ref[...].astype(BF16))

        @pl.when(t == n_t - 1)
        def _():
            o_ref[...] = acc_ref[...].astype(BF16)

    return pl.pallas_call(
        body, grid=(M // tm, N // tn, n_t),
        in_specs=[pl.BlockSpec((tt, tm), lambda i, j, t: (t, i)),
                  pl.BlockSpec((tt, tn), lambda i, j, t: (t, j))] + [ANY] * len(extra),
        out_specs=pl.BlockSpec((tm, tn), lambda i, j, t: (i + off, j)),
        out_shape=jax.ShapeDtypeStruct((rows or M, N), BF16),
        scratch_shapes=[pltpu.VMEM((tm, tn), F32)],
        input_output_aliases={} if into is None else {1 + len(extra): 0},
        name=name, compiler_params=_params("parallel", "parallel", "arbitrary"))(a, b, *extra)


def _layer_norm_fwd(gv, g, b):
    mu = jnp.mean(gv, axis=-1, keepdims=True)
    xc = gv - mu
    rs = lax.rsqrt(jnp.mean(xc * xc, axis=-1, keepdims=True) + EPS)
    xh = xc * rs
    return xh, rs, xh * g + b


def _tri_mask(transposed=False):
    r = lax.broadcasted_iota(jnp.int32, (CHUNK, CHUNK), 0)
    c = lax.broadcasted_iota(jnp.int32, (CHUNK, CHUNK), 1)
    return r <= c if transposed else c <= r


def _mixer_a_fwd(zm, av_g, av_b, w_s, b_col):
    T = zm.shape[0]
    tm = _tile(T, 512)
    n_chunk = tm // CHUNK

    def body(u_ref, v_ref, ga_ref, g_ref, b_ref, w_ref, bc_ref, y_ref, vn_s, mx_s):
        gu = _gelu(u_ref[...].astype(F32))
        _, _, vn = _layer_norm_fwd(_gelu(v_ref[...].astype(F32)), g_ref[...], b_ref[...])
        vn_s[...] = vn.astype(BF16)
        tri = _tri_mask()
        for gi in range(A_GROUPS):
            wm = jnp.where(tri, w_ref[gi], 0.0).astype(BF16)
            cols = slice(gi * CHUNK, (gi + 1) * CHUNK)
            for n in range(n_chunk):
                rows = slice(n * CHUNK, (n + 1) * CHUNK)
                mx_s[rows, cols] = _dot(wm, vn_s[rows, cols]) + bc_ref[gi]
        y_ref[...] = (_sigmoid(ga_ref[...].astype(F32)) * gu * mx_s[...]).astype(KEPT)

    col = lambda c: pl.BlockSpec((tm, D_MODEL), lambda i: (i, c))
    vec = pl.BlockSpec((1, D_MODEL), lambda i: (0, 0))
    return pl.pallas_call(
        body, grid=(T // tm,),
        in_specs=[col(0), col(1), col(2), vec, vec,
                  pl.BlockSpec((A_GROUPS, CHUNK, CHUNK), lambda i: (0, 0, 0)),
                  pl.BlockSpec((A_GROUPS, CHUNK, 1), lambda i: (0, 0, 0))],
        out_specs=pl.BlockSpec((tm, D_MODEL), lambda i: (i, 0)),
        out_shape=jax.ShapeDtypeStruct((T, D_MODEL), KEPT),
        scratch_shapes=[pltpu.VMEM((tm, D_MODEL), BF16), pltpu.VMEM((tm, D_MODEL), F32)],
        name="mixer_a_fwd", compiler_params=_params("parallel"))(zm, zm, zm, av_g, av_b, w_s, b_col)


def _mixer_bwd(zm, o, dm, av_g, av_b, w_s, w_st, b_col, dep):
    T = zm.shape[0]
    tm = _tile(T, 256)
    n_chunk = tm // CHUNK

    def body(u_ref, v_ref, ga_ref, gb_ref, o_ref, dm_ref, g_ref, b_ref, w_ref, wt_ref, bc_ref, dep_ref,
             dz_ref, do_ref, dg_ref, db_ref, dw_ref, dbs_ref, vn_s, mx_s, dmx_s, dvn_s):
        @pl.when(pl.program_id(0) == 0)
        def _():
            dg_ref[...] = jnp.zeros_like(dg_ref)
            db_ref[...] = jnp.zeros_like(db_ref)
            dw_ref[...] = jnp.zeros_like(dw_ref)
            dbs_ref[...] = jnp.zeros_like(dbs_ref)

        dm_v = dm_ref[...].astype(F32)
        gb = gb_ref[...].astype(F32)
        sb = _sigmoid(gb)
        o_v = o_ref[...].astype(F32)
        do_ref[...] = (dm_v * sb).astype(BF16)
        dz_ref[:, 3 * D_MODEL:4 * D_MODEL] = (dm_v * o_v * sb * (1.0 - sb)).astype(BF16)
        u = u_ref[...].astype(F32)
        v = v_ref[...].astype(F32)
        gu, gu_grad = _gelu_and_grad(u)
        gv, gv_grad = _gelu_and_grad(v)
        xh, rs, vn = _layer_norm_fwd(gv, g_ref[...], b_ref[...])
        vn_s[...] = vn.astype(BF16)
        tri = _tri_mask()
        for gi in range(A_GROUPS):
            wm = jnp.where(tri, w_ref[gi], 0.0).astype(BF16)
            cols = slice(gi * CHUNK, (gi + 1) * CHUNK)
            for n in range(n_chunk):
                rows = slice(n * CHUNK, (n + 1) * CHUNK)
                mx_s[rows, cols] = _dot(wm, vn_s[rows, cols]) + bc_ref[gi]
        mixed = mx_s[...]
        sa = _sigmoid(ga_ref[...].astype(F32))
        dya = dm_v * sa
        dz_ref[:, 2 * D_MODEL:3 * D_MODEL] = (dm_v * gu * mixed * sa * (1.0 - sa)).astype(BF16)
        dz_ref[:, 0:D_MODEL] = (dya * mixed * gu_grad).astype(BF16)
        dmx = dya * gu
        dmx_s[...] = dmx.astype(BF16)
        tri_t = _tri_mask(transposed=True)
        for gi in range(A_GROUPS):
            wmt = jnp.where(tri_t, wt_ref[gi], 0.0).astype(BF16)
            cols = slice(gi * CHUNK, (gi + 1) * CHUNK)
            dw_acc = jnp.zeros((CHUNK, CHUNK), F32)
            dmx_sum = jnp.zeros((CHUNK, CHUNK), F32)
            for n in range(n_chunk):
                rows = slice(n * CHUNK, (n + 1) * CHUNK)
                blk = dmx_s[rows, cols]
                dvn_s[rows, cols] = _dot(wmt, blk)
                dw_acc = dw_acc + _dot_nt(blk, vn_s[rows, cols])
                dmx_sum = dmx_sum + dmx[rows, cols]
            dw_ref[gi] += jnp.where(tri, dw_acc, 0.0)
            dbs_ref[gi] += jnp.sum(dmx_sum, axis=-1, keepdims=True)
        dvn = dvn_s[...]
        dg_ref[...] += jnp.sum(dvn * xh, axis=0, keepdims=True)
        db_ref[...] += jnp.sum(dvn, axis=0, keepdims=True)
        dxh = dvn * g_ref[...]
        dgv = rs * (dxh - jnp.mean(dxh, axis=-1, keepdims=True)
                    - xh * jnp.mean(dxh * xh, axis=-1, keepdims=True))
        dz_ref[:, D_MODEL:2 * D_MODEL] = (dgv * gv_grad).astype(BF16)

    col = lambda c: pl.BlockSpec((tm, D_MODEL), lambda i: (i, c))
    row = pl.BlockSpec((tm, D_MODEL), lambda i: (i, 0))
    vec = pl.BlockSpec((1, D_MODEL), lambda i: (0, 0))
    wsp = pl.BlockSpec((A_GROUPS, CHUNK, CHUNK), lambda i: (0, 0, 0))
    bsp = pl.BlockSpec((A_GROUPS, CHUNK, 1), lambda i: (0, 0, 0))
    return pl.pallas_call(
        body, grid=(T // tm,),
        in_specs=[col(0), col(1), col(2), col(3), row, row, vec, vec, wsp, wsp, bsp, ANY],
        out_specs=[pl.BlockSpec((tm, 4 * D_MODEL), lambda i: (i, 0)), row, vec, vec, wsp, bsp],
        out_shape=[jax.ShapeDtypeStruct((T, 4 * D_MODEL), BF16), jax.ShapeDtypeStruct((T, D_MODEL), BF16),
                   jax.ShapeDtypeStruct((1, D_MODEL), F32), jax.ShapeDtypeStruct((1, D_MODEL), F32),
                   jax.ShapeDtypeStruct((A_GROUPS, CHUNK, CHUNK), F32),
                   jax.ShapeDtypeStruct((A_GROUPS, CHUNK, 1), F32)],
        scratch_shapes=[pltpu.VMEM((tm, D_MODEL), BF16), pltpu.VMEM((tm, D_MODEL), F32),
                        pltpu.VMEM((tm, D_MODEL), BF16), pltpu.VMEM((tm, D_MODEL), F32)],
        name="mixer_bwd", compiler_params=_params("arbitrary"))(
            zm, zm, zm, zm, o, dm, av_g, av_b, w_s, w_st, b_col, dep)


def _rope_tables(pos_ref, invf_ref):
    ang = pos_ref[...].astype(F32) * invf_ref[...]
    lane = lax.broadcasted_iota(jnp.int32, ang.shape, 1)
    cos, sin = jnp.cos(ang), jnp.sin(ang)
    c = jnp.where(lane < QK_ROPE, cos, 0.0)
    sa = jnp.where(lane < QK_ROPE // 2, -sin, 0.0)
    sb = jnp.where((lane >= QK_ROPE // 2) & (lane < QK_ROPE), sin, 0.0)
    return c, sa, sb


def _rope(blk, tabs):
    c, sa, sb = tabs
    return blk * c + pltpu.roll(blk, LANES - QK_ROPE // 2, 1) * sa + pltpu.roll(blk, QK_ROPE // 2, 1) * sb


def _rope_t(dout, tabs):
    c, sa, sb = tabs
    return dout * c + pltpu.roll(dout * sa, QK_ROPE // 2, 1) + pltpu.roll(dout * sb, LANES - QK_ROPE // 2, 1)


def _rms_small(x, g):
    r = lax.rsqrt(jnp.mean(x * x, axis=-1, keepdims=True) + EPS)
    xh = x * r
    return xh, r, xh * g


def _mla_prep_fwd(zs, pos, invf, qg, kvg, wuq_p, wukv):
    T = zs.shape[0]
    tm = _tile(T, 512)
    HW = MLA_HEADS * HEAD_PAD

    def body(zs_ref, pos_ref, invf_ref, qg_ref, kvg_ref, wq_ref, wkv_ref, q_ref, k_ref, v_ref):
        tabs = _rope_tables(pos_ref, invf_ref)
        _, _, cqn = _rms_small(zs_ref[:, 0:Q_LORA], qg_ref[...])
        _, _, ckvn = _rms_small(zs_ref[:, Q_LORA:Q_LORA + KV_LORA], kvg_ref[...])
        q = _dot_nt(cqn.astype(BF16), wq_ref[...]) * ATTN_SCALE
        kv = _dot(ckvn.astype(BF16), wkv_ref[...])
        kr = _rope(zs_ref[:, Q_LORA + KV_LORA:ZS_W], tabs).astype(BF16)
        for h in range(MLA_HEADS):
            b0 = h * HEAD_PAD
            q_ref[:, b0:b0 + QK_NOPE] = q[:, b0:b0 + QK_NOPE].astype(BF16)
            q_ref[:, b0 + QK_NOPE:b0 + HEAD_PAD] = _rope(q[:, b0 + QK_NOPE:b0 + HEAD_PAD], tabs).astype(BF16)
            k_ref[:, b0:b0 + QK_NOPE] = kv[:, b0:b0 + QK_NOPE].astype(BF16)
            k_ref[:, b0 + QK_NOPE:b0 + HEAD_PAD] = kr
            v_ref[:, h * V_HEAD:(h + 1) * V_HEAD] = kv[:, b0 + QK_NOPE:b0 + HEAD_PAD].astype(BF16)

    full = lambda a: pl.BlockSpec(a.shape, lambda i: (0,) * a.ndim)
    return pl.pallas_call(
        body, grid=(T // tm,),
        in_specs=[pl.BlockSpec((tm, ZS_W), lambda i: (i, 0)), pl.BlockSpec((tm, 1), lambda i: (i, 0)),
                  full(invf), full(qg), full(kvg), full(wuq_p), full(wukv)],
        out_specs=[pl.BlockSpec((tm, HW), lambda i: (i, 0)), pl.BlockSpec((tm, HW), lambda i: (i, 0)),
                   pl.BlockSpec((tm, D_MODEL), lambda i: (i, 0))],
        out_shape=[jax.ShapeDtypeStruct((T, HW), BF16), jax.ShapeDtypeStruct((T, HW), BF16),
                   jax.ShapeDtypeStruct((T, D_MODEL), BF16)],
        name="mla_prep_fwd", compiler_params=_params("parallel"))(zs, pos, invf, qg, kvg, wuq_p, wukv)


def _mla_prep_bwd(zs, pos, invf, qg, kvg, wuq_p, wukv, dq, dk, dv):
    T = zs.shape[0]
    tm = _tile(T, 512)
    HW = MLA_HEADS * HEAD_PAD

    def body(zs_ref, pos_ref, invf_ref, qg_ref, kvg_ref, wq_ref, wkv_ref, dq_ref, dk_ref, dv_ref,
             dzs_ref, cqn_ref, dqp_ref, ckvn_ref, dkv_ref, dqg_ref, dkvg_ref):
        @pl.when(pl.program_id(0) == 0)
        def _():
            dqg_ref[...] = jnp.zeros_like(dqg_ref)
            dkvg_ref[...] = jnp.zeros_like(dkvg_ref)

        tabs = _rope_tables(pos_ref, invf_ref)
        cqh, rq, cqn = _rms_small(zs_ref[:, 0:Q_LORA], qg_ref[...])
        ckvh, rkv, ckvn = _rms_small(zs_ref[:, Q_LORA:Q_LORA + KV_LORA], kvg_ref[...])
        cqn_ref[...] = cqn.astype(BF16)
        ckvn_ref[...] = ckvn.astype(BF16)
        dkr = jnp.zeros((tm, LANES), F32)
        for h in range(MLA_HEADS):
            b0 = h * HEAD_PAD
            dqp_ref[:, b0:b0 + QK_NOPE] = dq_ref[:, b0:b0 + QK_NOPE]
            dqp_ref[:, b0 + QK_NOPE:b0 + HEAD_PAD] = _rope_t(
                dq_ref[:, b0 + QK_NOPE:b0 + HEAD_PAD].astype(F32), tabs).astype(BF16)
            dkv_ref[:, b0:b0 + QK_NOPE] = dk_ref[:, b0:b0 + QK_NOPE]
            dkv_ref[:, b0 + QK_NOPE:b0 + HEAD_PAD] = dv_ref[:, h * V_HEAD:(h + 1) * V_HEAD]
            dkr = dkr + dk_ref[:, b0 + QK_NOPE:b0 + HEAD_PAD].astype(F32)
        dcqn = _dot(dqp_ref[...], wq_ref[...])
        dckvn = _dot_nt(dkv_ref[...], wkv_ref[...])
        dqg_ref[...] += jnp.sum(dcqn * cqh, axis=0, keepdims=True)
        dkvg_ref[...] += jnp.sum(dckvn * ckvh, axis=0, keepdims=True)
        dxh = dcqn * qg_ref[...]
        dzs_ref[:, 0:Q_LORA] = (rq * (dxh - cqh * jnp.mean(dxh * cqh, axis=-1, keepdims=True))).astype(BF16)
        dxh = dckvn * kvg_ref[...]
        dzs_ref[:, Q_LORA:Q_LORA + KV_LORA] = (
            rkv * (dxh - ckvh * jnp.mean(dxh * ckvh, axis=-1, keepdims=True))).astype(BF16)
        dzs_ref[:, Q_LORA + KV_LORA:ZS_W] = _rope_t(dkr, tabs).astype(BF16)

    full = lambda a: pl.BlockSpec(a.shape, lambda i: (0,) * a.ndim)
    rowb = lambda w: pl.BlockSpec((tm, w), lambda i: (i, 0))
    return pl.pallas_call(
        body, grid=(T // tm,),
        in_specs=[rowb(ZS_W), rowb(1), full(invf), full(qg), full(kvg), full(wuq_p), full(wukv),
                  rowb(HW), rowb(HW), rowb(D_MODEL)],
        out_specs=[rowb(ZS_W), rowb(Q_LORA), rowb(HW), rowb(KV_LORA), rowb(HW), full(qg), full(kvg)],
        out_shape=[jax.ShapeDtypeStruct((T, ZS_W), BF16), jax.ShapeDtypeStruct((T, Q_LORA), BF16),
                   jax.ShapeDtypeStruct((T, HW), BF16), jax.ShapeDtypeStruct((T, KV_LORA), BF16),
                   jax.ShapeDtypeStruct((T, HW), BF16), jax.ShapeDtypeStruct(qg.shape, F32),
                   jax.ShapeDtypeStruct(kvg.shape, F32)],
        name="mla_prep_bwd", compiler_params=_params("arbitrary"))(
            zs, pos, invf, qg, kvg, wuq_p, wukv, dq, dk, dv)


def _causal(tq, kmax, q0):
    r = lax.broadcasted_iota(jnp.int32, (tq, kmax), 0) + q0
    c = lax.broadcasted_iota(jnp.int32, (tq, kmax), 1)
    return c <= r


def _attn_fwd(q, k, v, batch, seq):
    tq = _tile(seq, ATTN_TILE)
    nq = seq // tq

    def body(q_ref, k_ref, v_ref, o_ref, lse_ref):
        diag = _causal(tq, tq, 0)
        for qi in range(nq):
            rows = slice(qi * tq, (qi + 1) * tq)
            qr = q_ref[rows, :]
            s_d = jnp.where(diag, _dot_nt(qr, k_ref[rows, :]), NEG_BIG)
            m = jnp.max(s_d, axis=-1, keepdims=True)
            if qi > 0:
                before = slice(0, qi * tq)
                s_b = _dot_nt(qr, k_ref[before, :])
                m = jnp.maximum(m, jnp.max(s_b, axis=-1, keepdims=True))
                p_b = jnp.exp(s_b - m)
                l = jnp.sum(p_b, axis=-1, keepdims=True)
                acc = _dot(p_b.astype(BF16), v_ref[before, :])
            p_d = jnp.exp(s_d - m)
            l_d = jnp.sum(p_d, axis=-1, keepdims=True)
            acc_d = _dot(p_d.astype(BF16), v_ref[rows, :])
            l, acc = (l + l_d, acc + acc_d) if qi > 0 else (l_d, acc_d)
            o_ref[rows, :] = (acc / l).astype(KEPT)
            lse_ref[rows, :] = jnp.broadcast_to(m + jnp.log(l), (tq, V_HEAD))

    return pl.pallas_call(
        body, grid=(batch, MLA_HEADS),
        in_specs=[pl.BlockSpec((seq, HEAD_PAD), lambda b, h: (b, h)),
                  pl.BlockSpec((seq, HEAD_PAD), lambda b, h: (b, h)),
                  pl.BlockSpec((seq, V_HEAD), lambda b, h: (b, h))],
        out_specs=[pl.BlockSpec((seq, V_HEAD), lambda b, h: (b, h)),
                   pl.BlockSpec((seq, V_HEAD), lambda b, h: (b, h))],
        out_shape=[jax.ShapeDtypeStruct((batch * seq, D_MODEL), KEPT),
                   jax.ShapeDtypeStruct((batch * seq, D_MODEL), F32)],
        name="attn_fwd", compiler_params=_params("parallel", "parallel"))(q, k, v)


def _attn_bwd(q, k, v, o, do, lse, batch, seq, dep):
    tq = _tile(seq, ATTN_TILE)
    nq = seq // tq

    def body(q_ref, k_ref, v_ref, o_ref, do_ref, lse_ref, dep_ref, dq_ref, dk_ref, dv_ref, dk_acc, dv_acc):
        dk_acc[...] = jnp.zeros_like(dk_acc)
        dv_acc[...] = jnp.zeros_like(dv_acc)
        for qi in range(nq):
            rows = slice(qi * tq, (qi + 1) * tq)
            kmax = (qi + 1) * tq
            qr = q_ref[rows, :]
            dor = do_ref[rows, :]
            kk = k_ref[0:kmax, :]
            s = _dot_nt(qr, kk)
            p = jnp.where(_causal(tq, kmax, qi * tq), jnp.exp(s - lse_ref[rows, 0:1]), 0.0)
            dp = _dot_nt(dor, v_ref[0:kmax, :])
            delta = jnp.sum(dor.astype(F32) * o_ref[rows, :].astype(F32), axis=-1, keepdims=True)
            ds = (p * (dp - delta)).astype(BF16)
            dq_ref[rows, :] = (_dot(ds, kk) * ATTN_SCALE).astype(BF16)
            dk_acc[0:kmax, :] += _dot_tn(ds, qr)
            dv_acc[0:kmax, :] += _dot_tn(p.astype(BF16), dor)
        dk_ref[...] = dk_acc[...].astype(BF16)
        dv_ref[...] = dv_acc[...].astype(BF16)

    qspec = pl.BlockSpec((seq, HEAD_PAD), lambda b, h: (b, h))
    vspec = pl.BlockSpec((seq, V_HEAD), lambda b, h: (b, h))
    T = batch * seq
    return pl.pallas_call(
        body, grid=(batch, MLA_HEADS),
        in_specs=[qspec, qspec, vspec, vspec, vspec, vspec, ANY],
        out_specs=[qspec, qspec, vspec],
        out_shape=[jax.ShapeDtypeStruct((T, MLA_HEADS * HEAD_PAD), BF16),
                   jax.ShapeDtypeStruct((T, MLA_HEADS * HEAD_PAD), BF16),
                   jax.ShapeDtypeStruct((T, D_MODEL), BF16)],
        scratch_shapes=[pltpu.VMEM((seq, HEAD_PAD), F32), pltpu.VMEM((seq, V_HEAD), F32)],
        name="attn_bwd", compiler_params=_params("parallel", "parallel"))(q, k, v, o, do, lse, dep)


def _merge_out(x, yag, zm, o, w_out, ffn_g):
    T = x.shape[0]
    tm = _tile(T, 512)

    def body(x_ref, ya_ref, gb_ref, o_ref, w_ref, g_ref, mg_ref, x1_ref, h2_ref):
        mg = (ya_ref[...].astype(F32) + _sigmoid(gb_ref[...].astype(F32)) * o_ref[...].astype(F32)).astype(BF16)
        mg_ref[...] = mg
        x1 = x_ref[...] + _dot(mg, w_ref[...])
        x1_ref[...] = x1
        r = lax.rsqrt(jnp.mean(x1 * x1, axis=-1, keepdims=True) + EPS)
        h2_ref[...] = (x1 * r * g_ref[...]).astype(BF16)

    row = pl.BlockSpec((tm, D_MODEL), lambda i: (i, 0))
    return pl.pallas_call(
        body, grid=(T // tm,),
        in_specs=[row, row, pl.BlockSpec((tm, D_MODEL), lambda i: (i, 3)), row,
                  pl.BlockSpec((D_MODEL, D_MODEL), lambda i: (0, 0)), pl.BlockSpec((1, D_MODEL), lambda i: (0, 0))],
        out_specs=[row, row, row],
        out_shape=[jax.ShapeDtypeStruct((T, D_MODEL), BF16), jax.ShapeDtypeStruct((T, D_MODEL), F32),
                   jax.ShapeDtypeStruct((T, D_MODEL), BF16)],
        name="merge_out", compiler_params=_params("parallel"))(x, yag, zm, o, w_out, ffn_g)


FF_TILE = 256
FF_BLOCKS = D_FF // FF_TILE
FFB_TILE = 256


def _shift_down(x, k):
    row = lax.broadcasted_iota(jnp.int32, x.shape, 0)
    return jnp.where(row >= k, pltpu.roll(x, k, 0), 0.0)


def _shift_up(x, k):
    n = x.shape[0]
    row = lax.broadcasted_iota(jnp.int32, x.shape, 0)
    return jnp.where(row < n - k, pltpu.roll(x, n - k, 0), 0.0)


def _conv(x, w_ref, b_ref):
    return b_ref[...] + w_ref[2:3, :] * x + w_ref[1:2, :] * _shift_down(x, 1) + w_ref[0:1, :] * _shift_down(x, 2)


EDGE = 16


def _conv_wrapped(x, w_ref, b_ref):
    return (b_ref[...] + w_ref[2:3, :] * x + w_ref[1:2, :] * pltpu.roll(x, 1, 0)
            + w_ref[0:1, :] * pltpu.roll(x, 2, 0))


def _up_act(h2, wt_up, cw, cb, batch, seq):
    def body(h_ref, wug_ref, wuv_ref, wg_ref, wv_ref, bg_ref, bv_ref, ug_ref, uv_ref, g_ref, v_ref, a_ref):
        h = h_ref[...]
        ug = _dot_nt(h, wug_ref[...])
        uv = _dot_nt(h, wuv_ref[...])
        ug_ref[...] = ug.astype(KEPT)
        uv_ref[...] = uv.astype(KEPT)

        def write(rows, gate, val):
            g_ref[rows, :] = gate.astype(KEPT)
            v_ref[rows, :] = val.astype(KEPT)
            a_ref[rows, :] = (gate * _sigmoid(gate) * val).astype(BF16)

        write(slice(None), _conv_wrapped(ug, wg_ref, bg_ref), _conv_wrapped(uv, wv_ref, bv_ref))
        write(slice(0, EDGE), _conv(ug[0:EDGE], wg_ref, bg_ref), _conv(uv[0:EDGE], wv_ref, bv_ref))

    blk = pl.BlockSpec((seq, FF_TILE), lambda b, j: (b, j))
    wup = lambda off: pl.BlockSpec((FF_TILE, D_MODEL), lambda b, j: (j + off, 0))
    wsp = lambda off: pl.BlockSpec((3, FF_TILE), lambda b, j: (0, j + off))
    bsp = lambda off: pl.BlockSpec((1, FF_TILE), lambda b, j: (0, j + off))
    T = batch * seq
    kept = jax.ShapeDtypeStruct((T, D_FF), KEPT)
    return pl.pallas_call(
        body, grid=(batch, FF_BLOCKS),
        in_specs=[pl.BlockSpec((seq, D_MODEL), lambda b, j: (b, 0)), wup(0), wup(FF_BLOCKS),
                  wsp(0), wsp(FF_BLOCKS), bsp(0), bsp(FF_BLOCKS)],
        out_specs=[blk] * 5,
        out_shape=[kept, kept, kept, kept, jax.ShapeDtypeStruct((T, D_FF), BF16)],
        name="up_act", compiler_params=_params("parallel", "arbitrary"))(h2, wt_up, wt_up, cw, cw, cb, cb)


def _ffn_act_bwd(upg, upv, gate, val, cw, dx2b, w_down, batch, seq):
    def half(du, x, w_ref, dx_ref, dw_ref):
        j = pl.program_id(1)
        n = du.shape[0]
        up1, up2 = pltpu.roll(du, n - 1, 0), pltpu.roll(du, n - 2, 0)
        dx_ref[...] = (w_ref[2:3, :] * du + w_ref[1:2, :] * up1 + w_ref[0:1, :] * up2).astype(BF16)
        tail = du[n - EDGE:n]
        dx_ref[n - EDGE:n, :] = (w_ref[2:3, :] * tail + w_ref[1:2, :] * _shift_up(tail, 1)
                                 + w_ref[0:1, :] * _shift_up(tail, 2)).astype(BF16)
        row = lax.broadcasted_iota(jnp.int32, (EDGE, du.shape[1]), 0)
        head, x_tail = du[0:EDGE], x[n - EDGE:n]
        wrap1 = jnp.sum(jnp.where(row >= EDGE - 1, pltpu.roll(head, EDGE - 1, 0), 0.0) * x_tail, axis=0, keepdims=True)
        wrap2 = jnp.sum(jnp.where(row >= EDGE - 2, pltpu.roll(head, EDGE - 2, 0), 0.0) * x_tail, axis=0, keepdims=True)
        dw_ref[j, 2:3, :] += jnp.sum(du * x, axis=0, keepdims=True)
        dw_ref[j, 1:2, :] += jnp.sum(up1 * x, axis=0, keepdims=True) - wrap1
        dw_ref[j, 0:1, :] += jnp.sum(up2 * x, axis=0, keepdims=True) - wrap2
        dw_ref[j, 3:4, :] += jnp.sum(du, axis=0, keepdims=True)

    def body(ug_ref, uv_ref, g_ref, v_ref, wg_ref, wv_ref, dx_ref, wd_ref, dg_ref, dv_ref, dwg_ref, dwv_ref):
        @pl.when((pl.program_id(0) == 0) & (pl.program_id(1) == 0))
        def _():
            dwg_ref[...] = jnp.zeros_like(dwg_ref)
            dwv_ref[...] = jnp.zeros_like(dwv_ref)

        gate, val = g_ref[...].astype(F32), v_ref[...].astype(F32)
        sg = _sigmoid(gate)
        dav = _dot_nt(dx_ref[...], wd_ref[...])
        half(dav * val * sg * (1.0 + gate * (1.0 - sg)), ug_ref[...].astype(F32), wg_ref, dg_ref, dwg_ref)
        half(dav * gate * sg, uv_ref[...].astype(F32), wv_ref, dv_ref, dwv_ref)

    nb = D_FF // FFB_TILE
    blk = pl.BlockSpec((seq, FFB_TILE), lambda b, j: (b, j))
    wsp = lambda off: pl.BlockSpec((3, FFB_TILE), lambda b, j: (0, j + off))
    acc = pl.BlockSpec((nb, 4, FFB_TILE), lambda b, j: (0, 0, 0))
    T = batch * seq
    dupg, dupv, dwg, dwv = pl.pallas_call(
        body, grid=(batch, nb),
        in_specs=[blk, blk, blk, blk, wsp(0), wsp(nb),
                  pl.BlockSpec((seq, D_MODEL), lambda b, j: (b, 0)),
                  pl.BlockSpec((FFB_TILE, D_MODEL), lambda b, j: (j, 0))],
        out_specs=[blk, blk, acc, acc],
        out_shape=[jax.ShapeDtypeStruct((T, D_FF), BF16), jax.ShapeDtypeStruct((T, D_FF), BF16),
                   jax.ShapeDtypeStruct((nb, 4, FFB_TILE), F32), jax.ShapeDtypeStruct((nb, 4, FFB_TILE), F32)],
        name="ffn_act_bwd", compiler_params=_params("arbitrary", "arbitrary"))(
            upg, upv, gate, val, cw, cw, dx2b, w_down)
    dwg, dwv = (jnp.transpose(a, (1, 0, 2)).reshape(4, D_FF) for a in (dwg, dwv))
    return dupg, dupv, dwg[:3], dwv[:3], dwg[3:], dwv[3:]


def _down_loss(a, w_down, x1, target, gfin):
    T = x1.shape[0]
    tm = _tile(T, 512)

    def body(a_ref, w_ref, x1_ref, t_ref, g_ref, dx_ref, dxb_ref, loss_ref, dg_ref):
        @pl.when(pl.program_id(0) == 0)
        def _():
            loss_ref[...] = jnp.zeros_like(loss_ref)
            dg_ref[...] = jnp.zeros_like(dg_ref)

        x2 = x1_ref[...] + _dot(a_ref[...], w_ref[...])
        r = lax.rsqrt(jnp.mean(x2 * x2, axis=-1, keepdims=True) + EPS)
        xh = x2 * r
        g = g_ref[...]
        diff = xh * g - t_ref[...]
        loss_ref[...] += 0.5 * jnp.sum(jnp.mean(diff * diff, axis=-1, keepdims=True))
        dy = diff * (1.0 / D_MODEL)
        dg_ref[...] += jnp.sum(dy * xh, axis=0, keepdims=True)
        dxh = dy * g
        dx = r * (dxh - xh * jnp.mean(dxh * xh, axis=-1, keepdims=True))
        dx_ref[...] = dx
        dxb_ref[...] = dx.astype(BF16)

    row = pl.BlockSpec((tm, D_MODEL), lambda i: (i, 0))
    vec = pl.BlockSpec((1, D_MODEL), lambda i: (0, 0))
    return pl.pallas_call(
        body, grid=(T // tm,),
        in_specs=[pl.BlockSpec((tm, D_FF), lambda i: (i, 0)),
                  pl.BlockSpec((D_FF, D_MODEL), lambda i: (0, 0)), row, row, vec],
        out_specs=[row, row, pl.BlockSpec((8, LANES), lambda i: (0, 0)), vec],
        out_shape=[jax.ShapeDtypeStruct((T, D_MODEL), F32), jax.ShapeDtypeStruct((T, D_MODEL), BF16),
                   jax.ShapeDtypeStruct((8, LANES), F32), jax.ShapeDtypeStruct((1, D_MODEL), F32)],
        name="down_loss", compiler_params=_params("arbitrary"))(a, w_down, x1, target, gfin)


def _local_step(x, positions, target, mix_norm, av_g, av_b, w_s, b_s, q_norm, kv_norm, ffn_norm, conv_b,
                final_norm, comm):
    batch, seq, _ = x.shape
    T = batch * seq
    x = x.reshape(T, D_MODEL)
    target = target.reshape(T, D_MODEL)
    pos = positions.reshape(T, 1)
    half = jnp.arange(0, QK_ROPE, 2, dtype=F32) / QK_ROPE
    inv_freq = 1.0 / (ROPE_THETA ** half)
    invf = jnp.concatenate([inv_freq, inv_freq, jnp.zeros((LANES - QK_ROPE,), F32)]).reshape(1, LANES)
    w_st = jnp.swapaxes(w_s, 1, 2)
    b_col = b_s.reshape(A_GROUPS, CHUNK, 1)

    wt_in = comm.in_weights()
    h, zm, zs = _in_proj(x, mix_norm, wt_in)
    yag = _mixer_a_fwd(zm, av_g, av_b, w_s, b_col)
    wuq_p, wukv, w_out = comm.mla_weights(after=yag)
    q, k, v = _mla_prep_fwd(zs, pos, invf, q_norm, kv_norm, wuq_p, wukv)
    o, lse = _attn_fwd(q, k, v, batch, seq)
    merged, x1, h2 = _merge_out(x, yag, zm, o, w_out, ffn_norm)
    wt_up, conv_w, w_down = comm.ffn_weights(after=merged)
    upg, upv, gate, val, act = _up_act(h2, wt_up, conv_w, conv_b, batch, seq)
    dx2, dx2b, loss_acc, d_final = _down_loss(act, w_down, x1, target, final_norm)

    d_wdown = _mm_tn(act, dx2b, "dw_down")
    dupg, dupv, dcwg, dcwv, dcbg, dcbv = _ffn_act_bwd(upg, upv, gate, val, conv_w, dx2b, w_down, batch, seq)
    d_wt_up = _mm_tn(dupv, h2, "dw_up_val", rows=2 * D_FF, row0=D_FF,
                     into=_mm_tn(dupg, h2, "dw_up_gate", rows=2 * D_FF))
    dx1, d_ffn_norm, dmerged = _proj_bwd(
        [dupg, dupv], wt_up, [(0, (0, D_FF), (0, D_FF)), (1, (0, D_FF), (D_FF, 2 * D_FF))],
        x1, ffn_norm, dx2, "up_proj_bwd", w2=w_out)
    d_wout = _mm_tn(merged, dx1, "dw_out")
    dzm, do, d_avg, d_avb, d_ws, d_bs = _mixer_bwd(zm, o, dmerged, av_g, av_b, w_s, w_st, b_col, d_wout)
    small = [d_avg, d_avb, _small_2d(d_ws), d_bs.reshape(A_GROUPS, CHUNK), d_ffn_norm,
             jnp.concatenate([dcbg, dcbv], axis=1), d_final]
    token = comm.send_ffn_grads(d_wdown, d_wt_up, jnp.concatenate([dcwg, dcwv], axis=1), d_wout, small)
    dq, dk, dv = _attn_bwd(q, k, v, o, do, lse, batch, seq, token)
    dzs, cqn, dqp, ckvn, dkv, d_qn, d_kvn = _mla_prep_bwd(zs, pos, invf, q_norm, kv_norm, wuq_p, wukv, dq, dk, dv)
    d_wt_main = _mm_tn(dzm, h, "dw_in_main")
    d_wt_zs = _mm_tn(dzs, h, "dw_in_small")
    token = comm.send_in_grads(d_wt_main, d_wt_zs)
    d_wuq_p = _mm_tn(dqp, cqn, "dw_uq", dep=token)
    d_wukv = _mm_tn(ckvn, dkv, "dw_ukv", dep=token)
    terms = [(0, (i * D_MODEL, (i + 1) * D_MODEL), rows) for i, rows in enumerate(IN_ROWS_MAIN)]
    terms.append((1, (0, ZS_W), IN_ROWS_ZS))
    dx, d_mix_norm = _proj_bwd([dzm, dzs], wt_in, terms, x, mix_norm, dx1, "in_proj_bwd", dep=token, rows=512)
    token = comm.send_late_grads(d_wuq_p, d_wukv, [d_qn, d_kvn, d_mix_norm, loss_acc])
    return dx.reshape(batch, seq, D_MODEL), token


MESH_ID = pl.DeviceIdType.MESH
EFFECT = pltpu.SideEffectType.DATAFLOW_SIDE_EFFECTING


def _mesh_pos():
    return lax.axis_index("x"), lax.axis_index("y"), lax.axis_index("c")


def _peer(pos, d):
    x, y, c = pos
    px = 1 - x if d & 4 else x
    py = 1 - y if d & 2 else y
    pc = 1 - c if d & 1 else c
    return (px, py, pc), 4 * px + 2 * py + pc


def _copy(src_ref, land_ref, send_sems, recv_sems, a, d, pos, exchange, landing_here):
    peer, pid = _peer(pos, d)
    me = 4 * pos[0] + 2 * pos[1] + pos[2]
    if exchange:
        src, dst = src_ref.at[pid], land_ref.at[d]
    else:
        src, dst = src_ref, land_ref.at[pid if landing_here else me]
    return pltpu.make_async_remote_copy(
        src_ref=src, dst_ref=dst, send_sem=send_sems.at[a * (N_DEV - 1) + d - 1],
        recv_sem=recv_sems.at[a * (N_DEV - 1) + d - 1],
        device_id=peer, device_id_type=MESH_ID)


def _start_copies(groups, modes, name, dep=None):
    sizes = [len(g) for g in groups]
    srcs = [s for g in groups for s in g]
    lands = [lax.empty(s.shape if modes[gi] else (N_DEV,) + s.shape, s.dtype)
             for gi, g in enumerate(groups) for s in g]
    n, ng = len(srcs), len(groups)
    n_in = 2 * n + (dep is not None)

    def body(*refs):
        src_refs, land_refs = refs[:n], refs[n:2 * n]
        sems = refs[n_in:n_in + 3 * ng]
        token = refs[-1]
        pos = _mesh_pos()
        k = 0
        for gi, size in enumerate(sizes):
            for a in range(size):
                _own_copy(src_refs[k], land_refs[k], sems[3 * gi + 2], a, pos, modes[gi]).start()
                for d in range(1, N_DEV):
                    _copy(src_refs[k], land_refs[k], sems[3 * gi], sems[3 * gi + 1], a, d, pos, modes[gi],
                          landing_here=False).start()
                k += 1
        token[...] = jnp.zeros_like(token)

    sem_shapes = []
    for size in sizes:
        remote = pltpu.SemaphoreType.DMA((size * (N_DEV - 1),))
        sem_shapes += [remote, remote, pltpu.SemaphoreType.DMA((size,))]
    out = pl.pallas_call(
        body, name=name,
        out_shape=(*sem_shapes, *[pltpu.HBM(a.shape, a.dtype) for a in srcs + lands],
                   jax.ShapeDtypeStruct((8, LANES), F32)),
        in_specs=[HBM] * (2 * n) + [ANY] * (dep is not None),
        out_specs=(*[SEM] * (3 * ng), *[HBM] * (2 * n), pl.BlockSpec(memory_space=pltpu.VMEM)),
        input_output_aliases={i: 3 * ng + i for i in range(2 * n)},
        compiler_params=pltpu.CompilerParams(has_side_effects=EFFECT),
    )(*[pltpu.with_memory_space_constraint(a, pltpu.HBM) for a in srcs + lands], *([dep] if dep is not None else []))
    thru = out[3 * ng:3 * ng + 2 * n]
    handles, k = [], 0
    for gi, size in enumerate(sizes):
        handles.append((out[3 * gi:3 * gi + 3], thru[k:k + size], thru[n + k:n + k + size]))
        k += size
    return handles, out[-1]


def _own_copy(src_ref, land_ref, local_sems, a, pos, exchange):
    me = 4 * pos[0] + 2 * pos[1] + pos[2]
    src, dst = (src_ref.at[me], land_ref.at[0]) if exchange else (src_ref, land_ref.at[me])
    return pltpu.make_async_copy(src, dst, local_sems.at[a])


def _wait_copies(handle, exchange, after, name):
    sems, srcs, lands = handle
    n = len(srcs)

    def body(*refs):
        src_refs, land_refs = refs[:n], refs[n:2 * n]
        send, recv, local = refs[2 * n:2 * n + 3]
        pos = _mesh_pos()
        for a in range(n):
            _own_copy(src_refs[a], land_refs[a], local, a, pos, exchange).wait()
            for d in range(1, N_DEV):
                cp = _copy(src_refs[a], land_refs[a], send, recv, a, d, pos, exchange, landing_here=True)
                cp.wait_send()
                cp.wait_recv()

    out = pl.pallas_call(
        body, name=name,
        out_shape=tuple(pltpu.HBM(a.shape, a.dtype) for a in (*srcs, *lands)),
        in_specs=[HBM] * (2 * n) + [SEM, SEM, SEM, ANY], out_specs=[HBM] * (2 * n),
        input_output_aliases={i: i for i in range(2 * n)},
        compiler_params=pltpu.CompilerParams(has_side_effects=EFFECT),
    )(*srcs, *lands, *sems, after)
    return out[n:]


def _gather_now(a, name):
    def body(x_ref, out_ref, send_sems, recv_sems, local_sem):
        x, y, c = _mesh_pos()
        me, sibling = (x, y, c), (x, y, 1 - c)
        chips = [(1 - x, y), (x, 1 - y), (1 - x, 1 - y)]

        def slot(p):
            return out_ref.at[4 * p[0] + 2 * p[1] + p[2]]

        def copy(k, block, to, src=None):
            return pltpu.make_async_remote_copy(
                src_ref=slot(block) if src is None else src, dst_ref=slot(block), send_sem=send_sems.at[k],
                recv_sem=recv_sems.at[k], device_id=to, device_id_type=MESH_ID)

        mine = pltpu.make_async_copy(x_ref, slot(me), local_sem)
        mine.start()
        first = [copy(0, me, sibling, src=x_ref)]
        first += [copy(1 + j, me, (*chip, c), src=x_ref) for j, chip in enumerate(chips)]
        for cp in first:
            cp.start()
        passed = [copy(4 + j, (*chip, c), sibling) for j, chip in enumerate(chips)]
        for j, chip in enumerate(chips):
            copy(1 + j, (*chip, c), me).wait_recv()
            passed[j].start()
        copy(0, sibling, me).wait_recv()
        for j, chip in enumerate(chips):
            copy(4 + j, (*chip, 1 - c), me).wait_recv()
        for cp in first + passed:
            cp.wait_send()
        mine.wait()

    return pl.pallas_call(
        body, in_specs=[ANY], out_specs=ANY,
        out_shape=jax.ShapeDtypeStruct((N_DEV,) + a.shape, a.dtype),
        scratch_shapes=[pltpu.SemaphoreType.DMA((N_DEV - 1,)), pltpu.SemaphoreType.DMA((N_DEV - 1,)),
                        pltpu.SemaphoreType.DMA],
        name=name, compiler_params=pltpu.CompilerParams(has_side_effects=True))(a)


def _sum_parts(p_ref):
    g = p_ref[0].astype(F32)
    for k in range(1, N_DEV):
        g = g + p_ref[k].astype(F32)
    return g


def _adamw_update(p_ref, w_ref, m_ref, v_ref, g_ref, d_ref, nm_ref, nv_ref):
    c1 = 1.0 - ADAM_B1 ** ADAM_STEP
    c2 = 1.0 - ADAM_B2 ** ADAM_STEP
    g = _sum_parts(p_ref)
    nm = ADAM_B1 * m_ref[...] + (1.0 - ADAM_B1) * g
    nv = ADAM_B2 * v_ref[...] + (1.0 - ADAM_B2) * (g * g)
    g_ref[...] = g
    nm_ref[...] = nm
    nv_ref[...] = nv
    d_ref[...] = -ADAM_LR * ((nm / c1) / (jnp.sqrt(nv / c2) + ADAM_EPS) + ADAM_WD * w_ref[...])


def _adamw_many(parts, ws, ms, vs, sums, name):
    n, ns = len(ws), len(sums)

    def body(*refs):
        ins, outs = refs[:4 * n + ns], refs[4 * n + ns:]
        for i in range(n):
            _adamw_update(ins[i], ins[n + i], ins[2 * n + i], ins[3 * n + i],
                          outs[i], outs[n + i], outs[2 * n + i], outs[3 * n + i])
        for i in range(ns):
            outs[4 * n + i][...] = _sum_parts(ins[4 * n + i])

    full = lambda a: pl.BlockSpec(a.shape, lambda: (0,) * a.ndim)
    args = [*parts, *ws, *ms, *vs, *sums]
    outs = [jax.ShapeDtypeStruct(w.shape, F32) for _ in range(4) for w in ws]
    outs += [jax.ShapeDtypeStruct(s.shape[1:], F32) for s in sums]
    res = pl.pallas_call(
        body, in_specs=[full(a) for a in args], out_specs=[full(o) for o in outs], out_shape=outs,
        name=name, compiler_params=pltpu.CompilerParams(vmem_limit_bytes=VMEM_LIMIT))(*args)
    return res[:n], res[n:2 * n], res[2 * n:3 * n], res[3 * n:4 * n], res[4 * n:]


def _adamw(parts, w, m, v, name):
    R, C = w.shape
    tr, tc = R, C
    if N_DEV * R * C * parts.dtype.itemsize > SMALL_BLOCK_BYTES:
        tr = next((t for t in range(min(R, 256) // 16 * 16, 15, -16) if R % t == 0), R)
        if tr == R:
            tc = _tile(C, 256)

    def body(p_ref, w_ref, m_ref, v_ref, g_ref, d_ref, nm_ref, nv_ref):
        _adamw_update(p_ref, w_ref, m_ref, v_ref, g_ref, d_ref, nm_ref, nv_ref)

    blk = pl.BlockSpec((tr, tc), lambda i, j: (i, j))
    shp = jax.ShapeDtypeStruct((R, C), F32)
    return pl.pallas_call(
        body, grid=(R // tr, C // tc),
        in_specs=[pl.BlockSpec((N_DEV, tr, tc), lambda i, j: (0, i, j)), blk, blk, blk],
        out_specs=[blk, blk, blk, blk], out_shape=[shp, shp, shp, shp],
        name=name, compiler_params=_params("parallel", "parallel"))(parts, w, m, v)


SPLIT_V = 2 * D_MODEL
SPLIT_KR = SPLIT_V + Q_LORA + KV_LORA + QK_ROPE
IN_DIM = SPLIT_KR + 2 * D_MODEL
IN_ROWS_MAIN = ((0, D_MODEL), (D_MODEL, SPLIT_V), (SPLIT_KR, SPLIT_KR + D_MODEL), (SPLIT_KR + D_MODEL, IN_DIM))
IN_ROWS_ZS = (SPLIT_V, SPLIT_V + ZS_W)

SMALL_EARLY = ("a_v_norm_g", "a_v_norm_b", "a_spatial_w", "a_spatial_b", "ffn_norm", "conv_b", "final_norm")
SMALL_LATE = ("q_a_norm", "kv_a_norm", "mix_norm")


def _small_2d(a):
    return a.reshape(-1, a.shape[-1])


def _cols_from_shards(g):
    return jnp.transpose(g, (1, 0, 2)).reshape(g.shape[1], N_DEV * g.shape[2])


def _shards_from_cols(a):
    R, W = a.shape
    return jnp.transpose(a.reshape(R, N_DEV, W // N_DEV), (1, 0, 2))


class _Comm:
    GATHER_GROUPS = (("w_uq", "w_ukv", "w_out"), ("w_up", "conv_w", "w_down"))
    FFN_GRADS = ("w_down", "w_up", "conv_w", "w_out")
    LATE_GRADS = ("w_uq", "w_ukv")
    TRANSPOSED = ("w_in", "w_up", "w_uq")

    def __init__(self, shards):
        local = {n: a.astype(F32 if n == "conv_w" else BF16) for n, a in shards.items()}
        self.g_in = _gather_now(local["w_in"], "gather_w_in")
        groups = [[local[n] for n in g] for g in self.GATHER_GROUPS]
        (self.h_mla, self.h_ffn), _ = _start_copies(groups, [False] * 2, "gather_start", dep=self.g_in)

    def in_weights(self):
        return self.g_in.reshape(IN_DIM, D_MODEL)

    def mla_weights(self, after):
        g_uq, g_ukv, g_out = _wait_copies(self.h_mla, False, after, "gather_wait_mla")
        wuq_p = jnp.pad(g_uq, ((0, 0), (0, HEAD_PAD - QK_HEAD), (0, 0))).reshape(MLA_HEADS * HEAD_PAD, Q_LORA)
        return wuq_p, _cols_from_shards(g_ukv), g_out.reshape(D_MODEL, D_MODEL)

    def ffn_weights(self, after):
        g_up, g_cw, g_down = _wait_copies(self.h_ffn, False, after, "gather_wait_ffn")
        return g_up.reshape(2 * D_FF, D_MODEL), _cols_from_shards(g_cw), g_down.reshape(D_FF, D_MODEL)

    def send_ffn_grads(self, d_wdown, d_wt_up, d_convw, d_wout, small):
        group = [d_wdown.reshape(N_DEV, D_FF // N_DEV, D_MODEL), d_wt_up.reshape(N_DEV, 2 * D_FF // N_DEV, D_MODEL),
                 _shards_from_cols(d_convw), d_wout.reshape(N_DEV, D_MODEL // N_DEV, D_MODEL)]
        (self.h_ffn_grads, self.h_small_early), token = _start_copies(
            [group, small], [True, False], "ffn_grads_start")
        return token

    def send_in_grads(self, d_wt_main, d_wt_zs):
        d_in = jnp.concatenate([d_wt_main[:SPLIT_V], d_wt_zs[:SPLIT_KR - SPLIT_V], d_wt_main[SPLIT_V:]], axis=0)
        blocks = d_in.reshape(N_DEV, IN_DIM // N_DEV, D_MODEL)
        (self.h_in_grads,), token = _start_copies([[blocks]], [True], "in_grads_start")
        return token

    def send_late_grads(self, d_wuq_p, d_wukv, small):
        d_uq = d_wuq_p.reshape(MLA_HEADS, HEAD_PAD, Q_LORA)[:, :QK_HEAD, :]
        (self.h_late_grads, self.h_late_small), token = _start_copies(
            [[d_uq, _shards_from_cols(d_wukv)], small], [True, False], "late_grads_start")
        return token


def kernel(x, positions, mix_norm, w_in, a_v_norm_g, a_v_norm_b, a_spatial_w, a_spatial_b, q_a_norm, w_uq, kv_a_norm, w_ukv, w_out, ffn_norm, w_up, conv_w, conv_b, w_down, final_norm, loss_target, m_mix_norm, m_w_in, m_a_v_norm_g, m_a_v_norm_b, m_a_spatial_w, m_a_spatial_b, m_q_a_norm, m_w_uq, m_kv_a_norm, m_w_ukv, m_w_out, m_ffn_norm, m_w_up, m_conv_w, m_conv_b, m_w_down, m_final_norm, v_mix_norm, v_w_in, v_a_v_norm_g, v_a_v_norm_b, v_a_spatial_w, v_a_spatial_b, v_q_a_norm, v_w_uq, v_kv_a_norm, v_w_ukv, v_w_out, v_ffn_norm, v_w_up, v_conv_w, v_conv_b, v_w_down, v_final_norm):
    names = ("mix_norm", "w_in", "a_v_norm_g", "a_v_norm_b", "a_spatial_w", "a_spatial_b", "q_a_norm", "w_uq",
             "kv_a_norm", "w_ukv", "w_out", "ffn_norm", "w_up", "conv_w", "conv_b", "w_down", "final_norm")
    w = dict(zip(names, (mix_norm, w_in, a_v_norm_g, a_v_norm_b, a_spatial_w, a_spatial_b, q_a_norm, w_uq,
                         kv_a_norm, w_ukv, w_out, ffn_norm, w_up, conv_w, conv_b, w_down, final_norm)))
    m = dict(zip(names, (m_mix_norm, m_w_in, m_a_v_norm_g, m_a_v_norm_b, m_a_spatial_w, m_a_spatial_b,
                         m_q_a_norm, m_w_uq, m_kv_a_norm, m_w_ukv, m_w_out, m_ffn_norm, m_w_up, m_conv_w,
                         m_conv_b, m_w_down, m_final_norm)))
    v = dict(zip(names, (v_mix_norm, v_w_in, v_a_v_norm_g, v_a_v_norm_b, v_a_spatial_w, v_a_spatial_b,
                         v_q_a_norm, v_w_uq, v_kv_a_norm, v_w_ukv, v_w_out, v_ffn_norm, v_w_up, v_conv_w,
                         v_conv_b, v_w_down, v_final_norm)))
    shapes = {n: w[n].shape for n in names}
    def view(tree, n):
        a = tree[n].reshape(tree[n].shape[-2:])
        return a.T if n in _Comm.TRANSPOSED else a

    comm = _Comm({n: view(w, n) for n in ("w_in",) + _Comm.GATHER_GROUPS[0] + _Comm.GATHER_GROUPS[1]})

    grad_x, token = _local_step(
        x, positions, loss_target, w["mix_norm"], w["a_v_norm_g"], w["a_v_norm_b"], w["a_spatial_w"][0],
        w["a_spatial_b"][0], w["q_a_norm"], w["kv_a_norm"], w["ffn_norm"], w["conv_b"],
        w["final_norm"].reshape(1, D_MODEL), comm)

    out_g, out_d, out_m, out_v = {}, {}, {}, {}

    def update(n, parts):
        res = _adamw(parts, view(w, n), view(m, n), view(v, n), "adamw_" + n)
        out_g[n], out_d[n], out_m[n], out_v[n] = (
            (t.T if n in _Comm.TRANSPOSED else t).reshape(shapes[n]) for t in res)
        return res[1]

    def update_small(names, parts, sums, name):
        res = _adamw_many(parts, *[[_small_2d(t[n]) for n in names] for t in (w, m, v)], sums, name)
        for i, n in enumerate(names):
            out_g[n], out_d[n], out_m[n], out_v[n] = (r[i].reshape(shapes[n]) for r in res[:4])
        return res

    for n, parts in zip(_Comm.FFN_GRADS, _wait_copies(comm.h_ffn_grads, True, token, "ffn_grads_wait")):
        last = update(n, parts)
    early = _wait_copies(comm.h_small_early, False, last, "small_grads_wait")
    last = update_small(SMALL_EARLY, early, [], "adamw_small")[1][0]
    last = update("w_in", _wait_copies(comm.h_in_grads, True, last, "in_grads_wait")[0])
    for n, parts in zip(_Comm.LATE_GRADS, _wait_copies(comm.h_late_grads, True, last, "late_grads_wait")):
        last = update(n, parts)
    late = _wait_copies(comm.h_late_small, False, last, "late_small_wait")
    res = update_small(SMALL_LATE, late[:-1], late[-1:], "adamw_late")
    loss = res[4][0][0, 0]

    return (loss, grad_x, *[out_g[n] for n in names], *[out_d[n] for n in names],
            *[out_m[n] for n in names], *[out_v[n] for n in names])
```

```python
import math

import jax
import jax.numpy as jnp
from jax import lax
from jax.experimental import pallas as pl
from jax.experimental.pallas import tpu as pltpu

F32 = jnp.float32
BF16 = jnp.bfloat16
KEPT = jnp.bfloat16

N_DEV = 8
D_MODEL = 1024
EPS = 1e-6
A_GROUPS = 8
CHUNK = 128
MLA_HEADS = 8
QK_NOPE = 128
QK_ROPE = 64
QK_HEAD = QK_NOPE + QK_ROPE
HEAD_PAD = 256
V_HEAD = 128
Q_LORA = 256
KV_LORA = 128
ROPE_THETA = 10000.0
D_FF = 2816
ZS_W = 512
ATTN_SCALE = QK_HEAD ** -0.5
ATTN_TILE = 512
NEG_BIG = -1e30

ADAM_LR = 0.001
ADAM_B1 = 0.9
ADAM_B2 = 0.999
ADAM_EPS = 1e-08
ADAM_WD = 0.01
ADAM_STEP = 10

VMEM_LIMIT = 56 * 1024 * 1024
SMALL_BLOCK_BYTES = 5 * 1024 * 1024
LANES = 128

GELU_K = math.sqrt(2.0 / math.pi)
GELU_C = 0.044715

ANY = pl.BlockSpec(memory_space=pl.ANY)
HBM = pl.BlockSpec(memory_space=pltpu.HBM)
SEM = pl.BlockSpec(memory_space=pltpu.SEMAPHORE)


def _tile(n, pref):
    for t in (pref, 512, 256, 128, 64, 32, 16, 8):
        if t <= pref and n % t == 0:
            return t
    return n


def _wide_tile(n, cap=1408):
    return next((t for t in range(min(n, cap) // LANES * LANES, 0, -LANES) if n % t == 0), n)


def _params(*sem):
    return pltpu.CompilerParams(dimension_semantics=sem, vmem_limit_bytes=VMEM_LIMIT)


def _dot(a, b):
    return jnp.dot(a, b, preferred_element_type=F32)


def _dot_nt(a, b):
    return lax.dot_general(a, b, (((1,), (1,)), ((), ())), preferred_element_type=F32)


def _dot_tn(a, b):
    return lax.dot_general(a, b, (((0,), (0,)), ((), ())), preferred_element_type=F32)


def _sigmoid(x):
    return 1.0 / (1.0 + jnp.exp(-x))


def _gelu(x):
    t = jnp.tanh(GELU_K * (x + GELU_C * x * x * x))
    return 0.5 * x * (1.0 + t)


def _gelu_and_grad(x):
    x2 = x * x
    t = jnp.tanh(GELU_K * (x + GELU_C * x * x2))
    half = 0.5 * (1.0 + t)
    return x * half, half + 0.5 * x * (1.0 - t * t) * GELU_K * (1.0 + 3.0 * GELU_C * x2)


def _in_proj(x, g, wt):
    T, Dm = x.shape
    tm = _tile(T, 512)

    def body(x_ref, g_ref, wt_ref, h_ref, zm_ref, zs_ref):
        xf = x_ref[...]
        r = lax.rsqrt(jnp.mean(xf * xf, axis=-1, keepdims=True) + EPS)
        h = (xf * r * g_ref[...]).astype(BF16)
        h_ref[...] = h
        for i, (r0, r1) in enumerate(IN_ROWS_MAIN):
            zm_ref[:, i * D_MODEL:(i + 1) * D_MODEL] = _dot_nt(h, wt_ref[r0:r1, :]).astype(KEPT)
        zs_ref[...] = _dot_nt(h, wt_ref[IN_ROWS_ZS[0]:IN_ROWS_ZS[1], :])

    row = lambda n: pl.BlockSpec((tm, n), lambda i: (i, 0))
    return pl.pallas_call(
        body, grid=(T // tm,),
        in_specs=[row(Dm), pl.BlockSpec((1, Dm), lambda i: (0, 0)), pl.BlockSpec(wt.shape, lambda i: (0, 0))],
        out_specs=[row(Dm), row(4 * D_MODEL), row(ZS_W)],
        out_shape=[jax.ShapeDtypeStruct((T, Dm), BF16), jax.ShapeDtypeStruct((T, 4 * D_MODEL), KEPT),
                   jax.ShapeDtypeStruct((T, ZS_W), F32)],
        name="in_proj", compiler_params=_params("parallel"))(x, g, wt)


def _proj_bwd(acts, wt, terms, x, g, dres, name, w2=None, dep=None, rows=256):
    T, Dm = x.shape
    tm = _tile(T, rows)
    n_a = len(acts)

    def body(*refs):
        ins, outs = refs[:n_a + 4 + (w2 is not None) + (dep is not None)], refs[-2 - (w2 is not None):]
        wt_ref, x_ref, g_ref, dres_ref = ins[n_a:n_a + 4]
        dx_ref, dg_ref = outs[0], outs[1]

        @pl.when(pl.program_id(0) == 0)
        def _():
            dg_ref[...] = jnp.zeros_like(dg_ref)

        dy = None
        for i, (c0, c1), (r0, r1) in terms:
            t = _dot(ins[i][:, c0:c1], wt_ref[r0:r1, :])
            dy = t if dy is None else dy + t
        xf = x_ref[...]
        r = lax.rsqrt(jnp.mean(xf * xf, axis=-1, keepdims=True) + EPS)
        xh = xf * r
        dg_ref[...] += jnp.sum(dy * xh, axis=0, keepdims=True)
        dxh = dy * g_ref[...]
        dx = dres_ref[...] + r * (dxh - xh * jnp.mean(dxh * xh, axis=-1, keepdims=True))
        dx_ref[...] = dx
        if w2 is not None:
            outs[2][...] = _dot_nt(dx.astype(BF16), ins[n_a + 4][...]).astype(KEPT)

    row = pl.BlockSpec((tm, Dm), lambda i: (i, 0))
    vec = pl.BlockSpec((1, Dm), lambda i: (0, 0))
    in_specs = [pl.BlockSpec((tm, a.shape[1]), lambda i: (i, 0)) for a in acts]
    in_specs += [pl.BlockSpec(wt.shape, lambda i: (0, 0)), row, vec, row]
    args = [*acts, wt, x, g, dres]
    out_specs = [row, vec]
    out_shape = [jax.ShapeDtypeStruct((T, Dm), F32), jax.ShapeDtypeStruct((1, Dm), F32)]
    if w2 is not None:
        in_specs.append(pl.BlockSpec(w2.shape, lambda i: (0, 0)))
        args.append(w2)
        out_specs.append(pl.BlockSpec((tm, w2.shape[0]), lambda i: (i, 0)))
        out_shape.append(jax.ShapeDtypeStruct((T, w2.shape[0]), KEPT))
    if dep is not None:
        in_specs.append(ANY)
        args.append(dep)
    return pl.pallas_call(
        body, grid=(T // tm,), in_specs=in_specs, out_specs=out_specs, out_shape=out_shape,
        name=name, compiler_params=_params("arbitrary"))(*args)


def _mm_tn(a, b, name, dep=None, rows=None, row0=0, into=None):
    T, M = a.shape
    N = b.shape[1]
    tm, tn, tt = _wide_tile(M), _wide_tile(N), _tile(T, 2048)
    n_t = T // tt
    off = row0 // tm
    extra = ([dep] if dep is not None else []) + ([into] if into is not None else [])

    def body(a_ref, b_ref, *refs):
        o_ref, acc_ref = refs[-2:]
        t = pl.program_id(2)

        @pl.when(t == 0)
        def _():
            acc_ref[...] = jnp.zeros_like(acc_ref)

        acc_ref[...] += _dot_tn(a_ref[...].astype(BF16), b_ref[...].astype(BF16))

        @pl.when(t == n_t - 1)
        def _():
            o_ref[...] = acc_ref[...].astype(BF16)

    return pl.pallas_call(
        body, grid=(M // tm, N // tn, n_t),
        in_specs=[pl.BlockSpec((tt, tm), lambda i, j, t: (t, i)),
                  pl.BlockSpec((tt, tn), lambda i, j, t: (t, j))] + [ANY] * len(extra),
        out_specs=pl.BlockSpec((tm, tn), lambda i, j, t: (i + off, j)),
        out_shape=jax.ShapeDtypeStruct((rows or M, N), BF16),
        scratch_shapes=[pltpu.VMEM((tm, tn), F32)],
        input_output_aliases={} if into is None else {1 + len(extra): 0},
        name=name, compiler_params=_params("parallel", "parallel", "arbitrary"))(a, b, *extra)


def _layer_norm_fwd(gv, g, b):
    mu = jnp.mean(gv, axis=-1, keepdims=True)
    xc = gv - mu
    rs = lax.rsqrt(jnp.mean(xc * xc, axis=-1, keepdims=True) + EPS)
    xh = xc * rs
    return xh, rs, xh * g + b


def _tri_mask(transposed=False):
    r = lax.broadcasted_iota(jnp.int32, (CHUNK, CHUNK), 0)
    c = lax.broadcasted_iota(jnp.int32, (CHUNK, CHUNK), 1)
    return r <= c if transposed else c <= r


def _mixer_a_fwd(zm, av_g, av_b, w_s, b_col):
    T = zm.shape[0]
    tm = _tile(T, 512)
    n_chunk = tm // CHUNK

    def body(u_ref, v_ref, ga_ref, g_ref, b_ref, w_ref, bc_ref, y_ref, vn_s, mx_s):
        gu = _gelu(u_ref[...].astype(F32))
        _, _, vn = _layer_norm_fwd(_gelu(v_ref[...].astype(F32)), g_ref[...], b_ref[...])
        vn_s[...] = vn.astype(BF16)
        tri = _tri_mask()
        for gi in range(A_GROUPS):
            wm = jnp.where(tri, w_ref[gi], 0.0).astype(BF16)
            cols = slice(gi * CHUNK, (gi + 1) * CHUNK)
            for n in range(n_chunk):
                rows = slice(n * CHUNK, (n + 1) * CHUNK)
                mx_s[rows, cols] = _dot(wm, vn_s[rows, cols]) + bc_ref[gi]
        y_ref[...] = (_sigmoid(ga_ref[...].astype(F32)) * gu * mx_s[...]).astype(KEPT)

    col = lambda c: pl.BlockSpec((tm, D_MODEL), lambda i: (i, c))
    vec = pl.BlockSpec((1, D_MODEL), lambda i: (0, 0))
    return pl.pallas_call(
        body, grid=(T // tm,),
        in_specs=[col(0), col(1), col(2), vec, vec,
                  pl.BlockSpec((A_GROUPS, CHUNK, CHUNK), lambda i: (0, 0, 0)),
                  pl.BlockSpec((A_GROUPS, CHUNK, 1), lambda i: (0, 0, 0))],
        out_specs=pl.BlockSpec((tm, D_MODEL), lambda i: (i, 0)),
        out_shape=jax.ShapeDtypeStruct((T, D_MODEL), KEPT),
        scratch_shapes=[pltpu.VMEM((tm, D_MODEL), BF16), pltpu.VMEM((tm, D_MODEL), F32)],
        name="mixer_a_fwd", compiler_params=_params("parallel"))(zm, zm, zm, av_g, av_b, w_s, b_col)


def _mixer_bwd(zm, o, dm, av_g, av_b, w_s, w_st, b_col, dep):
    T = zm.shape[0]
    tm = _tile(T, 256)
    n_chunk = tm // CHUNK

    def body(u_ref, v_ref, ga_ref, gb_ref, o_ref, dm_ref, g_ref, b_ref, w_ref, wt_ref, bc_ref, dep_ref,
             dz_ref, do_ref, dg_ref, db_ref, dw_ref, dbs_ref, vn_s, mx_s, dmx_s, dvn_s):
        @pl.when(pl.program_id(0) == 0)
        def _():
            dg_ref[...] = jnp.zeros_like(dg_ref)
            db_ref[...] = jnp.zeros_like(db_ref)
            dw_ref[...] = jnp.zeros_like(dw_ref)
            dbs_ref[...] = jnp.zeros_like(dbs_ref)

        dm_v = dm_ref[...].astype(F32)
        gb = gb_ref[...].astype(F32)
        sb = _sigmoid(gb)
        o_v = o_ref[...].astype(F32)
        do_ref[...] = (dm_v * sb).astype(BF16)
        dz_ref[:, 3 * D_MODEL:4 * D_MODEL] = (dm_v * o_v * sb * (1.0 - sb)).astype(BF16)
        u = u_ref[...].astype(F32)
        v = v_ref[...].astype(F32)
        gu, gu_grad = _gelu_and_grad(u)
        gv, gv_grad = _gelu_and_grad(v)
        xh, rs, vn = _layer_norm_fwd(gv, g_ref[...], b_ref[...])
        vn_s[...] = vn.astype(BF16)
        tri = _tri_mask()
        for gi in range(A_GROUPS):
            wm = jnp.where(tri, w_ref[gi], 0.0).astype(BF16)
            cols = slice(gi * CHUNK, (gi + 1) * CHUNK)
            for n in range(n_chunk):
                rows = slice(n * CHUNK, (n + 1) * CHUNK)
                mx_s[rows, cols] = _dot(wm, vn_s[rows, cols]) + bc_ref[gi]
        mixed = mx_s[...]
        sa = _sigmoid(ga_ref[...].astype(F32))
        dya = dm_v * sa
        dz_ref[:, 2 * D_MODEL:3 * D_MODEL] = (dm_v * gu * mixed * sa * (1.0 - sa)).astype(BF16)
        dz_ref[:, 0:D_MODEL] = (dya * mixed * gu_grad).astype(BF16)
        dmx = dya * gu
        dmx_s[...] = dmx.astype(BF16)
        tri_t = _tri_mask(transposed=True)
        for gi in range(A_GROUPS):
            wmt = jnp.where(tri_t, wt_ref[gi], 0.0).astype(BF16)
            cols = slice(gi * CHUNK, (gi + 1) * CHUNK)
            dw_acc = jnp.zeros((CHUNK, CHUNK), F32)
            dmx_sum = jnp.zeros((CHUNK, CHUNK), F32)
            for n in range(n_chunk):
                rows = slice(n * CHUNK, (n + 1) * CHUNK)
                blk = dmx_s[rows, cols]
                dvn_s[rows, cols] = _dot(wmt, blk)
                dw_acc = dw_acc + _dot_nt(blk, vn_s[rows, cols])
                dmx_sum = dmx_sum + dmx[rows, cols]
            dw_ref[gi] += jnp.where(tri, dw_acc, 0.0)
            dbs_ref[gi] += jnp.sum(dmx_sum, axis=-1, keepdims=True)
        dvn = dvn_s[...]
        dg_ref[...] += jnp.sum(dvn * xh, axis=0, keepdims=True)
        db_ref[...] += jnp.sum(dvn, axis=0, keepdims=True)
        dxh = dvn * g_ref[...]
        dgv = rs * (dxh - jnp.mean(dxh, axis=-1, keepdims=True)
                    - xh * jnp.mean(dxh * xh, axis=-1, keepdims=True))
        dz_ref[:, D_MODEL:2 * D_MODEL] = (dgv * gv_grad).astype(BF16)

    col = lambda c: pl.BlockSpec((tm, D_MODEL), lambda i: (i, c))
    row = pl.BlockSpec((tm, D_MODEL), lambda i: (i, 0))
    vec = pl.BlockSpec((1, D_MODEL), lambda i: (0, 0))
    wsp = pl.BlockSpec((A_GROUPS, CHUNK, CHUNK), lambda i: (0, 0, 0))
    bsp = pl.BlockSpec((A_GROUPS, CHUNK, 1), lambda i: (0, 0, 0))
    return pl.pallas_call(
        body, grid=(T // tm,),
        in_specs=[col(0), col(1), col(2), col(3), row, row, vec, vec, wsp, wsp, bsp, ANY],
        out_specs=[pl.BlockSpec((tm, 4 * D_MODEL), lambda i: (i, 0)), row, vec, vec, wsp, bsp],
        out_shape=[jax.ShapeDtypeStruct((T, 4 * D_MODEL), BF16), jax.ShapeDtypeStruct((T, D_MODEL), BF16),
                   jax.ShapeDtypeStruct((1, D_MODEL), F32), jax.ShapeDtypeStruct((1, D_MODEL), F32),
                   jax.ShapeDtypeStruct((A_GROUPS, CHUNK, CHUNK), F32),
                   jax.ShapeDtypeStruct((A_GROUPS, CHUNK, 1), F32)],
        scratch_shapes=[pltpu.VMEM((tm, D_MODEL), BF16), pltpu.VMEM((tm, D_MODEL), F32),
                        pltpu.VMEM((tm, D_MODEL), BF16), pltpu.VMEM((tm, D_MODEL), F32)],
        name="mixer_bwd", compiler_params=_params("arbitrary"))(
            zm, zm, zm, zm, o, dm, av_g, av_b, w_s, w_st, b_col, dep)


def _rope_tables(pos_ref, invf_ref):
    ang = pos_ref[...].astype(F32) * invf_ref[...]
    lane = lax.broadcasted_iota(jnp.int32, ang.shape, 1)
    cos, sin = jnp.cos(ang), jnp.sin(ang)
    c = jnp.where(lane < QK_ROPE, cos, 0.0)
    sa = jnp.where(lane < QK_ROPE // 2, -sin, 0.0)
    sb = jnp.where((lane >= QK_ROPE // 2) & (lane < QK_ROPE), sin, 0.0)
    return c, sa, sb


def _rope(blk, tabs):
    c, sa, sb = tabs
    return blk * c + pltpu.roll(blk, LANES - QK_ROPE // 2, 1) * sa + pltpu.roll(blk, QK_ROPE // 2, 1) * sb


def _rope_t(dout, tabs):
    c, sa, sb = tabs
    return dout * c + pltpu.roll(dout * sa, QK_ROPE // 2, 1) + pltpu.roll(dout * sb, LANES - QK_ROPE // 2, 1)


def _rms_small(x, g):
    r = lax.rsqrt(jnp.mean(x * x, axis=-1, keepdims=True) + EPS)
    xh = x * r
    return xh, r, xh * g


def _mla_prep_fwd(zs, pos, invf, qg, kvg, wuq_p, wukv):
    T = zs.shape[0]
    tm = _tile(T, 512)
    HW = MLA_HEADS * HEAD_PAD

    def body(zs_ref, pos_ref, invf_ref, qg_ref, kvg_ref, wq_ref, wkv_ref, q_ref, k_ref, v_ref):
        tabs = _rope_tables(pos_ref, invf_ref)
        _, _, cqn = _rms_small(zs_ref[:, 0:Q_LORA], qg_ref[...])
        _, _, ckvn = _rms_small(zs_ref[:, Q_LORA:Q_LORA + KV_LORA], kvg_ref[...])
        q = _dot_nt(cqn.astype(BF16), wq_ref[...]) * ATTN_SCALE
        kv = _dot(ckvn.astype(BF16), wkv_ref[...])
        kr = _rope(zs_ref[:, Q_LORA + KV_LORA:ZS_W], tabs).astype(BF16)
        for h in range(MLA_HEADS):
            b0 = h * HEAD_PAD
            q_ref[:, b0:b0 + QK_NOPE] = q[:, b0:b0 + QK_NOPE].astype(BF16)
            q_ref[:, b0 + QK_NOPE:b0 + HEAD_PAD] = _rope(q[:, b0 + QK_NOPE:b0 + HEAD_PAD], tabs).astype(BF16)
            k_ref[:, b0:b0 + QK_NOPE] = kv[:, b0:b0 + QK_NOPE].astype(BF16)
            k_ref[:, b0 + QK_NOPE:b0 + HEAD_PAD] = kr
            v_ref[:, h * V_HEAD:(h + 1) * V_HEAD] = kv[:, b0 + QK_NOPE:b0 + HEAD_PAD].astype(BF16)

    full = lambda a: pl.BlockSpec(a.shape, lambda i: (0,) * a.ndim)
    return pl.pallas_call(
        body, grid=(T // tm,),
        in_specs=[pl.BlockSpec((tm, ZS_W), lambda i: (i, 0)), pl.BlockSpec((tm, 1), lambda i: (i, 0)),
                  full(invf), full(qg), full(kvg), full(wuq_p), full(wukv)],
        out_specs=[pl.BlockSpec((tm, HW), lambda i: (i, 0)), pl.BlockSpec((tm, HW), lambda i: (i, 0)),
                   pl.BlockSpec((tm, D_MODEL), lambda i: (i, 0))],
        out_shape=[jax.ShapeDtypeStruct((T, HW), BF16), jax.ShapeDtypeStruct((T, HW), BF16),
                   jax.ShapeDtypeStruct((T, D_MODEL), BF16)],
        name="mla_prep_fwd", compiler_params=_params("parallel"))(zs, pos, invf, qg, kvg, wuq_p, wukv)


def _mla_prep_bwd(zs, pos, invf, qg, kvg, wuq_p, wukv, dq, dk, dv):
    T = zs.shape[0]
    tm = _tile(T, 512)
    HW = MLA_HEADS * HEAD_PAD

    def body(zs_ref, pos_ref, invf_ref, qg_ref, kvg_ref, wq_ref, wkv_ref, dq_ref, dk_ref, dv_ref,
             dzs_ref, cqn_ref, dqp_ref, ckvn_ref, dkv_ref, dqg_ref, dkvg_ref):
        @pl.when(pl.program_id(0) == 0)
        def _():
            dqg_ref[...] = jnp.zeros_like(dqg_ref)
            dkvg_ref[...] = jnp.zeros_like(dkvg_ref)

        tabs = _rope_tables(pos_ref, invf_ref)
        cqh, rq, cqn = _rms_small(zs_ref[:, 0:Q_LORA], qg_ref[...])
        ckvh, rkv, ckvn = _rms_small(zs_ref[:, Q_LORA:Q_LORA + KV_LORA], kvg_ref[...])
        cqn_ref[...] = cqn.astype(BF16)
        ckvn_ref[...] = ckvn.astype(BF16)
        dkr = jnp.zeros((tm, LANES), F32)
        for h in range(MLA_HEADS):
            b0 = h * HEAD_PAD
            dqp_ref[:, b0:b0 + QK_NOPE] = dq_ref[:, b0:b0 + QK_NOPE]
            dqp_ref[:, b0 + QK_NOPE:b0 + HEAD_PAD] = _rope_t(
                dq_ref[:, b0 + QK_NOPE:b0 + HEAD_PAD].astype(F32), tabs).astype(BF16)
            dkv_ref[:, b0:b0 + QK_NOPE] = dk_ref[:, b0:b0 + QK_NOPE]
            dkv_ref[:, b0 + QK_NOPE:b0 + HEAD_PAD] = dv_ref[:, h * V_HEAD:(h + 1) * V_HEAD]
            dkr = dkr + dk_ref[:, b0 + QK_NOPE:b0 + HEAD_PAD].astype(F32)
        dcqn = _dot(dqp_ref[...], wq_ref[...])
        dckvn = _dot_nt(dkv_ref[...], wkv_ref[...])
        dqg_ref[...] += jnp.sum(dcqn * cqh, axis=0, keepdims=True)
        dkvg_ref[...] += jnp.sum(dckvn * ckvh, axis=0, keepdims=True)
        dxh = dcqn * qg_ref[...]
        dzs_ref[:, 0:Q_LORA] = (rq * (dxh - cqh * jnp.mean(dxh * cqh, axis=-1, keepdims=True))).astype(BF16)
        dxh = dckvn * kvg_ref[...]
        dzs_ref[:, Q_LORA:Q_LORA + KV_LORA] = (
            rkv * (dxh - ckvh * jnp.mean(dxh * ckvh, axis=-1, keepdims=True))).astype(BF16)
        dzs_ref[:, Q_LORA + KV_LORA:ZS_W] = _rope_t(dkr, tabs).astype(BF16)

    full = lambda a: pl.BlockSpec(a.shape, lambda i: (0,) * a.ndim)
    rowb = lambda w: pl.BlockSpec((tm, w), lambda i: (i, 0))
    return pl.pallas_call(
        body, grid=(T // tm,),
        in_specs=[rowb(ZS_W), rowb(1), full(invf), full(qg), full(kvg), full(wuq_p), full(wukv),
                  rowb(HW), rowb(HW), rowb(D_MODEL)],
        out_specs=[rowb(ZS_W), rowb(Q_LORA), rowb(HW), rowb(KV_LORA), rowb(HW), full(qg), full(kvg)],
        out_shape=[jax.ShapeDtypeStruct((T, ZS_W), BF16), jax.ShapeDtypeStruct((T, Q_LORA), BF16),
                   jax.ShapeDtypeStruct((T, HW), BF16), jax.ShapeDtypeStruct((T, KV_LORA), BF16),
                   jax.ShapeDtypeStruct((T, HW), BF16), jax.ShapeDtypeStruct(qg.shape, F32),
                   jax.ShapeDtypeStruct(kvg.shape, F32)],
        name="mla_prep_bwd", compiler_params=_params("arbitrary"))(
            zs, pos, invf, qg, kvg, wuq_p, wukv, dq, dk, dv)


def _causal(tq, kmax, q0):
    r = lax.broadcasted_iota(jnp.int32, (tq, kmax), 0) + q0
    c = lax.broadcasted_iota(jnp.int32, (tq, kmax), 1)
    return c <= r


def _attn_fwd(q, k, v, batch, seq):
    tq = _tile(seq, ATTN_TILE)
    nq = seq // tq

    def body(q_ref, k_ref, v_ref, o_ref, lse_ref):
        diag = _causal(tq, tq, 0)
        for qi in range(nq):
            rows = slice(qi * tq, (qi + 1) * tq)
            qr = q_ref[rows, :]
            s_d = jnp.where(diag, _dot_nt(qr, k_ref[rows, :]), NEG_BIG)
            m = jnp.max(s_d, axis=-1, keepdims=True)
            if qi > 0:
                before = slice(0, qi * tq)
                s_b = _dot_nt(qr, k_ref[before, :])
                m = jnp.maximum(m, jnp.max(s_b, axis=-1, keepdims=True))
                p_b = jnp.exp(s_b - m)
                l = jnp.sum(p_b, axis=-1, keepdims=True)
                acc = _dot(p_b.astype(BF16), v_ref[before, :])
            p_d = jnp.exp(s_d - m)
            l_d = jnp.sum(p_d, axis=-1, keepdims=True)
            acc_d = _dot(p_d.astype(BF16), v_ref[rows, :])
            l, acc = (l + l_d, acc + acc_d) if qi > 0 else (l_d, acc_d)
            o_ref[rows, :] = (acc / l).astype(KEPT)
            lse_ref[rows, :] = jnp.broadcast_to(m + jnp.log(l), (tq, V_HEAD))

    return pl.pallas_call(
        body, grid=(batch, MLA_HEADS),
        in_specs=[pl.BlockSpec((seq, HEAD_PAD), lambda b, h: (b, h)),
                  pl.BlockSpec((seq, HEAD_PAD), lambda b, h: (b, h)),
                  pl.BlockSpec((seq, V_HEAD), lambda b, h: (b, h))],
        out_specs=[pl.BlockSpec((seq, V_HEAD), lambda b, h: (b, h)),
                   pl.BlockSpec((seq, V_HEAD), lambda b, h: (b, h))],
        out_shape=[jax.ShapeDtypeStruct((batch * seq, D_MODEL), KEPT),
                   jax.ShapeDtypeStruct((batch * seq, D_MODEL), F32)],
        name="attn_fwd", compiler_params=_params("parallel", "parallel"))(q, k, v)


def _attn_bwd(q, k, v, o, do, lse, batch, seq, dep):
    tq = _tile(seq, ATTN_TILE)
    nq = seq // tq

    def body(q_ref, k_ref, v_ref, o_ref, do_ref, lse_ref, dep_ref, dq_ref, dk_ref, dv_ref, dk_acc, dv_acc):
        dk_acc[...] = jnp.zeros_like(dk_acc)
        dv_acc[...] = jnp.zeros_like(dv_acc)
        for qi in range(nq):
            rows = slice(qi * tq, (qi + 1) * tq)
            kmax = (qi + 1) * tq
            qr = q_ref[rows, :]
            dor = do_ref[rows, :]
            kk = k_ref[0:kmax, :]
            s = _dot_nt(qr, kk)
            p = jnp.where(_causal(tq, kmax, qi * tq), jnp.exp(s - lse_ref[rows, 0:1]), 0.0)
            dp = _dot_nt(dor, v_ref[0:kmax, :])
            delta = jnp.sum(dor.astype(F32) * o_ref[rows, :].astype(F32), axis=-1, keepdims=True)
            ds = (p * (dp - delta)).astype(BF16)
            dq_ref[rows, :] = (_dot(ds, kk) * ATTN_SCALE).astype(BF16)
            dk_acc[0:kmax, :] += _dot_tn(ds, qr)
            dv_acc[0:kmax, :] += _dot_tn(p.astype(BF16), dor)
        dk_ref[...] = dk_acc[...].astype(BF16)
        dv_ref[...] = dv_acc[...].astype(BF16)

    qspec = pl.BlockSpec((seq, HEAD_PAD), lambda b, h: (b, h))
    vspec = pl.BlockSpec((seq, V_HEAD), lambda b, h: (b, h))
    T = batch * seq
    return pl.pallas_call(
        body, grid=(batch, MLA_HEADS),
        in_specs=[qspec, qspec, vspec, vspec, vspec, vspec, ANY],
        out_specs=[qspec, qspec, vspec],
        out_shape=[jax.ShapeDtypeStruct((T, MLA_HEADS * HEAD_PAD), BF16),
                   jax.ShapeDtypeStruct((T, MLA_HEADS * HEAD_PAD), BF16),
                   jax.ShapeDtypeStruct((T, D_MODEL), BF16)],
        scratch_shapes=[pltpu.VMEM((seq, HEAD_PAD), F32), pltpu.VMEM((seq, V_HEAD), F32)],
        name="attn_bwd", compiler_params=_params("parallel", "parallel"))(q, k, v, o, do, lse, dep)


def _merge_out(x, yag, zm, o, w_out, ffn_g):
    T = x.shape[0]
    tm = _tile(T, 512)

    def body(x_ref, ya_ref, gb_ref, o_ref, w_ref, g_ref, mg_ref, x1_ref, h2_ref):
        mg = (ya_ref[...].astype(F32) + _sigmoid(gb_ref[...].astype(F32)) * o_ref[...].astype(F32)).astype(BF16)
        mg_ref[...] = mg
        x1 = x_ref[...] + _dot(mg, w_ref[...])
        x1_ref[...] = x1
        r = lax.rsqrt(jnp.mean(x1 * x1, axis=-1, keepdims=True) + EPS)
        h2_ref[...] = (x1 * r * g_ref[...]).astype(BF16)

    row = pl.BlockSpec((tm, D_MODEL), lambda i: (i, 0))
    return pl.pallas_call(
        body, grid=(T // tm,),
        in_specs=[row, row, pl.BlockSpec((tm, D_MODEL), lambda i: (i, 3)), row,
                  pl.BlockSpec((D_MODEL, D_MODEL), lambda i: (0, 0)), pl.BlockSpec((1, D_MODEL), lambda i: (0, 0))],
        out_specs=[row, row, row],
        out_shape=[jax.ShapeDtypeStruct((T, D_MODEL), BF16), jax.ShapeDtypeStruct((T, D_MODEL), F32),
                   jax.ShapeDtypeStruct((T, D_MODEL), BF16)],
        name="merge_out", compiler_params=_params("parallel"))(x, yag, zm, o, w_out, ffn_g)


FF_TILE = 256
FF_BLOCKS = D_FF // FF_TILE
FFB_TILE = 256


def _shift_down(x, k):
    row = lax.broadcasted_iota(jnp.int32, x.shape, 0)
    return jnp.where(row >= k, pltpu.roll(x, k, 0), 0.0)


def _shift_up(x, k):
    n = x.shape[0]
    row = lax.broadcasted_iota(jnp.int32, x.shape, 0)
    return jnp.where(row < n - k, pltpu.roll(x, n - k, 0), 0.0)


def _conv(x, w_ref, b_ref):
    return b_ref[...] + w_ref[2:3, :] * x + w_ref[1:2, :] * _shift_down(x, 1) + w_ref[0:1, :] * _shift_down(x, 2)


EDGE = 16


def _conv_wrapped(x, w_ref, b_ref):
    return (b_ref[...] + w_ref[2:3, :] * x + w_ref[1:2, :] * pltpu.roll(x, 1, 0)
            + w_ref[0:1, :] * pltpu.roll(x, 2, 0))


def _up_act(h2, wt_up, cw, cb, batch, seq):
    def body(h_ref, wug_ref, wuv_ref, wg_ref, wv_ref, bg_ref, bv_ref, ug_ref, uv_ref, g_ref, v_ref, a_ref):
        h = h_ref[...]
        ug = _dot_nt(h, wug_ref[...])
        uv = _dot_nt(h, wuv_ref[...])
        ug_ref[...] = ug.astype(KEPT)
        uv_ref[...] = uv.astype(KEPT)

        def write(rows, gate, val):
            g_ref[rows, :] = gate.astype(KEPT)
            v_ref[rows, :] = val.astype(KEPT)
            a_ref[rows, :] = (gate * _sigmoid(gate) * val).astype(BF16)

        write(slice(None), _conv_wrapped(ug, wg_ref, bg_ref), _conv_wrapped(uv, wv_ref, bv_ref))
        write(slice(0, EDGE), _conv(ug[0:EDGE], wg_ref, bg_ref), _conv(uv[0:EDGE], wv_ref, bv_ref))

    blk = pl.BlockSpec((seq, FF_TILE), lambda b, j: (b, j))
    wup = lambda off: pl.BlockSpec((FF_TILE, D_MODEL), lambda b, j: (j + off, 0))
    wsp = lambda off: pl.BlockSpec((3, FF_TILE), lambda b, j: (0, j + off))
    bsp = lambda off: pl.BlockSpec((1, FF_TILE), lambda b, j: (0, j + off))
    T = batch * seq
    kept = jax.ShapeDtypeStruct((T, D_FF), KEPT)
    return pl.pallas_call(
        body, grid=(batch, FF_BLOCKS),
        in_specs=[pl.BlockSpec((seq, D_MODEL), lambda b, j: (b, 0)), wup(0), wup(FF_BLOCKS),
                  wsp(0), wsp(FF_BLOCKS), bsp(0), bsp(FF_BLOCKS)],
        out_specs=[blk] * 5,
        out_shape=[kept, kept, kept, kept, jax.ShapeDtypeStruct((T, D_FF), BF16)],
        name="up_act", compiler_params=_params("parallel", "arbitrary"))(h2, wt_up, wt_up, cw, cw, cb, cb)


def _ffn_act_bwd(upg, upv, gate, val, cw, dx2b, w_down, batch, seq):
    def half(du, x, w_ref, dx_ref, dw_ref):
        j = pl.program_id(1)
        n = du.shape[0]
        up1, up2 = pltpu.roll(du, n - 1, 0), pltpu.roll(du, n - 2, 0)
        dx_ref[...] = (w_ref[2:3, :] * du + w_ref[1:2, :] * up1 + w_ref[0:1, :] * up2).astype(BF16)
        tail = du[n - EDGE:n]
        dx_ref[n - EDGE:n, :] = (w_ref[2:3, :] * tail + w_ref[1:2, :] * _shift_up(tail, 1)
                                 + w_ref[0:1, :] * _shift_up(tail, 2)).astype(BF16)
        row = lax.broadcasted_iota(jnp.int32, (EDGE, du.shape[1]), 0)
        head, x_tail = du[0:EDGE], x[n - EDGE:n]
        wrap1 = jnp.sum(jnp.where(row >= EDGE - 1, pltpu.roll(head, EDGE - 1, 0), 0.0) * x_tail, axis=0, keepdims=True)
        wrap2 = jnp.sum(jnp.where(row >= EDGE - 2, pltpu.roll(head, EDGE - 2, 0), 0.0) * x_tail, axis=0, keepdims=True)
        dw_ref[j, 2:3, :] += jnp.sum(du * x, axis=0, keepdims=True)
        dw_ref[j, 1:2, :] += jnp.sum(up1 * x, axis=0, keepdims=True) - wrap1
        dw_ref[j, 0:1, :] += jnp.sum(up2 * x, axis=0, keepdims=True) - wrap2
        dw_ref[j, 3:4, :] += jnp.sum(du, axis=0, keepdims=True)

    def body(ug_ref, uv_ref, g_ref, v_ref, wg_ref, wv_ref, dx_ref, wd_ref, dg_ref, dv_ref, dwg_ref, dwv_ref):
        @pl.when((pl.program_id(0) == 0) & (pl.program_id(1) == 0))
        def _():
            dwg_ref[...] = jnp.zeros_like(dwg_ref)
            dwv_ref[...] = jnp.zeros_like(dwv_ref)

        gate, val = g_ref[...].astype(F32), v_ref[...].astype(F32)
        sg = _sigmoid(gate)
        dav = _dot_nt(dx_ref[...], wd_ref[...])
        half(dav * val * sg * (1.0 + gate * (1.0 - sg)), ug_ref[...].astype(F32), wg_ref, dg_ref, dwg_ref)
        half(dav * gate * sg, uv_ref[...].astype(F32), wv_ref, dv_ref, dwv_ref)

    nb = D_FF // FFB_TILE
    blk = pl.BlockSpec((seq, FFB_TILE), lambda b, j: (b, j))
    wsp = lambda off: pl.BlockSpec((3, FFB_TILE), lambda b, j: (0, j + off))
    acc = pl.BlockSpec((nb, 4, FFB_TILE), lambda b, j: (0, 0, 0))
    T = batch * seq
    dupg, dupv, dwg, dwv = pl.pallas_call(
        body, grid=(batch, nb),
        in_specs=[blk, blk, blk, blk, wsp(0), wsp(nb),
                  pl.BlockSpec((seq, D_MODEL), lambda b, j: (b, 0)),
                  pl.BlockSpec((FFB_TILE, D_MODEL), lambda b, j: (j, 0))],
        out_specs=[blk, blk, acc, acc],
        out_shape=[jax.ShapeDtypeStruct((T, D_FF), BF16), jax.ShapeDtypeStruct((T, D_FF), BF16),
                   jax.ShapeDtypeStruct((nb, 4, FFB_TILE), F32), jax.ShapeDtypeStruct((nb, 4, FFB_TILE), F32)],
        name="ffn_act_bwd", compiler_params=_params("arbitrary", "arbitrary"))(
            upg, upv, gate, val, cw, cw, dx2b, w_down)
    dwg, dwv = (jnp.transpose(a, (1, 0, 2)).reshape(4, D_FF) for a in (dwg, dwv))
    return dupg, dupv, dwg[:3], dwv[:3], dwg[3:], dwv[3:]


def _down_loss(a, w_down, x1, target, gfin):
    T = x1.shape[0]
    tm = _tile(T, 512)

    def body(a_ref, w_ref, x1_ref, t_ref, g_ref, dx_ref, dxb_ref, loss_ref, dg_ref):
        @pl.when(pl.program_id(0) == 0)
        def _():
            loss_ref[...] = jnp.zeros_like(loss_ref)
            dg_ref[...] = jnp.zeros_like(dg_ref)

        x2 = x1_ref[...] + _dot(a_ref[...], w_ref[...])
        r = lax.rsqrt(jnp.mean(x2 * x2, axis=-1, keepdims=True) + EPS)
        xh = x2 * r
        g = g_ref[...]
        diff = xh * g - t_ref[...]
        loss_ref[...] += 0.5 * jnp.sum(jnp.mean(diff * diff, axis=-1, keepdims=True))
        dy = diff * (1.0 / D_MODEL)
        dg_ref[...] += jnp.sum(dy * xh, axis=0, keepdims=True)
        dxh = dy * g
        dx = r * (dxh - xh * jnp.mean(dxh * xh, axis=-1, keepdims=True))
        dx_ref[...] = dx
        dxb_ref[...] = dx.astype(BF16)

    row = pl.BlockSpec((tm, D_MODEL), lambda i: (i, 0))
    vec = pl.BlockSpec((1, D_MODEL), lambda i: (0, 0))
    return pl.pallas_call(
        body, grid=(T // tm,),
        in_specs=[pl.BlockSpec((tm, D_FF), lambda i: (i, 0)),
                  pl.BlockSpec((D_FF, D_MODEL), lambda i: (0, 0)), row, row, vec],
        out_specs=[row, row, pl.BlockSpec((8, LANES), lambda i: (0, 0)), vec],
        out_shape=[jax.ShapeDtypeStruct((T, D_MODEL), F32), jax.ShapeDtypeStruct((T, D_MODEL), BF16),
                   jax.ShapeDtypeStruct((8, LANES), F32), jax.ShapeDtypeStruct((1, D_MODEL), F32)],
        name="down_loss", compiler_params=_params("arbitrary"))(a, w_down, x1, target, gfin)


def _local_step(x, positions, target, mix_norm, av_g, av_b, w_s, b_s, q_norm, kv_norm, ffn_norm, conv_b,
                final_norm, comm):
    batch, seq, _ = x.shape
    T = batch * seq
    x = x.reshape(T, D_MODEL)
    target = target.reshape(T, D_MODEL)
    pos = positions.reshape(T, 1)
    half = jnp.arange(0, QK_ROPE, 2, dtype=F32) / QK_ROPE
    inv_freq = 1.0 / (ROPE_THETA ** half)
    invf = jnp.concatenate([inv_freq, inv_freq, jnp.zeros((LANES - QK_ROPE,), F32)]).reshape(1, LANES)
    w_st = jnp.swapaxes(w_s, 1, 2)
    b_col = b_s.reshape(A_GROUPS, CHUNK, 1)

    wt_in = comm.in_weights()
    h, zm, zs = _in_proj(x, mix_norm, wt_in)
    yag = _mixer_a_fwd(zm, av_g, av_b, w_s, b_col)
    wuq_p, wukv, w_out = comm.mla_weights(after=yag)
    q, k, v = _mla_prep_fwd(zs, pos, invf, q_norm, kv_norm, wuq_p, wukv)
    o, lse = _attn_fwd(q, k, v, batch, seq)
    merged, x1, h2 = _merge_out(x, yag, zm, o, w_out, ffn_norm)
    wt_up, conv_w, w_down = comm.ffn_weights(after=merged)
    upg, upv, gate, val, act = _up_act(h2, wt_up, conv_w, conv_b, batch, seq)
    dx2, dx2b, loss_acc, d_final = _down_loss(act, w_down, x1, target, final_norm)

    d_wdown = _mm_tn(act, dx2b, "dw_down")
    dupg, dupv, dcwg, dcwv, dcbg, dcbv = _ffn_act_bwd(upg, upv, gate, val, conv_w, dx2b, w_down, batch, seq)
    d_wt_up = _mm_tn(dupv, h2, "dw_up_val", rows=2 * D_FF, row0=D_FF,
                     into=_mm_tn(dupg, h2, "dw_up_gate", rows=2 * D_FF))
    dx1, d_ffn_norm, dmerged = _proj_bwd(
        [dupg, dupv], wt_up, [(0, (0, D_FF), (0, D_FF)), (1, (0, D_FF), (D_FF, 2 * D_FF))],
        x1, ffn_norm, dx2, "up_proj_bwd", w2=w_out)
    d_wout = _mm_tn(merged, dx1, "dw_out")
    token = comm.send_ffn_grads(d_wdown, d_wt_up, jnp.concatenate([dcwg, dcwv], axis=1), d_wout)
    dzm, do, d_avg, d_avb, d_ws, d_bs = _mixer_bwd(zm, o, dmerged, av_g, av_b, w_s, w_st, b_col, token)
    token = comm.send_small_grads([
        d_avg, d_avb, _small_2d(d_ws), d_bs.reshape(A_GROUPS, CHUNK), d_ffn_norm,
        jnp.concatenate([dcbg, dcbv], axis=1), d_final])
    dq, dk, dv = _attn_bwd(q, k, v, o, do, lse, batch, seq, token)
    dzs, cqn, dqp, ckvn, dkv, d_qn, d_kvn = _mla_prep_bwd(zs, pos, invf, q_norm, kv_norm, wuq_p, wukv, dq, dk, dv)
    d_wt_main = _mm_tn(dzm, h, "dw_in_main")
    d_wt_zs = _mm_tn(dzs, h, "dw_in_small")
    token = comm.send_in_grads(d_wt_main, d_wt_zs)
    d_wuq_p = _mm_tn(dqp, cqn, "dw_uq", dep=token)
    d_wukv = _mm_tn(ckvn, dkv, "dw_ukv", dep=token)
    terms = [(0, (i * D_MODEL, (i + 1) * D_MODEL), rows) for i, rows in enumerate(IN_ROWS_MAIN)]
    terms.append((1, (0, ZS_W), IN_ROWS_ZS))
    dx, d_mix_norm = _proj_bwd([dzm, dzs], wt_in, terms, x, mix_norm, dx1, "in_proj_bwd", dep=token, rows=512)
    token = comm.send_late_grads(d_wuq_p, d_wukv, [d_qn, d_kvn, d_mix_norm, loss_acc])
    return dx.reshape(batch, seq, D_MODEL), token


MESH_ID = pl.DeviceIdType.MESH
EFFECT = pltpu.SideEffectType.DATAFLOW_SIDE_EFFECTING


def _mesh_pos():
    return lax.axis_index("x"), lax.axis_index("y"), lax.axis_index("c")


def _peer(pos, d):
    x, y, c = pos
    px = 1 - x if d & 4 else x
    py = 1 - y if d & 2 else y
    pc = 1 - c if d & 1 else c
    return (px, py, pc), 4 * px + 2 * py + pc


def _copy(src_ref, land_ref, send_sems, recv_sems, a, d, pos, exchange, landing_here):
    peer, pid = _peer(pos, d)
    me = 4 * pos[0] + 2 * pos[1] + pos[2]
    if exchange:
        src, dst = src_ref.at[pid], land_ref.at[d]
    else:
        src, dst = src_ref, land_ref.at[pid if landing_here else me]
    return pltpu.make_async_remote_copy(
        src_ref=src, dst_ref=dst, send_sem=send_sems.at[a * (N_DEV - 1) + d - 1],
        recv_sem=recv_sems.at[a * (N_DEV - 1) + d - 1],
        device_id=peer, device_id_type=MESH_ID)


def _start_copies(groups, modes, name, dep=None):
    sizes = [len(g) for g in groups]
    srcs = [s for g in groups for s in g]
    lands = [lax.empty(s.shape if modes[gi] else (N_DEV,) + s.shape, s.dtype)
             for gi, g in enumerate(groups) for s in g]
    n, ng = len(srcs), len(groups)
    n_in = 2 * n + (dep is not None)

    def body(*refs):
        src_refs, land_refs = refs[:n], refs[n:2 * n]
        sems = refs[n_in:n_in + 3 * ng]
        token = refs[-1]
        pos = _mesh_pos()
        k = 0
        for gi, size in enumerate(sizes):
            for a in range(size):
                _own_copy(src_refs[k], land_refs[k], sems[3 * gi + 2], a, pos, modes[gi]).start()
                for d in range(1, N_DEV):
                    _copy(src_refs[k], land_refs[k], sems[3 * gi], sems[3 * gi + 1], a, d, pos, modes[gi],
                          landing_here=False).start()
                k += 1
        token[...] = jnp.zeros_like(token)

    sem_shapes = []
    for size in sizes:
        remote = pltpu.SemaphoreType.DMA((size * (N_DEV - 1),))
        sem_shapes += [remote, remote, pltpu.SemaphoreType.DMA((size,))]
    out = pl.pallas_call(
        body, name=name,
        out_shape=(*sem_shapes, *[pltpu.HBM(a.shape, a.dtype) for a in srcs + lands],
                   jax.ShapeDtypeStruct((8, LANES), F32)),
        in_specs=[HBM] * (2 * n) + [ANY] * (dep is not None),
        out_specs=(*[SEM] * (3 * ng), *[HBM] * (2 * n), pl.BlockSpec(memory_space=pltpu.VMEM)),
        input_output_aliases={i: 3 * ng + i for i in range(2 * n)},
        compiler_params=pltpu.CompilerParams(has_side_effects=EFFECT),
    )(*[pltpu.with_memory_space_constraint(a, pltpu.HBM) for a in srcs + lands], *([dep] if dep is not None else []))
    thru = out[3 * ng:3 * ng + 2 * n]
    handles, k = [], 0
    for gi, size in enumerate(sizes):
        handles.append((out[3 * gi:3 * gi + 3], thru[k:k + size], thru[n + k:n + k + size]))
        k += size
    return handles, out[-1]


def _own_copy(src_ref, land_ref, local_sems, a, pos, exchange):
    me = 4 * pos[0] + 2 * pos[1] + pos[2]
    src, dst = (src_ref.at[me], land_ref.at[0]) if exchange else (src_ref, land_ref.at[me])
    return pltpu.make_async_copy(src, dst, local_sems.at[a])


def _wait_copies(handle, exchange, after, name):
    sems, srcs, lands = handle
    n = len(srcs)

    def body(*refs):
        src_refs, land_refs = refs[:n], refs[n:2 * n]
        send, recv, local = refs[2 * n:2 * n + 3]
        pos = _mesh_pos()
        for a in range(n):
            _own_copy(src_refs[a], land_refs[a], local, a, pos, exchange).wait()
            for d in range(1, N_DEV):
                cp = _copy(src_refs[a], land_refs[a], send, recv, a, d, pos, exchange, landing_here=True)
                cp.wait_send()
                cp.wait_recv()

    out = pl.pallas_call(
        body, name=name,
        out_shape=tuple(pltpu.HBM(a.shape, a.dtype) for a in (*srcs, *lands)),
        in_specs=[HBM] * (2 * n) + [SEM, SEM, SEM, ANY], out_specs=[HBM] * (2 * n),
        input_output_aliases={i: i for i in range(2 * n)},
        compiler_params=pltpu.CompilerParams(has_side_effects=EFFECT),
    )(*srcs, *lands, *sems, after)
    return out[n:]


def _gather_now(a, name):
    def body(x_ref, out_ref, send_sems, recv_sems, local_sem):
        x, y, c = _mesh_pos()
        me, sibling = (x, y, c), (x, y, 1 - c)
        chips = [(1 - x, y), (x, 1 - y), (1 - x, 1 - y)]

        def slot(p):
            return out_ref.at[4 * p[0] + 2 * p[1] + p[2]]

        def copy(k, block, to, src=None):
            return pltpu.make_async_remote_copy(
                src_ref=slot(block) if src is None else src, dst_ref=slot(block), send_sem=send_sems.at[k],
                recv_sem=recv_sems.at[k], device_id=to, device_id_type=MESH_ID)

        mine = pltpu.make_async_copy(x_ref, slot(me), local_sem)
        mine.start()
        first = [copy(0, me, sibling, src=x_ref)]
        first += [copy(1 + j, me, (*chip, c), src=x_ref) for j, chip in enumerate(chips)]
        for cp in first:
            cp.start()
        passed = [copy(4 + j, (*chip, c), sibling) for j, chip in enumerate(chips)]
        for j, chip in enumerate(chips):
            copy(1 + j, (*chip, c), me).wait_recv()
            passed[j].start()
        copy(0, sibling, me).wait_recv()
        for j, chip in enumerate(chips):
            copy(4 + j, (*chip, 1 - c), me).wait_recv()
        for cp in first + passed:
            cp.wait_send()
        mine.wait()

    return pl.pallas_call(
        body, in_specs=[ANY], out_specs=ANY,
        out_shape=jax.ShapeDtypeStruct((N_DEV,) + a.shape, a.dtype),
        scratch_shapes=[pltpu.SemaphoreType.DMA((N_DEV - 1,)), pltpu.SemaphoreType.DMA((N_DEV - 1,)),
                        pltpu.SemaphoreType.DMA],
        name=name, compiler_params=pltpu.CompilerParams(has_side_effects=True))(a)


def _sum_parts(p_ref):
    g = p_ref[0].astype(F32)
    for k in range(1, N_DEV):
        g = g + p_ref[k].astype(F32)
    return g


def _adamw_update(p_ref, w_ref, m_ref, v_ref, g_ref, d_ref, nm_ref, nv_ref):
    c1 = 1.0 - ADAM_B1 ** ADAM_STEP
    c2 = 1.0 - ADAM_B2 ** ADAM_STEP
    g = _sum_parts(p_ref)
    nm = ADAM_B1 * m_ref[...] + (1.0 - ADAM_B1) * g
    nv = ADAM_B2 * v_ref[...] + (1.0 - ADAM_B2) * (g * g)
    g_ref[...] = g
    nm_ref[...] = nm
    nv_ref[...] = nv
    d_ref[...] = -ADAM_LR * ((nm / c1) / (jnp.sqrt(nv / c2) + ADAM_EPS) + ADAM_WD * w_ref[...])


def _adamw_many(parts, ws, ms, vs, sums, name):
    n, ns = len(ws), len(sums)

    def body(*refs):
        ins, outs = refs[:4 * n + ns], refs[4 * n + ns:]
        for i in range(n):
            _adamw_update(ins[i], ins[n + i], ins[2 * n + i], ins[3 * n + i],
                          outs[i], outs[n + i], outs[2 * n + i], outs[3 * n + i])
        for i in range(ns):
            outs[4 * n + i][...] = _sum_parts(ins[4 * n + i])

    full = lambda a: pl.BlockSpec(a.shape, lambda: (0,) * a.ndim)
    args = [*parts, *ws, *ms, *vs, *sums]
    outs = [jax.ShapeDtypeStruct(w.shape, F32) for _ in range(4) for w in ws]
    outs += [jax.ShapeDtypeStruct(s.shape[1:], F32) for s in sums]
    res = pl.pallas_call(
        body, in_specs=[full(a) for a in args], out_specs=[full(o) for o in outs], out_shape=outs,
        name=name, compiler_params=pltpu.CompilerParams(vmem_limit_bytes=VMEM_LIMIT))(*args)
    return res[:n], res[n:2 * n], res[2 * n:3 * n], res[3 * n:4 * n], res[4 * n:]


def _adamw(parts, w, m, v, name):
    R, C = w.shape
    tr, tc = R, C
    if N_DEV * R * C * parts.dtype.itemsize > SMALL_BLOCK_BYTES:
        tr = next((t for t in range(min(R, 256) // 16 * 16, 15, -16) if R % t == 0), R)
        if tr == R:
            tc = _tile(C, 256)

    def body(p_ref, w_ref, m_ref, v_ref, g_ref, d_ref, nm_ref, nv_ref):
        _adamw_update(p_ref, w_ref, m_ref, v_ref, g_ref, d_ref, nm_ref, nv_ref)

    blk = pl.BlockSpec((tr, tc), lambda i, j: (i, j))
    shp = jax.ShapeDtypeStruct((R, C), F32)
    return pl.pallas_call(
        body, grid=(R // tr, C // tc),
        in_specs=[pl.BlockSpec((N_DEV, tr, tc), lambda i, j: (0, i, j)), blk, blk, blk],
        out_specs=[blk, blk, blk, blk], out_shape=[shp, shp, shp, shp],
        name=name, compiler_params=_params("parallel", "parallel"))(parts, w, m, v)


SPLIT_V = 2 * D_MODEL
SPLIT_KR = SPLIT_V + Q_LORA + KV_LORA + QK_ROPE
IN_DIM = SPLIT_KR + 2 * D_MODEL
IN_ROWS_MAIN = ((0, D_MODEL), (D_MODEL, SPLIT_V), (SPLIT_KR, SPLIT_KR + D_MODEL), (SPLIT_KR + D_MODEL, IN_DIM))
IN_ROWS_ZS = (SPLIT_V, SPLIT_V + ZS_W)

SMALL_EARLY = ("a_v_norm_g", "a_v_norm_b", "a_spatial_w", "a_spatial_b", "ffn_norm", "conv_b", "final_norm")
SMALL_LATE = ("q_a_norm", "kv_a_norm", "mix_norm")


def _small_2d(a):
    return a.reshape(-1, a.shape[-1])


def _cols_from_shards(g):
    return jnp.transpose(g, (1, 0, 2)).reshape(g.shape[1], N_DEV * g.shape[2])


def _shards_from_cols(a):
    R, W = a.shape
    return jnp.transpose(a.reshape(R, N_DEV, W // N_DEV), (1, 0, 2))


class _Comm:
    GATHER_GROUPS = (("w_uq", "w_ukv", "w_out"), ("w_up", "conv_w", "w_down"))
    FFN_GRADS = ("w_down", "w_up", "conv_w", "w_out")
    LATE_GRADS = ("w_uq", "w_ukv")
    TRANSPOSED = ("w_in", "w_up", "w_uq")

    def __init__(self, shards):
        local = {n: a.astype(F32 if n == "conv_w" else BF16) for n, a in shards.items()}
        self.g_in = _gather_now(local["w_in"], "gather_w_in")
        groups = [[local[n] for n in g] for g in self.GATHER_GROUPS]
        (self.h_mla, self.h_ffn), _ = _start_copies(groups, [False] * 2, "gather_start", dep=self.g_in)

    def in_weights(self):
        return self.g_in.reshape(IN_DIM, D_MODEL)

    def mla_weights(self, after):
        g_uq, g_ukv, g_out = _wait_copies(self.h_mla, False, after, "gather_wait_mla")
        wuq_p = jnp.pad(g_uq, ((0, 0), (0, HEAD_PAD - QK_HEAD), (0, 0))).reshape(MLA_HEADS * HEAD_PAD, Q_LORA)
        return wuq_p, _cols_from_shards(g_ukv), g_out.reshape(D_MODEL, D_MODEL)

    def ffn_weights(self, after):
        g_up, g_cw, g_down = _wait_copies(self.h_ffn, False, after, "gather_wait_ffn")
        return g_up.reshape(2 * D_FF, D_MODEL), _cols_from_shards(g_cw), g_down.reshape(D_FF, D_MODEL)

    def send_ffn_grads(self, d_wdown, d_wt_up, d_convw, d_wout):
        group = [d_wdown.reshape(N_DEV, D_FF // N_DEV, D_MODEL), d_wt_up.reshape(N_DEV, 2 * D_FF // N_DEV, D_MODEL),
                 _shards_from_cols(d_convw), d_wout.reshape(N_DEV, D_MODEL // N_DEV, D_MODEL)]
        (self.h_ffn_grads,), token = _start_copies([group], [True], "ffn_grads_start")
        return token

    def send_small_grads(self, grads):
        (self.h_small_early,), token = _start_copies([grads], [False], "small_grads_start")
        return token

    def send_in_grads(self, d_wt_main, d_wt_zs):
        d_in = jnp.concatenate([d_wt_main[:SPLIT_V], d_wt_zs[:SPLIT_KR - SPLIT_V], d_wt_main[SPLIT_V:]], axis=0)
        blocks = d_in.reshape(N_DEV, IN_DIM // N_DEV, D_MODEL)
        (self.h_in_grads,), token = _start_copies([[blocks]], [True], "in_grads_start")
        return token

    def send_late_grads(self, d_wuq_p, d_wukv, small):
        d_uq = d_wuq_p.reshape(MLA_HEADS, HEAD_PAD, Q_LORA)[:, :QK_HEAD, :]
        (self.h_late_grads, self.h_late_small), token = _start_copies(
            [[d_uq, _shards_from_cols(d_wukv)], small], [True, False], "late_grads_start")
        return token


def kernel(x, positions, mix_norm, w_in, a_v_norm_g, a_v_norm_b, a_spatial_w, a_spatial_b, q_a_norm, w_uq, kv_a_norm, w_ukv, w_out, ffn_norm, w_up, conv_w, conv_b, w_down, final_norm, loss_target, m_mix_norm, m_w_in, m_a_v_norm_g, m_a_v_norm_b, m_a_spatial_w, m_a_spatial_b, m_q_a_norm, m_w_uq, m_kv_a_norm, m_w_ukv, m_w_out, m_ffn_norm, m_w_up, m_conv_w, m_conv_b, m_w_down, m_final_norm, v_mix_norm, v_w_in, v_a_v_norm_g, v_a_v_norm_b, v_a_spatial_w, v_a_spatial_b, v_q_a_norm, v_w_uq, v_kv_a_norm, v_w_ukv, v_w_out, v_ffn_norm, v_w_up, v_conv_w, v_conv_b, v_w_down, v_final_norm):
    names = ("mix_norm", "w_in", "a_v_norm_g", "a_v_norm_b", "a_spatial_w", "a_spatial_b", "q_a_norm", "w_uq",
             "kv_a_norm", "w_ukv", "w_out", "ffn_norm", "w_up", "conv_w", "conv_b", "w_down", "final_norm")
    w = dict(zip(names, (mix_norm, w_in, a_v_norm_g, a_v_norm_b, a_spatial_w, a_spatial_b, q_a_norm, w_uq,
                         kv_a_norm, w_ukv, w_out, ffn_norm, w_up, conv_w, conv_b, w_down, final_norm)))
    m = dict(zip(names, (m_mix_norm, m_w_in, m_a_v_norm_g, m_a_v_norm_b, m_a_spatial_w, m_a_spatial_b,
                         m_q_a_norm, m_w_uq, m_kv_a_norm, m_w_ukv, m_w_out, m_ffn_norm, m_w_up, m_conv_w,
                         m_conv_b, m_w_down, m_final_norm)))
    v = dict(zip(names, (v_mix_norm, v_w_in, v_a_v_norm_g, v_a_v_norm_b, v_a_spatial_w, v_a_spatial_b,
                         v_q_a_norm, v_w_uq, v_kv_a_norm, v_w_ukv, v_w_out, v_ffn_norm, v_w_up, v_conv_w,
                         v_conv_b, v_w_down, v_final_norm)))
    shapes = {n: w[n].shape for n in names}
    def view(tree, n):
        a = tree[n].reshape(tree[n].shape[-2:])
        return a.T if n in _Comm.TRANSPOSED else a

    comm = _Comm({n: view(w, n) for n in ("w_in",) + _Comm.GATHER_GROUPS[0] + _Comm.GATHER_GROUPS[1]})

    grad_x, token = _local_step(
        x, positions, loss_target, w["mix_norm"], w["a_v_norm_g"], w["a_v_norm_b"], w["a_spatial_w"][0],
        w["a_spatial_b"][0], w["q_a_norm"], w["kv_a_norm"], w["ffn_norm"], w["conv_b"],
        w["final_norm"].reshape(1, D_MODEL), comm)

    out_g, out_d, out_m, out_v = {}, {}, {}, {}

    def update(n, parts):
        res = _adamw(parts, view(w, n), view(m, n), view(v, n), "adamw_" + n)
        out_g[n], out_d[n], out_m[n], out_v[n] = (
            (t.T if n in _Comm.TRANSPOSED else t).reshape(shapes[n]) for t in res)
        return res[1]

    def update_small(names, parts, sums, name):
        res = _adamw_many(parts, *[[_small_2d(t[n]) for n in names] for t in (w, m, v)], sums, name)
        for i, n in enumerate(names):
            out_g[n], out_d[n], out_m[n], out_v[n] = (r[i].reshape(shapes[n]) for r in res[:4])
        return res

    for n, parts in zip(_Comm.FFN_GRADS, _wait_copies(comm.h_ffn_grads, True, token, "ffn_grads_wait")):
        last = update(n, parts)
    early = _wait_copies(comm.h_small_early, False, last, "small_grads_wait")
    last = update_small(SMALL_EARLY, early, [], "adamw_small")[1][0]
    last = update("w_in", _wait_copies(comm.h_in_grads, True, last, "in_grads_wait")[0])
    for n, parts in zip(_Comm.LATE_GRADS, _wait_copies(comm.h_late_grads, True, last, "late_grads_wait")):
        last = update(n, parts)
    late = _wait_copies(comm.h_late_small, False, last, "late_small_wait")
    res = update_small(SMALL_LATE, late[:-1], late[-1:], "adamw_late")
    loss = res[4][0][0, 0]

    return (loss, grad_x, *[out_g[n] for n in names], *[out_d[n] for n in names],
            *[out_m[n] for n in names], *[out_v[n] for n in names])
```

```python
import math

import jax
import jax.numpy as jnp
from jax import lax
from jax.experimental import pallas as pl
from jax.experimental.pallas import tpu as pltpu

F32 = jnp.float32
BF16 = jnp.bfloat16
KEPT = jnp.bfloat16

N_DEV = 8
D_MODEL = 1024
EPS = 1e-6
A_GROUPS = 8
CHUNK = 128
MLA_HEADS = 8
QK_NOPE = 128
QK_ROPE = 64
QK_HEAD = QK_NOPE + QK_ROPE
HEAD_PAD = 256
V_HEAD = 128
Q_LORA = 256
KV_LORA = 128
ROPE_THETA = 10000.0
D_FF = 2816
ZS_W = 512
ATTN_SCALE = QK_HEAD ** -0.5
ATTN_TILE = 512
NEG_BIG = -1e30

ADAM_LR = 0.001
ADAM_B1 = 0.9
ADAM_B2 = 0.999
ADAM_EPS = 1e-08
ADAM_WD = 0.01
ADAM_STEP = 10

VMEM_LIMIT = 56 * 1024 * 1024
SMALL_BLOCK_BYTES = 5 * 1024 * 1024
LANES = 128

GELU_K = math.sqrt(2.0 / math.pi)
GELU_C = 0.044715

ANY = pl.BlockSpec(memory_space=pl.ANY)
HBM = pl.BlockSpec(memory_space=pltpu.HBM)
SEM = pl.BlockSpec(memory_space=pltpu.SEMAPHORE)


def _tile(n, pref):
    for t in (pref, 512, 256, 128, 64, 32, 16, 8):
        if t <= pref and n % t == 0:
            return t
    return n


def _wide_tile(n, cap=1408):
    return next((t for t in range(min(n, cap) // LANES * LANES, 0, -LANES) if n % t == 0), n)


def _params(*sem):
    return pltpu.CompilerParams(dimension_semantics=sem, vmem_limit_bytes=VMEM_LIMIT)


def _dot(a, b):
    return jnp.dot(a, b, preferred_element_type=F32)


def _dot_nt(a, b):
    return lax.dot_general(a, b, (((1,), (1,)), ((), ())), preferred_element_type=F32)


def _dot_tn(a, b):
    return lax.dot_general(a, b, (((0,), (0,)), ((), ())), preferred_element_type=F32)


def _sigmoid(x):
    return 1.0 / (1.0 + jnp.exp(-x))


def _gelu(x):
    t = jnp.tanh(GELU_K * (x + GELU_C * x * x * x))
    return 0.5 * x * (1.0 + t)


def _gelu_and_grad(x):
    x2 = x * x
    t = jnp.tanh(GELU_K * (x + GELU_C * x * x2))
    half = 0.5 * (1.0 + t)
    return x * half, half + 0.5 * x * (1.0 - t * t) * GELU_K * (1.0 + 3.0 * GELU_C * x2)


def _in_proj(x, g, wt):
    T, Dm = x.shape
    tm = _tile(T, 512)

    def body(x_ref, g_ref, wt_ref, h_ref, zm_ref, zs_ref):
        xf = x_ref[...]
        r = lax.rsqrt(jnp.mean(xf * xf, axis=-1, keepdims=True) + EPS)
        h = (xf * r * g_ref[...]).astype(BF16)
        h_ref[...] = h
        for i, (r0, r1) in enumerate(IN_ROWS_MAIN):
            zm_ref[:, i * D_MODEL:(i + 1) * D_MODEL] = _dot_nt(h, wt_ref[r0:r1, :]).astype(KEPT)
        zs_ref[...] = _dot_nt(h, wt_ref[IN_ROWS_ZS[0]:IN_ROWS_ZS[1], :])

    row = lambda n: pl.BlockSpec((tm, n), lambda i: (i, 0))
    return pl.pallas_call(
        body, grid=(T // tm,),
        in_specs=[row(Dm), pl.BlockSpec((1, Dm), lambda i: (0, 0)), pl.BlockSpec(wt.shape, lambda i: (0, 0))],
        out_specs=[row(Dm), row(4 * D_MODEL), row(ZS_W)],
        out_shape=[jax.ShapeDtypeStruct((T, Dm), BF16), jax.ShapeDtypeStruct((T, 4 * D_MODEL), KEPT),
                   jax.ShapeDtypeStruct((T, ZS_W), F32)],
        name="in_proj", compiler_params=_params("parallel"))(x, g, wt)


def _proj_bwd(acts, wt, terms, x, g, dres, name, w2=None, dep=None, rows=256):
    T, Dm = x.shape
    tm = _tile(T, rows)
    n_a = len(acts)

    def body(*refs):
        ins, outs = refs[:n_a + 4 + (w2 is not None) + (dep is not None)], refs[-2 - (w2 is not None):]
        wt_ref, x_ref, g_ref, dres_ref = ins[n_a:n_a + 4]
        dx_ref, dg_ref = outs[0], outs[1]

        @pl.when(pl.program_id(0) == 0)
        def _():
            dg_ref[...] = jnp.zeros_like(dg_ref)

        dy = None
        for i, (c0, c1), (r0, r1) in terms:
            t = _dot(ins[i][:, c0:c1], wt_ref[r0:r1, :])
            dy = t if dy is None else dy + t
        xf = x_ref[...]
        r = lax.rsqrt(jnp.mean(xf * xf, axis=-1, keepdims=True) + EPS)
        xh = xf * r
        dg_ref[...] += jnp.sum(dy * xh, axis=0, keepdims=True)
        dxh = dy * g_ref[...]
        dx = dres_ref[...] + r * (dxh - xh * jnp.mean(dxh * xh, axis=-1, keepdims=True))
        dx_ref[...] = dx
        if w2 is not None:
            outs[2][...] = _dot_nt(dx.astype(BF16), ins[n_a + 4][...]).astype(KEPT)

    row = pl.BlockSpec((tm, Dm), lambda i: (i, 0))
    vec = pl.BlockSpec((1, Dm), lambda i: (0, 0))
    in_specs = [pl.BlockSpec((tm, a.shape[1]), lambda i: (i, 0)) for a in acts]
    in_specs += [pl.BlockSpec(wt.shape, lambda i: (0, 0)), row, vec, row]
    args = [*acts, wt, x, g, dres]
    out_specs = [row, vec]
    out_shape = [jax.ShapeDtypeStruct((T, Dm), F32), jax.ShapeDtypeStruct((1, Dm), F32)]
    if w2 is not None:
        in_specs.append(pl.BlockSpec(w2.shape, lambda i: (0, 0)))
        args.append(w2)
        out_specs.append(pl.BlockSpec((tm, w2.shape[0]), lambda i: (i, 0)))
        out_shape.append(jax.ShapeDtypeStruct((T, w2.shape[0]), KEPT))
    if dep is not None:
        in_specs.append(ANY)
        args.append(dep)
    return pl.pallas_call(
        body, grid=(T // tm,), in_specs=in_specs, out_specs=out_specs, out_shape=out_shape,
        name=name, compiler_params=_params("arbitrary"))(*args)


def _mm_tn(a, b, name, dep=None, rows=None, row0=0, into=None):
    T, M = a.shape
    N = b.shape[1]
    tm, tn, tt = _wide_tile(M), _wide_tile(N), _tile(T, 2048)
    n_t = T // tt
    off = row0 // tm
    extra = ([dep] if dep is not None else []) + ([into] if into is not None else [])

    def body(a_ref, b_ref, *refs):
        o_ref, acc_ref = refs[-2:]
        t = pl.program_id(2)

        @pl.when(t == 0)
        def _():
            acc_ref[...] = jnp.zeros_like(acc_ref)

        acc_ref[...] += _dot_tn(a_ref[...].astype(BF16), b_ref[...].astype(BF16))

        @pl.when(t == n_t - 1)
        def _():
            o_ref[...] = acc_ref[...].astype(BF16)

    return pl.pallas_call(
        body, grid=(M // tm, N // tn, n_t),
        in_specs=[pl.BlockSpec((tt, tm), lambda i, j, t: (t, i)),
                  pl.BlockSpec((tt, tn), lambda i, j, t: (t, j))] + [ANY] * len(extra),
        out_specs=pl.BlockSpec((tm, tn), lambda i, j, t: (i + off, j)),
        out_shape=jax.ShapeDtypeStruct((rows or M, N), BF16),
        scratch_shapes=[pltpu.VMEM((tm, tn), F32)],
        input_output_aliases={} if into is None else {1 + len(extra): 0},
        name=name, compiler_params=_params("parallel", "parallel", "arbitrary"))(a, b, *extra)


def _layer_norm_fwd(gv, g, b):
    mu = jnp.mean(gv, axis=-1, keepdims=True)
    xc = gv - mu
    rs = lax.rsqrt(jnp.mean(xc * xc, axis=-1, keepdims=True) + EPS)
    xh = xc * rs
    return xh, rs, xh * g + b


def _tri_mask(transposed=False):
    r = lax.broadcasted_iota(jnp.int32, (CHUNK, CHUNK), 0)
    c = lax.broadcasted_iota(jnp.int32, (CHUNK, CHUNK), 1)
    return r <= c if transposed else c <= r


def _mixer_a_fwd(zm, av_g, av_b, w_s, b_col):
    T = zm.shape[0]
    tm = _tile(T, 512)
    n_chunk = tm // CHUNK

    def body(u_ref, v_ref, ga_ref, g_ref, b_ref, w_ref, bc_ref, y_ref, vn_s, mx_s):
        gu = _gelu(u_ref[...].astype(F32))
        _, _, vn = _layer_norm_fwd(_gelu(v_ref[...].astype(F32)), g_ref[...], b_ref[...])
        vn_s[...] = vn.astype(BF16)
        tri = _tri_mask()
        for gi in range(A_GROUPS):
            wm = jnp.where(tri, w_ref[gi], 0.0).astype(BF16)
            cols = slice(gi * CHUNK, (gi + 1) * CHUNK)
            for n in range(n_chunk):
                rows = slice(n * CHUNK, (n + 1) * CHUNK)
                mx_s[rows, cols] = _dot(wm, vn_s[rows, cols]) + bc_ref[gi]
        y_ref[...] = (_sigmoid(ga_ref[...].astype(F32)) * gu * mx_s[...]).astype(KEPT)

    col = lambda c: pl.BlockSpec((tm, D_MODEL), lambda i: (i, c))
    vec = pl.BlockSpec((1, D_MODEL), lambda i: (0, 0))
    return pl.pallas_call(
        body, grid=(T // tm,),
        in_specs=[col(0), col(1), col(2), vec, vec,
                  pl.BlockSpec((A_GROUPS, CHUNK, CHUNK), lambda i: (0, 0, 0)),
                  pl.BlockSpec((A_GROUPS, CHUNK, 1), lambda i: (0, 0, 0))],
        out_specs=pl.BlockSpec((tm, D_MODEL), lambda i: (i, 0)),
        out_shape=jax.ShapeDtypeStruct((T, D_MODEL), KEPT),
        scratch_shapes=[pltpu.VMEM((tm, D_MODEL), BF16), pltpu.VMEM((tm, D_MODEL), F32)],
        name="mixer_a_fwd", compiler_params=_params("parallel"))(zm, zm, zm, av_g, av_b, w_s, b_col)


def _mixer_bwd(zm, o, dm, av_g, av_b, w_s, w_st, b_col, dep):
    T = zm.shape[0]
    tm = _tile(T, 256)
    n_chunk = tm // CHUNK

    def body(u_ref, v_ref, ga_ref, gb_ref, o_ref, dm_ref, g_ref, b_ref, w_ref, wt_ref, bc_ref, dep_ref,
             dz_ref, do_ref, dg_ref, db_ref, dw_ref, dbs_ref, vn_s, mx_s, dmx_s, dvn_s):
        @pl.when(pl.program_id(0) == 0)
        def _():
            dg_ref[...] = jnp.zeros_like(dg_ref)
            db_ref[...] = jnp.zeros_like(db_ref)
            dw_ref[...] = jnp.zeros_like(dw_ref)
            dbs_ref[...] = jnp.zeros_like(dbs_ref)

        dm_v = dm_ref[...].astype(F32)
        gb = gb_ref[...].astype(F32)
        sb = _sigmoid(gb)
        o_v = o_ref[...].astype(F32)
        do_ref[...] = (dm_v * sb).astype(BF16)
        dz_ref[:, 3 * D_MODEL:4 * D_MODEL] = (dm_v * o_v * sb * (1.0 - sb)).astype(BF16)
        u = u_ref[...].astype(F32)
        v = v_ref[...].astype(F32)
        gu, gu_grad = _gelu_and_grad(u)
        gv, gv_grad = _gelu_and_grad(v)
        xh, rs, vn = _layer_norm_fwd(gv, g_ref[...], b_ref[...])
        vn_s[...] = vn.astype(BF16)
        tri = _tri_mask()
        for gi in range(A_GROUPS):
            wm = jnp.where(tri, w_ref[gi], 0.0).astype(BF16)
            cols = slice(gi * CHUNK, (gi + 1) * CHUNK)
            for n in range(n_chunk):
                rows = slice(n * CHUNK, (n + 1) * CHUNK)
                mx_s[rows, cols] = _dot(wm, vn_s[rows, cols]) + bc_ref[gi]
        mixed = mx_s[...]
        sa = _sigmoid(ga_ref[...].astype(F32))
        dya = dm_v * sa
        dz_ref[:, 2 * D_MODEL:3 * D_MODEL] = (dm_v * gu * mixed * sa * (1.0 - sa)).astype(BF16)
        dz_ref[:, 0:D_MODEL] = (dya * mixed * gu_grad).astype(BF16)
        dmx = dya * gu
        dmx_s[...] = dmx.astype(BF16)
        tri_t = _tri_mask(transposed=True)
        for gi in range(A_GROUPS):
            wmt = jnp.where(tri_t, wt_ref[gi], 0.0).astype(BF16)
            cols = slice(gi * CHUNK, (gi + 1) * CHUNK)
            dw_acc = jnp.zeros((CHUNK, CHUNK), F32)
            dmx_sum = jnp.zeros((CHUNK, CHUNK), F32)
            for n in range(n_chunk):
                rows = slice(n * CHUNK, (n + 1) * CHUNK)
                blk = dmx_s[rows, cols]
                dvn_s[rows, cols] = _dot(wmt, blk)
                dw_acc = dw_acc + _dot_nt(blk, vn_s[rows, cols])
                dmx_sum = dmx_sum + dmx[rows, cols]
            dw_ref[gi] += jnp.where(tri, dw_acc, 0.0)
            dbs_ref[gi] += jnp.sum(dmx_sum, axis=-1, keepdims=True)
        dvn = dvn_s[...]
        dg_ref[...] += jnp.sum(dvn * xh, axis=0, keepdims=True)
        db_ref[...] += jnp.sum(dvn, axis=0, keepdims=True)
        dxh = dvn * g_ref[...]
        dgv = rs * (dxh - jnp.mean(dxh, axis=-1, keepdims=True)
                    - xh * jnp.mean(dxh * xh, axis=-1, keepdims=True))
        dz_ref[:, D_MODEL:2 * D_MODEL] = (dgv * gv_grad).astype(BF16)

    col = lambda c: pl.BlockSpec((tm, D_MODEL), lambda i: (i, c))
    row = pl.BlockSpec((tm, D_MODEL), lambda i: (i, 0))
    vec = pl.BlockSpec((1, D_MODEL), lambda i: (0, 0))
    wsp = pl.BlockSpec((A_GROUPS, CHUNK, CHUNK), lambda i: (0, 0, 0))
    bsp = pl.BlockSpec((A_GROUPS, CHUNK, 1), lambda i: (0, 0, 0))
    return pl.pallas_call(
        body, grid=(T // tm,),
        in_specs=[col(0), col(1), col(2), col(3), row, row, vec, vec, wsp, wsp, bsp, ANY],
        out_specs=[pl.BlockSpec((tm, 4 * D_MODEL), lambda i: (i, 0)), row, vec, vec, wsp, bsp],
        out_shape=[jax.ShapeDtypeStruct((T, 4 * D_MODEL), BF16), jax.ShapeDtypeStruct((T, D_MODEL), BF16),
                   jax.ShapeDtypeStruct((1, D_MODEL), F32), jax.ShapeDtypeStruct((1, D_MODEL), F32),
                   jax.ShapeDtypeStruct((A_GROUPS, CHUNK, CHUNK), F32),
                   jax.ShapeDtypeStruct((A_GROUPS, CHUNK, 1), F32)],
        scratch_shapes=[pltpu.VMEM((tm, D_MODEL), BF16), pltpu.VMEM((tm, D_MODEL), F32),
                        pltpu.VMEM((tm, D_MODEL), BF16), pltpu.VMEM((tm, D_MODEL), F32)],
        name="mixer_bwd", compiler_params=_params("arbitrary"))(
            zm, zm, zm, zm, o, dm, av_g, av_b, w_s, w_st, b_col, dep)


def _rope_tables(pos_ref, invf_ref):
    ang = pos_ref[...].astype(F32) * invf_ref[...]
    lane = lax.broadcasted_iota(jnp.int32, ang.shape, 1)
    cos, sin = jnp.cos(ang), jnp.sin(ang)
    c = jnp.where(lane < QK_ROPE, cos, 0.0)
    sa = jnp.where(lane < QK_ROPE // 2, -sin, 0.0)
    sb = jnp.where((lane >= QK_ROPE // 2) & (lane < QK_ROPE), sin, 0.0)
    return c, sa, sb


def _rope(blk, tabs):
    c, sa, sb = tabs
    return blk * c + pltpu.roll(blk, LANES - QK_ROPE // 2, 1) * sa + pltpu.roll(blk, QK_ROPE // 2, 1) * sb


def _rope_t(dout, tabs):
    c, sa, sb = tabs
    return dout * c + pltpu.roll(dout * sa, QK_ROPE // 2, 1) + pltpu.roll(dout * sb, LANES - QK_ROPE // 2, 1)


def _rms_small(x, g):
    r = lax.rsqrt(jnp.mean(x * x, axis=-1, keepdims=True) + EPS)
    xh = x * r
    return xh, r, xh * g


def _mla_prep_fwd(zs, pos, invf, qg, kvg, wuq_p, wukv):
    T = zs.shape[0]
    tm = _tile(T, 512)
    HW = MLA_HEADS * HEAD_PAD

    def body(zs_ref, pos_ref, invf_ref, qg_ref, kvg_ref, wq_ref, wkv_ref, q_ref, k_ref, v_ref):
        tabs = _rope_tables(pos_ref, invf_ref)
        _, _, cqn = _rms_small(zs_ref[:, 0:Q_LORA], qg_ref[...])
        _, _, ckvn = _rms_small(zs_ref[:, Q_LORA:Q_LORA + KV_LORA], kvg_ref[...])
        q = _dot_nt(cqn.astype(BF16), wq_ref[...]) * ATTN_SCALE
        kv = _dot(ckvn.astype(BF16), wkv_ref[...])
        kr = _rope(zs_ref[:, Q_LORA + KV_LORA:ZS_W], tabs).astype(BF16)
        for h in range(MLA_HEADS):
            b0 = h * HEAD_PAD
            q_ref[:, b0:b0 + QK_NOPE] = q[:, b0:b0 + QK_NOPE].astype(BF16)
            q_ref[:, b0 + QK_NOPE:b0 + HEAD_PAD] = _rope(q[:, b0 + QK_NOPE:b0 + HEAD_PAD], tabs).astype(BF16)
            k_ref[:, b0:b0 + QK_NOPE] = kv[:, b0:b0 + QK_NOPE].astype(BF16)
            k_ref[:, b0 + QK_NOPE:b0 + HEAD_PAD] = kr
            v_ref[:, h * V_HEAD:(h + 1) * V_HEAD] = kv[:, b0 + QK_NOPE:b0 + HEAD_PAD].astype(BF16)

    full = lambda a: pl.BlockSpec(a.shape, lambda i: (0,) * a.ndim)
    return pl.pallas_call(
        body, grid=(T // tm,),
        in_specs=[pl.BlockSpec((tm, ZS_W), lambda i: (i, 0)), pl.BlockSpec((tm, 1), lambda i: (i, 0)),
                  full(invf), full(qg), full(kvg), full(wuq_p), full(wukv)],
        out_specs=[pl.BlockSpec((tm, HW), lambda i: (i, 0)), pl.BlockSpec((tm, HW), lambda i: (i, 0)),
                   pl.BlockSpec((tm, D_MODEL), lambda i: (i, 0))],
        out_shape=[jax.ShapeDtypeStruct((T, HW), BF16), jax.ShapeDtypeStruct((T, HW), BF16),
                   jax.ShapeDtypeStruct((T, D_MODEL), BF16)],
        name="mla_prep_fwd", compiler_params=_params("parallel"))(zs, pos, invf, qg, kvg, wuq_p, wukv)


def _mla_prep_bwd(zs, pos, invf, qg, kvg, wuq_p, wukv, dq, dk, dv):
    T = zs.shape[0]
    tm = _tile(T, 512)
    n_t = T // tm
    HW = MLA_HEADS * HEAD_PAD

    def body(zs_ref, pos_ref, invf_ref, qg_ref, kvg_ref, wq_ref, wkv_ref, dq_ref, dk_ref, dv_ref,
             dzs_ref, dwq_ref, dwkv_ref, dqg_ref, dkvg_ref, dqp_ref, dkv_ref, accq_ref, acckv_ref):
        @pl.when(pl.program_id(0) == 0)
        def _():
            dqg_ref[...] = jnp.zeros_like(dqg_ref)
            dkvg_ref[...] = jnp.zeros_like(dkvg_ref)
            accq_ref[...] = jnp.zeros_like(accq_ref)
            acckv_ref[...] = jnp.zeros_like(acckv_ref)

        tabs = _rope_tables(pos_ref, invf_ref)
        cqh, rq, cqn = _rms_small(zs_ref[:, 0:Q_LORA], qg_ref[...])
        ckvh, rkv, ckvn = _rms_small(zs_ref[:, Q_LORA:Q_LORA + KV_LORA], kvg_ref[...])
        dkr = jnp.zeros((tm, LANES), F32)
        for h in range(MLA_HEADS):
            b0 = h * HEAD_PAD
            dqp_ref[:, b0:b0 + QK_NOPE] = dq_ref[:, b0:b0 + QK_NOPE]
            dqp_ref[:, b0 + QK_NOPE:b0 + HEAD_PAD] = _rope_t(
                dq_ref[:, b0 + QK_NOPE:b0 + HEAD_PAD].astype(F32), tabs).astype(BF16)
            dkv_ref[:, b0:b0 + QK_NOPE] = dk_ref[:, b0:b0 + QK_NOPE]
            dkv_ref[:, b0 + QK_NOPE:b0 + HEAD_PAD] = dv_ref[:, h * V_HEAD:(h + 1) * V_HEAD]
            dkr = dkr + dk_ref[:, b0 + QK_NOPE:b0 + HEAD_PAD].astype(F32)
        accq_ref[...] += _dot_tn(dqp_ref[...], cqn.astype(BF16))
        acckv_ref[...] += _dot_tn(ckvn.astype(BF16), dkv_ref[...])

        @pl.when(pl.program_id(0) == n_t - 1)
        def _():
            dwq_ref[...] = accq_ref[...].astype(BF16)
            dwkv_ref[...] = acckv_ref[...].astype(BF16)

        dcqn = _dot(dqp_ref[...], wq_ref[...])
        dckvn = _dot_nt(dkv_ref[...], wkv_ref[...])
        dqg_ref[...] += jnp.sum(dcqn * cqh, axis=0, keepdims=True)
        dkvg_ref[...] += jnp.sum(dckvn * ckvh, axis=0, keepdims=True)
        dxh = dcqn * qg_ref[...]
        dzs_ref[:, 0:Q_LORA] = (rq * (dxh - cqh * jnp.mean(dxh * cqh, axis=-1, keepdims=True))).astype(BF16)
        dxh = dckvn * kvg_ref[...]
        dzs_ref[:, Q_LORA:Q_LORA + KV_LORA] = (
            rkv * (dxh - ckvh * jnp.mean(dxh * ckvh, axis=-1, keepdims=True))).astype(BF16)
        dzs_ref[:, Q_LORA + KV_LORA:ZS_W] = _rope_t(dkr, tabs).astype(BF16)

    full = lambda a: pl.BlockSpec(a.shape, lambda i: (0,) * a.ndim)
    rowb = lambda w: pl.BlockSpec((tm, w), lambda i: (i, 0))
    return pl.pallas_call(
        body, grid=(T // tm,),
        in_specs=[rowb(ZS_W), rowb(1), full(invf), full(qg), full(kvg), full(wuq_p), full(wukv),
                  rowb(HW), rowb(HW), rowb(D_MODEL)],
        out_specs=[rowb(ZS_W), full(wuq_p), full(wukv), full(qg), full(kvg)],
        out_shape=[jax.ShapeDtypeStruct((T, ZS_W), BF16), jax.ShapeDtypeStruct(wuq_p.shape, BF16),
                   jax.ShapeDtypeStruct(wukv.shape, BF16), jax.ShapeDtypeStruct(qg.shape, F32),
                   jax.ShapeDtypeStruct(kvg.shape, F32)],
        scratch_shapes=[pltpu.VMEM((tm, HW), BF16), pltpu.VMEM((tm, HW), BF16),
                        pltpu.VMEM(wuq_p.shape, F32), pltpu.VMEM(wukv.shape, F32)],
        name="mla_prep_bwd", compiler_params=_params("arbitrary"))(
            zs, pos, invf, qg, kvg, wuq_p, wukv, dq, dk, dv)


def _causal(tq, kmax, q0):
    r = lax.broadcasted_iota(jnp.int32, (tq, kmax), 0) + q0
    c = lax.broadcasted_iota(jnp.int32, (tq, kmax), 1)
    return c <= r


def _attn_fwd(q, k, v, batch, seq):
    tq = _tile(seq, ATTN_TILE)
    nq = seq // tq

    def body(q_ref, k_ref, v_ref, o_ref, lse_ref):
        diag = _causal(tq, tq, 0)
        for qi in range(nq):
            rows = slice(qi * tq, (qi + 1) * tq)
            qr = q_ref[rows, :]
            s_d = jnp.where(diag, _dot_nt(qr, k_ref[rows, :]), NEG_BIG)
            m = jnp.max(s_d, axis=-1, keepdims=True)
            if qi > 0:
                before = slice(0, qi * tq)
                s_b = _dot_nt(qr, k_ref[before, :])
                m = jnp.maximum(m, jnp.max(s_b, axis=-1, keepdims=True))
                p_b = jnp.exp(s_b - m)
                l = jnp.sum(p_b, axis=-1, keepdims=True)
                acc = _dot(p_b.astype(BF16), v_ref[before, :])
            p_d = jnp.exp(s_d - m)
            l_d = jnp.sum(p_d, axis=-1, keepdims=True)
            acc_d = _dot(p_d.astype(BF16), v_ref[rows, :])
            l, acc = (l + l_d, acc + acc_d) if qi > 0 else (l_d, acc_d)
            o_ref[rows, :] = (acc / l).astype(KEPT)
            lse_ref[rows, :] = jnp.broadcast_to(m + jnp.log(l), (tq, V_HEAD))

    return pl.pallas_call(
        body, grid=(batch, MLA_HEADS),
        in_specs=[pl.BlockSpec((seq, HEAD_PAD), lambda b, h: (b, h)),
                  pl.BlockSpec((seq, HEAD_PAD), lambda b, h: (b, h)),
                  pl.BlockSpec((seq, V_HEAD), lambda b, h: (b, h))],
        out_specs=[pl.BlockSpec((seq, V_HEAD), lambda b, h: (b, h)),
                   pl.BlockSpec((seq, V_HEAD), lambda b, h: (b, h))],
        out_shape=[jax.ShapeDtypeStruct((batch * seq, D_MODEL), KEPT),
                   jax.ShapeDtypeStruct((batch * seq, D_MODEL), F32)],
        name="attn_fwd", compiler_params=_params("parallel", "parallel"))(q, k, v)


def _attn_bwd(q, k, v, o, do, lse, batch, seq, dep):
    tq = _tile(seq, ATTN_TILE)
    nq = seq // tq

    def body(q_ref, k_ref, v_ref, o_ref, do_ref, lse_ref, dep_ref, dq_ref, dk_ref, dv_ref, dk_acc, dv_acc):
        dk_acc[...] = jnp.zeros_like(dk_acc)
        dv_acc[...] = jnp.zeros_like(dv_acc)
        for qi in range(nq):
            rows = slice(qi * tq, (qi + 1) * tq)
            kmax = (qi + 1) * tq
            qr = q_ref[rows, :]
            dor = do_ref[rows, :]
            kk = k_ref[0:kmax, :]
            s = _dot_nt(qr, kk)
            p = jnp.where(_causal(tq, kmax, qi * tq), jnp.exp(s - lse_ref[rows, 0:1]), 0.0)
            dp = _dot_nt(dor, v_ref[0:kmax, :])
            delta = jnp.sum(dor.astype(F32) * o_ref[rows, :].astype(F32), axis=-1, keepdims=True)
            ds = (p * (dp - delta)).astype(BF16)
            dq_ref[rows, :] = (_dot(ds, kk) * ATTN_SCALE).astype(BF16)
            dk_acc[0:kmax, :] += _dot_tn(ds, qr)
            dv_acc[0:kmax, :] += _dot_tn(p.astype(BF16), dor)
        dk_ref[...] = dk_acc[...].astype(BF16)
        dv_ref[...] = dv_acc[...].astype(BF16)

    qspec = pl.BlockSpec((seq, HEAD_PAD), lambda b, h: (b, h))
    vspec = pl.BlockSpec((seq, V_HEAD), lambda b, h: (b, h))
    T = batch * seq
    return pl.pallas_call(
        body, grid=(batch, MLA_HEADS),
        in_specs=[qspec, qspec, vspec, vspec, vspec, vspec, ANY],
        out_specs=[qspec, qspec, vspec],
        out_shape=[jax.ShapeDtypeStruct((T, MLA_HEADS * HEAD_PAD), BF16),
                   jax.ShapeDtypeStruct((T, MLA_HEADS * HEAD_PAD), BF16),
                   jax.ShapeDtypeStruct((T, D_MODEL), BF16)],
        scratch_shapes=[pltpu.VMEM((seq, HEAD_PAD), F32), pltpu.VMEM((seq, V_HEAD), F32)],
        name="attn_bwd", compiler_params=_params("parallel", "parallel"))(q, k, v, o, do, lse, dep)


def _merge_out(x, yag, zm, o, w_out, ffn_g):
    T = x.shape[0]
    tm = _tile(T, 512)

    def body(x_ref, ya_ref, gb_ref, o_ref, w_ref, g_ref, mg_ref, x1_ref, h2_ref):
        mg = (ya_ref[...].astype(F32) + _sigmoid(gb_ref[...].astype(F32)) * o_ref[...].astype(F32)).astype(BF16)
        mg_ref[...] = mg
        x1 = x_ref[...] + _dot(mg, w_ref[...])
        x1_ref[...] = x1
        r = lax.rsqrt(jnp.mean(x1 * x1, axis=-1, keepdims=True) + EPS)
        h2_ref[...] = (x1 * r * g_ref[...]).astype(BF16)

    row = pl.BlockSpec((tm, D_MODEL), lambda i: (i, 0))
    return pl.pallas_call(
        body, grid=(T // tm,),
        in_specs=[row, row, pl.BlockSpec((tm, D_MODEL), lambda i: (i, 3)), row,
                  pl.BlockSpec((D_MODEL, D_MODEL), lambda i: (0, 0)), pl.BlockSpec((1, D_MODEL), lambda i: (0, 0))],
        out_specs=[row, row, row],
        out_shape=[jax.ShapeDtypeStruct((T, D_MODEL), BF16), jax.ShapeDtypeStruct((T, D_MODEL), F32),
                   jax.ShapeDtypeStruct((T, D_MODEL), BF16)],
        name="merge_out", compiler_params=_params("parallel"))(x, yag, zm, o, w_out, ffn_g)


FF_TILE = 256
FF_BLOCKS = D_FF // FF_TILE
FFB_TILE = 256


def _shift_down(x, k):
    row = lax.broadcasted_iota(jnp.int32, x.shape, 0)
    return jnp.where(row >= k, pltpu.roll(x, k, 0), 0.0)


def _shift_up(x, k):
    n = x.shape[0]
    row = lax.broadcasted_iota(jnp.int32, x.shape, 0)
    return jnp.where(row < n - k, pltpu.roll(x, n - k, 0), 0.0)


def _conv(x, w_ref, b_ref):
    return b_ref[...] + w_ref[2:3, :] * x + w_ref[1:2, :] * _shift_down(x, 1) + w_ref[0:1, :] * _shift_down(x, 2)


EDGE = 16


def _conv_wrapped(x, w_ref, b_ref):
    return (b_ref[...] + w_ref[2:3, :] * x + w_ref[1:2, :] * pltpu.roll(x, 1, 0)
            + w_ref[0:1, :] * pltpu.roll(x, 2, 0))


def _up_act(h2, wt_up, cw, cb, batch, seq):
    def body(h_ref, wug_ref, wuv_ref, wg_ref, wv_ref, bg_ref, bv_ref, ug_ref, uv_ref, g_ref, v_ref, a_ref):
        h = h_ref[...]
        ug = _dot_nt(h, wug_ref[...])
        uv = _dot_nt(h, wuv_ref[...])
        ug_ref[...] = ug.astype(KEPT)
        uv_ref[...] = uv.astype(KEPT)

        def write(rows, gate, val):
            g_ref[rows, :] = gate.astype(KEPT)
            v_ref[rows, :] = val.astype(KEPT)
            a_ref[rows, :] = (gate * _sigmoid(gate) * val).astype(BF16)

        write(slice(None), _conv_wrapped(ug, wg_ref, bg_ref), _conv_wrapped(uv, wv_ref, bv_ref))
        write(slice(0, EDGE), _conv(ug[0:EDGE], wg_ref, bg_ref), _conv(uv[0:EDGE], wv_ref, bv_ref))

    blk = pl.BlockSpec((seq, FF_TILE), lambda b, j: (b, j))
    wup = lambda off: pl.BlockSpec((FF_TILE, D_MODEL), lambda b, j: (j + off, 0))
    wsp = lambda off: pl.BlockSpec((3, FF_TILE), lambda b, j: (0, j + off))
    bsp = lambda off: pl.BlockSpec((1, FF_TILE), lambda b, j: (0, j + off))
    T = batch * seq
    kept = jax.ShapeDtypeStruct((T, D_FF), KEPT)
    return pl.pallas_call(
        body, grid=(batch, FF_BLOCKS),
        in_specs=[pl.BlockSpec((seq, D_MODEL), lambda b, j: (b, 0)), wup(0), wup(FF_BLOCKS),
                  wsp(0), wsp(FF_BLOCKS), bsp(0), bsp(FF_BLOCKS)],
        out_specs=[blk] * 5,
        out_shape=[kept, kept, kept, kept, jax.ShapeDtypeStruct((T, D_FF), BF16)],
        name="up_act", compiler_params=_params("parallel", "arbitrary"))(h2, wt_up, wt_up, cw, cw, cb, cb)


def _ffn_act_bwd(upg, upv, gate, val, cw, dx2b, w_down, batch, seq):
    def half(du, x, w_ref, dx_ref, dw_ref):
        j = pl.program_id(1)
        n = du.shape[0]
        up1, up2 = pltpu.roll(du, n - 1, 0), pltpu.roll(du, n - 2, 0)
        dx_ref[...] = (w_ref[2:3, :] * du + w_ref[1:2, :] * up1 + w_ref[0:1, :] * up2).astype(BF16)
        tail = du[n - EDGE:n]
        dx_ref[n - EDGE:n, :] = (w_ref[2:3, :] * tail + w_ref[1:2, :] * _shift_up(tail, 1)
                                 + w_ref[0:1, :] * _shift_up(tail, 2)).astype(BF16)
        row = lax.broadcasted_iota(jnp.int32, (EDGE, du.shape[1]), 0)
        head, x_tail = du[0:EDGE], x[n - EDGE:n]
        wrap1 = jnp.sum(jnp.where(row >= EDGE - 1, pltpu.roll(head, EDGE - 1, 0), 0.0) * x_tail, axis=0, keepdims=True)
        wrap2 = jnp.sum(jnp.where(row >= EDGE - 2, pltpu.roll(head, EDGE - 2, 0), 0.0) * x_tail, axis=0, keepdims=True)
        dw_ref[j, 2:3, :] += jnp.sum(du * x, axis=0, keepdims=True)
        dw_ref[j, 1:2, :] += jnp.sum(up1 * x, axis=0, keepdims=True) - wrap1
        dw_ref[j, 0:1, :] += jnp.sum(up2 * x, axis=0, keepdims=True) - wrap2
        dw_ref[j, 3:4, :] += jnp.sum(du, axis=0, keepdims=True)

    def body(ug_ref, uv_ref, g_ref, v_ref, wg_ref, wv_ref, dx_ref, wd_ref, dg_ref, dv_ref, dwg_ref, dwv_ref):
        @pl.when((pl.program_id(0) == 0) & (pl.program_id(1) == 0))
        def _():
            dwg_ref[...] = jnp.zeros_like(dwg_ref)
            dwv_ref[...] = jnp.zeros_like(dwv_ref)

        gate, val = g_ref[...].astype(F32), v_ref[...].astype(F32)
        sg = _sigmoid(gate)
        dav = _dot_nt(dx_ref[...], wd_ref[...])
        half(dav * val * sg * (1.0 + gate * (1.0 - sg)), ug_ref[...].astype(F32), wg_ref, dg_ref, dwg_ref)
        half(dav * gate * sg, uv_ref[...].astype(F32), wv_ref, dv_ref, dwv_ref)

    nb = D_FF // FFB_TILE
    blk = pl.BlockSpec((seq, FFB_TILE), lambda b, j: (b, j))
    wsp = lambda off: pl.BlockSpec((3, FFB_TILE), lambda b, j: (0, j + off))
    acc = pl.BlockSpec((nb, 4, FFB_TILE), lambda b, j: (0, 0, 0))
    T = batch * seq
    dupg, dupv, dwg, dwv = pl.pallas_call(
        body, grid=(batch, nb),
        in_specs=[blk, blk, blk, blk, wsp(0), wsp(nb),
                  pl.BlockSpec((seq, D_MODEL), lambda b, j: (b, 0)),
                  pl.BlockSpec((FFB_TILE, D_MODEL), lambda b, j: (j, 0))],
        out_specs=[blk, blk, acc, acc],
        out_shape=[jax.ShapeDtypeStruct((T, D_FF), BF16), jax.ShapeDtypeStruct((T, D_FF), BF16),
                   jax.ShapeDtypeStruct((nb, 4, FFB_TILE), F32), jax.ShapeDtypeStruct((nb, 4, FFB_TILE), F32)],
        name="ffn_act_bwd", compiler_params=_params("arbitrary", "arbitrary"))(
            upg, upv, gate, val, cw, cw, dx2b, w_down)
    dwg, dwv = (jnp.transpose(a, (1, 0, 2)).reshape(4, D_FF) for a in (dwg, dwv))
    return dupg, dupv, dwg[:3], dwv[:3], dwg[3:], dwv[3:]


def _down_loss(a, w_down, x1, target, gfin):
    T = x1.shape[0]
    tm = _tile(T, 512)

    def body(a_ref, w_ref, x1_ref, t_ref, g_ref, dx_ref, dxb_ref, loss_ref, dg_ref):
        @pl.when(pl.program_id(0) == 0)
        def _():
            loss_ref[...] = jnp.zeros_like(loss_ref)
            dg_ref[...] = jnp.zeros_like(dg_ref)

        x2 = x1_ref[...] + _dot(a_ref[...], w_ref[...])
        r = lax.rsqrt(jnp.mean(x2 * x2, axis=-1, keepdims=True) + EPS)
        xh = x2 * r
        g = g_ref[...]
        diff = xh * g - t_ref[...]
        loss_ref[...] += 0.5 * jnp.sum(jnp.mean(diff * diff, axis=-1, keepdims=True))
        dy = diff * (1.0 / D_MODEL)
        dg_ref[...] += jnp.sum(dy * xh, axis=0, keepdims=True)
        dxh = dy * g
        dx = r * (dxh - xh * jnp.mean(dxh * xh, axis=-1, keepdims=True))
        dx_ref[...] = dx
        dxb_ref[...] = dx.astype(BF16)

    row = pl.BlockSpec((tm, D_MODEL), lambda i: (i, 0))
    vec = pl.BlockSpec((1, D_MODEL), lambda i: (0, 0))
    return pl.pallas_call(
        body, grid=(T // tm,),
        in_specs=[pl.BlockSpec((tm, D_FF), lambda i: (i, 0)),
                  pl.BlockSpec((D_FF, D_MODEL), lambda i: (0, 0)), row, row, vec],
        out_specs=[row, row, pl.BlockSpec((8, LANES), lambda i: (0, 0)), vec],
        out_shape=[jax.ShapeDtypeStruct((T, D_MODEL), F32), jax.ShapeDtypeStruct((T, D_MODEL), BF16),
                   jax.ShapeDtypeStruct((8, LANES), F32), jax.ShapeDtypeStruct((1, D_MODEL), F32)],
        name="down_loss", compiler_params=_params("arbitrary"))(a, w_down, x1, target, gfin)


def _local_step(x, positions, target, mix_norm, av_g, av_b, w_s, b_s, q_norm, kv_norm, ffn_norm, conv_b,
                final_norm, comm):
    batch, seq, _ = x.shape
    T = batch * seq
    x = x.reshape(T, D_MODEL)
    target = target.reshape(T, D_MODEL)
    pos = positions.reshape(T, 1)
    half = jnp.arange(0, QK_ROPE, 2, dtype=F32) / QK_ROPE
    inv_freq = 1.0 / (ROPE_THETA ** half)
    invf = jnp.concatenate([inv_freq, inv_freq, jnp.zeros((LANES - QK_ROPE,), F32)]).reshape(1, LANES)
    w_st = jnp.swapaxes(w_s, 1, 2)
    b_col = b_s.reshape(A_GROUPS, CHUNK, 1)

    wt_in = comm.in_weights()
    h, zm, zs = _in_proj(x, mix_norm, wt_in)
    yag = _mixer_a_fwd(zm, av_g, av_b, w_s, b_col)
    wuq_p, wukv, w_out = comm.mla_weights(after=yag)
    q, k, v = _mla_prep_fwd(zs, pos, invf, q_norm, kv_norm, wuq_p, wukv)
    o, lse = _attn_fwd(q, k, v, batch, seq)
    merged, x1, h2 = _merge_out(x, yag, zm, o, w_out, ffn_norm)
    wt_up, conv_w, w_down = comm.ffn_weights(after=merged)
    upg, upv, gate, val, act = _up_act(h2, wt_up, conv_w, conv_b, batch, seq)
    dx2, dx2b, loss_acc, d_final = _down_loss(act, w_down, x1, target, final_norm)

    d_wdown = _mm_tn(act, dx2b, "dw_down")
    dupg, dupv, dcwg, dcwv, dcbg, dcbv = _ffn_act_bwd(upg, upv, gate, val, conv_w, dx2b, w_down, batch, seq)
    d_wt_up = _mm_tn(dupv, h2, "dw_up_val", rows=2 * D_FF, row0=D_FF,
                     into=_mm_tn(dupg, h2, "dw_up_gate", rows=2 * D_FF))
    dx1, d_ffn_norm, dmerged = _proj_bwd(
        [dupg, dupv], wt_up, [(0, (0, D_FF), (0, D_FF)), (1, (0, D_FF), (D_FF, 2 * D_FF))],
        x1, ffn_norm, dx2, "up_proj_bwd", w2=w_out)
    token = comm.send_ffn_grads(d_wdown, d_wt_up, jnp.concatenate([dcwg, dcwv], axis=1))
    dzm, do, d_avg, d_avb, d_ws, d_bs = _mixer_bwd(zm, o, dmerged, av_g, av_b, w_s, w_st, b_col, token)
    token = comm.send_small_grads([
        d_avg, d_avb, _small_2d(d_ws), d_bs.reshape(A_GROUPS, CHUNK), d_ffn_norm,
        jnp.concatenate([dcbg, dcbv], axis=1), d_final])
    dq, dk, dv = _attn_bwd(q, k, v, o, do, lse, batch, seq, token)
    dzs, d_wuq_p, d_wukv, d_qn, d_kvn = _mla_prep_bwd(zs, pos, invf, q_norm, kv_norm, wuq_p, wukv, dq, dk, dv)
    d_wt_main = _mm_tn(dzm, h, "dw_in_main")
    d_wt_zs = _mm_tn(dzs, h, "dw_in_small")
    token = comm.send_in_grads(d_wt_main, d_wt_zs)
    d_wout = _mm_tn(merged, dx1, "dw_out", dep=token)
    terms = [(0, (i * D_MODEL, (i + 1) * D_MODEL), rows) for i, rows in enumerate(IN_ROWS_MAIN)]
    terms.append((1, (0, ZS_W), IN_ROWS_ZS))
    dx, d_mix_norm = _proj_bwd([dzm, dzs], wt_in, terms, x, mix_norm, dx1, "in_proj_bwd", dep=token, rows=512)
    token = comm.send_late_grads(d_wuq_p, d_wukv, d_wout, [d_qn, d_kvn, d_mix_norm, loss_acc])
    return dx.reshape(batch, seq, D_MODEL), token


MESH_ID = pl.DeviceIdType.MESH
EFFECT = pltpu.SideEffectType.DATAFLOW_SIDE_EFFECTING


def _mesh_pos():
    return lax.axis_index("x"), lax.axis_index("y"), lax.axis_index("c")


def _peer(pos, d):
    x, y, c = pos
    px = 1 - x if d & 4 else x
    py = 1 - y if d & 2 else y
    pc = 1 - c if d & 1 else c
    return (px, py, pc), 4 * px + 2 * py + pc


def _copy(src_ref, land_ref, send_sems, recv_sems, a, d, pos, exchange, landing_here):
    peer, pid = _peer(pos, d)
    me = 4 * pos[0] + 2 * pos[1] + pos[2]
    if exchange:
        src, dst = src_ref.at[pid], land_ref.at[d]
    else:
        src, dst = src_ref, land_ref.at[pid if landing_here else me]
    return pltpu.make_async_remote_copy(
        src_ref=src, dst_ref=dst, send_sem=send_sems.at[a * (N_DEV - 1) + d - 1],
        recv_sem=recv_sems.at[a * (N_DEV - 1) + d - 1],
        device_id=peer, device_id_type=MESH_ID)


def _start_copies(groups, modes, name, dep=None):
    sizes = [len(g) for g in groups]
    srcs = [s for g in groups for s in g]
    lands = [lax.empty(s.shape if modes[gi] else (N_DEV,) + s.shape, s.dtype)
             for gi, g in enumerate(groups) for s in g]
    n, ng = len(srcs), len(groups)
    n_in = 2 * n + (dep is not None)

    def body(*refs):
        src_refs, land_refs = refs[:n], refs[n:2 * n]
        sems = refs[n_in:n_in + 3 * ng]
        token = refs[-1]
        pos = _mesh_pos()
        k = 0
        for gi, size in enumerate(sizes):
            for a in range(size):
                _own_copy(src_refs[k], land_refs[k], sems[3 * gi + 2], a, pos, modes[gi]).start()
                for d in range(1, N_DEV):
                    _copy(src_refs[k], land_refs[k], sems[3 * gi], sems[3 * gi + 1], a, d, pos, modes[gi],
                          landing_here=False).start()
                k += 1
        token[...] = jnp.zeros_like(token)

    sem_shapes = []
    for size in sizes:
        remote = pltpu.SemaphoreType.DMA((size * (N_DEV - 1),))
        sem_shapes += [remote, remote, pltpu.SemaphoreType.DMA((size,))]
    out = pl.pallas_call(
        body, name=name,
        out_shape=(*sem_shapes, *[pltpu.HBM(a.shape, a.dtype) for a in srcs + lands],
                   jax.ShapeDtypeStruct((8, LANES), F32)),
        in_specs=[HBM] * (2 * n) + [ANY] * (dep is not None),
        out_specs=(*[SEM] * (3 * ng), *[HBM] * (2 * n), pl.BlockSpec(memory_space=pltpu.VMEM)),
        input_output_aliases={i: 3 * ng + i for i in range(2 * n)},
        compiler_params=pltpu.CompilerParams(has_side_effects=EFFECT),
    )(*[pltpu.with_memory_space_constraint(a, pltpu.HBM) for a in srcs + lands], *([dep] if dep is not None else []))
    thru = out[3 * ng:3 * ng + 2 * n]
    handles, k = [], 0
    for gi, size in enumerate(sizes):
        handles.append((out[3 * gi:3 * gi + 3], thru[k:k + size], thru[n + k:n + k + size]))
        k += size
    return handles, out[-1]


def _own_copy(src_ref, land_ref, local_sems, a, pos, exchange):
    me = 4 * pos[0] + 2 * pos[1] + pos[2]
    src, dst = (src_ref.at[me], land_ref.at[0]) if exchange else (src_ref, land_ref.at[me])
    return pltpu.make_async_copy(src, dst, local_sems.at[a])


def _wait_copies(handle, exchange, after, name):
    sems, srcs, lands = handle
    n = len(srcs)

    def body(*refs):
        src_refs, land_refs = refs[:n], refs[n:2 * n]
        send, recv, local = refs[2 * n:2 * n + 3]
        pos = _mesh_pos()
        for a in range(n):
            _own_copy(src_refs[a], land_refs[a], local, a, pos, exchange).wait()
            for d in range(1, N_DEV):
                cp = _copy(src_refs[a], land_refs[a], send, recv, a, d, pos, exchange, landing_here=True)
                cp.wait_send()
                cp.wait_recv()

    out = pl.pallas_call(
        body, name=name,
        out_shape=tuple(pltpu.HBM(a.shape, a.dtype) for a in (*srcs, *lands)),
        in_specs=[HBM] * (2 * n) + [SEM, SEM, SEM, ANY], out_specs=[HBM] * (2 * n),
        input_output_aliases={i: i for i in range(2 * n)},
        compiler_params=pltpu.CompilerParams(has_side_effects=EFFECT),
    )(*srcs, *lands, *sems, after)
    return out[n:]


def _gather_now(a, name):
    def body(x_ref, out_ref, send_sems, recv_sems, local_sem):
        x, y, c = _mesh_pos()
        me, sibling = (x, y, c), (x, y, 1 - c)
        chips = [(1 - x, y), (x, 1 - y), (1 - x, 1 - y)]

        def slot(p):
            return out_ref.at[4 * p[0] + 2 * p[1] + p[2]]

        def copy(k, block, to, src=None):
            return pltpu.make_async_remote_copy(
                src_ref=slot(block) if src is None else src, dst_ref=slot(block), send_sem=send_sems.at[k],
                recv_sem=recv_sems.at[k], device_id=to, device_id_type=MESH_ID)

        mine = pltpu.make_async_copy(x_ref, slot(me), local_sem)
        mine.start()
        first = [copy(0, me, sibling, src=x_ref)]
        first += [copy(1 + j, me, (*chip, c), src=x_ref) for j, chip in enumerate(chips)]
        for cp in first:
            cp.start()
        passed = [copy(4 + j, (*chip, c), sibling) for j, chip in enumerate(chips)]
        for j, chip in enumerate(chips):
            copy(1 + j, (*chip, c), me).wait_recv()
            passed[j].start()
        copy(0, sibling, me).wait_recv()
        for j, chip in enumerate(chips):
            copy(4 + j, (*chip, 1 - c), me).wait_recv()
        for cp in first + passed:
            cp.wait_send()
        mine.wait()

    return pl.pallas_call(
        body, in_specs=[ANY], out_specs=ANY,
        out_shape=jax.ShapeDtypeStruct((N_DEV,) + a.shape, a.dtype),
        scratch_shapes=[pltpu.SemaphoreType.DMA((N_DEV - 1,)), pltpu.SemaphoreType.DMA((N_DEV - 1,)),
                        pltpu.SemaphoreType.DMA],
        name=name, compiler_params=pltpu.CompilerParams(has_side_effects=True))(a)


def _sum_parts(p_ref):
    g = p_ref[0].astype(F32)
    for k in range(1, N_DEV):
        g = g + p_ref[k].astype(F32)
    return g


def _adamw_update(p_ref, w_ref, m_ref, v_ref, g_ref, d_ref, nm_ref, nv_ref):
    c1 = 1.0 - ADAM_B1 ** ADAM_STEP
    c2 = 1.0 - ADAM_B2 ** ADAM_STEP
    g = _sum_parts(p_ref)
    nm = ADAM_B1 * m_ref[...] + (1.0 - ADAM_B1) * g
    nv = ADAM_B2 * v_ref[...] + (1.0 - ADAM_B2) * (g * g)
    g_ref[...] = g
    nm_ref[...] = nm
    nv_ref[...] = nv
    d_ref[...] = -ADAM_LR * ((nm / c1) / (jnp.sqrt(nv / c2) + ADAM_EPS) + ADAM_WD * w_ref[...])


def _adamw_many(parts, ws, ms, vs, sums, name):
    n, ns = len(ws), len(sums)

    def body(*refs):
        ins, outs = refs[:4 * n + ns], refs[4 * n + ns:]
        for i in range(n):
            _adamw_update(ins[i], ins[n + i], ins[2 * n + i], ins[3 * n + i],
                          outs[i], outs[n + i], outs[2 * n + i], outs[3 * n + i])
        for i in range(ns):
            outs[4 * n + i][...] = _sum_parts(ins[4 * n + i])

    full = lambda a: pl.BlockSpec(a.shape, lambda: (0,) * a.ndim)
    args = [*parts, *ws, *ms, *vs, *sums]
    outs = [jax.ShapeDtypeStruct(w.shape, F32) for _ in range(4) for w in ws]
    outs += [jax.ShapeDtypeStruct(s.shape[1:], F32) for s in sums]
    res = pl.pallas_call(
        body, in_specs=[full(a) for a in args], out_specs=[full(o) for o in outs], out_shape=outs,
        name=name, compiler_params=pltpu.CompilerParams(vmem_limit_bytes=VMEM_LIMIT))(*args)
    return res[:n], res[n:2 * n], res[2 * n:3 * n], res[3 * n:4 * n], res[4 * n:]


def _adamw(parts, w, m, v, name):
    R, C = w.shape
    tr, tc = R, C
    if N_DEV * R * C * parts.dtype.itemsize > SMALL_BLOCK_BYTES:
        tr = next((t for t in range(min(R, 256) // 16 * 16, 15, -16) if R % t == 0), R)
        if tr == R:
            tc = _tile(C, 256)

    def body(p_ref, w_ref, m_ref, v_ref, g_ref, d_ref, nm_ref, nv_ref):
        _adamw_update(p_ref, w_ref, m_ref, v_ref, g_ref, d_ref, nm_ref, nv_ref)

    blk = pl.BlockSpec((tr, tc), lambda i, j: (i, j))
    shp = jax.ShapeDtypeStruct((R, C), F32)
    return pl.pallas_call(
        body, grid=(R // tr, C // tc),
        in_specs=[pl.BlockSpec((N_DEV, tr, tc), lambda i, j: (0, i, j)), blk, blk, blk],
        out_specs=[blk, blk, blk, blk], out_shape=[shp, shp, shp, shp],
        name=name, compiler_params=_params("parallel", "parallel"))(parts, w, m, v)


SPLIT_V = 2 * D_MODEL
SPLIT_KR = SPLIT_V + Q_LORA + KV_LORA + QK_ROPE
IN_DIM = SPLIT_KR + 2 * D_MODEL
IN_ROWS_MAIN = ((0, D_MODEL), (D_MODEL, SPLIT_V), (SPLIT_KR, SPLIT_KR + D_MODEL), (SPLIT_KR + D_MODEL, IN_DIM))
IN_ROWS_ZS = (SPLIT_V, SPLIT_V + ZS_W)

SMALL_EARLY = ("a_v_norm_g", "a_v_norm_b", "a_spatial_w", "a_spatial_b", "ffn_norm", "conv_b", "final_norm")
SMALL_LATE = ("q_a_norm", "kv_a_norm", "mix_norm")


def _small_2d(a):
    return a.reshape(-1, a.shape[-1])


def _cols_from_shards(g):
    return jnp.transpose(g, (1, 0, 2)).reshape(g.shape[1], N_DEV * g.shape[2])


def _shards_from_cols(a):
    R, W = a.shape
    return jnp.transpose(a.reshape(R, N_DEV, W // N_DEV), (1, 0, 2))


class _Comm:
    GATHER_GROUPS = (("w_uq", "w_ukv", "w_out"), ("w_up", "conv_w", "w_down"))
    FFN_GRADS = ("w_down", "w_up", "conv_w")
    LATE_GRADS = ("w_uq", "w_ukv", "w_out")
    TRANSPOSED = ("w_in", "w_up", "w_uq")

    def __init__(self, shards):
        local = {n: a.astype(F32 if n == "conv_w" else BF16) for n, a in shards.items()}
        self.g_in = _gather_now(local["w_in"], "gather_w_in")
        groups = [[local[n] for n in g] for g in self.GATHER_GROUPS]
        (self.h_mla, self.h_ffn), _ = _start_copies(groups, [False] * 2, "gather_start", dep=self.g_in)

    def in_weights(self):
        return self.g_in.reshape(IN_DIM, D_MODEL)

    def mla_weights(self, after):
        g_uq, g_ukv, g_out = _wait_copies(self.h_mla, False, after, "gather_wait_mla")
        wuq_p = jnp.pad(g_uq, ((0, 0), (0, HEAD_PAD - QK_HEAD), (0, 0))).reshape(MLA_HEADS * HEAD_PAD, Q_LORA)
        return wuq_p, _cols_from_shards(g_ukv), g_out.reshape(D_MODEL, D_MODEL)

    def ffn_weights(self, after):
        g_up, g_cw, g_down = _wait_copies(self.h_ffn, False, after, "gather_wait_ffn")
        return g_up.reshape(2 * D_FF, D_MODEL), _cols_from_shards(g_cw), g_down.reshape(D_FF, D_MODEL)

    def send_ffn_grads(self, d_wdown, d_wt_up, d_convw):
        group = [d_wdown.reshape(N_DEV, D_FF // N_DEV, D_MODEL), d_wt_up.reshape(N_DEV, 2 * D_FF // N_DEV, D_MODEL),
                 _shards_from_cols(d_convw)]
        (self.h_ffn_grads,), token = _start_copies([group], [True], "ffn_grads_start")
        return token

    def send_small_grads(self, grads):
        (self.h_small_early,), token = _start_copies([grads], [False], "small_grads_start")
        return token

    def send_in_grads(self, d_wt_main, d_wt_zs):
        d_in = jnp.concatenate([d_wt_main[:SPLIT_V], d_wt_zs[:SPLIT_KR - SPLIT_V], d_wt_main[SPLIT_V:]], axis=0)
        blocks = d_in.reshape(N_DEV, IN_DIM // N_DEV, D_MODEL)
        (self.h_in_grads,), token = _start_copies([[blocks]], [True], "in_grads_start")
        return token

    def send_late_grads(self, d_wuq_p, d_wukv, d_wout, small):
        d_uq = d_wuq_p.reshape(MLA_HEADS, HEAD_PAD, Q_LORA)[:, :QK_HEAD, :]
        group = [d_uq, _shards_from_cols(d_wukv), d_wout.reshape(N_DEV, D_MODEL // N_DEV, D_MODEL)]
        (self.h_late_grads, self.h_late_small), token = _start_copies(
            [group, small], [True, False], "late_grads_start")
        return token


def kernel(x, positions, mix_norm, w_in, a_v_norm_g, a_v_norm_b, a_spatial_w, a_spatial_b, q_a_norm, w_uq, kv_a_norm, w_ukv, w_out, ffn_norm, w_up, conv_w, conv_b, w_down, final_norm, loss_target, m_mix_norm, m_w_in, m_a_v_norm_g, m_a_v_norm_b, m_a_spatial_w, m_a_spatial_b, m_q_a_norm, m_w_uq, m_kv_a_norm, m_w_ukv, m_w_out, m_ffn_norm, m_w_up, m_conv_w, m_conv_b, m_w_down, m_final_norm, v_mix_norm, v_w_in, v_a_v_norm_g, v_a_v_norm_b, v_a_spatial_w, v_a_spatial_b, v_q_a_norm, v_w_uq, v_kv_a_norm, v_w_ukv, v_w_out, v_ffn_norm, v_w_up, v_conv_w, v_conv_b, v_w_down, v_final_norm):
    names = ("mix_norm", "w_in", "a_v_norm_g", "a_v_norm_b", "a_spatial_w", "a_spatial_b", "q_a_norm", "w_uq",
             "kv_a_norm", "w_ukv", "w_out", "ffn_norm", "w_up", "conv_w", "conv_b", "w_down", "final_norm")
    w = dict(zip(names, (mix_norm, w_in, a_v_norm_g, a_v_norm_b, a_spatial_w, a_spatial_b, q_a_norm, w_uq,
                         kv_a_norm, w_ukv, w_out, ffn_norm, w_up, conv_w, conv_b, w_down, final_norm)))
    m = dict(zip(names, (m_mix_norm, m_w_in, m_a_v_norm_g, m_a_v_norm_b, m_a_spatial_w, m_a_spatial_b,
                         m_q_a_norm, m_w_uq, m_kv_a_norm, m_w_ukv, m_w_out, m_ffn_norm, m_w_up, m_conv_w,
                         m_conv_b, m_w_down, m_final_norm)))
    v = dict(zip(names, (v_mix_norm, v_w_in, v_a_v_norm_g, v_a_v_norm_b, v_a_spatial_w, v_a_spatial_b,
                         v_q_a_norm, v_w_uq, v_kv_a_norm, v_w_ukv, v_w_out, v_ffn_norm, v_w_up, v_conv_w,
                         v_conv_b, v_w_down, v_final_norm)))
    shapes = {n: w[n].shape for n in names}
    def view(tree, n):
        a = tree[n].reshape(tree[n].shape[-2:])
        return a.T if n in _Comm.TRANSPOSED else a

    comm = _Comm({n: view(w, n) for n in ("w_in",) + _Comm.GATHER_GROUPS[0] + _Comm.GATHER_GROUPS[1]})

    grad_x, token = _local_step(
        x, positions, loss_target, w["mix_norm"], w["a_v_norm_g"], w["a_v_norm_b"], w["a_spatial_w"][0],
        w["a_spatial_b"][0], w["q_a_norm"], w["kv_a_norm"], w["ffn_norm"], w["conv_b"],
        w["final_norm"].reshape(1, D_MODEL), comm)

    out_g, out_d, out_m, out_v = {}, {}, {}, {}

    def update(n, parts):
        res = _adamw(parts, view(w, n), view(m, n), view(v, n), "adamw_" + n)
        out_g[n], out_d[n], out_m[n], out_v[n] = (
            (t.T if n in _Comm.TRANSPOSED else t).reshape(shapes[n]) for t in res)
        return res[1]

    def update_small(names, parts, sums, name):
        res = _adamw_many(parts, *[[_small_2d(t[n]) for n in names] for t in (w, m, v)], sums, name)
        for i, n in enumerate(names):
            out_g[n], out_d[n], out_m[n], out_v[n] = (r[i].reshape(shapes[n]) for r in res[:4])
        return res

    for n, parts in zip(_Comm.FFN_GRADS, _wait_copies(comm.h_ffn_grads, True, token, "ffn_grads_wait")):
        last = update(n, parts)
    early = _wait_copies(comm.h_small_early, False, last, "small_grads_wait")
    last = update_small(SMALL_EARLY, early, [], "adamw_small")[1][0]
    last = update("w_in", _wait_copies(comm.h_in_grads, True, last, "in_grads_wait")[0])
    for n, parts in zip(_Comm.LATE_GRADS, _wait_copies(comm.h_late_grads, True, last, "late_grads_wait")):
        last = update(n, parts)
    late = _wait_copies(comm.h_late_small, False, last, "late_small_wait")
    res = update_small(SMALL_LATE, late[:-1], late[-1:], "adamw_late")
    loss = res[4][0][0, 0]

    return (loss, grad_x, *[out_g[n] for n in names], *[out_d[n] for n in names],
            *[out_m[n] for n in names], *[out_v[n] for n in names])
```

```python
import math

import jax
import jax.numpy as jnp
from jax import lax
from jax.experimental import pallas as pl
from jax.experimental.pallas import tpu as pltpu

F32 = jnp.float32
BF16 = jnp.bfloat16
KEPT = jnp.bfloat16

N_DEV = 8
D_MODEL = 1024
EPS = 1e-6
A_GROUPS = 8
CHUNK = 128
MLA_HEADS = 8
QK_NOPE = 128
QK_ROPE = 64
QK_HEAD = QK_NOPE + QK_ROPE
HEAD_PAD = 256
V_HEAD = 128
Q_LORA = 256
KV_LORA = 128
ROPE_THETA = 10000.0
D_FF = 2816
ZS_W = 512
ATTN_SCALE = QK_HEAD ** -0.5
ATTN_TILE = 512
NEG_BIG = -1e30

ADAM_LR = 0.001
ADAM_B1 = 0.9
ADAM_B2 = 0.999
ADAM_EPS = 1e-08
ADAM_WD = 0.01
ADAM_STEP = 10

VMEM_LIMIT = 56 * 1024 * 1024
SMALL_BLOCK_BYTES = 5 * 1024 * 1024
LANES = 128

GELU_K = math.sqrt(2.0 / math.pi)
GELU_C = 0.044715

ANY = pl.BlockSpec(memory_space=pl.ANY)
HBM = pl.BlockSpec(memory_space=pltpu.HBM)
SEM = pl.BlockSpec(memory_space=pltpu.SEMAPHORE)


def _tile(n, pref):
    for t in (pref, 512, 256, 128, 64, 32, 16, 8):
        if t <= pref and n % t == 0:
            return t
    return n


def _wide_tile(n, cap=1408):
    return next((t for t in range(min(n, cap) // LANES * LANES, 0, -LANES) if n % t == 0), n)


def _params(*sem):
    return pltpu.CompilerParams(dimension_semantics=sem, vmem_limit_bytes=VMEM_LIMIT)


def _dot(a, b):
    return jnp.dot(a, b, preferred_element_type=F32)


def _dot_nt(a, b):
    return lax.dot_general(a, b, (((1,), (1,)), ((), ())), preferred_element_type=F32)


def _dot_tn(a, b):
    return lax.dot_general(a, b, (((0,), (0,)), ((), ())), preferred_element_type=F32)


def _sigmoid(x):
    return 1.0 / (1.0 + jnp.exp(-x))


def _gelu(x):
    t = jnp.tanh(GELU_K * (x + GELU_C * x * x * x))
    return 0.5 * x * (1.0 + t)


def _gelu_and_grad(x):
    x2 = x * x
    t = jnp.tanh(GELU_K * (x + GELU_C * x * x2))
    half = 0.5 * (1.0 + t)
    return x * half, half + 0.5 * x * (1.0 - t * t) * GELU_K * (1.0 + 3.0 * GELU_C * x2)


def _in_proj(x, g, wt):
    T, Dm = x.shape
    tm = _tile(T, 512)

    def body(x_ref, g_ref, wt_ref, h_ref, zm_ref, zs_ref):
        xf = x_ref[...]
        r = lax.rsqrt(jnp.mean(xf * xf, axis=-1, keepdims=True) + EPS)
        h = (xf * r * g_ref[...]).astype(BF16)
        h_ref[...] = h
        for i, (r0, r1) in enumerate(IN_ROWS_MAIN):
            zm_ref[:, i * D_MODEL:(i + 1) * D_MODEL] = _dot_nt(h, wt_ref[r0:r1, :]).astype(KEPT)
        zs_ref[...] = _dot_nt(h, wt_ref[IN_ROWS_ZS[0]:IN_ROWS_ZS[1], :])

    row = lambda n: pl.BlockSpec((tm, n), lambda i: (i, 0))
    return pl.pallas_call(
        body, grid=(T // tm,),
        in_specs=[row(Dm), pl.BlockSpec((1, Dm), lambda i: (0, 0)), pl.BlockSpec(wt.shape, lambda i: (0, 0))],
        out_specs=[row(Dm), row(4 * D_MODEL), row(ZS_W)],
        out_shape=[jax.ShapeDtypeStruct((T, Dm), BF16), jax.ShapeDtypeStruct((T, 4 * D_MODEL), KEPT),
                   jax.ShapeDtypeStruct((T, ZS_W), F32)],
        name="in_proj", compiler_params=_params("parallel"))(x, g, wt)


def _proj_bwd(acts, wt, terms, x, g, dres, name, w2=None, dep=None, rows=256):
    T, Dm = x.shape
    tm = _tile(T, rows)
    n_a = len(acts)

    def body(*refs):
        ins, outs = refs[:n_a + 4 + (w2 is not None) + (dep is not None)], refs[-2 - (w2 is not None):]
        wt_ref, x_ref, g_ref, dres_ref = ins[n_a:n_a + 4]
        dx_ref, dg_ref = outs[0], outs[1]

        @pl.when(pl.program_id(0) == 0)
        def _():
            dg_ref[...] = jnp.zeros_like(dg_ref)

        dy = None
        for i, (c0, c1), (r0, r1) in terms:
            t = _dot(ins[i][:, c0:c1], wt_ref[r0:r1, :])
            dy = t if dy is None else dy + t
        xf = x_ref[...]
        r = lax.rsqrt(jnp.mean(xf * xf, axis=-1, keepdims=True) + EPS)
        xh = xf * r
        dg_ref[...] += jnp.sum(dy * xh, axis=0, keepdims=True)
        dxh = dy * g_ref[...]
        dx = dres_ref[...] + r * (dxh - xh * jnp.mean(dxh * xh, axis=-1, keepdims=True))
        dx_ref[...] = dx
        if w2 is not None:
            outs[2][...] = _dot_nt(dx.astype(BF16), ins[n_a + 4][...]).astype(KEPT)

    row = pl.BlockSpec((tm, Dm), lambda i: (i, 0))
    vec = pl.BlockSpec((1, Dm), lambda i: (0, 0))
    in_specs = [pl.BlockSpec((tm, a.shape[1]), lambda i: (i, 0)) for a in acts]
    in_specs += [pl.BlockSpec(wt.shape, lambda i: (0, 0)), row, vec, row]
    args = [*acts, wt, x, g, dres]
    out_specs = [row, vec]
    out_shape = [jax.ShapeDtypeStruct((T, Dm), F32), jax.ShapeDtypeStruct((1, Dm), F32)]
    if w2 is not None:
        in_specs.append(pl.BlockSpec(w2.shape, lambda i: (0, 0)))
        args.append(w2)
        out_specs.append(pl.BlockSpec((tm, w2.shape[0]), lambda i: (i, 0)))
        out_shape.append(jax.ShapeDtypeStruct((T, w2.shape[0]), KEPT))
    if dep is not None:
        in_specs.append(ANY)
        args.append(dep)
    return pl.pallas_call(
        body, grid=(T // tm,), in_specs=in_specs, out_specs=out_specs, out_shape=out_shape,
        name=name, compiler_params=_params("arbitrary"))(*args)


def _mm_tn(a, b, name, dep=None, rows=None, row0=0, into=None):
    T, M = a.shape
    N = b.shape[1]
    tm, tn, tt = _wide_tile(M), _wide_tile(N), _tile(T, 2048)
    n_t = T // tt
    off = row0 // tm
    extra = ([dep] if dep is not None else []) + ([into] if into is not None else [])

    def body(a_ref, b_ref, *refs):
        o_ref, acc_ref = refs[-2:]
        t = pl.program_id(2)

        @pl.when(t == 0)
        def _():
            acc_ref[...] = jnp.zeros_like(acc_ref)

        acc_ref[...] += _dot_tn(a_ref[...].astype(BF16), b_ref[...].astype(BF16))

        @pl.when(t == n_t - 1)
        def _():
            o_ref[...] = acc_ref[...].astype(BF16)

    return pl.pallas_call(
        body, grid=(M // tm, N // tn, n_t),
        in_specs=[pl.BlockSpec((tt, tm), lambda i, j, t: (t, i)),
                  pl.BlockSpec((tt, tn), lambda i, j, t: (t, j))] + [ANY] * len(extra),
        out_specs=pl.BlockSpec((tm, tn), lambda i, j, t: (i + off, j)),
        out_shape=jax.ShapeDtypeStruct((rows or M, N), BF16),
        scratch_shapes=[pltpu.VMEM((tm, tn), F32)],
        input_output_aliases={} if into is None else {1 + len(extra): 0},
        name=name, compiler_params=_params("parallel", "parallel", "arbitrary"))(a, b, *extra)


def _layer_norm_fwd(gv, g, b):
    mu = jnp.mean(gv, axis=-1, keepdims=True)
    xc = gv - mu
    rs = lax.rsqrt(jnp.mean(xc * xc, axis=-1, keepdims=True) + EPS)
    xh = xc * rs
    return xh, rs, xh * g + b


def _tri_mask(transposed=False):
    r = lax.broadcasted_iota(jnp.int32, (CHUNK, CHUNK), 0)
    c = lax.broadcasted_iota(jnp.int32, (CHUNK, CHUNK), 1)
    return r <= c if transposed else c <= r


def _mixer_a_fwd(zm, av_g, av_b, w_s, b_col):
    T = zm.shape[0]
    tm = _tile(T, 512)
    n_chunk = tm // CHUNK

    def body(u_ref, v_ref, ga_ref, g_ref, b_ref, w_ref, bc_ref, y_ref, vn_s, mx_s):
        gu = _gelu(u_ref[...].astype(F32))
        _, _, vn = _layer_norm_fwd(_gelu(v_ref[...].astype(F32)), g_ref[...], b_ref[...])
        vn_s[...] = vn.astype(BF16)
        tri = _tri_mask()
        for gi in range(A_GROUPS):
            wm = jnp.where(tri, w_ref[gi], 0.0).astype(BF16)
            cols = slice(gi * CHUNK, (gi + 1) * CHUNK)
            for n in range(n_chunk):
                rows = slice(n * CHUNK, (n + 1) * CHUNK)
                mx_s[rows, cols] = _dot(wm, vn_s[rows, cols]) + bc_ref[gi]
        y_ref[...] = (_sigmoid(ga_ref[...].astype(F32)) * gu * mx_s[...]).astype(KEPT)

    col = lambda c: pl.BlockSpec((tm, D_MODEL), lambda i: (i, c))
    vec = pl.BlockSpec((1, D_MODEL), lambda i: (0, 0))
    return pl.pallas_call(
        body, grid=(T // tm,),
        in_specs=[col(0), col(1), col(2), vec, vec,
                  pl.BlockSpec((A_GROUPS, CHUNK, CHUNK), lambda i: (0, 0, 0)),
                  pl.BlockSpec((A_GROUPS, CHUNK, 1), lambda i: (0, 0, 0))],
        out_specs=pl.BlockSpec((tm, D_MODEL), lambda i: (i, 0)),
        out_shape=jax.ShapeDtypeStruct((T, D_MODEL), KEPT),
        scratch_shapes=[pltpu.VMEM((tm, D_MODEL), BF16), pltpu.VMEM((tm, D_MODEL), F32)],
        name="mixer_a_fwd", compiler_params=_params("parallel"))(zm, zm, zm, av_g, av_b, w_s, b_col)


def _mixer_bwd(zm, o, dm, av_g, av_b, w_s, w_st, b_col, dep):
    T = zm.shape[0]
    tm = _tile(T, 256)
    n_chunk = tm // CHUNK

    def body(u_ref, v_ref, ga_ref, gb_ref, o_ref, dm_ref, g_ref, b_ref, w_ref, wt_ref, bc_ref, dep_ref,
             dz_ref, do_ref, dg_ref, db_ref, dw_ref, dbs_ref, vn_s, mx_s, dmx_s, dvn_s):
        @pl.when(pl.program_id(0) == 0)
        def _():
            dg_ref[...] = jnp.zeros_like(dg_ref)
            db_ref[...] = jnp.zeros_like(db_ref)
            dw_ref[...] = jnp.zeros_like(dw_ref)
            dbs_ref[...] = jnp.zeros_like(dbs_ref)

        dm_v = dm_ref[...].astype(F32)
        gb = gb_ref[...].astype(F32)
        sb = _sigmoid(gb)
        o_v = o_ref[...].astype(F32)
        do_ref[...] = (dm_v * sb).astype(BF16)
        dz_ref[:, 3 * D_MODEL:4 * D_MODEL] = (dm_v * o_v * sb * (1.0 - sb)).astype(BF16)
        u = u_ref[...].astype(F32)
        v = v_ref[...].astype(F32)
        gu, gu_grad = _gelu_and_grad(u)
        gv, gv_grad = _gelu_and_grad(v)
        xh, rs, vn = _layer_norm_fwd(gv, g_ref[...], b_ref[...])
        vn_s[...] = vn.astype(BF16)
        tri = _tri_mask()
        for gi in range(A_GROUPS):
            wm = jnp.where(tri, w_ref[gi], 0.0).astype(BF16)
            cols = slice(gi * CHUNK, (gi + 1) * CHUNK)
            for n in range(n_chunk):
                rows = slice(n * CHUNK, (n + 1) * CHUNK)
                mx_s[rows, cols] = _dot(wm, vn_s[rows, cols]) + bc_ref[gi]
        mixed = mx_s[...]
        sa = _sigmoid(ga_ref[...].astype(F32))
        dya = dm_v * sa
        dz_ref[:, 2 * D_MODEL:3 * D_MODEL] = (dm_v * gu * mixed * sa * (1.0 - sa)).astype(BF16)
        dz_ref[:, 0:D_MODEL] = (dya * mixed * gu_grad).astype(BF16)
        dmx = dya * gu
        dmx_s[...] = dmx.astype(BF16)
        tri_t = _tri_mask(transposed=True)
        for gi in range(A_GROUPS):
            wmt = jnp.where(tri_t, wt_ref[gi], 0.0).astype(BF16)
            cols = slice(gi * CHUNK, (gi + 1) * CHUNK)
            dw_acc = jnp.zeros((CHUNK, CHUNK), F32)
            dmx_sum = jnp.zeros((CHUNK, CHUNK), F32)
            for n in range(n_chunk):
                rows = slice(n * CHUNK, (n + 1) * CHUNK)
                blk = dmx_s[rows, cols]
                dvn_s[rows, cols] = _dot(wmt, blk)
                dw_acc = dw_acc + _dot_nt(blk, vn_s[rows, cols])
                dmx_sum = dmx_sum + dmx[rows, cols]
            dw_ref[gi] += jnp.where(tri, dw_acc, 0.0)
            dbs_ref[gi] += jnp.sum(dmx_sum, axis=-1, keepdims=True)
        dvn = dvn_s[...]
        dg_ref[...] += jnp.sum(dvn * xh, axis=0, keepdims=True)
        db_ref[...] += jnp.sum(dvn, axis=0, keepdims=True)
        dxh = dvn * g_ref[...]
        dgv = rs * (dxh - jnp.mean(dxh, axis=-1, keepdims=True)
                    - xh * jnp.mean(dxh * xh, axis=-1, keepdims=True))
        dz_ref[:, D_MODEL:2 * D_MODEL] = (dgv * gv_grad).astype(BF16)

    col = lambda c: pl.BlockSpec((tm, D_MODEL), lambda i: (i, c))
    row = pl.BlockSpec((tm, D_MODEL), lambda i: (i, 0))
    vec = pl.BlockSpec((1, D_MODEL), lambda i: (0, 0))
    wsp = pl.BlockSpec((A_GROUPS, CHUNK, CHUNK), lambda i: (0, 0, 0))
    bsp = pl.BlockSpec((A_GROUPS, CHUNK, 1), lambda i: (0, 0, 0))
    return pl.pallas_call(
        body, grid=(T // tm,),
        in_specs=[col(0), col(1), col(2), col(3), row, row, vec, vec, wsp, wsp, bsp, ANY],
        out_specs=[pl.BlockSpec((tm, 4 * D_MODEL), lambda i: (i, 0)), row, vec, vec, wsp, bsp],
        out_shape=[jax.ShapeDtypeStruct((T, 4 * D_MODEL), BF16), jax.ShapeDtypeStruct((T, D_MODEL), BF16),
                   jax.ShapeDtypeStruct((1, D_MODEL), F32), jax.ShapeDtypeStruct((1, D_MODEL), F32),
                   jax.ShapeDtypeStruct((A_GROUPS, CHUNK, CHUNK), F32),
                   jax.ShapeDtypeStruct((A_GROUPS, CHUNK, 1), F32)],
        scratch_shapes=[pltpu.VMEM((tm, D_MODEL), BF16), pltpu.VMEM((tm, D_MODEL), F32),
                        pltpu.VMEM((tm, D_MODEL), BF16), pltpu.VMEM((tm, D_MODEL), F32)],
        name="mixer_bwd", compiler_params=_params("arbitrary"))(
            zm, zm, zm, zm, o, dm, av_g, av_b, w_s, w_st, b_col, dep)


def _rope_tables(pos_ref, invf_ref):
    ang = pos_ref[...].astype(F32) * invf_ref[...]
    lane = lax.broadcasted_iota(jnp.int32, ang.shape, 1)
    cos, sin = jnp.cos(ang), jnp.sin(ang)
    c = jnp.where(lane < QK_ROPE, cos, 0.0)
    sa = jnp.where(lane < QK_ROPE // 2, -sin, 0.0)
    sb = jnp.where((lane >= QK_ROPE // 2) & (lane < QK_ROPE), sin, 0.0)
    return c, sa, sb


def _rope(blk, tabs):
    c, sa, sb = tabs
    return blk * c + pltpu.roll(blk, LANES - QK_ROPE // 2, 1) * sa + pltpu.roll(blk, QK_ROPE // 2, 1) * sb


def _rope_t(dout, tabs):
    c, sa, sb = tabs
    return dout * c + pltpu.roll(dout * sa, QK_ROPE // 2, 1) + pltpu.roll(dout * sb, LANES - QK_ROPE // 2, 1)


def _rms_small(x, g):
    r = lax.rsqrt(jnp.mean(x * x, axis=-1, keepdims=True) + EPS)
    xh = x * r
    return xh, r, xh * g


def _mla_prep_fwd(zs, pos, invf, qg, kvg, wuq_p, wukv):
    T = zs.shape[0]
    tm = _tile(T, 512)
    HW = MLA_HEADS * HEAD_PAD

    def body(zs_ref, pos_ref, invf_ref, qg_ref, kvg_ref, wq_ref, wkv_ref, q_ref, k_ref, v_ref):
        tabs = _rope_tables(pos_ref, invf_ref)
        _, _, cqn = _rms_small(zs_ref[:, 0:Q_LORA], qg_ref[...])
        _, _, ckvn = _rms_small(zs_ref[:, Q_LORA:Q_LORA + KV_LORA], kvg_ref[...])
        q = _dot_nt(cqn.astype(BF16), wq_ref[...]) * ATTN_SCALE
        kv = _dot(ckvn.astype(BF16), wkv_ref[...])
        kr = _rope(zs_ref[:, Q_LORA + KV_LORA:ZS_W], tabs).astype(BF16)
        for h in range(MLA_HEADS):
            b0 = h * HEAD_PAD
            q_ref[:, b0:b0 + QK_NOPE] = q[:, b0:b0 + QK_NOPE].astype(BF16)
            q_ref[:, b0 + QK_NOPE:b0 + HEAD_PAD] = _rope(q[:, b0 + QK_NOPE:b0 + HEAD_PAD], tabs).astype(BF16)
            k_ref[:, b0:b0 + QK_NOPE] = kv[:, b0:b0 + QK_NOPE].astype(BF16)
            k_ref[:, b0 + QK_NOPE:b0 + HEAD_PAD] = kr
            v_ref[:, h * V_HEAD:(h + 1) * V_HEAD] = kv[:, b0 + QK_NOPE:b0 + HEAD_PAD].astype(BF16)

    full = lambda a: pl.BlockSpec(a.shape, lambda i: (0,) * a.ndim)
    return pl.pallas_call(
        body, grid=(T // tm,),
        in_specs=[pl.BlockSpec((tm, ZS_W), lambda i: (i, 0)), pl.BlockSpec((tm, 1), lambda i: (i, 0)),
                  full(invf), full(qg), full(kvg), full(wuq_p), full(wukv)],
        out_specs=[pl.BlockSpec((tm, HW), lambda i: (i, 0)), pl.BlockSpec((tm, HW), lambda i: (i, 0)),
                   pl.BlockSpec((tm, D_MODEL), lambda i: (i, 0))],
        out_shape=[jax.ShapeDtypeStruct((T, HW), BF16), jax.ShapeDtypeStruct((T, HW), BF16),
                   jax.ShapeDtypeStruct((T, D_MODEL), BF16)],
        name="mla_prep_fwd", compiler_params=_params("parallel"))(zs, pos, invf, qg, kvg, wuq_p, wukv)


def _mla_prep_bwd(zs, pos, invf, qg, kvg, wuq_p, wukv, dq, dk, dv):
    T = zs.shape[0]
    tm = _tile(T, 512)
    n_t = T // tm
    HW = MLA_HEADS * HEAD_PAD

    def body(zs_ref, pos_ref, invf_ref, qg_ref, kvg_ref, wq_ref, wkv_ref, dq_ref, dk_ref, dv_ref,
             dzs_ref, dwq_ref, dwkv_ref, dqg_ref, dkvg_ref, dqp_ref, dkv_ref, accq_ref, acckv_ref):
        @pl.when(pl.program_id(0) == 0)
        def _():
            dqg_ref[...] = jnp.zeros_like(dqg_ref)
            dkvg_ref[...] = jnp.zeros_like(dkvg_ref)
            accq_ref[...] = jnp.zeros_like(accq_ref)
            acckv_ref[...] = jnp.zeros_like(acckv_ref)

        tabs = _rope_tables(pos_ref, invf_ref)
        cqh, rq, cqn = _rms_small(zs_ref[:, 0:Q_LORA], qg_ref[...])
        ckvh, rkv, ckvn = _rms_small(zs_ref[:, Q_LORA:Q_LORA + KV_LORA], kvg_ref[...])
        dkr = jnp.zeros((tm, LANES), F32)
        for h in range(MLA_HEADS):
            b0 = h * HEAD_PAD
            dqp_ref[:, b0:b0 + QK_NOPE] = dq_ref[:, b0:b0 + QK_NOPE]
            dqp_ref[:, b0 + QK_NOPE:b0 + HEAD_PAD] = _rope_t(
                dq_ref[:, b0 + QK_NOPE:b0 + HEAD_PAD].astype(F32), tabs).astype(BF16)
            dkv_ref[:, b0:b0 + QK_NOPE] = dk_ref[:, b0:b0 + QK_NOPE]
            dkv_ref[:, b0 + QK_NOPE:b0 + HEAD_PAD] = dv_ref[:, h * V_HEAD:(h + 1) * V_HEAD]
            dkr = dkr + dk_ref[:, b0 + QK_NOPE:b0 + HEAD_PAD].astype(F32)
        accq_ref[...] += _dot_tn(dqp_ref[...], cqn.astype(BF16))
        acckv_ref[...] += _dot_tn(ckvn.astype(BF16), dkv_ref[...])

        @pl.when(pl.program_id(0) == n_t - 1)
        def _():
            dwq_ref[...] = accq_ref[...].astype(BF16)
            dwkv_ref[...] = acckv_ref[...].astype(BF16)

        dcqn = _dot(dqp_ref[...], wq_ref[...])
        dckvn = _dot_nt(dkv_ref[...], wkv_ref[...])
        dqg_ref[...] += jnp.sum(dcqn * cqh, axis=0, keepdims=True)
        dkvg_ref[...] += jnp.sum(dckvn * ckvh, axis=0, keepdims=True)
        dxh = dcqn * qg_ref[...]
        dzs_ref[:, 0:Q_LORA] = (rq * (dxh - cqh * jnp.mean(dxh * cqh, axis=-1, keepdims=True))).astype(BF16)
        dxh = dckvn * kvg_ref[...]
        dzs_ref[:, Q_LORA:Q_LORA + KV_LORA] = (
            rkv * (dxh - ckvh * jnp.mean(dxh * ckvh, axis=-1, keepdims=True))).astype(BF16)
        dzs_ref[:, Q_LORA + KV_LORA:ZS_W] = _rope_t(dkr, tabs).astype(BF16)

    full = lambda a: pl.BlockSpec(a.shape, lambda i: (0,) * a.ndim)
    rowb = lambda w: pl.BlockSpec((tm, w), lambda i: (i, 0))
    return pl.pallas_call(
        body, grid=(T // tm,),
        in_specs=[rowb(ZS_W), rowb(1), full(invf), full(qg), full(kvg), full(wuq_p), full(wukv),
                  rowb(HW), rowb(HW), rowb(D_MODEL)],
        out_specs=[rowb(ZS_W), full(wuq_p), full(wukv), full(qg), full(kvg)],
        out_shape=[jax.ShapeDtypeStruct((T, ZS_W), BF16), jax.ShapeDtypeStruct(wuq_p.shape, BF16),
                   jax.ShapeDtypeStruct(wukv.shape, BF16), jax.ShapeDtypeStruct(qg.shape, F32),
                   jax.ShapeDtypeStruct(kvg.shape, F32)],
        scratch_shapes=[pltpu.VMEM((tm, HW), BF16), pltpu.VMEM((tm, HW), BF16),
                        pltpu.VMEM(wuq_p.shape, F32), pltpu.VMEM(wukv.shape, F32)],
        name="mla_prep_bwd", compiler_params=_params("arbitrary"))(
            zs, pos, invf, qg, kvg, wuq_p, wukv, dq, dk, dv)


def _causal(tq, kmax, q0):
    r = lax.broadcasted_iota(jnp.int32, (tq, kmax), 0) + q0
    c = lax.broadcasted_iota(jnp.int32, (tq, kmax), 1)
    return c <= r


def _attn_fwd(q, k, v, batch, seq):
    tq = _tile(seq, ATTN_TILE)
    nq = seq // tq

    def body(q_ref, k_ref, v_ref, o_ref, lse_ref):
        diag = _causal(tq, tq, 0)
        for qi in range(nq):
            rows = slice(qi * tq, (qi + 1) * tq)
            qr = q_ref[rows, :]
            s_d = jnp.where(diag, _dot_nt(qr, k_ref[rows, :]), NEG_BIG)
            m = jnp.max(s_d, axis=-1, keepdims=True)
            if qi > 0:
                before = slice(0, qi * tq)
                s_b = _dot_nt(qr, k_ref[before, :])
                m = jnp.maximum(m, jnp.max(s_b, axis=-1, keepdims=True))
                p_b = jnp.exp(s_b - m)
                l = jnp.sum(p_b, axis=-1, keepdims=True)
                acc = _dot(p_b.astype(BF16), v_ref[before, :])
            p_d = jnp.exp(s_d - m)
            l_d = jnp.sum(p_d, axis=-1, keepdims=True)
            acc_d = _dot(p_d.astype(BF16), v_ref[rows, :])
            l, acc = (l + l_d, acc + acc_d) if qi > 0 else (l_d, acc_d)
            o_ref[rows, :] = (acc / l).astype(KEPT)
            lse_ref[rows, :] = jnp.broadcast_to(m + jnp.log(l), (tq, V_HEAD))

    return pl.pallas_call(
        body, grid=(batch, MLA_HEADS),
        in_specs=[pl.BlockSpec((seq, HEAD_PAD), lambda b, h: (b, h)),
                  pl.BlockSpec((seq, HEAD_PAD), lambda b, h: (b, h)),
                  pl.BlockSpec((seq, V_HEAD), lambda b, h: (b, h))],
        out_specs=[pl.BlockSpec((seq, V_HEAD), lambda b, h: (b, h)),
                   pl.BlockSpec((seq, V_HEAD), lambda b, h: (b, h))],
        out_shape=[jax.ShapeDtypeStruct((batch * seq, D_MODEL), KEPT),
                   jax.ShapeDtypeStruct((batch * seq, D_MODEL), F32)],
        name="attn_fwd", compiler_params=_params("parallel", "parallel"))(q, k, v)


def _attn_bwd(q, k, v, o, do, lse, batch, seq, dep):
    tq = _tile(seq, ATTN_TILE)
    nq = seq // tq

    def body(q_ref, k_ref, v_ref, o_ref, do_ref, lse_ref, dep_ref, dq_ref, dk_ref, dv_ref, dk_acc, dv_acc):
        dk_acc[...] = jnp.zeros_like(dk_acc)
        dv_acc[...] = jnp.zeros_like(dv_acc)
        for qi in range(nq):
            rows = slice(qi * tq, (qi + 1) * tq)
            kmax = (qi + 1) * tq
            qr = q_ref[rows, :]
            dor = do_ref[rows, :]
            kk = k_ref[0:kmax, :]
            s = _dot_nt(qr, kk)
            p = jnp.where(_causal(tq, kmax, qi * tq), jnp.exp(s - lse_ref[rows, 0:1]), 0.0)
            dp = _dot_nt(dor, v_ref[0:kmax, :])
            delta = jnp.sum(dor.astype(F32) * o_ref[rows, :].astype(F32), axis=-1, keepdims=True)
            ds = (p * (dp - delta)).astype(BF16)
            dq_ref[rows, :] = (_dot(ds, kk) * ATTN_SCALE).astype(BF16)
            dk_acc[0:kmax, :] += _dot_tn(ds, qr)
            dv_acc[0:kmax, :] += _dot_tn(p.astype(BF16), dor)
        dk_ref[...] = dk_acc[...].astype(BF16)
        dv_ref[...] = dv_acc[...].astype(BF16)

    qspec = pl.BlockSpec((seq, HEAD_PAD), lambda b, h: (b, h))
    vspec = pl.BlockSpec((seq, V_HEAD), lambda b, h: (b, h))
    T = batch * seq
    return pl.pallas_call(
        body, grid=(batch, MLA_HEADS),
        in_specs=[qspec, qspec, vspec, vspec, vspec, vspec, ANY],
        out_specs=[qspec, qspec, vspec],
        out_shape=[jax.ShapeDtypeStruct((T, MLA_HEADS * HEAD_PAD), BF16),
                   jax.ShapeDtypeStruct((T, MLA_HEADS * HEAD_PAD), BF16),
                   jax.ShapeDtypeStruct((T, D_MODEL), BF16)],
        scratch_shapes=[pltpu.VMEM((seq, HEAD_PAD), F32), pltpu.VMEM((seq, V_HEAD), F32)],
        name="attn_bwd", compiler_params=_params("parallel", "parallel"))(q, k, v, o, do, lse, dep)


def _merge_out(x, yag, zm, o, w_out, ffn_g):
    T = x.shape[0]
    tm = _tile(T, 512)

    def body(x_ref, ya_ref, gb_ref, o_ref, w_ref, g_ref, mg_ref, x1_ref, h2_ref):
        mg = (ya_ref[...].astype(F32) + _sigmoid(gb_ref[...].astype(F32)) * o_ref[...].astype(F32)).astype(BF16)
        mg_ref[...] = mg
        x1 = x_ref[...] + _dot(mg, w_ref[...])
        x1_ref[...] = x1
        r = lax.rsqrt(jnp.mean(x1 * x1, axis=-1, keepdims=True) + EPS)
        h2_ref[...] = (x1 * r * g_ref[...]).astype(BF16)

    row = pl.BlockSpec((tm, D_MODEL), lambda i: (i, 0))
    return pl.pallas_call(
        body, grid=(T // tm,),
        in_specs=[row, row, pl.BlockSpec((tm, D_MODEL), lambda i: (i, 3)), row,
                  pl.BlockSpec((D_MODEL, D_MODEL), lambda i: (0, 0)), pl.BlockSpec((1, D_MODEL), lambda i: (0, 0))],
        out_specs=[row, row, row],
        out_shape=[jax.ShapeDtypeStruct((T, D_MODEL), BF16), jax.ShapeDtypeStruct((T, D_MODEL), F32),
                   jax.ShapeDtypeStruct((T, D_MODEL), BF16)],
        name="merge_out", compiler_params=_params("parallel"))(x, yag, zm, o, w_out, ffn_g)


FF_TILE = 256
FF_BLOCKS = D_FF // FF_TILE
FFB_TILE = 256


def _shift_down(x, k):
    row = lax.broadcasted_iota(jnp.int32, x.shape, 0)
    return jnp.where(row >= k, pltpu.roll(x, k, 0), 0.0)


def _shift_up(x, k):
    n = x.shape[0]
    row = lax.broadcasted_iota(jnp.int32, x.shape, 0)
    return jnp.where(row < n - k, pltpu.roll(x, n - k, 0), 0.0)


def _conv(x, w_ref, b_ref):
    return b_ref[...] + w_ref[2:3, :] * x + w_ref[1:2, :] * _shift_down(x, 1) + w_ref[0:1, :] * _shift_down(x, 2)


EDGE = 16


def _conv_wrapped(x, w_ref, b_ref):
    return (b_ref[...] + w_ref[2:3, :] * x + w_ref[1:2, :] * pltpu.roll(x, 1, 0)
            + w_ref[0:1, :] * pltpu.roll(x, 2, 0))


def _up_act(h2, wt_up, cw, cb, batch, seq):
    def body(h_ref, wug_ref, wuv_ref, wg_ref, wv_ref, bg_ref, bv_ref, ug_ref, uv_ref, g_ref, v_ref, a_ref):
        h = h_ref[...]
        ug = _dot_nt(h, wug_ref[...])
        uv = _dot_nt(h, wuv_ref[...])
        ug_ref[...] = ug.astype(KEPT)
        uv_ref[...] = uv.astype(KEPT)

        def write(rows, gate, val):
            g_ref[rows, :] = gate.astype(KEPT)
            v_ref[rows, :] = val.astype(KEPT)
            a_ref[rows, :] = (gate * _sigmoid(gate) * val).astype(BF16)

        write(slice(None), _conv_wrapped(ug, wg_ref, bg_ref), _conv_wrapped(uv, wv_ref, bv_ref))
        write(slice(0, EDGE), _conv(ug[0:EDGE], wg_ref, bg_ref), _conv(uv[0:EDGE], wv_ref, bv_ref))

    blk = pl.BlockSpec((seq, FF_TILE), lambda b, j: (b, j))
    wup = lambda off: pl.BlockSpec((FF_TILE, D_MODEL), lambda b, j: (j + off, 0))
    wsp = lambda off: pl.BlockSpec((3, FF_TILE), lambda b, j: (0, j + off))
    bsp = lambda off: pl.BlockSpec((1, FF_TILE), lambda b, j: (0, j + off))
    T = batch * seq
    kept = jax.ShapeDtypeStruct((T, D_FF), KEPT)
    return pl.pallas_call(
        body, grid=(batch, FF_BLOCKS),
        in_specs=[pl.BlockSpec((seq, D_MODEL), lambda b, j: (b, 0)), wup(0), wup(FF_BLOCKS),
                  wsp(0), wsp(FF_BLOCKS), bsp(0), bsp(FF_BLOCKS)],
        out_specs=[blk] * 5,
        out_shape=[kept, kept, kept, kept, jax.ShapeDtypeStruct((T, D_FF), BF16)],
        name="up_act", compiler_params=_params("parallel", "arbitrary"))(h2, wt_up, wt_up, cw, cw, cb, cb)


def _ffn_act_bwd(upg, upv, gate, val, cw, dx2b, w_down, batch, seq):
    def half(du, x, w_ref, dx_ref, dw_ref):
        j = pl.program_id(1)
        n = du.shape[0]
        up1, up2 = pltpu.roll(du, n - 1, 0), pltpu.roll(du, n - 2, 0)
        dx_ref[...] = (w_ref[2:3, :] * du + w_ref[1:2, :] * up1 + w_ref[0:1, :] * up2).astype(BF16)
        tail = du[n - EDGE:n]
        dx_ref[n - EDGE:n, :] = (w_ref[2:3, :] * tail + w_ref[1:2, :] * _shift_up(tail, 1)
                                 + w_ref[0:1, :] * _shift_up(tail, 2)).astype(BF16)
        row = lax.broadcasted_iota(jnp.int32, (EDGE, du.shape[1]), 0)
        head, x_tail = du[0:EDGE], x[n - EDGE:n]
        wrap1 = jnp.sum(jnp.where(row >= EDGE - 1, pltpu.roll(head, EDGE - 1, 0), 0.0) * x_tail, axis=0, keepdims=True)
        wrap2 = jnp.sum(jnp.where(row >= EDGE - 2, pltpu.roll(head, EDGE - 2, 0), 0.0) * x_tail, axis=0, keepdims=True)
        dw_ref[j, 2:3, :] += jnp.sum(du * x, axis=0, keepdims=True)
        dw_ref[j, 1:2, :] += jnp.sum(up1 * x, axis=0, keepdims=True) - wrap1
        dw_ref[j, 0:1, :] += jnp.sum(up2 * x, axis=0, keepdims=True) - wrap2
        dw_ref[j, 3:4, :] += jnp.sum(du, axis=0, keepdims=True)

    def body(ug_ref, uv_ref, g_ref, v_ref, wg_ref, wv_ref, dx_ref, wd_ref, dg_ref, dv_ref, dwg_ref, dwv_ref):
        @pl.when((pl.program_id(0) == 0) & (pl.program_id(1) == 0))
        def _():
            dwg_ref[...] = jnp.zeros_like(dwg_ref)
            dwv_ref[...] = jnp.zeros_like(dwv_ref)

        gate, val = g_ref[...].astype(F32), v_ref[...].astype(F32)
        sg = _sigmoid(gate)
        dav = _dot_nt(dx_ref[...], wd_ref[...])
        half(dav * val * sg * (1.0 + gate * (1.0 - sg)), ug_ref[...].astype(F32), wg_ref, dg_ref, dwg_ref)
        half(dav * gate * sg, uv_ref[...].astype(F32), wv_ref, dv_ref, dwv_ref)

    nb = D_FF // FFB_TILE
    blk = pl.BlockSpec((seq, FFB_TILE), lambda b, j: (b, j))
    wsp = lambda off: pl.BlockSpec((3, FFB_TILE), lambda b, j: (0, j + off))
    acc = pl.BlockSpec((nb, 4, FFB_TILE), lambda b, j: (0, 0, 0))
    T = batch * seq
    dupg, dupv, dwg, dwv = pl.pallas_call(
        body, grid=(batch, nb),
        in_specs=[blk, blk, blk, blk, wsp(0), wsp(nb),
                  pl.BlockSpec((seq, D_MODEL), lambda b, j: (b, 0)),
                  pl.BlockSpec((FFB_TILE, D_MODEL), lambda b, j: (j, 0))],
        out_specs=[blk, blk, acc, acc],
        out_shape=[jax.ShapeDtypeStruct((T, D_FF), BF16), jax.ShapeDtypeStruct((T, D_FF), BF16),
                   jax.ShapeDtypeStruct((nb, 4, FFB_TILE), F32), jax.ShapeDtypeStruct((nb, 4, FFB_TILE), F32)],
        name="ffn_act_bwd", compiler_params=_params("arbitrary", "arbitrary"))(
            upg, upv, gate, val, cw, cw, dx2b, w_down)
    dwg, dwv = (jnp.transpose(a, (1, 0, 2)).reshape(4, D_FF) for a in (dwg, dwv))
    return dupg, dupv, dwg[:3], dwv[:3], dwg[3:], dwv[3:]


def _down_loss(a, w_down, x1, target, gfin):
    T = x1.shape[0]
    tm = _tile(T, 512)

    def body(a_ref, w_ref, x1_ref, t_ref, g_ref, dx_ref, dxb_ref, loss_ref, dg_ref):
        @pl.when(pl.program_id(0) == 0)
        def _():
            loss_ref[...] = jnp.zeros_like(loss_ref)
            dg_ref[...] = jnp.zeros_like(dg_ref)

        x2 = x1_ref[...] + _dot(a_ref[...], w_ref[...])
        r = lax.rsqrt(jnp.mean(x2 * x2, axis=-1, keepdims=True) + EPS)
        xh = x2 * r
        g = g_ref[...]
        diff = xh * g - t_ref[...]
        loss_ref[...] += 0.5 * jnp.sum(jnp.mean(diff * diff, axis=-1, keepdims=True))
        dy = diff * (1.0 / D_MODEL)
        dg_ref[...] += jnp.sum(dy * xh, axis=0, keepdims=True)
        dxh = dy * g
        dx = r * (dxh - xh * jnp.mean(dxh * xh, axis=-1, keepdims=True))
        dx_ref[...] = dx
        dxb_ref[...] = dx.astype(BF16)

    row = pl.BlockSpec((tm, D_MODEL), lambda i: (i, 0))
    vec = pl.BlockSpec((1, D_MODEL), lambda i: (0, 0))
    return pl.pallas_call(
        body, grid=(T // tm,),
        in_specs=[pl.BlockSpec((tm, D_FF), lambda i: (i, 0)),
                  pl.BlockSpec((D_FF, D_MODEL), lambda i: (0, 0)), row, row, vec],
        out_specs=[row, row, pl.BlockSpec((8, LANES), lambda i: (0, 0)), vec],
        out_shape=[jax.ShapeDtypeStruct((T, D_MODEL), F32), jax.ShapeDtypeStruct((T, D_MODEL), BF16),
                   jax.ShapeDtypeStruct((8, LANES), F32), jax.ShapeDtypeStruct((1, D_MODEL), F32)],
        name="down_loss", compiler_params=_params("arbitrary"))(a, w_down, x1, target, gfin)


def _local_step(x, positions, target, mix_norm, av_g, av_b, w_s, b_s, q_norm, kv_norm, ffn_norm, conv_b,
                final_norm, comm):
    batch, seq, _ = x.shape
    T = batch * seq
    x = x.reshape(T, D_MODEL)
    target = target.reshape(T, D_MODEL)
    pos = positions.reshape(T, 1)
    half = jnp.arange(0, QK_ROPE, 2, dtype=F32) / QK_ROPE
    inv_freq = 1.0 / (ROPE_THETA ** half)
    invf = jnp.concatenate([inv_freq, inv_freq, jnp.zeros((LANES - QK_ROPE,), F32)]).reshape(1, LANES)
    w_st = jnp.swapaxes(w_s, 1, 2)
    b_col = b_s.reshape(A_GROUPS, CHUNK, 1)

    wt_in = comm.in_weights()
    h, zm, zs = _in_proj(x, mix_norm, wt_in)
    yag = _mixer_a_fwd(zm, av_g, av_b, w_s, b_col)
    wuq_p, wukv, w_out = comm.mla_weights(after=yag)
    q, k, v = _mla_prep_fwd(zs, pos, invf, q_norm, kv_norm, wuq_p, wukv)
    o, lse = _attn_fwd(q, k, v, batch, seq)
    merged, x1, h2 = _merge_out(x, yag, zm, o, w_out, ffn_norm)
    wt_up, conv_w, w_down = comm.ffn_weights(after=merged)
    upg, upv, gate, val, act = _up_act(h2, wt_up, conv_w, conv_b, batch, seq)
    dx2, dx2b, loss_acc, d_final = _down_loss(act, w_down, x1, target, final_norm)

    d_wdown = _mm_tn(act, dx2b, "dw_down")
    dupg, dupv, dcwg, dcwv, dcbg, dcbv = _ffn_act_bwd(upg, upv, gate, val, conv_w, dx2b, w_down, batch, seq)
    d_wt_up = _mm_tn(dupv, h2, "dw_up_val", rows=2 * D_FF, row0=D_FF,
                     into=_mm_tn(dupg, h2, "dw_up_gate", rows=2 * D_FF))
    dx1, d_ffn_norm, dmerged = _proj_bwd(
        [dupg, dupv], wt_up, [(0, (0, D_FF), (0, D_FF)), (1, (0, D_FF), (D_FF, 2 * D_FF))],
        x1, ffn_norm, dx2, "up_proj_bwd", w2=w_out)
    d_wout = _mm_tn(merged, dx1, "dw_out")
    token = comm.send_ffn_grads(d_wdown, d_wt_up, jnp.concatenate([dcwg, dcwv], axis=1), d_wout)
    dzm, do, d_avg, d_avb, d_ws, d_bs = _mixer_bwd(zm, o, dmerged, av_g, av_b, w_s, w_st, b_col, token)
    token = comm.send_small_grads([
        d_avg, d_avb, _small_2d(d_ws), d_bs.reshape(A_GROUPS, CHUNK), d_ffn_norm,
        jnp.concatenate([dcbg, dcbv], axis=1), d_final])
    d_wt_main = _mm_tn(dzm, h, "dw_in_main", dep=token)
    token = comm.send_in_grads(d_wt_main)
    dq, dk, dv = _attn_bwd(q, k, v, o, do, lse, batch, seq, token)
    dzs, d_wuq_p, d_wukv, d_qn, d_kvn = _mla_prep_bwd(zs, pos, invf, q_norm, kv_norm, wuq_p, wukv, dq, dk, dv)
    d_wt_zs = _mm_tn(dzs, h, "dw_in_small")
    token = comm.send_mla_grads(d_wuq_p, d_wukv, d_wt_zs)
    terms = [(0, (i * D_MODEL, (i + 1) * D_MODEL), rows) for i, rows in enumerate(IN_ROWS_MAIN)]
    terms.append((1, (0, ZS_W), IN_ROWS_ZS))
    dx, d_mix_norm = _proj_bwd([dzm, dzs], wt_in, terms, x, mix_norm, dx1, "in_proj_bwd", dep=token, rows=512)
    token = comm.send_late_grads([d_qn, d_kvn, d_mix_norm, loss_acc])
    return dx.reshape(batch, seq, D_MODEL), token


MESH_ID = pl.DeviceIdType.MESH
EFFECT = pltpu.SideEffectType.DATAFLOW_SIDE_EFFECTING


def _mesh_pos():
    return lax.axis_index("x"), lax.axis_index("y"), lax.axis_index("c")


def _peer(pos, d):
    x, y, c = pos
    px = 1 - x if d & 4 else x
    py = 1 - y if d & 2 else y
    pc = 1 - c if d & 1 else c
    return (px, py, pc), 4 * px + 2 * py + pc


def _copy(src_ref, land_ref, send_sems, recv_sems, a, d, pos, exchange, landing_here):
    peer, pid = _peer(pos, d)
    me = 4 * pos[0] + 2 * pos[1] + pos[2]
    if exchange:
        src, dst = src_ref.at[pid], land_ref.at[d]
    else:
        src, dst = src_ref, land_ref.at[pid if landing_here else me]
    return pltpu.make_async_remote_copy(
        src_ref=src, dst_ref=dst, send_sem=send_sems.at[a * (N_DEV - 1) + d - 1],
        recv_sem=recv_sems.at[a * (N_DEV - 1) + d - 1],
        device_id=peer, device_id_type=MESH_ID)


def _start_copies(groups, modes, name, dep=None):
    sizes = [len(g) for g in groups]
    srcs = [s for g in groups for s in g]
    lands = [lax.empty(s.shape if modes[gi] else (N_DEV,) + s.shape, s.dtype)
             for gi, g in enumerate(groups) for s in g]
    n, ng = len(srcs), len(groups)
    n_in = 2 * n + (dep is not None)

    def body(*refs):
        src_refs, land_refs = refs[:n], refs[n:2 * n]
        sems = refs[n_in:n_in + 3 * ng]
        token = refs[-1]
        pos = _mesh_pos()
        k = 0
        for gi, size in enumerate(sizes):
            for a in range(size):
                _own_copy(src_refs[k], land_refs[k], sems[3 * gi + 2], a, pos, modes[gi]).start()
                for d in range(1, N_DEV):
                    _copy(src_refs[k], land_refs[k], sems[3 * gi], sems[3 * gi + 1], a, d, pos, modes[gi],
                          landing_here=False).start()
                k += 1
        token[...] = jnp.zeros_like(token)

    sem_shapes = []
    for size in sizes:
        remote = pltpu.SemaphoreType.DMA((size * (N_DEV - 1),))
        sem_shapes += [remote, remote, pltpu.SemaphoreType.DMA((size,))]
    out = pl.pallas_call(
        body, name=name,
        out_shape=(*sem_shapes, *[pltpu.HBM(a.shape, a.dtype) for a in srcs + lands],
                   jax.ShapeDtypeStruct((8, LANES), F32)),
        in_specs=[HBM] * (2 * n) + [ANY] * (dep is not None),
        out_specs=(*[SEM] * (3 * ng), *[HBM] * (2 * n), pl.BlockSpec(memory_space=pltpu.VMEM)),
        input_output_aliases={i: 3 * ng + i for i in range(2 * n)},
        compiler_params=pltpu.CompilerParams(has_side_effects=EFFECT),
    )(*[pltpu.with_memory_space_constraint(a, pltpu.HBM) for a in srcs + lands], *([dep] if dep is not None else []))
    thru = out[3 * ng:3 * ng + 2 * n]
    handles, k = [], 0
    for gi, size in enumerate(sizes):
        handles.append((out[3 * gi:3 * gi + 3], thru[k:k + size], thru[n + k:n + k + size]))
        k += size
    return handles, out[-1]


def _own_copy(src_ref, land_ref, local_sems, a, pos, exchange):
    me = 4 * pos[0] + 2 * pos[1] + pos[2]
    src, dst = (src_ref.at[me], land_ref.at[0]) if exchange else (src_ref, land_ref.at[me])
    return pltpu.make_async_copy(src, dst, local_sems.at[a])


def _wait_copies(handle, exchange, after, name):
    sems, srcs, lands = handle
    n = len(srcs)

    def body(*refs):
        src_refs, land_refs = refs[:n], refs[n:2 * n]
        send, recv, local = refs[2 * n:2 * n + 3]
        pos = _mesh_pos()
        for a in range(n):
            _own_copy(src_refs[a], land_refs[a], local, a, pos, exchange).wait()
            for d in range(1, N_DEV):
                cp = _copy(src_refs[a], land_refs[a], send, recv, a, d, pos, exchange, landing_here=True)
                cp.wait_send()
                cp.wait_recv()

    out = pl.pallas_call(
        body, name=name,
        out_shape=tuple(pltpu.HBM(a.shape, a.dtype) for a in (*srcs, *lands)),
        in_specs=[HBM] * (2 * n) + [SEM, SEM, SEM, ANY], out_specs=[HBM] * (2 * n),
        input_output_aliases={i: i for i in range(2 * n)},
        compiler_params=pltpu.CompilerParams(has_side_effects=EFFECT),
    )(*srcs, *lands, *sems, after)
    return out[n:]


def _gather_now(a, name):
    def body(x_ref, out_ref, send_sems, recv_sems, local_sem):
        x, y, c = _mesh_pos()
        me, sibling = (x, y, c), (x, y, 1 - c)
        chips = [(1 - x, y), (x, 1 - y), (1 - x, 1 - y)]

        def slot(p):
            return out_ref.at[4 * p[0] + 2 * p[1] + p[2]]

        def copy(k, block, to, src=None):
            return pltpu.make_async_remote_copy(
                src_ref=slot(block) if src is None else src, dst_ref=slot(block), send_sem=send_sems.at[k],
                recv_sem=recv_sems.at[k], device_id=to, device_id_type=MESH_ID)

        mine = pltpu.make_async_copy(x_ref, slot(me), local_sem)
        mine.start()
        first = [copy(0, me, sibling, src=x_ref)]
        first += [copy(1 + j, me, (*chip, c), src=x_ref) for j, chip in enumerate(chips)]
        for cp in first:
            cp.start()
        passed = [copy(4 + j, (*chip, c), sibling) for j, chip in enumerate(chips)]
        for j, chip in enumerate(chips):
            copy(1 + j, (*chip, c), me).wait_recv()
            passed[j].start()
        copy(0, sibling, me).wait_recv()
        for j, chip in enumerate(chips):
            copy(4 + j, (*chip, 1 - c), me).wait_recv()
        for cp in first + passed:
            cp.wait_send()
        mine.wait()

    return pl.pallas_call(
        body, in_specs=[ANY], out_specs=ANY,
        out_shape=jax.ShapeDtypeStruct((N_DEV,) + a.shape, a.dtype),
        scratch_shapes=[pltpu.SemaphoreType.DMA((N_DEV - 1,)), pltpu.SemaphoreType.DMA((N_DEV - 1,)),
                        pltpu.SemaphoreType.DMA],
        name=name, compiler_params=pltpu.CompilerParams(has_side_effects=True))(a)


def _sum_parts(p_ref):
    g = p_ref[0].astype(F32)
    for k in range(1, N_DEV):
        g = g + p_ref[k].astype(F32)
    return g


def _adamw_update(p_ref, w_ref, m_ref, v_ref, g_ref, d_ref, nm_ref, nv_ref, extra_ref=None):
    c1 = 1.0 - ADAM_B1 ** ADAM_STEP
    c2 = 1.0 - ADAM_B2 ** ADAM_STEP
    g = _sum_parts(p_ref)
    if extra_ref is not None:
        g = g + extra_ref[...]
    nm = ADAM_B1 * m_ref[...] + (1.0 - ADAM_B1) * g
    nv = ADAM_B2 * v_ref[...] + (1.0 - ADAM_B2) * (g * g)
    g_ref[...] = g
    nm_ref[...] = nm
    nv_ref[...] = nv
    d_ref[...] = -ADAM_LR * ((nm / c1) / (jnp.sqrt(nv / c2) + ADAM_EPS) + ADAM_WD * w_ref[...])


def _adamw_many(parts, ws, ms, vs, sums, name):
    n, ns = len(ws), len(sums)

    def body(*refs):
        ins, outs = refs[:4 * n + ns], refs[4 * n + ns:]
        for i in range(n):
            _adamw_update(ins[i], ins[n + i], ins[2 * n + i], ins[3 * n + i],
                          outs[i], outs[n + i], outs[2 * n + i], outs[3 * n + i])
        for i in range(ns):
            outs[4 * n + i][...] = _sum_parts(ins[4 * n + i])

    full = lambda a: pl.BlockSpec(a.shape, lambda: (0,) * a.ndim)
    args = [*parts, *ws, *ms, *vs, *sums]
    outs = [jax.ShapeDtypeStruct(w.shape, F32) for _ in range(4) for w in ws]
    outs += [jax.ShapeDtypeStruct(s.shape[1:], F32) for s in sums]
    res = pl.pallas_call(
        body, in_specs=[full(a) for a in args], out_specs=[full(o) for o in outs], out_shape=outs,
        name=name, compiler_params=pltpu.CompilerParams(vmem_limit_bytes=VMEM_LIMIT))(*args)
    return res[:n], res[n:2 * n], res[2 * n:3 * n], res[3 * n:4 * n], res[4 * n:]


def _adamw(parts, w, m, v, name, extra=None):
    R, C = w.shape
    tr, tc = R, C
    if N_DEV * R * C * parts.dtype.itemsize > SMALL_BLOCK_BYTES:
        tr = next((t for t in range(min(R, 256) // 16 * 16, 15, -16) if R % t == 0), R)
        if tr == R:
            tc = _tile(C, 256)

    def body(p_ref, w_ref, m_ref, v_ref, *refs):
        _adamw_update(p_ref, w_ref, m_ref, v_ref, *refs[-4:], extra_ref=refs[0] if extra is not None else None)

    blk = pl.BlockSpec((tr, tc), lambda i, j: (i, j))
    shp = jax.ShapeDtypeStruct((R, C), F32)
    more = [] if extra is None else [extra]
    return pl.pallas_call(
        body, grid=(R // tr, C // tc),
        in_specs=[pl.BlockSpec((N_DEV, tr, tc), lambda i, j: (0, i, j)), blk, blk, blk] + [blk] * len(more),
        out_specs=[blk, blk, blk, blk], out_shape=[shp, shp, shp, shp],
        name=name, compiler_params=_params("parallel", "parallel"))(parts, w, m, v, *more)


SPLIT_V = 2 * D_MODEL
SPLIT_KR = SPLIT_V + Q_LORA + KV_LORA + QK_ROPE
IN_DIM = SPLIT_KR + 2 * D_MODEL
IN_ROWS_MAIN = ((0, D_MODEL), (D_MODEL, SPLIT_V), (SPLIT_KR, SPLIT_KR + D_MODEL), (SPLIT_KR + D_MODEL, IN_DIM))
IN_ROWS_ZS = (SPLIT_V, SPLIT_V + ZS_W)
ZS_ROWS = SPLIT_KR - SPLIT_V
IN_SHARD = IN_DIM // N_DEV
ZS_PIECES = tuple(
    (k, max(IN_SHARD * k, SPLIT_V) - SPLIT_V, max(IN_SHARD * k, SPLIT_V) - IN_SHARD * k)
    for k in range(N_DEV) if max(IN_SHARD * k, SPLIT_V) < min(IN_SHARD * (k + 1), SPLIT_KR))
ZS_PIECE_ROWS = ZS_ROWS // len(ZS_PIECES)
assert all(min(IN_SHARD * (k + 1), SPLIT_KR) - max(IN_SHARD * k, SPLIT_V) == ZS_PIECE_ROWS for k, _, _ in ZS_PIECES)

SMALL_EARLY = ("a_v_norm_g", "a_v_norm_b", "a_spatial_w", "a_spatial_b", "ffn_norm", "conv_b", "final_norm")
SMALL_LATE = ("q_a_norm", "kv_a_norm", "mix_norm")


def _small_2d(a):
    return a.reshape(-1, a.shape[-1])


def _cols_from_shards(g):
    return jnp.transpose(g, (1, 0, 2)).reshape(g.shape[1], N_DEV * g.shape[2])


def _shards_from_cols(a):
    R, W = a.shape
    return jnp.transpose(a.reshape(R, N_DEV, W // N_DEV), (1, 0, 2))


class _Comm:
    GATHER_GROUPS = (("w_uq", "w_ukv", "w_out"), ("w_up", "conv_w", "w_down"))
    FFN_GRADS = ("w_down", "w_up", "conv_w", "w_out")
    MLA_GRADS = ("w_uq", "w_ukv")
    TRANSPOSED = ("w_in", "w_up", "w_uq")

    def __init__(self, shards):
        local = {n: a.astype(F32 if n == "conv_w" else BF16) for n, a in shards.items()}
        self.g_in = _gather_now(local["w_in"], "gather_w_in")
        groups = [[local[n] for n in g] for g in self.GATHER_GROUPS]
        (self.h_mla, self.h_ffn), _ = _start_copies(groups, [False] * 2, "gather_start", dep=self.g_in)

    def in_weights(self):
        return self.g_in.reshape(IN_DIM, D_MODEL)

    def mla_weights(self, after):
        g_uq, g_ukv, g_out = _wait_copies(self.h_mla, False, after, "gather_wait_mla")
        wuq_p = jnp.pad(g_uq, ((0, 0), (0, HEAD_PAD - QK_HEAD), (0, 0))).reshape(MLA_HEADS * HEAD_PAD, Q_LORA)
        return wuq_p, _cols_from_shards(g_ukv), g_out.reshape(D_MODEL, D_MODEL)

    def ffn_weights(self, after):
        g_up, g_cw, g_down = _wait_copies(self.h_ffn, False, after, "gather_wait_ffn")
        return g_up.reshape(2 * D_FF, D_MODEL), _cols_from_shards(g_cw), g_down.reshape(D_FF, D_MODEL)

    def send_ffn_grads(self, d_wdown, d_wt_up, d_convw, d_wout):
        group = [d_wdown.reshape(N_DEV, D_FF // N_DEV, D_MODEL), d_wt_up.reshape(N_DEV, 2 * D_FF // N_DEV, D_MODEL),
                 _shards_from_cols(d_convw), d_wout.reshape(N_DEV, D_MODEL // N_DEV, D_MODEL)]
        (self.h_ffn_grads,), token = _start_copies([group], [True], "ffn_grads_start")
        return token

    def send_small_grads(self, grads):
        (self.h_small_early,), token = _start_copies([grads], [False], "small_grads_start")
        return token

    def send_in_grads(self, d_wt_main):
        hole = jnp.zeros((ZS_ROWS, D_MODEL), d_wt_main.dtype)
        d_in = jnp.concatenate([d_wt_main[:SPLIT_V], hole, d_wt_main[SPLIT_V:]], axis=0)
        blocks = d_in.reshape(N_DEV, IN_SHARD, D_MODEL)
        (self.h_in_grads,), token = _start_copies([[blocks]], [True], "in_grads_start")
        return token

    def send_mla_grads(self, d_wuq_p, d_wukv, d_wt_zs):
        d_uq = d_wuq_p.reshape(MLA_HEADS, HEAD_PAD, Q_LORA)[:, :QK_HEAD, :]
        zs_blocks = jnp.zeros((N_DEV, ZS_PIECE_ROWS, D_MODEL), d_wt_zs.dtype)
        for dev, first, _ in ZS_PIECES:
            zs_blocks = zs_blocks.at[dev].set(d_wt_zs[first:first + ZS_PIECE_ROWS])
        (self.h_mla_grads,), token = _start_copies(
            [[d_uq, _shards_from_cols(d_wukv), zs_blocks]], [True], "mla_grads_start")
        return token

    def send_late_grads(self, small):
        (self.h_late_small,), token = _start_copies([small], [False], "late_grads_start")
        return token


def kernel(x, positions, mix_norm, w_in, a_v_norm_g, a_v_norm_b, a_spatial_w, a_spatial_b, q_a_norm, w_uq, kv_a_norm, w_ukv, w_out, ffn_norm, w_up, conv_w, conv_b, w_down, final_norm, loss_target, m_mix_norm, m_w_in, m_a_v_norm_g, m_a_v_norm_b, m_a_spatial_w, m_a_spatial_b, m_q_a_norm, m_w_uq, m_kv_a_norm, m_w_ukv, m_w_out, m_ffn_norm, m_w_up, m_conv_w, m_conv_b, m_w_down, m_final_norm, v_mix_norm, v_w_in, v_a_v_norm_g, v_a_v_norm_b, v_a_spatial_w, v_a_spatial_b, v_q_a_norm, v_w_uq, v_kv_a_norm, v_w_ukv, v_w_out, v_ffn_norm, v_w_up, v_conv_w, v_conv_b, v_w_down, v_final_norm):
    names = ("mix_norm", "w_in", "a_v_norm_g", "a_v_norm_b", "a_spatial_w", "a_spatial_b", "q_a_norm", "w_uq",
             "kv_a_norm", "w_ukv", "w_out", "ffn_norm", "w_up", "conv_w", "conv_b", "w_down", "final_norm")
    w = dict(zip(names, (mix_norm, w_in, a_v_norm_g, a_v_norm_b, a_spatial_w, a_spatial_b, q_a_norm, w_uq,
                         kv_a_norm, w_ukv, w_out, ffn_norm, w_up, conv_w, conv_b, w_down, final_norm)))
    m = dict(zip(names, (m_mix_norm, m_w_in, m_a_v_norm_g, m_a_v_norm_b, m_a_spatial_w, m_a_spatial_b,
                         m_q_a_norm, m_w_uq, m_kv_a_norm, m_w_ukv, m_w_out, m_ffn_norm, m_w_up, m_conv_w,
                         m_conv_b, m_w_down, m_final_norm)))
    v = dict(zip(names, (v_mix_norm, v_w_in, v_a_v_norm_g, v_a_v_norm_b, v_a_spatial_w, v_a_spatial_b,
                         v_q_a_norm, v_w_uq, v_kv_a_norm, v_w_ukv, v_w_out, v_ffn_norm, v_w_up, v_conv_w,
                         v_conv_b, v_w_down, v_final_norm)))
    shapes = {n: w[n].shape for n in names}
    def view(tree, n):
        a = tree[n].reshape(tree[n].shape[-2:])
        return a.T if n in _Comm.TRANSPOSED else a

    comm = _Comm({n: view(w, n) for n in ("w_in",) + _Comm.GATHER_GROUPS[0] + _Comm.GATHER_GROUPS[1]})

    grad_x, token = _local_step(
        x, positions, loss_target, w["mix_norm"], w["a_v_norm_g"], w["a_v_norm_b"], w["a_spatial_w"][0],
        w["a_spatial_b"][0], w["q_a_norm"], w["kv_a_norm"], w["ffn_norm"], w["conv_b"],
        w["final_norm"].reshape(1, D_MODEL), comm)

    out_g, out_d, out_m, out_v = {}, {}, {}, {}

    def update(n, parts, extra=None):
        res = _adamw(parts, view(w, n), view(m, n), view(v, n), "adamw_" + n, extra=extra)
        out_g[n], out_d[n], out_m[n], out_v[n] = (
            (t.T if n in _Comm.TRANSPOSED else t).reshape(shapes[n]) for t in res)
        return res[1]

    def update_small(names, parts, sums, name):
        res = _adamw_many(parts, *[[_small_2d(t[n]) for n in names] for t in (w, m, v)], sums, name)
        for i, n in enumerate(names):
            out_g[n], out_d[n], out_m[n], out_v[n] = (r[i].reshape(shapes[n]) for r in res[:4])
        return res

    for n, parts in zip(_Comm.FFN_GRADS, _wait_copies(comm.h_ffn_grads, True, token, "ffn_grads_wait")):
        last = update(n, parts)
    early = _wait_copies(comm.h_small_early, False, last, "small_grads_wait")
    last = update_small(SMALL_EARLY, early, [], "adamw_small")[1][0]
    (in_parts,) = _wait_copies(comm.h_in_grads, True, last, "in_grads_wait")
    uq_parts, ukv_parts, zs_parts = _wait_copies(comm.h_mla_grads, True, in_parts, "mla_grads_wait")
    zs_sum = _adamw_many([], [], [], [], [zs_parts], "sum_zs_grads")[4][0]
    me = 4 * lax.axis_index("x") + 2 * lax.axis_index("y") + lax.axis_index("c")
    first_row = sum(jnp.where(me == dev, row, 0) for dev, _, row in ZS_PIECES)
    extra = lax.dynamic_update_slice(jnp.zeros((IN_SHARD, D_MODEL), F32), zs_sum, (first_row, 0))
    last = update("w_in", in_parts, extra=extra)
    last = update("w_uq", uq_parts)
    last = update("w_ukv", ukv_parts)
    late = _wait_copies(comm.h_late_small, False, last, "late_small_wait")
    res = update_small(SMALL_LATE, late[:-1], late[-1:], "adamw_late")
    loss = res[4][0][0, 0]

    return (loss, grad_x, *[out_g[n] for n in names], *[out_d[n] for n in names],
            *[out_m[n] for n in names], *[out_v[n] for n in names])
```

```python
import math

import jax
import jax.numpy as jnp
from jax import lax
from jax.experimental import pallas as pl
from jax.experimental.pallas import tpu as pltpu

F32 = jnp.float32
BF16 = jnp.bfloat16
KEPT = jnp.bfloat16

N_DEV = 8
D_MODEL = 1024
EPS = 1e-6
A_GROUPS = 8
CHUNK = 128
MLA_HEADS = 8
QK_NOPE = 128
QK_ROPE = 64
QK_HEAD = QK_NOPE + QK_ROPE
HEAD_PAD = 256
V_HEAD = 128
Q_LORA = 256
KV_LORA = 128
ROPE_THETA = 10000.0
D_FF = 2816
ZS_W = 512
ATTN_SCALE = QK_HEAD ** -0.5
ATTN_TILE = 512
NEG_BIG = -1e30

ADAM_LR = 0.001
ADAM_B1 = 0.9
ADAM_B2 = 0.999
ADAM_EPS = 1e-08
ADAM_WD = 0.01
ADAM_STEP = 10

VMEM_LIMIT = 56 * 1024 * 1024
SMALL_BLOCK_BYTES = 5 * 1024 * 1024
LANES = 128
SUBLANES = 8

GELU_K = math.sqrt(2.0 / math.pi)
GELU_C = 0.044715

ANY = pl.BlockSpec(memory_space=pl.ANY)
HBM = pl.BlockSpec(memory_space=pltpu.HBM)
SEM = pl.BlockSpec(memory_space=pltpu.SEMAPHORE)


def _tile(n, pref):
    for t in (pref, 512, 256, 128, 64, 32, 16, 8):
        if t <= pref and n % t == 0:
            return t
    return n


def _wide_tile(n, cap=1408):
    return next((t for t in range(min(n, cap) // LANES * LANES, 0, -LANES) if n % t == 0), n)


def _params(*sem):
    return pltpu.CompilerParams(dimension_semantics=sem, vmem_limit_bytes=VMEM_LIMIT)


def _dot(a, b):
    return jnp.dot(a, b, preferred_element_type=F32)


def _dot_nt(a, b):
    return lax.dot_general(a, b, (((1,), (1,)), ((), ())), preferred_element_type=F32)


def _dot_tn(a, b):
    return lax.dot_general(a, b, (((0,), (0,)), ((), ())), preferred_element_type=F32)


def _sigmoid(x):
    return 1.0 / (1.0 + jnp.exp(-x))


def _gelu(x):
    t = jnp.tanh(GELU_K * (x + GELU_C * x * x * x))
    return 0.5 * x * (1.0 + t)


def _gelu_and_grad(x):
    x2 = x * x
    t = jnp.tanh(GELU_K * (x + GELU_C * x * x2))
    half = 0.5 * (1.0 + t)
    return x * half, half + 0.5 * x * (1.0 - t * t) * GELU_K * (1.0 + 3.0 * GELU_C * x2)


def _in_proj(x, g, wt):
    T, Dm = x.shape
    tm = _tile(T, 512)

    def body(x_ref, g_ref, wt_ref, h_ref, zm_ref, zs_ref):
        xf = x_ref[...]
        r = lax.rsqrt(jnp.mean(xf * xf, axis=-1, keepdims=True) + EPS)
        h = (xf * r * g_ref[...]).astype(BF16)
        h_ref[...] = h
        for i, (r0, r1) in enumerate(IN_ROWS_MAIN):
            zm_ref[:, i * D_MODEL:(i + 1) * D_MODEL] = _dot_nt(h, wt_ref[r0:r1, :]).astype(KEPT)
        zs_ref[...] = _dot_nt(h, wt_ref[IN_ROWS_ZS[0]:IN_ROWS_ZS[1], :])

    row = lambda n: pl.BlockSpec((tm, n), lambda i: (i, 0))
    return pl.pallas_call(
        body, grid=(T // tm,),
        in_specs=[row(Dm), pl.BlockSpec((1, Dm), lambda i: (0, 0)), pl.BlockSpec(wt.shape, lambda i: (0, 0))],
        out_specs=[row(Dm), row(4 * D_MODEL), row(ZS_W)],
        out_shape=[jax.ShapeDtypeStruct((T, Dm), BF16), jax.ShapeDtypeStruct((T, 4 * D_MODEL), KEPT),
                   jax.ShapeDtypeStruct((T, ZS_W), F32)],
        name="in_proj", compiler_params=_params("parallel"))(x, g, wt)


def _proj_bwd(acts, wt, terms, x, g, dres, name, w2=None, dep=None, rows=256):
    T, Dm = x.shape
    tm = _tile(T, rows)
    n_a = len(acts)

    def body(*refs):
        ins, outs = refs[:n_a + 4 + (w2 is not None) + (dep is not None)], refs[-2 - (w2 is not None):]
        wt_ref, x_ref, g_ref, dres_ref = ins[n_a:n_a + 4]
        dx_ref, dg_ref = outs[0], outs[1]

        @pl.when(pl.program_id(0) == 0)
        def _():
            dg_ref[...] = jnp.zeros_like(dg_ref)

        dy = None
        for i, (c0, c1), (r0, r1) in terms:
            t = _dot(ins[i][:, c0:c1], wt_ref[r0:r1, :])
            dy = t if dy is None else dy + t
        xf = x_ref[...]
        r = lax.rsqrt(jnp.mean(xf * xf, axis=-1, keepdims=True) + EPS)
        xh = xf * r
        dg_ref[...] += jnp.sum(dy * xh, axis=0, keepdims=True)
        dxh = dy * g_ref[...]
        dx = dres_ref[...] + r * (dxh - xh * jnp.mean(dxh * xh, axis=-1, keepdims=True))
        dx_ref[...] = dx
        if w2 is not None:
            outs[2][...] = _dot_nt(dx.astype(BF16), ins[n_a + 4][...]).astype(KEPT)

    row = pl.BlockSpec((tm, Dm), lambda i: (i, 0))
    vec = pl.BlockSpec((1, Dm), lambda i: (0, 0))
    in_specs = [pl.BlockSpec((tm, a.shape[1]), lambda i: (i, 0)) for a in acts]
    in_specs += [pl.BlockSpec(wt.shape, lambda i: (0, 0)), row, vec, row]
    args = [*acts, wt, x, g, dres]
    out_specs = [row, vec]
    out_shape = [jax.ShapeDtypeStruct((T, Dm), F32), jax.ShapeDtypeStruct((1, Dm), F32)]
    if w2 is not None:
        in_specs.append(pl.BlockSpec(w2.shape, lambda i: (0, 0)))
        args.append(w2)
        out_specs.append(pl.BlockSpec((tm, w2.shape[0]), lambda i: (i, 0)))
        out_shape.append(jax.ShapeDtypeStruct((T, w2.shape[0]), KEPT))
    if dep is not None:
        in_specs.append(ANY)
        args.append(dep)
    return pl.pallas_call(
        body, grid=(T // tm,), in_specs=in_specs, out_specs=out_specs, out_shape=out_shape,
        name=name, compiler_params=_params("arbitrary"))(*args)


def _mm_tn(a, b, name, dep=None, rows=None, row0=0, into=None):
    T, M = a.shape
    N = b.shape[1]
    tm, tn, tt = _wide_tile(M), _wide_tile(N), _tile(T, 2048)
    n_t = T // tt
    off = row0 // tm
    extra = ([dep] if dep is not None else []) + ([into] if into is not None else [])

    def body(a_ref, b_ref, *refs):
        o_ref, acc_ref = refs[-2:]
        t = pl.program_id(2)

        @pl.when(t == 0)
        def _():
            acc_ref[...] = jnp.zeros_like(acc_ref)

        acc_ref[...] += _dot_tn(a_ref[...].astype(BF16), b_ref[...].astype(BF16))

        @pl.when(t == n_t - 1)
        def _():
            o_ref[...] = acc_ref[...].astype(BF16)

    return pl.pallas_call(
        body, grid=(M // tm, N // tn, n_t),
        in_specs=[pl.BlockSpec((tt, tm), lambda i, j, t: (t, i)),
                  pl.BlockSpec((tt, tn), lambda i, j, t: (t, j))] + [ANY] * len(extra),
        out_specs=pl.BlockSpec((tm, tn), lambda i, j, t: (i + off, j)),
        out_shape=jax.ShapeDtypeStruct((rows or M, N), BF16),
        scratch_shapes=[pltpu.VMEM((tm, tn), F32)],
        input_output_aliases={} if into is None else {1 + len(extra): 0},
        name=name, compiler_params=_params("parallel", "parallel", "arbitrary"))(a, b, *extra)


def _layer_norm_fwd(gv, g, b):
    mu = jnp.mean(gv, axis=-1, keepdims=True)
    xc = gv - mu
    rs = lax.rsqrt(jnp.mean(xc * xc, axis=-1, keepdims=True) + EPS)
    xh = xc * rs
    return xh, rs, xh * g + b


def _tri_mask(transposed=False):
    r = lax.broadcasted_iota(jnp.int32, (CHUNK, CHUNK), 0)
    c = lax.broadcasted_iota(jnp.int32, (CHUNK, CHUNK), 1)
    return r <= c if transposed else c <= r


def _mixer_a_fwd(zm, av_g, av_b, w_s, b_col):
    T = zm.shape[0]
    tm = _tile(T, 512)
    n_chunk = tm // CHUNK

    def body(u_ref, v_ref, ga_ref, g_ref, b_ref, w_ref, bc_ref, y_ref, vn_s, mx_s):
        gu = _gelu(u_ref[...].astype(F32))
        _, _, vn = _layer_norm_fwd(_gelu(v_ref[...].astype(F32)), g_ref[...], b_ref[...])
        vn_s[...] = vn.astype(BF16)
        tri = _tri_mask()
        for gi in range(A_GROUPS):
            wm = jnp.where(tri, w_ref[gi], 0.0).astype(BF16)
            cols = slice(gi * CHUNK, (gi + 1) * CHUNK)
            for n in range(n_chunk):
                rows = slice(n * CHUNK, (n + 1) * CHUNK)
                mx_s[rows, cols] = _dot(wm, vn_s[rows, cols]) + bc_ref[gi]
        y_ref[...] = (_sigmoid(ga_ref[...].astype(F32)) * gu * mx_s[...]).astype(KEPT)

    col = lambda c: pl.BlockSpec((tm, D_MODEL), lambda i: (i, c))
    vec = pl.BlockSpec((1, D_MODEL), lambda i: (0, 0))
    return pl.pallas_call(
        body, grid=(T // tm,),
        in_specs=[col(0), col(1), col(2), vec, vec,
                  pl.BlockSpec((A_GROUPS, CHUNK, CHUNK), lambda i: (0, 0, 0)),
                  pl.BlockSpec((A_GROUPS, CHUNK, 1), lambda i: (0, 0, 0))],
        out_specs=pl.BlockSpec((tm, D_MODEL), lambda i: (i, 0)),
        out_shape=jax.ShapeDtypeStruct((T, D_MODEL), KEPT),
        scratch_shapes=[pltpu.VMEM((tm, D_MODEL), BF16), pltpu.VMEM((tm, D_MODEL), F32)],
        name="mixer_a_fwd", compiler_params=_params("parallel"))(zm, zm, zm, av_g, av_b, w_s, b_col)


def _mixer_bwd(zm, o, dm, av_g, av_b, w_s, w_st, b_col, dep):
    T = zm.shape[0]
    tm = _tile(T, 256)
    n_chunk = tm // CHUNK

    def body(u_ref, v_ref, ga_ref, gb_ref, o_ref, dm_ref, g_ref, b_ref, w_ref, wt_ref, bc_ref, dep_ref,
             dz_ref, do_ref, dg_ref, db_ref, dw_ref, dbs_ref, vn_s, mx_s, dmx_s, dvn_s):
        @pl.when(pl.program_id(0) == 0)
        def _():
            dg_ref[...] = jnp.zeros_like(dg_ref)
            db_ref[...] = jnp.zeros_like(db_ref)
            dw_ref[...] = jnp.zeros_like(dw_ref)
            dbs_ref[...] = jnp.zeros_like(dbs_ref)

        dm_v = dm_ref[...].astype(F32)
        gb = gb_ref[...].astype(F32)
        sb = _sigmoid(gb)
        o_v = o_ref[...].astype(F32)
        do_ref[...] = (dm_v * sb).astype(BF16)
        dz_ref[:, 3 * D_MODEL:4 * D_MODEL] = (dm_v * o_v * sb * (1.0 - sb)).astype(BF16)
        u = u_ref[...].astype(F32)
        v = v_ref[...].astype(F32)
        gu, gu_grad = _gelu_and_grad(u)
        gv, gv_grad = _gelu_and_grad(v)
        xh, rs, vn = _layer_norm_fwd(gv, g_ref[...], b_ref[...])
        vn_s[...] = vn.astype(BF16)
        tri = _tri_mask()
        for gi in range(A_GROUPS):
            wm = jnp.where(tri, w_ref[gi], 0.0).astype(BF16)
            cols = slice(gi * CHUNK, (gi + 1) * CHUNK)
            for n in range(n_chunk):
                rows = slice(n * CHUNK, (n + 1) * CHUNK)
                mx_s[rows, cols] = _dot(wm, vn_s[rows, cols]) + bc_ref[gi]
        mixed = mx_s[...]
        sa = _sigmoid(ga_ref[...].astype(F32))
        dya = dm_v * sa
        dz_ref[:, 2 * D_MODEL:3 * D_MODEL] = (dm_v * gu * mixed * sa * (1.0 - sa)).astype(BF16)
        dz_ref[:, 0:D_MODEL] = (dya * mixed * gu_grad).astype(BF16)
        dmx = dya * gu
        dmx_s[...] = dmx.astype(BF16)
        tri_t = _tri_mask(transposed=True)
        for gi in range(A_GROUPS):
            wmt = jnp.where(tri_t, wt_ref[gi], 0.0).astype(BF16)
            cols = slice(gi * CHUNK, (gi + 1) * CHUNK)
            dw_acc = jnp.zeros((CHUNK, CHUNK), F32)
            dmx_sum = jnp.zeros((CHUNK, CHUNK), F32)
            for n in range(n_chunk):
                rows = slice(n * CHUNK, (n + 1) * CHUNK)
                blk = dmx_s[rows, cols]
                dvn_s[rows, cols] = _dot(wmt, blk)
                dw_acc = dw_acc + _dot_nt(blk, vn_s[rows, cols])
                dmx_sum = dmx_sum + dmx[rows, cols]
            dw_ref[gi] += jnp.where(tri, dw_acc, 0.0)
            dbs_ref[gi] += jnp.sum(dmx_sum, axis=-1, keepdims=True)
        dvn = dvn_s[...]
        dg_ref[...] += jnp.sum(dvn * xh, axis=0, keepdims=True)
        db_ref[...] += jnp.sum(dvn, axis=0, keepdims=True)
        dxh = dvn * g_ref[...]
        dgv = rs * (dxh - jnp.mean(dxh, axis=-1, keepdims=True)
                    - xh * jnp.mean(dxh * xh, axis=-1, keepdims=True))
        dz_ref[:, D_MODEL:2 * D_MODEL] = (dgv * gv_grad).astype(BF16)

    col = lambda c: pl.BlockSpec((tm, D_MODEL), lambda i: (i, c))
    row = pl.BlockSpec((tm, D_MODEL), lambda i: (i, 0))
    vec = pl.BlockSpec((1, D_MODEL), lambda i: (0, 0))
    wsp = pl.BlockSpec((A_GROUPS, CHUNK, CHUNK), lambda i: (0, 0, 0))
    bsp = pl.BlockSpec((A_GROUPS, CHUNK, 1), lambda i: (0, 0, 0))
    return pl.pallas_call(
        body, grid=(T // tm,),
        in_specs=[col(0), col(1), col(2), col(3), row, row, vec, vec, wsp, wsp, bsp, ANY],
        out_specs=[pl.BlockSpec((tm, 4 * D_MODEL), lambda i: (i, 0)), row, vec, vec, wsp, bsp],
        out_shape=[jax.ShapeDtypeStruct((T, 4 * D_MODEL), BF16), jax.ShapeDtypeStruct((T, D_MODEL), BF16),
                   jax.ShapeDtypeStruct((1, D_MODEL), F32), jax.ShapeDtypeStruct((1, D_MODEL), F32),
                   jax.ShapeDtypeStruct((A_GROUPS, CHUNK, CHUNK), F32),
                   jax.ShapeDtypeStruct((A_GROUPS, CHUNK, 1), F32)],
        scratch_shapes=[pltpu.VMEM((tm, D_MODEL), BF16), pltpu.VMEM((tm, D_MODEL), F32),
                        pltpu.VMEM((tm, D_MODEL), BF16), pltpu.VMEM((tm, D_MODEL), F32)],
        name="mixer_bwd", compiler_params=_params("arbitrary"))(
            zm, zm, zm, zm, o, dm, av_g, av_b, w_s, w_st, b_col, dep)


def _rope_tables(pos_ref, invf_ref):
    ang = pos_ref[...].astype(F32) * invf_ref[...]
    lane = lax.broadcasted_iota(jnp.int32, ang.shape, 1)
    cos, sin = jnp.cos(ang), jnp.sin(ang)
    c = jnp.where(lane < QK_ROPE, cos, 0.0)
    sa = jnp.where(lane < QK_ROPE // 2, -sin, 0.0)
    sb = jnp.where((lane >= QK_ROPE // 2) & (lane < QK_ROPE), sin, 0.0)
    return c, sa, sb


def _rope(blk, tabs):
    c, sa, sb = tabs
    return blk * c + pltpu.roll(blk, LANES - QK_ROPE // 2, 1) * sa + pltpu.roll(blk, QK_ROPE // 2, 1) * sb


def _rope_t(dout, tabs):
    c, sa, sb = tabs
    return dout * c + pltpu.roll(dout * sa, QK_ROPE // 2, 1) + pltpu.roll(dout * sb, LANES - QK_ROPE // 2, 1)


def _rms_small(x, g):
    r = lax.rsqrt(jnp.mean(x * x, axis=-1, keepdims=True) + EPS)
    xh = x * r
    return xh, r, xh * g


def _mla_prep_fwd(zs, pos, invf, qg, kvg, wuq_p, wukv):
    T = zs.shape[0]
    tm = _tile(T, 512)
    HW = MLA_HEADS * HEAD_PAD

    def body(zs_ref, pos_ref, invf_ref, qg_ref, kvg_ref, wq_ref, wkv_ref, q_ref, k_ref, v_ref):
        tabs = _rope_tables(pos_ref, invf_ref)
        _, _, cqn = _rms_small(zs_ref[:, 0:Q_LORA], qg_ref[...])
        _, _, ckvn = _rms_small(zs_ref[:, Q_LORA:Q_LORA + KV_LORA], kvg_ref[...])
        q = _dot_nt(cqn.astype(BF16), wq_ref[...]) * ATTN_SCALE
        kv = _dot(ckvn.astype(BF16), wkv_ref[...])
        kr = _rope(zs_ref[:, Q_LORA + KV_LORA:ZS_W], tabs).astype(BF16)
        for h in range(MLA_HEADS):
            b0 = h * HEAD_PAD
            q_ref[:, b0:b0 + QK_NOPE] = q[:, b0:b0 + QK_NOPE].astype(BF16)
            q_ref[:, b0 + QK_NOPE:b0 + HEAD_PAD] = _rope(q[:, b0 + QK_NOPE:b0 + HEAD_PAD], tabs).astype(BF16)
            k_ref[:, b0:b0 + QK_NOPE] = kv[:, b0:b0 + QK_NOPE].astype(BF16)
            k_ref[:, b0 + QK_NOPE:b0 + HEAD_PAD] = kr
            v_ref[:, h * V_HEAD:(h + 1) * V_HEAD] = kv[:, b0 + QK_NOPE:b0 + HEAD_PAD].astype(BF16)

    full = lambda a: pl.BlockSpec(a.shape, lambda i: (0,) * a.ndim)
    return pl.pallas_call(
        body, grid=(T // tm,),
        in_specs=[pl.BlockSpec((tm, ZS_W), lambda i: (i, 0)), pl.BlockSpec((tm, 1), lambda i: (i, 0)),
                  full(invf), full(qg), full(kvg), full(wuq_p), full(wukv)],
        out_specs=[pl.BlockSpec((tm, HW), lambda i: (i, 0)), pl.BlockSpec((tm, HW), lambda i: (i, 0)),
                   pl.BlockSpec((tm, D_MODEL), lambda i: (i, 0))],
        out_shape=[jax.ShapeDtypeStruct((T, HW), BF16), jax.ShapeDtypeStruct((T, HW), BF16),
                   jax.ShapeDtypeStruct((T, D_MODEL), BF16)],
        name="mla_prep_fwd", compiler_params=_params("parallel"))(zs, pos, invf, qg, kvg, wuq_p, wukv)


def _mla_prep_bwd(zs, pos, invf, qg, kvg, wuq_p, wukv, dq, dk, dv):
    T = zs.shape[0]
    tm = _tile(T, 512)
    HW = MLA_HEADS * HEAD_PAD

    def body(zs_ref, pos_ref, invf_ref, qg_ref, kvg_ref, wq_ref, wkv_ref, dq_ref, dk_ref, dv_ref,
             dzs_ref, cqn_ref, dqp_ref, ckvn_ref, dkv_ref, dqg_ref, dkvg_ref):
        @pl.when(pl.program_id(0) == 0)
        def _():
            dqg_ref[...] = jnp.zeros_like(dqg_ref)
            dkvg_ref[...] = jnp.zeros_like(dkvg_ref)

        tabs = _rope_tables(pos_ref, invf_ref)
        cqh, rq, cqn = _rms_small(zs_ref[:, 0:Q_LORA], qg_ref[...])
        ckvh, rkv, ckvn = _rms_small(zs_ref[:, Q_LORA:Q_LORA + KV_LORA], kvg_ref[...])
        cqn_ref[...] = cqn.astype(BF16)
        ckvn_ref[...] = ckvn.astype(BF16)
        dkr = jnp.zeros((tm, LANES), F32)
        for h in range(MLA_HEADS):
            b0 = h * HEAD_PAD
            dqp_ref[:, b0:b0 + QK_NOPE] = dq_ref[:, b0:b0 + QK_NOPE]
            dqp_ref[:, b0 + QK_NOPE:b0 + HEAD_PAD] = _rope_t(
                dq_ref[:, b0 + QK_NOPE:b0 + HEAD_PAD].astype(F32), tabs).astype(BF16)
            dkv_ref[:, b0:b0 + QK_NOPE] = dk_ref[:, b0:b0 + QK_NOPE]
            dkv_ref[:, b0 + QK_NOPE:b0 + HEAD_PAD] = dv_ref[:, h * V_HEAD:(h + 1) * V_HEAD]
            dkr = dkr + dk_ref[:, b0 + QK_NOPE:b0 + HEAD_PAD].astype(F32)
        dcqn = _dot(dqp_ref[...], wq_ref[...])
        dckvn = _dot_nt(dkv_ref[...], wkv_ref[...])
        dqg_ref[...] += jnp.sum(dcqn * cqh, axis=0, keepdims=True)
        dkvg_ref[...] += jnp.sum(dckvn * ckvh, axis=0, keepdims=True)
        dxh = dcqn * qg_ref[...]
        dzs_ref[:, 0:Q_LORA] = (rq * (dxh - cqh * jnp.mean(dxh * cqh, axis=-1, keepdims=True))).astype(BF16)
        dxh = dckvn * kvg_ref[...]
        dzs_ref[:, Q_LORA:Q_LORA + KV_LORA] = (
            rkv * (dxh - ckvh * jnp.mean(dxh * ckvh, axis=-1, keepdims=True))).astype(BF16)
        dzs_ref[:, Q_LORA + KV_LORA:ZS_W] = _rope_t(dkr, tabs).astype(BF16)

    full = lambda a: pl.BlockSpec(a.shape, lambda i: (0,) * a.ndim)
    rowb = lambda w: pl.BlockSpec((tm, w), lambda i: (i, 0))
    return pl.pallas_call(
        body, grid=(T // tm,),
        in_specs=[rowb(ZS_W), rowb(1), full(invf), full(qg), full(kvg), full(wuq_p), full(wukv),
                  rowb(HW), rowb(HW), rowb(D_MODEL)],
        out_specs=[rowb(ZS_W), rowb(Q_LORA), rowb(HW), rowb(KV_LORA), rowb(HW), full(qg), full(kvg)],
        out_shape=[jax.ShapeDtypeStruct((T, ZS_W), BF16), jax.ShapeDtypeStruct((T, Q_LORA), BF16),
                   jax.ShapeDtypeStruct((T, HW), BF16), jax.ShapeDtypeStruct((T, KV_LORA), BF16),
                   jax.ShapeDtypeStruct((T, HW), BF16), jax.ShapeDtypeStruct(qg.shape, F32),
                   jax.ShapeDtypeStruct(kvg.shape, F32)],
        name="mla_prep_bwd", compiler_params=_params("arbitrary"))(
            zs, pos, invf, qg, kvg, wuq_p, wukv, dq, dk, dv)


def _causal(tq, kmax, q0):
    r = lax.broadcasted_iota(jnp.int32, (tq, kmax), 0) + q0
    c = lax.broadcasted_iota(jnp.int32, (tq, kmax), 1)
    return c <= r


def _attn_fwd(q, k, v, batch, seq):
    tq = _tile(seq, ATTN_TILE)
    nq = seq // tq

    def body(q_ref, k_ref, v_ref, o_ref, lse_ref):
        diag = _causal(tq, tq, 0)
        for qi in range(nq):
            rows = slice(qi * tq, (qi + 1) * tq)
            qr = q_ref[rows, :]
            s_d = jnp.where(diag, _dot_nt(qr, k_ref[rows, :]), NEG_BIG)
            m = jnp.max(s_d, axis=-1, keepdims=True)
            if qi > 0:
                before = slice(0, qi * tq)
                s_b = _dot_nt(qr, k_ref[before, :])
                m = jnp.maximum(m, jnp.max(s_b, axis=-1, keepdims=True))
                p_b = jnp.exp(s_b - m)
                l = jnp.sum(p_b, axis=-1, keepdims=True)
                acc = _dot(p_b.astype(BF16), v_ref[before, :])
            p_d = jnp.exp(s_d - m)
            l_d = jnp.sum(p_d, axis=-1, keepdims=True)
            acc_d = _dot(p_d.astype(BF16), v_ref[rows, :])
            l, acc = (l + l_d, acc + acc_d) if qi > 0 else (l_d, acc_d)
            o_ref[rows, :] = (acc / l).astype(KEPT)
            lse_ref[rows, :] = jnp.broadcast_to(m + jnp.log(l), (tq, V_HEAD))

    return pl.pallas_call(
        body, grid=(batch, MLA_HEADS),
        in_specs=[pl.BlockSpec((seq, HEAD_PAD), lambda b, h: (b, h)),
                  pl.BlockSpec((seq, HEAD_PAD), lambda b, h: (b, h)),
                  pl.BlockSpec((seq, V_HEAD), lambda b, h: (b, h))],
        out_specs=[pl.BlockSpec((seq, V_HEAD), lambda b, h: (b, h)),
                   pl.BlockSpec((seq, V_HEAD), lambda b, h: (b, h))],
        out_shape=[jax.ShapeDtypeStruct((batch * seq, D_MODEL), KEPT),
                   jax.ShapeDtypeStruct((batch * seq, D_MODEL), F32)],
        name="attn_fwd", compiler_params=_params("parallel", "parallel"))(q, k, v)


def _attn_bwd(q, k, v, o, do, lse, batch, seq, dep):
    tq = _tile(seq, ATTN_TILE)
    nq = seq // tq

    def body(q_ref, k_ref, v_ref, o_ref, do_ref, lse_ref, dep_ref, dq_ref, dk_ref, dv_ref, dk_acc, dv_acc):
        dk_acc[...] = jnp.zeros_like(dk_acc)
        dv_acc[...] = jnp.zeros_like(dv_acc)
        for qi in range(nq):
            rows = slice(qi * tq, (qi + 1) * tq)
            kmax = (qi + 1) * tq
            qr = q_ref[rows, :]
            dor = do_ref[rows, :]
            kk = k_ref[0:kmax, :]
            s = _dot_nt(qr, kk)
            p = jnp.where(_causal(tq, kmax, qi * tq), jnp.exp(s - lse_ref[rows, 0:1]), 0.0)
            dp = _dot_nt(dor, v_ref[0:kmax, :])
            delta = jnp.sum(dor.astype(F32) * o_ref[rows, :].astype(F32), axis=-1, keepdims=True)
            ds = (p * (dp - delta)).astype(BF16)
            dq_ref[rows, :] = (_dot(ds, kk) * ATTN_SCALE).astype(BF16)
            dk_acc[0:kmax, :] += _dot_tn(ds, qr)
            dv_acc[0:kmax, :] += _dot_tn(p.astype(BF16), dor)
        dk_ref[...] = dk_acc[...].astype(BF16)
        dv_ref[...] = dv_acc[...].astype(BF16)

    qspec = pl.BlockSpec((seq, HEAD_PAD), lambda b, h: (b, h))
    vspec = pl.BlockSpec((seq, V_HEAD), lambda b, h: (b, h))
    T = batch * seq
    return pl.pallas_call(
        body, grid=(batch, MLA_HEADS),
        in_specs=[qspec, qspec, vspec, vspec, vspec, vspec, ANY],
        out_specs=[qspec, qspec, vspec],
        out_shape=[jax.ShapeDtypeStruct((T, MLA_HEADS * HEAD_PAD), BF16),
                   jax.ShapeDtypeStruct((T, MLA_HEADS * HEAD_PAD), BF16),
                   jax.ShapeDtypeStruct((T, D_MODEL), BF16)],
        scratch_shapes=[pltpu.VMEM((seq, HEAD_PAD), F32), pltpu.VMEM((seq, V_HEAD), F32)],
        name="attn_bwd", compiler_params=_params("parallel", "parallel"))(q, k, v, o, do, lse, dep)


def _merge_out(x, yag, zm, o, w_out, ffn_g):
    T = x.shape[0]
    tm = _tile(T, 512)

    def body(x_ref, ya_ref, gb_ref, o_ref, w_ref, g_ref, mg_ref, x1_ref, h2_ref):
        mg = (ya_ref[...].astype(F32) + _sigmoid(gb_ref[...].astype(F32)) * o_ref[...].astype(F32)).astype(BF16)
        mg_ref[...] = mg
        x1 = x_ref[...] + _dot(mg, w_ref[...])
        x1_ref[...] = x1
        r = lax.rsqrt(jnp.mean(x1 * x1, axis=-1, keepdims=True) + EPS)
        h2_ref[...] = (x1 * r * g_ref[...]).astype(BF16)

    row = pl.BlockSpec((tm, D_MODEL), lambda i: (i, 0))
    return pl.pallas_call(
        body, grid=(T // tm,),
        in_specs=[row, row, pl.BlockSpec((tm, D_MODEL), lambda i: (i, 3)), row,
                  pl.BlockSpec((D_MODEL, D_MODEL), lambda i: (0, 0)), pl.BlockSpec((1, D_MODEL), lambda i: (0, 0))],
        out_specs=[row, row, row],
        out_shape=[jax.ShapeDtypeStruct((T, D_MODEL), BF16), jax.ShapeDtypeStruct((T, D_MODEL), F32),
                   jax.ShapeDtypeStruct((T, D_MODEL), BF16)],
        name="merge_out", compiler_params=_params("parallel"))(x, yag, zm, o, w_out, ffn_g)


FF_TILE = 256
FF_BLOCKS = D_FF // FF_TILE
FFB_TILE = 256
UP_ROWS = 512
EDGE = 16


def _shift_up(x, k):
    n = x.shape[0]
    row = lax.broadcasted_iota(jnp.int32, x.shape, 0)
    return jnp.where(row < n - k, pltpu.roll(x, n - k, 0), 0.0)


def _up_act(h2, wt_up, cw, cb, batch, seq):
    def body(h_ref, wug_ref, wuv_ref, wg_ref, wv_ref, bg_ref, bv_ref, ug_ref, uv_ref, g_ref, v_ref, a_ref,
             ug_s, uv_s):
        for s in (ug_s, uv_s):
            s[0:SUBLANES, :] = jnp.zeros((SUBLANES, FF_TILE), F32)

        def conv(s, w_ref, b_ref, r0):
            return (b_ref[...] + w_ref[2:3, :] * s[r0:r0 + UP_ROWS, :]
                    + w_ref[1:2, :] * s[r0 - 1:r0 - 1 + UP_ROWS, :]
                    + w_ref[0:1, :] * s[r0 - 2:r0 - 2 + UP_ROWS, :])

        for c in range(seq // UP_ROWS):
            rows = slice(c * UP_ROWS, (c + 1) * UP_ROWS)
            r0 = SUBLANES + c * UP_ROWS
            h = h_ref[rows, :]
            for w_ref, u_ref, s in ((wug_ref, ug_ref, ug_s), (wuv_ref, uv_ref, uv_s)):
                u = _dot_nt(h, w_ref[...])
                u_ref[rows, :] = u.astype(KEPT)
                s[r0:r0 + UP_ROWS, :] = u
            gate, val = conv(ug_s, wg_ref, bg_ref, r0), conv(uv_s, wv_ref, bv_ref, r0)
            g_ref[rows, :] = gate.astype(KEPT)
            v_ref[rows, :] = val.astype(KEPT)
            a_ref[rows, :] = (gate * _sigmoid(gate) * val).astype(BF16)

    blk = pl.BlockSpec((seq, FF_TILE), lambda b, j: (b, j))
    wup = lambda off: pl.BlockSpec((FF_TILE, D_MODEL), lambda b, j: (j + off, 0))
    wsp = lambda off: pl.BlockSpec((3, FF_TILE), lambda b, j: (0, j + off))
    bsp = lambda off: pl.BlockSpec((1, FF_TILE), lambda b, j: (0, j + off))
    T = batch * seq
    kept = jax.ShapeDtypeStruct((T, D_FF), KEPT)
    return pl.pallas_call(
        body, grid=(batch, FF_BLOCKS),
        in_specs=[pl.BlockSpec((seq, D_MODEL), lambda b, j: (b, 0)), wup(0), wup(FF_BLOCKS),
                  wsp(0), wsp(FF_BLOCKS), bsp(0), bsp(FF_BLOCKS)],
        out_specs=[blk] * 5,
        out_shape=[kept, kept, kept, kept, jax.ShapeDtypeStruct((T, D_FF), BF16)],
        scratch_shapes=[pltpu.VMEM((SUBLANES + seq, FF_TILE), F32)] * 2,
        name="up_act", compiler_params=_params("parallel", "arbitrary"))(h2, wt_up, wt_up, cw, cw, cb, cb)


def _ffn_act_bwd(upg, upv, gate, val, cw, dx2b, w_down, batch, seq):
    def half(du, x, w_ref, dx_ref, dw_ref):
        j = pl.program_id(1)
        n = du.shape[0]
        up1, up2 = pltpu.roll(du, n - 1, 0), pltpu.roll(du, n - 2, 0)
        dx_ref[...] = (w_ref[2:3, :] * du + w_ref[1:2, :] * up1 + w_ref[0:1, :] * up2).astype(BF16)
        tail = du[n - EDGE:n]
        dx_ref[n - EDGE:n, :] = (w_ref[2:3, :] * tail + w_ref[1:2, :] * _shift_up(tail, 1)
                                 + w_ref[0:1, :] * _shift_up(tail, 2)).astype(BF16)
        row = lax.broadcasted_iota(jnp.int32, (EDGE, du.shape[1]), 0)
        head, x_tail = du[0:EDGE], x[n - EDGE:n]
        wrap1 = jnp.sum(jnp.where(row >= EDGE - 1, pltpu.roll(head, EDGE - 1, 0), 0.0) * x_tail, axis=0, keepdims=True)
        wrap2 = jnp.sum(jnp.where(row >= EDGE - 2, pltpu.roll(head, EDGE - 2, 0), 0.0) * x_tail, axis=0, keepdims=True)
        dw_ref[j, 2:3, :] += jnp.sum(du * x, axis=0, keepdims=True)
        dw_ref[j, 1:2, :] += jnp.sum(up1 * x, axis=0, keepdims=True) - wrap1
        dw_ref[j, 0:1, :] += jnp.sum(up2 * x, axis=0, keepdims=True) - wrap2
        dw_ref[j, 3:4, :] += jnp.sum(du, axis=0, keepdims=True)

    def body(ug_ref, uv_ref, g_ref, v_ref, wg_ref, wv_ref, dx_ref, wd_ref, dg_ref, dv_ref, dwg_ref, dwv_ref):
        @pl.when((pl.program_id(0) == 0) & (pl.program_id(1) == 0))
        def _():
            dwg_ref[...] = jnp.zeros_like(dwg_ref)
            dwv_ref[...] = jnp.zeros_like(dwv_ref)

        gate, val = g_ref[...].astype(F32), v_ref[...].astype(F32)
        sg = _sigmoid(gate)
        dav = _dot_nt(dx_ref[...], wd_ref[...])
        half(dav * val * sg * (1.0 + gate * (1.0 - sg)), ug_ref[...].astype(F32), wg_ref, dg_ref, dwg_ref)
        half(dav * gate * sg, uv_ref[...].astype(F32), wv_ref, dv_ref, dwv_ref)

    nb = D_FF // FFB_TILE
    blk = pl.BlockSpec((seq, FFB_TILE), lambda b, j: (b, j))
    wsp = lambda off: pl.BlockSpec((3, FFB_TILE), lambda b, j: (0, j + off))
    acc = pl.BlockSpec((nb, 4, FFB_TILE), lambda b, j: (0, 0, 0))
    T = batch * seq
    dupg, dupv, dwg, dwv = pl.pallas_call(
        body, grid=(batch, nb),
        in_specs=[blk, blk, blk, blk, wsp(0), wsp(nb),
                  pl.BlockSpec((seq, D_MODEL), lambda b, j: (b, 0)),
                  pl.BlockSpec((FFB_TILE, D_MODEL), lambda b, j: (j, 0))],
        out_specs=[blk, blk, acc, acc],
        out_shape=[jax.ShapeDtypeStruct((T, D_FF), BF16), jax.ShapeDtypeStruct((T, D_FF), BF16),
                   jax.ShapeDtypeStruct((nb, 4, FFB_TILE), F32), jax.ShapeDtypeStruct((nb, 4, FFB_TILE), F32)],
        name="ffn_act_bwd", compiler_params=_params("arbitrary", "arbitrary"))(
            upg, upv, gate, val, cw, cw, dx2b, w_down)
    dwg, dwv = (jnp.transpose(a, (1, 0, 2)).reshape(4, D_FF) for a in (dwg, dwv))
    return dupg, dupv, dwg[:3], dwv[:3], dwg[3:], dwv[3:]


def _down_loss(a, w_down, x1, target, gfin):
    T = x1.shape[0]
    tm = _tile(T, 512)

    def body(a_ref, w_ref, x1_ref, t_ref, g_ref, dx_ref, dxb_ref, loss_ref, dg_ref):
        @pl.when(pl.program_id(0) == 0)
        def _():
            loss_ref[...] = jnp.zeros_like(loss_ref)
            dg_ref[...] = jnp.zeros_like(dg_ref)

        x2 = x1_ref[...] + _dot(a_ref[...], w_ref[...])
        r = lax.rsqrt(jnp.mean(x2 * x2, axis=-1, keepdims=True) + EPS)
        xh = x2 * r
        g = g_ref[...]
        diff = xh * g - t_ref[...]
        loss_ref[...] += 0.5 * jnp.sum(jnp.mean(diff * diff, axis=-1, keepdims=True))
        dy = diff * (1.0 / D_MODEL)
        dg_ref[...] += jnp.sum(dy * xh, axis=0, keepdims=True)
        dxh = dy * g
        dx = r * (dxh - xh * jnp.mean(dxh * xh, axis=-1, keepdims=True))
        dx_ref[...] = dx
        dxb_ref[...] = dx.astype(BF16)

    row = pl.BlockSpec((tm, D_MODEL), lambda i: (i, 0))
    vec = pl.BlockSpec((1, D_MODEL), lambda i: (0, 0))
    return pl.pallas_call(
        body, grid=(T // tm,),
        in_specs=[pl.BlockSpec((tm, D_FF), lambda i: (i, 0)),
                  pl.BlockSpec((D_FF, D_MODEL), lambda i: (0, 0)), row, row, vec],
        out_specs=[row, row, pl.BlockSpec((8, LANES), lambda i: (0, 0)), vec],
        out_shape=[jax.ShapeDtypeStruct((T, D_MODEL), F32), jax.ShapeDtypeStruct((T, D_MODEL), BF16),
                   jax.ShapeDtypeStruct((8, LANES), F32), jax.ShapeDtypeStruct((1, D_MODEL), F32)],
        name="down_loss", compiler_params=_params("arbitrary"))(a, w_down, x1, target, gfin)


def _local_step(x, positions, target, mix_norm, av_g, av_b, w_s, b_s, q_norm, kv_norm, ffn_norm, conv_b,
                final_norm, comm):
    batch, seq, _ = x.shape
    T = batch * seq
    x = x.reshape(T, D_MODEL)
    target = target.reshape(T, D_MODEL)
    pos = positions.reshape(T, 1)
    half = jnp.arange(0, QK_ROPE, 2, dtype=F32) / QK_ROPE
    inv_freq = 1.0 / (ROPE_THETA ** half)
    invf = jnp.concatenate([inv_freq, inv_freq, jnp.zeros((LANES - QK_ROPE,), F32)]).reshape(1, LANES)
    w_st = jnp.swapaxes(w_s, 1, 2)
    b_col = b_s.reshape(A_GROUPS, CHUNK, 1)

    wt_in = comm.in_weights()
    h, zm, zs = _in_proj(x, mix_norm, wt_in)
    yag = _mixer_a_fwd(zm, av_g, av_b, w_s, b_col)
    wuq_p, wukv, w_out = comm.mla_weights(after=yag)
    q, k, v = _mla_prep_fwd(zs, pos, invf, q_norm, kv_norm, wuq_p, wukv)
    o, lse = _attn_fwd(q, k, v, batch, seq)
    merged, x1, h2 = _merge_out(x, yag, zm, o, w_out, ffn_norm)
    wt_up, conv_w, w_down = comm.ffn_weights(after=merged)
    upg, upv, gate, val, act = _up_act(h2, wt_up, conv_w, conv_b, batch, seq)
    dx2, dx2b, loss_acc, d_final = _down_loss(act, w_down, x1, target, final_norm)

    d_wdown = _mm_tn(act, dx2b, "dw_down")
    dupg, dupv, dcwg, dcwv, dcbg, dcbv = _ffn_act_bwd(upg, upv, gate, val, conv_w, dx2b, w_down, batch, seq)
    d_wt_up = _mm_tn(dupv, h2, "dw_up_val", rows=2 * D_FF, row0=D_FF,
                     into=_mm_tn(dupg, h2, "dw_up_gate", rows=2 * D_FF))
    dx1, d_ffn_norm, dmerged = _proj_bwd(
        [dupg, dupv], wt_up, [(0, (0, D_FF), (0, D_FF)), (1, (0, D_FF), (D_FF, 2 * D_FF))],
        x1, ffn_norm, dx2, "up_proj_bwd", w2=w_out)
    d_wout = _mm_tn(merged, dx1, "dw_out")
    token = comm.send_ffn_grads(d_wdown, d_wt_up, jnp.concatenate([dcwg, dcwv], axis=1), d_wout)
    dzm, do, d_avg, d_avb, d_ws, d_bs = _mixer_bwd(zm, o, dmerged, av_g, av_b, w_s, w_st, b_col, token)
    token = comm.send_small_grads([
        d_avg, d_avb, _small_2d(d_ws), d_bs.reshape(A_GROUPS, CHUNK), d_ffn_norm,
        jnp.concatenate([dcbg, dcbv], axis=1), d_final])
    dq, dk, dv = _attn_bwd(q, k, v, o, do, lse, batch, seq, token)
    dzs, cqn, dqp, ckvn, dkv, d_qn, d_kvn = _mla_prep_bwd(zs, pos, invf, q_norm, kv_norm, wuq_p, wukv, dq, dk, dv)
    d_wt_main = _mm_tn(dzm, h, "dw_in_main")
    d_wt_zs = _mm_tn(dzs, h, "dw_in_small")
    token = comm.send_in_grads(d_wt_main, d_wt_zs)
    d_wuq_p = _mm_tn(dqp, cqn, "dw_uq", dep=token)
    d_wukv = _mm_tn(ckvn, dkv, "dw_ukv", dep=token)
    terms = [(0, (i * D_MODEL, (i + 1) * D_MODEL), rows) for i, rows in enumerate(IN_ROWS_MAIN)]
    terms.append((1, (0, ZS_W), IN_ROWS_ZS))
    dx, d_mix_norm = _proj_bwd([dzm, dzs], wt_in, terms, x, mix_norm, dx1, "in_proj_bwd", dep=token, rows=512)
    token = comm.send_late_grads(d_wuq_p, d_wukv, [d_qn, d_kvn, d_mix_norm, loss_acc])
    return dx.reshape(batch, seq, D_MODEL), token


MESH_ID = pl.DeviceIdType.MESH
EFFECT = pltpu.SideEffectType.DATAFLOW_SIDE_EFFECTING


def _mesh_pos():
    return lax.axis_index("x"), lax.axis_index("y"), lax.axis_index("c")


def _peer(pos, d):
    x, y, c = pos
    px = 1 - x if d & 4 else x
    py = 1 - y if d & 2 else y
    pc = 1 - c if d & 1 else c
    return (px, py, pc), 4 * px + 2 * py + pc


def _copy(src_ref, land_ref, send_sems, recv_sems, a, d, pos, exchange, landing_here):
    peer, pid = _peer(pos, d)
    me = 4 * pos[0] + 2 * pos[1] + pos[2]
    if exchange:
        src, dst = src_ref.at[pid], land_ref.at[d]
    else:
        src, dst = src_ref, land_ref.at[pid if landing_here else me]
    return pltpu.make_async_remote_copy(
        src_ref=src, dst_ref=dst, send_sem=send_sems.at[a * (N_DEV - 1) + d - 1],
        recv_sem=recv_sems.at[a * (N_DEV - 1) + d - 1],
        device_id=peer, device_id_type=MESH_ID)


def _start_copies(groups, modes, name, dep=None):
    sizes = [len(g) for g in groups]
    srcs = [s for g in groups for s in g]
    lands = [lax.empty(s.shape if modes[gi] else (N_DEV,) + s.shape, s.dtype)
             for gi, g in enumerate(groups) for s in g]
    n, ng = len(srcs), len(groups)
    n_in = 2 * n + (dep is not None)

    def body(*refs):
        src_refs, land_refs = refs[:n], refs[n:2 * n]
        sems = refs[n_in:n_in + 3 * ng]
        token = refs[-1]
        pos = _mesh_pos()
        k = 0
        for gi, size in enumerate(sizes):
            for a in range(size):
                _own_copy(src_refs[k], land_refs[k], sems[3 * gi + 2], a, pos, modes[gi]).start()
                for d in range(1, N_DEV):
                    _copy(src_refs[k], land_refs[k], sems[3 * gi], sems[3 * gi + 1], a, d, pos, modes[gi],
                          landing_here=False).start()
                k += 1
        token[...] = jnp.zeros_like(token)

    sem_shapes = []
    for size in sizes:
        remote = pltpu.SemaphoreType.DMA((size * (N_DEV - 1),))
        sem_shapes += [remote, remote, pltpu.SemaphoreType.DMA((size,))]
    out = pl.pallas_call(
        body, name=name,
        out_shape=(*sem_shapes, *[pltpu.HBM(a.shape, a.dtype) for a in srcs + lands],
                   jax.ShapeDtypeStruct((8, LANES), F32)),
        in_specs=[HBM] * (2 * n) + [ANY] * (dep is not None),
        out_specs=(*[SEM] * (3 * ng), *[HBM] * (2 * n), pl.BlockSpec(memory_space=pltpu.VMEM)),
        input_output_aliases={i: 3 * ng + i for i in range(2 * n)},
        compiler_params=pltpu.CompilerParams(has_side_effects=EFFECT),
    )(*[pltpu.with_memory_space_constraint(a, pltpu.HBM) for a in srcs + lands], *([dep] if dep is not None else []))
    thru = out[3 * ng:3 * ng + 2 * n]
    handles, k = [], 0
    for gi, size in enumerate(sizes):
        handles.append((out[3 * gi:3 * gi + 3], thru[k:k + size], thru[n + k:n + k + size]))
        k += size
    return handles, out[-1]


def _own_copy(src_ref, land_ref, local_sems, a, pos, exchange):
    me = 4 * pos[0] + 2 * pos[1] + pos[2]
    src, dst = (src_ref.at[me], land_ref.at[0]) if exchange else (src_ref, land_ref.at[me])
    return pltpu.make_async_copy(src, dst, local_sems.at[a])


def _wait_copies(handle, exchange, after, name):
    sems, srcs, lands = handle
    n = len(srcs)

    def body(*refs):
        src_refs, land_refs = refs[:n], refs[n:2 * n]
        send, recv, local = refs[2 * n:2 * n + 3]
        pos = _mesh_pos()
        for a in range(n):
            _own_copy(src_refs[a], land_refs[a], local, a, pos, exchange).wait()
            for d in range(1, N_DEV):
                cp = _copy(src_refs[a], land_refs[a], send, recv, a, d, pos, exchange, landing_here=True)
                cp.wait_send()
                cp.wait_recv()

    out = pl.pallas_call(
        body, name=name,
        out_shape=tuple(pltpu.HBM(a.shape, a.dtype) for a in (*srcs, *lands)),
        in_specs=[HBM] * (2 * n) + [SEM, SEM, SEM, ANY], out_specs=[HBM] * (2 * n),
        input_output_aliases={i: i for i in range(2 * n)},
        compiler_params=pltpu.CompilerParams(has_side_effects=EFFECT),
    )(*srcs, *lands, *sems, after)
    return out[n:]


def _gather_now(a, name):
    def body(x_ref, out_ref, send_sems, recv_sems, local_sem):
        x, y, c = _mesh_pos()
        me, sibling = (x, y, c), (x, y, 1 - c)
        chips = [(1 - x, y), (x, 1 - y), (1 - x, 1 - y)]

        def slot(p):
            return out_ref.at[4 * p[0] + 2 * p[1] + p[2]]

        def copy(k, block, to, src=None):
            return pltpu.make_async_remote_copy(
                src_ref=slot(block) if src is None else src, dst_ref=slot(block), send_sem=send_sems.at[k],
                recv_sem=recv_sems.at[k], device_id=to, device_id_type=MESH_ID)

        mine = pltpu.make_async_copy(x_ref, slot(me), local_sem)
        mine.start()
        first = [copy(0, me, sibling, src=x_ref)]
        first += [copy(1 + j, me, (*chip, c), src=x_ref) for j, chip in enumerate(chips)]
        for cp in first:
            cp.start()
        passed = [copy(4 + j, (*chip, c), sibling) for j, chip in enumerate(chips)]
        for j, chip in enumerate(chips):
            copy(1 + j, (*chip, c), me).wait_recv()
            passed[j].start()
        copy(0, sibling, me).wait_recv()
        for j, chip in enumerate(chips):
            copy(4 + j, (*chip, 1 - c), me).wait_recv()
        for cp in first + passed:
            cp.wait_send()
        mine.wait()

    return pl.pallas_call(
        body, in_specs=[ANY], out_specs=ANY,
        out_shape=jax.ShapeDtypeStruct((N_DEV,) + a.shape, a.dtype),
        scratch_shapes=[pltpu.SemaphoreType.DMA((N_DEV - 1,)), pltpu.SemaphoreType.DMA((N_DEV - 1,)),
                        pltpu.SemaphoreType.DMA],
        name=name, compiler_params=pltpu.CompilerParams(has_side_effects=True))(a)


def _sum_parts(p_ref):
    g = p_ref[0].astype(F32)
    for k in range(1, N_DEV):
        g = g + p_ref[k].astype(F32)
    return g


def _adamw_update(p_ref, w_ref, m_ref, v_ref, g_ref, d_ref, nm_ref, nv_ref):
    c1 = 1.0 - ADAM_B1 ** ADAM_STEP
    c2 = 1.0 - ADAM_B2 ** ADAM_STEP
    g = _sum_parts(p_ref)
    nm = ADAM_B1 * m_ref[...] + (1.0 - ADAM_B1) * g
    nv = ADAM_B2 * v_ref[...] + (1.0 - ADAM_B2) * (g * g)
    g_ref[...] = g
    nm_ref[...] = nm
    nv_ref[...] = nv
    d_ref[...] = -ADAM_LR * ((nm / c1) / (jnp.sqrt(nv / c2) + ADAM_EPS) + ADAM_WD * w_ref[...])


def _adamw_many(parts, ws, ms, vs, sums, name):
    n, ns = len(ws), len(sums)

    def body(*refs):
        ins, outs = refs[:4 * n + ns], refs[4 * n + ns:]
        for i in range(n):
            _adamw_update(ins[i], ins[n + i], ins[2 * n + i], ins[3 * n + i],
                          outs[i], outs[n + i], outs[2 * n + i], outs[3 * n + i])
        for i in range(ns):
            outs[4 * n + i][...] = _sum_parts(ins[4 * n + i])

    full = lambda a: pl.BlockSpec(a.shape, lambda: (0,) * a.ndim)
    args = [*parts, *ws, *ms, *vs, *sums]
    outs = [jax.ShapeDtypeStruct(w.shape, F32) for _ in range(4) for w in ws]
    outs += [jax.ShapeDtypeStruct(s.shape[1:], F32) for s in sums]
    res = pl.pallas_call(
        body, in_specs=[full(a) for a in args], out_specs=[full(o) for o in outs], out_shape=outs,
        name=name, compiler_params=pltpu.CompilerParams(vmem_limit_bytes=VMEM_LIMIT))(*args)
    return res[:n], res[n:2 * n], res[2 * n:3 * n], res[3 * n:4 * n], res[4 * n:]


def _adamw(parts, w, m, v, name):
    R, C = w.shape
    tr, tc = R, C
    if N_DEV * R * C * parts.dtype.itemsize > SMALL_BLOCK_BYTES:
        tr = next((t for t in range(min(R, 256) // 16 * 16, 15, -16) if R % t == 0), R)
        if tr == R:
            tc = _tile(C, 256)

    def body(p_ref, w_ref, m_ref, v_ref, g_ref, d_ref, nm_ref, nv_ref):
        _adamw_update(p_ref, w_ref, m_ref, v_ref, g_ref, d_ref, nm_ref, nv_ref)

    blk = pl.BlockSpec((tr, tc), lambda i, j: (i, j))
    shp = jax.ShapeDtypeStruct((R, C), F32)
    return pl.pallas_call(
        body, grid=(R // tr, C // tc),
        in_specs=[pl.BlockSpec((N_DEV, tr, tc), lambda i, j: (0, i, j)), blk, blk, blk],
        out_specs=[blk, blk, blk, blk], out_shape=[shp, shp, shp, shp],
        name=name, compiler_params=_params("parallel", "parallel"))(parts, w, m, v)


SPLIT_V = 2 * D_MODEL
SPLIT_KR = SPLIT_V + Q_LORA + KV_LORA + QK_ROPE
IN_DIM = SPLIT_KR + 2 * D_MODEL
IN_ROWS_MAIN = ((0, D_MODEL), (D_MODEL, SPLIT_V), (SPLIT_KR, SPLIT_KR + D_MODEL), (SPLIT_KR + D_MODEL, IN_DIM))
IN_ROWS_ZS = (SPLIT_V, SPLIT_V + ZS_W)

SMALL_EARLY = ("a_v_norm_g", "a_v_norm_b", "a_spatial_w", "a_spatial_b", "ffn_norm", "conv_b", "final_norm")
SMALL_LATE = ("q_a_norm", "kv_a_norm", "mix_norm")


def _small_2d(a):
    return a.reshape(-1, a.shape[-1])


def _cols_from_shards(g):
    return jnp.transpose(g, (1, 0, 2)).reshape(g.shape[1], N_DEV * g.shape[2])


def _shards_from_cols(a):
    R, W = a.shape
    return jnp.transpose(a.reshape(R, N_DEV, W // N_DEV), (1, 0, 2))


class _Comm:
    GATHER_GROUPS = (("w_uq", "w_ukv", "w_out"), ("w_up", "conv_w", "w_down"))
    FFN_GRADS = ("w_down", "w_up", "conv_w", "w_out")
    LATE_GRADS = ("w_uq", "w_ukv")
    TRANSPOSED = ("w_in", "w_up", "w_uq")

    def __init__(self, shards):
        local = {n: a.astype(F32 if n == "conv_w" else BF16) for n, a in shards.items()}
        self.g_in = _gather_now(local["w_in"], "gather_w_in")
        groups = [[local[n] for n in g] for g in self.GATHER_GROUPS]
        (self.h_mla, self.h_ffn), _ = _start_copies(groups, [False] * 2, "gather_start", dep=self.g_in)

    def in_weights(self):
        return self.g_in.reshape(IN_DIM, D_MODEL)

    def mla_weights(self, after):
        g_uq, g_ukv, g_out = _wait_copies(self.h_mla, False, after, "gather_wait_mla")
        wuq_p = jnp.pad(g_uq, ((0, 0), (0, HEAD_PAD - QK_HEAD), (0, 0))).reshape(MLA_HEADS * HEAD_PAD, Q_LORA)
        return wuq_p, _cols_from_shards(g_ukv), g_out.reshape(D_MODEL, D_MODEL)

    def ffn_weights(self, after):
        g_up, g_cw, g_down = _wait_copies(self.h_ffn, False, after, "gather_wait_ffn")
        return g_up.reshape(2 * D_FF, D_MODEL), _cols_from_shards(g_cw), g_down.reshape(D_FF, D_MODEL)

    def send_ffn_grads(self, d_wdown, d_wt_up, d_convw, d_wout):
        group = [d_wdown.reshape(N_DEV, D_FF // N_DEV, D_MODEL), d_wt_up.reshape(N_DEV, 2 * D_FF // N_DEV, D_MODEL),
                 _shards_from_cols(d_convw), d_wout.reshape(N_DEV, D_MODEL // N_DEV, D_MODEL)]
        (self.h_ffn_grads,), token = _start_copies([group], [True], "ffn_grads_start")
        return token

    def send_small_grads(self, grads):
        (self.h_small_early,), token = _start_copies([grads], [False], "small_grads_start")
        return token

    def send_in_grads(self, d_wt_main, d_wt_zs):
        d_in = jnp.concatenate([d_wt_main[:SPLIT_V], d_wt_zs[:SPLIT_KR - SPLIT_V], d_wt_main[SPLIT_V:]], axis=0)
        blocks = d_in.reshape(N_DEV, IN_DIM // N_DEV, D_MODEL)
        (self.h_in_grads,), token = _start_copies([[blocks]], [True], "in_grads_start")
        return token

    def send_late_grads(self, d_wuq_p, d_wukv, small):
        d_uq = d_wuq_p.reshape(MLA_HEADS, HEAD_PAD, Q_LORA)[:, :QK_HEAD, :]
        (self.h_late_grads, self.h_late_small), token = _start_copies(
            [[d_uq, _shards_from_cols(d_wukv)], small], [True, False], "late_grads_start")
        return token


def kernel(x, positions, mix_norm, w_in, a_v_norm_g, a_v_norm_b, a_spatial_w, a_spatial_b, q_a_norm, w_uq, kv_a_norm, w_ukv, w_out, ffn_norm, w_up, conv_w, conv_b, w_down, final_norm, loss_target, m_mix_norm, m_w_in, m_a_v_norm_g, m_a_v_norm_b, m_a_spatial_w, m_a_spatial_b, m_q_a_norm, m_w_uq, m_kv_a_norm, m_w_ukv, m_w_out, m_ffn_norm, m_w_up, m_conv_w, m_conv_b, m_w_down, m_final_norm, v_mix_norm, v_w_in, v_a_v_norm_g, v_a_v_norm_b, v_a_spatial_w, v_a_spatial_b, v_q_a_norm, v_w_uq, v_kv_a_norm, v_w_ukv, v_w_out, v_ffn_norm, v_w_up, v_conv_w, v_conv_b, v_w_down, v_final_norm):
    names = ("mix_norm", "w_in", "a_v_norm_g", "a_v_norm_b", "a_spatial_w", "a_spatial_b", "q_a_norm", "w_uq",
             "kv_a_norm", "w_ukv", "w_out", "ffn_norm", "w_up", "conv_w", "conv_b", "w_down", "final_norm")
    w = dict(zip(names, (mix_norm, w_in, a_v_norm_g, a_v_norm_b, a_spatial_w, a_spatial_b, q_a_norm, w_uq,
                         kv_a_norm, w_ukv, w_out, ffn_norm, w_up, conv_w, conv_b, w_down, final_norm)))
    m = dict(zip(names, (m_mix_norm, m_w_in, m_a_v_norm_g, m_a_v_norm_b, m_a_spatial_w, m_a_spatial_b,
                         m_q_a_norm, m_w_uq, m_kv_a_norm, m_w_ukv, m_w_out, m_ffn_norm, m_w_up, m_conv_w,
                         m_conv_b, m_w_down, m_final_norm)))
    v = dict(zip(names, (v_mix_norm, v_w_in, v_a_v_norm_g, v_a_v_norm_b, v_a_spatial_w, v_a_spatial_b,
                         v_q_a_norm, v_w_uq, v_kv_a_norm, v_w_ukv, v_w_out, v_ffn_norm, v_w_up, v_conv_w,
                         v_conv_b, v_w_down, v_final_norm)))
    shapes = {n: w[n].shape for n in names}
    def view(tree, n):
        a = tree[n].reshape(tree[n].shape[-2:])
        return a.T if n in _Comm.TRANSPOSED else a

    comm = _Comm({n: view(w, n) for n in ("w_in",) + _Comm.GATHER_GROUPS[0] + _Comm.GATHER_GROUPS[1]})

    grad_x, token = _local_step(
        x, positions, loss_target, w["mix_norm"], w["a_v_norm_g"], w["a_v_norm_b"], w["a_spatial_w"][0],
        w["a_spatial_b"][0], w["q_a_norm"], w["kv_a_norm"], w["ffn_norm"], w["conv_b"],
        w["final_norm"].reshape(1, D_MODEL), comm)

    out_g, out_d, out_m, out_v = {}, {}, {}, {}

    def update(n, parts):
        res = _adamw(parts, view(w, n), view(m, n), view(v, n), "adamw_" + n)
        out_g[n], out_d[n], out_m[n], out_v[n] = (
            (t.T if n in _Comm.TRANSPOSED else t).reshape(shapes[n]) for t in res)
        return res[1]

    def update_small(names, parts, sums, name):
        res = _adamw_many(parts, *[[_small_2d(t[n]) for n in names] for t in (w, m, v)], sums, name)
        for i, n in enumerate(names):
            out_g[n], out_d[n], out_m[n], out_v[n] = (r[i].reshape(shapes[n]) for r in res[:4])
        return res

    for n, parts in zip(_Comm.FFN_GRADS, _wait_copies(comm.h_ffn_grads, True, token, "ffn_grads_wait")):
        last = update(n, parts)
    early = _wait_copies(comm.h_small_early, False, last, "small_grads_wait")
    last = update_small(SMALL_EARLY, early, [], "adamw_small")[1][0]
    last = update("w_in", _wait_copies(comm.h_in_grads, True, last, "in_grads_wait")[0])
    for n, parts in zip(_Comm.LATE_GRADS, _wait_copies(comm.h_late_grads, True, last, "late_grads_wait")):
        last = update(n, parts)
    late = _wait_copies(comm.h_late_small, False, last, "late_small_wait")
    res = update_small(SMALL_LATE, late[:-1], late[-1:], "adamw_late")
    loss = res[4][0][0, 0]

    return (loss, grad_x, *[out_g[n] for n in names], *[out_d[n] for n in names],
            *[out_m[n] for n in names], *[out_v[n] for n in names])
```

```python
import math

import jax
import jax.numpy as jnp
from jax import lax
from jax.experimental import pallas as pl
from jax.experimental.pallas import tpu as pltpu

F32 = jnp.float32
BF16 = jnp.bfloat16
KEPT = jnp.bfloat16

N_DEV = 8
D_MODEL = 1024
EPS = 1e-6
A_GROUPS = 8
CHUNK = 128
MLA_HEADS = 8
QK_NOPE = 128
QK_ROPE = 64
QK_HEAD = QK_NOPE + QK_ROPE
HEAD_PAD = 256
V_HEAD = 128
Q_LORA = 256
KV_LORA = 128
ROPE_THETA = 10000.0
D_FF = 2816
ZS_W = 512
ATTN_SCALE = QK_HEAD ** -0.5
ATTN_TILE = 512
NEG_BIG = -1e30

ADAM_LR = 0.001
ADAM_B1 = 0.9
ADAM_B2 = 0.999
ADAM_EPS = 1e-08
ADAM_WD = 0.01
ADAM_STEP = 10

VMEM_LIMIT = 56 * 1024 * 1024
SMALL_BLOCK_BYTES = 5 * 1024 * 1024
LANES = 128
SUBLANES = 8

GELU_K = math.sqrt(2.0 / math.pi)
GELU_C = 0.044715

ANY = pl.BlockSpec(memory_space=pl.ANY)
HBM = pl.BlockSpec(memory_space=pltpu.HBM)
SEM = pl.BlockSpec(memory_space=pltpu.SEMAPHORE)


def _tile(n, pref):
    for t in (pref, 512, 256, 128, 64, 32, 16, 8):
        if t <= pref and n % t == 0:
            return t
    return n


def _wide_tile(n, cap=1408):
    return next((t for t in range(min(n, cap) // LANES * LANES, 0, -LANES) if n % t == 0), n)


def _params(*sem):
    return pltpu.CompilerParams(dimension_semantics=sem, vmem_limit_bytes=VMEM_LIMIT)


def _dot(a, b):
    return jnp.dot(a, b, preferred_element_type=F32)


def _dot_nt(a, b):
    return lax.dot_general(a, b, (((1,), (1,)), ((), ())), preferred_element_type=F32)


def _dot_tn(a, b):
    return lax.dot_general(a, b, (((0,), (0,)), ((), ())), preferred_element_type=F32)


def _sigmoid(x):
    return 1.0 / (1.0 + jnp.exp(-x))


def _gelu(x):
    t = jnp.tanh(GELU_K * (x + GELU_C * x * x * x))
    return 0.5 * x * (1.0 + t)


def _gelu_and_grad(x):
    x2 = x * x
    t = jnp.tanh(GELU_K * (x + GELU_C * x * x2))
    half = 0.5 * (1.0 + t)
    return x * half, half + 0.5 * x * (1.0 - t * t) * GELU_K * (1.0 + 3.0 * GELU_C * x2)


def _in_proj(x, g, wt):
    T, Dm = x.shape
    tm = _tile(T, 512)

    def body(x_ref, g_ref, wt_ref, h_ref, zm_ref, zs_ref):
        xf = x_ref[...]
        r = lax.rsqrt(jnp.mean(xf * xf, axis=-1, keepdims=True) + EPS)
        h = (xf * r * g_ref[...]).astype(BF16)
        h_ref[...] = h
        for i, (r0, r1) in enumerate(IN_ROWS_MAIN):
            zm_ref[:, i * D_MODEL:(i + 1) * D_MODEL] = _dot_nt(h, wt_ref[r0:r1, :]).astype(KEPT)
        zs_ref[...] = _dot_nt(h, wt_ref[IN_ROWS_ZS[0]:IN_ROWS_ZS[1], :])

    row = lambda n: pl.BlockSpec((tm, n), lambda i: (i, 0))
    return pl.pallas_call(
        body, grid=(T // tm,),
        in_specs=[row(Dm), pl.BlockSpec((1, Dm), lambda i: (0, 0)), pl.BlockSpec(wt.shape, lambda i: (0, 0))],
        out_specs=[row(Dm), row(4 * D_MODEL), row(ZS_W)],
        out_shape=[jax.ShapeDtypeStruct((T, Dm), BF16), jax.ShapeDtypeStruct((T, 4 * D_MODEL), KEPT),
                   jax.ShapeDtypeStruct((T, ZS_W), F32)],
        name="in_proj", compiler_params=_params("parallel"))(x, g, wt)


def _proj_bwd(acts, wt, terms, x, g, dres, name, w2=None, dep=None, rows=256):
    T, Dm = x.shape
    tm = _tile(T, rows)
    n_a = len(acts)

    def body(*refs):
        ins, outs = refs[:n_a + 4 + (w2 is not None) + (dep is not None)], refs[-2 - (w2 is not None):]
        wt_ref, x_ref, g_ref, dres_ref = ins[n_a:n_a + 4]
        dx_ref, dg_ref = outs[0], outs[1]

        @pl.when(pl.program_id(0) == 0)
        def _():
            dg_ref[...] = jnp.zeros_like(dg_ref)

        dy = None
        for i, (c0, c1), (r0, r1) in terms:
            t = _dot(ins[i][:, c0:c1], wt_ref[r0:r1, :])
            dy = t if dy is None else dy + t
        xf = x_ref[...]
        r = lax.rsqrt(jnp.mean(xf * xf, axis=-1, keepdims=True) + EPS)
        xh = xf * r
        dg_ref[...] += jnp.sum(dy * xh, axis=0, keepdims=True)
        dxh = dy * g_ref[...]
        dx = dres_ref[...] + r * (dxh - xh * jnp.mean(dxh * xh, axis=-1, keepdims=True))
        dx_ref[...] = dx
        if w2 is not None:
            outs[2][...] = _dot_nt(dx.astype(BF16), ins[n_a + 4][...]).astype(KEPT)

    row = pl.BlockSpec((tm, Dm), lambda i: (i, 0))
    vec = pl.BlockSpec((1, Dm), lambda i: (0, 0))
    in_specs = [pl.BlockSpec((tm, a.shape[1]), lambda i: (i, 0)) for a in acts]
    in_specs += [pl.BlockSpec(wt.shape, lambda i: (0, 0)), row, vec, row]
    args = [*acts, wt, x, g, dres]
    out_specs = [row, vec]
    out_shape = [jax.ShapeDtypeStruct((T, Dm), F32), jax.ShapeDtypeStruct((1, Dm), F32)]
    if w2 is not None:
        in_specs.append(pl.BlockSpec(w2.shape, lambda i: (0, 0)))
        args.append(w2)
        out_specs.append(pl.BlockSpec((tm, w2.shape[0]), lambda i: (i, 0)))
        out_shape.append(jax.ShapeDtypeStruct((T, w2.shape[0]), KEPT))
    if dep is not None:
        in_specs.append(ANY)
        args.append(dep)
    return pl.pallas_call(
        body, grid=(T // tm,), in_specs=in_specs, out_specs=out_specs, out_shape=out_shape,
        name=name, compiler_params=_params("arbitrary"))(*args)


def _mm_tn(a, b, name, dep=None, rows=None, row0=0, into=None):
    T, M = a.shape
    N = b.shape[1]
    tm, tn, tt = _wide_tile(M), _wide_tile(N), _tile(T, 2048)
    n_t = T // tt
    off = row0 // tm
    extra = ([dep] if dep is not None else []) + ([into] if into is not None else [])

    def body(a_ref, b_ref, *refs):
        o_ref, acc_ref = refs[-2:]
        t = pl.program_id(2)

        @pl.when(t == 0)
        def _():
            acc_ref[...] = jnp.zeros_like(acc_ref)

        acc_ref[...] += _dot_tn(a_ref[...].astype(BF16), b_ref[...].astype(BF16))

        @pl.when(t == n_t - 1)
        def _():
            o_ref[...] = acc_ref[...].astype(BF16)

    return pl.pallas_call(
        body, grid=(M // tm, N // tn, n_t),
        in_specs=[pl.BlockSpec((tt, tm), lambda i, j, t: (t, i)),
                  pl.BlockSpec((tt, tn), lambda i, j, t: (t, j))] + [ANY] * len(extra),
        out_specs=pl.BlockSpec((tm, tn), lambda i, j, t: (i + off, j)),
        out_shape=jax.ShapeDtypeStruct((rows or M, N), BF16),
        scratch_shapes=[pltpu.VMEM((tm, tn), F32)],
        input_output_aliases={} if into is None else {1 + len(extra): 0},
        name=name, compiler_params=_params("parallel", "parallel", "arbitrary"))(a, b, *extra)


def _layer_norm_fwd(gv, g, b):
    mu = jnp.mean(gv, axis=-1, keepdims=True)
    xc = gv - mu
    rs = lax.rsqrt(jnp.mean(xc * xc, axis=-1, keepdims=True) + EPS)
    xh = xc * rs
    return xh, rs, xh * g + b


def _tri_mask(transposed=False):
    r = lax.broadcasted_iota(jnp.int32, (CHUNK, CHUNK), 0)
    c = lax.broadcasted_iota(jnp.int32, (CHUNK, CHUNK), 1)
    return r <= c if transposed else c <= r


def _mixer_a_fwd(zm, av_g, av_b, w_s, b_col):
    T = zm.shape[0]
    tm = _tile(T, 512)
    n_chunk = tm // CHUNK

    def body(u_ref, v_ref, ga_ref, g_ref, b_ref, w_ref, bc_ref, y_ref, vn_s, mx_s):
        gu = _gelu(u_ref[...].astype(F32))
        _, _, vn = _layer_norm_fwd(_gelu(v_ref[...].astype(F32)), g_ref[...], b_ref[...])
        vn_s[...] = vn.astype(BF16)
        tri = _tri_mask()
        for gi in range(A_GROUPS):
            wm = jnp.where(tri, w_ref[gi], 0.0).astype(BF16)
            cols = slice(gi * CHUNK, (gi + 1) * CHUNK)
            for n in range(n_chunk):
                rows = slice(n * CHUNK, (n + 1) * CHUNK)
                mx_s[rows, cols] = _dot(wm, vn_s[rows, cols]) + bc_ref[gi]
        y_ref[...] = (_sigmoid(ga_ref[...].astype(F32)) * gu * mx_s[...]).astype(KEPT)

    col = lambda c: pl.BlockSpec((tm, D_MODEL), lambda i: (i, c))
    vec = pl.BlockSpec((1, D_MODEL), lambda i: (0, 0))
    return pl.pallas_call(
        body, grid=(T // tm,),
        in_specs=[col(0), col(1), col(2), vec, vec,
                  pl.BlockSpec((A_GROUPS, CHUNK, CHUNK), lambda i: (0, 0, 0)),
                  pl.BlockSpec((A_GROUPS, CHUNK, 1), lambda i: (0, 0, 0))],
        out_specs=pl.BlockSpec((tm, D_MODEL), lambda i: (i, 0)),
        out_shape=jax.ShapeDtypeStruct((T, D_MODEL), KEPT),
        scratch_shapes=[pltpu.VMEM((tm, D_MODEL), BF16), pltpu.VMEM((tm, D_MODEL), F32)],
        name="mixer_a_fwd", compiler_params=_params("parallel"))(zm, zm, zm, av_g, av_b, w_s, b_col)


def _mixer_bwd(zm, o, dm, av_g, av_b, w_s, w_st, b_col, dep):
    T = zm.shape[0]
    tm = _tile(T, 256)
    n_chunk = tm // CHUNK

    def body(u_ref, v_ref, ga_ref, gb_ref, o_ref, dm_ref, g_ref, b_ref, w_ref, wt_ref, bc_ref, dep_ref,
             dz_ref, do_ref, dg_ref, db_ref, dw_ref, dbs_ref, vn_s, mx_s, dmx_s, dvn_s):
        @pl.when(pl.program_id(0) == 0)
        def _():
            dg_ref[...] = jnp.zeros_like(dg_ref)
            db_ref[...] = jnp.zeros_like(db_ref)
            dw_ref[...] = jnp.zeros_like(dw_ref)
            dbs_ref[...] = jnp.zeros_like(dbs_ref)

        dm_v = dm_ref[...].astype(F32)
        gb = gb_ref[...].astype(F32)
        sb = _sigmoid(gb)
        o_v = o_ref[...].astype(F32)
        do_ref[...] = (dm_v * sb).astype(BF16)
        dz_ref[:, 3 * D_MODEL:4 * D_MODEL] = (dm_v * o_v * sb * (1.0 - sb)).astype(BF16)
        u = u_ref[...].astype(F32)
        v = v_ref[...].astype(F32)
        gu, gu_grad = _gelu_and_grad(u)
        gv, gv_grad = _gelu_and_grad(v)
        xh, rs, vn = _layer_norm_fwd(gv, g_ref[...], b_ref[...])
        vn_s[...] = vn.astype(BF16)
        tri = _tri_mask()
        for gi in range(A_GROUPS):
            wm = jnp.where(tri, w_ref[gi], 0.0).astype(BF16)
            cols = slice(gi * CHUNK, (gi + 1) * CHUNK)
            for n in range(n_chunk):
                rows = slice(n * CHUNK, (n + 1) * CHUNK)
                mx_s[rows, cols] = _dot(wm, vn_s[rows, cols]) + bc_ref[gi]
        mixed = mx_s[...]
        sa = _sigmoid(ga_ref[...].astype(F32))
        dya = dm_v * sa
        dz_ref[:, 2 * D_MODEL:3 * D_MODEL] = (dm_v * gu * mixed * sa * (1.0 - sa)).astype(BF16)
        dz_ref[:, 0:D_MODEL] = (dya * mixed * gu_grad).astype(BF16)
        dmx = dya * gu
        dmx_s[...] = dmx.astype(BF16)
        tri_t = _tri_mask(transposed=True)
        for gi in range(A_GROUPS):
            wmt = jnp.where(tri_t, wt_ref[gi], 0.0).astype(BF16)
            cols = slice(gi * CHUNK, (gi + 1) * CHUNK)
            dw_acc = jnp.zeros((CHUNK, CHUNK), F32)
            dmx_sum = jnp.zeros((CHUNK, CHUNK), F32)
            for n in range(n_chunk):
                rows = slice(n * CHUNK, (n + 1) * CHUNK)
                blk = dmx_s[rows, cols]
                dvn_s[rows, cols] = _dot(wmt, blk)
                dw_acc = dw_acc + _dot_nt(blk, vn_s[rows, cols])
                dmx_sum = dmx_sum + dmx[rows, cols]
            dw_ref[gi] += jnp.where(tri, dw_acc, 0.0)
            dbs_ref[gi] += jnp.sum(dmx_sum, axis=-1, keepdims=True)
        dvn = dvn_s[...]
        dg_ref[...] += jnp.sum(dvn * xh, axis=0, keepdims=True)
        db_ref[...] += jnp.sum(dvn, axis=0, keepdims=True)
        dxh = dvn * g_ref[...]
        dgv = rs * (dxh - jnp.mean(dxh, axis=-1, keepdims=True)
                    - xh * jnp.mean(dxh * xh, axis=-1, keepdims=True))
        dz_ref[:, D_MODEL:2 * D_MODEL] = (dgv * gv_grad).astype(BF16)

    col = lambda c: pl.BlockSpec((tm, D_MODEL), lambda i: (i, c))
    row = pl.BlockSpec((tm, D_MODEL), lambda i: (i, 0))
    vec = pl.BlockSpec((1, D_MODEL), lambda i: (0, 0))
    wsp = pl.BlockSpec((A_GROUPS, CHUNK, CHUNK), lambda i: (0, 0, 0))
    bsp = pl.BlockSpec((A_GROUPS, CHUNK, 1), lambda i: (0, 0, 0))
    return pl.pallas_call(
        body, grid=(T // tm,),
        in_specs=[col(0), col(1), col(2), col(3), row, row, vec, vec, wsp, wsp, bsp, ANY],
        out_specs=[pl.BlockSpec((tm, 4 * D_MODEL), lambda i: (i, 0)), row, vec, vec, wsp, bsp],
        out_shape=[jax.ShapeDtypeStruct((T, 4 * D_MODEL), BF16), jax.ShapeDtypeStruct((T, D_MODEL), BF16),
                   jax.ShapeDtypeStruct((1, D_MODEL), F32), jax.ShapeDtypeStruct((1, D_MODEL), F32),
                   jax.ShapeDtypeStruct((A_GROUPS, CHUNK, CHUNK), F32),
                   jax.ShapeDtypeStruct((A_GROUPS, CHUNK, 1), F32)],
        scratch_shapes=[pltpu.VMEM((tm, D_MODEL), BF16), pltpu.VMEM((tm, D_MODEL), F32),
                        pltpu.VMEM((tm, D_MODEL), BF16), pltpu.VMEM((tm, D_MODEL), F32)],
        name="mixer_bwd", compiler_params=_params("arbitrary"))(
            zm, zm, zm, zm, o, dm, av_g, av_b, w_s, w_st, b_col, dep)


def _rope_tables(pos_ref, invf_ref):
    ang = pos_ref[...].astype(F32) * invf_ref[...]
    lane = lax.broadcasted_iota(jnp.int32, ang.shape, 1)
    cos, sin = jnp.cos(ang), jnp.sin(ang)
    c = jnp.where(lane < QK_ROPE, cos, 0.0)
    sa = jnp.where(lane < QK_ROPE // 2, -sin, 0.0)
    sb = jnp.where((lane >= QK_ROPE // 2) & (lane < QK_ROPE), sin, 0.0)
    return c, sa, sb


def _rope(blk, tabs):
    c, sa, sb = tabs
    return blk * c + pltpu.roll(blk, LANES - QK_ROPE // 2, 1) * sa + pltpu.roll(blk, QK_ROPE // 2, 1) * sb


def _rope_t(dout, tabs):
    c, sa, sb = tabs
    return dout * c + pltpu.roll(dout * sa, QK_ROPE // 2, 1) + pltpu.roll(dout * sb, LANES - QK_ROPE // 2, 1)


def _rms_small(x, g):
    r = lax.rsqrt(jnp.mean(x * x, axis=-1, keepdims=True) + EPS)
    xh = x * r
    return xh, r, xh * g


def _mla_prep_fwd(zs, pos, invf, qg, kvg, wuq_p, wukv):
    T = zs.shape[0]
    tm = _tile(T, 512)
    HW = MLA_HEADS * HEAD_PAD

    def body(zs_ref, pos_ref, invf_ref, qg_ref, kvg_ref, wq_ref, wkv_ref, q_ref, k_ref, v_ref):
        tabs = _rope_tables(pos_ref, invf_ref)
        _, _, cqn = _rms_small(zs_ref[:, 0:Q_LORA], qg_ref[...])
        _, _, ckvn = _rms_small(zs_ref[:, Q_LORA:Q_LORA + KV_LORA], kvg_ref[...])
        q = _dot_nt(cqn.astype(BF16), wq_ref[...]) * ATTN_SCALE
        kv = _dot(ckvn.astype(BF16), wkv_ref[...])
        kr = _rope(zs_ref[:, Q_LORA + KV_LORA:ZS_W], tabs).astype(BF16)
        for h in range(MLA_HEADS):
            b0 = h * HEAD_PAD
            q_ref[:, b0:b0 + QK_NOPE] = q[:, b0:b0 + QK_NOPE].astype(BF16)
            q_ref[:, b0 + QK_NOPE:b0 + HEAD_PAD] = _rope(q[:, b0 + QK_NOPE:b0 + HEAD_PAD], tabs).astype(BF16)
            k_ref[:, b0:b0 + QK_NOPE] = kv[:, b0:b0 + QK_NOPE].astype(BF16)
            k_ref[:, b0 + QK_NOPE:b0 + HEAD_PAD] = kr
            v_ref[:, h * V_HEAD:(h + 1) * V_HEAD] = kv[:, b0 + QK_NOPE:b0 + HEAD_PAD].astype(BF16)

    full = lambda a: pl.BlockSpec(a.shape, lambda i: (0,) * a.ndim)
    return pl.pallas_call(
        body, grid=(T // tm,),
        in_specs=[pl.BlockSpec((tm, ZS_W), lambda i: (i, 0)), pl.BlockSpec((tm, 1), lambda i: (i, 0)),
                  full(invf), full(qg), full(kvg), full(wuq_p), full(wukv)],
        out_specs=[pl.BlockSpec((tm, HW), lambda i: (i, 0)), pl.BlockSpec((tm, HW), lambda i: (i, 0)),
                   pl.BlockSpec((tm, D_MODEL), lambda i: (i, 0))],
        out_shape=[jax.ShapeDtypeStruct((T, HW), BF16), jax.ShapeDtypeStruct((T, HW), BF16),
                   jax.ShapeDtypeStruct((T, D_MODEL), BF16)],
        name="mla_prep_fwd", compiler_params=_params("parallel"))(zs, pos, invf, qg, kvg, wuq_p, wukv)


def _mla_prep_bwd(zs, pos, invf, qg, kvg, wuq_p, wukv, dq, dk, dv):
    T = zs.shape[0]
    tm = _tile(T, 512)
    n_t = T // tm
    HW = MLA_HEADS * HEAD_PAD

    def body(zs_ref, pos_ref, invf_ref, qg_ref, kvg_ref, wq_ref, wkv_ref, dq_ref, dk_ref, dv_ref,
             dzs_ref, dwq_ref, dwkv_ref, dqg_ref, dkvg_ref, dqp_ref, dkv_ref, accq_ref, acckv_ref):
        @pl.when(pl.program_id(0) == 0)
        def _():
            dqg_ref[...] = jnp.zeros_like(dqg_ref)
            dkvg_ref[...] = jnp.zeros_like(dkvg_ref)
            accq_ref[...] = jnp.zeros_like(accq_ref)
            acckv_ref[...] = jnp.zeros_like(acckv_ref)

        tabs = _rope_tables(pos_ref, invf_ref)
        cqh, rq, cqn = _rms_small(zs_ref[:, 0:Q_LORA], qg_ref[...])
        ckvh, rkv, ckvn = _rms_small(zs_ref[:, Q_LORA:Q_LORA + KV_LORA], kvg_ref[...])
        dkr = jnp.zeros((tm, LANES), F32)
        for h in range(MLA_HEADS):
            b0 = h * HEAD_PAD
            dqp_ref[:, b0:b0 + QK_NOPE] = dq_ref[:, b0:b0 + QK_NOPE]
            dqp_ref[:, b0 + QK_NOPE:b0 + HEAD_PAD] = _rope_t(
                dq_ref[:, b0 + QK_NOPE:b0 + HEAD_PAD].astype(F32), tabs).astype(BF16)
            dkv_ref[:, b0:b0 + QK_NOPE] = dk_ref[:, b0:b0 + QK_NOPE]
            dkv_ref[:, b0 + QK_NOPE:b0 + HEAD_PAD] = dv_ref[:, h * V_HEAD:(h + 1) * V_HEAD]
            dkr = dkr + dk_ref[:, b0 + QK_NOPE:b0 + HEAD_PAD].astype(F32)
        accq_ref[...] += _dot_tn(dqp_ref[...], cqn.astype(BF16))
        acckv_ref[...] += _dot_tn(ckvn.astype(BF16), dkv_ref[...])

        @pl.when(pl.program_id(0) == n_t - 1)
        def _():
            dwq_ref[...] = accq_ref[...].astype(BF16)
            dwkv_ref[...] = acckv_ref[...].astype(BF16)

        dcqn = _dot(dqp_ref[...], wq_ref[...])
        dckvn = _dot_nt(dkv_ref[...], wkv_ref[...])
        dqg_ref[...] += jnp.sum(dcqn * cqh, axis=0, keepdims=True)
        dkvg_ref[...] += jnp.sum(dckvn * ckvh, axis=0, keepdims=True)
        dxh = dcqn * qg_ref[...]
        dzs_ref[:, 0:Q_LORA] = (rq * (dxh - cqh * jnp.mean(dxh * cqh, axis=-1, keepdims=True))).astype(BF16)
        dxh = dckvn * kvg_ref[...]
        dzs_ref[:, Q_LORA:Q_LORA + KV_LORA] = (
            rkv * (dxh - ckvh * jnp.mean(dxh * ckvh, axis=-1, keepdims=True))).astype(BF16)
        dzs_ref[:, Q_LORA + KV_LORA:ZS_W] = _rope_t(dkr, tabs).astype(BF16)

    full = lambda a: pl.BlockSpec(a.shape, lambda i: (0,) * a.ndim)
    rowb = lambda w: pl.BlockSpec((tm, w), lambda i: (i, 0))
    return pl.pallas_call(
        body, grid=(T // tm,),
        in_specs=[rowb(ZS_W), rowb(1), full(invf), full(qg), full(kvg), full(wuq_p), full(wukv),
                  rowb(HW), rowb(HW), rowb(D_MODEL)],
        out_specs=[rowb(ZS_W), full(wuq_p), full(wukv), full(qg), full(kvg)],
        out_shape=[jax.ShapeDtypeStruct((T, ZS_W), BF16), jax.ShapeDtypeStruct(wuq_p.shape, BF16),
                   jax.ShapeDtypeStruct(wukv.shape, BF16), jax.ShapeDtypeStruct(qg.shape, F32),
                   jax.ShapeDtypeStruct(kvg.shape, F32)],
        scratch_shapes=[pltpu.VMEM((tm, HW), BF16), pltpu.VMEM((tm, HW), BF16),
                        pltpu.VMEM(wuq_p.shape, F32), pltpu.VMEM(wukv.shape, F32)],
        name="mla_prep_bwd", compiler_params=_params("arbitrary"))(
            zs, pos, invf, qg, kvg, wuq_p, wukv, dq, dk, dv)


def _causal(tq, kmax, q0):
    r = lax.broadcasted_iota(jnp.int32, (tq, kmax), 0) + q0
    c = lax.broadcasted_iota(jnp.int32, (tq, kmax), 1)
    return c <= r


def _attn_fwd(q, k, v, batch, seq):
    tq = _tile(seq, ATTN_TILE)
    nq = seq // tq

    def body(q_ref, k_ref, v_ref, o_ref, lse_ref):
        diag = _causal(tq, tq, 0)
        for qi in range(nq):
            rows = slice(qi * tq, (qi + 1) * tq)
            qr = q_ref[rows, :]
            s_d = jnp.where(diag, _dot_nt(qr, k_ref[rows, :]), NEG_BIG)
            m = jnp.max(s_d, axis=-1, keepdims=True)
            if qi > 0:
                before = slice(0, qi * tq)
                s_b = _dot_nt(qr, k_ref[before, :])
                m = jnp.maximum(m, jnp.max(s_b, axis=-1, keepdims=True))
                p_b = jnp.exp(s_b - m)
                l = jnp.sum(p_b, axis=-1, keepdims=True)
                acc = _dot(p_b.astype(BF16), v_ref[before, :])
            p_d = jnp.exp(s_d - m)
            l_d = jnp.sum(p_d, axis=-1, keepdims=True)
            acc_d = _dot(p_d.astype(BF16), v_ref[rows, :])
            l, acc = (l + l_d, acc + acc_d) if qi > 0 else (l_d, acc_d)
            o_ref[rows, :] = (acc / l).astype(KEPT)
            lse_ref[rows, :] = jnp.broadcast_to(m + jnp.log(l), (tq, V_HEAD))

    return pl.pallas_call(
        body, grid=(batch, MLA_HEADS),
        in_specs=[pl.BlockSpec((seq, HEAD_PAD), lambda b, h: (b, h)),
                  pl.BlockSpec((seq, HEAD_PAD), lambda b, h: (b, h)),
                  pl.BlockSpec((seq, V_HEAD), lambda b, h: (b, h))],
        out_specs=[pl.BlockSpec((seq, V_HEAD), lambda b, h: (b, h)),
                   pl.BlockSpec((seq, V_HEAD), lambda b, h: (b, h))],
        out_shape=[jax.ShapeDtypeStruct((batch * seq, D_MODEL), KEPT),
                   jax.ShapeDtypeStruct((batch * seq, D_MODEL), F32)],
        name="attn_fwd", compiler_params=_params("parallel", "parallel"))(q, k, v)


def _attn_bwd(q, k, v, o, do, lse, batch, seq, dep):
    tq = _tile(seq, ATTN_TILE)
    nq = seq // tq

    def body(q_ref, k_ref, v_ref, o_ref, do_ref, lse_ref, dep_ref, dq_ref, dk_ref, dv_ref, dk_acc, dv_acc):
        dk_acc[...] = jnp.zeros_like(dk_acc)
        dv_acc[...] = jnp.zeros_like(dv_acc)
        for qi in range(nq):
            rows = slice(qi * tq, (qi + 1) * tq)
            kmax = (qi + 1) * tq
            qr = q_ref[rows, :]
            dor = do_ref[rows, :]
            kk = k_ref[0:kmax, :]
            s = _dot_nt(qr, kk)
            p = jnp.where(_causal(tq, kmax, qi * tq), jnp.exp(s - lse_ref[rows, 0:1]), 0.0)
            dp = _dot_nt(dor, v_ref[0:kmax, :])
            delta = jnp.sum(dor.astype(F32) * o_ref[rows, :].astype(F32), axis=-1, keepdims=True)
            ds = (p * (dp - delta)).astype(BF16)
            dq_ref[rows, :] = (_dot(ds, kk) * ATTN_SCALE).astype(BF16)
            dk_acc[0:kmax, :] += _dot_tn(ds, qr)
            dv_acc[0:kmax, :] += _dot_tn(p.astype(BF16), dor)
        dk_ref[...] = dk_acc[...].astype(BF16)
        dv_ref[...] = dv_acc[...].astype(BF16)

    qspec = pl.BlockSpec((seq, HEAD_PAD), lambda b, h: (b, h))
    vspec = pl.BlockSpec((seq, V_HEAD), lambda b, h: (b, h))
    T = batch * seq
    return pl.pallas_call(
        body, grid=(batch, MLA_HEADS),
        in_specs=[qspec, qspec, vspec, vspec, vspec, vspec, ANY],
        out_specs=[qspec, qspec, vspec],
        out_shape=[jax.ShapeDtypeStruct((T, MLA_HEADS * HEAD_PAD), BF16),
                   jax.ShapeDtypeStruct((T, MLA_HEADS * HEAD_PAD), BF16),
                   jax.ShapeDtypeStruct((T, D_MODEL), BF16)],
        scratch_shapes=[pltpu.VMEM((seq, HEAD_PAD), F32), pltpu.VMEM((seq, V_HEAD), F32)],
        name="attn_bwd", compiler_params=_params("parallel", "parallel"))(q, k, v, o, do, lse, dep)


def _merge_out(x, yag, zm, o, w_out, ffn_g):
    T = x.shape[0]
    tm = _tile(T, 512)

    def body(x_ref, ya_ref, gb_ref, o_ref, w_ref, g_ref, mg_ref, x1_ref, h2_ref):
        mg = (ya_ref[...].astype(F32) + _sigmoid(gb_ref[...].astype(F32)) * o_ref[...].astype(F32)).astype(BF16)
        mg_ref[...] = mg
        x1 = x_ref[...] + _dot(mg, w_ref[...])
        x1_ref[...] = x1
        r = lax.rsqrt(jnp.mean(x1 * x1, axis=-1, keepdims=True) + EPS)
        h2_ref[...] = (x1 * r * g_ref[...]).astype(BF16)

    row = pl.BlockSpec((tm, D_MODEL), lambda i: (i, 0))
    return pl.pallas_call(
        body, grid=(T // tm,),
        in_specs=[row, row, pl.BlockSpec((tm, D_MODEL), lambda i: (i, 3)), row,
                  pl.BlockSpec((D_MODEL, D_MODEL), lambda i: (0, 0)), pl.BlockSpec((1, D_MODEL), lambda i: (0, 0))],
        out_specs=[row, row, row],
        out_shape=[jax.ShapeDtypeStruct((T, D_MODEL), BF16), jax.ShapeDtypeStruct((T, D_MODEL), F32),
                   jax.ShapeDtypeStruct((T, D_MODEL), BF16)],
        name="merge_out", compiler_params=_params("parallel"))(x, yag, zm, o, w_out, ffn_g)


FF_TILE = 256
FF_BLOCKS = D_FF // FF_TILE
FFB_TILE = 256
UP_ROWS = 512
EDGE = 16


def _shift_up(x, k):
    n = x.shape[0]
    row = lax.broadcasted_iota(jnp.int32, x.shape, 0)
    return jnp.where(row < n - k, pltpu.roll(x, n - k, 0), 0.0)


def _up_act(h2, wt_up, cw, cb, batch, seq):
    def body(h_ref, wug_ref, wuv_ref, wg_ref, wv_ref, bg_ref, bv_ref, ug_ref, uv_ref, g_ref, v_ref, a_ref,
             ug_s, uv_s):
        for s in (ug_s, uv_s):
            s[0:SUBLANES, :] = jnp.zeros((SUBLANES, FF_TILE), F32)

        def conv(s, w_ref, b_ref, r0):
            return (b_ref[...] + w_ref[2:3, :] * s[r0:r0 + UP_ROWS, :]
                    + w_ref[1:2, :] * s[r0 - 1:r0 - 1 + UP_ROWS, :]
                    + w_ref[0:1, :] * s[r0 - 2:r0 - 2 + UP_ROWS, :])

        for c in range(seq // UP_ROWS):
            rows = slice(c * UP_ROWS, (c + 1) * UP_ROWS)
            r0 = SUBLANES + c * UP_ROWS
            h = h_ref[rows, :]
            for w_ref, u_ref, s in ((wug_ref, ug_ref, ug_s), (wuv_ref, uv_ref, uv_s)):
                u = _dot_nt(h, w_ref[...])
                u_ref[rows, :] = u.astype(KEPT)
                s[r0:r0 + UP_ROWS, :] = u
            gate, val = conv(ug_s, wg_ref, bg_ref, r0), conv(uv_s, wv_ref, bv_ref, r0)
            g_ref[rows, :] = gate.astype(KEPT)
            v_ref[rows, :] = val.astype(KEPT)
            a_ref[rows, :] = (gate * _sigmoid(gate) * val).astype(BF16)

    blk = pl.BlockSpec((seq, FF_TILE), lambda b, j: (b, j))
    wup = lambda off: pl.BlockSpec((FF_TILE, D_MODEL), lambda b, j: (j + off, 0))
    wsp = lambda off: pl.BlockSpec((3, FF_TILE), lambda b, j: (0, j + off))
    bsp = lambda off: pl.BlockSpec((1, FF_TILE), lambda b, j: (0, j + off))
    T = batch * seq
    kept = jax.ShapeDtypeStruct((T, D_FF), KEPT)
    return pl.pallas_call(
        body, grid=(batch, FF_BLOCKS),
        in_specs=[pl.BlockSpec((seq, D_MODEL), lambda b, j: (b, 0)), wup(0), wup(FF_BLOCKS),
                  wsp(0), wsp(FF_BLOCKS), bsp(0), bsp(FF_BLOCKS)],
        out_specs=[blk] * 5,
        out_shape=[kept, kept, kept, kept, jax.ShapeDtypeStruct((T, D_FF), BF16)],
        scratch_shapes=[pltpu.VMEM((SUBLANES + seq, FF_TILE), F32)] * 2,
        name="up_act", compiler_params=_params("parallel", "arbitrary"))(h2, wt_up, wt_up, cw, cw, cb, cb)


def _ffn_act_bwd(upg, upv, gate, val, cw, dx2b, w_down, batch, seq):
    def half(du, x, w_ref, dx_ref, dw_ref):
        j = pl.program_id(1)
        n = du.shape[0]
        up1, up2 = pltpu.roll(du, n - 1, 0), pltpu.roll(du, n - 2, 0)
        dx_ref[...] = (w_ref[2:3, :] * du + w_ref[1:2, :] * up1 + w_ref[0:1, :] * up2).astype(BF16)
        tail = du[n - EDGE:n]
        dx_ref[n - EDGE:n, :] = (w_ref[2:3, :] * tail + w_ref[1:2, :] * _shift_up(tail, 1)
                                 + w_ref[0:1, :] * _shift_up(tail, 2)).astype(BF16)
        row = lax.broadcasted_iota(jnp.int32, (EDGE, du.shape[1]), 0)
        head, x_tail = du[0:EDGE], x[n - EDGE:n]
        wrap1 = jnp.sum(jnp.where(row >= EDGE - 1, pltpu.roll(head, EDGE - 1, 0), 0.0) * x_tail, axis=0, keepdims=True)
        wrap2 = jnp.sum(jnp.where(row >= EDGE - 2, pltpu.roll(head, EDGE - 2, 0), 0.0) * x_tail, axis=0, keepdims=True)
        dw_ref[j, 2:3, :] += jnp.sum(du * x, axis=0, keepdims=True)
        dw_ref[j, 1:2, :] += jnp.sum(up1 * x, axis=0, keepdims=True) - wrap1
        dw_ref[j, 0:1, :] += jnp.sum(up2 * x, axis=0, keepdims=True) - wrap2
        dw_ref[j, 3:4, :] += jnp.sum(du, axis=0, keepdims=True)

    def body(ug_ref, uv_ref, g_ref, v_ref, wg_ref, wv_ref, dx_ref, wd_ref, dg_ref, dv_ref, dwg_ref, dwv_ref):
        @pl.when((pl.program_id(0) == 0) & (pl.program_id(1) == 0))
        def _():
            dwg_ref[...] = jnp.zeros_like(dwg_ref)
            dwv_ref[...] = jnp.zeros_like(dwv_ref)

        gate, val = g_ref[...].astype(F32), v_ref[...].astype(F32)
        sg = _sigmoid(gate)
        dav = _dot_nt(dx_ref[...], wd_ref[...])
        half(dav * val * sg * (1.0 + gate * (1.0 - sg)), ug_ref[...].astype(F32), wg_ref, dg_ref, dwg_ref)
        half(dav * gate * sg, uv_ref[...].astype(F32), wv_ref, dv_ref, dwv_ref)

    nb = D_FF // FFB_TILE
    blk = pl.BlockSpec((seq, FFB_TILE), lambda b, j: (b, j))
    wsp = lambda off: pl.BlockSpec((3, FFB_TILE), lambda b, j: (0, j + off))
    acc = pl.BlockSpec((nb, 4, FFB_TILE), lambda b, j: (0, 0, 0))
    T = batch * seq
    dupg, dupv, dwg, dwv = pl.pallas_call(
        body, grid=(batch, nb),
        in_specs=[blk, blk, blk, blk, wsp(0), wsp(nb),
                  pl.BlockSpec((seq, D_MODEL), lambda b, j: (b, 0)),
                  pl.BlockSpec((FFB_TILE, D_MODEL), lambda b, j: (j, 0))],
        out_specs=[blk, blk, acc, acc],
        out_shape=[jax.ShapeDtypeStruct((T, D_FF), BF16), jax.ShapeDtypeStruct((T, D_FF), BF16),
                   jax.ShapeDtypeStruct((nb, 4, FFB_TILE), F32), jax.ShapeDtypeStruct((nb, 4, FFB_TILE), F32)],
        name="ffn_act_bwd", compiler_params=_params("arbitrary", "arbitrary"))(
            upg, upv, gate, val, cw, cw, dx2b, w_down)
    dwg, dwv = (jnp.transpose(a, (1, 0, 2)).reshape(4, D_FF) for a in (dwg, dwv))
    return dupg, dupv, dwg[:3], dwv[:3], dwg[3:], dwv[3:]


def _down_loss(a, w_down, x1, target, gfin):
    T = x1.shape[0]
    tm = _tile(T, 512)

    def body(a_ref, w_ref, x1_ref, t_ref, g_ref, dx_ref, dxb_ref, loss_ref, dg_ref):
        @pl.when(pl.program_id(0) == 0)
        def _():
            loss_ref[...] = jnp.zeros_like(loss_ref)
            dg_ref[...] = jnp.zeros_like(dg_ref)

        x2 = x1_ref[...] + _dot(a_ref[...], w_ref[...])
        r = lax.rsqrt(jnp.mean(x2 * x2, axis=-1, keepdims=True) + EPS)
        xh = x2 * r
        g = g_ref[...]
        diff = xh * g - t_ref[...]
        loss_ref[...] += 0.5 * jnp.sum(jnp.mean(diff * diff, axis=-1, keepdims=True))
        dy = diff * (1.0 / D_MODEL)
        dg_ref[...] += jnp.sum(dy * xh, axis=0, keepdims=True)
        dxh = dy * g
        dx = r * (dxh - xh * jnp.mean(dxh * xh, axis=-1, keepdims=True))
        dx_ref[...] = dx
        dxb_ref[...] = dx.astype(BF16)

    row = pl.BlockSpec((tm, D_MODEL), lambda i: (i, 0))
    vec = pl.BlockSpec((1, D_MODEL), lambda i: (0, 0))
    return pl.pallas_call(
        body, grid=(T // tm,),
        in_specs=[pl.BlockSpec((tm, D_FF), lambda i: (i, 0)),
                  pl.BlockSpec((D_FF, D_MODEL), lambda i: (0, 0)), row, row, vec],
        out_specs=[row, row, pl.BlockSpec((8, LANES), lambda i: (0, 0)), vec],
        out_shape=[jax.ShapeDtypeStruct((T, D_MODEL), F32), jax.ShapeDtypeStruct((T, D_MODEL), BF16),
                   jax.ShapeDtypeStruct((8, LANES), F32), jax.ShapeDtypeStruct((1, D_MODEL), F32)],
        name="down_loss", compiler_params=_params("arbitrary"))(a, w_down, x1, target, gfin)


def _local_step(x, positions, target, mix_norm, av_g, av_b, w_s, b_s, q_norm, kv_norm, ffn_norm, conv_b,
                final_norm, comm):
    batch, seq, _ = x.shape
    T = batch * seq
    x = x.reshape(T, D_MODEL)
    target = target.reshape(T, D_MODEL)
    pos = positions.reshape(T, 1)
    half = jnp.arange(0, QK_ROPE, 2, dtype=F32) / QK_ROPE
    inv_freq = 1.0 / (ROPE_THETA ** half)
    invf = jnp.concatenate([inv_freq, inv_freq, jnp.zeros((LANES - QK_ROPE,), F32)]).reshape(1, LANES)
    w_st = jnp.swapaxes(w_s, 1, 2)
    b_col = b_s.reshape(A_GROUPS, CHUNK, 1)

    wt_in = comm.in_weights()
    h, zm, zs = _in_proj(x, mix_norm, wt_in)
    yag = _mixer_a_fwd(zm, av_g, av_b, w_s, b_col)
    wuq_p, wukv, w_out = comm.mla_weights(after=yag)
    q, k, v = _mla_prep_fwd(zs, pos, invf, q_norm, kv_norm, wuq_p, wukv)
    o, lse = _attn_fwd(q, k, v, batch, seq)
    merged, x1, h2 = _merge_out(x, yag, zm, o, w_out, ffn_norm)
    wt_up, conv_w, w_down = comm.ffn_weights(after=merged)
    upg, upv, gate, val, act = _up_act(h2, wt_up, conv_w, conv_b, batch, seq)
    dx2, dx2b, loss_acc, d_final = _down_loss(act, w_down, x1, target, final_norm)

    d_wdown = _mm_tn(act, dx2b, "dw_down")
    dupg, dupv, dcwg, dcwv, dcbg, dcbv = _ffn_act_bwd(upg, upv, gate, val, conv_w, dx2b, w_down, batch, seq)
    d_wt_up = _mm_tn(dupv, h2, "dw_up_val", rows=2 * D_FF, row0=D_FF,
                     into=_mm_tn(dupg, h2, "dw_up_gate", rows=2 * D_FF))
    dx1, d_ffn_norm, dmerged = _proj_bwd(
        [dupg, dupv], wt_up, [(0, (0, D_FF), (0, D_FF)), (1, (0, D_FF), (D_FF, 2 * D_FF))],
        x1, ffn_norm, dx2, "up_proj_bwd", w2=w_out)
    d_wout = _mm_tn(merged, dx1, "dw_out")
    token = comm.send_ffn_grads(d_wdown, d_wt_up, jnp.concatenate([dcwg, dcwv], axis=1), d_wout)
    dzm, do, d_avg, d_avb, d_ws, d_bs = _mixer_bwd(zm, o, dmerged, av_g, av_b, w_s, w_st, b_col, token)
    token = comm.send_small_grads([
        d_avg, d_avb, _small_2d(d_ws), d_bs.reshape(A_GROUPS, CHUNK), d_ffn_norm,
        jnp.concatenate([dcbg, dcbv], axis=1), d_final])
    d_wt_main = _mm_tn(dzm, h, "dw_in_main", dep=token)
    token = comm.send_in_grads(d_wt_main)
    dq, dk, dv = _attn_bwd(q, k, v, o, do, lse, batch, seq, token)
    dzs, d_wuq_p, d_wukv, d_qn, d_kvn = _mla_prep_bwd(zs, pos, invf, q_norm, kv_norm, wuq_p, wukv, dq, dk, dv)
    d_wt_zs = _mm_tn(dzs, h, "dw_in_small")
    terms = [(0, (i * D_MODEL, (i + 1) * D_MODEL), rows) for i, rows in enumerate(IN_ROWS_MAIN)]
    terms.append((1, (0, ZS_W), IN_ROWS_ZS))
    dx, d_mix_norm = _proj_bwd([dzm, dzs], wt_in, terms, x, mix_norm, dx1, "in_proj_bwd", rows=512)
    token = comm.send_late_grads(d_wuq_p, d_wukv, d_wt_zs, [d_qn, d_kvn, d_mix_norm, loss_acc])
    return dx.reshape(batch, seq, D_MODEL), token


MESH_ID = pl.DeviceIdType.MESH
EFFECT = pltpu.SideEffectType.DATAFLOW_SIDE_EFFECTING


def _mesh_pos():
    return lax.axis_index("x"), lax.axis_index("y"), lax.axis_index("c")


def _peer(pos, d):
    x, y, c = pos
    px = 1 - x if d & 4 else x
    py = 1 - y if d & 2 else y
    pc = 1 - c if d & 1 else c
    return (px, py, pc), 4 * px + 2 * py + pc


def _copy(src_ref, land_ref, send_sems, recv_sems, a, d, pos, exchange, landing_here):
    peer, pid = _peer(pos, d)
    me = 4 * pos[0] + 2 * pos[1] + pos[2]
    if exchange:
        src, dst = src_ref.at[pid], land_ref.at[d]
    else:
        src, dst = src_ref, land_ref.at[pid if landing_here else me]
    return pltpu.make_async_remote_copy(
        src_ref=src, dst_ref=dst, send_sem=send_sems.at[a * (N_DEV - 1) + d - 1],
        recv_sem=recv_sems.at[a * (N_DEV - 1) + d - 1],
        device_id=peer, device_id_type=MESH_ID)


def _start_copies(groups, modes, name, dep=None):
    sizes = [len(g) for g in groups]
    srcs = [s for g in groups for s in g]
    lands = [lax.empty(s.shape if modes[gi] else (N_DEV,) + s.shape, s.dtype)
             for gi, g in enumerate(groups) for s in g]
    n, ng = len(srcs), len(groups)
    n_in = 2 * n + (dep is not None)

    def body(*refs):
        src_refs, land_refs = refs[:n], refs[n:2 * n]
        sems = refs[n_in:n_in + 3 * ng]
        token = refs[-1]
        pos = _mesh_pos()
        k = 0
        for gi, size in enumerate(sizes):
            for a in range(size):
                _own_copy(src_refs[k], land_refs[k], sems[3 * gi + 2], a, pos, modes[gi]).start()
                for d in range(1, N_DEV):
                    _copy(src_refs[k], land_refs[k], sems[3 * gi], sems[3 * gi + 1], a, d, pos, modes[gi],
                          landing_here=False).start()
                k += 1
        token[...] = jnp.zeros_like(token)

    sem_shapes = []
    for size in sizes:
        remote = pltpu.SemaphoreType.DMA((size * (N_DEV - 1),))
        sem_shapes += [remote, remote, pltpu.SemaphoreType.DMA((size,))]
    out = pl.pallas_call(
        body, name=name,
        out_shape=(*sem_shapes, *[pltpu.HBM(a.shape, a.dtype) for a in srcs + lands],
                   jax.ShapeDtypeStruct((8, LANES), F32)),
        in_specs=[HBM] * (2 * n) + [ANY] * (dep is not None),
        out_specs=(*[SEM] * (3 * ng), *[HBM] * (2 * n), pl.BlockSpec(memory_space=pltpu.VMEM)),
        input_output_aliases={i: 3 * ng + i for i in range(2 * n)},
        compiler_params=pltpu.CompilerParams(has_side_effects=EFFECT),
    )(*[pltpu.with_memory_space_constraint(a, pltpu.HBM) for a in srcs + lands], *([dep] if dep is not None else []))
    thru = out[3 * ng:3 * ng + 2 * n]
    handles, k = [], 0
    for gi, size in enumerate(sizes):
        handles.append((out[3 * gi:3 * gi + 3], thru[k:k + size], thru[n + k:n + k + size]))
        k += size
    return handles, out[-1]


def _own_copy(src_ref, land_ref, local_sems, a, pos, exchange):
    me = 4 * pos[0] + 2 * pos[1] + pos[2]
    src, dst = (src_ref.at[me], land_ref.at[0]) if exchange else (src_ref, land_ref.at[me])
    return pltpu.make_async_copy(src, dst, local_sems.at[a])


def _wait_copies(handle, exchange, after, name):
    sems, srcs, lands = handle
    n = len(srcs)

    def body(*refs):
        src_refs, land_refs = refs[:n], refs[n:2 * n]
        send, recv, local = refs[2 * n:2 * n + 3]
        pos = _mesh_pos()
        for a in range(n):
            _own_copy(src_refs[a], land_refs[a], local, a, pos, exchange).wait()
            for d in range(1, N_DEV):
                cp = _copy(src_refs[a], land_refs[a], send, recv, a, d, pos, exchange, landing_here=True)
                cp.wait_send()
                cp.wait_recv()

    out = pl.pallas_call(
        body, name=name,
        out_shape=tuple(pltpu.HBM(a.shape, a.dtype) for a in (*srcs, *lands)),
        in_specs=[HBM] * (2 * n) + [SEM, SEM, SEM, ANY], out_specs=[HBM] * (2 * n),
        input_output_aliases={i: i for i in range(2 * n)},
        compiler_params=pltpu.CompilerParams(has_side_effects=EFFECT),
    )(*srcs, *lands, *sems, after)
    return out[n:]


def _gather_now(a, name):
    def body(x_ref, out_ref, send_sems, recv_sems, local_sem):
        x, y, c = _mesh_pos()
        me, sibling = (x, y, c), (x, y, 1 - c)
        chips = [(1 - x, y), (x, 1 - y), (1 - x, 1 - y)]

        def slot(p):
            return out_ref.at[4 * p[0] + 2 * p[1] + p[2]]

        def copy(k, block, to, src=None):
            return pltpu.make_async_remote_copy(
                src_ref=slot(block) if src is None else src, dst_ref=slot(block), send_sem=send_sems.at[k],
                recv_sem=recv_sems.at[k], device_id=to, device_id_type=MESH_ID)

        mine = pltpu.make_async_copy(x_ref, slot(me), local_sem)
        mine.start()
        first = [copy(0, me, sibling, src=x_ref)]
        first += [copy(1 + j, me, (*chip, c), src=x_ref) for j, chip in enumerate(chips)]
        for cp in first:
            cp.start()
        passed = [copy(4 + j, (*chip, c), sibling) for j, chip in enumerate(chips)]
        for j, chip in enumerate(chips):
            copy(1 + j, (*chip, c), me).wait_recv()
            passed[j].start()
        copy(0, sibling, me).wait_recv()
        for j, chip in enumerate(chips):
            copy(4 + j, (*chip, 1 - c), me).wait_recv()
        for cp in first + passed:
            cp.wait_send()
        mine.wait()

    return pl.pallas_call(
        body, in_specs=[ANY], out_specs=ANY,
        out_shape=jax.ShapeDtypeStruct((N_DEV,) + a.shape, a.dtype),
        scratch_shapes=[pltpu.SemaphoreType.DMA((N_DEV - 1,)), pltpu.SemaphoreType.DMA((N_DEV - 1,)),
                        pltpu.SemaphoreType.DMA],
        name=name, compiler_params=pltpu.CompilerParams(has_side_effects=True))(a)


def _sum_parts(p_ref):
    g = p_ref[0].astype(F32)
    for k in range(1, N_DEV):
        g = g + p_ref[k].astype(F32)
    return g


def _adamw_update(p_ref, w_ref, m_ref, v_ref, g_ref, d_ref, nm_ref, nv_ref, patch=None):
    c1 = 1.0 - ADAM_B1 ** ADAM_STEP
    c2 = 1.0 - ADAM_B2 ** ADAM_STEP
    g = _sum_parts(p_ref)
    if patch is not None:
        g = patch(g)
    nm = ADAM_B1 * m_ref[...] + (1.0 - ADAM_B1) * g
    nv = ADAM_B2 * v_ref[...] + (1.0 - ADAM_B2) * (g * g)
    g_ref[...] = g
    nm_ref[...] = nm
    nv_ref[...] = nv
    d_ref[...] = -ADAM_LR * ((nm / c1) / (jnp.sqrt(nv / c2) + ADAM_EPS) + ADAM_WD * w_ref[...])


def _adamw_many(parts, ws, ms, vs, sums, name):
    n, ns = len(ws), len(sums)

    def body(*refs):
        ins, outs = refs[:4 * n + ns], refs[4 * n + ns:]
        for i in range(n):
            _adamw_update(ins[i], ins[n + i], ins[2 * n + i], ins[3 * n + i],
                          outs[i], outs[n + i], outs[2 * n + i], outs[3 * n + i])
        for i in range(ns):
            outs[4 * n + i][...] = _sum_parts(ins[4 * n + i])

    full = lambda a: pl.BlockSpec(a.shape, lambda: (0,) * a.ndim)
    args = [*parts, *ws, *ms, *vs, *sums]
    outs = [jax.ShapeDtypeStruct(w.shape, F32) for _ in range(4) for w in ws]
    outs += [jax.ShapeDtypeStruct(s.shape[1:], F32) for s in sums]
    res = pl.pallas_call(
        body, in_specs=[full(a) for a in args], out_specs=[full(o) for o in outs], out_shape=outs,
        name=name, compiler_params=pltpu.CompilerParams(vmem_limit_bytes=VMEM_LIMIT))(*args)
    return res[:n], res[n:2 * n], res[2 * n:3 * n], res[3 * n:4 * n], res[4 * n:]


def _adamw(parts, w, m, v, name, late=None):
    R, C = w.shape
    tr, tc = R, C
    if N_DEV * R * C * parts.dtype.itemsize > SMALL_BLOCK_BYTES:
        tr = next((t for t in range(min(R, 256) // 16 * 16, 15, -16) if R % t == 0), R)
        if tr == R:
            tc = _tile(C, 256)
    more, places = late if late is not None else (None, ())
    assert late is None or tr == R

    def body(p_ref, w_ref, m_ref, v_ref, *refs):
        def patch(g):
            more_ref, g_s = refs[0], refs[-1]
            x, y, c = _mesh_pos()
            me = 4 * x + 2 * y + c
            rows = _sum_parts(more_ref)
            g_s[...] = g
            for dev, row in places:
                g_s[row:row + rows.shape[0], :] += jnp.where(me == dev, rows, 0.0)
            return g_s[...]

        outs = refs[:4] if late is None else refs[1:5]
        _adamw_update(p_ref, w_ref, m_ref, v_ref, *outs, patch=None if late is None else patch)

    blk = pl.BlockSpec((tr, tc), lambda i, j: (i, j))
    shp = jax.ShapeDtypeStruct((R, C), F32)
    in_specs = [pl.BlockSpec((N_DEV, tr, tc), lambda i, j: (0, i, j)), blk, blk, blk]
    if late is not None:
        in_specs.append(pl.BlockSpec((N_DEV, more.shape[1], tc), lambda i, j: (0, 0, j)))
    return pl.pallas_call(
        body, grid=(R // tr, C // tc), in_specs=in_specs,
        out_specs=[blk, blk, blk, blk], out_shape=[shp, shp, shp, shp],
        scratch_shapes=[] if late is None else [pltpu.VMEM((tr, tc), F32)],
        name=name, compiler_params=_params("parallel", "parallel"))(parts, w, m, v, *([] if late is None else [more]))


SPLIT_V = 2 * D_MODEL
SPLIT_KR = SPLIT_V + Q_LORA + KV_LORA + QK_ROPE
IN_DIM = SPLIT_KR + 2 * D_MODEL
IN_ROWS_MAIN = ((0, D_MODEL), (D_MODEL, SPLIT_V), (SPLIT_KR, SPLIT_KR + D_MODEL), (SPLIT_KR + D_MODEL, IN_DIM))
IN_ROWS_ZS = (SPLIT_V, SPLIT_V + ZS_W)
ZS_ROWS = SPLIT_KR - SPLIT_V
IN_SHARD = IN_DIM // N_DEV
ZS_PIECES = tuple(
    (k, max(IN_SHARD * k, SPLIT_V) - SPLIT_V, max(IN_SHARD * k, SPLIT_V) - IN_SHARD * k)
    for k in range(N_DEV) if max(IN_SHARD * k, SPLIT_V) < min(IN_SHARD * (k + 1), SPLIT_KR))
ZS_PIECE_ROWS = ZS_ROWS // len(ZS_PIECES)
assert all(min(IN_SHARD * (k + 1), SPLIT_KR) - max(IN_SHARD * k, SPLIT_V) == ZS_PIECE_ROWS for k, _, _ in ZS_PIECES)

SMALL_EARLY = ("a_v_norm_g", "a_v_norm_b", "a_spatial_w", "a_spatial_b", "ffn_norm", "conv_b", "final_norm")
SMALL_LATE = ("q_a_norm", "kv_a_norm", "mix_norm")


def _small_2d(a):
    return a.reshape(-1, a.shape[-1])


def _cols_from_shards(g):
    return jnp.transpose(g, (1, 0, 2)).reshape(g.shape[1], N_DEV * g.shape[2])


def _shards_from_cols(a):
    R, W = a.shape
    return jnp.transpose(a.reshape(R, N_DEV, W // N_DEV), (1, 0, 2))


class _Comm:
    GATHER_GROUPS = (("w_uq", "w_ukv", "w_out"), ("w_up", "conv_w", "w_down"))
    FFN_GRADS = ("w_down", "w_up", "conv_w", "w_out")
    TRANSPOSED = ("w_in", "w_up", "w_uq")

    def __init__(self, shards):
        local = {n: a.astype(F32 if n == "conv_w" else BF16) for n, a in shards.items()}
        self.g_in = _gather_now(local["w_in"], "gather_w_in")
        groups = [[local[n] for n in g] for g in self.GATHER_GROUPS]
        (self.h_mla, self.h_ffn), _ = _start_copies(groups, [False] * 2, "gather_start", dep=self.g_in)

    def in_weights(self):
        return self.g_in.reshape(IN_DIM, D_MODEL)

    def mla_weights(self, after):
        g_uq, g_ukv, g_out = _wait_copies(self.h_mla, False, after, "gather_wait_mla")
        wuq_p = jnp.pad(g_uq, ((0, 0), (0, HEAD_PAD - QK_HEAD), (0, 0))).reshape(MLA_HEADS * HEAD_PAD, Q_LORA)
        return wuq_p, _cols_from_shards(g_ukv), g_out.reshape(D_MODEL, D_MODEL)

    def ffn_weights(self, after):
        g_up, g_cw, g_down = _wait_copies(self.h_ffn, False, after, "gather_wait_ffn")
        return g_up.reshape(2 * D_FF, D_MODEL), _cols_from_shards(g_cw), g_down.reshape(D_FF, D_MODEL)

    def send_ffn_grads(self, d_wdown, d_wt_up, d_convw, d_wout):
        group = [d_wdown.reshape(N_DEV, D_FF // N_DEV, D_MODEL), d_wt_up.reshape(N_DEV, 2 * D_FF // N_DEV, D_MODEL),
                 _shards_from_cols(d_convw), d_wout.reshape(N_DEV, D_MODEL // N_DEV, D_MODEL)]
        (self.h_ffn_grads,), token = _start_copies([group], [True], "ffn_grads_start")
        return token

    def send_small_grads(self, grads):
        (self.h_small_early,), token = _start_copies([grads], [False], "small_grads_start")
        return token

    def send_in_grads(self, d_wt_main):
        hole = jnp.zeros((ZS_ROWS, D_MODEL), d_wt_main.dtype)
        d_in = jnp.concatenate([d_wt_main[:SPLIT_V], hole, d_wt_main[SPLIT_V:]], axis=0)
        blocks = d_in.reshape(N_DEV, IN_SHARD, D_MODEL)
        (self.h_in_grads,), token = _start_copies([[blocks]], [True], "in_grads_start")
        return token

    def send_late_grads(self, d_wuq_p, d_wukv, d_wt_zs, small):
        d_uq = d_wuq_p.reshape(MLA_HEADS, HEAD_PAD, Q_LORA)[:, :QK_HEAD, :]
        zs_blocks = jnp.zeros((N_DEV, ZS_PIECE_ROWS, D_MODEL), d_wt_zs.dtype)
        for dev, first, _ in ZS_PIECES:
            zs_blocks = zs_blocks.at[dev].set(d_wt_zs[first:first + ZS_PIECE_ROWS])
        (self.h_late_grads, self.h_late_small), token = _start_copies(
            [[d_uq, _shards_from_cols(d_wukv), zs_blocks], small], [True, False], "late_grads_start")
        return token


def kernel(x, positions, mix_norm, w_in, a_v_norm_g, a_v_norm_b, a_spatial_w, a_spatial_b, q_a_norm, w_uq, kv_a_norm, w_ukv, w_out, ffn_norm, w_up, conv_w, conv_b, w_down, final_norm, loss_target, m_mix_norm, m_w_in, m_a_v_norm_g, m_a_v_norm_b, m_a_spatial_w, m_a_spatial_b, m_q_a_norm, m_w_uq, m_kv_a_norm, m_w_ukv, m_w_out, m_ffn_norm, m_w_up, m_conv_w, m_conv_b, m_w_down, m_final_norm, v_mix_norm, v_w_in, v_a_v_norm_g, v_a_v_norm_b, v_a_spatial_w, v_a_spatial_b, v_q_a_norm, v_w_uq, v_kv_a_norm, v_w_ukv, v_w_out, v_ffn_norm, v_w_up, v_conv_w, v_conv_b, v_w_down, v_final_norm):
    names = ("mix_norm", "w_in", "a_v_norm_g", "a_v_norm_b", "a_spatial_w", "a_spatial_b", "q_a_norm", "w_uq",
             "kv_a_norm", "w_ukv", "w_out", "ffn_norm", "w_up", "conv_w", "conv_b", "w_down", "final_norm")
    w = dict(zip(names, (mix_norm, w_in, a_v_norm_g, a_v_norm_b, a_spatial_w, a_spatial_b, q_a_norm, w_uq,
                         kv_a_norm, w_ukv, w_out, ffn_norm, w_up, conv_w, conv_b, w_down, final_norm)))
    m = dict(zip(names, (m_mix_norm, m_w_in, m_a_v_norm_g, m_a_v_norm_b, m_a_spatial_w, m_a_spatial_b,
                         m_q_a_norm, m_w_uq, m_kv_a_norm, m_w_ukv, m_w_out, m_ffn_norm, m_w_up, m_conv_w,
                         m_conv_b, m_w_down, m_final_norm)))
    v = dict(zip(names, (v_mix_norm, v_w_in, v_a_v_norm_g, v_a_v_norm_b, v_a_spatial_w, v_a_spatial_b,
                         v_q_a_norm, v_w_uq, v_kv_a_norm, v_w_ukv, v_w_out, v_ffn_norm, v_w_up, v_conv_w,
                         v_conv_b, v_w_down, v_final_norm)))
    shapes = {n: w[n].shape for n in names}
    def view(tree, n):
        a = tree[n].reshape(tree[n].shape[-2:])
        return a.T if n in _Comm.TRANSPOSED else a

    comm = _Comm({n: view(w, n) for n in ("w_in",) + _Comm.GATHER_GROUPS[0] + _Comm.GATHER_GROUPS[1]})

    grad_x, token = _local_step(
        x, positions, loss_target, w["mix_norm"], w["a_v_norm_g"], w["a_v_norm_b"], w["a_spatial_w"][0],
        w["a_spatial_b"][0], w["q_a_norm"], w["kv_a_norm"], w["ffn_norm"], w["conv_b"],
        w["final_norm"].reshape(1, D_MODEL), comm)

    out_g, out_d, out_m, out_v = {}, {}, {}, {}

    def update(n, parts, late=None):
        res = _adamw(parts, view(w, n), view(m, n), view(v, n), "adamw_" + n, late=late)
        out_g[n], out_d[n], out_m[n], out_v[n] = (
            (t.T if n in _Comm.TRANSPOSED else t).reshape(shapes[n]) for t in res)
        return res[1]

    def update_small(names, parts, sums, name):
        res = _adamw_many(parts, *[[_small_2d(t[n]) for n in names] for t in (w, m, v)], sums, name)
        for i, n in enumerate(names):
            out_g[n], out_d[n], out_m[n], out_v[n] = (r[i].reshape(shapes[n]) for r in res[:4])
        return res

    for n, parts in zip(_Comm.FFN_GRADS, _wait_copies(comm.h_ffn_grads, True, token, "ffn_grads_wait")):
        last = update(n, parts)
    early = _wait_copies(comm.h_small_early, False, last, "small_grads_wait")
    last = update_small(SMALL_EARLY, early, [], "adamw_small")[1][0]
    (in_parts,) = _wait_copies(comm.h_in_grads, True, last, "in_grads_wait")
    uq_parts, ukv_parts, zs_parts = _wait_copies(comm.h_late_grads, True, in_parts, "late_grads_wait")
    last = update("w_in", in_parts, late=(zs_parts, [(dev, row) for dev, _, row in ZS_PIECES]))
    last = update("w_uq", uq_parts)
    last = update("w_ukv", ukv_parts)
    late = _wait_copies(comm.h_late_small, False, last, "late_small_wait")
    res = update_small(SMALL_LATE, late[:-1], late[-1:], "adamw_late")
    loss = res[4][0][0, 0]

    return (loss, grad_x, *[out_g[n] for n in names], *[out_d[n] for n in names],
            *[out_m[n] for n in names], *[out_v[n] for n in names])
```

```python
import math

import jax
import jax.numpy as jnp
from jax import lax
from jax.experimental import pallas as pl
from jax.experimental.pallas import tpu as pltpu

F32 = jnp.float32
BF16 = jnp.bfloat16
KEPT = jnp.bfloat16

N_DEV = 8
D_MODEL = 1024
EPS = 1e-6
A_GROUPS = 8
CHUNK = 128
MLA_HEADS = 8
QK_NOPE = 128
QK_ROPE = 64
QK_HEAD = QK_NOPE + QK_ROPE
HEAD_PAD = 256
V_HEAD = 128
Q_LORA = 256
KV_LORA = 128
ROPE_THETA = 10000.0
D_FF = 2816
ZS_W = 512
ATTN_SCALE = QK_HEAD ** -0.5
ATTN_TILE = 512
NEG_BIG = -1e30

ADAM_LR = 0.001
ADAM_B1 = 0.9
ADAM_B2 = 0.999
ADAM_EPS = 1e-08
ADAM_WD = 0.01
ADAM_STEP = 10

VMEM_LIMIT = 56 * 1024 * 1024
SMALL_BLOCK_BYTES = 5 * 1024 * 1024
LANES = 128
SUBLANES = 8

GELU_K = math.sqrt(2.0 / math.pi)
GELU_C = 0.044715

ANY = pl.BlockSpec(memory_space=pl.ANY)
HBM = pl.BlockSpec(memory_space=pltpu.HBM)
SEM = pl.BlockSpec(memory_space=pltpu.SEMAPHORE)


def _tile(n, pref):
    for t in (pref, 512, 256, 128, 64, 32, 16, 8):
        if t <= pref and n % t == 0:
            return t
    return n


def _wide_tile(n, cap=1408):
    return next((t for t in range(min(n, cap) // LANES * LANES, 0, -LANES) if n % t == 0), n)


def _params(*sem):
    return pltpu.CompilerParams(dimension_semantics=sem, vmem_limit_bytes=VMEM_LIMIT)


def _dot(a, b):
    return jnp.dot(a, b, preferred_element_type=F32)


def _dot_nt(a, b):
    return lax.dot_general(a, b, (((1,), (1,)), ((), ())), preferred_element_type=F32)


def _dot_tn(a, b):
    return lax.dot_general(a, b, (((0,), (0,)), ((), ())), preferred_element_type=F32)


def _sigmoid(x):
    return 1.0 / (1.0 + jnp.exp(-x))


def _gelu(x):
    t = jnp.tanh(GELU_K * (x + GELU_C * x * x * x))
    return 0.5 * x * (1.0 + t)


def _gelu_and_grad(x):
    x2 = x * x
    t = jnp.tanh(GELU_K * (x + GELU_C * x * x2))
    half = 0.5 * (1.0 + t)
    return x * half, half + 0.5 * x * (1.0 - t * t) * GELU_K * (1.0 + 3.0 * GELU_C * x2)


def _in_proj(x, g, wt):
    T, Dm = x.shape
    tm = _tile(T, 512)

    def body(x_ref, g_ref, wt_ref, h_ref, zm_ref, zs_ref):
        xf = x_ref[...]
        r = lax.rsqrt(jnp.mean(xf * xf, axis=-1, keepdims=True) + EPS)
        h = (xf * r * g_ref[...]).astype(BF16)
        h_ref[...] = h
        for i, (r0, r1) in enumerate(IN_ROWS_MAIN):
            zm_ref[:, i * D_MODEL:(i + 1) * D_MODEL] = _dot_nt(h, wt_ref[r0:r1, :]).astype(KEPT)
        zs_ref[...] = _dot_nt(h, wt_ref[IN_ROWS_ZS[0]:IN_ROWS_ZS[1], :])

    row = lambda n: pl.BlockSpec((tm, n), lambda i: (i, 0))
    return pl.pallas_call(
        body, grid=(T // tm,),
        in_specs=[row(Dm), pl.BlockSpec((1, Dm), lambda i: (0, 0)), pl.BlockSpec(wt.shape, lambda i: (0, 0))],
        out_specs=[row(Dm), row(4 * D_MODEL), row(ZS_W)],
        out_shape=[jax.ShapeDtypeStruct((T, Dm), BF16), jax.ShapeDtypeStruct((T, 4 * D_MODEL), KEPT),
                   jax.ShapeDtypeStruct((T, ZS_W), F32)],
        name="in_proj", compiler_params=_params("parallel"))(x, g, wt)


def _proj_bwd(acts, wt, terms, x, g, dres, name, w2=None, dep=None, rows=256):
    T, Dm = x.shape
    tm = _tile(T, rows)
    n_a = len(acts)

    def body(*refs):
        ins, outs = refs[:n_a + 4 + (w2 is not None) + (dep is not None)], refs[-2 - (w2 is not None):]
        wt_ref, x_ref, g_ref, dres_ref = ins[n_a:n_a + 4]
        dx_ref, dg_ref = outs[0], outs[1]

        @pl.when(pl.program_id(0) == 0)
        def _():
            dg_ref[...] = jnp.zeros_like(dg_ref)

        dy = None
        for i, (c0, c1), (r0, r1) in terms:
            t = _dot(ins[i][:, c0:c1], wt_ref[r0:r1, :])
            dy = t if dy is None else dy + t
        xf = x_ref[...]
        r = lax.rsqrt(jnp.mean(xf * xf, axis=-1, keepdims=True) + EPS)
        xh = xf * r
        dg_ref[...] += jnp.sum(dy * xh, axis=0, keepdims=True)
        dxh = dy * g_ref[...]
        dx = dres_ref[...] + r * (dxh - xh * jnp.mean(dxh * xh, axis=-1, keepdims=True))
        dx_ref[...] = dx
        if w2 is not None:
            outs[2][...] = _dot_nt(dx.astype(BF16), ins[n_a + 4][...]).astype(KEPT)

    row = pl.BlockSpec((tm, Dm), lambda i: (i, 0))
    vec = pl.BlockSpec((1, Dm), lambda i: (0, 0))
    in_specs = [pl.BlockSpec((tm, a.shape[1]), lambda i: (i, 0)) for a in acts]
    in_specs += [pl.BlockSpec(wt.shape, lambda i: (0, 0)), row, vec, row]
    args = [*acts, wt, x, g, dres]
    out_specs = [row, vec]
    out_shape = [jax.ShapeDtypeStruct((T, Dm), F32), jax.ShapeDtypeStruct((1, Dm), F32)]
    if w2 is not None:
        in_specs.append(pl.BlockSpec(w2.shape, lambda i: (0, 0)))
        args.append(w2)
        out_specs.append(pl.BlockSpec((tm, w2.shape[0]), lambda i: (i, 0)))
        out_shape.append(jax.ShapeDtypeStruct((T, w2.shape[0]), KEPT))
    if dep is not None:
        in_specs.append(ANY)
        args.append(dep)
    return pl.pallas_call(
        body, grid=(T // tm,), in_specs=in_specs, out_specs=out_specs, out_shape=out_shape,
        name=name, compiler_params=_params("arbitrary"))(*args)


def _mm_tn(a, b, name, dep=None, rows=None, row0=0, into=None):
    T, M = a.shape
    N = b.shape[1]
    tm, tn, tt = _wide_tile(M), _wide_tile(N), _tile(T, 2048)
    n_t = T // tt
    off = row0 // tm
    extra = ([dep] if dep is not None else []) + ([into] if into is not None else [])

    def body(a_ref, b_ref, *refs):
        o_ref, acc_ref = refs[-2:]
        t = pl.program_id(2)

        @pl.when(t == 0)
        def _():
            acc_ref[...] = jnp.zeros_like(acc_ref)

        acc_ref[...] += _dot_tn(a_ref[...].astype(BF16), b_ref[...].astype(BF16))

        @pl.when(t == n_t - 1)
        def _():
            o_ref[...] = acc_ref[...].astype(BF16)

    return pl.pallas_call(
        body, grid=(M // tm, N // tn, n_t),
        in_specs=[pl.BlockSpec((tt, tm), lambda i, j, t: (t, i)),
                  pl.BlockSpec((tt, tn), lambda i, j, t: (t, j))] + [ANY] * len(extra),
        out_specs=pl.BlockSpec((tm, tn), lambda i, j, t: (i + off, j)),
        out_shape=jax.ShapeDtypeStruct((rows or M, N), BF16),
        scratch_shapes=[pltpu.VMEM((tm, tn), F32)],
        input_output_aliases={} if into is None else {1 + len(extra): 0},
        name=name, compiler_params=_params("parallel", "parallel", "arbitrary"))(a, b, *extra)


def _layer_norm_fwd(gv, g, b):
    mu = jnp.mean(gv, axis=-1, keepdims=True)
    xc = gv - mu
    rs = lax.rsqrt(jnp.mean(xc * xc, axis=-1, keepdims=True) + EPS)
    xh = xc * rs
    return xh, rs, xh * g + b


def _tri_mask(transposed=False):
    r = lax.broadcasted_iota(jnp.int32, (CHUNK, CHUNK), 0)
    c = lax.broadcasted_iota(jnp.int32, (CHUNK, CHUNK), 1)
    return r <= c if transposed else c <= r


def _mixer_a_fwd(zm, av_g, av_b, w_s, b_col):
    T = zm.shape[0]
    tm = _tile(T, 512)
    n_chunk = tm // CHUNK

    def body(u_ref, v_ref, ga_ref, g_ref, b_ref, w_ref, bc_ref, y_ref, vn_s, mx_s):
        gu = _gelu(u_ref[...].astype(F32))
        _, _, vn = _layer_norm_fwd(_gelu(v_ref[...].astype(F32)), g_ref[...], b_ref[...])
        vn_s[...] = vn.astype(BF16)
        tri = _tri_mask()
        for gi in range(A_GROUPS):
            wm = jnp.where(tri, w_ref[gi], 0.0).astype(BF16)
            cols = slice(gi * CHUNK, (gi + 1) * CHUNK)
            for n in range(n_chunk):
                rows = slice(n * CHUNK, (n + 1) * CHUNK)
                mx_s[rows, cols] = _dot(wm, vn_s[rows, cols]) + bc_ref[gi]
        y_ref[...] = (_sigmoid(ga_ref[...].astype(F32)) * gu * mx_s[...]).astype(KEPT)

    col = lambda c: pl.BlockSpec((tm, D_MODEL), lambda i: (i, c))
    vec = pl.BlockSpec((1, D_MODEL), lambda i: (0, 0))
    return pl.pallas_call(
        body, grid=(T // tm,),
        in_specs=[col(0), col(1), col(2), vec, vec,
                  pl.BlockSpec((A_GROUPS, CHUNK, CHUNK), lambda i: (0, 0, 0)),
                  pl.BlockSpec((A_GROUPS, CHUNK, 1), lambda i: (0, 0, 0))],
        out_specs=pl.BlockSpec((tm, D_MODEL), lambda i: (i, 0)),
        out_shape=jax.ShapeDtypeStruct((T, D_MODEL), KEPT),
        scratch_shapes=[pltpu.VMEM((tm, D_MODEL), BF16), pltpu.VMEM((tm, D_MODEL), F32)],
        name="mixer_a_fwd", compiler_params=_params("parallel"))(zm, zm, zm, av_g, av_b, w_s, b_col)


def _mixer_bwd(zm, o, dm, av_g, av_b, w_s, w_st, b_col, dep):
    T = zm.shape[0]
    tm = _tile(T, 256)
    n_chunk = tm // CHUNK

    def body(u_ref, v_ref, ga_ref, gb_ref, o_ref, dm_ref, g_ref, b_ref, w_ref, wt_ref, bc_ref, dep_ref,
             dz_ref, do_ref, dg_ref, db_ref, dw_ref, dbs_ref, vn_s, mx_s, dmx_s, dvn_s):
        @pl.when(pl.program_id(0) == 0)
        def _():
            dg_ref[...] = jnp.zeros_like(dg_ref)
            db_ref[...] = jnp.zeros_like(db_ref)
            dw_ref[...] = jnp.zeros_like(dw_ref)
            dbs_ref[...] = jnp.zeros_like(dbs_ref)

        dm_v = dm_ref[...].astype(F32)
        gb = gb_ref[...].astype(F32)
        sb = _sigmoid(gb)
        o_v = o_ref[...].astype(F32)
        do_ref[...] = (dm_v * sb).astype(BF16)
        dz_ref[:, 3 * D_MODEL:4 * D_MODEL] = (dm_v * o_v * sb * (1.0 - sb)).astype(BF16)
        u = u_ref[...].astype(F32)
        v = v_ref[...].astype(F32)
        gu, gu_grad = _gelu_and_grad(u)
        gv, gv_grad = _gelu_and_grad(v)
        xh, rs, vn = _layer_norm_fwd(gv, g_ref[...], b_ref[...])
        vn_s[...] = vn.astype(BF16)
        tri = _tri_mask()
        for gi in range(A_GROUPS):
            wm = jnp.where(tri, w_ref[gi], 0.0).astype(BF16)
            cols = slice(gi * CHUNK, (gi + 1) * CHUNK)
            for n in range(n_chunk):
                rows = slice(n * CHUNK, (n + 1) * CHUNK)
                mx_s[rows, cols] = _dot(wm, vn_s[rows, cols]) + bc_ref[gi]
        mixed = mx_s[...]
        sa = _sigmoid(ga_ref[...].astype(F32))
        dya = dm_v * sa
        dz_ref[:, 2 * D_MODEL:3 * D_MODEL] = (dm_v * gu * mixed * sa * (1.0 - sa)).astype(BF16)
        dz_ref[:, 0:D_MODEL] = (dya * mixed * gu_grad).astype(BF16)
        dmx = dya * gu
        dmx_s[...] = dmx.astype(BF16)
        tri_t = _tri_mask(transposed=True)
        for gi in range(A_GROUPS):
            wmt = jnp.where(tri_t, wt_ref[gi], 0.0).astype(BF16)
            cols = slice(gi * CHUNK, (gi + 1) * CHUNK)
            dw_acc = jnp.zeros((CHUNK, CHUNK), F32)
            dmx_sum = jnp.zeros((CHUNK, CHUNK), F32)
            for n in range(n_chunk):
                rows = slice(n * CHUNK, (n + 1) * CHUNK)
                blk = dmx_s[rows, cols]
                dvn_s[rows, cols] = _dot(wmt, blk)
                dw_acc = dw_acc + _dot_nt(blk, vn_s[rows, cols])
                dmx_sum = dmx_sum + dmx[rows, cols]
            dw_ref[gi] += jnp.where(tri, dw_acc, 0.0)
            dbs_ref[gi] += jnp.sum(dmx_sum, axis=-1, keepdims=True)
        dvn = dvn_s[...]
        dg_ref[...] += jnp.sum(dvn * xh, axis=0, keepdims=True)
        db_ref[...] += jnp.sum(dvn, axis=0, keepdims=True)
        dxh = dvn * g_ref[...]
        dgv = rs * (dxh - jnp.mean(dxh, axis=-1, keepdims=True)
                    - xh * jnp.mean(dxh * xh, axis=-1, keepdims=True))
        dz_ref[:, D_MODEL:2 * D_MODEL] = (dgv * gv_grad).astype(BF16)

    col = lambda c: pl.BlockSpec((tm, D_MODEL), lambda i: (i, c))
    row = pl.BlockSpec((tm, D_MODEL), lambda i: (i, 0))
    vec = pl.BlockSpec((1, D_MODEL), lambda i: (0, 0))
    wsp = pl.BlockSpec((A_GROUPS, CHUNK, CHUNK), lambda i: (0, 0, 0))
    bsp = pl.BlockSpec((A_GROUPS, CHUNK, 1), lambda i: (0, 0, 0))
    return pl.pallas_call(
        body, grid=(T // tm,),
        in_specs=[col(0), col(1), col(2), col(3), row, row, vec, vec, wsp, wsp, bsp, ANY],
        out_specs=[pl.BlockSpec((tm, 4 * D_MODEL), lambda i: (i, 0)), row, vec, vec, wsp, bsp],
        out_shape=[jax.ShapeDtypeStruct((T, 4 * D_MODEL), BF16), jax.ShapeDtypeStruct((T, D_MODEL), BF16),
                   jax.ShapeDtypeStruct((1, D_MODEL), F32), jax.ShapeDtypeStruct((1, D_MODEL), F32),
                   jax.ShapeDtypeStruct((A_GROUPS, CHUNK, CHUNK), F32),
                   jax.ShapeDtypeStruct((A_GROUPS, CHUNK, 1), F32)],
        scratch_shapes=[pltpu.VMEM((tm, D_MODEL), BF16), pltpu.VMEM((tm, D_MODEL), F32),
                        pltpu.VMEM((tm, D_MODEL), BF16), pltpu.VMEM((tm, D_MODEL), F32)],
        name="mixer_bwd", compiler_params=_params("arbitrary"))(
            zm, zm, zm, zm, o, dm, av_g, av_b, w_s, w_st, b_col, dep)


def _rope_tables(pos_ref, invf_ref):
    ang = pos_ref[...].astype(F32) * invf_ref[...]
    lane = lax.broadcasted_iota(jnp.int32, ang.shape, 1)
    cos, sin = jnp.cos(ang), jnp.sin(ang)
    c = jnp.where(lane < QK_ROPE, cos, 0.0)
    sa = jnp.where(lane < QK_ROPE // 2, -sin, 0.0)
    sb = jnp.where((lane >= QK_ROPE // 2) & (lane < QK_ROPE), sin, 0.0)
    return c, sa, sb


def _rope(blk, tabs):
    c, sa, sb = tabs
    return blk * c + pltpu.roll(blk, LANES - QK_ROPE // 2, 1) * sa + pltpu.roll(blk, QK_ROPE // 2, 1) * sb


def _rope_t(dout, tabs):
    c, sa, sb = tabs
    return dout * c + pltpu.roll(dout * sa, QK_ROPE // 2, 1) + pltpu.roll(dout * sb, LANES - QK_ROPE // 2, 1)


def _rms_small(x, g):
    r = lax.rsqrt(jnp.mean(x * x, axis=-1, keepdims=True) + EPS)
    xh = x * r
    return xh, r, xh * g


def _mla_prep_fwd(zs, pos, invf, qg, kvg, wuq_p, wukv):
    T = zs.shape[0]
    tm = _tile(T, 512)
    HW = MLA_HEADS * HEAD_PAD

    def body(zs_ref, pos_ref, invf_ref, qg_ref, kvg_ref, wq_ref, wkv_ref, q_ref, k_ref, v_ref):
        tabs = _rope_tables(pos_ref, invf_ref)
        _, _, cqn = _rms_small(zs_ref[:, 0:Q_LORA], qg_ref[...])
        _, _, ckvn = _rms_small(zs_ref[:, Q_LORA:Q_LORA + KV_LORA], kvg_ref[...])
        q = _dot_nt(cqn.astype(BF16), wq_ref[...]) * ATTN_SCALE
        kv = _dot(ckvn.astype(BF16), wkv_ref[...])
        kr = _rope(zs_ref[:, Q_LORA + KV_LORA:ZS_W], tabs).astype(BF16)
        for h in range(MLA_HEADS):
            b0 = h * HEAD_PAD
            q_ref[:, b0:b0 + QK_NOPE] = q[:, b0:b0 + QK_NOPE].astype(BF16)
            q_ref[:, b0 + QK_NOPE:b0 + HEAD_PAD] = _rope(q[:, b0 + QK_NOPE:b0 + HEAD_PAD], tabs).astype(BF16)
            k_ref[:, b0:b0 + QK_NOPE] = kv[:, b0:b0 + QK_NOPE].astype(BF16)
            k_ref[:, b0 + QK_NOPE:b0 + HEAD_PAD] = kr
            v_ref[:, h * V_HEAD:(h + 1) * V_HEAD] = kv[:, b0 + QK_NOPE:b0 + HEAD_PAD].astype(BF16)

    full = lambda a: pl.BlockSpec(a.shape, lambda i: (0,) * a.ndim)
    return pl.pallas_call(
        body, grid=(T // tm,),
        in_specs=[pl.BlockSpec((tm, ZS_W), lambda i: (i, 0)), pl.BlockSpec((tm, 1), lambda i: (i, 0)),
                  full(invf), full(qg), full(kvg), full(wuq_p), full(wukv)],
        out_specs=[pl.BlockSpec((tm, HW), lambda i: (i, 0)), pl.BlockSpec((tm, HW), lambda i: (i, 0)),
                   pl.BlockSpec((tm, D_MODEL), lambda i: (i, 0))],
        out_shape=[jax.ShapeDtypeStruct((T, HW), BF16), jax.ShapeDtypeStruct((T, HW), BF16),
                   jax.ShapeDtypeStruct((T, D_MODEL), BF16)],
        name="mla_prep_fwd", compiler_params=_params("parallel"))(zs, pos, invf, qg, kvg, wuq_p, wukv)


def _mla_prep_bwd(zs, pos, invf, qg, kvg, wuq_p, wukv, dq, dk, dv):
    T = zs.shape[0]
    tm = _tile(T, 512)
    n_t = T // tm
    HW = MLA_HEADS * HEAD_PAD

    def body(zs_ref, pos_ref, invf_ref, qg_ref, kvg_ref, wq_ref, wkv_ref, dq_ref, dk_ref, dv_ref,
             dzs_ref, dwq_ref, dwkv_ref, dqg_ref, dkvg_ref, dqp_ref, dkv_ref, accq_ref, acckv_ref):
        @pl.when(pl.program_id(0) == 0)
        def _():
            dqg_ref[...] = jnp.zeros_like(dqg_ref)
            dkvg_ref[...] = jnp.zeros_like(dkvg_ref)
            accq_ref[...] = jnp.zeros_like(accq_ref)
            acckv_ref[...] = jnp.zeros_like(acckv_ref)

        tabs = _rope_tables(pos_ref, invf_ref)
        cqh, rq, cqn = _rms_small(zs_ref[:, 0:Q_LORA], qg_ref[...])
        ckvh, rkv, ckvn = _rms_small(zs_ref[:, Q_LORA:Q_LORA + KV_LORA], kvg_ref[...])
        dkr = jnp.zeros((tm, LANES), F32)
        for h in range(MLA_HEADS):
            b0 = h * HEAD_PAD
            dqp_ref[:, b0:b0 + QK_NOPE] = dq_ref[:, b0:b0 + QK_NOPE]
            dqp_ref[:, b0 + QK_NOPE:b0 + HEAD_PAD] = _rope_t(
                dq_ref[:, b0 + QK_NOPE:b0 + HEAD_PAD].astype(F32), tabs).astype(BF16)
            dkv_ref[:, b0:b0 + QK_NOPE] = dk_ref[:, b0:b0 + QK_NOPE]
            dkv_ref[:, b0 + QK_NOPE:b0 + HEAD_PAD] = dv_ref[:, h * V_HEAD:(h + 1) * V_HEAD]
            dkr = dkr + dk_ref[:, b0 + QK_NOPE:b0 + HEAD_PAD].astype(F32)
        accq_ref[...] += _dot_tn(dqp_ref[...], cqn.astype(BF16))
        acckv_ref[...] += _dot_tn(ckvn.astype(BF16), dkv_ref[...])

        @pl.when(pl.program_id(0) == n_t - 1)
        def _():
            dwq_ref[...] = accq_ref[...].astype(BF16)
            dwkv_ref[...] = acckv_ref[...].astype(BF16)

        dcqn = _dot(dqp_ref[...], wq_ref[...])
        dckvn = _dot_nt(dkv_ref[...], wkv_ref[...])
        dqg_ref[...] += jnp.sum(dcqn * cqh, axis=0, keepdims=True)
        dkvg_ref[...] += jnp.sum(dckvn * ckvh, axis=0, keepdims=True)
        dxh = dcqn * qg_ref[...]
        dzs_ref[:, 0:Q_LORA] = (rq * (dxh - cqh * jnp.mean(dxh * cqh, axis=-1, keepdims=True))).astype(BF16)
        dxh = dckvn * kvg_ref[...]
        dzs_ref[:, Q_LORA:Q_LORA + KV_LORA] = (
            rkv * (dxh - ckvh * jnp.mean(dxh * ckvh, axis=-1, keepdims=True))).astype(BF16)
        dzs_ref[:, Q_LORA + KV_LORA:ZS_W] = _rope_t(dkr, tabs).astype(BF16)

    full = lambda a: pl.BlockSpec(a.shape, lambda i: (0,) * a.ndim)
    rowb = lambda w: pl.BlockSpec((tm, w), lambda i: (i, 0))
    return pl.pallas_call(
        body, grid=(T // tm,),
        in_specs=[rowb(ZS_W), rowb(1), full(invf), full(qg), full(kvg), full(wuq_p), full(wukv),
                  rowb(HW), rowb(HW), rowb(D_MODEL)],
        out_specs=[rowb(ZS_W), full(wuq_p), full(wukv), full(qg), full(kvg)],
        out_shape=[jax.ShapeDtypeStruct((T, ZS_W), BF16), jax.ShapeDtypeStruct(wuq_p.shape, BF16),
                   jax.ShapeDtypeStruct(wukv.shape, BF16), jax.ShapeDtypeStruct(qg.shape, F32),
                   jax.ShapeDtypeStruct(kvg.shape, F32)],
        scratch_shapes=[pltpu.VMEM((tm, HW), BF16), pltpu.VMEM((tm, HW), BF16),
                        pltpu.VMEM(wuq_p.shape, F32), pltpu.VMEM(wukv.shape, F32)],
        name="mla_prep_bwd", compiler_params=_params("arbitrary"))(
            zs, pos, invf, qg, kvg, wuq_p, wukv, dq, dk, dv)


def _causal(tq, kmax, q0):
    r = lax.broadcasted_iota(jnp.int32, (tq, kmax), 0) + q0
    c = lax.broadcasted_iota(jnp.int32, (tq, kmax), 1)
    return c <= r


def _attn_fwd(q, k, v, batch, seq):
    tq = _tile(seq, ATTN_TILE)
    nq = seq // tq

    def body(q_ref, k_ref, v_ref, o_ref, lse_ref):
        diag = _causal(tq, tq, 0)
        for qi in range(nq):
            rows = slice(qi * tq, (qi + 1) * tq)
            qr = q_ref[rows, :]
            s_d = jnp.where(diag, _dot_nt(qr, k_ref[rows, :]), NEG_BIG)
            m = jnp.max(s_d, axis=-1, keepdims=True)
            if qi > 0:
                before = slice(0, qi * tq)
                s_b = _dot_nt(qr, k_ref[before, :])
                m = jnp.maximum(m, jnp.max(s_b, axis=-1, keepdims=True))
                p_b = jnp.exp(s_b - m)
                l = jnp.sum(p_b, axis=-1, keepdims=True)
                acc = _dot(p_b.astype(BF16), v_ref[before, :])
            p_d = jnp.exp(s_d - m)
            l_d = jnp.sum(p_d, axis=-1, keepdims=True)
            acc_d = _dot(p_d.astype(BF16), v_ref[rows, :])
            l, acc = (l + l_d, acc + acc_d) if qi > 0 else (l_d, acc_d)
            o_ref[rows, :] = (acc / l).astype(KEPT)
            lse_ref[rows, :] = jnp.broadcast_to(m + jnp.log(l), (tq, V_HEAD))

    return pl.pallas_call(
        body, grid=(batch, MLA_HEADS),
        in_specs=[pl.BlockSpec((seq, HEAD_PAD), lambda b, h: (b, h)),
                  pl.BlockSpec((seq, HEAD_PAD), lambda b, h: (b, h)),
                  pl.BlockSpec((seq, V_HEAD), lambda b, h: (b, h))],
        out_specs=[pl.BlockSpec((seq, V_HEAD), lambda b, h: (b, h)),
                   pl.BlockSpec((seq, V_HEAD), lambda b, h: (b, h))],
        out_shape=[jax.ShapeDtypeStruct((batch * seq, D_MODEL), KEPT),
                   jax.ShapeDtypeStruct((batch * seq, D_MODEL), F32)],
        name="attn_fwd", compiler_params=_params("parallel", "parallel"))(q, k, v)


def _attn_bwd(q, k, v, o, do, lse, batch, seq, dep):
    tq = _tile(seq, ATTN_TILE)
    nq = seq // tq

    def body(q_ref, k_ref, v_ref, o_ref, do_ref, lse_ref, dep_ref, dq_ref, dk_ref, dv_ref, dk_acc, dv_acc):
        dk_acc[...] = jnp.zeros_like(dk_acc)
        dv_acc[...] = jnp.zeros_like(dv_acc)
        for qi in range(nq):
            rows = slice(qi * tq, (qi + 1) * tq)
            kmax = (qi + 1) * tq
            qr = q_ref[rows, :]
            dor = do_ref[rows, :]
            kk = k_ref[0:kmax, :]
            s = _dot_nt(qr, kk)
            p = jnp.where(_causal(tq, kmax, qi * tq), jnp.exp(s - lse_ref[rows, 0:1]), 0.0)
            dp = _dot_nt(dor, v_ref[0:kmax, :])
            delta = jnp.sum(dor.astype(F32) * o_ref[rows, :].astype(F32), axis=-1, keepdims=True)
            ds = (p * (dp - delta)).astype(BF16)
            dq_ref[rows, :] = (_dot(ds, kk) * ATTN_SCALE).astype(BF16)
            dk_acc[0:kmax, :] += _dot_tn(ds, qr)
            dv_acc[0:kmax, :] += _dot_tn(p.astype(BF16), dor)
        dk_ref[...] = dk_acc[...].astype(BF16)
        dv_ref[...] = dv_acc[...].astype(BF16)

    qspec = pl.BlockSpec((seq, HEAD_PAD), lambda b, h: (b, h))
    vspec = pl.BlockSpec((seq, V_HEAD), lambda b, h: (b, h))
    T = batch * seq
    return pl.pallas_call(
        body, grid=(batch, MLA_HEADS),
        in_specs=[qspec, qspec, vspec, vspec, vspec, vspec, ANY],
        out_specs=[qspec, qspec, vspec],
        out_shape=[jax.ShapeDtypeStruct((T, MLA_HEADS * HEAD_PAD), BF16),
                   jax.ShapeDtypeStruct((T, MLA_HEADS * HEAD_PAD), BF16),
                   jax.ShapeDtypeStruct((T, D_MODEL), BF16)],
        scratch_shapes=[pltpu.VMEM((seq, HEAD_PAD), F32), pltpu.VMEM((seq, V_HEAD), F32)],
        name="attn_bwd", compiler_params=_params("parallel", "parallel"))(q, k, v, o, do, lse, dep)


def _merge_out(x, yag, zm, o, w_out, ffn_g):
    T = x.shape[0]
    tm = _tile(T, 512)

    def body(x_ref, ya_ref, gb_ref, o_ref, w_ref, g_ref, mg_ref, x1_ref, h2_ref):
        mg = (ya_ref[...].astype(F32) + _sigmoid(gb_ref[...].astype(F32)) * o_ref[...].astype(F32)).astype(BF16)
        mg_ref[...] = mg
        x1 = x_ref[...] + _dot(mg, w_ref[...])
        x1_ref[...] = x1
        r = lax.rsqrt(jnp.mean(x1 * x1, axis=-1, keepdims=True) + EPS)
        h2_ref[...] = (x1 * r * g_ref[...]).astype(BF16)

    row = pl.BlockSpec((tm, D_MODEL), lambda i: (i, 0))
    return pl.pallas_call(
        body, grid=(T // tm,),
        in_specs=[row, row, pl.BlockSpec((tm, D_MODEL), lambda i: (i, 3)), row,
                  pl.BlockSpec((D_MODEL, D_MODEL), lambda i: (0, 0)), pl.BlockSpec((1, D_MODEL), lambda i: (0, 0))],
        out_specs=[row, row, row],
        out_shape=[jax.ShapeDtypeStruct((T, D_MODEL), BF16), jax.ShapeDtypeStruct((T, D_MODEL), F32),
                   jax.ShapeDtypeStruct((T, D_MODEL), BF16)],
        name="merge_out", compiler_params=_params("parallel"))(x, yag, zm, o, w_out, ffn_g)


FF_TILE = 256
FF_BLOCKS = D_FF // FF_TILE
FFB_TILE = 256
UP_ROWS = 512
EDGE = 16


def _shift_up(x, k):
    n = x.shape[0]
    row = lax.broadcasted_iota(jnp.int32, x.shape, 0)
    return jnp.where(row < n - k, pltpu.roll(x, n - k, 0), 0.0)


def _up_act(h2, wt_up, cw, cb, batch, seq):
    def body(h_ref, wug_ref, wuv_ref, wg_ref, wv_ref, bg_ref, bv_ref, ug_ref, uv_ref, g_ref, v_ref, a_ref,
             ug_s, uv_s):
        for s in (ug_s, uv_s):
            s[0:SUBLANES, :] = jnp.zeros((SUBLANES, FF_TILE), F32)

        def conv(s, w_ref, b_ref, r0):
            return (b_ref[...] + w_ref[2:3, :] * s[r0:r0 + UP_ROWS, :]
                    + w_ref[1:2, :] * s[r0 - 1:r0 - 1 + UP_ROWS, :]
                    + w_ref[0:1, :] * s[r0 - 2:r0 - 2 + UP_ROWS, :])

        for c in range(seq // UP_ROWS):
            rows = slice(c * UP_ROWS, (c + 1) * UP_ROWS)
            r0 = SUBLANES + c * UP_ROWS
            h = h_ref[rows, :]
            for w_ref, u_ref, s in ((wug_ref, ug_ref, ug_s), (wuv_ref, uv_ref, uv_s)):
                u = _dot_nt(h, w_ref[...])
                u_ref[rows, :] = u.astype(KEPT)
                s[r0:r0 + UP_ROWS, :] = u
            gate, val = conv(ug_s, wg_ref, bg_ref, r0), conv(uv_s, wv_ref, bv_ref, r0)
            g_ref[rows, :] = gate.astype(KEPT)
            v_ref[rows, :] = val.astype(KEPT)
            a_ref[rows, :] = (gate * _sigmoid(gate) * val).astype(BF16)

    blk = pl.BlockSpec((seq, FF_TILE), lambda b, j: (b, j))
    wup = lambda off: pl.BlockSpec((FF_TILE, D_MODEL), lambda b, j: (j + off, 0))
    wsp = lambda off: pl.BlockSpec((3, FF_TILE), lambda b, j: (0, j + off))
    bsp = lambda off: pl.BlockSpec((1, FF_TILE), lambda b, j: (0, j + off))
    T = batch * seq
    kept = jax.ShapeDtypeStruct((T, D_FF), KEPT)
    return pl.pallas_call(
        body, grid=(batch, FF_BLOCKS),
        in_specs=[pl.BlockSpec((seq, D_MODEL), lambda b, j: (b, 0)), wup(0), wup(FF_BLOCKS),
                  wsp(0), wsp(FF_BLOCKS), bsp(0), bsp(FF_BLOCKS)],
        out_specs=[blk] * 5,
        out_shape=[kept, kept, kept, kept, jax.ShapeDtypeStruct((T, D_FF), BF16)],
        scratch_shapes=[pltpu.VMEM((SUBLANES + seq, FF_TILE), F32)] * 2,
        name="up_act", compiler_params=_params("parallel", "arbitrary"))(h2, wt_up, wt_up, cw, cw, cb, cb)


def _ffn_act_bwd(upg, upv, gate, val, cw, dx2b, w_down, batch, seq):
    def half(du, x, w_ref, dx_ref, dw_ref):
        j = pl.program_id(1)
        n = du.shape[0]
        up1, up2 = pltpu.roll(du, n - 1, 0), pltpu.roll(du, n - 2, 0)
        dx_ref[...] = (w_ref[2:3, :] * du + w_ref[1:2, :] * up1 + w_ref[0:1, :] * up2).astype(BF16)
        tail = du[n - EDGE:n]
        dx_ref[n - EDGE:n, :] = (w_ref[2:3, :] * tail + w_ref[1:2, :] * _shift_up(tail, 1)
                                 + w_ref[0:1, :] * _shift_up(tail, 2)).astype(BF16)
        row = lax.broadcasted_iota(jnp.int32, (EDGE, du.shape[1]), 0)
        head, x_tail = du[0:EDGE], x[n - EDGE:n]
        wrap1 = jnp.sum(jnp.where(row >= EDGE - 1, pltpu.roll(head, EDGE - 1, 0), 0.0) * x_tail, axis=0, keepdims=True)
        wrap2 = jnp.sum(jnp.where(row >= EDGE - 2, pltpu.roll(head, EDGE - 2, 0), 0.0) * x_tail, axis=0, keepdims=True)
        dw_ref[j, 2:3, :] += jnp.sum(du * x, axis=0, keepdims=True)
        dw_ref[j, 1:2, :] += jnp.sum(up1 * x, axis=0, keepdims=True) - wrap1
        dw_ref[j, 0:1, :] += jnp.sum(up2 * x, axis=0, keepdims=True) - wrap2
        dw_ref[j, 3:4, :] += jnp.sum(du, axis=0, keepdims=True)

    def body(ug_ref, uv_ref, g_ref, v_ref, wg_ref, wv_ref, dx_ref, wd_ref, dg_ref, dv_ref, dwg_ref, dwv_ref):
        @pl.when((pl.program_id(0) == 0) & (pl.program_id(1) == 0))
        def _():
            dwg_ref[...] = jnp.zeros_like(dwg_ref)
            dwv_ref[...] = jnp.zeros_like(dwv_ref)

        gate, val = g_ref[...].astype(F32), v_ref[...].astype(F32)
        sg = _sigmoid(gate)
        dav = _dot_nt(dx_ref[...], wd_ref[...])
        half(dav * val * sg * (1.0 + gate * (1.0 - sg)), ug_ref[...].astype(F32), wg_ref, dg_ref, dwg_ref)
        half(dav * gate * sg, uv_ref[...].astype(F32), wv_ref, dv_ref, dwv_ref)

    nb = D_FF // FFB_TILE
    blk = pl.BlockSpec((seq, FFB_TILE), lambda b, j: (b, j))
    wsp = lambda off: pl.BlockSpec((3, FFB_TILE), lambda b, j: (0, j + off))
    acc = pl.BlockSpec((nb, 4, FFB_TILE), lambda b, j: (0, 0, 0))
    T = batch * seq
    dupg, dupv, dwg, dwv = pl.pallas_call(
        body, grid=(batch, nb),
        in_specs=[blk, blk, blk, blk, wsp(0), wsp(nb),
                  pl.BlockSpec((seq, D_MODEL), lambda b, j: (b, 0)),
                  pl.BlockSpec((FFB_TILE, D_MODEL), lambda b, j: (j, 0))],
        out_specs=[blk, blk, acc, acc],
        out_shape=[jax.ShapeDtypeStruct((T, D_FF), BF16), jax.ShapeDtypeStruct((T, D_FF), BF16),
                   jax.ShapeDtypeStruct((nb, 4, FFB_TILE), F32), jax.ShapeDtypeStruct((nb, 4, FFB_TILE), F32)],
        name="ffn_act_bwd", compiler_params=_params("arbitrary", "arbitrary"))(
            upg, upv, gate, val, cw, cw, dx2b, w_down)
    dwg, dwv = (jnp.transpose(a, (1, 0, 2)).reshape(4, D_FF) for a in (dwg, dwv))
    return dupg, dupv, dwg[:3], dwv[:3], dwg[3:], dwv[3:]


def _down_loss(a, w_down, x1, target, gfin):
    T = x1.shape[0]
    tm = _tile(T, 512)

    def body(a_ref, w_ref, x1_ref, t_ref, g_ref, dx_ref, dxb_ref, loss_ref, dg_ref):
        @pl.when(pl.program_id(0) == 0)
        def _():
            loss_ref[...] = jnp.zeros_like(loss_ref)
            dg_ref[...] = jnp.zeros_like(dg_ref)

        x2 = x1_ref[...] + _dot(a_ref[...], w_ref[...])
        r = lax.rsqrt(jnp.mean(x2 * x2, axis=-1, keepdims=True) + EPS)
        xh = x2 * r
        g = g_ref[...]
        diff = xh * g - t_ref[...]
        loss_ref[...] += 0.5 * jnp.sum(jnp.mean(diff * diff, axis=-1, keepdims=True))
        dy = diff * (1.0 / D_MODEL)
        dg_ref[...] += jnp.sum(dy * xh, axis=0, keepdims=True)
        dxh = dy * g
        dx = r * (dxh - xh * jnp.mean(dxh * xh, axis=-1, keepdims=True))
        dx_ref[...] = dx
        dxb_ref[...] = dx.astype(BF16)

    row = pl.BlockSpec((tm, D_MODEL), lambda i: (i, 0))
    vec = pl.BlockSpec((1, D_MODEL), lambda i: (0, 0))
    return pl.pallas_call(
        body, grid=(T // tm,),
        in_specs=[pl.BlockSpec((tm, D_FF), lambda i: (i, 0)),
                  pl.BlockSpec((D_FF, D_MODEL), lambda i: (0, 0)), row, row, vec],
        out_specs=[row, row, pl.BlockSpec((8, LANES), lambda i: (0, 0)), vec],
        out_shape=[jax.ShapeDtypeStruct((T, D_MODEL), F32), jax.ShapeDtypeStruct((T, D_MODEL), BF16),
                   jax.ShapeDtypeStruct((8, LANES), F32), jax.ShapeDtypeStruct((1, D_MODEL), F32)],
        name="down_loss", compiler_params=_params("arbitrary"))(a, w_down, x1, target, gfin)


def _local_step(x, positions, target, mix_norm, av_g, av_b, w_s, b_s, q_norm, kv_norm, ffn_norm, conv_b,
                final_norm, comm):
    batch, seq, _ = x.shape
    T = batch * seq
    x = x.reshape(T, D_MODEL)
    target = target.reshape(T, D_MODEL)
    pos = positions.reshape(T, 1)
    half = jnp.arange(0, QK_ROPE, 2, dtype=F32) / QK_ROPE
    inv_freq = 1.0 / (ROPE_THETA ** half)
    invf = jnp.concatenate([inv_freq, inv_freq, jnp.zeros((LANES - QK_ROPE,), F32)]).reshape(1, LANES)
    w_st = jnp.swapaxes(w_s, 1, 2)
    b_col = b_s.reshape(A_GROUPS, CHUNK, 1)

    wt_in = comm.in_weights()
    h, zm, zs = _in_proj(x, mix_norm, wt_in)
    yag = _mixer_a_fwd(zm, av_g, av_b, w_s, b_col)
    wuq_p, wukv, w_out = comm.mla_weights(after=yag)
    q, k, v = _mla_prep_fwd(zs, pos, invf, q_norm, kv_norm, wuq_p, wukv)
    o, lse = _attn_fwd(q, k, v, batch, seq)
    merged, x1, h2 = _merge_out(x, yag, zm, o, w_out, ffn_norm)
    wt_up, conv_w, w_down = comm.ffn_weights(after=merged)
    upg, upv, gate, val, act = _up_act(h2, wt_up, conv_w, conv_b, batch, seq)
    dx2, dx2b, loss_acc, d_final = _down_loss(act, w_down, x1, target, final_norm)

    d_wdown = _mm_tn(act, dx2b, "dw_down")
    dupg, dupv, dcwg, dcwv, dcbg, dcbv = _ffn_act_bwd(upg, upv, gate, val, conv_w, dx2b, w_down, batch, seq)
    d_wt_up = _mm_tn(dupv, h2, "dw_up_val", rows=2 * D_FF, row0=D_FF,
                     into=_mm_tn(dupg, h2, "dw_up_gate", rows=2 * D_FF))
    dx1, d_ffn_norm, dmerged = _proj_bwd(
        [dupg, dupv], wt_up, [(0, (0, D_FF), (0, D_FF)), (1, (0, D_FF), (D_FF, 2 * D_FF))],
        x1, ffn_norm, dx2, "up_proj_bwd", w2=w_out)
    d_wout = _mm_tn(merged, dx1, "dw_out")
    token = comm.send_ffn_grads(d_wdown, d_wt_up, jnp.concatenate([dcwg, dcwv], axis=1), d_wout)
    dzm, do, d_avg, d_avb, d_ws, d_bs = _mixer_bwd(zm, o, dmerged, av_g, av_b, w_s, w_st, b_col, token)
    d_wt_main = _mm_tn(dzm, h, "dw_in_main")
    token = comm.send_early_grads(d_wt_main, [
        d_avg, d_avb, _small_2d(d_ws), d_bs.reshape(A_GROUPS, CHUNK), d_ffn_norm,
        jnp.concatenate([dcbg, dcbv], axis=1), d_final])
    dq, dk, dv = _attn_bwd(q, k, v, o, do, lse, batch, seq, token)
    dzs, d_wuq_p, d_wukv, d_qn, d_kvn = _mla_prep_bwd(zs, pos, invf, q_norm, kv_norm, wuq_p, wukv, dq, dk, dv)
    d_wt_zs = _mm_tn(dzs, h, "dw_in_small")
    token = comm.send_mla_grads(d_wuq_p, d_wukv, d_wt_zs)
    terms = [(0, (i * D_MODEL, (i + 1) * D_MODEL), rows) for i, rows in enumerate(IN_ROWS_MAIN)]
    terms.append((1, (0, ZS_W), IN_ROWS_ZS))
    dx, d_mix_norm = _proj_bwd([dzm, dzs], wt_in, terms, x, mix_norm, dx1, "in_proj_bwd", dep=token, rows=512)
    token = comm.send_late_grads([d_qn, d_kvn, d_mix_norm, loss_acc])
    return dx.reshape(batch, seq, D_MODEL), token


MESH_ID = pl.DeviceIdType.MESH
EFFECT = pltpu.SideEffectType.DATAFLOW_SIDE_EFFECTING


def _mesh_pos():
    return lax.axis_index("x"), lax.axis_index("y"), lax.axis_index("c")


def _peer(pos, d):
    x, y, c = pos
    px = 1 - x if d & 4 else x
    py = 1 - y if d & 2 else y
    pc = 1 - c if d & 1 else c
    return (px, py, pc), 4 * px + 2 * py + pc


def _copy(src_ref, land_ref, send_sems, recv_sems, a, d, pos, exchange, landing_here):
    peer, pid = _peer(pos, d)
    me = 4 * pos[0] + 2 * pos[1] + pos[2]
    if exchange:
        src, dst = src_ref.at[pid], land_ref.at[d]
    else:
        src, dst = src_ref, land_ref.at[pid if landing_here else me]
    return pltpu.make_async_remote_copy(
        src_ref=src, dst_ref=dst, send_sem=send_sems.at[a * (N_DEV - 1) + d - 1],
        recv_sem=recv_sems.at[a * (N_DEV - 1) + d - 1],
        device_id=peer, device_id_type=MESH_ID)


def _start_copies(groups, modes, name, dep=None):
    sizes = [len(g) for g in groups]
    srcs = [s for g in groups for s in g]
    lands = [lax.empty(s.shape if modes[gi] else (N_DEV,) + s.shape, s.dtype)
             for gi, g in enumerate(groups) for s in g]
    n, ng = len(srcs), len(groups)
    n_in = 2 * n + (dep is not None)

    def body(*refs):
        src_refs, land_refs = refs[:n], refs[n:2 * n]
        sems = refs[n_in:n_in + 3 * ng]
        token = refs[-1]
        pos = _mesh_pos()
        k = 0
        for gi, size in enumerate(sizes):
            for a in range(size):
                _own_copy(src_refs[k], land_refs[k], sems[3 * gi + 2], a, pos, modes[gi]).start()
                for d in range(1, N_DEV):
                    _copy(src_refs[k], land_refs[k], sems[3 * gi], sems[3 * gi + 1], a, d, pos, modes[gi],
                          landing_here=False).start()
                k += 1
        token[...] = jnp.zeros_like(token)

    sem_shapes = []
    for size in sizes:
        remote = pltpu.SemaphoreType.DMA((size * (N_DEV - 1),))
        sem_shapes += [remote, remote, pltpu.SemaphoreType.DMA((size,))]
    out = pl.pallas_call(
        body, name=name,
        out_shape=(*sem_shapes, *[pltpu.HBM(a.shape, a.dtype) for a in srcs + lands],
                   jax.ShapeDtypeStruct((8, LANES), F32)),
        in_specs=[HBM] * (2 * n) + [ANY] * (dep is not None),
        out_specs=(*[SEM] * (3 * ng), *[HBM] * (2 * n), pl.BlockSpec(memory_space=pltpu.VMEM)),
        input_output_aliases={i: 3 * ng + i for i in range(2 * n)},
        compiler_params=pltpu.CompilerParams(has_side_effects=EFFECT),
    )(*[pltpu.with_memory_space_constraint(a, pltpu.HBM) for a in srcs + lands], *([dep] if dep is not None else []))
    thru = out[3 * ng:3 * ng + 2 * n]
    handles, k = [], 0
    for gi, size in enumerate(sizes):
        handles.append((out[3 * gi:3 * gi + 3], thru[k:k + size], thru[n + k:n + k + size]))
        k += size
    return handles, out[-1]


def _own_copy(src_ref, land_ref, local_sems, a, pos, exchange):
    me = 4 * pos[0] + 2 * pos[1] + pos[2]
    src, dst = (src_ref.at[me], land_ref.at[0]) if exchange else (src_ref, land_ref.at[me])
    return pltpu.make_async_copy(src, dst, local_sems.at[a])


def _wait_copies(handle, exchange, after, name):
    sems, srcs, lands = handle
    n = len(srcs)

    def body(*refs):
        src_refs, land_refs = refs[:n], refs[n:2 * n]
        send, recv, local = refs[2 * n:2 * n + 3]
        pos = _mesh_pos()
        for a in range(n):
            _own_copy(src_refs[a], land_refs[a], local, a, pos, exchange).wait()
            for d in range(1, N_DEV):
                cp = _copy(src_refs[a], land_refs[a], send, recv, a, d, pos, exchange, landing_here=True)
                cp.wait_send()
                cp.wait_recv()

    out = pl.pallas_call(
        body, name=name,
        out_shape=tuple(pltpu.HBM(a.shape, a.dtype) for a in (*srcs, *lands)),
        in_specs=[HBM] * (2 * n) + [SEM, SEM, SEM, ANY], out_specs=[HBM] * (2 * n),
        input_output_aliases={i: i for i in range(2 * n)},
        compiler_params=pltpu.CompilerParams(has_side_effects=EFFECT),
    )(*srcs, *lands, *sems, after)
    return out[n:]


def _gather_now(a, name):
    def body(x_ref, out_ref, send_sems, recv_sems, local_sem):
        x, y, c = _mesh_pos()
        me, sibling = (x, y, c), (x, y, 1 - c)
        chips = [(1 - x, y), (x, 1 - y), (1 - x, 1 - y)]

        def slot(p):
            return out_ref.at[4 * p[0] + 2 * p[1] + p[2]]

        def copy(k, block, to, src=None):
            return pltpu.make_async_remote_copy(
                src_ref=slot(block) if src is None else src, dst_ref=slot(block), send_sem=send_sems.at[k],
                recv_sem=recv_sems.at[k], device_id=to, device_id_type=MESH_ID)

        mine = pltpu.make_async_copy(x_ref, slot(me), local_sem)
        mine.start()
        first = [copy(0, me, sibling, src=x_ref)]
        first += [copy(1 + j, me, (*chip, c), src=x_ref) for j, chip in enumerate(chips)]
        for cp in first:
            cp.start()
        passed = [copy(4 + j, (*chip, c), sibling) for j, chip in enumerate(chips)]
        for j, chip in enumerate(chips):
            copy(1 + j, (*chip, c), me).wait_recv()
            passed[j].start()
        copy(0, sibling, me).wait_recv()
        for j, chip in enumerate(chips):
            copy(4 + j, (*chip, 1 - c), me).wait_recv()
        for cp in first + passed:
            cp.wait_send()
        mine.wait()

    return pl.pallas_call(
        body, in_specs=[ANY], out_specs=ANY,
        out_shape=jax.ShapeDtypeStruct((N_DEV,) + a.shape, a.dtype),
        scratch_shapes=[pltpu.SemaphoreType.DMA((N_DEV - 1,)), pltpu.SemaphoreType.DMA((N_DEV - 1,)),
                        pltpu.SemaphoreType.DMA],
        name=name, compiler_params=pltpu.CompilerParams(has_side_effects=True))(a)


def _sum_parts(p_ref):
    g = p_ref[0].astype(F32)
    for k in range(1, N_DEV):
        g = g + p_ref[k].astype(F32)
    return g


def _adamw_update(p_ref, w_ref, m_ref, v_ref, g_ref, d_ref, nm_ref, nv_ref, patch=None):
    c1 = 1.0 - ADAM_B1 ** ADAM_STEP
    c2 = 1.0 - ADAM_B2 ** ADAM_STEP
    g = _sum_parts(p_ref)
    if patch is not None:
        g = patch(g)
    nm = ADAM_B1 * m_ref[...] + (1.0 - ADAM_B1) * g
    nv = ADAM_B2 * v_ref[...] + (1.0 - ADAM_B2) * (g * g)
    g_ref[...] = g
    nm_ref[...] = nm
    nv_ref[...] = nv
    d_ref[...] = -ADAM_LR * ((nm / c1) / (jnp.sqrt(nv / c2) + ADAM_EPS) + ADAM_WD * w_ref[...])


def _adamw_many(parts, ws, ms, vs, sums, name):
    n, ns = len(ws), len(sums)

    def body(*refs):
        ins, outs = refs[:4 * n + ns], refs[4 * n + ns:]
        for i in range(n):
            _adamw_update(ins[i], ins[n + i], ins[2 * n + i], ins[3 * n + i],
                          outs[i], outs[n + i], outs[2 * n + i], outs[3 * n + i])
        for i in range(ns):
            outs[4 * n + i][...] = _sum_parts(ins[4 * n + i])

    full = lambda a: pl.BlockSpec(a.shape, lambda: (0,) * a.ndim)
    args = [*parts, *ws, *ms, *vs, *sums]
    outs = [jax.ShapeDtypeStruct(w.shape, F32) for _ in range(4) for w in ws]
    outs += [jax.ShapeDtypeStruct(s.shape[1:], F32) for s in sums]
    res = pl.pallas_call(
        body, in_specs=[full(a) for a in args], out_specs=[full(o) for o in outs], out_shape=outs,
        name=name, compiler_params=pltpu.CompilerParams(vmem_limit_bytes=VMEM_LIMIT))(*args)
    return res[:n], res[n:2 * n], res[2 * n:3 * n], res[3 * n:4 * n], res[4 * n:]


def _adamw(parts, w, m, v, name, late=None):
    R, C = w.shape
    tr, tc = R, C
    if N_DEV * R * C * parts.dtype.itemsize > SMALL_BLOCK_BYTES:
        tr = next((t for t in range(min(R, 256) // 16 * 16, 15, -16) if R % t == 0), R)
        if tr == R:
            tc = _tile(C, 256)
    more, places = late if late is not None else (None, ())
    assert late is None or tr == R

    def body(p_ref, w_ref, m_ref, v_ref, *refs):
        def patch(g):
            more_ref, g_s = refs[0], refs[-1]
            x, y, c = _mesh_pos()
            me = 4 * x + 2 * y + c
            rows = _sum_parts(more_ref)
            g_s[...] = g
            for dev, row in places:
                g_s[row:row + rows.shape[0], :] += jnp.where(me == dev, rows, 0.0)
            return g_s[...]

        outs = refs[:4] if late is None else refs[1:5]
        _adamw_update(p_ref, w_ref, m_ref, v_ref, *outs, patch=None if late is None else patch)

    blk = pl.BlockSpec((tr, tc), lambda i, j: (i, j))
    shp = jax.ShapeDtypeStruct((R, C), F32)
    in_specs = [pl.BlockSpec((N_DEV, tr, tc), lambda i, j: (0, i, j)), blk, blk, blk]
    if late is not None:
        in_specs.append(pl.BlockSpec((N_DEV, more.shape[1], tc), lambda i, j: (0, 0, j)))
    return pl.pallas_call(
        body, grid=(R // tr, C // tc), in_specs=in_specs,
        out_specs=[blk, blk, blk, blk], out_shape=[shp, shp, shp, shp],
        scratch_shapes=[] if late is None else [pltpu.VMEM((tr, tc), F32)],
        name=name, compiler_params=_params("parallel", "parallel"))(parts, w, m, v, *([] if late is None else [more]))


SPLIT_V = 2 * D_MODEL
SPLIT_KR = SPLIT_V + Q_LORA + KV_LORA + QK_ROPE
IN_DIM = SPLIT_KR + 2 * D_MODEL
IN_ROWS_MAIN = ((0, D_MODEL), (D_MODEL, SPLIT_V), (SPLIT_KR, SPLIT_KR + D_MODEL), (SPLIT_KR + D_MODEL, IN_DIM))
IN_ROWS_ZS = (SPLIT_V, SPLIT_V + ZS_W)
ZS_ROWS = SPLIT_KR - SPLIT_V
IN_SHARD = IN_DIM // N_DEV
ZS_PIECES = tuple(
    (k, max(IN_SHARD * k, SPLIT_V) - SPLIT_V, max(IN_SHARD * k, SPLIT_V) - IN_SHARD * k)
    for k in range(N_DEV) if max(IN_SHARD * k, SPLIT_V) < min(IN_SHARD * (k + 1), SPLIT_KR))
ZS_PIECE_ROWS = ZS_ROWS // len(ZS_PIECES)
assert all(min(IN_SHARD * (k + 1), SPLIT_KR) - max(IN_SHARD * k, SPLIT_V) == ZS_PIECE_ROWS for k, _, _ in ZS_PIECES)

SMALL_EARLY = ("a_v_norm_g", "a_v_norm_b", "a_spatial_w", "a_spatial_b", "ffn_norm", "conv_b", "final_norm")
SMALL_LATE = ("q_a_norm", "kv_a_norm", "mix_norm")


def _small_2d(a):
    return a.reshape(-1, a.shape[-1])


def _cols_from_shards(g):
    return jnp.transpose(g, (1, 0, 2)).reshape(g.shape[1], N_DEV * g.shape[2])


def _shards_from_cols(a):
    R, W = a.shape
    return jnp.transpose(a.reshape(R, N_DEV, W // N_DEV), (1, 0, 2))


class _Comm:
    GATHER_GROUPS = (("w_uq", "w_ukv", "w_out"), ("w_up", "conv_w", "w_down"))
    FFN_GRADS = ("w_down", "w_up", "conv_w", "w_out")
    TRANSPOSED = ("w_in", "w_up", "w_uq")

    def __init__(self, shards):
        local = {n: a.astype(F32 if n == "conv_w" else BF16) for n, a in shards.items()}
        self.g_in = _gather_now(local["w_in"], "gather_w_in")
        groups = [[local[n] for n in g] for g in self.GATHER_GROUPS]
        (self.h_mla, self.h_ffn), _ = _start_copies(groups, [False] * 2, "gather_start", dep=self.g_in)

    def in_weights(self):
        return self.g_in.reshape(IN_DIM, D_MODEL)

    def mla_weights(self, after):
        g_uq, g_ukv, g_out = _wait_copies(self.h_mla, False, after, "gather_wait_mla")
        wuq_p = jnp.pad(g_uq, ((0, 0), (0, HEAD_PAD - QK_HEAD), (0, 0))).reshape(MLA_HEADS * HEAD_PAD, Q_LORA)
        return wuq_p, _cols_from_shards(g_ukv), g_out.reshape(D_MODEL, D_MODEL)

    def ffn_weights(self, after):
        g_up, g_cw, g_down = _wait_copies(self.h_ffn, False, after, "gather_wait_ffn")
        return g_up.reshape(2 * D_FF, D_MODEL), _cols_from_shards(g_cw), g_down.reshape(D_FF, D_MODEL)

    def send_ffn_grads(self, d_wdown, d_wt_up, d_convw, d_wout):
        group = [d_wdown.reshape(N_DEV, D_FF // N_DEV, D_MODEL), d_wt_up.reshape(N_DEV, 2 * D_FF // N_DEV, D_MODEL),
                 _shards_from_cols(d_convw), d_wout.reshape(N_DEV, D_MODEL // N_DEV, D_MODEL)]
        (self.h_ffn_grads,), token = _start_copies([group], [True], "ffn_grads_start")
        return token

    def send_early_grads(self, d_wt_main, grads):
        hole = jnp.zeros((ZS_ROWS, D_MODEL), d_wt_main.dtype)
        d_in = jnp.concatenate([d_wt_main[:SPLIT_V], hole, d_wt_main[SPLIT_V:]], axis=0)
        blocks = d_in.reshape(N_DEV, IN_SHARD, D_MODEL)
        (self.h_small_early, self.h_in_grads), token = _start_copies(
            [grads, [blocks]], [False, True], "early_grads_start")
        return token

    def send_mla_grads(self, d_wuq_p, d_wukv, d_wt_zs):
        d_uq = d_wuq_p.reshape(MLA_HEADS, HEAD_PAD, Q_LORA)[:, :QK_HEAD, :]
        zs_blocks = jnp.zeros((N_DEV, ZS_PIECE_ROWS, D_MODEL), d_wt_zs.dtype)
        for dev, first, _ in ZS_PIECES:
            zs_blocks = zs_blocks.at[dev].set(d_wt_zs[first:first + ZS_PIECE_ROWS])
        (self.h_mla_grads,), token = _start_copies(
            [[d_uq, _shards_from_cols(d_wukv), zs_blocks]], [True], "mla_grads_start")
        return token

    def send_late_grads(self, small):
        (self.h_late_small,), token = _start_copies([small], [False], "late_grads_start")
        return token


def kernel(x, positions, mix_norm, w_in, a_v_norm_g, a_v_norm_b, a_spatial_w, a_spatial_b, q_a_norm, w_uq, kv_a_norm, w_ukv, w_out, ffn_norm, w_up, conv_w, conv_b, w_down, final_norm, loss_target, m_mix_norm, m_w_in, m_a_v_norm_g, m_a_v_norm_b, m_a_spatial_w, m_a_spatial_b, m_q_a_norm, m_w_uq, m_kv_a_norm, m_w_ukv, m_w_out, m_ffn_norm, m_w_up, m_conv_w, m_conv_b, m_w_down, m_final_norm, v_mix_norm, v_w_in, v_a_v_norm_g, v_a_v_norm_b, v_a_spatial_w, v_a_spatial_b, v_q_a_norm, v_w_uq, v_kv_a_norm, v_w_ukv, v_w_out, v_ffn_norm, v_w_up, v_conv_w, v_conv_b, v_w_down, v_final_norm):
    names = ("mix_norm", "w_in", "a_v_norm_g", "a_v_norm_b", "a_spatial_w", "a_spatial_b", "q_a_norm", "w_uq",
             "kv_a_norm", "w_ukv", "w_out", "ffn_norm", "w_up", "conv_w", "conv_b", "w_down", "final_norm")
    w = dict(zip(names, (mix_norm, w_in, a_v_norm_g, a_v_norm_b, a_spatial_w, a_spatial_b, q_a_norm, w_uq,
                         kv_a_norm, w_ukv, w_out, ffn_norm, w_up, conv_w, conv_b, w_down, final_norm)))
    m = dict(zip(names, (m_mix_norm, m_w_in, m_a_v_norm_g, m_a_v_norm_b, m_a_spatial_w, m_a_spatial_b,
                         m_q_a_norm, m_w_uq, m_kv_a_norm, m_w_ukv, m_w_out, m_ffn_norm, m_w_up, m_conv_w,
                         m_conv_b, m_w_down, m_final_norm)))
    v = dict(zip(names, (v_mix_norm, v_w_in, v_a_v_norm_g, v_a_v_norm_b, v_a_spatial_w, v_a_spatial_b,
                         v_q_a_norm, v_w_uq, v_kv_a_norm, v_w_ukv, v_w_out, v_ffn_norm, v_w_up, v_conv_w,
                         v_conv_b, v_w_down, v_final_norm)))
    shapes = {n: w[n].shape for n in names}
    def view(tree, n):
        a = tree[n].reshape(tree[n].shape[-2:])
        return a.T if n in _Comm.TRANSPOSED else a

    comm = _Comm({n: view(w, n) for n in ("w_in",) + _Comm.GATHER_GROUPS[0] + _Comm.GATHER_GROUPS[1]})

    grad_x, token = _local_step(
        x, positions, loss_target, w["mix_norm"], w["a_v_norm_g"], w["a_v_norm_b"], w["a_spatial_w"][0],
        w["a_spatial_b"][0], w["q_a_norm"], w["kv_a_norm"], w["ffn_norm"], w["conv_b"],
        w["final_norm"].reshape(1, D_MODEL), comm)

    out_g, out_d, out_m, out_v = {}, {}, {}, {}

    def update(n, parts, late=None):
        res = _adamw(parts, view(w, n), view(m, n), view(v, n), "adamw_" + n, late=late)
        out_g[n], out_d[n], out_m[n], out_v[n] = (
            (t.T if n in _Comm.TRANSPOSED else t).reshape(shapes[n]) for t in res)
        return res[1]

    def update_small(names, parts, sums, name):
        res = _adamw_many(parts, *[[_small_2d(t[n]) for n in names] for t in (w, m, v)], sums, name)
        for i, n in enumerate(names):
            out_g[n], out_d[n], out_m[n], out_v[n] = (r[i].reshape(shapes[n]) for r in res[:4])
        return res

    for n, parts in zip(_Comm.FFN_GRADS, _wait_copies(comm.h_ffn_grads, True, token, "ffn_grads_wait")):
        last = update(n, parts)
    early = _wait_copies(comm.h_small_early, False, last, "small_grads_wait")
    last = update_small(SMALL_EARLY, early, [], "adamw_small")[1][0]
    (in_parts,) = _wait_copies(comm.h_in_grads, True, last, "in_grads_wait")
    uq_parts, ukv_parts, zs_parts = _wait_copies(comm.h_mla_grads, True, in_parts, "mla_grads_wait")
    last = update("w_in", in_parts, late=(zs_parts, [(dev, row) for dev, _, row in ZS_PIECES]))
    last = update("w_uq", uq_parts)
    last = update("w_ukv", ukv_parts)
    late = _wait_copies(comm.h_late_small, False, last, "late_small_wait")
    res = update_small(SMALL_LATE, late[:-1], late[-1:], "adamw_late")
    loss = res[4][0][0, 0]

    return (loss, grad_x, *[out_g[n] for n in names], *[out_d[n] for n in names],
            *[out_m[n] for n in names], *[out_v[n] for n in names])
```

```python
import math

import jax
import jax.numpy as jnp
from jax import lax
from jax.experimental import pallas as pl
from jax.experimental.pallas import tpu as pltpu

F32 = jnp.float32
BF16 = jnp.bfloat16
KEPT = jnp.bfloat16

N_DEV = 8
D_MODEL = 1024
EPS = 1e-6
A_GROUPS = 8
CHUNK = 128
MLA_HEADS = 8
QK_NOPE = 128
QK_ROPE = 64
QK_HEAD = QK_NOPE + QK_ROPE
HEAD_PAD = 256
V_HEAD = 128
Q_LORA = 256
KV_LORA = 128
ROPE_THETA = 10000.0
D_FF = 2816
ZS_W = 512
ATTN_SCALE = QK_HEAD ** -0.5
ATTN_TILE = 512
NEG_BIG = -1e30

ADAM_LR = 0.001
ADAM_B1 = 0.9
ADAM_B2 = 0.999
ADAM_EPS = 1e-08
ADAM_WD = 0.01
ADAM_STEP = 10

VMEM_LIMIT = 56 * 1024 * 1024
SMALL_BLOCK_BYTES = 5 * 1024 * 1024
LANES = 128
SUBLANES = 8

GELU_K = math.sqrt(2.0 / math.pi)
GELU_C = 0.044715

ANY = pl.BlockSpec(memory_space=pl.ANY)
HBM = pl.BlockSpec(memory_space=pltpu.HBM)
SEM = pl.BlockSpec(memory_space=pltpu.SEMAPHORE)


def _tile(n, pref):
    for t in (pref, 512, 256, 128, 64, 32, 16, 8):
        if t <= pref and n % t == 0:
            return t
    return n


def _wide_tile(n, cap=1408):
    return next((t for t in range(min(n, cap) // LANES * LANES, 0, -LANES) if n % t == 0), n)


def _params(*sem):
    return pltpu.CompilerParams(dimension_semantics=sem, vmem_limit_bytes=VMEM_LIMIT)


def _dot(a, b):
    return jnp.dot(a, b, preferred_element_type=F32)


def _dot_nt(a, b):
    return lax.dot_general(a, b, (((1,), (1,)), ((), ())), preferred_element_type=F32)


def _dot_tn(a, b):
    return lax.dot_general(a, b, (((0,), (0,)), ((), ())), preferred_element_type=F32)


def _sigmoid(x):
    return 1.0 / (1.0 + jnp.exp(-x))


def _gelu(x):
    t = jnp.tanh(GELU_K * (x + GELU_C * x * x * x))
    return 0.5 * x * (1.0 + t)


def _gelu_and_grad(x):
    x2 = x * x
    t = jnp.tanh(GELU_K * (x + GELU_C * x * x2))
    half = 0.5 * (1.0 + t)
    return x * half, half + 0.5 * x * (1.0 - t * t) * GELU_K * (1.0 + 3.0 * GELU_C * x2)


def _in_proj(x, g, wt):
    T, Dm = x.shape
    tm = _tile(T, 512)

    def body(x_ref, g_ref, wt_ref, h_ref, zm_ref, zs_ref):
        xf = x_ref[...]
        r = lax.rsqrt(jnp.mean(xf * xf, axis=-1, keepdims=True) + EPS)
        h = (xf * r * g_ref[...]).astype(BF16)
        h_ref[...] = h
        for i, (r0, r1) in enumerate(IN_ROWS_MAIN):
            zm_ref[:, i * D_MODEL:(i + 1) * D_MODEL] = _dot_nt(h, wt_ref[r0:r1, :]).astype(KEPT)
        zs_ref[...] = _dot_nt(h, wt_ref[IN_ROWS_ZS[0]:IN_ROWS_ZS[1], :])

    row = lambda n: pl.BlockSpec((tm, n), lambda i: (i, 0))
    return pl.pallas_call(
        body, grid=(T // tm,),
        in_specs=[row(Dm), pl.BlockSpec((1, Dm), lambda i: (0, 0)), pl.BlockSpec(wt.shape, lambda i: (0, 0))],
        out_specs=[row(Dm), row(4 * D_MODEL), row(ZS_W)],
        out_shape=[jax.ShapeDtypeStruct((T, Dm), BF16), jax.ShapeDtypeStruct((T, 4 * D_MODEL), KEPT),
                   jax.ShapeDtypeStruct((T, ZS_W), F32)],
        name="in_proj", compiler_params=_params("parallel"))(x, g, wt)


def _proj_bwd(acts, wt, terms, x, g, dres, name, w2=None, dep=None, rows=256):
    T, Dm = x.shape
    tm = _tile(T, rows)
    n_a = len(acts)

    def body(*refs):
        ins, outs = refs[:n_a + 4 + (w2 is not None) + (dep is not None)], refs[-2 - (w2 is not None):]
        wt_ref, x_ref, g_ref, dres_ref = ins[n_a:n_a + 4]
        dx_ref, dg_ref = outs[0], outs[1]

        @pl.when(pl.program_id(0) == 0)
        def _():
            dg_ref[...] = jnp.zeros_like(dg_ref)

        dy = None
        for i, (c0, c1), (r0, r1) in terms:
            t = _dot(ins[i][:, c0:c1], wt_ref[r0:r1, :])
            dy = t if dy is None else dy + t
        xf = x_ref[...]
        r = lax.rsqrt(jnp.mean(xf * xf, axis=-1, keepdims=True) + EPS)
        xh = xf * r
        dg_ref[...] += jnp.sum(dy * xh, axis=0, keepdims=True)
        dxh = dy * g_ref[...]
        dx = dres_ref[...] + r * (dxh - xh * jnp.mean(dxh * xh, axis=-1, keepdims=True))
        dx_ref[...] = dx
        if w2 is not None:
            outs[2][...] = _dot_nt(dx.astype(BF16), ins[n_a + 4][...]).astype(KEPT)

    row = pl.BlockSpec((tm, Dm), lambda i: (i, 0))
    vec = pl.BlockSpec((1, Dm), lambda i: (0, 0))
    in_specs = [pl.BlockSpec((tm, a.shape[1]), lambda i: (i, 0)) for a in acts]
    in_specs += [pl.BlockSpec(wt.shape, lambda i: (0, 0)), row, vec, row]
    args = [*acts, wt, x, g, dres]
    out_specs = [row, vec]
    out_shape = [jax.ShapeDtypeStruct((T, Dm), F32), jax.ShapeDtypeStruct((1, Dm), F32)]
    if w2 is not None:
        in_specs.append(pl.BlockSpec(w2.shape, lambda i: (0, 0)))
        args.append(w2)
        out_specs.append(pl.BlockSpec((tm, w2.shape[0]), lambda i: (i, 0)))
        out_shape.append(jax.ShapeDtypeStruct((T, w2.shape[0]), KEPT))
    if dep is not None:
        in_specs.append(ANY)
        args.append(dep)
    return pl.pallas_call(
        body, grid=(T // tm,), in_specs=in_specs, out_specs=out_specs, out_shape=out_shape,
        name=name, compiler_params=_params("arbitrary"))(*args)


def _mm_tn(a, b, name, dep=None, rows=None, row0=0, into=None):
    T, M = a.shape
    N = b.shape[1]
    tm, tn, tt = _wide_tile(M), _wide_tile(N), _tile(T, 2048)
    n_t = T // tt
    off = row0 // tm
    extra = ([dep] if dep is not None else []) + ([into] if into is not None else [])

    def body(a_ref, b_ref, *refs):
        o_ref, acc_ref = refs[-2:]
        t = pl.program_id(2)

        @pl.when(t == 0)
        def _():
            acc_ref[...] = jnp.zeros_like(acc_ref)

        acc_ref[...] += _dot_tn(a_ref[...].astype(BF16), b_ref[...].astype(BF16))

        @pl.when(t == n_t - 1)
        def _():
            o_ref[...] = acc_ref[...].astype(BF16)

    return pl.pallas_call(
        body, grid=(M // tm, N // tn, n_t),
        in_specs=[pl.BlockSpec((tt, tm), lambda i, j, t: (t, i)),
                  pl.BlockSpec((tt, tn), lambda i, j, t: (t, j))] + [ANY] * len(extra),
        out_specs=pl.BlockSpec((tm, tn), lambda i, j, t: (i + off, j)),
        out_shape=jax.ShapeDtypeStruct((rows or M, N), BF16),
        scratch_shapes=[pltpu.VMEM((tm, tn), F32)],
        input_output_aliases={} if into is None else {1 + len(extra): 0},
        name=name, compiler_params=_params("parallel", "parallel", "arbitrary"))(a, b, *extra)


def _layer_norm_fwd(gv, g, b):
    mu = jnp.mean(gv, axis=-1, keepdims=True)
    xc = gv - mu
    rs = lax.rsqrt(jnp.mean(xc * xc, axis=-1, keepdims=True) + EPS)
    xh = xc * rs
    return xh, rs, xh * g + b


def _tri_mask(transposed=False):
    r = lax.broadcasted_iota(jnp.int32, (CHUNK, CHUNK), 0)
    c = lax.broadcasted_iota(jnp.int32, (CHUNK, CHUNK), 1)
    return r <= c if transposed else c <= r


def _mixer_a_fwd(zm, av_g, av_b, w_s, b_col):
    T = zm.shape[0]
    tm = _tile(T, 512)
    n_chunk = tm // CHUNK

    def body(u_ref, v_ref, ga_ref, g_ref, b_ref, w_ref, bc_ref, y_ref, vn_s, mx_s):
        gu = _gelu(u_ref[...].astype(F32))
        _, _, vn = _layer_norm_fwd(_gelu(v_ref[...].astype(F32)), g_ref[...], b_ref[...])
        vn_s[...] = vn.astype(BF16)
        tri = _tri_mask()
        for gi in range(A_GROUPS):
            wm = jnp.where(tri, w_ref[gi], 0.0).astype(BF16)
            cols = slice(gi * CHUNK, (gi + 1) * CHUNK)
            for n in range(n_chunk):
                rows = slice(n * CHUNK, (n + 1) * CHUNK)
                mx_s[rows, cols] = _dot(wm, vn_s[rows, cols]) + bc_ref[gi]
        y_ref[...] = (_sigmoid(ga_ref[...].astype(F32)) * gu * mx_s[...]).astype(KEPT)

    col = lambda c: pl.BlockSpec((tm, D_MODEL), lambda i: (i, c))
    vec = pl.BlockSpec((1, D_MODEL), lambda i: (0, 0))
    return pl.pallas_call(
        body, grid=(T // tm,),
        in_specs=[col(0), col(1), col(2), vec, vec,
                  pl.BlockSpec((A_GROUPS, CHUNK, CHUNK), lambda i: (0, 0, 0)),
                  pl.BlockSpec((A_GROUPS, CHUNK, 1), lambda i: (0, 0, 0))],
        out_specs=pl.BlockSpec((tm, D_MODEL), lambda i: (i, 0)),
        out_shape=jax.ShapeDtypeStruct((T, D_MODEL), KEPT),
        scratch_shapes=[pltpu.VMEM((tm, D_MODEL), BF16), pltpu.VMEM((tm, D_MODEL), F32)],
        name="mixer_a_fwd", compiler_params=_params("parallel"))(zm, zm, zm, av_g, av_b, w_s, b_col)


def _mixer_bwd(zm, o, dm, av_g, av_b, w_s, w_st, b_col, dep):
    T = zm.shape[0]
    tm = _tile(T, 256)
    n_chunk = tm // CHUNK

    def body(u_ref, v_ref, ga_ref, gb_ref, o_ref, dm_ref, g_ref, b_ref, w_ref, wt_ref, bc_ref, dep_ref,
             dz_ref, do_ref, dg_ref, db_ref, dw_ref, dbs_ref, vn_s, mx_s, dmx_s, dvn_s):
        @pl.when(pl.program_id(0) == 0)
        def _():
            dg_ref[...] = jnp.zeros_like(dg_ref)
            db_ref[...] = jnp.zeros_like(db_ref)
            dw_ref[...] = jnp.zeros_like(dw_ref)
            dbs_ref[...] = jnp.zeros_like(dbs_ref)

        dm_v = dm_ref[...].astype(F32)
        gb = gb_ref[...].astype(F32)
        sb = _sigmoid(gb)
        o_v = o_ref[...].astype(F32)
        do_ref[...] = (dm_v * sb).astype(BF16)
        dz_ref[:, 3 * D_MODEL:4 * D_MODEL] = (dm_v * o_v * sb * (1.0 - sb)).astype(BF16)
        u = u_ref[...].astype(F32)
        v = v_ref[...].astype(F32)
        gu, gu_grad = _gelu_and_grad(u)
        gv, gv_grad = _gelu_and_grad(v)
        xh, rs, vn = _layer_norm_fwd(gv, g_ref[...], b_ref[...])
        vn_s[...] = vn.astype(BF16)
        tri = _tri_mask()
        for gi in range(A_GROUPS):
            wm = jnp.where(tri, w_ref[gi], 0.0).astype(BF16)
            cols = slice(gi * CHUNK, (gi + 1) * CHUNK)
            for n in range(n_chunk):
                rows = slice(n * CHUNK, (n + 1) * CHUNK)
                mx_s[rows, cols] = _dot(wm, vn_s[rows, cols]) + bc_ref[gi]
        mixed = mx_s[...]
        sa = _sigmoid(ga_ref[...].astype(F32))
        dya = dm_v * sa
        dz_ref[:, 2 * D_MODEL:3 * D_MODEL] = (dm_v * gu * mixed * sa * (1.0 - sa)).astype(BF16)
        dz_ref[:, 0:D_MODEL] = (dya * mixed * gu_grad).astype(BF16)
        dmx = dya * gu
        dmx_s[...] = dmx.astype(BF16)
        tri_t = _tri_mask(transposed=True)
        for gi in range(A_GROUPS):
            wmt = jnp.where(tri_t, wt_ref[gi], 0.0).astype(BF16)
            cols = slice(gi * CHUNK, (gi + 1) * CHUNK)
            dw_acc = jnp.zeros((CHUNK, CHUNK), F32)
            dmx_sum = jnp.zeros((CHUNK, CHUNK), F32)
            for n in range(n_chunk):
                rows = slice(n * CHUNK, (n + 1) * CHUNK)
                blk = dmx_s[rows, cols]
                dvn_s[rows, cols] = _dot(wmt, blk)
                dw_acc = dw_acc + _dot_nt(blk, vn_s[rows, cols])
                dmx_sum = dmx_sum + dmx[rows, cols]
            dw_ref[gi] += jnp.where(tri, dw_acc, 0.0)
            dbs_ref[gi] += jnp.sum(dmx_sum, axis=-1, keepdims=True)
        dvn = dvn_s[...]
        dg_ref[...] += jnp.sum(dvn * xh, axis=0, keepdims=True)
        db_ref[...] += jnp.sum(dvn, axis=0, keepdims=True)
        dxh = dvn * g_ref[...]
        dgv = rs * (dxh - jnp.mean(dxh, axis=-1, keepdims=True)
                    - xh * jnp.mean(dxh * xh, axis=-1, keepdims=True))
        dz_ref[:, D_MODEL:2 * D_MODEL] = (dgv * gv_grad).astype(BF16)

    col = lambda c: pl.BlockSpec((tm, D_MODEL), lambda i: (i, c))
    row = pl.BlockSpec((tm, D_MODEL), lambda i: (i, 0))
    vec = pl.BlockSpec((1, D_MODEL), lambda i: (0, 0))
    wsp = pl.BlockSpec((A_GROUPS, CHUNK, CHUNK), lambda i: (0, 0, 0))
    bsp = pl.BlockSpec((A_GROUPS, CHUNK, 1), lambda i: (0, 0, 0))
    return pl.pallas_call(
        body, grid=(T // tm,),
        in_specs=[col(0), col(1), col(2), col(3), row, row, vec, vec, wsp, wsp, bsp, ANY],
        out_specs=[pl.BlockSpec((tm, 4 * D_MODEL), lambda i: (i, 0)), row, vec, vec, wsp, bsp],
        out_shape=[jax.ShapeDtypeStruct((T, 4 * D_MODEL), BF16), jax.ShapeDtypeStruct((T, D_MODEL), BF16),
                   jax.ShapeDtypeStruct((1, D_MODEL), F32), jax.ShapeDtypeStruct((1, D_MODEL), F32),
                   jax.ShapeDtypeStruct((A_GROUPS, CHUNK, CHUNK), F32),
                   jax.ShapeDtypeStruct((A_GROUPS, CHUNK, 1), F32)],
        scratch_shapes=[pltpu.VMEM((tm, D_MODEL), BF16), pltpu.VMEM((tm, D_MODEL), F32),
                        pltpu.VMEM((tm, D_MODEL), BF16), pltpu.VMEM((tm, D_MODEL), F32)],
        name="mixer_bwd", compiler_params=_params("arbitrary"))(
            zm, zm, zm, zm, o, dm, av_g, av_b, w_s, w_st, b_col, dep)


def _rope_tables(pos_ref, invf_ref):
    ang = pos_ref[...].astype(F32) * invf_ref[...]
    lane = lax.broadcasted_iota(jnp.int32, ang.shape, 1)
    cos, sin = jnp.cos(ang), jnp.sin(ang)
    c = jnp.where(lane < QK_ROPE, cos, 0.0)
    sa = jnp.where(lane < QK_ROPE // 2, -sin, 0.0)
    sb = jnp.where((lane >= QK_ROPE // 2) & (lane < QK_ROPE), sin, 0.0)
    return c, sa, sb


def _rope(blk, tabs):
    c, sa, sb = tabs
    return blk * c + pltpu.roll(blk, LANES - QK_ROPE // 2, 1) * sa + pltpu.roll(blk, QK_ROPE // 2, 1) * sb


def _rope_t(dout, tabs):
    c, sa, sb = tabs
    return dout * c + pltpu.roll(dout * sa, QK_ROPE // 2, 1) + pltpu.roll(dout * sb, LANES - QK_ROPE // 2, 1)


def _rms_small(x, g):
    r = lax.rsqrt(jnp.mean(x * x, axis=-1, keepdims=True) + EPS)
    xh = x * r
    return xh, r, xh * g


def _mla_prep_fwd(zs, pos, invf, qg, kvg, wuq_p, wukv):
    T = zs.shape[0]
    tm = _tile(T, 512)
    HW = MLA_HEADS * HEAD_PAD

    def body(zs_ref, pos_ref, invf_ref, qg_ref, kvg_ref, wq_ref, wkv_ref, q_ref, k_ref, v_ref):
        tabs = _rope_tables(pos_ref, invf_ref)
        _, _, cqn = _rms_small(zs_ref[:, 0:Q_LORA], qg_ref[...])
        _, _, ckvn = _rms_small(zs_ref[:, Q_LORA:Q_LORA + KV_LORA], kvg_ref[...])
        q = _dot_nt(cqn.astype(BF16), wq_ref[...]) * ATTN_SCALE
        kv = _dot(ckvn.astype(BF16), wkv_ref[...])
        kr = _rope(zs_ref[:, Q_LORA + KV_LORA:ZS_W], tabs).astype(BF16)
        for h in range(MLA_HEADS):
            b0 = h * HEAD_PAD
            q_ref[:, b0:b0 + QK_NOPE] = q[:, b0:b0 + QK_NOPE].astype(BF16)
            q_ref[:, b0 + QK_NOPE:b0 + HEAD_PAD] = _rope(q[:, b0 + QK_NOPE:b0 + HEAD_PAD], tabs).astype(BF16)
            k_ref[:, b0:b0 + QK_NOPE] = kv[:, b0:b0 + QK_NOPE].astype(BF16)
            k_ref[:, b0 + QK_NOPE:b0 + HEAD_PAD] = kr
            v_ref[:, h * V_HEAD:(h + 1) * V_HEAD] = kv[:, b0 + QK_NOPE:b0 + HEAD_PAD].astype(BF16)

    full = lambda a: pl.BlockSpec(a.shape, lambda i: (0,) * a.ndim)
    return pl.pallas_call(
        body, grid=(T // tm,),
        in_specs=[pl.BlockSpec((tm, ZS_W), lambda i: (i, 0)), pl.BlockSpec((tm, 1), lambda i: (i, 0)),
                  full(invf), full(qg), full(kvg), full(wuq_p), full(wukv)],
        out_specs=[pl.BlockSpec((tm, HW), lambda i: (i, 0)), pl.BlockSpec((tm, HW), lambda i: (i, 0)),
                   pl.BlockSpec((tm, D_MODEL), lambda i: (i, 0))],
        out_shape=[jax.ShapeDtypeStruct((T, HW), BF16), jax.ShapeDtypeStruct((T, HW), BF16),
                   jax.ShapeDtypeStruct((T, D_MODEL), BF16)],
        name="mla_prep_fwd", compiler_params=_params("parallel"))(zs, pos, invf, qg, kvg, wuq_p, wukv)


def _mla_prep_bwd(zs, pos, invf, qg, kvg, wuq_p, wukv, dq, dk, dv):
    T = zs.shape[0]
    tm = _tile(T, 512)
    n_t = T // tm
    HW = MLA_HEADS * HEAD_PAD

    def body(zs_ref, pos_ref, invf_ref, qg_ref, kvg_ref, wq_ref, wkv_ref, dq_ref, dk_ref, dv_ref,
             dzs_ref, dwq_ref, dwkv_ref, dqg_ref, dkvg_ref, dqp_ref, dkv_ref, accq_ref, acckv_ref):
        @pl.when(pl.program_id(0) == 0)
        def _():
            dqg_ref[...] = jnp.zeros_like(dqg_ref)
            dkvg_ref[...] = jnp.zeros_like(dkvg_ref)
            accq_ref[...] = jnp.zeros_like(accq_ref)
            acckv_ref[...] = jnp.zeros_like(acckv_ref)

        tabs = _rope_tables(pos_ref, invf_ref)
        cqh, rq, cqn = _rms_small(zs_ref[:, 0:Q_LORA], qg_ref[...])
        ckvh, rkv, ckvn = _rms_small(zs_ref[:, Q_LORA:Q_LORA + KV_LORA], kvg_ref[...])
        dkr = jnp.zeros((tm, LANES), F32)
        for h in range(MLA_HEADS):
            b0 = h * HEAD_PAD
            dqp_ref[:, b0:b0 + QK_NOPE] = dq_ref[:, b0:b0 + QK_NOPE]
            dqp_ref[:, b0 + QK_NOPE:b0 + HEAD_PAD] = _rope_t(
                dq_ref[:, b0 + QK_NOPE:b0 + HEAD_PAD].astype(F32), tabs).astype(BF16)
            dkv_ref[:, b0:b0 + QK_NOPE] = dk_ref[:, b0:b0 + QK_NOPE]
            dkv_ref[:, b0 + QK_NOPE:b0 + HEAD_PAD] = dv_ref[:, h * V_HEAD:(h + 1) * V_HEAD]
            dkr = dkr + dk_ref[:, b0 + QK_NOPE:b0 + HEAD_PAD].astype(F32)
        accq_ref[...] += _dot_tn(dqp_ref[...], cqn.astype(BF16))
        acckv_ref[...] += _dot_tn(ckvn.astype(BF16), dkv_ref[...])

        @pl.when(pl.program_id(0) == n_t - 1)
        def _():
            dwq_ref[...] = accq_ref[...].astype(BF16)
            dwkv_ref[...] = acckv_ref[...].astype(BF16)

        dcqn = _dot(dqp_ref[...], wq_ref[...])
        dckvn = _dot_nt(dkv_ref[...], wkv_ref[...])
        dqg_ref[...] += jnp.sum(dcqn * cqh, axis=0, keepdims=True)
        dkvg_ref[...] += jnp.sum(dckvn * ckvh, axis=0, keepdims=True)
        dxh = dcqn * qg_ref[...]
        dzs_ref[:, 0:Q_LORA] = (rq * (dxh - cqh * jnp.mean(dxh * cqh, axis=-1, keepdims=True))).astype(BF16)
        dxh = dckvn * kvg_ref[...]
        dzs_ref[:, Q_LORA:Q_LORA + KV_LORA] = (
            rkv * (dxh - ckvh * jnp.mean(dxh * ckvh, axis=-1, keepdims=True))).astype(BF16)
        dzs_ref[:, Q_LORA + KV_LORA:ZS_W] = _rope_t(dkr, tabs).astype(BF16)

    full = lambda a: pl.BlockSpec(a.shape, lambda i: (0,) * a.ndim)
    rowb = lambda w: pl.BlockSpec((tm, w), lambda i: (i, 0))
    return pl.pallas_call(
        body, grid=(T // tm,),
        in_specs=[rowb(ZS_W), rowb(1), full(invf), full(qg), full(kvg), full(wuq_p), full(wukv),
                  rowb(HW), rowb(HW), rowb(D_MODEL)],
        out_specs=[rowb(ZS_W), full(wuq_p), full(wukv), full(qg), full(kvg)],
        out_shape=[jax.ShapeDtypeStruct((T, ZS_W), BF16), jax.ShapeDtypeStruct(wuq_p.shape, BF16),
                   jax.ShapeDtypeStruct(wukv.shape, BF16), jax.ShapeDtypeStruct(qg.shape, F32),
                   jax.ShapeDtypeStruct(kvg.shape, F32)],
        scratch_shapes=[pltpu.VMEM((tm, HW), BF16), pltpu.VMEM((tm, HW), BF16),
                        pltpu.VMEM(wuq_p.shape, F32), pltpu.VMEM(wukv.shape, F32)],
        name="mla_prep_bwd", compiler_params=_params("arbitrary"))(
            zs, pos, invf, qg, kvg, wuq_p, wukv, dq, dk, dv)


def _causal(tq, kmax, q0):
    r = lax.broadcasted_iota(jnp.int32, (tq, kmax), 0) + q0
    c = lax.broadcasted_iota(jnp.int32, (tq, kmax), 1)
    return c <= r


def _attn_fwd(q, k, v, batch, seq):
    tq = _tile(seq, ATTN_TILE)
    nq = seq // tq

    def body(q_ref, k_ref, v_ref, o_ref, lse_ref):
        diag = _causal(tq, tq, 0)
        for qi in range(nq):
            rows = slice(qi * tq, (qi + 1) * tq)
            qr = q_ref[rows, :]
            s_d = jnp.where(diag, _dot_nt(qr, k_ref[rows, :]), NEG_BIG)
            m = jnp.max(s_d, axis=-1, keepdims=True)
            if qi > 0:
                before = slice(0, qi * tq)
                s_b = _dot_nt(qr, k_ref[before, :])
                m = jnp.maximum(m, jnp.max(s_b, axis=-1, keepdims=True))
                p_b = jnp.exp(s_b - m)
                l = jnp.sum(p_b, axis=-1, keepdims=True)
                acc = _dot(p_b.astype(BF16), v_ref[before, :])
            p_d = jnp.exp(s_d - m)
            l_d = jnp.sum(p_d, axis=-1, keepdims=True)
            acc_d = _dot(p_d.astype(BF16), v_ref[rows, :])
            l, acc = (l + l_d, acc + acc_d) if qi > 0 else (l_d, acc_d)
            o_ref[rows, :] = (acc / l).astype(KEPT)
            lse_ref[rows, :] = jnp.broadcast_to(m + jnp.log(l), (tq, V_HEAD))

    return pl.pallas_call(
        body, grid=(batch, MLA_HEADS),
        in_specs=[pl.BlockSpec((seq, HEAD_PAD), lambda b, h: (b, h)),
                  pl.BlockSpec((seq, HEAD_PAD), lambda b, h: (b, h)),
                  pl.BlockSpec((seq, V_HEAD), lambda b, h: (b, h))],
        out_specs=[pl.BlockSpec((seq, V_HEAD), lambda b, h: (b, h)),
                   pl.BlockSpec((seq, V_HEAD), lambda b, h: (b, h))],
        out_shape=[jax.ShapeDtypeStruct((batch * seq, D_MODEL), KEPT),
                   jax.ShapeDtypeStruct((batch * seq, D_MODEL), F32)],
        name="attn_fwd", compiler_params=_params("parallel", "parallel"))(q, k, v)


def _attn_bwd(q, k, v, o, do, lse, batch, seq, dep):
    tq = _tile(seq, ATTN_TILE)
    nq = seq // tq

    def body(q_ref, k_ref, v_ref, o_ref, do_ref, lse_ref, dep_ref, dq_ref, dk_ref, dv_ref, dk_acc, dv_acc):
        dk_acc[...] = jnp.zeros_like(dk_acc)
        dv_acc[...] = jnp.zeros_like(dv_acc)
        for qi in range(nq):
            rows = slice(qi * tq, (qi + 1) * tq)
            kmax = (qi + 1) * tq
            qr = q_ref[rows, :]
            dor = do_ref[rows, :]
            kk = k_ref[0:kmax, :]
            s = _dot_nt(qr, kk)
            p = jnp.where(_causal(tq, kmax, qi * tq), jnp.exp(s - lse_ref[rows, 0:1]), 0.0)
            dp = _dot_nt(dor, v_ref[0:kmax, :])
            delta = jnp.sum(dor.astype(F32) * o_ref[rows, :].astype(F32), axis=-1, keepdims=True)
            ds = (p * (dp - delta)).astype(BF16)
            dq_ref[rows, :] = (_dot(ds, kk) * ATTN_SCALE).astype(BF16)
            dk_acc[0:kmax, :] += _dot_tn(ds, qr)
            dv_acc[0:kmax, :] += _dot_tn(p.astype(BF16), dor)
        dk_ref[...] = dk_acc[...].astype(BF16)
        dv_ref[...] = dv_acc[...].astype(BF16)

    qspec = pl.BlockSpec((seq, HEAD_PAD), lambda b, h: (b, h))
    vspec = pl.BlockSpec((seq, V_HEAD), lambda b, h: (b, h))
    T = batch * seq
    return pl.pallas_call(
        body, grid=(batch, MLA_HEADS),
        in_specs=[qspec, qspec, vspec, vspec, vspec, vspec, ANY],
        out_specs=[qspec, qspec, vspec],
        out_shape=[jax.ShapeDtypeStruct((T, MLA_HEADS * HEAD_PAD), BF16),
                   jax.ShapeDtypeStruct((T, MLA_HEADS * HEAD_PAD), BF16),
                   jax.ShapeDtypeStruct((T, D_MODEL), BF16)],
        scratch_shapes=[pltpu.VMEM((seq, HEAD_PAD), F32), pltpu.VMEM((seq, V_HEAD), F32)],
        name="attn_bwd", compiler_params=_params("parallel", "parallel"))(q, k, v, o, do, lse, dep)


def _merge_out(x, yag, zm, o, w_out, ffn_g):
    T = x.shape[0]
    tm = _tile(T, 512)

    def body(x_ref, ya_ref, gb_ref, o_ref, w_ref, g_ref, mg_ref, x1_ref, h2_ref):
        mg = (ya_ref[...].astype(F32) + _sigmoid(gb_ref[...].astype(F32)) * o_ref[...].astype(F32)).astype(BF16)
        mg_ref[...] = mg
        x1 = x_ref[...] + _dot(mg, w_ref[...])
        x1_ref[...] = x1
        r = lax.rsqrt(jnp.mean(x1 * x1, axis=-1, keepdims=True) + EPS)
        h2_ref[...] = (x1 * r * g_ref[...]).astype(BF16)

    row = pl.BlockSpec((tm, D_MODEL), lambda i: (i, 0))
    return pl.pallas_call(
        body, grid=(T // tm,),
        in_specs=[row, row, pl.BlockSpec((tm, D_MODEL), lambda i: (i, 3)), row,
                  pl.BlockSpec((D_MODEL, D_MODEL), lambda i: (0, 0)), pl.BlockSpec((1, D_MODEL), lambda i: (0, 0))],
        out_specs=[row, row, row],
        out_shape=[jax.ShapeDtypeStruct((T, D_MODEL), BF16), jax.ShapeDtypeStruct((T, D_MODEL), F32),
                   jax.ShapeDtypeStruct((T, D_MODEL), BF16)],
        name="merge_out", compiler_params=_params("parallel"))(x, yag, zm, o, w_out, ffn_g)


FF_TILE = 256
FF_BLOCKS = D_FF // FF_TILE
FFB_TILE = 256
UP_ROWS = 512
EDGE = 16


def _shift_up(x, k):
    n = x.shape[0]
    row = lax.broadcasted_iota(jnp.int32, x.shape, 0)
    return jnp.where(row < n - k, pltpu.roll(x, n - k, 0), 0.0)


def _up_act(h2, wt_up, cw, cb, batch, seq):
    def body(h_ref, wug_ref, wuv_ref, wg_ref, wv_ref, bg_ref, bv_ref, ug_ref, uv_ref, g_ref, v_ref, a_ref,
             ug_s, uv_s):
        for s in (ug_s, uv_s):
            s[0:SUBLANES, :] = jnp.zeros((SUBLANES, FF_TILE), F32)

        def conv(s, w_ref, b_ref, r0):
            return (b_ref[...] + w_ref[2:3, :] * s[r0:r0 + UP_ROWS, :]
                    + w_ref[1:2, :] * s[r0 - 1:r0 - 1 + UP_ROWS, :]
                    + w_ref[0:1, :] * s[r0 - 2:r0 - 2 + UP_ROWS, :])

        for c in range(seq // UP_ROWS):
            rows = slice(c * UP_ROWS, (c + 1) * UP_ROWS)
            r0 = SUBLANES + c * UP_ROWS
            h = h_ref[rows, :]
            for w_ref, u_ref, s in ((wug_ref, ug_ref, ug_s), (wuv_ref, uv_ref, uv_s)):
                u = _dot_nt(h, w_ref[...])
                u_ref[rows, :] = u.astype(KEPT)
                s[r0:r0 + UP_ROWS, :] = u
            gate, val = conv(ug_s, wg_ref, bg_ref, r0), conv(uv_s, wv_ref, bv_ref, r0)
            g_ref[rows, :] = gate.astype(KEPT)
            v_ref[rows, :] = val.astype(KEPT)
            a_ref[rows, :] = (gate * _sigmoid(gate) * val).astype(BF16)

    blk = pl.BlockSpec((seq, FF_TILE), lambda b, j: (b, j))
    wup = lambda off: pl.BlockSpec((FF_TILE, D_MODEL), lambda b, j: (j + off, 0))
    wsp = lambda off: pl.BlockSpec((3, FF_TILE), lambda b, j: (0, j + off))
    bsp = lambda off: pl.BlockSpec((1, FF_TILE), lambda b, j: (0, j + off))
    T = batch * seq
    kept = jax.ShapeDtypeStruct((T, D_FF), KEPT)
    return pl.pallas_call(
        body, grid=(batch, FF_BLOCKS),
        in_specs=[pl.BlockSpec((seq, D_MODEL), lambda b, j: (b, 0)), wup(0), wup(FF_BLOCKS),
                  wsp(0), wsp(FF_BLOCKS), bsp(0), bsp(FF_BLOCKS)],
        out_specs=[blk] * 5,
        out_shape=[kept, kept, kept, kept, jax.ShapeDtypeStruct((T, D_FF), BF16)],
        scratch_shapes=[pltpu.VMEM((SUBLANES + seq, FF_TILE), F32)] * 2,
        name="up_act", compiler_params=_params("parallel", "arbitrary"))(h2, wt_up, wt_up, cw, cw, cb, cb)


def _ffn_act_bwd(upg, upv, gate, val, cw, dx2b, w_down, batch, seq):
    def half(du, x, w_ref, dx_ref, dw_ref):
        j = pl.program_id(1)
        n = du.shape[0]
        up1, up2 = pltpu.roll(du, n - 1, 0), pltpu.roll(du, n - 2, 0)
        dx_ref[...] = (w_ref[2:3, :] * du + w_ref[1:2, :] * up1 + w_ref[0:1, :] * up2).astype(BF16)
        tail = du[n - EDGE:n]
        dx_ref[n - EDGE:n, :] = (w_ref[2:3, :] * tail + w_ref[1:2, :] * _shift_up(tail, 1)
                                 + w_ref[0:1, :] * _shift_up(tail, 2)).astype(BF16)
        row = lax.broadcasted_iota(jnp.int32, (EDGE, du.shape[1]), 0)
        head, x_tail = du[0:EDGE], x[n - EDGE:n]
        wrap1 = jnp.sum(jnp.where(row >= EDGE - 1, pltpu.roll(head, EDGE - 1, 0), 0.0) * x_tail, axis=0, keepdims=True)
        wrap2 = jnp.sum(jnp.where(row >= EDGE - 2, pltpu.roll(head, EDGE - 2, 0), 0.0) * x_tail, axis=0, keepdims=True)
        dw_ref[j, 2:3, :] += jnp.sum(du * x, axis=0, keepdims=True)
        dw_ref[j, 1:2, :] += jnp.sum(up1 * x, axis=0, keepdims=True) - wrap1
        dw_ref[j, 0:1, :] += jnp.sum(up2 * x, axis=0, keepdims=True) - wrap2
        dw_ref[j, 3:4, :] += jnp.sum(du, axis=0, keepdims=True)

    def body(ug_ref, uv_ref, g_ref, v_ref, wg_ref, wv_ref, dx_ref, wd_ref, dg_ref, dv_ref, dwg_ref, dwv_ref):
        @pl.when((pl.program_id(0) == 0) & (pl.program_id(1) == 0))
        def _():
            dwg_ref[...] = jnp.zeros_like(dwg_ref)
            dwv_ref[...] = jnp.zeros_like(dwv_ref)

        gate, val = g_ref[...].astype(F32), v_ref[...].astype(F32)
        sg = _sigmoid(gate)
        dav = _dot_nt(dx_ref[...], wd_ref[...])
        half(dav * val * sg * (1.0 + gate * (1.0 - sg)), ug_ref[...].astype(F32), wg_ref, dg_ref, dwg_ref)
        half(dav * gate * sg, uv_ref[...].astype(F32), wv_ref, dv_ref, dwv_ref)

    nb = D_FF // FFB_TILE
    blk = pl.BlockSpec((seq, FFB_TILE), lambda b, j: (b, j))
    wsp = lambda off: pl.BlockSpec((3, FFB_TILE), lambda b, j: (0, j + off))
    acc = pl.BlockSpec((nb, 4, FFB_TILE), lambda b, j: (0, 0, 0))
    T = batch * seq
    dupg, dupv, dwg, dwv = pl.pallas_call(
        body, grid=(batch, nb),
        in_specs=[blk, blk, blk, blk, wsp(0), wsp(nb),
                  pl.BlockSpec((seq, D_MODEL), lambda b, j: (b, 0)),
                  pl.BlockSpec((FFB_TILE, D_MODEL), lambda b, j: (j, 0))],
        out_specs=[blk, blk, acc, acc],
        out_shape=[jax.ShapeDtypeStruct((T, D_FF), BF16), jax.ShapeDtypeStruct((T, D_FF), BF16),
                   jax.ShapeDtypeStruct((nb, 4, FFB_TILE), F32), jax.ShapeDtypeStruct((nb, 4, FFB_TILE), F32)],
        name="ffn_act_bwd", compiler_params=_params("arbitrary", "arbitrary"))(
            upg, upv, gate, val, cw, cw, dx2b, w_down)
    dwg, dwv = (jnp.transpose(a, (1, 0, 2)).reshape(4, D_FF) for a in (dwg, dwv))
    return dupg, dupv, dwg[:3], dwv[:3], dwg[3:], dwv[3:]


def _down_loss(a, w_down, x1, target, gfin):
    T = x1.shape[0]
    tm = _tile(T, 512)

    def body(a_ref, w_ref, x1_ref, t_ref, g_ref, dx_ref, dxb_ref, loss_ref, dg_ref):
        @pl.when(pl.program_id(0) == 0)
        def _():
            loss_ref[...] = jnp.zeros_like(loss_ref)
            dg_ref[...] = jnp.zeros_like(dg_ref)

        x2 = x1_ref[...] + _dot(a_ref[...], w_ref[...])
        r = lax.rsqrt(jnp.mean(x2 * x2, axis=-1, keepdims=True) + EPS)
        xh = x2 * r
        g = g_ref[...]
        diff = xh * g - t_ref[...]
        loss_ref[...] += 0.5 * jnp.sum(jnp.mean(diff * diff, axis=-1, keepdims=True))
        dy = diff * (1.0 / D_MODEL)
        dg_ref[...] += jnp.sum(dy * xh, axis=0, keepdims=True)
        dxh = dy * g
        dx = r * (dxh - xh * jnp.mean(dxh * xh, axis=-1, keepdims=True))
        dx_ref[...] = dx
        dxb_ref[...] = dx.astype(BF16)

    row = pl.BlockSpec((tm, D_MODEL), lambda i: (i, 0))
    vec = pl.BlockSpec((1, D_MODEL), lambda i: (0, 0))
    return pl.pallas_call(
        body, grid=(T // tm,),
        in_specs=[pl.BlockSpec((tm, D_FF), lambda i: (i, 0)),
                  pl.BlockSpec((D_FF, D_MODEL), lambda i: (0, 0)), row, row, vec],
        out_specs=[row, row, pl.BlockSpec((8, LANES), lambda i: (0, 0)), vec],
        out_shape=[jax.ShapeDtypeStruct((T, D_MODEL), F32), jax.ShapeDtypeStruct((T, D_MODEL), BF16),
                   jax.ShapeDtypeStruct((8, LANES), F32), jax.ShapeDtypeStruct((1, D_MODEL), F32)],
        name="down_loss", compiler_params=_params("arbitrary"))(a, w_down, x1, target, gfin)


def _local_step(x, positions, target, mix_norm, av_g, av_b, w_s, b_s, q_norm, kv_norm, ffn_norm, conv_b,
                final_norm, comm):
    batch, seq, _ = x.shape
    T = batch * seq
    x = x.reshape(T, D_MODEL)
    target = target.reshape(T, D_MODEL)
    pos = positions.reshape(T, 1)
    half = jnp.arange(0, QK_ROPE, 2, dtype=F32) / QK_ROPE
    inv_freq = 1.0 / (ROPE_THETA ** half)
    invf = jnp.concatenate([inv_freq, inv_freq, jnp.zeros((LANES - QK_ROPE,), F32)]).reshape(1, LANES)
    w_st = jnp.swapaxes(w_s, 1, 2)
    b_col = b_s.reshape(A_GROUPS, CHUNK, 1)

    wt_in = comm.in_weights()
    h, zm, zs = _in_proj(x, mix_norm, wt_in)
    yag = _mixer_a_fwd(zm, av_g, av_b, w_s, b_col)
    wuq_p, wukv, w_out = comm.mla_weights(after=yag)
    q, k, v = _mla_prep_fwd(zs, pos, invf, q_norm, kv_norm, wuq_p, wukv)
    o, lse = _attn_fwd(q, k, v, batch, seq)
    merged, x1, h2 = _merge_out(x, yag, zm, o, w_out, ffn_norm)
    wt_up, conv_w, w_down = comm.ffn_weights(after=merged)
    upg, upv, gate, val, act = _up_act(h2, wt_up, conv_w, conv_b, batch, seq)
    dx2, dx2b, loss_acc, d_final = _down_loss(act, w_down, x1, target, final_norm)

    d_wdown = _mm_tn(act, dx2b, "dw_down")
    dupg, dupv, dcwg, dcwv, dcbg, dcbv = _ffn_act_bwd(upg, upv, gate, val, conv_w, dx2b, w_down, batch, seq)
    d_wt_up = _mm_tn(dupv, h2, "dw_up_val", rows=2 * D_FF, row0=D_FF,
                     into=_mm_tn(dupg, h2, "dw_up_gate", rows=2 * D_FF))
    dx1, d_ffn_norm, dmerged = _proj_bwd(
        [dupg, dupv], wt_up, [(0, (0, D_FF), (0, D_FF)), (1, (0, D_FF), (D_FF, 2 * D_FF))],
        x1, ffn_norm, dx2, "up_proj_bwd", w2=w_out)
    d_wout = _mm_tn(merged, dx1, "dw_out")
    token = comm.send_ffn_grads(d_wdown, d_wt_up, jnp.concatenate([dcwg, dcwv], axis=1), d_wout)
    dzm, do, d_avg, d_avb, d_ws, d_bs = _mixer_bwd(zm, o, dmerged, av_g, av_b, w_s, w_st, b_col, token)
    d_wt_main = _mm_tn(dzm, h, "dw_in_main")
    token = comm.send_early_grads(d_wt_main, [
        d_avg, d_avb, _small_2d(d_ws).astype(BF16), d_bs.reshape(A_GROUPS, CHUNK), d_ffn_norm,
        jnp.concatenate([dcbg, dcbv], axis=1), d_final])
    dq, dk, dv = _attn_bwd(q, k, v, o, do, lse, batch, seq, token)
    dzs, d_wuq_p, d_wukv, d_qn, d_kvn = _mla_prep_bwd(zs, pos, invf, q_norm, kv_norm, wuq_p, wukv, dq, dk, dv)
    d_wt_zs = _mm_tn(dzs, h, "dw_in_small")
    token = comm.send_mla_grads(d_wuq_p, d_wukv, d_wt_zs)
    terms = [(0, (i * D_MODEL, (i + 1) * D_MODEL), rows) for i, rows in enumerate(IN_ROWS_MAIN)]
    terms.append((1, (0, ZS_W), IN_ROWS_ZS))
    dx, d_mix_norm = _proj_bwd([dzm, dzs], wt_in, terms, x, mix_norm, dx1, "in_proj_bwd", dep=token, rows=512)
    token = comm.send_late_grads([d_qn, d_kvn, d_mix_norm, loss_acc])
    return dx.reshape(batch, seq, D_MODEL), token


MESH_ID = pl.DeviceIdType.MESH
EFFECT = pltpu.SideEffectType.DATAFLOW_SIDE_EFFECTING


def _mesh_pos():
    return lax.axis_index("x"), lax.axis_index("y"), lax.axis_index("c")


def _peer(pos, d):
    x, y, c = pos
    px = 1 - x if d & 4 else x
    py = 1 - y if d & 2 else y
    pc = 1 - c if d & 1 else c
    return (px, py, pc), 4 * px + 2 * py + pc


def _copy(src_ref, land_ref, send_sems, recv_sems, a, d, pos, exchange, landing_here):
    peer, pid = _peer(pos, d)
    me = 4 * pos[0] + 2 * pos[1] + pos[2]
    if exchange:
        src, dst = src_ref.at[pid], land_ref.at[d]
    else:
        src, dst = src_ref, land_ref.at[pid if landing_here else me]
    return pltpu.make_async_remote_copy(
        src_ref=src, dst_ref=dst, send_sem=send_sems.at[a * (N_DEV - 1) + d - 1],
        recv_sem=recv_sems.at[a * (N_DEV - 1) + d - 1],
        device_id=peer, device_id_type=MESH_ID)


def _start_copies(groups, modes, name, dep=None):
    sizes = [len(g) for g in groups]
    srcs = [s for g in groups for s in g]
    lands = [lax.empty(s.shape if modes[gi] else (N_DEV,) + s.shape, s.dtype)
             for gi, g in enumerate(groups) for s in g]
    n, ng = len(srcs), len(groups)
    n_in = 2 * n + (dep is not None)

    def body(*refs):
        src_refs, land_refs = refs[:n], refs[n:2 * n]
        sems = refs[n_in:n_in + 3 * ng]
        token = refs[-1]
        pos = _mesh_pos()
        k = 0
        for gi, size in enumerate(sizes):
            for a in range(size):
                _own_copy(src_refs[k], land_refs[k], sems[3 * gi + 2], a, pos, modes[gi]).start()
                for d in range(1, N_DEV):
                    _copy(src_refs[k], land_refs[k], sems[3 * gi], sems[3 * gi + 1], a, d, pos, modes[gi],
                          landing_here=False).start()
                k += 1
        token[...] = jnp.zeros_like(token)

    sem_shapes = []
    for size in sizes:
        remote = pltpu.SemaphoreType.DMA((size * (N_DEV - 1),))
        sem_shapes += [remote, remote, pltpu.SemaphoreType.DMA((size,))]
    out = pl.pallas_call(
        body, name=name,
        out_shape=(*sem_shapes, *[pltpu.HBM(a.shape, a.dtype) for a in srcs + lands],
                   jax.ShapeDtypeStruct((8, LANES), F32)),
        in_specs=[HBM] * (2 * n) + [ANY] * (dep is not None),
        out_specs=(*[SEM] * (3 * ng), *[HBM] * (2 * n), pl.BlockSpec(memory_space=pltpu.VMEM)),
        input_output_aliases={i: 3 * ng + i for i in range(2 * n)},
        compiler_params=pltpu.CompilerParams(has_side_effects=EFFECT),
    )(*[pltpu.with_memory_space_constraint(a, pltpu.HBM) for a in srcs + lands], *([dep] if dep is not None else []))
    thru = out[3 * ng:3 * ng + 2 * n]
    handles, k = [], 0
    for gi, size in enumerate(sizes):
        handles.append((out[3 * gi:3 * gi + 3], thru[k:k + size], thru[n + k:n + k + size]))
        k += size
    return handles, out[-1]


def _own_copy(src_ref, land_ref, local_sems, a, pos, exchange):
    me = 4 * pos[0] + 2 * pos[1] + pos[2]
    src, dst = (src_ref.at[me], land_ref.at[0]) if exchange else (src_ref, land_ref.at[me])
    return pltpu.make_async_copy(src, dst, local_sems.at[a])


def _wait_copies(handle, exchange, after, name):
    sems, srcs, lands = handle
    n = len(srcs)

    def body(*refs):
        src_refs, land_refs = refs[:n], refs[n:2 * n]
        send, recv, local = refs[2 * n:2 * n + 3]
        pos = _mesh_pos()
        for a in range(n):
            _own_copy(src_refs[a], land_refs[a], local, a, pos, exchange).wait()
            for d in range(1, N_DEV):
                cp = _copy(src_refs[a], land_refs[a], send, recv, a, d, pos, exchange, landing_here=True)
                cp.wait_send()
                cp.wait_recv()

    out = pl.pallas_call(
        body, name=name,
        out_shape=tuple(pltpu.HBM(a.shape, a.dtype) for a in (*srcs, *lands)),
        in_specs=[HBM] * (2 * n) + [SEM, SEM, SEM, ANY], out_specs=[HBM] * (2 * n),
        input_output_aliases={i: i for i in range(2 * n)},
        compiler_params=pltpu.CompilerParams(has_side_effects=EFFECT),
    )(*srcs, *lands, *sems, after)
    return out[n:]


def _gather_now(a, name):
    def body(x_ref, out_ref, send_sems, recv_sems, local_sem):
        x, y, c = _mesh_pos()
        me, sibling = (x, y, c), (x, y, 1 - c)
        chips = [(1 - x, y), (x, 1 - y), (1 - x, 1 - y)]

        def slot(p):
            return out_ref.at[4 * p[0] + 2 * p[1] + p[2]]

        def copy(k, block, to, src=None):
            return pltpu.make_async_remote_copy(
                src_ref=slot(block) if src is None else src, dst_ref=slot(block), send_sem=send_sems.at[k],
                recv_sem=recv_sems.at[k], device_id=to, device_id_type=MESH_ID)

        mine = pltpu.make_async_copy(x_ref, slot(me), local_sem)
        mine.start()
        first = [copy(0, me, sibling, src=x_ref)]
        first += [copy(1 + j, me, (*chip, c), src=x_ref) for j, chip in enumerate(chips)]
        for cp in first:
            cp.start()
        passed = [copy(4 + j, (*chip, c), sibling) for j, chip in enumerate(chips)]
        for j, chip in enumerate(chips):
            copy(1 + j, (*chip, c), me).wait_recv()
            passed[j].start()
        copy(0, sibling, me).wait_recv()
        for j, chip in enumerate(chips):
            copy(4 + j, (*chip, 1 - c), me).wait_recv()
        for cp in first + passed:
            cp.wait_send()
        mine.wait()

    return pl.pallas_call(
        body, in_specs=[ANY], out_specs=ANY,
        out_shape=jax.ShapeDtypeStruct((N_DEV,) + a.shape, a.dtype),
        scratch_shapes=[pltpu.SemaphoreType.DMA((N_DEV - 1,)), pltpu.SemaphoreType.DMA((N_DEV - 1,)),
                        pltpu.SemaphoreType.DMA],
        name=name, compiler_params=pltpu.CompilerParams(has_side_effects=True))(a)


def _sum_parts(p_ref):
    g = p_ref[0].astype(F32)
    for k in range(1, N_DEV):
        g = g + p_ref[k].astype(F32)
    return g


def _adamw_update(p_ref, w_ref, m_ref, v_ref, g_ref, d_ref, nm_ref, nv_ref, patch=None):
    c1 = 1.0 - ADAM_B1 ** ADAM_STEP
    c2 = 1.0 - ADAM_B2 ** ADAM_STEP
    g = _sum_parts(p_ref)
    if patch is not None:
        g = patch(g)
    nm = ADAM_B1 * m_ref[...] + (1.0 - ADAM_B1) * g
    nv = ADAM_B2 * v_ref[...] + (1.0 - ADAM_B2) * (g * g)
    g_ref[...] = g
    nm_ref[...] = nm
    nv_ref[...] = nv
    d_ref[...] = -ADAM_LR * ((nm / c1) / (jnp.sqrt(nv / c2) + ADAM_EPS) + ADAM_WD * w_ref[...])


def _adamw_many(parts, ws, ms, vs, sums, name):
    n, ns = len(ws), len(sums)

    def body(*refs):
        ins, outs = refs[:4 * n + ns], refs[4 * n + ns:]
        for i in range(n):
            _adamw_update(ins[i], ins[n + i], ins[2 * n + i], ins[3 * n + i],
                          outs[i], outs[n + i], outs[2 * n + i], outs[3 * n + i])
        for i in range(ns):
            outs[4 * n + i][...] = _sum_parts(ins[4 * n + i])

    full = lambda a: pl.BlockSpec(a.shape, lambda: (0,) * a.ndim)
    args = [*parts, *ws, *ms, *vs, *sums]
    outs = [jax.ShapeDtypeStruct(w.shape, F32) for _ in range(4) for w in ws]
    outs += [jax.ShapeDtypeStruct(s.shape[1:], F32) for s in sums]
    res = pl.pallas_call(
        body, in_specs=[full(a) for a in args], out_specs=[full(o) for o in outs], out_shape=outs,
        name=name, compiler_params=pltpu.CompilerParams(vmem_limit_bytes=VMEM_LIMIT))(*args)
    return res[:n], res[n:2 * n], res[2 * n:3 * n], res[3 * n:4 * n], res[4 * n:]


def _adamw(parts, w, m, v, name, late=None):
    R, C = w.shape
    tr, tc = R, C
    if N_DEV * R * C * parts.dtype.itemsize > SMALL_BLOCK_BYTES:
        tr = next((t for t in range(min(R, 256) // 16 * 16, 15, -16) if R % t == 0), R)
        if tr == R:
            tc = _tile(C, 256)
    more, places = late if late is not None else (None, ())
    assert late is None or tr == R

    def body(p_ref, w_ref, m_ref, v_ref, *refs):
        def patch(g):
            more_ref, g_s = refs[0], refs[-1]
            x, y, c = _mesh_pos()
            me = 4 * x + 2 * y + c
            rows = _sum_parts(more_ref)
            g_s[...] = g
            for dev, row in places:
                g_s[row:row + rows.shape[0], :] += jnp.where(me == dev, rows, 0.0)
            return g_s[...]

        outs = refs[:4] if late is None else refs[1:5]
        _adamw_update(p_ref, w_ref, m_ref, v_ref, *outs, patch=None if late is None else patch)

    blk = pl.BlockSpec((tr, tc), lambda i, j: (i, j))
    shp = jax.ShapeDtypeStruct((R, C), F32)
    in_specs = [pl.BlockSpec((N_DEV, tr, tc), lambda i, j: (0, i, j)), blk, blk, blk]
    if late is not None:
        in_specs.append(pl.BlockSpec((N_DEV, more.shape[1], tc), lambda i, j: (0, 0, j)))
    return pl.pallas_call(
        body, grid=(R // tr, C // tc), in_specs=in_specs,
        out_specs=[blk, blk, blk, blk], out_shape=[shp, shp, shp, shp],
        scratch_shapes=[] if late is None else [pltpu.VMEM((tr, tc), F32)],
        name=name, compiler_params=_params("parallel", "parallel"))(parts, w, m, v, *([] if late is None else [more]))


SPLIT_V = 2 * D_MODEL
SPLIT_KR = SPLIT_V + Q_LORA + KV_LORA + QK_ROPE
IN_DIM = SPLIT_KR + 2 * D_MODEL
IN_ROWS_MAIN = ((0, D_MODEL), (D_MODEL, SPLIT_V), (SPLIT_KR, SPLIT_KR + D_MODEL), (SPLIT_KR + D_MODEL, IN_DIM))
IN_ROWS_ZS = (SPLIT_V, SPLIT_V + ZS_W)
ZS_ROWS = SPLIT_KR - SPLIT_V
IN_SHARD = IN_DIM // N_DEV
ZS_PIECES = tuple(
    (k, max(IN_SHARD * k, SPLIT_V) - SPLIT_V, max(IN_SHARD * k, SPLIT_V) - IN_SHARD * k)
    for k in range(N_DEV) if max(IN_SHARD * k, SPLIT_V) < min(IN_SHARD * (k + 1), SPLIT_KR))
ZS_PIECE_ROWS = ZS_ROWS // len(ZS_PIECES)
assert all(min(IN_SHARD * (k + 1), SPLIT_KR) - max(IN_SHARD * k, SPLIT_V) == ZS_PIECE_ROWS for k, _, _ in ZS_PIECES)

SMALL_EARLY = ("a_v_norm_g", "a_v_norm_b", "a_spatial_w", "a_spatial_b", "ffn_norm", "conv_b", "final_norm")
SMALL_LATE = ("q_a_norm", "kv_a_norm", "mix_norm")


def _small_2d(a):
    return a.reshape(-1, a.shape[-1])


def _cols_from_shards(g):
    return jnp.transpose(g, (1, 0, 2)).reshape(g.shape[1], N_DEV * g.shape[2])


def _shards_from_cols(a):
    R, W = a.shape
    return jnp.transpose(a.reshape(R, N_DEV, W // N_DEV), (1, 0, 2))


class _Comm:
    GATHER_GROUPS = (("w_uq", "w_ukv", "w_out"), ("w_up", "conv_w", "w_down"))
    FFN_GRADS = ("w_down", "w_up", "conv_w", "w_out")
    TRANSPOSED = ("w_in", "w_up", "w_uq")

    def __init__(self, shards):
        local = {n: a.astype(F32 if n == "conv_w" else BF16) for n, a in shards.items()}
        self.g_in = _gather_now(local["w_in"], "gather_w_in")
        groups = [[local[n] for n in g] for g in self.GATHER_GROUPS]
        (self.h_mla, self.h_ffn), _ = _start_copies(groups, [False] * 2, "gather_start", dep=self.g_in)

    def in_weights(self):
        return self.g_in.reshape(IN_DIM, D_MODEL)

    def mla_weights(self, after):
        g_uq, g_ukv, g_out = _wait_copies(self.h_mla, False, after, "gather_wait_mla")
        wuq_p = jnp.pad(g_uq, ((0, 0), (0, HEAD_PAD - QK_HEAD), (0, 0))).reshape(MLA_HEADS * HEAD_PAD, Q_LORA)
        return wuq_p, _cols_from_shards(g_ukv), g_out.reshape(D_MODEL, D_MODEL)

    def ffn_weights(self, after):
        g_up, g_cw, g_down = _wait_copies(self.h_ffn, False, after, "gather_wait_ffn")
        return g_up.reshape(2 * D_FF, D_MODEL), _cols_from_shards(g_cw), g_down.reshape(D_FF, D_MODEL)

    def send_ffn_grads(self, d_wdown, d_wt_up, d_convw, d_wout):
        group = [d_wdown.reshape(N_DEV, D_FF // N_DEV, D_MODEL), d_wt_up.reshape(N_DEV, 2 * D_FF // N_DEV, D_MODEL),
                 _shards_from_cols(d_convw), d_wout.reshape(N_DEV, D_MODEL // N_DEV, D_MODEL)]
        (self.h_ffn_grads,), token = _start_copies([group], [True], "ffn_grads_start")
        return token

    def send_early_grads(self, d_wt_main, grads):
        hole = jnp.zeros((ZS_ROWS, D_MODEL), d_wt_main.dtype)
        d_in = jnp.concatenate([d_wt_main[:SPLIT_V], hole, d_wt_main[SPLIT_V:]], axis=0)
        blocks = d_in.reshape(N_DEV, IN_SHARD, D_MODEL)
        (self.h_small_early, self.h_in_grads), token = _start_copies(
            [grads, [blocks]], [False, True], "early_grads_start")
        return token

    def send_mla_grads(self, d_wuq_p, d_wukv, d_wt_zs):
        d_uq = d_wuq_p.reshape(MLA_HEADS, HEAD_PAD, Q_LORA)[:, :QK_HEAD, :]
        zs_blocks = jnp.zeros((N_DEV, ZS_PIECE_ROWS, D_MODEL), d_wt_zs.dtype)
        for dev, first, _ in ZS_PIECES:
            zs_blocks = zs_blocks.at[dev].set(d_wt_zs[first:first + ZS_PIECE_ROWS])
        (self.h_mla_grads,), token = _start_copies(
            [[d_uq, _shards_from_cols(d_wukv), zs_blocks]], [True], "mla_grads_start")
        return token

    def send_late_grads(self, small):
        (self.h_late_small,), token = _start_copies([small], [False], "late_grads_start")
        return token


def kernel(x, positions, mix_norm, w_in, a_v_norm_g, a_v_norm_b, a_spatial_w, a_spatial_b, q_a_norm, w_uq, kv_a_norm, w_ukv, w_out, ffn_norm, w_up, conv_w, conv_b, w_down, final_norm, loss_target, m_mix_norm, m_w_in, m_a_v_norm_g, m_a_v_norm_b, m_a_spatial_w, m_a_spatial_b, m_q_a_norm, m_w_uq, m_kv_a_norm, m_w_ukv, m_w_out, m_ffn_norm, m_w_up, m_conv_w, m_conv_b, m_w_down, m_final_norm, v_mix_norm, v_w_in, v_a_v_norm_g, v_a_v_norm_b, v_a_spatial_w, v_a_spatial_b, v_q_a_norm, v_w_uq, v_kv_a_norm, v_w_ukv, v_w_out, v_ffn_norm, v_w_up, v_conv_w, v_conv_b, v_w_down, v_final_norm):
    names = ("mix_norm", "w_in", "a_v_norm_g", "a_v_norm_b", "a_spatial_w", "a_spatial_b", "q_a_norm", "w_uq",
             "kv_a_norm", "w_ukv", "w_out", "ffn_norm", "w_up", "conv_w", "conv_b", "w_down", "final_norm")
    w = dict(zip(names, (mix_norm, w_in, a_v_norm_g, a_v_norm_b, a_spatial_w, a_spatial_b, q_a_norm, w_uq,
                         kv_a_norm, w_ukv, w_out, ffn_norm, w_up, conv_w, conv_b, w_down, final_norm)))
    m = dict(zip(names, (m_mix_norm, m_w_in, m_a_v_norm_g, m_a_v_norm_b, m_a_spatial_w, m_a_spatial_b,
                         m_q_a_norm, m_w_uq, m_kv_a_norm, m_w_ukv, m_w_out, m_ffn_norm, m_w_up, m_conv_w,
                         m_conv_b, m_w_down, m_final_norm)))
    v = dict(zip(names, (v_mix_norm, v_w_in, v_a_v_norm_g, v_a_v_norm_b, v_a_spatial_w, v_a_spatial_b,
                         v_q_a_norm, v_w_uq, v_kv_a_norm, v_w_ukv, v_w_out, v_ffn_norm, v_w_up, v_conv_w,
                         v_conv_b, v_w_down, v_final_norm)))
    shapes = {n: w[n].shape for n in names}
    def view(tree, n):
        a = tree[n].reshape(tree[n].shape[-2:])
        return a.T if n in _Comm.TRANSPOSED else a

    comm = _Comm({n: view(w, n) for n in ("w_in",) + _Comm.GATHER_GROUPS[0] + _Comm.GATHER_GROUPS[1]})

    grad_x, token = _local_step(
        x, positions, loss_target, w["mix_norm"], w["a_v_norm_g"], w["a_v_norm_b"], w["a_spatial_w"][0],
        w["a_spatial_b"][0], w["q_a_norm"], w["kv_a_norm"], w["ffn_norm"], w["conv_b"],
        w["final_norm"].reshape(1, D_MODEL), comm)

    out_g, out_d, out_m, out_v = {}, {}, {}, {}

    def update(n, parts, late=None):
        res = _adamw(parts, view(w, n), view(m, n), view(v, n), "adamw_" + n, late=late)
        out_g[n], out_d[n], out_m[n], out_v[n] = (
            (t.T if n in _Comm.TRANSPOSED else t).reshape(shapes[n]) for t in res)
        return res[1]

    def update_small(names, parts, sums, name):
        res = _adamw_many(parts, *[[_small_2d(t[n]) for n in names] for t in (w, m, v)], sums, name)
        for i, n in enumerate(names):
            out_g[n], out_d[n], out_m[n], out_v[n] = (r[i].reshape(shapes[n]) for r in res[:4])
        return res

    for n, parts in zip(_Comm.FFN_GRADS, _wait_copies(comm.h_ffn_grads, True, token, "ffn_grads_wait")):
        last = update(n, parts)
    early = _wait_copies(comm.h_small_early, False, last, "small_grads_wait")
    last = update_small(SMALL_EARLY, early, [], "adamw_small")[1][0]
    (in_parts,) = _wait_copies(comm.h_in_grads, True, last, "in_grads_wait")
    uq_parts, ukv_parts, zs_parts = _wait_copies(comm.h_mla_grads, True, in_parts, "mla_grads_wait")
    last = update("w_in", in_parts, late=(zs_parts, [(dev, row) for dev, _, row in ZS_PIECES]))
    last = update("w_uq", uq_parts)
    last = update("w_ukv", ukv_parts)
    late = _wait_copies(comm.h_late_small, False, last, "late_small_wait")
    res = update_small(SMALL_LATE, late[:-1], late[-1:], "adamw_late")
    loss = res[4][0][0, 0]

    return (loss, grad_x, *[out_g[n] for n in names], *[out_d[n] for n in names],
            *[out_m[n] for n in names], *[out_v[n] for n in names])
```

```python
import math

import jax
import jax.numpy as jnp
from jax import lax
from jax.experimental import pallas as pl
from jax.experimental.pallas import tpu as pltpu

F32 = jnp.float32
BF16 = jnp.bfloat16
KEPT = jnp.bfloat16

N_DEV = 8
D_MODEL = 1024
EPS = 1e-6
A_GROUPS = 8
CHUNK = 128
MLA_HEADS = 8
QK_NOPE = 128
QK_ROPE = 64
QK_HEAD = QK_NOPE + QK_ROPE
HEAD_PAD = 256
V_HEAD = 128
Q_LORA = 256
KV_LORA = 128
ROPE_THETA = 10000.0
D_FF = 2816
ZS_W = 512
ATTN_SCALE = QK_HEAD ** -0.5
ATTN_TILE = 512
NEG_BIG = -1e30

ADAM_LR = 0.001
ADAM_B1 = 0.9
ADAM_B2 = 0.999
ADAM_EPS = 1e-08
ADAM_WD = 0.01
ADAM_STEP = 10

VMEM_LIMIT = 56 * 1024 * 1024
SMALL_BLOCK_BYTES = 5 * 1024 * 1024
LANES = 128
SUBLANES = 8

GELU_K = math.sqrt(2.0 / math.pi)
GELU_C = 0.044715

ANY = pl.BlockSpec(memory_space=pl.ANY)
HBM = pl.BlockSpec(memory_space=pltpu.HBM)
SEM = pl.BlockSpec(memory_space=pltpu.SEMAPHORE)


def _tile(n, pref):
    for t in (pref, 512, 256, 128, 64, 32, 16, 8):
        if t <= pref and n % t == 0:
            return t
    return n


def _wide_tile(n, cap=1408):
    return next((t for t in range(min(n, cap) // LANES * LANES, 0, -LANES) if n % t == 0), n)


def _params(*sem):
    return pltpu.CompilerParams(dimension_semantics=sem, vmem_limit_bytes=VMEM_LIMIT)


def _dot(a, b):
    return jnp.dot(a, b, preferred_element_type=F32)


def _dot_nt(a, b):
    return lax.dot_general(a, b, (((1,), (1,)), ((), ())), preferred_element_type=F32)


def _dot_tn(a, b):
    return lax.dot_general(a, b, (((0,), (0,)), ((), ())), preferred_element_type=F32)


def _sigmoid(x):
    return 1.0 / (1.0 + jnp.exp(-x))


def _gelu(x):
    t = jnp.tanh(GELU_K * (x + GELU_C * x * x * x))
    return 0.5 * x * (1.0 + t)


def _gelu_and_grad(x):
    x2 = x * x
    t = jnp.tanh(GELU_K * (x + GELU_C * x * x2))
    half = 0.5 * (1.0 + t)
    return x * half, half + 0.5 * x * (1.0 - t * t) * GELU_K * (1.0 + 3.0 * GELU_C * x2)


def _in_proj(x, g, wt):
    T, Dm = x.shape
    tm = _tile(T, 512)

    def body(x_ref, g_ref, wt_ref, h_ref, zm_ref, zs_ref):
        xf = x_ref[...]
        r = lax.rsqrt(jnp.mean(xf * xf, axis=-1, keepdims=True) + EPS)
        h = (xf * r * g_ref[...]).astype(BF16)
        h_ref[...] = h
        for i, (r0, r1) in enumerate(IN_ROWS_MAIN):
            zm_ref[:, i * D_MODEL:(i + 1) * D_MODEL] = _dot_nt(h, wt_ref[r0:r1, :]).astype(KEPT)
        zs_ref[...] = _dot_nt(h, wt_ref[IN_ROWS_ZS[0]:IN_ROWS_ZS[1], :])

    row = lambda n: pl.BlockSpec((tm, n), lambda i: (i, 0))
    return pl.pallas_call(
        body, grid=(T // tm,),
        in_specs=[row(Dm), pl.BlockSpec((1, Dm), lambda i: (0, 0)), pl.BlockSpec(wt.shape, lambda i: (0, 0))],
        out_specs=[row(Dm), row(4 * D_MODEL), row(ZS_W)],
        out_shape=[jax.ShapeDtypeStruct((T, Dm), BF16), jax.ShapeDtypeStruct((T, 4 * D_MODEL), KEPT),
                   jax.ShapeDtypeStruct((T, ZS_W), F32)],
        name="in_proj", compiler_params=_params("parallel"))(x, g, wt)


def _proj_bwd(acts, wt, terms, x, g, dres, name, w2=None, dep=None, rows=256):
    T, Dm = x.shape
    tm = _tile(T, rows)
    n_a = len(acts)

    def body(*refs):
        ins, outs = refs[:n_a + 4 + (w2 is not None) + (dep is not None)], refs[-2 - (w2 is not None):]
        wt_ref, x_ref, g_ref, dres_ref = ins[n_a:n_a + 4]
        dx_ref, dg_ref = outs[0], outs[1]

        @pl.when(pl.program_id(0) == 0)
        def _():
            dg_ref[...] = jnp.zeros_like(dg_ref)

        dy = None
        for i, (c0, c1), (r0, r1) in terms:
            t = _dot(ins[i][:, c0:c1], wt_ref[r0:r1, :])
            dy = t if dy is None else dy + t
        xf = x_ref[...]
        r = lax.rsqrt(jnp.mean(xf * xf, axis=-1, keepdims=True) + EPS)
        xh = xf * r
        dg_ref[...] += jnp.sum(dy * xh, axis=0, keepdims=True)
        dxh = dy * g_ref[...]
        dx = dres_ref[...] + r * (dxh - xh * jnp.mean(dxh * xh, axis=-1, keepdims=True))
        dx_ref[...] = dx
        if w2 is not None:
            outs[2][...] = _dot_nt(dx.astype(BF16), ins[n_a + 4][...]).astype(KEPT)

    row = pl.BlockSpec((tm, Dm), lambda i: (i, 0))
    vec = pl.BlockSpec((1, Dm), lambda i: (0, 0))
    in_specs = [pl.BlockSpec((tm, a.shape[1]), lambda i: (i, 0)) for a in acts]
    in_specs += [pl.BlockSpec(wt.shape, lambda i: (0, 0)), row, vec, row]
    args = [*acts, wt, x, g, dres]
    out_specs = [row, vec]
    out_shape = [jax.ShapeDtypeStruct((T, Dm), F32), jax.ShapeDtypeStruct((1, Dm), F32)]
    if w2 is not None:
        in_specs.append(pl.BlockSpec(w2.shape, lambda i: (0, 0)))
        args.append(w2)
        out_specs.append(pl.BlockSpec((tm, w2.shape[0]), lambda i: (i, 0)))
        out_shape.append(jax.ShapeDtypeStruct((T, w2.shape[0]), KEPT))
    if dep is not None:
        in_specs.append(ANY)
        args.append(dep)
    return pl.pallas_call(
        body, grid=(T // tm,), in_specs=in_specs, out_specs=out_specs, out_shape=out_shape,
        name=name, compiler_params=_params("arbitrary"))(*args)


def _mm_tn(a, b, name, dep=None, rows=None, row0=0, into=None, gap=None):
    T, M = a.shape
    N = b.shape[1]
    tm, tn, tt = _wide_tile(M), _wide_tile(N), _tile(T, 2048)
    n_t = T // tt
    off = row0 // tm
    extra = ([dep] if dep is not None else []) + ([into] if into is not None else [])
    if gap is None:
        out_spec = pl.BlockSpec((tm, tn), lambda i, j, t: (i + off, j))
    else:
        unit = 16
        assert row0 == 0 and gap[0] % tm == 0 and tm % unit == 0 and gap[1] % unit == 0
        out_spec = pl.BlockSpec(
            (pl.Element(tm), pl.Element(tn)),
            lambda i, j, t: ((i * (tm // unit) + jnp.where(i * tm >= gap[0], gap[1] // unit, 0)) * unit, j * tn))

    def body(a_ref, b_ref, *refs):
        o_ref, acc_ref = refs[-2:]
        t = pl.program_id(2)

        @pl.when(t == 0)
        def _():
            acc_ref[...] = jnp.zeros_like(acc_ref)

        acc_ref[...] += _dot_tn(a_ref[...].astype(BF16), b_ref[...].astype(BF16))

        @pl.when(t == n_t - 1)
        def _():
            o_ref[...] = acc_ref[...].astype(BF16)

    return pl.pallas_call(
        body, grid=(M // tm, N // tn, n_t),
        in_specs=[pl.BlockSpec((tt, tm), lambda i, j, t: (t, i)),
                  pl.BlockSpec((tt, tn), lambda i, j, t: (t, j))] + [ANY] * len(extra),
        out_specs=out_spec,
        out_shape=jax.ShapeDtypeStruct((rows or M, N), BF16),
        scratch_shapes=[pltpu.VMEM((tm, tn), F32)],
        input_output_aliases={} if into is None else {1 + len(extra): 0},
        name=name, compiler_params=_params("parallel", "parallel", "arbitrary"))(a, b, *extra)


def _layer_norm_fwd(gv, g, b):
    mu = jnp.mean(gv, axis=-1, keepdims=True)
    xc = gv - mu
    rs = lax.rsqrt(jnp.mean(xc * xc, axis=-1, keepdims=True) + EPS)
    xh = xc * rs
    return xh, rs, xh * g + b


def _tri_mask(transposed=False):
    r = lax.broadcasted_iota(jnp.int32, (CHUNK, CHUNK), 0)
    c = lax.broadcasted_iota(jnp.int32, (CHUNK, CHUNK), 1)
    return r <= c if transposed else c <= r


def _mixer_a_fwd(zm, av_g, av_b, w_s, b_col):
    T = zm.shape[0]
    tm = _tile(T, 512)
    n_chunk = tm // CHUNK

    def body(u_ref, v_ref, ga_ref, g_ref, b_ref, w_ref, bc_ref, y_ref, vn_s, mx_s):
        gu = _gelu(u_ref[...].astype(F32))
        _, _, vn = _layer_norm_fwd(_gelu(v_ref[...].astype(F32)), g_ref[...], b_ref[...])
        vn_s[...] = vn.astype(BF16)
        tri = _tri_mask()
        for gi in range(A_GROUPS):
            wm = jnp.where(tri, w_ref[gi], 0.0).astype(BF16)
            cols = slice(gi * CHUNK, (gi + 1) * CHUNK)
            for n in range(n_chunk):
                rows = slice(n * CHUNK, (n + 1) * CHUNK)
                mx_s[rows, cols] = _dot(wm, vn_s[rows, cols]) + bc_ref[gi]
        y_ref[...] = (_sigmoid(ga_ref[...].astype(F32)) * gu * mx_s[...]).astype(KEPT)

    col = lambda c: pl.BlockSpec((tm, D_MODEL), lambda i: (i, c))
    vec = pl.BlockSpec((1, D_MODEL), lambda i: (0, 0))
    return pl.pallas_call(
        body, grid=(T // tm,),
        in_specs=[col(0), col(1), col(2), vec, vec,
                  pl.BlockSpec((A_GROUPS, CHUNK, CHUNK), lambda i: (0, 0, 0)),
                  pl.BlockSpec((A_GROUPS, CHUNK, 1), lambda i: (0, 0, 0))],
        out_specs=pl.BlockSpec((tm, D_MODEL), lambda i: (i, 0)),
        out_shape=jax.ShapeDtypeStruct((T, D_MODEL), KEPT),
        scratch_shapes=[pltpu.VMEM((tm, D_MODEL), BF16), pltpu.VMEM((tm, D_MODEL), F32)],
        name="mixer_a_fwd", compiler_params=_params("parallel"))(zm, zm, zm, av_g, av_b, w_s, b_col)


def _mixer_bwd(zm, o, dm, av_g, av_b, w_s, w_st, b_col, dep):
    T = zm.shape[0]
    tm = _tile(T, 256)
    n_chunk = tm // CHUNK

    def body(u_ref, v_ref, ga_ref, gb_ref, o_ref, dm_ref, g_ref, b_ref, w_ref, wt_ref, bc_ref, dep_ref,
             dz_ref, do_ref, dg_ref, db_ref, dw_ref, dbs_ref, vn_s, mx_s, dmx_s, dvn_s):
        @pl.when(pl.program_id(0) == 0)
        def _():
            dg_ref[...] = jnp.zeros_like(dg_ref)
            db_ref[...] = jnp.zeros_like(db_ref)
            dw_ref[...] = jnp.zeros_like(dw_ref)
            dbs_ref[...] = jnp.zeros_like(dbs_ref)

        dm_v = dm_ref[...].astype(F32)
        gb = gb_ref[...].astype(F32)
        sb = _sigmoid(gb)
        o_v = o_ref[...].astype(F32)
        do_ref[...] = (dm_v * sb).astype(BF16)
        dz_ref[:, 3 * D_MODEL:4 * D_MODEL] = (dm_v * o_v * sb * (1.0 - sb)).astype(BF16)
        u = u_ref[...].astype(F32)
        v = v_ref[...].astype(F32)
        gu, gu_grad = _gelu_and_grad(u)
        gv, gv_grad = _gelu_and_grad(v)
        xh, rs, vn = _layer_norm_fwd(gv, g_ref[...], b_ref[...])
        vn_s[...] = vn.astype(BF16)
        tri = _tri_mask()
        for gi in range(A_GROUPS):
            wm = jnp.where(tri, w_ref[gi], 0.0).astype(BF16)
            cols = slice(gi * CHUNK, (gi + 1) * CHUNK)
            for n in range(n_chunk):
                rows = slice(n * CHUNK, (n + 1) * CHUNK)
                mx_s[rows, cols] = _dot(wm, vn_s[rows, cols]) + bc_ref[gi]
        mixed = mx_s[...]
        sa = _sigmoid(ga_ref[...].astype(F32))
        dya = dm_v * sa
        dz_ref[:, 2 * D_MODEL:3 * D_MODEL] = (dm_v * gu * mixed * sa * (1.0 - sa)).astype(BF16)
        dz_ref[:, 0:D_MODEL] = (dya * mixed * gu_grad).astype(BF16)
        dmx = dya * gu
        dmx_s[...] = dmx.astype(BF16)
        tri_t = _tri_mask(transposed=True)
        for gi in range(A_GROUPS):
            wmt = jnp.where(tri_t, wt_ref[gi], 0.0).astype(BF16)
            cols = slice(gi * CHUNK, (gi + 1) * CHUNK)
            dw_acc = jnp.zeros((CHUNK, CHUNK), F32)
            dmx_sum = jnp.zeros((CHUNK, CHUNK), F32)
            for n in range(n_chunk):
                rows = slice(n * CHUNK, (n + 1) * CHUNK)
                blk = dmx_s[rows, cols]
                dvn_s[rows, cols] = _dot(wmt, blk)
                dw_acc = dw_acc + _dot_nt(blk, vn_s[rows, cols])
                dmx_sum = dmx_sum + dmx[rows, cols]
            dw_ref[gi] += jnp.where(tri, dw_acc, 0.0)
            dbs_ref[gi] += jnp.sum(dmx_sum, axis=-1, keepdims=True)
        dvn = dvn_s[...]
        dg_ref[...] += jnp.sum(dvn * xh, axis=0, keepdims=True)
        db_ref[...] += jnp.sum(dvn, axis=0, keepdims=True)
        dxh = dvn * g_ref[...]
        dgv = rs * (dxh - jnp.mean(dxh, axis=-1, keepdims=True)
                    - xh * jnp.mean(dxh * xh, axis=-1, keepdims=True))
        dz_ref[:, D_MODEL:2 * D_MODEL] = (dgv * gv_grad).astype(BF16)

    col = lambda c: pl.BlockSpec((tm, D_MODEL), lambda i: (i, c))
    row = pl.BlockSpec((tm, D_MODEL), lambda i: (i, 0))
    vec = pl.BlockSpec((1, D_MODEL), lambda i: (0, 0))
    wsp = pl.BlockSpec((A_GROUPS, CHUNK, CHUNK), lambda i: (0, 0, 0))
    bsp = pl.BlockSpec((A_GROUPS, CHUNK, 1), lambda i: (0, 0, 0))
    return pl.pallas_call(
        body, grid=(T // tm,),
        in_specs=[col(0), col(1), col(2), col(3), row, row, vec, vec, wsp, wsp, bsp, ANY],
        out_specs=[pl.BlockSpec((tm, 4 * D_MODEL), lambda i: (i, 0)), row, vec, vec, wsp, bsp],
        out_shape=[jax.ShapeDtypeStruct((T, 4 * D_MODEL), BF16), jax.ShapeDtypeStruct((T, D_MODEL), BF16),
                   jax.ShapeDtypeStruct((1, D_MODEL), F32), jax.ShapeDtypeStruct((1, D_MODEL), F32),
                   jax.ShapeDtypeStruct((A_GROUPS, CHUNK, CHUNK), F32),
                   jax.ShapeDtypeStruct((A_GROUPS, CHUNK, 1), F32)],
        scratch_shapes=[pltpu.VMEM((tm, D_MODEL), BF16), pltpu.VMEM((tm, D_MODEL), F32),
                        pltpu.VMEM((tm, D_MODEL), BF16), pltpu.VMEM((tm, D_MODEL), F32)],
        name="mixer_bwd", compiler_params=_params("arbitrary"))(
            zm, zm, zm, zm, o, dm, av_g, av_b, w_s, w_st, b_col, dep)


def _rope_tables(pos_ref, invf_ref):
    ang = pos_ref[...].astype(F32) * invf_ref[...]
    lane = lax.broadcasted_iota(jnp.int32, ang.shape, 1)
    cos, sin = jnp.cos(ang), jnp.sin(ang)
    c = jnp.where(lane < QK_ROPE, cos, 0.0)
    sa = jnp.where(lane < QK_ROPE // 2, -sin, 0.0)
    sb = jnp.where((lane >= QK_ROPE // 2) & (lane < QK_ROPE), sin, 0.0)
    return c, sa, sb


def _rope(blk, tabs):
    c, sa, sb = tabs
    return blk * c + pltpu.roll(blk, LANES - QK_ROPE // 2, 1) * sa + pltpu.roll(blk, QK_ROPE // 2, 1) * sb


def _rope_t(dout, tabs):
    c, sa, sb = tabs
    return dout * c + pltpu.roll(dout * sa, QK_ROPE // 2, 1) + pltpu.roll(dout * sb, LANES - QK_ROPE // 2, 1)


def _rms_small(x, g):
    r = lax.rsqrt(jnp.mean(x * x, axis=-1, keepdims=True) + EPS)
    xh = x * r
    return xh, r, xh * g


def _mla_prep_fwd(zs, pos, invf, qg, kvg, wuq_p, wukv):
    T = zs.shape[0]
    tm = _tile(T, 512)
    HW = MLA_HEADS * HEAD_PAD

    def body(zs_ref, pos_ref, invf_ref, qg_ref, kvg_ref, wq_ref, wkv_ref, q_ref, k_ref, v_ref):
        tabs = _rope_tables(pos_ref, invf_ref)
        _, _, cqn = _rms_small(zs_ref[:, 0:Q_LORA], qg_ref[...])
        _, _, ckvn = _rms_small(zs_ref[:, Q_LORA:Q_LORA + KV_LORA], kvg_ref[...])
        q = _dot_nt(cqn.astype(BF16), wq_ref[...]) * ATTN_SCALE
        kv = _dot(ckvn.astype(BF16), wkv_ref[...])
        kr = _rope(zs_ref[:, Q_LORA + KV_LORA:ZS_W], tabs).astype(BF16)
        for h in range(MLA_HEADS):
            b0 = h * HEAD_PAD
            q_ref[:, b0:b0 + QK_NOPE] = q[:, b0:b0 + QK_NOPE].astype(BF16)
            q_ref[:, b0 + QK_NOPE:b0 + HEAD_PAD] = _rope(q[:, b0 + QK_NOPE:b0 + HEAD_PAD], tabs).astype(BF16)
            k_ref[:, b0:b0 + QK_NOPE] = kv[:, b0:b0 + QK_NOPE].astype(BF16)
            k_ref[:, b0 + QK_NOPE:b0 + HEAD_PAD] = kr
            v_ref[:, h * V_HEAD:(h + 1) * V_HEAD] = kv[:, b0 + QK_NOPE:b0 + HEAD_PAD].astype(BF16)

    full = lambda a: pl.BlockSpec(a.shape, lambda i: (0,) * a.ndim)
    return pl.pallas_call(
        body, grid=(T // tm,),
        in_specs=[pl.BlockSpec((tm, ZS_W), lambda i: (i, 0)), pl.BlockSpec((tm, 1), lambda i: (i, 0)),
                  full(invf), full(qg), full(kvg), full(wuq_p), full(wukv)],
        out_specs=[pl.BlockSpec((tm, HW), lambda i: (i, 0)), pl.BlockSpec((tm, HW), lambda i: (i, 0)),
                   pl.BlockSpec((tm, D_MODEL), lambda i: (i, 0))],
        out_shape=[jax.ShapeDtypeStruct((T, HW), BF16), jax.ShapeDtypeStruct((T, HW), BF16),
                   jax.ShapeDtypeStruct((T, D_MODEL), BF16)],
        name="mla_prep_fwd", compiler_params=_params("parallel"))(zs, pos, invf, qg, kvg, wuq_p, wukv)


def _mla_prep_bwd(zs, pos, invf, qg, kvg, wuq_p, wukv, dq, dk, dv):
    T = zs.shape[0]
    tm = _tile(T, 512)
    n_t = T // tm
    HW = MLA_HEADS * HEAD_PAD

    def body(zs_ref, pos_ref, invf_ref, qg_ref, kvg_ref, wq_ref, wkv_ref, dq_ref, dk_ref, dv_ref,
             dzs_ref, dwq_ref, dwkv_ref, dqg_ref, dkvg_ref, dqp_ref, dkv_ref, accq_ref, acckv_ref):
        @pl.when(pl.program_id(0) == 0)
        def _():
            dqg_ref[...] = jnp.zeros_like(dqg_ref)
            dkvg_ref[...] = jnp.zeros_like(dkvg_ref)
            accq_ref[...] = jnp.zeros_like(accq_ref)
            acckv_ref[...] = jnp.zeros_like(acckv_ref)

        tabs = _rope_tables(pos_ref, invf_ref)
        cqh, rq, cqn = _rms_small(zs_ref[:, 0:Q_LORA], qg_ref[...])
        ckvh, rkv, ckvn = _rms_small(zs_ref[:, Q_LORA:Q_LORA + KV_LORA], kvg_ref[...])
        dkr = jnp.zeros((tm, LANES), F32)
        for h in range(MLA_HEADS):
            b0 = h * HEAD_PAD
            dqp_ref[:, b0:b0 + QK_NOPE] = dq_ref[:, b0:b0 + QK_NOPE]
            dqp_ref[:, b0 + QK_NOPE:b0 + HEAD_PAD] = _rope_t(
                dq_ref[:, b0 + QK_NOPE:b0 + HEAD_PAD].astype(F32), tabs).astype(BF16)
            dkv_ref[:, b0:b0 + QK_NOPE] = dk_ref[:, b0:b0 + QK_NOPE]
            dkv_ref[:, b0 + QK_NOPE:b0 + HEAD_PAD] = dv_ref[:, h * V_HEAD:(h + 1) * V_HEAD]
            dkr = dkr + dk_ref[:, b0 + QK_NOPE:b0 + HEAD_PAD].astype(F32)
        accq_ref[...] += _dot_tn(dqp_ref[...], cqn.astype(BF16))
        acckv_ref[...] += _dot_tn(ckvn.astype(BF16), dkv_ref[...])

        @pl.when(pl.program_id(0) == n_t - 1)
        def _():
            dwq_ref[...] = accq_ref[...].astype(BF16)
            dwkv_ref[...] = acckv_ref[...].astype(BF16)

        dcqn = _dot(dqp_ref[...], wq_ref[...])
        dckvn = _dot_nt(dkv_ref[...], wkv_ref[...])
        dqg_ref[...] += jnp.sum(dcqn * cqh, axis=0, keepdims=True)
        dkvg_ref[...] += jnp.sum(dckvn * ckvh, axis=0, keepdims=True)
        dxh = dcqn * qg_ref[...]
        dzs_ref[:, 0:Q_LORA] = (rq * (dxh - cqh * jnp.mean(dxh * cqh, axis=-1, keepdims=True))).astype(BF16)
        dxh = dckvn * kvg_ref[...]
        dzs_ref[:, Q_LORA:Q_LORA + KV_LORA] = (
            rkv * (dxh - ckvh * jnp.mean(dxh * ckvh, axis=-1, keepdims=True))).astype(BF16)
        dzs_ref[:, Q_LORA + KV_LORA:ZS_W] = _rope_t(dkr, tabs).astype(BF16)

    full = lambda a: pl.BlockSpec(a.shape, lambda i: (0,) * a.ndim)
    rowb = lambda w: pl.BlockSpec((tm, w), lambda i: (i, 0))
    return pl.pallas_call(
        body, grid=(T // tm,),
        in_specs=[rowb(ZS_W), rowb(1), full(invf), full(qg), full(kvg), full(wuq_p), full(wukv),
                  rowb(HW), rowb(HW), rowb(D_MODEL)],
        out_specs=[rowb(ZS_W), full(wuq_p), full(wukv), full(qg), full(kvg)],
        out_shape=[jax.ShapeDtypeStruct((T, ZS_W), BF16), jax.ShapeDtypeStruct(wuq_p.shape, BF16),
                   jax.ShapeDtypeStruct(wukv.shape, BF16), jax.ShapeDtypeStruct(qg.shape, F32),
                   jax.ShapeDtypeStruct(kvg.shape, F32)],
        scratch_shapes=[pltpu.VMEM((tm, HW), BF16), pltpu.VMEM((tm, HW), BF16),
                        pltpu.VMEM(wuq_p.shape, F32), pltpu.VMEM(wukv.shape, F32)],
        name="mla_prep_bwd", compiler_params=_params("arbitrary"))(
            zs, pos, invf, qg, kvg, wuq_p, wukv, dq, dk, dv)


def _causal(tq, kmax, q0):
    r = lax.broadcasted_iota(jnp.int32, (tq, kmax), 0) + q0
    c = lax.broadcasted_iota(jnp.int32, (tq, kmax), 1)
    return c <= r


def _attn_fwd(q, k, v, batch, seq):
    tq = _tile(seq, ATTN_TILE)
    nq = seq // tq

    def body(q_ref, k_ref, v_ref, o_ref, lse_ref):
        diag = _causal(tq, tq, 0)
        for qi in range(nq):
            rows = slice(qi * tq, (qi + 1) * tq)
            qr = q_ref[rows, :]
            s_d = jnp.where(diag, _dot_nt(qr, k_ref[rows, :]), NEG_BIG)
            m = jnp.max(s_d, axis=-1, keepdims=True)
            if qi > 0:
                before = slice(0, qi * tq)
                s_b = _dot_nt(qr, k_ref[before, :])
                m = jnp.maximum(m, jnp.max(s_b, axis=-1, keepdims=True))
                p_b = jnp.exp(s_b - m)
                l = jnp.sum(p_b, axis=-1, keepdims=True)
                acc = _dot(p_b.astype(BF16), v_ref[before, :])
            p_d = jnp.exp(s_d - m)
            l_d = jnp.sum(p_d, axis=-1, keepdims=True)
            acc_d = _dot(p_d.astype(BF16), v_ref[rows, :])
            l, acc = (l + l_d, acc + acc_d) if qi > 0 else (l_d, acc_d)
            o_ref[rows, :] = (acc / l).astype(KEPT)
            lse_ref[rows, :] = jnp.broadcast_to(m + jnp.log(l), (tq, V_HEAD))

    return pl.pallas_call(
        body, grid=(batch, MLA_HEADS),
        in_specs=[pl.BlockSpec((seq, HEAD_PAD), lambda b, h: (b, h)),
                  pl.BlockSpec((seq, HEAD_PAD), lambda b, h: (b, h)),
                  pl.BlockSpec((seq, V_HEAD), lambda b, h: (b, h))],
        out_specs=[pl.BlockSpec((seq, V_HEAD), lambda b, h: (b, h)),
                   pl.BlockSpec((seq, V_HEAD), lambda b, h: (b, h))],
        out_shape=[jax.ShapeDtypeStruct((batch * seq, D_MODEL), KEPT),
                   jax.ShapeDtypeStruct((batch * seq, D_MODEL), F32)],
        name="attn_fwd", compiler_params=_params("parallel", "parallel"))(q, k, v)


def _attn_bwd(q, k, v, o, do, lse, batch, seq, dep):
    tq = _tile(seq, ATTN_TILE)
    nq = seq // tq

    def body(q_ref, k_ref, v_ref, o_ref, do_ref, lse_ref, dep_ref, dq_ref, dk_ref, dv_ref, dk_acc, dv_acc):
        dk_acc[...] = jnp.zeros_like(dk_acc)
        dv_acc[...] = jnp.zeros_like(dv_acc)
        for qi in range(nq):
            rows = slice(qi * tq, (qi + 1) * tq)
            kmax = (qi + 1) * tq
            qr = q_ref[rows, :]
            dor = do_ref[rows, :]
            kk = k_ref[0:kmax, :]
            s = _dot_nt(qr, kk)
            p = jnp.where(_causal(tq, kmax, qi * tq), jnp.exp(s - lse_ref[rows, 0:1]), 0.0)
            dp = _dot_nt(dor, v_ref[0:kmax, :])
            delta = jnp.sum(dor.astype(F32) * o_ref[rows, :].astype(F32), axis=-1, keepdims=True)
            ds = (p * (dp - delta)).astype(BF16)
            dq_ref[rows, :] = (_dot(ds, kk) * ATTN_SCALE).astype(BF16)
            dk_acc[0:kmax, :] += _dot_tn(ds, qr)
            dv_acc[0:kmax, :] += _dot_tn(p.astype(BF16), dor)
        dk_ref[...] = dk_acc[...].astype(BF16)
        dv_ref[...] = dv_acc[...].astype(BF16)

    qspec = pl.BlockSpec((seq, HEAD_PAD), lambda b, h: (b, h))
    vspec = pl.BlockSpec((seq, V_HEAD), lambda b, h: (b, h))
    T = batch * seq
    return pl.pallas_call(
        body, grid=(batch, MLA_HEADS),
        in_specs=[qspec, qspec, vspec, vspec, vspec, vspec, ANY],
        out_specs=[qspec, qspec, vspec],
        out_shape=[jax.ShapeDtypeStruct((T, MLA_HEADS * HEAD_PAD), BF16),
                   jax.ShapeDtypeStruct((T, MLA_HEADS * HEAD_PAD), BF16),
                   jax.ShapeDtypeStruct((T, D_MODEL), BF16)],
        scratch_shapes=[pltpu.VMEM((seq, HEAD_PAD), F32), pltpu.VMEM((seq, V_HEAD), F32)],
        name="attn_bwd", compiler_params=_params("parallel", "parallel"))(q, k, v, o, do, lse, dep)


def _merge_out(x, yag, zm, o, w_out, ffn_g):
    T = x.shape[0]
    tm = _tile(T, 512)

    def body(x_ref, ya_ref, gb_ref, o_ref, w_ref, g_ref, mg_ref, x1_ref, h2_ref):
        mg = (ya_ref[...].astype(F32) + _sigmoid(gb_ref[...].astype(F32)) * o_ref[...].astype(F32)).astype(BF16)
        mg_ref[...] = mg
        x1 = x_ref[...] + _dot(mg, w_ref[...])
        x1_ref[...] = x1
        r = lax.rsqrt(jnp.mean(x1 * x1, axis=-1, keepdims=True) + EPS)
        h2_ref[...] = (x1 * r * g_ref[...]).astype(BF16)

    row = pl.BlockSpec((tm, D_MODEL), lambda i: (i, 0))
    return pl.pallas_call(
        body, grid=(T // tm,),
        in_specs=[row, row, pl.BlockSpec((tm, D_MODEL), lambda i: (i, 3)), row,
                  pl.BlockSpec((D_MODEL, D_MODEL), lambda i: (0, 0)), pl.BlockSpec((1, D_MODEL), lambda i: (0, 0))],
        out_specs=[row, row, row],
        out_shape=[jax.ShapeDtypeStruct((T, D_MODEL), BF16), jax.ShapeDtypeStruct((T, D_MODEL), F32),
                   jax.ShapeDtypeStruct((T, D_MODEL), BF16)],
        name="merge_out", compiler_params=_params("parallel"))(x, yag, zm, o, w_out, ffn_g)


FF_TILE = 256
FF_BLOCKS = D_FF // FF_TILE
FFB_TILE = 256
UP_ROWS = 512
EDGE = 16


def _shift_up(x, k):
    n = x.shape[0]
    row = lax.broadcasted_iota(jnp.int32, x.shape, 0)
    return jnp.where(row < n - k, pltpu.roll(x, n - k, 0), 0.0)


def _up_act(h2, wt_up, cw, cb, batch, seq):
    def body(h_ref, wug_ref, wuv_ref, wg_ref, wv_ref, bg_ref, bv_ref, ug_ref, uv_ref, g_ref, v_ref, a_ref,
             ug_s, uv_s):
        for s in (ug_s, uv_s):
            s[0:SUBLANES, :] = jnp.zeros((SUBLANES, FF_TILE), F32)

        def conv(s, w_ref, b_ref, r0):
            return (b_ref[...] + w_ref[2:3, :] * s[r0:r0 + UP_ROWS, :]
                    + w_ref[1:2, :] * s[r0 - 1:r0 - 1 + UP_ROWS, :]
                    + w_ref[0:1, :] * s[r0 - 2:r0 - 2 + UP_ROWS, :])

        for c in range(seq // UP_ROWS):
            rows = slice(c * UP_ROWS, (c + 1) * UP_ROWS)
            r0 = SUBLANES + c * UP_ROWS
            h = h_ref[rows, :]
            for w_ref, u_ref, s in ((wug_ref, ug_ref, ug_s), (wuv_ref, uv_ref, uv_s)):
                u = _dot_nt(h, w_ref[...])
                u_ref[rows, :] = u.astype(KEPT)
                s[r0:r0 + UP_ROWS, :] = u
            gate, val = conv(ug_s, wg_ref, bg_ref, r0), conv(uv_s, wv_ref, bv_ref, r0)
            g_ref[rows, :] = gate.astype(KEPT)
            v_ref[rows, :] = val.astype(KEPT)
            a_ref[rows, :] = (gate * _sigmoid(gate) * val).astype(BF16)

    blk = pl.BlockSpec((seq, FF_TILE), lambda b, j: (b, j))
    wup = lambda off: pl.BlockSpec((FF_TILE, D_MODEL), lambda b, j: (j + off, 0))
    wsp = lambda off: pl.BlockSpec((3, FF_TILE), lambda b, j: (0, j + off))
    bsp = lambda off: pl.BlockSpec((1, FF_TILE), lambda b, j: (0, j + off))
    T = batch * seq
    kept = jax.ShapeDtypeStruct((T, D_FF), KEPT)
    return pl.pallas_call(
        body, grid=(batch, FF_BLOCKS),
        in_specs=[pl.BlockSpec((seq, D_MODEL), lambda b, j: (b, 0)), wup(0), wup(FF_BLOCKS),
                  wsp(0), wsp(FF_BLOCKS), bsp(0), bsp(FF_BLOCKS)],
        out_specs=[blk] * 5,
        out_shape=[kept, kept, kept, kept, jax.ShapeDtypeStruct((T, D_FF), BF16)],
        scratch_shapes=[pltpu.VMEM((SUBLANES + seq, FF_TILE), F32)] * 2,
        name="up_act", compiler_params=_params("parallel", "arbitrary"))(h2, wt_up, wt_up, cw, cw, cb, cb)


def _ffn_act_bwd(upg, upv, gate, val, cw, dx2b, w_down, batch, seq):
    def half(du, x, w_ref, dx_ref, dw_ref):
        j = pl.program_id(1)
        n = du.shape[0]
        up1, up2 = pltpu.roll(du, n - 1, 0), pltpu.roll(du, n - 2, 0)
        dx_ref[...] = (w_ref[2:3, :] * du + w_ref[1:2, :] * up1 + w_ref[0:1, :] * up2).astype(BF16)
        tail = du[n - EDGE:n]
        dx_ref[n - EDGE:n, :] = (w_ref[2:3, :] * tail + w_ref[1:2, :] * _shift_up(tail, 1)
                                 + w_ref[0:1, :] * _shift_up(tail, 2)).astype(BF16)
        row = lax.broadcasted_iota(jnp.int32, (EDGE, du.shape[1]), 0)
        head, x_tail = du[0:EDGE], x[n - EDGE:n]
        wrap1 = jnp.sum(jnp.where(row >= EDGE - 1, pltpu.roll(head, EDGE - 1, 0), 0.0) * x_tail, axis=0, keepdims=True)
        wrap2 = jnp.sum(jnp.where(row >= EDGE - 2, pltpu.roll(head, EDGE - 2, 0), 0.0) * x_tail, axis=0, keepdims=True)
        dw_ref[j, 2:3, :] += jnp.sum(du * x, axis=0, keepdims=True)
        dw_ref[j, 1:2, :] += jnp.sum(up1 * x, axis=0, keepdims=True) - wrap1
        dw_ref[j, 0:1, :] += jnp.sum(up2 * x, axis=0, keepdims=True) - wrap2
        dw_ref[j, 3:4, :] += jnp.sum(du, axis=0, keepdims=True)

    def body(ug_ref, uv_ref, g_ref, v_ref, wg_ref, wv_ref, dx_ref, wd_ref, dg_ref, dv_ref, dwg_ref, dwv_ref):
        @pl.when((pl.program_id(0) == 0) & (pl.program_id(1) == 0))
        def _():
            dwg_ref[...] = jnp.zeros_like(dwg_ref)
            dwv_ref[...] = jnp.zeros_like(dwv_ref)

        gate, val = g_ref[...].astype(F32), v_ref[...].astype(F32)
        sg = _sigmoid(gate)
        dav = _dot_nt(dx_ref[...], wd_ref[...])
        half(dav * val * sg * (1.0 + gate * (1.0 - sg)), ug_ref[...].astype(F32), wg_ref, dg_ref, dwg_ref)
        half(dav * gate * sg, uv_ref[...].astype(F32), wv_ref, dv_ref, dwv_ref)

    nb = D_FF // FFB_TILE
    blk = pl.BlockSpec((seq, FFB_TILE), lambda b, j: (b, j))
    wsp = lambda off: pl.BlockSpec((3, FFB_TILE), lambda b, j: (0, j + off))
    acc = pl.BlockSpec((nb, 4, FFB_TILE), lambda b, j: (0, 0, 0))
    T = batch * seq
    dupg, dupv, dwg, dwv = pl.pallas_call(
        body, grid=(batch, nb),
        in_specs=[blk, blk, blk, blk, wsp(0), wsp(nb),
                  pl.BlockSpec((seq, D_MODEL), lambda b, j: (b, 0)),
                  pl.BlockSpec((FFB_TILE, D_MODEL), lambda b, j: (j, 0))],
        out_specs=[blk, blk, acc, acc],
        out_shape=[jax.ShapeDtypeStruct((T, D_FF), BF16), jax.ShapeDtypeStruct((T, D_FF), BF16),
                   jax.ShapeDtypeStruct((nb, 4, FFB_TILE), F32), jax.ShapeDtypeStruct((nb, 4, FFB_TILE), F32)],
        name="ffn_act_bwd", compiler_params=_params("arbitrary", "arbitrary"))(
            upg, upv, gate, val, cw, cw, dx2b, w_down)
    dwg, dwv = (jnp.transpose(a, (1, 0, 2)).reshape(4, D_FF) for a in (dwg, dwv))
    return dupg, dupv, dwg[:3], dwv[:3], dwg[3:], dwv[3:]


def _down_loss(a, w_down, x1, target, gfin):
    T = x1.shape[0]
    tm = _tile(T, 512)

    def body(a_ref, w_ref, x1_ref, t_ref, g_ref, dx_ref, dxb_ref, loss_ref, dg_ref):
        @pl.when(pl.program_id(0) == 0)
        def _():
            loss_ref[...] = jnp.zeros_like(loss_ref)
            dg_ref[...] = jnp.zeros_like(dg_ref)

        x2 = x1_ref[...] + _dot(a_ref[...], w_ref[...])
        r = lax.rsqrt(jnp.mean(x2 * x2, axis=-1, keepdims=True) + EPS)
        xh = x2 * r
        g = g_ref[...]
        diff = xh * g - t_ref[...]
        loss_ref[...] += 0.5 * jnp.sum(jnp.mean(diff * diff, axis=-1, keepdims=True))
        dy = diff * (1.0 / D_MODEL)
        dg_ref[...] += jnp.sum(dy * xh, axis=0, keepdims=True)
        dxh = dy * g
        dx = r * (dxh - xh * jnp.mean(dxh * xh, axis=-1, keepdims=True))
        dx_ref[...] = dx
        dxb_ref[...] = dx.astype(BF16)

    row = pl.BlockSpec((tm, D_MODEL), lambda i: (i, 0))
    vec = pl.BlockSpec((1, D_MODEL), lambda i: (0, 0))
    return pl.pallas_call(
        body, grid=(T // tm,),
        in_specs=[pl.BlockSpec((tm, D_FF), lambda i: (i, 0)),
                  pl.BlockSpec((D_FF, D_MODEL), lambda i: (0, 0)), row, row, vec],
        out_specs=[row, row, pl.BlockSpec((8, LANES), lambda i: (0, 0)), vec],
        out_shape=[jax.ShapeDtypeStruct((T, D_MODEL), F32), jax.ShapeDtypeStruct((T, D_MODEL), BF16),
                   jax.ShapeDtypeStruct((8, LANES), F32), jax.ShapeDtypeStruct((1, D_MODEL), F32)],
        name="down_loss", compiler_params=_params("arbitrary"))(a, w_down, x1, target, gfin)


def _local_step(x, positions, target, mix_norm, av_g, av_b, w_s, b_s, q_norm, kv_norm, ffn_norm, conv_b,
                final_norm, comm):
    batch, seq, _ = x.shape
    T = batch * seq
    x = x.reshape(T, D_MODEL)
    target = target.reshape(T, D_MODEL)
    pos = positions.reshape(T, 1)
    half = jnp.arange(0, QK_ROPE, 2, dtype=F32) / QK_ROPE
    inv_freq = 1.0 / (ROPE_THETA ** half)
    invf = jnp.concatenate([inv_freq, inv_freq, jnp.zeros((LANES - QK_ROPE,), F32)]).reshape(1, LANES)
    w_st = jnp.swapaxes(w_s, 1, 2)
    b_col = b_s.reshape(A_GROUPS, CHUNK, 1)

    wt_in = comm.in_weights()
    h, zm, zs = _in_proj(x, mix_norm, wt_in)
    yag = _mixer_a_fwd(zm, av_g, av_b, w_s, b_col)
    wuq_p, wukv, w_out = comm.mla_weights(after=yag)
    q, k, v = _mla_prep_fwd(zs, pos, invf, q_norm, kv_norm, wuq_p, wukv)
    o, lse = _attn_fwd(q, k, v, batch, seq)
    merged, x1, h2 = _merge_out(x, yag, zm, o, w_out, ffn_norm)
    wt_up, conv_w, w_down = comm.ffn_weights(after=merged)
    upg, upv, gate, val, act = _up_act(h2, wt_up, conv_w, conv_b, batch, seq)
    dx2, dx2b, loss_acc, d_final = _down_loss(act, w_down, x1, target, final_norm)

    d_wdown = _mm_tn(act, dx2b, "dw_down")
    dupg, dupv, dcwg, dcwv, dcbg, dcbv = _ffn_act_bwd(upg, upv, gate, val, conv_w, dx2b, w_down, batch, seq)
    d_wt_up = _mm_tn(dupv, h2, "dw_up_val", rows=2 * D_FF, row0=D_FF,
                     into=_mm_tn(dupg, h2, "dw_up_gate", rows=2 * D_FF))
    dx1, d_ffn_norm, dmerged = _proj_bwd(
        [dupg, dupv], wt_up, [(0, (0, D_FF), (0, D_FF)), (1, (0, D_FF), (D_FF, 2 * D_FF))],
        x1, ffn_norm, dx2, "up_proj_bwd", w2=w_out)
    d_wout = _mm_tn(merged, dx1, "dw_out")
    token = comm.send_ffn_grads(d_wdown, d_wt_up, jnp.concatenate([dcwg, dcwv], axis=1), d_wout)
    dzm, do, d_avg, d_avb, d_ws, d_bs = _mixer_bwd(zm, o, dmerged, av_g, av_b, w_s, w_st, b_col, token)
    d_wt_in = _mm_tn(dzm, h, "dw_in_main", rows=IN_DIM, gap=(SPLIT_V, ZS_ROWS),
                     into=jnp.zeros((IN_DIM, D_MODEL), BF16))
    token = comm.send_early_grads(d_wt_in, [
        d_avg, d_avb, _small_2d(d_ws).astype(BF16), d_bs.reshape(A_GROUPS, CHUNK), d_ffn_norm,
        jnp.concatenate([dcbg, dcbv], axis=1), d_final])
    dq, dk, dv = _attn_bwd(q, k, v, o, do, lse, batch, seq, token)
    dzs, d_wuq_p, d_wukv, d_qn, d_kvn = _mla_prep_bwd(zs, pos, invf, q_norm, kv_norm, wuq_p, wukv, dq, dk, dv)
    d_wt_zs = _mm_tn(dzs, h, "dw_in_small")
    token = comm.send_mla_grads(d_wuq_p, d_wukv, d_wt_zs)
    terms = [(0, (i * D_MODEL, (i + 1) * D_MODEL), rows) for i, rows in enumerate(IN_ROWS_MAIN)]
    terms.append((1, (0, ZS_W), IN_ROWS_ZS))
    dx, d_mix_norm = _proj_bwd([dzm, dzs], wt_in, terms, x, mix_norm, dx1, "in_proj_bwd", dep=token, rows=512)
    token = comm.send_late_grads([d_qn, d_kvn, d_mix_norm, loss_acc])
    return dx.reshape(batch, seq, D_MODEL), token


MESH_ID = pl.DeviceIdType.MESH
EFFECT = pltpu.SideEffectType.DATAFLOW_SIDE_EFFECTING


def _mesh_pos():
    return lax.axis_index("x"), lax.axis_index("y"), lax.axis_index("c")


def _peer(pos, d):
    x, y, c = pos
    px = 1 - x if d & 4 else x
    py = 1 - y if d & 2 else y
    pc = 1 - c if d & 1 else c
    return (px, py, pc), 4 * px + 2 * py + pc


def _copy(src_ref, land_ref, send_sems, recv_sems, a, d, pos, exchange, landing_here):
    peer, pid = _peer(pos, d)
    me = 4 * pos[0] + 2 * pos[1] + pos[2]
    if exchange:
        src, dst = src_ref.at[pid], land_ref.at[d]
    else:
        src, dst = src_ref, land_ref.at[pid if landing_here else me]
    return pltpu.make_async_remote_copy(
        src_ref=src, dst_ref=dst, send_sem=send_sems.at[a * (N_DEV - 1) + d - 1],
        recv_sem=recv_sems.at[a * (N_DEV - 1) + d - 1],
        device_id=peer, device_id_type=MESH_ID)


def _start_copies(groups, modes, name, dep=None):
    sizes = [len(g) for g in groups]
    srcs = [s for g in groups for s in g]
    lands = [lax.empty(s.shape if modes[gi] else (N_DEV,) + s.shape, s.dtype)
             for gi, g in enumerate(groups) for s in g]
    n, ng = len(srcs), len(groups)
    n_in = 2 * n + (dep is not None)

    def body(*refs):
        src_refs, land_refs = refs[:n], refs[n:2 * n]
        sems = refs[n_in:n_in + 3 * ng]
        token = refs[-1]
        pos = _mesh_pos()
        k = 0
        for gi, size in enumerate(sizes):
            for a in range(size):
                _own_copy(src_refs[k], land_refs[k], sems[3 * gi + 2], a, pos, modes[gi]).start()
                for d in range(1, N_DEV):
                    _copy(src_refs[k], land_refs[k], sems[3 * gi], sems[3 * gi + 1], a, d, pos, modes[gi],
                          landing_here=False).start()
                k += 1
        token[...] = jnp.zeros_like(token)

    sem_shapes = []
    for size in sizes:
        remote = pltpu.SemaphoreType.DMA((size * (N_DEV - 1),))
        sem_shapes += [remote, remote, pltpu.SemaphoreType.DMA((size,))]
    out = pl.pallas_call(
        body, name=name,
        out_shape=(*sem_shapes, *[pltpu.HBM(a.shape, a.dtype) for a in srcs + lands],
                   jax.ShapeDtypeStruct((8, LANES), F32)),
        in_specs=[HBM] * (2 * n) + [ANY] * (dep is not None),
        out_specs=(*[SEM] * (3 * ng), *[HBM] * (2 * n), pl.BlockSpec(memory_space=pltpu.VMEM)),
        input_output_aliases={i: 3 * ng + i for i in range(2 * n)},
        compiler_params=pltpu.CompilerParams(has_side_effects=EFFECT),
    )(*[pltpu.with_memory_space_constraint(a, pltpu.HBM) for a in srcs + lands], *([dep] if dep is not None else []))
    thru = out[3 * ng:3 * ng + 2 * n]
    handles, k = [], 0
    for gi, size in enumerate(sizes):
        handles.append((out[3 * gi:3 * gi + 3], thru[k:k + size], thru[n + k:n + k + size]))
        k += size
    return handles, out[-1]


def _own_copy(src_ref, land_ref, local_sems, a, pos, exchange):
    me = 4 * pos[0] + 2 * pos[1] + pos[2]
    src, dst = (src_ref.at[me], land_ref.at[0]) if exchange else (src_ref, land_ref.at[me])
    return pltpu.make_async_copy(src, dst, local_sems.at[a])


def _wait_copies(handle, exchange, after, name):
    sems, srcs, lands = handle
    n = len(srcs)

    def body(*refs):
        src_refs, land_refs = refs[:n], refs[n:2 * n]
        send, recv, local = refs[2 * n:2 * n + 3]
        pos = _mesh_pos()
        for a in range(n):
            _own_copy(src_refs[a], land_refs[a], local, a, pos, exchange).wait()
            for d in range(1, N_DEV):
                cp = _copy(src_refs[a], land_refs[a], send, recv, a, d, pos, exchange, landing_here=True)
                cp.wait_send()
                cp.wait_recv()

    out = pl.pallas_call(
        body, name=name,
        out_shape=tuple(pltpu.HBM(a.shape, a.dtype) for a in (*srcs, *lands)),
        in_specs=[HBM] * (2 * n) + [SEM, SEM, SEM, ANY], out_specs=[HBM] * (2 * n),
        input_output_aliases={i: i for i in range(2 * n)},
        compiler_params=pltpu.CompilerParams(has_side_effects=EFFECT),
    )(*srcs, *lands, *sems, after)
    return out[n:]


def _gather_now(a, name):
    def body(x_ref, out_ref, send_sems, recv_sems, local_sem):
        x, y, c = _mesh_pos()
        me, sibling = (x, y, c), (x, y, 1 - c)
        chips = [(1 - x, y), (x, 1 - y), (1 - x, 1 - y)]

        def slot(p):
            return out_ref.at[4 * p[0] + 2 * p[1] + p[2]]

        def copy(k, block, to, src=None):
            return pltpu.make_async_remote_copy(
                src_ref=slot(block) if src is None else src, dst_ref=slot(block), send_sem=send_sems.at[k],
                recv_sem=recv_sems.at[k], device_id=to, device_id_type=MESH_ID)

        mine = pltpu.make_async_copy(x_ref, slot(me), local_sem)
        mine.start()
        first = [copy(0, me, sibling, src=x_ref)]
        first += [copy(1 + j, me, (*chip, c), src=x_ref) for j, chip in enumerate(chips)]
        for cp in first:
            cp.start()
        passed = [copy(4 + j, (*chip, c), sibling) for j, chip in enumerate(chips)]
        for j, chip in enumerate(chips):
            copy(1 + j, (*chip, c), me).wait_recv()
            passed[j].start()
        copy(0, sibling, me).wait_recv()
        for j, chip in enumerate(chips):
            copy(4 + j, (*chip, 1 - c), me).wait_recv()
        for cp in first + passed:
            cp.wait_send()
        mine.wait()

    return pl.pallas_call(
        body, in_specs=[ANY], out_specs=ANY,
        out_shape=jax.ShapeDtypeStruct((N_DEV,) + a.shape, a.dtype),
        scratch_shapes=[pltpu.SemaphoreType.DMA((N_DEV - 1,)), pltpu.SemaphoreType.DMA((N_DEV - 1,)),
                        pltpu.SemaphoreType.DMA],
        name=name, compiler_params=pltpu.CompilerParams(has_side_effects=True))(a)


def _sum_parts(p_ref):
    g = p_ref[0].astype(F32)
    for k in range(1, N_DEV):
        g = g + p_ref[k].astype(F32)
    return g


def _adamw_update(p_ref, w_ref, m_ref, v_ref, g_ref, d_ref, nm_ref, nv_ref, patch=None):
    c1 = 1.0 - ADAM_B1 ** ADAM_STEP
    c2 = 1.0 - ADAM_B2 ** ADAM_STEP
    g = _sum_parts(p_ref)
    if patch is not None:
        g = patch(g)
    nm = ADAM_B1 * m_ref[...] + (1.0 - ADAM_B1) * g
    nv = ADAM_B2 * v_ref[...] + (1.0 - ADAM_B2) * (g * g)
    g_ref[...] = g
    nm_ref[...] = nm
    nv_ref[...] = nv
    d_ref[...] = -ADAM_LR * ((nm / c1) / (jnp.sqrt(nv / c2) + ADAM_EPS) + ADAM_WD * w_ref[...])


def _adamw_many(parts, ws, ms, vs, sums, name):
    n, ns = len(ws), len(sums)

    def body(*refs):
        ins, outs = refs[:4 * n + ns], refs[4 * n + ns:]
        for i in range(n):
            _adamw_update(ins[i], ins[n + i], ins[2 * n + i], ins[3 * n + i],
                          outs[i], outs[n + i], outs[2 * n + i], outs[3 * n + i])
        for i in range(ns):
            outs[4 * n + i][...] = _sum_parts(ins[4 * n + i])

    full = lambda a: pl.BlockSpec(a.shape, lambda: (0,) * a.ndim)
    args = [*parts, *ws, *ms, *vs, *sums]
    outs = [jax.ShapeDtypeStruct(w.shape, F32) for _ in range(4) for w in ws]
    outs += [jax.ShapeDtypeStruct(s.shape[1:], F32) for s in sums]
    res = pl.pallas_call(
        body, in_specs=[full(a) for a in args], out_specs=[full(o) for o in outs], out_shape=outs,
        name=name, compiler_params=pltpu.CompilerParams(vmem_limit_bytes=VMEM_LIMIT))(*args)
    return res[:n], res[n:2 * n], res[2 * n:3 * n], res[3 * n:4 * n], res[4 * n:]


def _adamw(parts, w, m, v, name, late=None):
    R, C = w.shape
    tr, tc = R, C
    if N_DEV * R * C * parts.dtype.itemsize > SMALL_BLOCK_BYTES:
        tr = next((t for t in range(min(R, 256) // 16 * 16, 15, -16) if R % t == 0), R)
        if tr == R:
            tc = _tile(C, 256)
    more, places = late if late is not None else (None, ())
    assert late is None or tr == R

    def body(p_ref, w_ref, m_ref, v_ref, *refs):
        def patch(g):
            more_ref, g_s = refs[0], refs[-1]
            x, y, c = _mesh_pos()
            me = 4 * x + 2 * y + c
            rows = _sum_parts(more_ref)
            g_s[...] = g
            for dev, row in places:
                g_s[row:row + rows.shape[0], :] += jnp.where(me == dev, rows, 0.0)
            return g_s[...]

        outs = refs[:4] if late is None else refs[1:5]
        _adamw_update(p_ref, w_ref, m_ref, v_ref, *outs, patch=None if late is None else patch)

    blk = pl.BlockSpec((tr, tc), lambda i, j: (i, j))
    shp = jax.ShapeDtypeStruct((R, C), F32)
    in_specs = [pl.BlockSpec((N_DEV, tr, tc), lambda i, j: (0, i, j)), blk, blk, blk]
    if late is not None:
        in_specs.append(pl.BlockSpec((N_DEV, more.shape[1], tc), lambda i, j: (0, 0, j)))
    return pl.pallas_call(
        body, grid=(R // tr, C // tc), in_specs=in_specs,
        out_specs=[blk, blk, blk, blk], out_shape=[shp, shp, shp, shp],
        scratch_shapes=[] if late is None else [pltpu.VMEM((tr, tc), F32)],
        name=name, compiler_params=_params("parallel", "parallel"))(parts, w, m, v, *([] if late is None else [more]))


SPLIT_V = 2 * D_MODEL
SPLIT_KR = SPLIT_V + Q_LORA + KV_LORA + QK_ROPE
IN_DIM = SPLIT_KR + 2 * D_MODEL
IN_ROWS_MAIN = ((0, D_MODEL), (D_MODEL, SPLIT_V), (SPLIT_KR, SPLIT_KR + D_MODEL), (SPLIT_KR + D_MODEL, IN_DIM))
IN_ROWS_ZS = (SPLIT_V, SPLIT_V + ZS_W)
ZS_ROWS = SPLIT_KR - SPLIT_V
IN_SHARD = IN_DIM // N_DEV
ZS_PIECES = tuple(
    (k, max(IN_SHARD * k, SPLIT_V) - SPLIT_V, max(IN_SHARD * k, SPLIT_V) - IN_SHARD * k)
    for k in range(N_DEV) if max(IN_SHARD * k, SPLIT_V) < min(IN_SHARD * (k + 1), SPLIT_KR))
ZS_PIECE_ROWS = ZS_ROWS // len(ZS_PIECES)
assert all(min(IN_SHARD * (k + 1), SPLIT_KR) - max(IN_SHARD * k, SPLIT_V) == ZS_PIECE_ROWS for k, _, _ in ZS_PIECES)

SMALL_EARLY = ("a_v_norm_g", "a_v_norm_b", "a_spatial_w", "a_spatial_b", "ffn_norm", "conv_b", "final_norm")
SMALL_LATE = ("q_a_norm", "kv_a_norm", "mix_norm")


def _small_2d(a):
    return a.reshape(-1, a.shape[-1])


def _cols_from_shards(g):
    return jnp.transpose(g, (1, 0, 2)).reshape(g.shape[1], N_DEV * g.shape[2])


def _shards_from_cols(a):
    R, W = a.shape
    return jnp.transpose(a.reshape(R, N_DEV, W // N_DEV), (1, 0, 2))


class _Comm:
    GATHER_GROUPS = (("w_uq", "w_ukv", "w_out"), ("w_up", "conv_w", "w_down"))
    FFN_GRADS = ("w_down", "w_up", "conv_w", "w_out")
    TRANSPOSED = ("w_in", "w_up", "w_uq")

    def __init__(self, shards):
        local = {n: a.astype(F32 if n == "conv_w" else BF16) for n, a in shards.items()}
        self.g_in = _gather_now(local["w_in"], "gather_w_in")
        groups = [[local[n] for n in g] for g in self.GATHER_GROUPS]
        (self.h_mla, self.h_ffn), _ = _start_copies(groups, [False] * 2, "gather_start", dep=self.g_in)

    def in_weights(self):
        return self.g_in.reshape(IN_DIM, D_MODEL)

    def mla_weights(self, after):
        g_uq, g_ukv, g_out = _wait_copies(self.h_mla, False, after, "gather_wait_mla")
        wuq_p = jnp.pad(g_uq, ((0, 0), (0, HEAD_PAD - QK_HEAD), (0, 0))).reshape(MLA_HEADS * HEAD_PAD, Q_LORA)
        return wuq_p, _cols_from_shards(g_ukv), g_out.reshape(D_MODEL, D_MODEL)

    def ffn_weights(self, after):
        g_up, g_cw, g_down = _wait_copies(self.h_ffn, False, after, "gather_wait_ffn")
        return g_up.reshape(2 * D_FF, D_MODEL), _cols_from_shards(g_cw), g_down.reshape(D_FF, D_MODEL)

    def send_ffn_grads(self, d_wdown, d_wt_up, d_convw, d_wout):
        group = [d_wdown.reshape(N_DEV, D_FF // N_DEV, D_MODEL), d_wt_up.reshape(N_DEV, 2 * D_FF // N_DEV, D_MODEL),
                 _shards_from_cols(d_convw), d_wout.reshape(N_DEV, D_MODEL // N_DEV, D_MODEL)]
        (self.h_ffn_grads,), token = _start_copies([group], [True], "ffn_grads_start")
        return token

    def send_early_grads(self, d_wt_in, grads):
        blocks = d_wt_in.reshape(N_DEV, IN_SHARD, D_MODEL)
        (self.h_small_early, self.h_in_grads), token = _start_copies(
            [grads, [blocks]], [False, True], "early_grads_start")
        return token

    def send_mla_grads(self, d_wuq_p, d_wukv, d_wt_zs):
        d_uq = d_wuq_p.reshape(MLA_HEADS, HEAD_PAD, Q_LORA)[:, :QK_HEAD, :]
        zs_blocks = jnp.zeros((N_DEV, ZS_PIECE_ROWS, D_MODEL), d_wt_zs.dtype)
        for dev, first, _ in ZS_PIECES:
            zs_blocks = zs_blocks.at[dev].set(d_wt_zs[first:first + ZS_PIECE_ROWS])
        (self.h_mla_grads,), token = _start_copies(
            [[d_uq, _shards_from_cols(d_wukv), zs_blocks]], [True], "mla_grads_start")
        return token

    def send_late_grads(self, small):
        (self.h_late_small,), token = _start_copies([small], [False], "late_grads_start")
        return token


def kernel(x, positions, mix_norm, w_in, a_v_norm_g, a_v_norm_b, a_spatial_w, a_spatial_b, q_a_norm, w_uq, kv_a_norm, w_ukv, w_out, ffn_norm, w_up, conv_w, conv_b, w_down, final_norm, loss_target, m_mix_norm, m_w_in, m_a_v_norm_g, m_a_v_norm_b, m_a_spatial_w, m_a_spatial_b, m_q_a_norm, m_w_uq, m_kv_a_norm, m_w_ukv, m_w_out, m_ffn_norm, m_w_up, m_conv_w, m_conv_b, m_w_down, m_final_norm, v_mix_norm, v_w_in, v_a_v_norm_g, v_a_v_norm_b, v_a_spatial_w, v_a_spatial_b, v_q_a_norm, v_w_uq, v_kv_a_norm, v_w_ukv, v_w_out, v_ffn_norm, v_w_up, v_conv_w, v_conv_b, v_w_down, v_final_norm):
    names = ("mix_norm", "w_in", "a_v_norm_g", "a_v_norm_b", "a_spatial_w", "a_spatial_b", "q_a_norm", "w_uq",
             "kv_a_norm", "w_ukv", "w_out", "ffn_norm", "w_up", "conv_w", "conv_b", "w_down", "final_norm")
    w = dict(zip(names, (mix_norm, w_in, a_v_norm_g, a_v_norm_b, a_spatial_w, a_spatial_b, q_a_norm, w_uq,
                         kv_a_norm, w_ukv, w_out, ffn_norm, w_up, conv_w, conv_b, w_down, final_norm)))
    m = dict(zip(names, (m_mix_norm, m_w_in, m_a_v_norm_g, m_a_v_norm_b, m_a_spatial_w, m_a_spatial_b,
                         m_q_a_norm, m_w_uq, m_kv_a_norm, m_w_ukv, m_w_out, m_ffn_norm, m_w_up, m_conv_w,
                         m_conv_b, m_w_down, m_final_norm)))
    v = dict(zip(names, (v_mix_norm, v_w_in, v_a_v_norm_g, v_a_v_norm_b, v_a_spatial_w, v_a_spatial_b,
                         v_q_a_norm, v_w_uq, v_kv_a_norm, v_w_ukv, v_w_out, v_ffn_norm, v_w_up, v_conv_w,
                         v_conv_b, v_w_down, v_final_norm)))
    shapes = {n: w[n].shape for n in names}
    def view(tree, n):
        a = tree[n].reshape(tree[n].shape[-2:])
        return a.T if n in _Comm.TRANSPOSED else a

    comm = _Comm({n: view(w, n) for n in ("w_in",) + _Comm.GATHER_GROUPS[0] + _Comm.GATHER_GROUPS[1]})

    grad_x, token = _local_step(
        x, positions, loss_target, w["mix_norm"], w["a_v_norm_g"], w["a_v_norm_b"], w["a_spatial_w"][0],
        w["a_spatial_b"][0], w["q_a_norm"], w["kv_a_norm"], w["ffn_norm"], w["conv_b"],
        w["final_norm"].reshape(1, D_MODEL), comm)

    out_g, out_d, out_m, out_v = {}, {}, {}, {}

    def update(n, parts, late=None):
        res = _adamw(parts, view(w, n), view(m, n), view(v, n), "adamw_" + n, late=late)
        out_g[n], out_d[n], out_m[n], out_v[n] = (
            (t.T if n in _Comm.TRANSPOSED else t).reshape(shapes[n]) for t in res)
        return res[1]

    def update_small(names, parts, sums, name):
        res = _adamw_many(parts, *[[_small_2d(t[n]) for n in names] for t in (w, m, v)], sums, name)
        for i, n in enumerate(names):
            out_g[n], out_d[n], out_m[n], out_v[n] = (r[i].reshape(shapes[n]) for r in res[:4])
        return res

    for n, parts in zip(_Comm.FFN_GRADS, _wait_copies(comm.h_ffn_grads, True, token, "ffn_grads_wait")):
        last = update(n, parts)
    early = _wait_copies(comm.h_small_early, False, last, "small_grads_wait")
    last = update_small(SMALL_EARLY, early, [], "adamw_small")[1][0]
    (in_parts,) = _wait_copies(comm.h_in_grads, True, last, "in_grads_wait")
    uq_parts, ukv_parts, zs_parts = _wait_copies(comm.h_mla_grads, True, in_parts, "mla_grads_wait")
    last = update("w_in", in_parts, late=(zs_parts, [(dev, row) for dev, _, row in ZS_PIECES]))
    last = update("w_uq", uq_parts)
    last = update("w_ukv", ukv_parts)
    late = _wait_copies(comm.h_late_small, False, last, "late_small_wait")
    res = update_small(SMALL_LATE, late[:-1], late[-1:], "adamw_late")
    loss = res[4][0][0, 0]

    return (loss, grad_x, *[out_g[n] for n in names], *[out_d[n] for n in names],
            *[out_m[n] for n in names], *[out_v[n] for n in names])
```

```python
import math

import jax
import jax.numpy as jnp
from jax import lax
from jax.experimental import pallas as pl
from jax.experimental.pallas import tpu as pltpu

F32 = jnp.float32
BF16 = jnp.bfloat16
KEPT = jnp.bfloat16

N_DEV = 8
D_MODEL = 1024
EPS = 1e-6
A_GROUPS = 8
CHUNK = 128
MLA_HEADS = 8
QK_NOPE = 128
QK_ROPE = 64
QK_HEAD = QK_NOPE + QK_ROPE
HEAD_PAD = 256
V_HEAD = 128
Q_LORA = 256
KV_LORA = 128
ROPE_THETA = 10000.0
D_FF = 2816
ZS_W = 512
ATTN_SCALE = QK_HEAD ** -0.5
ATTN_TILE = 512
NEG_BIG = -1e30

ADAM_LR = 0.001
ADAM_B1 = 0.9
ADAM_B2 = 0.999
ADAM_EPS = 1e-08
ADAM_WD = 0.01
ADAM_STEP = 10

VMEM_LIMIT = 56 * 1024 * 1024
SMALL_BLOCK_BYTES = 5 * 1024 * 1024
LANES = 128
SUBLANES = 8

GELU_K = math.sqrt(2.0 / math.pi)
GELU_C = 0.044715

ANY = pl.BlockSpec(memory_space=pl.ANY)
HBM = pl.BlockSpec(memory_space=pltpu.HBM)
SEM = pl.BlockSpec(memory_space=pltpu.SEMAPHORE)


def _tile(n, pref):
    for t in (pref, 512, 256, 128, 64, 32, 16, 8):
        if t <= pref and n % t == 0:
            return t
    return n


def _wide_tile(n, cap=1408):
    return next((t for t in range(min(n, cap) // LANES * LANES, 0, -LANES) if n % t == 0), n)


def _params(*sem):
    return pltpu.CompilerParams(dimension_semantics=sem, vmem_limit_bytes=VMEM_LIMIT)


def _dot(a, b):
    return jnp.dot(a, b, preferred_element_type=F32)


def _dot_nt(a, b):
    return lax.dot_general(a, b, (((1,), (1,)), ((), ())), preferred_element_type=F32)


def _dot_tn(a, b):
    return lax.dot_general(a, b, (((0,), (0,)), ((), ())), preferred_element_type=F32)


def _sigmoid(x):
    return 1.0 / (1.0 + jnp.exp(-x))


def _gelu(x):
    t = jnp.tanh(GELU_K * (x + GELU_C * x * x * x))
    return 0.5 * x * (1.0 + t)


def _gelu_and_grad(x):
    x2 = x * x
    t = jnp.tanh(GELU_K * (x + GELU_C * x * x2))
    half = 0.5 * (1.0 + t)
    return x * half, half + 0.5 * x * (1.0 - t * t) * GELU_K * (1.0 + 3.0 * GELU_C * x2)


def _in_proj(x, g, wt):
    T, Dm = x.shape
    tm = _tile(T, 512)

    def body(x_ref, g_ref, wt_ref, h_ref, zm_ref, zs_ref):
        xf = x_ref[...]
        r = lax.rsqrt(jnp.mean(xf * xf, axis=-1, keepdims=True) + EPS)
        h = (xf * r * g_ref[...]).astype(BF16)
        h_ref[...] = h
        for i, (r0, r1) in enumerate(IN_ROWS_MAIN):
            zm_ref[:, i * D_MODEL:(i + 1) * D_MODEL] = _dot_nt(h, wt_ref[r0:r1, :]).astype(KEPT)
        zs_ref[...] = _dot_nt(h, wt_ref[IN_ROWS_ZS[0]:IN_ROWS_ZS[1], :])

    row = lambda n: pl.BlockSpec((tm, n), lambda i: (i, 0))
    return pl.pallas_call(
        body, grid=(T // tm,),
        in_specs=[row(Dm), pl.BlockSpec((1, Dm), lambda i: (0, 0)), pl.BlockSpec(wt.shape, lambda i: (0, 0))],
        out_specs=[row(Dm), row(4 * D_MODEL), row(ZS_W)],
        out_shape=[jax.ShapeDtypeStruct((T, Dm), BF16), jax.ShapeDtypeStruct((T, 4 * D_MODEL), KEPT),
                   jax.ShapeDtypeStruct((T, ZS_W), F32)],
        name="in_proj", compiler_params=_params("parallel"))(x, g, wt)


def _proj_bwd(acts, wt, terms, x, g, dres, name, w2=None, dep=None, rows=256):
    T, Dm = x.shape
    tm = _tile(T, rows)
    n_a = len(acts)

    def body(*refs):
        ins, outs = refs[:n_a + 4 + (w2 is not None) + (dep is not None)], refs[-2 - (w2 is not None):]
        wt_ref, x_ref, g_ref, dres_ref = ins[n_a:n_a + 4]
        dx_ref, dg_ref = outs[0], outs[1]

        @pl.when(pl.program_id(0) == 0)
        def _():
            dg_ref[...] = jnp.zeros_like(dg_ref)

        dy = None
        for i, (c0, c1), (r0, r1) in terms:
            t = _dot(ins[i][:, c0:c1], wt_ref[r0:r1, :])
            dy = t if dy is None else dy + t
        xf = x_ref[...]
        r = lax.rsqrt(jnp.mean(xf * xf, axis=-1, keepdims=True) + EPS)
        xh = xf * r
        dg_ref[...] += jnp.sum(dy * xh, axis=0, keepdims=True)
        dxh = dy * g_ref[...]
        dx = dres_ref[...] + r * (dxh - xh * jnp.mean(dxh * xh, axis=-1, keepdims=True))
        dx_ref[...] = dx
        if w2 is not None:
            outs[2][...] = _dot_nt(dx.astype(BF16), ins[n_a + 4][...]).astype(KEPT)

    row = pl.BlockSpec((tm, Dm), lambda i: (i, 0))
    vec = pl.BlockSpec((1, Dm), lambda i: (0, 0))
    in_specs = [pl.BlockSpec((tm, a.shape[1]), lambda i: (i, 0)) for a in acts]
    in_specs += [pl.BlockSpec(wt.shape, lambda i: (0, 0)), row, vec, row]
    args = [*acts, wt, x, g, dres]
    out_specs = [row, vec]
    out_shape = [jax.ShapeDtypeStruct((T, Dm), F32), jax.ShapeDtypeStruct((1, Dm), F32)]
    if w2 is not None:
        in_specs.append(pl.BlockSpec(w2.shape, lambda i: (0, 0)))
        args.append(w2)
        out_specs.append(pl.BlockSpec((tm, w2.shape[0]), lambda i: (i, 0)))
        out_shape.append(jax.ShapeDtypeStruct((T, w2.shape[0]), KEPT))
    if dep is not None:
        in_specs.append(ANY)
        args.append(dep)
    return pl.pallas_call(
        body, grid=(T // tm,), in_specs=in_specs, out_specs=out_specs, out_shape=out_shape,
        name=name, compiler_params=_params("arbitrary"))(*args)


def _mm_tn(a, b, name, dep=None, rows=None, row0=0, into=None, gap=None):
    T, M = a.shape
    N = b.shape[1]
    tm, tn, tt = _wide_tile(M), _wide_tile(N), _tile(T, 2048)
    n_t = T // tt
    off = row0 // tm
    extra = ([dep] if dep is not None else []) + ([into] if into is not None else [])
    if gap is None:
        out_spec = pl.BlockSpec((tm, tn), lambda i, j, t: (i + off, j))
    else:
        unit = 16
        assert row0 == 0 and gap[0] % tm == 0 and tm % unit == 0 and gap[1] % unit == 0
        out_spec = pl.BlockSpec(
            (pl.Element(tm), pl.Element(tn)),
            lambda i, j, t: ((i * (tm // unit) + jnp.where(i * tm >= gap[0], gap[1] // unit, 0)) * unit, j * tn))

    def body(a_ref, b_ref, *refs):
        o_ref, acc_ref = refs[-2:]
        t = pl.program_id(2)

        @pl.when(t == 0)
        def _():
            acc_ref[...] = jnp.zeros_like(acc_ref)

        acc_ref[...] += _dot_tn(a_ref[...].astype(BF16), b_ref[...].astype(BF16))

        @pl.when(t == n_t - 1)
        def _():
            o_ref[...] = acc_ref[...].astype(BF16)

    return pl.pallas_call(
        body, grid=(M // tm, N // tn, n_t),
        in_specs=[pl.BlockSpec((tt, tm), lambda i, j, t: (t, i)),
                  pl.BlockSpec((tt, tn), lambda i, j, t: (t, j))] + [ANY] * len(extra),
        out_specs=out_spec,
        out_shape=jax.ShapeDtypeStruct((rows or M, N), BF16),
        scratch_shapes=[pltpu.VMEM((tm, tn), F32)],
        input_output_aliases={} if into is None else {1 + len(extra): 0},
        name=name, compiler_params=_params("parallel", "parallel", "arbitrary"))(a, b, *extra)


def _layer_norm_fwd(gv, g, b):
    mu = jnp.mean(gv, axis=-1, keepdims=True)
    xc = gv - mu
    rs = lax.rsqrt(jnp.mean(xc * xc, axis=-1, keepdims=True) + EPS)
    xh = xc * rs
    return xh, rs, xh * g + b


def _tri_mask(transposed=False):
    r = lax.broadcasted_iota(jnp.int32, (CHUNK, CHUNK), 0)
    c = lax.broadcasted_iota(jnp.int32, (CHUNK, CHUNK), 1)
    return r <= c if transposed else c <= r


def _mixer_a_fwd(zm, av_g, av_b, w_s, b_col):
    T = zm.shape[0]
    tm = _tile(T, 512)
    n_chunk = tm // CHUNK

    def body(u_ref, v_ref, ga_ref, g_ref, b_ref, w_ref, bc_ref, y_ref, vn_s, mx_s):
        gu = _gelu(u_ref[...].astype(F32))
        _, _, vn = _layer_norm_fwd(_gelu(v_ref[...].astype(F32)), g_ref[...], b_ref[...])
        vn_s[...] = vn.astype(BF16)
        tri = _tri_mask()
        for gi in range(A_GROUPS):
            wm = jnp.where(tri, w_ref[gi], 0.0).astype(BF16)
            cols = slice(gi * CHUNK, (gi + 1) * CHUNK)
            for n in range(n_chunk):
                rows = slice(n * CHUNK, (n + 1) * CHUNK)
                mx_s[rows, cols] = _dot(wm, vn_s[rows, cols]) + bc_ref[gi]
        y_ref[...] = (_sigmoid(ga_ref[...].astype(F32)) * gu * mx_s[...]).astype(KEPT)

    col = lambda c: pl.BlockSpec((tm, D_MODEL), lambda i: (i, c))
    vec = pl.BlockSpec((1, D_MODEL), lambda i: (0, 0))
    return pl.pallas_call(
        body, grid=(T // tm,),
        in_specs=[col(0), col(1), col(2), vec, vec,
                  pl.BlockSpec((A_GROUPS, CHUNK, CHUNK), lambda i: (0, 0, 0)),
                  pl.BlockSpec((A_GROUPS, CHUNK, 1), lambda i: (0, 0, 0))],
        out_specs=pl.BlockSpec((tm, D_MODEL), lambda i: (i, 0)),
        out_shape=jax.ShapeDtypeStruct((T, D_MODEL), KEPT),
        scratch_shapes=[pltpu.VMEM((tm, D_MODEL), BF16), pltpu.VMEM((tm, D_MODEL), F32)],
        name="mixer_a_fwd", compiler_params=_params("parallel"))(zm, zm, zm, av_g, av_b, w_s, b_col)


def _mixer_bwd(zm, o, dm, av_g, av_b, w_s, w_st, b_col, dep):
    T = zm.shape[0]
    tm = _tile(T, 256)
    n_chunk = tm // CHUNK

    def body(u_ref, v_ref, ga_ref, gb_ref, o_ref, dm_ref, g_ref, b_ref, w_ref, wt_ref, bc_ref, dep_ref,
             dz_ref, do_ref, dg_ref, db_ref, dw_ref, dbs_ref, vn_s, mx_s, dmx_s, dvn_s):
        @pl.when(pl.program_id(0) == 0)
        def _():
            dg_ref[...] = jnp.zeros_like(dg_ref)
            db_ref[...] = jnp.zeros_like(db_ref)
            dw_ref[...] = jnp.zeros_like(dw_ref)
            dbs_ref[...] = jnp.zeros_like(dbs_ref)

        dm_v = dm_ref[...].astype(F32)
        gb = gb_ref[...].astype(F32)
        sb = _sigmoid(gb)
        o_v = o_ref[...].astype(F32)
        do_ref[...] = (dm_v * sb).astype(BF16)
        dz_ref[:, 3 * D_MODEL:4 * D_MODEL] = (dm_v * o_v * sb * (1.0 - sb)).astype(BF16)
        u = u_ref[...].astype(F32)
        v = v_ref[...].astype(F32)
        gu, gu_grad = _gelu_and_grad(u)
        gv, gv_grad = _gelu_and_grad(v)
        xh, rs, vn = _layer_norm_fwd(gv, g_ref[...], b_ref[...])
        vn_s[...] = vn.astype(BF16)
        tri = _tri_mask()
        for gi in range(A_GROUPS):
            wm = jnp.where(tri, w_ref[gi], 0.0).astype(BF16)
            cols = slice(gi * CHUNK, (gi + 1) * CHUNK)
            for n in range(n_chunk):
                rows = slice(n * CHUNK, (n + 1) * CHUNK)
                mx_s[rows, cols] = _dot(wm, vn_s[rows, cols]) + bc_ref[gi]
        mixed = mx_s[...]
        sa = _sigmoid(ga_ref[...].astype(F32))
        dya = dm_v * sa
        dz_ref[:, 2 * D_MODEL:3 * D_MODEL] = (dm_v * gu * mixed * sa * (1.0 - sa)).astype(BF16)
        dz_ref[:, 0:D_MODEL] = (dya * mixed * gu_grad).astype(BF16)
        dmx = dya * gu
        dmx_s[...] = dmx.astype(BF16)
        tri_t = _tri_mask(transposed=True)
        for gi in range(A_GROUPS):
            wmt = jnp.where(tri_t, wt_ref[gi], 0.0).astype(BF16)
            cols = slice(gi * CHUNK, (gi + 1) * CHUNK)
            dw_acc = jnp.zeros((CHUNK, CHUNK), F32)
            dmx_sum = jnp.zeros((CHUNK, CHUNK), F32)
            for n in range(n_chunk):
                rows = slice(n * CHUNK, (n + 1) * CHUNK)
                blk = dmx_s[rows, cols]
                dvn_s[rows, cols] = _dot(wmt, blk)
                dw_acc = dw_acc + _dot_nt(blk, vn_s[rows, cols])
                dmx_sum = dmx_sum + dmx[rows, cols]
            dw_ref[gi] += jnp.where(tri, dw_acc, 0.0)
            dbs_ref[gi] += jnp.sum(dmx_sum, axis=-1, keepdims=True)
        dvn = dvn_s[...]
        dg_ref[...] += jnp.sum(dvn * xh, axis=0, keepdims=True)
        db_ref[...] += jnp.sum(dvn, axis=0, keepdims=True)
        dxh = dvn * g_ref[...]
        dgv = rs * (dxh - jnp.mean(dxh, axis=-1, keepdims=True)
                    - xh * jnp.mean(dxh * xh, axis=-1, keepdims=True))
        dz_ref[:, D_MODEL:2 * D_MODEL] = (dgv * gv_grad).astype(BF16)

    col = lambda c: pl.BlockSpec((tm, D_MODEL), lambda i: (i, c))
    row = pl.BlockSpec((tm, D_MODEL), lambda i: (i, 0))
    vec = pl.BlockSpec((1, D_MODEL), lambda i: (0, 0))
    wsp = pl.BlockSpec((A_GROUPS, CHUNK, CHUNK), lambda i: (0, 0, 0))
    bsp = pl.BlockSpec((A_GROUPS, CHUNK, 1), lambda i: (0, 0, 0))
    return pl.pallas_call(
        body, grid=(T // tm,),
        in_specs=[col(0), col(1), col(2), col(3), row, row, vec, vec, wsp, wsp, bsp, ANY],
        out_specs=[pl.BlockSpec((tm, 4 * D_MODEL), lambda i: (i, 0)), row, vec, vec, wsp, bsp],
        out_shape=[jax.ShapeDtypeStruct((T, 4 * D_MODEL), BF16), jax.ShapeDtypeStruct((T, D_MODEL), BF16),
                   jax.ShapeDtypeStruct((1, D_MODEL), F32), jax.ShapeDtypeStruct((1, D_MODEL), F32),
                   jax.ShapeDtypeStruct((A_GROUPS, CHUNK, CHUNK), F32),
                   jax.ShapeDtypeStruct((A_GROUPS, CHUNK, 1), F32)],
        scratch_shapes=[pltpu.VMEM((tm, D_MODEL), BF16), pltpu.VMEM((tm, D_MODEL), F32),
                        pltpu.VMEM((tm, D_MODEL), BF16), pltpu.VMEM((tm, D_MODEL), F32)],
        name="mixer_bwd", compiler_params=_params("arbitrary"))(
            zm, zm, zm, zm, o, dm, av_g, av_b, w_s, w_st, b_col, dep)


def _rope_tables(pos_ref, invf_ref):
    ang = pos_ref[...].astype(F32) * invf_ref[...]
    lane = lax.broadcasted_iota(jnp.int32, ang.shape, 1)
    cos, sin = jnp.cos(ang), jnp.sin(ang)
    c = jnp.where(lane < QK_ROPE, cos, 0.0)
    sa = jnp.where(lane < QK_ROPE // 2, -sin, 0.0)
    sb = jnp.where((lane >= QK_ROPE // 2) & (lane < QK_ROPE), sin, 0.0)
    return c, sa, sb


def _rope(blk, tabs):
    c, sa, sb = tabs
    return blk * c + pltpu.roll(blk, LANES - QK_ROPE // 2, 1) * sa + pltpu.roll(blk, QK_ROPE // 2, 1) * sb


def _rope_t(dout, tabs):
    c, sa, sb = tabs
    return dout * c + pltpu.roll(dout * sa, QK_ROPE // 2, 1) + pltpu.roll(dout * sb, LANES - QK_ROPE // 2, 1)


def _rms_small(x, g):
    r = lax.rsqrt(jnp.mean(x * x, axis=-1, keepdims=True) + EPS)
    xh = x * r
    return xh, r, xh * g


def _mla_prep_fwd(zs, pos, invf, qg, kvg, wuq_p, wukv):
    T = zs.shape[0]
    tm = _tile(T, 512)
    HW = MLA_HEADS * HEAD_PAD

    def body(zs_ref, pos_ref, invf_ref, qg_ref, kvg_ref, wq_ref, wkv_ref, q_ref, k_ref, v_ref):
        tabs = _rope_tables(pos_ref, invf_ref)
        _, _, cqn = _rms_small(zs_ref[:, 0:Q_LORA], qg_ref[...])
        _, _, ckvn = _rms_small(zs_ref[:, Q_LORA:Q_LORA + KV_LORA], kvg_ref[...])
        q = _dot_nt(cqn.astype(BF16), wq_ref[...]) * ATTN_SCALE
        kv = _dot(ckvn.astype(BF16), wkv_ref[...])
        kr = _rope(zs_ref[:, Q_LORA + KV_LORA:ZS_W], tabs).astype(BF16)
        for h in range(MLA_HEADS):
            b0 = h * HEAD_PAD
            q_ref[:, b0:b0 + QK_NOPE] = q[:, b0:b0 + QK_NOPE].astype(BF16)
            q_ref[:, b0 + QK_NOPE:b0 + HEAD_PAD] = _rope(q[:, b0 + QK_NOPE:b0 + HEAD_PAD], tabs).astype(BF16)
            k_ref[:, b0:b0 + QK_NOPE] = kv[:, b0:b0 + QK_NOPE].astype(BF16)
            k_ref[:, b0 + QK_NOPE:b0 + HEAD_PAD] = kr
            v_ref[:, h * V_HEAD:(h + 1) * V_HEAD] = kv[:, b0 + QK_NOPE:b0 + HEAD_PAD].astype(BF16)

    full = lambda a: pl.BlockSpec(a.shape, lambda i: (0,) * a.ndim)
    return pl.pallas_call(
        body, grid=(T // tm,),
        in_specs=[pl.BlockSpec((tm, ZS_W), lambda i: (i, 0)), pl.BlockSpec((tm, 1), lambda i: (i, 0)),
                  full(invf), full(qg), full(kvg), full(wuq_p), full(wukv)],
        out_specs=[pl.BlockSpec((tm, HW), lambda i: (i, 0)), pl.BlockSpec((tm, HW), lambda i: (i, 0)),
                   pl.BlockSpec((tm, D_MODEL), lambda i: (i, 0))],
        out_shape=[jax.ShapeDtypeStruct((T, HW), BF16), jax.ShapeDtypeStruct((T, HW), BF16),
                   jax.ShapeDtypeStruct((T, D_MODEL), BF16)],
        name="mla_prep_fwd", compiler_params=_params("parallel"))(zs, pos, invf, qg, kvg, wuq_p, wukv)


def _mla_prep_bwd(zs, pos, invf, qg, kvg, wuq_p, wukv, dq, dk, dv):
    T = zs.shape[0]
    tm = _tile(T, 512)
    n_t = T // tm
    HW = MLA_HEADS * HEAD_PAD

    def body(zs_ref, pos_ref, invf_ref, qg_ref, kvg_ref, wq_ref, wkv_ref, dq_ref, dk_ref, dv_ref,
             dzs_ref, dwq_ref, dwkv_ref, dqg_ref, dkvg_ref, dqp_ref, dkv_ref, accq_ref, acckv_ref):
        @pl.when(pl.program_id(0) == 0)
        def _():
            dqg_ref[...] = jnp.zeros_like(dqg_ref)
            dkvg_ref[...] = jnp.zeros_like(dkvg_ref)
            accq_ref[...] = jnp.zeros_like(accq_ref)
            acckv_ref[...] = jnp.zeros_like(acckv_ref)

        tabs = _rope_tables(pos_ref, invf_ref)
        cqh, rq, cqn = _rms_small(zs_ref[:, 0:Q_LORA], qg_ref[...])
        ckvh, rkv, ckvn = _rms_small(zs_ref[:, Q_LORA:Q_LORA + KV_LORA], kvg_ref[...])
        dkr = jnp.zeros((tm, LANES), F32)
        for h in range(MLA_HEADS):
            b0 = h * HEAD_PAD
            dqp_ref[:, b0:b0 + QK_NOPE] = dq_ref[:, b0:b0 + QK_NOPE]
            dqp_ref[:, b0 + QK_NOPE:b0 + HEAD_PAD] = _rope_t(
                dq_ref[:, b0 + QK_NOPE:b0 + HEAD_PAD].astype(F32), tabs).astype(BF16)
            dkv_ref[:, b0:b0 + QK_NOPE] = dk_ref[:, b0:b0 + QK_NOPE]
            dkv_ref[:, b0 + QK_NOPE:b0 + HEAD_PAD] = dv_ref[:, h * V_HEAD:(h + 1) * V_HEAD]
            dkr = dkr + dk_ref[:, b0 + QK_NOPE:b0 + HEAD_PAD].astype(F32)
        accq_ref[...] += _dot_tn(dqp_ref[...], cqn.astype(BF16))
        acckv_ref[...] += _dot_tn(ckvn.astype(BF16), dkv_ref[...])

        @pl.when(pl.program_id(0) == n_t - 1)
        def _():
            dwq_ref[...] = accq_ref[...].astype(BF16)
            dwkv_ref[...] = acckv_ref[...].astype(BF16)

        dcqn = _dot(dqp_ref[...], wq_ref[...])
        dckvn = _dot_nt(dkv_ref[...], wkv_ref[...])
        dqg_ref[...] += jnp.sum(dcqn * cqh, axis=0, keepdims=True)
        dkvg_ref[...] += jnp.sum(dckvn * ckvh, axis=0, keepdims=True)
        dxh = dcqn * qg_ref[...]
        dzs_ref[:, 0:Q_LORA] = (rq * (dxh - cqh * jnp.mean(dxh * cqh, axis=-1, keepdims=True))).astype(BF16)
        dxh = dckvn * kvg_ref[...]
        dzs_ref[:, Q_LORA:Q_LORA + KV_LORA] = (
            rkv * (dxh - ckvh * jnp.mean(dxh * ckvh, axis=-1, keepdims=True))).astype(BF16)
        dzs_ref[:, Q_LORA + KV_LORA:ZS_W] = _rope_t(dkr, tabs).astype(BF16)

    full = lambda a: pl.BlockSpec(a.shape, lambda i: (0,) * a.ndim)
    rowb = lambda w: pl.BlockSpec((tm, w), lambda i: (i, 0))
    return pl.pallas_call(
        body, grid=(T // tm,),
        in_specs=[rowb(ZS_W), rowb(1), full(invf), full(qg), full(kvg), full(wuq_p), full(wukv),
                  rowb(HW), rowb(HW), rowb(D_MODEL)],
        out_specs=[rowb(ZS_W), full(wuq_p), full(wukv), full(qg), full(kvg)],
        out_shape=[jax.ShapeDtypeStruct((T, ZS_W), BF16), jax.ShapeDtypeStruct(wuq_p.shape, BF16),
                   jax.ShapeDtypeStruct(wukv.shape, BF16), jax.ShapeDtypeStruct(qg.shape, F32),
                   jax.ShapeDtypeStruct(kvg.shape, F32)],
        scratch_shapes=[pltpu.VMEM((tm, HW), BF16), pltpu.VMEM((tm, HW), BF16),
                        pltpu.VMEM(wuq_p.shape, F32), pltpu.VMEM(wukv.shape, F32)],
        name="mla_prep_bwd", compiler_params=_params("arbitrary"))(
            zs, pos, invf, qg, kvg, wuq_p, wukv, dq, dk, dv)


def _causal(tq, kmax, q0):
    r = lax.broadcasted_iota(jnp.int32, (tq, kmax), 0) + q0
    c = lax.broadcasted_iota(jnp.int32, (tq, kmax), 1)
    return c <= r


def _attn_fwd(q, k, v, batch, seq):
    tq = _tile(seq, ATTN_TILE)
    nq = seq // tq

    def body(q_ref, k_ref, v_ref, o_ref, lse_ref):
        diag = _causal(tq, tq, 0)
        for qi in range(nq):
            rows = slice(qi * tq, (qi + 1) * tq)
            qr = q_ref[rows, :]
            s_d = jnp.where(diag, _dot_nt(qr, k_ref[rows, :]), NEG_BIG)
            m = jnp.max(s_d, axis=-1, keepdims=True)
            if qi > 0:
                before = slice(0, qi * tq)
                s_b = _dot_nt(qr, k_ref[before, :])
                m = jnp.maximum(m, jnp.max(s_b, axis=-1, keepdims=True))
                p_b = jnp.exp(s_b - m)
                l = jnp.sum(p_b, axis=-1, keepdims=True)
                acc = _dot(p_b.astype(BF16), v_ref[before, :])
            p_d = jnp.exp(s_d - m)
            l_d = jnp.sum(p_d, axis=-1, keepdims=True)
            acc_d = _dot(p_d.astype(BF16), v_ref[rows, :])
            l, acc = (l + l_d, acc + acc_d) if qi > 0 else (l_d, acc_d)
            o_ref[rows, :] = (acc / l).astype(KEPT)
            lse_ref[rows, :] = jnp.broadcast_to(m + jnp.log(l), (tq, V_HEAD))

    return pl.pallas_call(
        body, grid=(batch, MLA_HEADS),
        in_specs=[pl.BlockSpec((seq, HEAD_PAD), lambda b, h: (b, h)),
                  pl.BlockSpec((seq, HEAD_PAD), lambda b, h: (b, h)),
                  pl.BlockSpec((seq, V_HEAD), lambda b, h: (b, h))],
        out_specs=[pl.BlockSpec((seq, V_HEAD), lambda b, h: (b, h)),
                   pl.BlockSpec((seq, V_HEAD), lambda b, h: (b, h))],
        out_shape=[jax.ShapeDtypeStruct((batch * seq, D_MODEL), KEPT),
                   jax.ShapeDtypeStruct((batch * seq, D_MODEL), F32)],
        name="attn_fwd", compiler_params=_params("parallel", "parallel"))(q, k, v)


def _attn_bwd(q, k, v, o, do, lse, batch, seq, dep):
    tq = _tile(seq, ATTN_TILE)
    nq = seq // tq

    def body(q_ref, k_ref, v_ref, o_ref, do_ref, lse_ref, dep_ref, dq_ref, dk_ref, dv_ref, dk_acc, dv_acc):
        dk_acc[...] = jnp.zeros_like(dk_acc)
        dv_acc[...] = jnp.zeros_like(dv_acc)
        for qi in range(nq):
            rows = slice(qi * tq, (qi + 1) * tq)
            kmax = (qi + 1) * tq
            qr = q_ref[rows, :]
            dor = do_ref[rows, :]
            kk = k_ref[0:kmax, :]
            s = _dot_nt(qr, kk)
            p = jnp.where(_causal(tq, kmax, qi * tq), jnp.exp(s - lse_ref[rows, 0:1]), 0.0)
            dp = _dot_nt(dor, v_ref[0:kmax, :])
            delta = jnp.sum(dor.astype(F32) * o_ref[rows, :].astype(F32), axis=-1, keepdims=True)
            ds = (p * (dp - delta)).astype(BF16)
            dq_ref[rows, :] = (_dot(ds, kk) * ATTN_SCALE).astype(BF16)
            dk_acc[0:kmax, :] += _dot_tn(ds, qr)
            dv_acc[0:kmax, :] += _dot_tn(p.astype(BF16), dor)
        dk_ref[...] = dk_acc[...].astype(BF16)
        dv_ref[...] = dv_acc[...].astype(BF16)

    qspec = pl.BlockSpec((seq, HEAD_PAD), lambda b, h: (b, h))
    vspec = pl.BlockSpec((seq, V_HEAD), lambda b, h: (b, h))
    T = batch * seq
    return pl.pallas_call(
        body, grid=(batch, MLA_HEADS),
        in_specs=[qspec, qspec, vspec, vspec, vspec, vspec, ANY],
        out_specs=[qspec, qspec, vspec],
        out_shape=[jax.ShapeDtypeStruct((T, MLA_HEADS * HEAD_PAD), BF16),
                   jax.ShapeDtypeStruct((T, MLA_HEADS * HEAD_PAD), BF16),
                   jax.ShapeDtypeStruct((T, D_MODEL), BF16)],
        scratch_shapes=[pltpu.VMEM((seq, HEAD_PAD), F32), pltpu.VMEM((seq, V_HEAD), F32)],
        name="attn_bwd", compiler_params=_params("parallel", "parallel"))(q, k, v, o, do, lse, dep)


def _merge_out(x, yag, zm, o, w_out, ffn_g):
    T = x.shape[0]
    tm = _tile(T, 512)

    def body(x_ref, ya_ref, gb_ref, o_ref, w_ref, g_ref, mg_ref, x1_ref, h2_ref):
        mg = (ya_ref[...].astype(F32) + _sigmoid(gb_ref[...].astype(F32)) * o_ref[...].astype(F32)).astype(BF16)
        mg_ref[...] = mg
        x1 = x_ref[...] + _dot(mg, w_ref[...])
        x1_ref[...] = x1
        r = lax.rsqrt(jnp.mean(x1 * x1, axis=-1, keepdims=True) + EPS)
        h2_ref[...] = (x1 * r * g_ref[...]).astype(BF16)

    row = pl.BlockSpec((tm, D_MODEL), lambda i: (i, 0))
    return pl.pallas_call(
        body, grid=(T // tm,),
        in_specs=[row, row, pl.BlockSpec((tm, D_MODEL), lambda i: (i, 3)), row,
                  pl.BlockSpec((D_MODEL, D_MODEL), lambda i: (0, 0)), pl.BlockSpec((1, D_MODEL), lambda i: (0, 0))],
        out_specs=[row, row, row],
        out_shape=[jax.ShapeDtypeStruct((T, D_MODEL), BF16), jax.ShapeDtypeStruct((T, D_MODEL), F32),
                   jax.ShapeDtypeStruct((T, D_MODEL), BF16)],
        name="merge_out", compiler_params=_params("parallel"))(x, yag, zm, o, w_out, ffn_g)


FF_TILE = 256
FF_BLOCKS = D_FF // FF_TILE
FFB_TILE = 256
UP_ROWS = 512
EDGE = 16


def _shift_up(x, k):
    n = x.shape[0]
    row = lax.broadcasted_iota(jnp.int32, x.shape, 0)
    return jnp.where(row < n - k, pltpu.roll(x, n - k, 0), 0.0)


def _up_act(h2, wt_up, cw, cb, batch, seq):
    def body(h_ref, wug_ref, wuv_ref, wg_ref, wv_ref, bg_ref, bv_ref, ug_ref, uv_ref, g_ref, v_ref, a_ref,
             ug_s, uv_s):
        for s in (ug_s, uv_s):
            s[0:SUBLANES, :] = jnp.zeros((SUBLANES, FF_TILE), F32)

        def conv(s, w_ref, b_ref, r0):
            return (b_ref[...] + w_ref[2:3, :] * s[r0:r0 + UP_ROWS, :]
                    + w_ref[1:2, :] * s[r0 - 1:r0 - 1 + UP_ROWS, :]
                    + w_ref[0:1, :] * s[r0 - 2:r0 - 2 + UP_ROWS, :])

        for c in range(seq // UP_ROWS):
            rows = slice(c * UP_ROWS, (c + 1) * UP_ROWS)
            r0 = SUBLANES + c * UP_ROWS
            h = h_ref[rows, :]
            for w_ref, u_ref, s in ((wug_ref, ug_ref, ug_s), (wuv_ref, uv_ref, uv_s)):
                u = _dot_nt(h, w_ref[...])
                u_ref[rows, :] = u.astype(KEPT)
                s[r0:r0 + UP_ROWS, :] = u
            gate, val = conv(ug_s, wg_ref, bg_ref, r0), conv(uv_s, wv_ref, bv_ref, r0)
            g_ref[rows, :] = gate.astype(KEPT)
            v_ref[rows, :] = val.astype(KEPT)
            a_ref[rows, :] = (gate * _sigmoid(gate) * val).astype(BF16)

    blk = pl.BlockSpec((seq, FF_TILE), lambda b, j: (b, j))
    wup = lambda off: pl.BlockSpec((FF_TILE, D_MODEL), lambda b, j: (j + off, 0))
    wsp = lambda off: pl.BlockSpec((3, FF_TILE), lambda b, j: (0, j + off))
    bsp = lambda off: pl.BlockSpec((1, FF_TILE), lambda b, j: (0, j + off))
    T = batch * seq
    kept = jax.ShapeDtypeStruct((T, D_FF), KEPT)
    return pl.pallas_call(
        body, grid=(batch, FF_BLOCKS),
        in_specs=[pl.BlockSpec((seq, D_MODEL), lambda b, j: (b, 0)), wup(0), wup(FF_BLOCKS),
                  wsp(0), wsp(FF_BLOCKS), bsp(0), bsp(FF_BLOCKS)],
        out_specs=[blk] * 5,
        out_shape=[kept, kept, kept, kept, jax.ShapeDtypeStruct((T, D_FF), BF16)],
        scratch_shapes=[pltpu.VMEM((SUBLANES + seq, FF_TILE), F32)] * 2,
        name="up_act", compiler_params=_params("parallel", "arbitrary"))(h2, wt_up, wt_up, cw, cw, cb, cb)


def _ffn_act_bwd(upg, upv, gate, val, cw, dx2b, w_down, batch, seq):
    def half(du, x, w_ref, dx_ref, dw_ref):
        j = pl.program_id(1)
        n = du.shape[0]
        up1, up2 = pltpu.roll(du, n - 1, 0), pltpu.roll(du, n - 2, 0)
        dx_ref[...] = (w_ref[2:3, :] * du + w_ref[1:2, :] * up1 + w_ref[0:1, :] * up2).astype(BF16)
        tail = du[n - EDGE:n]
        dx_ref[n - EDGE:n, :] = (w_ref[2:3, :] * tail + w_ref[1:2, :] * _shift_up(tail, 1)
                                 + w_ref[0:1, :] * _shift_up(tail, 2)).astype(BF16)
        row = lax.broadcasted_iota(jnp.int32, (EDGE, du.shape[1]), 0)
        head, x_tail = du[0:EDGE], x[n - EDGE:n]
        wrap1 = jnp.sum(jnp.where(row >= EDGE - 1, pltpu.roll(head, EDGE - 1, 0), 0.0) * x_tail, axis=0, keepdims=True)
        wrap2 = jnp.sum(jnp.where(row >= EDGE - 2, pltpu.roll(head, EDGE - 2, 0), 0.0) * x_tail, axis=0, keepdims=True)
        dw_ref[j, 2:3, :] += jnp.sum(du * x, axis=0, keepdims=True)
        dw_ref[j, 1:2, :] += jnp.sum(up1 * x, axis=0, keepdims=True) - wrap1
        dw_ref[j, 0:1, :] += jnp.sum(up2 * x, axis=0, keepdims=True) - wrap2
        dw_ref[j, 3:4, :] += jnp.sum(du, axis=0, keepdims=True)

    def body(ug_ref, uv_ref, g_ref, v_ref, wg_ref, wv_ref, dx_ref, wd_ref, dg_ref, dv_ref, dwg_ref, dwv_ref):
        @pl.when((pl.program_id(0) == 0) & (pl.program_id(1) == 0))
        def _():
            dwg_ref[...] = jnp.zeros_like(dwg_ref)
            dwv_ref[...] = jnp.zeros_like(dwv_ref)

        gate, val = g_ref[...].astype(F32), v_ref[...].astype(F32)
        sg = _sigmoid(gate)
        dav = _dot_nt(dx_ref[...], wd_ref[...])
        half(dav * val * sg * (1.0 + gate * (1.0 - sg)), ug_ref[...].astype(F32), wg_ref, dg_ref, dwg_ref)
        half(dav * gate * sg, uv_ref[...].astype(F32), wv_ref, dv_ref, dwv_ref)

    nb = D_FF // FFB_TILE
    blk = pl.BlockSpec((seq, FFB_TILE), lambda b, j: (b, j))
    wsp = lambda off: pl.BlockSpec((3, FFB_TILE), lambda b, j: (0, j + off))
    acc = pl.BlockSpec((nb, 4, FFB_TILE), lambda b, j: (0, 0, 0))
    T = batch * seq
    dupg, dupv, dwg, dwv = pl.pallas_call(
        body, grid=(batch, nb),
        in_specs=[blk, blk, blk, blk, wsp(0), wsp(nb),
                  pl.BlockSpec((seq, D_MODEL), lambda b, j: (b, 0)),
                  pl.BlockSpec((FFB_TILE, D_MODEL), lambda b, j: (j, 0))],
        out_specs=[blk, blk, acc, acc],
        out_shape=[jax.ShapeDtypeStruct((T, D_FF), BF16), jax.ShapeDtypeStruct((T, D_FF), BF16),
                   jax.ShapeDtypeStruct((nb, 4, FFB_TILE), F32), jax.ShapeDtypeStruct((nb, 4, FFB_TILE), F32)],
        name="ffn_act_bwd", compiler_params=_params("arbitrary", "arbitrary"))(
            upg, upv, gate, val, cw, cw, dx2b, w_down)
    dwg, dwv = (jnp.transpose(a, (1, 0, 2)).reshape(4, D_FF) for a in (dwg, dwv))
    return dupg, dupv, dwg[:3], dwv[:3], dwg[3:], dwv[3:]


def _down_loss(a, w_down, x1, target, gfin):
    T = x1.shape[0]
    tm = _tile(T, 512)

    def body(a_ref, w_ref, x1_ref, t_ref, g_ref, dx_ref, dxb_ref, loss_ref, dg_ref):
        @pl.when(pl.program_id(0) == 0)
        def _():
            loss_ref[...] = jnp.zeros_like(loss_ref)
            dg_ref[...] = jnp.zeros_like(dg_ref)

        x2 = x1_ref[...] + _dot(a_ref[...], w_ref[...])
        r = lax.rsqrt(jnp.mean(x2 * x2, axis=-1, keepdims=True) + EPS)
        xh = x2 * r
        g = g_ref[...]
        diff = xh * g - t_ref[...]
        loss_ref[...] += 0.5 * jnp.sum(jnp.mean(diff * diff, axis=-1, keepdims=True))
        dy = diff * (1.0 / D_MODEL)
        dg_ref[...] += jnp.sum(dy * xh, axis=0, keepdims=True)
        dxh = dy * g
        dx = r * (dxh - xh * jnp.mean(dxh * xh, axis=-1, keepdims=True))
        dx_ref[...] = dx
        dxb_ref[...] = dx.astype(BF16)

    row = pl.BlockSpec((tm, D_MODEL), lambda i: (i, 0))
    vec = pl.BlockSpec((1, D_MODEL), lambda i: (0, 0))
    return pl.pallas_call(
        body, grid=(T // tm,),
        in_specs=[pl.BlockSpec((tm, D_FF), lambda i: (i, 0)),
                  pl.BlockSpec((D_FF, D_MODEL), lambda i: (0, 0)), row, row, vec],
        out_specs=[row, row, pl.BlockSpec((8, LANES), lambda i: (0, 0)), vec],
        out_shape=[jax.ShapeDtypeStruct((T, D_MODEL), F32), jax.ShapeDtypeStruct((T, D_MODEL), BF16),
                   jax.ShapeDtypeStruct((8, LANES), F32), jax.ShapeDtypeStruct((1, D_MODEL), F32)],
        name="down_loss", compiler_params=_params("arbitrary"))(a, w_down, x1, target, gfin)


def _local_step(x, positions, target, mix_norm, av_g, av_b, w_s, b_s, q_norm, kv_norm, ffn_norm, conv_b,
                final_norm, comm):
    batch, seq, _ = x.shape
    T = batch * seq
    x = x.reshape(T, D_MODEL)
    target = target.reshape(T, D_MODEL)
    pos = positions.reshape(T, 1)
    half = jnp.arange(0, QK_ROPE, 2, dtype=F32) / QK_ROPE
    inv_freq = 1.0 / (ROPE_THETA ** half)
    invf = jnp.concatenate([inv_freq, inv_freq, jnp.zeros((LANES - QK_ROPE,), F32)]).reshape(1, LANES)
    w_st = jnp.swapaxes(w_s, 1, 2)
    b_col = b_s.reshape(A_GROUPS, CHUNK, 1)

    wt_in = comm.in_weights()
    h, zm, zs = _in_proj(x, mix_norm, wt_in)
    yag = _mixer_a_fwd(zm, av_g, av_b, w_s, b_col)
    wuq_p, wukv, w_out = comm.mla_weights(after=yag)
    q, k, v = _mla_prep_fwd(zs, pos, invf, q_norm, kv_norm, wuq_p, wukv)
    o, lse = _attn_fwd(q, k, v, batch, seq)
    merged, x1, h2 = _merge_out(x, yag, zm, o, w_out, ffn_norm)
    wt_up, conv_w, w_down = comm.ffn_weights(after=merged)
    upg, upv, gate, val, act = _up_act(h2, wt_up, conv_w, conv_b, batch, seq)
    dx2, dx2b, loss_acc, d_final = _down_loss(act, w_down, x1, target, final_norm)

    d_wdown = _mm_tn(act, dx2b, "dw_down")
    dupg, dupv, dcwg, dcwv, dcbg, dcbv = _ffn_act_bwd(upg, upv, gate, val, conv_w, dx2b, w_down, batch, seq)
    d_wt_up = _mm_tn(dupv, h2, "dw_up_val", rows=2 * D_FF, row0=D_FF,
                     into=_mm_tn(dupg, h2, "dw_up_gate", rows=2 * D_FF))
    dx1, d_ffn_norm, dmerged = _proj_bwd(
        [dupg, dupv], wt_up, [(0, (0, D_FF), (0, D_FF)), (1, (0, D_FF), (D_FF, 2 * D_FF))],
        x1, ffn_norm, dx2, "up_proj_bwd", w2=w_out)
    d_wout = _mm_tn(merged, dx1, "dw_out")
    token = comm.send_ffn_grads(d_wdown, d_wt_up, jnp.concatenate([dcwg, dcwv], axis=1), d_wout)
    dzm, do, d_avg, d_avb, d_ws, d_bs = _mixer_bwd(zm, o, dmerged, av_g, av_b, w_s, w_st, b_col, token)
    zs_rows_zero = lax.empty((IN_DIM, D_MODEL), BF16).at[SPLIT_V:SPLIT_KR].set(0)
    d_wt_in = _mm_tn(dzm, h, "dw_in_main", rows=IN_DIM, gap=(SPLIT_V, ZS_ROWS), into=zs_rows_zero)
    token = comm.send_early_grads(d_wt_in, [
        d_avg, d_avb, _small_2d(d_ws).astype(BF16), d_bs.reshape(A_GROUPS, CHUNK), d_ffn_norm,
        jnp.concatenate([dcbg, dcbv], axis=1), d_final])
    dq, dk, dv = _attn_bwd(q, k, v, o, do, lse, batch, seq, token)
    dzs, d_wuq_p, d_wukv, d_qn, d_kvn = _mla_prep_bwd(zs, pos, invf, q_norm, kv_norm, wuq_p, wukv, dq, dk, dv)
    d_wt_zs = _mm_tn(dzs, h, "dw_in_small")
    token = comm.send_mla_grads(d_wuq_p, d_wukv, d_wt_zs)
    terms = [(0, (i * D_MODEL, (i + 1) * D_MODEL), rows) for i, rows in enumerate(IN_ROWS_MAIN)]
    terms.append((1, (0, ZS_W), IN_ROWS_ZS))
    dx, d_mix_norm = _proj_bwd([dzm, dzs], wt_in, terms, x, mix_norm, dx1, "in_proj_bwd", dep=token, rows=512)
    token = comm.send_late_grads([d_qn, d_kvn, d_mix_norm, loss_acc])
    return dx.reshape(batch, seq, D_MODEL), token


MESH_ID = pl.DeviceIdType.MESH
EFFECT = pltpu.SideEffectType.DATAFLOW_SIDE_EFFECTING


def _mesh_pos():
    return lax.axis_index("x"), lax.axis_index("y"), lax.axis_index("c")


def _peer(pos, d):
    x, y, c = pos
    px = 1 - x if d & 4 else x
    py = 1 - y if d & 2 else y
    pc = 1 - c if d & 1 else c
    return (px, py, pc), 4 * px + 2 * py + pc


def _copy(src_ref, land_ref, send_sems, recv_sems, a, d, pos, exchange, landing_here):
    peer, pid = _peer(pos, d)
    me = 4 * pos[0] + 2 * pos[1] + pos[2]
    if exchange:
        src, dst = src_ref.at[pid], land_ref.at[d]
    else:
        src, dst = src_ref, land_ref.at[pid if landing_here else me]
    return pltpu.make_async_remote_copy(
        src_ref=src, dst_ref=dst, send_sem=send_sems.at[a * (N_DEV - 1) + d - 1],
        recv_sem=recv_sems.at[a * (N_DEV - 1) + d - 1],
        device_id=peer, device_id_type=MESH_ID)


def _start_copies(groups, modes, name, dep=None):
    sizes = [len(g) for g in groups]
    srcs = [s for g in groups for s in g]
    lands = [lax.empty(s.shape if modes[gi] else (N_DEV,) + s.shape, s.dtype)
             for gi, g in enumerate(groups) for s in g]
    n, ng = len(srcs), len(groups)
    n_in = 2 * n + (dep is not None)

    def body(*refs):
        src_refs, land_refs = refs[:n], refs[n:2 * n]
        sems = refs[n_in:n_in + 3 * ng]
        token = refs[-1]
        pos = _mesh_pos()
        k = 0
        for gi, size in enumerate(sizes):
            for a in range(size):
                _own_copy(src_refs[k], land_refs[k], sems[3 * gi + 2], a, pos, modes[gi]).start()
                for d in range(1, N_DEV):
                    _copy(src_refs[k], land_refs[k], sems[3 * gi], sems[3 * gi + 1], a, d, pos, modes[gi],
                          landing_here=False).start()
                k += 1
        token[...] = jnp.zeros_like(token)

    sem_shapes = []
    for size in sizes:
        remote = pltpu.SemaphoreType.DMA((size * (N_DEV - 1),))
        sem_shapes += [remote, remote, pltpu.SemaphoreType.DMA((size,))]
    out = pl.pallas_call(
        body, name=name,
        out_shape=(*sem_shapes, *[pltpu.HBM(a.shape, a.dtype) for a in srcs + lands],
                   jax.ShapeDtypeStruct((8, LANES), F32)),
        in_specs=[HBM] * (2 * n) + [ANY] * (dep is not None),
        out_specs=(*[SEM] * (3 * ng), *[HBM] * (2 * n), pl.BlockSpec(memory_space=pltpu.VMEM)),
        input_output_aliases={i: 3 * ng + i for i in range(2 * n)},
        compiler_params=pltpu.CompilerParams(has_side_effects=EFFECT),
    )(*[pltpu.with_memory_space_constraint(a, pltpu.HBM) for a in srcs + lands], *([dep] if dep is not None else []))
    thru = out[3 * ng:3 * ng + 2 * n]
    handles, k = [], 0
    for gi, size in enumerate(sizes):
        handles.append((out[3 * gi:3 * gi + 3], thru[k:k + size], thru[n + k:n + k + size]))
        k += size
    return handles, out[-1]


def _own_copy(src_ref, land_ref, local_sems, a, pos, exchange):
    me = 4 * pos[0] + 2 * pos[1] + pos[2]
    src, dst = (src_ref.at[me], land_ref.at[0]) if exchange else (src_ref, land_ref.at[me])
    return pltpu.make_async_copy(src, dst, local_sems.at[a])


def _wait_copies(handle, exchange, after, name):
    sems, srcs, lands = handle
    n = len(srcs)

    def body(*refs):
        src_refs, land_refs = refs[:n], refs[n:2 * n]
        send, recv, local = refs[2 * n:2 * n + 3]
        pos = _mesh_pos()
        for a in range(n):
            _own_copy(src_refs[a], land_refs[a], local, a, pos, exchange).wait()
            for d in range(1, N_DEV):
                cp = _copy(src_refs[a], land_refs[a], send, recv, a, d, pos, exchange, landing_here=True)
                cp.wait_send()
                cp.wait_recv()

    out = pl.pallas_call(
        body, name=name,
        out_shape=tuple(pltpu.HBM(a.shape, a.dtype) for a in (*srcs, *lands)),
        in_specs=[HBM] * (2 * n) + [SEM, SEM, SEM, ANY], out_specs=[HBM] * (2 * n),
        input_output_aliases={i: i for i in range(2 * n)},
        compiler_params=pltpu.CompilerParams(has_side_effects=EFFECT),
    )(*srcs, *lands, *sems, after)
    return out[n:]


def _gather_now(a, name):
    def body(x_ref, out_ref, send_sems, recv_sems, local_sem):
        x, y, c = _mesh_pos()
        me, sibling = (x, y, c), (x, y, 1 - c)
        chips = [(1 - x, y), (x, 1 - y), (1 - x, 1 - y)]

        def slot(p):
            return out_ref.at[4 * p[0] + 2 * p[1] + p[2]]

        def copy(k, block, to, src=None):
            return pltpu.make_async_remote_copy(
                src_ref=slot(block) if src is None else src, dst_ref=slot(block), send_sem=send_sems.at[k],
                recv_sem=recv_sems.at[k], device_id=to, device_id_type=MESH_ID)

        mine = pltpu.make_async_copy(x_ref, slot(me), local_sem)
        mine.start()
        first = [copy(0, me, sibling, src=x_ref)]
        first += [copy(1 + j, me, (*chip, c), src=x_ref) for j, chip in enumerate(chips)]
        for cp in first:
            cp.start()
        passed = [copy(4 + j, (*chip, c), sibling) for j, chip in enumerate(chips)]
        for j, chip in enumerate(chips):
            copy(1 + j, (*chip, c), me).wait_recv()
            passed[j].start()
        copy(0, sibling, me).wait_recv()
        for j, chip in enumerate(chips):
            copy(4 + j, (*chip, 1 - c), me).wait_recv()
        for cp in first + passed:
            cp.wait_send()
        mine.wait()

    return pl.pallas_call(
        body, in_specs=[ANY], out_specs=ANY,
        out_shape=jax.ShapeDtypeStruct((N_DEV,) + a.shape, a.dtype),
        scratch_shapes=[pltpu.SemaphoreType.DMA((N_DEV - 1,)), pltpu.SemaphoreType.DMA((N_DEV - 1,)),
                        pltpu.SemaphoreType.DMA],
        name=name, compiler_params=pltpu.CompilerParams(has_side_effects=True))(a)


def _sum_parts(p_ref):
    g = p_ref[0].astype(F32)
    for k in range(1, N_DEV):
        g = g + p_ref[k].astype(F32)
    return g


def _adamw_update(p_ref, w_ref, m_ref, v_ref, g_ref, d_ref, nm_ref, nv_ref, patch=None):
    c1 = 1.0 - ADAM_B1 ** ADAM_STEP
    c2 = 1.0 - ADAM_B2 ** ADAM_STEP
    g = _sum_parts(p_ref)
    if patch is not None:
        g = patch(g)
    nm = ADAM_B1 * m_ref[...] + (1.0 - ADAM_B1) * g
    nv = ADAM_B2 * v_ref[...] + (1.0 - ADAM_B2) * (g * g)
    g_ref[...] = g
    nm_ref[...] = nm
    nv_ref[...] = nv
    d_ref[...] = -ADAM_LR * ((nm / c1) / (jnp.sqrt(nv / c2) + ADAM_EPS) + ADAM_WD * w_ref[...])


def _adamw_many(parts, ws, ms, vs, sums, name):
    n, ns = len(ws), len(sums)

    def body(*refs):
        ins, outs = refs[:4 * n + ns], refs[4 * n + ns:]
        for i in range(n):
            _adamw_update(ins[i], ins[n + i], ins[2 * n + i], ins[3 * n + i],
                          outs[i], outs[n + i], outs[2 * n + i], outs[3 * n + i])
        for i in range(ns):
            outs[4 * n + i][...] = _sum_parts(ins[4 * n + i])

    full = lambda a: pl.BlockSpec(a.shape, lambda: (0,) * a.ndim)
    args = [*parts, *ws, *ms, *vs, *sums]
    outs = [jax.ShapeDtypeStruct(w.shape, F32) for _ in range(4) for w in ws]
    outs += [jax.ShapeDtypeStruct(s.shape[1:], F32) for s in sums]
    res = pl.pallas_call(
        body, in_specs=[full(a) for a in args], out_specs=[full(o) for o in outs], out_shape=outs,
        name=name, compiler_params=pltpu.CompilerParams(vmem_limit_bytes=VMEM_LIMIT))(*args)
    return res[:n], res[n:2 * n], res[2 * n:3 * n], res[3 * n:4 * n], res[4 * n:]


def _adamw(parts, w, m, v, name, late=None):
    R, C = w.shape
    tr, tc = R, C
    if N_DEV * R * C * parts.dtype.itemsize > SMALL_BLOCK_BYTES:
        tr = next((t for t in range(min(R, 256) // 16 * 16, 15, -16) if R % t == 0), R)
        if tr == R:
            tc = _tile(C, 256)
    more, places = late if late is not None else (None, ())
    assert late is None or tr == R

    def body(p_ref, w_ref, m_ref, v_ref, *refs):
        def patch(g):
            more_ref, g_s = refs[0], refs[-1]
            x, y, c = _mesh_pos()
            me = 4 * x + 2 * y + c
            rows = _sum_parts(more_ref)
            g_s[...] = g
            for dev, row in places:
                g_s[row:row + rows.shape[0], :] += jnp.where(me == dev, rows, 0.0)
            return g_s[...]

        outs = refs[:4] if late is None else refs[1:5]
        _adamw_update(p_ref, w_ref, m_ref, v_ref, *outs, patch=None if late is None else patch)

    blk = pl.BlockSpec((tr, tc), lambda i, j: (i, j))
    shp = jax.ShapeDtypeStruct((R, C), F32)
    in_specs = [pl.BlockSpec((N_DEV, tr, tc), lambda i, j: (0, i, j)), blk, blk, blk]
    if late is not None:
        in_specs.append(pl.BlockSpec((N_DEV, more.shape[1], tc), lambda i, j: (0, 0, j)))
    return pl.pallas_call(
        body, grid=(R // tr, C // tc), in_specs=in_specs,
        out_specs=[blk, blk, blk, blk], out_shape=[shp, shp, shp, shp],
        scratch_shapes=[] if late is None else [pltpu.VMEM((tr, tc), F32)],
        name=name, compiler_params=_params("parallel", "parallel"))(parts, w, m, v, *([] if late is None else [more]))


SPLIT_V = 2 * D_MODEL
SPLIT_KR = SPLIT_V + Q_LORA + KV_LORA + QK_ROPE
IN_DIM = SPLIT_KR + 2 * D_MODEL
IN_ROWS_MAIN = ((0, D_MODEL), (D_MODEL, SPLIT_V), (SPLIT_KR, SPLIT_KR + D_MODEL), (SPLIT_KR + D_MODEL, IN_DIM))
IN_ROWS_ZS = (SPLIT_V, SPLIT_V + ZS_W)
ZS_ROWS = SPLIT_KR - SPLIT_V
IN_SHARD = IN_DIM // N_DEV
ZS_PIECES = tuple(
    (k, max(IN_SHARD * k, SPLIT_V) - SPLIT_V, max(IN_SHARD * k, SPLIT_V) - IN_SHARD * k)
    for k in range(N_DEV) if max(IN_SHARD * k, SPLIT_V) < min(IN_SHARD * (k + 1), SPLIT_KR))
ZS_PIECE_ROWS = ZS_ROWS // len(ZS_PIECES)
assert all(min(IN_SHARD * (k + 1), SPLIT_KR) - max(IN_SHARD * k, SPLIT_V) == ZS_PIECE_ROWS for k, _, _ in ZS_PIECES)

SMALL_EARLY = ("a_v_norm_g", "a_v_norm_b", "a_spatial_w", "a_spatial_b", "ffn_norm", "conv_b", "final_norm")
SMALL_LATE = ("q_a_norm", "kv_a_norm", "mix_norm")


def _small_2d(a):
    return a.reshape(-1, a.shape[-1])


def _cols_from_shards(g):
    return jnp.transpose(g, (1, 0, 2)).reshape(g.shape[1], N_DEV * g.shape[2])


def _shards_from_cols(a):
    R, W = a.shape
    return jnp.transpose(a.reshape(R, N_DEV, W // N_DEV), (1, 0, 2))


class _Comm:
    GATHER_GROUPS = (("w_uq", "w_ukv", "w_out"), ("w_up", "conv_w", "w_down"))
    FFN_GRADS = ("w_down", "w_up", "conv_w", "w_out")
    TRANSPOSED = ("w_in", "w_up", "w_uq")

    def __init__(self, shards):
        local = {n: a.astype(F32 if n == "conv_w" else BF16) for n, a in shards.items()}
        self.g_in = _gather_now(local["w_in"], "gather_w_in")
        groups = [[local[n] for n in g] for g in self.GATHER_GROUPS]
        (self.h_mla, self.h_ffn), _ = _start_copies(groups, [False] * 2, "gather_start", dep=self.g_in)

    def in_weights(self):
        return self.g_in.reshape(IN_DIM, D_MODEL)

    def mla_weights(self, after):
        g_uq, g_ukv, g_out = _wait_copies(self.h_mla, False, after, "gather_wait_mla")
        wuq_p = jnp.pad(g_uq, ((0, 0), (0, HEAD_PAD - QK_HEAD), (0, 0))).reshape(MLA_HEADS * HEAD_PAD, Q_LORA)
        return wuq_p, _cols_from_shards(g_ukv), g_out.reshape(D_MODEL, D_MODEL)

    def ffn_weights(self, after):
        g_up, g_cw, g_down = _wait_copies(self.h_ffn, False, after, "gather_wait_ffn")
        return g_up.reshape(2 * D_FF, D_MODEL), _cols_from_shards(g_cw), g_down.reshape(D_FF, D_MODEL)

    def send_ffn_grads(self, d_wdown, d_wt_up, d_convw, d_wout):
        group = [d_wdown.reshape(N_DEV, D_FF // N_DEV, D_MODEL), d_wt_up.reshape(N_DEV, 2 * D_FF // N_DEV, D_MODEL),
                 _shards_from_cols(d_convw), d_wout.reshape(N_DEV, D_MODEL // N_DEV, D_MODEL)]
        (self.h_ffn_grads,), token = _start_copies([group], [True], "ffn_grads_start")
        return token

    def send_early_grads(self, d_wt_in, grads):
        blocks = d_wt_in.reshape(N_DEV, IN_SHARD, D_MODEL)
        (self.h_small_early, self.h_in_grads), token = _start_copies(
            [grads, [blocks]], [False, True], "early_grads_start")
        return token

    def send_mla_grads(self, d_wuq_p, d_wukv, d_wt_zs):
        d_uq = d_wuq_p.reshape(MLA_HEADS, HEAD_PAD, Q_LORA)[:, :QK_HEAD, :]
        zs_blocks = jnp.zeros((N_DEV, ZS_PIECE_ROWS, D_MODEL), d_wt_zs.dtype)
        for dev, first, _ in ZS_PIECES:
            zs_blocks = zs_blocks.at[dev].set(d_wt_zs[first:first + ZS_PIECE_ROWS])
        (self.h_mla_grads,), token = _start_copies(
            [[d_uq, _shards_from_cols(d_wukv), zs_blocks]], [True], "mla_grads_start")
        return token

    def send_late_grads(self, small):
        (self.h_late_small,), token = _start_copies([small], [False], "late_grads_start")
        return token


def kernel(x, positions, mix_norm, w_in, a_v_norm_g, a_v_norm_b, a_spatial_w, a_spatial_b, q_a_norm, w_uq, kv_a_norm, w_ukv, w_out, ffn_norm, w_up, conv_w, conv_b, w_down, final_norm, loss_target, m_mix_norm, m_w_in, m_a_v_norm_g, m_a_v_norm_b, m_a_spatial_w, m_a_spatial_b, m_q_a_norm, m_w_uq, m_kv_a_norm, m_w_ukv, m_w_out, m_ffn_norm, m_w_up, m_conv_w, m_conv_b, m_w_down, m_final_norm, v_mix_norm, v_w_in, v_a_v_norm_g, v_a_v_norm_b, v_a_spatial_w, v_a_spatial_b, v_q_a_norm, v_w_uq, v_kv_a_norm, v_w_ukv, v_w_out, v_ffn_norm, v_w_up, v_conv_w, v_conv_b, v_w_down, v_final_norm):
    names = ("mix_norm", "w_in", "a_v_norm_g", "a_v_norm_b", "a_spatial_w", "a_spatial_b", "q_a_norm", "w_uq",
             "kv_a_norm", "w_ukv", "w_out", "ffn_norm", "w_up", "conv_w", "conv_b", "w_down", "final_norm")
    w = dict(zip(names, (mix_norm, w_in, a_v_norm_g, a_v_norm_b, a_spatial_w, a_spatial_b, q_a_norm, w_uq,
                         kv_a_norm, w_ukv, w_out, ffn_norm, w_up, conv_w, conv_b, w_down, final_norm)))
    m = dict(zip(names, (m_mix_norm, m_w_in, m_a_v_norm_g, m_a_v_norm_b, m_a_spatial_w, m_a_spatial_b,
                         m_q_a_norm, m_w_uq, m_kv_a_norm, m_w_ukv, m_w_out, m_ffn_norm, m_w_up, m_conv_w,
                         m_conv_b, m_w_down, m_final_norm)))
    v = dict(zip(names, (v_mix_norm, v_w_in, v_a_v_norm_g, v_a_v_norm_b, v_a_spatial_w, v_a_spatial_b,
                         v_q_a_norm, v_w_uq, v_kv_a_norm, v_w_ukv, v_w_out, v_ffn_norm, v_w_up, v_conv_w,
                         v_conv_b, v_w_down, v_final_norm)))
    shapes = {n: w[n].shape for n in names}
    def view(tree, n):
        a = tree[n].reshape(tree[n].shape[-2:])
        return a.T if n in _Comm.TRANSPOSED else a

    comm = _Comm({n: view(w, n) for n in ("w_in",) + _Comm.GATHER_GROUPS[0] + _Comm.GATHER_GROUPS[1]})

    grad_x, token = _local_step(
        x, positions, loss_target, w["mix_norm"], w["a_v_norm_g"], w["a_v_norm_b"], w["a_spatial_w"][0],
        w["a_spatial_b"][0], w["q_a_norm"], w["kv_a_norm"], w["ffn_norm"], w["conv_b"],
        w["final_norm"].reshape(1, D_MODEL), comm)

    out_g, out_d, out_m, out_v = {}, {}, {}, {}

    def update(n, parts, late=None):
        res = _adamw(parts, view(w, n), view(m, n), view(v, n), "adamw_" + n, late=late)
        out_g[n], out_d[n], out_m[n], out_v[n] = (
            (t.T if n in _Comm.TRANSPOSED else t).reshape(shapes[n]) for t in res)
        return res[1]

    def update_small(names, parts, sums, name):
        res = _adamw_many(parts, *[[_small_2d(t[n]) for n in names] for t in (w, m, v)], sums, name)
        for i, n in enumerate(names):
            out_g[n], out_d[n], out_m[n], out_v[n] = (r[i].reshape(shapes[n]) for r in res[:4])
        return res

    for n, parts in zip(_Comm.FFN_GRADS, _wait_copies(comm.h_ffn_grads, True, token, "ffn_grads_wait")):
        last = update(n, parts)
    early = _wait_copies(comm.h_small_early, False, last, "small_grads_wait")
    last = update_small(SMALL_EARLY, early, [], "adamw_small")[1][0]
    (in_parts,) = _wait_copies(comm.h_in_grads, True, last, "in_grads_wait")
    uq_parts, ukv_parts, zs_parts = _wait_copies(comm.h_mla_grads, True, in_parts, "mla_grads_wait")
    last = update("w_in", in_parts, late=(zs_parts, [(dev, row) for dev, _, row in ZS_PIECES]))
    last = update("w_uq", uq_parts)
    last = update("w_ukv", ukv_parts)
    late = _wait_copies(comm.h_late_small, False, last, "late_small_wait")
    res = update_small(SMALL_LATE, late[:-1], late[-1:], "adamw_late")
    loss = res[4][0][0, 0]

    return (loss, grad_x, *[out_g[n] for n in names], *[out_d[n] for n in names],
            *[out_m[n] for n in names], *[out_v[n] for n in names])
```

```python
import math

import jax
import jax.numpy as jnp
from jax import lax
from jax.experimental import pallas as pl
from jax.experimental.pallas import tpu as pltpu

F32 = jnp.float32
BF16 = jnp.bfloat16
KEPT = jnp.bfloat16

N_DEV = 8
D_MODEL = 1024
EPS = 1e-6
A_GROUPS = 8
CHUNK = 128
MLA_HEADS = 8
QK_NOPE = 128
QK_ROPE = 64
QK_HEAD = QK_NOPE + QK_ROPE
HEAD_PAD = 256
V_HEAD = 128
Q_LORA = 256
KV_LORA = 128
ROPE_THETA = 10000.0
D_FF = 2816
ZS_W = 512
ATTN_SCALE = QK_HEAD ** -0.5
ATTN_TILE = 512
NEG_BIG = -1e30

ADAM_LR = 0.001
ADAM_B1 = 0.9
ADAM_B2 = 0.999
ADAM_EPS = 1e-08
ADAM_WD = 0.01
ADAM_STEP = 10

VMEM_LIMIT = 56 * 1024 * 1024
SMALL_BLOCK_BYTES = 5 * 1024 * 1024
LANES = 128
SUBLANES = 8

GELU_K = math.sqrt(2.0 / math.pi)
GELU_C = 0.044715

ANY = pl.BlockSpec(memory_space=pl.ANY)
HBM = pl.BlockSpec(memory_space=pltpu.HBM)
SEM = pl.BlockSpec(memory_space=pltpu.SEMAPHORE)


def _tile(n, pref):
    for t in (pref, 512, 256, 128, 64, 32, 16, 8):
        if t <= pref and n % t == 0:
            return t
    return n


def _wide_tile(n, cap=1408):
    return next((t for t in range(min(n, cap) // LANES * LANES, 0, -LANES) if n % t == 0), n)


def _params(*sem):
    return pltpu.CompilerParams(dimension_semantics=sem, vmem_limit_bytes=VMEM_LIMIT)


def _dot(a, b):
    return jnp.dot(a, b, preferred_element_type=F32)


def _dot_nt(a, b):
    return lax.dot_general(a, b, (((1,), (1,)), ((), ())), preferred_element_type=F32)


def _dot_tn(a, b):
    return lax.dot_general(a, b, (((0,), (0,)), ((), ())), preferred_element_type=F32)


def _sigmoid(x):
    return 1.0 / (1.0 + jnp.exp(-x))


def _gelu(x):
    t = jnp.tanh(GELU_K * (x + GELU_C * x * x * x))
    return 0.5 * x * (1.0 + t)


def _gelu_and_grad(x):
    x2 = x * x
    t = jnp.tanh(GELU_K * (x + GELU_C * x * x2))
    half = 0.5 * (1.0 + t)
    return x * half, half + 0.5 * x * (1.0 - t * t) * GELU_K * (1.0 + 3.0 * GELU_C * x2)


def _in_proj(x, g, wt):
    T, Dm = x.shape
    tm = _tile(T, 512)

    def body(x_ref, g_ref, wt_ref, h_ref, zm_ref, zs_ref):
        xf = x_ref[...]
        r = lax.rsqrt(jnp.mean(xf * xf, axis=-1, keepdims=True) + EPS)
        h = (xf * r * g_ref[...]).astype(BF16)
        h_ref[...] = h
        for i, (r0, r1) in enumerate(IN_ROWS_MAIN):
            zm_ref[:, i * D_MODEL:(i + 1) * D_MODEL] = _dot_nt(h, wt_ref[r0:r1, :]).astype(KEPT)
        zs_ref[...] = _dot_nt(h, wt_ref[IN_ROWS_ZS[0]:IN_ROWS_ZS[1], :])

    row = lambda n: pl.BlockSpec((tm, n), lambda i: (i, 0))
    return pl.pallas_call(
        body, grid=(T // tm,),
        in_specs=[row(Dm), pl.BlockSpec((1, Dm), lambda i: (0, 0)), pl.BlockSpec(wt.shape, lambda i: (0, 0))],
        out_specs=[row(Dm), row(4 * D_MODEL), row(ZS_W)],
        out_shape=[jax.ShapeDtypeStruct((T, Dm), BF16), jax.ShapeDtypeStruct((T, 4 * D_MODEL), KEPT),
                   jax.ShapeDtypeStruct((T, ZS_W), F32)],
        name="in_proj", compiler_params=_params("parallel"))(x, g, wt)


def _proj_bwd(acts, wt, terms, x, g, dres, name, w2=None, dep=None, rows=256):
    T, Dm = x.shape
    tm = _tile(T, rows)
    n_a = len(acts)

    def body(*refs):
        ins, outs = refs[:n_a + 4 + (w2 is not None) + (dep is not None)], refs[-2 - (w2 is not None):]
        wt_ref, x_ref, g_ref, dres_ref = ins[n_a:n_a + 4]
        dx_ref, dg_ref = outs[0], outs[1]

        @pl.when(pl.program_id(0) == 0)
        def _():
            dg_ref[...] = jnp.zeros_like(dg_ref)

        dy = None
        for i, (c0, c1), (r0, r1) in terms:
            t = _dot(ins[i][:, c0:c1], wt_ref[r0:r1, :])
            dy = t if dy is None else dy + t
        xf = x_ref[...]
        r = lax.rsqrt(jnp.mean(xf * xf, axis=-1, keepdims=True) + EPS)
        xh = xf * r
        dg_ref[...] += jnp.sum(dy * xh, axis=0, keepdims=True)
        dxh = dy * g_ref[...]
        dx = dres_ref[...] + r * (dxh - xh * jnp.mean(dxh * xh, axis=-1, keepdims=True))
        dx_ref[...] = dx
        if w2 is not None:
            outs[2][...] = _dot_nt(dx.astype(BF16), ins[n_a + 4][...]).astype(KEPT)

    row = pl.BlockSpec((tm, Dm), lambda i: (i, 0))
    vec = pl.BlockSpec((1, Dm), lambda i: (0, 0))
    in_specs = [pl.BlockSpec((tm, a.shape[1]), lambda i: (i, 0)) for a in acts]
    in_specs += [pl.BlockSpec(wt.shape, lambda i: (0, 0)), row, vec, row]
    args = [*acts, wt, x, g, dres]
    out_specs = [row, vec]
    out_shape = [jax.ShapeDtypeStruct((T, Dm), F32), jax.ShapeDtypeStruct((1, Dm), F32)]
    if w2 is not None:
        in_specs.append(pl.BlockSpec(w2.shape, lambda i: (0, 0)))
        args.append(w2)
        out_specs.append(pl.BlockSpec((tm, w2.shape[0]), lambda i: (i, 0)))
        out_shape.append(jax.ShapeDtypeStruct((T, w2.shape[0]), KEPT))
    if dep is not None:
        in_specs.append(ANY)
        args.append(dep)
    return pl.pallas_call(
        body, grid=(T // tm,), in_specs=in_specs, out_specs=out_specs, out_shape=out_shape,
        name=name, compiler_params=_params("arbitrary"))(*args)


def _mm_tn(a, b, name, dep=None, rows=None, row0=0, into=None, gap=None):
    T, M = a.shape
    N = b.shape[1]
    tm, tn, tt = _wide_tile(M), _wide_tile(N), _tile(T, 2048)
    n_t = T // tt
    off = row0 // tm
    extra = ([dep] if dep is not None else []) + ([into] if into is not None else [])
    if gap is None:
        out_spec = pl.BlockSpec((tm, tn), lambda i, j, t: (i + off, j))
    else:
        unit = 16
        assert row0 == 0 and gap[0] % tm == 0 and tm % unit == 0 and gap[1] % unit == 0
        out_spec = pl.BlockSpec(
            (pl.Element(tm), pl.Element(tn)),
            lambda i, j, t: ((i * (tm // unit) + jnp.where(i * tm >= gap[0], gap[1] // unit, 0)) * unit, j * tn))

    def body(a_ref, b_ref, *refs):
        o_ref, acc_ref = refs[-2:]
        t = pl.program_id(2)

        @pl.when(t == 0)
        def _():
            acc_ref[...] = jnp.zeros_like(acc_ref)

        acc_ref[...] += _dot_tn(a_ref[...].astype(BF16), b_ref[...].astype(BF16))

        @pl.when(t == n_t - 1)
        def _():
            o_ref[...] = acc_ref[...].astype(BF16)

    return pl.pallas_call(
        body, grid=(M // tm, N // tn, n_t),
        in_specs=[pl.BlockSpec((tt, tm), lambda i, j, t: (t, i)),
                  pl.BlockSpec((tt, tn), lambda i, j, t: (t, j))] + [ANY] * len(extra),
        out_specs=out_spec,
        out_shape=jax.ShapeDtypeStruct((rows or M, N), BF16),
        scratch_shapes=[pltpu.VMEM((tm, tn), F32)],
        input_output_aliases={} if into is None else {1 + len(extra): 0},
        name=name, compiler_params=_params("parallel", "parallel", "arbitrary"))(a, b, *extra)


def _layer_norm_fwd(gv, g, b):
    mu = jnp.mean(gv, axis=-1, keepdims=True)
    xc = gv - mu
    rs = lax.rsqrt(jnp.mean(xc * xc, axis=-1, keepdims=True) + EPS)
    xh = xc * rs
    return xh, rs, xh * g + b


def _tri_mask(transposed=False):
    r = lax.broadcasted_iota(jnp.int32, (CHUNK, CHUNK), 0)
    c = lax.broadcasted_iota(jnp.int32, (CHUNK, CHUNK), 1)
    return r <= c if transposed else c <= r


def _mixer_a_fwd(zm, av_g, av_b, w_s, b_col):
    T = zm.shape[0]
    tm = _tile(T, 512)
    n_chunk = tm // CHUNK

    def body(u_ref, v_ref, ga_ref, g_ref, b_ref, w_ref, bc_ref, y_ref, vn_s, mx_s):
        gu = _gelu(u_ref[...].astype(F32))
        _, _, vn = _layer_norm_fwd(_gelu(v_ref[...].astype(F32)), g_ref[...], b_ref[...])
        vn_s[...] = vn.astype(BF16)
        tri = _tri_mask()
        for gi in range(A_GROUPS):
            wm = jnp.where(tri, w_ref[gi], 0.0).astype(BF16)
            cols = slice(gi * CHUNK, (gi + 1) * CHUNK)
            for n in range(n_chunk):
                rows = slice(n * CHUNK, (n + 1) * CHUNK)
                mx_s[rows, cols] = _dot(wm, vn_s[rows, cols]) + bc_ref[gi]
        y_ref[...] = (_sigmoid(ga_ref[...].astype(F32)) * gu * mx_s[...]).astype(KEPT)

    col = lambda c: pl.BlockSpec((tm, D_MODEL), lambda i: (i, c))
    vec = pl.BlockSpec((1, D_MODEL), lambda i: (0, 0))
    return pl.pallas_call(
        body, grid=(T // tm,),
        in_specs=[col(0), col(1), col(2), vec, vec,
                  pl.BlockSpec((A_GROUPS, CHUNK, CHUNK), lambda i: (0, 0, 0)),
                  pl.BlockSpec((A_GROUPS, CHUNK, 1), lambda i: (0, 0, 0))],
        out_specs=pl.BlockSpec((tm, D_MODEL), lambda i: (i, 0)),
        out_shape=jax.ShapeDtypeStruct((T, D_MODEL), KEPT),
        scratch_shapes=[pltpu.VMEM((tm, D_MODEL), BF16), pltpu.VMEM((tm, D_MODEL), F32)],
        name="mixer_a_fwd", compiler_params=_params("parallel"))(zm, zm, zm, av_g, av_b, w_s, b_col)


def _mixer_bwd(zm, o, dm, av_g, av_b, w_s, w_st, b_col, dep):
    T = zm.shape[0]
    tm = _tile(T, 256)
    n_chunk = tm // CHUNK

    def body(u_ref, v_ref, ga_ref, gb_ref, o_ref, dm_ref, g_ref, b_ref, w_ref, wt_ref, bc_ref, dep_ref,
             dz_ref, do_ref, dg_ref, db_ref, dw_ref, dbs_ref, vn_s, mx_s, dmx_s, dvn_s):
        @pl.when(pl.program_id(0) == 0)
        def _():
            dg_ref[...] = jnp.zeros_like(dg_ref)
            db_ref[...] = jnp.zeros_like(db_ref)
            dw_ref[...] = jnp.zeros_like(dw_ref)
            dbs_ref[...] = jnp.zeros_like(dbs_ref)

        dm_v = dm_ref[...].astype(F32)
        gb = gb_ref[...].astype(F32)
        sb = _sigmoid(gb)
        o_v = o_ref[...].astype(F32)
        do_ref[...] = (dm_v * sb).astype(BF16)
        dz_ref[:, 3 * D_MODEL:4 * D_MODEL] = (dm_v * o_v * sb * (1.0 - sb)).astype(BF16)
        u = u_ref[...].astype(F32)
        v = v_ref[...].astype(F32)
        gu, gu_grad = _gelu_and_grad(u)
        gv, gv_grad = _gelu_and_grad(v)
        xh, rs, vn = _layer_norm_fwd(gv, g_ref[...], b_ref[...])
        vn_s[...] = vn.astype(BF16)
        tri = _tri_mask()
        for gi in range(A_GROUPS):
            wm = jnp.where(tri, w_ref[gi], 0.0).astype(BF16)
            cols = slice(gi * CHUNK, (gi + 1) * CHUNK)
            for n in range(n_chunk):
                rows = slice(n * CHUNK, (n + 1) * CHUNK)
                mx_s[rows, cols] = _dot(wm, vn_s[rows, cols]) + bc_ref[gi]
        mixed = mx_s[...]
        sa = _sigmoid(ga_ref[...].astype(F32))
        dya = dm_v * sa
        dz_ref[:, 2 * D_MODEL:3 * D_MODEL] = (dm_v * gu * mixed * sa * (1.0 - sa)).astype(BF16)
        dz_ref[:, 0:D_MODEL] = (dya * mixed * gu_grad).astype(BF16)
        dmx = dya * gu
        dmx_s[...] = dmx.astype(BF16)
        tri_t = _tri_mask(transposed=True)
        for gi in range(A_GROUPS):
            wmt = jnp.where(tri_t, wt_ref[gi], 0.0).astype(BF16)
            cols = slice(gi * CHUNK, (gi + 1) * CHUNK)
            dw_acc = jnp.zeros((CHUNK, CHUNK), F32)
            dmx_sum = jnp.zeros((CHUNK, CHUNK), F32)
            for n in range(n_chunk):
                rows = slice(n * CHUNK, (n + 1) * CHUNK)
                blk = dmx_s[rows, cols]
                dvn_s[rows, cols] = _dot(wmt, blk)
                dw_acc = dw_acc + _dot_nt(blk, vn_s[rows, cols])
                dmx_sum = dmx_sum + dmx[rows, cols]
            dw_ref[gi] += jnp.where(tri, dw_acc, 0.0)
            dbs_ref[gi] += jnp.sum(dmx_sum, axis=-1, keepdims=True)
        dvn = dvn_s[...]
        dg_ref[...] += jnp.sum(dvn * xh, axis=0, keepdims=True)
        db_ref[...] += jnp.sum(dvn, axis=0, keepdims=True)
        dxh = dvn * g_ref[...]
        dgv = rs * (dxh - jnp.mean(dxh, axis=-1, keepdims=True)
                    - xh * jnp.mean(dxh * xh, axis=-1, keepdims=True))
        dz_ref[:, D_MODEL:2 * D_MODEL] = (dgv * gv_grad).astype(BF16)

    col = lambda c: pl.BlockSpec((tm, D_MODEL), lambda i: (i, c))
    row = pl.BlockSpec((tm, D_MODEL), lambda i: (i, 0))
    vec = pl.BlockSpec((1, D_MODEL), lambda i: (0, 0))
    wsp = pl.BlockSpec((A_GROUPS, CHUNK, CHUNK), lambda i: (0, 0, 0))
    bsp = pl.BlockSpec((A_GROUPS, CHUNK, 1), lambda i: (0, 0, 0))
    return pl.pallas_call(
        body, grid=(T // tm,),
        in_specs=[col(0), col(1), col(2), col(3), row, row, vec, vec, wsp, wsp, bsp, ANY],
        out_specs=[pl.BlockSpec((tm, 4 * D_MODEL), lambda i: (i, 0)), row, vec, vec, wsp, bsp],
        out_shape=[jax.ShapeDtypeStruct((T, 4 * D_MODEL), BF16), jax.ShapeDtypeStruct((T, D_MODEL), BF16),
                   jax.ShapeDtypeStruct((1, D_MODEL), F32), jax.ShapeDtypeStruct((1, D_MODEL), F32),
                   jax.ShapeDtypeStruct((A_GROUPS, CHUNK, CHUNK), F32),
                   jax.ShapeDtypeStruct((A_GROUPS, CHUNK, 1), F32)],
        scratch_shapes=[pltpu.VMEM((tm, D_MODEL), BF16), pltpu.VMEM((tm, D_MODEL), F32),
                        pltpu.VMEM((tm, D_MODEL), BF16), pltpu.VMEM((tm, D_MODEL), F32)],
        name="mixer_bwd", compiler_params=_params("arbitrary"))(
            zm, zm, zm, zm, o, dm, av_g, av_b, w_s, w_st, b_col, dep)


def _rope_tables(pos_ref, invf_ref):
    ang = pos_ref[...].astype(F32) * invf_ref[...]
    lane = lax.broadcasted_iota(jnp.int32, ang.shape, 1)
    cos, sin = jnp.cos(ang), jnp.sin(ang)
    c = jnp.where(lane < QK_ROPE, cos, 0.0)
    sa = jnp.where(lane < QK_ROPE // 2, -sin, 0.0)
    sb = jnp.where((lane >= QK_ROPE // 2) & (lane < QK_ROPE), sin, 0.0)
    return c, sa, sb


def _rope(blk, tabs):
    c, sa, sb = tabs
    return blk * c + pltpu.roll(blk, LANES - QK_ROPE // 2, 1) * sa + pltpu.roll(blk, QK_ROPE // 2, 1) * sb


def _rope_t(dout, tabs):
    c, sa, sb = tabs
    return dout * c + pltpu.roll(dout * sa, QK_ROPE // 2, 1) + pltpu.roll(dout * sb, LANES - QK_ROPE // 2, 1)


def _rms_small(x, g):
    r = lax.rsqrt(jnp.mean(x * x, axis=-1, keepdims=True) + EPS)
    xh = x * r
    return xh, r, xh * g


def _mla_prep_fwd(zs, pos, invf, qg, kvg, wuq_p, wukv):
    T = zs.shape[0]
    tm = _tile(T, 512)
    HW = MLA_HEADS * HEAD_PAD

    def body(zs_ref, pos_ref, invf_ref, qg_ref, kvg_ref, wq_ref, wkv_ref, q_ref, k_ref, v_ref):
        tabs = _rope_tables(pos_ref, invf_ref)
        _, _, cqn = _rms_small(zs_ref[:, 0:Q_LORA], qg_ref[...])
        _, _, ckvn = _rms_small(zs_ref[:, Q_LORA:Q_LORA + KV_LORA], kvg_ref[...])
        q = _dot_nt(cqn.astype(BF16), wq_ref[...]) * ATTN_SCALE
        ckvn = ckvn.astype(BF16)
        kr = _rope(zs_ref[:, Q_LORA + KV_LORA:ZS_W], tabs).astype(BF16)
        for h in range(MLA_HEADS):
            b0 = h * HEAD_PAD
            kv = _dot(ckvn, wkv_ref[h])
            q_ref[:, b0:b0 + QK_NOPE] = q[:, b0:b0 + QK_NOPE].astype(BF16)
            q_ref[:, b0 + QK_NOPE:b0 + HEAD_PAD] = _rope(q[:, b0 + QK_NOPE:b0 + HEAD_PAD], tabs).astype(BF16)
            k_ref[:, b0:b0 + QK_NOPE] = kv[:, 0:QK_NOPE].astype(BF16)
            k_ref[:, b0 + QK_NOPE:b0 + HEAD_PAD] = kr
            v_ref[:, h * V_HEAD:(h + 1) * V_HEAD] = kv[:, QK_NOPE:QK_NOPE + V_HEAD].astype(BF16)

    full = lambda a: pl.BlockSpec(a.shape, lambda i: (0,) * a.ndim)
    return pl.pallas_call(
        body, grid=(T // tm,),
        in_specs=[pl.BlockSpec((tm, ZS_W), lambda i: (i, 0)), pl.BlockSpec((tm, 1), lambda i: (i, 0)),
                  full(invf), full(qg), full(kvg), full(wuq_p), full(wukv)],
        out_specs=[pl.BlockSpec((tm, HW), lambda i: (i, 0)), pl.BlockSpec((tm, HW), lambda i: (i, 0)),
                   pl.BlockSpec((tm, D_MODEL), lambda i: (i, 0))],
        out_shape=[jax.ShapeDtypeStruct((T, HW), BF16), jax.ShapeDtypeStruct((T, HW), BF16),
                   jax.ShapeDtypeStruct((T, D_MODEL), BF16)],
        name="mla_prep_fwd", compiler_params=_params("parallel"))(zs, pos, invf, qg, kvg, wuq_p, wukv)


def _mla_prep_bwd(zs, pos, invf, qg, kvg, wuq_p, wukv, dq, dk, dv):
    T = zs.shape[0]
    tm = _tile(T, 512)
    n_t = T // tm
    HW = MLA_HEADS * HEAD_PAD

    def body(zs_ref, pos_ref, invf_ref, qg_ref, kvg_ref, wq_ref, wkv_ref, dq_ref, dk_ref, dv_ref,
             dzs_ref, dwq_ref, dwkv_ref, dqg_ref, dkvg_ref, dqp_ref, dkv_ref, accq_ref, acckv_ref):
        @pl.when(pl.program_id(0) == 0)
        def _():
            dqg_ref[...] = jnp.zeros_like(dqg_ref)
            dkvg_ref[...] = jnp.zeros_like(dkvg_ref)
            accq_ref[...] = jnp.zeros_like(accq_ref)
            acckv_ref[...] = jnp.zeros_like(acckv_ref)

        tabs = _rope_tables(pos_ref, invf_ref)
        cqh, rq, cqn = _rms_small(zs_ref[:, 0:Q_LORA], qg_ref[...])
        ckvh, rkv, ckvn = _rms_small(zs_ref[:, Q_LORA:Q_LORA + KV_LORA], kvg_ref[...])
        dkr = jnp.zeros((tm, LANES), F32)
        for h in range(MLA_HEADS):
            b0 = h * HEAD_PAD
            dqp_ref[:, b0:b0 + QK_NOPE] = dq_ref[:, b0:b0 + QK_NOPE]
            dqp_ref[:, b0 + QK_NOPE:b0 + HEAD_PAD] = _rope_t(
                dq_ref[:, b0 + QK_NOPE:b0 + HEAD_PAD].astype(F32), tabs).astype(BF16)
            dkv_ref[:, b0:b0 + QK_NOPE] = dk_ref[:, b0:b0 + QK_NOPE]
            dkv_ref[:, b0 + QK_NOPE:b0 + HEAD_PAD] = dv_ref[:, h * V_HEAD:(h + 1) * V_HEAD]
            dkr = dkr + dk_ref[:, b0 + QK_NOPE:b0 + HEAD_PAD].astype(F32)
        accq_ref[...] += _dot_tn(dqp_ref[...], cqn.astype(BF16))
        ckvn = ckvn.astype(BF16)
        dckvn = jnp.zeros((tm, KV_LORA), F32)
        for h in range(MLA_HEADS):
            dkv_h = dkv_ref[:, h * HEAD_PAD:(h + 1) * HEAD_PAD]
            acckv_ref[h] += _dot_tn(ckvn, dkv_h)
            dckvn = dckvn + _dot_nt(dkv_h, wkv_ref[h])

        @pl.when(pl.program_id(0) == n_t - 1)
        def _():
            dwq_ref[...] = accq_ref[...].astype(BF16)
            dwkv_ref[...] = acckv_ref[...].astype(BF16)

        dcqn = _dot(dqp_ref[...], wq_ref[...])
        dqg_ref[...] += jnp.sum(dcqn * cqh, axis=0, keepdims=True)
        dkvg_ref[...] += jnp.sum(dckvn * ckvh, axis=0, keepdims=True)
        dxh = dcqn * qg_ref[...]
        dzs_ref[:, 0:Q_LORA] = (rq * (dxh - cqh * jnp.mean(dxh * cqh, axis=-1, keepdims=True))).astype(BF16)
        dxh = dckvn * kvg_ref[...]
        dzs_ref[:, Q_LORA:Q_LORA + KV_LORA] = (
            rkv * (dxh - ckvh * jnp.mean(dxh * ckvh, axis=-1, keepdims=True))).astype(BF16)
        dzs_ref[:, Q_LORA + KV_LORA:ZS_W] = _rope_t(dkr, tabs).astype(BF16)

    full = lambda a: pl.BlockSpec(a.shape, lambda i: (0,) * a.ndim)
    rowb = lambda w: pl.BlockSpec((tm, w), lambda i: (i, 0))
    return pl.pallas_call(
        body, grid=(T // tm,),
        in_specs=[rowb(ZS_W), rowb(1), full(invf), full(qg), full(kvg), full(wuq_p), full(wukv),
                  rowb(HW), rowb(HW), rowb(D_MODEL)],
        out_specs=[rowb(ZS_W), full(wuq_p), full(wukv), full(qg), full(kvg)],
        out_shape=[jax.ShapeDtypeStruct((T, ZS_W), BF16), jax.ShapeDtypeStruct(wuq_p.shape, BF16),
                   jax.ShapeDtypeStruct(wukv.shape, BF16), jax.ShapeDtypeStruct(qg.shape, F32),
                   jax.ShapeDtypeStruct(kvg.shape, F32)],
        scratch_shapes=[pltpu.VMEM((tm, HW), BF16), pltpu.VMEM((tm, HW), BF16),
                        pltpu.VMEM(wuq_p.shape, F32), pltpu.VMEM(wukv.shape, F32)],
        name="mla_prep_bwd", compiler_params=_params("arbitrary"))(
            zs, pos, invf, qg, kvg, wuq_p, wukv, dq, dk, dv)


def _causal(tq, kmax, q0):
    r = lax.broadcasted_iota(jnp.int32, (tq, kmax), 0) + q0
    c = lax.broadcasted_iota(jnp.int32, (tq, kmax), 1)
    return c <= r


def _attn_fwd(q, k, v, batch, seq):
    tq = _tile(seq, ATTN_TILE)
    nq = seq // tq

    def body(q_ref, k_ref, v_ref, o_ref, lse_ref):
        diag = _causal(tq, tq, 0)
        for qi in range(nq):
            rows = slice(qi * tq, (qi + 1) * tq)
            qr = q_ref[rows, :]
            s_d = jnp.where(diag, _dot_nt(qr, k_ref[rows, :]), NEG_BIG)
            m = jnp.max(s_d, axis=-1, keepdims=True)
            if qi > 0:
                before = slice(0, qi * tq)
                s_b = _dot_nt(qr, k_ref[before, :])
                m = jnp.maximum(m, jnp.max(s_b, axis=-1, keepdims=True))
                p_b = jnp.exp(s_b - m)
                l = jnp.sum(p_b, axis=-1, keepdims=True)
                acc = _dot(p_b.astype(BF16), v_ref[before, :])
            p_d = jnp.exp(s_d - m)
            l_d = jnp.sum(p_d, axis=-1, keepdims=True)
            acc_d = _dot(p_d.astype(BF16), v_ref[rows, :])
            l, acc = (l + l_d, acc + acc_d) if qi > 0 else (l_d, acc_d)
            o_ref[rows, :] = (acc / l).astype(KEPT)
            lse_ref[rows, :] = jnp.broadcast_to(m + jnp.log(l), (tq, V_HEAD))

    return pl.pallas_call(
        body, grid=(batch, MLA_HEADS),
        in_specs=[pl.BlockSpec((seq, HEAD_PAD), lambda b, h: (b, h)),
                  pl.BlockSpec((seq, HEAD_PAD), lambda b, h: (b, h)),
                  pl.BlockSpec((seq, V_HEAD), lambda b, h: (b, h))],
        out_specs=[pl.BlockSpec((seq, V_HEAD), lambda b, h: (b, h)),
                   pl.BlockSpec((seq, V_HEAD), lambda b, h: (b, h))],
        out_shape=[jax.ShapeDtypeStruct((batch * seq, D_MODEL), KEPT),
                   jax.ShapeDtypeStruct((batch * seq, D_MODEL), F32)],
        name="attn_fwd", compiler_params=_params("parallel", "parallel"))(q, k, v)


def _attn_bwd(q, k, v, o, do, lse, batch, seq, dep):
    tq = _tile(seq, ATTN_TILE)
    nq = seq // tq

    def body(q_ref, k_ref, v_ref, o_ref, do_ref, lse_ref, dep_ref, dq_ref, dk_ref, dv_ref, dk_acc, dv_acc):
        dk_acc[...] = jnp.zeros_like(dk_acc)
        dv_acc[...] = jnp.zeros_like(dv_acc)
        for qi in range(nq):
            rows = slice(qi * tq, (qi + 1) * tq)
            kmax = (qi + 1) * tq
            qr = q_ref[rows, :]
            dor = do_ref[rows, :]
            kk = k_ref[0:kmax, :]
            s = _dot_nt(qr, kk)
            p = jnp.where(_causal(tq, kmax, qi * tq), jnp.exp(s - lse_ref[rows, 0:1]), 0.0)
            dp = _dot_nt(dor, v_ref[0:kmax, :])
            delta = jnp.sum(dor.astype(F32) * o_ref[rows, :].astype(F32), axis=-1, keepdims=True)
            ds = (p * (dp - delta)).astype(BF16)
            dq_ref[rows, :] = (_dot(ds, kk) * ATTN_SCALE).astype(BF16)
            dk_acc[0:kmax, :] += _dot_tn(ds, qr)
            dv_acc[0:kmax, :] += _dot_tn(p.astype(BF16), dor)
        dk_ref[...] = dk_acc[...].astype(BF16)
        dv_ref[...] = dv_acc[...].astype(BF16)

    qspec = pl.BlockSpec((seq, HEAD_PAD), lambda b, h: (b, h))
    vspec = pl.BlockSpec((seq, V_HEAD), lambda b, h: (b, h))
    T = batch * seq
    return pl.pallas_call(
        body, grid=(batch, MLA_HEADS),
        in_specs=[qspec, qspec, vspec, vspec, vspec, vspec, ANY],
        out_specs=[qspec, qspec, vspec],
        out_shape=[jax.ShapeDtypeStruct((T, MLA_HEADS * HEAD_PAD), BF16),
                   jax.ShapeDtypeStruct((T, MLA_HEADS * HEAD_PAD), BF16),
                   jax.ShapeDtypeStruct((T, D_MODEL), BF16)],
        scratch_shapes=[pltpu.VMEM((seq, HEAD_PAD), F32), pltpu.VMEM((seq, V_HEAD), F32)],
        name="attn_bwd", compiler_params=_params("parallel", "parallel"))(q, k, v, o, do, lse, dep)


def _merge_out(x, yag, zm, o, w_out, ffn_g):
    T = x.shape[0]
    tm = _tile(T, 512)

    def body(x_ref, ya_ref, gb_ref, o_ref, w_ref, g_ref, mg_ref, x1_ref, h2_ref):
        mg = (ya_ref[...].astype(F32) + _sigmoid(gb_ref[...].astype(F32)) * o_ref[...].astype(F32)).astype(BF16)
        mg_ref[...] = mg
        x1 = x_ref[...] + _dot(mg, w_ref[...])
        x1_ref[...] = x1
        r = lax.rsqrt(jnp.mean(x1 * x1, axis=-1, keepdims=True) + EPS)
        h2_ref[...] = (x1 * r * g_ref[...]).astype(BF16)

    row = pl.BlockSpec((tm, D_MODEL), lambda i: (i, 0))
    return pl.pallas_call(
        body, grid=(T // tm,),
        in_specs=[row, row, pl.BlockSpec((tm, D_MODEL), lambda i: (i, 3)), row,
                  pl.BlockSpec((D_MODEL, D_MODEL), lambda i: (0, 0)), pl.BlockSpec((1, D_MODEL), lambda i: (0, 0))],
        out_specs=[row, row, row],
        out_shape=[jax.ShapeDtypeStruct((T, D_MODEL), BF16), jax.ShapeDtypeStruct((T, D_MODEL), F32),
                   jax.ShapeDtypeStruct((T, D_MODEL), BF16)],
        name="merge_out", compiler_params=_params("parallel"))(x, yag, zm, o, w_out, ffn_g)


FF_TILE = 256
FF_BLOCKS = D_FF // FF_TILE
FFB_TILE = 256
UP_ROWS = 512
EDGE = 16


def _shift_up(x, k):
    n = x.shape[0]
    row = lax.broadcasted_iota(jnp.int32, x.shape, 0)
    return jnp.where(row < n - k, pltpu.roll(x, n - k, 0), 0.0)


def _up_act(h2, wt_up, cw, cb, batch, seq):
    def body(h_ref, wug_ref, wuv_ref, wg_ref, wv_ref, bg_ref, bv_ref, ug_ref, uv_ref, g_ref, v_ref, a_ref,
             ug_s, uv_s):
        for s in (ug_s, uv_s):
            s[0:SUBLANES, :] = jnp.zeros((SUBLANES, FF_TILE), F32)

        def conv(s, w_ref, b_ref, r0):
            return (b_ref[...] + w_ref[2:3, :] * s[r0:r0 + UP_ROWS, :]
                    + w_ref[1:2, :] * s[r0 - 1:r0 - 1 + UP_ROWS, :]
                    + w_ref[0:1, :] * s[r0 - 2:r0 - 2 + UP_ROWS, :])

        for c in range(seq // UP_ROWS):
            rows = slice(c * UP_ROWS, (c + 1) * UP_ROWS)
            r0 = SUBLANES + c * UP_ROWS
            h = h_ref[rows, :]
            for w_ref, u_ref, s in ((wug_ref, ug_ref, ug_s), (wuv_ref, uv_ref, uv_s)):
                u = _dot_nt(h, w_ref[...])
                u_ref[rows, :] = u.astype(KEPT)
                s[r0:r0 + UP_ROWS, :] = u
            gate, val = conv(ug_s, wg_ref, bg_ref, r0), conv(uv_s, wv_ref, bv_ref, r0)
            g_ref[rows, :] = gate.astype(KEPT)
            v_ref[rows, :] = val.astype(KEPT)
            a_ref[rows, :] = (gate * _sigmoid(gate) * val).astype(BF16)

    blk = pl.BlockSpec((seq, FF_TILE), lambda b, j: (b, j))
    wup = lambda off: pl.BlockSpec((FF_TILE, D_MODEL), lambda b, j: (j + off, 0))
    wsp = lambda off: pl.BlockSpec((3, FF_TILE), lambda b, j: (0, j + off))
    bsp = lambda off: pl.BlockSpec((1, FF_TILE), lambda b, j: (0, j + off))
    T = batch * seq
    kept = jax.ShapeDtypeStruct((T, D_FF), KEPT)
    return pl.pallas_call(
        body, grid=(batch, FF_BLOCKS),
        in_specs=[pl.BlockSpec((seq, D_MODEL), lambda b, j: (b, 0)), wup(0), wup(FF_BLOCKS),
                  wsp(0), wsp(FF_BLOCKS), bsp(0), bsp(FF_BLOCKS)],
        out_specs=[blk] * 5,
        out_shape=[kept, kept, kept, kept, jax.ShapeDtypeStruct((T, D_FF), BF16)],
        scratch_shapes=[pltpu.VMEM((SUBLANES + seq, FF_TILE), F32)] * 2,
        name="up_act", compiler_params=_params("parallel", "arbitrary"))(h2, wt_up, wt_up, cw, cw, cb, cb)


def _ffn_act_bwd(upg, upv, gate, val, cw, dx2b, w_down, batch, seq):
    def half(du, x, w_ref, dx_ref, dw_ref):
        j = pl.program_id(1)
        n = du.shape[0]
        up1, up2 = pltpu.roll(du, n - 1, 0), pltpu.roll(du, n - 2, 0)
        dx_ref[...] = (w_ref[2:3, :] * du + w_ref[1:2, :] * up1 + w_ref[0:1, :] * up2).astype(BF16)
        tail = du[n - EDGE:n]
        dx_ref[n - EDGE:n, :] = (w_ref[2:3, :] * tail + w_ref[1:2, :] * _shift_up(tail, 1)
                                 + w_ref[0:1, :] * _shift_up(tail, 2)).astype(BF16)
        row = lax.broadcasted_iota(jnp.int32, (EDGE, du.shape[1]), 0)
        head, x_tail = du[0:EDGE], x[n - EDGE:n]
        wrap1 = jnp.sum(jnp.where(row >= EDGE - 1, pltpu.roll(head, EDGE - 1, 0), 0.0) * x_tail, axis=0, keepdims=True)
        wrap2 = jnp.sum(jnp.where(row >= EDGE - 2, pltpu.roll(head, EDGE - 2, 0), 0.0) * x_tail, axis=0, keepdims=True)
        dw_ref[j, 2:3, :] += jnp.sum(du * x, axis=0, keepdims=True)
        dw_ref[j, 1:2, :] += jnp.sum(up1 * x, axis=0, keepdims=True) - wrap1
        dw_ref[j, 0:1, :] += jnp.sum(up2 * x, axis=0, keepdims=True) - wrap2
        dw_ref[j, 3:4, :] += jnp.sum(du, axis=0, keepdims=True)

    def body(ug_ref, uv_ref, g_ref, v_ref, wg_ref, wv_ref, dx_ref, wd_ref, dg_ref, dv_ref, dwg_ref, dwv_ref):
        @pl.when((pl.program_id(0) == 0) & (pl.program_id(1) == 0))
        def _():
            dwg_ref[...] = jnp.zeros_like(dwg_ref)
            dwv_ref[...] = jnp.zeros_like(dwv_ref)

        gate, val = g_ref[...].astype(F32), v_ref[...].astype(F32)
        sg = _sigmoid(gate)
        dav = _dot_nt(dx_ref[...], wd_ref[...])
        half(dav * val * sg * (1.0 + gate * (1.0 - sg)), ug_ref[...].astype(F32), wg_ref, dg_ref, dwg_ref)
        half(dav * gate * sg, uv_ref[...].astype(F32), wv_ref, dv_ref, dwv_ref)

    nb = D_FF // FFB_TILE
    blk = pl.BlockSpec((seq, FFB_TILE), lambda b, j: (b, j))
    wsp = lambda off: pl.BlockSpec((3, FFB_TILE), lambda b, j: (0, j + off))
    acc = pl.BlockSpec((nb, 4, FFB_TILE), lambda b, j: (0, 0, 0))
    T = batch * seq
    dupg, dupv, dwg, dwv = pl.pallas_call(
        body, grid=(batch, nb),
        in_specs=[blk, blk, blk, blk, wsp(0), wsp(nb),
                  pl.BlockSpec((seq, D_MODEL), lambda b, j: (b, 0)),
                  pl.BlockSpec((FFB_TILE, D_MODEL), lambda b, j: (j, 0))],
        out_specs=[blk, blk, acc, acc],
        out_shape=[jax.ShapeDtypeStruct((T, D_FF), BF16), jax.ShapeDtypeStruct((T, D_FF), BF16),
                   jax.ShapeDtypeStruct((nb, 4, FFB_TILE), F32), jax.ShapeDtypeStruct((nb, 4, FFB_TILE), F32)],
        name="ffn_act_bwd", compiler_params=_params("arbitrary", "arbitrary"))(
            upg, upv, gate, val, cw, cw, dx2b, w_down)
    dwg, dwv = (jnp.transpose(a, (1, 0, 2)).reshape(4, D_FF) for a in (dwg, dwv))
    return dupg, dupv, dwg[:3], dwv[:3], dwg[3:], dwv[3:]


def _down_loss(a, w_down, x1, target, gfin):
    T = x1.shape[0]
    tm = _tile(T, 512)

    def body(a_ref, w_ref, x1_ref, t_ref, g_ref, dx_ref, dxb_ref, loss_ref, dg_ref):
        @pl.when(pl.program_id(0) == 0)
        def _():
            loss_ref[...] = jnp.zeros_like(loss_ref)
            dg_ref[...] = jnp.zeros_like(dg_ref)

        x2 = x1_ref[...] + _dot(a_ref[...], w_ref[...])
        r = lax.rsqrt(jnp.mean(x2 * x2, axis=-1, keepdims=True) + EPS)
        xh = x2 * r
        g = g_ref[...]
        diff = xh * g - t_ref[...]
        loss_ref[...] += 0.5 * jnp.sum(jnp.mean(diff * diff, axis=-1, keepdims=True))
        dy = diff * (1.0 / D_MODEL)
        dg_ref[...] += jnp.sum(dy * xh, axis=0, keepdims=True)
        dxh = dy * g
        dx = r * (dxh - xh * jnp.mean(dxh * xh, axis=-1, keepdims=True))
        dx_ref[...] = dx
        dxb_ref[...] = dx.astype(BF16)

    row = pl.BlockSpec((tm, D_MODEL), lambda i: (i, 0))
    vec = pl.BlockSpec((1, D_MODEL), lambda i: (0, 0))
    return pl.pallas_call(
        body, grid=(T // tm,),
        in_specs=[pl.BlockSpec((tm, D_FF), lambda i: (i, 0)),
                  pl.BlockSpec((D_FF, D_MODEL), lambda i: (0, 0)), row, row, vec],
        out_specs=[row, row, pl.BlockSpec((8, LANES), lambda i: (0, 0)), vec],
        out_shape=[jax.ShapeDtypeStruct((T, D_MODEL), F32), jax.ShapeDtypeStruct((T, D_MODEL), BF16),
                   jax.ShapeDtypeStruct((8, LANES), F32), jax.ShapeDtypeStruct((1, D_MODEL), F32)],
        name="down_loss", compiler_params=_params("arbitrary"))(a, w_down, x1, target, gfin)


def _local_step(x, positions, target, mix_norm, av_g, av_b, w_s, b_s, q_norm, kv_norm, ffn_norm, conv_b,
                final_norm, comm):
    batch, seq, _ = x.shape
    T = batch * seq
    x = x.reshape(T, D_MODEL)
    target = target.reshape(T, D_MODEL)
    pos = positions.reshape(T, 1)
    half = jnp.arange(0, QK_ROPE, 2, dtype=F32) / QK_ROPE
    inv_freq = 1.0 / (ROPE_THETA ** half)
    invf = jnp.concatenate([inv_freq, inv_freq, jnp.zeros((LANES - QK_ROPE,), F32)]).reshape(1, LANES)
    w_st = jnp.swapaxes(w_s, 1, 2)
    b_col = b_s.reshape(A_GROUPS, CHUNK, 1)

    wt_in = comm.in_weights()
    h, zm, zs = _in_proj(x, mix_norm, wt_in)
    yag = _mixer_a_fwd(zm, av_g, av_b, w_s, b_col)
    wuq_p, wukv, w_out = comm.mla_weights(after=yag)
    q, k, v = _mla_prep_fwd(zs, pos, invf, q_norm, kv_norm, wuq_p, wukv)
    o, lse = _attn_fwd(q, k, v, batch, seq)
    merged, x1, h2 = _merge_out(x, yag, zm, o, w_out, ffn_norm)
    wt_up, conv_w, w_down = comm.ffn_weights(after=merged)
    upg, upv, gate, val, act = _up_act(h2, wt_up, conv_w, conv_b, batch, seq)
    dx2, dx2b, loss_acc, d_final = _down_loss(act, w_down, x1, target, final_norm)

    d_wdown = _mm_tn(act, dx2b, "dw_down")
    dupg, dupv, dcwg, dcwv, dcbg, dcbv = _ffn_act_bwd(upg, upv, gate, val, conv_w, dx2b, w_down, batch, seq)
    d_wt_up = _mm_tn(dupv, h2, "dw_up_val", rows=2 * D_FF, row0=D_FF,
                     into=_mm_tn(dupg, h2, "dw_up_gate", rows=2 * D_FF))
    dx1, d_ffn_norm, dmerged = _proj_bwd(
        [dupg, dupv], wt_up, [(0, (0, D_FF), (0, D_FF)), (1, (0, D_FF), (D_FF, 2 * D_FF))],
        x1, ffn_norm, dx2, "up_proj_bwd", w2=w_out)
    d_wout = _mm_tn(merged, dx1, "dw_out")
    token = comm.send_ffn_grads(d_wdown, d_wt_up, jnp.concatenate([dcwg, dcwv], axis=1), d_wout)
    dzm, do, d_avg, d_avb, d_ws, d_bs = _mixer_bwd(zm, o, dmerged, av_g, av_b, w_s, w_st, b_col, token)
    zs_rows_zero = lax.empty((IN_DIM, D_MODEL), BF16).at[SPLIT_V:SPLIT_KR].set(0)
    d_wt_in = _mm_tn(dzm, h, "dw_in_main", rows=IN_DIM, gap=(SPLIT_V, ZS_ROWS), into=zs_rows_zero)
    token = comm.send_early_grads(d_wt_in, [
        d_avg, d_avb, _small_2d(d_ws).astype(BF16), d_bs.reshape(A_GROUPS, CHUNK), d_ffn_norm,
        jnp.concatenate([dcbg, dcbv], axis=1), d_final])
    dq, dk, dv = _attn_bwd(q, k, v, o, do, lse, batch, seq, token)
    dzs, d_wuq_p, d_wukv, d_qn, d_kvn = _mla_prep_bwd(zs, pos, invf, q_norm, kv_norm, wuq_p, wukv, dq, dk, dv)
    d_wt_zs = _mm_tn(dzs, h, "dw_in_small")
    token = comm.send_mla_grads(d_wuq_p, d_wukv, d_wt_zs)
    terms = [(0, (i * D_MODEL, (i + 1) * D_MODEL), rows) for i, rows in enumerate(IN_ROWS_MAIN)]
    terms.append((1, (0, ZS_W), IN_ROWS_ZS))
    dx, d_mix_norm = _proj_bwd([dzm, dzs], wt_in, terms, x, mix_norm, dx1, "in_proj_bwd", dep=token, rows=512)
    token = comm.send_late_grads([d_qn, d_kvn, d_mix_norm, loss_acc])
    return dx.reshape(batch, seq, D_MODEL), token


MESH_ID = pl.DeviceIdType.MESH
EFFECT = pltpu.SideEffectType.DATAFLOW_SIDE_EFFECTING


def _mesh_pos():
    return lax.axis_index("x"), lax.axis_index("y"), lax.axis_index("c")


def _peer(pos, d):
    x, y, c = pos
    px = 1 - x if d & 4 else x
    py = 1 - y if d & 2 else y
    pc = 1 - c if d & 1 else c
    return (px, py, pc), 4 * px + 2 * py + pc


def _copy(src_ref, land_ref, send_sems, recv_sems, a, d, pos, exchange, landing_here):
    peer, pid = _peer(pos, d)
    me = 4 * pos[0] + 2 * pos[1] + pos[2]
    if exchange:
        src, dst = src_ref.at[pid], land_ref.at[d]
    else:
        src, dst = src_ref, land_ref.at[pid if landing_here else me]
    return pltpu.make_async_remote_copy(
        src_ref=src, dst_ref=dst, send_sem=send_sems.at[a * (N_DEV - 1) + d - 1],
        recv_sem=recv_sems.at[a * (N_DEV - 1) + d - 1],
        device_id=peer, device_id_type=MESH_ID)


def _start_copies(groups, modes, name, dep=None):
    sizes = [len(g) for g in groups]
    srcs = [s for g in groups for s in g]
    lands = [lax.empty(s.shape if modes[gi] else (N_DEV,) + s.shape, s.dtype)
             for gi, g in enumerate(groups) for s in g]
    n, ng = len(srcs), len(groups)
    n_in = 2 * n + (dep is not None)

    def body(*refs):
        src_refs, land_refs = refs[:n], refs[n:2 * n]
        sems = refs[n_in:n_in + 3 * ng]
        token = refs[-1]
        pos = _mesh_pos()
        k = 0
        for gi, size in enumerate(sizes):
            for a in range(size):
                _own_copy(src_refs[k], land_refs[k], sems[3 * gi + 2], a, pos, modes[gi]).start()
                for d in range(1, N_DEV):
                    _copy(src_refs[k], land_refs[k], sems[3 * gi], sems[3 * gi + 1], a, d, pos, modes[gi],
                          landing_here=False).start()
                k += 1
        token[...] = jnp.zeros_like(token)

    sem_shapes = []
    for size in sizes:
        remote = pltpu.SemaphoreType.DMA((size * (N_DEV - 1),))
        sem_shapes += [remote, remote, pltpu.SemaphoreType.DMA((size,))]
    out = pl.pallas_call(
        body, name=name,
        out_shape=(*sem_shapes, *[pltpu.HBM(a.shape, a.dtype) for a in srcs + lands],
                   jax.ShapeDtypeStruct((8, LANES), F32)),
        in_specs=[HBM] * (2 * n) + [ANY] * (dep is not None),
        out_specs=(*[SEM] * (3 * ng), *[HBM] * (2 * n), pl.BlockSpec(memory_space=pltpu.VMEM)),
        input_output_aliases={i: 3 * ng + i for i in range(2 * n)},
        compiler_params=pltpu.CompilerParams(has_side_effects=EFFECT),
    )(*[pltpu.with_memory_space_constraint(a, pltpu.HBM) for a in srcs + lands], *([dep] if dep is not None else []))
    thru = out[3 * ng:3 * ng + 2 * n]
    handles, k = [], 0
    for gi, size in enumerate(sizes):
        handles.append((out[3 * gi:3 * gi + 3], thru[k:k + size], thru[n + k:n + k + size]))
        k += size
    return handles, out[-1]


def _own_copy(src_ref, land_ref, local_sems, a, pos, exchange):
    me = 4 * pos[0] + 2 * pos[1] + pos[2]
    src, dst = (src_ref.at[me], land_ref.at[0]) if exchange else (src_ref, land_ref.at[me])
    return pltpu.make_async_copy(src, dst, local_sems.at[a])


def _wait_copies(handle, exchange, after, name):
    sems, srcs, lands = handle
    n = len(srcs)

    def body(*refs):
        src_refs, land_refs = refs[:n], refs[n:2 * n]
        send, recv, local = refs[2 * n:2 * n + 3]
        pos = _mesh_pos()
        for a in range(n):
            _own_copy(src_refs[a], land_refs[a], local, a, pos, exchange).wait()
            for d in range(1, N_DEV):
                cp = _copy(src_refs[a], land_refs[a], send, recv, a, d, pos, exchange, landing_here=True)
                cp.wait_send()
                cp.wait_recv()

    out = pl.pallas_call(
        body, name=name,
        out_shape=tuple(pltpu.HBM(a.shape, a.dtype) for a in (*srcs, *lands)),
        in_specs=[HBM] * (2 * n) + [SEM, SEM, SEM, ANY], out_specs=[HBM] * (2 * n),
        input_output_aliases={i: i for i in range(2 * n)},
        compiler_params=pltpu.CompilerParams(has_side_effects=EFFECT),
    )(*srcs, *lands, *sems, after)
    return out[n:]


def _gather_now(a, name):
    def body(x_ref, out_ref, send_sems, recv_sems, local_sem):
        x, y, c = _mesh_pos()
        me, sibling = (x, y, c), (x, y, 1 - c)
        chips = [(1 - x, y), (x, 1 - y), (1 - x, 1 - y)]

        def slot(p):
            return out_ref.at[4 * p[0] + 2 * p[1] + p[2]]

        def copy(k, block, to, src=None):
            return pltpu.make_async_remote_copy(
                src_ref=slot(block) if src is None else src, dst_ref=slot(block), send_sem=send_sems.at[k],
                recv_sem=recv_sems.at[k], device_id=to, device_id_type=MESH_ID)

        mine = pltpu.make_async_copy(x_ref, slot(me), local_sem)
        mine.start()
        first = [copy(0, me, sibling, src=x_ref)]
        first += [copy(1 + j, me, (*chip, c), src=x_ref) for j, chip in enumerate(chips)]
        for cp in first:
            cp.start()
        passed = [copy(4 + j, (*chip, c), sibling) for j, chip in enumerate(chips)]
        for j, chip in enumerate(chips):
            copy(1 + j, (*chip, c), me).wait_recv()
            passed[j].start()
        copy(0, sibling, me).wait_recv()
        for j, chip in enumerate(chips):
            copy(4 + j, (*chip, 1 - c), me).wait_recv()
        for cp in first + passed:
            cp.wait_send()
        mine.wait()

    return pl.pallas_call(
        body, in_specs=[ANY], out_specs=ANY,
        out_shape=jax.ShapeDtypeStruct((N_DEV,) + a.shape, a.dtype),
        scratch_shapes=[pltpu.SemaphoreType.DMA((N_DEV - 1,)), pltpu.SemaphoreType.DMA((N_DEV - 1,)),
                        pltpu.SemaphoreType.DMA],
        name=name, compiler_params=pltpu.CompilerParams(has_side_effects=True))(a)


def _sum_parts(p_ref):
    g = p_ref[0].astype(F32)
    for k in range(1, N_DEV):
        g = g + p_ref[k].astype(F32)
    return g


def _adamw_update(p_ref, w_ref, m_ref, v_ref, g_ref, d_ref, nm_ref, nv_ref, patch=None):
    c1 = 1.0 - ADAM_B1 ** ADAM_STEP
    c2 = 1.0 - ADAM_B2 ** ADAM_STEP
    g = _sum_parts(p_ref)
    if patch is not None:
        g = patch(g)
    nm = ADAM_B1 * m_ref[...] + (1.0 - ADAM_B1) * g
    nv = ADAM_B2 * v_ref[...] + (1.0 - ADAM_B2) * (g * g)
    g_ref[...] = g
    nm_ref[...] = nm
    nv_ref[...] = nv
    d_ref[...] = -ADAM_LR * ((nm / c1) / (jnp.sqrt(nv / c2) + ADAM_EPS) + ADAM_WD * w_ref[...])


def _adamw_many(parts, ws, ms, vs, sums, name):
    n, ns = len(ws), len(sums)

    def body(*refs):
        ins, outs = refs[:4 * n + ns], refs[4 * n + ns:]
        for i in range(n):
            _adamw_update(ins[i], ins[n + i], ins[2 * n + i], ins[3 * n + i],
                          outs[i], outs[n + i], outs[2 * n + i], outs[3 * n + i])
        for i in range(ns):
            outs[4 * n + i][...] = _sum_parts(ins[4 * n + i])

    full = lambda a: pl.BlockSpec(a.shape, lambda: (0,) * a.ndim)
    args = [*parts, *ws, *ms, *vs, *sums]
    outs = [jax.ShapeDtypeStruct(w.shape, F32) for _ in range(4) for w in ws]
    outs += [jax.ShapeDtypeStruct(s.shape[1:], F32) for s in sums]
    res = pl.pallas_call(
        body, in_specs=[full(a) for a in args], out_specs=[full(o) for o in outs], out_shape=outs,
        name=name, compiler_params=pltpu.CompilerParams(vmem_limit_bytes=VMEM_LIMIT))(*args)
    return res[:n], res[n:2 * n], res[2 * n:3 * n], res[3 * n:4 * n], res[4 * n:]


def _adamw(parts, w, m, v, name, late=None):
    R, C = w.shape
    tr, tc = R, C
    if N_DEV * R * C * parts.dtype.itemsize > SMALL_BLOCK_BYTES:
        tr = next((t for t in range(min(R, 256) // 16 * 16, 15, -16) if R % t == 0), R)
        if tr == R:
            tc = _tile(C, 256)
    more, places = late if late is not None else (None, ())
    assert late is None or tr == R

    def body(p_ref, w_ref, m_ref, v_ref, *refs):
        def patch(g):
            more_ref, g_s = refs[0], refs[-1]
            x, y, c = _mesh_pos()
            me = 4 * x + 2 * y + c
            rows = _sum_parts(more_ref)
            g_s[...] = g
            for dev, row in places:
                g_s[row:row + rows.shape[0], :] += jnp.where(me == dev, rows, 0.0)
            return g_s[...]

        outs = refs[:4] if late is None else refs[1:5]
        _adamw_update(p_ref, w_ref, m_ref, v_ref, *outs, patch=None if late is None else patch)

    blk = pl.BlockSpec((tr, tc), lambda i, j: (i, j))
    shp = jax.ShapeDtypeStruct((R, C), F32)
    in_specs = [pl.BlockSpec((N_DEV, tr, tc), lambda i, j: (0, i, j)), blk, blk, blk]
    if late is not None:
        in_specs.append(pl.BlockSpec((N_DEV, more.shape[1], tc), lambda i, j: (0, 0, j)))
    return pl.pallas_call(
        body, grid=(R // tr, C // tc), in_specs=in_specs,
        out_specs=[blk, blk, blk, blk], out_shape=[shp, shp, shp, shp],
        scratch_shapes=[] if late is None else [pltpu.VMEM((tr, tc), F32)],
        name=name, compiler_params=_params("parallel", "parallel"))(parts, w, m, v, *([] if late is None else [more]))


SPLIT_V = 2 * D_MODEL
SPLIT_KR = SPLIT_V + Q_LORA + KV_LORA + QK_ROPE
IN_DIM = SPLIT_KR + 2 * D_MODEL
IN_ROWS_MAIN = ((0, D_MODEL), (D_MODEL, SPLIT_V), (SPLIT_KR, SPLIT_KR + D_MODEL), (SPLIT_KR + D_MODEL, IN_DIM))
IN_ROWS_ZS = (SPLIT_V, SPLIT_V + ZS_W)
ZS_ROWS = SPLIT_KR - SPLIT_V
IN_SHARD = IN_DIM // N_DEV
ZS_PIECES = tuple(
    (k, max(IN_SHARD * k, SPLIT_V) - SPLIT_V, max(IN_SHARD * k, SPLIT_V) - IN_SHARD * k)
    for k in range(N_DEV) if max(IN_SHARD * k, SPLIT_V) < min(IN_SHARD * (k + 1), SPLIT_KR))
ZS_PIECE_ROWS = ZS_ROWS // len(ZS_PIECES)
assert all(min(IN_SHARD * (k + 1), SPLIT_KR) - max(IN_SHARD * k, SPLIT_V) == ZS_PIECE_ROWS for k, _, _ in ZS_PIECES)

SMALL_EARLY = ("a_v_norm_g", "a_v_norm_b", "a_spatial_w", "a_spatial_b", "ffn_norm", "conv_b", "final_norm")
SMALL_LATE = ("q_a_norm", "kv_a_norm", "mix_norm")


def _small_2d(a):
    return a.reshape(-1, a.shape[-1])


def _cols_from_shards(g):
    return jnp.transpose(g, (1, 0, 2)).reshape(g.shape[1], N_DEV * g.shape[2])


def _shards_from_cols(a):
    R, W = a.shape
    return jnp.transpose(a.reshape(R, N_DEV, W // N_DEV), (1, 0, 2))


class _Comm:
    GATHER_GROUPS = (("w_uq", "w_ukv", "w_out"), ("w_up", "conv_w", "w_down"))
    FFN_GRADS = ("w_down", "w_up", "conv_w", "w_out")
    TRANSPOSED = ("w_in", "w_up", "w_uq")

    def __init__(self, shards):
        local = {n: a.astype(F32 if n == "conv_w" else BF16) for n, a in shards.items()}
        self.g_in = _gather_now(local["w_in"], "gather_w_in")
        groups = [[local[n] for n in g] for g in self.GATHER_GROUPS]
        (self.h_mla, self.h_ffn), _ = _start_copies(groups, [False] * 2, "gather_start", dep=self.g_in)

    def in_weights(self):
        return self.g_in.reshape(IN_DIM, D_MODEL)

    def mla_weights(self, after):
        g_uq, g_ukv, g_out = _wait_copies(self.h_mla, False, after, "gather_wait_mla")
        wuq_p = jnp.pad(g_uq, ((0, 0), (0, HEAD_PAD - QK_HEAD), (0, 0))).reshape(MLA_HEADS * HEAD_PAD, Q_LORA)
        return wuq_p, g_ukv, g_out.reshape(D_MODEL, D_MODEL)

    def ffn_weights(self, after):
        g_up, g_cw, g_down = _wait_copies(self.h_ffn, False, after, "gather_wait_ffn")
        return g_up.reshape(2 * D_FF, D_MODEL), _cols_from_shards(g_cw), g_down.reshape(D_FF, D_MODEL)

    def send_ffn_grads(self, d_wdown, d_wt_up, d_convw, d_wout):
        group = [d_wdown.reshape(N_DEV, D_FF // N_DEV, D_MODEL), d_wt_up.reshape(N_DEV, 2 * D_FF // N_DEV, D_MODEL),
                 _shards_from_cols(d_convw), d_wout.reshape(N_DEV, D_MODEL // N_DEV, D_MODEL)]
        (self.h_ffn_grads,), token = _start_copies([group], [True], "ffn_grads_start")
        return token

    def send_early_grads(self, d_wt_in, grads):
        blocks = d_wt_in.reshape(N_DEV, IN_SHARD, D_MODEL)
        (self.h_small_early, self.h_in_grads), token = _start_copies(
            [grads, [blocks]], [False, True], "early_grads_start")
        return token

    def send_mla_grads(self, d_wuq_p, d_wukv, d_wt_zs):
        d_uq = d_wuq_p.reshape(MLA_HEADS, HEAD_PAD, Q_LORA)[:, :QK_HEAD, :]
        zs_blocks = jnp.zeros((N_DEV, ZS_PIECE_ROWS, D_MODEL), d_wt_zs.dtype)
        for dev, first, _ in ZS_PIECES:
            zs_blocks = zs_blocks.at[dev].set(d_wt_zs[first:first + ZS_PIECE_ROWS])
        (self.h_mla_grads,), token = _start_copies(
            [[d_uq, d_wukv, zs_blocks]], [True], "mla_grads_start")
        return token

    def send_late_grads(self, small):
        (self.h_late_small,), token = _start_copies([small], [False], "late_grads_start")
        return token


def kernel(x, positions, mix_norm, w_in, a_v_norm_g, a_v_norm_b, a_spatial_w, a_spatial_b, q_a_norm, w_uq, kv_a_norm, w_ukv, w_out, ffn_norm, w_up, conv_w, conv_b, w_down, final_norm, loss_target, m_mix_norm, m_w_in, m_a_v_norm_g, m_a_v_norm_b, m_a_spatial_w, m_a_spatial_b, m_q_a_norm, m_w_uq, m_kv_a_norm, m_w_ukv, m_w_out, m_ffn_norm, m_w_up, m_conv_w, m_conv_b, m_w_down, m_final_norm, v_mix_norm, v_w_in, v_a_v_norm_g, v_a_v_norm_b, v_a_spatial_w, v_a_spatial_b, v_q_a_norm, v_w_uq, v_kv_a_norm, v_w_ukv, v_w_out, v_ffn_norm, v_w_up, v_conv_w, v_conv_b, v_w_down, v_final_norm):
    names = ("mix_norm", "w_in", "a_v_norm_g", "a_v_norm_b", "a_spatial_w", "a_spatial_b", "q_a_norm", "w_uq",
             "kv_a_norm", "w_ukv", "w_out", "ffn_norm", "w_up", "conv_w", "conv_b", "w_down", "final_norm")
    w = dict(zip(names, (mix_norm, w_in, a_v_norm_g, a_v_norm_b, a_spatial_w, a_spatial_b, q_a_norm, w_uq,
                         kv_a_norm, w_ukv, w_out, ffn_norm, w_up, conv_w, conv_b, w_down, final_norm)))
    m = dict(zip(names, (m_mix_norm, m_w_in, m_a_v_norm_g, m_a_v_norm_b, m_a_spatial_w, m_a_spatial_b,
                         m_q_a_norm, m_w_uq, m_kv_a_norm, m_w_ukv, m_w_out, m_ffn_norm, m_w_up, m_conv_w,
                         m_conv_b, m_w_down, m_final_norm)))
    v = dict(zip(names, (v_mix_norm, v_w_in, v_a_v_norm_g, v_a_v_norm_b, v_a_spatial_w, v_a_spatial_b,
                         v_q_a_norm, v_w_uq, v_kv_a_norm, v_w_ukv, v_w_out, v_ffn_norm, v_w_up, v_conv_w,
                         v_conv_b, v_w_down, v_final_norm)))
    shapes = {n: w[n].shape for n in names}
    def view(tree, n):
        a = tree[n].reshape(tree[n].shape[-2:])
        return a.T if n in _Comm.TRANSPOSED else a

    comm = _Comm({n: view(w, n) for n in ("w_in",) + _Comm.GATHER_GROUPS[0] + _Comm.GATHER_GROUPS[1]})

    grad_x, token = _local_step(
        x, positions, loss_target, w["mix_norm"], w["a_v_norm_g"], w["a_v_norm_b"], w["a_spatial_w"][0],
        w["a_spatial_b"][0], w["q_a_norm"], w["kv_a_norm"], w["ffn_norm"], w["conv_b"],
        w["final_norm"].reshape(1, D_MODEL), comm)

    out_g, out_d, out_m, out_v = {}, {}, {}, {}

    def update(n, parts, late=None):
        res = _adamw(parts, view(w, n), view(m, n), view(v, n), "adamw_" + n, late=late)
        out_g[n], out_d[n], out_m[n], out_v[n] = (
            (t.T if n in _Comm.TRANSPOSED else t).reshape(shapes[n]) for t in res)
        return res[1]

    def update_small(names, parts, sums, name):
        res = _adamw_many(parts, *[[_small_2d(t[n]) for n in names] for t in (w, m, v)], sums, name)
        for i, n in enumerate(names):
            out_g[n], out_d[n], out_m[n], out_v[n] = (r[i].reshape(shapes[n]) for r in res[:4])
        return res

    for n, parts in zip(_Comm.FFN_GRADS, _wait_copies(comm.h_ffn_grads, True, token, "ffn_grads_wait")):
        last = update(n, parts)
    early = _wait_copies(comm.h_small_early, False, last, "small_grads_wait")
    last = update_small(SMALL_EARLY, early, [], "adamw_small")[1][0]
    (in_parts,) = _wait_copies(comm.h_in_grads, True, last, "in_grads_wait")
    uq_parts, ukv_parts, zs_parts = _wait_copies(comm.h_mla_grads, True, in_parts, "mla_grads_wait")
    last = update("w_in", in_parts, late=(zs_parts, [(dev, row) for dev, _, row in ZS_PIECES]))
    last = update("w_uq", uq_parts)
    last = update("w_ukv", ukv_parts)
    late = _wait_copies(comm.h_late_small, False, last, "late_small_wait")
    res = update_small(SMALL_LATE, late[:-1], late[-1:], "adamw_late")
    loss = res[4][0][0, 0]

    return (loss, grad_x, *[out_g[n] for n in names], *[out_d[n] for n in names],
            *[out_m[n] for n in names], *[out_v[n] for n in names])
```

```python
import math

import jax
import jax.numpy as jnp
from jax import lax
from jax.experimental import pallas as pl
from jax.experimental.pallas import tpu as pltpu

F32 = jnp.float32
BF16 = jnp.bfloat16
KEPT = jnp.bfloat16

N_DEV = 8
D_MODEL = 1024
EPS = 1e-6
A_GROUPS = 8
CHUNK = 128
MLA_HEADS = 8
QK_NOPE = 128
QK_ROPE = 64
QK_HEAD = QK_NOPE + QK_ROPE
HEAD_PAD = 256
V_HEAD = 128
Q_LORA = 256
KV_LORA = 128
ROPE_THETA = 10000.0
D_FF = 2816
ZS_W = 512
ATTN_SCALE = QK_HEAD ** -0.5
ATTN_TILE = 512
NEG_BIG = -1e30

ADAM_LR = 0.001
ADAM_B1 = 0.9
ADAM_B2 = 0.999
ADAM_EPS = 1e-08
ADAM_WD = 0.01
ADAM_STEP = 10

VMEM_LIMIT = 56 * 1024 * 1024
SMALL_BLOCK_BYTES = 5 * 1024 * 1024
LANES = 128
SUBLANES = 8

GELU_K = math.sqrt(2.0 / math.pi)
GELU_C = 0.044715

ANY = pl.BlockSpec(memory_space=pl.ANY)
HBM = pl.BlockSpec(memory_space=pltpu.HBM)
SEM = pl.BlockSpec(memory_space=pltpu.SEMAPHORE)


def _tile(n, pref):
    for t in (pref, 512, 256, 128, 64, 32, 16, 8):
        if t <= pref and n % t == 0:
            return t
    return n


def _wide_tile(n, cap=1408):
    return next((t for t in range(min(n, cap) // LANES * LANES, 0, -LANES) if n % t == 0), n)


def _params(*sem):
    return pltpu.CompilerParams(dimension_semantics=sem, vmem_limit_bytes=VMEM_LIMIT)


def _dot(a, b):
    return jnp.dot(a, b, preferred_element_type=F32)


def _dot_nt(a, b):
    return lax.dot_general(a, b, (((1,), (1,)), ((), ())), preferred_element_type=F32)


def _dot_tn(a, b):
    return lax.dot_general(a, b, (((0,), (0,)), ((), ())), preferred_element_type=F32)


def _sigmoid(x):
    return 1.0 / (1.0 + jnp.exp(-x))


def _gelu(x):
    t = jnp.tanh(GELU_K * (x + GELU_C * x * x * x))
    return 0.5 * x * (1.0 + t)


def _gelu_and_grad(x):
    x2 = x * x
    t = jnp.tanh(GELU_K * (x + GELU_C * x * x2))
    half = 0.5 * (1.0 + t)
    return x * half, half + 0.5 * x * (1.0 - t * t) * GELU_K * (1.0 + 3.0 * GELU_C * x2)


def _in_proj(x, g, wt):
    T, Dm = x.shape
    tm = _tile(T, 512)

    def body(x_ref, g_ref, wt_ref, h_ref, zm_ref, zs_ref):
        xf = x_ref[...]
        r = lax.rsqrt(jnp.mean(xf * xf, axis=-1, keepdims=True) + EPS)
        h = (xf * r * g_ref[...]).astype(BF16)
        h_ref[...] = h
        for i, (r0, r1) in enumerate(IN_ROWS_MAIN):
            zm_ref[:, i * D_MODEL:(i + 1) * D_MODEL] = _dot_nt(h, wt_ref[r0:r1, :]).astype(KEPT)
        zs_ref[...] = _dot_nt(h, wt_ref[IN_ROWS_ZS[0]:IN_ROWS_ZS[1], :])

    row = lambda n: pl.BlockSpec((tm, n), lambda i: (i, 0))
    return pl.pallas_call(
        body, grid=(T // tm,),
        in_specs=[row(Dm), pl.BlockSpec((1, Dm), lambda i: (0, 0)), pl.BlockSpec(wt.shape, lambda i: (0, 0))],
        out_specs=[row(Dm), row(4 * D_MODEL), row(ZS_W)],
        out_shape=[jax.ShapeDtypeStruct((T, Dm), BF16), jax.ShapeDtypeStruct((T, 4 * D_MODEL), KEPT),
                   jax.ShapeDtypeStruct((T, ZS_W), F32)],
        name="in_proj", compiler_params=_params("parallel"))(x, g, wt)


def _proj_bwd(acts, wt, terms, x, g, dres, name, w2=None, dep=None, rows=256):
    T, Dm = x.shape
    tm = _tile(T, rows)
    n_a = len(acts)

    def body(*refs):
        ins, outs = refs[:n_a + 4 + (w2 is not None) + (dep is not None)], refs[-2 - (w2 is not None):]
        wt_ref, x_ref, g_ref, dres_ref = ins[n_a:n_a + 4]
        dx_ref, dg_ref = outs[0], outs[1]

        @pl.when(pl.program_id(0) == 0)
        def _():
            dg_ref[...] = jnp.zeros_like(dg_ref)

        dy = None
        for i, (c0, c1), (r0, r1) in terms:
            t = _dot(ins[i][:, c0:c1], wt_ref[r0:r1, :])
            dy = t if dy is None else dy + t
        xf = x_ref[...]
        r = lax.rsqrt(jnp.mean(xf * xf, axis=-1, keepdims=True) + EPS)
        xh = xf * r
        dg_ref[...] += jnp.sum(dy * xh, axis=0, keepdims=True)
        dxh = dy * g_ref[...]
        dx = dres_ref[...] + r * (dxh - xh * jnp.mean(dxh * xh, axis=-1, keepdims=True))
        dx_ref[...] = dx
        if w2 is not None:
            outs[2][...] = _dot_nt(dx.astype(BF16), ins[n_a + 4][...]).astype(KEPT)

    row = pl.BlockSpec((tm, Dm), lambda i: (i, 0))
    vec = pl.BlockSpec((1, Dm), lambda i: (0, 0))
    in_specs = [pl.BlockSpec((tm, a.shape[1]), lambda i: (i, 0)) for a in acts]
    in_specs += [pl.BlockSpec(wt.shape, lambda i: (0, 0)), row, vec, row]
    args = [*acts, wt, x, g, dres]
    out_specs = [row, vec]
    out_shape = [jax.ShapeDtypeStruct((T, Dm), F32), jax.ShapeDtypeStruct((1, Dm), F32)]
    if w2 is not None:
        in_specs.append(pl.BlockSpec(w2.shape, lambda i: (0, 0)))
        args.append(w2)
        out_specs.append(pl.BlockSpec((tm, w2.shape[0]), lambda i: (i, 0)))
        out_shape.append(jax.ShapeDtypeStruct((T, w2.shape[0]), KEPT))
    if dep is not None:
        in_specs.append(ANY)
        args.append(dep)
    return pl.pallas_call(
        body, grid=(T // tm,), in_specs=in_specs, out_specs=out_specs, out_shape=out_shape,
        name=name, compiler_params=_params("arbitrary"))(*args)


def _mm_tn(a, b, name, dep=None, rows=None, row0=0, into=None, gap=None):
    T, M = a.shape
    N = b.shape[1]
    tm, tn, tt = _wide_tile(M), _wide_tile(N), _tile(T, 2048)
    n_t = T // tt
    off = row0 // tm
    extra = ([dep] if dep is not None else []) + ([into] if into is not None else [])
    if gap is None:
        out_spec = pl.BlockSpec((tm, tn), lambda i, j, t: (i + off, j))
    else:
        unit = 16
        assert row0 == 0 and gap[0] % tm == 0 and tm % unit == 0 and gap[1] % unit == 0
        out_spec = pl.BlockSpec(
            (pl.Element(tm), pl.Element(tn)),
            lambda i, j, t: ((i * (tm // unit) + jnp.where(i * tm >= gap[0], gap[1] // unit, 0)) * unit, j * tn))

    def body(a_ref, b_ref, *refs):
        o_ref, acc_ref = refs[-2:]
        t = pl.program_id(2)

        @pl.when(t == 0)
        def _():
            acc_ref[...] = jnp.zeros_like(acc_ref)

        acc_ref[...] += _dot_tn(a_ref[...].astype(BF16), b_ref[...].astype(BF16))

        @pl.when(t == n_t - 1)
        def _():
            o_ref[...] = acc_ref[...].astype(BF16)

    return pl.pallas_call(
        body, grid=(M // tm, N // tn, n_t),
        in_specs=[pl.BlockSpec((tt, tm), lambda i, j, t: (t, i)),
                  pl.BlockSpec((tt, tn), lambda i, j, t: (t, j))] + [ANY] * len(extra),
        out_specs=out_spec,
        out_shape=jax.ShapeDtypeStruct((rows or M, N), BF16),
        scratch_shapes=[pltpu.VMEM((tm, tn), F32)],
        input_output_aliases={} if into is None else {1 + len(extra): 0},
        name=name, compiler_params=_params("parallel", "parallel", "arbitrary"))(a, b, *extra)


def _layer_norm_fwd(gv, g, b):
    mu = jnp.mean(gv, axis=-1, keepdims=True)
    xc = gv - mu
    rs = lax.rsqrt(jnp.mean(xc * xc, axis=-1, keepdims=True) + EPS)
    xh = xc * rs
    return xh, rs, xh * g + b


def _tri_mask(transposed=False):
    r = lax.broadcasted_iota(jnp.int32, (CHUNK, CHUNK), 0)
    c = lax.broadcasted_iota(jnp.int32, (CHUNK, CHUNK), 1)
    return r <= c if transposed else c <= r


def _mixer_a_fwd(zm, av_g, av_b, w_s, b_col):
    T = zm.shape[0]
    tm = _tile(T, 512)
    n_chunk = tm // CHUNK

    def body(u_ref, v_ref, ga_ref, g_ref, b_ref, w_ref, bc_ref, y_ref, vn_s, mx_s):
        gu = _gelu(u_ref[...].astype(F32))
        _, _, vn = _layer_norm_fwd(_gelu(v_ref[...].astype(F32)), g_ref[...], b_ref[...])
        vn_s[...] = vn.astype(BF16)
        tri = _tri_mask()
        for gi in range(A_GROUPS):
            wm = jnp.where(tri, w_ref[gi], 0.0).astype(BF16)
            cols = slice(gi * CHUNK, (gi + 1) * CHUNK)
            for n in range(n_chunk):
                rows = slice(n * CHUNK, (n + 1) * CHUNK)
                mx_s[rows, cols] = _dot(wm, vn_s[rows, cols]) + bc_ref[gi]
        y_ref[...] = (_sigmoid(ga_ref[...].astype(F32)) * gu * mx_s[...]).astype(KEPT)

    col = lambda c: pl.BlockSpec((tm, D_MODEL), lambda i: (i, c))
    vec = pl.BlockSpec((1, D_MODEL), lambda i: (0, 0))
    return pl.pallas_call(
        body, grid=(T // tm,),
        in_specs=[col(0), col(1), col(2), vec, vec,
                  pl.BlockSpec((A_GROUPS, CHUNK, CHUNK), lambda i: (0, 0, 0)),
                  pl.BlockSpec((A_GROUPS, CHUNK, 1), lambda i: (0, 0, 0))],
        out_specs=pl.BlockSpec((tm, D_MODEL), lambda i: (i, 0)),
        out_shape=jax.ShapeDtypeStruct((T, D_MODEL), KEPT),
        scratch_shapes=[pltpu.VMEM((tm, D_MODEL), BF16), pltpu.VMEM((tm, D_MODEL), F32)],
        name="mixer_a_fwd", compiler_params=_params("parallel"))(zm, zm, zm, av_g, av_b, w_s, b_col)


def _mixer_bwd(zm, o, dm, av_g, av_b, w_s, w_st, b_col, dep):
    T = zm.shape[0]
    tm = _tile(T, 256)
    n_chunk = tm // CHUNK

    def body(u_ref, v_ref, ga_ref, gb_ref, o_ref, dm_ref, g_ref, b_ref, w_ref, wt_ref, bc_ref, dep_ref,
             dz_ref, do_ref, dg_ref, db_ref, dw_ref, dbs_ref, vn_s, mx_s, dmx_s, dvn_s):
        @pl.when(pl.program_id(0) == 0)
        def _():
            dg_ref[...] = jnp.zeros_like(dg_ref)
            db_ref[...] = jnp.zeros_like(db_ref)
            dw_ref[...] = jnp.zeros_like(dw_ref)
            dbs_ref[...] = jnp.zeros_like(dbs_ref)

        dm_v = dm_ref[...].astype(F32)
        gb = gb_ref[...].astype(F32)
        sb = _sigmoid(gb)
        o_v = o_ref[...].astype(F32)
        do_ref[...] = (dm_v * sb).astype(BF16)
        dz_ref[:, 3 * D_MODEL:4 * D_MODEL] = (dm_v * o_v * sb * (1.0 - sb)).astype(BF16)
        u = u_ref[...].astype(F32)
        v = v_ref[...].astype(F32)
        gu, gu_grad = _gelu_and_grad(u)
        gv, gv_grad = _gelu_and_grad(v)
        xh, rs, vn = _layer_norm_fwd(gv, g_ref[...], b_ref[...])
        vn_s[...] = vn.astype(BF16)
        tri = _tri_mask()
        for gi in range(A_GROUPS):
            wm = jnp.where(tri, w_ref[gi], 0.0).astype(BF16)
            cols = slice(gi * CHUNK, (gi + 1) * CHUNK)
            for n in range(n_chunk):
                rows = slice(n * CHUNK, (n + 1) * CHUNK)
                mx_s[rows, cols] = _dot(wm, vn_s[rows, cols]) + bc_ref[gi]
        mixed = mx_s[...]
        sa = _sigmoid(ga_ref[...].astype(F32))
        dya = dm_v * sa
        dz_ref[:, 2 * D_MODEL:3 * D_MODEL] = (dm_v * gu * mixed * sa * (1.0 - sa)).astype(BF16)
        dz_ref[:, 0:D_MODEL] = (dya * mixed * gu_grad).astype(BF16)
        dmx = dya * gu
        dmx_s[...] = dmx.astype(BF16)
        tri_t = _tri_mask(transposed=True)
        for gi in range(A_GROUPS):
            wmt = jnp.where(tri_t, wt_ref[gi], 0.0).astype(BF16)
            cols = slice(gi * CHUNK, (gi + 1) * CHUNK)
            dw_acc = jnp.zeros((CHUNK, CHUNK), F32)
            dmx_sum = jnp.zeros((CHUNK, CHUNK), F32)
            for n in range(n_chunk):
                rows = slice(n * CHUNK, (n + 1) * CHUNK)
                blk = dmx_s[rows, cols]
                dvn_s[rows, cols] = _dot(wmt, blk)
                dw_acc = dw_acc + _dot_nt(blk, vn_s[rows, cols])
                dmx_sum = dmx_sum + dmx[rows, cols]
            dw_ref[gi] += jnp.where(tri, dw_acc, 0.0)
            dbs_ref[gi] += jnp.sum(dmx_sum, axis=-1, keepdims=True)
        dvn = dvn_s[...]
        dg_ref[...] += jnp.sum(dvn * xh, axis=0, keepdims=True)
        db_ref[...] += jnp.sum(dvn, axis=0, keepdims=True)
        dxh = dvn * g_ref[...]
        dgv = rs * (dxh - jnp.mean(dxh, axis=-1, keepdims=True)
                    - xh * jnp.mean(dxh * xh, axis=-1, keepdims=True))
        dz_ref[:, D_MODEL:2 * D_MODEL] = (dgv * gv_grad).astype(BF16)

    col = lambda c: pl.BlockSpec((tm, D_MODEL), lambda i: (i, c))
    row = pl.BlockSpec((tm, D_MODEL), lambda i: (i, 0))
    vec = pl.BlockSpec((1, D_MODEL), lambda i: (0, 0))
    wsp = pl.BlockSpec((A_GROUPS, CHUNK, CHUNK), lambda i: (0, 0, 0))
    bsp = pl.BlockSpec((A_GROUPS, CHUNK, 1), lambda i: (0, 0, 0))
    return pl.pallas_call(
        body, grid=(T // tm,),
        in_specs=[col(0), col(1), col(2), col(3), row, row, vec, vec, wsp, wsp, bsp, ANY],
        out_specs=[pl.BlockSpec((tm, 4 * D_MODEL), lambda i: (i, 0)), row, vec, vec, wsp, bsp],
        out_shape=[jax.ShapeDtypeStruct((T, 4 * D_MODEL), BF16), jax.ShapeDtypeStruct((T, D_MODEL), BF16),
                   jax.ShapeDtypeStruct((1, D_MODEL), F32), jax.ShapeDtypeStruct((1, D_MODEL), F32),
                   jax.ShapeDtypeStruct((A_GROUPS, CHUNK, CHUNK), F32),
                   jax.ShapeDtypeStruct((A_GROUPS, CHUNK, 1), F32)],
        scratch_shapes=[pltpu.VMEM((tm, D_MODEL), BF16), pltpu.VMEM((tm, D_MODEL), F32),
                        pltpu.VMEM((tm, D_MODEL), BF16), pltpu.VMEM((tm, D_MODEL), F32)],
        name="mixer_bwd", compiler_params=_params("arbitrary"))(
            zm, zm, zm, zm, o, dm, av_g, av_b, w_s, w_st, b_col, dep)


def _rope_tables(pos_ref, invf_ref):
    ang = pos_ref[...].astype(F32) * invf_ref[...]
    lane = lax.broadcasted_iota(jnp.int32, ang.shape, 1)
    cos, sin = jnp.cos(ang), jnp.sin(ang)
    c = jnp.where(lane < QK_ROPE, cos, 0.0)
    sa = jnp.where(lane < QK_ROPE // 2, -sin, 0.0)
    sb = jnp.where((lane >= QK_ROPE // 2) & (lane < QK_ROPE), sin, 0.0)
    return c, sa, sb


def _rope(blk, tabs):
    c, sa, sb = tabs
    return blk * c + pltpu.roll(blk, LANES - QK_ROPE // 2, 1) * sa + pltpu.roll(blk, QK_ROPE // 2, 1) * sb


def _rope_t(dout, tabs):
    c, sa, sb = tabs
    return dout * c + pltpu.roll(dout * sa, QK_ROPE // 2, 1) + pltpu.roll(dout * sb, LANES - QK_ROPE // 2, 1)


def _rms_small(x, g):
    r = lax.rsqrt(jnp.mean(x * x, axis=-1, keepdims=True) + EPS)
    xh = x * r
    return xh, r, xh * g


def _mla_prep_fwd(zs, pos, invf, qg, kvg, wuq_p, wukv):
    T = zs.shape[0]
    tm = _tile(T, 512)
    HW = MLA_HEADS * HEAD_PAD

    def body(zs_ref, pos_ref, invf_ref, qg_ref, kvg_ref, wq_ref, wkv_ref, q_ref, k_ref, v_ref):
        tabs = _rope_tables(pos_ref, invf_ref)
        _, _, cqn = _rms_small(zs_ref[:, 0:Q_LORA], qg_ref[...])
        _, _, ckvn = _rms_small(zs_ref[:, Q_LORA:Q_LORA + KV_LORA], kvg_ref[...])
        q = _dot_nt(cqn.astype(BF16), wq_ref[...]) * ATTN_SCALE
        kv = _dot(ckvn.astype(BF16), wkv_ref[...])
        kr = _rope(zs_ref[:, Q_LORA + KV_LORA:ZS_W], tabs).astype(BF16)
        for h in range(MLA_HEADS):
            b0 = h * HEAD_PAD
            q_ref[:, b0:b0 + QK_NOPE] = q[:, b0:b0 + QK_NOPE].astype(BF16)
            q_ref[:, b0 + QK_NOPE:b0 + HEAD_PAD] = _rope(q[:, b0 + QK_NOPE:b0 + HEAD_PAD], tabs).astype(BF16)
            k_ref[:, b0:b0 + QK_NOPE] = kv[:, b0:b0 + QK_NOPE].astype(BF16)
            k_ref[:, b0 + QK_NOPE:b0 + HEAD_PAD] = kr
            v_ref[:, h * V_HEAD:(h + 1) * V_HEAD] = kv[:, b0 + QK_NOPE:b0 + HEAD_PAD].astype(BF16)

    full = lambda a: pl.BlockSpec(a.shape, lambda i: (0,) * a.ndim)
    return pl.pallas_call(
        body, grid=(T // tm,),
        in_specs=[pl.BlockSpec((tm, ZS_W), lambda i: (i, 0)), pl.BlockSpec((tm, 1), lambda i: (i, 0)),
                  full(invf), full(qg), full(kvg), full(wuq_p), full(wukv)],
        out_specs=[pl.BlockSpec((tm, HW), lambda i: (i, 0)), pl.BlockSpec((tm, HW), lambda i: (i, 0)),
                   pl.BlockSpec((tm, D_MODEL), lambda i: (i, 0))],
        out_shape=[jax.ShapeDtypeStruct((T, HW), BF16), jax.ShapeDtypeStruct((T, HW), BF16),
                   jax.ShapeDtypeStruct((T, D_MODEL), BF16)],
        name="mla_prep_fwd", compiler_params=_params("parallel"))(zs, pos, invf, qg, kvg, wuq_p, wukv)


def _mla_prep_bwd(zs, pos, invf, qg, kvg, wuq_p, wukv, dq, dk, dv):
    T = zs.shape[0]
    tm = _tile(T, 512)
    n_t = T // tm
    HW = MLA_HEADS * HEAD_PAD

    def body(zs_ref, pos_ref, invf_ref, qg_ref, kvg_ref, wq_ref, wkv_ref, dq_ref, dk_ref, dv_ref,
             dzs_ref, dwq_ref, dwkv_ref, dqg_ref, dkvg_ref, dqp_ref, dkv_ref, accq_ref, acckv_ref):
        @pl.when(pl.program_id(0) == 0)
        def _():
            dqg_ref[...] = jnp.zeros_like(dqg_ref)
            dkvg_ref[...] = jnp.zeros_like(dkvg_ref)
            accq_ref[...] = jnp.zeros_like(accq_ref)
            acckv_ref[...] = jnp.zeros_like(acckv_ref)

        tabs = _rope_tables(pos_ref, invf_ref)
        cqh, rq, cqn = _rms_small(zs_ref[:, 0:Q_LORA], qg_ref[...])
        ckvh, rkv, ckvn = _rms_small(zs_ref[:, Q_LORA:Q_LORA + KV_LORA], kvg_ref[...])
        dkr = jnp.zeros((tm, LANES), F32)
        for h in range(MLA_HEADS):
            b0 = h * HEAD_PAD
            dqp_ref[:, b0:b0 + QK_NOPE] = dq_ref[:, b0:b0 + QK_NOPE]
            dqp_ref[:, b0 + QK_NOPE:b0 + HEAD_PAD] = _rope_t(
                dq_ref[:, b0 + QK_NOPE:b0 + HEAD_PAD].astype(F32), tabs).astype(BF16)
            dkv_ref[:, b0:b0 + QK_NOPE] = dk_ref[:, b0:b0 + QK_NOPE]
            dkv_ref[:, b0 + QK_NOPE:b0 + HEAD_PAD] = dv_ref[:, h * V_HEAD:(h + 1) * V_HEAD]
            dkr = dkr + dk_ref[:, b0 + QK_NOPE:b0 + HEAD_PAD].astype(F32)
        accq_ref[...] += _dot_tn(dqp_ref[...], cqn.astype(BF16))
        acckv_ref[...] += _dot_tn(ckvn.astype(BF16), dkv_ref[...])

        @pl.when(pl.program_id(0) == n_t - 1)
        def _():
            dwq_ref[...] = accq_ref[...].astype(BF16)
            dwkv_ref[...] = acckv_ref[...].astype(BF16)

        dcqn = _dot(dqp_ref[...], wq_ref[...])
        dckvn = _dot_nt(dkv_ref[...], wkv_ref[...])
        dqg_ref[...] += jnp.sum(dcqn * cqh, axis=0, keepdims=True)
        dkvg_ref[...] += jnp.sum(dckvn * ckvh, axis=0, keepdims=True)
        dxh = dcqn * qg_ref[...]
        dzs_ref[:, 0:Q_LORA] = (rq * (dxh - cqh * jnp.mean(dxh * cqh, axis=-1, keepdims=True))).astype(BF16)
        dxh = dckvn * kvg_ref[...]
        dzs_ref[:, Q_LORA:Q_LORA + KV_LORA] = (
            rkv * (dxh - ckvh * jnp.mean(dxh * ckvh, axis=-1, keepdims=True))).astype(BF16)
        dzs_ref[:, Q_LORA + KV_LORA:ZS_W] = _rope_t(dkr, tabs).astype(BF16)

    full = lambda a: pl.BlockSpec(a.shape, lambda i: (0,) * a.ndim)
    rowb = lambda w: pl.BlockSpec((tm, w), lambda i: (i, 0))
    return pl.pallas_call(
        body, grid=(T // tm,),
        in_specs=[rowb(ZS_W), rowb(1), full(invf), full(qg), full(kvg), full(wuq_p), full(wukv),
                  rowb(HW), rowb(HW), rowb(D_MODEL)],
        out_specs=[rowb(ZS_W), full(wuq_p), full(wukv), full(qg), full(kvg)],
        out_shape=[jax.ShapeDtypeStruct((T, ZS_W), BF16), jax.ShapeDtypeStruct(wuq_p.shape, BF16),
                   jax.ShapeDtypeStruct(wukv.shape, BF16), jax.ShapeDtypeStruct(qg.shape, F32),
                   jax.ShapeDtypeStruct(kvg.shape, F32)],
        scratch_shapes=[pltpu.VMEM((tm, HW), BF16), pltpu.VMEM((tm, HW), BF16),
                        pltpu.VMEM(wuq_p.shape, F32), pltpu.VMEM(wukv.shape, F32)],
        name="mla_prep_bwd", compiler_params=_params("arbitrary"))(
            zs, pos, invf, qg, kvg, wuq_p, wukv, dq, dk, dv)


def _causal(tq, kmax, q0):
    r = lax.broadcasted_iota(jnp.int32, (tq, kmax), 0) + q0
    c = lax.broadcasted_iota(jnp.int32, (tq, kmax), 1)
    return c <= r


def _attn_fwd(q, k, v, batch, seq):
    tq = _tile(seq, ATTN_TILE)
    nq = seq // tq

    def body(q_ref, k_ref, v_ref, o_ref, lse_ref):
        diag = _causal(tq, tq, 0)
        for qi in range(nq):
            rows = slice(qi * tq, (qi + 1) * tq)
            qr = q_ref[rows, :]
            s_d = jnp.where(diag, _dot_nt(qr, k_ref[rows, :]), NEG_BIG)
            m = jnp.max(s_d, axis=-1, keepdims=True)
            if qi > 0:
                before = slice(0, qi * tq)
                s_b = _dot_nt(qr, k_ref[before, :])
                m = jnp.maximum(m, jnp.max(s_b, axis=-1, keepdims=True))
                p_b = jnp.exp(s_b - m)
                l = jnp.sum(p_b, axis=-1, keepdims=True)
                acc = _dot(p_b.astype(BF16), v_ref[before, :])
            p_d = jnp.exp(s_d - m)
            l_d = jnp.sum(p_d, axis=-1, keepdims=True)
            acc_d = _dot(p_d.astype(BF16), v_ref[rows, :])
            l, acc = (l + l_d, acc + acc_d) if qi > 0 else (l_d, acc_d)
            o_ref[rows, :] = (acc / l).astype(KEPT)
            lse_ref[rows, :] = jnp.broadcast_to(m + jnp.log(l), (tq, V_HEAD))

    return pl.pallas_call(
        body, grid=(batch, MLA_HEADS),
        in_specs=[pl.BlockSpec((seq, HEAD_PAD), lambda b, h: (b, h)),
                  pl.BlockSpec((seq, HEAD_PAD), lambda b, h: (b, h)),
                  pl.BlockSpec((seq, V_HEAD), lambda b, h: (b, h))],
        out_specs=[pl.BlockSpec((seq, V_HEAD), lambda b, h: (b, h)),
                   pl.BlockSpec((seq, V_HEAD), lambda b, h: (b, h))],
        out_shape=[jax.ShapeDtypeStruct((batch * seq, D_MODEL), KEPT),
                   jax.ShapeDtypeStruct((batch * seq, D_MODEL), F32)],
        name="attn_fwd", compiler_params=_params("parallel", "parallel"))(q, k, v)


def _attn_bwd(q, k, v, o, do, lse, batch, seq, dep):
    tq = _tile(seq, ATTN_TILE)
    nq = seq // tq

    def body(q_ref, k_ref, v_ref, o_ref, do_ref, lse_ref, dep_ref, dq_ref, dk_ref, dv_ref, dk_acc, dv_acc):
        dk_acc[...] = jnp.zeros_like(dk_acc)
        dv_acc[...] = jnp.zeros_like(dv_acc)
        for qi in range(nq):
            rows = slice(qi * tq, (qi + 1) * tq)
            kmax = (qi + 1) * tq
            qr = q_ref[rows, :]
            dor = do_ref[rows, :]
            kk = k_ref[0:kmax, :]
            s = _dot_nt(qr, kk)
            p = jnp.where(_causal(tq, kmax, qi * tq), jnp.exp(s - lse_ref[rows, 0:1]), 0.0)
            dp = _dot_nt(dor, v_ref[0:kmax, :])
            delta = jnp.sum(dor.astype(F32) * o_ref[rows, :].astype(F32), axis=-1, keepdims=True)
            ds = (p * (dp - delta)).astype(BF16)
            dq_ref[rows, :] = (_dot(ds, kk) * ATTN_SCALE).astype(BF16)
            dk_acc[0:kmax, :] += _dot_tn(ds, qr)
            dv_acc[0:kmax, :] += _dot_tn(p.astype(BF16), dor)
        dk_ref[...] = dk_acc[...].astype(BF16)
        dv_ref[...] = dv_acc[...].astype(BF16)

    qspec = pl.BlockSpec((seq, HEAD_PAD), lambda b, h: (b, h))
    vspec = pl.BlockSpec((seq, V_HEAD), lambda b, h: (b, h))
    T = batch * seq
    return pl.pallas_call(
        body, grid=(batch, MLA_HEADS),
        in_specs=[qspec, qspec, vspec, vspec, vspec, vspec, ANY],
        out_specs=[qspec, qspec, vspec],
        out_shape=[jax.ShapeDtypeStruct((T, MLA_HEADS * HEAD_PAD), BF16),
                   jax.ShapeDtypeStruct((T, MLA_HEADS * HEAD_PAD), BF16),
                   jax.ShapeDtypeStruct((T, D_MODEL), BF16)],
        scratch_shapes=[pltpu.VMEM((seq, HEAD_PAD), F32), pltpu.VMEM((seq, V_HEAD), F32)],
        name="attn_bwd", compiler_params=_params("parallel", "parallel"))(q, k, v, o, do, lse, dep)


def _merge_out(x, yag, zm, o, w_out, ffn_g):
    T = x.shape[0]
    tm = _tile(T, 512)

    def body(x_ref, ya_ref, gb_ref, o_ref, w_ref, g_ref, mg_ref, x1_ref, h2_ref):
        mg = (ya_ref[...].astype(F32) + _sigmoid(gb_ref[...].astype(F32)) * o_ref[...].astype(F32)).astype(BF16)
        mg_ref[...] = mg
        x1 = x_ref[...] + _dot(mg, w_ref[...])
        x1_ref[...] = x1
        r = lax.rsqrt(jnp.mean(x1 * x1, axis=-1, keepdims=True) + EPS)
        h2_ref[...] = (x1 * r * g_ref[...]).astype(BF16)

    row = pl.BlockSpec((tm, D_MODEL), lambda i: (i, 0))
    return pl.pallas_call(
        body, grid=(T // tm,),
        in_specs=[row, row, pl.BlockSpec((tm, D_MODEL), lambda i: (i, 3)), row,
                  pl.BlockSpec((D_MODEL, D_MODEL), lambda i: (0, 0)), pl.BlockSpec((1, D_MODEL), lambda i: (0, 0))],
        out_specs=[row, row, row],
        out_shape=[jax.ShapeDtypeStruct((T, D_MODEL), BF16), jax.ShapeDtypeStruct((T, D_MODEL), F32),
                   jax.ShapeDtypeStruct((T, D_MODEL), BF16)],
        name="merge_out", compiler_params=_params("parallel"))(x, yag, zm, o, w_out, ffn_g)


FF_TILE = 256
FF_BLOCKS = D_FF // FF_TILE
FFB_TILE = 256
UP_ROWS = 512
EDGE = 16


def _shift_up(x, k):
    n = x.shape[0]
    row = lax.broadcasted_iota(jnp.int32, x.shape, 0)
    return jnp.where(row < n - k, pltpu.roll(x, n - k, 0), 0.0)


def _up_act(h2, wt_up, cw, cb, batch, seq):
    def body(h_ref, wug_ref, wuv_ref, wg_ref, wv_ref, bg_ref, bv_ref, ug_ref, uv_ref, g_ref, v_ref, a_ref,
             ug_s, uv_s):
        for s in (ug_s, uv_s):
            s[0:SUBLANES, :] = jnp.zeros((SUBLANES, FF_TILE), F32)

        def conv(s, w_ref, b_ref, r0):
            return (b_ref[...] + w_ref[2:3, :] * s[r0:r0 + UP_ROWS, :]
                    + w_ref[1:2, :] * s[r0 - 1:r0 - 1 + UP_ROWS, :]
                    + w_ref[0:1, :] * s[r0 - 2:r0 - 2 + UP_ROWS, :])

        for c in range(seq // UP_ROWS):
            rows = slice(c * UP_ROWS, (c + 1) * UP_ROWS)
            r0 = SUBLANES + c * UP_ROWS
            h = h_ref[rows, :]
            for w_ref, u_ref, s in ((wug_ref, ug_ref, ug_s), (wuv_ref, uv_ref, uv_s)):
                u = _dot_nt(h, w_ref[...])
                u_ref[rows, :] = u.astype(KEPT)
                s[r0:r0 + UP_ROWS, :] = u
            gate, val = conv(ug_s, wg_ref, bg_ref, r0), conv(uv_s, wv_ref, bv_ref, r0)
            g_ref[rows, :] = gate.astype(KEPT)
            v_ref[rows, :] = val.astype(KEPT)
            a_ref[rows, :] = (gate * _sigmoid(gate) * val).astype(BF16)

    blk = pl.BlockSpec((seq, FF_TILE), lambda b, j: (b, j))
    wup = lambda off: pl.BlockSpec((FF_TILE, D_MODEL), lambda b, j: (j + off, 0))
    wsp = lambda off: pl.BlockSpec((3, FF_TILE), lambda b, j: (0, j + off))
    bsp = lambda off: pl.BlockSpec((1, FF_TILE), lambda b, j: (0, j + off))
    T = batch * seq
    kept = jax.ShapeDtypeStruct((T, D_FF), KEPT)
    return pl.pallas_call(
        body, grid=(batch, FF_BLOCKS),
        in_specs=[pl.BlockSpec((seq, D_MODEL), lambda b, j: (b, 0)), wup(0), wup(FF_BLOCKS),
                  wsp(0), wsp(FF_BLOCKS), bsp(0), bsp(FF_BLOCKS)],
        out_specs=[blk] * 5,
        out_shape=[kept, kept, kept, kept, jax.ShapeDtypeStruct((T, D_FF), BF16)],
        scratch_shapes=[pltpu.VMEM((SUBLANES + seq, FF_TILE), F32)] * 2,
        name="up_act", compiler_params=_params("parallel", "arbitrary"))(h2, wt_up, wt_up, cw, cw, cb, cb)


def _ffn_act_bwd(upg, upv, gate, val, cw, dx2b, w_down, batch, seq):
    def half(du, x, w_ref, dx_ref, dw_ref):
        j = pl.program_id(1)
        n = du.shape[0]
        up1, up2 = pltpu.roll(du, n - 1, 0), pltpu.roll(du, n - 2, 0)
        dx_ref[...] = (w_ref[2:3, :] * du + w_ref[1:2, :] * up1 + w_ref[0:1, :] * up2).astype(BF16)
        tail = du[n - EDGE:n]
        dx_ref[n - EDGE:n, :] = (w_ref[2:3, :] * tail + w_ref[1:2, :] * _shift_up(tail, 1)
                                 + w_ref[0:1, :] * _shift_up(tail, 2)).astype(BF16)
        row = lax.broadcasted_iota(jnp.int32, (EDGE, du.shape[1]), 0)
        head, x_tail = du[0:EDGE], x[n - EDGE:n]
        wrap1 = jnp.sum(jnp.where(row >= EDGE - 1, pltpu.roll(head, EDGE - 1, 0), 0.0) * x_tail, axis=0, keepdims=True)
        wrap2 = jnp.sum(jnp.where(row >= EDGE - 2, pltpu.roll(head, EDGE - 2, 0), 0.0) * x_tail, axis=0, keepdims=True)
        dw_ref[j, 2:3, :] += jnp.sum(du * x, axis=0, keepdims=True)
        dw_ref[j, 1:2, :] += jnp.sum(up1 * x, axis=0, keepdims=True) - wrap1
        dw_ref[j, 0:1, :] += jnp.sum(up2 * x, axis=0, keepdims=True) - wrap2
        dw_ref[j, 3:4, :] += jnp.sum(du, axis=0, keepdims=True)

    def body(ug_ref, uv_ref, g_ref, v_ref, wg_ref, wv_ref, dx_ref, wd_ref, dg_ref, dv_ref, dwg_ref, dwv_ref):
        @pl.when((pl.program_id(0) == 0) & (pl.program_id(1) == 0))
        def _():
            dwg_ref[...] = jnp.zeros_like(dwg_ref)
            dwv_ref[...] = jnp.zeros_like(dwv_ref)

        gate, val = g_ref[...].astype(F32), v_ref[...].astype(F32)
        sg = _sigmoid(gate)
        dav = _dot_nt(dx_ref[...], wd_ref[...])
        half(dav * val * sg * (1.0 + gate * (1.0 - sg)), ug_ref[...].astype(F32), wg_ref, dg_ref, dwg_ref)
        half(dav * gate * sg, uv_ref[...].astype(F32), wv_ref, dv_ref, dwv_ref)

    nb = D_FF // FFB_TILE
    blk = pl.BlockSpec((seq, FFB_TILE), lambda b, j: (b, j))
    wsp = lambda off: pl.BlockSpec((3, FFB_TILE), lambda b, j: (0, j + off))
    acc = pl.BlockSpec((nb, 4, FFB_TILE), lambda b, j: (0, 0, 0))
    T = batch * seq
    dupg, dupv, dwg, dwv = pl.pallas_call(
        body, grid=(batch, nb),
        in_specs=[blk, blk, blk, blk, wsp(0), wsp(nb),
                  pl.BlockSpec((seq, D_MODEL), lambda b, j: (b, 0)),
                  pl.BlockSpec((FFB_TILE, D_MODEL), lambda b, j: (j, 0))],
        out_specs=[blk, blk, acc, acc],
        out_shape=[jax.ShapeDtypeStruct((T, D_FF), BF16), jax.ShapeDtypeStruct((T, D_FF), BF16),
                   jax.ShapeDtypeStruct((nb, 4, FFB_TILE), F32), jax.ShapeDtypeStruct((nb, 4, FFB_TILE), F32)],
        name="ffn_act_bwd", compiler_params=_params("arbitrary", "arbitrary"))(
            upg, upv, gate, val, cw, cw, dx2b, w_down)
    dwg, dwv = (jnp.transpose(a, (1, 0, 2)).reshape(4, D_FF) for a in (dwg, dwv))
    return dupg, dupv, dwg[:3], dwv[:3], dwg[3:], dwv[3:]


def _down_loss(a, w_down, x1, target, gfin):
    T = x1.shape[0]
    tm = _tile(T, 512)

    def body(a_ref, w_ref, x1_ref, t_ref, g_ref, dx_ref, dxb_ref, loss_ref, dg_ref):
        @pl.when(pl.program_id(0) == 0)
        def _():
            loss_ref[...] = jnp.zeros_like(loss_ref)
            dg_ref[...] = jnp.zeros_like(dg_ref)

        x2 = x1_ref[...] + _dot(a_ref[...], w_ref[...])
        r = lax.rsqrt(jnp.mean(x2 * x2, axis=-1, keepdims=True) + EPS)
        xh = x2 * r
        g = g_ref[...]
        diff = xh * g - t_ref[...]
        loss_ref[...] += 0.5 * jnp.sum(jnp.mean(diff * diff, axis=-1, keepdims=True))
        dy = diff * (1.0 / D_MODEL)
        dg_ref[...] += jnp.sum(dy * xh, axis=0, keepdims=True)
        dxh = dy * g
        dx = r * (dxh - xh * jnp.mean(dxh * xh, axis=-1, keepdims=True))
        dx_ref[...] = dx
        dxb_ref[...] = dx.astype(BF16)

    row = pl.BlockSpec((tm, D_MODEL), lambda i: (i, 0))
    vec = pl.BlockSpec((1, D_MODEL), lambda i: (0, 0))
    return pl.pallas_call(
        body, grid=(T // tm,),
        in_specs=[pl.BlockSpec((tm, D_FF), lambda i: (i, 0)),
                  pl.BlockSpec((D_FF, D_MODEL), lambda i: (0, 0)), row, row, vec],
        out_specs=[row, row, pl.BlockSpec((8, LANES), lambda i: (0, 0)), vec],
        out_shape=[jax.ShapeDtypeStruct((T, D_MODEL), F32), jax.ShapeDtypeStruct((T, D_MODEL), BF16),
                   jax.ShapeDtypeStruct((8, LANES), F32), jax.ShapeDtypeStruct((1, D_MODEL), F32)],
        name="down_loss", compiler_params=_params("arbitrary"))(a, w_down, x1, target, gfin)


def _local_step(x, positions, target, mix_norm, av_g, av_b, w_s, b_s, q_norm, kv_norm, ffn_norm, conv_b,
                final_norm, comm):
    batch, seq, _ = x.shape
    T = batch * seq
    x = x.reshape(T, D_MODEL)
    target = target.reshape(T, D_MODEL)
    pos = positions.reshape(T, 1)
    half = jnp.arange(0, QK_ROPE, 2, dtype=F32) / QK_ROPE
    inv_freq = 1.0 / (ROPE_THETA ** half)
    invf = jnp.concatenate([inv_freq, inv_freq, jnp.zeros((LANES - QK_ROPE,), F32)]).reshape(1, LANES)
    w_st = jnp.swapaxes(w_s, 1, 2)
    b_col = b_s.reshape(A_GROUPS, CHUNK, 1)

    wt_in = comm.in_weights()
    h, zm, zs = _in_proj(x, mix_norm, wt_in)
    yag = _mixer_a_fwd(zm, av_g, av_b, w_s, b_col)
    wuq_p, wukv, w_out = comm.mla_weights(after=yag)
    q, k, v = _mla_prep_fwd(zs, pos, invf, q_norm, kv_norm, wuq_p, wukv)
    o, lse = _attn_fwd(q, k, v, batch, seq)
    merged, x1, h2 = _merge_out(x, yag, zm, o, w_out, ffn_norm)
    wt_up, conv_w, w_down = comm.ffn_weights(after=merged)
    upg, upv, gate, val, act = _up_act(h2, wt_up, conv_w, conv_b, batch, seq)
    dx2, dx2b, loss_acc, d_final = _down_loss(act, w_down, x1, target, final_norm)

    d_wdown = _mm_tn(act, dx2b, "dw_down")
    dupg, dupv, dcwg, dcwv, dcbg, dcbv = _ffn_act_bwd(upg, upv, gate, val, conv_w, dx2b, w_down, batch, seq)
    d_wt_up = _mm_tn(dupv, h2, "dw_up_val", rows=2 * D_FF, row0=D_FF,
                     into=_mm_tn(dupg, h2, "dw_up_gate", rows=2 * D_FF))
    dx1, d_ffn_norm, dmerged = _proj_bwd(
        [dupg, dupv], wt_up, [(0, (0, D_FF), (0, D_FF)), (1, (0, D_FF), (D_FF, 2 * D_FF))],
        x1, ffn_norm, dx2, "up_proj_bwd", w2=w_out)
    d_wout = _mm_tn(merged, dx1, "dw_out")
    token = comm.send_ffn_grads(d_wdown, d_wt_up, jnp.concatenate([dcwg, dcwv], axis=1), d_wout)
    dzm, do, d_avg, d_avb, d_ws, d_bs = _mixer_bwd(zm, o, dmerged, av_g, av_b, w_s, w_st, b_col, token)
    zs_rows_zero = lax.empty((IN_DIM, D_MODEL), BF16).at[SPLIT_V:SPLIT_KR].set(0)
    d_wt_in = _mm_tn(dzm, h, "dw_in_main", rows=IN_DIM, gap=(SPLIT_V, ZS_ROWS), into=zs_rows_zero)
    token = comm.send_early_grads(d_wt_in, [
        d_avg, d_avb, _small_2d(d_ws).astype(BF16), d_bs.reshape(A_GROUPS, CHUNK), d_ffn_norm,
        jnp.concatenate([dcbg, dcbv], axis=1), d_final])
    dq, dk, dv = _attn_bwd(q, k, v, o, do, lse, batch, seq, token)
    dzs, d_wuq_p, d_wukv, d_qn, d_kvn = _mla_prep_bwd(zs, pos, invf, q_norm, kv_norm, wuq_p, wukv, dq, dk, dv)
    d_wt_zs = _mm_tn(dzs, h, "dw_in_small")
    token = comm.send_mla_grads(d_wuq_p, d_wukv, d_wt_zs)
    terms = [(0, (i * D_MODEL, (i + 1) * D_MODEL), rows) for i, rows in enumerate(IN_ROWS_MAIN)]
    terms.append((1, (0, ZS_W), IN_ROWS_ZS))
    dx, d_mix_norm = _proj_bwd([dzm, dzs], wt_in, terms, x, mix_norm, dx1, "in_proj_bwd", dep=token, rows=512)
    token = comm.send_late_grads([d_qn, d_kvn, d_mix_norm, loss_acc])
    return dx.reshape(batch, seq, D_MODEL), token


MESH_ID = pl.DeviceIdType.MESH
EFFECT = pltpu.SideEffectType.DATAFLOW_SIDE_EFFECTING


def _mesh_pos():
    return lax.axis_index("x"), lax.axis_index("y"), lax.axis_index("c")


def _peer(pos, d):
    x, y, c = pos
    px = 1 - x if d & 4 else x
    py = 1 - y if d & 2 else y
    pc = 1 - c if d & 1 else c
    return (px, py, pc), 4 * px + 2 * py + pc


def _copy(src_ref, land_ref, send_sems, recv_sems, a, d, pos, exchange, landing_here):
    peer, pid = _peer(pos, d)
    me = 4 * pos[0] + 2 * pos[1] + pos[2]
    if exchange:
        src, dst = src_ref.at[pid], land_ref.at[d]
    else:
        src, dst = src_ref, land_ref.at[pid if landing_here else me]
    return pltpu.make_async_remote_copy(
        src_ref=src, dst_ref=dst, send_sem=send_sems.at[a * (N_DEV - 1) + d - 1],
        recv_sem=recv_sems.at[a * (N_DEV - 1) + d - 1],
        device_id=peer, device_id_type=MESH_ID)


def _start_copies(groups, modes, name, dep=None):
    sizes = [len(g) for g in groups]
    srcs = [s for g in groups for s in g]
    lands = [lax.empty(s.shape if modes[gi] else (N_DEV,) + s.shape, s.dtype)
             for gi, g in enumerate(groups) for s in g]
    n, ng = len(srcs), len(groups)
    n_in = 2 * n + (dep is not None)

    def body(*refs):
        src_refs, land_refs = refs[:n], refs[n:2 * n]
        sems = refs[n_in:n_in + 3 * ng]
        token = refs[-1]
        pos = _mesh_pos()
        k = 0
        for gi, size in enumerate(sizes):
            for a in range(size):
                _own_copy(src_refs[k], land_refs[k], sems[3 * gi + 2], a, pos, modes[gi]).start()
                for d in range(1, N_DEV):
                    _copy(src_refs[k], land_refs[k], sems[3 * gi], sems[3 * gi + 1], a, d, pos, modes[gi],
                          landing_here=False).start()
                k += 1
        token[...] = jnp.zeros_like(token)

    sem_shapes = []
    for size in sizes:
        remote = pltpu.SemaphoreType.DMA((size * (N_DEV - 1),))
        sem_shapes += [remote, remote, pltpu.SemaphoreType.DMA((size,))]
    out = pl.pallas_call(
        body, name=name,
        out_shape=(*sem_shapes, *[pltpu.HBM(a.shape, a.dtype) for a in srcs + lands],
                   jax.ShapeDtypeStruct((8, LANES), F32)),
        in_specs=[HBM] * (2 * n) + [ANY] * (dep is not None),
        out_specs=(*[SEM] * (3 * ng), *[HBM] * (2 * n), pl.BlockSpec(memory_space=pltpu.VMEM)),
        input_output_aliases={i: 3 * ng + i for i in range(2 * n)},
        compiler_params=pltpu.CompilerParams(has_side_effects=EFFECT),
    )(*[pltpu.with_memory_space_constraint(a, pltpu.HBM) for a in srcs + lands], *([dep] if dep is not None else []))
    thru = out[3 * ng:3 * ng + 2 * n]
    handles, k = [], 0
    for gi, size in enumerate(sizes):
        handles.append((out[3 * gi:3 * gi + 3], thru[k:k + size], thru[n + k:n + k + size]))
        k += size
    return handles, out[-1]


def _own_copy(src_ref, land_ref, local_sems, a, pos, exchange):
    me = 4 * pos[0] + 2 * pos[1] + pos[2]
    src, dst = (src_ref.at[me], land_ref.at[0]) if exchange else (src_ref, land_ref.at[me])
    return pltpu.make_async_copy(src, dst, local_sems.at[a])


def _wait_copies(handle, exchange, after, name):
    sems, srcs, lands = handle
    n = len(srcs)

    def body(*refs):
        src_refs, land_refs = refs[:n], refs[n:2 * n]
        send, recv, local = refs[2 * n:2 * n + 3]
        pos = _mesh_pos()
        for a in range(n):
            _own_copy(src_refs[a], land_refs[a], local, a, pos, exchange).wait()
            for d in range(1, N_DEV):
                cp = _copy(src_refs[a], land_refs[a], send, recv, a, d, pos, exchange, landing_here=True)
                cp.wait_send()
                cp.wait_recv()

    out = pl.pallas_call(
        body, name=name,
        out_shape=tuple(pltpu.HBM(a.shape, a.dtype) for a in (*srcs, *lands)),
        in_specs=[HBM] * (2 * n) + [SEM, SEM, SEM, ANY], out_specs=[HBM] * (2 * n),
        input_output_aliases={i: i for i in range(2 * n)},
        compiler_params=pltpu.CompilerParams(has_side_effects=EFFECT),
    )(*srcs, *lands, *sems, after)
    return out[n:]


def _gather_now(a, name):
    def body(x_ref, out_ref, send_sems, recv_sems, local_sem):
        x, y, c = _mesh_pos()
        me, sibling = (x, y, c), (x, y, 1 - c)
        chips = [(1 - x, y), (x, 1 - y), (1 - x, 1 - y)]

        def slot(p):
            return out_ref.at[4 * p[0] + 2 * p[1] + p[2]]

        def copy(k, block, to, src=None):
            return pltpu.make_async_remote_copy(
                src_ref=slot(block) if src is None else src, dst_ref=slot(block), send_sem=send_sems.at[k],
                recv_sem=recv_sems.at[k], device_id=to, device_id_type=MESH_ID)

        mine = pltpu.make_async_copy(x_ref, slot(me), local_sem)
        mine.start()
        first = [copy(0, me, sibling, src=x_ref)]
        first += [copy(1 + j, me, (*chip, c), src=x_ref) for j, chip in enumerate(chips)]
        for cp in first:
            cp.start()
        passed = [copy(4 + j, (*chip, c), sibling) for j, chip in enumerate(chips)]
        for j, chip in enumerate(chips):
            copy(1 + j, (*chip, c), me).wait_recv()
            passed[j].start()
        copy(0, sibling, me).wait_recv()
        for j, chip in enumerate(chips):
            copy(4 + j, (*chip, 1 - c), me).wait_recv()
        for cp in first + passed:
            cp.wait_send()
        mine.wait()

    return pl.pallas_call(
        body, in_specs=[ANY], out_specs=ANY,
        out_shape=jax.ShapeDtypeStruct((N_DEV,) + a.shape, a.dtype),
        scratch_shapes=[pltpu.SemaphoreType.DMA((N_DEV - 1,)), pltpu.SemaphoreType.DMA((N_DEV - 1,)),
                        pltpu.SemaphoreType.DMA],
        name=name, compiler_params=pltpu.CompilerParams(has_side_effects=True))(a)


def _sum_parts(p_ref):
    g = p_ref[0].astype(F32)
    for k in range(1, N_DEV):
        g = g + p_ref[k].astype(F32)
    return g


def _adamw_update(p_ref, w_ref, m_ref, v_ref, g_ref, d_ref, nm_ref, nv_ref, patch=None):
    c1 = 1.0 - ADAM_B1 ** ADAM_STEP
    c2 = 1.0 - ADAM_B2 ** ADAM_STEP
    g = _sum_parts(p_ref)
    if patch is not None:
        g = patch(g)
    nm = ADAM_B1 * m_ref[...] + (1.0 - ADAM_B1) * g
    nv = ADAM_B2 * v_ref[...] + (1.0 - ADAM_B2) * (g * g)
    g_ref[...] = g
    nm_ref[...] = nm
    nv_ref[...] = nv
    d_ref[...] = -ADAM_LR * ((nm / c1) / (jnp.sqrt(nv / c2) + ADAM_EPS) + ADAM_WD * w_ref[...])


def _adamw_many(parts, ws, ms, vs, sums, name):
    n, ns = len(ws), len(sums)

    def body(*refs):
        ins, outs = refs[:4 * n + ns], refs[4 * n + ns:]
        for i in range(n):
            _adamw_update(ins[i], ins[n + i], ins[2 * n + i], ins[3 * n + i],
                          outs[i], outs[n + i], outs[2 * n + i], outs[3 * n + i])
        for i in range(ns):
            outs[4 * n + i][...] = _sum_parts(ins[4 * n + i])

    full = lambda a: pl.BlockSpec(a.shape, lambda: (0,) * a.ndim)
    args = [*parts, *ws, *ms, *vs, *sums]
    outs = [jax.ShapeDtypeStruct(w.shape, F32) for _ in range(4) for w in ws]
    outs += [jax.ShapeDtypeStruct(s.shape[1:], F32) for s in sums]
    res = pl.pallas_call(
        body, in_specs=[full(a) for a in args], out_specs=[full(o) for o in outs], out_shape=outs,
        name=name, compiler_params=pltpu.CompilerParams(vmem_limit_bytes=VMEM_LIMIT))(*args)
    return res[:n], res[n:2 * n], res[2 * n:3 * n], res[3 * n:4 * n], res[4 * n:]


def _adamw(parts, w, m, v, name, late=None):
    R, C = w.shape
    tr, tc = R, C
    if N_DEV * R * C * parts.dtype.itemsize > SMALL_BLOCK_BYTES:
        tr = next((t for t in range(min(R, 256) // 16 * 16, 15, -16) if R % t == 0), R)
        if tr == R:
            tc = _tile(C, 256)
    more, places = late if late is not None else (None, ())
    assert late is None or tr == R

    def body(p_ref, w_ref, m_ref, v_ref, *refs):
        def patch(g):
            more_ref, g_s = refs[0], refs[-1]
            x, y, c = _mesh_pos()
            me = 4 * x + 2 * y + c
            rows = _sum_parts(more_ref)
            g_s[...] = g
            for dev, row in places:
                g_s[row:row + rows.shape[0], :] += jnp.where(me == dev, rows, 0.0)
            return g_s[...]

        outs = refs[:4] if late is None else refs[1:5]
        _adamw_update(p_ref, w_ref, m_ref, v_ref, *outs, patch=None if late is None else patch)

    blk = pl.BlockSpec((tr, tc), lambda i, j: (i, j))
    shp = jax.ShapeDtypeStruct((R, C), F32)
    in_specs = [pl.BlockSpec((N_DEV, tr, tc), lambda i, j: (0, i, j)), blk, blk, blk]
    if late is not None:
        in_specs.append(pl.BlockSpec((N_DEV, more.shape[1], tc), lambda i, j: (0, 0, j)))
    return pl.pallas_call(
        body, grid=(R // tr, C // tc), in_specs=in_specs,
        out_specs=[blk, blk, blk, blk], out_shape=[shp, shp, shp, shp],
        scratch_shapes=[] if late is None else [pltpu.VMEM((tr, tc), F32)],
        name=name, compiler_params=_params("parallel", "parallel"))(parts, w, m, v, *([] if late is None else [more]))


def _adamw_rows(parts, w, m, v, name):
    R = w.shape[0]

    def body(p_ref, w_ref, m_ref, v_ref, g_ref, d_ref, nm_ref, nv_ref):
        for k in range(R):
            _adamw_update(p_ref.at[:, k:k + 1, :], w_ref.at[k], m_ref.at[k], v_ref.at[k],
                          g_ref.at[k], d_ref.at[k], nm_ref.at[k], nv_ref.at[k])

    full = lambda a: pl.BlockSpec(a.shape, lambda: (0,) * a.ndim)
    shp = jax.ShapeDtypeStruct(w.shape, F32)
    return pl.pallas_call(
        body, in_specs=[full(a) for a in (parts, w, m, v)], out_specs=[full(w)] * 4, out_shape=[shp] * 4,
        name=name, compiler_params=pltpu.CompilerParams(vmem_limit_bytes=VMEM_LIMIT))(parts, w, m, v)


SPLIT_V = 2 * D_MODEL
SPLIT_KR = SPLIT_V + Q_LORA + KV_LORA + QK_ROPE
IN_DIM = SPLIT_KR + 2 * D_MODEL
IN_ROWS_MAIN = ((0, D_MODEL), (D_MODEL, SPLIT_V), (SPLIT_KR, SPLIT_KR + D_MODEL), (SPLIT_KR + D_MODEL, IN_DIM))
IN_ROWS_ZS = (SPLIT_V, SPLIT_V + ZS_W)
ZS_ROWS = SPLIT_KR - SPLIT_V
IN_SHARD = IN_DIM // N_DEV
ZS_PIECES = tuple(
    (k, max(IN_SHARD * k, SPLIT_V) - SPLIT_V, max(IN_SHARD * k, SPLIT_V) - IN_SHARD * k)
    for k in range(N_DEV) if max(IN_SHARD * k, SPLIT_V) < min(IN_SHARD * (k + 1), SPLIT_KR))
ZS_PIECE_ROWS = ZS_ROWS // len(ZS_PIECES)
assert all(min(IN_SHARD * (k + 1), SPLIT_KR) - max(IN_SHARD * k, SPLIT_V) == ZS_PIECE_ROWS for k, _, _ in ZS_PIECES)

SMALL_EARLY = ("a_v_norm_g", "a_v_norm_b", "a_spatial_w", "a_spatial_b", "ffn_norm", "conv_b", "final_norm")
SMALL_LATE = ("q_a_norm", "kv_a_norm", "mix_norm")


def _small_2d(a):
    return a.reshape(-1, a.shape[-1])


def _cols_from_shards(g):
    return jnp.transpose(g, (1, 0, 2)).reshape(g.shape[1], N_DEV * g.shape[2])


def _shards_from_cols(a):
    R, W = a.shape
    return jnp.transpose(a.reshape(R, N_DEV, W // N_DEV), (1, 0, 2))


class _Comm:
    GATHER_GROUPS = (("w_uq", "w_ukv", "w_out"), ("w_up", "conv_w", "w_down"))
    FFN_GRADS = ("w_down", "w_up", "conv_w", "w_out")
    TRANSPOSED = ("w_in", "w_up", "w_uq")

    def __init__(self, shards):
        local = {n: a.astype(F32 if n == "conv_w" else BF16) for n, a in shards.items()}
        self.g_in = _gather_now(local["w_in"], "gather_w_in")
        groups = [[local[n] for n in g] for g in self.GATHER_GROUPS]
        (self.h_mla, self.h_ffn), _ = _start_copies(groups, [False] * 2, "gather_start", dep=self.g_in)

    def in_weights(self):
        return self.g_in.reshape(IN_DIM, D_MODEL)

    def mla_weights(self, after):
        g_uq, g_ukv, g_out = _wait_copies(self.h_mla, False, after, "gather_wait_mla")
        wuq_p = jnp.pad(g_uq, ((0, 0), (0, HEAD_PAD - QK_HEAD), (0, 0))).reshape(MLA_HEADS * HEAD_PAD, Q_LORA)
        return wuq_p, _cols_from_shards(g_ukv), g_out.reshape(D_MODEL, D_MODEL)

    def ffn_weights(self, after):
        g_up, g_cw, g_down = _wait_copies(self.h_ffn, False, after, "gather_wait_ffn")
        return g_up.reshape(2 * D_FF, D_MODEL), _cols_from_shards(g_cw), g_down.reshape(D_FF, D_MODEL)

    def send_ffn_grads(self, d_wdown, d_wt_up, d_convw, d_wout):
        group = [d_wdown.reshape(N_DEV, D_FF // N_DEV, D_MODEL), d_wt_up.reshape(N_DEV, 2 * D_FF // N_DEV, D_MODEL),
                 _shards_from_cols(d_convw), d_wout.reshape(N_DEV, D_MODEL // N_DEV, D_MODEL)]
        (self.h_ffn_grads,), token = _start_copies([group], [True], "ffn_grads_start")
        return token

    def send_early_grads(self, d_wt_in, grads):
        blocks = d_wt_in.reshape(N_DEV, IN_SHARD, D_MODEL)
        (self.h_small_early, self.h_in_grads), token = _start_copies(
            [grads, [blocks]], [False, True], "early_grads_start")
        return token

    def send_mla_grads(self, d_wuq_p, d_wukv, d_wt_zs):
        d_uq = d_wuq_p.reshape(MLA_HEADS, HEAD_PAD, Q_LORA)[:, :QK_HEAD, :]
        zs_blocks = jnp.zeros((N_DEV, ZS_PIECE_ROWS, D_MODEL), d_wt_zs.dtype)
        for dev, first, _ in ZS_PIECES:
            zs_blocks = zs_blocks.at[dev].set(d_wt_zs[first:first + ZS_PIECE_ROWS])
        (self.h_mla_grads,), token = _start_copies(
            [[d_uq, _shards_from_cols(d_wukv), zs_blocks]], [True], "mla_grads_start")
        return token

    def send_late_grads(self, small):
        (self.h_late_small,), token = _start_copies([small], [False], "late_grads_start")
        return token


def kernel(x, positions, mix_norm, w_in, a_v_norm_g, a_v_norm_b, a_spatial_w, a_spatial_b, q_a_norm, w_uq, kv_a_norm, w_ukv, w_out, ffn_norm, w_up, conv_w, conv_b, w_down, final_norm, loss_target, m_mix_norm, m_w_in, m_a_v_norm_g, m_a_v_norm_b, m_a_spatial_w, m_a_spatial_b, m_q_a_norm, m_w_uq, m_kv_a_norm, m_w_ukv, m_w_out, m_ffn_norm, m_w_up, m_conv_w, m_conv_b, m_w_down, m_final_norm, v_mix_norm, v_w_in, v_a_v_norm_g, v_a_v_norm_b, v_a_spatial_w, v_a_spatial_b, v_q_a_norm, v_w_uq, v_kv_a_norm, v_w_ukv, v_w_out, v_ffn_norm, v_w_up, v_conv_w, v_conv_b, v_w_down, v_final_norm):
    names = ("mix_norm", "w_in", "a_v_norm_g", "a_v_norm_b", "a_spatial_w", "a_spatial_b", "q_a_norm", "w_uq",
             "kv_a_norm", "w_ukv", "w_out", "ffn_norm", "w_up", "conv_w", "conv_b", "w_down", "final_norm")
    w = dict(zip(names, (mix_norm, w_in, a_v_norm_g, a_v_norm_b, a_spatial_w, a_spatial_b, q_a_norm, w_uq,
                         kv_a_norm, w_ukv, w_out, ffn_norm, w_up, conv_w, conv_b, w_down, final_norm)))
    m = dict(zip(names, (m_mix_norm, m_w_in, m_a_v_norm_g, m_a_v_norm_b, m_a_spatial_w, m_a_spatial_b,
                         m_q_a_norm, m_w_uq, m_kv_a_norm, m_w_ukv, m_w_out, m_ffn_norm, m_w_up, m_conv_w,
                         m_conv_b, m_w_down, m_final_norm)))
    v = dict(zip(names, (v_mix_norm, v_w_in, v_a_v_norm_g, v_a_v_norm_b, v_a_spatial_w, v_a_spatial_b,
                         v_q_a_norm, v_w_uq, v_kv_a_norm, v_w_ukv, v_w_out, v_ffn_norm, v_w_up, v_conv_w,
                         v_conv_b, v_w_down, v_final_norm)))
    shapes = {n: w[n].shape for n in names}
    def view(tree, n):
        a = tree[n].reshape(tree[n].shape[-2:])
        return a.T if n in _Comm.TRANSPOSED else a

    comm = _Comm({n: view(w, n) for n in ("w_in",) + _Comm.GATHER_GROUPS[0] + _Comm.GATHER_GROUPS[1]})

    grad_x, token = _local_step(
        x, positions, loss_target, w["mix_norm"], w["a_v_norm_g"], w["a_v_norm_b"], w["a_spatial_w"][0],
        w["a_spatial_b"][0], w["q_a_norm"], w["kv_a_norm"], w["ffn_norm"], w["conv_b"],
        w["final_norm"].reshape(1, D_MODEL), comm)

    out_g, out_d, out_m, out_v = {}, {}, {}, {}

    def update(n, parts, late=None):
        if n == "conv_w":
            rows = lambda t: t[n].reshape(t[n].shape[-2], 1, t[n].shape[-1])
            res = _adamw_rows(parts, rows(w), rows(m), rows(v), "adamw_" + n)
        else:
            res = _adamw(parts, view(w, n), view(m, n), view(v, n), "adamw_" + n, late=late)
        out_g[n], out_d[n], out_m[n], out_v[n] = (
            (t.T if n in _Comm.TRANSPOSED else t).reshape(shapes[n]) for t in res)
        return res[1]

    def update_small(names, parts, sums, name):
        res = _adamw_many(parts, *[[_small_2d(t[n]) for n in names] for t in (w, m, v)], sums, name)
        for i, n in enumerate(names):
            out_g[n], out_d[n], out_m[n], out_v[n] = (r[i].reshape(shapes[n]) for r in res[:4])
        return res

    for n, parts in zip(_Comm.FFN_GRADS, _wait_copies(comm.h_ffn_grads, True, token, "ffn_grads_wait")):
        last = update(n, parts)
    early = _wait_copies(comm.h_small_early, False, last, "small_grads_wait")
    last = update_small(SMALL_EARLY, early, [], "adamw_small")[1][0]
    (in_parts,) = _wait_copies(comm.h_in_grads, True, last, "in_grads_wait")
    uq_parts, ukv_parts, zs_parts = _wait_copies(comm.h_mla_grads, True, in_parts, "mla_grads_wait")
    last = update("w_in", in_parts, late=(zs_parts, [(dev, row) for dev, _, row in ZS_PIECES]))
    last = update("w_uq", uq_parts)
    last = update("w_ukv", ukv_parts)
    late = _wait_copies(comm.h_late_small, False, last, "late_small_wait")
    res = update_small(SMALL_LATE, late[:-1], late[-1:], "adamw_late")
    loss = res[4][0][0, 0]

    return (loss, grad_x, *[out_g[n] for n in names], *[out_d[n] for n in names],
            *[out_m[n] for n in names], *[out_v[n] for n in names])
```

```python
import math

import jax
import jax.numpy as jnp
from jax import lax
from jax.experimental import pallas as pl
from jax.experimental.pallas import tpu as pltpu

F32 = jnp.float32
BF16 = jnp.bfloat16
KEPT = jnp.bfloat16

N_DEV = 8
D_MODEL = 1024
EPS = 1e-6
A_GROUPS = 8
CHUNK = 128
MLA_HEADS = 8
QK_NOPE = 128
QK_ROPE = 64
QK_HEAD = QK_NOPE + QK_ROPE
HEAD_PAD = 256
V_HEAD = 128
Q_LORA = 256
KV_LORA = 128
ROPE_THETA = 10000.0
D_FF = 2816
ZS_W = 512
ATTN_SCALE = QK_HEAD ** -0.5
ATTN_TILE = 512
NEG_BIG = -1e30

ADAM_LR = 0.001
ADAM_B1 = 0.9
ADAM_B2 = 0.999
ADAM_EPS = 1e-08
ADAM_WD = 0.01
ADAM_STEP = 10

VMEM_LIMIT = 56 * 1024 * 1024
SMALL_BLOCK_BYTES = 5 * 1024 * 1024
LANES = 128
SUBLANES = 8

GELU_K = math.sqrt(2.0 / math.pi)
GELU_C = 0.044715

ANY = pl.BlockSpec(memory_space=pl.ANY)
HBM = pl.BlockSpec(memory_space=pltpu.HBM)
SEM = pl.BlockSpec(memory_space=pltpu.SEMAPHORE)


def _tile(n, pref):
    for t in (pref, 512, 256, 128, 64, 32, 16, 8):
        if t <= pref and n % t == 0:
            return t
    return n


def _wide_tile(n, cap=1408):
    return next((t for t in range(min(n, cap) // LANES * LANES, 0, -LANES) if n % t == 0), n)


def _params(*sem):
    return pltpu.CompilerParams(dimension_semantics=sem, vmem_limit_bytes=VMEM_LIMIT)


def _dot(a, b):
    return jnp.dot(a, b, preferred_element_type=F32)


def _dot_nt(a, b):
    return lax.dot_general(a, b, (((1,), (1,)), ((), ())), preferred_element_type=F32)


def _dot_tn(a, b):
    return lax.dot_general(a, b, (((0,), (0,)), ((), ())), preferred_element_type=F32)


def _sigmoid(x):
    return 1.0 / (1.0 + jnp.exp(-x))


def _gelu(x):
    t = jnp.tanh(GELU_K * (x + GELU_C * x * x * x))
    return 0.5 * x * (1.0 + t)


def _gelu_and_grad(x):
    x2 = x * x
    t = jnp.tanh(GELU_K * (x + GELU_C * x * x2))
    half = 0.5 * (1.0 + t)
    return x * half, half + 0.5 * x * (1.0 - t * t) * GELU_K * (1.0 + 3.0 * GELU_C * x2)


def _in_proj(x, g, wt):
    T, Dm = x.shape
    tm = _tile(T, 512)

    def body(x_ref, g_ref, wt_ref, h_ref, zm_ref, zs_ref):
        xf = x_ref[...]
        r = lax.rsqrt(jnp.mean(xf * xf, axis=-1, keepdims=True) + EPS)
        h = (xf * r * g_ref[...]).astype(BF16)
        h_ref[...] = h
        for i, (r0, r1) in enumerate(IN_ROWS_MAIN):
            zm_ref[:, i * D_MODEL:(i + 1) * D_MODEL] = _dot_nt(h, wt_ref[r0:r1, :]).astype(KEPT)
        zs_ref[...] = _dot_nt(h, wt_ref[IN_ROWS_ZS[0]:IN_ROWS_ZS[1], :])

    row = lambda n: pl.BlockSpec((tm, n), lambda i: (i, 0))
    return pl.pallas_call(
        body, grid=(T // tm,),
        in_specs=[row(Dm), pl.BlockSpec((1, Dm), lambda i: (0, 0)), pl.BlockSpec(wt.shape, lambda i: (0, 0))],
        out_specs=[row(Dm), row(4 * D_MODEL), row(ZS_W)],
        out_shape=[jax.ShapeDtypeStruct((T, Dm), BF16), jax.ShapeDtypeStruct((T, 4 * D_MODEL), KEPT),
                   jax.ShapeDtypeStruct((T, ZS_W), F32)],
        name="in_proj", compiler_params=_params("parallel"))(x, g, wt)


def _proj_bwd(acts, wt, terms, x, g, dres, name, w2=None, dep=None, rows=256):
    T, Dm = x.shape
    tm = _tile(T, rows)
    n_a = len(acts)

    def body(*refs):
        ins, outs = refs[:n_a + 4 + (w2 is not None) + (dep is not None)], refs[-2 - (w2 is not None):]
        wt_ref, x_ref, g_ref, dres_ref = ins[n_a:n_a + 4]
        dx_ref, dg_ref = outs[0], outs[1]

        @pl.when(pl.program_id(0) == 0)
        def _():
            dg_ref[...] = jnp.zeros_like(dg_ref)

        dy = None
        for i, (c0, c1), (r0, r1) in terms:
            t = _dot(ins[i][:, c0:c1], wt_ref[r0:r1, :])
            dy = t if dy is None else dy + t
        xf = x_ref[...]
        r = lax.rsqrt(jnp.mean(xf * xf, axis=-1, keepdims=True) + EPS)
        xh = xf * r
        dg_ref[...] += jnp.sum(dy * xh, axis=0, keepdims=True)
        dxh = dy * g_ref[...]
        dx = dres_ref[...] + r * (dxh - xh * jnp.mean(dxh * xh, axis=-1, keepdims=True))
        dx_ref[...] = dx
        if w2 is not None:
            outs[2][...] = _dot_nt(dx.astype(BF16), ins[n_a + 4][...]).astype(KEPT)

    row = pl.BlockSpec((tm, Dm), lambda i: (i, 0))
    vec = pl.BlockSpec((1, Dm), lambda i: (0, 0))
    in_specs = [pl.BlockSpec((tm, a.shape[1]), lambda i: (i, 0)) for a in acts]
    in_specs += [pl.BlockSpec(wt.shape, lambda i: (0, 0)), row, vec, row]
    args = [*acts, wt, x, g, dres]
    out_specs = [row, vec]
    out_shape = [jax.ShapeDtypeStruct((T, Dm), F32), jax.ShapeDtypeStruct((1, Dm), F32)]
    if w2 is not None:
        in_specs.append(pl.BlockSpec(w2.shape, lambda i: (0, 0)))
        args.append(w2)
        out_specs.append(pl.BlockSpec((tm, w2.shape[0]), lambda i: (i, 0)))
        out_shape.append(jax.ShapeDtypeStruct((T, w2.shape[0]), KEPT))
    if dep is not None:
        in_specs.append(ANY)
        args.append(dep)
    return pl.pallas_call(
        body, grid=(T // tm,), in_specs=in_specs, out_specs=out_specs, out_shape=out_shape,
        name=name, compiler_params=_params("arbitrary"))(*args)


def _mm_tn(a, b, name, dep=None, rows=None, row0=0, into=None, gap=None):
    T, M = a.shape
    N = b.shape[1]
    tm, tn, tt = _wide_tile(M), _wide_tile(N), _tile(T, 2048)
    n_t = T // tt
    off = row0 // tm
    extra = ([dep] if dep is not None else []) + ([into] if into is not None else [])
    if gap is None:
        out_spec = pl.BlockSpec((tm, tn), lambda i, j, t: (i + off, j))
    else:
        unit = 16
        assert row0 == 0 and gap[0] % tm == 0 and tm % unit == 0 and gap[1] % unit == 0
        out_spec = pl.BlockSpec(
            (pl.Element(tm), pl.Element(tn)),
            lambda i, j, t: ((i * (tm // unit) + jnp.where(i * tm >= gap[0], gap[1] // unit, 0)) * unit, j * tn))

    def body(a_ref, b_ref, *refs):
        o_ref, acc_ref = refs[-2:]
        t = pl.program_id(2)

        @pl.when(t == 0)
        def _():
            acc_ref[...] = jnp.zeros_like(acc_ref)

        acc_ref[...] += _dot_tn(a_ref[...].astype(BF16), b_ref[...].astype(BF16))

        @pl.when(t == n_t - 1)
        def _():
            o_ref[...] = acc_ref[...].astype(BF16)

    return pl.pallas_call(
        body, grid=(M // tm, N // tn, n_t),
        in_specs=[pl.BlockSpec((tt, tm), lambda i, j, t: (t, i)),
                  pl.BlockSpec((tt, tn), lambda i, j, t: (t, j))] + [ANY] * len(extra),
        out_specs=out_spec,
        out_shape=jax.ShapeDtypeStruct((rows or M, N), BF16),
        scratch_shapes=[pltpu.VMEM((tm, tn), F32)],
        input_output_aliases={} if into is None else {1 + len(extra): 0},
        name=name, compiler_params=_params("parallel", "parallel", "arbitrary"))(a, b, *extra)


def _layer_norm_fwd(gv, g, b):
    mu = jnp.mean(gv, axis=-1, keepdims=True)
    xc = gv - mu
    rs = lax.rsqrt(jnp.mean(xc * xc, axis=-1, keepdims=True) + EPS)
    xh = xc * rs
    return xh, rs, xh * g + b


def _tri_mask(transposed=False):
    r = lax.broadcasted_iota(jnp.int32, (CHUNK, CHUNK), 0)
    c = lax.broadcasted_iota(jnp.int32, (CHUNK, CHUNK), 1)
    return r <= c if transposed else c <= r


def _mixer_a_fwd(zm, av_g, av_b, w_s, b_col):
    T = zm.shape[0]
    tm = _tile(T, 512)
    n_chunk = tm // CHUNK

    def body(u_ref, v_ref, ga_ref, g_ref, b_ref, w_ref, bc_ref, y_ref, vn_s, mx_s):
        gu = _gelu(u_ref[...].astype(F32))
        _, _, vn = _layer_norm_fwd(_gelu(v_ref[...].astype(F32)), g_ref[...], b_ref[...])
        vn_s[...] = vn.astype(BF16)
        tri = _tri_mask()
        for gi in range(A_GROUPS):
            wm = jnp.where(tri, w_ref[gi], 0.0).astype(BF16)
            cols = slice(gi * CHUNK, (gi + 1) * CHUNK)
            for n in range(n_chunk):
                rows = slice(n * CHUNK, (n + 1) * CHUNK)
                mx_s[rows, cols] = _dot(wm, vn_s[rows, cols]) + bc_ref[gi]
        y_ref[...] = (_sigmoid(ga_ref[...].astype(F32)) * gu * mx_s[...]).astype(KEPT)

    col = lambda c: pl.BlockSpec((tm, D_MODEL), lambda i: (i, c))
    vec = pl.BlockSpec((1, D_MODEL), lambda i: (0, 0))
    return pl.pallas_call(
        body, grid=(T // tm,),
        in_specs=[col(0), col(1), col(2), vec, vec,
                  pl.BlockSpec((A_GROUPS, CHUNK, CHUNK), lambda i: (0, 0, 0)),
                  pl.BlockSpec((A_GROUPS, CHUNK, 1), lambda i: (0, 0, 0))],
        out_specs=pl.BlockSpec((tm, D_MODEL), lambda i: (i, 0)),
        out_shape=jax.ShapeDtypeStruct((T, D_MODEL), KEPT),
        scratch_shapes=[pltpu.VMEM((tm, D_MODEL), BF16), pltpu.VMEM((tm, D_MODEL), F32)],
        name="mixer_a_fwd", compiler_params=_params("parallel"))(zm, zm, zm, av_g, av_b, w_s, b_col)


def _mixer_bwd(zm, o, dm, av_g, av_b, w_s, w_st, b_col, dep):
    T = zm.shape[0]
    tm = _tile(T, 512)
    n_chunk = tm // CHUNK

    def body(u_ref, v_ref, ga_ref, gb_ref, o_ref, dm_ref, g_ref, b_ref, w_ref, wt_ref, bc_ref, dep_ref,
             dz_ref, do_ref, dg_ref, db_ref, dw_ref, dbs_ref, vn_s, mx_s, dmx_s, dvn_s):
        @pl.when(pl.program_id(0) == 0)
        def _():
            dg_ref[...] = jnp.zeros_like(dg_ref)
            db_ref[...] = jnp.zeros_like(db_ref)
            dw_ref[...] = jnp.zeros_like(dw_ref)
            dbs_ref[...] = jnp.zeros_like(dbs_ref)

        dm_v = dm_ref[...].astype(F32)
        gb = gb_ref[...].astype(F32)
        sb = _sigmoid(gb)
        o_v = o_ref[...].astype(F32)
        do_ref[...] = (dm_v * sb).astype(BF16)
        dz_ref[:, 3 * D_MODEL:4 * D_MODEL] = (dm_v * o_v * sb * (1.0 - sb)).astype(BF16)
        u = u_ref[...].astype(F32)
        v = v_ref[...].astype(F32)
        gu, gu_grad = _gelu_and_grad(u)
        gv, gv_grad = _gelu_and_grad(v)
        xh, rs, vn = _layer_norm_fwd(gv, g_ref[...], b_ref[...])
        vn_s[...] = vn.astype(BF16)
        tri = _tri_mask()
        for gi in range(A_GROUPS):
            wm = jnp.where(tri, w_ref[gi], 0.0).astype(BF16)
            cols = slice(gi * CHUNK, (gi + 1) * CHUNK)
            for n in range(n_chunk):
                rows = slice(n * CHUNK, (n + 1) * CHUNK)
                mx_s[rows, cols] = _dot(wm, vn_s[rows, cols]) + bc_ref[gi]
        mixed = mx_s[...]
        sa = _sigmoid(ga_ref[...].astype(F32))
        dya = dm_v * sa
        dz_ref[:, 2 * D_MODEL:3 * D_MODEL] = (dm_v * gu * mixed * sa * (1.0 - sa)).astype(BF16)
        dz_ref[:, 0:D_MODEL] = (dya * mixed * gu_grad).astype(BF16)
        dmx = dya * gu
        dmx_s[...] = dmx.astype(BF16)
        tri_t = _tri_mask(transposed=True)
        for gi in range(A_GROUPS):
            wmt = jnp.where(tri_t, wt_ref[gi], 0.0).astype(BF16)
            cols = slice(gi * CHUNK, (gi + 1) * CHUNK)
            dw_acc = jnp.zeros((CHUNK, CHUNK), F32)
            dmx_sum = jnp.zeros((CHUNK, CHUNK), F32)
            for n in range(n_chunk):
                rows = slice(n * CHUNK, (n + 1) * CHUNK)
                blk = dmx_s[rows, cols]
                dvn_s[rows, cols] = _dot(wmt, blk)
                dw_acc = dw_acc + _dot_nt(blk, vn_s[rows, cols])
                dmx_sum = dmx_sum + dmx[rows, cols]
            dw_ref[gi] += jnp.where(tri, dw_acc, 0.0)
            dbs_ref[gi] += jnp.sum(dmx_sum, axis=-1, keepdims=True)
        dvn = dvn_s[...]
        dg_ref[...] += jnp.sum(dvn * xh, axis=0, keepdims=True)
        db_ref[...] += jnp.sum(dvn, axis=0, keepdims=True)
        dxh = dvn * g_ref[...]
        dgv = rs * (dxh - jnp.mean(dxh, axis=-1, keepdims=True)
                    - xh * jnp.mean(dxh * xh, axis=-1, keepdims=True))
        dz_ref[:, D_MODEL:2 * D_MODEL] = (dgv * gv_grad).astype(BF16)

    col = lambda c: pl.BlockSpec((tm, D_MODEL), lambda i: (i, c))
    row = pl.BlockSpec((tm, D_MODEL), lambda i: (i, 0))
    vec = pl.BlockSpec((1, D_MODEL), lambda i: (0, 0))
    wsp = pl.BlockSpec((A_GROUPS, CHUNK, CHUNK), lambda i: (0, 0, 0))
    bsp = pl.BlockSpec((A_GROUPS, CHUNK, 1), lambda i: (0, 0, 0))
    return pl.pallas_call(
        body, grid=(T // tm,),
        in_specs=[col(0), col(1), col(2), col(3), row, row, vec, vec, wsp, wsp, bsp, ANY],
        out_specs=[pl.BlockSpec((tm, 4 * D_MODEL), lambda i: (i, 0)), row, vec, vec, wsp, bsp],
        out_shape=[jax.ShapeDtypeStruct((T, 4 * D_MODEL), BF16), jax.ShapeDtypeStruct((T, D_MODEL), BF16),
                   jax.ShapeDtypeStruct((1, D_MODEL), F32), jax.ShapeDtypeStruct((1, D_MODEL), F32),
                   jax.ShapeDtypeStruct((A_GROUPS, CHUNK, CHUNK), F32),
                   jax.ShapeDtypeStruct((A_GROUPS, CHUNK, 1), F32)],
        scratch_shapes=[pltpu.VMEM((tm, D_MODEL), BF16), pltpu.VMEM((tm, D_MODEL), F32),
                        pltpu.VMEM((tm, D_MODEL), BF16), pltpu.VMEM((tm, D_MODEL), F32)],
        name="mixer_bwd", compiler_params=_params("arbitrary"))(
            zm, zm, zm, zm, o, dm, av_g, av_b, w_s, w_st, b_col, dep)


def _rope_tables(pos_ref, invf_ref):
    ang = pos_ref[...].astype(F32) * invf_ref[...]
    lane = lax.broadcasted_iota(jnp.int32, ang.shape, 1)
    cos, sin = jnp.cos(ang), jnp.sin(ang)
    c = jnp.where(lane < QK_ROPE, cos, 0.0)
    sa = jnp.where(lane < QK_ROPE // 2, -sin, 0.0)
    sb = jnp.where((lane >= QK_ROPE // 2) & (lane < QK_ROPE), sin, 0.0)
    return c, sa, sb


def _rope(blk, tabs):
    c, sa, sb = tabs
    return blk * c + pltpu.roll(blk, LANES - QK_ROPE // 2, 1) * sa + pltpu.roll(blk, QK_ROPE // 2, 1) * sb


def _rope_t(dout, tabs):
    c, sa, sb = tabs
    return dout * c + pltpu.roll(dout * sa, QK_ROPE // 2, 1) + pltpu.roll(dout * sb, LANES - QK_ROPE // 2, 1)


def _rms_small(x, g):
    r = lax.rsqrt(jnp.mean(x * x, axis=-1, keepdims=True) + EPS)
    xh = x * r
    return xh, r, xh * g


def _mla_prep_fwd(zs, pos, invf, qg, kvg, wuq_p, wukv):
    T = zs.shape[0]
    tm = _tile(T, 512)
    HW = MLA_HEADS * HEAD_PAD

    def body(zs_ref, pos_ref, invf_ref, qg_ref, kvg_ref, wq_ref, wkv_ref, q_ref, k_ref, v_ref):
        tabs = _rope_tables(pos_ref, invf_ref)
        _, _, cqn = _rms_small(zs_ref[:, 0:Q_LORA], qg_ref[...])
        _, _, ckvn = _rms_small(zs_ref[:, Q_LORA:Q_LORA + KV_LORA], kvg_ref[...])
        q = _dot_nt(cqn.astype(BF16), wq_ref[...]) * ATTN_SCALE
        kv = _dot(ckvn.astype(BF16), wkv_ref[...])
        kr = _rope(zs_ref[:, Q_LORA + KV_LORA:ZS_W], tabs).astype(BF16)
        for h in range(MLA_HEADS):
            b0 = h * HEAD_PAD
            q_ref[:, b0:b0 + QK_NOPE] = q[:, b0:b0 + QK_NOPE].astype(BF16)
            q_ref[:, b0 + QK_NOPE:b0 + HEAD_PAD] = _rope(q[:, b0 + QK_NOPE:b0 + HEAD_PAD], tabs).astype(BF16)
            k_ref[:, b0:b0 + QK_NOPE] = kv[:, b0:b0 + QK_NOPE].astype(BF16)
            k_ref[:, b0 + QK_NOPE:b0 + HEAD_PAD] = kr
            v_ref[:, h * V_HEAD:(h + 1) * V_HEAD] = kv[:, b0 + QK_NOPE:b0 + HEAD_PAD].astype(BF16)

    full = lambda a: pl.BlockSpec(a.shape, lambda i: (0,) * a.ndim)
    return pl.pallas_call(
        body, grid=(T // tm,),
        in_specs=[pl.BlockSpec((tm, ZS_W), lambda i: (i, 0)), pl.BlockSpec((tm, 1), lambda i: (i, 0)),
                  full(invf), full(qg), full(kvg), full(wuq_p), full(wukv)],
        out_specs=[pl.BlockSpec((tm, HW), lambda i: (i, 0)), pl.BlockSpec((tm, HW), lambda i: (i, 0)),
                   pl.BlockSpec((tm, D_MODEL), lambda i: (i, 0))],
        out_shape=[jax.ShapeDtypeStruct((T, HW), BF16), jax.ShapeDtypeStruct((T, HW), BF16),
                   jax.ShapeDtypeStruct((T, D_MODEL), BF16)],
        name="mla_prep_fwd", compiler_params=_params("parallel"))(zs, pos, invf, qg, kvg, wuq_p, wukv)


def _mla_prep_bwd(zs, pos, invf, qg, kvg, wuq_p, wukv, dq, dk, dv):
    T = zs.shape[0]
    tm = _tile(T, 512)
    n_t = T // tm
    HW = MLA_HEADS * HEAD_PAD

    def body(zs_ref, pos_ref, invf_ref, qg_ref, kvg_ref, wq_ref, wkv_ref, dq_ref, dk_ref, dv_ref,
             dzs_ref, dwq_ref, dwkv_ref, dqg_ref, dkvg_ref, dqp_ref, dkv_ref, accq_ref, acckv_ref):
        @pl.when(pl.program_id(0) == 0)
        def _():
            dqg_ref[...] = jnp.zeros_like(dqg_ref)
            dkvg_ref[...] = jnp.zeros_like(dkvg_ref)
            accq_ref[...] = jnp.zeros_like(accq_ref)
            acckv_ref[...] = jnp.zeros_like(acckv_ref)

        tabs = _rope_tables(pos_ref, invf_ref)
        cqh, rq, cqn = _rms_small(zs_ref[:, 0:Q_LORA], qg_ref[...])
        ckvh, rkv, ckvn = _rms_small(zs_ref[:, Q_LORA:Q_LORA + KV_LORA], kvg_ref[...])
        dkr = jnp.zeros((tm, LANES), F32)
        for h in range(MLA_HEADS):
            b0 = h * HEAD_PAD
            dqp_ref[:, b0:b0 + QK_NOPE] = dq_ref[:, b0:b0 + QK_NOPE]
            dqp_ref[:, b0 + QK_NOPE:b0 + HEAD_PAD] = _rope_t(
                dq_ref[:, b0 + QK_NOPE:b0 + HEAD_PAD].astype(F32), tabs).astype(BF16)
            dkv_ref[:, b0:b0 + QK_NOPE] = dk_ref[:, b0:b0 + QK_NOPE]
            dkv_ref[:, b0 + QK_NOPE:b0 + HEAD_PAD] = dv_ref[:, h * V_HEAD:(h + 1) * V_HEAD]
            dkr = dkr + dk_ref[:, b0 + QK_NOPE:b0 + HEAD_PAD].astype(F32)
        accq_ref[...] += _dot_tn(dqp_ref[...], cqn.astype(BF16))
        acckv_ref[...] += _dot_tn(ckvn.astype(BF16), dkv_ref[...])

        @pl.when(pl.program_id(0) == n_t - 1)
        def _():
            dwq_ref[...] = accq_ref[...].astype(BF16)
            dwkv_ref[...] = acckv_ref[...].astype(BF16)

        dcqn = _dot(dqp_ref[...], wq_ref[...])
        dckvn = _dot_nt(dkv_ref[...], wkv_ref[...])
        dqg_ref[...] += jnp.sum(dcqn * cqh, axis=0, keepdims=True)
        dkvg_ref[...] += jnp.sum(dckvn * ckvh, axis=0, keepdims=True)
        dxh = dcqn * qg_ref[...]
        dzs_ref[:, 0:Q_LORA] = (rq * (dxh - cqh * jnp.mean(dxh * cqh, axis=-1, keepdims=True))).astype(BF16)
        dxh = dckvn * kvg_ref[...]
        dzs_ref[:, Q_LORA:Q_LORA + KV_LORA] = (
            rkv * (dxh - ckvh * jnp.mean(dxh * ckvh, axis=-1, keepdims=True))).astype(BF16)
        dzs_ref[:, Q_LORA + KV_LORA:ZS_W] = _rope_t(dkr, tabs).astype(BF16)

    full = lambda a: pl.BlockSpec(a.shape, lambda i: (0,) * a.ndim)
    rowb = lambda w: pl.BlockSpec((tm, w), lambda i: (i, 0))
    return pl.pallas_call(
        body, grid=(T // tm,),
        in_specs=[rowb(ZS_W), rowb(1), full(invf), full(qg), full(kvg), full(wuq_p), full(wukv),
                  rowb(HW), rowb(HW), rowb(D_MODEL)],
        out_specs=[rowb(ZS_W), full(wuq_p), full(wukv), full(qg), full(kvg)],
        out_shape=[jax.ShapeDtypeStruct((T, ZS_W), BF16), jax.ShapeDtypeStruct(wuq_p.shape, BF16),
                   jax.ShapeDtypeStruct(wukv.shape, BF16), jax.ShapeDtypeStruct(qg.shape, F32),
                   jax.ShapeDtypeStruct(kvg.shape, F32)],
        scratch_shapes=[pltpu.VMEM((tm, HW), BF16), pltpu.VMEM((tm, HW), BF16),
                        pltpu.VMEM(wuq_p.shape, F32), pltpu.VMEM(wukv.shape, F32)],
        name="mla_prep_bwd", compiler_params=_params("arbitrary"))(
            zs, pos, invf, qg, kvg, wuq_p, wukv, dq, dk, dv)


def _causal(tq, kmax, q0):
    r = lax.broadcasted_iota(jnp.int32, (tq, kmax), 0) + q0
    c = lax.broadcasted_iota(jnp.int32, (tq, kmax), 1)
    return c <= r


def _attn_fwd(q, k, v, batch, seq):
    tq = _tile(seq, ATTN_TILE)
    nq = seq // tq

    def body(q_ref, k_ref, v_ref, o_ref, lse_ref):
        diag = _causal(tq, tq, 0)
        for qi in range(nq):
            rows = slice(qi * tq, (qi + 1) * tq)
            qr = q_ref[rows, :]
            s_d = jnp.where(diag, _dot_nt(qr, k_ref[rows, :]), NEG_BIG)
            m = jnp.max(s_d, axis=-1, keepdims=True)
            if qi > 0:
                before = slice(0, qi * tq)
                s_b = _dot_nt(qr, k_ref[before, :])
                m = jnp.maximum(m, jnp.max(s_b, axis=-1, keepdims=True))
                p_b = jnp.exp(s_b - m)
                l = jnp.sum(p_b, axis=-1, keepdims=True)
                acc = _dot(p_b.astype(BF16), v_ref[before, :])
            p_d = jnp.exp(s_d - m)
            l_d = jnp.sum(p_d, axis=-1, keepdims=True)
            acc_d = _dot(p_d.astype(BF16), v_ref[rows, :])
            l, acc = (l + l_d, acc + acc_d) if qi > 0 else (l_d, acc_d)
            o_ref[rows, :] = (acc / l).astype(KEPT)
            lse_ref[rows, :] = jnp.broadcast_to(m + jnp.log(l), (tq, V_HEAD))

    return pl.pallas_call(
        body, grid=(batch, MLA_HEADS),
        in_specs=[pl.BlockSpec((seq, HEAD_PAD), lambda b, h: (b, h)),
                  pl.BlockSpec((seq, HEAD_PAD), lambda b, h: (b, h)),
                  pl.BlockSpec((seq, V_HEAD), lambda b, h: (b, h))],
        out_specs=[pl.BlockSpec((seq, V_HEAD), lambda b, h: (b, h)),
                   pl.BlockSpec((seq, V_HEAD), lambda b, h: (b, h))],
        out_shape=[jax.ShapeDtypeStruct((batch * seq, D_MODEL), KEPT),
                   jax.ShapeDtypeStruct((batch * seq, D_MODEL), F32)],
        name="attn_fwd", compiler_params=_params("parallel", "parallel"))(q, k, v)


def _attn_bwd(q, k, v, o, do, lse, batch, seq, dep):
    tq = _tile(seq, ATTN_TILE)
    nq = seq // tq

    def body(q_ref, k_ref, v_ref, o_ref, do_ref, lse_ref, dep_ref, dq_ref, dk_ref, dv_ref, dk_acc, dv_acc):
        dk_acc[...] = jnp.zeros_like(dk_acc)
        dv_acc[...] = jnp.zeros_like(dv_acc)
        for qi in range(nq):
            rows = slice(qi * tq, (qi + 1) * tq)
            kmax = (qi + 1) * tq
            qr = q_ref[rows, :]
            dor = do_ref[rows, :]
            kk = k_ref[0:kmax, :]
            s = _dot_nt(qr, kk)
            p = jnp.where(_causal(tq, kmax, qi * tq), jnp.exp(s - lse_ref[rows, 0:1]), 0.0)
            dp = _dot_nt(dor, v_ref[0:kmax, :])
            delta = jnp.sum(dor.astype(F32) * o_ref[rows, :].astype(F32), axis=-1, keepdims=True)
            ds = (p * (dp - delta)).astype(BF16)
            dq_ref[rows, :] = (_dot(ds, kk) * ATTN_SCALE).astype(BF16)
            dk_acc[0:kmax, :] += _dot_tn(ds, qr)
            dv_acc[0:kmax, :] += _dot_tn(p.astype(BF16), dor)
        dk_ref[...] = dk_acc[...].astype(BF16)
        dv_ref[...] = dv_acc[...].astype(BF16)

    qspec = pl.BlockSpec((seq, HEAD_PAD), lambda b, h: (b, h))
    vspec = pl.BlockSpec((seq, V_HEAD), lambda b, h: (b, h))
    T = batch * seq
    return pl.pallas_call(
        body, grid=(batch, MLA_HEADS),
        in_specs=[qspec, qspec, vspec, vspec, vspec, vspec, ANY],
        out_specs=[qspec, qspec, vspec],
        out_shape=[jax.ShapeDtypeStruct((T, MLA_HEADS * HEAD_PAD), BF16),
                   jax.ShapeDtypeStruct((T, MLA_HEADS * HEAD_PAD), BF16),
                   jax.ShapeDtypeStruct((T, D_MODEL), BF16)],
        scratch_shapes=[pltpu.VMEM((seq, HEAD_PAD), F32), pltpu.VMEM((seq, V_HEAD), F32)],
        name="attn_bwd", compiler_params=_params("parallel", "parallel"))(q, k, v, o, do, lse, dep)


def _merge_out(x, yag, zm, o, w_out, ffn_g):
    T = x.shape[0]
    tm = _tile(T, 512)

    def body(x_ref, ya_ref, gb_ref, o_ref, w_ref, g_ref, mg_ref, x1_ref, h2_ref):
        mg = (ya_ref[...].astype(F32) + _sigmoid(gb_ref[...].astype(F32)) * o_ref[...].astype(F32)).astype(BF16)
        mg_ref[...] = mg
        x1 = x_ref[...] + _dot(mg, w_ref[...])
        x1_ref[...] = x1
        r = lax.rsqrt(jnp.mean(x1 * x1, axis=-1, keepdims=True) + EPS)
        h2_ref[...] = (x1 * r * g_ref[...]).astype(BF16)

    row = pl.BlockSpec((tm, D_MODEL), lambda i: (i, 0))
    return pl.pallas_call(
        body, grid=(T // tm,),
        in_specs=[row, row, pl.BlockSpec((tm, D_MODEL), lambda i: (i, 3)), row,
                  pl.BlockSpec((D_MODEL, D_MODEL), lambda i: (0, 0)), pl.BlockSpec((1, D_MODEL), lambda i: (0, 0))],
        out_specs=[row, row, row],
        out_shape=[jax.ShapeDtypeStruct((T, D_MODEL), BF16), jax.ShapeDtypeStruct((T, D_MODEL), F32),
                   jax.ShapeDtypeStruct((T, D_MODEL), BF16)],
        name="merge_out", compiler_params=_params("parallel"))(x, yag, zm, o, w_out, ffn_g)


FF_TILE = 256
FF_BLOCKS = D_FF // FF_TILE
FFB_TILE = 256
UP_ROWS = 512
EDGE = 16


def _shift_up(x, k):
    n = x.shape[0]
    row = lax.broadcasted_iota(jnp.int32, x.shape, 0)
    return jnp.where(row < n - k, pltpu.roll(x, n - k, 0), 0.0)


def _up_act(h2, wt_up, cw, cb, batch, seq):
    def body(h_ref, wug_ref, wuv_ref, wg_ref, wv_ref, bg_ref, bv_ref, ug_ref, uv_ref, g_ref, v_ref, a_ref,
             ug_s, uv_s):
        for s in (ug_s, uv_s):
            s[0:SUBLANES, :] = jnp.zeros((SUBLANES, FF_TILE), F32)

        def conv(s, w_ref, b_ref, r0):
            return (b_ref[...] + w_ref[2:3, :] * s[r0:r0 + UP_ROWS, :]
                    + w_ref[1:2, :] * s[r0 - 1:r0 - 1 + UP_ROWS, :]
                    + w_ref[0:1, :] * s[r0 - 2:r0 - 2 + UP_ROWS, :])

        for c in range(seq // UP_ROWS):
            rows = slice(c * UP_ROWS, (c + 1) * UP_ROWS)
            r0 = SUBLANES + c * UP_ROWS
            h = h_ref[rows, :]
            for w_ref, u_ref, s in ((wug_ref, ug_ref, ug_s), (wuv_ref, uv_ref, uv_s)):
                u = _dot_nt(h, w_ref[...])
                u_ref[rows, :] = u.astype(KEPT)
                s[r0:r0 + UP_ROWS, :] = u
            gate, val = conv(ug_s, wg_ref, bg_ref, r0), conv(uv_s, wv_ref, bv_ref, r0)
            g_ref[rows, :] = gate.astype(KEPT)
            v_ref[rows, :] = val.astype(KEPT)
            a_ref[rows, :] = (gate * _sigmoid(gate) * val).astype(BF16)

    blk = pl.BlockSpec((seq, FF_TILE), lambda b, j: (b, j))
    wup = lambda off: pl.BlockSpec((FF_TILE, D_MODEL), lambda b, j: (j + off, 0))
    wsp = lambda off: pl.BlockSpec((3, FF_TILE), lambda b, j: (0, j + off))
    bsp = lambda off: pl.BlockSpec((1, FF_TILE), lambda b, j: (0, j + off))
    T = batch * seq
    kept = jax.ShapeDtypeStruct((T, D_FF), KEPT)
    return pl.pallas_call(
        body, grid=(batch, FF_BLOCKS),
        in_specs=[pl.BlockSpec((seq, D_MODEL), lambda b, j: (b, 0)), wup(0), wup(FF_BLOCKS),
                  wsp(0), wsp(FF_BLOCKS), bsp(0), bsp(FF_BLOCKS)],
        out_specs=[blk] * 5,
        out_shape=[kept, kept, kept, kept, jax.ShapeDtypeStruct((T, D_FF), BF16)],
        scratch_shapes=[pltpu.VMEM((SUBLANES + seq, FF_TILE), F32)] * 2,
        name="up_act", compiler_params=_params("parallel", "arbitrary"))(h2, wt_up, wt_up, cw, cw, cb, cb)


def _ffn_act_bwd(upg, upv, gate, val, cw, dx2b, w_down, batch, seq):
    def half(du, x, w_ref, dx_ref, dw_ref):
        j = pl.program_id(1)
        n = du.shape[0]
        up1, up2 = pltpu.roll(du, n - 1, 0), pltpu.roll(du, n - 2, 0)
        dx_ref[...] = (w_ref[2:3, :] * du + w_ref[1:2, :] * up1 + w_ref[0:1, :] * up2).astype(BF16)
        tail = du[n - EDGE:n]
        dx_ref[n - EDGE:n, :] = (w_ref[2:3, :] * tail + w_ref[1:2, :] * _shift_up(tail, 1)
                                 + w_ref[0:1, :] * _shift_up(tail, 2)).astype(BF16)
        row = lax.broadcasted_iota(jnp.int32, (EDGE, du.shape[1]), 0)
        head, x_tail = du[0:EDGE], x[n - EDGE:n]
        wrap1 = jnp.sum(jnp.where(row >= EDGE - 1, pltpu.roll(head, EDGE - 1, 0), 0.0) * x_tail, axis=0, keepdims=True)
        wrap2 = jnp.sum(jnp.where(row >= EDGE - 2, pltpu.roll(head, EDGE - 2, 0), 0.0) * x_tail, axis=0, keepdims=True)
        dw_ref[j, 2:3, :] += jnp.sum(du * x, axis=0, keepdims=True)
        dw_ref[j, 1:2, :] += jnp.sum(up1 * x, axis=0, keepdims=True) - wrap1
        dw_ref[j, 0:1, :] += jnp.sum(up2 * x, axis=0, keepdims=True) - wrap2
        dw_ref[j, 3:4, :] += jnp.sum(du, axis=0, keepdims=True)

    def body(ug_ref, uv_ref, g_ref, v_ref, wg_ref, wv_ref, dx_ref, wd_ref, dg_ref, dv_ref, dwg_ref, dwv_ref):
        @pl.when((pl.program_id(0) == 0) & (pl.program_id(1) == 0))
        def _():
            dwg_ref[...] = jnp.zeros_like(dwg_ref)
            dwv_ref[...] = jnp.zeros_like(dwv_ref)

        gate, val = g_ref[...].astype(F32), v_ref[...].astype(F32)
        sg = _sigmoid(gate)
        dav = _dot_nt(dx_ref[...], wd_ref[...])
        half(dav * val * sg * (1.0 + gate * (1.0 - sg)), ug_ref[...].astype(F32), wg_ref, dg_ref, dwg_ref)
        half(dav * gate * sg, uv_ref[...].astype(F32), wv_ref, dv_ref, dwv_ref)

    nb = D_FF // FFB_TILE
    blk = pl.BlockSpec((seq, FFB_TILE), lambda b, j: (b, j))
    wsp = lambda off: pl.BlockSpec((3, FFB_TILE), lambda b, j: (0, j + off))
    acc = pl.BlockSpec((nb, 4, FFB_TILE), lambda b, j: (0, 0, 0))
    T = batch * seq
    dupg, dupv, dwg, dwv = pl.pallas_call(
        body, grid=(batch, nb),
        in_specs=[blk, blk, blk, blk, wsp(0), wsp(nb),
                  pl.BlockSpec((seq, D_MODEL), lambda b, j: (b, 0)),
                  pl.BlockSpec((FFB_TILE, D_MODEL), lambda b, j: (j, 0))],
        out_specs=[blk, blk, acc, acc],
        out_shape=[jax.ShapeDtypeStruct((T, D_FF), BF16), jax.ShapeDtypeStruct((T, D_FF), BF16),
                   jax.ShapeDtypeStruct((nb, 4, FFB_TILE), F32), jax.ShapeDtypeStruct((nb, 4, FFB_TILE), F32)],
        name="ffn_act_bwd", compiler_params=_params("arbitrary", "arbitrary"))(
            upg, upv, gate, val, cw, cw, dx2b, w_down)
    dwg, dwv = (jnp.transpose(a, (1, 0, 2)).reshape(4, D_FF) for a in (dwg, dwv))
    return dupg, dupv, dwg[:3], dwv[:3], dwg[3:], dwv[3:]


def _down_loss(a, w_down, x1, target, gfin):
    T = x1.shape[0]
    tm = _tile(T, 512)

    def body(a_ref, w_ref, x1_ref, t_ref, g_ref, dx_ref, dxb_ref, loss_ref, dg_ref):
        @pl.when(pl.program_id(0) == 0)
        def _():
            loss_ref[...] = jnp.zeros_like(loss_ref)
            dg_ref[...] = jnp.zeros_like(dg_ref)

        x2 = x1_ref[...] + _dot(a_ref[...], w_ref[...])
        r = lax.rsqrt(jnp.mean(x2 * x2, axis=-1, keepdims=True) + EPS)
        xh = x2 * r
        g = g_ref[...]
        diff = xh * g - t_ref[...]
        loss_ref[...] += 0.5 * jnp.sum(jnp.mean(diff * diff, axis=-1, keepdims=True))
        dy = diff * (1.0 / D_MODEL)
        dg_ref[...] += jnp.sum(dy * xh, axis=0, keepdims=True)
        dxh = dy * g
        dx = r * (dxh - xh * jnp.mean(dxh * xh, axis=-1, keepdims=True))
        dx_ref[...] = dx
        dxb_ref[...] = dx.astype(BF16)

    row = pl.BlockSpec((tm, D_MODEL), lambda i: (i, 0))
    vec = pl.BlockSpec((1, D_MODEL), lambda i: (0, 0))
    return pl.pallas_call(
        body, grid=(T // tm,),
        in_specs=[pl.BlockSpec((tm, D_FF), lambda i: (i, 0)),
                  pl.BlockSpec((D_FF, D_MODEL), lambda i: (0, 0)), row, row, vec],
        out_specs=[row, row, pl.BlockSpec((8, LANES), lambda i: (0, 0)), vec],
        out_shape=[jax.ShapeDtypeStruct((T, D_MODEL), F32), jax.ShapeDtypeStruct((T, D_MODEL), BF16),
                   jax.ShapeDtypeStruct((8, LANES), F32), jax.ShapeDtypeStruct((1, D_MODEL), F32)],
        name="down_loss", compiler_params=_params("arbitrary"))(a, w_down, x1, target, gfin)


def _local_step(x, positions, target, mix_norm, av_g, av_b, w_s, b_s, q_norm, kv_norm, ffn_norm, conv_b,
                final_norm, comm):
    batch, seq, _ = x.shape
    T = batch * seq
    x = x.reshape(T, D_MODEL)
    target = target.reshape(T, D_MODEL)
    pos = positions.reshape(T, 1)
    half = jnp.arange(0, QK_ROPE, 2, dtype=F32) / QK_ROPE
    inv_freq = 1.0 / (ROPE_THETA ** half)
    invf = jnp.concatenate([inv_freq, inv_freq, jnp.zeros((LANES - QK_ROPE,), F32)]).reshape(1, LANES)
    w_st = jnp.swapaxes(w_s, 1, 2)
    b_col = b_s.reshape(A_GROUPS, CHUNK, 1)

    wt_in = comm.in_weights()
    h, zm, zs = _in_proj(x, mix_norm, wt_in)
    yag = _mixer_a_fwd(zm, av_g, av_b, w_s, b_col)
    wuq_p, wukv, w_out = comm.mla_weights(after=yag)
    q, k, v = _mla_prep_fwd(zs, pos, invf, q_norm, kv_norm, wuq_p, wukv)
    o, lse = _attn_fwd(q, k, v, batch, seq)
    merged, x1, h2 = _merge_out(x, yag, zm, o, w_out, ffn_norm)
    wt_up, conv_w, w_down = comm.ffn_weights(after=merged)
    upg, upv, gate, val, act = _up_act(h2, wt_up, conv_w, conv_b, batch, seq)
    dx2, dx2b, loss_acc, d_final = _down_loss(act, w_down, x1, target, final_norm)

    d_wdown = _mm_tn(act, dx2b, "dw_down")
    dupg, dupv, dcwg, dcwv, dcbg, dcbv = _ffn_act_bwd(upg, upv, gate, val, conv_w, dx2b, w_down, batch, seq)
    d_wt_up = _mm_tn(dupv, h2, "dw_up_val", rows=2 * D_FF, row0=D_FF,
                     into=_mm_tn(dupg, h2, "dw_up_gate", rows=2 * D_FF))
    dx1, d_ffn_norm, dmerged = _proj_bwd(
        [dupg, dupv], wt_up, [(0, (0, D_FF), (0, D_FF)), (1, (0, D_FF), (D_FF, 2 * D_FF))],
        x1, ffn_norm, dx2, "up_proj_bwd", w2=w_out)
    d_wout = _mm_tn(merged, dx1, "dw_out")
    token = comm.send_ffn_grads(d_wdown, d_wt_up, jnp.concatenate([dcwg, dcwv], axis=1), d_wout)
    dzm, do, d_avg, d_avb, d_ws, d_bs = _mixer_bwd(zm, o, dmerged, av_g, av_b, w_s, w_st, b_col, token)
    zs_rows_zero = lax.empty((IN_DIM, D_MODEL), BF16).at[SPLIT_V:SPLIT_KR].set(0)
    d_wt_in = _mm_tn(dzm, h, "dw_in_main", rows=IN_DIM, gap=(SPLIT_V, ZS_ROWS), into=zs_rows_zero)
    token = comm.send_early_grads(d_wt_in, [
        d_avg, d_avb, _small_2d(d_ws).astype(BF16), d_bs.reshape(A_GROUPS, CHUNK), d_ffn_norm,
        jnp.concatenate([dcbg, dcbv], axis=1), d_final])
    dq, dk, dv = _attn_bwd(q, k, v, o, do, lse, batch, seq, token)
    dzs, d_wuq_p, d_wukv, d_qn, d_kvn = _mla_prep_bwd(zs, pos, invf, q_norm, kv_norm, wuq_p, wukv, dq, dk, dv)
    d_wt_zs = _mm_tn(dzs, h, "dw_in_small")
    token = comm.send_mla_grads(d_wuq_p, d_wukv, d_wt_zs)
    terms = [(0, (i * D_MODEL, (i + 1) * D_MODEL), rows) for i, rows in enumerate(IN_ROWS_MAIN)]
    terms.append((1, (0, ZS_W), IN_ROWS_ZS))
    dx, d_mix_norm = _proj_bwd([dzm, dzs], wt_in, terms, x, mix_norm, dx1, "in_proj_bwd", dep=token, rows=512)
    token = comm.send_late_grads([d_qn, d_kvn, d_mix_norm, loss_acc])
    return dx.reshape(batch, seq, D_MODEL), token


MESH_ID = pl.DeviceIdType.MESH
EFFECT = pltpu.SideEffectType.DATAFLOW_SIDE_EFFECTING


def _mesh_pos():
    return lax.axis_index("x"), lax.axis_index("y"), lax.axis_index("c")


def _peer(pos, d):
    x, y, c = pos
    px = 1 - x if d & 4 else x
    py = 1 - y if d & 2 else y
    pc = 1 - c if d & 1 else c
    return (px, py, pc), 4 * px + 2 * py + pc


def _copy(src_ref, land_ref, send_sems, recv_sems, a, d, pos, exchange, landing_here):
    peer, pid = _peer(pos, d)
    me = 4 * pos[0] + 2 * pos[1] + pos[2]
    if exchange:
        src, dst = src_ref.at[pid], land_ref.at[d]
    else:
        src, dst = src_ref, land_ref.at[pid if landing_here else me]
    return pltpu.make_async_remote_copy(
        src_ref=src, dst_ref=dst, send_sem=send_sems.at[a * (N_DEV - 1) + d - 1],
        recv_sem=recv_sems.at[a * (N_DEV - 1) + d - 1],
        device_id=peer, device_id_type=MESH_ID)


def _start_copies(groups, modes, name, dep=None):
    sizes = [len(g) for g in groups]
    srcs = [s for g in groups for s in g]
    lands = [lax.empty(s.shape if modes[gi] else (N_DEV,) + s.shape, s.dtype)
             for gi, g in enumerate(groups) for s in g]
    n, ng = len(srcs), len(groups)
    n_in = 2 * n + (dep is not None)

    def body(*refs):
        src_refs, land_refs = refs[:n], refs[n:2 * n]
        sems = refs[n_in:n_in + 3 * ng]
        token = refs[-1]
        pos = _mesh_pos()
        k = 0
        for gi, size in enumerate(sizes):
            for a in range(size):
                _own_copy(src_refs[k], land_refs[k], sems[3 * gi + 2], a, pos, modes[gi]).start()
                for d in range(1, N_DEV):
                    _copy(src_refs[k], land_refs[k], sems[3 * gi], sems[3 * gi + 1], a, d, pos, modes[gi],
                          landing_here=False).start()
                k += 1
        token[...] = jnp.zeros_like(token)

    sem_shapes = []
    for size in sizes:
        remote = pltpu.SemaphoreType.DMA((size * (N_DEV - 1),))
        sem_shapes += [remote, remote, pltpu.SemaphoreType.DMA((size,))]
    out = pl.pallas_call(
        body, name=name,
        out_shape=(*sem_shapes, *[pltpu.HBM(a.shape, a.dtype) for a in srcs + lands],
                   jax.ShapeDtypeStruct((8, LANES), F32)),
        in_specs=[HBM] * (2 * n) + [ANY] * (dep is not None),
        out_specs=(*[SEM] * (3 * ng), *[HBM] * (2 * n), pl.BlockSpec(memory_space=pltpu.VMEM)),
        input_output_aliases={i: 3 * ng + i for i in range(2 * n)},
        compiler_params=pltpu.CompilerParams(has_side_effects=EFFECT),
    )(*[pltpu.with_memory_space_constraint(a, pltpu.HBM) for a in srcs + lands], *([dep] if dep is not None else []))
    thru = out[3 * ng:3 * ng + 2 * n]
    handles, k = [], 0
    for gi, size in enumerate(sizes):
        handles.append((out[3 * gi:3 * gi + 3], thru[k:k + size], thru[n + k:n + k + size]))
        k += size
    return handles, out[-1]


def _own_copy(src_ref, land_ref, local_sems, a, pos, exchange):
    me = 4 * pos[0] + 2 * pos[1] + pos[2]
    src, dst = (src_ref.at[me], land_ref.at[0]) if exchange else (src_ref, land_ref.at[me])
    return pltpu.make_async_copy(src, dst, local_sems.at[a])


def _wait_copies(handle, exchange, after, name):
    sems, srcs, lands = handle
    n = len(srcs)

    def body(*refs):
        src_refs, land_refs = refs[:n], refs[n:2 * n]
        send, recv, local = refs[2 * n:2 * n + 3]
        pos = _mesh_pos()
        for a in range(n):
            _own_copy(src_refs[a], land_refs[a], local, a, pos, exchange).wait()
            for d in range(1, N_DEV):
                cp = _copy(src_refs[a], land_refs[a], send, recv, a, d, pos, exchange, landing_here=True)
                cp.wait_send()
                cp.wait_recv()

    out = pl.pallas_call(
        body, name=name,
        out_shape=tuple(pltpu.HBM(a.shape, a.dtype) for a in (*srcs, *lands)),
        in_specs=[HBM] * (2 * n) + [SEM, SEM, SEM, ANY], out_specs=[HBM] * (2 * n),
        input_output_aliases={i: i for i in range(2 * n)},
        compiler_params=pltpu.CompilerParams(has_side_effects=EFFECT),
    )(*srcs, *lands, *sems, after)
    return out[n:]


def _gather_now(a, name):
    def body(x_ref, out_ref, send_sems, recv_sems, local_sem):
        x, y, c = _mesh_pos()
        me, sibling = (x, y, c), (x, y, 1 - c)
        chips = [(1 - x, y), (x, 1 - y), (1 - x, 1 - y)]

        def slot(p):
            return out_ref.at[4 * p[0] + 2 * p[1] + p[2]]

        def copy(k, block, to, src=None):
            return pltpu.make_async_remote_copy(
                src_ref=slot(block) if src is None else src, dst_ref=slot(block), send_sem=send_sems.at[k],
                recv_sem=recv_sems.at[k], device_id=to, device_id_type=MESH_ID)

        mine = pltpu.make_async_copy(x_ref, slot(me), local_sem)
        mine.start()
        first = [copy(0, me, sibling, src=x_ref)]
        first += [copy(1 + j, me, (*chip, c), src=x_ref) for j, chip in enumerate(chips)]
        for cp in first:
            cp.start()
        passed = [copy(4 + j, (*chip, c), sibling) for j, chip in enumerate(chips)]
        for j, chip in enumerate(chips):
            copy(1 + j, (*chip, c), me).wait_recv()
            passed[j].start()
        copy(0, sibling, me).wait_recv()
        for j, chip in enumerate(chips):
            copy(4 + j, (*chip, 1 - c), me).wait_recv()
        for cp in first + passed:
            cp.wait_send()
        mine.wait()

    return pl.pallas_call(
        body, in_specs=[ANY], out_specs=ANY,
        out_shape=jax.ShapeDtypeStruct((N_DEV,) + a.shape, a.dtype),
        scratch_shapes=[pltpu.SemaphoreType.DMA((N_DEV - 1,)), pltpu.SemaphoreType.DMA((N_DEV - 1,)),
                        pltpu.SemaphoreType.DMA],
        name=name, compiler_params=pltpu.CompilerParams(has_side_effects=True))(a)


def _sum_parts(p_ref):
    g = p_ref[0].astype(F32)
    for k in range(1, N_DEV):
        g = g + p_ref[k].astype(F32)
    return g


def _adamw_update(p_ref, w_ref, m_ref, v_ref, g_ref, d_ref, nm_ref, nv_ref, patch=None):
    c1 = 1.0 - ADAM_B1 ** ADAM_STEP
    c2 = 1.0 - ADAM_B2 ** ADAM_STEP
    g = _sum_parts(p_ref)
    if patch is not None:
        g = patch(g)
    nm = ADAM_B1 * m_ref[...] + (1.0 - ADAM_B1) * g
    nv = ADAM_B2 * v_ref[...] + (1.0 - ADAM_B2) * (g * g)
    g_ref[...] = g
    nm_ref[...] = nm
    nv_ref[...] = nv
    d_ref[...] = -ADAM_LR * ((nm / c1) / (jnp.sqrt(nv / c2) + ADAM_EPS) + ADAM_WD * w_ref[...])


def _adamw_many(parts, ws, ms, vs, sums, name):
    n, ns = len(ws), len(sums)

    def body(*refs):
        ins, outs = refs[:4 * n + ns], refs[4 * n + ns:]
        for i in range(n):
            _adamw_update(ins[i], ins[n + i], ins[2 * n + i], ins[3 * n + i],
                          outs[i], outs[n + i], outs[2 * n + i], outs[3 * n + i])
        for i in range(ns):
            outs[4 * n + i][...] = _sum_parts(ins[4 * n + i])

    full = lambda a: pl.BlockSpec(a.shape, lambda: (0,) * a.ndim)
    args = [*parts, *ws, *ms, *vs, *sums]
    outs = [jax.ShapeDtypeStruct(w.shape, F32) for _ in range(4) for w in ws]
    outs += [jax.ShapeDtypeStruct(s.shape[1:], F32) for s in sums]
    res = pl.pallas_call(
        body, in_specs=[full(a) for a in args], out_specs=[full(o) for o in outs], out_shape=outs,
        name=name, compiler_params=pltpu.CompilerParams(vmem_limit_bytes=VMEM_LIMIT))(*args)
    return res[:n], res[n:2 * n], res[2 * n:3 * n], res[3 * n:4 * n], res[4 * n:]


def _adamw(parts, w, m, v, name, late=None):
    R, C = w.shape
    tr, tc = R, C
    if N_DEV * R * C * parts.dtype.itemsize > SMALL_BLOCK_BYTES:
        tr = next((t for t in range(min(R, 256) // 16 * 16, 15, -16) if R % t == 0), R)
        if tr == R:
            tc = _tile(C, 256)
    more, places = late if late is not None else (None, ())
    assert late is None or tr == R

    def body(p_ref, w_ref, m_ref, v_ref, *refs):
        def patch(g):
            more_ref, g_s = refs[0], refs[-1]
            x, y, c = _mesh_pos()
            me = 4 * x + 2 * y + c
            rows = _sum_parts(more_ref)
            g_s[...] = g
            for dev, row in places:
                g_s[row:row + rows.shape[0], :] += jnp.where(me == dev, rows, 0.0)
            return g_s[...]

        outs = refs[:4] if late is None else refs[1:5]
        _adamw_update(p_ref, w_ref, m_ref, v_ref, *outs, patch=None if late is None else patch)

    blk = pl.BlockSpec((tr, tc), lambda i, j: (i, j))
    shp = jax.ShapeDtypeStruct((R, C), F32)
    in_specs = [pl.BlockSpec((N_DEV, tr, tc), lambda i, j: (0, i, j)), blk, blk, blk]
    if late is not None:
        in_specs.append(pl.BlockSpec((N_DEV, more.shape[1], tc), lambda i, j: (0, 0, j)))
    return pl.pallas_call(
        body, grid=(R // tr, C // tc), in_specs=in_specs,
        out_specs=[blk, blk, blk, blk], out_shape=[shp, shp, shp, shp],
        scratch_shapes=[] if late is None else [pltpu.VMEM((tr, tc), F32)],
        name=name, compiler_params=_params("parallel", "parallel"))(parts, w, m, v, *([] if late is None else [more]))


def _adamw_rows(parts, w, m, v, name):
    R = w.shape[0]

    def body(p_ref, w_ref, m_ref, v_ref, g_ref, d_ref, nm_ref, nv_ref):
        for k in range(R):
            _adamw_update(p_ref.at[:, k:k + 1, :], w_ref.at[k], m_ref.at[k], v_ref.at[k],
                          g_ref.at[k], d_ref.at[k], nm_ref.at[k], nv_ref.at[k])

    full = lambda a: pl.BlockSpec(a.shape, lambda: (0,) * a.ndim)
    shp = jax.ShapeDtypeStruct(w.shape, F32)
    return pl.pallas_call(
        body, in_specs=[full(a) for a in (parts, w, m, v)], out_specs=[full(w)] * 4, out_shape=[shp] * 4,
        name=name, compiler_params=pltpu.CompilerParams(vmem_limit_bytes=VMEM_LIMIT))(parts, w, m, v)


SPLIT_V = 2 * D_MODEL
SPLIT_KR = SPLIT_V + Q_LORA + KV_LORA + QK_ROPE
IN_DIM = SPLIT_KR + 2 * D_MODEL
IN_ROWS_MAIN = ((0, D_MODEL), (D_MODEL, SPLIT_V), (SPLIT_KR, SPLIT_KR + D_MODEL), (SPLIT_KR + D_MODEL, IN_DIM))
IN_ROWS_ZS = (SPLIT_V, SPLIT_V + ZS_W)
ZS_ROWS = SPLIT_KR - SPLIT_V
IN_SHARD = IN_DIM // N_DEV
ZS_PIECES = tuple(
    (k, max(IN_SHARD * k, SPLIT_V) - SPLIT_V, max(IN_SHARD * k, SPLIT_V) - IN_SHARD * k)
    for k in range(N_DEV) if max(IN_SHARD * k, SPLIT_V) < min(IN_SHARD * (k + 1), SPLIT_KR))
ZS_PIECE_ROWS = ZS_ROWS // len(ZS_PIECES)
assert all(min(IN_SHARD * (k + 1), SPLIT_KR) - max(IN_SHARD * k, SPLIT_V) == ZS_PIECE_ROWS for k, _, _ in ZS_PIECES)

SMALL_EARLY = ("a_v_norm_g", "a_v_norm_b", "a_spatial_w", "a_spatial_b", "ffn_norm", "conv_b", "final_norm")
SMALL_LATE = ("q_a_norm", "kv_a_norm", "mix_norm")


def _small_2d(a):
    return a.reshape(-1, a.shape[-1])


def _cols_from_shards(g):
    return jnp.transpose(g, (1, 0, 2)).reshape(g.shape[1], N_DEV * g.shape[2])


def _shards_from_cols(a):
    R, W = a.shape
    return jnp.transpose(a.reshape(R, N_DEV, W // N_DEV), (1, 0, 2))


class _Comm:
    GATHER_GROUPS = (("w_uq", "w_ukv", "w_out"), ("w_up", "conv_w", "w_down"))
    FFN_GRADS = ("w_down", "w_up", "conv_w", "w_out")
    TRANSPOSED = ("w_in", "w_up", "w_uq")

    def __init__(self, shards):
        local = {n: a.astype(F32 if n == "conv_w" else BF16) for n, a in shards.items()}
        self.g_in = _gather_now(local["w_in"], "gather_w_in")
        groups = [[local[n] for n in g] for g in self.GATHER_GROUPS]
        (self.h_mla, self.h_ffn), _ = _start_copies(groups, [False] * 2, "gather_start", dep=self.g_in)

    def in_weights(self):
        return self.g_in.reshape(IN_DIM, D_MODEL)

    def mla_weights(self, after):
        g_uq, g_ukv, g_out = _wait_copies(self.h_mla, False, after, "gather_wait_mla")
        wuq_p = jnp.pad(g_uq, ((0, 0), (0, HEAD_PAD - QK_HEAD), (0, 0))).reshape(MLA_HEADS * HEAD_PAD, Q_LORA)
        return wuq_p, _cols_from_shards(g_ukv), g_out.reshape(D_MODEL, D_MODEL)

    def ffn_weights(self, after):
        g_up, g_cw, g_down = _wait_copies(self.h_ffn, False, after, "gather_wait_ffn")
        return g_up.reshape(2 * D_FF, D_MODEL), _cols_from_shards(g_cw), g_down.reshape(D_FF, D_MODEL)

    def send_ffn_grads(self, d_wdown, d_wt_up, d_convw, d_wout):
        group = [d_wdown.reshape(N_DEV, D_FF // N_DEV, D_MODEL), d_wt_up.reshape(N_DEV, 2 * D_FF // N_DEV, D_MODEL),
                 _shards_from_cols(d_convw), d_wout.reshape(N_DEV, D_MODEL // N_DEV, D_MODEL)]
        (self.h_ffn_grads,), token = _start_copies([group], [True], "ffn_grads_start")
        return token

    def send_early_grads(self, d_wt_in, grads):
        blocks = d_wt_in.reshape(N_DEV, IN_SHARD, D_MODEL)
        (self.h_small_early, self.h_in_grads), token = _start_copies(
            [grads, [blocks]], [False, True], "early_grads_start")
        return token

    def send_mla_grads(self, d_wuq_p, d_wukv, d_wt_zs):
        d_uq = d_wuq_p.reshape(MLA_HEADS, HEAD_PAD, Q_LORA)[:, :QK_HEAD, :]
        zs_blocks = jnp.zeros((N_DEV, ZS_PIECE_ROWS, D_MODEL), d_wt_zs.dtype)
        for dev, first, _ in ZS_PIECES:
            zs_blocks = zs_blocks.at[dev].set(d_wt_zs[first:first + ZS_PIECE_ROWS])
        (self.h_mla_grads,), token = _start_copies(
            [[d_uq, _shards_from_cols(d_wukv), zs_blocks]], [True], "mla_grads_start")
        return token

    def send_late_grads(self, small):
        (self.h_late_small,), token = _start_copies([small], [False], "late_grads_start")
        return token


def kernel(x, positions, mix_norm, w_in, a_v_norm_g, a_v_norm_b, a_spatial_w, a_spatial_b, q_a_norm, w_uq, kv_a_norm, w_ukv, w_out, ffn_norm, w_up, conv_w, conv_b, w_down, final_norm, loss_target, m_mix_norm, m_w_in, m_a_v_norm_g, m_a_v_norm_b, m_a_spatial_w, m_a_spatial_b, m_q_a_norm, m_w_uq, m_kv_a_norm, m_w_ukv, m_w_out, m_ffn_norm, m_w_up, m_conv_w, m_conv_b, m_w_down, m_final_norm, v_mix_norm, v_w_in, v_a_v_norm_g, v_a_v_norm_b, v_a_spatial_w, v_a_spatial_b, v_q_a_norm, v_w_uq, v_kv_a_norm, v_w_ukv, v_w_out, v_ffn_norm, v_w_up, v_conv_w, v_conv_b, v_w_down, v_final_norm):
    names = ("mix_norm", "w_in", "a_v_norm_g", "a_v_norm_b", "a_spatial_w", "a_spatial_b", "q_a_norm", "w_uq",
             "kv_a_norm", "w_ukv", "w_out", "ffn_norm", "w_up", "conv_w", "conv_b", "w_down", "final_norm")
    w = dict(zip(names, (mix_norm, w_in, a_v_norm_g, a_v_norm_b, a_spatial_w, a_spatial_b, q_a_norm, w_uq,
                         kv_a_norm, w_ukv, w_out, ffn_norm, w_up, conv_w, conv_b, w_down, final_norm)))
    m = dict(zip(names, (m_mix_norm, m_w_in, m_a_v_norm_g, m_a_v_norm_b, m_a_spatial_w, m_a_spatial_b,
                         m_q_a_norm, m_w_uq, m_kv_a_norm, m_w_ukv, m_w_out, m_ffn_norm, m_w_up, m_conv_w,
                         m_conv_b, m_w_down, m_final_norm)))
    v = dict(zip(names, (v_mix_norm, v_w_in, v_a_v_norm_g, v_a_v_norm_b, v_a_spatial_w, v_a_spatial_b,
                         v_q_a_norm, v_w_uq, v_kv_a_norm, v_w_ukv, v_w_out, v_ffn_norm, v_w_up, v_conv_w,
                         v_conv_b, v_w_down, v_final_norm)))
    shapes = {n: w[n].shape for n in names}
    def view(tree, n):
        a = tree[n].reshape(tree[n].shape[-2:])
        return a.T if n in _Comm.TRANSPOSED else a

    comm = _Comm({n: view(w, n) for n in ("w_in",) + _Comm.GATHER_GROUPS[0] + _Comm.GATHER_GROUPS[1]})

    grad_x, token = _local_step(
        x, positions, loss_target, w["mix_norm"], w["a_v_norm_g"], w["a_v_norm_b"], w["a_spatial_w"][0],
        w["a_spatial_b"][0], w["q_a_norm"], w["kv_a_norm"], w["ffn_norm"], w["conv_b"],
        w["final_norm"].reshape(1, D_MODEL), comm)

    out_g, out_d, out_m, out_v = {}, {}, {}, {}

    def update(n, parts, late=None):
        if n == "conv_w":
            rows = lambda t: t[n].reshape(t[n].shape[-2], 1, t[n].shape[-1])
            res = _adamw_rows(parts, rows(w), rows(m), rows(v), "adamw_" + n)
        else:
            res = _adamw(parts, view(w, n), view(m, n), view(v, n), "adamw_" + n, late=late)
        out_g[n], out_d[n], out_m[n], out_v[n] = (
            (t.T if n in _Comm.TRANSPOSED else t).reshape(shapes[n]) for t in res)
        return res[1]

    def update_small(names, parts, sums, name):
        res = _adamw_many(parts, *[[_small_2d(t[n]) for n in names] for t in (w, m, v)], sums, name)
        for i, n in enumerate(names):
            out_g[n], out_d[n], out_m[n], out_v[n] = (r[i].reshape(shapes[n]) for r in res[:4])
        return res

    for n, parts in zip(_Comm.FFN_GRADS, _wait_copies(comm.h_ffn_grads, True, token, "ffn_grads_wait")):
        last = update(n, parts)
    early = _wait_copies(comm.h_small_early, False, last, "small_grads_wait")
    last = update_small(SMALL_EARLY, early, [], "adamw_small")[1][0]
    (in_parts,) = _wait_copies(comm.h_in_grads, True, last, "in_grads_wait")
    uq_parts, ukv_parts, zs_parts = _wait_copies(comm.h_mla_grads, True, in_parts, "mla_grads_wait")
    last = update("w_in", in_parts, late=(zs_parts, [(dev, row) for dev, _, row in ZS_PIECES]))
    last = update("w_uq", uq_parts)
    last = update("w_ukv", ukv_parts)
    late = _wait_copies(comm.h_late_small, False, last, "late_small_wait")
    res = update_small(SMALL_LATE, late[:-1], late[-1:], "adamw_late")
    loss = res[4][0][0, 0]

    return (loss, grad_x, *[out_g[n] for n in names], *[out_d[n] for n in names],
            *[out_m[n] for n in names], *[out_v[n] for n in names])
```

```python
import math

import jax
import jax.numpy as jnp
from jax import lax
from jax.experimental import pallas as pl
from jax.experimental.pallas import tpu as pltpu

F32 = jnp.float32
BF16 = jnp.bfloat16
KEPT = jnp.bfloat16

N_DEV = 8
D_MODEL = 1024
EPS = 1e-6
A_GROUPS = 8
CHUNK = 128
MLA_HEADS = 8
QK_NOPE = 128
QK_ROPE = 64
QK_HEAD = QK_NOPE + QK_ROPE
HEAD_PAD = 256
V_HEAD = 128
Q_LORA = 256
KV_LORA = 128
ROPE_THETA = 10000.0
D_FF = 2816
ZS_W = 512
ATTN_SCALE = QK_HEAD ** -0.5
ATTN_TILE = 512
NEG_BIG = -1e30

ADAM_LR = 0.001
ADAM_B1 = 0.9
ADAM_B2 = 0.999
ADAM_EPS = 1e-08
ADAM_WD = 0.01
ADAM_STEP = 10

VMEM_LIMIT = 56 * 1024 * 1024
SMALL_BLOCK_BYTES = 5 * 1024 * 1024
LANES = 128
SUBLANES = 8

GELU_K = math.sqrt(2.0 / math.pi)
GELU_C = 0.044715

ANY = pl.BlockSpec(memory_space=pl.ANY)
HBM = pl.BlockSpec(memory_space=pltpu.HBM)
SEM = pl.BlockSpec(memory_space=pltpu.SEMAPHORE)


def _tile(n, pref):
    for t in (pref, 512, 256, 128, 64, 32, 16, 8):
        if t <= pref and n % t == 0:
            return t
    return n


def _wide_tile(n, cap=1408):
    return next((t for t in range(min(n, cap) // LANES * LANES, 0, -LANES) if n % t == 0), n)


def _params(*sem):
    return pltpu.CompilerParams(dimension_semantics=sem, vmem_limit_bytes=VMEM_LIMIT)


def _dot(a, b):
    return jnp.dot(a, b, preferred_element_type=F32)


def _dot_nt(a, b):
    return lax.dot_general(a, b, (((1,), (1,)), ((), ())), preferred_element_type=F32)


def _dot_tn(a, b):
    return lax.dot_general(a, b, (((0,), (0,)), ((), ())), preferred_element_type=F32)


def _sigmoid(x):
    return 1.0 / (1.0 + jnp.exp(-x))


def _gelu(x):
    t = jnp.tanh(GELU_K * (x + GELU_C * x * x * x))
    return 0.5 * x * (1.0 + t)


def _gelu_and_grad(x):
    x2 = x * x
    t = jnp.tanh(GELU_K * (x + GELU_C * x * x2))
    half = 0.5 * (1.0 + t)
    return x * half, half + 0.5 * x * (1.0 - t * t) * GELU_K * (1.0 + 3.0 * GELU_C * x2)


def _in_proj(x, g, wt):
    T, Dm = x.shape
    tm = _tile(T, 512)

    def body(x_ref, g_ref, wt_ref, h_ref, zm_ref, zs_ref):
        xf = x_ref[...]
        r = lax.rsqrt(jnp.mean(xf * xf, axis=-1, keepdims=True) + EPS)
        h = (xf * r * g_ref[...]).astype(BF16)
        h_ref[...] = h
        for i, (r0, r1) in enumerate(IN_ROWS_MAIN):
            zm_ref[:, i * D_MODEL:(i + 1) * D_MODEL] = _dot_nt(h, wt_ref[r0:r1, :]).astype(KEPT)
        zs_ref[...] = _dot_nt(h, wt_ref[IN_ROWS_ZS[0]:IN_ROWS_ZS[1], :])

    row = lambda n: pl.BlockSpec((tm, n), lambda i: (i, 0))
    return pl.pallas_call(
        body, grid=(T // tm,),
        in_specs=[row(Dm), pl.BlockSpec((1, Dm), lambda i: (0, 0)), pl.BlockSpec(wt.shape, lambda i: (0, 0))],
        out_specs=[row(Dm), row(4 * D_MODEL), row(ZS_W)],
        out_shape=[jax.ShapeDtypeStruct((T, Dm), BF16), jax.ShapeDtypeStruct((T, 4 * D_MODEL), KEPT),
                   jax.ShapeDtypeStruct((T, ZS_W), F32)],
        name="in_proj", compiler_params=_params("parallel"))(x, g, wt)


def _proj_bwd(acts, wt, terms, x, g, dres, name, w2=None, dep=None, rows=256):
    T, Dm = x.shape
    tm = _tile(T, rows)
    n_a = len(acts)

    def body(*refs):
        ins, outs = refs[:n_a + 4 + (w2 is not None) + (dep is not None)], refs[-2 - (w2 is not None):]
        wt_ref, x_ref, g_ref, dres_ref = ins[n_a:n_a + 4]
        dx_ref, dg_ref = outs[0], outs[1]

        @pl.when(pl.program_id(0) == 0)
        def _():
            dg_ref[...] = jnp.zeros_like(dg_ref)

        dy = None
        for i, (c0, c1), (r0, r1) in terms:
            t = _dot(ins[i][:, c0:c1], wt_ref[r0:r1, :])
            dy = t if dy is None else dy + t
        xf = x_ref[...]
        r = lax.rsqrt(jnp.mean(xf * xf, axis=-1, keepdims=True) + EPS)
        xh = xf * r
        dg_ref[...] += jnp.sum(dy * xh, axis=0, keepdims=True)
        dxh = dy * g_ref[...]
        dx = dres_ref[...] + r * (dxh - xh * jnp.mean(dxh * xh, axis=-1, keepdims=True))
        dx_ref[...] = dx
        if w2 is not None:
            outs[2][...] = _dot_nt(dx.astype(BF16), ins[n_a + 4][...]).astype(KEPT)

    row = pl.BlockSpec((tm, Dm), lambda i: (i, 0))
    vec = pl.BlockSpec((1, Dm), lambda i: (0, 0))
    in_specs = [pl.BlockSpec((tm, a.shape[1]), lambda i: (i, 0)) for a in acts]
    in_specs += [pl.BlockSpec(wt.shape, lambda i: (0, 0)), row, vec, row]
    args = [*acts, wt, x, g, dres]
    out_specs = [row, vec]
    out_shape = [jax.ShapeDtypeStruct((T, Dm), F32), jax.ShapeDtypeStruct((1, Dm), F32)]
    if w2 is not None:
        in_specs.append(pl.BlockSpec(w2.shape, lambda i: (0, 0)))
        args.append(w2)
        out_specs.append(pl.BlockSpec((tm, w2.shape[0]), lambda i: (i, 0)))
        out_shape.append(jax.ShapeDtypeStruct((T, w2.shape[0]), KEPT))
    if dep is not None:
        in_specs.append(ANY)
        args.append(dep)
    return pl.pallas_call(
        body, grid=(T // tm,), in_specs=in_specs, out_specs=out_specs, out_shape=out_shape,
        name=name, compiler_params=_params("arbitrary"))(*args)


def _mm_tn(a, b, name, dep=None, rows=None, row0=0, into=None, gap=None):
    T, M = a.shape
    N = b.shape[1]
    tm, tn, tt = _wide_tile(M), _wide_tile(N), _tile(T, 2048)
    n_t = T // tt
    off = row0 // tm
    extra = ([dep] if dep is not None else []) + ([into] if into is not None else [])
    if gap is None:
        out_spec = pl.BlockSpec((tm, tn), lambda i, j, t: (i + off, j))
    else:
        unit = 16
        assert row0 == 0 and gap[0] % tm == 0 and tm % unit == 0 and gap[1] % unit == 0
        out_spec = pl.BlockSpec(
            (pl.Element(tm), pl.Element(tn)),
            lambda i, j, t: ((i * (tm // unit) + jnp.where(i * tm >= gap[0], gap[1] // unit, 0)) * unit, j * tn))

    def body(a_ref, b_ref, *refs):
        o_ref, acc_ref = refs[-2:]
        t = pl.program_id(2)

        @pl.when(t == 0)
        def _():
            acc_ref[...] = jnp.zeros_like(acc_ref)

        acc_ref[...] += _dot_tn(a_ref[...].astype(BF16), b_ref[...].astype(BF16))

        @pl.when(t == n_t - 1)
        def _():
            o_ref[...] = acc_ref[...].astype(BF16)

    return pl.pallas_call(
        body, grid=(M // tm, N // tn, n_t),
        in_specs=[pl.BlockSpec((tt, tm), lambda i, j, t: (t, i)),
                  pl.BlockSpec((tt, tn), lambda i, j, t: (t, j))] + [ANY] * len(extra),
        out_specs=out_spec,
        out_shape=jax.ShapeDtypeStruct((rows or M, N), BF16),
        scratch_shapes=[pltpu.VMEM((tm, tn), F32)],
        input_output_aliases={} if into is None else {1 + len(extra): 0},
        name=name, compiler_params=_params("parallel", "parallel", "arbitrary"))(a, b, *extra)


def _layer_norm_fwd(gv, g, b):
    mu = jnp.mean(gv, axis=-1, keepdims=True)
    xc = gv - mu
    rs = lax.rsqrt(jnp.mean(xc * xc, axis=-1, keepdims=True) + EPS)
    xh = xc * rs
    return xh, rs, xh * g + b


def _tri_mask(transposed=False):
    r = lax.broadcasted_iota(jnp.int32, (CHUNK, CHUNK), 0)
    c = lax.broadcasted_iota(jnp.int32, (CHUNK, CHUNK), 1)
    return r <= c if transposed else c <= r


def _mixer_a_fwd(zm, av_g, av_b, w_s, b_col):
    T = zm.shape[0]
    tm = _tile(T, 512)
    n_chunk = tm // CHUNK

    def body(u_ref, v_ref, ga_ref, g_ref, b_ref, w_ref, bc_ref, y_ref, vn_s, mx_s):
        gu = _gelu(u_ref[...].astype(F32))
        _, _, vn = _layer_norm_fwd(_gelu(v_ref[...].astype(F32)), g_ref[...], b_ref[...])
        vn_s[...] = vn.astype(BF16)
        tri = _tri_mask()
        for gi in range(A_GROUPS):
            wm = jnp.where(tri, w_ref[gi], 0.0).astype(BF16)
            cols = slice(gi * CHUNK, (gi + 1) * CHUNK)
            for n in range(n_chunk):
                rows = slice(n * CHUNK, (n + 1) * CHUNK)
                mx_s[rows, cols] = _dot(wm, vn_s[rows, cols]) + bc_ref[gi]
        y_ref[...] = (_sigmoid(ga_ref[...].astype(F32)) * gu * mx_s[...]).astype(KEPT)

    col = lambda c: pl.BlockSpec((tm, D_MODEL), lambda i: (i, c))
    vec = pl.BlockSpec((1, D_MODEL), lambda i: (0, 0))
    return pl.pallas_call(
        body, grid=(T // tm,),
        in_specs=[col(0), col(1), col(2), vec, vec,
                  pl.BlockSpec((A_GROUPS, CHUNK, CHUNK), lambda i: (0, 0, 0)),
                  pl.BlockSpec((A_GROUPS, CHUNK, 1), lambda i: (0, 0, 0))],
        out_specs=pl.BlockSpec((tm, D_MODEL), lambda i: (i, 0)),
        out_shape=jax.ShapeDtypeStruct((T, D_MODEL), KEPT),
        scratch_shapes=[pltpu.VMEM((tm, D_MODEL), BF16), pltpu.VMEM((tm, D_MODEL), F32)],
        name="mixer_a_fwd", compiler_params=_params("parallel"))(zm, zm, zm, av_g, av_b, w_s, b_col)


def _mixer_bwd(zm, o, dm, av_g, av_b, w_s, w_st, b_col, dep):
    T = zm.shape[0]
    tm = _tile(T, 256)
    n_chunk = tm // CHUNK

    def body(u_ref, v_ref, ga_ref, gb_ref, o_ref, dm_ref, g_ref, b_ref, w_ref, wt_ref, bc_ref, dep_ref,
             dz_ref, do_ref, dg_ref, db_ref, dw_ref, dbs_ref, vn_s, mx_s, dmx_s, dvn_s):
        @pl.when(pl.program_id(0) == 0)
        def _():
            dg_ref[...] = jnp.zeros_like(dg_ref)
            db_ref[...] = jnp.zeros_like(db_ref)
            dw_ref[...] = jnp.zeros_like(dw_ref)
            dbs_ref[...] = jnp.zeros_like(dbs_ref)

        dm_v = dm_ref[...].astype(F32)
        gb = gb_ref[...].astype(F32)
        sb = _sigmoid(gb)
        o_v = o_ref[...].astype(F32)
        do_ref[...] = (dm_v * sb).astype(BF16)
        dz_ref[:, 3 * D_MODEL:4 * D_MODEL] = (dm_v * o_v * sb * (1.0 - sb)).astype(BF16)
        u = u_ref[...].astype(F32)
        v = v_ref[...].astype(F32)
        gu, gu_grad = _gelu_and_grad(u)
        gv, gv_grad = _gelu_and_grad(v)
        xh, rs, vn = _layer_norm_fwd(gv, g_ref[...], b_ref[...])
        vn_s[...] = vn.astype(BF16)
        tri = _tri_mask()
        for gi in range(A_GROUPS):
            wm = jnp.where(tri, w_ref[gi], 0.0).astype(BF16)
            cols = slice(gi * CHUNK, (gi + 1) * CHUNK)
            for n in range(n_chunk):
                rows = slice(n * CHUNK, (n + 1) * CHUNK)
                mx_s[rows, cols] = _dot(wm, vn_s[rows, cols]) + bc_ref[gi]
        mixed = mx_s[...]
        sa = _sigmoid(ga_ref[...].astype(F32))
        dya = dm_v * sa
        dz_ref[:, 2 * D_MODEL:3 * D_MODEL] = (dm_v * gu * mixed * sa * (1.0 - sa)).astype(BF16)
        dz_ref[:, 0:D_MODEL] = (dya * mixed * gu_grad).astype(BF16)
        dmx = dya * gu
        dmx_s[...] = dmx.astype(BF16)
        tri_t = _tri_mask(transposed=True)
        for gi in range(A_GROUPS):
            wmt = jnp.where(tri_t, wt_ref[gi], 0.0).astype(BF16)
            cols = slice(gi * CHUNK, (gi + 1) * CHUNK)
            dw_acc = jnp.zeros((CHUNK, CHUNK), F32)
            dmx_sum = jnp.zeros((CHUNK, CHUNK), F32)
            for n in range(n_chunk):
                rows = slice(n * CHUNK, (n + 1) * CHUNK)
                blk = dmx_s[rows, cols]
                dvn_s[rows, cols] = _dot(wmt, blk)
                dw_acc = dw_acc + _dot_nt(blk, vn_s[rows, cols])
                dmx_sum = dmx_sum + dmx[rows, cols]
            dw_ref[gi] += jnp.where(tri, dw_acc, 0.0)
            dbs_ref[gi] += jnp.sum(dmx_sum, axis=-1, keepdims=True)
        dvn = dvn_s[...]
        dg_ref[...] += jnp.sum(dvn * xh, axis=0, keepdims=True)
        db_ref[...] += jnp.sum(dvn, axis=0, keepdims=True)
        dxh = dvn * g_ref[...]
        dgv = rs * (dxh - jnp.mean(dxh, axis=-1, keepdims=True)
                    - xh * jnp.mean(dxh * xh, axis=-1, keepdims=True))
        dz_ref[:, D_MODEL:2 * D_MODEL] = (dgv * gv_grad).astype(BF16)

    col = lambda c: pl.BlockSpec((tm, D_MODEL), lambda i: (i, c))
    row = pl.BlockSpec((tm, D_MODEL), lambda i: (i, 0))
    vec = pl.BlockSpec((1, D_MODEL), lambda i: (0, 0))
    wsp = pl.BlockSpec((A_GROUPS, CHUNK, CHUNK), lambda i: (0, 0, 0))
    bsp = pl.BlockSpec((A_GROUPS, CHUNK, 1), lambda i: (0, 0, 0))
    return pl.pallas_call(
        body, grid=(T // tm,),
        in_specs=[col(0), col(1), col(2), col(3), row, row, vec, vec, wsp, wsp, bsp, ANY],
        out_specs=[pl.BlockSpec((tm, 4 * D_MODEL), lambda i: (i, 0)), row, vec, vec, wsp, bsp],
        out_shape=[jax.ShapeDtypeStruct((T, 4 * D_MODEL), BF16), jax.ShapeDtypeStruct((T, D_MODEL), BF16),
                   jax.ShapeDtypeStruct((1, D_MODEL), F32), jax.ShapeDtypeStruct((1, D_MODEL), F32),
                   jax.ShapeDtypeStruct((A_GROUPS, CHUNK, CHUNK), F32),
                   jax.ShapeDtypeStruct((A_GROUPS, CHUNK, 1), F32)],
        scratch_shapes=[pltpu.VMEM((tm, D_MODEL), BF16), pltpu.VMEM((tm, D_MODEL), F32),
                        pltpu.VMEM((tm, D_MODEL), BF16), pltpu.VMEM((tm, D_MODEL), F32)],
        name="mixer_bwd", compiler_params=_params("arbitrary"))(
            zm, zm, zm, zm, o, dm, av_g, av_b, w_s, w_st, b_col, dep)


def _rope_tables(pos_ref, invf_ref):
    ang = pos_ref[...].astype(F32) * invf_ref[...]
    lane = lax.broadcasted_iota(jnp.int32, ang.shape, 1)
    cos, sin = jnp.cos(ang), jnp.sin(ang)
    c = jnp.where(lane < QK_ROPE, cos, 0.0)
    sa = jnp.where(lane < QK_ROPE // 2, -sin, 0.0)
    sb = jnp.where((lane >= QK_ROPE // 2) & (lane < QK_ROPE), sin, 0.0)
    return c, sa, sb


def _rope(blk, tabs):
    c, sa, sb = tabs
    return blk * c + pltpu.roll(blk, LANES - QK_ROPE // 2, 1) * sa + pltpu.roll(blk, QK_ROPE // 2, 1) * sb


def _rope_t(dout, tabs):
    c, sa, sb = tabs
    return dout * c + pltpu.roll(dout * sa, QK_ROPE // 2, 1) + pltpu.roll(dout * sb, LANES - QK_ROPE // 2, 1)


def _rms_small(x, g):
    r = lax.rsqrt(jnp.mean(x * x, axis=-1, keepdims=True) + EPS)
    xh = x * r
    return xh, r, xh * g


def _mla_prep_fwd(zs, pos, invf, qg, kvg, wuq_p, wukv):
    T = zs.shape[0]
    tm = _tile(T, 512)
    HW = MLA_HEADS * HEAD_PAD

    def body(zs_ref, pos_ref, invf_ref, qg_ref, kvg_ref, wq_ref, wkv_ref, q_ref, k_ref, v_ref):
        tabs = _rope_tables(pos_ref, invf_ref)
        _, _, cqn = _rms_small(zs_ref[:, 0:Q_LORA], qg_ref[...])
        _, _, ckvn = _rms_small(zs_ref[:, Q_LORA:Q_LORA + KV_LORA], kvg_ref[...])
        q = _dot_nt(cqn.astype(BF16), wq_ref[...]) * ATTN_SCALE
        kv = _dot(ckvn.astype(BF16), wkv_ref[...])
        kr = _rope(zs_ref[:, Q_LORA + KV_LORA:ZS_W], tabs).astype(BF16)
        for h in range(MLA_HEADS):
            b0 = h * HEAD_PAD
            q_ref[:, b0:b0 + QK_NOPE] = q[:, b0:b0 + QK_NOPE].astype(BF16)
            q_ref[:, b0 + QK_NOPE:b0 + HEAD_PAD] = _rope(q[:, b0 + QK_NOPE:b0 + HEAD_PAD], tabs).astype(BF16)
            k_ref[:, b0:b0 + QK_NOPE] = kv[:, b0:b0 + QK_NOPE].astype(BF16)
            k_ref[:, b0 + QK_NOPE:b0 + HEAD_PAD] = kr
            v_ref[:, h * V_HEAD:(h + 1) * V_HEAD] = kv[:, b0 + QK_NOPE:b0 + HEAD_PAD].astype(BF16)

    full = lambda a: pl.BlockSpec(a.shape, lambda i: (0,) * a.ndim)
    return pl.pallas_call(
        body, grid=(T // tm,),
        in_specs=[pl.BlockSpec((tm, ZS_W), lambda i: (i, 0)), pl.BlockSpec((tm, 1), lambda i: (i, 0)),
                  full(invf), full(qg), full(kvg), full(wuq_p), full(wukv)],
        out_specs=[pl.BlockSpec((tm, HW), lambda i: (i, 0)), pl.BlockSpec((tm, HW), lambda i: (i, 0)),
                   pl.BlockSpec((tm, D_MODEL), lambda i: (i, 0))],
        out_shape=[jax.ShapeDtypeStruct((T, HW), BF16), jax.ShapeDtypeStruct((T, HW), BF16),
                   jax.ShapeDtypeStruct((T, D_MODEL), BF16)],
        name="mla_prep_fwd", compiler_params=_params("parallel"))(zs, pos, invf, qg, kvg, wuq_p, wukv)


def _mla_prep_bwd(zs, pos, invf, qg, kvg, wuq_p, wukv, dq, dk, dv):
    T = zs.shape[0]
    tm = _tile(T, 512)
    n_t = T // tm
    HW = MLA_HEADS * HEAD_PAD

    def body(zs_ref, pos_ref, invf_ref, qg_ref, kvg_ref, wq_ref, wkv_ref, dq_ref, dk_ref, dv_ref,
             dzs_ref, dwq_ref, dwkv_ref, dqg_ref, dkvg_ref, dqp_ref, dkv_ref, accq_ref, acckv_ref):
        @pl.when(pl.program_id(0) == 0)
        def _():
            dqg_ref[...] = jnp.zeros_like(dqg_ref)
            dkvg_ref[...] = jnp.zeros_like(dkvg_ref)
            accq_ref[...] = jnp.zeros_like(accq_ref)
            acckv_ref[...] = jnp.zeros_like(acckv_ref)

        tabs = _rope_tables(pos_ref, invf_ref)
        cqh, rq, cqn = _rms_small(zs_ref[:, 0:Q_LORA], qg_ref[...])
        ckvh, rkv, ckvn = _rms_small(zs_ref[:, Q_LORA:Q_LORA + KV_LORA], kvg_ref[...])
        dkr = jnp.zeros((tm, LANES), F32)
        for h in range(MLA_HEADS):
            b0 = h * HEAD_PAD
            dqp_ref[:, b0:b0 + QK_NOPE] = dq_ref[:, b0:b0 + QK_NOPE]
            dqp_ref[:, b0 + QK_NOPE:b0 + HEAD_PAD] = _rope_t(
                dq_ref[:, b0 + QK_NOPE:b0 + HEAD_PAD].astype(F32), tabs).astype(BF16)
            dkv_ref[:, b0:b0 + QK_NOPE] = dk_ref[:, b0:b0 + QK_NOPE]
            dkv_ref[:, b0 + QK_NOPE:b0 + HEAD_PAD] = dv_ref[:, h * V_HEAD:(h + 1) * V_HEAD]
            dkr = dkr + dk_ref[:, b0 + QK_NOPE:b0 + HEAD_PAD].astype(F32)
        accq_ref[...] += _dot_tn(dqp_ref[...], cqn.astype(BF16))
        acckv_ref[...] += _dot_tn(ckvn.astype(BF16), dkv_ref[...])

        @pl.when(pl.program_id(0) == n_t - 1)
        def _():
            dwq_ref[...] = accq_ref[...].astype(BF16)
            dwkv_ref[...] = acckv_ref[...].astype(BF16)

        dcqn = _dot(dqp_ref[...], wq_ref[...])
        dckvn = _dot_nt(dkv_ref[...], wkv_ref[...])
        dqg_ref[...] += jnp.sum(dcqn * cqh, axis=0, keepdims=True)
        dkvg_ref[...] += jnp.sum(dckvn * ckvh, axis=0, keepdims=True)
        dxh = dcqn * qg_ref[...]
        dzs_ref[:, 0:Q_LORA] = (rq * (dxh - cqh * jnp.mean(dxh * cqh, axis=-1, keepdims=True))).astype(BF16)
        dxh = dckvn * kvg_ref[...]
        dzs_ref[:, Q_LORA:Q_LORA + KV_LORA] = (
            rkv * (dxh - ckvh * jnp.mean(dxh * ckvh, axis=-1, keepdims=True))).astype(BF16)
        dzs_ref[:, Q_LORA + KV_LORA:ZS_W] = _rope_t(dkr, tabs).astype(BF16)

    full = lambda a: pl.BlockSpec(a.shape, lambda i: (0,) * a.ndim)
    rowb = lambda w: pl.BlockSpec((tm, w), lambda i: (i, 0))
    return pl.pallas_call(
        body, grid=(T // tm,),
        in_specs=[rowb(ZS_W), rowb(1), full(invf), full(qg), full(kvg), full(wuq_p), full(wukv),
                  rowb(HW), rowb(HW), rowb(D_MODEL)],
        out_specs=[rowb(ZS_W), full(wuq_p), full(wukv), full(qg), full(kvg)],
        out_shape=[jax.ShapeDtypeStruct((T, ZS_W), BF16), jax.ShapeDtypeStruct(wuq_p.shape, BF16),
                   jax.ShapeDtypeStruct(wukv.shape, BF16), jax.ShapeDtypeStruct(qg.shape, F32),
                   jax.ShapeDtypeStruct(kvg.shape, F32)],
        scratch_shapes=[pltpu.VMEM((tm, HW), BF16), pltpu.VMEM((tm, HW), BF16),
                        pltpu.VMEM(wuq_p.shape, F32), pltpu.VMEM(wukv.shape, F32)],
        name="mla_prep_bwd", compiler_params=_params("arbitrary"))(
            zs, pos, invf, qg, kvg, wuq_p, wukv, dq, dk, dv)


def _causal(tq, kmax, q0):
    r = lax.broadcasted_iota(jnp.int32, (tq, kmax), 0) + q0
    c = lax.broadcasted_iota(jnp.int32, (tq, kmax), 1)
    return c <= r


def _attn_fwd(q, k, v, batch, seq):
    tq = _tile(seq, ATTN_TILE)
    nq = seq // tq

    def body(q_ref, k_ref, v_ref, o_ref, lse_ref):
        diag = _causal(tq, tq, 0)
        for qi in range(nq):
            rows = slice(qi * tq, (qi + 1) * tq)
            qr = q_ref[rows, :]
            s_d = jnp.where(diag, _dot_nt(qr, k_ref[rows, :]), NEG_BIG)
            m = jnp.max(s_d, axis=-1, keepdims=True)
            if qi > 0:
                before = slice(0, qi * tq)
                s_b = _dot_nt(qr, k_ref[before, :])
                m = jnp.maximum(m, jnp.max(s_b, axis=-1, keepdims=True))
                p_b = jnp.exp(s_b - m)
                l = jnp.sum(p_b, axis=-1, keepdims=True)
                acc = _dot(p_b.astype(BF16), v_ref[before, :])
            p_d = jnp.exp(s_d - m)
            l_d = jnp.sum(p_d, axis=-1, keepdims=True)
            acc_d = _dot(p_d.astype(BF16), v_ref[rows, :])
            l, acc = (l + l_d, acc + acc_d) if qi > 0 else (l_d, acc_d)
            o_ref[rows, :] = (acc / l).astype(KEPT)
            lse_ref[rows, :] = jnp.broadcast_to(m + jnp.log(l), (tq, V_HEAD))

    return pl.pallas_call(
        body, grid=(batch, MLA_HEADS),
        in_specs=[pl.BlockSpec((seq, HEAD_PAD), lambda b, h: (b, h)),
                  pl.BlockSpec((seq, HEAD_PAD), lambda b, h: (b, h)),
                  pl.BlockSpec((seq, V_HEAD), lambda b, h: (b, h))],
        out_specs=[pl.BlockSpec((seq, V_HEAD), lambda b, h: (b, h)),
                   pl.BlockSpec((seq, V_HEAD), lambda b, h: (b, h))],
        out_shape=[jax.ShapeDtypeStruct((batch * seq, D_MODEL), KEPT),
                   jax.ShapeDtypeStruct((batch * seq, D_MODEL), F32)],
        name="attn_fwd", compiler_params=_params("parallel", "parallel"))(q, k, v)


def _attn_bwd(q, k, v, o, do, lse, batch, seq, dep):
    tq = _tile(seq, ATTN_TILE)
    nq = seq // tq

    def body(q_ref, k_ref, v_ref, o_ref, do_ref, lse_ref, dep_ref, dq_ref, dk_ref, dv_ref, dk_acc, dv_acc):
        dk_acc[...] = jnp.zeros_like(dk_acc)
        dv_acc[...] = jnp.zeros_like(dv_acc)
        for qi in range(nq):
            rows = slice(qi * tq, (qi + 1) * tq)
            kmax = (qi + 1) * tq
            qr = q_ref[rows, :]
            dor = do_ref[rows, :]
            kk = k_ref[0:kmax, :]
            s = _dot_nt(qr, kk)
            p = jnp.where(_causal(tq, kmax, qi * tq), jnp.exp(s - lse_ref[rows, 0:1]), 0.0)
            dp = _dot_nt(dor, v_ref[0:kmax, :])
            delta = jnp.sum(dor.astype(F32) * o_ref[rows, :].astype(F32), axis=-1, keepdims=True)
            ds = (p * (dp - delta)).astype(BF16)
            dq_ref[rows, :] = (_dot(ds, kk) * ATTN_SCALE).astype(BF16)
            dk_acc[0:kmax, :] += _dot_tn(ds, qr)
            dv_acc[0:kmax, :] += _dot_tn(p.astype(BF16), dor)
        dk_ref[...] = dk_acc[...].astype(BF16)
        dv_ref[...] = dv_acc[...].astype(BF16)

    qspec = pl.BlockSpec((seq, HEAD_PAD), lambda b, h: (b, h))
    vspec = pl.BlockSpec((seq, V_HEAD), lambda b, h: (b, h))
    T = batch * seq
    return pl.pallas_call(
        body, grid=(batch, MLA_HEADS),
        in_specs=[qspec, qspec, vspec, vspec, vspec, vspec, ANY],
        out_specs=[qspec, qspec, vspec],
        out_shape=[jax.ShapeDtypeStruct((T, MLA_HEADS * HEAD_PAD), BF16),
                   jax.ShapeDtypeStruct((T, MLA_HEADS * HEAD_PAD), BF16),
                   jax.ShapeDtypeStruct((T, D_MODEL), BF16)],
        scratch_shapes=[pltpu.VMEM((seq, HEAD_PAD), F32), pltpu.VMEM((seq, V_HEAD), F32)],
        name="attn_bwd", compiler_params=_params("parallel", "parallel"))(q, k, v, o, do, lse, dep)


def _merge_out(x, yag, zm, o, w_out, ffn_g):
    T = x.shape[0]
    tm = _tile(T, 512)

    def body(x_ref, ya_ref, gb_ref, o_ref, w_ref, g_ref, mg_ref, x1_ref, h2_ref):
        mg = (ya_ref[...].astype(F32) + _sigmoid(gb_ref[...].astype(F32)) * o_ref[...].astype(F32)).astype(BF16)
        mg_ref[...] = mg
        x1 = x_ref[...] + _dot(mg, w_ref[...])
        x1_ref[...] = x1
        r = lax.rsqrt(jnp.mean(x1 * x1, axis=-1, keepdims=True) + EPS)
        h2_ref[...] = (x1 * r * g_ref[...]).astype(BF16)

    row = pl.BlockSpec((tm, D_MODEL), lambda i: (i, 0))
    return pl.pallas_call(
        body, grid=(T // tm,),
        in_specs=[row, row, pl.BlockSpec((tm, D_MODEL), lambda i: (i, 3)), row,
                  pl.BlockSpec((D_MODEL, D_MODEL), lambda i: (0, 0)), pl.BlockSpec((1, D_MODEL), lambda i: (0, 0))],
        out_specs=[row, row, row],
        out_shape=[jax.ShapeDtypeStruct((T, D_MODEL), BF16), jax.ShapeDtypeStruct((T, D_MODEL), F32),
                   jax.ShapeDtypeStruct((T, D_MODEL), BF16)],
        name="merge_out", compiler_params=_params("parallel"))(x, yag, zm, o, w_out, ffn_g)


FF_TILE = 256
FF_BLOCKS = D_FF // FF_TILE
FFB_TILE = 256
UP_ROWS = 512
EDGE = 16


def _shift_up(x, k):
    n = x.shape[0]
    row = lax.broadcasted_iota(jnp.int32, x.shape, 0)
    return jnp.where(row < n - k, pltpu.roll(x, n - k, 0), 0.0)


def _up_act(h2, wt_up, cw, cb, batch, seq):
    def body(h_ref, wug_ref, wuv_ref, wg_ref, wv_ref, bg_ref, bv_ref, ug_ref, uv_ref, g_ref, v_ref, a_ref,
             ug_s, uv_s):
        for s in (ug_s, uv_s):
            s[0:SUBLANES, :] = jnp.zeros((SUBLANES, FF_TILE), F32)

        def conv(s, w_ref, b_ref, r0):
            return (b_ref[...] + w_ref[2:3, :] * s[r0:r0 + UP_ROWS, :]
                    + w_ref[1:2, :] * s[r0 - 1:r0 - 1 + UP_ROWS, :]
                    + w_ref[0:1, :] * s[r0 - 2:r0 - 2 + UP_ROWS, :])

        for c in range(seq // UP_ROWS):
            rows = slice(c * UP_ROWS, (c + 1) * UP_ROWS)
            r0 = SUBLANES + c * UP_ROWS
            h = h_ref[rows, :]
            for w_ref, u_ref, s in ((wug_ref, ug_ref, ug_s), (wuv_ref, uv_ref, uv_s)):
                u = _dot_nt(h, w_ref[...])
                u_ref[rows, :] = u.astype(KEPT)
                s[r0:r0 + UP_ROWS, :] = u
            gate, val = conv(ug_s, wg_ref, bg_ref, r0), conv(uv_s, wv_ref, bv_ref, r0)
            g_ref[rows, :] = gate.astype(KEPT)
            v_ref[rows, :] = val.astype(KEPT)
            a_ref[rows, :] = (gate * _sigmoid(gate) * val).astype(BF16)

    blk = pl.BlockSpec((seq, FF_TILE), lambda b, j: (b, j))
    wup = lambda off: pl.BlockSpec((FF_TILE, D_MODEL), lambda b, j: (j + off, 0))
    wsp = lambda off: pl.BlockSpec((3, FF_TILE), lambda b, j: (0, j + off))
    bsp = lambda off: pl.BlockSpec((1, FF_TILE), lambda b, j: (0, j + off))
    T = batch * seq
    kept = jax.ShapeDtypeStruct((T, D_FF), KEPT)
    return pl.pallas_call(
        body, grid=(batch, FF_BLOCKS),
        in_specs=[pl.BlockSpec((seq, D_MODEL), lambda b, j: (b, 0)), wup(0), wup(FF_BLOCKS),
                  wsp(0), wsp(FF_BLOCKS), bsp(0), bsp(FF_BLOCKS)],
        out_specs=[blk] * 5,
        out_shape=[kept, kept, kept, kept, jax.ShapeDtypeStruct((T, D_FF), BF16)],
        scratch_shapes=[pltpu.VMEM((SUBLANES + seq, FF_TILE), F32)] * 2,
        name="up_act", compiler_params=_params("parallel", "arbitrary"))(h2, wt_up, wt_up, cw, cw, cb, cb)


def _ffn_act_bwd(upg, upv, gate, val, cw, dx2b, w_down, batch, seq):
    def half(du, x, w_ref, dx_ref, dw_ref):
        j = pl.program_id(1)
        n = du.shape[0]
        up1, up2 = pltpu.roll(du, n - 1, 0), pltpu.roll(du, n - 2, 0)
        dx_ref[...] = (w_ref[2:3, :] * du + w_ref[1:2, :] * up1 + w_ref[0:1, :] * up2).astype(BF16)
        tail = du[n - EDGE:n]
        dx_ref[n - EDGE:n, :] = (w_ref[2:3, :] * tail + w_ref[1:2, :] * _shift_up(tail, 1)
                                 + w_ref[0:1, :] * _shift_up(tail, 2)).astype(BF16)
        row = lax.broadcasted_iota(jnp.int32, (EDGE, du.shape[1]), 0)
        head, x_tail = du[0:EDGE], x[n - EDGE:n]
        wrap1 = jnp.sum(jnp.where(row >= EDGE - 1, pltpu.roll(head, EDGE - 1, 0), 0.0) * x_tail, axis=0, keepdims=True)
        wrap2 = jnp.sum(jnp.where(row >= EDGE - 2, pltpu.roll(head, EDGE - 2, 0), 0.0) * x_tail, axis=0, keepdims=True)
        dw_ref[j, 2:3, :] += jnp.sum(du * x, axis=0, keepdims=True)
        dw_ref[j, 1:2, :] += jnp.sum(up1 * x, axis=0, keepdims=True) - wrap1
        dw_ref[j, 0:1, :] += jnp.sum(up2 * x, axis=0, keepdims=True) - wrap2
        dw_ref[j, 3:4, :] += jnp.sum(du, axis=0, keepdims=True)

    def body(ug_ref, uv_ref, g_ref, v_ref, wg_ref, wv_ref, dx_ref, wd_ref, dg_ref, dv_ref, dwg_ref, dwv_ref):
        @pl.when((pl.program_id(0) == 0) & (pl.program_id(1) == 0))
        def _():
            dwg_ref[...] = jnp.zeros_like(dwg_ref)
            dwv_ref[...] = jnp.zeros_like(dwv_ref)

        gate, val = g_ref[...].astype(F32), v_ref[...].astype(F32)
        sg = _sigmoid(gate)
        dav = _dot_nt(dx_ref[...], wd_ref[...])
        half(dav * val * sg * (1.0 + gate * (1.0 - sg)), ug_ref[...].astype(F32), wg_ref, dg_ref, dwg_ref)
        half(dav * gate * sg, uv_ref[...].astype(F32), wv_ref, dv_ref, dwv_ref)

    nb = D_FF // FFB_TILE
    blk = pl.BlockSpec((seq, FFB_TILE), lambda b, j: (b, j))
    wsp = lambda off: pl.BlockSpec((3, FFB_TILE), lambda b, j: (0, j + off))
    acc = pl.BlockSpec((nb, 4, FFB_TILE), lambda b, j: (0, 0, 0))
    T = batch * seq
    dupg, dupv, dwg, dwv = pl.pallas_call(
        body, grid=(batch, nb),
        in_specs=[blk, blk, blk, blk, wsp(0), wsp(nb),
                  pl.BlockSpec((seq, D_MODEL), lambda b, j: (b, 0)),
                  pl.BlockSpec((FFB_TILE, D_MODEL), lambda b, j: (j, 0))],
        out_specs=[blk, blk, acc, acc],
        out_shape=[jax.ShapeDtypeStruct((T, D_FF), BF16), jax.ShapeDtypeStruct((T, D_FF), BF16),
                   jax.ShapeDtypeStruct((nb, 4, FFB_TILE), F32), jax.ShapeDtypeStruct((nb, 4, FFB_TILE), F32)],
        name="ffn_act_bwd", compiler_params=_params("arbitrary", "arbitrary"))(
            upg, upv, gate, val, cw, cw, dx2b, w_down)
    dwg, dwv = (jnp.transpose(a, (1, 0, 2)).reshape(4, D_FF) for a in (dwg, dwv))
    return dupg, dupv, dwg[:3], dwv[:3], dwg[3:], dwv[3:]


def _down_loss(a, w_down, x1, target, gfin):
    T = x1.shape[0]
    tm = _tile(T, 512)

    def body(a_ref, w_ref, x1_ref, t_ref, g_ref, dx_ref, dxb_ref, loss_ref, dg_ref):
        @pl.when(pl.program_id(0) == 0)
        def _():
            loss_ref[...] = jnp.zeros_like(loss_ref)
            dg_ref[...] = jnp.zeros_like(dg_ref)

        x2 = x1_ref[...] + _dot(a_ref[...], w_ref[...])
        r = lax.rsqrt(jnp.mean(x2 * x2, axis=-1, keepdims=True) + EPS)
        xh = x2 * r
        g = g_ref[...]
        diff = xh * g - t_ref[...]
        loss_ref[...] += 0.5 * jnp.sum(jnp.mean(diff * diff, axis=-1, keepdims=True))
        dy = diff * (1.0 / D_MODEL)
        dg_ref[...] += jnp.sum(dy * xh, axis=0, keepdims=True)
        dxh = dy * g
        dx = r * (dxh - xh * jnp.mean(dxh * xh, axis=-1, keepdims=True))
        dx_ref[...] = dx
        dxb_ref[...] = dx.astype(BF16)

    row = pl.BlockSpec((tm, D_MODEL), lambda i: (i, 0))
    vec = pl.BlockSpec((1, D_MODEL), lambda i: (0, 0))
    return pl.pallas_call(
        body, grid=(T // tm,),
        in_specs=[pl.BlockSpec((tm, D_FF), lambda i: (i, 0)),
                  pl.BlockSpec((D_FF, D_MODEL), lambda i: (0, 0)), row, row, vec],
        out_specs=[row, row, pl.BlockSpec((8, LANES), lambda i: (0, 0)), vec],
        out_shape=[jax.ShapeDtypeStruct((T, D_MODEL), F32), jax.ShapeDtypeStruct((T, D_MODEL), BF16),
                   jax.ShapeDtypeStruct((8, LANES), F32), jax.ShapeDtypeStruct((1, D_MODEL), F32)],
        name="down_loss", compiler_params=_params("arbitrary"))(a, w_down, x1, target, gfin)


def _local_step(x, positions, target, mix_norm, av_g, av_b, w_s, b_s, q_norm, kv_norm, ffn_norm, conv_b,
                final_norm, comm):
    batch, seq, _ = x.shape
    T = batch * seq
    x = x.reshape(T, D_MODEL)
    target = target.reshape(T, D_MODEL)
    pos = positions.reshape(T, 1)
    half = jnp.arange(0, QK_ROPE, 2, dtype=F32) / QK_ROPE
    inv_freq = 1.0 / (ROPE_THETA ** half)
    invf = jnp.concatenate([inv_freq, inv_freq, jnp.zeros((LANES - QK_ROPE,), F32)]).reshape(1, LANES)
    w_st = jnp.swapaxes(w_s, 1, 2)
    b_col = b_s.reshape(A_GROUPS, CHUNK, 1)

    wt_in = comm.in_weights()
    h, zm, zs = _in_proj(x, mix_norm, wt_in)
    yag = _mixer_a_fwd(zm, av_g, av_b, w_s, b_col)
    wuq_p, wukv, w_out = comm.mla_weights(after=yag)
    q, k, v = _mla_prep_fwd(zs, pos, invf, q_norm, kv_norm, wuq_p, wukv)
    o, lse = _attn_fwd(q, k, v, batch, seq)
    merged, x1, h2 = _merge_out(x, yag, zm, o, w_out, ffn_norm)
    wt_up, conv_w, w_down = comm.ffn_weights(after=merged)
    upg, upv, gate, val, act = _up_act(h2, wt_up, conv_w, conv_b, batch, seq)
    dx2, dx2b, loss_acc, d_final = _down_loss(act, w_down, x1, target, final_norm)

    d_wdown = _mm_tn(act, dx2b, "dw_down")
    dupg, dupv, dcwg, dcwv, dcbg, dcbv = _ffn_act_bwd(upg, upv, gate, val, conv_w, dx2b, w_down, batch, seq)
    d_wt_up = _mm_tn(dupv, h2, "dw_up_val", rows=2 * D_FF, row0=D_FF,
                     into=_mm_tn(dupg, h2, "dw_up_gate", rows=2 * D_FF))
    dx1, d_ffn_norm, dmerged = _proj_bwd(
        [dupg, dupv], wt_up, [(0, (0, D_FF), (0, D_FF)), (1, (0, D_FF), (D_FF, 2 * D_FF))],
        x1, ffn_norm, dx2, "up_proj_bwd", w2=w_out)
    d_wout = _mm_tn(merged, dx1, "dw_out")
    token = comm.send_ffn_grads(d_wdown, d_wt_up, jnp.concatenate([dcwg, dcwv], axis=1), d_wout)
    dzm, do, d_avg, d_avb, d_ws, d_bs = _mixer_bwd(zm, o, dmerged, av_g, av_b, w_s, w_st, b_col, token)
    zs_rows_zero = lax.empty((IN_DIM, D_MODEL), BF16).at[SPLIT_V:SPLIT_KR].set(0)
    d_wt_in = _mm_tn(dzm, h, "dw_in_main", rows=IN_DIM, gap=(SPLIT_V, ZS_ROWS), into=zs_rows_zero)
    token = comm.send_early_grads(d_wt_in, [
        d_avg, d_avb, _small_2d(d_ws).astype(BF16), d_bs.reshape(A_GROUPS, CHUNK), d_ffn_norm,
        jnp.concatenate([dcbg, dcbv], axis=1), d_final])
    dq, dk, dv = _attn_bwd(q, k, v, o, do, lse, batch, seq, token)
    dzs, d_wuq_p, d_wukv, d_qn, d_kvn = _mla_prep_bwd(zs, pos, invf, q_norm, kv_norm, wuq_p, wukv, dq, dk, dv)
    d_wt_zs = _mm_tn(dzs, h, "dw_in_small")
    token = comm.send_mla_grads(d_wuq_p, d_wukv, d_wt_zs)
    terms = [(0, (i * D_MODEL, (i + 1) * D_MODEL), rows) for i, rows in enumerate(IN_ROWS_MAIN)]
    terms.append((1, (0, ZS_W), IN_ROWS_ZS))
    dx, d_mix_norm = _proj_bwd([dzm, dzs], wt_in, terms, x, mix_norm, dx1, "in_proj_bwd", dep=token, rows=512)
    token = comm.send_late_grads([d_qn, d_kvn, d_mix_norm, loss_acc])
    return dx.reshape(batch, seq, D_MODEL), token


MESH_ID = pl.DeviceIdType.MESH
EFFECT = pltpu.SideEffectType.DATAFLOW_SIDE_EFFECTING


def _mesh_pos():
    return lax.axis_index("x"), lax.axis_index("y"), lax.axis_index("c")


def _peer(pos, d):
    x, y, c = pos
    px = 1 - x if d & 4 else x
    py = 1 - y if d & 2 else y
    pc = 1 - c if d & 1 else c
    return (px, py, pc), 4 * px + 2 * py + pc


def _copy(src_ref, land_ref, send_sems, recv_sems, a, d, pos, exchange, landing_here):
    peer, pid = _peer(pos, d)
    me = 4 * pos[0] + 2 * pos[1] + pos[2]
    if exchange:
        src, dst = src_ref.at[pid], land_ref.at[d]
    else:
        src, dst = src_ref, land_ref.at[pid if landing_here else me]
    return pltpu.make_async_remote_copy(
        src_ref=src, dst_ref=dst, send_sem=send_sems.at[a * (N_DEV - 1) + d - 1],
        recv_sem=recv_sems.at[a * (N_DEV - 1) + d - 1],
        device_id=peer, device_id_type=MESH_ID)


def _start_copies(groups, modes, name, dep=None):
    sizes = [len(g) for g in groups]
    srcs = [s for g in groups for s in g]
    lands = [lax.empty(s.shape if modes[gi] else (N_DEV,) + s.shape, s.dtype)
             for gi, g in enumerate(groups) for s in g]
    n, ng = len(srcs), len(groups)
    n_in = 2 * n + (dep is not None)

    def body(*refs):
        src_refs, land_refs = refs[:n], refs[n:2 * n]
        sems = refs[n_in:n_in + 3 * ng]
        token = refs[-1]
        pos = _mesh_pos()
        k = 0
        for gi, size in enumerate(sizes):
            for a in range(size):
                _own_copy(src_refs[k], land_refs[k], sems[3 * gi + 2], a, pos, modes[gi]).start()
                for d in range(1, N_DEV):
                    _copy(src_refs[k], land_refs[k], sems[3 * gi], sems[3 * gi + 1], a, d, pos, modes[gi],
                          landing_here=False).start()
                k += 1
        token[...] = jnp.zeros_like(token)

    sem_shapes = []
    for size in sizes:
        remote = pltpu.SemaphoreType.DMA((size * (N_DEV - 1),))
        sem_shapes += [remote, remote, pltpu.SemaphoreType.DMA((size,))]
    out = pl.pallas_call(
        body, name=name,
        out_shape=(*sem_shapes, *[pltpu.HBM(a.shape, a.dtype) for a in srcs + lands],
                   jax.ShapeDtypeStruct((8, LANES), F32)),
        in_specs=[HBM] * (2 * n) + [ANY] * (dep is not None),
        out_specs=(*[SEM] * (3 * ng), *[HBM] * (2 * n), pl.BlockSpec(memory_space=pltpu.VMEM)),
        input_output_aliases={i: 3 * ng + i for i in range(2 * n)},
        compiler_params=pltpu.CompilerParams(has_side_effects=EFFECT),
    )(*[pltpu.with_memory_space_constraint(a, pltpu.HBM) for a in srcs + lands], *([dep] if dep is not None else []))
    thru = out[3 * ng:3 * ng + 2 * n]
    handles, k = [], 0
    for gi, size in enumerate(sizes):
        handles.append((out[3 * gi:3 * gi + 3], thru[k:k + size], thru[n + k:n + k + size]))
        k += size
    return handles, out[-1]


def _own_copy(src_ref, land_ref, local_sems, a, pos, exchange):
    me = 4 * pos[0] + 2 * pos[1] + pos[2]
    src, dst = (src_ref.at[me], land_ref.at[0]) if exchange else (src_ref, land_ref.at[me])
    return pltpu.make_async_copy(src, dst, local_sems.at[a])


def _wait_copies(handle, exchange, after, name):
    sems, srcs, lands = handle
    n = len(srcs)

    def body(*refs):
        src_refs, land_refs = refs[:n], refs[n:2 * n]
        send, recv, local = refs[2 * n:2 * n + 3]
        pos = _mesh_pos()
        for a in range(n):
            _own_copy(src_refs[a], land_refs[a], local, a, pos, exchange).wait()
            for d in range(1, N_DEV):
                cp = _copy(src_refs[a], land_refs[a], send, recv, a, d, pos, exchange, landing_here=True)
                cp.wait_send()
                cp.wait_recv()

    out = pl.pallas_call(
        body, name=name,
        out_shape=tuple(pltpu.HBM(a.shape, a.dtype) for a in (*srcs, *lands)),
        in_specs=[HBM] * (2 * n) + [SEM, SEM, SEM, ANY], out_specs=[HBM] * (2 * n),
        input_output_aliases={i: i for i in range(2 * n)},
        compiler_params=pltpu.CompilerParams(has_side_effects=EFFECT),
    )(*srcs, *lands, *sems, after)
    return out[n:]


def _gather_now(a, name):
    def body(x_ref, out_ref, send_sems, recv_sems, local_sem):
        x, y, c = _mesh_pos()
        me, sibling = (x, y, c), (x, y, 1 - c)
        chips = [(1 - x, y), (x, 1 - y), (1 - x, 1 - y)]

        def slot(p):
            return out_ref.at[4 * p[0] + 2 * p[1] + p[2]]

        def copy(k, block, to, src=None):
            return pltpu.make_async_remote_copy(
                src_ref=slot(block) if src is None else src, dst_ref=slot(block), send_sem=send_sems.at[k],
                recv_sem=recv_sems.at[k], device_id=to, device_id_type=MESH_ID)

        mine = pltpu.make_async_copy(x_ref, slot(me), local_sem)
        mine.start()
        first = [copy(0, me, sibling, src=x_ref)]
        first += [copy(1 + j, me, (*chip, c), src=x_ref) for j, chip in enumerate(chips)]
        for cp in first:
            cp.start()
        passed = [copy(4 + j, (*chip, c), sibling) for j, chip in enumerate(chips)]
        for j, chip in enumerate(chips):
            copy(1 + j, (*chip, c), me).wait_recv()
            passed[j].start()
        copy(0, sibling, me).wait_recv()
        for j, chip in enumerate(chips):
            copy(4 + j, (*chip, 1 - c), me).wait_recv()
        for cp in first + passed:
            cp.wait_send()
        mine.wait()

    return pl.pallas_call(
        body, in_specs=[ANY], out_specs=ANY,
        out_shape=jax.ShapeDtypeStruct((N_DEV,) + a.shape, a.dtype),
        scratch_shapes=[pltpu.SemaphoreType.DMA((N_DEV - 1,)), pltpu.SemaphoreType.DMA((N_DEV - 1,)),
                        pltpu.SemaphoreType.DMA],
        name=name, compiler_params=pltpu.CompilerParams(has_side_effects=True))(a)


def _gather_in_copies(x_ref, out_ref, send_sems, recv_sems, local_sems):
    x, y, c = _mesh_pos()
    me, sibling = (x, y, c), (x, y, 1 - c)
    chips = [(1 - x, y), (x, 1 - y), (1 - x, 1 - y)]

    def slot(p):
        return out_ref.at[4 * p[0] + 2 * p[1] + p[2]]

    def copy(k, block, to, src=None):
        return pltpu.make_async_remote_copy(
            src_ref=slot(block) if src is None else src, dst_ref=slot(block), send_sem=send_sems.at[k],
            recv_sem=recv_sems.at[k], device_id=to, device_id_type=MESH_ID)

    mine = pltpu.make_async_copy(x_ref, slot(me), local_sems.at[0])
    first = [copy(0, me, sibling, src=x_ref)] + [copy(1 + j, me, (*chip, c), src=x_ref) for j, chip in enumerate(chips)]
    passed = [copy(4 + j, (*chip, c), sibling) for j, chip in enumerate(chips)]
    from_chips = [copy(1 + j, (*chip, c), me) for j, chip in enumerate(chips)]
    from_sibling = [copy(0, sibling, me)] + [copy(4 + j, (*chip, 1 - c), me) for j, chip in enumerate(chips)]
    return mine, first, passed, from_chips, from_sibling


def _gather_in_start(a, name):
    land = lax.empty((N_DEV,) + a.shape, a.dtype)

    def body(x_ref, land_ref, send, recv, local, x_out, land_out, token):
        mine, first, _, _, _ = _gather_in_copies(x_ref, land_ref, send, recv, local)
        mine.start()
        for cp in first:
            cp.start()
        token[...] = jnp.zeros_like(token)

    remote = pltpu.SemaphoreType.DMA((N_DEV - 1,))
    out = pl.pallas_call(
        body, name=name,
        out_shape=(remote, remote, pltpu.SemaphoreType.DMA((1,)), pltpu.HBM(a.shape, a.dtype),
                   pltpu.HBM(land.shape, land.dtype), jax.ShapeDtypeStruct((8, LANES), F32)),
        in_specs=[HBM, HBM], out_specs=(SEM, SEM, SEM, HBM, HBM, pl.BlockSpec(memory_space=pltpu.VMEM)),
        input_output_aliases={0: 3, 1: 4},
        compiler_params=pltpu.CompilerParams(has_side_effects=EFFECT),
    )(pltpu.with_memory_space_constraint(a, pltpu.HBM), pltpu.with_memory_space_constraint(land, pltpu.HBM))
    return (out[0:3], out[3], out[4]), out[5]


def _gather_in_finish(handle, after, name):
    sems, a, land = handle

    def arrived(x_ref, land_ref, send, recv, local, after_ref, x_out, land_out):
        for arrival in _gather_in_copies(x_ref, land_ref, send, recv, local)[3]:
            arrival.wait_recv()

    def forward(x_ref, land_ref, send, recv, local, after_ref, x_out, land_out):
        for cp in _gather_in_copies(x_ref, land_ref, send, recv, local)[2]:
            cp.start()

    def wait(x_ref, land_ref, send, recv, local, after_ref, x_out, land_out):
        mine, first, passed, _, from_sibling = _gather_in_copies(x_ref, land_ref, send, recv, local)
        for arrival in from_sibling:
            arrival.wait_recv()
        for cp in first + passed:
            cp.wait_send()
        mine.wait()

    for body, suffix in ((arrived, "_arrived"), (forward, "_forward"), (wait, "")):
        a, land = pl.pallas_call(
            body, name=name + suffix, out_shape=(pltpu.HBM(a.shape, a.dtype), pltpu.HBM(land.shape, land.dtype)),
            in_specs=[HBM, HBM, SEM, SEM, SEM, ANY], out_specs=[HBM, HBM], input_output_aliases={0: 0, 1: 1},
            compiler_params=pltpu.CompilerParams(has_side_effects=EFFECT),
        )(a, land, *sems, after)
    return land


def _sum_parts(p_ref):
    g = p_ref[0].astype(F32)
    for k in range(1, N_DEV):
        g = g + p_ref[k].astype(F32)
    return g


def _adamw_update(p_ref, w_ref, m_ref, v_ref, g_ref, d_ref, nm_ref, nv_ref, patch=None):
    c1 = 1.0 - ADAM_B1 ** ADAM_STEP
    c2 = 1.0 - ADAM_B2 ** ADAM_STEP
    g = _sum_parts(p_ref)
    if patch is not None:
        g = patch(g)
    nm = ADAM_B1 * m_ref[...] + (1.0 - ADAM_B1) * g
    nv = ADAM_B2 * v_ref[...] + (1.0 - ADAM_B2) * (g * g)
    g_ref[...] = g
    nm_ref[...] = nm
    nv_ref[...] = nv
    d_ref[...] = -ADAM_LR * ((nm / c1) / (jnp.sqrt(nv / c2) + ADAM_EPS) + ADAM_WD * w_ref[...])


def _adamw_many(parts, ws, ms, vs, sums, name):
    n, ns = len(ws), len(sums)

    def body(*refs):
        ins, outs = refs[:4 * n + ns], refs[4 * n + ns:]
        for i in range(n):
            _adamw_update(ins[i], ins[n + i], ins[2 * n + i], ins[3 * n + i],
                          outs[i], outs[n + i], outs[2 * n + i], outs[3 * n + i])
        for i in range(ns):
            outs[4 * n + i][...] = _sum_parts(ins[4 * n + i])

    full = lambda a: pl.BlockSpec(a.shape, lambda: (0,) * a.ndim)
    args = [*parts, *ws, *ms, *vs, *sums]
    outs = [jax.ShapeDtypeStruct(w.shape, F32) for _ in range(4) for w in ws]
    outs += [jax.ShapeDtypeStruct(s.shape[1:], F32) for s in sums]
    res = pl.pallas_call(
        body, in_specs=[full(a) for a in args], out_specs=[full(o) for o in outs], out_shape=outs,
        name=name, compiler_params=pltpu.CompilerParams(vmem_limit_bytes=VMEM_LIMIT))(*args)
    return res[:n], res[n:2 * n], res[2 * n:3 * n], res[3 * n:4 * n], res[4 * n:]


def _adamw(parts, w, m, v, name, late=None):
    R, C = w.shape
    tr, tc = R, C
    if N_DEV * R * C * parts.dtype.itemsize > SMALL_BLOCK_BYTES:
        tr = next((t for t in range(min(R, 256) // 16 * 16, 15, -16) if R % t == 0), R)
        if tr == R:
            tc = _tile(C, 256)
    more, places = late if late is not None else (None, ())
    assert late is None or tr == R

    def body(p_ref, w_ref, m_ref, v_ref, *refs):
        def patch(g):
            more_ref, g_s = refs[0], refs[-1]
            x, y, c = _mesh_pos()
            me = 4 * x + 2 * y + c
            rows = _sum_parts(more_ref)
            g_s[...] = g
            for dev, row in places:
                g_s[row:row + rows.shape[0], :] += jnp.where(me == dev, rows, 0.0)
            return g_s[...]

        outs = refs[:4] if late is None else refs[1:5]
        _adamw_update(p_ref, w_ref, m_ref, v_ref, *outs, patch=None if late is None else patch)

    blk = pl.BlockSpec((tr, tc), lambda i, j: (i, j))
    shp = jax.ShapeDtypeStruct((R, C), F32)
    in_specs = [pl.BlockSpec((N_DEV, tr, tc), lambda i, j: (0, i, j)), blk, blk, blk]
    if late is not None:
        in_specs.append(pl.BlockSpec((N_DEV, more.shape[1], tc), lambda i, j: (0, 0, j)))
    return pl.pallas_call(
        body, grid=(R // tr, C // tc), in_specs=in_specs,
        out_specs=[blk, blk, blk, blk], out_shape=[shp, shp, shp, shp],
        scratch_shapes=[] if late is None else [pltpu.VMEM((tr, tc), F32)],
        name=name, compiler_params=_params("parallel", "parallel"))(parts, w, m, v, *([] if late is None else [more]))


def _adamw_rows(parts, w, m, v, name):
    R = w.shape[0]

    def body(p_ref, w_ref, m_ref, v_ref, g_ref, d_ref, nm_ref, nv_ref):
        for k in range(R):
            _adamw_update(p_ref.at[:, k:k + 1, :], w_ref.at[k], m_ref.at[k], v_ref.at[k],
                          g_ref.at[k], d_ref.at[k], nm_ref.at[k], nv_ref.at[k])

    full = lambda a: pl.BlockSpec(a.shape, lambda: (0,) * a.ndim)
    shp = jax.ShapeDtypeStruct(w.shape, F32)
    return pl.pallas_call(
        body, in_specs=[full(a) for a in (parts, w, m, v)], out_specs=[full(w)] * 4, out_shape=[shp] * 4,
        name=name, compiler_params=pltpu.CompilerParams(vmem_limit_bytes=VMEM_LIMIT))(parts, w, m, v)


SPLIT_V = 2 * D_MODEL
SPLIT_KR = SPLIT_V + Q_LORA + KV_LORA + QK_ROPE
IN_DIM = SPLIT_KR + 2 * D_MODEL
IN_ROWS_MAIN = ((0, D_MODEL), (D_MODEL, SPLIT_V), (SPLIT_KR, SPLIT_KR + D_MODEL), (SPLIT_KR + D_MODEL, IN_DIM))
IN_ROWS_ZS = (SPLIT_V, SPLIT_V + ZS_W)
ZS_ROWS = SPLIT_KR - SPLIT_V
IN_SHARD = IN_DIM // N_DEV
ZS_PIECES = tuple(
    (k, max(IN_SHARD * k, SPLIT_V) - SPLIT_V, max(IN_SHARD * k, SPLIT_V) - IN_SHARD * k)
    for k in range(N_DEV) if max(IN_SHARD * k, SPLIT_V) < min(IN_SHARD * (k + 1), SPLIT_KR))
ZS_PIECE_ROWS = ZS_ROWS // len(ZS_PIECES)
assert all(min(IN_SHARD * (k + 1), SPLIT_KR) - max(IN_SHARD * k, SPLIT_V) == ZS_PIECE_ROWS for k, _, _ in ZS_PIECES)

SMALL_EARLY = ("a_v_norm_g", "a_v_norm_b", "a_spatial_w", "a_spatial_b", "ffn_norm", "conv_b", "final_norm")
SMALL_LATE = ("q_a_norm", "kv_a_norm", "mix_norm")


def _small_2d(a):
    return a.reshape(-1, a.shape[-1])


def _cols_from_shards(g):
    return jnp.transpose(g, (1, 0, 2)).reshape(g.shape[1], N_DEV * g.shape[2])


def _shards_from_cols(a):
    R, W = a.shape
    return jnp.transpose(a.reshape(R, N_DEV, W // N_DEV), (1, 0, 2))


class _Comm:
    GATHER_GROUPS = (("w_uq", "w_ukv", "w_out"), ("w_up", "conv_w", "w_down"))
    FFN_GRADS = ("w_down", "w_up", "conv_w", "w_out")
    TRANSPOSED = ("w_in", "w_up", "w_uq")

    def __init__(self, shards):
        local = {n: a.astype(F32 if n == "conv_w" else BF16) for n, a in shards.items()}
        h_in, token = _gather_in_start(local["w_in"], "gather_w_in_start")
        groups = [[local[n] for n in g] for g in self.GATHER_GROUPS]
        (self.h_mla, self.h_ffn), token = _start_copies(groups, [False] * 2, "gather_start", dep=token)
        self.g_in = _gather_in_finish(h_in, token, "gather_w_in")

    def in_weights(self):
        return self.g_in.reshape(IN_DIM, D_MODEL)

    def mla_weights(self, after):
        g_uq, g_ukv, g_out = _wait_copies(self.h_mla, False, after, "gather_wait_mla")
        wuq_p = jnp.pad(g_uq, ((0, 0), (0, HEAD_PAD - QK_HEAD), (0, 0))).reshape(MLA_HEADS * HEAD_PAD, Q_LORA)
        return wuq_p, _cols_from_shards(g_ukv), g_out.reshape(D_MODEL, D_MODEL)

    def ffn_weights(self, after):
        g_up, g_cw, g_down = _wait_copies(self.h_ffn, False, after, "gather_wait_ffn")
        return g_up.reshape(2 * D_FF, D_MODEL), _cols_from_shards(g_cw), g_down.reshape(D_FF, D_MODEL)

    def send_ffn_grads(self, d_wdown, d_wt_up, d_convw, d_wout):
        group = [d_wdown.reshape(N_DEV, D_FF // N_DEV, D_MODEL), d_wt_up.reshape(N_DEV, 2 * D_FF // N_DEV, D_MODEL),
                 _shards_from_cols(d_convw), d_wout.reshape(N_DEV, D_MODEL // N_DEV, D_MODEL)]
        (self.h_ffn_grads,), token = _start_copies([group], [True], "ffn_grads_start")
        return token

    def send_early_grads(self, d_wt_in, grads):
        blocks = d_wt_in.reshape(N_DEV, IN_SHARD, D_MODEL)
        (self.h_small_early, self.h_in_grads), token = _start_copies(
            [grads, [blocks]], [False, True], "early_grads_start")
        return token

    def send_mla_grads(self, d_wuq_p, d_wukv, d_wt_zs):
        d_uq = d_wuq_p.reshape(MLA_HEADS, HEAD_PAD, Q_LORA)[:, :QK_HEAD, :]
        zs_blocks = jnp.zeros((N_DEV, ZS_PIECE_ROWS, D_MODEL), d_wt_zs.dtype)
        for dev, first, _ in ZS_PIECES:
            zs_blocks = zs_blocks.at[dev].set(d_wt_zs[first:first + ZS_PIECE_ROWS])
        (self.h_mla_grads,), token = _start_copies(
            [[d_uq, _shards_from_cols(d_wukv), zs_blocks]], [True], "mla_grads_start")
        return token

    def send_late_grads(self, small):
        (self.h_late_small,), token = _start_copies([small], [False], "late_grads_start")
        return token


def kernel(x, positions, mix_norm, w_in, a_v_norm_g, a_v_norm_b, a_spatial_w, a_spatial_b, q_a_norm, w_uq, kv_a_norm, w_ukv, w_out, ffn_norm, w_up, conv_w, conv_b, w_down, final_norm, loss_target, m_mix_norm, m_w_in, m_a_v_norm_g, m_a_v_norm_b, m_a_spatial_w, m_a_spatial_b, m_q_a_norm, m_w_uq, m_kv_a_norm, m_w_ukv, m_w_out, m_ffn_norm, m_w_up, m_conv_w, m_conv_b, m_w_down, m_final_norm, v_mix_norm, v_w_in, v_a_v_norm_g, v_a_v_norm_b, v_a_spatial_w, v_a_spatial_b, v_q_a_norm, v_w_uq, v_kv_a_norm, v_w_ukv, v_w_out, v_ffn_norm, v_w_up, v_conv_w, v_conv_b, v_w_down, v_final_norm):
    names = ("mix_norm", "w_in", "a_v_norm_g", "a_v_norm_b", "a_spatial_w", "a_spatial_b", "q_a_norm", "w_uq",
             "kv_a_norm", "w_ukv", "w_out", "ffn_norm", "w_up", "conv_w", "conv_b", "w_down", "final_norm")
    w = dict(zip(names, (mix_norm, w_in, a_v_norm_g, a_v_norm_b, a_spatial_w, a_spatial_b, q_a_norm, w_uq,
                         kv_a_norm, w_ukv, w_out, ffn_norm, w_up, conv_w, conv_b, w_down, final_norm)))
    m = dict(zip(names, (m_mix_norm, m_w_in, m_a_v_norm_g, m_a_v_norm_b, m_a_spatial_w, m_a_spatial_b,
                         m_q_a_norm, m_w_uq, m_kv_a_norm, m_w_ukv, m_w_out, m_ffn_norm, m_w_up, m_conv_w,
                         m_conv_b, m_w_down, m_final_norm)))
    v = dict(zip(names, (v_mix_norm, v_w_in, v_a_v_norm_g, v_a_v_norm_b, v_a_spatial_w, v_a_spatial_b,
                         v_q_a_norm, v_w_uq, v_kv_a_norm, v_w_ukv, v_w_out, v_ffn_norm, v_w_up, v_conv_w,
                         v_conv_b, v_w_down, v_final_norm)))
    shapes = {n: w[n].shape for n in names}
    def view(tree, n):
        a = tree[n].reshape(tree[n].shape[-2:])
        return a.T if n in _Comm.TRANSPOSED else a

    comm = _Comm({n: view(w, n) for n in ("w_in",) + _Comm.GATHER_GROUPS[0] + _Comm.GATHER_GROUPS[1]})

    grad_x, token = _local_step(
        x, positions, loss_target, w["mix_norm"], w["a_v_norm_g"], w["a_v_norm_b"], w["a_spatial_w"][0],
        w["a_spatial_b"][0], w["q_a_norm"], w["kv_a_norm"], w["ffn_norm"], w["conv_b"],
        w["final_norm"].reshape(1, D_MODEL), comm)

    out_g, out_d, out_m, out_v = {}, {}, {}, {}

    def update(n, parts, late=None):
        if n == "conv_w":
            rows = lambda t: t[n].reshape(t[n].shape[-2], 1, t[n].shape[-1])
            res = _adamw_rows(parts, rows(w), rows(m), rows(v), "adamw_" + n)
        else:
            res = _adamw(parts, view(w, n), view(m, n), view(v, n), "adamw_" + n, late=late)
        out_g[n], out_d[n], out_m[n], out_v[n] = (
            (t.T if n in _Comm.TRANSPOSED else t).reshape(shapes[n]) for t in res)
        return res[1]

    def update_small(names, parts, sums, name):
        res = _adamw_many(parts, *[[_small_2d(t[n]) for n in names] for t in (w, m, v)], sums, name)
        for i, n in enumerate(names):
            out_g[n], out_d[n], out_m[n], out_v[n] = (r[i].reshape(shapes[n]) for r in res[:4])
        return res

    for n, parts in zip(_Comm.FFN_GRADS, _wait_copies(comm.h_ffn_grads, True, token, "ffn_grads_wait")):
        last = update(n, parts)
    early = _wait_copies(comm.h_small_early, False, last, "small_grads_wait")
    last = update_small(SMALL_EARLY, early, [], "adamw_small")[1][0]
    (in_parts,) = _wait_copies(comm.h_in_grads, True, last, "in_grads_wait")
    uq_parts, ukv_parts, zs_parts = _wait_copies(comm.h_mla_grads, True, in_parts, "mla_grads_wait")
    last = update("w_in", in_parts, late=(zs_parts, [(dev, row) for dev, _, row in ZS_PIECES]))
    last = update("w_uq", uq_parts)
    last = update("w_ukv", ukv_parts)
    late = _wait_copies(comm.h_late_small, False, last, "late_small_wait")
    res = update_small(SMALL_LATE, late[:-1], late[-1:], "adamw_late")
    loss = res[4][0][0, 0]

    return (loss, grad_x, *[out_g[n] for n in names], *[out_d[n] for n in names],
            *[out_m[n] for n in names], *[out_v[n] for n in names])
```
